```python
import math
import jax, jax.numpy as jnp
from jax import lax
import numpy as np

D_MODEL = 1024
BATCH = 32
SEQ = 2048
DEPTH = 1

HEAD_DIM = 64
SB_HEADS = 8
DIL_GROUPS = ((128, 1), (512, 4), (2048, 16))
DIL_HEADS = 4
MEM_HEADS = 4
MEM_HEAD_DIM = 128
MEM_LEN = 256
N_BRANCHES = 3
D_FF = ((-(-8 * D_MODEL // 3) + 255) // 256) * 256
BLOCK = 128
ROPE_THETA = 10000.0
NORM_EPS = 1e-6
NEG_INF = -1e30

SB_W = SB_HEADS * HEAD_DIM
DIL_W = DIL_HEADS * HEAD_DIM
MEM_W = MEM_HEADS * MEM_HEAD_DIM
IN_SPLITS = (SB_W,) * 3 + (DIL_W,) * (3 * len(DIL_GROUPS)) + (MEM_W,)
D_IN = sum(IN_SPLITS)

kernel_name = 'hybrid_stickbreak_dilated_memory_block'


def rms_norm(x, g):
    xf = x.astype(jnp.float32)
    y = xf * lax.rsqrt(jnp.mean(xf * xf, axis=-1, keepdims=True) + NORM_EPS)
    return (y * g.astype(jnp.float32)).astype(x.dtype)


def rope(x, pos):
    dh = x.shape[-1]
    half = dh // 2
    inv_freq = ROPE_THETA ** (-jnp.arange(half, dtype=jnp.float32) * 2.0 / dh)
    ang = pos.astype(jnp.float32)[:, None] * inv_freq[None, :]
    cos = jnp.cos(ang)[None, :, None, :]
    sin = jnp.sin(ang)[None, :, None, :]
    xf = x.astype(jnp.float32)
    x1, x2 = xf[..., :half], xf[..., half:]
    return jnp.concatenate([x1 * cos - x2 * sin, x2 * cos + x1 * sin], axis=-1).astype(x.dtype)


def stick_breaking_attention(q, k, v):
    B, S, H, dh = q.shape
    scale = dh ** -0.5
    outs = []
    for i in range(S // BLOCK):
        t0 = i * BLOCK
        t1 = t0 + BLOCK
        z = jnp.einsum('bqhd,bkhd->bhqk', q[:, t0:t1], k[:, :t1]).astype(jnp.float32) * scale
        t_pos = t0 + jnp.arange(BLOCK)[:, None]
        s_pos = jnp.arange(t1)[None, :]
        causal = s_pos < t_pos
        log_beta = jax.nn.log_sigmoid(z)
        log_keep = jnp.where(causal, jax.nn.log_sigmoid(-z), 0.0)
        log_keep_after = lax.cumsum(log_keep, axis=3, reverse=True) - log_keep
        weight = jnp.where(causal, jnp.exp(log_beta + log_keep_after), 0.0)
        outs.append(jnp.einsum('bhqk,bkhd->bqhd', weight, v[:, :t1].astype(jnp.float32)))
    return jnp.concatenate(outs, axis=1).astype(q.dtype)


def banded_attention(q, k, v, span):
    N, L, H, dh = q.shape
    nb = -(-L // BLOCK)
    pad = nb * BLOCK - L
    padf = lambda t: jnp.pad(t, ((0, 0), (0, pad), (0, 0), (0, 0))).reshape(N, nb, BLOCK, H, dh)
    qb, kb, vb = padf(q), padf(k), padf(v)
    prev = lambda t: jnp.concatenate([jnp.zeros_like(t[:, :1]), t[:, :-1]], axis=1)
    kk = jnp.concatenate([prev(kb), kb], axis=2)
    vv = jnp.concatenate([prev(vb), vb], axis=2)
    s = jnp.einsum('nbqhd,nbkhd->nbhqk', qb, kk).astype(jnp.float32) * (dh ** -0.5)
    qi = jnp.arange(BLOCK)[:, None] + BLOCK
    kj = jnp.arange(2 * BLOCK)[None, :]
    dist = qi - kj
    band = (dist >= 0) & (dist <= span)
    has_prev = (jnp.arange(nb)[:, None, None] > 0) | (kj[None] >= BLOCK)
    valid = band[None] & has_prev
    s = jnp.where(valid[None, :, None], s, NEG_INF)
    m = jnp.max(s, axis=-1, keepdims=True)
    p = jnp.exp(s - m)
    den = jnp.sum(p, axis=-1, keepdims=True)
    o = jnp.einsum('nbhqk,nbkhd->nbhqd', p, vv.astype(jnp.float32)) / den
    lse = (m + jnp.log(den))[..., 0]
    o = o.transpose(0, 1, 3, 2, 4).reshape(N, nb * BLOCK, H, dh)[:, :L]
    lse = lse.transpose(0, 1, 3, 2).reshape(N, nb * BLOCK, H)[:, :L]
    return o, lse


def dilated_window_attention(q, k, v, window, dilation):
    B, S, H, dh = q.shape
    L = S // dilation
    def to_sub(t):
        return t.reshape(B, L, dilation, H, dh).transpose(0, 2, 1, 3, 4).reshape(B * dilation, L, H, dh)
    o, lse = banded_attention(to_sub(q), to_sub(k), to_sub(v), window // dilation)
    o = o.reshape(B, dilation, L, H, dh).transpose(0, 2, 1, 3, 4).reshape(B, S, H, dh)
    lse = lse.reshape(B, dilation, L, H).transpose(0, 2, 1, 3).reshape(B, S, H)
    return o, lse


def memory_cross_attention(q, k, v):
    s = jnp.einsum('bshd,bmhd->bhsm', q, k).astype(jnp.float32) * (q.shape[-1] ** -0.5)
    p = jax.nn.softmax(s, axis=-1)
    return jnp.einsum('bhsm,bmhd->bshd', p, v.astype(jnp.float32)).astype(q.dtype)


def _fwd_setup_inputs(seed: int = 0) -> dict:
    key = jax.random.key(seed)
    ks = jax.random.split(key, 20)
    def w(k, shape, fan_in):
        return jax.random.normal(k, shape, jnp.float32) * (fan_in ** -0.5)
    def gain(k):
        return 1.0 + 0.05 * jax.random.normal(k, (DEPTH, D_MODEL), jnp.float32)
    return {
        'x': jax.random.normal(ks[0], (BATCH, SEQ, D_MODEL), jnp.float32),
        'mem': jax.random.normal(ks[1], (BATCH, MEM_LEN, D_MODEL), jnp.float32),
        'g_pre_mix': gain(ks[2]),
        'g_post_mix': gain(ks[3]),
        'g_pre_ffn': gain(ks[4]),
        'g_post_ffn': gain(ks[5]),
        'g_mem': gain(ks[6]),
        'w_in': w(ks[7], (DEPTH, D_MODEL, D_IN), D_MODEL),
        'w_mem_kv': w(ks[8], (DEPTH, D_MODEL, 2 * MEM_W), D_MODEL),
        'w_br_sb': w(ks[9], (DEPTH, SB_W, D_MODEL), SB_W),
        'w_br_dil': w(ks[10], (DEPTH, DIL_W, D_MODEL), DIL_W),
        'w_br_mem': w(ks[11], (DEPTH, MEM_W, D_MODEL), MEM_W),
        'w_gate': w(ks[12], (DEPTH, D_MODEL, N_BRANCHES * D_MODEL), D_MODEL),
        'b_gate': 0.02 * jax.random.normal(ks[13], (DEPTH, N_BRANCHES * D_MODEL), jnp.float32),
        'w_o': w(ks[14], (DEPTH, D_MODEL, D_MODEL), D_MODEL),
        'w_ffn_in': w(ks[15], (DEPTH, D_MODEL, 2 * D_FF), D_MODEL),
        'w_ffn_out': w(ks[16], (DEPTH, D_FF, D_MODEL), D_FF),
    }


def _fwd_reference(x, mem, g_pre_mix, g_post_mix, g_pre_ffn, g_post_ffn, g_mem, w_in, w_mem_kv,
              w_br_sb, w_br_dil, w_br_mem, w_gate, b_gate, w_o, w_ffn_in, w_ffn_out):
    B, S, D = x.shape
    pos = jnp.arange(S)
    split_idx = [int(i) for i in np.cumsum(IN_SPLITS)[:-1]]
    n_g = len(DIL_GROUPS)
    for l in range(DEPTH):
        h = rms_norm(x, g_pre_mix[l])
        proj = jnp.einsum('bsd,de->bse', h, w_in[l])
        parts = jnp.split(proj, split_idx, axis=-1)
        heads = lambda t, n, dh: t.reshape(B, S, n, dh)

        q_a, k_a, v_a = (heads(t, SB_HEADS, HEAD_DIM) for t in parts[0:3])
        o_a = stick_breaking_attention(q_a, k_a, v_a).reshape(B, S, SB_W)

        outs, lses = [], []
        for g, (window, dilation) in enumerate(DIL_GROUPS):
            q_g, k_g, v_g = (heads(t, DIL_HEADS, HEAD_DIM) for t in parts[3 + 3 * g: 6 + 3 * g])
            o_g, lse_g = dilated_window_attention(rope(q_g, pos), rope(k_g, pos), v_g, window, dilation)
            outs.append(o_g)
            lses.append(lse_g)
        alpha = jax.nn.softmax(jnp.stack(lses, axis=0), axis=0)[..., None]
        o_b = jnp.sum(alpha * jnp.stack(outs, axis=0), axis=0).astype(x.dtype).reshape(B, S, DIL_W)

        q_c = heads(parts[3 + 3 * n_g], MEM_HEADS, MEM_HEAD_DIM)
        kv_m = jnp.einsum('bmd,de->bme', rms_norm(mem, g_mem[l]), w_mem_kv[l])
        k_m = kv_m[..., :MEM_W].reshape(B, MEM_LEN, MEM_HEADS, MEM_HEAD_DIM)
        v_m = kv_m[..., MEM_W:].reshape(B, MEM_LEN, MEM_HEADS, MEM_HEAD_DIM)
        o_c = memory_cross_attention(q_c, k_m, v_m).reshape(B, S, MEM_W)

        y_a = jnp.einsum('bse,ed->bsd', o_a, w_br_sb[l])
        y_b = jnp.einsum('bse,ed->bsd', o_b, w_br_dil[l])
        y_c = jnp.einsum('bse,ed->bsd', o_c, w_br_mem[l])
        gates = jax.nn.sigmoid(jnp.einsum('bsd,de->bse', h, w_gate[l]) + b_gate[l]).reshape(B, S, N_BRANCHES, D)
        merged = gates[:, :, 0] * y_a + gates[:, :, 1] * y_b + gates[:, :, 2] * y_c
        mix = jnp.einsum('bsd,de->bse', merged, w_o[l])
        x = x + rms_norm(mix, g_post_mix[l])

        h2 = rms_norm(x, g_pre_ffn[l])
        gu = jnp.einsum('bsd,df->bsf', h2, w_ffn_in[l])
        f = jax.nn.silu(gu[..., :D_FF]) * gu[..., D_FF:]
        f = jnp.einsum('bsf,fd->bsd', f, w_ffn_out[l])
        x = x + rms_norm(f, g_post_ffn[l])
    return x


import jax as _jax
import jax.numpy as _jnp

TWIN_FORMAT = 'train_step'
FWD_PARAMS = ['x', 'mem', 'g_pre_mix', 'g_post_mix', 'g_pre_ffn', 'g_post_ffn', 'g_mem', 'w_in', 'w_mem_kv', 'w_br_sb', 'w_br_dil', 'w_br_mem', 'w_gate', 'b_gate', 'w_o', 'w_ffn_in', 'w_ffn_out']
TWIN_WEIGHTS = ['g_pre_mix', 'g_post_mix', 'g_pre_ffn', 'g_post_ffn', 'g_mem', 'w_in', 'w_mem_kv', 'w_br_sb', 'w_br_dil', 'w_br_mem', 'w_gate', 'b_gate', 'w_o', 'w_ffn_in', 'w_ffn_out']
TWIN_DIFF_INPUT = 'x'
TWIN_INPUTS = ['x', 'mem', 'g_pre_mix', 'g_post_mix', 'g_pre_ffn', 'g_post_ffn', 'g_mem', 'w_in', 'w_mem_kv', 'w_br_sb', 'w_br_dil', 'w_br_mem', 'w_gate', 'b_gate', 'w_o', 'w_ffn_in', 'w_ffn_out', 'loss_target', 'm_g_pre_mix', 'm_g_post_mix', 'm_g_pre_ffn', 'm_g_post_ffn', 'm_g_mem', 'm_w_in', 'm_w_mem_kv', 'm_w_br_sb', 'm_w_br_dil', 'm_w_br_mem', 'm_w_gate', 'm_b_gate', 'm_w_o', 'm_w_ffn_in', 'm_w_ffn_out', 'v_g_pre_mix', 'v_g_post_mix', 'v_g_pre_ffn', 'v_g_post_ffn', 'v_g_mem', 'v_w_in', 'v_w_mem_kv', 'v_w_br_sb', 'v_w_br_dil', 'v_w_br_mem', 'v_w_gate', 'v_b_gate', 'v_w_o', 'v_w_ffn_in', 'v_w_ffn_out']
TWIN_OUTPUTS = ['loss', 'grad_x', 'grad_g_pre_mix', 'grad_g_post_mix', 'grad_g_pre_ffn', 'grad_g_post_ffn', 'grad_g_mem', 'grad_w_in', 'grad_w_mem_kv', 'grad_w_br_sb', 'grad_w_br_dil', 'grad_w_br_mem', 'grad_w_gate', 'grad_b_gate', 'grad_w_o', 'grad_w_ffn_in', 'grad_w_ffn_out', 'delta_g_pre_mix', 'delta_g_post_mix', 'delta_g_pre_ffn', 'delta_g_post_ffn', 'delta_g_mem', 'delta_w_in', 'delta_w_mem_kv', 'delta_w_br_sb', 'delta_w_br_dil', 'delta_w_br_mem', 'delta_w_gate', 'delta_b_gate', 'delta_w_o', 'delta_w_ffn_in', 'delta_w_ffn_out', 'new_m_g_pre_mix', 'new_m_g_post_mix', 'new_m_g_pre_ffn', 'new_m_g_post_ffn', 'new_m_g_mem', 'new_m_w_in', 'new_m_w_mem_kv', 'new_m_w_br_sb', 'new_m_w_br_dil', 'new_m_w_br_mem', 'new_m_w_gate', 'new_m_b_gate', 'new_m_w_o', 'new_m_w_ffn_in', 'new_m_w_ffn_out', 'new_v_g_pre_mix', 'new_v_g_post_mix', 'new_v_g_pre_ffn', 'new_v_g_post_ffn', 'new_v_g_mem', 'new_v_w_in', 'new_v_w_mem_kv', 'new_v_w_br_sb', 'new_v_w_br_dil', 'new_v_w_br_mem', 'new_v_w_gate', 'new_v_b_gate', 'new_v_w_o', 'new_v_w_ffn_in', 'new_v_w_ffn_out']
TWIN_LEAF_KINDS = {'loss': 'loss', 'grad_x': 'grad_x', 'grad_g_pre_mix': 'grad_w', 'grad_g_post_mix': 'grad_w', 'grad_g_pre_ffn': 'grad_w', 'grad_g_post_ffn': 'grad_w', 'grad_g_mem': 'grad_w', 'grad_w_in': 'grad_w', 'grad_w_mem_kv': 'grad_w', 'grad_w_br_sb': 'grad_w', 'grad_w_br_dil': 'grad_w', 'grad_w_br_mem': 'grad_w', 'grad_w_gate': 'grad_w', 'grad_b_gate': 'grad_w', 'grad_w_o': 'grad_w', 'grad_w_ffn_in': 'grad_w', 'grad_w_ffn_out': 'grad_w', 'delta_g_pre_mix': 'delta_w', 'delta_g_post_mix': 'delta_w', 'delta_g_pre_ffn': 'delta_w', 'delta_g_post_ffn': 'delta_w', 'delta_g_mem': 'delta_w', 'delta_w_in': 'delta_w', 'delta_w_mem_kv': 'delta_w', 'delta_w_br_sb': 'delta_w', 'delta_w_br_dil': 'delta_w', 'delta_w_br_mem': 'delta_w', 'delta_w_gate': 'delta_w', 'delta_b_gate': 'delta_w', 'delta_w_o': 'delta_w', 'delta_w_ffn_in': 'delta_w', 'delta_w_ffn_out': 'delta_w', 'new_m_g_pre_mix': 'new_m', 'new_m_g_post_mix': 'new_m', 'new_m_g_pre_ffn': 'new_m', 'new_m_g_post_ffn': 'new_m', 'new_m_g_mem': 'new_m', 'new_m_w_in': 'new_m', 'new_m_w_mem_kv': 'new_m', 'new_m_w_br_sb': 'new_m', 'new_m_w_br_dil': 'new_m', 'new_m_w_br_mem': 'new_m', 'new_m_w_gate': 'new_m', 'new_m_b_gate': 'new_m', 'new_m_w_o': 'new_m', 'new_m_w_ffn_in': 'new_m', 'new_m_w_ffn_out': 'new_m', 'new_v_g_pre_mix': 'new_v', 'new_v_g_post_mix': 'new_v', 'new_v_g_pre_ffn': 'new_v', 'new_v_g_post_ffn': 'new_v', 'new_v_g_mem': 'new_v', 'new_v_w_in': 'new_v', 'new_v_w_mem_kv': 'new_v', 'new_v_w_br_sb': 'new_v', 'new_v_w_br_dil': 'new_v', 'new_v_w_br_mem': 'new_v', 'new_v_w_gate': 'new_v', 'new_v_b_gate': 'new_v', 'new_v_w_o': 'new_v', 'new_v_w_ffn_in': 'new_v', 'new_v_w_ffn_out': 'new_v'}


def _forward(args):
    return _fwd_reference(*[args[k] for k in FWD_PARAMS])


def _output_shape():
    out = _jax.eval_shape(lambda: _forward(_fwd_setup_inputs(0)))
    return out.shape, out.dtype

N_MICROBATCH = 1
ADAM_LR = 0.001
ADAM_B1 = 0.9
ADAM_B2 = 0.999
ADAM_EPS = 1e-08
ADAM_WD = 0.01
ADAM_STEP = 10
PER_EXAMPLE_BATCH_AXIS = {'x': 0, 'mem': 0, 'loss_target': 0}
SHARED_INPUTS = []
_WEIGHT_DTYPES = {'g_pre_mix': _jnp.float32, 'g_post_mix': _jnp.float32, 'g_pre_ffn': _jnp.float32, 'g_post_ffn': _jnp.float32, 'g_mem': _jnp.float32, 'w_in': _jnp.float32, 'w_mem_kv': _jnp.float32, 'w_br_sb': _jnp.float32, 'w_br_dil': _jnp.float32, 'w_br_mem': _jnp.float32, 'w_gate': _jnp.float32, 'b_gate': _jnp.float32, 'w_o': _jnp.float32, 'w_ffn_in': _jnp.float32, 'w_ffn_out': _jnp.float32}
MOMENT_SCALE = {'g_pre_mix': 9.492558e-01, 'g_post_mix': 6.406888e+01, 'g_pre_ffn': 8.382963e-01, 'g_post_ffn': 6.364091e+01, 'g_mem': 1.776973e-01, 'w_in': 3.748706e-01, 'w_mem_kv': 1.822958e-01, 'w_br_sb': 7.554665e-01, 'w_br_dil': 1.712079e-01, 'w_br_mem': 1.445323e-01, 'w_gate': 1.406577e-01, 'b_gate': 1.981653e-01, 'w_o': 8.176493e-01, 'w_ffn_in': 3.620086e-01, 'w_ffn_out': 7.417923e-01}


def _to_microbatches(a, axis):
    t = _jnp.moveaxis(a, axis, 0)
    t = t.reshape((N_MICROBATCH, t.shape[0] // N_MICROBATCH) + t.shape[1:])
    return _jnp.moveaxis(t, 1, axis + 1)


def setup_inputs(seed: int = 0) -> dict:
    inp = _fwd_setup_inputs(seed)
    key = _jax.random.fold_in(_jax.random.key(seed), 7919)
    shape, _ = _output_shape()
    out = dict(inp)
    out["loss_target"] = _jax.random.normal(_jax.random.fold_in(key, 0), shape, _jnp.float32)
    for i, name in enumerate(TWIN_WEIGHTS):
        w = inp[name].astype(_jnp.float32)
        if MOMENT_SCALE is None:
            s = _jnp.sqrt(_jnp.mean(_jnp.square(w)) + 1e-30)
        else:
            s = MOMENT_SCALE[name]
        km, kv = _jax.random.split(_jax.random.fold_in(key, i + 1))
        out[name] = w
        out["m_" + name] = s * _jax.random.normal(km, w.shape, _jnp.float32)
        out["v_" + name] = (s * s) * _jax.random.uniform(kv, w.shape, _jnp.float32, 0.5, 1.5)
    if N_MICROBATCH > 1:
        for name, axis in PER_EXAMPLE_BATCH_AXIS.items():
            out[name] = _to_microbatches(out[name], axis)
    return {'x': out['x'], 'mem': out['mem'], 'g_pre_mix': out['g_pre_mix'], 'g_post_mix': out['g_post_mix'], 'g_pre_ffn': out['g_pre_ffn'], 'g_post_ffn': out['g_post_ffn'], 'g_mem': out['g_mem'], 'w_in': out['w_in'], 'w_mem_kv': out['w_mem_kv'], 'w_br_sb': out['w_br_sb'], 'w_br_dil': out['w_br_dil'], 'w_br_mem': out['w_br_mem'], 'w_gate': out['w_gate'], 'b_gate': out['b_gate'], 'w_o': out['w_o'], 'w_ffn_in': out['w_ffn_in'], 'w_ffn_out': out['w_ffn_out'], 'loss_target': out['loss_target'], 'm_g_pre_mix': out['m_g_pre_mix'], 'm_g_post_mix': out['m_g_post_mix'], 'm_g_pre_ffn': out['m_g_pre_ffn'], 'm_g_post_ffn': out['m_g_post_ffn'], 'm_g_mem': out['m_g_mem'], 'm_w_in': out['m_w_in'], 'm_w_mem_kv': out['m_w_mem_kv'], 'm_w_br_sb': out['m_w_br_sb'], 'm_w_br_dil': out['m_w_br_dil'], 'm_w_br_mem': out['m_w_br_mem'], 'm_w_gate': out['m_w_gate'], 'm_b_gate': out['m_b_gate'], 'm_w_o': out['m_w_o'], 'm_w_ffn_in': out['m_w_ffn_in'], 'm_w_ffn_out': out['m_w_ffn_out'], 'v_g_pre_mix': out['v_g_pre_mix'], 'v_g_post_mix': out['v_g_post_mix'], 'v_g_pre_ffn': out['v_g_pre_ffn'], 'v_g_post_ffn': out['v_g_post_ffn'], 'v_g_mem': out['v_g_mem'], 'v_w_in': out['v_w_in'], 'v_w_mem_kv': out['v_w_mem_kv'], 'v_w_br_sb': out['v_w_br_sb'], 'v_w_br_dil': out['v_w_br_dil'], 'v_w_br_mem': out['v_w_br_mem'], 'v_w_gate': out['v_w_gate'], 'v_b_gate': out['v_b_gate'], 'v_w_o': out['v_w_o'], 'v_w_ffn_in': out['v_w_ffn_in'], 'v_w_ffn_out': out['v_w_ffn_out']}


def _loss(weights, diff, rest, loss_target):
    with _jax.named_scope("forward"):
        args = {**rest, TWIN_DIFF_INPUT: diff, **{k: w.astype(_WEIGHT_DTYPES[k]) for k, w in weights.items()}}
        y = _forward(args)
    with _jax.named_scope("loss_head"):
        err = _jnp.square(y.astype(_jnp.float32) - loss_target)
        return 0.5 * _jnp.sum(_jnp.mean(err, axis=-1)) if err.ndim else 0.5 * err


def _adamw(w, g, m, v):
    m = ADAM_B1 * m + (1.0 - ADAM_B1) * g
    v = ADAM_B2 * v + (1.0 - ADAM_B2) * _jnp.square(g)
    m_hat = m / (1.0 - ADAM_B1 ** ADAM_STEP)
    v_hat = v / (1.0 - ADAM_B2 ** ADAM_STEP)
    delta = -ADAM_LR * (m_hat / (_jnp.sqrt(v_hat) + ADAM_EPS) + ADAM_WD * w)
    return delta, m, v


def reference(x, mem, g_pre_mix, g_post_mix, g_pre_ffn, g_post_ffn, g_mem, w_in, w_mem_kv, w_br_sb, w_br_dil, w_br_mem, w_gate, b_gate, w_o, w_ffn_in, w_ffn_out, loss_target, m_g_pre_mix, m_g_post_mix, m_g_pre_ffn, m_g_post_ffn, m_g_mem, m_w_in, m_w_mem_kv, m_w_br_sb, m_w_br_dil, m_w_br_mem, m_w_gate, m_b_gate, m_w_o, m_w_ffn_in, m_w_ffn_out, v_g_pre_mix, v_g_post_mix, v_g_pre_ffn, v_g_post_ffn, v_g_mem, v_w_in, v_w_mem_kv, v_w_br_sb, v_w_br_dil, v_w_br_mem, v_w_gate, v_b_gate, v_w_o, v_w_ffn_in, v_w_ffn_out):
    given = dict(x=x, mem=mem, g_pre_mix=g_pre_mix, g_post_mix=g_post_mix, g_pre_ffn=g_pre_ffn, g_post_ffn=g_post_ffn, g_mem=g_mem, w_in=w_in, w_mem_kv=w_mem_kv, w_br_sb=w_br_sb, w_br_dil=w_br_dil, w_br_mem=w_br_mem, w_gate=w_gate, b_gate=b_gate, w_o=w_o, w_ffn_in=w_ffn_in, w_ffn_out=w_ffn_out, loss_target=loss_target, m_g_pre_mix=m_g_pre_mix, m_g_post_mix=m_g_post_mix, m_g_pre_ffn=m_g_pre_ffn, m_g_post_ffn=m_g_post_ffn, m_g_mem=m_g_mem, m_w_in=m_w_in, m_w_mem_kv=m_w_mem_kv, m_w_br_sb=m_w_br_sb, m_w_br_dil=m_w_br_dil, m_w_br_mem=m_w_br_mem, m_w_gate=m_w_gate, m_b_gate=m_b_gate, m_w_o=m_w_o, m_w_ffn_in=m_w_ffn_in, m_w_ffn_out=m_w_ffn_out, v_g_pre_mix=v_g_pre_mix, v_g_post_mix=v_g_post_mix, v_g_pre_ffn=v_g_pre_ffn, v_g_post_ffn=v_g_post_ffn, v_g_mem=v_g_mem, v_w_in=v_w_in, v_w_mem_kv=v_w_mem_kv, v_w_br_sb=v_w_br_sb, v_w_br_dil=v_w_br_dil, v_w_br_mem=v_w_br_mem, v_w_gate=v_w_gate, v_b_gate=v_b_gate, v_w_o=v_w_o, v_w_ffn_in=v_w_ffn_in, v_w_ffn_out=v_w_ffn_out)
    weights = {n: given[n] for n in TWIN_WEIGHTS}
    shared = {n: given[n] for n in SHARED_INPUTS}
    per_example = {n: given[n] for n in ['x', 'mem']}
    grad_fn = _jax.value_and_grad(_loss, argnums=(0, 1))

    def one_microbatch(ex, loss_target):
        ex = dict(ex)
        diff = ex.pop(TWIN_DIFF_INPUT)
        return grad_fn(weights, diff, {**shared, **ex}, loss_target)

    if N_MICROBATCH == 1:
        loss, (grad_w, grad_x) = one_microbatch(per_example, given["loss_target"])
    else:
        def body(carry, xs):
            loss_sum, grad_sum = carry
            l_k, (gw_k, gx_k) = one_microbatch(xs[0], xs[1])
            with _jax.named_scope("update"):
                return (loss_sum + l_k, _jax.tree.map(_jnp.add, grad_sum, gw_k)), gx_k

        init = (_jnp.zeros((), _jnp.float32), _jax.tree.map(_jnp.zeros_like, weights))
        (loss, grad_w), grad_x = _jax.lax.scan(body, init, (per_example, given["loss_target"]))
    with _jax.named_scope("update"):
        delta_w, new_m, new_v = {}, {}, {}
        for n in TWIN_WEIGHTS:
            delta_w[n], new_m[n], new_v[n] = _adamw(weights[n], grad_w[n], given["m_" + n], given["v_" + n])
    return (loss, grad_x, *[grad_w[n] for n in TWIN_WEIGHTS], *[delta_w[n] for n in TWIN_WEIGHTS],
            *[new_m[n] for n in TWIN_WEIGHTS], *[new_v[n] for n in TWIN_WEIGHTS])
```

```python
import functools

import jax
import jax.numpy as jnp
from jax import lax
from jax.experimental import pallas as pl
from jax.experimental.pallas import tpu as pltpu

F32 = jnp.float32
BF16 = jnp.bfloat16
MESH = pl.DeviceIdType.MESH

D_MODEL = 1024
SEQ = 2048
HEAD_DIM = 64
SB_W = 512
DIL_W = 256
MEM_W = 512
MEM_LEN = 256
D_IN = 3 * SB_W + 9 * DIL_W + MEM_W
D_FF = 2816
DIL_D = (1, 4, 16)
ROPE_THETA = 10000.0
NORM_EPS = 1e-6
NEG_INF = -1e30
LANES = 128

ADAM_LR = 0.001
ADAM_B1 = 0.9
ADAM_B2 = 0.999
ADAM_EPS = 1e-08
ADAM_WD = 0.01
ADAM_STEP = 10

N_CHIPS = 4
PACK = (
    ("w_in", (1024, 1088), 1),
    ("w_mem_kv", (256, 1024), 0),
    ("w_br_sb", (512, 256), 1),
    ("w_br_dil", (256, 256), 1),
    ("w_br_mem", (512, 256), 1),
    ("w_gate", (1024, 768), 1),
    ("w_o", (256, 1024), 0),
    ("w_ffn_in", (1024, 1408), 1),
    ("w_ffn_out", (704, 1024), 0),
)
PACK_ROWS = sum(a * b for _, (a, b), _ in PACK) // D_MODEL
HALF_ROWS = PACK_ROWS // 2


def _dot(a, b):
    return lax.dot_general(a, b, (((1,), (0,)), ((), ())), preferred_element_type=F32)


def _dot_nt(a, b):
    return lax.dot_general(a, b, (((1,), (1,)), ((), ())), preferred_element_type=F32)


def _dot_tn(a, b):
    return lax.dot_general(a, b, (((0,), (0,)), ((), ())), preferred_element_type=F32)


def _split_dot(x, u):
    hi = x.astype(BF16)
    lo = (x - hi.astype(F32)).astype(BF16)
    return _dot(hi, u) + _dot(lo, u)


def _mm(pairs, *, nt, tm, tn, out_dtypes, name, bias=None, extras=(), epilogue=None):
    m = pairs[0][0].shape[0]
    n = pairs[0][1].shape[0] if nt else pairs[0][1].shape[1]
    n_pairs, n_extra, n_out = len(pairs), len(extras), len(out_dtypes)
    tm = min(tm, m)
    assert m % tm == 0 and n % tn == 0
    has_bias = bias is not None

    def body(*refs):
        acc = None
        for i in range(n_pairs):
            a, b = refs[2 * i][...], refs[2 * i + 1][...]
            p = _dot_nt(a, b) if nt else _dot(a, b)
            acc = p if acc is None else acc + p
        pos = 2 * n_pairs
        if has_bias:
            acc = acc + refs[pos][...]
            pos += 1
        ex = [r[...] for r in refs[pos:pos + n_extra]]
        outs = refs[pos + n_extra:]
        vals = (acc,) if epilogue is None else epilogue(acc, *ex)
        for r, v, dt in zip(outs, vals, out_dtypes):
            r[...] = v.astype(dt)

    in_specs, args = [], []
    for a, b in pairs:
        k = a.shape[1]
        in_specs.append(pl.BlockSpec((tm, k), lambda j, i: (i, 0)))
        in_specs.append(pl.BlockSpec((tn, k), lambda j, i: (j, 0)) if nt else pl.BlockSpec((k, tn), lambda j, i: (0, j)))
        args += [a, b]
    if has_bias:
        in_specs.append(pl.BlockSpec((1, tn), lambda j, i: (0, j)))
        args.append(bias)
    for e in extras:
        in_specs.append(pl.BlockSpec((tm, tn), lambda j, i: (i, j)))
        args.append(e)
    out = pl.pallas_call(
        body,
        grid=(n // tn, m // tm),
        in_specs=in_specs,
        out_specs=[pl.BlockSpec((tm, tn), lambda j, i: (i, j)) for _ in range(n_out)],
        out_shape=[jax.ShapeDtypeStruct((m, n), dt) for dt in out_dtypes],
        compiler_params=pltpu.CompilerParams(dimension_semantics=("parallel", "parallel")),
        name=name,
    )(*args)
    return out[0] if n_out == 1 else out


def _mm_tn(a, b, *, tm, tn, tk, name):
    k, m = a.shape
    n = b.shape[1]
    tk = min(tk, k)
    assert m % tm == 0 and n % tn == 0 and k % tk == 0

    def body(a_ref, b_ref, o_ref):
        @pl.when(pl.program_id(2) == 0)
        def _():
            o_ref[...] = jnp.zeros_like(o_ref)

        o_ref[...] += _dot_tn(a_ref[...], b_ref[...])

    return pl.pallas_call(
        body,
        grid=(m // tm, n // tn, k // tk),
        in_specs=[pl.BlockSpec((tk, tm), lambda i, j, kk: (kk, i)), pl.BlockSpec((tk, tn), lambda i, j, kk: (kk, j))],
        out_specs=pl.BlockSpec((tm, tn), lambda i, j, kk: (i, j)),
        out_shape=jax.ShapeDtypeStruct((m, n), F32),
        compiler_params=pltpu.CompilerParams(dimension_semantics=("parallel", "parallel", "arbitrary")),
        name=name,
    )(a, b)


def _rowwise(fn, ins, outs, *, tm, name):
    rows = next(a.shape[0] for a, kind in ins if kind == "row")
    tm = min(tm, rows)
    assert rows % tm == 0
    n_in = len(ins)

    def body(*refs):
        vals = fn(*[r[...] for r in refs[:n_in]])
        for (_, dt, kind), r, v in zip(outs, refs[n_in:], vals):
            if kind == "row":
                r[...] = v.astype(dt)
            else:
                @pl.when(pl.program_id(0) == 0)
                def _(r=r):
                    r[...] = jnp.zeros_like(r)

                r[...] += v

    in_specs = [pl.BlockSpec((tm, a.shape[1]), lambda i: (i, 0)) if kind == "row" else pl.BlockSpec(a.shape, lambda i: (0, 0))
                for a, kind in ins]
    out_specs = [pl.BlockSpec((tm, c), lambda i: (i, 0)) if kind == "row" else pl.BlockSpec((1, c), lambda i: (0, 0))
                 for c, _, kind in outs]
    out_shape = [jax.ShapeDtypeStruct((rows if kind == "row" else 1, c), dt) for c, dt, kind in outs]
    has_acc = any(kind == "acc" for _, _, kind in outs)
    return pl.pallas_call(
        body,
        grid=(rows // tm,),
        in_specs=in_specs,
        out_specs=out_specs,
        out_shape=out_shape,
        compiler_params=pltpu.CompilerParams(dimension_semantics=("arbitrary" if has_acc else "parallel",)),
        name=name,
    )(*[a for a, _ in ins])


def _rstd(x):
    return lax.rsqrt(jnp.mean(x * x, axis=-1, keepdims=True) + NORM_EPS)


def _norm_bwd(dout, xin, g):
    r = _rstd(xin)
    n = xin * r
    dn = dout * g
    dg = jnp.sum(dout * n, axis=0, keepdims=True)
    dx = r * (dn - n * jnp.mean(dn * n, axis=-1, keepdims=True))
    return dx, dg


def _sigmoid(x):
    return 1.0 / (1.0 + jnp.exp(-x))


def _norm_fwd(x, g, *, name):
    def fn(x, g):
        return ((x * _rstd(x)) * g,)

    return _rowwise(fn, [(x, "row"), (g, "vec")], [(D_MODEL, BF16, "row")], tm=512, name=name)[0]


def _mid_fwd(mix, x, g_post_mix, g_pre_ffn):
    def fn(mix, x, g2, g3):
        x1 = x + (mix * _rstd(mix)) * g2
        return x1, (x1 * _rstd(x1)) * g3

    return _rowwise(fn, [(mix, "row"), (x, "row"), (g_post_mix, "vec"), (g_pre_ffn, "vec")],
                    [(D_MODEL, F32, "row"), (D_MODEL, BF16, "row")], tm=512, name="mid_fwd")


def _loss_bwd(f2, x1, g_post_ffn, tgt):
    def fn(f2, x1, g4, tgt):
        r = _rstd(f2)
        n = f2 * r
        err = x1 + n * g4 - tgt
        loss = 0.5 * jnp.sum(jnp.mean(err * err, axis=-1, keepdims=True), axis=0, keepdims=True)
        dy = err * (1.0 / D_MODEL)
        dn = dy * g4
        dg4 = jnp.sum(dy * n, axis=0, keepdims=True)
        df2 = r * (dn - n * jnp.mean(dn * n, axis=-1, keepdims=True))
        return dy, df2, dg4, jnp.broadcast_to(loss, (1, LANES))

    return _rowwise(fn, [(f2, "row"), (x1, "row"), (g_post_ffn, "vec"), (tgt, "row")],
                    [(D_MODEL, F32, "row"), (D_MODEL, BF16, "row"), (D_MODEL, F32, "acc"), (LANES, F32, "acc")],
                    tm=512, name="loss_bwd")


def _mid_bwd(dh2, x1, mix, g_pre_ffn, g_post_mix, dy):
    def fn(dh2, x1, mix, g3, g2, dy):
        d3, dg3 = _norm_bwd(dh2, x1, g3)
        dx1 = dy + d3
        dmix, dg2 = _norm_bwd(dx1, mix, g2)
        return dx1, dmix, dg3, dg2

    return _rowwise(fn, [(dh2, "row"), (x1, "row"), (mix, "row"), (g_pre_ffn, "vec"), (g_post_mix, "vec"), (dy, "row")],
                    [(D_MODEL, F32, "row"), (D_MODEL, BF16, "row"), (D_MODEL, F32, "acc"), (D_MODEL, F32, "acc")],
                    tm=256, name="mid_bwd")


def _first_bwd(dh, x, g_pre_mix, dx1):
    def fn(dh, x, g1, dx1):
        d1, dg1 = _norm_bwd(dh, x, g1)
        return dx1 + d1, dg1

    return _rowwise(fn, [(dh, "row"), (x, "row"), (g_pre_mix, "vec"), (dx1, "row")],
                    [(D_MODEL, F32, "row"), (D_MODEL, F32, "acc")], tm=512, name="first_bwd")


def _mem_norm_bwd(dhm, mem, g_mem):
    def fn(dhm, mem, g):
        return (jnp.sum(dhm * (mem * _rstd(mem)), axis=0, keepdims=True),)

    return _rowwise(fn, [(dhm, "row"), (mem, "row"), (g_mem, "vec")], [(D_MODEL, F32, "acc")], tm=512, name="mem_norm_bwd")[0]


def _gate_bwd(dmerged, gates, ya, yb, yc):
    def fn(dm, gt, ya, yb, yc):
        gt = gt.astype(F32)
        outs, dgp = [], []
        for i, y in enumerate((ya, yb, yc)):
            gi = gt[:, i * D_MODEL:(i + 1) * D_MODEL]
            outs.append(dm * gi)
            dgp.append(dm * y.astype(F32) * gi * (1.0 - gi))
        dgpre = jnp.concatenate(dgp, axis=1)
        return outs[0], outs[1], outs[2], dgpre, jnp.sum(dgpre, axis=0, keepdims=True)

    return _rowwise(fn, [(dmerged, "row"), (gates, "row"), (ya, "row"), (yb, "row"), (yc, "row")],
                    [(D_MODEL, BF16, "row")] * 3 + [(3 * D_MODEL, BF16, "row"), (3 * D_MODEL, F32, "acc")],
                    tm=256, name="gate_bwd")


def _adamw(w, g, m, v, *, tm, name):
    def fn(w, g, m, v):
        m = ADAM_B1 * m + (1.0 - ADAM_B1) * g
        v = ADAM_B2 * v + (1.0 - ADAM_B2) * (g * g)
        m_hat = m / (1.0 - ADAM_B1 ** ADAM_STEP)
        v_hat = v / (1.0 - ADAM_B2 ** ADAM_STEP)
        delta = -ADAM_LR * (m_hat / (jnp.sqrt(v_hat) + ADAM_EPS) + ADAM_WD * w)
        return delta, m, v

    c = w.shape[1]
    return _rowwise(fn, [(w, "row"), (g, "row"), (m, "row"), (v, "row")], [(c, F32, "row")] * 3, tm=tm, name=name)


def _ffn_in_fwd(h2, wg, wu):
    m, tm, tn = h2.shape[0], 512, 1408

    def body(h_ref, wg_ref, wu_ref, g_ref, u_ref, f_ref):
        h = h_ref[...]
        g = _dot(h, wg_ref[...])
        u = _dot(h, wu_ref[...])
        g_ref[...] = g.astype(BF16)
        u_ref[...] = u.astype(BF16)
        f_ref[...] = (g * _sigmoid(g) * u).astype(BF16)

    w_spec = pl.BlockSpec((D_MODEL, tn), lambda j, i: (0, j))
    o_spec = pl.BlockSpec((tm, tn), lambda j, i: (i, j))
    return pl.pallas_call(
        body,
        grid=(D_FF // tn, m // tm),
        in_specs=[pl.BlockSpec((tm, D_MODEL), lambda j, i: (i, 0)), w_spec, w_spec],
        out_specs=[o_spec, o_spec, o_spec],
        out_shape=[jax.ShapeDtypeStruct((m, D_FF), BF16)] * 3,
        compiler_params=pltpu.CompilerParams(dimension_semantics=("parallel", "parallel")),
        name="ffn_in_fwd",
    )(h2, wg, wu)


def _swiglu_bwd_epilogue(df, g, u):
    g = g.astype(F32)
    u = u.astype(F32)
    sg = _sigmoid(g)
    return df * u * (sg * (1.0 + g * (1.0 - sg))), df * (g * sg)


def _branch_merge_fwd(o_a, o_b, o_c, w_sb, w_dil, w_mem, gates):
    m, tm = o_a.shape[0], 256

    def body(oa_ref, ob_ref, oc_ref, wa_ref, wb_ref, wc_ref, gt_ref, ya_ref, yb_ref, yc_ref, mg_ref):
        ya = _dot(oa_ref[...], wa_ref[...])
        yb = _dot(ob_ref[...], wb_ref[...])
        yc = _dot(oc_ref[...], wc_ref[...])
        gt = gt_ref[...].astype(F32)
        ya_ref[...] = ya.astype(BF16)
        yb_ref[...] = yb.astype(BF16)
        yc_ref[...] = yc.astype(BF16)
        mg_ref[...] = (gt[:, :D_MODEL] * ya + gt[:, D_MODEL:2 * D_MODEL] * yb + gt[:, 2 * D_MODEL:] * yc).astype(BF16)

    row = lambda c: pl.BlockSpec((tm, c), lambda i: (i, 0))
    full = lambda a: pl.BlockSpec(a.shape, lambda i: (0, 0))
    return pl.pallas_call(
        body,
        grid=(m // tm,),
        in_specs=[row(SB_W), row(DIL_W), row(MEM_W), full(w_sb), full(w_dil), full(w_mem), row(3 * D_MODEL)],
        out_specs=[row(D_MODEL)] * 4,
        out_shape=[jax.ShapeDtypeStruct((m, D_MODEL), BF16)] * 4,
        compiler_params=pltpu.CompilerParams(dimension_semantics=("parallel",)),
        name="branch_merge_fwd",
    )(o_a, o_b, o_c, w_sb, w_dil, w_mem, gates)


SB_T = 256
SB_SCALE = HEAD_DIM ** -0.5


def _sb_masks():
    row = lax.broadcasted_iota(jnp.int32, (SB_T, SB_T), 0)
    col = lax.broadcasted_iota(jnp.int32, (SB_T, SB_T), 1)
    lane = lax.broadcasted_iota(jnp.int32, (1, LANES), 1)
    return row, col, lane


def _sb_logs(z):
    e = jnp.exp(-jnp.abs(z))
    sp = jnp.log1p(e)
    return jnp.minimum(z, 0.0) - sp, jnp.minimum(-z, 0.0) - sp, e


def _sb_specs(n_heads_pairs, col0):
    q = pl.BlockSpec((None, SB_T, LANES), lambda b, p, i: (b, i, col0 + p))
    k = pl.BlockSpec((None, SEQ, LANES), lambda b, p, i: (b, 0, col0 + n_heads_pairs + p))
    v = pl.BlockSpec((None, SEQ, LANES), lambda b, p, i: (b, 0, col0 + 2 * n_heads_pairs + p))
    return q, k, v


def _sb_fwd(proj3):
    bl = proj3.shape[0]
    n_pairs = SB_W // LANES

    def body(q_ref, k_ref, v_ref, o_ref, o32_ref):
        i = pl.program_id(2)
        row, col, lane = _sb_masks()
        causal = col < row
        u_excl = (row > col).astype(BF16)
        q = q_ref[...]
        acc = jnp.zeros((SB_T, LANES), F32)
        for h in range(2):
            mh = (lane // HEAD_DIM) == h
            qh = jnp.where(mh, q, jnp.zeros_like(q)) * SB_SCALE

            def block(j, carry, acc, diag, mh=mh, qh=qh):
                start = pl.multiple_of(j * SB_T, SB_T)
                kj = k_ref[pl.ds(start, SB_T), :]
                vj = v_ref[pl.ds(start, SB_T), :]
                lb, lk, _ = _sb_logs(_dot_nt(qh, kj))
                if diag:
                    lk = jnp.where(causal, lk, 0.0)
                r = _split_dot(lk, u_excl)
                w = jnp.exp(lb + r + carry)
                if diag:
                    w = jnp.where(causal, w, 0.0)
                acc = acc + _dot(w.astype(BF16), jnp.where(mh, vj, jnp.zeros_like(vj)))
                return carry + (r[:, :1] + lk[:, :1]), acc

            carry, acc = block(i, jnp.zeros((SB_T, 1), F32), acc, True)
            carry, acc = lax.fori_loop(0, i, lambda jj, c: block(i - 1 - jj, c[0], c[1], False), (carry, acc))
        o_ref[...] = acc.astype(BF16)
        o32_ref[...] = acc

    q_spec, k_spec, v_spec = _sb_specs(n_pairs, 0)
    blk = pl.BlockSpec((None, SB_T, LANES), lambda b, p, i: (b, i, p))
    return pl.pallas_call(
        body,
        grid=(bl, n_pairs, SEQ // SB_T),
        in_specs=[q_spec, k_spec, v_spec],
        out_specs=[blk, blk],
        out_shape=[jax.ShapeDtypeStruct((bl, SEQ, SB_W), BF16), jax.ShapeDtypeStruct((bl, SEQ, SB_W), F32)],
        compiler_params=pltpu.CompilerParams(dimension_semantics=("parallel", "parallel", "arbitrary")),
        name="sb_fwd",
    )(proj3, proj3, proj3)


def _sb_bwd(proj3, o_a, do_a):
    bl = proj3.shape[0]
    n_pairs = SB_W // LANES
    nq = SEQ // SB_T

    def body(q_ref, k_ref, v_ref, o_ref, do_ref, dq_ref, dk_ref, dv_ref, dk_acc, dv_acc):
        i = pl.program_id(2)

        @pl.when(i == 0)
        def _():
            dk_acc[...] = jnp.zeros_like(dk_acc)
            dv_acc[...] = jnp.zeros_like(dv_acc)

        row, col, lane = _sb_masks()
        causal = col < row
        u_excl = (row > col).astype(BF16)
        u_incl = (row >= col).astype(BF16)
        q = q_ref[...]
        do = do_ref[...]
        prod = do.astype(F32) * o_ref[...]
        dq = jnp.zeros((SB_T, LANES), F32)
        for h in range(2):
            mh = (lane // HEAD_DIM) == h
            qh = jnp.where(mh, q, jnp.zeros_like(q)) * SB_SCALE
            doh = jnp.where(mh, do, jnp.zeros_like(do))
            d_tot = jnp.sum(jnp.where(mh, prod, 0.0), axis=1, keepdims=True)

            def block(j, carry, c_da, dq, diag, mh=mh, qh=qh, doh=doh, d_tot=d_tot):
                start = pl.multiple_of(j * SB_T, SB_T)
                kj = k_ref[pl.ds(start, SB_T), :]
                vj = v_ref[pl.ds(start, SB_T), :]
                z = _dot_nt(qh, kj)
                lb, lk, e = _sb_logs(z)
                if diag:
                    lk = jnp.where(causal, lk, 0.0)
                r = _split_dot(lk, u_excl)
                w = jnp.exp(lb + r + carry)
                if diag:
                    w = jnp.where(causal, w, 0.0)
                wb = w.astype(BF16)
                da = _dot_nt(doh, vj) * wb.astype(F32)
                sfx = _split_dot(da, u_incl)
                dlk = d_tot - c_da - sfx
                if diag:
                    dlk = jnp.where(causal, dlk, 0.0)
                inv = 1.0 / (1.0 + e)
                pos = z >= 0.0
                beta = jnp.where(pos, inv, e * inv)
                one_m_beta = jnp.where(pos, e * inv, inv)
                dz = (da * one_m_beta - dlk * beta).astype(BF16)
                dq = dq + _dot(dz, jnp.where(mh, kj, jnp.zeros_like(kj)))
                dk_acc[pl.ds(start, SB_T), :] += _dot_tn(dz, qh)
                dv_acc[pl.ds(start, SB_T), :] += _dot_tn(wb, doh)
                return carry + (r[:, :1] + lk[:, :1]), c_da + sfx[:, :1], dq

            zero = jnp.zeros((SB_T, 1), F32)
            carry, c_da, dq = block(i, zero, zero, dq, True)
            carry, c_da, dq = lax.fori_loop(0, i, lambda jj, c: block(i - 1 - jj, c[0], c[1], c[2], False), (carry, c_da, dq))
        dq_ref[...] = (dq * SB_SCALE).astype(BF16)

        @pl.when(i == nq - 1)
        def _():
            dk_ref[...] = dk_acc[...].astype(BF16)
            dv_ref[...] = dv_acc[...].astype(BF16)

    q_spec, k_spec, v_spec = _sb_specs(n_pairs, 0)
    blk = pl.BlockSpec((None, SB_T, LANES), lambda b, p, i: (b, i, p))
    seq = pl.BlockSpec((None, SEQ, LANES), lambda b, p, i: (b, 0, p))
    shape = jax.ShapeDtypeStruct((bl, SEQ, SB_W), BF16)
    return pl.pallas_call(
        body,
        grid=(bl, n_pairs, nq),
        in_specs=[q_spec, k_spec, v_spec, blk, blk],
        out_specs=[blk, seq, seq],
        out_shape=[shape, shape, shape],
        scratch_shapes=[pltpu.VMEM((SEQ, LANES), F32), pltpu.VMEM((SEQ, LANES), F32)],
        compiler_params=pltpu.CompilerParams(dimension_semantics=("parallel", "parallel", "arbitrary")),
        name="sb_bwd",
    )(proj3, proj3, proj3, o_a, do_a)


BAND = 128


def _swap_half(x):
    lane = lax.broadcasted_iota(jnp.int32, (1, LANES), 1)
    return jnp.where((lane % HEAD_DIM) < HEAD_DIM // 2, pltpu.roll(x, LANES - HEAD_DIM // 2, 1), pltpu.roll(x, HEAD_DIM // 2, 1))


def _rope(x, cos, sin_signed):
    x = x.astype(F32)
    return x * cos + _swap_half(x) * sin_signed


def _band_valid(g, i):
    nb = jnp.where(g == 0, 16, jnp.where(g == 1, 4, 1))
    first_key = jnp.where(lax.rem(i, nb) != 0, 0, BAND)
    qi = lax.broadcasted_iota(jnp.int32, (BAND, 2 * BAND), 0) + BAND
    kj = lax.broadcasted_iota(jnp.int32, (BAND, 2 * BAND), 1)
    dist = qi - kj
    return (dist >= 0) & (dist <= BAND) & (kj >= first_key)


def _band_specs():
    cur = pl.BlockSpec((None, None, BAND, LANES), lambda g, i, p: (g, i, 0, p))
    prev = pl.BlockSpec((None, None, BAND, LANES), lambda g, i, p: (g, jnp.maximum(i - 1, 0), 0, p))
    tab = pl.BlockSpec((None, None, BAND, LANES), lambda g, i, p: (g, lax.rem(i, 16), 0, 0))
    tab_prev = pl.BlockSpec((None, None, BAND, LANES), lambda g, i, p: (g, lax.rem(jnp.maximum(i - 1, 0), 16), 0, 0))
    return cur, prev, tab, tab_prev


def _band_load(q_ref, kc_ref, kp_ref, vc_ref, vp_ref, cc_ref, sc_ref, cp_ref, sp_ref):
    q = (_rope(q_ref[...], cc_ref[...], sc_ref[...]) * SB_SCALE).astype(BF16)
    kc = _rope(kc_ref[...], cc_ref[...], sc_ref[...]).astype(BF16)
    kp = _rope(kp_ref[...], cp_ref[...], sp_ref[...]).astype(BF16)
    k2 = jnp.concatenate([kp, kc], axis=0)
    v2 = jnp.concatenate([vp_ref[...], vc_ref[...]], axis=0)
    return q, k2, v2


def _band_fwd(q_s, k_s, v_s, cos_t, sin_t):
    def body(q_ref, kc_ref, kp_ref, vc_ref, vp_ref, cc_ref, sc_ref, cp_ref, sp_ref, o_ref, lse_ref):
        valid = _band_valid(pl.program_id(0), pl.program_id(1))
        q, k2, v2 = _band_load(q_ref, kc_ref, kp_ref, vc_ref, vp_ref, cc_ref, sc_ref, cp_ref, sp_ref)
        lane = lax.broadcasted_iota(jnp.int32, (1, LANES), 1)
        o = jnp.zeros((BAND, LANES), F32)
        lse = jnp.zeros((BAND, LANES), F32)
        for h in range(2):
            mh = (lane // HEAD_DIM) == h
            s = jnp.where(valid, _dot_nt(jnp.where(mh, q, jnp.zeros_like(q)), k2), NEG_INF)
            m = jnp.max(s, axis=1, keepdims=True)
            p = jnp.exp(s - m)
            den = jnp.sum(p, axis=1, keepdims=True)
            o = o + _dot(p.astype(BF16), jnp.where(mh, v2, jnp.zeros_like(v2))) * (1.0 / den)
            lse = jnp.where(mh, m + jnp.log(den), lse)
        o_ref[...] = o
        lse_ref[...] = lse

    cur, prev, tab, tab_prev = _band_specs()
    shape = jax.ShapeDtypeStruct(q_s.shape, F32)
    return pl.pallas_call(
        body,
        grid=(3, q_s.shape[1], 2),
        in_specs=[cur, cur, prev, cur, prev, tab, tab, tab_prev, tab_prev],
        out_specs=[cur, cur],
        out_shape=[shape, shape],
        compiler_params=pltpu.CompilerParams(dimension_semantics=("parallel", "parallel", "parallel")),
        name="band_fwd",
    )(q_s, k_s, k_s, v_s, v_s, cos_t, sin_t, cos_t, sin_t)


def _band_bwd(q_s, k_s, v_s, cos_t, sin_t, do_s, lse_s, delta_s):
    def body(q_ref, kc_ref, kp_ref, vc_ref, vp_ref, cc_ref, sc_ref, cp_ref, sp_ref, do_ref, lse_ref, dl_ref,
             dq_ref, dkc_ref, dkp_ref, dvc_ref, dvp_ref):
        valid = _band_valid(pl.program_id(0), pl.program_id(1))
        q, k2, v2 = _band_load(q_ref, kc_ref, kp_ref, vc_ref, vp_ref, cc_ref, sc_ref, cp_ref, sp_ref)
        lane = lax.broadcasted_iota(jnp.int32, (1, LANES), 1)
        do = do_ref[...]
        lse = lse_ref[...]
        dl = dl_ref[...]
        dq = jnp.zeros((BAND, LANES), F32)
        dk2 = jnp.zeros((2 * BAND, LANES), F32)
        dv2 = jnp.zeros((2 * BAND, LANES), F32)
        for h in range(2):
            mh = (lane // HEAD_DIM) == h
            qh = jnp.where(mh, q, jnp.zeros_like(q))
            doh = jnp.where(mh, do, jnp.zeros_like(do))
            c0 = h * HEAD_DIM
            p = jnp.where(valid, jnp.exp(_dot_nt(qh, k2) - lse[:, c0:c0 + 1]), 0.0)
            ds = (p * (_dot_nt(doh, v2) - dl[:, c0:c0 + 1])).astype(BF16)
            dq = dq + _dot(ds, jnp.where(mh, k2, jnp.zeros_like(k2)))
            dk2 = dk2 + _dot_tn(ds, qh)
            dv2 = dv2 + _dot_tn(p.astype(BF16), doh)
        dq_ref[...] = dq * SB_SCALE
        dkp_ref[...] = dk2[:BAND]
        dkc_ref[...] = dk2[BAND:]
        dvp_ref[...] = dv2[:BAND]
        dvc_ref[...] = dv2[BAND:]

    cur, prev, tab, tab_prev = _band_specs()
    shape = jax.ShapeDtypeStruct(q_s.shape, F32)
    return pl.pallas_call(
        body,
        grid=(3, q_s.shape[1], 2),
        in_specs=[cur, cur, prev, cur, prev, tab, tab, tab_prev, tab_prev, cur, cur, cur],
        out_specs=[cur] * 5,
        out_shape=[shape] * 5,
        compiler_params=pltpu.CompilerParams(dimension_semantics=("parallel", "parallel", "parallel")),
        name="band_bwd",
    )(q_s, k_s, k_s, v_s, v_s, cos_t, sin_t, cos_t, sin_t, do_s, lse_s, delta_s)


def _band_combine(dq, dkc, dkp, dvc, dvp, cos_t, sin_t):
    n_blk = dq.shape[1]

    def body(dq_ref, dkc_ref, dkn_ref, dvc_ref, dvn_ref, c_ref, s_ref, oq_ref, ok_ref, ov_ref):
        nxt = (pl.program_id(1) < n_blk - 1).astype(F32)
        cos, sin = c_ref[...], s_ref[...]
        dq = dq_ref[...]
        dk = dkc_ref[...] + nxt * dkn_ref[...]
        oq_ref[...] = (dq * cos - _swap_half(dq) * sin).astype(BF16)
        ok_ref[...] = (dk * cos - _swap_half(dk) * sin).astype(BF16)
        ov_ref[...] = (dvc_ref[...] + nxt * dvn_ref[...]).astype(BF16)

    cur, _, tab, _ = _band_specs()
    nxt = pl.BlockSpec((None, None, BAND, LANES), lambda g, i, p: (g, jnp.minimum(i + 1, n_blk - 1), 0, p))
    shape = jax.ShapeDtypeStruct(dq.shape, BF16)
    return pl.pallas_call(
        body,
        grid=(3, n_blk, 2),
        in_specs=[cur, cur, nxt, cur, nxt, tab, tab],
        out_specs=[cur] * 3,
        out_shape=[shape] * 3,
        compiler_params=pltpu.CompilerParams(dimension_semantics=("parallel", "parallel", "parallel")),
        name="band_combine",
    )(dq, dkc, dkp, dvc, dvp, cos_t, sin_t)


def _band_merge(o_g, lse_g):
    t, tm = o_g.shape[1], 512

    def body(o_ref, l_ref, ob_ref, lse_ref):
        l0, l1, l2 = l_ref[0], l_ref[1], l_ref[2]
        m = jnp.maximum(jnp.maximum(l0, l1), l2)
        lse = m + jnp.log(jnp.exp(l0 - m) + jnp.exp(l1 - m) + jnp.exp(l2 - m))
        ob_ref[...] = (jnp.exp(l0 - lse) * o_ref[0] + jnp.exp(l1 - lse) * o_ref[1] + jnp.exp(l2 - lse) * o_ref[2]).astype(BF16)
        lse_ref[...] = lse

    spec3 = pl.BlockSpec((3, tm, DIL_W), lambda i: (0, i, 0))
    spec = pl.BlockSpec((tm, DIL_W), lambda i: (i, 0))
    return pl.pallas_call(
        body,
        grid=(t // tm,),
        in_specs=[spec3, spec3],
        out_specs=[spec, spec],
        out_shape=[jax.ShapeDtypeStruct((t, DIL_W), BF16), jax.ShapeDtypeStruct((t, DIL_W), F32)],
        compiler_params=pltpu.CompilerParams(dimension_semantics=("parallel",)),
        name="band_merge",
    )(o_g, lse_g)


def _band_delta(do_b, o_b):
    def fn(do, o):
        r = lax.broadcasted_iota(jnp.int32, (DIL_W, DIL_W), 0) // HEAD_DIM
        c = lax.broadcasted_iota(jnp.int32, (DIL_W, DIL_W), 1) // HEAD_DIM
        return (_split_dot(do.astype(F32) * o.astype(F32), (r == c).astype(BF16)),)

    return _rowwise(fn, [(do_b, "row"), (o_b, "row")], [(DIL_W, F32, "row")], tm=512, name="band_delta")[0]


MEM_T = 512
MEM_SCALE = 128 ** -0.5
MEM_Q_COL = (D_IN - MEM_W) // LANES


def _mem_specs():
    q = pl.BlockSpec((None, MEM_T, LANES), lambda b, h, i: (b, i, MEM_Q_COL + h))
    k = pl.BlockSpec((None, MEM_LEN, LANES), lambda b, h, i: (b, 0, h))
    v = pl.BlockSpec((None, MEM_LEN, LANES), lambda b, h, i: (b, 0, MEM_W // LANES + h))
    blk = pl.BlockSpec((None, MEM_T, LANES), lambda b, h, i: (b, i, h))
    return q, k, v, blk


def _mem_probs(q, k):
    s = _dot_nt(q, k) * MEM_SCALE
    p = jnp.exp(s - jnp.max(s, axis=1, keepdims=True))
    return p * (1.0 / jnp.sum(p, axis=1, keepdims=True))


def _mem_fwd(proj3, kv3):
    bl = proj3.shape[0]

    def body(q_ref, k_ref, v_ref, o_ref):
        p = _mem_probs(q_ref[...], k_ref[...])
        o_ref[...] = _dot(p.astype(BF16), v_ref[...]).astype(BF16)

    q, k, v, blk = _mem_specs()
    return pl.pallas_call(
        body,
        grid=(bl, MEM_W // LANES, SEQ // MEM_T),
        in_specs=[q, k, v],
        out_specs=blk,
        out_shape=jax.ShapeDtypeStruct((bl, SEQ, MEM_W), BF16),
        compiler_params=pltpu.CompilerParams(dimension_semantics=("parallel", "parallel", "parallel")),
        name="mem_fwd",
    )(proj3, kv3, kv3)


def _mem_bwd(proj3, kv3, do_c):
    bl = proj3.shape[0]

    def body(q_ref, k_ref, v_ref, do_ref, dq_ref, dk_ref, dv_ref):
        @pl.when(pl.program_id(2) == 0)
        def _():
            dk_ref[...] = jnp.zeros_like(dk_ref)
            dv_ref[...] = jnp.zeros_like(dv_ref)

        q, k, do = q_ref[...], k_ref[...], do_ref[...]
        p = _mem_probs(q, k)
        dp = _dot_nt(do, v_ref[...])
        ds = (p * (dp - jnp.sum(p * dp, axis=1, keepdims=True)) * MEM_SCALE).astype(BF16)
        dq_ref[...] = _dot(ds, k).astype(BF16)
        dk_ref[...] += _dot_tn(ds, q)
        dv_ref[...] += _dot_tn(p.astype(BF16), do)

    q, k, v, blk = _mem_specs()
    kv_out = pl.BlockSpec((None, MEM_LEN, LANES), lambda b, h, i: (b, 0, h))
    return pl.pallas_call(
        body,
        grid=(bl, MEM_W // LANES, SEQ // MEM_T),
        in_specs=[q, k, v, blk],
        out_specs=[blk, kv_out, kv_out],
        out_shape=[jax.ShapeDtypeStruct((bl, SEQ, MEM_W), BF16), jax.ShapeDtypeStruct((bl, MEM_LEN, MEM_W), F32),
                   jax.ShapeDtypeStruct((bl, MEM_LEN, MEM_W), F32)],
        compiler_params=pltpu.CompilerParams(dimension_semantics=("parallel", "parallel", "arbitrary")),
        name="mem_bwd",
    )(proj3, kv3, kv3, do_c)


def _place():
    x, y, c = lax.axis_index("x"), lax.axis_index("y"), lax.axis_index("c")
    return x, y, c


def _other_chips(x, y):
    return [(1 - x, y), (x, 1 - y), (1 - x, 1 - y)]


def _remote(src, dst, send_sem, recv_sem, to):
    return pltpu.make_async_remote_copy(src_ref=src, dst_ref=dst, send_sem=send_sem, recv_sem=recv_sem,
                                        device_id=to, device_id_type=MESH)


ANY = pl.BlockSpec(memory_space=pl.ANY)


def _gather_weights(shard):
    def body(in_ref, out_ref, send_sems, recv_sems, local_sem):
        x, y, c = _place()
        sibling = (x, y, 1 - c)
        chips = _other_chips(x, y)

        def half(px, py, pc):
            return out_ref.at[2 * px + py, pl.ds(pc * HALF_ROWS, HALF_ROWS), :]

        mine = pltpu.make_async_copy(in_ref, out_ref.at[2 * x + y], local_sem)
        mine.start()
        src = in_ref.at[pl.ds(c * HALF_ROWS, HALF_ROWS), :]
        first = [_remote(src, half(x, y, c), send_sems.at[j], recv_sems.at[j], (*chip, c)) for j, chip in enumerate(chips)]
        for cp in first:
            cp.start()
        passed = [_remote(half(*chip, c), half(*chip, c), send_sems.at[3 + j], recv_sems.at[3 + j], sibling)
                  for j, chip in enumerate(chips)]
        for j, chip in enumerate(chips):
            _remote(src, half(*chip, c), send_sems.at[j], recv_sems.at[j], (*chip, c)).wait_recv()
            passed[j].start()
        for j, chip in enumerate(chips):
            _remote(src, half(*chip, 1 - c), send_sems.at[3 + j], recv_sems.at[3 + j], sibling).wait_recv()
        for cp in first + passed:
            cp.wait_send()
        mine.wait()

    return pl.pallas_call(
        body,
        in_specs=[ANY],
        out_specs=ANY,
        out_shape=jax.ShapeDtypeStruct((N_CHIPS, PACK_ROWS, D_MODEL), BF16),
        scratch_shapes=[pltpu.SemaphoreType.DMA((6,)), pltpu.SemaphoreType.DMA((6,)), pltpu.SemaphoreType.DMA],
        name="gather_weights",
    )(shard)


def _pair_exchange(grads):
    def body(g_ref, land_ref, send_sem, recv_sem):
        x, y, c = _place()
        src = g_ref.at[:, pl.ds((1 - c) * HALF_ROWS, HALF_ROWS), :]
        cp = _remote(src, land_ref, send_sem, recv_sem, (x, y, 1 - c))
        cp.start()
        cp.wait()

    return pl.pallas_call(
        body,
        in_specs=[ANY],
        out_specs=ANY,
        out_shape=jax.ShapeDtypeStruct((N_CHIPS, HALF_ROWS, D_MODEL), F32),
        scratch_shapes=[pltpu.SemaphoreType.DMA, pltpu.SemaphoreType.DMA],
        name="pair_exchange",
    )(grads)


def _pair_add(grads, land, c_arr):
    tr = 480

    def body(c_ref, g_ref, l_ref, o_ref):
        o_ref[...] = g_ref[...] + l_ref[...]

    g4 = grads.reshape(N_CHIPS, 2, HALF_ROWS, D_MODEL)
    return pl.pallas_call(
        body,
        grid_spec=pltpu.PrefetchScalarGridSpec(
            num_scalar_prefetch=1,
            grid=(N_CHIPS, HALF_ROWS // tr),
            in_specs=[pl.BlockSpec((None, None, tr, D_MODEL), lambda s, r, c_ref: (s, c_ref[0], r, 0)),
                      pl.BlockSpec((None, tr, D_MODEL), lambda s, r, c_ref: (s, r, 0))],
            out_specs=pl.BlockSpec((None, tr, D_MODEL), lambda s, r, c_ref: (s, r, 0)),
        ),
        out_shape=jax.ShapeDtypeStruct((N_CHIPS, HALF_ROWS, D_MODEL), F32),
        compiler_params=pltpu.CompilerParams(dimension_semantics=("parallel", "parallel")),
        name="pair_add",
    )(c_arr, g4, land)


def _chip_exchange(part):
    def body(p_ref, land_ref, send_sems, recv_sems, local_sem):
        x, y, c = _place()
        me = 2 * x + y
        mine = pltpu.make_async_copy(p_ref.at[me], land_ref.at[me], local_sem)
        mine.start()
        chips = _other_chips(x, y)
        sends = [_remote(p_ref.at[2 * cx + cy], land_ref.at[me], send_sems.at[j], recv_sems.at[j], (cx, cy, c))
                 for j, (cx, cy) in enumerate(chips)]
        for cp in sends:
            cp.start()
        for j, (cx, cy) in enumerate(chips):
            _remote(p_ref.at[me], land_ref.at[2 * cx + cy], send_sems.at[j], recv_sems.at[j], (cx, cy, c)).wait_recv()
        for cp in sends:
            cp.wait_send()
        mine.wait()

    return pl.pallas_call(
        body,
        in_specs=[ANY],
        out_specs=ANY,
        out_shape=jax.ShapeDtypeStruct((N_CHIPS, HALF_ROWS, D_MODEL), F32),
        scratch_shapes=[pltpu.SemaphoreType.DMA((3,)), pltpu.SemaphoreType.DMA((3,)), pltpu.SemaphoreType.DMA],
        name="chip_exchange",
    )(part)


def _chip_add(land):
    tr = 480

    def body(l_ref, o_ref):
        o_ref[...] = ((l_ref[0] + l_ref[1]) + l_ref[2]) + l_ref[3]

    return pl.pallas_call(
        body,
        grid=(HALF_ROWS // tr,),
        in_specs=[pl.BlockSpec((N_CHIPS, tr, D_MODEL), lambda r: (0, r, 0))],
        out_specs=pl.BlockSpec((tr, D_MODEL), lambda r: (r, 0)),
        out_shape=jax.ShapeDtypeStruct((HALF_ROWS, D_MODEL), F32),
        compiler_params=pltpu.CompilerParams(dimension_semantics=("parallel",)),
        name="chip_add",
    )(land)


def _pair_share(half):
    def body(h_ref, out_ref, send_sem, recv_sem, local_sem):
        x, y, c = _place()
        mine = pltpu.make_async_copy(h_ref, out_ref.at[c], local_sem)
        mine.start()
        cp = _remote(h_ref, out_ref.at[c], send_sem, recv_sem, (x, y, 1 - c))
        cp.start()
        _remote(h_ref, out_ref.at[1 - c], send_sem, recv_sem, (x, y, 1 - c)).wait_recv()
        cp.wait_send()
        mine.wait()

    return pl.pallas_call(
        body,
        in_specs=[ANY],
        out_specs=ANY,
        out_shape=jax.ShapeDtypeStruct((2, HALF_ROWS, D_MODEL), F32),
        scratch_shapes=[pltpu.SemaphoreType.DMA, pltpu.SemaphoreType.DMA, pltpu.SemaphoreType.DMA],
        name="pair_share",
    )(half)


def _all_sum_small(part):
    def body(p_ref, o_ref, slots, send_sems, recv_sems):
        x, y, c = _place()
        me = 4 * x + 2 * y + c
        slots[me] = p_ref[...]
        peers = [(x ^ dx, y ^ dy, c ^ dc) for dx in (0, 1) for dy in (0, 1) for dc in (0, 1)][1:]
        sends = [_remote(p_ref, slots.at[me], send_sems.at[k], recv_sems.at[k], peer) for k, peer in enumerate(peers)]
        for cp in sends:
            cp.start()
        for k, (px, py, pc) in enumerate(peers):
            _remote(p_ref, slots.at[4 * px + 2 * py + pc], send_sems.at[k], recv_sems.at[k], (px, py, pc)).wait_recv()
        for cp in sends:
            cp.wait_send()
        acc = slots[0]
        for d in range(1, 8):
            acc = acc + slots[d]
        o_ref[...] = acc

    vmem = pl.BlockSpec(memory_space=pltpu.VMEM)
    return pl.pallas_call(
        body,
        in_specs=[vmem],
        out_specs=vmem,
        out_shape=jax.ShapeDtypeStruct(part.shape, F32),
        scratch_shapes=[pltpu.VMEM((8,) + part.shape, F32), pltpu.SemaphoreType.DMA((7,)), pltpu.SemaphoreType.DMA((7,))],
        name="all_sum_small",
    )(part)


def _pack_shard(parts):
    return jnp.concatenate([p.reshape(-1) for p in parts]).reshape(PACK_ROWS, D_MODEL)


def _unpack_shard(flat):
    lead = flat.shape[:-2]
    flat = flat.reshape(lead + (PACK_ROWS * D_MODEL,))
    out, pos = {}, 0
    for name, (a, b), _ in PACK:
        out[name] = flat[..., pos:pos + a * b].reshape(lead + (a, b))
        pos += a * b
    return out


def _full_weights(gathered):
    parts = _unpack_shard(gathered)
    return {name: jnp.concatenate([parts[name][s] for s in range(N_CHIPS)], axis=axis) for name, _, axis in PACK}


def _pack_grads(grads):
    slots = []
    for s in range(N_CHIPS):
        parts = []
        for name, (a, b), axis in PACK:
            g = grads[name]
            parts.append(g[:, s * b:(s + 1) * b] if axis == 1 else g[s * a:(s + 1) * a, :])
        slots.append(_pack_shard(parts))
    return jnp.stack(slots)


def _deinterleave(a, d):
    b, s, c = a.shape
    return a.reshape(b, s // d, d, c).transpose(0, 2, 1, 3).reshape(b * s // BAND, BAND, c)


def _reinterleave(a, d, b):
    c = a.shape[-1]
    return a.reshape(b, d, SEQ // d, c).transpose(0, 2, 1, 3).reshape(b, SEQ, c)


def _rope_tables():
    half = HEAD_DIM // 2
    inv_freq = ROPE_THETA ** (-jnp.arange(half, dtype=F32) * 2.0 / HEAD_DIM)
    ang = jnp.arange(SEQ, dtype=F32)[:, None] * inv_freq[None, :]
    cos = jnp.tile(jnp.cos(ang), (1, 4))
    sin = jnp.tile(jnp.concatenate([-jnp.sin(ang), jnp.sin(ang)], axis=1), (1, 2))
    cos_t = jnp.stack([_deinterleave(cos[None], d) for d in DIL_D])
    sin_t = jnp.stack([_deinterleave(sin[None], d) for d in DIL_D])
    return cos_t, sin_t


def _local_step(x, mem, loss_target, g_pre_mix, g_post_mix, g_pre_ffn, g_post_ffn, g_mem, b_gate, w):
    bl = x.shape[0]
    t = bl * SEQ
    w_ffn_g, w_ffn_u = w["w_ffn_in"][:, :D_FF], w["w_ffn_in"][:, D_FF:]

    x2 = x.reshape(t, D_MODEL)
    tgt2 = loss_target.reshape(t, D_MODEL)
    mem2 = mem.reshape(bl * MEM_LEN, D_MODEL)

    h = _norm_fwd(x2, g_pre_mix, name="norm_x")
    proj = _mm([(h, w["w_in"])], nt=False, tm=512, tn=2176, out_dtypes=[BF16], name="proj")
    gates = _mm([(h, w["w_gate"])], nt=False, tm=512, tn=1536, out_dtypes=[BF16], name="gates", bias=b_gate,
                epilogue=lambda acc: (_sigmoid(acc),))
    hm = _norm_fwd(mem2, g_mem, name="norm_mem")
    kv_m = _mm([(hm, w["w_mem_kv"])], nt=False, tm=512, tn=1024, out_dtypes=[BF16], name="mem_kv")
    proj3 = proj.reshape(bl, SEQ, D_IN)
    kv3 = kv_m.reshape(bl, MEM_LEN, 2 * MEM_W)

    o_a, o_a32 = _sb_fwd(proj3)

    cos_t, sin_t = _rope_tables()
    dil0 = 3 * SB_W

    def group_cols(g, part):
        c0 = dil0 + (3 * g + part) * DIL_W
        return proj3[:, :, c0:c0 + DIL_W]

    q_s = jnp.stack([_deinterleave(group_cols(g, 0), d) for g, d in enumerate(DIL_D)])
    k_s = jnp.stack([_deinterleave(group_cols(g, 1), d) for g, d in enumerate(DIL_D)])
    v_s = jnp.stack([_deinterleave(group_cols(g, 2), d) for g, d in enumerate(DIL_D)])
    o_gs, lse_gs = _band_fwd(q_s, k_s, v_s, cos_t, sin_t)
    o_g = jnp.stack([_reinterleave(o_gs[g], d, bl) for g, d in enumerate(DIL_D)]).reshape(3, t, DIL_W)
    lse_g = jnp.stack([_reinterleave(lse_gs[g], d, bl) for g, d in enumerate(DIL_D)]).reshape(3, t, DIL_W)
    o_b, lse_b = _band_merge(o_g, lse_g)

    o_c = _mem_fwd(proj3, kv3)

    o_a2, o_c2 = o_a.reshape(t, SB_W), o_c.reshape(t, MEM_W)
    y_a, y_b, y_c, merged = _branch_merge_fwd(o_a2, o_b, o_c2, w["w_br_sb"], w["w_br_dil"], w["w_br_mem"], gates)
    mix = _mm([(merged, w["w_o"])], nt=False, tm=512, tn=1024, out_dtypes=[F32], name="mix")
    x1, h2 = _mid_fwd(mix, x2, g_post_mix, g_pre_ffn)
    gg, uu, f = _ffn_in_fwd(h2, w_ffn_g, w_ffn_u)
    f2 = _mm([(f, w["w_ffn_out"])], nt=False, tm=512, tn=1024, out_dtypes=[F32], name="ffn_out")

    dy, df2, dg_post_ffn, loss_row = _loss_bwd(f2, x1, g_post_ffn, tgt2)

    dg_ffn, du_ffn = _mm([(df2, w["w_ffn_out"])], nt=True, tm=512, tn=1408, out_dtypes=[BF16, BF16], name="d_ffn_act",
                         extras=(gg, uu), epilogue=_swiglu_bwd_epilogue)
    gw = {}
    gw["w_ffn_out"] = _mm_tn(f, df2, tm=1408, tn=1024, tk=512, name="gw_ffn_out")
    gw_ffn_g = _mm_tn(h2, dg_ffn, tm=1024, tn=1408, tk=512, name="gw_ffn_gate")
    gw_ffn_u = _mm_tn(h2, du_ffn, tm=1024, tn=1408, tk=512, name="gw_ffn_up")
    gw["w_ffn_in"] = jnp.concatenate([gw_ffn_g, gw_ffn_u], axis=1)
    dh2 = _mm([(dg_ffn, w_ffn_g), (du_ffn, w_ffn_u)], nt=True, tm=256, tn=1024, out_dtypes=[F32], name="d_h2")
    dx1, dmix, dg_pre_ffn, dg_post_mix = _mid_bwd(dh2, x1, mix, g_pre_ffn, g_post_mix, dy)

    gw["w_o"] = _mm_tn(merged, dmix, tm=1024, tn=1024, tk=512, name="gw_o")
    dmerged = _mm([(dmix, w["w_o"])], nt=True, tm=512, tn=1024, out_dtypes=[F32], name="d_merged")
    dy_a, dy_b, dy_c, dgpre, db_gate = _gate_bwd(dmerged, gates, y_a, y_b, y_c)
    gw["w_br_sb"] = _mm_tn(o_a2, dy_a, tm=512, tn=1024, tk=512, name="gw_br_sb")
    gw["w_br_dil"] = _mm_tn(o_b, dy_b, tm=256, tn=1024, tk=512, name="gw_br_dil")
    gw["w_br_mem"] = _mm_tn(o_c2, dy_c, tm=512, tn=1024, tk=512, name="gw_br_mem")
    gw["w_gate"] = _mm_tn(h, dgpre, tm=1024, tn=1536, tk=512, name="gw_gate")
    do_a = _mm([(dy_a, w["w_br_sb"])], nt=True, tm=512, tn=SB_W, out_dtypes=[BF16], name="d_o_a")
    do_b = _mm([(dy_b, w["w_br_dil"])], nt=True, tm=512, tn=DIL_W, out_dtypes=[BF16], name="d_o_b")
    do_c = _mm([(dy_c, w["w_br_mem"])], nt=True, tm=512, tn=MEM_W, out_dtypes=[BF16], name="d_o_c")

    dq_c, dk_m, dv_m = _mem_bwd(proj3, kv3, do_c.reshape(bl, SEQ, MEM_W))
    dkv_m = jnp.concatenate([dk_m, dv_m], axis=-1).reshape(bl * MEM_LEN, 2 * MEM_W).astype(BF16)
    gw["w_mem_kv"] = _mm_tn(hm, dkv_m, tm=1024, tn=1024, tk=512, name="gw_mem_kv")
    dhm = _mm([(dkv_m, w["w_mem_kv"])], nt=True, tm=512, tn=1024, out_dtypes=[F32], name="d_hm")
    dg_mem = _mem_norm_bwd(dhm, mem2, g_mem)

    delta_b = _band_delta(do_b, o_b)
    do_b3, lse_b3, delta_b3 = (a.reshape(bl, SEQ, DIL_W) for a in (do_b, lse_b, delta_b))
    do_s = jnp.stack([_deinterleave(do_b3, d) for d in DIL_D])
    lse_s = jnp.stack([_deinterleave(lse_b3, d) for d in DIL_D])
    delta_s = jnp.stack([_deinterleave(delta_b3, d) for d in DIL_D])
    dq_r, dkc, dkp, dvc, dvp = _band_bwd(q_s, k_s, v_s, cos_t, sin_t, do_s, lse_s, delta_s)
    dq_s, dk_s, dv_s = _band_combine(dq_r, dkc, dkp, dvc, dvp, cos_t, sin_t)
    d_dil = []
    for g, d in enumerate(DIL_D):
        d_dil += [_reinterleave(a[g], d, bl) for a in (dq_s, dk_s, dv_s)]

    dq_a, dk_a, dv_a = _sb_bwd(proj3, o_a32, do_a.reshape(bl, SEQ, SB_W))

    dproj = jnp.concatenate([dq_a, dk_a, dv_a] + d_dil + [dq_c], axis=-1).reshape(t, D_IN)
    gw["w_in"] = _mm_tn(h, dproj, tm=1024, tn=2176, tk=512, name="gw_in")
    dh = _mm([(dproj, w["w_in"]), (dgpre, w["w_gate"])], nt=True, tm=256, tn=1024, out_dtypes=[F32], name="d_h")
    grad_x, dg_pre_mix = _first_bwd(dh, x2, g_pre_mix, dx1)
    small = jnp.concatenate([dg_pre_mix, dg_post_mix, dg_pre_ffn, dg_post_ffn, dg_mem, db_gate.reshape(3, D_MODEL)], axis=0)
    return loss_row[0, 0], grad_x.reshape(bl, SEQ, D_MODEL), gw, small


def kernel(x, mem, g_pre_mix, g_post_mix, g_pre_ffn, g_post_ffn, g_mem, w_in, w_mem_kv, w_br_sb, w_br_dil, w_br_mem, w_gate, b_gate, w_o, w_ffn_in, w_ffn_out, loss_target, m_g_pre_mix, m_g_post_mix, m_g_pre_ffn, m_g_post_ffn, m_g_mem, m_w_in, m_w_mem_kv, m_w_br_sb, m_w_br_dil, m_w_br_mem, m_w_gate, m_b_gate, m_w_o, m_w_ffn_in, m_w_ffn_out, v_g_pre_mix, v_g_post_mix, v_g_pre_ffn, v_g_post_ffn, v_g_mem, v_w_in, v_w_mem_kv, v_w_br_sb, v_w_br_dil, v_w_br_mem, v_w_gate, v_b_gate, v_w_o, v_w_ffn_in, v_w_ffn_out):
    w_shards = dict(w_in=w_in[0], w_mem_kv=w_mem_kv[0], w_br_sb=w_br_sb[0], w_br_dil=w_br_dil[0], w_br_mem=w_br_mem[0],
                    w_gate=w_gate[0], w_o=w_o[0], w_ffn_in=w_ffn_in[0], w_ffn_out=w_ffn_out[0])
    m_shards = dict(w_in=m_w_in[0], w_mem_kv=m_w_mem_kv[0], w_br_sb=m_w_br_sb[0], w_br_dil=m_w_br_dil[0], w_br_mem=m_w_br_mem[0],
                    w_gate=m_w_gate[0], w_o=m_w_o[0], w_ffn_in=m_w_ffn_in[0], w_ffn_out=m_w_ffn_out[0])
    v_shards = dict(w_in=v_w_in[0], w_mem_kv=v_w_mem_kv[0], w_br_sb=v_w_br_sb[0], w_br_dil=v_w_br_dil[0], w_br_mem=v_w_br_mem[0],
                    w_gate=v_w_gate[0], w_o=v_w_o[0], w_ffn_in=v_w_ffn_in[0], w_ffn_out=v_w_ffn_out[0])

    shard_bf = _pack_shard([w_shards[name].astype(BF16) for name, _, _ in PACK])
    w = _full_weights(_gather_weights(shard_bf))

    loss_local, grad_x, gw, small = _local_step(x, mem, loss_target, g_pre_mix, g_post_mix, g_pre_ffn, g_post_ffn, g_mem, b_gate, w)
    loss = lax.psum(loss_local, ("x", "y", "c"))

    c_arr = lax.axis_index("c").astype(jnp.int32).reshape(1)
    packed = _pack_grads(gw)
    part = _pair_add(packed, _pair_exchange(packed), c_arr)
    red = _pair_share(_chip_add(_chip_exchange(part))).reshape(PACK_ROWS, D_MODEL)
    g_shards = _unpack_shard(red)
    small = _all_sum_small(small)

    upd = {}
    for name, (a, b), _ in PACK:
        tm = a // 2 if a % 16 == 0 else a
        upd[name] = _adamw(w_shards[name], g_shards[name], m_shards[name], v_shards[name], tm=tm, name="adamw_" + name)

    def small8(gs, b):
        return jnp.concatenate(gs + [b.reshape(3, D_MODEL)], axis=0)

    sw = small8([g_pre_mix, g_post_mix, g_pre_ffn, g_post_ffn, g_mem], b_gate)
    sm = small8([m_g_pre_mix, m_g_post_mix, m_g_pre_ffn, m_g_post_ffn, m_g_mem], m_b_gate)
    sv = small8([v_g_pre_mix, v_g_post_mix, v_g_pre_ffn, v_g_post_ffn, v_g_mem], v_b_gate)
    s_upd = _adamw(sw, small, sm, sv, tm=8, name="adamw_small")

    def small_out(a):
        return [a[0:1], a[1:2], a[2:3], a[3:4], a[4:5]]

    order = ["w_in", "w_mem_kv", "w_br_sb", "w_br_dil", "w_br_mem", "w_gate", "b_gate", "w_o", "w_ffn_in", "w_ffn_out"]

    def leaves(small_arr, big):
        out = small_out(small_arr)
        for name in order:
            out.append(small_arr[5:8].reshape(1, 3 * D_MODEL) if name == "b_gate" else big[name][None])
        return out

    grads_out = leaves(small, g_shards)
    delta_out = leaves(s_upd[0], {n: u[0] for n, u in upd.items()})
    m_out = leaves(s_upd[1], {n: u[1] for n, u in upd.items()})
    v_out = leaves(s_upd[2], {n: u[2] for n, u in upd.items()})
    return (loss, grad_x, *grads_out, *delta_out, *m_out, *v_out)
```

```python
import functools

import jax
import jax.numpy as jnp
from jax import lax
from jax.experimental import pallas as pl
from jax.experimental.pallas import tpu as pltpu

F32 = jnp.float32
BF16 = jnp.bfloat16
MESH = pl.DeviceIdType.MESH

D_MODEL = 1024
SEQ = 2048
HEAD_DIM = 64
SB_W = 512
DIL_W = 256
MEM_W = 512
MEM_LEN = 256
D_IN = 3 * SB_W + 9 * DIL_W + MEM_W
D_FF = 2816
DIL_D = (1, 4, 16)
ROPE_THETA = 10000.0
NORM_EPS = 1e-6
NEG_INF = -1e30
LANES = 128

ADAM_LR = 0.001
ADAM_B1 = 0.9
ADAM_B2 = 0.999
ADAM_EPS = 1e-08
ADAM_WD = 0.01
ADAM_STEP = 10

N_CHIPS = 4
PACK = (
    ("w_in", (1024, 1088), 1),
    ("w_mem_kv", (256, 1024), 0),
    ("w_br_sb", (512, 256), 1),
    ("w_br_dil", (256, 256), 1),
    ("w_br_mem", (512, 256), 1),
    ("w_gate", (1024, 768), 1),
    ("w_o", (256, 1024), 0),
    ("w_ffn_in", (1024, 1408), 1),
    ("w_ffn_out", (704, 1024), 0),
)
PACK_ROWS = sum(a * b for _, (a, b), _ in PACK) // D_MODEL
HALF_ROWS = PACK_ROWS // 2


def _dot(a, b):
    return lax.dot_general(a, b, (((1,), (0,)), ((), ())), preferred_element_type=F32)


def _dot_nt(a, b):
    return lax.dot_general(a, b, (((1,), (1,)), ((), ())), preferred_element_type=F32)


def _dot_tn(a, b):
    return lax.dot_general(a, b, (((0,), (0,)), ((), ())), preferred_element_type=F32)


def _split_dot(x, u):
    hi = x.astype(BF16)
    lo = (x - hi.astype(F32)).astype(BF16)
    return _dot(hi, u) + _dot(lo, u)


def _mm(pairs, *, nt, tm, tn, out_dtypes, name, bias=None, extras=(), epilogue=None):
    m = pairs[0][0].shape[0]
    n = pairs[0][1].shape[0] if nt else pairs[0][1].shape[1]
    n_pairs, n_extra, n_out = len(pairs), len(extras), len(out_dtypes)
    tm = min(tm, m)
    assert m % tm == 0 and n % tn == 0
    has_bias = bias is not None

    def body(*refs):
        acc = None
        for i in range(n_pairs):
            a, b = refs[2 * i][...], refs[2 * i + 1][...]
            p = _dot_nt(a, b) if nt else _dot(a, b)
            acc = p if acc is None else acc + p
        pos = 2 * n_pairs
        if has_bias:
            acc = acc + refs[pos][...]
            pos += 1
        ex = [r[...] for r in refs[pos:pos + n_extra]]
        outs = refs[pos + n_extra:]
        vals = (acc,) if epilogue is None else epilogue(acc, *ex)
        for r, v, dt in zip(outs, vals, out_dtypes):
            r[...] = v.astype(dt)

    in_specs, args = [], []
    for a, b in pairs:
        k = a.shape[1]
        in_specs.append(pl.BlockSpec((tm, k), lambda j, i: (i, 0)))
        in_specs.append(pl.BlockSpec((tn, k), lambda j, i: (j, 0)) if nt else pl.BlockSpec((k, tn), lambda j, i: (0, j)))
        args += [a, b]
    if has_bias:
        in_specs.append(pl.BlockSpec((1, tn), lambda j, i: (0, j)))
        args.append(bias)
    for e in extras:
        in_specs.append(pl.BlockSpec((tm, tn), lambda j, i: (i, j)))
        args.append(e)
    out = pl.pallas_call(
        body,
        grid=(n // tn, m // tm),
        in_specs=in_specs,
        out_specs=[pl.BlockSpec((tm, tn), lambda j, i: (i, j)) for _ in range(n_out)],
        out_shape=[jax.ShapeDtypeStruct((m, n), dt) for dt in out_dtypes],
        compiler_params=pltpu.CompilerParams(dimension_semantics=("parallel", "parallel")),
        name=name,
    )(*args)
    return out[0] if n_out == 1 else out


def _mm_tn(a, b, *, tm, tn, tk, name):
    k, m = a.shape
    n = b.shape[1]
    tk = min(tk, k)
    assert m % tm == 0 and n % tn == 0 and k % tk == 0

    def body(a_ref, b_ref, o_ref):
        @pl.when(pl.program_id(2) == 0)
        def _():
            o_ref[...] = jnp.zeros_like(o_ref)

        o_ref[...] += _dot_tn(a_ref[...], b_ref[...])

    return pl.pallas_call(
        body,
        grid=(m // tm, n // tn, k // tk),
        in_specs=[pl.BlockSpec((tk, tm), lambda i, j, kk: (kk, i)), pl.BlockSpec((tk, tn), lambda i, j, kk: (kk, j))],
        out_specs=pl.BlockSpec((tm, tn), lambda i, j, kk: (i, j)),
        out_shape=jax.ShapeDtypeStruct((m, n), F32),
        compiler_params=pltpu.CompilerParams(dimension_semantics=("parallel", "parallel", "arbitrary")),
        name=name,
    )(a, b)


def _rowwise(fn, ins, outs, *, tm, name):
    rows = next(a.shape[0] for a, kind in ins if kind == "row")
    tm = min(tm, rows)
    assert rows % tm == 0
    n_in = len(ins)

    def body(*refs):
        vals = fn(*[r[...] for r in refs[:n_in]])
        for (_, dt, kind), r, v in zip(outs, refs[n_in:], vals):
            if kind == "row":
                r[...] = v.astype(dt)
            else:
                @pl.when(pl.program_id(0) == 0)
                def _(r=r):
                    r[...] = jnp.zeros_like(r)

                r[...] += v

    in_specs = [pl.BlockSpec((tm, a.shape[1]), lambda i: (i, 0)) if kind == "row" else pl.BlockSpec(a.shape, lambda i: (0, 0))
                for a, kind in ins]
    out_specs = [pl.BlockSpec((tm, c), lambda i: (i, 0)) if kind == "row" else pl.BlockSpec((1, c), lambda i: (0, 0))
                 for c, _, kind in outs]
    out_shape = [jax.ShapeDtypeStruct((rows if kind == "row" else 1, c), dt) for c, dt, kind in outs]
    has_acc = any(kind == "acc" for _, _, kind in outs)
    return pl.pallas_call(
        body,
        grid=(rows // tm,),
        in_specs=in_specs,
        out_specs=out_specs,
        out_shape=out_shape,
        compiler_params=pltpu.CompilerParams(dimension_semantics=("arbitrary" if has_acc else "parallel",)),
        name=name,
    )(*[a for a, _ in ins])


def _rstd(x):
    return lax.rsqrt(jnp.mean(x * x, axis=-1, keepdims=True) + NORM_EPS)


def _norm_bwd(dout, xin, g):
    r = _rstd(xin)
    n = xin * r
    dn = dout * g
    dg = jnp.sum(dout * n, axis=0, keepdims=True)
    dx = r * (dn - n * jnp.mean(dn * n, axis=-1, keepdims=True))
    return dx, dg


def _sigmoid(x):
    return 1.0 / (1.0 + jnp.exp(-x))


def _norm_fwd(x, g, *, name):
    def fn(x, g):
        return ((x * _rstd(x)) * g,)

    return _rowwise(fn, [(x, "row"), (g, "vec")], [(D_MODEL, BF16, "row")], tm=512, name=name)[0]


def _mid_fwd(mix, x, g_post_mix, g_pre_ffn):
    def fn(mix, x, g2, g3):
        x1 = x + (mix * _rstd(mix)) * g2
        return x1, (x1 * _rstd(x1)) * g3

    return _rowwise(fn, [(mix, "row"), (x, "row"), (g_post_mix, "vec"), (g_pre_ffn, "vec")],
                    [(D_MODEL, F32, "row"), (D_MODEL, BF16, "row")], tm=512, name="mid_fwd")


def _loss_bwd(f2, x1, g_post_ffn, tgt):
    def fn(f2, x1, g4, tgt):
        r = _rstd(f2)
        n = f2 * r
        err = x1 + n * g4 - tgt
        loss = 0.5 * jnp.sum(jnp.mean(err * err, axis=-1, keepdims=True), axis=0, keepdims=True)
        dy = err * (1.0 / D_MODEL)
        dn = dy * g4
        dg4 = jnp.sum(dy * n, axis=0, keepdims=True)
        df2 = r * (dn - n * jnp.mean(dn * n, axis=-1, keepdims=True))
        return dy, df2, dg4, jnp.broadcast_to(loss, (1, LANES))

    return _rowwise(fn, [(f2, "row"), (x1, "row"), (g_post_ffn, "vec"), (tgt, "row")],
                    [(D_MODEL, F32, "row"), (D_MODEL, BF16, "row"), (D_MODEL, F32, "acc"), (LANES, F32, "acc")],
                    tm=512, name="loss_bwd")


def _mid_bwd(dh2, x1, mix, g_pre_ffn, g_post_mix, dy):
    def fn(dh2, x1, mix, g3, g2, dy):
        d3, dg3 = _norm_bwd(dh2, x1, g3)
        dx1 = dy + d3
        dmix, dg2 = _norm_bwd(dx1, mix, g2)
        return dx1, dmix, dg3, dg2

    return _rowwise(fn, [(dh2, "row"), (x1, "row"), (mix, "row"), (g_pre_ffn, "vec"), (g_post_mix, "vec"), (dy, "row")],
                    [(D_MODEL, F32, "row"), (D_MODEL, BF16, "row"), (D_MODEL, F32, "acc"), (D_MODEL, F32, "acc")],
                    tm=256, name="mid_bwd")


def _first_bwd(dh, x, g_pre_mix, dx1):
    def fn(dh, x, g1, dx1):
        d1, dg1 = _norm_bwd(dh, x, g1)
        return dx1 + d1, dg1

    return _rowwise(fn, [(dh, "row"), (x, "row"), (g_pre_mix, "vec"), (dx1, "row")],
                    [(D_MODEL, F32, "row"), (D_MODEL, F32, "acc")], tm=512, name="first_bwd")


def _mem_norm_bwd(dhm, mem, g_mem):
    def fn(dhm, mem, g):
        return (jnp.sum(dhm * (mem * _rstd(mem)), axis=0, keepdims=True),)

    return _rowwise(fn, [(dhm, "row"), (mem, "row"), (g_mem, "vec")], [(D_MODEL, F32, "acc")], tm=512, name="mem_norm_bwd")[0]


def _gate_bwd(dmerged, gates, ya, yb, yc):
    def fn(dm, gt, ya, yb, yc):
        gt = gt.astype(F32)
        outs, dgp = [], []
        for i, y in enumerate((ya, yb, yc)):
            gi = gt[:, i * D_MODEL:(i + 1) * D_MODEL]
            outs.append(dm * gi)
            dgp.append(dm * y.astype(F32) * gi * (1.0 - gi))
        dgpre = jnp.concatenate(dgp, axis=1)
        return outs[0], outs[1], outs[2], dgpre, jnp.sum(dgpre, axis=0, keepdims=True)

    return _rowwise(fn, [(dmerged, "row"), (gates, "row"), (ya, "row"), (yb, "row"), (yc, "row")],
                    [(D_MODEL, BF16, "row")] * 3 + [(3 * D_MODEL, BF16, "row"), (3 * D_MODEL, F32, "acc")],
                    tm=256, name="gate_bwd")


def _adamw(w, g, m, v, *, tm, name):
    def fn(w, g, m, v):
        m = ADAM_B1 * m + (1.0 - ADAM_B1) * g
        v = ADAM_B2 * v + (1.0 - ADAM_B2) * (g * g)
        m_hat = m / (1.0 - ADAM_B1 ** ADAM_STEP)
        v_hat = v / (1.0 - ADAM_B2 ** ADAM_STEP)
        delta = -ADAM_LR * (m_hat / (jnp.sqrt(v_hat) + ADAM_EPS) + ADAM_WD * w)
        return delta, m, v

    c = w.shape[1]
    return _rowwise(fn, [(w, "row"), (g, "row"), (m, "row"), (v, "row")], [(c, F32, "row")] * 3, tm=tm, name=name)


def _ffn_in_fwd(h2, wg, wu):
    m, tm, tn = h2.shape[0], 512, 1408

    def body(h_ref, wg_ref, wu_ref, g_ref, u_ref, f_ref):
        h = h_ref[...]
        g = _dot(h, wg_ref[...])
        u = _dot(h, wu_ref[...])
        g_ref[...] = g.astype(BF16)
        u_ref[...] = u.astype(BF16)
        f_ref[...] = (g * _sigmoid(g) * u).astype(BF16)

    w_spec = pl.BlockSpec((D_MODEL, tn), lambda j, i: (0, j))
    o_spec = pl.BlockSpec((tm, tn), lambda j, i: (i, j))
    return pl.pallas_call(
        body,
        grid=(D_FF // tn, m // tm),
        in_specs=[pl.BlockSpec((tm, D_MODEL), lambda j, i: (i, 0)), w_spec, w_spec],
        out_specs=[o_spec, o_spec, o_spec],
        out_shape=[jax.ShapeDtypeStruct((m, D_FF), BF16)] * 3,
        compiler_params=pltpu.CompilerParams(dimension_semantics=("parallel", "parallel")),
        name="ffn_in_fwd",
    )(h2, wg, wu)


def _swiglu_bwd_epilogue(df, g, u):
    g = g.astype(F32)
    u = u.astype(F32)
    sg = _sigmoid(g)
    return df * u * (sg * (1.0 + g * (1.0 - sg))), df * (g * sg)


def _branch_merge_fwd(o_a, o_b, o_c, w_sb, w_dil, w_mem, gates):
    m, tm = o_a.shape[0], 256

    def body(oa_ref, ob_ref, oc_ref, wa_ref, wb_ref, wc_ref, gt_ref, ya_ref, yb_ref, yc_ref, mg_ref):
        ya = _dot(oa_ref[...], wa_ref[...])
        yb = _dot(ob_ref[...], wb_ref[...])
        yc = _dot(oc_ref[...], wc_ref[...])
        gt = gt_ref[...].astype(F32)
        ya_ref[...] = ya.astype(BF16)
        yb_ref[...] = yb.astype(BF16)
        yc_ref[...] = yc.astype(BF16)
        mg_ref[...] = (gt[:, :D_MODEL] * ya + gt[:, D_MODEL:2 * D_MODEL] * yb + gt[:, 2 * D_MODEL:] * yc).astype(BF16)

    row = lambda c: pl.BlockSpec((tm, c), lambda i: (i, 0))
    full = lambda a: pl.BlockSpec(a.shape, lambda i: (0, 0))
    return pl.pallas_call(
        body,
        grid=(m // tm,),
        in_specs=[row(SB_W), row(DIL_W), row(MEM_W), full(w_sb), full(w_dil), full(w_mem), row(3 * D_MODEL)],
        out_specs=[row(D_MODEL)] * 4,
        out_shape=[jax.ShapeDtypeStruct((m, D_MODEL), BF16)] * 4,
        compiler_params=pltpu.CompilerParams(dimension_semantics=("parallel",)),
        name="branch_merge_fwd",
    )(o_a, o_b, o_c, w_sb, w_dil, w_mem, gates)


SB_T = 256
SB_SCALE = HEAD_DIM ** -0.5


def _sb_masks():
    row = lax.broadcasted_iota(jnp.int32, (SB_T, SB_T), 0)
    col = lax.broadcasted_iota(jnp.int32, (SB_T, SB_T), 1)
    lane = lax.broadcasted_iota(jnp.int32, (1, LANES), 1)
    return row, col, lane


def _sb_logs(z):
    e = jnp.exp(-jnp.abs(z))
    sp = jnp.log1p(e)
    return jnp.minimum(z, 0.0) - sp, jnp.minimum(-z, 0.0) - sp, e


def _sb_specs(n_heads_pairs, col0):
    q = pl.BlockSpec((None, SB_T, LANES), lambda b, p, i: (b, i, col0 + p))
    k = pl.BlockSpec((None, SEQ, LANES), lambda b, p, i: (b, 0, col0 + n_heads_pairs + p))
    v = pl.BlockSpec((None, SEQ, LANES), lambda b, p, i: (b, 0, col0 + 2 * n_heads_pairs + p))
    return q, k, v


def _sb_fwd(proj3):
    bl = proj3.shape[0]
    n_pairs = SB_W // LANES

    def body(q_ref, k_ref, v_ref, o_ref, o32_ref):
        i = pl.program_id(2)
        row, col, lane = _sb_masks()
        causal = col < row
        u_excl = (row > col).astype(BF16)
        q = q_ref[...]
        heads = []
        for h in range(2):
            mh = (lane // HEAD_DIM) == h
            heads.append((mh, jnp.where(mh, q, jnp.zeros_like(q)) * SB_SCALE))

        def blocks(js, carries, acc, diag):
            ks = [k_ref[pl.ds(pl.multiple_of(j * SB_T, SB_T), SB_T), :] for j in js]
            vs = [v_ref[pl.ds(pl.multiple_of(j * SB_T, SB_T), SB_T), :] for j in js]
            chains = [(b, h) for b in range(len(js)) for h in range(2)]
            z = {c: _dot_nt(heads[c[1]][1], ks[c[0]]) for c in chains}
            lb, lk = {}, {}
            for c in chains:
                lb[c], lk[c], _ = _sb_logs(z[c])
                if diag:
                    lk[c] = jnp.where(causal, lk[c], 0.0)
            r = {c: _split_dot(lk[c], u_excl) for c in chains}
            carries = list(carries)
            w = {}
            for b, h in chains:
                w_c = jnp.exp(lb[b, h] + r[b, h] + carries[h])
                w[b, h] = (jnp.where(causal, w_c, 0.0) if diag else w_c).astype(BF16)
                carries[h] = carries[h] + (r[b, h][:, :1] + lk[b, h][:, :1])
            for b, h in chains:
                acc = acc + _dot(w[b, h], jnp.where(heads[h][0], vs[b], jnp.zeros_like(vs[b])))
            return tuple(carries), acc

        zero = jnp.zeros((SB_T, 1), F32)
        carries, acc = blocks([i], (zero, zero), jnp.zeros((SB_T, LANES), F32), True)
        carries, acc = lax.fori_loop(0, i // 2, lambda jj, c: blocks([i - 1 - 2 * jj, i - 2 - 2 * jj], c[0], c[1], False),
                                     (carries, acc))
        carries, acc = lax.fori_loop(0, i % 2, lambda jj, c: blocks([0], c[0], c[1], False), (carries, acc))
        o_ref[...] = acc.astype(BF16)
        o32_ref[...] = acc

    q_spec, k_spec, v_spec = _sb_specs(n_pairs, 0)
    blk = pl.BlockSpec((None, SB_T, LANES), lambda b, p, i: (b, i, p))
    return pl.pallas_call(
        body,
        grid=(bl, n_pairs, SEQ // SB_T),
        in_specs=[q_spec, k_spec, v_spec],
        out_specs=[blk, blk],
        out_shape=[jax.ShapeDtypeStruct((bl, SEQ, SB_W), BF16), jax.ShapeDtypeStruct((bl, SEQ, SB_W), F32)],
        compiler_params=pltpu.CompilerParams(dimension_semantics=("parallel", "parallel", "arbitrary")),
        name="sb_fwd",
    )(proj3, proj3, proj3)


def _sb_bwd(proj3, o_a, do_a):
    bl = proj3.shape[0]
    n_pairs = SB_W // LANES
    nq = SEQ // SB_T

    def body(q_ref, k_ref, v_ref, o_ref, do_ref, dq_ref, dk_ref, dv_ref, dk_acc, dv_acc):
        i = pl.program_id(2)

        @pl.when(i == 0)
        def _():
            dk_acc[...] = jnp.zeros_like(dk_acc)
            dv_acc[...] = jnp.zeros_like(dv_acc)

        row, col, lane = _sb_masks()
        causal = col < row
        u_excl = (row > col).astype(BF16)
        u_incl = (row >= col).astype(BF16)
        q = q_ref[...]
        do = do_ref[...]
        prod = do.astype(F32) * o_ref[...]
        heads = []
        for h in range(2):
            mh = (lane // HEAD_DIM) == h
            d_tot = jnp.sum(jnp.where(mh, prod, 0.0), axis=1, keepdims=True)
            heads.append((mh, jnp.where(mh, q, jnp.zeros_like(q)) * SB_SCALE, jnp.where(mh, do, jnp.zeros_like(do)), d_tot))

        def blocks(js, carries, c_das, dq, diag):
            starts = [pl.multiple_of(j * SB_T, SB_T) for j in js]
            ks = [k_ref[pl.ds(s, SB_T), :] for s in starts]
            vs = [v_ref[pl.ds(s, SB_T), :] for s in starts]
            chains = [(b, h) for b in range(len(js)) for h in range(2)]
            z = {c: _dot_nt(heads[c[1]][1], ks[c[0]]) for c in chains}
            dw = {c: _dot_nt(heads[c[1]][2], vs[c[0]]) for c in chains}
            lb, lk, e = {}, {}, {}
            for c in chains:
                lb[c], lk[c], e[c] = _sb_logs(z[c])
                if diag:
                    lk[c] = jnp.where(causal, lk[c], 0.0)
            r = {c: _split_dot(lk[c], u_excl) for c in chains}
            carries, c_das = list(carries), list(c_das)
            wb, da = {}, {}
            for b, h in chains:
                w_c = jnp.exp(lb[b, h] + r[b, h] + carries[h])
                wb[b, h] = (jnp.where(causal, w_c, 0.0) if diag else w_c).astype(BF16)
                da[b, h] = dw[b, h] * wb[b, h].astype(F32)
                carries[h] = carries[h] + (r[b, h][:, :1] + lk[b, h][:, :1])
            sfx = {c: _split_dot(da[c], u_incl) for c in chains}
            dz = {}
            for b, h in chains:
                dlk = heads[h][3] - c_das[h] - sfx[b, h]
                if diag:
                    dlk = jnp.where(causal, dlk, 0.0)
                c_das[h] = c_das[h] + sfx[b, h][:, :1]
                inv = 1.0 / (1.0 + e[b, h])
                pos = z[b, h] >= 0.0
                beta = jnp.where(pos, inv, e[b, h] * inv)
                one_m_beta = jnp.where(pos, e[b, h] * inv, inv)
                dz[b, h] = (da[b, h] * one_m_beta - dlk * beta).astype(BF16)
            for b, h in chains:
                dq = dq + _dot(dz[b, h], jnp.where(heads[h][0], ks[b], jnp.zeros_like(ks[b])))
            for b in range(len(js)):
                dk_acc[pl.ds(starts[b], SB_T), :] += _dot_tn(dz[b, 0], heads[0][1]) + _dot_tn(dz[b, 1], heads[1][1])
                dv_acc[pl.ds(starts[b], SB_T), :] += _dot_tn(wb[b, 0], heads[0][2]) + _dot_tn(wb[b, 1], heads[1][2])
            return tuple(carries), tuple(c_das), dq

        zero = jnp.zeros((SB_T, 1), F32)
        state = blocks([i], (zero, zero), (zero, zero), jnp.zeros((SB_T, LANES), F32), True)
        state = lax.fori_loop(0, i // 2, lambda jj, c: blocks([i - 1 - 2 * jj, i - 2 - 2 * jj], c[0], c[1], c[2], False), state)
        state = lax.fori_loop(0, i % 2, lambda jj, c: blocks([0], c[0], c[1], c[2], False), state)
        dq_ref[...] = (state[2] * SB_SCALE).astype(BF16)

        @pl.when(i == nq - 1)
        def _():
            dk_ref[...] = dk_acc[...].astype(BF16)
            dv_ref[...] = dv_acc[...].astype(BF16)

    q_spec, k_spec, v_spec = _sb_specs(n_pairs, 0)
    blk = pl.BlockSpec((None, SB_T, LANES), lambda b, p, i: (b, i, p))
    seq = pl.BlockSpec((None, SEQ, LANES), lambda b, p, i: (b, 0, p))
    shape = jax.ShapeDtypeStruct((bl, SEQ, SB_W), BF16)
    return pl.pallas_call(
        body,
        grid=(bl, n_pairs, nq),
        in_specs=[q_spec, k_spec, v_spec, blk, blk],
        out_specs=[blk, seq, seq],
        out_shape=[shape, shape, shape],
        scratch_shapes=[pltpu.VMEM((SEQ, LANES), F32), pltpu.VMEM((SEQ, LANES), F32)],
        compiler_params=pltpu.CompilerParams(dimension_semantics=("parallel", "parallel", "arbitrary")),
        name="sb_bwd",
    )(proj3, proj3, proj3, o_a, do_a)


BAND = 128


BAND_CH = 4
BAND_HEADS = DIL_W // HEAD_DIM


def _swap_half(x):
    n = x.shape[-1]
    lane = lax.broadcasted_iota(jnp.int32, (1, n), 1)
    return jnp.where((lane % HEAD_DIM) < HEAD_DIM // 2, pltpu.roll(x, n - HEAD_DIM // 2, 1), pltpu.roll(x, HEAD_DIM // 2, 1))


def _rope(x, cos, sin_signed):
    x = x.astype(F32)
    return x * cos + _swap_half(x) * sin_signed


def _band_valid(g, blk):
    nb = jnp.where(g == 0, 16, jnp.where(g == 1, 4, 1))
    first_key = jnp.where(lax.rem(blk, nb) != 0, 0, BAND)
    qi = lax.broadcasted_iota(jnp.int32, (BAND, 2 * BAND), 0) + BAND
    kj = lax.broadcasted_iota(jnp.int32, (BAND, 2 * BAND), 1)
    dist = qi - kj
    return (dist >= 0) & (dist <= BAND) & (kj >= first_key)


def _band_specs():
    last_before = lambda i: jnp.maximum(i * BAND_CH - 1, 0)
    cur = pl.BlockSpec((None, BAND_CH, BAND, DIL_W), lambda g, i: (g, i, 0, 0))
    prev = pl.BlockSpec((None, None, BAND, DIL_W), lambda g, i: (g, last_before(i), 0, 0))
    tab = pl.BlockSpec((None, BAND_CH, BAND, DIL_W), lambda g, i: (g, lax.rem(i, 16 // BAND_CH), 0, 0))
    tab_prev = pl.BlockSpec((None, None, BAND, DIL_W), lambda g, i: (g, lax.rem(last_before(i), 16), 0, 0))
    return cur, prev, tab, tab_prev


def _band_load(q_ref, k_ref, kp_ref, v_ref, vp_ref, c_ref, s_ref, cp_ref, sp_ref):
    qs = [(_rope(q_ref[b], c_ref[b], s_ref[b]) * SB_SCALE).astype(BF16) for b in range(BAND_CH)]
    ks = [_rope(kp_ref[...], cp_ref[...], sp_ref[...]).astype(BF16)]
    ks += [_rope(k_ref[b], c_ref[b], s_ref[b]).astype(BF16) for b in range(BAND_CH)]
    vs = [vp_ref[...]] + [v_ref[b] for b in range(BAND_CH)]
    k2 = [jnp.concatenate([ks[b], ks[b + 1]], axis=0) for b in range(BAND_CH)]
    v2 = [jnp.concatenate([vs[b], vs[b + 1]], axis=0) for b in range(BAND_CH)]
    return qs, k2, v2


def _band_fwd(q_s, k_s, v_s, cos_t, sin_t):
    def body(q_ref, k_ref, kp_ref, v_ref, vp_ref, c_ref, s_ref, cp_ref, sp_ref, o_ref, lse_ref):
        g, i = pl.program_id(0), pl.program_id(1)
        qs, k2, v2 = _band_load(q_ref, k_ref, kp_ref, v_ref, vp_ref, c_ref, s_ref, cp_ref, sp_ref)
        lane = lax.broadcasted_iota(jnp.int32, (1, DIL_W), 1)
        for b in range(BAND_CH):
            valid = _band_valid(g, i * BAND_CH + b)
            hs = range(BAND_HEADS)
            mh = [(lane // HEAD_DIM) == h for h in hs]
            s = [jnp.where(valid, _dot_nt(jnp.where(mh[h], qs[b], jnp.zeros_like(qs[b])), k2[b]), NEG_INF) for h in hs]
            m = [jnp.max(s[h], axis=1, keepdims=True) for h in hs]
            p = [jnp.exp(s[h] - m[h]) for h in hs]
            den = [jnp.sum(p[h], axis=1, keepdims=True) for h in hs]
            pv = [_dot(p[h].astype(BF16), jnp.where(mh[h], v2[b], jnp.zeros_like(v2[b]))) for h in hs]
            o = jnp.zeros((BAND, DIL_W), F32)
            lse = jnp.zeros((BAND, DIL_W), F32)
            for h in hs:
                o = o + pv[h] * (1.0 / den[h])
                lse = jnp.where(mh[h], m[h] + jnp.log(den[h]), lse)
            o_ref[b] = o
            lse_ref[b] = lse

    cur, prev, tab, tab_prev = _band_specs()
    shape = jax.ShapeDtypeStruct(q_s.shape, F32)
    return pl.pallas_call(
        body,
        grid=(3, q_s.shape[1] // BAND_CH),
        in_specs=[cur, cur, prev, cur, prev, tab, tab, tab_prev, tab_prev],
        out_specs=[cur, cur],
        out_shape=[shape, shape],
        compiler_params=pltpu.CompilerParams(dimension_semantics=("parallel", "parallel")),
        name="band_fwd",
    )(q_s, k_s, k_s, v_s, v_s, cos_t, sin_t, cos_t, sin_t)


def _band_bwd(q_s, k_s, v_s, cos_t, sin_t, do_s, lse_s, delta_s):
    def body(q_ref, k_ref, kp_ref, v_ref, vp_ref, c_ref, s_ref, cp_ref, sp_ref, do_ref, lse_ref, dl_ref,
             dq_ref, dk_ref, dv_ref, dkf_ref, dvf_ref):
        g, i = pl.program_id(0), pl.program_id(1)
        qs, k2, v2 = _band_load(q_ref, k_ref, kp_ref, v_ref, vp_ref, c_ref, s_ref, cp_ref, sp_ref)
        lane = lax.broadcasted_iota(jnp.int32, (1, DIL_W), 1)
        dks, dvs = [], []
        for b in range(BAND_CH):
            valid = _band_valid(g, i * BAND_CH + b)
            do, lse, dl = do_ref[b], lse_ref[b], dl_ref[b]
            hs = range(BAND_HEADS)
            mh = [(lane // HEAD_DIM) == h for h in hs]
            qh = [jnp.where(mh[h], qs[b], jnp.zeros_like(qs[b])) for h in hs]
            doh = [jnp.where(mh[h], do, jnp.zeros_like(do)) for h in hs]
            s = [_dot_nt(qh[h], k2[b]) for h in hs]
            dp = [_dot_nt(doh[h], v2[b]) for h in hs]
            p = [jnp.where(valid, jnp.exp(s[h] - lse[:, h * HEAD_DIM:h * HEAD_DIM + 1]), 0.0) for h in hs]
            ds = [(p[h] * (dp[h] - dl[:, h * HEAD_DIM:h * HEAD_DIM + 1])).astype(BF16) for h in hs]
            pb = [p[h].astype(BF16) for h in hs]
            dq = sum(_dot(ds[h], jnp.where(mh[h], k2[b], jnp.zeros_like(k2[b]))) for h in hs)
            dk2 = sum(_dot_tn(ds[h], qh[h]) for h in hs)
            dv2 = sum(_dot_tn(pb[h], doh[h]) for h in hs)
            dq_ref[b] = dq * SB_SCALE
            dks.append(dk2)
            dvs.append(dv2)
        dkf_ref[...] = dks[0][:BAND]
        dvf_ref[...] = dvs[0][:BAND]
        for b in range(BAND_CH):
            last = b == BAND_CH - 1
            dk_ref[b] = dks[b][BAND:] if last else dks[b][BAND:] + dks[b + 1][:BAND]
            dv_ref[b] = dvs[b][BAND:] if last else dvs[b][BAND:] + dvs[b + 1][:BAND]

    cur, prev, tab, tab_prev = _band_specs()
    first = pl.BlockSpec((None, None, BAND, DIL_W), lambda g, i: (g, i, 0, 0))
    n_chunks = q_s.shape[1] // BAND_CH
    shape = jax.ShapeDtypeStruct(q_s.shape, F32)
    shape_first = jax.ShapeDtypeStruct((3, n_chunks, BAND, DIL_W), F32)
    return pl.pallas_call(
        body,
        grid=(3, n_chunks),
        in_specs=[cur, cur, prev, cur, prev, tab, tab, tab_prev, tab_prev, cur, cur, cur],
        out_specs=[cur, cur, cur, first, first],
        out_shape=[shape, shape, shape, shape_first, shape_first],
        compiler_params=pltpu.CompilerParams(dimension_semantics=("parallel", "parallel")),
        name="band_bwd",
    )(q_s, k_s, k_s, v_s, v_s, cos_t, sin_t, cos_t, sin_t, do_s, lse_s, delta_s)


def _band_combine(dq, dk, dv, dk_first, dv_first, cos_t, sin_t):
    n_chunks = dk_first.shape[1]

    def body(dq_ref, dk_ref, dkn_ref, dv_ref, dvn_ref, c_ref, s_ref, oq_ref, ok_ref, ov_ref):
        nxt = (pl.program_id(1) < n_chunks - 1).astype(F32)
        for b in range(BAND_CH):
            cos, sin = c_ref[b], s_ref[b]
            dq_b, dk_b, dv_b = dq_ref[b], dk_ref[b], dv_ref[b]
            if b == BAND_CH - 1:
                dk_b = dk_b + nxt * dkn_ref[...]
                dv_b = dv_b + nxt * dvn_ref[...]
            oq_ref[b] = (dq_b * cos - _swap_half(dq_b) * sin).astype(BF16)
            ok_ref[b] = (dk_b * cos - _swap_half(dk_b) * sin).astype(BF16)
            ov_ref[b] = dv_b.astype(BF16)

    cur, _, tab, _ = _band_specs()
    nxt = pl.BlockSpec((None, None, BAND, DIL_W), lambda g, i: (g, jnp.minimum(i + 1, n_chunks - 1), 0, 0))
    shape = jax.ShapeDtypeStruct(dq.shape, BF16)
    return pl.pallas_call(
        body,
        grid=(3, n_chunks),
        in_specs=[cur, cur, nxt, cur, nxt, tab, tab],
        out_specs=[cur] * 3,
        out_shape=[shape] * 3,
        compiler_params=pltpu.CompilerParams(dimension_semantics=("parallel", "parallel")),
        name="band_combine",
    )(dq, dk, dk_first, dv, dv_first, cos_t, sin_t)


def _band_merge(o_g, lse_g):
    t, tm = o_g.shape[1], 512

    def body(o_ref, l_ref, ob_ref, lse_ref):
        l0, l1, l2 = l_ref[0], l_ref[1], l_ref[2]
        m = jnp.maximum(jnp.maximum(l0, l1), l2)
        lse = m + jnp.log(jnp.exp(l0 - m) + jnp.exp(l1 - m) + jnp.exp(l2 - m))
        ob_ref[...] = (jnp.exp(l0 - lse) * o_ref[0] + jnp.exp(l1 - lse) * o_ref[1] + jnp.exp(l2 - lse) * o_ref[2]).astype(BF16)
        lse_ref[...] = lse

    spec3 = pl.BlockSpec((3, tm, DIL_W), lambda i: (0, i, 0))
    spec = pl.BlockSpec((tm, DIL_W), lambda i: (i, 0))
    return pl.pallas_call(
        body,
        grid=(t // tm,),
        in_specs=[spec3, spec3],
        out_specs=[spec, spec],
        out_shape=[jax.ShapeDtypeStruct((t, DIL_W), BF16), jax.ShapeDtypeStruct((t, DIL_W), F32)],
        compiler_params=pltpu.CompilerParams(dimension_semantics=("parallel",)),
        name="band_merge",
    )(o_g, lse_g)


def _band_delta(do_b, o_b):
    def fn(do, o):
        r = lax.broadcasted_iota(jnp.int32, (DIL_W, DIL_W), 0) // HEAD_DIM
        c = lax.broadcasted_iota(jnp.int32, (DIL_W, DIL_W), 1) // HEAD_DIM
        return (_split_dot(do.astype(F32) * o.astype(F32), (r == c).astype(BF16)),)

    return _rowwise(fn, [(do_b, "row"), (o_b, "row")], [(DIL_W, F32, "row")], tm=512, name="band_delta")[0]


MEM_T = 512
MEM_SCALE = 128 ** -0.5
MEM_Q_COL = (D_IN - MEM_W) // LANES


def _mem_specs():
    q = pl.BlockSpec((None, MEM_T, LANES), lambda b, h, i: (b, i, MEM_Q_COL + h))
    k = pl.BlockSpec((None, MEM_LEN, LANES), lambda b, h, i: (b, 0, h))
    v = pl.BlockSpec((None, MEM_LEN, LANES), lambda b, h, i: (b, 0, MEM_W // LANES + h))
    blk = pl.BlockSpec((None, MEM_T, LANES), lambda b, h, i: (b, i, h))
    return q, k, v, blk


def _mem_probs(q, k):
    s = _dot_nt(q, k) * MEM_SCALE
    p = jnp.exp(s - jnp.max(s, axis=1, keepdims=True))
    return p * (1.0 / jnp.sum(p, axis=1, keepdims=True))


def _mem_fwd(proj3, kv3):
    bl = proj3.shape[0]

    def body(q_ref, k_ref, v_ref, o_ref):
        p = _mem_probs(q_ref[...], k_ref[...])
        o_ref[...] = _dot(p.astype(BF16), v_ref[...]).astype(BF16)

    q, k, v, blk = _mem_specs()
    return pl.pallas_call(
        body,
        grid=(bl, MEM_W // LANES, SEQ // MEM_T),
        in_specs=[q, k, v],
        out_specs=blk,
        out_shape=jax.ShapeDtypeStruct((bl, SEQ, MEM_W), BF16),
        compiler_params=pltpu.CompilerParams(dimension_semantics=("parallel", "parallel", "parallel")),
        name="mem_fwd",
    )(proj3, kv3, kv3)


def _mem_bwd(proj3, kv3, do_c):
    bl = proj3.shape[0]

    def body(q_ref, k_ref, v_ref, do_ref, dq_ref, dk_ref, dv_ref):
        @pl.when(pl.program_id(2) == 0)
        def _():
            dk_ref[...] = jnp.zeros_like(dk_ref)
            dv_ref[...] = jnp.zeros_like(dv_ref)

        q, k, do = q_ref[...], k_ref[...], do_ref[...]
        p = _mem_probs(q, k)
        dp = _dot_nt(do, v_ref[...])
        ds = (p * (dp - jnp.sum(p * dp, axis=1, keepdims=True)) * MEM_SCALE).astype(BF16)
        dq_ref[...] = _dot(ds, k).astype(BF16)
        dk_ref[...] += _dot_tn(ds, q)
        dv_ref[...] += _dot_tn(p.astype(BF16), do)

    q, k, v, blk = _mem_specs()
    kv_out = pl.BlockSpec((None, MEM_LEN, LANES), lambda b, h, i: (b, 0, h))
    return pl.pallas_call(
        body,
        grid=(bl, MEM_W // LANES, SEQ // MEM_T),
        in_specs=[q, k, v, blk],
        out_specs=[blk, kv_out, kv_out],
        out_shape=[jax.ShapeDtypeStruct((bl, SEQ, MEM_W), BF16), jax.ShapeDtypeStruct((bl, MEM_LEN, MEM_W), F32),
                   jax.ShapeDtypeStruct((bl, MEM_LEN, MEM_W), F32)],
        compiler_params=pltpu.CompilerParams(dimension_semantics=("parallel", "parallel", "arbitrary")),
        name="mem_bwd",
    )(proj3, kv3, kv3, do_c)


def _place():
    x, y, c = lax.axis_index("x"), lax.axis_index("y"), lax.axis_index("c")
    return x, y, c


def _other_chips(x, y):
    return [(1 - x, y), (x, 1 - y), (1 - x, 1 - y)]


def _remote(src, dst, send_sem, recv_sem, to):
    return pltpu.make_async_remote_copy(src_ref=src, dst_ref=dst, send_sem=send_sem, recv_sem=recv_sem,
                                        device_id=to, device_id_type=MESH)


ANY = pl.BlockSpec(memory_space=pl.ANY)


def _gather_weights(shard):
    def body(in_ref, out_ref, send_sems, recv_sems, local_sem):
        x, y, c = _place()
        sibling = (x, y, 1 - c)
        chips = _other_chips(x, y)

        def half(px, py, pc):
            return out_ref.at[2 * px + py, pl.ds(pc * HALF_ROWS, HALF_ROWS), :]

        mine = pltpu.make_async_copy(in_ref, out_ref.at[2 * x + y], local_sem)
        mine.start()
        src = in_ref.at[pl.ds(c * HALF_ROWS, HALF_ROWS), :]
        first = [_remote(src, half(x, y, c), send_sems.at[j], recv_sems.at[j], (*chip, c)) for j, chip in enumerate(chips)]
        for cp in first:
            cp.start()
        passed = [_remote(half(*chip, c), half(*chip, c), send_sems.at[3 + j], recv_sems.at[3 + j], sibling)
                  for j, chip in enumerate(chips)]
        for j, chip in enumerate(chips):
            _remote(src, half(*chip, c), send_sems.at[j], recv_sems.at[j], (*chip, c)).wait_recv()
            passed[j].start()
        for j, chip in enumerate(chips):
            _remote(src, half(*chip, 1 - c), send_sems.at[3 + j], recv_sems.at[3 + j], sibling).wait_recv()
        for cp in first + passed:
            cp.wait_send()
        mine.wait()

    return pl.pallas_call(
        body,
        in_specs=[ANY],
        out_specs=ANY,
        out_shape=jax.ShapeDtypeStruct((N_CHIPS, PACK_ROWS, D_MODEL), BF16),
        scratch_shapes=[pltpu.SemaphoreType.DMA((6,)), pltpu.SemaphoreType.DMA((6,)), pltpu.SemaphoreType.DMA],
        name="gather_weights",
    )(shard)


def _pair_exchange(grads):
    def body(g_ref, land_ref, send_sem, recv_sem):
        x, y, c = _place()
        src = g_ref.at[:, pl.ds((1 - c) * HALF_ROWS, HALF_ROWS), :]
        cp = _remote(src, land_ref, send_sem, recv_sem, (x, y, 1 - c))
        cp.start()
        cp.wait()

    return pl.pallas_call(
        body,
        in_specs=[ANY],
        out_specs=ANY,
        out_shape=jax.ShapeDtypeStruct((N_CHIPS, HALF_ROWS, D_MODEL), F32),
        scratch_shapes=[pltpu.SemaphoreType.DMA, pltpu.SemaphoreType.DMA],
        name="pair_exchange",
    )(grads)


def _pair_add(grads, land, c_arr):
    tr = 480

    def body(c_ref, g_ref, l_ref, o_ref):
        o_ref[...] = g_ref[...] + l_ref[...]

    g4 = grads.reshape(N_CHIPS, 2, HALF_ROWS, D_MODEL)
    return pl.pallas_call(
        body,
        grid_spec=pltpu.PrefetchScalarGridSpec(
            num_scalar_prefetch=1,
            grid=(N_CHIPS, HALF_ROWS // tr),
            in_specs=[pl.BlockSpec((None, None, tr, D_MODEL), lambda s, r, c_ref: (s, c_ref[0], r, 0)),
                      pl.BlockSpec((None, tr, D_MODEL), lambda s, r, c_ref: (s, r, 0))],
            out_specs=pl.BlockSpec((None, tr, D_MODEL), lambda s, r, c_ref: (s, r, 0)),
        ),
        out_shape=jax.ShapeDtypeStruct((N_CHIPS, HALF_ROWS, D_MODEL), F32),
        compiler_params=pltpu.CompilerParams(dimension_semantics=("parallel", "parallel")),
        name="pair_add",
    )(c_arr, g4, land)


def _chip_exchange(part):
    def body(p_ref, land_ref, send_sems, recv_sems, local_sem):
        x, y, c = _place()
        me = 2 * x + y
        mine = pltpu.make_async_copy(p_ref.at[me], land_ref.at[me], local_sem)
        mine.start()
        chips = _other_chips(x, y)
        sends = [_remote(p_ref.at[2 * cx + cy], land_ref.at[me], send_sems.at[j], recv_sems.at[j], (cx, cy, c))
                 for j, (cx, cy) in enumerate(chips)]
        for cp in sends:
            cp.start()
        for j, (cx, cy) in enumerate(chips):
            _remote(p_ref.at[me], land_ref.at[2 * cx + cy], send_sems.at[j], recv_sems.at[j], (cx, cy, c)).wait_recv()
        for cp in sends:
            cp.wait_send()
        mine.wait()

    return pl.pallas_call(
        body,
        in_specs=[ANY],
        out_specs=ANY,
        out_shape=jax.ShapeDtypeStruct((N_CHIPS, HALF_ROWS, D_MODEL), F32),
        scratch_shapes=[pltpu.SemaphoreType.DMA((3,)), pltpu.SemaphoreType.DMA((3,)), pltpu.SemaphoreType.DMA],
        name="chip_exchange",
    )(part)


def _chip_add(land):
    tr = 480

    def body(l_ref, o_ref):
        o_ref[...] = ((l_ref[0] + l_ref[1]) + l_ref[2]) + l_ref[3]

    return pl.pallas_call(
        body,
        grid=(HALF_ROWS // tr,),
        in_specs=[pl.BlockSpec((N_CHIPS, tr, D_MODEL), lambda r: (0, r, 0))],
        out_specs=pl.BlockSpec((tr, D_MODEL), lambda r: (r, 0)),
        out_shape=jax.ShapeDtypeStruct((HALF_ROWS, D_MODEL), F32),
        compiler_params=pltpu.CompilerParams(dimension_semantics=("parallel",)),
        name="chip_add",
    )(land)


def _pair_share(half):
    def body(h_ref, out_ref, send_sem, recv_sem, local_sem):
        x, y, c = _place()
        mine = pltpu.make_async_copy(h_ref, out_ref.at[c], local_sem)
        mine.start()
        cp = _remote(h_ref, out_ref.at[c], send_sem, recv_sem, (x, y, 1 - c))
        cp.start()
        _remote(h_ref, out_ref.at[1 - c], send_sem, recv_sem, (x, y, 1 - c)).wait_recv()
        cp.wait_send()
        mine.wait()

    return pl.pallas_call(
        body,
        in_specs=[ANY],
        out_specs=ANY,
        out_shape=jax.ShapeDtypeStruct((2, HALF_ROWS, D_MODEL), F32),
        scratch_shapes=[pltpu.SemaphoreType.DMA, pltpu.SemaphoreType.DMA, pltpu.SemaphoreType.DMA],
        name="pair_share",
    )(half)


def _all_sum_small(part):
    def body(p_ref, o_ref, slots, send_sems, recv_sems):
        x, y, c = _place()
        me = 4 * x + 2 * y + c
        slots[me] = p_ref[...]
        peers = [(x ^ dx, y ^ dy, c ^ dc) for dx in (0, 1) for dy in (0, 1) for dc in (0, 1)][1:]
        sends = [_remote(p_ref, slots.at[me], send_sems.at[k], recv_sems.at[k], peer) for k, peer in enumerate(peers)]
        for cp in sends:
            cp.start()
        for k, (px, py, pc) in enumerate(peers):
            _remote(p_ref, slots.at[4 * px + 2 * py + pc], send_sems.at[k], recv_sems.at[k], (px, py, pc)).wait_recv()
        for cp in sends:
            cp.wait_send()
        acc = slots[0]
        for d in range(1, 8):
            acc = acc + slots[d]
        o_ref[...] = acc

    vmem = pl.BlockSpec(memory_space=pltpu.VMEM)
    return pl.pallas_call(
        body,
        in_specs=[vmem],
        out_specs=vmem,
        out_shape=jax.ShapeDtypeStruct(part.shape, F32),
        scratch_shapes=[pltpu.VMEM((8,) + part.shape, F32), pltpu.SemaphoreType.DMA((7,)), pltpu.SemaphoreType.DMA((7,))],
        name="all_sum_small",
    )(part)


def _pack_shard(parts):
    return jnp.concatenate([p.reshape(-1) for p in parts]).reshape(PACK_ROWS, D_MODEL)


def _unpack_shard(flat):
    lead = flat.shape[:-2]
    flat = flat.reshape(lead + (PACK_ROWS * D_MODEL,))
    out, pos = {}, 0
    for name, (a, b), _ in PACK:
        out[name] = flat[..., pos:pos + a * b].reshape(lead + (a, b))
        pos += a * b
    return out


def _full_weights(gathered):
    parts = _unpack_shard(gathered)
    return {name: jnp.concatenate([parts[name][s] for s in range(N_CHIPS)], axis=axis) for name, _, axis in PACK}


def _pack_grads(grads):
    slots = []
    for s in range(N_CHIPS):
        parts = []
        for name, (a, b), axis in PACK:
            g = grads[name]
            parts.append(g[:, s * b:(s + 1) * b] if axis == 1 else g[s * a:(s + 1) * a, :])
        slots.append(_pack_shard(parts))
    return jnp.stack(slots)


def _deinterleave(a, d):
    b, s, c = a.shape
    return a.reshape(b, s // d, d, c).transpose(0, 2, 1, 3).reshape(b * s // BAND, BAND, c)


def _reinterleave(a, d, b):
    c = a.shape[-1]
    return a.reshape(b, d, SEQ // d, c).transpose(0, 2, 1, 3).reshape(b, SEQ, c)


def _rope_tables():
    half = HEAD_DIM // 2
    inv_freq = ROPE_THETA ** (-jnp.arange(half, dtype=F32) * 2.0 / HEAD_DIM)
    ang = jnp.arange(SEQ, dtype=F32)[:, None] * inv_freq[None, :]
    cos = jnp.tile(jnp.cos(ang), (1, 2 * BAND_HEADS))
    sin = jnp.tile(jnp.concatenate([-jnp.sin(ang), jnp.sin(ang)], axis=1), (1, BAND_HEADS))
    cos_t = jnp.stack([_deinterleave(cos[None], d) for d in DIL_D])
    sin_t = jnp.stack([_deinterleave(sin[None], d) for d in DIL_D])
    return cos_t, sin_t


def _local_step(x, mem, loss_target, g_pre_mix, g_post_mix, g_pre_ffn, g_post_ffn, g_mem, b_gate, w):
    bl = x.shape[0]
    t = bl * SEQ
    w_ffn_g, w_ffn_u = w["w_ffn_in"][:, :D_FF], w["w_ffn_in"][:, D_FF:]

    x2 = x.reshape(t, D_MODEL)
    tgt2 = loss_target.reshape(t, D_MODEL)
    mem2 = mem.reshape(bl * MEM_LEN, D_MODEL)

    h = _norm_fwd(x2, g_pre_mix, name="norm_x")
    proj = _mm([(h, w["w_in"])], nt=False, tm=512, tn=2176, out_dtypes=[BF16], name="proj")
    gates = _mm([(h, w["w_gate"])], nt=False, tm=512, tn=1536, out_dtypes=[BF16], name="gates", bias=b_gate,
                epilogue=lambda acc: (_sigmoid(acc),))
    hm = _norm_fwd(mem2, g_mem, name="norm_mem")
    kv_m = _mm([(hm, w["w_mem_kv"])], nt=False, tm=512, tn=1024, out_dtypes=[BF16], name="mem_kv")
    proj3 = proj.reshape(bl, SEQ, D_IN)
    kv3 = kv_m.reshape(bl, MEM_LEN, 2 * MEM_W)

    o_a, o_a32 = _sb_fwd(proj3)

    cos_t, sin_t = _rope_tables()
    dil0 = 3 * SB_W

    def group_cols(g, part):
        c0 = dil0 + (3 * g + part) * DIL_W
        return proj3[:, :, c0:c0 + DIL_W]

    q_s = jnp.stack([_deinterleave(group_cols(g, 0), d) for g, d in enumerate(DIL_D)])
    k_s = jnp.stack([_deinterleave(group_cols(g, 1), d) for g, d in enumerate(DIL_D)])
    v_s = jnp.stack([_deinterleave(group_cols(g, 2), d) for g, d in enumerate(DIL_D)])
    o_gs, lse_gs = _band_fwd(q_s, k_s, v_s, cos_t, sin_t)
    o_g = jnp.stack([_reinterleave(o_gs[g], d, bl) for g, d in enumerate(DIL_D)]).reshape(3, t, DIL_W)
    lse_g = jnp.stack([_reinterleave(lse_gs[g], d, bl) for g, d in enumerate(DIL_D)]).reshape(3, t, DIL_W)
    o_b, lse_b = _band_merge(o_g, lse_g)

    o_c = _mem_fwd(proj3, kv3)

    o_a2, o_c2 = o_a.reshape(t, SB_W), o_c.reshape(t, MEM_W)
    y_a, y_b, y_c, merged = _branch_merge_fwd(o_a2, o_b, o_c2, w["w_br_sb"], w["w_br_dil"], w["w_br_mem"], gates)
    mix = _mm([(merged, w["w_o"])], nt=False, tm=512, tn=1024, out_dtypes=[F32], name="mix")
    x1, h2 = _mid_fwd(mix, x2, g_post_mix, g_pre_ffn)
    gg, uu, f = _ffn_in_fwd(h2, w_ffn_g, w_ffn_u)
    f2 = _mm([(f, w["w_ffn_out"])], nt=False, tm=512, tn=1024, out_dtypes=[F32], name="ffn_out")

    dy, df2, dg_post_ffn, loss_row = _loss_bwd(f2, x1, g_post_ffn, tgt2)

    dg_ffn, du_ffn = _mm([(df2, w["w_ffn_out"])], nt=True, tm=512, tn=1408, out_dtypes=[BF16, BF16], name="d_ffn_act",
                         extras=(gg, uu), epilogue=_swiglu_bwd_epilogue)
    gw = {}
    gw["w_ffn_out"] = _mm_tn(f, df2, tm=1408, tn=1024, tk=512, name="gw_ffn_out")
    gw_ffn_g = _mm_tn(h2, dg_ffn, tm=1024, tn=1408, tk=512, name="gw_ffn_gate")
    gw_ffn_u = _mm_tn(h2, du_ffn, tm=1024, tn=1408, tk=512, name="gw_ffn_up")
    gw["w_ffn_in"] = jnp.concatenate([gw_ffn_g, gw_ffn_u], axis=1)
    dh2 = _mm([(dg_ffn, w_ffn_g), (du_ffn, w_ffn_u)], nt=True, tm=256, tn=1024, out_dtypes=[F32], name="d_h2")
    dx1, dmix, dg_pre_ffn, dg_post_mix = _mid_bwd(dh2, x1, mix, g_pre_ffn, g_post_mix, dy)

    gw["w_o"] = _mm_tn(merged, dmix, tm=1024, tn=1024, tk=512, name="gw_o")
    dmerged = _mm([(dmix, w["w_o"])], nt=True, tm=512, tn=1024, out_dtypes=[F32], name="d_merged")
    dy_a, dy_b, dy_c, dgpre, db_gate = _gate_bwd(dmerged, gates, y_a, y_b, y_c)
    gw["w_br_sb"] = _mm_tn(o_a2, dy_a, tm=512, tn=1024, tk=512, name="gw_br_sb")
    gw["w_br_dil"] = _mm_tn(o_b, dy_b, tm=256, tn=1024, tk=512, name="gw_br_dil")
    gw["w_br_mem"] = _mm_tn(o_c2, dy_c, tm=512, tn=1024, tk=512, name="gw_br_mem")
    gw["w_gate"] = _mm_tn(h, dgpre, tm=1024, tn=1536, tk=512, name="gw_gate")
    do_a = _mm([(dy_a, w["w_br_sb"])], nt=True, tm=512, tn=SB_W, out_dtypes=[BF16], name="d_o_a")
    do_b = _mm([(dy_b, w["w_br_dil"])], nt=True, tm=512, tn=DIL_W, out_dtypes=[BF16], name="d_o_b")
    do_c = _mm([(dy_c, w["w_br_mem"])], nt=True, tm=512, tn=MEM_W, out_dtypes=[BF16], name="d_o_c")

    dq_c, dk_m, dv_m = _mem_bwd(proj3, kv3, do_c.reshape(bl, SEQ, MEM_W))
    dkv_m = jnp.concatenate([dk_m, dv_m], axis=-1).reshape(bl * MEM_LEN, 2 * MEM_W).astype(BF16)
    gw["w_mem_kv"] = _mm_tn(hm, dkv_m, tm=1024, tn=1024, tk=512, name="gw_mem_kv")
    dhm = _mm([(dkv_m, w["w_mem_kv"])], nt=True, tm=512, tn=1024, out_dtypes=[F32], name="d_hm")
    dg_mem = _mem_norm_bwd(dhm, mem2, g_mem)

    delta_b = _band_delta(do_b, o_b)
    do_b3, lse_b3, delta_b3 = (a.reshape(bl, SEQ, DIL_W) for a in (do_b, lse_b, delta_b))
    do_s = jnp.stack([_deinterleave(do_b3, d) for d in DIL_D])
    lse_s = jnp.stack([_deinterleave(lse_b3, d) for d in DIL_D])
    delta_s = jnp.stack([_deinterleave(delta_b3, d) for d in DIL_D])
    dq_r, dk_r, dv_r, dk_first, dv_first = _band_bwd(q_s, k_s, v_s, cos_t, sin_t, do_s, lse_s, delta_s)
    dq_s, dk_s, dv_s = _band_combine(dq_r, dk_r, dv_r, dk_first, dv_first, cos_t, sin_t)
    d_dil = []
    for g, d in enumerate(DIL_D):
        d_dil += [_reinterleave(a[g], d, bl) for a in (dq_s, dk_s, dv_s)]

    dq_a, dk_a, dv_a = _sb_bwd(proj3, o_a32, do_a.reshape(bl, SEQ, SB_W))

    dproj = jnp.concatenate([dq_a, dk_a, dv_a] + d_dil + [dq_c], axis=-1).reshape(t, D_IN)
    gw["w_in"] = _mm_tn(h, dproj, tm=1024, tn=2176, tk=512, name="gw_in")
    dh = _mm([(dproj, w["w_in"]), (dgpre, w["w_gate"])], nt=True, tm=256, tn=1024, out_dtypes=[F32], name="d_h")
    grad_x, dg_pre_mix = _first_bwd(dh, x2, g_pre_mix, dx1)
    small = jnp.concatenate([dg_pre_mix, dg_post_mix, dg_pre_ffn, dg_post_ffn, dg_mem, db_gate.reshape(3, D_MODEL)], axis=0)
    return loss_row[0, 0], grad_x.reshape(bl, SEQ, D_MODEL), gw, small


def kernel(x, mem, g_pre_mix, g_post_mix, g_pre_ffn, g_post_ffn, g_mem, w_in, w_mem_kv, w_br_sb, w_br_dil, w_br_mem, w_gate, b_gate, w_o, w_ffn_in, w_ffn_out, loss_target, m_g_pre_mix, m_g_post_mix, m_g_pre_ffn, m_g_post_ffn, m_g_mem, m_w_in, m_w_mem_kv, m_w_br_sb, m_w_br_dil, m_w_br_mem, m_w_gate, m_b_gate, m_w_o, m_w_ffn_in, m_w_ffn_out, v_g_pre_mix, v_g_post_mix, v_g_pre_ffn, v_g_post_ffn, v_g_mem, v_w_in, v_w_mem_kv, v_w_br_sb, v_w_br_dil, v_w_br_mem, v_w_gate, v_b_gate, v_w_o, v_w_ffn_in, v_w_ffn_out):
    w_shards = dict(w_in=w_in[0], w_mem_kv=w_mem_kv[0], w_br_sb=w_br_sb[0], w_br_dil=w_br_dil[0], w_br_mem=w_br_mem[0],
                    w_gate=w_gate[0], w_o=w_o[0], w_ffn_in=w_ffn_in[0], w_ffn_out=w_ffn_out[0])
    m_shards = dict(w_in=m_w_in[0], w_mem_kv=m_w_mem_kv[0], w_br_sb=m_w_br_sb[0], w_br_dil=m_w_br_dil[0], w_br_mem=m_w_br_mem[0],
                    w_gate=m_w_gate[0], w_o=m_w_o[0], w_ffn_in=m_w_ffn_in[0], w_ffn_out=m_w_ffn_out[0])
    v_shards = dict(w_in=v_w_in[0], w_mem_kv=v_w_mem_kv[0], w_br_sb=v_w_br_sb[0], w_br_dil=v_w_br_dil[0], w_br_mem=v_w_br_mem[0],
                    w_gate=v_w_gate[0], w_o=v_w_o[0], w_ffn_in=v_w_ffn_in[0], w_ffn_out=v_w_ffn_out[0])

    shard_bf = _pack_shard([w_shards[name].astype(BF16) for name, _, _ in PACK])
    w = _full_weights(_gather_weights(shard_bf))

    loss_local, grad_x, gw, small = _local_step(x, mem, loss_target, g_pre_mix, g_post_mix, g_pre_ffn, g_post_ffn, g_mem, b_gate, w)
    loss = lax.psum(loss_local, ("x", "y", "c"))

    c_arr = lax.axis_index("c").astype(jnp.int32).reshape(1)
    packed = _pack_grads(gw)
    part = _pair_add(packed, _pair_exchange(packed), c_arr)
    red = _pair_share(_chip_add(_chip_exchange(part))).reshape(PACK_ROWS, D_MODEL)
    g_shards = _unpack_shard(red)
    small = _all_sum_small(small)

    upd = {}
    for name, (a, b), _ in PACK:
        tm = a // 2 if a % 16 == 0 else a
        upd[name] = _adamw(w_shards[name], g_shards[name], m_shards[name], v_shards[name], tm=tm, name="adamw_" + name)

    def small8(gs, b):
        return jnp.concatenate(gs + [b.reshape(3, D_MODEL)], axis=0)

    sw = small8([g_pre_mix, g_post_mix, g_pre_ffn, g_post_ffn, g_mem], b_gate)
    sm = small8([m_g_pre_mix, m_g_post_mix, m_g_pre_ffn, m_g_post_ffn, m_g_mem], m_b_gate)
    sv = small8([v_g_pre_mix, v_g_post_mix, v_g_pre_ffn, v_g_post_ffn, v_g_mem], v_b_gate)
    s_upd = _adamw(sw, small, sm, sv, tm=8, name="adamw_small")

    def small_out(a):
        return [a[0:1], a[1:2], a[2:3], a[3:4], a[4:5]]

    order = ["w_in", "w_mem_kv", "w_br_sb", "w_br_dil", "w_br_mem", "w_gate", "b_gate", "w_o", "w_ffn_in", "w_ffn_out"]

    def leaves(small_arr, big):
        out = small_out(small_arr)
        for name in order:
            out.append(small_arr[5:8].reshape(1, 3 * D_MODEL) if name == "b_gate" else big[name][None])
        return out

    grads_out = leaves(small, g_shards)
    delta_out = leaves(s_upd[0], {n: u[0] for n, u in upd.items()})
    m_out = leaves(s_upd[1], {n: u[1] for n, u in upd.items()})
    v_out = leaves(s_upd[2], {n: u[2] for n, u in upd.items()})
    return (loss, grad_x, *grads_out, *delta_out, *m_out, *v_out)
```

```python
import functools

import jax
import jax.numpy as jnp
from jax import lax
from jax.experimental import pallas as pl
from jax.experimental.pallas import tpu as pltpu

F32 = jnp.float32
BF16 = jnp.bfloat16
MESH = pl.DeviceIdType.MESH

D_MODEL = 1024
SEQ = 2048
HEAD_DIM = 64
SB_W = 512
DIL_W = 256
MEM_W = 512
MEM_LEN = 256
D_IN = 3 * SB_W + 9 * DIL_W + MEM_W
D_FF = 2816
DIL_D = (1, 4, 16)
ROPE_THETA = 10000.0
NORM_EPS = 1e-6
NEG_INF = -1e30
LANES = 128

ADAM_LR = 0.001
ADAM_B1 = 0.9
ADAM_B2 = 0.999
ADAM_EPS = 1e-08
ADAM_WD = 0.01
ADAM_STEP = 10

N_CHIPS = 4
PACK = (
    ("w_in", (1024, 1088), 1),
    ("w_mem_kv", (256, 1024), 0),
    ("w_br_sb", (512, 256), 1),
    ("w_br_dil", (256, 256), 1),
    ("w_br_mem", (512, 256), 1),
    ("w_gate", (1024, 768), 1),
    ("w_o", (256, 1024), 0),
    ("w_ffn_in", (1024, 1408), 1),
    ("w_ffn_out", (704, 1024), 0),
)
PACK_ROWS = sum(a * b for _, (a, b), _ in PACK) // D_MODEL
HALF_ROWS = PACK_ROWS // 2


def _dot(a, b):
    return lax.dot_general(a, b, (((1,), (0,)), ((), ())), preferred_element_type=F32)


def _dot_nt(a, b):
    return lax.dot_general(a, b, (((1,), (1,)), ((), ())), preferred_element_type=F32)


def _dot_tn(a, b):
    return lax.dot_general(a, b, (((0,), (0,)), ((), ())), preferred_element_type=F32)


def _split_dot(x, u):
    hi = x.astype(BF16)
    lo = (x - hi.astype(F32)).astype(BF16)
    return _dot(hi, u) + _dot(lo, u)


def _mm(pairs, *, nt, tm, tn, out_dtypes, name, bias=None, extras=(), epilogue=None):
    pairs = [p if len(p) == 4 else (p[0], p[1], None, None) for p in pairs]
    m = pairs[0][0].shape[-2]
    b0 = pairs[0][1]
    if nt:
        n = b0.shape[-2]
    else:
        n = b0.shape[-1] * (b0.shape[0] if b0.ndim == 3 else 1)
    n_pairs, n_extra, n_out = len(pairs), len(extras), len(out_dtypes)
    tm = min(tm, m)
    assert m % tm == 0 and n % tn == 0
    has_bias = bias is not None

    def body(*refs):
        acc = None
        for i in range(n_pairs):
            a, b = refs[2 * i][...], refs[2 * i + 1][...]
            p = _dot_nt(a, b) if nt else _dot(a, b)
            acc = p if acc is None else acc + p
        pos = 2 * n_pairs
        if has_bias:
            acc = acc + refs[pos][...]
            pos += 1
        ex = [r[...] for r in refs[pos:pos + n_extra]]
        outs = refs[pos + n_extra:]
        vals = (acc,) if epilogue is None else epilogue(acc, *ex)
        for r, v, dt in zip(outs, vals, out_dtypes):
            r[...] = v.astype(dt)

    in_specs, args = [], []
    for a, b, a_col, b_sel in pairs:
        k = b.shape[-1] if nt else b.shape[-2]
        assert a_col is not None or a.shape[1] == k
        if a.ndim == 3:
            in_specs.append(pl.BlockSpec((None, tm, k), lambda j, i, c=a_col: (c, i, 0)))
        else:
            in_specs.append(pl.BlockSpec((tm, k), lambda j, i, c=a_col or 0: (i, c)))
        if b.ndim == 2:
            in_specs.append(pl.BlockSpec((tn, k), lambda j, i: (j, 0)) if nt else pl.BlockSpec((k, tn), lambda j, i: (0, j)))
        elif nt:
            in_specs.append(pl.BlockSpec((None, tn, k), lambda j, i, s=b_sel: (s, j, 0)))
        else:
            assert b_sel == "j" and b.shape[-1] == tn
            in_specs.append(pl.BlockSpec((None, k, tn), lambda j, i: (j, 0, 0)))
        args += [a, b]
    if has_bias:
        in_specs.append(pl.BlockSpec((1, tn), lambda j, i: (0, j)))
        args.append(bias)
    for e in extras:
        in_specs.append(pl.BlockSpec((tm, tn), lambda j, i: (i, j)))
        args.append(e)
    out = pl.pallas_call(
        body,
        grid=(n // tn, m // tm),
        in_specs=in_specs,
        out_specs=[pl.BlockSpec((tm, tn), lambda j, i: (i, j)) for _ in range(n_out)],
        out_shape=[jax.ShapeDtypeStruct((m, n), dt) for dt in out_dtypes],
        compiler_params=pltpu.CompilerParams(dimension_semantics=("parallel", "parallel")),
        name=name,
    )(*args)
    return out[0] if n_out == 1 else out


def _mm_tn(a, b, *, tm, tn, tk, name, out_shards=False):
    k, m = a.shape
    b_shards = b.ndim == 3
    out_shards = out_shards or b_shards
    n = b.shape[0] * b.shape[2] if b_shards else b.shape[1]
    tk = min(tk, k)
    assert m % tm == 0 and n % tn == 0 and k % tk == 0 and (not b_shards or b.shape[2] == tn)

    def body(a_ref, b_ref, o_ref):
        @pl.when(pl.program_id(2) == 0)
        def _():
            o_ref[...] = jnp.zeros_like(o_ref)

        o_ref[...] += _dot_tn(a_ref[...], b_ref[...])

    if b_shards:
        b_spec = pl.BlockSpec((None, tk, tn), lambda i, j, kk: (j, kk, 0))
    else:
        b_spec = pl.BlockSpec((tk, tn), lambda i, j, kk: (kk, j))
    if out_shards:
        out_spec = pl.BlockSpec((None, tm, tn), lambda i, j, kk: (j, i, 0))
        out_shape = jax.ShapeDtypeStruct((n // tn, m, tn), F32)
    else:
        out_spec = pl.BlockSpec((tm, tn), lambda i, j, kk: (i, j))
        out_shape = jax.ShapeDtypeStruct((m, n), F32)
    return pl.pallas_call(
        body,
        grid=(m // tm, n // tn, k // tk),
        in_specs=[pl.BlockSpec((tk, tm), lambda i, j, kk: (kk, i)), b_spec],
        out_specs=out_spec,
        out_shape=out_shape,
        compiler_params=pltpu.CompilerParams(dimension_semantics=("parallel", "parallel", "arbitrary")),
        name=name,
    )(a, b)


def _rowwise(fn, ins, outs, *, tm, name):
    rows = next(a.shape[0] for a, kind in ins if kind == "row")
    tm = min(tm, rows)
    assert rows % tm == 0
    n_in = len(ins)

    def body(*refs):
        vals = fn(*[r[...] for r in refs[:n_in]])
        for (_, dt, kind), r, v in zip(outs, refs[n_in:], vals):
            if kind == "row":
                r[...] = v.astype(dt)
            else:
                @pl.when(pl.program_id(0) == 0)
                def _(r=r):
                    r[...] = jnp.zeros_like(r)

                r[...] += v

    in_specs = [pl.BlockSpec((tm, a.shape[1]), lambda i: (i, 0)) if kind == "row" else pl.BlockSpec(a.shape, lambda i: (0, 0))
                for a, kind in ins]
    out_specs = [pl.BlockSpec((tm, c), lambda i: (i, 0)) if kind == "row" else pl.BlockSpec((1, c), lambda i: (0, 0))
                 for c, _, kind in outs]
    out_shape = [jax.ShapeDtypeStruct((rows if kind == "row" else 1, c), dt) for c, dt, kind in outs]
    has_acc = any(kind == "acc" for _, _, kind in outs)
    return pl.pallas_call(
        body,
        grid=(rows // tm,),
        in_specs=in_specs,
        out_specs=out_specs,
        out_shape=out_shape,
        compiler_params=pltpu.CompilerParams(dimension_semantics=("arbitrary" if has_acc else "parallel",)),
        name=name,
    )(*[a for a, _ in ins])


def _rstd(x):
    return lax.rsqrt(jnp.mean(x * x, axis=-1, keepdims=True) + NORM_EPS)


def _norm_bwd(dout, xin, g):
    r = _rstd(xin)
    n = xin * r
    dn = dout * g
    dg = jnp.sum(dout * n, axis=0, keepdims=True)
    dx = r * (dn - n * jnp.mean(dn * n, axis=-1, keepdims=True))
    return dx, dg


def _sigmoid(x):
    return 1.0 / (1.0 + jnp.exp(-x))


def _norm_fwd(x, g, *, name):
    def fn(x, g):
        return ((x * _rstd(x)) * g,)

    return _rowwise(fn, [(x, "row"), (g, "vec")], [(D_MODEL, BF16, "row")], tm=512, name=name)[0]


def _mid_fwd(mix, x, g_post_mix, g_pre_ffn):
    def fn(mix, x, g2, g3):
        x1 = x + (mix * _rstd(mix)) * g2
        return x1, (x1 * _rstd(x1)) * g3

    return _rowwise(fn, [(mix, "row"), (x, "row"), (g_post_mix, "vec"), (g_pre_ffn, "vec")],
                    [(D_MODEL, F32, "row"), (D_MODEL, BF16, "row")], tm=512, name="mid_fwd")


def _loss_bwd(f2, x1, g_post_ffn, tgt):
    def fn(f2, x1, g4, tgt):
        r = _rstd(f2)
        n = f2 * r
        err = x1 + n * g4 - tgt
        loss = 0.5 * jnp.sum(jnp.mean(err * err, axis=-1, keepdims=True), axis=0, keepdims=True)
        dy = err * (1.0 / D_MODEL)
        dn = dy * g4
        dg4 = jnp.sum(dy * n, axis=0, keepdims=True)
        df2 = r * (dn - n * jnp.mean(dn * n, axis=-1, keepdims=True))
        return dy, df2, dg4, jnp.broadcast_to(loss, (1, LANES))

    return _rowwise(fn, [(f2, "row"), (x1, "row"), (g_post_ffn, "vec"), (tgt, "row")],
                    [(D_MODEL, F32, "row"), (D_MODEL, BF16, "row"), (D_MODEL, F32, "acc"), (LANES, F32, "acc")],
                    tm=512, name="loss_bwd")


def _mid_bwd(dh2, x1, mix, g_pre_ffn, g_post_mix, dy):
    def fn(dh2, x1, mix, g3, g2, dy):
        d3, dg3 = _norm_bwd(dh2, x1, g3)
        dx1 = dy + d3
        dmix, dg2 = _norm_bwd(dx1, mix, g2)
        return dx1, dmix, dg3, dg2

    return _rowwise(fn, [(dh2, "row"), (x1, "row"), (mix, "row"), (g_pre_ffn, "vec"), (g_post_mix, "vec"), (dy, "row")],
                    [(D_MODEL, F32, "row"), (D_MODEL, BF16, "row"), (D_MODEL, F32, "acc"), (D_MODEL, F32, "acc")],
                    tm=256, name="mid_bwd")


def _first_bwd(dh, x, g_pre_mix, dx1):
    def fn(dh, x, g1, dx1):
        d1, dg1 = _norm_bwd(dh, x, g1)
        return dx1 + d1, dg1

    return _rowwise(fn, [(dh, "row"), (x, "row"), (g_pre_mix, "vec"), (dx1, "row")],
                    [(D_MODEL, F32, "row"), (D_MODEL, F32, "acc")], tm=512, name="first_bwd")


def _mem_norm_bwd(dhm, mem, g_mem):
    def fn(dhm, mem, g):
        return (jnp.sum(dhm * (mem * _rstd(mem)), axis=0, keepdims=True),)

    return _rowwise(fn, [(dhm, "row"), (mem, "row"), (g_mem, "vec")], [(D_MODEL, F32, "acc")], tm=512, name="mem_norm_bwd")[0]


def _gate_bwd(dmerged, gates, ya, yb, yc):
    def fn(dm, gt, ya, yb, yc):
        gt = gt.astype(F32)
        outs, dgp = [], []
        for i, y in enumerate((ya, yb, yc)):
            gi = gt[:, i * D_MODEL:(i + 1) * D_MODEL]
            outs.append(dm * gi)
            dgp.append(dm * y.astype(F32) * gi * (1.0 - gi))
        dgpre = jnp.concatenate(dgp, axis=1)
        return outs[0], outs[1], outs[2], dgpre, jnp.sum(dgpre, axis=0, keepdims=True)

    return _rowwise(fn, [(dmerged, "row"), (gates, "row"), (ya, "row"), (yb, "row"), (yc, "row")],
                    [(D_MODEL, BF16, "row")] * 3 + [(3 * D_MODEL, BF16, "row"), (3 * D_MODEL, F32, "acc")],
                    tm=256, name="gate_bwd")


def _adamw(w, g, m, v, *, tm, name):
    def fn(w, g, m, v):
        m = ADAM_B1 * m + (1.0 - ADAM_B1) * g
        v = ADAM_B2 * v + (1.0 - ADAM_B2) * (g * g)
        m_hat = m / (1.0 - ADAM_B1 ** ADAM_STEP)
        v_hat = v / (1.0 - ADAM_B2 ** ADAM_STEP)
        delta = -ADAM_LR * (m_hat / (jnp.sqrt(v_hat) + ADAM_EPS) + ADAM_WD * w)
        return delta, m, v

    c = w.shape[1]
    return _rowwise(fn, [(w, "row"), (g, "row"), (m, "row"), (v, "row")], [(c, F32, "row")] * 3, tm=tm, name=name)


def _ffn_in_fwd(h2, w_ffn):
    m, tm, tn = h2.shape[0], 512, w_ffn.shape[2]
    assert 2 * tn == D_FF

    def body(h_ref, wg_ref, wu_ref, g_ref, u_ref, f_ref):
        h = h_ref[...]
        g = _dot(h, wg_ref[...])
        u = _dot(h, wu_ref[...])
        g_ref[...] = g.astype(BF16)
        u_ref[...] = u.astype(BF16)
        f_ref[...] = (g * _sigmoid(g) * u).astype(BF16)

    o_spec = pl.BlockSpec((tm, tn), lambda j, i: (i, j))
    return pl.pallas_call(
        body,
        grid=(D_FF // tn, m // tm),
        in_specs=[pl.BlockSpec((tm, D_MODEL), lambda j, i: (i, 0)),
                  pl.BlockSpec((None, D_MODEL, tn), lambda j, i: (j, 0, 0)),
                  pl.BlockSpec((None, D_MODEL, tn), lambda j, i: (j + 2, 0, 0))],
        out_specs=[o_spec, o_spec, o_spec],
        out_shape=[jax.ShapeDtypeStruct((m, D_FF), BF16)] * 3,
        compiler_params=pltpu.CompilerParams(dimension_semantics=("parallel", "parallel")),
        name="ffn_in_fwd",
    )(h2, w_ffn, w_ffn)


def _swiglu_bwd_epilogue(df, g, u):
    g = g.astype(F32)
    u = u.astype(F32)
    sg = _sigmoid(g)
    return df * u * (sg * (1.0 + g * (1.0 - sg))), df * (g * sg)


def _branch_merge_fwd(o_a, o_b, o_c, w_sb, w_dil, w_mem, gates):
    m, tm = o_a.shape[0], 256

    def body(oa_ref, ob_ref, oc_ref, wa_ref, wb_ref, wc_ref, gt_ref, ya_ref, yb_ref, yc_ref, mg_ref):
        def project(o_ref, w_ref):
            o = o_ref[...]
            return jnp.concatenate([_dot(o, w_ref[s]) for s in range(N_CHIPS)], axis=1)

        ya = project(oa_ref, wa_ref)
        yb = project(ob_ref, wb_ref)
        yc = project(oc_ref, wc_ref)
        gt = gt_ref[...].astype(F32)
        ya_ref[...] = ya.astype(BF16)
        yb_ref[...] = yb.astype(BF16)
        yc_ref[...] = yc.astype(BF16)
        mg_ref[...] = (gt[:, :D_MODEL] * ya + gt[:, D_MODEL:2 * D_MODEL] * yb + gt[:, 2 * D_MODEL:] * yc).astype(BF16)

    row = lambda c: pl.BlockSpec((tm, c), lambda i: (i, 0))
    full = lambda a: pl.BlockSpec(a.shape, lambda i: (0, 0, 0))
    return pl.pallas_call(
        body,
        grid=(m // tm,),
        in_specs=[row(SB_W), row(DIL_W), row(MEM_W), full(w_sb), full(w_dil), full(w_mem), row(3 * D_MODEL)],
        out_specs=[row(D_MODEL)] * 4,
        out_shape=[jax.ShapeDtypeStruct((m, D_MODEL), BF16)] * 4,
        compiler_params=pltpu.CompilerParams(dimension_semantics=("parallel",)),
        name="branch_merge_fwd",
    )(o_a, o_b, o_c, w_sb, w_dil, w_mem, gates)


SB_T = 256
SB_SCALE = HEAD_DIM ** -0.5


def _sb_masks():
    row = lax.broadcasted_iota(jnp.int32, (SB_T, SB_T), 0)
    col = lax.broadcasted_iota(jnp.int32, (SB_T, SB_T), 1)
    lane = lax.broadcasted_iota(jnp.int32, (1, LANES), 1)
    return row, col, lane


def _sb_logs(z):
    e = jnp.exp(-jnp.abs(z))
    sp = jnp.log1p(e)
    return jnp.minimum(z, 0.0) - sp, jnp.minimum(-z, 0.0) - sp, e


def _sb_specs(n_heads_pairs, col0):
    q = pl.BlockSpec((None, SB_T, LANES), lambda b, p, i: (b, i, col0 + p))
    k = pl.BlockSpec((None, SEQ, LANES), lambda b, p, i: (b, 0, col0 + n_heads_pairs + p))
    v = pl.BlockSpec((None, SEQ, LANES), lambda b, p, i: (b, 0, col0 + 2 * n_heads_pairs + p))
    return q, k, v


def _sb_fwd(proj3):
    bl = proj3.shape[0]
    n_pairs = SB_W // LANES

    def body(q_ref, k_ref, v_ref, o_ref, o32_ref):
        i = pl.program_id(2)
        row, col, lane = _sb_masks()
        causal = col < row
        u_excl = (row > col).astype(BF16)
        q = q_ref[...]
        heads = []
        for h in range(2):
            mh = (lane // HEAD_DIM) == h
            heads.append((mh, jnp.where(mh, q, jnp.zeros_like(q)) * SB_SCALE))

        def blocks(js, carries, acc, diag):
            ks = [k_ref[pl.ds(pl.multiple_of(j * SB_T, SB_T), SB_T), :] for j in js]
            vs = [v_ref[pl.ds(pl.multiple_of(j * SB_T, SB_T), SB_T), :] for j in js]
            chains = [(b, h) for b in range(len(js)) for h in range(2)]
            z = {c: _dot_nt(heads[c[1]][1], ks[c[0]]) for c in chains}
            lb, lk = {}, {}
            for c in chains:
                lb[c], lk[c], _ = _sb_logs(z[c])
                if diag:
                    lk[c] = jnp.where(causal, lk[c], 0.0)
            r = {c: _split_dot(lk[c], u_excl) for c in chains}
            carries = list(carries)
            w = {}
            for b, h in chains:
                w_c = jnp.exp(lb[b, h] + r[b, h] + carries[h])
                w[b, h] = (jnp.where(causal, w_c, 0.0) if diag else w_c).astype(BF16)
                carries[h] = carries[h] + (r[b, h][:, :1] + lk[b, h][:, :1])
            for b, h in chains:
                acc = acc + _dot(w[b, h], jnp.where(heads[h][0], vs[b], jnp.zeros_like(vs[b])))
            return tuple(carries), acc

        zero = jnp.zeros((SB_T, 1), F32)
        carries, acc = blocks([i], (zero, zero), jnp.zeros((SB_T, LANES), F32), True)
        carries, acc = lax.fori_loop(0, i // 2, lambda jj, c: blocks([i - 1 - 2 * jj, i - 2 - 2 * jj], c[0], c[1], False),
                                     (carries, acc))
        carries, acc = lax.fori_loop(0, i % 2, lambda jj, c: blocks([0], c[0], c[1], False), (carries, acc))
        o_ref[...] = acc.astype(BF16)
        o32_ref[...] = acc

    q_spec, k_spec, v_spec = _sb_specs(n_pairs, 0)
    blk = pl.BlockSpec((None, SB_T, LANES), lambda b, p, i: (b, i, p))
    return pl.pallas_call(
        body,
        grid=(bl, n_pairs, SEQ // SB_T),
        in_specs=[q_spec, k_spec, v_spec],
        out_specs=[blk, blk],
        out_shape=[jax.ShapeDtypeStruct((bl, SEQ, SB_W), BF16), jax.ShapeDtypeStruct((bl, SEQ, SB_W), F32)],
        compiler_params=pltpu.CompilerParams(dimension_semantics=("parallel", "parallel", "arbitrary")),
        name="sb_fwd",
    )(proj3, proj3, proj3)


def _sb_bwd(proj3, o_a, do_a):
    bl = proj3.shape[0]
    n_pairs = SB_W // LANES
    nq = SEQ // SB_T

    def body(q_ref, k_ref, v_ref, o_ref, do_ref, dq_ref, dk_ref, dv_ref, dk_acc, dv_acc):
        i = pl.program_id(2)

        @pl.when(i == 0)
        def _():
            dk_acc[...] = jnp.zeros_like(dk_acc)
            dv_acc[...] = jnp.zeros_like(dv_acc)

        row, col, lane = _sb_masks()
        causal = col < row
        u_excl = (row > col).astype(BF16)
        u_incl = (row >= col).astype(BF16)
        q = q_ref[...]
        do = do_ref[...]
        prod = do.astype(F32) * o_ref[...]
        heads = []
        for h in range(2):
            mh = (lane // HEAD_DIM) == h
            d_tot = jnp.sum(jnp.where(mh, prod, 0.0), axis=1, keepdims=True)
            heads.append((mh, jnp.where(mh, q, jnp.zeros_like(q)) * SB_SCALE, jnp.where(mh, do, jnp.zeros_like(do)), d_tot))

        def blocks(js, carries, c_das, dq, diag):
            starts = [pl.multiple_of(j * SB_T, SB_T) for j in js]
            ks = [k_ref[pl.ds(s, SB_T), :] for s in starts]
            vs = [v_ref[pl.ds(s, SB_T), :] for s in starts]
            chains = [(b, h) for b in range(len(js)) for h in range(2)]
            z = {c: _dot_nt(heads[c[1]][1], ks[c[0]]) for c in chains}
            dw = {c: _dot_nt(heads[c[1]][2], vs[c[0]]) for c in chains}
            lb, lk, e = {}, {}, {}
            for c in chains:
                lb[c], lk[c], e[c] = _sb_logs(z[c])
                if diag:
                    lk[c] = jnp.where(causal, lk[c], 0.0)
            r = {c: _split_dot(lk[c], u_excl) for c in chains}
            carries, c_das = list(carries), list(c_das)
            wb, da = {}, {}
            for b, h in chains:
                w_c = jnp.exp(lb[b, h] + r[b, h] + carries[h])
                wb[b, h] = (jnp.where(causal, w_c, 0.0) if diag else w_c).astype(BF16)
                da[b, h] = dw[b, h] * wb[b, h].astype(F32)
                carries[h] = carries[h] + (r[b, h][:, :1] + lk[b, h][:, :1])
            sfx = {c: _split_dot(da[c], u_incl) for c in chains}
            dz = {}
            for b, h in chains:
                dlk = heads[h][3] - c_das[h] - sfx[b, h]
                if diag:
                    dlk = jnp.where(causal, dlk, 0.0)
                c_das[h] = c_das[h] + sfx[b, h][:, :1]
                inv = 1.0 / (1.0 + e[b, h])
                pos = z[b, h] >= 0.0
                beta = jnp.where(pos, inv, e[b, h] * inv)
                one_m_beta = jnp.where(pos, e[b, h] * inv, inv)
                dz[b, h] = (da[b, h] * one_m_beta - dlk * beta).astype(BF16)
            for b, h in chains:
                dq = dq + _dot(dz[b, h], jnp.where(heads[h][0], ks[b], jnp.zeros_like(ks[b])))
            for b in range(len(js)):
                dk_acc[pl.ds(starts[b], SB_T), :] += _dot_tn(dz[b, 0], heads[0][1]) + _dot_tn(dz[b, 1], heads[1][1])
                dv_acc[pl.ds(starts[b], SB_T), :] += _dot_tn(wb[b, 0], heads[0][2]) + _dot_tn(wb[b, 1], heads[1][2])
            return tuple(carries), tuple(c_das), dq

        zero = jnp.zeros((SB_T, 1), F32)
        state = blocks([i], (zero, zero), (zero, zero), jnp.zeros((SB_T, LANES), F32), True)
        state = lax.fori_loop(0, i // 2, lambda jj, c: blocks([i - 1 - 2 * jj, i - 2 - 2 * jj], c[0], c[1], c[2], False), state)
        state = lax.fori_loop(0, i % 2, lambda jj, c: blocks([0], c[0], c[1], c[2], False), state)
        dq_ref[...] = (state[2] * SB_SCALE).astype(BF16)

        @pl.when(i == nq - 1)
        def _():
            dk_ref[...] = dk_acc[...].astype(BF16)
            dv_ref[...] = dv_acc[...].astype(BF16)

    q_spec, k_spec, v_spec = _sb_specs(n_pairs, 0)
    blk = pl.BlockSpec((None, SB_T, LANES), lambda b, p, i: (b, i, p))
    seq = pl.BlockSpec((None, SEQ, LANES), lambda b, p, i: (b, 0, p))
    shape = jax.ShapeDtypeStruct((bl, SEQ, SB_W), BF16)
    return pl.pallas_call(
        body,
        grid=(bl, n_pairs, nq),
        in_specs=[q_spec, k_spec, v_spec, blk, blk],
        out_specs=[blk, seq, seq],
        out_shape=[shape, shape, shape],
        scratch_shapes=[pltpu.VMEM((SEQ, LANES), F32), pltpu.VMEM((SEQ, LANES), F32)],
        compiler_params=pltpu.CompilerParams(dimension_semantics=("parallel", "parallel", "arbitrary")),
        name="sb_bwd",
    )(proj3, proj3, proj3, o_a, do_a)


BAND = 128


BAND_CH = 4
BAND_HEADS = DIL_W // HEAD_DIM


def _swap_half(x):
    n = x.shape[-1]
    lane = lax.broadcasted_iota(jnp.int32, (1, n), 1)
    return jnp.where((lane % HEAD_DIM) < HEAD_DIM // 2, pltpu.roll(x, n - HEAD_DIM // 2, 1), pltpu.roll(x, HEAD_DIM // 2, 1))


def _rope(x, cos, sin_signed):
    x = x.astype(F32)
    return x * cos + _swap_half(x) * sin_signed


def _band_valid(g, blk):
    nb = jnp.where(g == 0, 16, jnp.where(g == 1, 4, 1))
    first_key = jnp.where(lax.rem(blk, nb) != 0, 0, BAND)
    qi = lax.broadcasted_iota(jnp.int32, (BAND, 2 * BAND), 0) + BAND
    kj = lax.broadcasted_iota(jnp.int32, (BAND, 2 * BAND), 1)
    dist = qi - kj
    return (dist >= 0) & (dist <= BAND) & (kj >= first_key)


def _band_specs():
    last_before = lambda i: jnp.maximum(i * BAND_CH - 1, 0)
    cur = pl.BlockSpec((None, BAND_CH, BAND, DIL_W), lambda g, i: (g, i, 0, 0))
    prev = pl.BlockSpec((None, None, BAND, DIL_W), lambda g, i: (g, last_before(i), 0, 0))
    tab = pl.BlockSpec((None, BAND_CH, BAND, DIL_W), lambda g, i: (g, lax.rem(i, 16 // BAND_CH), 0, 0))
    tab_prev = pl.BlockSpec((None, None, BAND, DIL_W), lambda g, i: (g, lax.rem(last_before(i), 16), 0, 0))
    return cur, prev, tab, tab_prev


def _band_load(q_ref, k_ref, kp_ref, v_ref, vp_ref, c_ref, s_ref, cp_ref, sp_ref):
    qs = [(_rope(q_ref[b], c_ref[b], s_ref[b]) * SB_SCALE).astype(BF16) for b in range(BAND_CH)]
    ks = [_rope(kp_ref[...], cp_ref[...], sp_ref[...]).astype(BF16)]
    ks += [_rope(k_ref[b], c_ref[b], s_ref[b]).astype(BF16) for b in range(BAND_CH)]
    vs = [vp_ref[...]] + [v_ref[b] for b in range(BAND_CH)]
    k2 = [jnp.concatenate([ks[b], ks[b + 1]], axis=0) for b in range(BAND_CH)]
    v2 = [jnp.concatenate([vs[b], vs[b + 1]], axis=0) for b in range(BAND_CH)]
    return qs, k2, v2


def _band_fwd(q_s, k_s, v_s, cos_t, sin_t):
    def body(q_ref, k_ref, kp_ref, v_ref, vp_ref, c_ref, s_ref, cp_ref, sp_ref, o_ref, lse_ref):
        g, i = pl.program_id(0), pl.program_id(1)
        qs, k2, v2 = _band_load(q_ref, k_ref, kp_ref, v_ref, vp_ref, c_ref, s_ref, cp_ref, sp_ref)
        lane = lax.broadcasted_iota(jnp.int32, (1, DIL_W), 1)
        for b in range(BAND_CH):
            valid = _band_valid(g, i * BAND_CH + b)
            hs = range(BAND_HEADS)
            mh = [(lane // HEAD_DIM) == h for h in hs]
            s = [jnp.where(valid, _dot_nt(jnp.where(mh[h], qs[b], jnp.zeros_like(qs[b])), k2[b]), NEG_INF) for h in hs]
            m = [jnp.max(s[h], axis=1, keepdims=True) for h in hs]
            p = [jnp.exp(s[h] - m[h]) for h in hs]
            den = [jnp.sum(p[h], axis=1, keepdims=True) for h in hs]
            pv = [_dot(p[h].astype(BF16), jnp.where(mh[h], v2[b], jnp.zeros_like(v2[b]))) for h in hs]
            o = jnp.zeros((BAND, DIL_W), F32)
            lse = jnp.zeros((BAND, DIL_W), F32)
            for h in hs:
                o = o + pv[h] * (1.0 / den[h])
                lse = jnp.where(mh[h], m[h] + jnp.log(den[h]), lse)
            o_ref[b] = o
            lse_ref[b] = lse

    cur, prev, tab, tab_prev = _band_specs()
    shape = jax.ShapeDtypeStruct(q_s.shape, F32)
    return pl.pallas_call(
        body,
        grid=(3, q_s.shape[1] // BAND_CH),
        in_specs=[cur, cur, prev, cur, prev, tab, tab, tab_prev, tab_prev],
        out_specs=[cur, cur],
        out_shape=[shape, shape],
        compiler_params=pltpu.CompilerParams(dimension_semantics=("parallel", "parallel")),
        name="band_fwd",
    )(q_s, k_s, k_s, v_s, v_s, cos_t, sin_t, cos_t, sin_t)


def _band_bwd(q_s, k_s, v_s, cos_t, sin_t, do_s, lse_s, delta_s):
    def body(q_ref, k_ref, kp_ref, v_ref, vp_ref, c_ref, s_ref, cp_ref, sp_ref, do_ref, lse_ref, dl_ref,
             dq_ref, dk_ref, dv_ref, dkf_ref, dvf_ref):
        g, i = pl.program_id(0), pl.program_id(1)
        qs, k2, v2 = _band_load(q_ref, k_ref, kp_ref, v_ref, vp_ref, c_ref, s_ref, cp_ref, sp_ref)
        lane = lax.broadcasted_iota(jnp.int32, (1, DIL_W), 1)
        dks, dvs = [], []
        for b in range(BAND_CH):
            valid = _band_valid(g, i * BAND_CH + b)
            do, lse, dl = do_ref[b], lse_ref[b], dl_ref[b]
            hs = range(BAND_HEADS)
            mh = [(lane // HEAD_DIM) == h for h in hs]
            qh = [jnp.where(mh[h], qs[b], jnp.zeros_like(qs[b])) for h in hs]
            doh = [jnp.where(mh[h], do, jnp.zeros_like(do)) for h in hs]
            s = [_dot_nt(qh[h], k2[b]) for h in hs]
            dp = [_dot_nt(doh[h], v2[b]) for h in hs]
            p = [jnp.where(valid, jnp.exp(s[h] - lse[:, h * HEAD_DIM:h * HEAD_DIM + 1]), 0.0) for h in hs]
            ds = [(p[h] * (dp[h] - dl[:, h * HEAD_DIM:h * HEAD_DIM + 1])).astype(BF16) for h in hs]
            pb = [p[h].astype(BF16) for h in hs]
            dq = sum(_dot(ds[h], jnp.where(mh[h], k2[b], jnp.zeros_like(k2[b]))) for h in hs)
            dk2 = sum(_dot_tn(ds[h], qh[h]) for h in hs)
            dv2 = sum(_dot_tn(pb[h], doh[h]) for h in hs)
            dq_ref[b] = dq * SB_SCALE
            dks.append(dk2)
            dvs.append(dv2)
        dkf_ref[...] = dks[0][:BAND]
        dvf_ref[...] = dvs[0][:BAND]
        for b in range(BAND_CH):
            last = b == BAND_CH - 1
            dk_ref[b] = dks[b][BAND:] if last else dks[b][BAND:] + dks[b + 1][:BAND]
            dv_ref[b] = dvs[b][BAND:] if last else dvs[b][BAND:] + dvs[b + 1][:BAND]

    cur, prev, tab, tab_prev = _band_specs()
    first = pl.BlockSpec((None, None, BAND, DIL_W), lambda g, i: (g, i, 0, 0))
    n_chunks = q_s.shape[1] // BAND_CH
    shape = jax.ShapeDtypeStruct(q_s.shape, F32)
    shape_first = jax.ShapeDtypeStruct((3, n_chunks, BAND, DIL_W), F32)
    return pl.pallas_call(
        body,
        grid=(3, n_chunks),
        in_specs=[cur, cur, prev, cur, prev, tab, tab, tab_prev, tab_prev, cur, cur, cur],
        out_specs=[cur, cur, cur, first, first],
        out_shape=[shape, shape, shape, shape_first, shape_first],
        compiler_params=pltpu.CompilerParams(dimension_semantics=("parallel", "parallel")),
        name="band_bwd",
    )(q_s, k_s, k_s, v_s, v_s, cos_t, sin_t, cos_t, sin_t, do_s, lse_s, delta_s)


def _band_combine(dq, dk, dv, dk_first, dv_first, cos_t, sin_t):
    n_chunks = dk_first.shape[1]

    def body(dq_ref, dk_ref, dkn_ref, dv_ref, dvn_ref, c_ref, s_ref, oq_ref, ok_ref, ov_ref):
        nxt = (pl.program_id(1) < n_chunks - 1).astype(F32)
        for b in range(BAND_CH):
            cos, sin = c_ref[b], s_ref[b]
            dq_b, dk_b, dv_b = dq_ref[b], dk_ref[b], dv_ref[b]
            if b == BAND_CH - 1:
                dk_b = dk_b + nxt * dkn_ref[...]
                dv_b = dv_b + nxt * dvn_ref[...]
            oq_ref[b] = (dq_b * cos - _swap_half(dq_b) * sin).astype(BF16)
            ok_ref[b] = (dk_b * cos - _swap_half(dk_b) * sin).astype(BF16)
            ov_ref[b] = dv_b.astype(BF16)

    cur, _, tab, _ = _band_specs()
    nxt = pl.BlockSpec((None, None, BAND, DIL_W), lambda g, i: (g, jnp.minimum(i + 1, n_chunks - 1), 0, 0))
    shape = jax.ShapeDtypeStruct(dq.shape, BF16)
    return pl.pallas_call(
        body,
        grid=(3, n_chunks),
        in_specs=[cur, cur, nxt, cur, nxt, tab, tab],
        out_specs=[cur] * 3,
        out_shape=[shape] * 3,
        compiler_params=pltpu.CompilerParams(dimension_semantics=("parallel", "parallel")),
        name="band_combine",
    )(dq, dk, dk_first, dv, dv_first, cos_t, sin_t)


def _band_merge(o_g, lse_g):
    t, tm = o_g.shape[1], 512

    def body(o_ref, l_ref, ob_ref, lse_ref):
        l0, l1, l2 = l_ref[0], l_ref[1], l_ref[2]
        m = jnp.maximum(jnp.maximum(l0, l1), l2)
        lse = m + jnp.log(jnp.exp(l0 - m) + jnp.exp(l1 - m) + jnp.exp(l2 - m))
        ob_ref[...] = (jnp.exp(l0 - lse) * o_ref[0] + jnp.exp(l1 - lse) * o_ref[1] + jnp.exp(l2 - lse) * o_ref[2]).astype(BF16)
        lse_ref[...] = lse

    spec3 = pl.BlockSpec((3, tm, DIL_W), lambda i: (0, i, 0))
    spec = pl.BlockSpec((tm, DIL_W), lambda i: (i, 0))
    return pl.pallas_call(
        body,
        grid=(t // tm,),
        in_specs=[spec3, spec3],
        out_specs=[spec, spec],
        out_shape=[jax.ShapeDtypeStruct((t, DIL_W), BF16), jax.ShapeDtypeStruct((t, DIL_W), F32)],
        compiler_params=pltpu.CompilerParams(dimension_semantics=("parallel",)),
        name="band_merge",
    )(o_g, lse_g)


def _band_delta(do_b, o_b):
    def fn(do, o):
        r = lax.broadcasted_iota(jnp.int32, (DIL_W, DIL_W), 0) // HEAD_DIM
        c = lax.broadcasted_iota(jnp.int32, (DIL_W, DIL_W), 1) // HEAD_DIM
        return (_split_dot(do.astype(F32) * o.astype(F32), (r == c).astype(BF16)),)

    return _rowwise(fn, [(do_b, "row"), (o_b, "row")], [(DIL_W, F32, "row")], tm=512, name="band_delta")[0]


MEM_T = 512
MEM_SCALE = 128 ** -0.5
MEM_Q_COL = (D_IN - MEM_W) // LANES


def _mem_specs():
    q = pl.BlockSpec((None, MEM_T, LANES), lambda b, h, i: (b, i, MEM_Q_COL + h))
    k = pl.BlockSpec((None, MEM_LEN, LANES), lambda b, h, i: (b, 0, h))
    v = pl.BlockSpec((None, MEM_LEN, LANES), lambda b, h, i: (b, 0, MEM_W // LANES + h))
    blk = pl.BlockSpec((None, MEM_T, LANES), lambda b, h, i: (b, i, h))
    return q, k, v, blk


def _mem_probs(q, k):
    s = _dot_nt(q, k) * MEM_SCALE
    p = jnp.exp(s - jnp.max(s, axis=1, keepdims=True))
    return p * (1.0 / jnp.sum(p, axis=1, keepdims=True))


def _mem_fwd(proj3, kv3):
    bl = proj3.shape[0]

    def body(q_ref, k_ref, v_ref, o_ref):
        p = _mem_probs(q_ref[...], k_ref[...])
        o_ref[...] = _dot(p.astype(BF16), v_ref[...]).astype(BF16)

    q, k, v, blk = _mem_specs()
    return pl.pallas_call(
        body,
        grid=(bl, MEM_W // LANES, SEQ // MEM_T),
        in_specs=[q, k, v],
        out_specs=blk,
        out_shape=jax.ShapeDtypeStruct((bl, SEQ, MEM_W), BF16),
        compiler_params=pltpu.CompilerParams(dimension_semantics=("parallel", "parallel", "parallel")),
        name="mem_fwd",
    )(proj3, kv3, kv3)


def _mem_bwd(proj3, kv3, do_c):
    bl = proj3.shape[0]

    def body(q_ref, k_ref, v_ref, do_ref, dq_ref, dk_ref, dv_ref):
        @pl.when(pl.program_id(2) == 0)
        def _():
            dk_ref[...] = jnp.zeros_like(dk_ref)
            dv_ref[...] = jnp.zeros_like(dv_ref)

        q, k, do = q_ref[...], k_ref[...], do_ref[...]
        p = _mem_probs(q, k)
        dp = _dot_nt(do, v_ref[...])
        ds = (p * (dp - jnp.sum(p * dp, axis=1, keepdims=True)) * MEM_SCALE).astype(BF16)
        dq_ref[...] = _dot(ds, k).astype(BF16)
        dk_ref[...] += _dot_tn(ds, q)
        dv_ref[...] += _dot_tn(p.astype(BF16), do)

    q, k, v, blk = _mem_specs()
    kv_out = pl.BlockSpec((None, MEM_LEN, LANES), lambda b, h, i: (b, 0, h))
    return pl.pallas_call(
        body,
        grid=(bl, MEM_W // LANES, SEQ // MEM_T),
        in_specs=[q, k, v, blk],
        out_specs=[blk, kv_out, kv_out],
        out_shape=[jax.ShapeDtypeStruct((bl, SEQ, MEM_W), BF16), jax.ShapeDtypeStruct((bl, MEM_LEN, MEM_W), F32),
                   jax.ShapeDtypeStruct((bl, MEM_LEN, MEM_W), F32)],
        compiler_params=pltpu.CompilerParams(dimension_semantics=("parallel", "parallel", "arbitrary")),
        name="mem_bwd",
    )(proj3, kv3, kv3, do_c)


def _place():
    x, y, c = lax.axis_index("x"), lax.axis_index("y"), lax.axis_index("c")
    return x, y, c


def _other_chips(x, y):
    return [(1 - x, y), (x, 1 - y), (1 - x, 1 - y)]


def _remote(src, dst, send_sem, recv_sem, to):
    return pltpu.make_async_remote_copy(src_ref=src, dst_ref=dst, send_sem=send_sem, recv_sem=recv_sem,
                                        device_id=to, device_id_type=MESH)


ANY = pl.BlockSpec(memory_space=pl.ANY)


def _gather_weights(shards):
    n = len(shards)

    def body(*refs):
        in_refs, out_refs = refs[:n], refs[n:2 * n]
        send_sems, recv_sems, local_sems = refs[2 * n:]
        x, y, c = _place()
        sibling = (x, y, 1 - c)
        chips = _other_chips(x, y)
        local, first, passed = [], [], []
        for k in range(n):
            hf = in_refs[k].shape[0] // 2

            def half(px, py, pc, k=k, hf=hf):
                return out_refs[k].at[2 * px + py, pl.ds(pc * hf, hf), :]

            local.append(pltpu.make_async_copy(in_refs[k], out_refs[k].at[2 * x + y], local_sems.at[k]))
            src = in_refs[k].at[pl.ds(c * hf, hf), :]
            for j, chip in enumerate(chips):
                s = 6 * k + j
                first.append(_remote(src, half(x, y, c), send_sems.at[s], recv_sems.at[s], (*chip, c)))
                passed.append((_remote(src, half(*chip, c), send_sems.at[s], recv_sems.at[s], (*chip, c)),
                               _remote(half(*chip, c), half(*chip, c), send_sems.at[s + 3], recv_sems.at[s + 3], sibling),
                               _remote(src, half(*chip, 1 - c), send_sems.at[s + 3], recv_sems.at[s + 3], sibling)))
        for cp in local + first:
            cp.start()
        for landed, forward, _ in passed:
            landed.wait_recv()
            forward.start()
        for _, _, from_sibling in passed:
            from_sibling.wait_recv()
        for cp in first:
            cp.wait_send()
        for _, forward, _ in passed:
            forward.wait_send()
        for cp in local:
            cp.wait()

    return pl.pallas_call(
        body,
        in_specs=[ANY] * n,
        out_specs=[ANY] * n,
        out_shape=[jax.ShapeDtypeStruct((N_CHIPS,) + s.shape, s.dtype) for s in shards],
        scratch_shapes=[pltpu.SemaphoreType.DMA((6 * n,)), pltpu.SemaphoreType.DMA((6 * n,)), pltpu.SemaphoreType.DMA((n,))],
        name="gather_weights",
    )(*shards)


def _pair_exchange(grads):
    n = len(grads)

    def body(*refs):
        g_refs, land_refs = refs[:n], refs[n:2 * n]
        send_sems, recv_sems = refs[2 * n:]
        x, y, c = _place()
        cps = []
        for k in range(n):
            hf = g_refs[k].shape[1] // 2
            src = g_refs[k].at[:, pl.ds((1 - c) * hf, hf), :]
            cps.append(_remote(src, land_refs[k], send_sems.at[k], recv_sems.at[k], (x, y, 1 - c)))
        for cp in cps:
            cp.start()
        for cp in cps:
            cp.wait()

    return pl.pallas_call(
        body,
        in_specs=[ANY] * n,
        out_specs=[ANY] * n,
        out_shape=[jax.ShapeDtypeStruct((N_CHIPS, g.shape[1] // 2, g.shape[2]), F32) for g in grads],
        scratch_shapes=[pltpu.SemaphoreType.DMA((n,)), pltpu.SemaphoreType.DMA((n,))],
        name="pair_exchange",
    )(*grads)


def _pair_add(g, land, c_arr, *, name):
    _, a, b = g.shape
    hf = a // 2

    def body(c_ref, g_ref, l_ref, o_ref):
        o_ref[...] = (g_ref[...] + l_ref[...]).astype(BF16)

    return pl.pallas_call(
        body,
        grid_spec=pltpu.PrefetchScalarGridSpec(
            num_scalar_prefetch=1,
            grid=(N_CHIPS,),
            in_specs=[pl.BlockSpec((None, None, hf, b), lambda s, c_ref: (s, c_ref[0], 0, 0)),
                      pl.BlockSpec((None, hf, b), lambda s, c_ref: (s, 0, 0))],
            out_specs=pl.BlockSpec((None, hf, b), lambda s, c_ref: (s, 0, 0)),
        ),
        out_shape=jax.ShapeDtypeStruct((N_CHIPS, hf, b), BF16),
        compiler_params=pltpu.CompilerParams(dimension_semantics=("parallel",)),
        name=name,
    )(c_arr, g.reshape(N_CHIPS, 2, hf, b), land)


def _chip_exchange(parts):
    n = len(parts)

    def body(*refs):
        p_refs, land_refs = refs[:n], refs[n:2 * n]
        send_sems, recv_sems, local_sems = refs[2 * n:]
        x, y, c = _place()
        me = 2 * x + y
        chips = _other_chips(x, y)
        local, sends, recvs = [], [], []
        for k in range(n):
            local.append(pltpu.make_async_copy(p_refs[k].at[me], land_refs[k].at[me], local_sems.at[k]))
            for j, (cx, cy) in enumerate(chips):
                s = 3 * k + j
                sends.append(_remote(p_refs[k].at[2 * cx + cy], land_refs[k].at[me], send_sems.at[s], recv_sems.at[s], (cx, cy, c)))
                recvs.append(_remote(p_refs[k].at[me], land_refs[k].at[2 * cx + cy], send_sems.at[s], recv_sems.at[s], (cx, cy, c)))
        for cp in local + sends:
            cp.start()
        for cp in recvs:
            cp.wait_recv()
        for cp in sends:
            cp.wait_send()
        for cp in local:
            cp.wait()

    return pl.pallas_call(
        body,
        in_specs=[ANY] * n,
        out_specs=[ANY] * n,
        out_shape=[jax.ShapeDtypeStruct(p.shape, p.dtype) for p in parts],
        scratch_shapes=[pltpu.SemaphoreType.DMA((3 * n,)), pltpu.SemaphoreType.DMA((3 * n,)), pltpu.SemaphoreType.DMA((n,))],
        name="chip_exchange",
    )(*parts)


def _chip_add(land, *, name):
    _, r, b = land.shape

    def body(l_ref, o_ref):
        o_ref[...] = ((l_ref[0].astype(F32) + l_ref[1].astype(F32)) + l_ref[2].astype(F32)) + l_ref[3].astype(F32)

    tr = r // 2
    return pl.pallas_call(
        body,
        grid=(r // tr,),
        in_specs=[pl.BlockSpec((N_CHIPS, tr, b), lambda i: (0, i, 0))],
        out_specs=pl.BlockSpec((tr, b), lambda i: (i, 0)),
        out_shape=jax.ShapeDtypeStruct((r, b), F32),
        compiler_params=pltpu.CompilerParams(dimension_semantics=("parallel",)),
        name=name,
    )(land)


def _pair_share(halves):
    n = len(halves)

    def body(*refs):
        h_refs, out_refs = refs[:n], refs[n:2 * n]
        send_sems, recv_sems, local_sems = refs[2 * n:]
        x, y, c = _place()
        local, sends, recvs = [], [], []
        for k in range(n):
            local.append(pltpu.make_async_copy(h_refs[k], out_refs[k].at[c], local_sems.at[k]))
            sends.append(_remote(h_refs[k], out_refs[k].at[c], send_sems.at[k], recv_sems.at[k], (x, y, 1 - c)))
            recvs.append(_remote(h_refs[k], out_refs[k].at[1 - c], send_sems.at[k], recv_sems.at[k], (x, y, 1 - c)))
        for cp in local + sends:
            cp.start()
        for cp in recvs:
            cp.wait_recv()
        for cp in sends:
            cp.wait_send()
        for cp in local:
            cp.wait()

    return pl.pallas_call(
        body,
        in_specs=[ANY] * n,
        out_specs=[ANY] * n,
        out_shape=[jax.ShapeDtypeStruct((2,) + h.shape, F32) for h in halves],
        scratch_shapes=[pltpu.SemaphoreType.DMA((n,)), pltpu.SemaphoreType.DMA((n,)), pltpu.SemaphoreType.DMA((n,))],
        name="pair_share",
    )(*halves)


def _all_sum_small(part):
    def body(p_ref, o_ref, slots, send_sems, recv_sems):
        x, y, c = _place()
        me = 4 * x + 2 * y + c
        slots[me] = p_ref[...]
        peers = [(x ^ dx, y ^ dy, c ^ dc) for dx in (0, 1) for dy in (0, 1) for dc in (0, 1)][1:]
        sends = [_remote(p_ref, slots.at[me], send_sems.at[k], recv_sems.at[k], peer) for k, peer in enumerate(peers)]
        for cp in sends:
            cp.start()
        for k, (px, py, pc) in enumerate(peers):
            _remote(p_ref, slots.at[4 * px + 2 * py + pc], send_sems.at[k], recv_sems.at[k], (px, py, pc)).wait_recv()
        for cp in sends:
            cp.wait_send()
        acc = slots[0]
        for d in range(1, 8):
            acc = acc + slots[d]
        o_ref[...] = acc

    vmem = pl.BlockSpec(memory_space=pltpu.VMEM)
    return pl.pallas_call(
        body,
        in_specs=[vmem],
        out_specs=vmem,
        out_shape=jax.ShapeDtypeStruct(part.shape, F32),
        scratch_shapes=[pltpu.VMEM((8,) + part.shape, F32), pltpu.SemaphoreType.DMA((7,)), pltpu.SemaphoreType.DMA((7,))],
        name="all_sum_small",
    )(part)


def _deinterleave(a, d):
    b, s, c = a.shape
    return a.reshape(b, s // d, d, c).transpose(0, 2, 1, 3).reshape(b * s // BAND, BAND, c)


def _reinterleave(a, d, b):
    c = a.shape[-1]
    return a.reshape(b, d, SEQ // d, c).transpose(0, 2, 1, 3).reshape(b, SEQ, c)


def _rope_tables():
    half = HEAD_DIM // 2
    inv_freq = ROPE_THETA ** (-jnp.arange(half, dtype=F32) * 2.0 / HEAD_DIM)
    ang = jnp.arange(SEQ, dtype=F32)[:, None] * inv_freq[None, :]
    cos = jnp.tile(jnp.cos(ang), (1, 2 * BAND_HEADS))
    sin = jnp.tile(jnp.concatenate([-jnp.sin(ang), jnp.sin(ang)], axis=1), (1, BAND_HEADS))
    cos_t = jnp.stack([_deinterleave(cos[None], d) for d in DIL_D])
    sin_t = jnp.stack([_deinterleave(sin[None], d) for d in DIL_D])
    return cos_t, sin_t


def _local_step(x, mem, loss_target, g_pre_mix, g_post_mix, g_pre_ffn, g_post_ffn, g_mem, b_gate, w):
    bl = x.shape[0]
    t = bl * SEQ
    chips = range(N_CHIPS)
    w_in_full = jnp.concatenate([w["w_in"][s] for s in chips], axis=1)
    w_mem_kv_full = w["w_mem_kv"].reshape(D_MODEL, 2 * MEM_W)
    w_o_full = w["w_o"].reshape(D_MODEL, D_MODEL)
    w_ffn_out_full = w["w_ffn_out"].reshape(D_FF, D_MODEL)
    half_ff = D_FF // 2

    x2 = x.reshape(t, D_MODEL)
    tgt2 = loss_target.reshape(t, D_MODEL)
    mem2 = mem.reshape(bl * MEM_LEN, D_MODEL)

    h = _norm_fwd(x2, g_pre_mix, name="norm_x")
    proj = _mm([(h, w_in_full)], nt=False, tm=512, tn=2176, out_dtypes=[BF16], name="proj")
    gates = _mm([(h, w["w_gate"], None, "j")], nt=False, tm=512, tn=w["w_gate"].shape[2], out_dtypes=[BF16], name="gates",
                bias=b_gate, epilogue=lambda acc: (_sigmoid(acc),))
    hm = _norm_fwd(mem2, g_mem, name="norm_mem")
    kv_m = _mm([(hm, w_mem_kv_full)], nt=False, tm=512, tn=1024, out_dtypes=[BF16], name="mem_kv")
    proj3 = proj.reshape(bl, SEQ, D_IN)
    kv3 = kv_m.reshape(bl, MEM_LEN, 2 * MEM_W)

    o_a, o_a32 = _sb_fwd(proj3)

    cos_t, sin_t = _rope_tables()
    dil0 = 3 * SB_W

    def group_cols(g, part):
        c0 = dil0 + (3 * g + part) * DIL_W
        return proj3[:, :, c0:c0 + DIL_W]

    q_s = jnp.stack([_deinterleave(group_cols(g, 0), d) for g, d in enumerate(DIL_D)])
    k_s = jnp.stack([_deinterleave(group_cols(g, 1), d) for g, d in enumerate(DIL_D)])
    v_s = jnp.stack([_deinterleave(group_cols(g, 2), d) for g, d in enumerate(DIL_D)])
    o_gs, lse_gs = _band_fwd(q_s, k_s, v_s, cos_t, sin_t)
    o_g = jnp.stack([_reinterleave(o_gs[g], d, bl) for g, d in enumerate(DIL_D)]).reshape(3, t, DIL_W)
    lse_g = jnp.stack([_reinterleave(lse_gs[g], d, bl) for g, d in enumerate(DIL_D)]).reshape(3, t, DIL_W)
    o_b, lse_b = _band_merge(o_g, lse_g)

    o_c = _mem_fwd(proj3, kv3)

    o_a2, o_c2 = o_a.reshape(t, SB_W), o_c.reshape(t, MEM_W)
    y_a, y_b, y_c, merged = _branch_merge_fwd(o_a2, o_b, o_c2, w["w_br_sb"], w["w_br_dil"], w["w_br_mem"], gates)
    mix = _mm([(merged, w_o_full)], nt=False, tm=512, tn=1024, out_dtypes=[F32], name="mix")
    x1, h2 = _mid_fwd(mix, x2, g_post_mix, g_pre_ffn)
    gg, uu, f = _ffn_in_fwd(h2, w["w_ffn_in"])
    f2 = _mm([(f, w_ffn_out_full)], nt=False, tm=512, tn=1024, out_dtypes=[F32], name="ffn_out")

    dy, df2, dg_post_ffn, loss_row = _loss_bwd(f2, x1, g_post_ffn, tgt2)

    dg_ffn, du_ffn = _mm([(df2, w_ffn_out_full)], nt=True, tm=512, tn=half_ff, out_dtypes=[BF16, BF16], name="d_ffn_act",
                         extras=(gg, uu), epilogue=_swiglu_bwd_epilogue)
    gw = {}
    gw["w_ffn_out"] = _mm_tn(f, df2, tm=half_ff, tn=1024, tk=512, name="gw_ffn_out").reshape(N_CHIPS, D_FF // N_CHIPS, D_MODEL)
    gw_ffn_g = _mm_tn(h2, dg_ffn, tm=1024, tn=half_ff, tk=512, name="gw_ffn_gate", out_shards=True)
    gw_ffn_u = _mm_tn(h2, du_ffn, tm=1024, tn=half_ff, tk=512, name="gw_ffn_up", out_shards=True)
    gw["w_ffn_in"] = jnp.concatenate([gw_ffn_g, gw_ffn_u], axis=0)
    dh2 = _mm([(dg_ffn, w["w_ffn_in"], 0, 0), (dg_ffn, w["w_ffn_in"], 1, 1), (du_ffn, w["w_ffn_in"], 0, 2),
               (du_ffn, w["w_ffn_in"], 1, 3)], nt=True, tm=256, tn=1024, out_dtypes=[F32], name="d_h2")
    dx1, dmix, dg_pre_ffn, dg_post_mix = _mid_bwd(dh2, x1, mix, g_pre_ffn, g_post_mix, dy)

    gw["w_o"] = _mm_tn(merged, dmix, tm=1024, tn=1024, tk=512, name="gw_o").reshape(N_CHIPS, D_MODEL // N_CHIPS, D_MODEL)
    dmerged = _mm([(dmix, w_o_full)], nt=True, tm=512, tn=1024, out_dtypes=[F32], name="d_merged")
    dy_a, dy_b, dy_c, dgpre, db_gate = _gate_bwd(dmerged, gates, y_a, y_b, y_c)
    br_cols = D_MODEL // N_CHIPS
    gw["w_br_sb"] = _mm_tn(o_a2, dy_a, tm=512, tn=br_cols, tk=1024, name="gw_br_sb", out_shards=True)
    gw["w_br_dil"] = _mm_tn(o_b, dy_b, tm=256, tn=br_cols, tk=1024, name="gw_br_dil", out_shards=True)
    gw["w_br_mem"] = _mm_tn(o_c2, dy_c, tm=512, tn=br_cols, tk=1024, name="gw_br_mem", out_shards=True)
    gw["w_gate"] = _mm_tn(h, dgpre, tm=1024, tn=w["w_gate"].shape[2], tk=512, name="gw_gate", out_shards=True)
    do_a = _mm([(dy_a, w["w_br_sb"], s, s) for s in chips], nt=True, tm=512, tn=SB_W, out_dtypes=[BF16], name="d_o_a")
    do_b = _mm([(dy_b, w["w_br_dil"], s, s) for s in chips], nt=True, tm=512, tn=DIL_W, out_dtypes=[BF16], name="d_o_b")
    do_c = _mm([(dy_c, w["w_br_mem"], s, s) for s in chips], nt=True, tm=512, tn=MEM_W, out_dtypes=[BF16], name="d_o_c")

    dq_c, dk_m, dv_m = _mem_bwd(proj3, kv3, do_c.reshape(bl, SEQ, MEM_W))
    dkv_m = jnp.concatenate([dk_m, dv_m], axis=-1).reshape(bl * MEM_LEN, 2 * MEM_W).astype(BF16)
    gw["w_mem_kv"] = _mm_tn(hm, dkv_m, tm=1024, tn=1024, tk=512, name="gw_mem_kv").reshape(N_CHIPS, D_MODEL // N_CHIPS, 2 * MEM_W)
    dhm = _mm([(dkv_m, w_mem_kv_full)], nt=True, tm=512, tn=1024, out_dtypes=[F32], name="d_hm")
    dg_mem = _mem_norm_bwd(dhm, mem2, g_mem)

    delta_b = _band_delta(do_b, o_b)
    do_b3, lse_b3, delta_b3 = (a.reshape(bl, SEQ, DIL_W) for a in (do_b, lse_b, delta_b))
    do_s = jnp.stack([_deinterleave(do_b3, d) for d in DIL_D])
    lse_s = jnp.stack([_deinterleave(lse_b3, d) for d in DIL_D])
    delta_s = jnp.stack([_deinterleave(delta_b3, d) for d in DIL_D])
    dq_r, dk_r, dv_r, dk_first, dv_first = _band_bwd(q_s, k_s, v_s, cos_t, sin_t, do_s, lse_s, delta_s)
    dq_s, dk_s, dv_s = _band_combine(dq_r, dk_r, dv_r, dk_first, dv_first, cos_t, sin_t)
    d_dil = []
    for g, d in enumerate(DIL_D):
        d_dil += [_reinterleave(a[g], d, bl) for a in (dq_s, dk_s, dv_s)]

    dq_a, dk_a, dv_a = _sb_bwd(proj3, o_a32, do_a.reshape(bl, SEQ, SB_W))

    dproj = jnp.concatenate([dq_a, dk_a, dv_a] + d_dil + [dq_c], axis=-1).reshape(t, D_IN)
    in_cols = D_IN // N_CHIPS
    dproj_s = jnp.stack([dproj[:, s * in_cols:(s + 1) * in_cols] for s in chips])
    gw["w_in"] = _mm_tn(h, dproj_s, tm=1024, tn=in_cols, tk=512, name="gw_in")
    dh = _mm([(dproj_s, w["w_in"], s, s) for s in chips] + [(dgpre, w["w_gate"], s, s) for s in chips],
             nt=True, tm=256, tn=1024, out_dtypes=[F32], name="d_h")
    grad_x, dg_pre_mix = _first_bwd(dh, x2, g_pre_mix, dx1)
    small = jnp.concatenate([dg_pre_mix, dg_post_mix, dg_pre_ffn, dg_post_ffn, dg_mem, db_gate.reshape(3, D_MODEL)], axis=0)
    return loss_row[0, 0], grad_x.reshape(bl, SEQ, D_MODEL), gw, small


def kernel(x, mem, g_pre_mix, g_post_mix, g_pre_ffn, g_post_ffn, g_mem, w_in, w_mem_kv, w_br_sb, w_br_dil, w_br_mem, w_gate, b_gate, w_o, w_ffn_in, w_ffn_out, loss_target, m_g_pre_mix, m_g_post_mix, m_g_pre_ffn, m_g_post_ffn, m_g_mem, m_w_in, m_w_mem_kv, m_w_br_sb, m_w_br_dil, m_w_br_mem, m_w_gate, m_b_gate, m_w_o, m_w_ffn_in, m_w_ffn_out, v_g_pre_mix, v_g_post_mix, v_g_pre_ffn, v_g_post_ffn, v_g_mem, v_w_in, v_w_mem_kv, v_w_br_sb, v_w_br_dil, v_w_br_mem, v_w_gate, v_b_gate, v_w_o, v_w_ffn_in, v_w_ffn_out):
    w_shards = dict(w_in=w_in[0], w_mem_kv=w_mem_kv[0], w_br_sb=w_br_sb[0], w_br_dil=w_br_dil[0], w_br_mem=w_br_mem[0],
                    w_gate=w_gate[0], w_o=w_o[0], w_ffn_in=w_ffn_in[0], w_ffn_out=w_ffn_out[0])
    m_shards = dict(w_in=m_w_in[0], w_mem_kv=m_w_mem_kv[0], w_br_sb=m_w_br_sb[0], w_br_dil=m_w_br_dil[0], w_br_mem=m_w_br_mem[0],
                    w_gate=m_w_gate[0], w_o=m_w_o[0], w_ffn_in=m_w_ffn_in[0], w_ffn_out=m_w_ffn_out[0])
    v_shards = dict(w_in=v_w_in[0], w_mem_kv=v_w_mem_kv[0], w_br_sb=v_w_br_sb[0], w_br_dil=v_w_br_dil[0], w_br_mem=v_w_br_mem[0],
                    w_gate=v_w_gate[0], w_o=v_w_o[0], w_ffn_in=v_w_ffn_in[0], w_ffn_out=v_w_ffn_out[0])

    names = [name for name, _, _ in PACK]
    w = dict(zip(names, _gather_weights([w_shards[name].astype(BF16) for name in names])))

    loss_local, grad_x, gw, small = _local_step(x, mem, loss_target, g_pre_mix, g_post_mix, g_pre_ffn, g_post_ffn, g_mem, b_gate, w)
    loss = lax.psum(loss_local, ("x", "y", "c"))

    c_arr = lax.axis_index("c").astype(jnp.int32).reshape(1)
    grads = [gw[name] for name in names]
    lands = _pair_exchange(grads)
    parts = [_pair_add(g, l, c_arr, name="pair_add_" + name) for name, g, l in zip(names, grads, lands)]
    halves = [_chip_add(l, name="chip_add_" + name) for name, l in zip(names, _chip_exchange(parts))]
    g_shards = {name: r.reshape(w_shards[name].shape) for name, r in zip(names, _pair_share(halves))}
    small = _all_sum_small(small)

    upd = {}
    for name, (a, b), _ in PACK:
        tm = a // 2 if a % 16 == 0 else a
        upd[name] = _adamw(w_shards[name], g_shards[name], m_shards[name], v_shards[name], tm=tm, name="adamw_" + name)

    def small8(gs, b):
        return jnp.concatenate(gs + [b.reshape(3, D_MODEL)], axis=0)

    sw = small8([g_pre_mix, g_post_mix, g_pre_ffn, g_post_ffn, g_mem], b_gate)
    sm = small8([m_g_pre_mix, m_g_post_mix, m_g_pre_ffn, m_g_post_ffn, m_g_mem], m_b_gate)
    sv = small8([v_g_pre_mix, v_g_post_mix, v_g_pre_ffn, v_g_post_ffn, v_g_mem], v_b_gate)
    s_upd = _adamw(sw, small, sm, sv, tm=8, name="adamw_small")

    def small_out(a):
        return [a[0:1], a[1:2], a[2:3], a[3:4], a[4:5]]

    order = ["w_in", "w_mem_kv", "w_br_sb", "w_br_dil", "w_br_mem", "w_gate", "b_gate", "w_o", "w_ffn_in", "w_ffn_out"]

    def leaves(small_arr, big):
        out = small_out(small_arr)
        for name in order:
            out.append(small_arr[5:8].reshape(1, 3 * D_MODEL) if name == "b_gate" else big[name][None])
        return out

    grads_out = leaves(small, g_shards)
    delta_out = leaves(s_upd[0], {n: u[0] for n, u in upd.items()})
    m_out = leaves(s_upd[1], {n: u[1] for n, u in upd.items()})
    v_out = leaves(s_upd[2], {n: u[2] for n, u in upd.items()})
    return (loss, grad_x, *grads_out, *delta_out, *m_out, *v_out)
```

```python
import functools

import jax
import jax.numpy as jnp
from jax import lax
from jax.experimental import pallas as pl
from jax.experimental.pallas import tpu as pltpu

F32 = jnp.float32
BF16 = jnp.bfloat16
MESH = pl.DeviceIdType.MESH

D_MODEL = 1024
SEQ = 2048
HEAD_DIM = 64
SB_W = 512
DIL_W = 256
MEM_W = 512
MEM_LEN = 256
D_IN = 3 * SB_W + 9 * DIL_W + MEM_W
D_FF = 2816
DIL_D = (1, 4, 16)
ROPE_THETA = 10000.0
NORM_EPS = 1e-6
NEG_INF = -1e30
LANES = 128

ADAM_LR = 0.001
ADAM_B1 = 0.9
ADAM_B2 = 0.999
ADAM_EPS = 1e-08
ADAM_WD = 0.01
ADAM_STEP = 10

N_CHIPS = 4
PACK = (
    ("w_in", (1024, 1088), 1),
    ("w_mem_kv", (256, 1024), 0),
    ("w_br_sb", (512, 256), 1),
    ("w_br_dil", (256, 256), 1),
    ("w_br_mem", (512, 256), 1),
    ("w_gate", (1024, 768), 1),
    ("w_o", (256, 1024), 0),
    ("w_ffn_in", (1024, 1408), 1),
    ("w_ffn_out", (704, 1024), 0),
)
PACK_ROWS = sum(a * b for _, (a, b), _ in PACK) // D_MODEL
HALF_ROWS = PACK_ROWS // 2


def _dot(a, b):
    return lax.dot_general(a, b, (((1,), (0,)), ((), ())), preferred_element_type=F32)


def _dot_nt(a, b):
    return lax.dot_general(a, b, (((1,), (1,)), ((), ())), preferred_element_type=F32)


def _dot_tn(a, b):
    return lax.dot_general(a, b, (((0,), (0,)), ((), ())), preferred_element_type=F32)


def _split_dot(x, u):
    hi = x.astype(BF16)
    lo = (x - hi.astype(F32)).astype(BF16)
    return _dot(hi, u) + _dot(lo, u)


def _mm(pairs, *, nt, tm, tn, out_dtypes, name, bias=None, extras=(), epilogue=None):
    pairs = [p if len(p) == 4 else (p[0], p[1], None, None) for p in pairs]
    m = pairs[0][0].shape[-2]
    b0 = pairs[0][1]
    if nt:
        n = b0.shape[-2]
    else:
        n = b0.shape[-1] * (b0.shape[0] if b0.ndim == 3 else 1)
    n_pairs, n_extra, n_out = len(pairs), len(extras), len(out_dtypes)
    tm = min(tm, m)
    assert m % tm == 0 and n % tn == 0
    has_bias = bias is not None

    def body(*refs):
        acc = None
        for i in range(n_pairs):
            a, b = refs[2 * i][...], refs[2 * i + 1][...]
            p = _dot_nt(a, b) if nt else _dot(a, b)
            acc = p if acc is None else acc + p
        pos = 2 * n_pairs
        if has_bias:
            acc = acc + refs[pos][...]
            pos += 1
        ex = [r[...] for r in refs[pos:pos + n_extra]]
        outs = refs[pos + n_extra:]
        vals = (acc,) if epilogue is None else epilogue(acc, *ex)
        for r, v, dt in zip(outs, vals, out_dtypes):
            r[...] = v.astype(dt)

    in_specs, args = [], []
    for a, b, a_col, b_sel in pairs:
        k = b.shape[-1] if nt else b.shape[-2]
        assert a_col is not None or a.shape[1] == k
        if a.ndim == 3:
            in_specs.append(pl.BlockSpec((None, tm, k), lambda j, i, c=a_col: (c, i, 0)))
        else:
            in_specs.append(pl.BlockSpec((tm, k), lambda j, i, c=a_col or 0: (i, c)))
        if b.ndim == 2:
            in_specs.append(pl.BlockSpec((tn, k), lambda j, i: (j, 0)) if nt else pl.BlockSpec((k, tn), lambda j, i: (0, j)))
        elif nt:
            in_specs.append(pl.BlockSpec((None, tn, k), lambda j, i, s=b_sel: (s, j, 0)))
        else:
            assert b_sel == "j" and b.shape[-1] == tn
            in_specs.append(pl.BlockSpec((None, k, tn), lambda j, i: (j, 0, 0)))
        args += [a, b]
    if has_bias:
        in_specs.append(pl.BlockSpec((1, tn), lambda j, i: (0, j)))
        args.append(bias)
    for e in extras:
        in_specs.append(pl.BlockSpec((tm, tn), lambda j, i: (i, j)))
        args.append(e)
    out = pl.pallas_call(
        body,
        grid=(n // tn, m // tm),
        in_specs=in_specs,
        out_specs=[pl.BlockSpec((tm, tn), lambda j, i: (i, j)) for _ in range(n_out)],
        out_shape=[jax.ShapeDtypeStruct((m, n), dt) for dt in out_dtypes],
        compiler_params=pltpu.CompilerParams(dimension_semantics=("parallel", "parallel")),
        name=name,
    )(*args)
    return out[0] if n_out == 1 else out


def _mm_tn(a, b, *, tm, tn, tk, name, out_shards=False):
    k, m = a.shape
    b_shards = b.ndim == 3
    out_shards = out_shards or b_shards
    n = b.shape[0] * b.shape[2] if b_shards else b.shape[1]
    tk = min(tk, k)
    assert m % tm == 0 and n % tn == 0 and k % tk == 0 and (not b_shards or b.shape[2] == tn)

    def body(a_ref, b_ref, o_ref):
        @pl.when(pl.program_id(2) == 0)
        def _():
            o_ref[...] = jnp.zeros_like(o_ref)

        o_ref[...] += _dot_tn(a_ref[...], b_ref[...])

    if b_shards:
        b_spec = pl.BlockSpec((None, tk, tn), lambda i, j, kk: (j, kk, 0))
    else:
        b_spec = pl.BlockSpec((tk, tn), lambda i, j, kk: (kk, j))
    if out_shards:
        out_spec = pl.BlockSpec((None, tm, tn), lambda i, j, kk: (j, i, 0))
        out_shape = jax.ShapeDtypeStruct((n // tn, m, tn), F32)
    else:
        out_spec = pl.BlockSpec((tm, tn), lambda i, j, kk: (i, j))
        out_shape = jax.ShapeDtypeStruct((m, n), F32)
    return pl.pallas_call(
        body,
        grid=(m // tm, n // tn, k // tk),
        in_specs=[pl.BlockSpec((tk, tm), lambda i, j, kk: (kk, i)), b_spec],
        out_specs=out_spec,
        out_shape=out_shape,
        compiler_params=pltpu.CompilerParams(dimension_semantics=("parallel", "parallel", "arbitrary")),
        name=name,
    )(a, b)


def _rowwise(fn, ins, outs, *, tm, name):
    rows = next(a.shape[0] for a, kind in ins if kind == "row")
    tm = min(tm, rows)
    assert rows % tm == 0
    n_in = len(ins)

    def body(*refs):
        vals = fn(*[r[...] for r in refs[:n_in]])
        for (_, dt, kind), r, v in zip(outs, refs[n_in:], vals):
            if kind == "row":
                r[...] = v.astype(dt)
            else:
                @pl.when(pl.program_id(0) == 0)
                def _(r=r):
                    r[...] = jnp.zeros_like(r)

                r[...] += v

    in_specs = [pl.BlockSpec((tm, a.shape[1]), lambda i: (i, 0)) if kind == "row" else pl.BlockSpec(a.shape, lambda i: (0, 0))
                for a, kind in ins]
    out_specs = [pl.BlockSpec((tm, c), lambda i: (i, 0)) if kind == "row" else pl.BlockSpec((1, c), lambda i: (0, 0))
                 for c, _, kind in outs]
    out_shape = [jax.ShapeDtypeStruct((rows if kind == "row" else 1, c), dt) for c, dt, kind in outs]
    has_acc = any(kind == "acc" for _, _, kind in outs)
    return pl.pallas_call(
        body,
        grid=(rows // tm,),
        in_specs=in_specs,
        out_specs=out_specs,
        out_shape=out_shape,
        compiler_params=pltpu.CompilerParams(dimension_semantics=("arbitrary" if has_acc else "parallel",)),
        name=name,
    )(*[a for a, _ in ins])


def _rstd(x):
    return lax.rsqrt(jnp.mean(x * x, axis=-1, keepdims=True) + NORM_EPS)


def _norm_bwd(dout, xin, g):
    r = _rstd(xin)
    n = xin * r
    dn = dout * g
    dg = jnp.sum(dout * n, axis=0, keepdims=True)
    dx = r * (dn - n * jnp.mean(dn * n, axis=-1, keepdims=True))
    return dx, dg


def _sigmoid(x):
    return 1.0 / (1.0 + jnp.exp(-x))


def _norm_fwd(x, g, *, name):
    def fn(x, g):
        return ((x * _rstd(x)) * g,)

    return _rowwise(fn, [(x, "row"), (g, "vec")], [(D_MODEL, BF16, "row")], tm=512, name=name)[0]


def _mid_fwd(mix, x, g_post_mix, g_pre_ffn):
    def fn(mix, x, g2, g3):
        x1 = x + (mix * _rstd(mix)) * g2
        return x1, (x1 * _rstd(x1)) * g3

    return _rowwise(fn, [(mix, "row"), (x, "row"), (g_post_mix, "vec"), (g_pre_ffn, "vec")],
                    [(D_MODEL, F32, "row"), (D_MODEL, BF16, "row")], tm=512, name="mid_fwd")


def _loss_bwd(f2, x1, g_post_ffn, tgt):
    def fn(f2, x1, g4, tgt):
        r = _rstd(f2)
        n = f2 * r
        err = x1 + n * g4 - tgt
        loss = 0.5 * jnp.sum(jnp.mean(err * err, axis=-1, keepdims=True), axis=0, keepdims=True)
        dy = err * (1.0 / D_MODEL)
        dn = dy * g4
        dg4 = jnp.sum(dy * n, axis=0, keepdims=True)
        df2 = r * (dn - n * jnp.mean(dn * n, axis=-1, keepdims=True))
        return dy, df2, dg4, jnp.broadcast_to(loss, (1, LANES))

    return _rowwise(fn, [(f2, "row"), (x1, "row"), (g_post_ffn, "vec"), (tgt, "row")],
                    [(D_MODEL, F32, "row"), (D_MODEL, BF16, "row"), (D_MODEL, F32, "acc"), (LANES, F32, "acc")],
                    tm=512, name="loss_bwd")


def _mid_bwd(dh2, x1, mix, g_pre_ffn, g_post_mix, dy):
    def fn(dh2, x1, mix, g3, g2, dy):
        d3, dg3 = _norm_bwd(dh2, x1, g3)
        dx1 = dy + d3
        dmix, dg2 = _norm_bwd(dx1, mix, g2)
        return dx1, dmix, dg3, dg2

    return _rowwise(fn, [(dh2, "row"), (x1, "row"), (mix, "row"), (g_pre_ffn, "vec"), (g_post_mix, "vec"), (dy, "row")],
                    [(D_MODEL, F32, "row"), (D_MODEL, BF16, "row"), (D_MODEL, F32, "acc"), (D_MODEL, F32, "acc")],
                    tm=256, name="mid_bwd")


def _first_bwd(dh, x, g_pre_mix, dx1):
    def fn(dh, x, g1, dx1):
        d1, dg1 = _norm_bwd(dh, x, g1)
        return dx1 + d1, dg1

    return _rowwise(fn, [(dh, "row"), (x, "row"), (g_pre_mix, "vec"), (dx1, "row")],
                    [(D_MODEL, F32, "row"), (D_MODEL, F32, "acc")], tm=512, name="first_bwd")


def _mem_norm_bwd(dhm, mem, g_mem):
    def fn(dhm, mem, g):
        return (jnp.sum(dhm * (mem * _rstd(mem)), axis=0, keepdims=True),)

    return _rowwise(fn, [(dhm, "row"), (mem, "row"), (g_mem, "vec")], [(D_MODEL, F32, "acc")], tm=512, name="mem_norm_bwd")[0]


def _gate_bwd(dmerged, gates, ya, yb, yc):
    def fn(dm, gt, ya, yb, yc):
        gt = gt.astype(F32)
        outs, dgp = [], []
        for i, y in enumerate((ya, yb, yc)):
            gi = gt[:, i * D_MODEL:(i + 1) * D_MODEL]
            outs.append(dm * gi)
            dgp.append(dm * y.astype(F32) * gi * (1.0 - gi))
        dgpre = jnp.concatenate(dgp, axis=1)
        return outs[0], outs[1], outs[2], dgpre, jnp.sum(dgpre, axis=0, keepdims=True)

    return _rowwise(fn, [(dmerged, "row"), (gates, "row"), (ya, "row"), (yb, "row"), (yc, "row")],
                    [(D_MODEL, BF16, "row")] * 3 + [(3 * D_MODEL, BF16, "row"), (3 * D_MODEL, F32, "acc")],
                    tm=256, name="gate_bwd")


def _adamw_math(w, g, m, v):
    m = ADAM_B1 * m + (1.0 - ADAM_B1) * g
    v = ADAM_B2 * v + (1.0 - ADAM_B2) * (g * g)
    m_hat = m / (1.0 - ADAM_B1 ** ADAM_STEP)
    v_hat = v / (1.0 - ADAM_B2 ** ADAM_STEP)
    delta = -ADAM_LR * (m_hat / (jnp.sqrt(v_hat) + ADAM_EPS) + ADAM_WD * w)
    return delta, m, v


def _adamw(w, g, m, v, *, tm, name):
    c = w.shape[1]
    return _rowwise(_adamw_math, [(w, "row"), (g, "row"), (m, "row"), (v, "row")], [(c, F32, "row")] * 3, tm=tm, name=name)


def _adamw_halves(w, g_mine, g_theirs, m, v, c_arr, *, name):
    a, b = w.shape
    hf = a // 2
    tr = hf // 4

    def body(c_ref, w_ref, gm_ref, gt_ref, m_ref, v_ref, g_out, d_out, m_out, v_out):
        g = jnp.where(pl.program_id(0) == c_ref[0], gm_ref[...], gt_ref[...])
        d, m_new, v_new = _adamw_math(w_ref[...], g, m_ref[...], v_ref[...])
        g_out[...] = g
        d_out[...] = d
        m_out[...] = m_new
        v_out[...] = v_new

    full = pl.BlockSpec((tr, b), lambda hh, i, c_ref: (hh * (hf // tr) + i, 0))
    half = pl.BlockSpec((tr, b), lambda hh, i, c_ref: (i, 0))
    return pl.pallas_call(
        body,
        grid_spec=pltpu.PrefetchScalarGridSpec(
            num_scalar_prefetch=1,
            grid=(2, hf // tr),
            in_specs=[full, half, half, full, full],
            out_specs=[full] * 4,
        ),
        out_shape=[jax.ShapeDtypeStruct((a, b), F32)] * 4,
        compiler_params=pltpu.CompilerParams(dimension_semantics=("parallel", "parallel")),
        name=name,
    )(c_arr, w, g_mine, g_theirs, m, v)


def _ffn_in_fwd(h2, w_ffn):
    m, tm, tn = h2.shape[0], 512, w_ffn.shape[2]
    assert 2 * tn == D_FF

    def body(h_ref, wg_ref, wu_ref, g_ref, u_ref, f_ref):
        h = h_ref[...]
        g = _dot(h, wg_ref[...])
        u = _dot(h, wu_ref[...])
        g_ref[...] = g.astype(BF16)
        u_ref[...] = u.astype(BF16)
        f_ref[...] = (g * _sigmoid(g) * u).astype(BF16)

    o_spec = pl.BlockSpec((tm, tn), lambda j, i: (i, j))
    return pl.pallas_call(
        body,
        grid=(D_FF // tn, m // tm),
        in_specs=[pl.BlockSpec((tm, D_MODEL), lambda j, i: (i, 0)),
                  pl.BlockSpec((None, D_MODEL, tn), lambda j, i: (j, 0, 0)),
                  pl.BlockSpec((None, D_MODEL, tn), lambda j, i: (j + 2, 0, 0))],
        out_specs=[o_spec, o_spec, o_spec],
        out_shape=[jax.ShapeDtypeStruct((m, D_FF), BF16)] * 3,
        compiler_params=pltpu.CompilerParams(dimension_semantics=("parallel", "parallel")),
        name="ffn_in_fwd",
    )(h2, w_ffn, w_ffn)


def _swiglu_bwd_epilogue(df, g, u):
    g = g.astype(F32)
    u = u.astype(F32)
    sg = _sigmoid(g)
    return df * u * (sg * (1.0 + g * (1.0 - sg))), df * (g * sg)


def _branch_merge_fwd(o_a, o_b, o_c, w_sb, w_dil, w_mem, gates):
    m, tm = o_a.shape[0], 256

    def body(oa_ref, ob_ref, oc_ref, wa_ref, wb_ref, wc_ref, gt_ref, ya_ref, yb_ref, yc_ref, mg_ref):
        def project(o_ref, w_ref):
            o = o_ref[...]
            return jnp.concatenate([_dot(o, w_ref[s]) for s in range(N_CHIPS)], axis=1)

        ya = project(oa_ref, wa_ref)
        yb = project(ob_ref, wb_ref)
        yc = project(oc_ref, wc_ref)
        gt = gt_ref[...].astype(F32)
        ya_ref[...] = ya.astype(BF16)
        yb_ref[...] = yb.astype(BF16)
        yc_ref[...] = yc.astype(BF16)
        mg_ref[...] = (gt[:, :D_MODEL] * ya + gt[:, D_MODEL:2 * D_MODEL] * yb + gt[:, 2 * D_MODEL:] * yc).astype(BF16)

    row = lambda c: pl.BlockSpec((tm, c), lambda i: (i, 0))
    full = lambda a: pl.BlockSpec(a.shape, lambda i: (0, 0, 0))
    return pl.pallas_call(
        body,
        grid=(m // tm,),
        in_specs=[row(SB_W), row(DIL_W), row(MEM_W), full(w_sb), full(w_dil), full(w_mem), row(3 * D_MODEL)],
        out_specs=[row(D_MODEL)] * 4,
        out_shape=[jax.ShapeDtypeStruct((m, D_MODEL), BF16)] * 4,
        compiler_params=pltpu.CompilerParams(dimension_semantics=("parallel",)),
        name="branch_merge_fwd",
    )(o_a, o_b, o_c, w_sb, w_dil, w_mem, gates)


SB_T = 256
SB_SCALE = HEAD_DIM ** -0.5


def _sb_masks():
    row = lax.broadcasted_iota(jnp.int32, (SB_T, SB_T), 0)
    col = lax.broadcasted_iota(jnp.int32, (SB_T, SB_T), 1)
    lane = lax.broadcasted_iota(jnp.int32, (1, LANES), 1)
    return row, col, lane


def _sb_logs(z):
    e = jnp.exp(-jnp.abs(z))
    sp = jnp.log1p(e)
    return jnp.minimum(z, 0.0) - sp, jnp.minimum(-z, 0.0) - sp, e


def _sb_specs(n_heads_pairs, col0):
    q = pl.BlockSpec((None, SB_T, LANES), lambda b, p, i: (b, i, col0 + p))
    k = pl.BlockSpec((None, SEQ, LANES), lambda b, p, i: (b, 0, col0 + n_heads_pairs + p))
    v = pl.BlockSpec((None, SEQ, LANES), lambda b, p, i: (b, 0, col0 + 2 * n_heads_pairs + p))
    return q, k, v


def _sb_fwd(proj3):
    bl = proj3.shape[0]
    n_pairs = SB_W // LANES

    def body(q_ref, k_ref, v_ref, o_ref, o32_ref):
        i = pl.program_id(2)
        row, col, lane = _sb_masks()
        causal = col < row
        u_excl = (row > col).astype(BF16)
        q = q_ref[...]
        heads = []
        for h in range(2):
            mh = (lane // HEAD_DIM) == h
            heads.append((mh, jnp.where(mh, q, jnp.zeros_like(q)) * SB_SCALE))

        def blocks(js, carries, acc, diag):
            ks = [k_ref[pl.ds(pl.multiple_of(j * SB_T, SB_T), SB_T), :] for j in js]
            vs = [v_ref[pl.ds(pl.multiple_of(j * SB_T, SB_T), SB_T), :] for j in js]
            chains = [(b, h) for b in range(len(js)) for h in range(2)]
            z = {c: _dot_nt(heads[c[1]][1], ks[c[0]]) for c in chains}
            lb, lk = {}, {}
            for c in chains:
                lb[c], lk[c], _ = _sb_logs(z[c])
                if diag:
                    lk[c] = jnp.where(causal, lk[c], 0.0)
            r = {c: _split_dot(lk[c], u_excl) for c in chains}
            carries = list(carries)
            w = {}
            for b, h in chains:
                w_c = jnp.exp(lb[b, h] + r[b, h] + carries[h])
                w[b, h] = (jnp.where(causal, w_c, 0.0) if diag else w_c).astype(BF16)
                carries[h] = carries[h] + (r[b, h][:, :1] + lk[b, h][:, :1])
            for b, h in chains:
                acc = acc + _dot(w[b, h], jnp.where(heads[h][0], vs[b], jnp.zeros_like(vs[b])))
            return tuple(carries), acc

        zero = jnp.zeros((SB_T, 1), F32)
        carries, acc = blocks([i], (zero, zero), jnp.zeros((SB_T, LANES), F32), True)
        carries, acc = lax.fori_loop(0, i // 2, lambda jj, c: blocks([i - 1 - 2 * jj, i - 2 - 2 * jj], c[0], c[1], False),
                                     (carries, acc))
        carries, acc = lax.fori_loop(0, i % 2, lambda jj, c: blocks([0], c[0], c[1], False), (carries, acc))
        o_ref[...] = acc.astype(BF16)
        o32_ref[...] = acc

    q_spec, k_spec, v_spec = _sb_specs(n_pairs, 0)
    blk = pl.BlockSpec((None, SB_T, LANES), lambda b, p, i: (b, i, p))
    return pl.pallas_call(
        body,
        grid=(bl, n_pairs, SEQ // SB_T),
        in_specs=[q_spec, k_spec, v_spec],
        out_specs=[blk, blk],
        out_shape=[jax.ShapeDtypeStruct((bl, SEQ, SB_W), BF16), jax.ShapeDtypeStruct((bl, SEQ, SB_W), F32)],
        compiler_params=pltpu.CompilerParams(dimension_semantics=("parallel", "parallel", "arbitrary")),
        name="sb_fwd",
    )(proj3, proj3, proj3)


def _sb_bwd(proj3, o_a, do_a):
    bl = proj3.shape[0]
    n_pairs = SB_W // LANES
    nq = SEQ // SB_T

    def body(q_ref, k_ref, v_ref, o_ref, do_ref, dq_ref, dk_ref, dv_ref, dk_acc, dv_acc):
        i = pl.program_id(2)

        @pl.when(i == 0)
        def _():
            dk_acc[...] = jnp.zeros_like(dk_acc)
            dv_acc[...] = jnp.zeros_like(dv_acc)

        row, col, lane = _sb_masks()
        causal = col < row
        u_excl = (row > col).astype(BF16)
        u_incl = (row >= col).astype(BF16)
        q = q_ref[...]
        do = do_ref[...]
        prod = do.astype(F32) * o_ref[...]
        heads = []
        for h in range(2):
            mh = (lane // HEAD_DIM) == h
            d_tot = jnp.sum(jnp.where(mh, prod, 0.0), axis=1, keepdims=True)
            heads.append((mh, jnp.where(mh, q, jnp.zeros_like(q)) * SB_SCALE, jnp.where(mh, do, jnp.zeros_like(do)), d_tot))

        def blocks(js, carries, c_das, dq, diag):
            starts = [pl.multiple_of(j * SB_T, SB_T) for j in js]
            ks = [k_ref[pl.ds(s, SB_T), :] for s in starts]
            vs = [v_ref[pl.ds(s, SB_T), :] for s in starts]
            chains = [(b, h) for b in range(len(js)) for h in range(2)]
            z = {c: _dot_nt(heads[c[1]][1], ks[c[0]]) for c in chains}
            dw = {c: _dot_nt(heads[c[1]][2], vs[c[0]]) for c in chains}
            lb, lk, e = {}, {}, {}
            for c in chains:
                lb[c], lk[c], e[c] = _sb_logs(z[c])
                if diag:
                    lk[c] = jnp.where(causal, lk[c], 0.0)
            r = {c: _split_dot(lk[c], u_excl) for c in chains}
            carries, c_das = list(carries), list(c_das)
            wb, da = {}, {}
            for b, h in chains:
                w_c = jnp.exp(lb[b, h] + r[b, h] + carries[h])
                wb[b, h] = (jnp.where(causal, w_c, 0.0) if diag else w_c).astype(BF16)
                da[b, h] = dw[b, h] * wb[b, h].astype(F32)
                carries[h] = carries[h] + (r[b, h][:, :1] + lk[b, h][:, :1])
            sfx = {c: _split_dot(da[c], u_incl) for c in chains}
            dz = {}
            for b, h in chains:
                dlk = heads[h][3] - c_das[h] - sfx[b, h]
                if diag:
                    dlk = jnp.where(causal, dlk, 0.0)
                c_das[h] = c_das[h] + sfx[b, h][:, :1]
                inv = 1.0 / (1.0 + e[b, h])
                pos = z[b, h] >= 0.0
                beta = jnp.where(pos, inv, e[b, h] * inv)
                one_m_beta = jnp.where(pos, e[b, h] * inv, inv)
                dz[b, h] = (da[b, h] * one_m_beta - dlk * beta).astype(BF16)
            for b, h in chains:
                dq = dq + _dot(dz[b, h], jnp.where(heads[h][0], ks[b], jnp.zeros_like(ks[b])))
            for b in range(len(js)):
                dk_acc[pl.ds(starts[b], SB_T), :] += _dot_tn(dz[b, 0], heads[0][1]) + _dot_tn(dz[b, 1], heads[1][1])
                dv_acc[pl.ds(starts[b], SB_T), :] += _dot_tn(wb[b, 0], heads[0][2]) + _dot_tn(wb[b, 1], heads[1][2])
            return tuple(carries), tuple(c_das), dq

        zero = jnp.zeros((SB_T, 1), F32)
        state = blocks([i], (zero, zero), (zero, zero), jnp.zeros((SB_T, LANES), F32), True)
        state = lax.fori_loop(0, i // 2, lambda jj, c: blocks([i - 1 - 2 * jj, i - 2 - 2 * jj], c[0], c[1], c[2], False), state)
        state = lax.fori_loop(0, i % 2, lambda jj, c: blocks([0], c[0], c[1], c[2], False), state)
        dq_ref[...] = (state[2] * SB_SCALE).astype(BF16)

        @pl.when(i == nq - 1)
        def _():
            dk_ref[...] = dk_acc[...].astype(BF16)
            dv_ref[...] = dv_acc[...].astype(BF16)

    q_spec, k_spec, v_spec = _sb_specs(n_pairs, 0)
    blk = pl.BlockSpec((None, SB_T, LANES), lambda b, p, i: (b, i, p))
    seq = pl.BlockSpec((None, SEQ, LANES), lambda b, p, i: (b, 0, p))
    shape = jax.ShapeDtypeStruct((bl, SEQ, SB_W), BF16)
    return pl.pallas_call(
        body,
        grid=(bl, n_pairs, nq),
        in_specs=[q_spec, k_spec, v_spec, blk, blk],
        out_specs=[blk, seq, seq],
        out_shape=[shape, shape, shape],
        scratch_shapes=[pltpu.VMEM((SEQ, LANES), F32), pltpu.VMEM((SEQ, LANES), F32)],
        compiler_params=pltpu.CompilerParams(dimension_semantics=("parallel", "parallel", "arbitrary")),
        name="sb_bwd",
    )(proj3, proj3, proj3, o_a, do_a)


BAND = 128


BAND_CH = 4
BAND_HEADS = DIL_W // HEAD_DIM


def _swap_half(x):
    n = x.shape[-1]
    lane = lax.broadcasted_iota(jnp.int32, (1, n), 1)
    return jnp.where((lane % HEAD_DIM) < HEAD_DIM // 2, pltpu.roll(x, n - HEAD_DIM // 2, 1), pltpu.roll(x, HEAD_DIM // 2, 1))


def _rope(x, cos, sin_signed):
    x = x.astype(F32)
    return x * cos + _swap_half(x) * sin_signed


def _band_valid(g, blk):
    nb = jnp.where(g == 0, 16, jnp.where(g == 1, 4, 1))
    first_key = jnp.where(lax.rem(blk, nb) != 0, 0, BAND)
    qi = lax.broadcasted_iota(jnp.int32, (BAND, 2 * BAND), 0) + BAND
    kj = lax.broadcasted_iota(jnp.int32, (BAND, 2 * BAND), 1)
    dist = qi - kj
    return (dist >= 0) & (dist <= BAND) & (kj >= first_key)


def _band_specs():
    last_before = lambda i: jnp.maximum(i * BAND_CH - 1, 0)
    cur = lambda col: pl.BlockSpec((None, BAND_CH, BAND, DIL_W), lambda g, i: (g, i, 0, col))
    prev = lambda col: pl.BlockSpec((None, None, BAND, DIL_W), lambda g, i: (g, last_before(i), 0, col))
    tab = pl.BlockSpec((None, BAND_CH, BAND, DIL_W), lambda g, i: (g, lax.rem(i, 16 // BAND_CH), 0, 0))
    tab_prev = pl.BlockSpec((None, None, BAND, DIL_W), lambda g, i: (g, lax.rem(last_before(i), 16), 0, 0))
    return cur, prev, tab, tab_prev


def _band_load(q_ref, k_ref, kp_ref, v_ref, vp_ref, c_ref, s_ref, cp_ref, sp_ref):
    qs = [(_rope(q_ref[b], c_ref[b], s_ref[b]) * SB_SCALE).astype(BF16) for b in range(BAND_CH)]
    ks = [_rope(kp_ref[...], cp_ref[...], sp_ref[...]).astype(BF16)]
    ks += [_rope(k_ref[b], c_ref[b], s_ref[b]).astype(BF16) for b in range(BAND_CH)]
    vs = [vp_ref[...]] + [v_ref[b] for b in range(BAND_CH)]
    k2 = [jnp.concatenate([ks[b], ks[b + 1]], axis=0) for b in range(BAND_CH)]
    v2 = [jnp.concatenate([vs[b], vs[b + 1]], axis=0) for b in range(BAND_CH)]
    return qs, k2, v2


def _band_fwd(qkv_s, cos_t, sin_t):
    def body(q_ref, k_ref, kp_ref, v_ref, vp_ref, c_ref, s_ref, cp_ref, sp_ref, ol_ref):
        g, i = pl.program_id(0), pl.program_id(1)
        qs, k2, v2 = _band_load(q_ref, k_ref, kp_ref, v_ref, vp_ref, c_ref, s_ref, cp_ref, sp_ref)
        lane = lax.broadcasted_iota(jnp.int32, (1, DIL_W), 1)
        for b in range(BAND_CH):
            valid = _band_valid(g, i * BAND_CH + b)
            hs = range(BAND_HEADS)
            mh = [(lane // HEAD_DIM) == h for h in hs]
            s = [jnp.where(valid, _dot_nt(jnp.where(mh[h], qs[b], jnp.zeros_like(qs[b])), k2[b]), NEG_INF) for h in hs]
            m = [jnp.max(s[h], axis=1, keepdims=True) for h in hs]
            p = [jnp.exp(s[h] - m[h]) for h in hs]
            den = [jnp.sum(p[h], axis=1, keepdims=True) for h in hs]
            pv = [_dot(p[h].astype(BF16), jnp.where(mh[h], v2[b], jnp.zeros_like(v2[b]))) for h in hs]
            o = jnp.zeros((BAND, DIL_W), F32)
            lse = jnp.zeros((BAND, DIL_W), F32)
            for h in hs:
                o = o + pv[h] * (1.0 / den[h])
                lse = jnp.where(mh[h], m[h] + jnp.log(den[h]), lse)
            ol_ref[b, :, :DIL_W] = o
            ol_ref[b, :, DIL_W:] = lse

    cur, prev, tab, tab_prev = _band_specs()
    n_blk = qkv_s.shape[1]
    return pl.pallas_call(
        body,
        grid=(3, n_blk // BAND_CH),
        in_specs=[cur(0), cur(1), prev(1), cur(2), prev(2), tab, tab, tab_prev, tab_prev],
        out_specs=pl.BlockSpec((None, BAND_CH, BAND, 2 * DIL_W), lambda g, i: (g, i, 0, 0)),
        out_shape=jax.ShapeDtypeStruct((3, n_blk, BAND, 2 * DIL_W), F32),
        compiler_params=pltpu.CompilerParams(dimension_semantics=("parallel", "parallel")),
        name="band_fwd",
    )(qkv_s, qkv_s, qkv_s, qkv_s, qkv_s, cos_t, sin_t, cos_t, sin_t)


def _band_bwd(qkv_s, cos_t, sin_t, dcat_s):
    def body(q_ref, k_ref, kp_ref, v_ref, vp_ref, c_ref, s_ref, cp_ref, sp_ref, do_ref, lse_ref, dl_ref,
             dq_ref, dk_ref, dv_ref, dkf_ref, dvf_ref):
        g, i = pl.program_id(0), pl.program_id(1)
        qs, k2, v2 = _band_load(q_ref, k_ref, kp_ref, v_ref, vp_ref, c_ref, s_ref, cp_ref, sp_ref)
        lane = lax.broadcasted_iota(jnp.int32, (1, DIL_W), 1)
        dks, dvs = [], []
        for b in range(BAND_CH):
            valid = _band_valid(g, i * BAND_CH + b)
            do, lse, dl = do_ref[b].astype(BF16), lse_ref[b], dl_ref[b]
            hs = range(BAND_HEADS)
            mh = [(lane // HEAD_DIM) == h for h in hs]
            qh = [jnp.where(mh[h], qs[b], jnp.zeros_like(qs[b])) for h in hs]
            doh = [jnp.where(mh[h], do, jnp.zeros_like(do)) for h in hs]
            s = [_dot_nt(qh[h], k2[b]) for h in hs]
            dp = [_dot_nt(doh[h], v2[b]) for h in hs]
            p = [jnp.where(valid, jnp.exp(s[h] - lse[:, h * HEAD_DIM:h * HEAD_DIM + 1]), 0.0) for h in hs]
            ds = [(p[h] * (dp[h] - dl[:, h * HEAD_DIM:h * HEAD_DIM + 1])).astype(BF16) for h in hs]
            pb = [p[h].astype(BF16) for h in hs]
            dq = sum(_dot(ds[h], jnp.where(mh[h], k2[b], jnp.zeros_like(k2[b]))) for h in hs)
            dk2 = sum(_dot_tn(ds[h], qh[h]) for h in hs)
            dv2 = sum(_dot_tn(pb[h], doh[h]) for h in hs)
            dq_ref[b] = dq * SB_SCALE
            dks.append(dk2)
            dvs.append(dv2)
        dkf_ref[...] = dks[0][:BAND]
        dvf_ref[...] = dvs[0][:BAND]
        for b in range(BAND_CH):
            last = b == BAND_CH - 1
            dk_ref[b] = dks[b][BAND:] if last else dks[b][BAND:] + dks[b + 1][:BAND]
            dv_ref[b] = dvs[b][BAND:] if last else dvs[b][BAND:] + dvs[b + 1][:BAND]

    cur, prev, tab, tab_prev = _band_specs()
    first = pl.BlockSpec((None, None, BAND, DIL_W), lambda g, i: (g, i, 0, 0))
    n_blk = qkv_s.shape[1]
    n_chunks = n_blk // BAND_CH
    shape = jax.ShapeDtypeStruct((3, n_blk, BAND, DIL_W), F32)
    shape_first = jax.ShapeDtypeStruct((3, n_chunks, BAND, DIL_W), F32)
    return pl.pallas_call(
        body,
        grid=(3, n_chunks),
        in_specs=[cur(0), cur(1), prev(1), cur(2), prev(2), tab, tab, tab_prev, tab_prev, cur(0), cur(1), cur(2)],
        out_specs=[cur(0), cur(0), cur(0), first, first],
        out_shape=[shape, shape, shape, shape_first, shape_first],
        compiler_params=pltpu.CompilerParams(dimension_semantics=("parallel", "parallel")),
        name="band_bwd",
    )(qkv_s, qkv_s, qkv_s, qkv_s, qkv_s, cos_t, sin_t, cos_t, sin_t, dcat_s, dcat_s, dcat_s)


def _band_combine(dq, dk, dv, dk_first, dv_first, cos_t, sin_t):
    n_chunks = dk_first.shape[1]

    def body(dq_ref, dk_ref, dkn_ref, dv_ref, dvn_ref, c_ref, s_ref, out_ref):
        nxt = (pl.program_id(1) < n_chunks - 1).astype(F32)
        for b in range(BAND_CH):
            cos, sin = c_ref[b], s_ref[b]
            dq_b, dk_b, dv_b = dq_ref[b], dk_ref[b], dv_ref[b]
            if b == BAND_CH - 1:
                dk_b = dk_b + nxt * dkn_ref[...]
                dv_b = dv_b + nxt * dvn_ref[...]
            out_ref[b, :, :DIL_W] = (dq_b * cos - _swap_half(dq_b) * sin).astype(BF16)
            out_ref[b, :, DIL_W:2 * DIL_W] = (dk_b * cos - _swap_half(dk_b) * sin).astype(BF16)
            out_ref[b, :, 2 * DIL_W:] = dv_b.astype(BF16)

    cur, _, tab, _ = _band_specs()
    nxt = pl.BlockSpec((None, None, BAND, DIL_W), lambda g, i: (g, jnp.minimum(i + 1, n_chunks - 1), 0, 0))
    return pl.pallas_call(
        body,
        grid=(3, n_chunks),
        in_specs=[cur(0), cur(0), nxt, cur(0), nxt, tab, tab],
        out_specs=pl.BlockSpec((None, BAND_CH, BAND, 3 * DIL_W), lambda g, i: (g, i, 0, 0)),
        out_shape=jax.ShapeDtypeStruct(dq.shape[:3] + (3 * DIL_W,), BF16),
        compiler_params=pltpu.CompilerParams(dimension_semantics=("parallel", "parallel")),
        name="band_combine",
    )(dq, dk, dk_first, dv, dv_first, cos_t, sin_t)


def _band_merge(ol):
    t, tm = ol.shape[1], 512

    def body(o_ref, l_ref, ob_ref, lse_ref):
        l0, l1, l2 = l_ref[0], l_ref[1], l_ref[2]
        m = jnp.maximum(jnp.maximum(l0, l1), l2)
        lse = m + jnp.log(jnp.exp(l0 - m) + jnp.exp(l1 - m) + jnp.exp(l2 - m))
        ob_ref[...] = (jnp.exp(l0 - lse) * o_ref[0] + jnp.exp(l1 - lse) * o_ref[1] + jnp.exp(l2 - lse) * o_ref[2]).astype(BF16)
        lse_ref[...] = lse

    spec = pl.BlockSpec((tm, DIL_W), lambda i: (i, 0))
    return pl.pallas_call(
        body,
        grid=(t // tm,),
        in_specs=[pl.BlockSpec((3, tm, DIL_W), lambda i: (0, i, 0)), pl.BlockSpec((3, tm, DIL_W), lambda i: (0, i, 1))],
        out_specs=[spec, spec],
        out_shape=[jax.ShapeDtypeStruct((t, DIL_W), BF16), jax.ShapeDtypeStruct((t, DIL_W), F32)],
        compiler_params=pltpu.CompilerParams(dimension_semantics=("parallel",)),
        name="band_merge",
    )(ol, ol)


def _band_delta(do_b, o_b, lse_b):
    def fn(do, o, lse):
        r = lax.broadcasted_iota(jnp.int32, (DIL_W, DIL_W), 0) // HEAD_DIM
        c = lax.broadcasted_iota(jnp.int32, (DIL_W, DIL_W), 1) // HEAD_DIM
        do = do.astype(F32)
        delta = _split_dot(do * o.astype(F32), (r == c).astype(BF16))
        return (jnp.concatenate([do, lse, delta], axis=1),)

    return _rowwise(fn, [(do_b, "row"), (o_b, "row"), (lse_b, "row")], [(3 * DIL_W, F32, "row")], tm=512, name="band_delta")[0]


MEM_T = 512
MEM_SCALE = 128 ** -0.5
MEM_Q_COL = (D_IN - MEM_W) // LANES


def _mem_specs():
    q = pl.BlockSpec((None, MEM_T, LANES), lambda b, h, i: (b, i, MEM_Q_COL + h))
    k = pl.BlockSpec((None, MEM_LEN, LANES), lambda b, h, i: (b, 0, h))
    v = pl.BlockSpec((None, MEM_LEN, LANES), lambda b, h, i: (b, 0, MEM_W // LANES + h))
    blk = pl.BlockSpec((None, MEM_T, LANES), lambda b, h, i: (b, i, h))
    return q, k, v, blk


def _mem_probs(q, k):
    s = _dot_nt(q, k) * MEM_SCALE
    p = jnp.exp(s - jnp.max(s, axis=1, keepdims=True))
    return p * (1.0 / jnp.sum(p, axis=1, keepdims=True))


def _mem_fwd(proj3, kv3):
    bl = proj3.shape[0]

    def body(q_ref, k_ref, v_ref, o_ref):
        p = _mem_probs(q_ref[...], k_ref[...])
        o_ref[...] = _dot(p.astype(BF16), v_ref[...]).astype(BF16)

    q, k, v, blk = _mem_specs()
    return pl.pallas_call(
        body,
        grid=(bl, MEM_W // LANES, SEQ // MEM_T),
        in_specs=[q, k, v],
        out_specs=blk,
        out_shape=jax.ShapeDtypeStruct((bl, SEQ, MEM_W), BF16),
        compiler_params=pltpu.CompilerParams(dimension_semantics=("parallel", "parallel", "parallel")),
        name="mem_fwd",
    )(proj3, kv3, kv3)


def _mem_bwd(proj3, kv3, do_c):
    bl = proj3.shape[0]

    def body(q_ref, k_ref, v_ref, do_ref, dq_ref, dk_ref, dv_ref):
        @pl.when(pl.program_id(2) == 0)
        def _():
            dk_ref[...] = jnp.zeros_like(dk_ref)
            dv_ref[...] = jnp.zeros_like(dv_ref)

        q, k, do = q_ref[...], k_ref[...], do_ref[...]
        p = _mem_probs(q, k)
        dp = _dot_nt(do, v_ref[...])
        ds = (p * (dp - jnp.sum(p * dp, axis=1, keepdims=True)) * MEM_SCALE).astype(BF16)
        dq_ref[...] = _dot(ds, k).astype(BF16)
        dk_ref[...] += _dot_tn(ds, q)
        dv_ref[...] += _dot_tn(p.astype(BF16), do)

    q, k, v, blk = _mem_specs()
    kv_out = pl.BlockSpec((None, MEM_LEN, LANES), lambda b, h, i: (b, 0, h))
    return pl.pallas_call(
        body,
        grid=(bl, MEM_W // LANES, SEQ // MEM_T),
        in_specs=[q, k, v, blk],
        out_specs=[blk, kv_out, kv_out],
        out_shape=[jax.ShapeDtypeStruct((bl, SEQ, MEM_W), BF16), jax.ShapeDtypeStruct((bl, MEM_LEN, MEM_W), F32),
                   jax.ShapeDtypeStruct((bl, MEM_LEN, MEM_W), F32)],
        compiler_params=pltpu.CompilerParams(dimension_semantics=("parallel", "parallel", "arbitrary")),
        name="mem_bwd",
    )(proj3, kv3, kv3, do_c)


def _place():
    x, y, c = lax.axis_index("x"), lax.axis_index("y"), lax.axis_index("c")
    return x, y, c


def _other_chips(x, y):
    return [(1 - x, y), (x, 1 - y), (1 - x, 1 - y)]


def _remote(src, dst, send_sem, recv_sem, to):
    return pltpu.make_async_remote_copy(src_ref=src, dst_ref=dst, send_sem=send_sem, recv_sem=recv_sem,
                                        device_id=to, device_id_type=MESH)


ANY = pl.BlockSpec(memory_space=pl.ANY)


def _gather_weights(shards):
    n = len(shards)

    def body(*refs):
        in_refs, out_refs = refs[:n], refs[n:2 * n]
        send_sems, recv_sems = refs[2 * n:]
        x, y, c = _place()
        sibling = (x, y, 1 - c)
        chips = _other_chips(x, y)
        first, passed = [], []
        for k in range(n):
            hf = in_refs[k].shape[0] // 2

            def half(px, py, pc, k=k, hf=hf):
                return out_refs[k].at[2 * px + py, pl.ds(pc * hf, hf), :]

            src = in_refs[k].at[pl.ds(c * hf, hf), :]
            for j, chip in enumerate(chips):
                s = 6 * k + j
                first.append(_remote(src, half(x, y, c), send_sems.at[s], recv_sems.at[s], (*chip, c)))
                passed.append((_remote(src, half(*chip, c), send_sems.at[s], recv_sems.at[s], (*chip, c)),
                               _remote(half(*chip, c), half(*chip, c), send_sems.at[s + 3], recv_sems.at[s + 3], sibling),
                               _remote(src, half(*chip, 1 - c), send_sems.at[s + 3], recv_sems.at[s + 3], sibling)))
        for cp in first:
            cp.start()
        for landed, forward, _ in passed:
            landed.wait_recv()
            forward.start()
        for _, _, from_sibling in passed:
            from_sibling.wait_recv()
        for cp in first:
            cp.wait_send()
        for _, forward, _ in passed:
            forward.wait_send()

    return pl.pallas_call(
        body,
        in_specs=[ANY] * n,
        out_specs=[ANY] * n,
        out_shape=[jax.ShapeDtypeStruct((N_CHIPS,) + s.shape, s.dtype) for s in shards],
        scratch_shapes=[pltpu.SemaphoreType.DMA((6 * n,)), pltpu.SemaphoreType.DMA((6 * n,))],
        name="gather_weights",
    )(*shards)


def _pair_exchange(grads):
    n = len(grads)

    def body(*refs):
        g_refs, land_refs = refs[:n], refs[n:2 * n]
        send_sems, recv_sems = refs[2 * n:]
        x, y, c = _place()
        cps = []
        for k in range(n):
            hf = g_refs[k].shape[1] // 2
            src = g_refs[k].at[:, pl.ds((1 - c) * hf, hf), :]
            cps.append(_remote(src, land_refs[k], send_sems.at[k], recv_sems.at[k], (x, y, 1 - c)))
        for cp in cps:
            cp.start()
        for cp in cps:
            cp.wait()

    return pl.pallas_call(
        body,
        in_specs=[ANY] * n,
        out_specs=[ANY] * n,
        out_shape=[jax.ShapeDtypeStruct((N_CHIPS, g.shape[1] // 2, g.shape[2]), F32) for g in grads],
        scratch_shapes=[pltpu.SemaphoreType.DMA((n,)), pltpu.SemaphoreType.DMA((n,))],
        name="pair_exchange",
    )(*grads)


def _pair_add(g, land, c_arr, *, name):
    _, a, b = g.shape
    hf = a // 2

    def body(c_ref, g_ref, l_ref, o_ref):
        o_ref[...] = (g_ref[...] + l_ref[...]).astype(BF16)

    return pl.pallas_call(
        body,
        grid_spec=pltpu.PrefetchScalarGridSpec(
            num_scalar_prefetch=1,
            grid=(N_CHIPS,),
            in_specs=[pl.BlockSpec((None, None, hf, b), lambda s, c_ref: (s, c_ref[0], 0, 0)),
                      pl.BlockSpec((None, hf, b), lambda s, c_ref: (s, 0, 0))],
            out_specs=pl.BlockSpec((None, hf, b), lambda s, c_ref: (s, 0, 0)),
        ),
        out_shape=jax.ShapeDtypeStruct((N_CHIPS, hf, b), BF16),
        compiler_params=pltpu.CompilerParams(dimension_semantics=("parallel",)),
        name=name,
    )(c_arr, g.reshape(N_CHIPS, 2, hf, b), land)


def _chip_exchange(parts):
    n = len(parts)

    def body(*refs):
        p_refs, land_refs = refs[:n], refs[n:2 * n]
        send_sems, recv_sems = refs[2 * n:]
        x, y, c = _place()
        me = 2 * x + y
        chips = _other_chips(x, y)
        sends, recvs = [], []
        for k in range(n):
            for j, (cx, cy) in enumerate(chips):
                s = 3 * k + j
                sends.append(_remote(p_refs[k].at[2 * cx + cy], land_refs[k].at[me], send_sems.at[s], recv_sems.at[s], (cx, cy, c)))
                recvs.append(_remote(p_refs[k].at[me], land_refs[k].at[2 * cx + cy], send_sems.at[s], recv_sems.at[s], (cx, cy, c)))
        for cp in sends:
            cp.start()
        for cp in recvs:
            cp.wait_recv()
        for cp in sends:
            cp.wait_send()

    return pl.pallas_call(
        body,
        in_specs=[ANY] * n,
        out_specs=[ANY] * n,
        out_shape=[jax.ShapeDtypeStruct(p.shape, p.dtype) for p in parts],
        scratch_shapes=[pltpu.SemaphoreType.DMA((3 * n,)), pltpu.SemaphoreType.DMA((3 * n,))],
        name="chip_exchange",
    )(*parts)


def _chip_add(land, part, me_arr, *, name):
    _, r, b = land.shape

    def body(me_ref, p_ref, l1_ref, l2_ref, l3_ref, o_ref):
        o_ref[...] = ((p_ref[...].astype(F32) + l1_ref[...].astype(F32)) + l2_ref[...].astype(F32)) + l3_ref[...].astype(F32)

    tr = r // 2
    other = lambda j: pl.BlockSpec((None, tr, b), lambda i, me_ref: (jnp.bitwise_xor(me_ref[0], j), i, 0))
    return pl.pallas_call(
        body,
        grid_spec=pltpu.PrefetchScalarGridSpec(
            num_scalar_prefetch=1,
            grid=(r // tr,),
            in_specs=[pl.BlockSpec((None, tr, b), lambda i, me_ref: (me_ref[0], i, 0)), other(2), other(1), other(3)],
            out_specs=pl.BlockSpec((tr, b), lambda i, me_ref: (i, 0)),
        ),
        out_shape=jax.ShapeDtypeStruct((r, b), F32),
        compiler_params=pltpu.CompilerParams(dimension_semantics=("parallel",)),
        name=name,
    )(me_arr, part, land, land, land)


def _pair_share(halves):
    n = len(halves)

    def body(*refs):
        h_refs, out_refs = refs[:n], refs[n:2 * n]
        send_sems, recv_sems = refs[2 * n:]
        x, y, c = _place()
        cps = [_remote(h_refs[k], out_refs[k], send_sems.at[k], recv_sems.at[k], (x, y, 1 - c)) for k in range(n)]
        for cp in cps:
            cp.start()
        for cp in cps:
            cp.wait()

    return pl.pallas_call(
        body,
        in_specs=[ANY] * n,
        out_specs=[ANY] * n,
        out_shape=[jax.ShapeDtypeStruct(h.shape, F32) for h in halves],
        scratch_shapes=[pltpu.SemaphoreType.DMA((n,)), pltpu.SemaphoreType.DMA((n,))],
        name="pair_share",
    )(*halves)


def _all_sum_small(part):
    def body(p_ref, o_ref, slots, send_sems, recv_sems):
        x, y, c = _place()
        me = 4 * x + 2 * y + c
        slots[me] = p_ref[...]
        peers = [(x ^ dx, y ^ dy, c ^ dc) for dx in (0, 1) for dy in (0, 1) for dc in (0, 1)][1:]
        sends = [_remote(p_ref, slots.at[me], send_sems.at[k], recv_sems.at[k], peer) for k, peer in enumerate(peers)]
        for cp in sends:
            cp.start()
        for k, (px, py, pc) in enumerate(peers):
            _remote(p_ref, slots.at[4 * px + 2 * py + pc], send_sems.at[k], recv_sems.at[k], (px, py, pc)).wait_recv()
        for cp in sends:
            cp.wait_send()
        acc = slots[0]
        for d in range(1, 8):
            acc = acc + slots[d]
        o_ref[...] = acc

    vmem = pl.BlockSpec(memory_space=pltpu.VMEM)
    return pl.pallas_call(
        body,
        in_specs=[vmem],
        out_specs=vmem,
        out_shape=jax.ShapeDtypeStruct(part.shape, F32),
        scratch_shapes=[pltpu.VMEM((8,) + part.shape, F32), pltpu.SemaphoreType.DMA((7,)), pltpu.SemaphoreType.DMA((7,))],
        name="all_sum_small",
    )(part)


def _deinterleave(a, d):
    b, s, c = a.shape
    return a.reshape(b, s // d, d, c).transpose(0, 2, 1, 3).reshape(b * s // BAND, BAND, c)


def _reinterleave(a, d, b):
    c = a.shape[-1]
    return a.reshape(b, d, SEQ // d, c).transpose(0, 2, 1, 3).reshape(b, SEQ, c)


def _rope_tables():
    half = HEAD_DIM // 2
    inv_freq = ROPE_THETA ** (-jnp.arange(half, dtype=F32) * 2.0 / HEAD_DIM)
    ang = jnp.arange(SEQ, dtype=F32)[:, None] * inv_freq[None, :]
    cos = jnp.tile(jnp.cos(ang), (1, 2 * BAND_HEADS))
    sin = jnp.tile(jnp.concatenate([-jnp.sin(ang), jnp.sin(ang)], axis=1), (1, BAND_HEADS))
    cos_t = jnp.stack([_deinterleave(cos[None], d) for d in DIL_D])
    sin_t = jnp.stack([_deinterleave(sin[None], d) for d in DIL_D])
    return cos_t, sin_t


def _local_step(x, mem, loss_target, g_pre_mix, g_post_mix, g_pre_ffn, g_post_ffn, g_mem, b_gate, w):
    bl = x.shape[0]
    t = bl * SEQ
    chips = range(N_CHIPS)
    w_in_full = jnp.concatenate([w["w_in"][s] for s in chips], axis=1)
    w_mem_kv_full = w["w_mem_kv"].reshape(D_MODEL, 2 * MEM_W)
    w_o_full = w["w_o"].reshape(D_MODEL, D_MODEL)
    w_ffn_out_full = w["w_ffn_out"].reshape(D_FF, D_MODEL)
    half_ff = D_FF // 2

    x2 = x.reshape(t, D_MODEL)
    tgt2 = loss_target.reshape(t, D_MODEL)
    mem2 = mem.reshape(bl * MEM_LEN, D_MODEL)

    h = _norm_fwd(x2, g_pre_mix, name="norm_x")
    proj = _mm([(h, w_in_full)], nt=False, tm=512, tn=2176, out_dtypes=[BF16], name="proj")
    gates = _mm([(h, w["w_gate"], None, "j")], nt=False, tm=512, tn=w["w_gate"].shape[2], out_dtypes=[BF16], name="gates",
                bias=b_gate, epilogue=lambda acc: (_sigmoid(acc),))
    hm = _norm_fwd(mem2, g_mem, name="norm_mem")
    kv_m = _mm([(hm, w_mem_kv_full)], nt=False, tm=512, tn=1024, out_dtypes=[BF16], name="mem_kv")
    proj3 = proj.reshape(bl, SEQ, D_IN)
    kv3 = kv_m.reshape(bl, MEM_LEN, 2 * MEM_W)

    o_a, o_a32 = _sb_fwd(proj3)

    cos_t, sin_t = _rope_tables()
    dil0 = 3 * SB_W

    grp_w = 3 * DIL_W
    qkv_s = jnp.stack([_deinterleave(proj3[:, :, dil0 + g * grp_w:dil0 + (g + 1) * grp_w], d) for g, d in enumerate(DIL_D)])
    ol_s = _band_fwd(qkv_s, cos_t, sin_t)
    ol = jnp.stack([_reinterleave(ol_s[g], d, bl) for g, d in enumerate(DIL_D)]).reshape(3, t, 2 * DIL_W)
    o_b, lse_b = _band_merge(ol)

    o_c = _mem_fwd(proj3, kv3)

    o_a2, o_c2 = o_a.reshape(t, SB_W), o_c.reshape(t, MEM_W)
    y_a, y_b, y_c, merged = _branch_merge_fwd(o_a2, o_b, o_c2, w["w_br_sb"], w["w_br_dil"], w["w_br_mem"], gates)
    mix = _mm([(merged, w_o_full)], nt=False, tm=512, tn=1024, out_dtypes=[F32], name="mix")
    x1, h2 = _mid_fwd(mix, x2, g_post_mix, g_pre_ffn)
    gg, uu, f = _ffn_in_fwd(h2, w["w_ffn_in"])
    f2 = _mm([(f, w_ffn_out_full)], nt=False, tm=512, tn=1024, out_dtypes=[F32], name="ffn_out")

    dy, df2, dg_post_ffn, loss_row = _loss_bwd(f2, x1, g_post_ffn, tgt2)

    dg_ffn, du_ffn = _mm([(df2, w_ffn_out_full)], nt=True, tm=512, tn=half_ff, out_dtypes=[BF16, BF16], name="d_ffn_act",
                         extras=(gg, uu), epilogue=_swiglu_bwd_epilogue)
    gw = {}
    gw["w_ffn_out"] = _mm_tn(f, df2, tm=half_ff, tn=1024, tk=512, name="gw_ffn_out").reshape(N_CHIPS, D_FF // N_CHIPS, D_MODEL)
    gw_ffn_g = _mm_tn(h2, dg_ffn, tm=1024, tn=half_ff, tk=512, name="gw_ffn_gate", out_shards=True)
    gw_ffn_u = _mm_tn(h2, du_ffn, tm=1024, tn=half_ff, tk=512, name="gw_ffn_up", out_shards=True)
    gw["w_ffn_in"] = jnp.concatenate([gw_ffn_g, gw_ffn_u], axis=0)
    dh2 = _mm([(dg_ffn, w["w_ffn_in"], 0, 0), (dg_ffn, w["w_ffn_in"], 1, 1), (du_ffn, w["w_ffn_in"], 0, 2),
               (du_ffn, w["w_ffn_in"], 1, 3)], nt=True, tm=256, tn=1024, out_dtypes=[F32], name="d_h2")
    dx1, dmix, dg_pre_ffn, dg_post_mix = _mid_bwd(dh2, x1, mix, g_pre_ffn, g_post_mix, dy)

    gw["w_o"] = _mm_tn(merged, dmix, tm=1024, tn=1024, tk=512, name="gw_o").reshape(N_CHIPS, D_MODEL // N_CHIPS, D_MODEL)
    dmerged = _mm([(dmix, w_o_full)], nt=True, tm=512, tn=1024, out_dtypes=[F32], name="d_merged")
    dy_a, dy_b, dy_c, dgpre, db_gate = _gate_bwd(dmerged, gates, y_a, y_b, y_c)
    br_cols = D_MODEL // N_CHIPS
    gw["w_br_sb"] = _mm_tn(o_a2, dy_a, tm=512, tn=br_cols, tk=1024, name="gw_br_sb", out_shards=True)
    gw["w_br_dil"] = _mm_tn(o_b, dy_b, tm=256, tn=br_cols, tk=1024, name="gw_br_dil", out_shards=True)
    gw["w_br_mem"] = _mm_tn(o_c2, dy_c, tm=512, tn=br_cols, tk=1024, name="gw_br_mem", out_shards=True)
    gw["w_gate"] = _mm_tn(h, dgpre, tm=1024, tn=w["w_gate"].shape[2], tk=512, name="gw_gate", out_shards=True)
    do_a = _mm([(dy_a, w["w_br_sb"], s, s) for s in chips], nt=True, tm=512, tn=SB_W, out_dtypes=[BF16], name="d_o_a")
    do_b = _mm([(dy_b, w["w_br_dil"], s, s) for s in chips], nt=True, tm=512, tn=DIL_W, out_dtypes=[BF16], name="d_o_b")
    do_c = _mm([(dy_c, w["w_br_mem"], s, s) for s in chips], nt=True, tm=512, tn=MEM_W, out_dtypes=[BF16], name="d_o_c")

    dq_c, dk_m, dv_m = _mem_bwd(proj3, kv3, do_c.reshape(bl, SEQ, MEM_W))
    dkv_m = jnp.concatenate([dk_m, dv_m], axis=-1).reshape(bl * MEM_LEN, 2 * MEM_W).astype(BF16)
    gw["w_mem_kv"] = _mm_tn(hm, dkv_m, tm=1024, tn=1024, tk=512, name="gw_mem_kv").reshape(N_CHIPS, D_MODEL // N_CHIPS, 2 * MEM_W)
    dhm = _mm([(dkv_m, w_mem_kv_full)], nt=True, tm=512, tn=1024, out_dtypes=[F32], name="d_hm")
    dg_mem = _mem_norm_bwd(dhm, mem2, g_mem)

    dcat = _band_delta(do_b, o_b, lse_b).reshape(bl, SEQ, grp_w)
    dcat_s = jnp.stack([_deinterleave(dcat, d) for d in DIL_D])
    dq_r, dk_r, dv_r, dk_first, dv_first = _band_bwd(qkv_s, cos_t, sin_t, dcat_s)
    dqkv_s = _band_combine(dq_r, dk_r, dv_r, dk_first, dv_first, cos_t, sin_t)
    d_dil = [_reinterleave(dqkv_s[g], d, bl) for g, d in enumerate(DIL_D)]

    dq_a, dk_a, dv_a = _sb_bwd(proj3, o_a32, do_a.reshape(bl, SEQ, SB_W))

    dproj = jnp.concatenate([dq_a, dk_a, dv_a] + d_dil + [dq_c], axis=-1).reshape(t, D_IN)
    in_cols = D_IN // N_CHIPS
    dproj_s = jnp.stack([dproj[:, s * in_cols:(s + 1) * in_cols] for s in chips])
    gw["w_in"] = _mm_tn(h, dproj_s, tm=1024, tn=in_cols, tk=512, name="gw_in")
    dh = _mm([(dproj_s, w["w_in"], s, s) for s in chips] + [(dgpre, w["w_gate"], s, s) for s in chips],
             nt=True, tm=256, tn=1024, out_dtypes=[F32], name="d_h")
    grad_x, dg_pre_mix = _first_bwd(dh, x2, g_pre_mix, dx1)
    small = jnp.concatenate([dg_pre_mix, dg_post_mix, dg_pre_ffn, dg_post_ffn, dg_mem, db_gate.reshape(3, D_MODEL)], axis=0)
    return loss_row[0, 0], grad_x.reshape(bl, SEQ, D_MODEL), gw, small


def kernel(x, mem, g_pre_mix, g_post_mix, g_pre_ffn, g_post_ffn, g_mem, w_in, w_mem_kv, w_br_sb, w_br_dil, w_br_mem, w_gate, b_gate, w_o, w_ffn_in, w_ffn_out, loss_target, m_g_pre_mix, m_g_post_mix, m_g_pre_ffn, m_g_post_ffn, m_g_mem, m_w_in, m_w_mem_kv, m_w_br_sb, m_w_br_dil, m_w_br_mem, m_w_gate, m_b_gate, m_w_o, m_w_ffn_in, m_w_ffn_out, v_g_pre_mix, v_g_post_mix, v_g_pre_ffn, v_g_post_ffn, v_g_mem, v_w_in, v_w_mem_kv, v_w_br_sb, v_w_br_dil, v_w_br_mem, v_w_gate, v_b_gate, v_w_o, v_w_ffn_in, v_w_ffn_out):
    w_shards = dict(w_in=w_in[0], w_mem_kv=w_mem_kv[0], w_br_sb=w_br_sb[0], w_br_dil=w_br_dil[0], w_br_mem=w_br_mem[0],
                    w_gate=w_gate[0], w_o=w_o[0], w_ffn_in=w_ffn_in[0], w_ffn_out=w_ffn_out[0])
    m_shards = dict(w_in=m_w_in[0], w_mem_kv=m_w_mem_kv[0], w_br_sb=m_w_br_sb[0], w_br_dil=m_w_br_dil[0], w_br_mem=m_w_br_mem[0],
                    w_gate=m_w_gate[0], w_o=m_w_o[0], w_ffn_in=m_w_ffn_in[0], w_ffn_out=m_w_ffn_out[0])
    v_shards = dict(w_in=v_w_in[0], w_mem_kv=v_w_mem_kv[0], w_br_sb=v_w_br_sb[0], w_br_dil=v_w_br_dil[0], w_br_mem=v_w_br_mem[0],
                    w_gate=v_w_gate[0], w_o=v_w_o[0], w_ffn_in=v_w_ffn_in[0], w_ffn_out=v_w_ffn_out[0])

    names = [name for name, _, _ in PACK]
    c_arr = lax.axis_index("c").astype(jnp.int32).reshape(1)
    me_arr = (2 * lax.axis_index("x") + lax.axis_index("y")).astype(jnp.int32).reshape(1)
    shards_bf = [w_shards[name].astype(BF16) for name in names]
    gathered = _gather_weights(shards_bf)
    w = {name: lax.dynamic_update_slice(g, s[None], (me_arr[0], 0, 0)) for name, g, s in zip(names, gathered, shards_bf)}

    loss_local, grad_x, gw, small = _local_step(x, mem, loss_target, g_pre_mix, g_post_mix, g_pre_ffn, g_post_ffn, g_mem, b_gate, w)
    loss = lax.psum(loss_local, ("x", "y", "c"))

    grads = [gw[name] for name in names]
    lands = _pair_exchange(grads)
    parts = [_pair_add(g, l, c_arr, name="pair_add_" + name) for name, g, l in zip(names, grads, lands)]
    halves = [_chip_add(l, p, me_arr, name="chip_add_" + name) for name, l, p in zip(names, _chip_exchange(parts), parts)]
    theirs = _pair_share(halves)
    small = _all_sum_small(small)

    upd = {}
    for name, mine, other in zip(names, halves, theirs):
        upd[name] = _adamw_halves(w_shards[name], mine, other, m_shards[name], v_shards[name], c_arr, name="adamw_" + name)
    g_shards = {name: u[0] for name, u in upd.items()}

    def small8(gs, b):
        return jnp.concatenate(gs + [b.reshape(3, D_MODEL)], axis=0)

    sw = small8([g_pre_mix, g_post_mix, g_pre_ffn, g_post_ffn, g_mem], b_gate)
    sm = small8([m_g_pre_mix, m_g_post_mix, m_g_pre_ffn, m_g_post_ffn, m_g_mem], m_b_gate)
    sv = small8([v_g_pre_mix, v_g_post_mix, v_g_pre_ffn, v_g_post_ffn, v_g_mem], v_b_gate)
    s_upd = _adamw(sw, small, sm, sv, tm=8, name="adamw_small")

    def small_out(a):
        return [a[0:1], a[1:2], a[2:3], a[3:4], a[4:5]]

    order = ["w_in", "w_mem_kv", "w_br_sb", "w_br_dil", "w_br_mem", "w_gate", "b_gate", "w_o", "w_ffn_in", "w_ffn_out"]

    def leaves(small_arr, big):
        out = small_out(small_arr)
        for name in order:
            out.append(small_arr[5:8].reshape(1, 3 * D_MODEL) if name == "b_gate" else big[name][None])
        return out

    grads_out = leaves(small, g_shards)
    delta_out = leaves(s_upd[0], {n: u[1] for n, u in upd.items()})
    m_out = leaves(s_upd[1], {n: u[2] for n, u in upd.items()})
    v_out = leaves(s_upd[2], {n: u[3] for n, u in upd.items()})
    return (loss, grad_x, *grads_out, *delta_out, *m_out, *v_out)
```

```python
import functools

import jax
import jax.numpy as jnp
from jax import lax
from jax.experimental import pallas as pl
from jax.experimental.pallas import tpu as pltpu

F32 = jnp.float32
BF16 = jnp.bfloat16
MESH = pl.DeviceIdType.MESH

D_MODEL = 1024
SEQ = 2048
HEAD_DIM = 64
SB_W = 512
DIL_W = 256
MEM_W = 512
MEM_LEN = 256
D_IN = 3 * SB_W + 9 * DIL_W + MEM_W
D_FF = 2816
DIL_D = (1, 4, 16)
ROPE_THETA = 10000.0
NORM_EPS = 1e-6
NEG_INF = -1e30
LANES = 128

ADAM_LR = 0.001
ADAM_B1 = 0.9
ADAM_B2 = 0.999
ADAM_EPS = 1e-08
ADAM_WD = 0.01
ADAM_STEP = 10

N_CHIPS = 4
PACK = (
    ("w_in", (1024, 1088), 1),
    ("w_mem_kv", (256, 1024), 0),
    ("w_br_sb", (512, 256), 1),
    ("w_br_dil", (256, 256), 1),
    ("w_br_mem", (512, 256), 1),
    ("w_gate", (1024, 768), 1),
    ("w_o", (256, 1024), 0),
    ("w_ffn_in", (1024, 1408), 1),
    ("w_ffn_out", (704, 1024), 0),
)
PACK_ROWS = sum(a * b for _, (a, b), _ in PACK) // D_MODEL
HALF_ROWS = PACK_ROWS // 2


def _dot(a, b):
    return lax.dot_general(a, b, (((1,), (0,)), ((), ())), preferred_element_type=F32)


def _dot_nt(a, b):
    return lax.dot_general(a, b, (((1,), (1,)), ((), ())), preferred_element_type=F32)


def _dot_tn(a, b):
    return lax.dot_general(a, b, (((0,), (0,)), ((), ())), preferred_element_type=F32)


def _split_dot(x, u):
    hi = x.astype(BF16)
    lo = (x - hi.astype(F32)).astype(BF16)
    return _dot(hi, u) + _dot(lo, u)


def _mm(pairs, *, nt, tm, tn, out_dtypes, name, bias=None, extras=(), epilogue=None):
    pairs = [p if len(p) == 4 else (p[0], p[1], None, None) for p in pairs]
    m = pairs[0][0].shape[-2]
    b0 = pairs[0][1]
    if nt:
        n = b0.shape[-2]
    else:
        n = b0.shape[-1] * (b0.shape[0] if b0.ndim == 3 else 1)
    n_pairs, n_extra, n_out = len(pairs), len(extras), len(out_dtypes)
    tm = min(tm, m)
    assert m % tm == 0 and n % tn == 0
    has_bias = bias is not None

    def body(*refs):
        acc = None
        for i in range(n_pairs):
            a, b = refs[2 * i][...], refs[2 * i + 1][...]
            p = _dot_nt(a, b) if nt else _dot(a, b)
            acc = p if acc is None else acc + p
        pos = 2 * n_pairs
        if has_bias:
            acc = acc + refs[pos][...]
            pos += 1
        ex = [r[...] for r in refs[pos:pos + n_extra]]
        outs = refs[pos + n_extra:]
        vals = (acc,) if epilogue is None else epilogue(acc, *ex)
        for r, v, dt in zip(outs, vals, out_dtypes):
            r[...] = v.astype(dt)

    in_specs, args = [], []
    for a, b, a_col, b_sel in pairs:
        k = b.shape[-1] if nt else b.shape[-2]
        assert a_col is not None or a.shape[1] == k
        if a.ndim == 3:
            in_specs.append(pl.BlockSpec((None, tm, k), lambda j, i, c=a_col: (c, i, 0)))
        else:
            in_specs.append(pl.BlockSpec((tm, k), lambda j, i, c=a_col or 0: (i, c)))
        if b.ndim == 2:
            in_specs.append(pl.BlockSpec((tn, k), lambda j, i: (j, 0)) if nt else pl.BlockSpec((k, tn), lambda j, i: (0, j)))
        elif nt:
            in_specs.append(pl.BlockSpec((None, tn, k), lambda j, i, s=b_sel: (s, j, 0)))
        else:
            assert b_sel == "j" and b.shape[-1] == tn
            in_specs.append(pl.BlockSpec((None, k, tn), lambda j, i: (j, 0, 0)))
        args += [a, b]
    if has_bias:
        in_specs.append(pl.BlockSpec((1, tn), lambda j, i: (0, j)))
        args.append(bias)
    for e in extras:
        in_specs.append(pl.BlockSpec((tm, tn), lambda j, i: (i, j)))
        args.append(e)
    out = pl.pallas_call(
        body,
        grid=(n // tn, m // tm),
        in_specs=in_specs,
        out_specs=[pl.BlockSpec((tm, tn), lambda j, i: (i, j)) for _ in range(n_out)],
        out_shape=[jax.ShapeDtypeStruct((m, n), dt) for dt in out_dtypes],
        compiler_params=pltpu.CompilerParams(dimension_semantics=("parallel", "parallel")),
        name=name,
    )(*args)
    return out[0] if n_out == 1 else out


def _mm_tn(a, b, *, tm, tn, tk, name, out_shards=False):
    k, m = a.shape
    b_shards = b.ndim == 3
    out_shards = out_shards or b_shards
    n = b.shape[0] * b.shape[2] if b_shards else b.shape[1]
    tk = min(tk, k)
    assert m % tm == 0 and n % tn == 0 and k % tk == 0 and (not b_shards or b.shape[2] == tn)

    def body(a_ref, b_ref, o_ref):
        @pl.when(pl.program_id(2) == 0)
        def _():
            o_ref[...] = jnp.zeros_like(o_ref)

        o_ref[...] += _dot_tn(a_ref[...], b_ref[...])

    if b_shards:
        b_spec = pl.BlockSpec((None, tk, tn), lambda i, j, kk: (j, kk, 0))
    else:
        b_spec = pl.BlockSpec((tk, tn), lambda i, j, kk: (kk, j))
    if out_shards:
        out_spec = pl.BlockSpec((None, tm, tn), lambda i, j, kk: (j, i, 0))
        out_shape = jax.ShapeDtypeStruct((n // tn, m, tn), F32)
    else:
        out_spec = pl.BlockSpec((tm, tn), lambda i, j, kk: (i, j))
        out_shape = jax.ShapeDtypeStruct((m, n), F32)
    return pl.pallas_call(
        body,
        grid=(m // tm, n // tn, k // tk),
        in_specs=[pl.BlockSpec((tk, tm), lambda i, j, kk: (kk, i)), b_spec],
        out_specs=out_spec,
        out_shape=out_shape,
        compiler_params=pltpu.CompilerParams(dimension_semantics=("parallel", "parallel", "arbitrary")),
        name=name,
    )(a, b)


def _rowwise(fn, ins, outs, *, tm, name):
    rows = next(a.shape[0] for a, kind in ins if kind == "row")
    tm = min(tm, rows)
    assert rows % tm == 0
    n_in = len(ins)

    def body(*refs):
        vals = fn(*[r[...] for r in refs[:n_in]])
        for (_, dt, kind), r, v in zip(outs, refs[n_in:], vals):
            if kind == "row":
                r[...] = v.astype(dt)
            else:
                @pl.when(pl.program_id(0) == 0)
                def _(r=r):
                    r[...] = jnp.zeros_like(r)

                r[...] += v

    in_specs = [pl.BlockSpec((tm, a.shape[1]), lambda i: (i, 0)) if kind == "row" else pl.BlockSpec(a.shape, lambda i: (0, 0))
                for a, kind in ins]
    out_specs = [pl.BlockSpec((tm, c), lambda i: (i, 0)) if kind == "row" else pl.BlockSpec((1, c), lambda i: (0, 0))
                 for c, _, kind in outs]
    out_shape = [jax.ShapeDtypeStruct((rows if kind == "row" else 1, c), dt) for c, dt, kind in outs]
    has_acc = any(kind == "acc" for _, _, kind in outs)
    return pl.pallas_call(
        body,
        grid=(rows // tm,),
        in_specs=in_specs,
        out_specs=out_specs,
        out_shape=out_shape,
        compiler_params=pltpu.CompilerParams(dimension_semantics=("arbitrary" if has_acc else "parallel",)),
        name=name,
    )(*[a for a, _ in ins])


def _rstd(x):
    return lax.rsqrt(jnp.mean(x * x, axis=-1, keepdims=True) + NORM_EPS)


def _norm_bwd(dout, xin, g):
    r = _rstd(xin)
    n = xin * r
    dn = dout * g
    dg = jnp.sum(dout * n, axis=0, keepdims=True)
    dx = r * (dn - n * jnp.mean(dn * n, axis=-1, keepdims=True))
    return dx, dg


def _sigmoid(x):
    return 1.0 / (1.0 + jnp.exp(-x))


def _norm_fwd(x, g, *, name):
    def fn(x, g):
        return ((x * _rstd(x)) * g,)

    return _rowwise(fn, [(x, "row"), (g, "vec")], [(D_MODEL, BF16, "row")], tm=512, name=name)[0]


def _mid_fwd(mix, x, g_post_mix, g_pre_ffn):
    def fn(mix, x, g2, g3):
        x1 = x + (mix * _rstd(mix)) * g2
        return x1, (x1 * _rstd(x1)) * g3

    return _rowwise(fn, [(mix, "row"), (x, "row"), (g_post_mix, "vec"), (g_pre_ffn, "vec")],
                    [(D_MODEL, F32, "row"), (D_MODEL, BF16, "row")], tm=512, name="mid_fwd")


def _loss_bwd(f2, x1, g_post_ffn, tgt):
    def fn(f2, x1, g4, tgt):
        r = _rstd(f2)
        n = f2 * r
        err = x1 + n * g4 - tgt
        loss = 0.5 * jnp.sum(jnp.mean(err * err, axis=-1, keepdims=True), axis=0, keepdims=True)
        dy = err * (1.0 / D_MODEL)
        dn = dy * g4
        dg4 = jnp.sum(dy * n, axis=0, keepdims=True)
        df2 = r * (dn - n * jnp.mean(dn * n, axis=-1, keepdims=True))
        return dy, df2, dg4, jnp.broadcast_to(loss, (1, LANES))

    return _rowwise(fn, [(f2, "row"), (x1, "row"), (g_post_ffn, "vec"), (tgt, "row")],
                    [(D_MODEL, F32, "row"), (D_MODEL, BF16, "row"), (D_MODEL, F32, "acc"), (LANES, F32, "acc")],
                    tm=512, name="loss_bwd")


def _mid_bwd(dh2, x1, mix, g_pre_ffn, g_post_mix, dy):
    def fn(dh2, x1, mix, g3, g2, dy):
        d3, dg3 = _norm_bwd(dh2, x1, g3)
        dx1 = dy + d3
        dmix, dg2 = _norm_bwd(dx1, mix, g2)
        return dx1, dmix, dg3, dg2

    return _rowwise(fn, [(dh2, "row"), (x1, "row"), (mix, "row"), (g_pre_ffn, "vec"), (g_post_mix, "vec"), (dy, "row")],
                    [(D_MODEL, F32, "row"), (D_MODEL, BF16, "row"), (D_MODEL, F32, "acc"), (D_MODEL, F32, "acc")],
                    tm=256, name="mid_bwd")


def _first_bwd(dh, x, g_pre_mix, dx1):
    def fn(dh, x, g1, dx1):
        d1, dg1 = _norm_bwd(dh, x, g1)
        return dx1 + d1, dg1

    return _rowwise(fn, [(dh, "row"), (x, "row"), (g_pre_mix, "vec"), (dx1, "row")],
                    [(D_MODEL, F32, "row"), (D_MODEL, F32, "acc")], tm=512, name="first_bwd")


def _mem_norm_bwd(dhm, mem, g_mem):
    def fn(dhm, mem, g):
        return (jnp.sum(dhm * (mem * _rstd(mem)), axis=0, keepdims=True),)

    return _rowwise(fn, [(dhm, "row"), (mem, "row"), (g_mem, "vec")], [(D_MODEL, F32, "acc")], tm=512, name="mem_norm_bwd")[0]


def _gate_bwd(dmerged, gates, ya, yb, yc):
    def fn(dm, gt, ya, yb, yc):
        gt = gt.astype(F32)
        outs, dgp = [], []
        for i, y in enumerate((ya, yb, yc)):
            gi = gt[:, i * D_MODEL:(i + 1) * D_MODEL]
            outs.append(dm * gi)
            dgp.append(dm * y.astype(F32) * gi * (1.0 - gi))
        dgpre = jnp.concatenate(dgp, axis=1)
        return outs[0], outs[1], outs[2], dgpre, jnp.sum(dgpre, axis=0, keepdims=True)

    return _rowwise(fn, [(dmerged, "row"), (gates, "row"), (ya, "row"), (yb, "row"), (yc, "row")],
                    [(D_MODEL, BF16, "row")] * 3 + [(3 * D_MODEL, BF16, "row"), (3 * D_MODEL, F32, "acc")],
                    tm=256, name="gate_bwd")


def _adamw_math(w, g, m, v):
    m = ADAM_B1 * m + (1.0 - ADAM_B1) * g
    v = ADAM_B2 * v + (1.0 - ADAM_B2) * (g * g)
    m_hat = m / (1.0 - ADAM_B1 ** ADAM_STEP)
    v_hat = v / (1.0 - ADAM_B2 ** ADAM_STEP)
    delta = -ADAM_LR * (m_hat / (jnp.sqrt(v_hat) + ADAM_EPS) + ADAM_WD * w)
    return delta, m, v


def _adamw(w, g, m, v, *, tm, name):
    c = w.shape[1]
    return _rowwise(_adamw_math, [(w, "row"), (g, "row"), (m, "row"), (v, "row")], [(c, F32, "row")] * 3, tm=tm, name=name)


def _adamw_halves(w, g_mine, g_theirs, m, v, c_arr, *, name):
    a, b = w.shape
    hf = a // 2
    tr = hf // 4

    def body(c_ref, w_ref, gm_ref, gt_ref, m_ref, v_ref, g_out, d_out, m_out, v_out):
        g = jnp.where(pl.program_id(0) == c_ref[0], gm_ref[...], gt_ref[...])
        d, m_new, v_new = _adamw_math(w_ref[...], g, m_ref[...], v_ref[...])
        g_out[...] = g
        d_out[...] = d
        m_out[...] = m_new
        v_out[...] = v_new

    full = pl.BlockSpec((tr, b), lambda hh, i, c_ref: (hh * (hf // tr) + i, 0))
    half = pl.BlockSpec((tr, b), lambda hh, i, c_ref: (i, 0))
    return pl.pallas_call(
        body,
        grid_spec=pltpu.PrefetchScalarGridSpec(
            num_scalar_prefetch=1,
            grid=(2, hf // tr),
            in_specs=[full, half, half, full, full],
            out_specs=[full] * 4,
        ),
        out_shape=[jax.ShapeDtypeStruct((a, b), F32)] * 4,
        compiler_params=pltpu.CompilerParams(dimension_semantics=("parallel", "parallel")),
        name=name,
    )(c_arr, w, g_mine, g_theirs, m, v)


def _ffn_in_fwd(h2, w_ffn):
    m, tm, tn = h2.shape[0], 512, w_ffn.shape[2]
    assert 2 * tn == D_FF

    def body(h_ref, wg_ref, wu_ref, g_ref, u_ref, f_ref):
        h = h_ref[...]
        g = _dot(h, wg_ref[...])
        u = _dot(h, wu_ref[...])
        g_ref[...] = g.astype(BF16)
        u_ref[...] = u.astype(BF16)
        f_ref[...] = (g * _sigmoid(g) * u).astype(BF16)

    o_spec = pl.BlockSpec((tm, tn), lambda j, i: (i, j))
    return pl.pallas_call(
        body,
        grid=(D_FF // tn, m // tm),
        in_specs=[pl.BlockSpec((tm, D_MODEL), lambda j, i: (i, 0)),
                  pl.BlockSpec((None, D_MODEL, tn), lambda j, i: (j, 0, 0)),
                  pl.BlockSpec((None, D_MODEL, tn), lambda j, i: (j + 2, 0, 0))],
        out_specs=[o_spec, o_spec, o_spec],
        out_shape=[jax.ShapeDtypeStruct((m, D_FF), BF16)] * 3,
        compiler_params=pltpu.CompilerParams(dimension_semantics=("parallel", "parallel")),
        name="ffn_in_fwd",
    )(h2, w_ffn, w_ffn)


def _swiglu_bwd_epilogue(df, g, u):
    g = g.astype(F32)
    u = u.astype(F32)
    sg = _sigmoid(g)
    return df * u * (sg * (1.0 + g * (1.0 - sg))), df * (g * sg)


def _branch_merge_fwd(o_a, o_b, o_c, w_sb, w_dil, w_mem, gates):
    m, tm = o_a.shape[0], 256

    def body(oa_ref, ob_ref, oc_ref, wa_ref, wb_ref, wc_ref, gt_ref, ya_ref, yb_ref, yc_ref, mg_ref):
        def project(o_ref, w_ref):
            o = o_ref[...]
            return jnp.concatenate([_dot(o, w_ref[s]) for s in range(N_CHIPS)], axis=1)

        ya = project(oa_ref, wa_ref)
        yb = project(ob_ref, wb_ref)
        yc = project(oc_ref, wc_ref)
        gt = gt_ref[...].astype(F32)
        ya_ref[...] = ya.astype(BF16)
        yb_ref[...] = yb.astype(BF16)
        yc_ref[...] = yc.astype(BF16)
        mg_ref[...] = (gt[:, :D_MODEL] * ya + gt[:, D_MODEL:2 * D_MODEL] * yb + gt[:, 2 * D_MODEL:] * yc).astype(BF16)

    row = lambda c: pl.BlockSpec((tm, c), lambda i: (i, 0))
    full = lambda a: pl.BlockSpec(a.shape, lambda i: (0, 0, 0))
    return pl.pallas_call(
        body,
        grid=(m // tm,),
        in_specs=[row(SB_W), row(DIL_W), row(MEM_W), full(w_sb), full(w_dil), full(w_mem), row(3 * D_MODEL)],
        out_specs=[row(D_MODEL)] * 4,
        out_shape=[jax.ShapeDtypeStruct((m, D_MODEL), BF16)] * 4,
        compiler_params=pltpu.CompilerParams(dimension_semantics=("parallel",)),
        name="branch_merge_fwd",
    )(o_a, o_b, o_c, w_sb, w_dil, w_mem, gates)


SB_T = 256
SB_SCALE = HEAD_DIM ** -0.5


def _sb_masks():
    row = lax.broadcasted_iota(jnp.int32, (SB_T, SB_T), 0)
    col = lax.broadcasted_iota(jnp.int32, (SB_T, SB_T), 1)
    lane = lax.broadcasted_iota(jnp.int32, (1, LANES), 1)
    return row, col, lane


def _sb_logs(z):
    e = jnp.exp(-jnp.abs(z))
    sp = jnp.log1p(e)
    return jnp.minimum(z, 0.0) - sp, jnp.minimum(-z, 0.0) - sp, e


def _sb_specs(n_heads_pairs, col0):
    q = pl.BlockSpec((None, SB_T, LANES), lambda b, p, i: (b, i, col0 + p))
    k = pl.BlockSpec((None, SEQ, LANES), lambda b, p, i: (b, 0, col0 + n_heads_pairs + p))
    v = pl.BlockSpec((None, SEQ, LANES), lambda b, p, i: (b, 0, col0 + 2 * n_heads_pairs + p))
    return q, k, v


def _grid_step(n_pairs, nq):
    return (pl.program_id(0) * n_pairs + pl.program_id(1)) * nq + pl.program_id(2)


def _sb_fwd(proj3, late_shards):
    bl = proj3.shape[0]
    n_pairs = SB_W // LANES
    nq = SEQ // SB_T
    n_late = len(late_shards)
    n_steps = bl * n_pairs * nq

    def body(q_ref, k_ref, v_ref, *rest):
        late_in, (o_ref, o32_ref), late_out = rest[:n_late], rest[n_late:n_late + 2], rest[n_late + 2:2 * n_late + 2]
        step = _grid_step(n_pairs, nq)
        if n_late:
            send, forward, finish = _gather_phases(late_in, late_out, *rest[2 * n_late + 2:])
            pl.when(step == 0)(send)
            pl.when(step == n_steps // 2)(forward)
        i = pl.program_id(2)
        row, col, lane = _sb_masks()
        causal = col < row
        u_excl = (row > col).astype(BF16)
        q = q_ref[...]
        heads = []
        for h in range(2):
            mh = (lane // HEAD_DIM) == h
            heads.append((mh, jnp.where(mh, q, jnp.zeros_like(q)) * SB_SCALE))

        def blocks(js, carries, acc, diag):
            ks = [k_ref[pl.ds(pl.multiple_of(j * SB_T, SB_T), SB_T), :] for j in js]
            vs = [v_ref[pl.ds(pl.multiple_of(j * SB_T, SB_T), SB_T), :] for j in js]
            chains = [(b, h) for b in range(len(js)) for h in range(2)]
            z = {c: _dot_nt(heads[c[1]][1], ks[c[0]]) for c in chains}
            lb, lk = {}, {}
            for c in chains:
                lb[c], lk[c], _ = _sb_logs(z[c])
                if diag:
                    lk[c] = jnp.where(causal, lk[c], 0.0)
            r = {c: _split_dot(lk[c], u_excl) for c in chains}
            carries = list(carries)
            w = {}
            for b, h in chains:
                w_c = jnp.exp(lb[b, h] + r[b, h] + carries[h])
                w[b, h] = (jnp.where(causal, w_c, 0.0) if diag else w_c).astype(BF16)
                carries[h] = carries[h] + (r[b, h][:, :1] + lk[b, h][:, :1])
            for b, h in chains:
                acc = acc + _dot(w[b, h], jnp.where(heads[h][0], vs[b], jnp.zeros_like(vs[b])))
            return tuple(carries), acc

        zero = jnp.zeros((SB_T, 1), F32)
        carries, acc = blocks([i], (zero, zero), jnp.zeros((SB_T, LANES), F32), True)
        carries, acc = lax.fori_loop(0, i // 2, lambda jj, c: blocks([i - 1 - 2 * jj, i - 2 - 2 * jj], c[0], c[1], False),
                                     (carries, acc))
        carries, acc = lax.fori_loop(0, i % 2, lambda jj, c: blocks([0], c[0], c[1], False), (carries, acc))
        o_ref[...] = acc.astype(BF16)
        o32_ref[...] = acc
        if n_late:
            pl.when(step == n_steps - 1)(finish)

    q_spec, k_spec, v_spec = _sb_specs(n_pairs, 0)
    blk = pl.BlockSpec((None, SB_T, LANES), lambda b, p, i: (b, i, p))
    out = pl.pallas_call(
        body,
        grid=(bl, n_pairs, nq),
        in_specs=[q_spec, k_spec, v_spec] + [ANY] * n_late,
        out_specs=[blk, blk] + [ANY] * n_late,
        out_shape=[jax.ShapeDtypeStruct((bl, SEQ, SB_W), BF16), jax.ShapeDtypeStruct((bl, SEQ, SB_W), F32)]
        + _gather_out_shapes(late_shards),
        scratch_shapes=_gather_sems(n_late) if n_late else [],
        compiler_params=pltpu.CompilerParams(dimension_semantics=("arbitrary", "arbitrary", "arbitrary")),
        name="sb_fwd",
    )(proj3, proj3, proj3, *late_shards)
    return out[0], out[1], out[2:]


def _sb_bwd(proj3, o_a, do_a, parts):
    bl = proj3.shape[0]
    n_pairs = SB_W // LANES
    nq = SEQ // SB_T
    n_parts = len(parts)
    n_steps = bl * n_pairs * nq

    def body(q_ref, k_ref, v_ref, o_ref, do_ref, *rest):
        p_refs, (dq_ref, dk_ref, dv_ref), land_refs = rest[:n_parts], rest[n_parts:n_parts + 3], rest[n_parts + 3:2 * n_parts + 3]
        dk_acc, dv_acc = rest[2 * n_parts + 3:2 * n_parts + 5]
        step = _grid_step(n_pairs, nq)
        if n_parts:
            send, finish = _chip_exchange_phases(p_refs, land_refs, *rest[2 * n_parts + 5:])
            pl.when(step == 0)(send)
        i = pl.program_id(2)

        @pl.when(i == 0)
        def _():
            dk_acc[...] = jnp.zeros_like(dk_acc)
            dv_acc[...] = jnp.zeros_like(dv_acc)

        row, col, lane = _sb_masks()
        causal = col < row
        u_excl = (row > col).astype(BF16)
        u_incl = (row >= col).astype(BF16)
        q = q_ref[...]
        do = do_ref[...]
        prod = do.astype(F32) * o_ref[...]
        heads = []
        for h in range(2):
            mh = (lane // HEAD_DIM) == h
            d_tot = jnp.sum(jnp.where(mh, prod, 0.0), axis=1, keepdims=True)
            heads.append((mh, jnp.where(mh, q, jnp.zeros_like(q)) * SB_SCALE, jnp.where(mh, do, jnp.zeros_like(do)), d_tot))

        def blocks(js, carries, c_das, dq, diag):
            starts = [pl.multiple_of(j * SB_T, SB_T) for j in js]
            ks = [k_ref[pl.ds(s, SB_T), :] for s in starts]
            vs = [v_ref[pl.ds(s, SB_T), :] for s in starts]
            chains = [(b, h) for b in range(len(js)) for h in range(2)]
            z = {c: _dot_nt(heads[c[1]][1], ks[c[0]]) for c in chains}
            dw = {c: _dot_nt(heads[c[1]][2], vs[c[0]]) for c in chains}
            lb, lk, e = {}, {}, {}
            for c in chains:
                lb[c], lk[c], e[c] = _sb_logs(z[c])
                if diag:
                    lk[c] = jnp.where(causal, lk[c], 0.0)
            r = {c: _split_dot(lk[c], u_excl) for c in chains}
            carries, c_das = list(carries), list(c_das)
            wb, da = {}, {}
            for b, h in chains:
                w_c = jnp.exp(lb[b, h] + r[b, h] + carries[h])
                wb[b, h] = (jnp.where(causal, w_c, 0.0) if diag else w_c).astype(BF16)
                da[b, h] = dw[b, h] * wb[b, h].astype(F32)
                carries[h] = carries[h] + (r[b, h][:, :1] + lk[b, h][:, :1])
            sfx = {c: _split_dot(da[c], u_incl) for c in chains}
            dz = {}
            for b, h in chains:
                dlk = heads[h][3] - c_das[h] - sfx[b, h]
                if diag:
                    dlk = jnp.where(causal, dlk, 0.0)
                c_das[h] = c_das[h] + sfx[b, h][:, :1]
                inv = 1.0 / (1.0 + e[b, h])
                pos = z[b, h] >= 0.0
                beta = jnp.where(pos, inv, e[b, h] * inv)
                one_m_beta = jnp.where(pos, e[b, h] * inv, inv)
                dz[b, h] = (da[b, h] * one_m_beta - dlk * beta).astype(BF16)
            for b, h in chains:
                dq = dq + _dot(dz[b, h], jnp.where(heads[h][0], ks[b], jnp.zeros_like(ks[b])))
            for b in range(len(js)):
                dk_acc[pl.ds(starts[b], SB_T), :] += _dot_tn(dz[b, 0], heads[0][1]) + _dot_tn(dz[b, 1], heads[1][1])
                dv_acc[pl.ds(starts[b], SB_T), :] += _dot_tn(wb[b, 0], heads[0][2]) + _dot_tn(wb[b, 1], heads[1][2])
            return tuple(carries), tuple(c_das), dq

        zero = jnp.zeros((SB_T, 1), F32)
        state = blocks([i], (zero, zero), (zero, zero), jnp.zeros((SB_T, LANES), F32), True)
        state = lax.fori_loop(0, i // 2, lambda jj, c: blocks([i - 1 - 2 * jj, i - 2 - 2 * jj], c[0], c[1], c[2], False), state)
        state = lax.fori_loop(0, i % 2, lambda jj, c: blocks([0], c[0], c[1], c[2], False), state)
        dq_ref[...] = (state[2] * SB_SCALE).astype(BF16)

        @pl.when(i == nq - 1)
        def _():
            dk_ref[...] = dk_acc[...].astype(BF16)
            dv_ref[...] = dv_acc[...].astype(BF16)

        if n_parts:
            pl.when(step == n_steps - 1)(finish)

    q_spec, k_spec, v_spec = _sb_specs(n_pairs, 0)
    blk = pl.BlockSpec((None, SB_T, LANES), lambda b, p, i: (b, i, p))
    seq = pl.BlockSpec((None, SEQ, LANES), lambda b, p, i: (b, 0, p))
    shape = jax.ShapeDtypeStruct((bl, SEQ, SB_W), BF16)
    out = pl.pallas_call(
        body,
        grid=(bl, n_pairs, nq),
        in_specs=[q_spec, k_spec, v_spec, blk, blk] + [ANY] * n_parts,
        out_specs=[blk, seq, seq] + [ANY] * n_parts,
        out_shape=[shape, shape, shape] + [jax.ShapeDtypeStruct(p.shape, p.dtype) for p in parts],
        scratch_shapes=[pltpu.VMEM((SEQ, LANES), F32), pltpu.VMEM((SEQ, LANES), F32)]
        + (_chip_exchange_sems(n_parts) if n_parts else []),
        compiler_params=pltpu.CompilerParams(dimension_semantics=("arbitrary", "arbitrary", "arbitrary")),
        name="sb_bwd",
    )(proj3, proj3, proj3, o_a, do_a, *parts)
    return out[0], out[1], out[2], out[3:]


BAND = 128


BAND_CH = 4
BAND_HEADS = DIL_W // HEAD_DIM


def _swap_half(x):
    n = x.shape[-1]
    lane = lax.broadcasted_iota(jnp.int32, (1, n), 1)
    return jnp.where((lane % HEAD_DIM) < HEAD_DIM // 2, pltpu.roll(x, n - HEAD_DIM // 2, 1), pltpu.roll(x, HEAD_DIM // 2, 1))


def _rope(x, cos, sin_signed):
    x = x.astype(F32)
    return x * cos + _swap_half(x) * sin_signed


def _band_valid(g, blk):
    nb = jnp.where(g == 0, 16, jnp.where(g == 1, 4, 1))
    first_key = jnp.where(lax.rem(blk, nb) != 0, 0, BAND)
    qi = lax.broadcasted_iota(jnp.int32, (BAND, 2 * BAND), 0) + BAND
    kj = lax.broadcasted_iota(jnp.int32, (BAND, 2 * BAND), 1)
    dist = qi - kj
    return (dist >= 0) & (dist <= BAND) & (kj >= first_key)


def _band_specs():
    last_before = lambda i: jnp.maximum(i * BAND_CH - 1, 0)
    cur = lambda col: pl.BlockSpec((None, BAND_CH, BAND, DIL_W), lambda g, i: (g, i, 0, col))
    prev = lambda col: pl.BlockSpec((None, None, BAND, DIL_W), lambda g, i: (g, last_before(i), 0, col))
    tab = pl.BlockSpec((None, BAND_CH, BAND, DIL_W), lambda g, i: (g, lax.rem(i, 16 // BAND_CH), 0, 0))
    tab_prev = pl.BlockSpec((None, None, BAND, DIL_W), lambda g, i: (g, lax.rem(last_before(i), 16), 0, 0))
    return cur, prev, tab, tab_prev


def _band_load(q_ref, k_ref, kp_ref, v_ref, vp_ref, c_ref, s_ref, cp_ref, sp_ref):
    qs = [(_rope(q_ref[b], c_ref[b], s_ref[b]) * SB_SCALE).astype(BF16) for b in range(BAND_CH)]
    ks = [_rope(kp_ref[...], cp_ref[...], sp_ref[...]).astype(BF16)]
    ks += [_rope(k_ref[b], c_ref[b], s_ref[b]).astype(BF16) for b in range(BAND_CH)]
    vs = [vp_ref[...]] + [v_ref[b] for b in range(BAND_CH)]
    k2 = [jnp.concatenate([ks[b], ks[b + 1]], axis=0) for b in range(BAND_CH)]
    v2 = [jnp.concatenate([vs[b], vs[b + 1]], axis=0) for b in range(BAND_CH)]
    return qs, k2, v2


def _band_fwd(qkv_s, cos_t, sin_t):
    def body(q_ref, k_ref, kp_ref, v_ref, vp_ref, c_ref, s_ref, cp_ref, sp_ref, ol_ref):
        g, i = pl.program_id(0), pl.program_id(1)
        qs, k2, v2 = _band_load(q_ref, k_ref, kp_ref, v_ref, vp_ref, c_ref, s_ref, cp_ref, sp_ref)
        lane = lax.broadcasted_iota(jnp.int32, (1, DIL_W), 1)
        for b in range(BAND_CH):
            valid = _band_valid(g, i * BAND_CH + b)
            hs = range(BAND_HEADS)
            mh = [(lane // HEAD_DIM) == h for h in hs]
            s = [jnp.where(valid, _dot_nt(jnp.where(mh[h], qs[b], jnp.zeros_like(qs[b])), k2[b]), NEG_INF) for h in hs]
            m = [jnp.max(s[h], axis=1, keepdims=True) for h in hs]
            p = [jnp.exp(s[h] - m[h]) for h in hs]
            den = [jnp.sum(p[h], axis=1, keepdims=True) for h in hs]
            pv = [_dot(p[h].astype(BF16), jnp.where(mh[h], v2[b], jnp.zeros_like(v2[b]))) for h in hs]
            o = jnp.zeros((BAND, DIL_W), F32)
            lse = jnp.zeros((BAND, DIL_W), F32)
            for h in hs:
                o = o + pv[h] * (1.0 / den[h])
                lse = jnp.where(mh[h], m[h] + jnp.log(den[h]), lse)
            ol_ref[b, :, :DIL_W] = o
            ol_ref[b, :, DIL_W:] = lse

    cur, prev, tab, tab_prev = _band_specs()
    n_blk = qkv_s.shape[1]
    return pl.pallas_call(
        body,
        grid=(3, n_blk // BAND_CH),
        in_specs=[cur(0), cur(1), prev(1), cur(2), prev(2), tab, tab, tab_prev, tab_prev],
        out_specs=pl.BlockSpec((None, BAND_CH, BAND, 2 * DIL_W), lambda g, i: (g, i, 0, 0)),
        out_shape=jax.ShapeDtypeStruct((3, n_blk, BAND, 2 * DIL_W), F32),
        compiler_params=pltpu.CompilerParams(dimension_semantics=("parallel", "parallel")),
        name="band_fwd",
    )(qkv_s, qkv_s, qkv_s, qkv_s, qkv_s, cos_t, sin_t, cos_t, sin_t)


def _band_bwd(qkv_s, cos_t, sin_t, dcat_s):
    def body(q_ref, k_ref, kp_ref, v_ref, vp_ref, c_ref, s_ref, cp_ref, sp_ref, do_ref, lse_ref, dl_ref,
             dq_ref, dk_ref, dv_ref, dkf_ref, dvf_ref):
        g, i = pl.program_id(0), pl.program_id(1)
        qs, k2, v2 = _band_load(q_ref, k_ref, kp_ref, v_ref, vp_ref, c_ref, s_ref, cp_ref, sp_ref)
        lane = lax.broadcasted_iota(jnp.int32, (1, DIL_W), 1)
        dks, dvs = [], []
        for b in range(BAND_CH):
            valid = _band_valid(g, i * BAND_CH + b)
            do, lse, dl = do_ref[b].astype(BF16), lse_ref[b], dl_ref[b]
            hs = range(BAND_HEADS)
            mh = [(lane // HEAD_DIM) == h for h in hs]
            qh = [jnp.where(mh[h], qs[b], jnp.zeros_like(qs[b])) for h in hs]
            doh = [jnp.where(mh[h], do, jnp.zeros_like(do)) for h in hs]
            s = [_dot_nt(qh[h], k2[b]) for h in hs]
            dp = [_dot_nt(doh[h], v2[b]) for h in hs]
            p = [jnp.where(valid, jnp.exp(s[h] - lse[:, h * HEAD_DIM:h * HEAD_DIM + 1]), 0.0) for h in hs]
            ds = [(p[h] * (dp[h] - dl[:, h * HEAD_DIM:h * HEAD_DIM + 1])).astype(BF16) for h in hs]
            pb = [p[h].astype(BF16) for h in hs]
            dq = sum(_dot(ds[h], jnp.where(mh[h], k2[b], jnp.zeros_like(k2[b]))) for h in hs)
            dk2 = sum(_dot_tn(ds[h], qh[h]) for h in hs)
            dv2 = sum(_dot_tn(pb[h], doh[h]) for h in hs)
            dq_ref[b] = dq * SB_SCALE
            dks.append(dk2)
            dvs.append(dv2)
        dkf_ref[...] = dks[0][:BAND]
        dvf_ref[...] = dvs[0][:BAND]
        for b in range(BAND_CH):
            last = b == BAND_CH - 1
            dk_ref[b] = dks[b][BAND:] if last else dks[b][BAND:] + dks[b + 1][:BAND]
            dv_ref[b] = dvs[b][BAND:] if last else dvs[b][BAND:] + dvs[b + 1][:BAND]

    cur, prev, tab, tab_prev = _band_specs()
    first = pl.BlockSpec((None, None, BAND, DIL_W), lambda g, i: (g, i, 0, 0))
    n_blk = qkv_s.shape[1]
    n_chunks = n_blk // BAND_CH
    shape = jax.ShapeDtypeStruct((3, n_blk, BAND, DIL_W), F32)
    shape_first = jax.ShapeDtypeStruct((3, n_chunks, BAND, DIL_W), F32)
    return pl.pallas_call(
        body,
        grid=(3, n_chunks),
        in_specs=[cur(0), cur(1), prev(1), cur(2), prev(2), tab, tab, tab_prev, tab_prev, cur(0), cur(1), cur(2)],
        out_specs=[cur(0), cur(0), cur(0), first, first],
        out_shape=[shape, shape, shape, shape_first, shape_first],
        compiler_params=pltpu.CompilerParams(dimension_semantics=("parallel", "parallel")),
        name="band_bwd",
    )(qkv_s, qkv_s, qkv_s, qkv_s, qkv_s, cos_t, sin_t, cos_t, sin_t, dcat_s, dcat_s, dcat_s)


def _band_combine(dq, dk, dv, dk_first, dv_first, cos_t, sin_t):
    n_chunks = dk_first.shape[1]

    def body(dq_ref, dk_ref, dkn_ref, dv_ref, dvn_ref, c_ref, s_ref, out_ref):
        nxt = (pl.program_id(1) < n_chunks - 1).astype(F32)
        for b in range(BAND_CH):
            cos, sin = c_ref[b], s_ref[b]
            dq_b, dk_b, dv_b = dq_ref[b], dk_ref[b], dv_ref[b]
            if b == BAND_CH - 1:
                dk_b = dk_b + nxt * dkn_ref[...]
                dv_b = dv_b + nxt * dvn_ref[...]
            out_ref[b, :, :DIL_W] = (dq_b * cos - _swap_half(dq_b) * sin).astype(BF16)
            out_ref[b, :, DIL_W:2 * DIL_W] = (dk_b * cos - _swap_half(dk_b) * sin).astype(BF16)
            out_ref[b, :, 2 * DIL_W:] = dv_b.astype(BF16)

    cur, _, tab, _ = _band_specs()
    nxt = pl.BlockSpec((None, None, BAND, DIL_W), lambda g, i: (g, jnp.minimum(i + 1, n_chunks - 1), 0, 0))
    return pl.pallas_call(
        body,
        grid=(3, n_chunks),
        in_specs=[cur(0), cur(0), nxt, cur(0), nxt, tab, tab],
        out_specs=pl.BlockSpec((None, BAND_CH, BAND, 3 * DIL_W), lambda g, i: (g, i, 0, 0)),
        out_shape=jax.ShapeDtypeStruct(dq.shape[:3] + (3 * DIL_W,), BF16),
        compiler_params=pltpu.CompilerParams(dimension_semantics=("parallel", "parallel")),
        name="band_combine",
    )(dq, dk, dk_first, dv, dv_first, cos_t, sin_t)


def _band_merge(ol):
    t, tm = ol.shape[1], 512

    def body(o_ref, l_ref, ob_ref, lse_ref):
        l0, l1, l2 = l_ref[0], l_ref[1], l_ref[2]
        m = jnp.maximum(jnp.maximum(l0, l1), l2)
        lse = m + jnp.log(jnp.exp(l0 - m) + jnp.exp(l1 - m) + jnp.exp(l2 - m))
        ob_ref[...] = (jnp.exp(l0 - lse) * o_ref[0] + jnp.exp(l1 - lse) * o_ref[1] + jnp.exp(l2 - lse) * o_ref[2]).astype(BF16)
        lse_ref[...] = lse

    spec = pl.BlockSpec((tm, DIL_W), lambda i: (i, 0))
    return pl.pallas_call(
        body,
        grid=(t // tm,),
        in_specs=[pl.BlockSpec((3, tm, DIL_W), lambda i: (0, i, 0)), pl.BlockSpec((3, tm, DIL_W), lambda i: (0, i, 1))],
        out_specs=[spec, spec],
        out_shape=[jax.ShapeDtypeStruct((t, DIL_W), BF16), jax.ShapeDtypeStruct((t, DIL_W), F32)],
        compiler_params=pltpu.CompilerParams(dimension_semantics=("parallel",)),
        name="band_merge",
    )(ol, ol)


def _band_delta(do_b, o_b, lse_b):
    def fn(do, o, lse):
        r = lax.broadcasted_iota(jnp.int32, (DIL_W, DIL_W), 0) // HEAD_DIM
        c = lax.broadcasted_iota(jnp.int32, (DIL_W, DIL_W), 1) // HEAD_DIM
        do = do.astype(F32)
        delta = _split_dot(do * o.astype(F32), (r == c).astype(BF16))
        return (jnp.concatenate([do, lse, delta], axis=1),)

    return _rowwise(fn, [(do_b, "row"), (o_b, "row"), (lse_b, "row")], [(3 * DIL_W, F32, "row")], tm=512, name="band_delta")[0]


MEM_T = 512
MEM_SCALE = 128 ** -0.5
MEM_Q_COL = (D_IN - MEM_W) // LANES


def _mem_specs():
    q = pl.BlockSpec((None, MEM_T, LANES), lambda b, h, i: (b, i, MEM_Q_COL + h))
    k = pl.BlockSpec((None, MEM_LEN, LANES), lambda b, h, i: (b, 0, h))
    v = pl.BlockSpec((None, MEM_LEN, LANES), lambda b, h, i: (b, 0, MEM_W // LANES + h))
    blk = pl.BlockSpec((None, MEM_T, LANES), lambda b, h, i: (b, i, h))
    return q, k, v, blk


def _mem_probs(q, k):
    s = _dot_nt(q, k) * MEM_SCALE
    p = jnp.exp(s - jnp.max(s, axis=1, keepdims=True))
    return p * (1.0 / jnp.sum(p, axis=1, keepdims=True))


def _mem_fwd(proj3, kv3):
    bl = proj3.shape[0]

    def body(q_ref, k_ref, v_ref, o_ref):
        p = _mem_probs(q_ref[...], k_ref[...])
        o_ref[...] = _dot(p.astype(BF16), v_ref[...]).astype(BF16)

    q, k, v, blk = _mem_specs()
    return pl.pallas_call(
        body,
        grid=(bl, MEM_W // LANES, SEQ // MEM_T),
        in_specs=[q, k, v],
        out_specs=blk,
        out_shape=jax.ShapeDtypeStruct((bl, SEQ, MEM_W), BF16),
        compiler_params=pltpu.CompilerParams(dimension_semantics=("parallel", "parallel", "parallel")),
        name="mem_fwd",
    )(proj3, kv3, kv3)


def _mem_bwd(proj3, kv3, do_c):
    bl = proj3.shape[0]

    def body(q_ref, k_ref, v_ref, do_ref, dq_ref, dk_ref, dv_ref):
        @pl.when(pl.program_id(2) == 0)
        def _():
            dk_ref[...] = jnp.zeros_like(dk_ref)
            dv_ref[...] = jnp.zeros_like(dv_ref)

        q, k, do = q_ref[...], k_ref[...], do_ref[...]
        p = _mem_probs(q, k)
        dp = _dot_nt(do, v_ref[...])
        ds = (p * (dp - jnp.sum(p * dp, axis=1, keepdims=True)) * MEM_SCALE).astype(BF16)
        dq_ref[...] = _dot(ds, k).astype(BF16)
        dk_ref[...] += _dot_tn(ds, q)
        dv_ref[...] += _dot_tn(p.astype(BF16), do)

    q, k, v, blk = _mem_specs()
    kv_out = pl.BlockSpec((None, MEM_LEN, LANES), lambda b, h, i: (b, 0, h))
    return pl.pallas_call(
        body,
        grid=(bl, MEM_W // LANES, SEQ // MEM_T),
        in_specs=[q, k, v, blk],
        out_specs=[blk, kv_out, kv_out],
        out_shape=[jax.ShapeDtypeStruct((bl, SEQ, MEM_W), BF16), jax.ShapeDtypeStruct((bl, MEM_LEN, MEM_W), F32),
                   jax.ShapeDtypeStruct((bl, MEM_LEN, MEM_W), F32)],
        compiler_params=pltpu.CompilerParams(dimension_semantics=("parallel", "parallel", "arbitrary")),
        name="mem_bwd",
    )(proj3, kv3, kv3, do_c)


def _place():
    x, y, c = lax.axis_index("x"), lax.axis_index("y"), lax.axis_index("c")
    return x, y, c


def _other_chips(x, y):
    return [(1 - x, y), (x, 1 - y), (1 - x, 1 - y)]


def _remote(src, dst, send_sem, recv_sem, to):
    return pltpu.make_async_remote_copy(src_ref=src, dst_ref=dst, send_sem=send_sem, recv_sem=recv_sem,
                                        device_id=to, device_id_type=MESH)


ANY = pl.BlockSpec(memory_space=pl.ANY)


def _gather_weights(shards):
    n = len(shards)

    def body(*refs):
        send, forward, finish = _gather_phases(refs[:n], refs[n:2 * n], *refs[2 * n:])
        send()
        forward()
        finish()

    return pl.pallas_call(
        body,
        in_specs=[ANY] * n,
        out_specs=[ANY] * n,
        out_shape=_gather_out_shapes(shards),
        scratch_shapes=_gather_sems(n),
        name="gather_weights",
    )(*shards)


def _gather_out_shapes(shards):
    return [jax.ShapeDtypeStruct((N_CHIPS,) + s.shape, s.dtype) for s in shards]


def _gather_sems(n):
    return [pltpu.SemaphoreType.DMA((6 * n,)), pltpu.SemaphoreType.DMA((6 * n,))]


def _gather_phases(in_refs, out_refs, send_sems, recv_sems):
    x, y, c = _place()
    sibling = (x, y, 1 - c)
    chips = _other_chips(x, y)
    first, passed = [], []
    for k in range(len(in_refs)):
        hf = in_refs[k].shape[0] // 2

        def half(px, py, pc, k=k, hf=hf):
            return out_refs[k].at[2 * px + py, pl.ds(pc * hf, hf), :]

        src = in_refs[k].at[pl.ds(c * hf, hf), :]
        for j, chip in enumerate(chips):
            s = 6 * k + j
            first.append(_remote(src, half(x, y, c), send_sems.at[s], recv_sems.at[s], (*chip, c)))
            passed.append((_remote(src, half(*chip, c), send_sems.at[s], recv_sems.at[s], (*chip, c)),
                           _remote(half(*chip, c), half(*chip, c), send_sems.at[s + 3], recv_sems.at[s + 3], sibling),
                           _remote(src, half(*chip, 1 - c), send_sems.at[s + 3], recv_sems.at[s + 3], sibling)))

    def send():
        for cp in first:
            cp.start()

    def forward():
        for landed, fwd, _ in passed:
            landed.wait_recv()
            fwd.start()

    def finish():
        for _, _, from_sibling in passed:
            from_sibling.wait_recv()
        for cp in first:
            cp.wait_send()
        for _, fwd, _ in passed:
            fwd.wait_send()

    return send, forward, finish


def _pair_exchange(grads, *, name):
    n = len(grads)

    def body(*refs):
        g_refs, land_refs = refs[:n], refs[n:2 * n]
        send_sems, recv_sems = refs[2 * n:]
        x, y, c = _place()
        cps = []
        for k in range(n):
            hf = g_refs[k].shape[1] // 2
            src = g_refs[k].at[:, pl.ds((1 - c) * hf, hf), :]
            cps.append(_remote(src, land_refs[k], send_sems.at[k], recv_sems.at[k], (x, y, 1 - c)))
        for cp in cps:
            cp.start()
        for cp in cps:
            cp.wait()

    return pl.pallas_call(
        body,
        in_specs=[ANY] * n,
        out_specs=[ANY] * n,
        out_shape=[jax.ShapeDtypeStruct((N_CHIPS, g.shape[1] // 2, g.shape[2]), F32) for g in grads],
        scratch_shapes=[pltpu.SemaphoreType.DMA((n,)), pltpu.SemaphoreType.DMA((n,))],
        name=name,
    )(*grads)


def _pair_add(g, land, c_arr, *, name):
    _, a, b = g.shape
    hf = a // 2

    def body(c_ref, g_ref, l_ref, o_ref):
        o_ref[...] = (g_ref[...] + l_ref[...]).astype(BF16)

    return pl.pallas_call(
        body,
        grid_spec=pltpu.PrefetchScalarGridSpec(
            num_scalar_prefetch=1,
            grid=(N_CHIPS,),
            in_specs=[pl.BlockSpec((None, None, hf, b), lambda s, c_ref: (s, c_ref[0], 0, 0)),
                      pl.BlockSpec((None, hf, b), lambda s, c_ref: (s, 0, 0))],
            out_specs=pl.BlockSpec((None, hf, b), lambda s, c_ref: (s, 0, 0)),
        ),
        out_shape=jax.ShapeDtypeStruct((N_CHIPS, hf, b), BF16),
        compiler_params=pltpu.CompilerParams(dimension_semantics=("parallel",)),
        name=name,
    )(c_arr, g.reshape(N_CHIPS, 2, hf, b), land)


def _chip_exchange(parts):
    n = len(parts)

    def body(*refs):
        send, finish = _chip_exchange_phases(refs[:n], refs[n:2 * n], *refs[2 * n:])
        send()
        finish()

    return pl.pallas_call(
        body,
        in_specs=[ANY] * n,
        out_specs=[ANY] * n,
        out_shape=[jax.ShapeDtypeStruct(p.shape, p.dtype) for p in parts],
        scratch_shapes=_chip_exchange_sems(n),
        name="chip_exchange",
    )(*parts)


def _chip_exchange_sems(n):
    return [pltpu.SemaphoreType.DMA((3 * n,)), pltpu.SemaphoreType.DMA((3 * n,))]


def _chip_exchange_phases(p_refs, land_refs, send_sems, recv_sems):
    x, y, c = _place()
    me = 2 * x + y
    sends, recvs = [], []
    for k in range(len(p_refs)):
        for j, (cx, cy) in enumerate(_other_chips(x, y)):
            s = 3 * k + j
            sends.append(_remote(p_refs[k].at[2 * cx + cy], land_refs[k].at[me], send_sems.at[s], recv_sems.at[s], (cx, cy, c)))
            recvs.append(_remote(p_refs[k].at[me], land_refs[k].at[2 * cx + cy], send_sems.at[s], recv_sems.at[s], (cx, cy, c)))

    def send():
        for cp in sends:
            cp.start()

    def finish():
        for cp in recvs:
            cp.wait_recv()
        for cp in sends:
            cp.wait_send()

    return send, finish


def _chip_add(land, part, me_arr, *, name):
    _, r, b = land.shape

    def body(me_ref, p_ref, l1_ref, l2_ref, l3_ref, o_ref):
        o_ref[...] = ((p_ref[...].astype(F32) + l1_ref[...].astype(F32)) + l2_ref[...].astype(F32)) + l3_ref[...].astype(F32)

    tr = r // 2
    other = lambda j: pl.BlockSpec((None, tr, b), lambda i, me_ref: (jnp.bitwise_xor(me_ref[0], j), i, 0))
    return pl.pallas_call(
        body,
        grid_spec=pltpu.PrefetchScalarGridSpec(
            num_scalar_prefetch=1,
            grid=(r // tr,),
            in_specs=[pl.BlockSpec((None, tr, b), lambda i, me_ref: (me_ref[0], i, 0)), other(2), other(1), other(3)],
            out_specs=pl.BlockSpec((tr, b), lambda i, me_ref: (i, 0)),
        ),
        out_shape=jax.ShapeDtypeStruct((r, b), F32),
        compiler_params=pltpu.CompilerParams(dimension_semantics=("parallel",)),
        name=name,
    )(me_arr, part, land, land, land)


def _pair_share(halves):
    n = len(halves)

    def body(*refs):
        h_refs, out_refs = refs[:n], refs[n:2 * n]
        send_sems, recv_sems = refs[2 * n:]
        x, y, c = _place()
        cps = [_remote(h_refs[k], out_refs[k], send_sems.at[k], recv_sems.at[k], (x, y, 1 - c)) for k in range(n)]
        for cp in cps:
            cp.start()
        for cp in cps:
            cp.wait()

    return pl.pallas_call(
        body,
        in_specs=[ANY] * n,
        out_specs=[ANY] * n,
        out_shape=[jax.ShapeDtypeStruct(h.shape, F32) for h in halves],
        scratch_shapes=[pltpu.SemaphoreType.DMA((n,)), pltpu.SemaphoreType.DMA((n,))],
        name="pair_share",
    )(*halves)


def _all_sum_small(part):
    def body(p_ref, o_ref, slots, send_sems, recv_sems):
        x, y, c = _place()
        me = 4 * x + 2 * y + c
        slots[me] = p_ref[...]
        peers = [(x ^ dx, y ^ dy, c ^ dc) for dx in (0, 1) for dy in (0, 1) for dc in (0, 1)][1:]
        sends = [_remote(p_ref, slots.at[me], send_sems.at[k], recv_sems.at[k], peer) for k, peer in enumerate(peers)]
        for cp in sends:
            cp.start()
        for k, (px, py, pc) in enumerate(peers):
            _remote(p_ref, slots.at[4 * px + 2 * py + pc], send_sems.at[k], recv_sems.at[k], (px, py, pc)).wait_recv()
        for cp in sends:
            cp.wait_send()
        acc = slots[0]
        for d in range(1, 8):
            acc = acc + slots[d]
        o_ref[...] = acc

    vmem = pl.BlockSpec(memory_space=pltpu.VMEM)
    return pl.pallas_call(
        body,
        in_specs=[vmem],
        out_specs=vmem,
        out_shape=jax.ShapeDtypeStruct(part.shape, F32),
        scratch_shapes=[pltpu.VMEM((8,) + part.shape, F32), pltpu.SemaphoreType.DMA((7,)), pltpu.SemaphoreType.DMA((7,))],
        name="all_sum_small",
    )(part)


def _deinterleave(a, d):
    b, s, c = a.shape
    return a.reshape(b, s // d, d, c).transpose(0, 2, 1, 3).reshape(b * s // BAND, BAND, c)


def _reinterleave(a, d, b):
    c = a.shape[-1]
    return a.reshape(b, d, SEQ // d, c).transpose(0, 2, 1, 3).reshape(b, SEQ, c)


def _rope_tables():
    half = HEAD_DIM // 2
    inv_freq = ROPE_THETA ** (-jnp.arange(half, dtype=F32) * 2.0 / HEAD_DIM)
    ang = jnp.arange(SEQ, dtype=F32)[:, None] * inv_freq[None, :]
    cos = jnp.tile(jnp.cos(ang), (1, 2 * BAND_HEADS))
    sin = jnp.tile(jnp.concatenate([-jnp.sin(ang), jnp.sin(ang)], axis=1), (1, BAND_HEADS))
    cos_t = jnp.stack([_deinterleave(cos[None], d) for d in DIL_D])
    sin_t = jnp.stack([_deinterleave(sin[None], d) for d in DIL_D])
    return cos_t, sin_t


def _local_step(x, mem, loss_target, g_pre_mix, g_post_mix, g_pre_ffn, g_post_ffn, g_mem, b_gate, w, comm=None):
    bl = x.shape[0]
    t = bl * SEQ
    chips = range(N_CHIPS)
    w_in_full = jnp.concatenate([w["w_in"][s] for s in chips], axis=1)
    w_mem_kv_full = w["w_mem_kv"].reshape(D_MODEL, 2 * MEM_W)
    half_ff = D_FF // 2

    x2 = x.reshape(t, D_MODEL)
    tgt2 = loss_target.reshape(t, D_MODEL)
    mem2 = mem.reshape(bl * MEM_LEN, D_MODEL)

    h = _norm_fwd(x2, g_pre_mix, name="norm_x")
    proj = _mm([(h, w_in_full)], nt=False, tm=512, tn=2176, out_dtypes=[BF16], name="proj")
    gates = _mm([(h, w["w_gate"], None, "j")], nt=False, tm=512, tn=w["w_gate"].shape[2], out_dtypes=[BF16], name="gates",
                bias=b_gate, epilogue=lambda acc: (_sigmoid(acc),))
    hm = _norm_fwd(mem2, g_mem, name="norm_mem")
    kv_m = _mm([(hm, w_mem_kv_full)], nt=False, tm=512, tn=1024, out_dtypes=[BF16], name="mem_kv")
    proj3 = proj.reshape(bl, SEQ, D_IN)
    kv3 = kv_m.reshape(bl, MEM_LEN, 2 * MEM_W)

    o_a, o_a32, late_gathered = _sb_fwd(proj3, comm["late_shards"] if comm else [])
    if comm:
        w = {**w, **{name: lax.dynamic_update_slice(g, s[None], (comm["me"][0], 0, 0))
                     for name, g, s in zip(comm["late_names"], late_gathered, comm["late_shards"])}}
    w_o_full = w["w_o"].reshape(D_MODEL, D_MODEL)
    w_ffn_out_full = w["w_ffn_out"].reshape(D_FF, D_MODEL)

    cos_t, sin_t = _rope_tables()
    dil0 = 3 * SB_W

    grp_w = 3 * DIL_W
    qkv_s = jnp.stack([_deinterleave(proj3[:, :, dil0 + g * grp_w:dil0 + (g + 1) * grp_w], d) for g, d in enumerate(DIL_D)])
    ol_s = _band_fwd(qkv_s, cos_t, sin_t)
    ol = jnp.stack([_reinterleave(ol_s[g], d, bl) for g, d in enumerate(DIL_D)]).reshape(3, t, 2 * DIL_W)
    o_b, lse_b = _band_merge(ol)

    o_c = _mem_fwd(proj3, kv3)

    o_a2, o_c2 = o_a.reshape(t, SB_W), o_c.reshape(t, MEM_W)
    y_a, y_b, y_c, merged = _branch_merge_fwd(o_a2, o_b, o_c2, w["w_br_sb"], w["w_br_dil"], w["w_br_mem"], gates)
    mix = _mm([(merged, w_o_full)], nt=False, tm=512, tn=1024, out_dtypes=[F32], name="mix")
    x1, h2 = _mid_fwd(mix, x2, g_post_mix, g_pre_ffn)
    gg, uu, f = _ffn_in_fwd(h2, w["w_ffn_in"])
    f2 = _mm([(f, w_ffn_out_full)], nt=False, tm=512, tn=1024, out_dtypes=[F32], name="ffn_out")

    dy, df2, dg_post_ffn, loss_row = _loss_bwd(f2, x1, g_post_ffn, tgt2)

    dg_ffn, du_ffn = _mm([(df2, w_ffn_out_full)], nt=True, tm=512, tn=half_ff, out_dtypes=[BF16, BF16], name="d_ffn_act",
                         extras=(gg, uu), epilogue=_swiglu_bwd_epilogue)
    gw = {}
    gw["w_ffn_out"] = _mm_tn(f, df2, tm=half_ff, tn=1024, tk=512, name="gw_ffn_out").reshape(N_CHIPS, D_FF // N_CHIPS, D_MODEL)
    gw_ffn_g = _mm_tn(h2, dg_ffn, tm=1024, tn=half_ff, tk=512, name="gw_ffn_gate", out_shards=True)
    gw_ffn_u = _mm_tn(h2, du_ffn, tm=1024, tn=half_ff, tk=512, name="gw_ffn_up", out_shards=True)
    gw["w_ffn_in"] = jnp.concatenate([gw_ffn_g, gw_ffn_u], axis=0)
    dh2 = _mm([(dg_ffn, w["w_ffn_in"], 0, 0), (dg_ffn, w["w_ffn_in"], 1, 1), (du_ffn, w["w_ffn_in"], 0, 2),
               (du_ffn, w["w_ffn_in"], 1, 3)], nt=True, tm=256, tn=1024, out_dtypes=[F32], name="d_h2")
    dx1, dmix, dg_pre_ffn, dg_post_mix = _mid_bwd(dh2, x1, mix, g_pre_ffn, g_post_mix, dy)

    gw["w_o"] = _mm_tn(merged, dmix, tm=1024, tn=1024, tk=512, name="gw_o").reshape(N_CHIPS, D_MODEL // N_CHIPS, D_MODEL)
    dmerged = _mm([(dmix, w_o_full)], nt=True, tm=512, tn=1024, out_dtypes=[F32], name="d_merged")
    dy_a, dy_b, dy_c, dgpre, db_gate = _gate_bwd(dmerged, gates, y_a, y_b, y_c)
    br_cols = D_MODEL // N_CHIPS
    gw["w_br_sb"] = _mm_tn(o_a2, dy_a, tm=512, tn=br_cols, tk=1024, name="gw_br_sb", out_shards=True)
    gw["w_br_dil"] = _mm_tn(o_b, dy_b, tm=256, tn=br_cols, tk=1024, name="gw_br_dil", out_shards=True)
    gw["w_br_mem"] = _mm_tn(o_c2, dy_c, tm=512, tn=br_cols, tk=1024, name="gw_br_mem", out_shards=True)
    gw["w_gate"] = _mm_tn(h, dgpre, tm=1024, tn=w["w_gate"].shape[2], tk=512, name="gw_gate", out_shards=True)
    do_a = _mm([(dy_a, w["w_br_sb"], s, s) for s in chips], nt=True, tm=512, tn=SB_W, out_dtypes=[BF16], name="d_o_a")
    do_b = _mm([(dy_b, w["w_br_dil"], s, s) for s in chips], nt=True, tm=512, tn=DIL_W, out_dtypes=[BF16], name="d_o_b")
    do_c = _mm([(dy_c, w["w_br_mem"], s, s) for s in chips], nt=True, tm=512, tn=MEM_W, out_dtypes=[BF16], name="d_o_c")

    dq_c, dk_m, dv_m = _mem_bwd(proj3, kv3, do_c.reshape(bl, SEQ, MEM_W))
    dkv_m = jnp.concatenate([dk_m, dv_m], axis=-1).reshape(bl * MEM_LEN, 2 * MEM_W).astype(BF16)
    gw["w_mem_kv"] = _mm_tn(hm, dkv_m, tm=1024, tn=1024, tk=512, name="gw_mem_kv").reshape(N_CHIPS, D_MODEL // N_CHIPS, 2 * MEM_W)
    dhm = _mm([(dkv_m, w_mem_kv_full)], nt=True, tm=512, tn=1024, out_dtypes=[F32], name="d_hm")
    dg_mem = _mem_norm_bwd(dhm, mem2, g_mem)

    dcat = _band_delta(do_b, o_b, lse_b).reshape(bl, SEQ, grp_w)
    dcat_s = jnp.stack([_deinterleave(dcat, d) for d in DIL_D])
    dq_r, dk_r, dv_r, dk_first, dv_first = _band_bwd(qkv_s, cos_t, sin_t, dcat_s)
    dqkv_s = _band_combine(dq_r, dk_r, dv_r, dk_first, dv_first, cos_t, sin_t)
    d_dil = [_reinterleave(dqkv_s[g], d, bl) for g, d in enumerate(DIL_D)]

    early, parts = [], []
    if comm:
        early = [name for name, _, _ in PACK if name != "w_in"]
        grads = [gw[name] for name in early]
        parts = [_pair_add(g, l, comm["c"], name="pair_add_" + name)
                 for name, g, l in zip(early, grads, _pair_exchange(grads, name="pair_exchange_early"))]
    dq_a, dk_a, dv_a, lands = _sb_bwd(proj3, o_a32, do_a.reshape(bl, SEQ, SB_W), parts)
    reduced = {name: (p, l) for name, p, l in zip(early, parts, lands)}

    dproj = jnp.concatenate([dq_a, dk_a, dv_a] + d_dil + [dq_c], axis=-1).reshape(t, D_IN)
    in_cols = D_IN // N_CHIPS
    dproj_s = jnp.stack([dproj[:, s * in_cols:(s + 1) * in_cols] for s in chips])
    gw["w_in"] = _mm_tn(h, dproj_s, tm=1024, tn=in_cols, tk=512, name="gw_in")
    dh = _mm([(dproj_s, w["w_in"], s, s) for s in chips] + [(dgpre, w["w_gate"], s, s) for s in chips],
             nt=True, tm=256, tn=1024, out_dtypes=[F32], name="d_h")
    grad_x, dg_pre_mix = _first_bwd(dh, x2, g_pre_mix, dx1)
    small = jnp.concatenate([dg_pre_mix, dg_post_mix, dg_pre_ffn, dg_post_ffn, dg_mem, db_gate.reshape(3, D_MODEL)], axis=0)
    return loss_row[0, 0], grad_x.reshape(bl, SEQ, D_MODEL), gw, small, reduced


def kernel(x, mem, g_pre_mix, g_post_mix, g_pre_ffn, g_post_ffn, g_mem, w_in, w_mem_kv, w_br_sb, w_br_dil, w_br_mem, w_gate, b_gate, w_o, w_ffn_in, w_ffn_out, loss_target, m_g_pre_mix, m_g_post_mix, m_g_pre_ffn, m_g_post_ffn, m_g_mem, m_w_in, m_w_mem_kv, m_w_br_sb, m_w_br_dil, m_w_br_mem, m_w_gate, m_b_gate, m_w_o, m_w_ffn_in, m_w_ffn_out, v_g_pre_mix, v_g_post_mix, v_g_pre_ffn, v_g_post_ffn, v_g_mem, v_w_in, v_w_mem_kv, v_w_br_sb, v_w_br_dil, v_w_br_mem, v_w_gate, v_b_gate, v_w_o, v_w_ffn_in, v_w_ffn_out):
    w_shards = dict(w_in=w_in[0], w_mem_kv=w_mem_kv[0], w_br_sb=w_br_sb[0], w_br_dil=w_br_dil[0], w_br_mem=w_br_mem[0],
                    w_gate=w_gate[0], w_o=w_o[0], w_ffn_in=w_ffn_in[0], w_ffn_out=w_ffn_out[0])
    m_shards = dict(w_in=m_w_in[0], w_mem_kv=m_w_mem_kv[0], w_br_sb=m_w_br_sb[0], w_br_dil=m_w_br_dil[0], w_br_mem=m_w_br_mem[0],
                    w_gate=m_w_gate[0], w_o=m_w_o[0], w_ffn_in=m_w_ffn_in[0], w_ffn_out=m_w_ffn_out[0])
    v_shards = dict(w_in=v_w_in[0], w_mem_kv=v_w_mem_kv[0], w_br_sb=v_w_br_sb[0], w_br_dil=v_w_br_dil[0], w_br_mem=v_w_br_mem[0],
                    w_gate=v_w_gate[0], w_o=v_w_o[0], w_ffn_in=v_w_ffn_in[0], w_ffn_out=v_w_ffn_out[0])

    names = [name for name, _, _ in PACK]
    c_arr = lax.axis_index("c").astype(jnp.int32).reshape(1)
    me_arr = (2 * lax.axis_index("x") + lax.axis_index("y")).astype(jnp.int32).reshape(1)
    first_names = ["w_in", "w_gate", "w_mem_kv"]
    late_names = [name for name in names if name not in first_names]
    first_bf = [w_shards[name].astype(BF16) for name in first_names]
    w = {name: lax.dynamic_update_slice(g, s[None], (me_arr[0], 0, 0))
         for name, g, s in zip(first_names, _gather_weights(first_bf), first_bf)}
    comm = dict(c=c_arr, me=me_arr, late_names=late_names, late_shards=[w_shards[name].astype(BF16) for name in late_names])

    loss_local, grad_x, gw, small, reduced = _local_step(x, mem, loss_target, g_pre_mix, g_post_mix, g_pre_ffn, g_post_ffn,
                                                         g_mem, b_gate, w, comm)
    loss = lax.psum(loss_local, ("x", "y", "c"))

    part_in = _pair_add(gw["w_in"], _pair_exchange([gw["w_in"]], name="pair_exchange_w_in")[0], c_arr, name="pair_add_w_in")
    reduced["w_in"] = (part_in, _chip_exchange([part_in])[0])
    halves = [_chip_add(reduced[name][1], reduced[name][0], me_arr, name="chip_add_" + name) for name in names]
    theirs = _pair_share(halves)
    small = _all_sum_small(small)

    upd = {}
    for name, mine, other in zip(names, halves, theirs):
        upd[name] = _adamw_halves(w_shards[name], mine, other, m_shards[name], v_shards[name], c_arr, name="adamw_" + name)
    g_shards = {name: u[0] for name, u in upd.items()}

    def small8(gs, b):
        return jnp.concatenate(gs + [b.reshape(3, D_MODEL)], axis=0)

    sw = small8([g_pre_mix, g_post_mix, g_pre_ffn, g_post_ffn, g_mem], b_gate)
    sm = small8([m_g_pre_mix, m_g_post_mix, m_g_pre_ffn, m_g_post_ffn, m_g_mem], m_b_gate)
    sv = small8([v_g_pre_mix, v_g_post_mix, v_g_pre_ffn, v_g_post_ffn, v_g_mem], v_b_gate)
    s_upd = _adamw(sw, small, sm, sv, tm=8, name="adamw_small")

    def small_out(a):
        return [a[0:1], a[1:2], a[2:3], a[3:4], a[4:5]]

    order = ["w_in", "w_mem_kv", "w_br_sb", "w_br_dil", "w_br_mem", "w_gate", "b_gate", "w_o", "w_ffn_in", "w_ffn_out"]

    def leaves(small_arr, big):
        out = small_out(small_arr)
        for name in order:
            out.append(small_arr[5:8].reshape(1, 3 * D_MODEL) if name == "b_gate" else big[name][None])
        return out

    grads_out = leaves(small, g_shards)
    delta_out = leaves(s_upd[0], {n: u[1] for n, u in upd.items()})
    m_out = leaves(s_upd[1], {n: u[2] for n, u in upd.items()})
    v_out = leaves(s_upd[2], {n: u[3] for n, u in upd.items()})
    return (loss, grad_x, *grads_out, *delta_out, *m_out, *v_out)
```

```python
import functools

import jax
import jax.numpy as jnp
from jax import lax
from jax.experimental import pallas as pl
from jax.experimental.pallas import tpu as pltpu

F32 = jnp.float32
BF16 = jnp.bfloat16
MESH = pl.DeviceIdType.MESH

D_MODEL = 1024
SEQ = 2048
HEAD_DIM = 64
SB_W = 512
DIL_W = 256
MEM_W = 512
MEM_LEN = 256
D_IN = 3 * SB_W + 9 * DIL_W + MEM_W
D_FF = 2816
DIL_D = (1, 4, 16)
ROPE_THETA = 10000.0
NORM_EPS = 1e-6
NEG_INF = -1e30
LANES = 128

ADAM_LR = 0.001
ADAM_B1 = 0.9
ADAM_B2 = 0.999
ADAM_EPS = 1e-08
ADAM_WD = 0.01
ADAM_STEP = 10

N_CHIPS = 4
PACK = (
    ("w_in", (1024, 1088), 1),
    ("w_mem_kv", (256, 1024), 0),
    ("w_br_sb", (512, 256), 1),
    ("w_br_dil", (256, 256), 1),
    ("w_br_mem", (512, 256), 1),
    ("w_gate", (1024, 768), 1),
    ("w_o", (256, 1024), 0),
    ("w_ffn_in", (1024, 1408), 1),
    ("w_ffn_out", (704, 1024), 0),
)
PACK_ROWS = sum(a * b for _, (a, b), _ in PACK) // D_MODEL
HALF_ROWS = PACK_ROWS // 2


def _dot(a, b):
    return lax.dot_general(a, b, (((1,), (0,)), ((), ())), preferred_element_type=F32)


def _dot_nt(a, b):
    return lax.dot_general(a, b, (((1,), (1,)), ((), ())), preferred_element_type=F32)


def _dot_tn(a, b):
    return lax.dot_general(a, b, (((0,), (0,)), ((), ())), preferred_element_type=F32)


def _split_dot(x, u):
    hi = x.astype(BF16)
    lo = (x - hi.astype(F32)).astype(BF16)
    return _dot(hi, u) + _dot(lo, u)


def _mm(pairs, *, nt, tm, tn, out_dtypes, name, bias=None, extras=(), epilogue=None):
    pairs = [p if len(p) == 4 else (p[0], p[1], None, None) for p in pairs]
    m = pairs[0][0].shape[-2]
    b0 = pairs[0][1]
    if nt:
        n = b0.shape[-2]
    else:
        n = b0.shape[-1] * (b0.shape[0] if b0.ndim == 3 else 1)
    n_pairs, n_extra, n_out = len(pairs), len(extras), len(out_dtypes)
    tm = min(tm, m)
    assert m % tm == 0 and n % tn == 0
    has_bias = bias is not None

    def body(*refs):
        acc = None
        for i in range(n_pairs):
            a, b = refs[2 * i][...], refs[2 * i + 1][...]
            p = _dot_nt(a, b) if nt else _dot(a, b)
            acc = p if acc is None else acc + p
        pos = 2 * n_pairs
        if has_bias:
            acc = acc + refs[pos][...]
            pos += 1
        ex = [r[...] for r in refs[pos:pos + n_extra]]
        outs = refs[pos + n_extra:]
        vals = (acc,) if epilogue is None else epilogue(acc, *ex)
        for r, v, dt in zip(outs, vals, out_dtypes):
            r[...] = v.astype(dt)

    in_specs, args = [], []
    for a, b, a_col, b_sel in pairs:
        k = b.shape[-1] if nt else b.shape[-2]
        assert a_col is not None or a.shape[1] == k
        if a.ndim == 3:
            in_specs.append(pl.BlockSpec((None, tm, k), lambda j, i, c=a_col: (c, i, 0)))
        else:
            in_specs.append(pl.BlockSpec((tm, k), lambda j, i, c=a_col or 0: (i, c)))
        if b.ndim == 2:
            in_specs.append(pl.BlockSpec((tn, k), lambda j, i: (j, 0)) if nt else pl.BlockSpec((k, tn), lambda j, i: (0, j)))
        elif nt:
            in_specs.append(pl.BlockSpec((None, tn, k), lambda j, i, s=b_sel: (s, j, 0)))
        else:
            assert b_sel == "j" and b.shape[-1] == tn
            in_specs.append(pl.BlockSpec((None, k, tn), lambda j, i: (j, 0, 0)))
        args += [a, b]
    if has_bias:
        in_specs.append(pl.BlockSpec((1, tn), lambda j, i: (0, j)))
        args.append(bias)
    for e in extras:
        in_specs.append(pl.BlockSpec((tm, tn), lambda j, i: (i, j)))
        args.append(e)
    out = pl.pallas_call(
        body,
        grid=(n // tn, m // tm),
        in_specs=in_specs,
        out_specs=[pl.BlockSpec((tm, tn), lambda j, i: (i, j)) for _ in range(n_out)],
        out_shape=[jax.ShapeDtypeStruct((m, n), dt) for dt in out_dtypes],
        compiler_params=pltpu.CompilerParams(dimension_semantics=("parallel", "parallel")),
        name=name,
    )(*args)
    return out[0] if n_out == 1 else out


def _mm_tn(a, b, *, tm, tn, tk, name, out_shards=False):
    k, m = a.shape
    b_shards = b.ndim == 3
    out_shards = out_shards or b_shards
    n = b.shape[0] * b.shape[2] if b_shards else b.shape[1]
    tk = min(tk, k)
    assert m % tm == 0 and n % tn == 0 and k % tk == 0 and (not b_shards or b.shape[2] == tn)

    def body(a_ref, b_ref, o_ref):
        @pl.when(pl.program_id(2) == 0)
        def _():
            o_ref[...] = jnp.zeros_like(o_ref)

        o_ref[...] += _dot_tn(a_ref[...], b_ref[...])

    if b_shards:
        b_spec = pl.BlockSpec((None, tk, tn), lambda i, j, kk: (j, kk, 0))
    else:
        b_spec = pl.BlockSpec((tk, tn), lambda i, j, kk: (kk, j))
    if out_shards:
        out_spec = pl.BlockSpec((None, tm, tn), lambda i, j, kk: (j, i, 0))
        out_shape = jax.ShapeDtypeStruct((n // tn, m, tn), F32)
    else:
        out_spec = pl.BlockSpec((tm, tn), lambda i, j, kk: (i, j))
        out_shape = jax.ShapeDtypeStruct((m, n), F32)
    return pl.pallas_call(
        body,
        grid=(m // tm, n // tn, k // tk),
        in_specs=[pl.BlockSpec((tk, tm), lambda i, j, kk: (kk, i)), b_spec],
        out_specs=out_spec,
        out_shape=out_shape,
        compiler_params=pltpu.CompilerParams(dimension_semantics=("parallel", "parallel", "arbitrary")),
        name=name,
    )(a, b)


def _rowwise(fn, ins, outs, *, tm, name):
    rows = next(a.shape[0] for a, kind in ins if kind == "row")
    tm = min(tm, rows)
    assert rows % tm == 0
    n_in = len(ins)

    def body(*refs):
        vals = fn(*[r[...] for r in refs[:n_in]])
        for (_, dt, kind), r, v in zip(outs, refs[n_in:], vals):
            if kind == "row":
                r[...] = v.astype(dt)
            else:
                @pl.when(pl.program_id(0) == 0)
                def _(r=r):
                    r[...] = jnp.zeros_like(r)

                r[...] += v

    in_specs = [pl.BlockSpec((tm, a.shape[1]), lambda i: (i, 0)) if kind == "row" else pl.BlockSpec(a.shape, lambda i: (0, 0))
                for a, kind in ins]
    out_specs = [pl.BlockSpec((tm, c), lambda i: (i, 0)) if kind == "row" else pl.BlockSpec((1, c), lambda i: (0, 0))
                 for c, _, kind in outs]
    out_shape = [jax.ShapeDtypeStruct((rows if kind == "row" else 1, c), dt) for c, dt, kind in outs]
    has_acc = any(kind == "acc" for _, _, kind in outs)
    return pl.pallas_call(
        body,
        grid=(rows // tm,),
        in_specs=in_specs,
        out_specs=out_specs,
        out_shape=out_shape,
        compiler_params=pltpu.CompilerParams(dimension_semantics=("arbitrary" if has_acc else "parallel",)),
        name=name,
    )(*[a for a, _ in ins])


def _rstd(x):
    return lax.rsqrt(jnp.mean(x * x, axis=-1, keepdims=True) + NORM_EPS)


def _norm_bwd(dout, xin, g):
    r = _rstd(xin)
    n = xin * r
    dn = dout * g
    dg = jnp.sum(dout * n, axis=0, keepdims=True)
    dx = r * (dn - n * jnp.mean(dn * n, axis=-1, keepdims=True))
    return dx, dg


def _sigmoid(x):
    return 1.0 / (1.0 + jnp.exp(-x))


def _norm_fwd(x, g, *, name):
    def fn(x, g):
        return ((x * _rstd(x)) * g,)

    return _rowwise(fn, [(x, "row"), (g, "vec")], [(D_MODEL, BF16, "row")], tm=512, name=name)[0]


def _mid_fwd(mix, x, g_post_mix, g_pre_ffn):
    def fn(mix, x, g2, g3):
        x1 = x + (mix * _rstd(mix)) * g2
        return x1, (x1 * _rstd(x1)) * g3

    return _rowwise(fn, [(mix, "row"), (x, "row"), (g_post_mix, "vec"), (g_pre_ffn, "vec")],
                    [(D_MODEL, F32, "row"), (D_MODEL, BF16, "row")], tm=512, name="mid_fwd")


def _loss_bwd(f2, x1, g_post_ffn, tgt):
    def fn(f2, x1, g4, tgt):
        r = _rstd(f2)
        n = f2 * r
        err = x1 + n * g4 - tgt
        loss = 0.5 * jnp.sum(jnp.mean(err * err, axis=-1, keepdims=True), axis=0, keepdims=True)
        dy = err * (1.0 / D_MODEL)
        dn = dy * g4
        dg4 = jnp.sum(dy * n, axis=0, keepdims=True)
        df2 = r * (dn - n * jnp.mean(dn * n, axis=-1, keepdims=True))
        return dy, df2, dg4, jnp.broadcast_to(loss, (1, LANES))

    return _rowwise(fn, [(f2, "row"), (x1, "row"), (g_post_ffn, "vec"), (tgt, "row")],
                    [(D_MODEL, F32, "row"), (D_MODEL, BF16, "row"), (D_MODEL, F32, "acc"), (LANES, F32, "acc")],
                    tm=512, name="loss_bwd")


def _mid_bwd(dh2, x1, mix, g_pre_ffn, g_post_mix, dy):
    def fn(dh2, x1, mix, g3, g2, dy):
        d3, dg3 = _norm_bwd(dh2, x1, g3)
        dx1 = dy + d3
        dmix, dg2 = _norm_bwd(dx1, mix, g2)
        return dx1, dmix, dg3, dg2

    return _rowwise(fn, [(dh2, "row"), (x1, "row"), (mix, "row"), (g_pre_ffn, "vec"), (g_post_mix, "vec"), (dy, "row")],
                    [(D_MODEL, F32, "row"), (D_MODEL, BF16, "row"), (D_MODEL, F32, "acc"), (D_MODEL, F32, "acc")],
                    tm=256, name="mid_bwd")


def _first_bwd(dh, x, g_pre_mix, dx1):
    def fn(dh, x, g1, dx1):
        d1, dg1 = _norm_bwd(dh, x, g1)
        return dx1 + d1, dg1

    return _rowwise(fn, [(dh, "row"), (x, "row"), (g_pre_mix, "vec"), (dx1, "row")],
                    [(D_MODEL, F32, "row"), (D_MODEL, F32, "acc")], tm=512, name="first_bwd")


def _mem_norm_bwd(dhm, mem, g_mem):
    def fn(dhm, mem, g):
        return (jnp.sum(dhm * (mem * _rstd(mem)), axis=0, keepdims=True),)

    return _rowwise(fn, [(dhm, "row"), (mem, "row"), (g_mem, "vec")], [(D_MODEL, F32, "acc")], tm=512, name="mem_norm_bwd")[0]


def _gate_bwd(dmerged, gates, ya, yb, yc):
    def fn(dm, gt, ya, yb, yc):
        gt = gt.astype(F32)
        outs, dgp = [], []
        for i, y in enumerate((ya, yb, yc)):
            gi = gt[:, i * D_MODEL:(i + 1) * D_MODEL]
            outs.append(dm * gi)
            dgp.append(dm * y.astype(F32) * gi * (1.0 - gi))
        dgpre = jnp.concatenate(dgp, axis=1)
        return outs[0], outs[1], outs[2], dgpre, jnp.sum(dgpre, axis=0, keepdims=True)

    return _rowwise(fn, [(dmerged, "row"), (gates, "row"), (ya, "row"), (yb, "row"), (yc, "row")],
                    [(D_MODEL, BF16, "row")] * 3 + [(3 * D_MODEL, BF16, "row"), (3 * D_MODEL, F32, "acc")],
                    tm=256, name="gate_bwd")


def _adamw_math(w, g, m, v):
    m = ADAM_B1 * m + (1.0 - ADAM_B1) * g
    v = ADAM_B2 * v + (1.0 - ADAM_B2) * (g * g)
    m_hat = m / (1.0 - ADAM_B1 ** ADAM_STEP)
    v_hat = v / (1.0 - ADAM_B2 ** ADAM_STEP)
    delta = -ADAM_LR * (m_hat / (jnp.sqrt(v_hat) + ADAM_EPS) + ADAM_WD * w)
    return delta, m, v


def _adamw(w, g, m, v, *, tm, name):
    c = w.shape[1]
    return _rowwise(_adamw_math, [(w, "row"), (g, "row"), (m, "row"), (v, "row")], [(c, F32, "row")] * 3, tm=tm, name=name)


def _adamw_halves(w, g_mine, g_theirs, m, v, c_arr, *, name):
    a, b = w.shape
    hf = a // 2
    tr = hf // 4

    def body(c_ref, w_ref, gm_ref, gt_ref, m_ref, v_ref, g_out, d_out, m_out, v_out):
        g = jnp.where(pl.program_id(0) == c_ref[0], gm_ref[...], gt_ref[...])
        d, m_new, v_new = _adamw_math(w_ref[...], g, m_ref[...], v_ref[...])
        g_out[...] = g
        d_out[...] = d
        m_out[...] = m_new
        v_out[...] = v_new

    full = pl.BlockSpec((tr, b), lambda hh, i, c_ref: (hh * (hf // tr) + i, 0))
    half = pl.BlockSpec((tr, b), lambda hh, i, c_ref: (i, 0))
    return pl.pallas_call(
        body,
        grid_spec=pltpu.PrefetchScalarGridSpec(
            num_scalar_prefetch=1,
            grid=(2, hf // tr),
            in_specs=[full, half, half, full, full],
            out_specs=[full] * 4,
        ),
        out_shape=[jax.ShapeDtypeStruct((a, b), F32)] * 4,
        compiler_params=pltpu.CompilerParams(dimension_semantics=("parallel", "parallel")),
        name=name,
    )(c_arr, w, g_mine, g_theirs, m, v)


def _ffn_in_fwd(h2, w_ffn):
    m, tm, tn = h2.shape[0], 512, w_ffn.shape[2]
    assert 2 * tn == D_FF

    def body(h_ref, wg_ref, wu_ref, g_ref, u_ref, f_ref):
        h = h_ref[...]
        g = _dot(h, wg_ref[...])
        u = _dot(h, wu_ref[...])
        g_ref[...] = g.astype(BF16)
        u_ref[...] = u.astype(BF16)
        f_ref[...] = (g * _sigmoid(g) * u).astype(BF16)

    o_spec = pl.BlockSpec((tm, tn), lambda j, i: (i, j))
    return pl.pallas_call(
        body,
        grid=(D_FF // tn, m // tm),
        in_specs=[pl.BlockSpec((tm, D_MODEL), lambda j, i: (i, 0)),
                  pl.BlockSpec((None, D_MODEL, tn), lambda j, i: (j, 0, 0)),
                  pl.BlockSpec((None, D_MODEL, tn), lambda j, i: (j + 2, 0, 0))],
        out_specs=[o_spec, o_spec, o_spec],
        out_shape=[jax.ShapeDtypeStruct((m, D_FF), BF16)] * 3,
        compiler_params=pltpu.CompilerParams(dimension_semantics=("parallel", "parallel")),
        name="ffn_in_fwd",
    )(h2, w_ffn, w_ffn)


def _swiglu_bwd_epilogue(df, g, u):
    g = g.astype(F32)
    u = u.astype(F32)
    sg = _sigmoid(g)
    return df * u * (sg * (1.0 + g * (1.0 - sg))), df * (g * sg)


def _branch_merge_fwd(o_a, o_b, o_c, w_sb, w_dil, w_mem, gates):
    m, tm = o_a.shape[0], 256

    def body(oa_ref, ob_ref, oc_ref, wa_ref, wb_ref, wc_ref, gt_ref, ya_ref, yb_ref, yc_ref, mg_ref):
        def project(o_ref, w_ref):
            o = o_ref[...]
            return jnp.concatenate([_dot(o, w_ref[s]) for s in range(N_CHIPS)], axis=1)

        ya = project(oa_ref, wa_ref)
        yb = project(ob_ref, wb_ref)
        yc = project(oc_ref, wc_ref)
        gt = gt_ref[...].astype(F32)
        ya_ref[...] = ya.astype(BF16)
        yb_ref[...] = yb.astype(BF16)
        yc_ref[...] = yc.astype(BF16)
        mg_ref[...] = (gt[:, :D_MODEL] * ya + gt[:, D_MODEL:2 * D_MODEL] * yb + gt[:, 2 * D_MODEL:] * yc).astype(BF16)

    row = lambda c: pl.BlockSpec((tm, c), lambda i: (i, 0))
    full = lambda a: pl.BlockSpec(a.shape, lambda i: (0, 0, 0))
    return pl.pallas_call(
        body,
        grid=(m // tm,),
        in_specs=[row(SB_W), row(DIL_W), row(MEM_W), full(w_sb), full(w_dil), full(w_mem), row(3 * D_MODEL)],
        out_specs=[row(D_MODEL)] * 4,
        out_shape=[jax.ShapeDtypeStruct((m, D_MODEL), BF16)] * 4,
        compiler_params=pltpu.CompilerParams(dimension_semantics=("parallel",)),
        name="branch_merge_fwd",
    )(o_a, o_b, o_c, w_sb, w_dil, w_mem, gates)


SB_T = 256
SB_SCALE = HEAD_DIM ** -0.5


def _sb_masks():
    row = lax.broadcasted_iota(jnp.int32, (SB_T, SB_T), 0)
    col = lax.broadcasted_iota(jnp.int32, (SB_T, SB_T), 1)
    lane = lax.broadcasted_iota(jnp.int32, (1, LANES), 1)
    return row, col, lane


def _sb_logs(z):
    e = jnp.exp(-jnp.abs(z))
    lb = jnp.minimum(z, 0.0) - jnp.log(1.0 + e)
    return lb, lb - z, e


def _sb_specs(n_heads_pairs, col0):
    q = pl.BlockSpec((None, SB_T, LANES), lambda b, p, i: (b, i, col0 + p))
    k = pl.BlockSpec((None, SEQ, LANES), lambda b, p, i: (b, 0, col0 + n_heads_pairs + p))
    v = pl.BlockSpec((None, SEQ, LANES), lambda b, p, i: (b, 0, col0 + 2 * n_heads_pairs + p))
    return q, k, v


def _grid_step(n_pairs, nq):
    return (pl.program_id(0) * n_pairs + pl.program_id(1)) * nq + pl.program_id(2)


def _sb_fwd(proj3, late_shards):
    bl = proj3.shape[0]
    n_pairs = SB_W // LANES
    nq = SEQ // SB_T
    n_late = len(late_shards)
    n_steps = bl * n_pairs * nq

    def body(q_ref, k_ref, v_ref, *rest):
        late_in, (o_ref, o32_ref), late_out = rest[:n_late], rest[n_late:n_late + 2], rest[n_late + 2:2 * n_late + 2]
        step = _grid_step(n_pairs, nq)
        if n_late:
            send, forward, finish = _gather_phases(late_in, late_out, *rest[2 * n_late + 2:])
            pl.when(step == 0)(send)
            pl.when(step == n_steps // 2)(forward)
        i = pl.program_id(2)
        row, col, lane = _sb_masks()
        causal = col < row
        u_excl = (row > col).astype(BF16)
        q = q_ref[...]
        heads = []
        for h in range(2):
            mh = (lane // HEAD_DIM) == h
            heads.append((mh, jnp.where(mh, q, jnp.zeros_like(q)) * SB_SCALE))

        def blocks(js, carries, acc, diag):
            ks = [k_ref[pl.ds(pl.multiple_of(j * SB_T, SB_T), SB_T), :] for j in js]
            vs = [v_ref[pl.ds(pl.multiple_of(j * SB_T, SB_T), SB_T), :] for j in js]
            chains = [(b, h) for b in range(len(js)) for h in range(2)]
            z = {c: _dot_nt(heads[c[1]][1], ks[c[0]]) for c in chains}
            lb, lk = {}, {}
            for c in chains:
                lb[c], lk[c], _ = _sb_logs(z[c])
                if diag:
                    lk[c] = jnp.where(causal, lk[c], 0.0)
            r = {c: _split_dot(lk[c], u_excl) for c in chains}
            carries = list(carries)
            w = {}
            for b, h in chains:
                w_c = jnp.exp(lb[b, h] + r[b, h] + carries[h])
                w[b, h] = (jnp.where(causal, w_c, 0.0) if diag else w_c).astype(BF16)
                carries[h] = carries[h] + (r[b, h][:, :1] + lk[b, h][:, :1])
            for b, h in chains:
                acc = acc + _dot(w[b, h], jnp.where(heads[h][0], vs[b], jnp.zeros_like(vs[b])))
            return tuple(carries), acc

        zero = jnp.zeros((SB_T, 1), F32)
        carries, acc = blocks([i], (zero, zero), jnp.zeros((SB_T, LANES), F32), True)
        carries, acc = lax.fori_loop(0, i // 2, lambda jj, c: blocks([i - 1 - 2 * jj, i - 2 - 2 * jj], c[0], c[1], False),
                                     (carries, acc))
        carries, acc = lax.fori_loop(0, i % 2, lambda jj, c: blocks([0], c[0], c[1], False), (carries, acc))
        o_ref[...] = acc.astype(BF16)
        o32_ref[...] = acc
        if n_late:
            pl.when(step == n_steps - 1)(finish)

    q_spec, k_spec, v_spec = _sb_specs(n_pairs, 0)
    blk = pl.BlockSpec((None, SB_T, LANES), lambda b, p, i: (b, i, p))
    out = pl.pallas_call(
        body,
        grid=(bl, n_pairs, nq),
        in_specs=[q_spec, k_spec, v_spec] + [ANY] * n_late,
        out_specs=[blk, blk] + [ANY] * n_late,
        out_shape=[jax.ShapeDtypeStruct((bl, SEQ, SB_W), BF16), jax.ShapeDtypeStruct((bl, SEQ, SB_W), F32)]
        + _gather_out_shapes(late_shards),
        scratch_shapes=_gather_sems(n_late) if n_late else [],
        compiler_params=pltpu.CompilerParams(dimension_semantics=("arbitrary", "arbitrary", "arbitrary")),
        name="sb_fwd",
    )(proj3, proj3, proj3, *late_shards)
    return out[0], out[1], out[2:]


def _sb_bwd(proj3, o_a, do_a, parts):
    bl = proj3.shape[0]
    n_pairs = SB_W // LANES
    nq = SEQ // SB_T
    n_parts = len(parts)
    n_steps = bl * n_pairs * nq

    def body(q_ref, k_ref, v_ref, o_ref, do_ref, *rest):
        p_refs, (dq_ref, dk_ref, dv_ref), land_refs = rest[:n_parts], rest[n_parts:n_parts + 3], rest[n_parts + 3:2 * n_parts + 3]
        dk_acc, dv_acc = rest[2 * n_parts + 3:2 * n_parts + 5]
        step = _grid_step(n_pairs, nq)
        if n_parts:
            send, finish = _chip_exchange_phases(p_refs, land_refs, *rest[2 * n_parts + 5:])
            pl.when(step == 0)(send)
        i = pl.program_id(2)

        @pl.when(i == 0)
        def _():
            dk_acc[...] = jnp.zeros_like(dk_acc)
            dv_acc[...] = jnp.zeros_like(dv_acc)

        row, col, lane = _sb_masks()
        causal = col < row
        u_excl = (row > col).astype(BF16)
        u_incl = (row >= col).astype(BF16)
        q = q_ref[...]
        do = do_ref[...]
        prod = do.astype(F32) * o_ref[...]
        heads = []
        for h in range(2):
            mh = (lane // HEAD_DIM) == h
            d_tot = jnp.sum(jnp.where(mh, prod, 0.0), axis=1, keepdims=True)
            heads.append((mh, jnp.where(mh, q, jnp.zeros_like(q)) * SB_SCALE, jnp.where(mh, do, jnp.zeros_like(do)), d_tot))

        def blocks(js, carries, c_das, dq, diag):
            starts = [pl.multiple_of(j * SB_T, SB_T) for j in js]
            ks = [k_ref[pl.ds(s, SB_T), :] for s in starts]
            vs = [v_ref[pl.ds(s, SB_T), :] for s in starts]
            chains = [(b, h) for b in range(len(js)) for h in range(2)]
            z = {c: _dot_nt(heads[c[1]][1], ks[c[0]]) for c in chains}
            dw = {c: _dot_nt(heads[c[1]][2], vs[c[0]]) for c in chains}
            lb, lk, e = {}, {}, {}
            for c in chains:
                lb[c], lk[c], e[c] = _sb_logs(z[c])
                if diag:
                    lk[c] = jnp.where(causal, lk[c], 0.0)
            r = {c: _split_dot(lk[c], u_excl) for c in chains}
            carries, c_das = list(carries), list(c_das)
            wb, da = {}, {}
            for b, h in chains:
                w_c = jnp.exp(lb[b, h] + r[b, h] + carries[h])
                wb[b, h] = (jnp.where(causal, w_c, 0.0) if diag else w_c).astype(BF16)
                da[b, h] = dw[b, h] * wb[b, h].astype(F32)
                carries[h] = carries[h] + (r[b, h][:, :1] + lk[b, h][:, :1])
            sfx = {c: _split_dot(da[c], u_incl) for c in chains}
            dz = {}
            for b, h in chains:
                dlk = heads[h][3] - c_das[h] - sfx[b, h]
                if diag:
                    dlk = jnp.where(causal, dlk, 0.0)
                c_das[h] = c_das[h] + sfx[b, h][:, :1]
                inv = 1.0 / (1.0 + e[b, h])
                pos = z[b, h] >= 0.0
                beta = jnp.where(pos, inv, e[b, h] * inv)
                one_m_beta = jnp.where(pos, e[b, h] * inv, inv)
                dz[b, h] = (da[b, h] * one_m_beta - dlk * beta).astype(BF16)
            for b, h in chains:
                dq = dq + _dot(dz[b, h], jnp.where(heads[h][0], ks[b], jnp.zeros_like(ks[b])))
            for b in range(len(js)):
                dk_acc[pl.ds(starts[b], SB_T), :] += _dot_tn(dz[b, 0], heads[0][1]) + _dot_tn(dz[b, 1], heads[1][1])
                dv_acc[pl.ds(starts[b], SB_T), :] += _dot_tn(wb[b, 0], heads[0][2]) + _dot_tn(wb[b, 1], heads[1][2])
            return tuple(carries), tuple(c_das), dq

        zero = jnp.zeros((SB_T, 1), F32)
        state = blocks([i], (zero, zero), (zero, zero), jnp.zeros((SB_T, LANES), F32), True)
        state = lax.fori_loop(0, i // 2, lambda jj, c: blocks([i - 1 - 2 * jj, i - 2 - 2 * jj], c[0], c[1], c[2], False), state)
        state = lax.fori_loop(0, i % 2, lambda jj, c: blocks([0], c[0], c[1], c[2], False), state)
        dq_ref[...] = (state[2] * SB_SCALE).astype(BF16)

        @pl.when(i == nq - 1)
        def _():
            dk_ref[...] = dk_acc[...].astype(BF16)
            dv_ref[...] = dv_acc[...].astype(BF16)

        if n_parts:
            pl.when(step == n_steps - 1)(finish)

    q_spec, k_spec, v_spec = _sb_specs(n_pairs, 0)
    blk = pl.BlockSpec((None, SB_T, LANES), lambda b, p, i: (b, i, p))
    seq = pl.BlockSpec((None, SEQ, LANES), lambda b, p, i: (b, 0, p))
    shape = jax.ShapeDtypeStruct((bl, SEQ, SB_W), BF16)
    out = pl.pallas_call(
        body,
        grid=(bl, n_pairs, nq),
        in_specs=[q_spec, k_spec, v_spec, blk, blk] + [ANY] * n_parts,
        out_specs=[blk, seq, seq] + [ANY] * n_parts,
        out_shape=[shape, shape, shape] + [jax.ShapeDtypeStruct(p.shape, p.dtype) for p in parts],
        scratch_shapes=[pltpu.VMEM((SEQ, LANES), F32), pltpu.VMEM((SEQ, LANES), F32)]
        + (_chip_exchange_sems(n_parts) if n_parts else []),
        compiler_params=pltpu.CompilerParams(dimension_semantics=("arbitrary", "arbitrary", "arbitrary")),
        name="sb_bwd",
    )(proj3, proj3, proj3, o_a, do_a, *parts)
    return out[0], out[1], out[2], out[3:]


BAND = 128


BAND_CH = 4
BAND_HEADS = DIL_W // HEAD_DIM


def _swap_half(x):
    n = x.shape[-1]
    lane = lax.broadcasted_iota(jnp.int32, (1, n), 1)
    return jnp.where((lane % HEAD_DIM) < HEAD_DIM // 2, pltpu.roll(x, n - HEAD_DIM // 2, 1), pltpu.roll(x, HEAD_DIM // 2, 1))


def _rope(x, cos, sin_signed):
    x = x.astype(F32)
    return x * cos + _swap_half(x) * sin_signed


def _band_valid(g, blk):
    nb = jnp.where(g == 0, 16, jnp.where(g == 1, 4, 1))
    first_key = jnp.where(lax.rem(blk, nb) != 0, 0, BAND)
    qi = lax.broadcasted_iota(jnp.int32, (BAND, 2 * BAND), 0) + BAND
    kj = lax.broadcasted_iota(jnp.int32, (BAND, 2 * BAND), 1)
    dist = qi - kj
    return (dist >= 0) & (dist <= BAND) & (kj >= first_key)


def _band_specs():
    last_before = lambda i: jnp.maximum(i * BAND_CH - 1, 0)
    cur = lambda col: pl.BlockSpec((None, BAND_CH, BAND, DIL_W), lambda g, i: (g, i, 0, col))
    prev = lambda col: pl.BlockSpec((None, None, BAND, DIL_W), lambda g, i: (g, last_before(i), 0, col))
    tab = pl.BlockSpec((None, BAND_CH, BAND, DIL_W), lambda g, i: (g, lax.rem(i, 16 // BAND_CH), 0, 0))
    tab_prev = pl.BlockSpec((None, None, BAND, DIL_W), lambda g, i: (g, lax.rem(last_before(i), 16), 0, 0))
    return cur, prev, tab, tab_prev


def _band_load(q_ref, k_ref, kp_ref, v_ref, vp_ref, c_ref, s_ref, cp_ref, sp_ref):
    qs = [(_rope(q_ref[b], c_ref[b], s_ref[b]) * SB_SCALE).astype(BF16) for b in range(BAND_CH)]
    ks = [_rope(kp_ref[...], cp_ref[...], sp_ref[...]).astype(BF16)]
    ks += [_rope(k_ref[b], c_ref[b], s_ref[b]).astype(BF16) for b in range(BAND_CH)]
    vs = [vp_ref[...]] + [v_ref[b] for b in range(BAND_CH)]
    k2 = [jnp.concatenate([ks[b], ks[b + 1]], axis=0) for b in range(BAND_CH)]
    v2 = [jnp.concatenate([vs[b], vs[b + 1]], axis=0) for b in range(BAND_CH)]
    return qs, k2, v2


def _band_fwd(qkv_s, cos_t, sin_t):
    def body(q_ref, k_ref, kp_ref, v_ref, vp_ref, c_ref, s_ref, cp_ref, sp_ref, ol_ref):
        g, i = pl.program_id(0), pl.program_id(1)
        qs, k2, v2 = _band_load(q_ref, k_ref, kp_ref, v_ref, vp_ref, c_ref, s_ref, cp_ref, sp_ref)
        lane = lax.broadcasted_iota(jnp.int32, (1, DIL_W), 1)
        for b in range(BAND_CH):
            valid = _band_valid(g, i * BAND_CH + b)
            hs = range(BAND_HEADS)
            mh = [(lane // HEAD_DIM) == h for h in hs]
            s = [jnp.where(valid, _dot_nt(jnp.where(mh[h], qs[b], jnp.zeros_like(qs[b])), k2[b]), NEG_INF) for h in hs]
            m = [jnp.max(s[h], axis=1, keepdims=True) for h in hs]
            p = [jnp.exp(s[h] - m[h]) for h in hs]
            den = [jnp.sum(p[h], axis=1, keepdims=True) for h in hs]
            pv = [_dot(p[h].astype(BF16), jnp.where(mh[h], v2[b], jnp.zeros_like(v2[b]))) for h in hs]
            o = jnp.zeros((BAND, DIL_W), F32)
            lse = jnp.zeros((BAND, DIL_W), F32)
            for h in hs:
                o = o + pv[h] * (1.0 / den[h])
                lse = jnp.where(mh[h], m[h] + jnp.log(den[h]), lse)
            ol_ref[b, :, :DIL_W] = o
            ol_ref[b, :, DIL_W:] = lse

    cur, prev, tab, tab_prev = _band_specs()
    n_blk = qkv_s.shape[1]
    return pl.pallas_call(
        body,
        grid=(3, n_blk // BAND_CH),
        in_specs=[cur(0), cur(1), prev(1), cur(2), prev(2), tab, tab, tab_prev, tab_prev],
        out_specs=pl.BlockSpec((None, BAND_CH, BAND, 2 * DIL_W), lambda g, i: (g, i, 0, 0)),
        out_shape=jax.ShapeDtypeStruct((3, n_blk, BAND, 2 * DIL_W), F32),
        compiler_params=pltpu.CompilerParams(dimension_semantics=("parallel", "parallel")),
        name="band_fwd",
    )(qkv_s, qkv_s, qkv_s, qkv_s, qkv_s, cos_t, sin_t, cos_t, sin_t)


def _band_bwd(qkv_s, cos_t, sin_t, dcat_s):
    def body(q_ref, k_ref, kp_ref, v_ref, vp_ref, c_ref, s_ref, cp_ref, sp_ref, do_ref, lse_ref, dl_ref,
             dq_ref, dk_ref, dv_ref, dkf_ref, dvf_ref):
        g, i = pl.program_id(0), pl.program_id(1)
        qs, k2, v2 = _band_load(q_ref, k_ref, kp_ref, v_ref, vp_ref, c_ref, s_ref, cp_ref, sp_ref)
        lane = lax.broadcasted_iota(jnp.int32, (1, DIL_W), 1)
        dks, dvs = [], []
        for b in range(BAND_CH):
            valid = _band_valid(g, i * BAND_CH + b)
            do, lse, dl = do_ref[b].astype(BF16), lse_ref[b], dl_ref[b]
            hs = range(BAND_HEADS)
            mh = [(lane // HEAD_DIM) == h for h in hs]
            qh = [jnp.where(mh[h], qs[b], jnp.zeros_like(qs[b])) for h in hs]
            doh = [jnp.where(mh[h], do, jnp.zeros_like(do)) for h in hs]
            s = [_dot_nt(qh[h], k2[b]) for h in hs]
            dp = [_dot_nt(doh[h], v2[b]) for h in hs]
            p = [jnp.where(valid, jnp.exp(s[h] - lse[:, h * HEAD_DIM:h * HEAD_DIM + 1]), 0.0) for h in hs]
            ds = [(p[h] * (dp[h] - dl[:, h * HEAD_DIM:h * HEAD_DIM + 1])).astype(BF16) for h in hs]
            pb = [p[h].astype(BF16) for h in hs]
            dq = sum(_dot(ds[h], jnp.where(mh[h], k2[b], jnp.zeros_like(k2[b]))) for h in hs)
            dk2 = sum(_dot_tn(ds[h], qh[h]) for h in hs)
            dv2 = sum(_dot_tn(pb[h], doh[h]) for h in hs)
            dq_ref[b] = dq * SB_SCALE
            dks.append(dk2)
            dvs.append(dv2)
        dkf_ref[...] = dks[0][:BAND]
        dvf_ref[...] = dvs[0][:BAND]
        for b in range(BAND_CH):
            last = b == BAND_CH - 1
            dk_ref[b] = dks[b][BAND:] if last else dks[b][BAND:] + dks[b + 1][:BAND]
            dv_ref[b] = dvs[b][BAND:] if last else dvs[b][BAND:] + dvs[b + 1][:BAND]

    cur, prev, tab, tab_prev = _band_specs()
    first = pl.BlockSpec((None, None, BAND, DIL_W), lambda g, i: (g, i, 0, 0))
    n_blk = qkv_s.shape[1]
    n_chunks = n_blk // BAND_CH
    shape = jax.ShapeDtypeStruct((3, n_blk, BAND, DIL_W), F32)
    shape_first = jax.ShapeDtypeStruct((3, n_chunks, BAND, DIL_W), F32)
    return pl.pallas_call(
        body,
        grid=(3, n_chunks),
        in_specs=[cur(0), cur(1), prev(1), cur(2), prev(2), tab, tab, tab_prev, tab_prev, cur(0), cur(1), cur(2)],
        out_specs=[cur(0), cur(0), cur(0), first, first],
        out_shape=[shape, shape, shape, shape_first, shape_first],
        compiler_params=pltpu.CompilerParams(dimension_semantics=("parallel", "parallel")),
        name="band_bwd",
    )(qkv_s, qkv_s, qkv_s, qkv_s, qkv_s, cos_t, sin_t, cos_t, sin_t, dcat_s, dcat_s, dcat_s)


def _band_combine(dq, dk, dv, dk_first, dv_first, cos_t, sin_t):
    n_chunks = dk_first.shape[1]

    def body(dq_ref, dk_ref, dkn_ref, dv_ref, dvn_ref, c_ref, s_ref, out_ref):
        nxt = (pl.program_id(1) < n_chunks - 1).astype(F32)
        for b in range(BAND_CH):
            cos, sin = c_ref[b], s_ref[b]
            dq_b, dk_b, dv_b = dq_ref[b], dk_ref[b], dv_ref[b]
            if b == BAND_CH - 1:
                dk_b = dk_b + nxt * dkn_ref[...]
                dv_b = dv_b + nxt * dvn_ref[...]
            out_ref[b, :, :DIL_W] = (dq_b * cos - _swap_half(dq_b) * sin).astype(BF16)
            out_ref[b, :, DIL_W:2 * DIL_W] = (dk_b * cos - _swap_half(dk_b) * sin).astype(BF16)
            out_ref[b, :, 2 * DIL_W:] = dv_b.astype(BF16)

    cur, _, tab, _ = _band_specs()
    nxt = pl.BlockSpec((None, None, BAND, DIL_W), lambda g, i: (g, jnp.minimum(i + 1, n_chunks - 1), 0, 0))
    return pl.pallas_call(
        body,
        grid=(3, n_chunks),
        in_specs=[cur(0), cur(0), nxt, cur(0), nxt, tab, tab],
        out_specs=pl.BlockSpec((None, BAND_CH, BAND, 3 * DIL_W), lambda g, i: (g, i, 0, 0)),
        out_shape=jax.ShapeDtypeStruct(dq.shape[:3] + (3 * DIL_W,), BF16),
        compiler_params=pltpu.CompilerParams(dimension_semantics=("parallel", "parallel")),
        name="band_combine",
    )(dq, dk, dk_first, dv, dv_first, cos_t, sin_t)


def _band_merge(ol):
    t, tm = ol.shape[1], 512

    def body(o_ref, l_ref, ob_ref, lse_ref):
        l0, l1, l2 = l_ref[0], l_ref[1], l_ref[2]
        m = jnp.maximum(jnp.maximum(l0, l1), l2)
        lse = m + jnp.log(jnp.exp(l0 - m) + jnp.exp(l1 - m) + jnp.exp(l2 - m))
        ob_ref[...] = (jnp.exp(l0 - lse) * o_ref[0] + jnp.exp(l1 - lse) * o_ref[1] + jnp.exp(l2 - lse) * o_ref[2]).astype(BF16)
        lse_ref[...] = lse

    spec = pl.BlockSpec((tm, DIL_W), lambda i: (i, 0))
    return pl.pallas_call(
        body,
        grid=(t // tm,),
        in_specs=[pl.BlockSpec((3, tm, DIL_W), lambda i: (0, i, 0)), pl.BlockSpec((3, tm, DIL_W), lambda i: (0, i, 1))],
        out_specs=[spec, spec],
        out_shape=[jax.ShapeDtypeStruct((t, DIL_W), BF16), jax.ShapeDtypeStruct((t, DIL_W), F32)],
        compiler_params=pltpu.CompilerParams(dimension_semantics=("parallel",)),
        name="band_merge",
    )(ol, ol)


def _band_delta(do_b, o_b, lse_b):
    def fn(do, o, lse):
        r = lax.broadcasted_iota(jnp.int32, (DIL_W, DIL_W), 0) // HEAD_DIM
        c = lax.broadcasted_iota(jnp.int32, (DIL_W, DIL_W), 1) // HEAD_DIM
        do = do.astype(F32)
        delta = _split_dot(do * o.astype(F32), (r == c).astype(BF16))
        return (jnp.concatenate([do, lse, delta], axis=1),)

    return _rowwise(fn, [(do_b, "row"), (o_b, "row"), (lse_b, "row")], [(3 * DIL_W, F32, "row")], tm=512, name="band_delta")[0]


def _band_masks():
    qi = lax.broadcasted_iota(jnp.int32, (BAND, 2 * BAND), 0) + BAND
    kj = lax.broadcasted_iota(jnp.int32, (BAND, 2 * BAND), 1)
    dist = qi - kj
    row = lax.broadcasted_iota(jnp.int32, (BAND, BAND), 0)
    col = lax.broadcasted_iota(jnp.int32, (BAND, BAND), 1)
    return col <= row, (dist >= 0) & (dist <= BAND)


def _band_attend(q, k, v, valid):
    lane = lax.broadcasted_iota(jnp.int32, (1, DIL_W), 1)
    hs = range(BAND_HEADS)
    mh = [(lane // HEAD_DIM) == h for h in hs]
    s = [jnp.where(valid, _dot_nt(jnp.where(mh[h], q, jnp.zeros_like(q)), k), NEG_INF) for h in hs]
    m = [jnp.max(s[h], axis=1, keepdims=True) for h in hs]
    p = [jnp.exp(s[h] - m[h]) for h in hs]
    den = [jnp.sum(p[h], axis=1, keepdims=True) for h in hs]
    pv = [_dot(p[h].astype(BF16), jnp.where(mh[h], v, jnp.zeros_like(v))) for h in hs]
    o = jnp.zeros((BAND, DIL_W), F32)
    lse = jnp.zeros((BAND, DIL_W), F32)
    for h in hs:
        o = o + pv[h] * (1.0 / den[h])
        lse = jnp.where(mh[h], m[h] + jnp.log(den[h]), lse)
    return o, lse


def _band_attend_bwd(q, k, v, valid, do, lse, dl):
    lane = lax.broadcasted_iota(jnp.int32, (1, DIL_W), 1)
    hs = range(BAND_HEADS)
    mh = [(lane // HEAD_DIM) == h for h in hs]
    qh = [jnp.where(mh[h], q, jnp.zeros_like(q)) for h in hs]
    doh = [jnp.where(mh[h], do, jnp.zeros_like(do)) for h in hs]
    s = [_dot_nt(qh[h], k) for h in hs]
    dp = [_dot_nt(doh[h], v) for h in hs]
    p = [jnp.where(valid, jnp.exp(s[h] - lse[:, h * HEAD_DIM:h * HEAD_DIM + 1]), 0.0) for h in hs]
    ds = [(p[h] * (dp[h] - dl[:, h * HEAD_DIM:h * HEAD_DIM + 1])).astype(BF16) for h in hs]
    pb = [p[h].astype(BF16) for h in hs]
    dq = sum(_dot(ds[h], jnp.where(mh[h], k, jnp.zeros_like(k))) for h in hs)
    dk = sum(_dot_tn(ds[h], qh[h]) for h in hs)
    dv = sum(_dot_tn(pb[h], doh[h]) for h in hs)
    return dq, dk, dv


def _band_group_specs(lead, rows, cls, col0):
    def spec(width):
        if lead == "rows":
            return pl.BlockSpec((None, rows, width), lambda b, i: (b, 0, col0))
        return pl.BlockSpec((None, rows, cls * width), lambda b, i: (b, 0, i))
    return spec


def _band_group_fwd(a, cos_g, sin_g, *, rows, cls, steps, col0, name):
    bl = a.shape[0]
    nb = rows // BAND
    grp_w = 3 * DIL_W

    def body(a_ref, c_ref, s_ref, ol_ref, qr, kr):
        first_valid, later_valid = _band_masks()
        for j in range(cls):
            a0, t0, o0 = j * grp_w, j * DIL_W, j * 2 * DIL_W
            cos, sin = c_ref[:, t0:t0 + DIL_W], s_ref[:, t0:t0 + DIL_W]
            qr[...] = (_rope(a_ref[:, a0:a0 + DIL_W], cos, sin) * SB_SCALE).astype(BF16)
            kr[...] = _rope(a_ref[:, a0 + DIL_W:a0 + 2 * DIL_W], cos, sin).astype(BF16)

            def block(q0, k0, keys, valid, a0=a0, o0=o0):
                o, lse = _band_attend(qr[pl.ds(q0, BAND), :], kr[pl.ds(k0, keys), :],
                                      a_ref[pl.ds(k0, keys), a0 + 2 * DIL_W:a0 + grp_w], valid)
                ol_ref[pl.ds(q0, BAND), o0:o0 + DIL_W] = o
                ol_ref[pl.ds(q0, BAND), o0 + DIL_W:o0 + 2 * DIL_W] = lse

            block(0, 0, BAND, first_valid)
            if nb > 1:
                def later(b, carry, block=block):
                    block(pl.multiple_of(b * BAND, BAND), pl.multiple_of((b - 1) * BAND, BAND), 2 * BAND, later_valid)
                    return carry

                lax.fori_loop(1, nb, later, 0)

    lead = "rows" if col0 is not None else "cols"
    spec = _band_group_specs(lead, rows, cls, col0)
    tab = pl.BlockSpec((rows, cls * DIL_W), lambda b, i: (0, i))
    n_cls = cos_g.shape[1] // DIL_W
    return pl.pallas_call(
        body,
        grid=(bl, steps),
        in_specs=[spec(grp_w), tab, tab],
        out_specs=pl.BlockSpec((None, rows, cls * 2 * DIL_W), lambda b, i: (b, 0, i)),
        out_shape=jax.ShapeDtypeStruct((bl, rows, n_cls * 2 * DIL_W), F32),
        scratch_shapes=[pltpu.VMEM((rows, DIL_W), BF16), pltpu.VMEM((rows, DIL_W), BF16)],
        compiler_params=pltpu.CompilerParams(dimension_semantics=("parallel", "parallel")),
        name=name,
    )(a, cos_g, sin_g)


def _band_group_bwd(a, d, cos_g, sin_g, *, rows, cls, steps, col0, name):
    bl = a.shape[0]
    nb = rows // BAND
    grp_w = 3 * DIL_W

    def body(a_ref, d_ref, c_ref, s_ref, out_ref, qr, kr, dk_acc, dv_acc):
        first_valid, later_valid = _band_masks()
        for j in range(cls):
            a0, t0 = j * grp_w, j * DIL_W
            cos, sin = c_ref[:, t0:t0 + DIL_W], s_ref[:, t0:t0 + DIL_W]
            qr[...] = (_rope(a_ref[:, a0:a0 + DIL_W], cos, sin) * SB_SCALE).astype(BF16)
            kr[...] = _rope(a_ref[:, a0 + DIL_W:a0 + 2 * DIL_W], cos, sin).astype(BF16)
            dk_acc[...] = jnp.zeros_like(dk_acc)
            dv_acc[...] = jnp.zeros_like(dv_acc)

            def block(q0, k0, keys, valid, a0=a0, t0=t0):
                qrows, krows = pl.ds(q0, BAND), pl.ds(k0, keys)
                dq, dk, dv = _band_attend_bwd(
                    qr[qrows, :], kr[krows, :], a_ref[krows, a0 + 2 * DIL_W:a0 + grp_w], valid,
                    d_ref[qrows, a0:a0 + DIL_W].astype(BF16), d_ref[qrows, a0 + DIL_W:a0 + 2 * DIL_W],
                    d_ref[qrows, a0 + 2 * DIL_W:a0 + grp_w])
                dq = dq * SB_SCALE
                out_ref[qrows, a0:a0 + DIL_W] = (dq * c_ref[qrows, t0:t0 + DIL_W]
                                                 - _swap_half(dq) * s_ref[qrows, t0:t0 + DIL_W]).astype(BF16)
                dk_acc[krows, :] += dk
                dv_acc[krows, :] += dv

            block(0, 0, BAND, first_valid)
            if nb > 1:
                def later(b, carry, block=block):
                    block(pl.multiple_of(b * BAND, BAND), pl.multiple_of((b - 1) * BAND, BAND), 2 * BAND, later_valid)
                    return carry

                lax.fori_loop(1, nb, later, 0)
            dk = dk_acc[...]
            out_ref[:, a0 + DIL_W:a0 + 2 * DIL_W] = (dk * cos - _swap_half(dk) * sin).astype(BF16)
            out_ref[:, a0 + 2 * DIL_W:a0 + grp_w] = dv_acc[...].astype(BF16)

    lead = "rows" if col0 is not None else "cols"
    spec = _band_group_specs(lead, rows, cls, col0)
    dspec = _band_group_specs(lead, rows, cls, 0 if col0 is not None else None)
    tab = pl.BlockSpec((rows, cls * DIL_W), lambda b, i: (0, i))
    n_cls = cos_g.shape[1] // DIL_W
    return pl.pallas_call(
        body,
        grid=(bl, steps),
        in_specs=[spec(grp_w), dspec(grp_w), tab, tab],
        out_specs=pl.BlockSpec((None, rows, cls * grp_w), lambda b, i: (b, 0, i)),
        out_shape=jax.ShapeDtypeStruct((bl, rows, n_cls * grp_w), BF16),
        scratch_shapes=[pltpu.VMEM((rows, DIL_W), BF16), pltpu.VMEM((rows, DIL_W), BF16),
                        pltpu.VMEM((rows, DIL_W), F32), pltpu.VMEM((rows, DIL_W), F32)],
        compiler_params=pltpu.CompilerParams(dimension_semantics=("parallel", "parallel")),
        name=name,
    )(a, d, cos_g, sin_g)


def _band_merge3(ols):
    t, tm = ols[0].shape[0], 512

    def body(o0, l0, o1, l1, o2, l2, ob_ref, lse_ref):
        a, b, c = l0[...], l1[...], l2[...]
        m = jnp.maximum(jnp.maximum(a, b), c)
        lse = m + jnp.log(jnp.exp(a - m) + jnp.exp(b - m) + jnp.exp(c - m))
        ob_ref[...] = (jnp.exp(a - lse) * o0[...] + jnp.exp(b - lse) * o1[...] + jnp.exp(c - lse) * o2[...]).astype(BF16)
        lse_ref[...] = lse

    spec = pl.BlockSpec((tm, DIL_W), lambda i: (i, 0))
    spec_l = pl.BlockSpec((tm, DIL_W), lambda i: (i, 1))
    return pl.pallas_call(
        body,
        grid=(t // tm,),
        in_specs=[spec, spec_l] * 3,
        out_specs=[spec, spec],
        out_shape=[jax.ShapeDtypeStruct((t, DIL_W), BF16), jax.ShapeDtypeStruct((t, DIL_W), F32)],
        compiler_params=pltpu.CompilerParams(dimension_semantics=("parallel",)),
        name="band_merge",
    )(ols[0], ols[0], ols[1], ols[1], ols[2], ols[2])


MEM_T = 512
MEM_SCALE = 128 ** -0.5
MEM_Q_COL = (D_IN - MEM_W) // LANES


def _mem_specs():
    q = pl.BlockSpec((None, MEM_T, LANES), lambda b, h, i: (b, i, MEM_Q_COL + h))
    k = pl.BlockSpec((None, MEM_LEN, LANES), lambda b, h, i: (b, 0, h))
    v = pl.BlockSpec((None, MEM_LEN, LANES), lambda b, h, i: (b, 0, MEM_W // LANES + h))
    blk = pl.BlockSpec((None, MEM_T, LANES), lambda b, h, i: (b, i, h))
    return q, k, v, blk


def _mem_probs(q, k):
    s = _dot_nt(q, k) * MEM_SCALE
    p = jnp.exp(s - jnp.max(s, axis=1, keepdims=True))
    return p * (1.0 / jnp.sum(p, axis=1, keepdims=True))


def _mem_fwd(proj3, kv3):
    bl = proj3.shape[0]

    def body(q_ref, k_ref, v_ref, o_ref):
        p = _mem_probs(q_ref[...], k_ref[...])
        o_ref[...] = _dot(p.astype(BF16), v_ref[...]).astype(BF16)

    q, k, v, blk = _mem_specs()
    return pl.pallas_call(
        body,
        grid=(bl, MEM_W // LANES, SEQ // MEM_T),
        in_specs=[q, k, v],
        out_specs=blk,
        out_shape=jax.ShapeDtypeStruct((bl, SEQ, MEM_W), BF16),
        compiler_params=pltpu.CompilerParams(dimension_semantics=("parallel", "parallel", "parallel")),
        name="mem_fwd",
    )(proj3, kv3, kv3)


def _mem_bwd(proj3, kv3, do_c):
    bl = proj3.shape[0]

    def body(q_ref, k_ref, v_ref, do_ref, dq_ref, dk_ref, dv_ref):
        @pl.when(pl.program_id(2) == 0)
        def _():
            dk_ref[...] = jnp.zeros_like(dk_ref)
            dv_ref[...] = jnp.zeros_like(dv_ref)

        q, k, do = q_ref[...], k_ref[...], do_ref[...]
        p = _mem_probs(q, k)
        dp = _dot_nt(do, v_ref[...])
        ds = (p * (dp - jnp.sum(p * dp, axis=1, keepdims=True)) * MEM_SCALE).astype(BF16)
        dq_ref[...] = _dot(ds, k).astype(BF16)
        dk_ref[...] += _dot_tn(ds, q)
        dv_ref[...] += _dot_tn(p.astype(BF16), do)

    q, k, v, blk = _mem_specs()
    kv_out = pl.BlockSpec((None, MEM_LEN, LANES), lambda b, h, i: (b, 0, h))
    return pl.pallas_call(
        body,
        grid=(bl, MEM_W // LANES, SEQ // MEM_T),
        in_specs=[q, k, v, blk],
        out_specs=[blk, kv_out, kv_out],
        out_shape=[jax.ShapeDtypeStruct((bl, SEQ, MEM_W), BF16), jax.ShapeDtypeStruct((bl, MEM_LEN, MEM_W), F32),
                   jax.ShapeDtypeStruct((bl, MEM_LEN, MEM_W), F32)],
        compiler_params=pltpu.CompilerParams(dimension_semantics=("parallel", "parallel", "arbitrary")),
        name="mem_bwd",
    )(proj3, kv3, kv3, do_c)


def _place():
    x, y, c = lax.axis_index("x"), lax.axis_index("y"), lax.axis_index("c")
    return x, y, c


def _other_chips(x, y):
    return [(1 - x, y), (x, 1 - y), (1 - x, 1 - y)]


def _remote(src, dst, send_sem, recv_sem, to):
    return pltpu.make_async_remote_copy(src_ref=src, dst_ref=dst, send_sem=send_sem, recv_sem=recv_sem,
                                        device_id=to, device_id_type=MESH)


ANY = pl.BlockSpec(memory_space=pl.ANY)


def _gather_weights(shards):
    n = len(shards)

    def body(*refs):
        send, forward, finish = _gather_phases(refs[:n], refs[n:2 * n], *refs[2 * n:])
        send()
        forward()
        finish()

    return pl.pallas_call(
        body,
        in_specs=[ANY] * n,
        out_specs=[ANY] * n,
        out_shape=_gather_out_shapes(shards),
        scratch_shapes=_gather_sems(n),
        name="gather_weights",
    )(*shards)


def _gather_out_shapes(shards):
    return [jax.ShapeDtypeStruct((N_CHIPS,) + s.shape, s.dtype) for s in shards]


def _gather_sems(n):
    return [pltpu.SemaphoreType.DMA((6 * n,)), pltpu.SemaphoreType.DMA((6 * n,))]


def _gather_phases(in_refs, out_refs, send_sems, recv_sems):
    x, y, c = _place()
    sibling = (x, y, 1 - c)
    chips = _other_chips(x, y)
    first, passed = [], []
    for k in range(len(in_refs)):
        hf = in_refs[k].shape[0] // 2

        def half(px, py, pc, k=k, hf=hf):
            return out_refs[k].at[2 * px + py, pl.ds(pc * hf, hf), :]

        src = in_refs[k].at[pl.ds(c * hf, hf), :]
        for j, chip in enumerate(chips):
            s = 6 * k + j
            first.append(_remote(src, half(x, y, c), send_sems.at[s], recv_sems.at[s], (*chip, c)))
            passed.append((_remote(src, half(*chip, c), send_sems.at[s], recv_sems.at[s], (*chip, c)),
                           _remote(half(*chip, c), half(*chip, c), send_sems.at[s + 3], recv_sems.at[s + 3], sibling),
                           _remote(src, half(*chip, 1 - c), send_sems.at[s + 3], recv_sems.at[s + 3], sibling)))

    def send():
        for cp in first:
            cp.start()

    def forward():
        for landed, fwd, _ in passed:
            landed.wait_recv()
            fwd.start()

    def finish():
        for _, _, from_sibling in passed:
            from_sibling.wait_recv()
        for cp in first:
            cp.wait_send()
        for _, fwd, _ in passed:
            fwd.wait_send()

    return send, forward, finish


def _pair_exchange(grads, *, name):
    n = len(grads)

    def body(*refs):
        g_refs, land_refs = refs[:n], refs[n:2 * n]
        send_sems, recv_sems = refs[2 * n:]
        x, y, c = _place()
        cps = []
        for k in range(n):
            hf = g_refs[k].shape[1] // 2
            src = g_refs[k].at[:, pl.ds((1 - c) * hf, hf), :]
            cps.append(_remote(src, land_refs[k], send_sems.at[k], recv_sems.at[k], (x, y, 1 - c)))
        for cp in cps:
            cp.start()
        for cp in cps:
            cp.wait()

    return pl.pallas_call(
        body,
        in_specs=[ANY] * n,
        out_specs=[ANY] * n,
        out_shape=[jax.ShapeDtypeStruct((N_CHIPS, g.shape[1] // 2, g.shape[2]), F32) for g in grads],
        scratch_shapes=[pltpu.SemaphoreType.DMA((n,)), pltpu.SemaphoreType.DMA((n,))],
        name=name,
    )(*grads)


def _pair_add(g, land, c_arr, *, name):
    _, a, b = g.shape
    hf = a // 2

    def body(c_ref, g_ref, l_ref, o_ref):
        o_ref[...] = (g_ref[...] + l_ref[...]).astype(BF16)

    return pl.pallas_call(
        body,
        grid_spec=pltpu.PrefetchScalarGridSpec(
            num_scalar_prefetch=1,
            grid=(N_CHIPS,),
            in_specs=[pl.BlockSpec((None, None, hf, b), lambda s, c_ref: (s, c_ref[0], 0, 0)),
                      pl.BlockSpec((None, hf, b), lambda s, c_ref: (s, 0, 0))],
            out_specs=pl.BlockSpec((None, hf, b), lambda s, c_ref: (s, 0, 0)),
        ),
        out_shape=jax.ShapeDtypeStruct((N_CHIPS, hf, b), BF16),
        compiler_params=pltpu.CompilerParams(dimension_semantics=("parallel",)),
        name=name,
    )(c_arr, g.reshape(N_CHIPS, 2, hf, b), land)


def _chip_exchange(parts):
    n = len(parts)

    def body(*refs):
        send, finish = _chip_exchange_phases(refs[:n], refs[n:2 * n], *refs[2 * n:])
        send()
        finish()

    return pl.pallas_call(
        body,
        in_specs=[ANY] * n,
        out_specs=[ANY] * n,
        out_shape=[jax.ShapeDtypeStruct(p.shape, p.dtype) for p in parts],
        scratch_shapes=_chip_exchange_sems(n),
        name="chip_exchange",
    )(*parts)


def _chip_exchange_sems(n):
    return [pltpu.SemaphoreType.DMA((3 * n,)), pltpu.SemaphoreType.DMA((3 * n,))]


def _chip_exchange_phases(p_refs, land_refs, send_sems, recv_sems):
    x, y, c = _place()
    me = 2 * x + y
    sends, recvs = [], []
    for k in range(len(p_refs)):
        for j, (cx, cy) in enumerate(_other_chips(x, y)):
            s = 3 * k + j
            sends.append(_remote(p_refs[k].at[2 * cx + cy], land_refs[k].at[me], send_sems.at[s], recv_sems.at[s], (cx, cy, c)))
            recvs.append(_remote(p_refs[k].at[me], land_refs[k].at[2 * cx + cy], send_sems.at[s], recv_sems.at[s], (cx, cy, c)))

    def send():
        for cp in sends:
            cp.start()

    def finish():
        for cp in recvs:
            cp.wait_recv()
        for cp in sends:
            cp.wait_send()

    return send, finish


def _chip_add(land, part, me_arr, *, name):
    _, r, b = land.shape

    def body(me_ref, p_ref, l1_ref, l2_ref, l3_ref, o_ref):
        o_ref[...] = ((p_ref[...].astype(F32) + l1_ref[...].astype(F32)) + l2_ref[...].astype(F32)) + l3_ref[...].astype(F32)

    tr = r // 2
    other = lambda j: pl.BlockSpec((None, tr, b), lambda i, me_ref: (jnp.bitwise_xor(me_ref[0], j), i, 0))
    return pl.pallas_call(
        body,
        grid_spec=pltpu.PrefetchScalarGridSpec(
            num_scalar_prefetch=1,
            grid=(r // tr,),
            in_specs=[pl.BlockSpec((None, tr, b), lambda i, me_ref: (me_ref[0], i, 0)), other(2), other(1), other(3)],
            out_specs=pl.BlockSpec((tr, b), lambda i, me_ref: (i, 0)),
        ),
        out_shape=jax.ShapeDtypeStruct((r, b), F32),
        compiler_params=pltpu.CompilerParams(dimension_semantics=("parallel",)),
        name=name,
    )(me_arr, part, land, land, land)


def _pair_share(halves):
    n = len(halves)

    def body(*refs):
        h_refs, out_refs = refs[:n], refs[n:2 * n]
        send_sems, recv_sems = refs[2 * n:]
        x, y, c = _place()
        cps = [_remote(h_refs[k], out_refs[k], send_sems.at[k], recv_sems.at[k], (x, y, 1 - c)) for k in range(n)]
        for cp in cps:
            cp.start()
        for cp in cps:
            cp.wait()

    return pl.pallas_call(
        body,
        in_specs=[ANY] * n,
        out_specs=[ANY] * n,
        out_shape=[jax.ShapeDtypeStruct(h.shape, F32) for h in halves],
        scratch_shapes=[pltpu.SemaphoreType.DMA((n,)), pltpu.SemaphoreType.DMA((n,))],
        name="pair_share",
    )(*halves)


def _all_sum_small(part):
    def body(p_ref, o_ref, slots, send_sems, recv_sems):
        x, y, c = _place()
        me = 4 * x + 2 * y + c
        slots[me] = p_ref[...]
        peers = [(x ^ dx, y ^ dy, c ^ dc) for dx in (0, 1) for dy in (0, 1) for dc in (0, 1)][1:]
        sends = [_remote(p_ref, slots.at[me], send_sems.at[k], recv_sems.at[k], peer) for k, peer in enumerate(peers)]
        for cp in sends:
            cp.start()
        for k, (px, py, pc) in enumerate(peers):
            _remote(p_ref, slots.at[4 * px + 2 * py + pc], send_sems.at[k], recv_sems.at[k], (px, py, pc)).wait_recv()
        for cp in sends:
            cp.wait_send()
        acc = slots[0]
        for d in range(1, 8):
            acc = acc + slots[d]
        o_ref[...] = acc

    vmem = pl.BlockSpec(memory_space=pltpu.VMEM)
    return pl.pallas_call(
        body,
        in_specs=[vmem],
        out_specs=vmem,
        out_shape=jax.ShapeDtypeStruct(part.shape, F32),
        scratch_shapes=[pltpu.VMEM((8,) + part.shape, F32), pltpu.SemaphoreType.DMA((7,)), pltpu.SemaphoreType.DMA((7,))],
        name="all_sum_small",
    )(part)


def _deinterleave(a, d):
    b, s, c = a.shape
    return a.reshape(b, s // d, d, c).transpose(0, 2, 1, 3).reshape(b * s // BAND, BAND, c)


def _reinterleave(a, d, b):
    c = a.shape[-1]
    return a.reshape(b, d, SEQ // d, c).transpose(0, 2, 1, 3).reshape(b, SEQ, c)


def _rope_tables():
    half = HEAD_DIM // 2
    inv_freq = ROPE_THETA ** (-jnp.arange(half, dtype=F32) * 2.0 / HEAD_DIM)
    ang = jnp.arange(SEQ, dtype=F32)[:, None] * inv_freq[None, :]
    cos = jnp.tile(jnp.cos(ang), (1, 2 * BAND_HEADS))
    sin = jnp.tile(jnp.concatenate([-jnp.sin(ang), jnp.sin(ang)], axis=1), (1, BAND_HEADS))
    return cos, sin


def _band_groups():
    out = []
    for d in DIL_D:
        rows = SEQ // d
        cls = max(1, 512 // rows) if d > 1 else 1
        out.append(dict(rows=rows, cls=cls, steps=d // cls))
    return out


def _local_step(x, mem, loss_target, g_pre_mix, g_post_mix, g_pre_ffn, g_post_ffn, g_mem, b_gate, w, comm=None):
    bl = x.shape[0]
    t = bl * SEQ
    chips = range(N_CHIPS)
    w_in_full = jnp.concatenate([w["w_in"][s] for s in chips], axis=1)
    w_mem_kv_full = w["w_mem_kv"].reshape(D_MODEL, 2 * MEM_W)
    half_ff = D_FF // 2

    x2 = x.reshape(t, D_MODEL)
    tgt2 = loss_target.reshape(t, D_MODEL)
    mem2 = mem.reshape(bl * MEM_LEN, D_MODEL)

    h = _norm_fwd(x2, g_pre_mix, name="norm_x")
    proj = _mm([(h, w_in_full)], nt=False, tm=512, tn=2176, out_dtypes=[BF16], name="proj")
    gates = _mm([(h, w["w_gate"], None, "j")], nt=False, tm=512, tn=w["w_gate"].shape[2], out_dtypes=[BF16], name="gates",
                bias=b_gate, epilogue=lambda acc: (_sigmoid(acc),))
    hm = _norm_fwd(mem2, g_mem, name="norm_mem")
    kv_m = _mm([(hm, w_mem_kv_full)], nt=False, tm=512, tn=1024, out_dtypes=[BF16], name="mem_kv")
    proj3 = proj.reshape(bl, SEQ, D_IN)
    kv3 = kv_m.reshape(bl, MEM_LEN, 2 * MEM_W)

    o_a, o_a32, late_gathered = _sb_fwd(proj3, comm["late_shards"] if comm else [])
    if comm:
        w = {**w, **{name: lax.dynamic_update_slice(g, s[None], (comm["me"][0], 0, 0))
                     for name, g, s in zip(comm["late_names"], late_gathered, comm["late_shards"])}}
    w_o_full = w["w_o"].reshape(D_MODEL, D_MODEL)
    w_ffn_out_full = w["w_ffn_out"].reshape(D_FF, D_MODEL)

    cos_t, sin_t = _rope_tables()
    dil0 = 3 * SB_W

    grp_w = 3 * DIL_W
    band = []
    for g, (d, cfg) in enumerate(zip(DIL_D, _band_groups())):
        a_g = proj3 if d == 1 else proj3[:, :, dil0 + g * grp_w:dil0 + (g + 1) * grp_w].reshape(bl, SEQ // d, d * grp_w)
        band.append(dict(cfg, a=a_g, col0=dil0 // grp_w if d == 1 else None, cos=cos_t.reshape(SEQ // d, d * DIL_W),
                         sin=sin_t.reshape(SEQ // d, d * DIL_W)))
    ols = [_band_group_fwd(b["a"], b["cos"], b["sin"], rows=b["rows"], cls=b["cls"], steps=b["steps"], col0=b["col0"],
                           name=f"band_fwd_{g}").reshape(t, 2 * DIL_W) for g, b in enumerate(band)]
    o_b, lse_b = _band_merge3(ols)

    o_c = _mem_fwd(proj3, kv3)

    o_a2, o_c2 = o_a.reshape(t, SB_W), o_c.reshape(t, MEM_W)
    y_a, y_b, y_c, merged = _branch_merge_fwd(o_a2, o_b, o_c2, w["w_br_sb"], w["w_br_dil"], w["w_br_mem"], gates)
    mix = _mm([(merged, w_o_full)], nt=False, tm=512, tn=1024, out_dtypes=[F32], name="mix")
    x1, h2 = _mid_fwd(mix, x2, g_post_mix, g_pre_ffn)
    gg, uu, f = _ffn_in_fwd(h2, w["w_ffn_in"])
    f2 = _mm([(f, w_ffn_out_full)], nt=False, tm=512, tn=1024, out_dtypes=[F32], name="ffn_out")

    dy, df2, dg_post_ffn, loss_row = _loss_bwd(f2, x1, g_post_ffn, tgt2)

    dg_ffn, du_ffn = _mm([(df2, w_ffn_out_full)], nt=True, tm=512, tn=half_ff, out_dtypes=[BF16, BF16], name="d_ffn_act",
                         extras=(gg, uu), epilogue=_swiglu_bwd_epilogue)
    gw = {}
    gw["w_ffn_out"] = _mm_tn(f, df2, tm=half_ff, tn=1024, tk=512, name="gw_ffn_out").reshape(N_CHIPS, D_FF // N_CHIPS, D_MODEL)
    gw_ffn_g = _mm_tn(h2, dg_ffn, tm=1024, tn=half_ff, tk=512, name="gw_ffn_gate", out_shards=True)
    gw_ffn_u = _mm_tn(h2, du_ffn, tm=1024, tn=half_ff, tk=512, name="gw_ffn_up", out_shards=True)
    gw["w_ffn_in"] = jnp.concatenate([gw_ffn_g, gw_ffn_u], axis=0)
    dh2 = _mm([(dg_ffn, w["w_ffn_in"], 0, 0), (dg_ffn, w["w_ffn_in"], 1, 1), (du_ffn, w["w_ffn_in"], 0, 2),
               (du_ffn, w["w_ffn_in"], 1, 3)], nt=True, tm=256, tn=1024, out_dtypes=[F32], name="d_h2")
    dx1, dmix, dg_pre_ffn, dg_post_mix = _mid_bwd(dh2, x1, mix, g_pre_ffn, g_post_mix, dy)

    gw["w_o"] = _mm_tn(merged, dmix, tm=1024, tn=1024, tk=512, name="gw_o").reshape(N_CHIPS, D_MODEL // N_CHIPS, D_MODEL)
    dmerged = _mm([(dmix, w_o_full)], nt=True, tm=512, tn=1024, out_dtypes=[F32], name="d_merged")
    dy_a, dy_b, dy_c, dgpre, db_gate = _gate_bwd(dmerged, gates, y_a, y_b, y_c)
    br_cols = D_MODEL // N_CHIPS
    gw["w_br_sb"] = _mm_tn(o_a2, dy_a, tm=512, tn=br_cols, tk=1024, name="gw_br_sb", out_shards=True)
    gw["w_br_dil"] = _mm_tn(o_b, dy_b, tm=256, tn=br_cols, tk=1024, name="gw_br_dil", out_shards=True)
    gw["w_br_mem"] = _mm_tn(o_c2, dy_c, tm=512, tn=br_cols, tk=1024, name="gw_br_mem", out_shards=True)
    gw["w_gate"] = _mm_tn(h, dgpre, tm=1024, tn=w["w_gate"].shape[2], tk=512, name="gw_gate", out_shards=True)
    do_a = _mm([(dy_a, w["w_br_sb"], s, s) for s in chips], nt=True, tm=512, tn=SB_W, out_dtypes=[BF16], name="d_o_a")
    do_b = _mm([(dy_b, w["w_br_dil"], s, s) for s in chips], nt=True, tm=512, tn=DIL_W, out_dtypes=[BF16], name="d_o_b")
    do_c = _mm([(dy_c, w["w_br_mem"], s, s) for s in chips], nt=True, tm=512, tn=MEM_W, out_dtypes=[BF16], name="d_o_c")

    dq_c, dk_m, dv_m = _mem_bwd(proj3, kv3, do_c.reshape(bl, SEQ, MEM_W))
    dkv_m = jnp.concatenate([dk_m, dv_m], axis=-1).reshape(bl * MEM_LEN, 2 * MEM_W).astype(BF16)
    gw["w_mem_kv"] = _mm_tn(hm, dkv_m, tm=1024, tn=1024, tk=512, name="gw_mem_kv").reshape(N_CHIPS, D_MODEL // N_CHIPS, 2 * MEM_W)
    dhm = _mm([(dkv_m, w_mem_kv_full)], nt=True, tm=512, tn=1024, out_dtypes=[F32], name="d_hm")
    dg_mem = _mem_norm_bwd(dhm, mem2, g_mem)

    dcat = _band_delta(do_b, o_b, lse_b)
    d_dil = [_band_group_bwd(b["a"], dcat.reshape(bl, SEQ // d, d * grp_w), b["cos"], b["sin"], rows=b["rows"], cls=b["cls"],
                             steps=b["steps"], col0=b["col0"], name=f"band_bwd_{g}").reshape(bl, SEQ, grp_w)
             for g, (d, b) in enumerate(zip(DIL_D, band))]

    early, parts = [], []
    if comm:
        early = [name for name, _, _ in PACK if name != "w_in"]
        grads = [gw[name] for name in early]
        parts = [_pair_add(g, l, comm["c"], name="pair_add_" + name)
                 for name, g, l in zip(early, grads, _pair_exchange(grads, name="pair_exchange_early"))]
    dq_a, dk_a, dv_a, lands = _sb_bwd(proj3, o_a32, do_a.reshape(bl, SEQ, SB_W), parts)
    reduced = {name: (p, l) for name, p, l in zip(early, parts, lands)}

    dproj = jnp.concatenate([dq_a, dk_a, dv_a] + d_dil + [dq_c], axis=-1).reshape(t, D_IN)
    in_cols = D_IN // N_CHIPS
    dproj_s = jnp.stack([dproj[:, s * in_cols:(s + 1) * in_cols] for s in chips])
    gw["w_in"] = _mm_tn(h, dproj_s, tm=1024, tn=in_cols, tk=512, name="gw_in")
    dh = _mm([(dproj_s, w["w_in"], s, s) for s in chips] + [(dgpre, w["w_gate"], s, s) for s in chips],
             nt=True, tm=256, tn=1024, out_dtypes=[F32], name="d_h")
    grad_x, dg_pre_mix = _first_bwd(dh, x2, g_pre_mix, dx1)
    small = jnp.concatenate([dg_pre_mix, dg_post_mix, dg_pre_ffn, dg_post_ffn, dg_mem, db_gate.reshape(3, D_MODEL)], axis=0)
    return loss_row[0, 0], grad_x.reshape(bl, SEQ, D_MODEL), gw, small, reduced


def kernel(x, mem, g_pre_mix, g_post_mix, g_pre_ffn, g_post_ffn, g_mem, w_in, w_mem_kv, w_br_sb, w_br_dil, w_br_mem, w_gate, b_gate, w_o, w_ffn_in, w_ffn_out, loss_target, m_g_pre_mix, m_g_post_mix, m_g_pre_ffn, m_g_post_ffn, m_g_mem, m_w_in, m_w_mem_kv, m_w_br_sb, m_w_br_dil, m_w_br_mem, m_w_gate, m_b_gate, m_w_o, m_w_ffn_in, m_w_ffn_out, v_g_pre_mix, v_g_post_mix, v_g_pre_ffn, v_g_post_ffn, v_g_mem, v_w_in, v_w_mem_kv, v_w_br_sb, v_w_br_dil, v_w_br_mem, v_w_gate, v_b_gate, v_w_o, v_w_ffn_in, v_w_ffn_out):
    w_shards = dict(w_in=w_in[0], w_mem_kv=w_mem_kv[0], w_br_sb=w_br_sb[0], w_br_dil=w_br_dil[0], w_br_mem=w_br_mem[0],
                    w_gate=w_gate[0], w_o=w_o[0], w_ffn_in=w_ffn_in[0], w_ffn_out=w_ffn_out[0])
    m_shards = dict(w_in=m_w_in[0], w_mem_kv=m_w_mem_kv[0], w_br_sb=m_w_br_sb[0], w_br_dil=m_w_br_dil[0], w_br_mem=m_w_br_mem[0],
                    w_gate=m_w_gate[0], w_o=m_w_o[0], w_ffn_in=m_w_ffn_in[0], w_ffn_out=m_w_ffn_out[0])
    v_shards = dict(w_in=v_w_in[0], w_mem_kv=v_w_mem_kv[0], w_br_sb=v_w_br_sb[0], w_br_dil=v_w_br_dil[0], w_br_mem=v_w_br_mem[0],
                    w_gate=v_w_gate[0], w_o=v_w_o[0], w_ffn_in=v_w_ffn_in[0], w_ffn_out=v_w_ffn_out[0])

    names = [name for name, _, _ in PACK]
    c_arr = lax.axis_index("c").astype(jnp.int32).reshape(1)
    me_arr = (2 * lax.axis_index("x") + lax.axis_index("y")).astype(jnp.int32).reshape(1)
    first_names = ["w_in", "w_gate", "w_mem_kv"]
    late_names = [name for name in names if name not in first_names]
    first_bf = [w_shards[name].astype(BF16) for name in first_names]
    w = {name: lax.dynamic_update_slice(g, s[None], (me_arr[0], 0, 0))
         for name, g, s in zip(first_names, _gather_weights(first_bf), first_bf)}
    comm = dict(c=c_arr, me=me_arr, late_names=late_names, late_shards=[w_shards[name].astype(BF16) for name in late_names])

    loss_local, grad_x, gw, small, reduced = _local_step(x, mem, loss_target, g_pre_mix, g_post_mix, g_pre_ffn, g_post_ffn,
                                                         g_mem, b_gate, w, comm)
    loss = lax.psum(loss_local, ("x", "y", "c"))

    part_in = _pair_add(gw["w_in"], _pair_exchange([gw["w_in"]], name="pair_exchange_w_in")[0], c_arr, name="pair_add_w_in")
    reduced["w_in"] = (part_in, _chip_exchange([part_in])[0])
    halves = [_chip_add(reduced[name][1], reduced[name][0], me_arr, name="chip_add_" + name) for name in names]
    theirs = _pair_share(halves)
    small = _all_sum_small(small)

    upd = {}
    for name, mine, other in zip(names, halves, theirs):
        upd[name] = _adamw_halves(w_shards[name], mine, other, m_shards[name], v_shards[name], c_arr, name="adamw_" + name)
    g_shards = {name: u[0] for name, u in upd.items()}

    def small8(gs, b):
        return jnp.concatenate(gs + [b.reshape(3, D_MODEL)], axis=0)

    sw = small8([g_pre_mix, g_post_mix, g_pre_ffn, g_post_ffn, g_mem], b_gate)
    sm = small8([m_g_pre_mix, m_g_post_mix, m_g_pre_ffn, m_g_post_ffn, m_g_mem], m_b_gate)
    sv = small8([v_g_pre_mix, v_g_post_mix, v_g_pre_ffn, v_g_post_ffn, v_g_mem], v_b_gate)
    s_upd = _adamw(sw, small, sm, sv, tm=8, name="adamw_small")

    def small_out(a):
        return [a[0:1], a[1:2], a[2:3], a[3:4], a[4:5]]

    order = ["w_in", "w_mem_kv", "w_br_sb", "w_br_dil", "w_br_mem", "w_gate", "b_gate", "w_o", "w_ffn_in", "w_ffn_out"]

    def leaves(small_arr, big):
        out = small_out(small_arr)
        for name in order:
            out.append(small_arr[5:8].reshape(1, 3 * D_MODEL) if name == "b_gate" else big[name][None])
        return out

    grads_out = leaves(small, g_shards)
    delta_out = leaves(s_upd[0], {n: u[1] for n, u in upd.items()})
    m_out = leaves(s_upd[1], {n: u[2] for n, u in upd.items()})
    v_out = leaves(s_upd[2], {n: u[3] for n, u in upd.items()})
    return (loss, grad_x, *grads_out, *delta_out, *m_out, *v_out)
```

```python
import functools

import jax
import jax.numpy as jnp
from jax import lax
from jax.experimental import pallas as pl
from jax.experimental.pallas import tpu as pltpu

F32 = jnp.float32
BF16 = jnp.bfloat16
MESH = pl.DeviceIdType.MESH

D_MODEL = 1024
SEQ = 2048
HEAD_DIM = 64
SB_W = 512
DIL_W = 256
MEM_W = 512
MEM_LEN = 256
D_IN = 3 * SB_W + 9 * DIL_W + MEM_W
D_FF = 2816
DIL_D = (1, 4, 16)
ROPE_THETA = 10000.0
NORM_EPS = 1e-6
NEG_INF = -1e30
LANES = 128

ADAM_LR = 0.001
ADAM_B1 = 0.9
ADAM_B2 = 0.999
ADAM_EPS = 1e-08
ADAM_WD = 0.01
ADAM_STEP = 10

N_CHIPS = 4
PACK = (
    ("w_in", (1024, 1088), 1),
    ("w_mem_kv", (256, 1024), 0),
    ("w_br_sb", (512, 256), 1),
    ("w_br_dil", (256, 256), 1),
    ("w_br_mem", (512, 256), 1),
    ("w_gate", (1024, 768), 1),
    ("w_o", (256, 1024), 0),
    ("w_ffn_in", (1024, 1408), 1),
    ("w_ffn_out", (704, 1024), 0),
)
PACK_ROWS = sum(a * b for _, (a, b), _ in PACK) // D_MODEL
HALF_ROWS = PACK_ROWS // 2


def _dot(a, b):
    return lax.dot_general(a, b, (((1,), (0,)), ((), ())), preferred_element_type=F32)


def _dot_nt(a, b):
    return lax.dot_general(a, b, (((1,), (1,)), ((), ())), preferred_element_type=F32)


def _dot_tn(a, b):
    return lax.dot_general(a, b, (((0,), (0,)), ((), ())), preferred_element_type=F32)


def _split_dot(x, u):
    hi = x.astype(BF16)
    lo = (x - hi.astype(F32)).astype(BF16)
    return _dot(hi, u) + _dot(lo, u)


V7X_VMEM_BUDGET = 44 * 2 ** 20


def _rows_that_fit(m, row_bytes, fixed_bytes):
    for tm in (1024, 512, 256, 128):
        if m % tm == 0 and fixed_bytes + tm * row_bytes <= V7X_VMEM_BUDGET:
            return tm
    return min(m, 128)


def _mm(pairs, *, nt, tn, out_dtypes, name, bias=None, extras=(), epilogue=None, gather=()):
    pairs = [p if len(p) == 4 else (p[0], p[1], None, None) for p in pairs]
    m = pairs[0][0].shape[-2]
    b0 = pairs[0][1]
    if nt:
        n = b0.shape[-2]
    else:
        n = b0.shape[-1] * (b0.shape[0] if b0.ndim == 3 else 1)
    n_pairs, n_extra, n_out = len(pairs), len(extras), len(out_dtypes)
    assert n % tn == 0
    one_col = n == tn
    ks = [(b.shape[-1] if nt else b.shape[-2]) for _, b, _, _ in pairs]
    fixed = sum(k * tn * 2 for k in ks) * (1 if one_col else 2)
    row_bytes = 2 * sum(k * 2 for k in ks) + 2 * tn * (sum(jnp.dtype(dt).itemsize for dt in out_dtypes) + 2 * n_extra) + 2 * tn * 4
    tm = _rows_that_fit(m, row_bytes, fixed)
    assert m % tm == 0
    b_mode = dict(pipeline_mode=pl.Buffered(1)) if one_col else {}
    has_bias = bias is not None
    n_side = len(gather)
    n_main_in = 2 * n_pairs + has_bias + n_extra
    n_steps = (n // tn) * (m // tm)

    def body(*refs):
        if n_side:
            side_in = refs[n_main_in:n_main_in + n_side]
            side_out = refs[n_main_in + n_side + n_out:n_main_in + 2 * n_side + n_out]
            send, forward, finish = _gather_phases(side_in, side_out, *refs[n_main_in + 2 * n_side + n_out:])
            step = pl.program_id(0) * (m // tm) + pl.program_id(1)
            pl.when(step == 0)(send)
            pl.when(step == n_steps // 2)(forward)
        acc = None
        for i in range(n_pairs):
            a, b = refs[2 * i][...], refs[2 * i + 1][...]
            p = _dot_nt(a, b) if nt else _dot(a, b)
            acc = p if acc is None else acc + p
        pos = 2 * n_pairs
        if has_bias:
            acc = acc + refs[pos][...]
            pos += 1
        ex = [r[...] for r in refs[pos:pos + n_extra]]
        outs = refs[n_main_in + n_side:n_main_in + n_side + n_out]
        vals = (acc,) if epilogue is None else epilogue(acc, *ex)
        for r, v, dt in zip(outs, vals, out_dtypes):
            r[...] = v.astype(dt)
        if n_side:
            pl.when(step == n_steps - 1)(finish)

    in_specs, args = [], []
    for a, b, a_col, b_sel in pairs:
        k = b.shape[-1] if nt else b.shape[-2]
        assert a_col is not None or a.shape[1] == k
        if a.ndim == 3:
            in_specs.append(pl.BlockSpec((None, tm, k), lambda j, i, c=a_col: (c, i, 0)))
        else:
            in_specs.append(pl.BlockSpec((tm, k), lambda j, i, c=a_col or 0: (i, c)))
        if b.ndim == 2:
            in_specs.append(pl.BlockSpec((tn, k), lambda j, i: (j, 0), **b_mode) if nt
                            else pl.BlockSpec((k, tn), lambda j, i: (0, j), **b_mode))
        elif nt:
            in_specs.append(pl.BlockSpec((None, tn, k), lambda j, i, s=b_sel: (s, j, 0), **b_mode))
        else:
            assert b_sel == "j" and b.shape[-1] == tn
            in_specs.append(pl.BlockSpec((None, k, tn), lambda j, i: (j, 0, 0), **b_mode))
        args += [a, b]
    if has_bias:
        in_specs.append(pl.BlockSpec((1, tn), lambda j, i: (0, j)))
        args.append(bias)
    for e in extras:
        in_specs.append(pl.BlockSpec((tm, tn), lambda j, i: (i, j)))
        args.append(e)
    out = pl.pallas_call(
        body,
        grid=(n // tn, m // tm),
        in_specs=in_specs + [ANY] * n_side,
        out_specs=[pl.BlockSpec((tm, tn), lambda j, i: (i, j)) for _ in range(n_out)] + [ANY] * n_side,
        out_shape=[jax.ShapeDtypeStruct((m, n), dt) for dt in out_dtypes] + _gather_out_shapes(gather),
        scratch_shapes=_gather_sems(n_side) if n_side else [],
        compiler_params=pltpu.CompilerParams(dimension_semantics=("arbitrary", "arbitrary") if n_side else ("parallel", "parallel")),
        name=name,
    )(*args, *gather)
    if n_side:
        return (out[0] if n_out == 1 else out[:n_out]), out[n_out:]
    return out[0] if n_out == 1 else out


def _mm_tn(a, b, *, tm, tn, name, out_shards=False):
    k, m = a.shape
    b_shards = b.ndim == 3
    out_shards = out_shards or b_shards
    n = b.shape[0] * b.shape[2] if b_shards else b.shape[1]
    tk = _rows_that_fit(k, 2 * 2 * (tm + tn), 3 * tm * tn * 4)
    assert m % tm == 0 and n % tn == 0 and k % tk == 0 and (not b_shards or b.shape[2] == tn)

    def body(a_ref, b_ref, o_ref):
        @pl.when(pl.program_id(2) == 0)
        def _():
            o_ref[...] = jnp.zeros_like(o_ref)

        o_ref[...] += _dot_tn(a_ref[...], b_ref[...])

    if b_shards:
        b_spec = pl.BlockSpec((None, tk, tn), lambda i, j, kk: (j, kk, 0))
    else:
        b_spec = pl.BlockSpec((tk, tn), lambda i, j, kk: (kk, j))
    if out_shards:
        out_spec = pl.BlockSpec((None, tm, tn), lambda i, j, kk: (j, i, 0))
        out_shape = jax.ShapeDtypeStruct((n // tn, m, tn), F32)
    else:
        out_spec = pl.BlockSpec((tm, tn), lambda i, j, kk: (i, j))
        out_shape = jax.ShapeDtypeStruct((m, n), F32)
    return pl.pallas_call(
        body,
        grid=(m // tm, n // tn, k // tk),
        in_specs=[pl.BlockSpec((tk, tm), lambda i, j, kk: (kk, i)), b_spec],
        out_specs=out_spec,
        out_shape=out_shape,
        compiler_params=pltpu.CompilerParams(dimension_semantics=("parallel", "parallel", "arbitrary")),
        name=name,
    )(a, b)


def _rowwise(fn, ins, outs, *, tm, name):
    rows = next(a.shape[0] for a, kind in ins if kind == "row")
    tm = min(tm, rows)
    assert rows % tm == 0
    n_in = len(ins)

    def body(*refs):
        vals = fn(*[r[...] for r in refs[:n_in]])
        for (_, dt, kind), r, v in zip(outs, refs[n_in:], vals):
            if kind == "row":
                r[...] = v.astype(dt)
            else:
                @pl.when(pl.program_id(0) == 0)
                def _(r=r):
                    r[...] = jnp.zeros_like(r)

                r[...] += v

    in_specs = [pl.BlockSpec((tm, a.shape[1]), lambda i: (i, 0)) if kind == "row" else pl.BlockSpec(a.shape, lambda i: (0, 0))
                for a, kind in ins]
    out_specs = [pl.BlockSpec((tm, c), lambda i: (i, 0)) if kind == "row" else pl.BlockSpec((1, c), lambda i: (0, 0))
                 for c, _, kind in outs]
    out_shape = [jax.ShapeDtypeStruct((rows if kind == "row" else 1, c), dt) for c, dt, kind in outs]
    has_acc = any(kind == "acc" for _, _, kind in outs)
    return pl.pallas_call(
        body,
        grid=(rows // tm,),
        in_specs=in_specs,
        out_specs=out_specs,
        out_shape=out_shape,
        compiler_params=pltpu.CompilerParams(dimension_semantics=("arbitrary" if has_acc else "parallel",)),
        name=name,
    )(*[a for a, _ in ins])


def _rstd(x):
    return lax.rsqrt(jnp.mean(x * x, axis=-1, keepdims=True) + NORM_EPS)


def _norm_bwd(dout, xin, g):
    r = _rstd(xin)
    n = xin * r
    dn = dout * g
    dg = jnp.sum(dout * n, axis=0, keepdims=True)
    dx = r * (dn - n * jnp.mean(dn * n, axis=-1, keepdims=True))
    return dx, dg


def _sigmoid(x):
    return 1.0 / (1.0 + jnp.exp(-x))


def _norm_fwd(x, g, *, name):
    def fn(x, g):
        return ((x * _rstd(x)) * g,)

    return _rowwise(fn, [(x, "row"), (g, "vec")], [(D_MODEL, BF16, "row")], tm=512, name=name)[0]


def _mid_fwd(mix, x, g_post_mix, g_pre_ffn):
    def fn(mix, x, g2, g3):
        x1 = x + (mix * _rstd(mix)) * g2
        return x1, (x1 * _rstd(x1)) * g3

    return _rowwise(fn, [(mix, "row"), (x, "row"), (g_post_mix, "vec"), (g_pre_ffn, "vec")],
                    [(D_MODEL, F32, "row"), (D_MODEL, BF16, "row")], tm=512, name="mid_fwd")


def _loss_bwd(f2, x1, g_post_ffn, tgt):
    def fn(f2, x1, g4, tgt):
        r = _rstd(f2)
        n = f2 * r
        err = x1 + n * g4 - tgt
        loss = 0.5 * jnp.sum(jnp.mean(err * err, axis=-1, keepdims=True), axis=0, keepdims=True)
        dy = err * (1.0 / D_MODEL)
        dn = dy * g4
        dg4 = jnp.sum(dy * n, axis=0, keepdims=True)
        df2 = r * (dn - n * jnp.mean(dn * n, axis=-1, keepdims=True))
        return dy, df2, dg4, jnp.broadcast_to(loss, (1, LANES))

    return _rowwise(fn, [(f2, "row"), (x1, "row"), (g_post_ffn, "vec"), (tgt, "row")],
                    [(D_MODEL, F32, "row"), (D_MODEL, BF16, "row"), (D_MODEL, F32, "acc"), (LANES, F32, "acc")],
                    tm=512, name="loss_bwd")


def _mid_bwd(dh2, x1, mix, g_pre_ffn, g_post_mix, dy):
    def fn(dh2, x1, mix, g3, g2, dy):
        d3, dg3 = _norm_bwd(dh2, x1, g3)
        dx1 = dy + d3
        dmix, dg2 = _norm_bwd(dx1, mix, g2)
        return dx1, dmix, dg3, dg2

    return _rowwise(fn, [(dh2, "row"), (x1, "row"), (mix, "row"), (g_pre_ffn, "vec"), (g_post_mix, "vec"), (dy, "row")],
                    [(D_MODEL, F32, "row"), (D_MODEL, BF16, "row"), (D_MODEL, F32, "acc"), (D_MODEL, F32, "acc")],
                    tm=256, name="mid_bwd")


def _first_bwd(dh, x, g_pre_mix, dx1):
    def fn(dh, x, g1, dx1):
        d1, dg1 = _norm_bwd(dh, x, g1)
        return dx1 + d1, dg1

    return _rowwise(fn, [(dh, "row"), (x, "row"), (g_pre_mix, "vec"), (dx1, "row")],
                    [(D_MODEL, F32, "row"), (D_MODEL, F32, "acc")], tm=512, name="first_bwd")


def _mem_norm_bwd(dhm, mem, g_mem):
    def fn(dhm, mem, g):
        return (jnp.sum(dhm * (mem * _rstd(mem)), axis=0, keepdims=True),)

    return _rowwise(fn, [(dhm, "row"), (mem, "row"), (g_mem, "vec")], [(D_MODEL, F32, "acc")], tm=512, name="mem_norm_bwd")[0]


def _gate_bwd(dmerged, gates, ya, yb, yc):
    def fn(dm, gt, ya, yb, yc):
        gt = gt.astype(F32)
        outs, dgp = [], []
        for i, y in enumerate((ya, yb, yc)):
            gi = gt[:, i * D_MODEL:(i + 1) * D_MODEL]
            outs.append(dm * gi)
            dgp.append(dm * y.astype(F32) * gi * (1.0 - gi))
        dgpre = jnp.concatenate(dgp, axis=1)
        return outs[0], outs[1], outs[2], dgpre, jnp.sum(dgpre, axis=0, keepdims=True)

    return _rowwise(fn, [(dmerged, "row"), (gates, "row"), (ya, "row"), (yb, "row"), (yc, "row")],
                    [(D_MODEL, BF16, "row")] * 3 + [(3 * D_MODEL, BF16, "row"), (3 * D_MODEL, F32, "acc")],
                    tm=256, name="gate_bwd")


def _adamw_math(w, g, m, v):
    m = ADAM_B1 * m + (1.0 - ADAM_B1) * g
    v = ADAM_B2 * v + (1.0 - ADAM_B2) * (g * g)
    m_hat = m / (1.0 - ADAM_B1 ** ADAM_STEP)
    v_hat = v / (1.0 - ADAM_B2 ** ADAM_STEP)
    delta = -ADAM_LR * (m_hat / (jnp.sqrt(v_hat) + ADAM_EPS) + ADAM_WD * w)
    return delta, m, v


def _adamw(w, g, m, v, *, tm, name):
    c = w.shape[1]
    return _rowwise(_adamw_math, [(w, "row"), (g, "row"), (m, "row"), (v, "row")], [(c, F32, "row")] * 3, tm=tm, name=name)


def _adamw_halves(w, g_mine, g_theirs, m, v, c_arr, *, name):
    a, b = w.shape
    hf = a // 2
    tr = hf // 4

    def body(c_ref, w_ref, gm_ref, gt_ref, m_ref, v_ref, g_out, d_out, m_out, v_out):
        g = jnp.where(pl.program_id(0) == c_ref[0], gm_ref[...], gt_ref[...])
        d, m_new, v_new = _adamw_math(w_ref[...], g, m_ref[...], v_ref[...])
        g_out[...] = g
        d_out[...] = d
        m_out[...] = m_new
        v_out[...] = v_new

    full = pl.BlockSpec((tr, b), lambda hh, i, c_ref: (hh * (hf // tr) + i, 0))
    half = pl.BlockSpec((tr, b), lambda hh, i, c_ref: (i, 0))
    return pl.pallas_call(
        body,
        grid_spec=pltpu.PrefetchScalarGridSpec(
            num_scalar_prefetch=1,
            grid=(2, hf // tr),
            in_specs=[full, half, half, full, full],
            out_specs=[full] * 4,
        ),
        out_shape=[jax.ShapeDtypeStruct((a, b), F32)] * 4,
        compiler_params=pltpu.CompilerParams(dimension_semantics=("parallel", "parallel")),
        name=name,
    )(c_arr, w, g_mine, g_theirs, m, v)


def _ffn_in_fwd(h2, w_ffn):
    m, tm, tn = h2.shape[0], 512, w_ffn.shape[2]
    assert 2 * tn == D_FF

    def body(h_ref, wg_ref, wu_ref, g_ref, u_ref, f_ref):
        h = h_ref[...]
        g = _dot(h, wg_ref[...])
        u = _dot(h, wu_ref[...])
        g_ref[...] = g.astype(BF16)
        u_ref[...] = u.astype(BF16)
        f_ref[...] = (g * _sigmoid(g) * u).astype(BF16)

    o_spec = pl.BlockSpec((tm, tn), lambda j, i: (i, j))
    return pl.pallas_call(
        body,
        grid=(D_FF // tn, m // tm),
        in_specs=[pl.BlockSpec((tm, D_MODEL), lambda j, i: (i, 0)),
                  pl.BlockSpec((None, D_MODEL, tn), lambda j, i: (j, 0, 0)),
                  pl.BlockSpec((None, D_MODEL, tn), lambda j, i: (j + 2, 0, 0))],
        out_specs=[o_spec, o_spec, o_spec],
        out_shape=[jax.ShapeDtypeStruct((m, D_FF), BF16)] * 3,
        compiler_params=pltpu.CompilerParams(dimension_semantics=("parallel", "parallel")),
        name="ffn_in_fwd",
    )(h2, w_ffn, w_ffn)


def _swiglu_bwd_epilogue(df, g, u):
    g = g.astype(F32)
    u = u.astype(F32)
    sg = _sigmoid(g)
    return df * u * (sg * (1.0 + g * (1.0 - sg))), df * (g * sg)


def _branch_merge_fwd(o_a, o_b, o_c, w_sb, w_dil, w_mem, gates):
    m, tm = o_a.shape[0], 256

    def body(oa_ref, ob_ref, oc_ref, wa_ref, wb_ref, wc_ref, gt_ref, ya_ref, yb_ref, yc_ref, mg_ref):
        def project(o_ref, w_ref):
            o = o_ref[...]
            return jnp.concatenate([_dot(o, w_ref[s]) for s in range(N_CHIPS)], axis=1)

        ya = project(oa_ref, wa_ref)
        yb = project(ob_ref, wb_ref)
        yc = project(oc_ref, wc_ref)
        gt = gt_ref[...].astype(F32)
        ya_ref[...] = ya.astype(BF16)
        yb_ref[...] = yb.astype(BF16)
        yc_ref[...] = yc.astype(BF16)
        mg_ref[...] = (gt[:, :D_MODEL] * ya + gt[:, D_MODEL:2 * D_MODEL] * yb + gt[:, 2 * D_MODEL:] * yc).astype(BF16)

    row = lambda c: pl.BlockSpec((tm, c), lambda i: (i, 0))
    full = lambda a: pl.BlockSpec(a.shape, lambda i: (0, 0, 0))
    return pl.pallas_call(
        body,
        grid=(m // tm,),
        in_specs=[row(SB_W), row(DIL_W), row(MEM_W), full(w_sb), full(w_dil), full(w_mem), row(3 * D_MODEL)],
        out_specs=[row(D_MODEL)] * 4,
        out_shape=[jax.ShapeDtypeStruct((m, D_MODEL), BF16)] * 4,
        compiler_params=pltpu.CompilerParams(dimension_semantics=("parallel",)),
        name="branch_merge_fwd",
    )(o_a, o_b, o_c, w_sb, w_dil, w_mem, gates)


SB_T = 256
SB_SCALE = HEAD_DIM ** -0.5


def _sb_masks():
    row = lax.broadcasted_iota(jnp.int32, (SB_T, SB_T), 0)
    col = lax.broadcasted_iota(jnp.int32, (SB_T, SB_T), 1)
    lane = lax.broadcasted_iota(jnp.int32, (1, LANES), 1)
    return row, col, lane


def _sb_logs(z):
    e = jnp.exp(-jnp.abs(z))
    lb = jnp.minimum(z, 0.0) - jnp.log(1.0 + e)
    return lb, lb - z, e


def _sb_specs(n_heads_pairs, col0):
    q = pl.BlockSpec((None, SB_T, LANES), lambda b, p, i: (b, i, col0 + p))
    k = pl.BlockSpec((None, SEQ, LANES), lambda b, p, i: (b, 0, col0 + n_heads_pairs + p))
    v = pl.BlockSpec((None, SEQ, LANES), lambda b, p, i: (b, 0, col0 + 2 * n_heads_pairs + p))
    return q, k, v


def _grid_step(n_pairs, nq):
    return (pl.program_id(0) * n_pairs + pl.program_id(1)) * nq + pl.program_id(2)


def _sb_fwd(proj3, late_shards):
    bl = proj3.shape[0]
    n_pairs = SB_W // LANES
    nq = SEQ // SB_T
    n_late = len(late_shards)
    n_steps = bl * n_pairs * nq

    def body(q_ref, k_ref, v_ref, *rest):
        late_in, (o_ref, o32_ref), late_out = rest[:n_late], rest[n_late:n_late + 2], rest[n_late + 2:2 * n_late + 2]
        step = _grid_step(n_pairs, nq)
        if n_late:
            send, forward, finish = _gather_phases(late_in, late_out, *rest[2 * n_late + 2:])
            pl.when(step == 0)(send)
            pl.when(step == n_steps // 2)(forward)
        i = pl.program_id(2)
        row, col, lane = _sb_masks()
        causal = col < row
        u_excl = (row > col).astype(BF16)
        q = q_ref[...]
        heads = []
        for h in range(2):
            mh = (lane // HEAD_DIM) == h
            heads.append((mh, jnp.where(mh, q, jnp.zeros_like(q)) * SB_SCALE))

        def blocks(js, carries, acc, diag):
            ks = [k_ref[pl.ds(pl.multiple_of(j * SB_T, SB_T), SB_T), :] for j in js]
            vs = [v_ref[pl.ds(pl.multiple_of(j * SB_T, SB_T), SB_T), :] for j in js]
            chains = [(b, h) for b in range(len(js)) for h in range(2)]
            z = {c: _dot_nt(heads[c[1]][1], ks[c[0]]) for c in chains}
            lb, lk = {}, {}
            for c in chains:
                lb[c], lk[c], _ = _sb_logs(z[c])
                if diag:
                    lk[c] = jnp.where(causal, lk[c], 0.0)
            r = {c: _split_dot(lk[c], u_excl) for c in chains}
            carries = list(carries)
            w = {}
            for b, h in chains:
                w_c = jnp.exp(lb[b, h] + r[b, h] + carries[h])
                w[b, h] = (jnp.where(causal, w_c, 0.0) if diag else w_c).astype(BF16)
                carries[h] = carries[h] + (r[b, h][:, :1] + lk[b, h][:, :1])
            for b, h in chains:
                acc = acc + _dot(w[b, h], jnp.where(heads[h][0], vs[b], jnp.zeros_like(vs[b])))
            return tuple(carries), acc

        zero = jnp.zeros((SB_T, 1), F32)
        carries, acc = blocks([i], (zero, zero), jnp.zeros((SB_T, LANES), F32), True)
        carries, acc = lax.fori_loop(0, i // 2, lambda jj, c: blocks([i - 1 - 2 * jj, i - 2 - 2 * jj], c[0], c[1], False),
                                     (carries, acc))
        carries, acc = lax.fori_loop(0, i % 2, lambda jj, c: blocks([0], c[0], c[1], False), (carries, acc))
        o_ref[...] = acc.astype(BF16)
        o32_ref[...] = acc
        if n_late:
            pl.when(step == n_steps - 1)(finish)

    q_spec, k_spec, v_spec = _sb_specs(n_pairs, 0)
    blk = pl.BlockSpec((None, SB_T, LANES), lambda b, p, i: (b, i, p))
    out = pl.pallas_call(
        body,
        grid=(bl, n_pairs, nq),
        in_specs=[q_spec, k_spec, v_spec] + [ANY] * n_late,
        out_specs=[blk, blk] + [ANY] * n_late,
        out_shape=[jax.ShapeDtypeStruct((bl, SEQ, SB_W), BF16), jax.ShapeDtypeStruct((bl, SEQ, SB_W), F32)]
        + _gather_out_shapes(late_shards),
        scratch_shapes=_gather_sems(n_late) if n_late else [],
        compiler_params=pltpu.CompilerParams(dimension_semantics=("arbitrary", "arbitrary", "arbitrary")),
        name="sb_fwd",
    )(proj3, proj3, proj3, *late_shards)
    return out[0], out[1], out[2:]


def _sb_bwd(proj3, o_a, do_a, parts):
    bl = proj3.shape[0]
    n_pairs = SB_W // LANES
    nq = SEQ // SB_T
    n_parts = len(parts)
    n_steps = bl * n_pairs * nq

    def body(q_ref, k_ref, v_ref, o_ref, do_ref, *rest):
        p_refs, (dq_ref, dk_ref, dv_ref), land_refs = rest[:n_parts], rest[n_parts:n_parts + 3], rest[n_parts + 3:2 * n_parts + 3]
        dk_acc, dv_acc = rest[2 * n_parts + 3:2 * n_parts + 5]
        step = _grid_step(n_pairs, nq)
        if n_parts:
            send, finish = _chip_exchange_phases(p_refs, land_refs, *rest[2 * n_parts + 5:])
            pl.when(step == 0)(send)
        i = pl.program_id(2)

        @pl.when(i == 0)
        def _():
            dk_acc[...] = jnp.zeros_like(dk_acc)
            dv_acc[...] = jnp.zeros_like(dv_acc)

        row, col, lane = _sb_masks()
        causal = col < row
        u_excl = (row > col).astype(BF16)
        u_incl = (row >= col).astype(BF16)
        q = q_ref[...]
        do = do_ref[...]
        prod = do.astype(F32) * o_ref[...]
        heads = []
        for h in range(2):
            mh = (lane // HEAD_DIM) == h
            d_tot = jnp.sum(jnp.where(mh, prod, 0.0), axis=1, keepdims=True)
            heads.append((mh, jnp.where(mh, q, jnp.zeros_like(q)) * SB_SCALE, jnp.where(mh, do, jnp.zeros_like(do)), d_tot))

        def blocks(js, carries, c_das, dq, diag):
            starts = [pl.multiple_of(j * SB_T, SB_T) for j in js]
            ks = [k_ref[pl.ds(s, SB_T), :] for s in starts]
            vs = [v_ref[pl.ds(s, SB_T), :] for s in starts]
            chains = [(b, h) for b in range(len(js)) for h in range(2)]
            z = {c: _dot_nt(heads[c[1]][1], ks[c[0]]) for c in chains}
            dw = {c: _dot_nt(heads[c[1]][2], vs[c[0]]) for c in chains}
            lb, lk, e = {}, {}, {}
            for c in chains:
                lb[c], lk[c], e[c] = _sb_logs(z[c])
                if diag:
                    lk[c] = jnp.where(causal, lk[c], 0.0)
            r = {c: _split_dot(lk[c], u_excl) for c in chains}
            carries, c_das = list(carries), list(c_das)
            wb, da = {}, {}
            for b, h in chains:
                w_c = jnp.exp(lb[b, h] + r[b, h] + carries[h])
                wb[b, h] = (jnp.where(causal, w_c, 0.0) if diag else w_c).astype(BF16)
                da[b, h] = dw[b, h] * wb[b, h].astype(F32)
                carries[h] = carries[h] + (r[b, h][:, :1] + lk[b, h][:, :1])
            sfx = {c: _split_dot(da[c], u_incl) for c in chains}
            dz = {}
            for b, h in chains:
                dlk = heads[h][3] - c_das[h] - sfx[b, h]
                if diag:
                    dlk = jnp.where(causal, dlk, 0.0)
                c_das[h] = c_das[h] + sfx[b, h][:, :1]
                inv = 1.0 / (1.0 + e[b, h])
                pos = z[b, h] >= 0.0
                beta = jnp.where(pos, inv, e[b, h] * inv)
                one_m_beta = jnp.where(pos, e[b, h] * inv, inv)
                dz[b, h] = (da[b, h] * one_m_beta - dlk * beta).astype(BF16)
            for b, h in chains:
                dq = dq + _dot(dz[b, h], jnp.where(heads[h][0], ks[b], jnp.zeros_like(ks[b])))
            for b in range(len(js)):
                dk_acc[pl.ds(starts[b], SB_T), :] += _dot_tn(dz[b, 0], heads[0][1]) + _dot_tn(dz[b, 1], heads[1][1])
                dv_acc[pl.ds(starts[b], SB_T), :] += _dot_tn(wb[b, 0], heads[0][2]) + _dot_tn(wb[b, 1], heads[1][2])
            return tuple(carries), tuple(c_das), dq

        zero = jnp.zeros((SB_T, 1), F32)
        state = blocks([i], (zero, zero), (zero, zero), jnp.zeros((SB_T, LANES), F32), True)
        state = lax.fori_loop(0, i // 2, lambda jj, c: blocks([i - 1 - 2 * jj, i - 2 - 2 * jj], c[0], c[1], c[2], False), state)
        state = lax.fori_loop(0, i % 2, lambda jj, c: blocks([0], c[0], c[1], c[2], False), state)
        dq_ref[...] = (state[2] * SB_SCALE).astype(BF16)

        @pl.when(i == nq - 1)
        def _():
            dk_ref[...] = dk_acc[...].astype(BF16)
            dv_ref[...] = dv_acc[...].astype(BF16)

        if n_parts:
            pl.when(step == n_steps - 1)(finish)

    q_spec, k_spec, v_spec = _sb_specs(n_pairs, 0)
    blk = pl.BlockSpec((None, SB_T, LANES), lambda b, p, i: (b, i, p))
    seq = pl.BlockSpec((None, SEQ, LANES), lambda b, p, i: (b, 0, p))
    shape = jax.ShapeDtypeStruct((bl, SEQ, SB_W), BF16)
    out = pl.pallas_call(
        body,
        grid=(bl, n_pairs, nq),
        in_specs=[q_spec, k_spec, v_spec, blk, blk] + [ANY] * n_parts,
        out_specs=[blk, seq, seq] + [ANY] * n_parts,
        out_shape=[shape, shape, shape] + [jax.ShapeDtypeStruct(p.shape, p.dtype) for p in parts],
        scratch_shapes=[pltpu.VMEM((SEQ, LANES), F32), pltpu.VMEM((SEQ, LANES), F32)]
        + (_chip_exchange_sems(n_parts) if n_parts else []),
        compiler_params=pltpu.CompilerParams(dimension_semantics=("arbitrary", "arbitrary", "arbitrary")),
        name="sb_bwd",
    )(proj3, proj3, proj3, o_a, do_a, *parts)
    return out[0], out[1], out[2], out[3:]


BAND = 128


BAND_CH = 4
BAND_HEADS = DIL_W // HEAD_DIM


def _swap_half(x):
    n = x.shape[-1]
    lane = lax.broadcasted_iota(jnp.int32, (1, n), 1)
    return jnp.where((lane % HEAD_DIM) < HEAD_DIM // 2, pltpu.roll(x, n - HEAD_DIM // 2, 1), pltpu.roll(x, HEAD_DIM // 2, 1))


def _rope(x, cos, sin_signed):
    x = x.astype(F32)
    return x * cos + _swap_half(x) * sin_signed


def _band_valid(g, blk):
    nb = jnp.where(g == 0, 16, jnp.where(g == 1, 4, 1))
    first_key = jnp.where(lax.rem(blk, nb) != 0, 0, BAND)
    qi = lax.broadcasted_iota(jnp.int32, (BAND, 2 * BAND), 0) + BAND
    kj = lax.broadcasted_iota(jnp.int32, (BAND, 2 * BAND), 1)
    dist = qi - kj
    return (dist >= 0) & (dist <= BAND) & (kj >= first_key)


def _band_specs():
    last_before = lambda i: jnp.maximum(i * BAND_CH - 1, 0)
    cur = lambda col: pl.BlockSpec((None, BAND_CH, BAND, DIL_W), lambda g, i: (g, i, 0, col))
    prev = lambda col: pl.BlockSpec((None, None, BAND, DIL_W), lambda g, i: (g, last_before(i), 0, col))
    tab = pl.BlockSpec((None, BAND_CH, BAND, DIL_W), lambda g, i: (g, lax.rem(i, 16 // BAND_CH), 0, 0))
    tab_prev = pl.BlockSpec((None, None, BAND, DIL_W), lambda g, i: (g, lax.rem(last_before(i), 16), 0, 0))
    return cur, prev, tab, tab_prev


def _band_load(q_ref, k_ref, kp_ref, v_ref, vp_ref, c_ref, s_ref, cp_ref, sp_ref):
    qs = [(_rope(q_ref[b], c_ref[b], s_ref[b]) * SB_SCALE).astype(BF16) for b in range(BAND_CH)]
    ks = [_rope(kp_ref[...], cp_ref[...], sp_ref[...]).astype(BF16)]
    ks += [_rope(k_ref[b], c_ref[b], s_ref[b]).astype(BF16) for b in range(BAND_CH)]
    vs = [vp_ref[...]] + [v_ref[b] for b in range(BAND_CH)]
    k2 = [jnp.concatenate([ks[b], ks[b + 1]], axis=0) for b in range(BAND_CH)]
    v2 = [jnp.concatenate([vs[b], vs[b + 1]], axis=0) for b in range(BAND_CH)]
    return qs, k2, v2


def _band_fwd(qkv_s, cos_t, sin_t):
    def body(q_ref, k_ref, kp_ref, v_ref, vp_ref, c_ref, s_ref, cp_ref, sp_ref, ol_ref):
        g, i = pl.program_id(0), pl.program_id(1)
        qs, k2, v2 = _band_load(q_ref, k_ref, kp_ref, v_ref, vp_ref, c_ref, s_ref, cp_ref, sp_ref)
        lane = lax.broadcasted_iota(jnp.int32, (1, DIL_W), 1)
        for b in range(BAND_CH):
            valid = _band_valid(g, i * BAND_CH + b)
            hs = range(BAND_HEADS)
            mh = [(lane // HEAD_DIM) == h for h in hs]
            s = [jnp.where(valid, _dot_nt(jnp.where(mh[h], qs[b], jnp.zeros_like(qs[b])), k2[b]), NEG_INF) for h in hs]
            m = [jnp.max(s[h], axis=1, keepdims=True) for h in hs]
            p = [jnp.exp(s[h] - m[h]) for h in hs]
            den = [jnp.sum(p[h], axis=1, keepdims=True) for h in hs]
            pv = [_dot(p[h].astype(BF16), jnp.where(mh[h], v2[b], jnp.zeros_like(v2[b]))) for h in hs]
            o = jnp.zeros((BAND, DIL_W), F32)
            lse = jnp.zeros((BAND, DIL_W), F32)
            for h in hs:
                o = o + pv[h] * (1.0 / den[h])
                lse = jnp.where(mh[h], m[h] + jnp.log(den[h]), lse)
            ol_ref[b, :, :DIL_W] = o
            ol_ref[b, :, DIL_W:] = lse

    cur, prev, tab, tab_prev = _band_specs()
    n_blk = qkv_s.shape[1]
    return pl.pallas_call(
        body,
        grid=(3, n_blk // BAND_CH),
        in_specs=[cur(0), cur(1), prev(1), cur(2), prev(2), tab, tab, tab_prev, tab_prev],
        out_specs=pl.BlockSpec((None, BAND_CH, BAND, 2 * DIL_W), lambda g, i: (g, i, 0, 0)),
        out_shape=jax.ShapeDtypeStruct((3, n_blk, BAND, 2 * DIL_W), F32),
        compiler_params=pltpu.CompilerParams(dimension_semantics=("parallel", "parallel")),
        name="band_fwd",
    )(qkv_s, qkv_s, qkv_s, qkv_s, qkv_s, cos_t, sin_t, cos_t, sin_t)


def _band_bwd(qkv_s, cos_t, sin_t, dcat_s):
    def body(q_ref, k_ref, kp_ref, v_ref, vp_ref, c_ref, s_ref, cp_ref, sp_ref, do_ref, lse_ref, dl_ref,
             dq_ref, dk_ref, dv_ref, dkf_ref, dvf_ref):
        g, i = pl.program_id(0), pl.program_id(1)
        qs, k2, v2 = _band_load(q_ref, k_ref, kp_ref, v_ref, vp_ref, c_ref, s_ref, cp_ref, sp_ref)
        lane = lax.broadcasted_iota(jnp.int32, (1, DIL_W), 1)
        dks, dvs = [], []
        for b in range(BAND_CH):
            valid = _band_valid(g, i * BAND_CH + b)
            do, lse, dl = do_ref[b].astype(BF16), lse_ref[b], dl_ref[b]
            hs = range(BAND_HEADS)
            mh = [(lane // HEAD_DIM) == h for h in hs]
            qh = [jnp.where(mh[h], qs[b], jnp.zeros_like(qs[b])) for h in hs]
            doh = [jnp.where(mh[h], do, jnp.zeros_like(do)) for h in hs]
            s = [_dot_nt(qh[h], k2[b]) for h in hs]
            dp = [_dot_nt(doh[h], v2[b]) for h in hs]
            p = [jnp.where(valid, jnp.exp(s[h] - lse[:, h * HEAD_DIM:h * HEAD_DIM + 1]), 0.0) for h in hs]
            ds = [(p[h] * (dp[h] - dl[:, h * HEAD_DIM:h * HEAD_DIM + 1])).astype(BF16) for h in hs]
            pb = [p[h].astype(BF16) for h in hs]
            dq = sum(_dot(ds[h], jnp.where(mh[h], k2[b], jnp.zeros_like(k2[b]))) for h in hs)
            dk2 = sum(_dot_tn(ds[h], qh[h]) for h in hs)
            dv2 = sum(_dot_tn(pb[h], doh[h]) for h in hs)
            dq_ref[b] = dq * SB_SCALE
            dks.append(dk2)
            dvs.append(dv2)
        dkf_ref[...] = dks[0][:BAND]
        dvf_ref[...] = dvs[0][:BAND]
        for b in range(BAND_CH):
            last = b == BAND_CH - 1
            dk_ref[b] = dks[b][BAND:] if last else dks[b][BAND:] + dks[b + 1][:BAND]
            dv_ref[b] = dvs[b][BAND:] if last else dvs[b][BAND:] + dvs[b + 1][:BAND]

    cur, prev, tab, tab_prev = _band_specs()
    first = pl.BlockSpec((None, None, BAND, DIL_W), lambda g, i: (g, i, 0, 0))
    n_blk = qkv_s.shape[1]
    n_chunks = n_blk // BAND_CH
    shape = jax.ShapeDtypeStruct((3, n_blk, BAND, DIL_W), F32)
    shape_first = jax.ShapeDtypeStruct((3, n_chunks, BAND, DIL_W), F32)
    return pl.pallas_call(
        body,
        grid=(3, n_chunks),
        in_specs=[cur(0), cur(1), prev(1), cur(2), prev(2), tab, tab, tab_prev, tab_prev, cur(0), cur(1), cur(2)],
        out_specs=[cur(0), cur(0), cur(0), first, first],
        out_shape=[shape, shape, shape, shape_first, shape_first],
        compiler_params=pltpu.CompilerParams(dimension_semantics=("parallel", "parallel")),
        name="band_bwd",
    )(qkv_s, qkv_s, qkv_s, qkv_s, qkv_s, cos_t, sin_t, cos_t, sin_t, dcat_s, dcat_s, dcat_s)


def _band_combine(dq, dk, dv, dk_first, dv_first, cos_t, sin_t):
    n_chunks = dk_first.shape[1]

    def body(dq_ref, dk_ref, dkn_ref, dv_ref, dvn_ref, c_ref, s_ref, out_ref):
        nxt = (pl.program_id(1) < n_chunks - 1).astype(F32)
        for b in range(BAND_CH):
            cos, sin = c_ref[b], s_ref[b]
            dq_b, dk_b, dv_b = dq_ref[b], dk_ref[b], dv_ref[b]
            if b == BAND_CH - 1:
                dk_b = dk_b + nxt * dkn_ref[...]
                dv_b = dv_b + nxt * dvn_ref[...]
            out_ref[b, :, :DIL_W] = (dq_b * cos - _swap_half(dq_b) * sin).astype(BF16)
            out_ref[b, :, DIL_W:2 * DIL_W] = (dk_b * cos - _swap_half(dk_b) * sin).astype(BF16)
            out_ref[b, :, 2 * DIL_W:] = dv_b.astype(BF16)

    cur, _, tab, _ = _band_specs()
    nxt = pl.BlockSpec((None, None, BAND, DIL_W), lambda g, i: (g, jnp.minimum(i + 1, n_chunks - 1), 0, 0))
    return pl.pallas_call(
        body,
        grid=(3, n_chunks),
        in_specs=[cur(0), cur(0), nxt, cur(0), nxt, tab, tab],
        out_specs=pl.BlockSpec((None, BAND_CH, BAND, 3 * DIL_W), lambda g, i: (g, i, 0, 0)),
        out_shape=jax.ShapeDtypeStruct(dq.shape[:3] + (3 * DIL_W,), BF16),
        compiler_params=pltpu.CompilerParams(dimension_semantics=("parallel", "parallel")),
        name="band_combine",
    )(dq, dk, dk_first, dv, dv_first, cos_t, sin_t)


def _band_merge(ol):
    t, tm = ol.shape[1], 512

    def body(o_ref, l_ref, ob_ref, lse_ref):
        l0, l1, l2 = l_ref[0], l_ref[1], l_ref[2]
        m = jnp.maximum(jnp.maximum(l0, l1), l2)
        lse = m + jnp.log(jnp.exp(l0 - m) + jnp.exp(l1 - m) + jnp.exp(l2 - m))
        ob_ref[...] = (jnp.exp(l0 - lse) * o_ref[0] + jnp.exp(l1 - lse) * o_ref[1] + jnp.exp(l2 - lse) * o_ref[2]).astype(BF16)
        lse_ref[...] = lse

    spec = pl.BlockSpec((tm, DIL_W), lambda i: (i, 0))
    return pl.pallas_call(
        body,
        grid=(t // tm,),
        in_specs=[pl.BlockSpec((3, tm, DIL_W), lambda i: (0, i, 0)), pl.BlockSpec((3, tm, DIL_W), lambda i: (0, i, 1))],
        out_specs=[spec, spec],
        out_shape=[jax.ShapeDtypeStruct((t, DIL_W), BF16), jax.ShapeDtypeStruct((t, DIL_W), F32)],
        compiler_params=pltpu.CompilerParams(dimension_semantics=("parallel",)),
        name="band_merge",
    )(ol, ol)


def _band_delta(do_b, o_b, lse_b):
    def fn(do, o, lse):
        r = lax.broadcasted_iota(jnp.int32, (DIL_W, DIL_W), 0) // HEAD_DIM
        c = lax.broadcasted_iota(jnp.int32, (DIL_W, DIL_W), 1) // HEAD_DIM
        do = do.astype(F32)
        delta = _split_dot(do * o.astype(F32), (r == c).astype(BF16))
        return (jnp.concatenate([do, lse, delta], axis=1),)

    return _rowwise(fn, [(do_b, "row"), (o_b, "row"), (lse_b, "row")], [(3 * DIL_W, F32, "row")], tm=512, name="band_delta")[0]


def _band_masks():
    qi = lax.broadcasted_iota(jnp.int32, (BAND, 2 * BAND), 0) + BAND
    kj = lax.broadcasted_iota(jnp.int32, (BAND, 2 * BAND), 1)
    dist = qi - kj
    row = lax.broadcasted_iota(jnp.int32, (BAND, BAND), 0)
    col = lax.broadcasted_iota(jnp.int32, (BAND, BAND), 1)
    return col <= row, (dist >= 0) & (dist <= BAND)


def _band_attend(q, k, v, valid):
    lane = lax.broadcasted_iota(jnp.int32, (1, DIL_W), 1)
    hs = range(BAND_HEADS)
    mh = [(lane // HEAD_DIM) == h for h in hs]
    s = [jnp.where(valid, _dot_nt(jnp.where(mh[h], q, jnp.zeros_like(q)), k), NEG_INF) for h in hs]
    m = [jnp.max(s[h], axis=1, keepdims=True) for h in hs]
    p = [jnp.exp(s[h] - m[h]) for h in hs]
    den = [jnp.sum(p[h], axis=1, keepdims=True) for h in hs]
    pv = [_dot(p[h].astype(BF16), jnp.where(mh[h], v, jnp.zeros_like(v))) for h in hs]
    o = jnp.zeros((BAND, DIL_W), F32)
    lse = jnp.zeros((BAND, DIL_W), F32)
    for h in hs:
        o = o + pv[h] * (1.0 / den[h])
        lse = jnp.where(mh[h], m[h] + jnp.log(den[h]), lse)
    return o, lse


def _band_attend_bwd(q, k, v, valid, do, lse, dl):
    lane = lax.broadcasted_iota(jnp.int32, (1, DIL_W), 1)
    hs = range(BAND_HEADS)
    mh = [(lane // HEAD_DIM) == h for h in hs]
    qh = [jnp.where(mh[h], q, jnp.zeros_like(q)) for h in hs]
    doh = [jnp.where(mh[h], do, jnp.zeros_like(do)) for h in hs]
    s = [_dot_nt(qh[h], k) for h in hs]
    dp = [_dot_nt(doh[h], v) for h in hs]
    p = [jnp.where(valid, jnp.exp(s[h] - lse[:, h * HEAD_DIM:h * HEAD_DIM + 1]), 0.0) for h in hs]
    ds = [(p[h] * (dp[h] - dl[:, h * HEAD_DIM:h * HEAD_DIM + 1])).astype(BF16) for h in hs]
    pb = [p[h].astype(BF16) for h in hs]
    dq = sum(_dot(ds[h], jnp.where(mh[h], k, jnp.zeros_like(k))) for h in hs)
    dk = sum(_dot_tn(ds[h], qh[h]) for h in hs)
    dv = sum(_dot_tn(pb[h], doh[h]) for h in hs)
    return dq, dk, dv


def _band_group_specs(lead, rows, cls, col0):
    def spec(width):
        if lead == "rows":
            return pl.BlockSpec((None, rows, width), lambda b, i: (b, 0, col0))
        return pl.BlockSpec((None, rows, cls * width), lambda b, i: (b, 0, i))
    return spec


def _band_group_fwd(a, cos_g, sin_g, *, rows, cls, steps, col0, name):
    bl = a.shape[0]
    nb = rows // BAND
    grp_w = 3 * DIL_W

    def body(a_ref, c_ref, s_ref, ol_ref, qr, kr):
        first_valid, later_valid = _band_masks()
        for j in range(cls):
            a0, t0, o0 = j * grp_w, j * DIL_W, j * 2 * DIL_W
            cos, sin = c_ref[:, t0:t0 + DIL_W], s_ref[:, t0:t0 + DIL_W]
            qr[...] = (_rope(a_ref[:, a0:a0 + DIL_W], cos, sin) * SB_SCALE).astype(BF16)
            kr[...] = _rope(a_ref[:, a0 + DIL_W:a0 + 2 * DIL_W], cos, sin).astype(BF16)

            def block(q0, k0, keys, valid, a0=a0, o0=o0):
                o, lse = _band_attend(qr[pl.ds(q0, BAND), :], kr[pl.ds(k0, keys), :],
                                      a_ref[pl.ds(k0, keys), a0 + 2 * DIL_W:a0 + grp_w], valid)
                ol_ref[pl.ds(q0, BAND), o0:o0 + DIL_W] = o
                ol_ref[pl.ds(q0, BAND), o0 + DIL_W:o0 + 2 * DIL_W] = lse

            block(0, 0, BAND, first_valid)
            if nb > 1:
                def later(b, carry, block=block):
                    block(pl.multiple_of(b * BAND, BAND), pl.multiple_of((b - 1) * BAND, BAND), 2 * BAND, later_valid)
                    return carry

                lax.fori_loop(1, nb, later, 0)

    lead = "rows" if col0 is not None else "cols"
    spec = _band_group_specs(lead, rows, cls, col0)
    tab = pl.BlockSpec((rows, cls * DIL_W), lambda b, i: (0, i))
    n_cls = cos_g.shape[1] // DIL_W
    return pl.pallas_call(
        body,
        grid=(bl, steps),
        in_specs=[spec(grp_w), tab, tab],
        out_specs=pl.BlockSpec((None, rows, cls * 2 * DIL_W), lambda b, i: (b, 0, i)),
        out_shape=jax.ShapeDtypeStruct((bl, rows, n_cls * 2 * DIL_W), F32),
        scratch_shapes=[pltpu.VMEM((rows, DIL_W), BF16), pltpu.VMEM((rows, DIL_W), BF16)],
        compiler_params=pltpu.CompilerParams(dimension_semantics=("parallel", "parallel")),
        name=name,
    )(a, cos_g, sin_g)


def _band_group_bwd(a, d, cos_g, sin_g, *, rows, cls, steps, col0, name):
    bl = a.shape[0]
    nb = rows // BAND
    grp_w = 3 * DIL_W

    def body(a_ref, d_ref, c_ref, s_ref, out_ref, qr, kr, dk_acc, dv_acc):
        first_valid, later_valid = _band_masks()
        for j in range(cls):
            a0, t0 = j * grp_w, j * DIL_W
            cos, sin = c_ref[:, t0:t0 + DIL_W], s_ref[:, t0:t0 + DIL_W]
            qr[...] = (_rope(a_ref[:, a0:a0 + DIL_W], cos, sin) * SB_SCALE).astype(BF16)
            kr[...] = _rope(a_ref[:, a0 + DIL_W:a0 + 2 * DIL_W], cos, sin).astype(BF16)
            dk_acc[...] = jnp.zeros_like(dk_acc)
            dv_acc[...] = jnp.zeros_like(dv_acc)

            def block(q0, k0, keys, valid, a0=a0, t0=t0):
                qrows, krows = pl.ds(q0, BAND), pl.ds(k0, keys)
                dq, dk, dv = _band_attend_bwd(
                    qr[qrows, :], kr[krows, :], a_ref[krows, a0 + 2 * DIL_W:a0 + grp_w], valid,
                    d_ref[qrows, a0:a0 + DIL_W].astype(BF16), d_ref[qrows, a0 + DIL_W:a0 + 2 * DIL_W],
                    d_ref[qrows, a0 + 2 * DIL_W:a0 + grp_w])
                dq = dq * SB_SCALE
                out_ref[qrows, a0:a0 + DIL_W] = (dq * c_ref[qrows, t0:t0 + DIL_W]
                                                 - _swap_half(dq) * s_ref[qrows, t0:t0 + DIL_W]).astype(BF16)
                dk_acc[krows, :] += dk
                dv_acc[krows, :] += dv

            block(0, 0, BAND, first_valid)
            if nb > 1:
                def later(b, carry, block=block):
                    block(pl.multiple_of(b * BAND, BAND), pl.multiple_of((b - 1) * BAND, BAND), 2 * BAND, later_valid)
                    return carry

                lax.fori_loop(1, nb, later, 0)
            dk = dk_acc[...]
            out_ref[:, a0 + DIL_W:a0 + 2 * DIL_W] = (dk * cos - _swap_half(dk) * sin).astype(BF16)
            out_ref[:, a0 + 2 * DIL_W:a0 + grp_w] = dv_acc[...].astype(BF16)

    lead = "rows" if col0 is not None else "cols"
    spec = _band_group_specs(lead, rows, cls, col0)
    dspec = _band_group_specs(lead, rows, cls, 0 if col0 is not None else None)
    tab = pl.BlockSpec((rows, cls * DIL_W), lambda b, i: (0, i))
    n_cls = cos_g.shape[1] // DIL_W
    return pl.pallas_call(
        body,
        grid=(bl, steps),
        in_specs=[spec(grp_w), dspec(grp_w), tab, tab],
        out_specs=pl.BlockSpec((None, rows, cls * grp_w), lambda b, i: (b, 0, i)),
        out_shape=jax.ShapeDtypeStruct((bl, rows, n_cls * grp_w), BF16),
        scratch_shapes=[pltpu.VMEM((rows, DIL_W), BF16), pltpu.VMEM((rows, DIL_W), BF16),
                        pltpu.VMEM((rows, DIL_W), F32), pltpu.VMEM((rows, DIL_W), F32)],
        compiler_params=pltpu.CompilerParams(dimension_semantics=("parallel", "parallel")),
        name=name,
    )(a, d, cos_g, sin_g)


def _band_merge3(ols):
    t, tm = ols[0].shape[0], 512

    def body(o0, l0, o1, l1, o2, l2, ob_ref, lse_ref):
        a, b, c = l0[...], l1[...], l2[...]
        m = jnp.maximum(jnp.maximum(a, b), c)
        lse = m + jnp.log(jnp.exp(a - m) + jnp.exp(b - m) + jnp.exp(c - m))
        ob_ref[...] = (jnp.exp(a - lse) * o0[...] + jnp.exp(b - lse) * o1[...] + jnp.exp(c - lse) * o2[...]).astype(BF16)
        lse_ref[...] = lse

    spec = pl.BlockSpec((tm, DIL_W), lambda i: (i, 0))
    spec_l = pl.BlockSpec((tm, DIL_W), lambda i: (i, 1))
    return pl.pallas_call(
        body,
        grid=(t // tm,),
        in_specs=[spec, spec_l] * 3,
        out_specs=[spec, spec],
        out_shape=[jax.ShapeDtypeStruct((t, DIL_W), BF16), jax.ShapeDtypeStruct((t, DIL_W), F32)],
        compiler_params=pltpu.CompilerParams(dimension_semantics=("parallel",)),
        name="band_merge",
    )(ols[0], ols[0], ols[1], ols[1], ols[2], ols[2])


MEM_T = 512
MEM_SCALE = 128 ** -0.5
MEM_Q_COL = (D_IN - MEM_W) // LANES


def _mem_specs():
    q = pl.BlockSpec((None, MEM_T, LANES), lambda b, h, i: (b, i, MEM_Q_COL + h))
    k = pl.BlockSpec((None, MEM_LEN, LANES), lambda b, h, i: (b, 0, h))
    v = pl.BlockSpec((None, MEM_LEN, LANES), lambda b, h, i: (b, 0, MEM_W // LANES + h))
    blk = pl.BlockSpec((None, MEM_T, LANES), lambda b, h, i: (b, i, h))
    return q, k, v, blk


def _mem_probs(q, k):
    s = _dot_nt(q, k) * MEM_SCALE
    p = jnp.exp(s - jnp.max(s, axis=1, keepdims=True))
    return p * (1.0 / jnp.sum(p, axis=1, keepdims=True))


def _mem_fwd(proj3, kv3):
    bl = proj3.shape[0]

    def body(q_ref, k_ref, v_ref, o_ref):
        p = _mem_probs(q_ref[...], k_ref[...])
        o_ref[...] = _dot(p.astype(BF16), v_ref[...]).astype(BF16)

    q, k, v, blk = _mem_specs()
    return pl.pallas_call(
        body,
        grid=(bl, MEM_W // LANES, SEQ // MEM_T),
        in_specs=[q, k, v],
        out_specs=blk,
        out_shape=jax.ShapeDtypeStruct((bl, SEQ, MEM_W), BF16),
        compiler_params=pltpu.CompilerParams(dimension_semantics=("parallel", "parallel", "parallel")),
        name="mem_fwd",
    )(proj3, kv3, kv3)


def _mem_bwd(proj3, kv3, do_c):
    bl = proj3.shape[0]

    def body(q_ref, k_ref, v_ref, do_ref, dq_ref, dk_ref, dv_ref):
        @pl.when(pl.program_id(2) == 0)
        def _():
            dk_ref[...] = jnp.zeros_like(dk_ref)
            dv_ref[...] = jnp.zeros_like(dv_ref)

        q, k, do = q_ref[...], k_ref[...], do_ref[...]
        p = _mem_probs(q, k)
        dp = _dot_nt(do, v_ref[...])
        ds = (p * (dp - jnp.sum(p * dp, axis=1, keepdims=True)) * MEM_SCALE).astype(BF16)
        dq_ref[...] = _dot(ds, k).astype(BF16)
        dk_ref[...] += _dot_tn(ds, q)
        dv_ref[...] += _dot_tn(p.astype(BF16), do)

    q, k, v, blk = _mem_specs()
    kv_out = pl.BlockSpec((None, MEM_LEN, LANES), lambda b, h, i: (b, 0, h))
    return pl.pallas_call(
        body,
        grid=(bl, MEM_W // LANES, SEQ // MEM_T),
        in_specs=[q, k, v, blk],
        out_specs=[blk, kv_out, kv_out],
        out_shape=[jax.ShapeDtypeStruct((bl, SEQ, MEM_W), BF16), jax.ShapeDtypeStruct((bl, MEM_LEN, MEM_W), F32),
                   jax.ShapeDtypeStruct((bl, MEM_LEN, MEM_W), F32)],
        compiler_params=pltpu.CompilerParams(dimension_semantics=("parallel", "parallel", "arbitrary")),
        name="mem_bwd",
    )(proj3, kv3, kv3, do_c)


def _place():
    x, y, c = lax.axis_index("x"), lax.axis_index("y"), lax.axis_index("c")
    return x, y, c


def _other_chips(x, y):
    return [(1 - x, y), (x, 1 - y), (1 - x, 1 - y)]


def _remote(src, dst, send_sem, recv_sem, to):
    return pltpu.make_async_remote_copy(src_ref=src, dst_ref=dst, send_sem=send_sem, recv_sem=recv_sem,
                                        device_id=to, device_id_type=MESH)


ANY = pl.BlockSpec(memory_space=pl.ANY)


def _gather_weights(shards):
    n = len(shards)

    def body(*refs):
        send, forward, finish = _gather_phases(refs[:n], refs[n:2 * n], *refs[2 * n:])
        send()
        forward()
        finish()

    return pl.pallas_call(
        body,
        in_specs=[ANY] * n,
        out_specs=[ANY] * n,
        out_shape=_gather_out_shapes(shards),
        scratch_shapes=_gather_sems(n),
        name="gather_weights",
    )(*shards)


def _gather_out_shapes(shards):
    return [jax.ShapeDtypeStruct((N_CHIPS,) + s.shape, s.dtype) for s in shards]


def _gather_sems(n):
    return [pltpu.SemaphoreType.DMA((6 * n,)), pltpu.SemaphoreType.DMA((6 * n,))]


def _gather_phases(in_refs, out_refs, send_sems, recv_sems):
    x, y, c = _place()
    sibling = (x, y, 1 - c)
    chips = _other_chips(x, y)
    first, passed = [], []
    for k in range(len(in_refs)):
        hf = in_refs[k].shape[0] // 2

        def half(px, py, pc, k=k, hf=hf):
            return out_refs[k].at[2 * px + py, pl.ds(pc * hf, hf), :]

        src = in_refs[k].at[pl.ds(c * hf, hf), :]
        for j, chip in enumerate(chips):
            s = 6 * k + j
            first.append(_remote(src, half(x, y, c), send_sems.at[s], recv_sems.at[s], (*chip, c)))
            passed.append((_remote(src, half(*chip, c), send_sems.at[s], recv_sems.at[s], (*chip, c)),
                           _remote(half(*chip, c), half(*chip, c), send_sems.at[s + 3], recv_sems.at[s + 3], sibling),
                           _remote(src, half(*chip, 1 - c), send_sems.at[s + 3], recv_sems.at[s + 3], sibling)))

    def send():
        for cp in first:
            cp.start()

    def forward():
        for landed, fwd, _ in passed:
            landed.wait_recv()
            fwd.start()

    def finish():
        for _, _, from_sibling in passed:
            from_sibling.wait_recv()
        for cp in first:
            cp.wait_send()
        for _, fwd, _ in passed:
            fwd.wait_send()

    return send, forward, finish


def _pair_exchange(grads, *, name):
    n = len(grads)

    def body(*refs):
        g_refs, land_refs = refs[:n], refs[n:2 * n]
        send_sems, recv_sems = refs[2 * n:]
        x, y, c = _place()
        cps = []
        for k in range(n):
            hf = g_refs[k].shape[1] // 2
            src = g_refs[k].at[:, pl.ds((1 - c) * hf, hf), :]
            cps.append(_remote(src, land_refs[k], send_sems.at[k], recv_sems.at[k], (x, y, 1 - c)))
        for cp in cps:
            cp.start()
        for cp in cps:
            cp.wait()

    return pl.pallas_call(
        body,
        in_specs=[ANY] * n,
        out_specs=[ANY] * n,
        out_shape=[jax.ShapeDtypeStruct((N_CHIPS, g.shape[1] // 2, g.shape[2]), F32) for g in grads],
        scratch_shapes=[pltpu.SemaphoreType.DMA((n,)), pltpu.SemaphoreType.DMA((n,))],
        name=name,
    )(*grads)


def _pair_add(g, land, c_arr, *, name):
    _, a, b = g.shape
    hf = a // 2

    def body(c_ref, g_ref, l_ref, o_ref):
        o_ref[...] = (g_ref[...] + l_ref[...]).astype(BF16)

    return pl.pallas_call(
        body,
        grid_spec=pltpu.PrefetchScalarGridSpec(
            num_scalar_prefetch=1,
            grid=(N_CHIPS,),
            in_specs=[pl.BlockSpec((None, None, hf, b), lambda s, c_ref: (s, c_ref[0], 0, 0)),
                      pl.BlockSpec((None, hf, b), lambda s, c_ref: (s, 0, 0))],
            out_specs=pl.BlockSpec((None, hf, b), lambda s, c_ref: (s, 0, 0)),
        ),
        out_shape=jax.ShapeDtypeStruct((N_CHIPS, hf, b), BF16),
        compiler_params=pltpu.CompilerParams(dimension_semantics=("parallel",)),
        name=name,
    )(c_arr, g.reshape(N_CHIPS, 2, hf, b), land)


def _chip_exchange(parts):
    n = len(parts)

    def body(*refs):
        send, finish = _chip_exchange_phases(refs[:n], refs[n:2 * n], *refs[2 * n:])
        send()
        finish()

    return pl.pallas_call(
        body,
        in_specs=[ANY] * n,
        out_specs=[ANY] * n,
        out_shape=[jax.ShapeDtypeStruct(p.shape, p.dtype) for p in parts],
        scratch_shapes=_chip_exchange_sems(n),
        name="chip_exchange",
    )(*parts)


def _chip_exchange_sems(n):
    return [pltpu.SemaphoreType.DMA((3 * n,)), pltpu.SemaphoreType.DMA((3 * n,))]


def _chip_exchange_phases(p_refs, land_refs, send_sems, recv_sems):
    x, y, c = _place()
    me = 2 * x + y
    sends, recvs = [], []
    for k in range(len(p_refs)):
        for j, (cx, cy) in enumerate(_other_chips(x, y)):
            s = 3 * k + j
            sends.append(_remote(p_refs[k].at[2 * cx + cy], land_refs[k].at[me], send_sems.at[s], recv_sems.at[s], (cx, cy, c)))
            recvs.append(_remote(p_refs[k].at[me], land_refs[k].at[2 * cx + cy], send_sems.at[s], recv_sems.at[s], (cx, cy, c)))

    def send():
        for cp in sends:
            cp.start()

    def finish():
        for cp in recvs:
            cp.wait_recv()
        for cp in sends:
            cp.wait_send()

    return send, finish


def _chip_add(land, part, me_arr, *, name):
    _, r, b = land.shape

    def body(me_ref, p_ref, l1_ref, l2_ref, l3_ref, o_ref):
        o_ref[...] = ((p_ref[...].astype(F32) + l1_ref[...].astype(F32)) + l2_ref[...].astype(F32)) + l3_ref[...].astype(F32)

    tr = r // 2
    other = lambda j: pl.BlockSpec((None, tr, b), lambda i, me_ref: (jnp.bitwise_xor(me_ref[0], j), i, 0))
    return pl.pallas_call(
        body,
        grid_spec=pltpu.PrefetchScalarGridSpec(
            num_scalar_prefetch=1,
            grid=(r // tr,),
            in_specs=[pl.BlockSpec((None, tr, b), lambda i, me_ref: (me_ref[0], i, 0)), other(2), other(1), other(3)],
            out_specs=pl.BlockSpec((tr, b), lambda i, me_ref: (i, 0)),
        ),
        out_shape=jax.ShapeDtypeStruct((r, b), F32),
        compiler_params=pltpu.CompilerParams(dimension_semantics=("parallel",)),
        name=name,
    )(me_arr, part, land, land, land)


def _pair_share(halves):
    n = len(halves)

    def body(*refs):
        h_refs, out_refs = refs[:n], refs[n:2 * n]
        send_sems, recv_sems = refs[2 * n:]
        x, y, c = _place()
        cps = [_remote(h_refs[k], out_refs[k], send_sems.at[k], recv_sems.at[k], (x, y, 1 - c)) for k in range(n)]
        for cp in cps:
            cp.start()
        for cp in cps:
            cp.wait()

    return pl.pallas_call(
        body,
        in_specs=[ANY] * n,
        out_specs=[ANY] * n,
        out_shape=[jax.ShapeDtypeStruct(h.shape, F32) for h in halves],
        scratch_shapes=[pltpu.SemaphoreType.DMA((n,)), pltpu.SemaphoreType.DMA((n,))],
        name="pair_share",
    )(*halves)


def _all_sum_small(part):
    def body(p_ref, o_ref, slots, send_sems, recv_sems):
        x, y, c = _place()
        me = 4 * x + 2 * y + c
        slots[me] = p_ref[...]
        peers = [(x ^ dx, y ^ dy, c ^ dc) for dx in (0, 1) for dy in (0, 1) for dc in (0, 1)][1:]
        sends = [_remote(p_ref, slots.at[me], send_sems.at[k], recv_sems.at[k], peer) for k, peer in enumerate(peers)]
        for cp in sends:
            cp.start()
        for k, (px, py, pc) in enumerate(peers):
            _remote(p_ref, slots.at[4 * px + 2 * py + pc], send_sems.at[k], recv_sems.at[k], (px, py, pc)).wait_recv()
        for cp in sends:
            cp.wait_send()
        acc = slots[0]
        for d in range(1, 8):
            acc = acc + slots[d]
        o_ref[...] = acc

    vmem = pl.BlockSpec(memory_space=pltpu.VMEM)
    return pl.pallas_call(
        body,
        in_specs=[vmem],
        out_specs=vmem,
        out_shape=jax.ShapeDtypeStruct(part.shape, F32),
        scratch_shapes=[pltpu.VMEM((8,) + part.shape, F32), pltpu.SemaphoreType.DMA((7,)), pltpu.SemaphoreType.DMA((7,))],
        name="all_sum_small",
    )(part)


def _deinterleave(a, d):
    b, s, c = a.shape
    return a.reshape(b, s // d, d, c).transpose(0, 2, 1, 3).reshape(b * s // BAND, BAND, c)


def _reinterleave(a, d, b):
    c = a.shape[-1]
    return a.reshape(b, d, SEQ // d, c).transpose(0, 2, 1, 3).reshape(b, SEQ, c)


def _rope_tables():
    half = HEAD_DIM // 2
    inv_freq = ROPE_THETA ** (-jnp.arange(half, dtype=F32) * 2.0 / HEAD_DIM)
    ang = jnp.arange(SEQ, dtype=F32)[:, None] * inv_freq[None, :]
    cos = jnp.tile(jnp.cos(ang), (1, 2 * BAND_HEADS))
    sin = jnp.tile(jnp.concatenate([-jnp.sin(ang), jnp.sin(ang)], axis=1), (1, BAND_HEADS))
    return cos, sin


def _band_groups():
    out = []
    for d in DIL_D:
        rows = SEQ // d
        cls = max(1, 512 // rows) if d > 1 else 1
        out.append(dict(rows=rows, cls=cls, steps=d // cls))
    return out


def _local_step(x, mem, loss_target, g_pre_mix, g_post_mix, g_pre_ffn, g_post_ffn, g_mem, b_gate, w, comm=None):
    bl = x.shape[0]
    t = bl * SEQ
    chips = range(N_CHIPS)
    w_in_full = jnp.concatenate([w["w_in"][s] for s in chips], axis=1)
    half_ff = D_FF // 2

    def with_gathered(w, names, gathered, shards):
        return {**w, **{name: lax.dynamic_update_slice(g, s[None], (comm["me"][0], 0, 0))
                        for name, g, s in zip(names, gathered, shards)}}

    x2 = x.reshape(t, D_MODEL)
    tgt2 = loss_target.reshape(t, D_MODEL)
    mem2 = mem.reshape(bl * MEM_LEN, D_MODEL)

    h = _norm_fwd(x2, g_pre_mix, name="norm_x")
    proj = _mm([(h, w_in_full)], nt=False, tn=2176, out_dtypes=[BF16], name="proj", gather=comm["mid_shards"] if comm else ())
    if comm:
        w = with_gathered(w, comm["mid_names"], proj[1], comm["mid_shards"])
        proj = proj[0]
    w_mem_kv_full = w["w_mem_kv"].reshape(D_MODEL, 2 * MEM_W)
    gates = _mm([(h, w["w_gate"], None, "j")], nt=False,tn=w["w_gate"].shape[2], out_dtypes=[BF16], name="gates",
                bias=b_gate, epilogue=lambda acc: (_sigmoid(acc),))
    hm = _norm_fwd(mem2, g_mem, name="norm_mem")
    kv_m = _mm([(hm, w_mem_kv_full)], nt=False,tn=1024, out_dtypes=[BF16], name="mem_kv")
    proj3 = proj.reshape(bl, SEQ, D_IN)
    kv3 = kv_m.reshape(bl, MEM_LEN, 2 * MEM_W)

    o_a, o_a32, late_gathered = _sb_fwd(proj3, comm["late_shards"] if comm else [])
    if comm:
        w = with_gathered(w, comm["late_names"], late_gathered, comm["late_shards"])
    w_o_full = w["w_o"].reshape(D_MODEL, D_MODEL)
    w_ffn_out_full = w["w_ffn_out"].reshape(D_FF, D_MODEL)

    cos_t, sin_t = _rope_tables()
    dil0 = 3 * SB_W

    grp_w = 3 * DIL_W
    band = []
    for g, (d, cfg) in enumerate(zip(DIL_D, _band_groups())):
        a_g = proj3 if d == 1 else proj3[:, :, dil0 + g * grp_w:dil0 + (g + 1) * grp_w].reshape(bl, SEQ // d, d * grp_w)
        band.append(dict(cfg, a=a_g, col0=dil0 // grp_w if d == 1 else None, cos=cos_t.reshape(SEQ // d, d * DIL_W),
                         sin=sin_t.reshape(SEQ // d, d * DIL_W)))
    ols = [_band_group_fwd(b["a"], b["cos"], b["sin"], rows=b["rows"], cls=b["cls"], steps=b["steps"], col0=b["col0"],
                           name=f"band_fwd_{g}").reshape(t, 2 * DIL_W) for g, b in enumerate(band)]
    o_b, lse_b = _band_merge3(ols)

    o_c = _mem_fwd(proj3, kv3)

    o_a2, o_c2 = o_a.reshape(t, SB_W), o_c.reshape(t, MEM_W)
    y_a, y_b, y_c, merged = _branch_merge_fwd(o_a2, o_b, o_c2, w["w_br_sb"], w["w_br_dil"], w["w_br_mem"], gates)
    mix = _mm([(merged, w_o_full)], nt=False,tn=1024, out_dtypes=[F32], name="mix")
    x1, h2 = _mid_fwd(mix, x2, g_post_mix, g_pre_ffn)
    gg, uu, f = _ffn_in_fwd(h2, w["w_ffn_in"])
    f2 = _mm([(f, w_ffn_out_full)], nt=False,tn=1024, out_dtypes=[F32], name="ffn_out")

    dy, df2, dg_post_ffn, loss_row = _loss_bwd(f2, x1, g_post_ffn, tgt2)

    dg_ffn, du_ffn = _mm([(df2, w_ffn_out_full)], nt=True,tn=half_ff, out_dtypes=[BF16, BF16], name="d_ffn_act",
                         extras=(gg, uu), epilogue=_swiglu_bwd_epilogue)
    gw = {}
    gw["w_ffn_out"] = _mm_tn(f, df2, tm=half_ff, tn=1024, name="gw_ffn_out").reshape(N_CHIPS, D_FF // N_CHIPS, D_MODEL)
    gw_ffn_g = _mm_tn(h2, dg_ffn, tm=1024, tn=half_ff, name="gw_ffn_gate", out_shards=True)
    gw_ffn_u = _mm_tn(h2, du_ffn, tm=1024, tn=half_ff, name="gw_ffn_up", out_shards=True)
    gw["w_ffn_in"] = jnp.concatenate([gw_ffn_g, gw_ffn_u], axis=0)
    dh2 = _mm([(dg_ffn, w["w_ffn_in"], 0, 0), (dg_ffn, w["w_ffn_in"], 1, 1), (du_ffn, w["w_ffn_in"], 0, 2),
               (du_ffn, w["w_ffn_in"], 1, 3)], nt=True,tn=1024, out_dtypes=[F32], name="d_h2")
    dx1, dmix, dg_pre_ffn, dg_post_mix = _mid_bwd(dh2, x1, mix, g_pre_ffn, g_post_mix, dy)

    gw["w_o"] = _mm_tn(merged, dmix, tm=1024, tn=1024, name="gw_o").reshape(N_CHIPS, D_MODEL // N_CHIPS, D_MODEL)
    dmerged = _mm([(dmix, w_o_full)], nt=True,tn=1024, out_dtypes=[F32], name="d_merged")
    dy_a, dy_b, dy_c, dgpre, db_gate = _gate_bwd(dmerged, gates, y_a, y_b, y_c)
    br_cols = D_MODEL // N_CHIPS
    gw["w_br_sb"] = _mm_tn(o_a2, dy_a, tm=512, tn=br_cols, name="gw_br_sb", out_shards=True)
    gw["w_br_dil"] = _mm_tn(o_b, dy_b, tm=256, tn=br_cols, name="gw_br_dil", out_shards=True)
    gw["w_br_mem"] = _mm_tn(o_c2, dy_c, tm=512, tn=br_cols, name="gw_br_mem", out_shards=True)
    gw["w_gate"] = _mm_tn(h, dgpre, tm=1024, tn=w["w_gate"].shape[2], name="gw_gate", out_shards=True)
    do_a = _mm([(dy_a, w["w_br_sb"], s, s) for s in chips], nt=True,tn=SB_W, out_dtypes=[BF16], name="d_o_a")
    do_b = _mm([(dy_b, w["w_br_dil"], s, s) for s in chips], nt=True,tn=DIL_W, out_dtypes=[BF16], name="d_o_b")
    do_c = _mm([(dy_c, w["w_br_mem"], s, s) for s in chips], nt=True,tn=MEM_W, out_dtypes=[BF16], name="d_o_c")

    dq_c, dk_m, dv_m = _mem_bwd(proj3, kv3, do_c.reshape(bl, SEQ, MEM_W))
    dkv_m = jnp.concatenate([dk_m, dv_m], axis=-1).reshape(bl * MEM_LEN, 2 * MEM_W).astype(BF16)
    gw["w_mem_kv"] = _mm_tn(hm, dkv_m, tm=1024, tn=1024, name="gw_mem_kv").reshape(N_CHIPS, D_MODEL // N_CHIPS, 2 * MEM_W)
    dhm = _mm([(dkv_m, w_mem_kv_full)], nt=True,tn=1024, out_dtypes=[F32], name="d_hm")
    dg_mem = _mem_norm_bwd(dhm, mem2, g_mem)

    dcat = _band_delta(do_b, o_b, lse_b)
    d_dil = [_band_group_bwd(b["a"], dcat.reshape(bl, SEQ // d, d * grp_w), b["cos"], b["sin"], rows=b["rows"], cls=b["cls"],
                             steps=b["steps"], col0=b["col0"], name=f"band_bwd_{g}").reshape(bl, SEQ, grp_w)
             for g, (d, b) in enumerate(zip(DIL_D, band))]

    early, parts = [], []
    if comm:
        early = [name for name, _, _ in PACK if name != "w_in"]
        grads = [gw[name] for name in early]
        parts = [_pair_add(g, l, comm["c"], name="pair_add_" + name)
                 for name, g, l in zip(early, grads, _pair_exchange(grads, name="pair_exchange_early"))]
    dq_a, dk_a, dv_a, lands = _sb_bwd(proj3, o_a32, do_a.reshape(bl, SEQ, SB_W), parts)
    reduced = {name: (p, l) for name, p, l in zip(early, parts, lands)}

    dproj = jnp.concatenate([dq_a, dk_a, dv_a] + d_dil + [dq_c], axis=-1).reshape(t, D_IN)
    in_cols = D_IN // N_CHIPS
    dproj_s = jnp.stack([dproj[:, s * in_cols:(s + 1) * in_cols] for s in chips])
    gw["w_in"] = _mm_tn(h, dproj_s, tm=1024, tn=in_cols, name="gw_in")
    dh = _mm([(dproj_s, w["w_in"], s, s) for s in chips] + [(dgpre, w["w_gate"], s, s) for s in chips],
             nt=True,tn=1024, out_dtypes=[F32], name="d_h")
    grad_x, dg_pre_mix = _first_bwd(dh, x2, g_pre_mix, dx1)
    small = jnp.concatenate([dg_pre_mix, dg_post_mix, dg_pre_ffn, dg_post_ffn, dg_mem, db_gate.reshape(3, D_MODEL)], axis=0)
    return loss_row[0, 0], grad_x.reshape(bl, SEQ, D_MODEL), gw, small, reduced


def kernel(x, mem, g_pre_mix, g_post_mix, g_pre_ffn, g_post_ffn, g_mem, w_in, w_mem_kv, w_br_sb, w_br_dil, w_br_mem, w_gate, b_gate, w_o, w_ffn_in, w_ffn_out, loss_target, m_g_pre_mix, m_g_post_mix, m_g_pre_ffn, m_g_post_ffn, m_g_mem, m_w_in, m_w_mem_kv, m_w_br_sb, m_w_br_dil, m_w_br_mem, m_w_gate, m_b_gate, m_w_o, m_w_ffn_in, m_w_ffn_out, v_g_pre_mix, v_g_post_mix, v_g_pre_ffn, v_g_post_ffn, v_g_mem, v_w_in, v_w_mem_kv, v_w_br_sb, v_w_br_dil, v_w_br_mem, v_w_gate, v_b_gate, v_w_o, v_w_ffn_in, v_w_ffn_out):
    w_shards = dict(w_in=w_in[0], w_mem_kv=w_mem_kv[0], w_br_sb=w_br_sb[0], w_br_dil=w_br_dil[0], w_br_mem=w_br_mem[0],
                    w_gate=w_gate[0], w_o=w_o[0], w_ffn_in=w_ffn_in[0], w_ffn_out=w_ffn_out[0])
    m_shards = dict(w_in=m_w_in[0], w_mem_kv=m_w_mem_kv[0], w_br_sb=m_w_br_sb[0], w_br_dil=m_w_br_dil[0], w_br_mem=m_w_br_mem[0],
                    w_gate=m_w_gate[0], w_o=m_w_o[0], w_ffn_in=m_w_ffn_in[0], w_ffn_out=m_w_ffn_out[0])
    v_shards = dict(w_in=v_w_in[0], w_mem_kv=v_w_mem_kv[0], w_br_sb=v_w_br_sb[0], w_br_dil=v_w_br_dil[0], w_br_mem=v_w_br_mem[0],
                    w_gate=v_w_gate[0], w_o=v_w_o[0], w_ffn_in=v_w_ffn_in[0], w_ffn_out=v_w_ffn_out[0])

    names = [name for name, _, _ in PACK]
    c_arr = lax.axis_index("c").astype(jnp.int32).reshape(1)
    me_arr = (2 * lax.axis_index("x") + lax.axis_index("y")).astype(jnp.int32).reshape(1)
    mid_names = ["w_gate", "w_mem_kv"]
    late_names = [name for name in names if name not in ["w_in"] + mid_names]
    bf = {name: w_shards[name].astype(BF16) for name in names}
    w = {"w_in": lax.dynamic_update_slice(_gather_weights([bf["w_in"]])[0], bf["w_in"][None], (me_arr[0], 0, 0))}
    comm = dict(c=c_arr, me=me_arr, mid_names=mid_names, mid_shards=[bf[name] for name in mid_names],
                late_names=late_names, late_shards=[bf[name] for name in late_names])

    loss_local, grad_x, gw, small, reduced = _local_step(x, mem, loss_target, g_pre_mix, g_post_mix, g_pre_ffn, g_post_ffn,
                                                         g_mem, b_gate, w, comm)
    loss = lax.psum(loss_local, ("x", "y", "c"))

    part_in = _pair_add(gw["w_in"], _pair_exchange([gw["w_in"]], name="pair_exchange_w_in")[0], c_arr, name="pair_add_w_in")
    reduced["w_in"] = (part_in, _chip_exchange([part_in])[0])
    halves = [_chip_add(reduced[name][1], reduced[name][0], me_arr, name="chip_add_" + name) for name in names]
    theirs = _pair_share(halves)
    small = _all_sum_small(small)

    upd = {}
    for name, mine, other in zip(names, halves, theirs):
        upd[name] = _adamw_halves(w_shards[name], mine, other, m_shards[name], v_shards[name], c_arr, name="adamw_" + name)
    g_shards = {name: u[0] for name, u in upd.items()}

    def small8(gs, b):
        return jnp.concatenate(gs + [b.reshape(3, D_MODEL)], axis=0)

    sw = small8([g_pre_mix, g_post_mix, g_pre_ffn, g_post_ffn, g_mem], b_gate)
    sm = small8([m_g_pre_mix, m_g_post_mix, m_g_pre_ffn, m_g_post_ffn, m_g_mem], m_b_gate)
    sv = small8([v_g_pre_mix, v_g_post_mix, v_g_pre_ffn, v_g_post_ffn, v_g_mem], v_b_gate)
    s_upd = _adamw(sw, small, sm, sv, tm=8, name="adamw_small")

    def small_out(a):
        return [a[0:1], a[1:2], a[2:3], a[3:4], a[4:5]]

    order = ["w_in", "w_mem_kv", "w_br_sb", "w_br_dil", "w_br_mem", "w_gate", "b_gate", "w_o", "w_ffn_in", "w_ffn_out"]

    def leaves(small_arr, big):
        out = small_out(small_arr)
        for name in order:
            out.append(small_arr[5:8].reshape(1, 3 * D_MODEL) if name == "b_gate" else big[name][None])
        return out

    grads_out = leaves(small, g_shards)
    delta_out = leaves(s_upd[0], {n: u[1] for n, u in upd.items()})
    m_out = leaves(s_upd[1], {n: u[2] for n, u in upd.items()})
    v_out = leaves(s_upd[2], {n: u[3] for n, u in upd.items()})
    return (loss, grad_x, *grads_out, *delta_out, *m_out, *v_out)
```

```python
import jax
import jax.numpy as jnp
import numpy as np
from jax import lax
from jax.experimental import pallas as pl
from jax.experimental.pallas import tpu as pltpu

F32 = jnp.float32
BF16 = jnp.bfloat16
MESH = pl.DeviceIdType.MESH

D_MODEL = 1024
SEQ = 2048
HEAD_DIM = 64
SB_W = 512
DIL_W = 256
MEM_W = 512
MEM_LEN = 256
D_IN = 3 * SB_W + 9 * DIL_W + MEM_W
D_FF = 2816
DIL_D = (1, 4, 16)
ROPE_THETA = 10000.0
NORM_EPS = 1e-6
NEG_INF = -1e30
LANES = 128

ADAM_LR = 0.001
ADAM_B1 = 0.9
ADAM_B2 = 0.999
ADAM_EPS = 1e-08
ADAM_WD = 0.01
ADAM_STEP = 10

N_CHIPS = 4
PACK = (
    ("w_in", (1024, 1088), 1),
    ("w_mem_kv", (256, 1024), 0),
    ("w_br_sb", (512, 256), 1),
    ("w_br_dil", (256, 256), 1),
    ("w_br_mem", (512, 256), 1),
    ("w_gate", (1024, 768), 1),
    ("w_o", (256, 1024), 0),
    ("w_ffn_in", (1024, 1408), 1),
    ("w_ffn_out", (704, 1024), 0),
)
PACK_ROWS = sum(a * b for _, (a, b), _ in PACK) // D_MODEL
HALF_ROWS = PACK_ROWS // 2


def _dot(a, b):
    return lax.dot_general(a, b, (((1,), (0,)), ((), ())), preferred_element_type=F32)


def _dot_nt(a, b):
    return lax.dot_general(a, b, (((1,), (1,)), ((), ())), preferred_element_type=F32)


def _dot_tn(a, b):
    return lax.dot_general(a, b, (((0,), (0,)), ((), ())), preferred_element_type=F32)


def _split_dot(x, u):
    hi = x.astype(BF16)
    lo = (x - hi.astype(F32)).astype(BF16)
    return _dot(hi, u) + _dot(lo, u)


V7X_VMEM_BUDGET = 44 * 2 ** 20


def _rows_that_fit(m, row_bytes, fixed_bytes):
    for tm in (1024, 512, 256, 128):
        if m % tm == 0 and fixed_bytes + tm * row_bytes <= V7X_VMEM_BUDGET:
            return tm
    return min(m, 128)


def _mm(pairs, *, nt, tn, out_dtypes, name, bias=None, extras=(), epilogue=None, gather=()):
    pairs = [p if len(p) == 4 else (p[0], p[1], None, None) for p in pairs]
    m = pairs[0][0].shape[-2]
    b0 = pairs[0][1]
    if nt:
        n = b0.shape[-2]
    else:
        n = b0.shape[-1] * (b0.shape[0] if b0.ndim == 3 else 1)
    n_pairs, n_extra, n_out = len(pairs), len(extras), len(out_dtypes)
    assert n % tn == 0
    one_col = n == tn
    ks = [(b.shape[-1] if nt else b.shape[-2]) for _, b, _, _ in pairs]
    fixed = sum(k * tn * 2 for k in ks) * (1 if one_col else 2)
    row_bytes = 2 * sum(k * 2 for k in ks) + 2 * tn * (sum(jnp.dtype(dt).itemsize for dt in out_dtypes) + 2 * n_extra) + 2 * tn * 4
    tm = _rows_that_fit(m, row_bytes, fixed)
    assert m % tm == 0
    b_mode = dict(pipeline_mode=pl.Buffered(1)) if one_col else {}
    has_bias = bias is not None
    n_side = len(gather)
    n_main_in = 2 * n_pairs + has_bias + n_extra
    n_steps = (n // tn) * (m // tm)

    def body(*refs):
        if n_side:
            side_in = refs[n_main_in:n_main_in + n_side]
            side_out = refs[n_main_in + n_side + n_out:n_main_in + 2 * n_side + n_out]
            send, forward, finish = _gather_phases(side_in, side_out, *refs[n_main_in + 2 * n_side + n_out:])
            step = pl.program_id(0) * (m // tm) + pl.program_id(1)
            pl.when(step == 0)(send)
            pl.when(step == n_steps // 2)(forward)
        acc = None
        for i in range(n_pairs):
            a, b = refs[2 * i][...], refs[2 * i + 1][...]
            p = _dot_nt(a, b) if nt else _dot(a, b)
            acc = p if acc is None else acc + p
        pos = 2 * n_pairs
        if has_bias:
            acc = acc + refs[pos][...]
            pos += 1
        ex = [r[...] for r in refs[pos:pos + n_extra]]
        outs = refs[n_main_in + n_side:n_main_in + n_side + n_out]
        vals = (acc,) if epilogue is None else epilogue(acc, *ex)
        for r, v, dt in zip(outs, vals, out_dtypes):
            r[...] = v.astype(dt)
        if n_side:
            pl.when(step == n_steps - 1)(finish)

    in_specs, args = [], []
    for a, b, a_col, b_sel in pairs:
        k = b.shape[-1] if nt else b.shape[-2]
        assert a_col is not None or a.shape[1] == k
        if a.ndim == 3:
            in_specs.append(pl.BlockSpec((None, tm, k), lambda j, i, c=a_col: (c, i, 0)))
        else:
            in_specs.append(pl.BlockSpec((tm, k), lambda j, i, c=a_col or 0: (i, c)))
        if b.ndim == 2:
            in_specs.append(pl.BlockSpec((tn, k), lambda j, i: (j, 0), **b_mode) if nt
                            else pl.BlockSpec((k, tn), lambda j, i: (0, j), **b_mode))
        elif nt:
            in_specs.append(pl.BlockSpec((None, tn, k), lambda j, i, s=b_sel: (s, j, 0), **b_mode))
        else:
            assert b_sel == "j" and b.shape[-1] == tn
            in_specs.append(pl.BlockSpec((None, k, tn), lambda j, i: (j, 0, 0), **b_mode))
        args += [a, b]
    if has_bias:
        in_specs.append(pl.BlockSpec((1, tn), lambda j, i: (0, j)))
        args.append(bias)
    for e in extras:
        in_specs.append(pl.BlockSpec((tm, tn), lambda j, i: (i, j)))
        args.append(e)
    out = pl.pallas_call(
        body,
        grid=(n // tn, m // tm),
        in_specs=in_specs + [ANY] * n_side,
        out_specs=[pl.BlockSpec((tm, tn), lambda j, i: (i, j)) for _ in range(n_out)] + [ANY] * n_side,
        out_shape=[jax.ShapeDtypeStruct((m, n), dt) for dt in out_dtypes] + _gather_out_shapes(gather),
        scratch_shapes=_gather_sems(n_side) if n_side else [],
        compiler_params=pltpu.CompilerParams(dimension_semantics=("arbitrary", "arbitrary") if n_side else ("parallel", "parallel")),
        name=name,
    )(*args, *gather)
    if n_side:
        return (out[0] if n_out == 1 else out[:n_out]), out[n_out:]
    return out[0] if n_out == 1 else out


def _mm_tn(a, b, *, tm, tn, name, out_shards=False):
    k, m = a.shape
    b_shards = b.ndim == 3
    out_shards = out_shards or b_shards
    n = b.shape[0] * b.shape[2] if b_shards else b.shape[1]
    tk = _rows_that_fit(k, 2 * 2 * (tm + tn), 3 * tm * tn * 4)
    assert m % tm == 0 and n % tn == 0 and k % tk == 0 and (not b_shards or b.shape[2] == tn)

    def body(a_ref, b_ref, o_ref):
        @pl.when(pl.program_id(2) == 0)
        def _():
            o_ref[...] = jnp.zeros_like(o_ref)

        o_ref[...] += _dot_tn(a_ref[...], b_ref[...])

    if b_shards:
        b_spec = pl.BlockSpec((None, tk, tn), lambda i, j, kk: (j, kk, 0))
    else:
        b_spec = pl.BlockSpec((tk, tn), lambda i, j, kk: (kk, j))
    if out_shards:
        out_spec = pl.BlockSpec((None, tm, tn), lambda i, j, kk: (j, i, 0))
        out_shape = jax.ShapeDtypeStruct((n // tn, m, tn), F32)
    else:
        out_spec = pl.BlockSpec((tm, tn), lambda i, j, kk: (i, j))
        out_shape = jax.ShapeDtypeStruct((m, n), F32)
    return pl.pallas_call(
        body,
        grid=(m // tm, n // tn, k // tk),
        in_specs=[pl.BlockSpec((tk, tm), lambda i, j, kk: (kk, i)), b_spec],
        out_specs=out_spec,
        out_shape=out_shape,
        compiler_params=pltpu.CompilerParams(dimension_semantics=("parallel", "parallel", "arbitrary")),
        name=name,
    )(a, b)


def _rowwise(fn, ins, outs, *, tm, name):
    rows = next(a.shape[0] for a, kind in ins if kind == "row")
    tm = min(tm, rows)
    assert rows % tm == 0
    n_in = len(ins)

    def body(*refs):
        vals = fn(*[r[...] for r in refs[:n_in]])
        for (_, dt, kind), r, v in zip(outs, refs[n_in:], vals):
            if kind == "row":
                r[...] = v.astype(dt)
            else:
                @pl.when(pl.program_id(0) == 0)
                def _(r=r):
                    r[...] = jnp.zeros_like(r)

                r[...] += v

    in_specs = [pl.BlockSpec((tm, a.shape[1]), lambda i: (i, 0)) if kind == "row" else pl.BlockSpec(a.shape, lambda i: (0, 0))
                for a, kind in ins]
    out_specs = [pl.BlockSpec((tm, c), lambda i: (i, 0)) if kind == "row" else pl.BlockSpec((1, c), lambda i: (0, 0))
                 for c, _, kind in outs]
    out_shape = [jax.ShapeDtypeStruct((rows if kind == "row" else 1, c), dt) for c, dt, kind in outs]
    has_acc = any(kind == "acc" for _, _, kind in outs)
    return pl.pallas_call(
        body,
        grid=(rows // tm,),
        in_specs=in_specs,
        out_specs=out_specs,
        out_shape=out_shape,
        compiler_params=pltpu.CompilerParams(dimension_semantics=("arbitrary" if has_acc else "parallel",)),
        name=name,
    )(*[a for a, _ in ins])


def _rstd(x):
    return lax.rsqrt(jnp.mean(x * x, axis=-1, keepdims=True) + NORM_EPS)


def _norm_bwd(dout, xin, g):
    r = _rstd(xin)
    n = xin * r
    dn = dout * g
    dg = jnp.sum(dout * n, axis=0, keepdims=True)
    dx = r * (dn - n * jnp.mean(dn * n, axis=-1, keepdims=True))
    return dx, dg


def _sigmoid(x):
    return 1.0 / (1.0 + jnp.exp(-x))


def _norm_fwd(x, g, *, name):
    def fn(x, g):
        return ((x * _rstd(x)) * g,)

    return _rowwise(fn, [(x, "row"), (g, "vec")], [(D_MODEL, BF16, "row")], tm=512, name=name)[0]


def _mid_fwd(mix, x, g_post_mix, g_pre_ffn):
    def fn(mix, x, g2, g3):
        x1 = x + (mix * _rstd(mix)) * g2
        return x1, (x1 * _rstd(x1)) * g3

    return _rowwise(fn, [(mix, "row"), (x, "row"), (g_post_mix, "vec"), (g_pre_ffn, "vec")],
                    [(D_MODEL, F32, "row"), (D_MODEL, BF16, "row")], tm=512, name="mid_fwd")


def _loss_bwd(f2, x1, g_post_ffn, tgt):
    def fn(f2, x1, g4, tgt):
        r = _rstd(f2)
        n = f2 * r
        err = x1 + n * g4 - tgt
        loss = 0.5 * jnp.sum(jnp.mean(err * err, axis=-1, keepdims=True), axis=0, keepdims=True)
        dy = err * (1.0 / D_MODEL)
        dn = dy * g4
        dg4 = jnp.sum(dy * n, axis=0, keepdims=True)
        df2 = r * (dn - n * jnp.mean(dn * n, axis=-1, keepdims=True))
        return dy, df2, dg4, jnp.broadcast_to(loss, (1, LANES))

    return _rowwise(fn, [(f2, "row"), (x1, "row"), (g_post_ffn, "vec"), (tgt, "row")],
                    [(D_MODEL, F32, "row"), (D_MODEL, BF16, "row"), (D_MODEL, F32, "acc"), (LANES, F32, "acc")],
                    tm=512, name="loss_bwd")


def _mid_bwd(dh2, x1, mix, g_pre_ffn, g_post_mix, dy):
    def fn(dh2, x1, mix, g3, g2, dy):
        d3, dg3 = _norm_bwd(dh2, x1, g3)
        dx1 = dy + d3
        dmix, dg2 = _norm_bwd(dx1, mix, g2)
        return dx1, dmix, dg3, dg2

    return _rowwise(fn, [(dh2, "row"), (x1, "row"), (mix, "row"), (g_pre_ffn, "vec"), (g_post_mix, "vec"), (dy, "row")],
                    [(D_MODEL, F32, "row"), (D_MODEL, BF16, "row"), (D_MODEL, F32, "acc"), (D_MODEL, F32, "acc")],
                    tm=256, name="mid_bwd")


def _first_bwd(dh, x, g_pre_mix, dx1):
    def fn(dh, x, g1, dx1):
        d1, dg1 = _norm_bwd(dh, x, g1)
        return dx1 + d1, dg1

    return _rowwise(fn, [(dh, "row"), (x, "row"), (g_pre_mix, "vec"), (dx1, "row")],
                    [(D_MODEL, F32, "row"), (D_MODEL, F32, "acc")], tm=512, name="first_bwd")


def _mem_norm_bwd(dhm, mem, g_mem):
    def fn(dhm, mem, g):
        return (jnp.sum(dhm * (mem * _rstd(mem)), axis=0, keepdims=True),)

    return _rowwise(fn, [(dhm, "row"), (mem, "row"), (g_mem, "vec")], [(D_MODEL, F32, "acc")], tm=512, name="mem_norm_bwd")[0]


def _gate_bwd(dmerged, gates, ya, yb, yc):
    def fn(dm, gt, ya, yb, yc):
        gt = gt.astype(F32)
        outs, dgp = [], []
        for i, y in enumerate((ya, yb, yc)):
            gi = gt[:, i * D_MODEL:(i + 1) * D_MODEL]
            outs.append(dm * gi)
            dgp.append(dm * y.astype(F32) * gi * (1.0 - gi))
        dgpre = jnp.concatenate(dgp, axis=1)
        return outs[0], outs[1], outs[2], dgpre, jnp.sum(dgpre, axis=0, keepdims=True)

    return _rowwise(fn, [(dmerged, "row"), (gates, "row"), (ya, "row"), (yb, "row"), (yc, "row")],
                    [(D_MODEL, BF16, "row")] * 3 + [(3 * D_MODEL, BF16, "row"), (3 * D_MODEL, F32, "acc")],
                    tm=256, name="gate_bwd")


def _adamw_math(w, g, m, v):
    m = ADAM_B1 * m + (1.0 - ADAM_B1) * g
    v = ADAM_B2 * v + (1.0 - ADAM_B2) * (g * g)
    m_hat = m / (1.0 - ADAM_B1 ** ADAM_STEP)
    v_hat = v / (1.0 - ADAM_B2 ** ADAM_STEP)
    delta = -ADAM_LR * (m_hat / (jnp.sqrt(v_hat) + ADAM_EPS) + ADAM_WD * w)
    return delta, m, v


def _adamw(w, g, m, v, *, tm, name):
    c = w.shape[1]
    return _rowwise(_adamw_math, [(w, "row"), (g, "row"), (m, "row"), (v, "row")], [(c, F32, "row")] * 3, tm=tm, name=name)


def _adamw_halves(w, g_mine, g_theirs, m, v, c_arr, *, name):
    a, b = w.shape
    hf = a // 2
    tr = hf // 4

    def body(c_ref, w_ref, gm_ref, gt_ref, m_ref, v_ref, g_out, d_out, m_out, v_out):
        g = jnp.where(pl.program_id(0) == c_ref[0], gm_ref[...], gt_ref[...])
        d, m_new, v_new = _adamw_math(w_ref[...], g, m_ref[...], v_ref[...])
        g_out[...] = g
        d_out[...] = d
        m_out[...] = m_new
        v_out[...] = v_new

    full = pl.BlockSpec((tr, b), lambda hh, i, c_ref: (hh * (hf // tr) + i, 0))
    half = pl.BlockSpec((tr, b), lambda hh, i, c_ref: (i, 0))
    return pl.pallas_call(
        body,
        grid_spec=pltpu.PrefetchScalarGridSpec(
            num_scalar_prefetch=1,
            grid=(2, hf // tr),
            in_specs=[full, half, half, full, full],
            out_specs=[full] * 4,
        ),
        out_shape=[jax.ShapeDtypeStruct((a, b), F32)] * 4,
        compiler_params=pltpu.CompilerParams(dimension_semantics=("parallel", "parallel")),
        name=name,
    )(c_arr, w, g_mine, g_theirs, m, v)


def _ffn_in_fwd(h2, w_ffn):
    m, tm, tn = h2.shape[0], 512, w_ffn.shape[2]
    assert 2 * tn == D_FF

    def body(h_ref, wg_ref, wu_ref, g_ref, u_ref, f_ref):
        h = h_ref[...]
        g = _dot(h, wg_ref[...])
        u = _dot(h, wu_ref[...])
        g_ref[...] = g.astype(BF16)
        u_ref[...] = u.astype(BF16)
        f_ref[...] = (g * _sigmoid(g) * u).astype(BF16)

    o_spec = pl.BlockSpec((tm, tn), lambda j, i: (i, j))
    return pl.pallas_call(
        body,
        grid=(D_FF // tn, m // tm),
        in_specs=[pl.BlockSpec((tm, D_MODEL), lambda j, i: (i, 0)),
                  pl.BlockSpec((None, D_MODEL, tn), lambda j, i: (j, 0, 0)),
                  pl.BlockSpec((None, D_MODEL, tn), lambda j, i: (j + 2, 0, 0))],
        out_specs=[o_spec, o_spec, o_spec],
        out_shape=[jax.ShapeDtypeStruct((m, D_FF), BF16)] * 3,
        compiler_params=pltpu.CompilerParams(dimension_semantics=("parallel", "parallel")),
        name="ffn_in_fwd",
    )(h2, w_ffn, w_ffn)


def _swiglu_bwd_epilogue(df, g, u):
    g = g.astype(F32)
    u = u.astype(F32)
    sg = _sigmoid(g)
    return df * u * (sg * (1.0 + g * (1.0 - sg))), df * (g * sg)


def _branch_merge_fwd(o_a, o_b, o_c, w_sb, w_dil, w_mem, gates):
    m, tm = o_a.shape[0], 256

    def body(oa_ref, ob_ref, oc_ref, wa_ref, wb_ref, wc_ref, gt_ref, ya_ref, yb_ref, yc_ref, mg_ref):
        def project(o_ref, w_ref):
            o = o_ref[...]
            return jnp.concatenate([_dot(o, w_ref[s]) for s in range(N_CHIPS)], axis=1)

        ya = project(oa_ref, wa_ref)
        yb = project(ob_ref, wb_ref)
        yc = project(oc_ref, wc_ref)
        gt = gt_ref[...].astype(F32)
        ya_ref[...] = ya.astype(BF16)
        yb_ref[...] = yb.astype(BF16)
        yc_ref[...] = yc.astype(BF16)
        mg_ref[...] = (gt[:, :D_MODEL] * ya + gt[:, D_MODEL:2 * D_MODEL] * yb + gt[:, 2 * D_MODEL:] * yc).astype(BF16)

    row = lambda c: pl.BlockSpec((tm, c), lambda i: (i, 0))
    full = lambda a: pl.BlockSpec(a.shape, lambda i: (0, 0, 0))
    return pl.pallas_call(
        body,
        grid=(m // tm,),
        in_specs=[row(SB_W), row(DIL_W), row(MEM_W), full(w_sb), full(w_dil), full(w_mem), row(3 * D_MODEL)],
        out_specs=[row(D_MODEL)] * 4,
        out_shape=[jax.ShapeDtypeStruct((m, D_MODEL), BF16)] * 4,
        compiler_params=pltpu.CompilerParams(dimension_semantics=("parallel",)),
        name="branch_merge_fwd",
    )(o_a, o_b, o_c, w_sb, w_dil, w_mem, gates)


SB_T = 256
SB_SCALE = HEAD_DIM ** -0.5


def _sb_masks():
    row = lax.broadcasted_iota(jnp.int32, (SB_T, SB_T), 0)
    col = lax.broadcasted_iota(jnp.int32, (SB_T, SB_T), 1)
    lane = lax.broadcasted_iota(jnp.int32, (1, LANES), 1)
    return row, col, lane


def _sb_logs(z):
    e = jnp.exp(-jnp.abs(z))
    lb = jnp.minimum(z, 0.0) - jnp.log(1.0 + e)
    return lb, lb - z, e


def _sb_specs(n_heads_pairs, col0):
    q = pl.BlockSpec((None, SB_T, LANES), lambda b, p, i: (b, i, col0 + p))
    k = pl.BlockSpec((None, SEQ, LANES), lambda b, p, i: (b, 0, col0 + n_heads_pairs + p))
    v = pl.BlockSpec((None, SEQ, LANES), lambda b, p, i: (b, 0, col0 + 2 * n_heads_pairs + p))
    return q, k, v


def _grid_step(n_pairs, nq):
    return (pl.program_id(0) * n_pairs + pl.program_id(1)) * nq + pl.program_id(2)


def _sb_fwd(proj3, late_shards):
    bl = proj3.shape[0]
    n_pairs = SB_W // LANES
    nq = SEQ // SB_T
    n_late = len(late_shards)
    n_steps = bl * n_pairs * nq

    def body(q_ref, k_ref, v_ref, *rest):
        late_in, (o_ref, o32_ref, w_ref), late_out = rest[:n_late], rest[n_late:n_late + 3], rest[n_late + 3:2 * n_late + 3]
        step = _grid_step(n_pairs, nq)
        if n_late:
            send, forward, finish = _gather_phases(late_in, late_out, *rest[2 * n_late + 3:])
            pl.when(step == 0)(send)
            pl.when(step == n_steps // 2)(forward)
        i = pl.program_id(2)
        row, col, lane = _sb_masks()
        causal = col < row
        u_excl = (row > col).astype(BF16)
        q = q_ref[...]
        heads = []
        for h in range(2):
            mh = (lane // HEAD_DIM) == h
            heads.append((mh, jnp.where(mh, q, jnp.zeros_like(q)) * SB_SCALE))

        def blocks(js, carries, acc, diag):
            ks = [k_ref[pl.ds(pl.multiple_of(j * SB_T, SB_T), SB_T), :] for j in js]
            vs = [v_ref[pl.ds(pl.multiple_of(j * SB_T, SB_T), SB_T), :] for j in js]
            chains = [(b, h) for b in range(len(js)) for h in range(2)]
            z = {c: _dot_nt(heads[c[1]][1], ks[c[0]]) for c in chains}
            lb, lk = {}, {}
            for c in chains:
                lb[c], lk[c], _ = _sb_logs(z[c])
                if diag:
                    lk[c] = jnp.where(causal, lk[c], 0.0)
            r = {c: _split_dot(lk[c], u_excl) for c in chains}
            carries = list(carries)
            w = {}
            for b, h in chains:
                w_c = jnp.exp(lb[b, h] + r[b, h] + carries[h])
                w[b, h] = (jnp.where(causal, w_c, 0.0) if diag else w_c).astype(BF16)
                w_ref[h, js[b]] = w[b, h]
                carries[h] = carries[h] + (r[b, h][:, :1] + lk[b, h][:, :1])
            for b, h in chains:
                acc = acc + _dot(w[b, h], jnp.where(heads[h][0], vs[b], jnp.zeros_like(vs[b])))
            return tuple(carries), acc

        zero = jnp.zeros((SB_T, 1), F32)
        carries, acc = blocks([i], (zero, zero), jnp.zeros((SB_T, LANES), F32), True)
        carries, acc = lax.fori_loop(0, i // 2, lambda jj, c: blocks([i - 1 - 2 * jj, i - 2 - 2 * jj], c[0], c[1], False),
                                     (carries, acc))
        carries, acc = lax.fori_loop(0, i % 2, lambda jj, c: blocks([0], c[0], c[1], False), (carries, acc))
        o_ref[...] = acc.astype(BF16)
        o32_ref[...] = acc
        if n_late:
            pl.when(step == n_steps - 1)(finish)

    q_spec, k_spec, v_spec = _sb_specs(n_pairs, 0)
    blk = pl.BlockSpec((None, SB_T, LANES), lambda b, p, i: (b, i, p))
    out = pl.pallas_call(
        body,
        grid=(bl, n_pairs, nq),
        in_specs=[q_spec, k_spec, v_spec] + [ANY] * n_late,
        out_specs=[blk, blk, _sb_weight_spec(nq)] + [ANY] * n_late,
        out_shape=[jax.ShapeDtypeStruct((bl, SEQ, SB_W), BF16), jax.ShapeDtypeStruct((bl, SEQ, SB_W), F32),
                   jax.ShapeDtypeStruct((bl, n_pairs, nq, 2, nq, SB_T, SB_T), BF16)] + _gather_out_shapes(late_shards),
        scratch_shapes=_gather_sems(n_late) if n_late else [],
        compiler_params=pltpu.CompilerParams(dimension_semantics=("arbitrary", "arbitrary", "arbitrary")),
        name="sb_fwd",
    )(proj3, proj3, proj3, *late_shards)
    return out[0], out[1], out[2], out[3:]


def _sb_weight_spec(nq):
    return pl.BlockSpec((None, None, None, 2, nq, SB_T, SB_T), lambda b, p, i: (b, p, i, 0, 0, 0, 0))


def _sb_bwd(proj3, o_a, do_a, w_all, parts):
    bl = proj3.shape[0]
    n_pairs = SB_W // LANES
    nq = SEQ // SB_T
    n_parts = len(parts)
    n_steps = bl * n_pairs * nq

    def body(q_ref, k_ref, v_ref, o_ref, do_ref, w_ref, *rest):
        p_refs, (dq_ref, dk_ref, dv_ref), land_refs = rest[:n_parts], rest[n_parts:n_parts + 3], rest[n_parts + 3:2 * n_parts + 3]
        dk_acc, dv_acc = rest[2 * n_parts + 3:2 * n_parts + 5]
        step = _grid_step(n_pairs, nq)
        if n_parts:
            send, finish = _chip_exchange_phases(p_refs, land_refs, *rest[2 * n_parts + 5:])
            pl.when(step == 0)(send)
        i = pl.program_id(2)

        @pl.when(i == 0)
        def _():
            dk_acc[...] = jnp.zeros_like(dk_acc)
            dv_acc[...] = jnp.zeros_like(dv_acc)

        row, col, lane = _sb_masks()
        causal = col < row
        u_incl = (row >= col).astype(BF16)
        q = q_ref[...]
        do = do_ref[...]
        prod = do.astype(F32) * o_ref[...]
        heads = []
        for h in range(2):
            mh = (lane // HEAD_DIM) == h
            d_tot = jnp.sum(jnp.where(mh, prod, 0.0), axis=1, keepdims=True)
            heads.append((mh, jnp.where(mh, q, jnp.zeros_like(q)) * SB_SCALE, jnp.where(mh, do, jnp.zeros_like(do)), d_tot))

        def blocks(js, c_das, dq, diag):
            starts = [pl.multiple_of(j * SB_T, SB_T) for j in js]
            ks = [k_ref[pl.ds(s, SB_T), :] for s in starts]
            vs = [v_ref[pl.ds(s, SB_T), :] for s in starts]
            chains = [(b, h) for b in range(len(js)) for h in range(2)]
            z = {c: _dot_nt(heads[c[1]][1], ks[c[0]]) for c in chains}
            dw = {c: _dot_nt(heads[c[1]][2], vs[c[0]]) for c in chains}
            wb = {(b, h): w_ref[h, js[b]] for b, h in chains}
            da = {c: dw[c] * wb[c].astype(F32) for c in chains}
            sfx = {c: _split_dot(da[c], u_incl) for c in chains}
            c_das = list(c_das)
            dz = {}
            for b, h in chains:
                dlk = heads[h][3] - c_das[h] - sfx[b, h]
                if diag:
                    dlk = jnp.where(causal, dlk, 0.0)
                c_das[h] = c_das[h] + sfx[b, h][:, :1]
                e = jnp.exp(-jnp.abs(z[b, h]))
                inv = 1.0 / (1.0 + e)
                pos = z[b, h] >= 0.0
                beta = jnp.where(pos, inv, e * inv)
                one_m_beta = jnp.where(pos, e * inv, inv)
                dz[b, h] = (da[b, h] * one_m_beta - dlk * beta).astype(BF16)
            for b, h in chains:
                dq = dq + _dot(dz[b, h], jnp.where(heads[h][0], ks[b], jnp.zeros_like(ks[b])))
            for b in range(len(js)):
                dk_acc[pl.ds(starts[b], SB_T), :] += _dot_tn(dz[b, 0], heads[0][1]) + _dot_tn(dz[b, 1], heads[1][1])
                dv_acc[pl.ds(starts[b], SB_T), :] += _dot_tn(wb[b, 0], heads[0][2]) + _dot_tn(wb[b, 1], heads[1][2])
            return tuple(c_das), dq

        zero = jnp.zeros((SB_T, 1), F32)
        state = blocks([i], (zero, zero), jnp.zeros((SB_T, LANES), F32), True)
        state = lax.fori_loop(0, i // 2, lambda jj, c: blocks([i - 1 - 2 * jj, i - 2 - 2 * jj], c[0], c[1], False), state)
        state = lax.fori_loop(0, i % 2, lambda jj, c: blocks([0], c[0], c[1], False), state)
        dq_ref[...] = (state[1] * SB_SCALE).astype(BF16)

        @pl.when(i == nq - 1)
        def _():
            dk_ref[...] = dk_acc[...].astype(BF16)
            dv_ref[...] = dv_acc[...].astype(BF16)

        if n_parts:
            pl.when(step == n_steps - 1)(finish)

    q_spec, k_spec, v_spec = _sb_specs(n_pairs, 0)
    blk = pl.BlockSpec((None, SB_T, LANES), lambda b, p, i: (b, i, p))
    seq = pl.BlockSpec((None, SEQ, LANES), lambda b, p, i: (b, 0, p))
    shape = jax.ShapeDtypeStruct((bl, SEQ, SB_W), BF16)
    out = pl.pallas_call(
        body,
        grid=(bl, n_pairs, nq),
        in_specs=[q_spec, k_spec, v_spec, blk, blk, _sb_weight_spec(nq)] + [ANY] * n_parts,
        out_specs=[blk, seq, seq] + [ANY] * n_parts,
        out_shape=[shape, shape, shape] + [jax.ShapeDtypeStruct(p.shape, p.dtype) for p in parts],
        scratch_shapes=[pltpu.VMEM((SEQ, LANES), F32), pltpu.VMEM((SEQ, LANES), F32)]
        + (_chip_exchange_sems(n_parts) if n_parts else []),
        compiler_params=pltpu.CompilerParams(dimension_semantics=("arbitrary", "arbitrary", "arbitrary")),
        name="sb_bwd",
    )(proj3, proj3, proj3, o_a, do_a, w_all, *parts)
    return out[0], out[1], out[2], out[3:]


BAND = 128


BAND_CH = 4
BAND_HEADS = DIL_W // HEAD_DIM


def _swap_half(x):
    n = x.shape[-1]
    lane = lax.broadcasted_iota(jnp.int32, (1, n), 1)
    return jnp.where((lane % HEAD_DIM) < HEAD_DIM // 2, pltpu.roll(x, n - HEAD_DIM // 2, 1), pltpu.roll(x, HEAD_DIM // 2, 1))


def _rope(x, cos, sin_signed):
    x = x.astype(F32)
    return x * cos + _swap_half(x) * sin_signed


def _band_valid(g, blk):
    nb = jnp.where(g == 0, 16, jnp.where(g == 1, 4, 1))
    first_key = jnp.where(lax.rem(blk, nb) != 0, 0, BAND)
    qi = lax.broadcasted_iota(jnp.int32, (BAND, 2 * BAND), 0) + BAND
    kj = lax.broadcasted_iota(jnp.int32, (BAND, 2 * BAND), 1)
    dist = qi - kj
    return (dist >= 0) & (dist <= BAND) & (kj >= first_key)


def _band_specs():
    last_before = lambda i: jnp.maximum(i * BAND_CH - 1, 0)
    cur = lambda col: pl.BlockSpec((None, BAND_CH, BAND, DIL_W), lambda g, i: (g, i, 0, col))
    prev = lambda col: pl.BlockSpec((None, None, BAND, DIL_W), lambda g, i: (g, last_before(i), 0, col))
    tab = pl.BlockSpec((None, BAND_CH, BAND, DIL_W), lambda g, i: (g, lax.rem(i, 16 // BAND_CH), 0, 0))
    tab_prev = pl.BlockSpec((None, None, BAND, DIL_W), lambda g, i: (g, lax.rem(last_before(i), 16), 0, 0))
    return cur, prev, tab, tab_prev


def _band_load(q_ref, k_ref, kp_ref, v_ref, vp_ref, c_ref, s_ref, cp_ref, sp_ref):
    qs = [(_rope(q_ref[b], c_ref[b], s_ref[b]) * SB_SCALE).astype(BF16) for b in range(BAND_CH)]
    ks = [_rope(kp_ref[...], cp_ref[...], sp_ref[...]).astype(BF16)]
    ks += [_rope(k_ref[b], c_ref[b], s_ref[b]).astype(BF16) for b in range(BAND_CH)]
    vs = [vp_ref[...]] + [v_ref[b] for b in range(BAND_CH)]
    k2 = [jnp.concatenate([ks[b], ks[b + 1]], axis=0) for b in range(BAND_CH)]
    v2 = [jnp.concatenate([vs[b], vs[b + 1]], axis=0) for b in range(BAND_CH)]
    return qs, k2, v2


def _band_fwd(qkv_s, cos_t, sin_t):
    def body(q_ref, k_ref, kp_ref, v_ref, vp_ref, c_ref, s_ref, cp_ref, sp_ref, ol_ref):
        g, i = pl.program_id(0), pl.program_id(1)
        qs, k2, v2 = _band_load(q_ref, k_ref, kp_ref, v_ref, vp_ref, c_ref, s_ref, cp_ref, sp_ref)
        lane = lax.broadcasted_iota(jnp.int32, (1, DIL_W), 1)
        for b in range(BAND_CH):
            valid = _band_valid(g, i * BAND_CH + b)
            hs = range(BAND_HEADS)
            mh = [(lane // HEAD_DIM) == h for h in hs]
            s = [jnp.where(valid, _dot_nt(jnp.where(mh[h], qs[b], jnp.zeros_like(qs[b])), k2[b]), NEG_INF) for h in hs]
            m = [jnp.max(s[h], axis=1, keepdims=True) for h in hs]
            p = [jnp.exp(s[h] - m[h]) for h in hs]
            den = [jnp.sum(p[h], axis=1, keepdims=True) for h in hs]
            pv = [_dot(p[h].astype(BF16), jnp.where(mh[h], v2[b], jnp.zeros_like(v2[b]))) for h in hs]
            o = jnp.zeros((BAND, DIL_W), F32)
            lse = jnp.zeros((BAND, DIL_W), F32)
            for h in hs:
                o = o + pv[h] * (1.0 / den[h])
                lse = jnp.where(mh[h], m[h] + jnp.log(den[h]), lse)
            ol_ref[b, :, :DIL_W] = o
            ol_ref[b, :, DIL_W:] = lse

    cur, prev, tab, tab_prev = _band_specs()
    n_blk = qkv_s.shape[1]
    return pl.pallas_call(
        body,
        grid=(3, n_blk // BAND_CH),
        in_specs=[cur(0), cur(1), prev(1), cur(2), prev(2), tab, tab, tab_prev, tab_prev],
        out_specs=pl.BlockSpec((None, BAND_CH, BAND, 2 * DIL_W), lambda g, i: (g, i, 0, 0)),
        out_shape=jax.ShapeDtypeStruct((3, n_blk, BAND, 2 * DIL_W), F32),
        compiler_params=pltpu.CompilerParams(dimension_semantics=("parallel", "parallel")),
        name="band_fwd",
    )(qkv_s, qkv_s, qkv_s, qkv_s, qkv_s, cos_t, sin_t, cos_t, sin_t)


def _band_bwd(qkv_s, cos_t, sin_t, dcat_s):
    def body(q_ref, k_ref, kp_ref, v_ref, vp_ref, c_ref, s_ref, cp_ref, sp_ref, do_ref, lse_ref, dl_ref,
             dq_ref, dk_ref, dv_ref, dkf_ref, dvf_ref):
        g, i = pl.program_id(0), pl.program_id(1)
        qs, k2, v2 = _band_load(q_ref, k_ref, kp_ref, v_ref, vp_ref, c_ref, s_ref, cp_ref, sp_ref)
        lane = lax.broadcasted_iota(jnp.int32, (1, DIL_W), 1)
        dks, dvs = [], []
        for b in range(BAND_CH):
            valid = _band_valid(g, i * BAND_CH + b)
            do, lse, dl = do_ref[b].astype(BF16), lse_ref[b], dl_ref[b]
            hs = range(BAND_HEADS)
            mh = [(lane // HEAD_DIM) == h for h in hs]
            qh = [jnp.where(mh[h], qs[b], jnp.zeros_like(qs[b])) for h in hs]
            doh = [jnp.where(mh[h], do, jnp.zeros_like(do)) for h in hs]
            s = [_dot_nt(qh[h], k2[b]) for h in hs]
            dp = [_dot_nt(doh[h], v2[b]) for h in hs]
            p = [jnp.where(valid, jnp.exp(s[h] - lse[:, h * HEAD_DIM:h * HEAD_DIM + 1]), 0.0) for h in hs]
            ds = [(p[h] * (dp[h] - dl[:, h * HEAD_DIM:h * HEAD_DIM + 1])).astype(BF16) for h in hs]
            pb = [p[h].astype(BF16) for h in hs]
            dq = sum(_dot(ds[h], jnp.where(mh[h], k2[b], jnp.zeros_like(k2[b]))) for h in hs)
            dk2 = sum(_dot_tn(ds[h], qh[h]) for h in hs)
            dv2 = sum(_dot_tn(pb[h], doh[h]) for h in hs)
            dq_ref[b] = dq * SB_SCALE
            dks.append(dk2)
            dvs.append(dv2)
        dkf_ref[...] = dks[0][:BAND]
        dvf_ref[...] = dvs[0][:BAND]
        for b in range(BAND_CH):
            last = b == BAND_CH - 1
            dk_ref[b] = dks[b][BAND:] if last else dks[b][BAND:] + dks[b + 1][:BAND]
            dv_ref[b] = dvs[b][BAND:] if last else dvs[b][BAND:] + dvs[b + 1][:BAND]

    cur, prev, tab, tab_prev = _band_specs()
    first = pl.BlockSpec((None, None, BAND, DIL_W), lambda g, i: (g, i, 0, 0))
    n_blk = qkv_s.shape[1]
    n_chunks = n_blk // BAND_CH
    shape = jax.ShapeDtypeStruct((3, n_blk, BAND, DIL_W), F32)
    shape_first = jax.ShapeDtypeStruct((3, n_chunks, BAND, DIL_W), F32)
    return pl.pallas_call(
        body,
        grid=(3, n_chunks),
        in_specs=[cur(0), cur(1), prev(1), cur(2), prev(2), tab, tab, tab_prev, tab_prev, cur(0), cur(1), cur(2)],
        out_specs=[cur(0), cur(0), cur(0), first, first],
        out_shape=[shape, shape, shape, shape_first, shape_first],
        compiler_params=pltpu.CompilerParams(dimension_semantics=("parallel", "parallel")),
        name="band_bwd",
    )(qkv_s, qkv_s, qkv_s, qkv_s, qkv_s, cos_t, sin_t, cos_t, sin_t, dcat_s, dcat_s, dcat_s)


def _band_combine(dq, dk, dv, dk_first, dv_first, cos_t, sin_t):
    n_chunks = dk_first.shape[1]

    def body(dq_ref, dk_ref, dkn_ref, dv_ref, dvn_ref, c_ref, s_ref, out_ref):
        nxt = (pl.program_id(1) < n_chunks - 1).astype(F32)
        for b in range(BAND_CH):
            cos, sin = c_ref[b], s_ref[b]
            dq_b, dk_b, dv_b = dq_ref[b], dk_ref[b], dv_ref[b]
            if b == BAND_CH - 1:
                dk_b = dk_b + nxt * dkn_ref[...]
                dv_b = dv_b + nxt * dvn_ref[...]
            out_ref[b, :, :DIL_W] = (dq_b * cos - _swap_half(dq_b) * sin).astype(BF16)
            out_ref[b, :, DIL_W:2 * DIL_W] = (dk_b * cos - _swap_half(dk_b) * sin).astype(BF16)
            out_ref[b, :, 2 * DIL_W:] = dv_b.astype(BF16)

    cur, _, tab, _ = _band_specs()
    nxt = pl.BlockSpec((None, None, BAND, DIL_W), lambda g, i: (g, jnp.minimum(i + 1, n_chunks - 1), 0, 0))
    return pl.pallas_call(
        body,
        grid=(3, n_chunks),
        in_specs=[cur(0), cur(0), nxt, cur(0), nxt, tab, tab],
        out_specs=pl.BlockSpec((None, BAND_CH, BAND, 3 * DIL_W), lambda g, i: (g, i, 0, 0)),
        out_shape=jax.ShapeDtypeStruct(dq.shape[:3] + (3 * DIL_W,), BF16),
        compiler_params=pltpu.CompilerParams(dimension_semantics=("parallel", "parallel")),
        name="band_combine",
    )(dq, dk, dk_first, dv, dv_first, cos_t, sin_t)


def _band_merge(ol):
    t, tm = ol.shape[1], 512

    def body(o_ref, l_ref, ob_ref, lse_ref):
        l0, l1, l2 = l_ref[0], l_ref[1], l_ref[2]
        m = jnp.maximum(jnp.maximum(l0, l1), l2)
        lse = m + jnp.log(jnp.exp(l0 - m) + jnp.exp(l1 - m) + jnp.exp(l2 - m))
        ob_ref[...] = (jnp.exp(l0 - lse) * o_ref[0] + jnp.exp(l1 - lse) * o_ref[1] + jnp.exp(l2 - lse) * o_ref[2]).astype(BF16)
        lse_ref[...] = lse

    spec = pl.BlockSpec((tm, DIL_W), lambda i: (i, 0))
    return pl.pallas_call(
        body,
        grid=(t // tm,),
        in_specs=[pl.BlockSpec((3, tm, DIL_W), lambda i: (0, i, 0)), pl.BlockSpec((3, tm, DIL_W), lambda i: (0, i, 1))],
        out_specs=[spec, spec],
        out_shape=[jax.ShapeDtypeStruct((t, DIL_W), BF16), jax.ShapeDtypeStruct((t, DIL_W), F32)],
        compiler_params=pltpu.CompilerParams(dimension_semantics=("parallel",)),
        name="band_merge",
    )(ol, ol)


def _band_delta(do_b, o_b, lse_b):
    def fn(do, o, lse):
        r = lax.broadcasted_iota(jnp.int32, (DIL_W, DIL_W), 0) // HEAD_DIM
        c = lax.broadcasted_iota(jnp.int32, (DIL_W, DIL_W), 1) // HEAD_DIM
        do = do.astype(F32)
        delta = _split_dot(do * o.astype(F32), (r == c).astype(BF16))
        return (jnp.concatenate([do, lse, delta], axis=1),)

    return _rowwise(fn, [(do_b, "row"), (o_b, "row"), (lse_b, "row")], [(3 * DIL_W, F32, "row")], tm=512, name="band_delta")[0]


def _band_masks():
    qi = lax.broadcasted_iota(jnp.int32, (BAND, 2 * BAND), 0) + BAND
    kj = lax.broadcasted_iota(jnp.int32, (BAND, 2 * BAND), 1)
    dist = qi - kj
    row = lax.broadcasted_iota(jnp.int32, (BAND, BAND), 0)
    col = lax.broadcasted_iota(jnp.int32, (BAND, BAND), 1)
    return col <= row, (dist >= 0) & (dist <= BAND)


def _band_attend(q, k, v, valid):
    lane = lax.broadcasted_iota(jnp.int32, (1, DIL_W), 1)
    hs = range(BAND_HEADS)
    mh = [(lane // HEAD_DIM) == h for h in hs]
    s = [jnp.where(valid, _dot_nt(jnp.where(mh[h], q, jnp.zeros_like(q)), k), NEG_INF) for h in hs]
    m = [jnp.max(s[h], axis=1, keepdims=True) for h in hs]
    p = [jnp.exp(s[h] - m[h]) for h in hs]
    den = [jnp.sum(p[h], axis=1, keepdims=True) for h in hs]
    pv = [_dot(p[h].astype(BF16), jnp.where(mh[h], v, jnp.zeros_like(v))) for h in hs]
    o = jnp.zeros((BAND, DIL_W), F32)
    lse = jnp.zeros((BAND, DIL_W), F32)
    for h in hs:
        o = o + pv[h] * (1.0 / den[h])
        lse = jnp.where(mh[h], m[h] + jnp.log(den[h]), lse)
    return o, lse


def _band_attend_bwd(q, k, v, valid, do, lse, dl):
    lane = lax.broadcasted_iota(jnp.int32, (1, DIL_W), 1)
    hs = range(BAND_HEADS)
    mh = [(lane // HEAD_DIM) == h for h in hs]
    qh = [jnp.where(mh[h], q, jnp.zeros_like(q)) for h in hs]
    doh = [jnp.where(mh[h], do, jnp.zeros_like(do)) for h in hs]
    s = [_dot_nt(qh[h], k) for h in hs]
    dp = [_dot_nt(doh[h], v) for h in hs]
    p = [jnp.where(valid, jnp.exp(s[h] - lse[:, h * HEAD_DIM:h * HEAD_DIM + 1]), 0.0) for h in hs]
    ds = [(p[h] * (dp[h] - dl[:, h * HEAD_DIM:h * HEAD_DIM + 1])).astype(BF16) for h in hs]
    pb = [p[h].astype(BF16) for h in hs]
    dq = sum(_dot(ds[h], jnp.where(mh[h], k, jnp.zeros_like(k))) for h in hs)
    dk = sum(_dot_tn(ds[h], qh[h]) for h in hs)
    dv = sum(_dot_tn(pb[h], doh[h]) for h in hs)
    return dq, dk, dv


def _band_group_specs(lead, rows, cls, col0):
    def spec(width):
        if lead == "rows":
            return pl.BlockSpec((None, rows, width), lambda b, i: (b, 0, col0))
        return pl.BlockSpec((None, rows, cls * width), lambda b, i: (b, 0, i))
    return spec


def _band_group_fwd(a, cos_g, sin_g, *, rows, cls, steps, col0, name):
    bl = a.shape[0]
    nb = rows // BAND
    grp_w = 3 * DIL_W

    def body(a_ref, c_ref, s_ref, ol_ref, qr, kr):
        first_valid, later_valid = _band_masks()
        for j in range(cls):
            a0, t0, o0 = j * grp_w, j * DIL_W, j * 2 * DIL_W
            cos, sin = c_ref[:, t0:t0 + DIL_W], s_ref[:, t0:t0 + DIL_W]
            qr[...] = (_rope(a_ref[:, a0:a0 + DIL_W], cos, sin) * SB_SCALE).astype(BF16)
            kr[...] = _rope(a_ref[:, a0 + DIL_W:a0 + 2 * DIL_W], cos, sin).astype(BF16)

            def block(q0, k0, keys, valid, a0=a0, o0=o0):
                o, lse = _band_attend(qr[pl.ds(q0, BAND), :], kr[pl.ds(k0, keys), :],
                                      a_ref[pl.ds(k0, keys), a0 + 2 * DIL_W:a0 + grp_w], valid)
                ol_ref[pl.ds(q0, BAND), o0:o0 + DIL_W] = o
                ol_ref[pl.ds(q0, BAND), o0 + DIL_W:o0 + 2 * DIL_W] = lse

            block(0, 0, BAND, first_valid)
            if nb > 1:
                def later(b, carry, block=block):
                    block(pl.multiple_of(b * BAND, BAND), pl.multiple_of((b - 1) * BAND, BAND), 2 * BAND, later_valid)
                    return carry

                lax.fori_loop(1, nb, later, 0)

    lead = "rows" if col0 is not None else "cols"
    spec = _band_group_specs(lead, rows, cls, col0)
    tab = pl.BlockSpec((rows, cls * DIL_W), lambda b, i: (0, i))
    n_cls = cos_g.shape[1] // DIL_W
    return pl.pallas_call(
        body,
        grid=(bl, steps),
        in_specs=[spec(grp_w), tab, tab],
        out_specs=pl.BlockSpec((None, rows, cls * 2 * DIL_W), lambda b, i: (b, 0, i)),
        out_shape=jax.ShapeDtypeStruct((bl, rows, n_cls * 2 * DIL_W), F32),
        scratch_shapes=[pltpu.VMEM((rows, DIL_W), BF16), pltpu.VMEM((rows, DIL_W), BF16)],
        compiler_params=pltpu.CompilerParams(dimension_semantics=("parallel", "parallel")),
        name=name,
    )(a, cos_g, sin_g)


def _band_group_bwd(a, d, cos_g, sin_g, *, rows, cls, steps, col0, name):
    bl = a.shape[0]
    nb = rows // BAND
    grp_w = 3 * DIL_W

    def body(a_ref, d_ref, c_ref, s_ref, out_ref, qr, kr, dk_acc, dv_acc):
        first_valid, later_valid = _band_masks()
        for j in range(cls):
            a0, t0 = j * grp_w, j * DIL_W
            cos, sin = c_ref[:, t0:t0 + DIL_W], s_ref[:, t0:t0 + DIL_W]
            qr[...] = (_rope(a_ref[:, a0:a0 + DIL_W], cos, sin) * SB_SCALE).astype(BF16)
            kr[...] = _rope(a_ref[:, a0 + DIL_W:a0 + 2 * DIL_W], cos, sin).astype(BF16)
            dk_acc[...] = jnp.zeros_like(dk_acc)
            dv_acc[...] = jnp.zeros_like(dv_acc)

            def block(q0, k0, keys, valid, a0=a0, t0=t0):
                qrows, krows = pl.ds(q0, BAND), pl.ds(k0, keys)
                dq, dk, dv = _band_attend_bwd(
                    qr[qrows, :], kr[krows, :], a_ref[krows, a0 + 2 * DIL_W:a0 + grp_w], valid,
                    d_ref[qrows, a0:a0 + DIL_W].astype(BF16), d_ref[qrows, a0 + DIL_W:a0 + 2 * DIL_W],
                    d_ref[qrows, a0 + 2 * DIL_W:a0 + grp_w])
                dq = dq * SB_SCALE
                out_ref[qrows, a0:a0 + DIL_W] = (dq * c_ref[qrows, t0:t0 + DIL_W]
                                                 - _swap_half(dq) * s_ref[qrows, t0:t0 + DIL_W]).astype(BF16)
                dk_acc[krows, :] += dk
                dv_acc[krows, :] += dv

            block(0, 0, BAND, first_valid)
            if nb > 1:
                def later(b, carry, block=block):
                    block(pl.multiple_of(b * BAND, BAND), pl.multiple_of((b - 1) * BAND, BAND), 2 * BAND, later_valid)
                    return carry

                lax.fori_loop(1, nb, later, 0)
            dk = dk_acc[...]
            out_ref[:, a0 + DIL_W:a0 + 2 * DIL_W] = (dk * cos - _swap_half(dk) * sin).astype(BF16)
            out_ref[:, a0 + 2 * DIL_W:a0 + grp_w] = dv_acc[...].astype(BF16)

    lead = "rows" if col0 is not None else "cols"
    spec = _band_group_specs(lead, rows, cls, col0)
    dspec = _band_group_specs(lead, rows, cls, 0 if col0 is not None else None)
    tab = pl.BlockSpec((rows, cls * DIL_W), lambda b, i: (0, i))
    n_cls = cos_g.shape[1] // DIL_W
    return pl.pallas_call(
        body,
        grid=(bl, steps),
        in_specs=[spec(grp_w), dspec(grp_w), tab, tab],
        out_specs=pl.BlockSpec((None, rows, cls * grp_w), lambda b, i: (b, 0, i)),
        out_shape=jax.ShapeDtypeStruct((bl, rows, n_cls * grp_w), BF16),
        scratch_shapes=[pltpu.VMEM((rows, DIL_W), BF16), pltpu.VMEM((rows, DIL_W), BF16),
                        pltpu.VMEM((rows, DIL_W), F32), pltpu.VMEM((rows, DIL_W), F32)],
        compiler_params=pltpu.CompilerParams(dimension_semantics=("parallel", "parallel")),
        name=name,
    )(a, d, cos_g, sin_g)


def _band_merge3(ols):
    t, tm = ols[0].shape[0], 512

    def body(o0, l0, o1, l1, o2, l2, ob_ref, lse_ref):
        a, b, c = l0[...], l1[...], l2[...]
        m = jnp.maximum(jnp.maximum(a, b), c)
        lse = m + jnp.log(jnp.exp(a - m) + jnp.exp(b - m) + jnp.exp(c - m))
        ob_ref[...] = (jnp.exp(a - lse) * o0[...] + jnp.exp(b - lse) * o1[...] + jnp.exp(c - lse) * o2[...]).astype(BF16)
        lse_ref[...] = lse

    spec = pl.BlockSpec((tm, DIL_W), lambda i: (i, 0))
    spec_l = pl.BlockSpec((tm, DIL_W), lambda i: (i, 1))
    return pl.pallas_call(
        body,
        grid=(t // tm,),
        in_specs=[spec, spec_l] * 3,
        out_specs=[spec, spec],
        out_shape=[jax.ShapeDtypeStruct((t, DIL_W), BF16), jax.ShapeDtypeStruct((t, DIL_W), F32)],
        compiler_params=pltpu.CompilerParams(dimension_semantics=("parallel",)),
        name="band_merge",
    )(ols[0], ols[0], ols[1], ols[1], ols[2], ols[2])


MEM_T = 512
MEM_SCALE = 128 ** -0.5
MEM_Q_COL = (D_IN - MEM_W) // LANES


def _mem_specs():
    q = pl.BlockSpec((None, MEM_T, LANES), lambda b, h, i: (b, i, MEM_Q_COL + h))
    k = pl.BlockSpec((None, MEM_LEN, LANES), lambda b, h, i: (b, 0, h))
    v = pl.BlockSpec((None, MEM_LEN, LANES), lambda b, h, i: (b, 0, MEM_W // LANES + h))
    blk = pl.BlockSpec((None, MEM_T, LANES), lambda b, h, i: (b, i, h))
    return q, k, v, blk


def _mem_probs(q, k):
    s = _dot_nt(q, k) * MEM_SCALE
    p = jnp.exp(s - jnp.max(s, axis=1, keepdims=True))
    return p * (1.0 / jnp.sum(p, axis=1, keepdims=True))


def _mem_fwd(proj3, kv3):
    bl = proj3.shape[0]

    def body(q_ref, k_ref, v_ref, o_ref):
        p = _mem_probs(q_ref[...], k_ref[...])
        o_ref[...] = _dot(p.astype(BF16), v_ref[...]).astype(BF16)

    q, k, v, blk = _mem_specs()
    return pl.pallas_call(
        body,
        grid=(bl, MEM_W // LANES, SEQ // MEM_T),
        in_specs=[q, k, v],
        out_specs=blk,
        out_shape=jax.ShapeDtypeStruct((bl, SEQ, MEM_W), BF16),
        compiler_params=pltpu.CompilerParams(dimension_semantics=("parallel", "parallel", "parallel")),
        name="mem_fwd",
    )(proj3, kv3, kv3)


def _mem_bwd(proj3, kv3, do_c):
    bl = proj3.shape[0]

    def body(q_ref, k_ref, v_ref, do_ref, dq_ref, dk_ref, dv_ref):
        @pl.when(pl.program_id(2) == 0)
        def _():
            dk_ref[...] = jnp.zeros_like(dk_ref)
            dv_ref[...] = jnp.zeros_like(dv_ref)

        q, k, do = q_ref[...], k_ref[...], do_ref[...]
        p = _mem_probs(q, k)
        dp = _dot_nt(do, v_ref[...])
        ds = (p * (dp - jnp.sum(p * dp, axis=1, keepdims=True)) * MEM_SCALE).astype(BF16)
        dq_ref[...] = _dot(ds, k).astype(BF16)
        dk_ref[...] += _dot_tn(ds, q)
        dv_ref[...] += _dot_tn(p.astype(BF16), do)

    q, k, v, blk = _mem_specs()
    kv_out = pl.BlockSpec((None, MEM_LEN, LANES), lambda b, h, i: (b, 0, h))
    return pl.pallas_call(
        body,
        grid=(bl, MEM_W // LANES, SEQ // MEM_T),
        in_specs=[q, k, v, blk],
        out_specs=[blk, kv_out, kv_out],
        out_shape=[jax.ShapeDtypeStruct((bl, SEQ, MEM_W), BF16), jax.ShapeDtypeStruct((bl, MEM_LEN, MEM_W), F32),
                   jax.ShapeDtypeStruct((bl, MEM_LEN, MEM_W), F32)],
        compiler_params=pltpu.CompilerParams(dimension_semantics=("parallel", "parallel", "arbitrary")),
        name="mem_bwd",
    )(proj3, kv3, kv3, do_c)


def _place():
    x, y, c = lax.axis_index("x"), lax.axis_index("y"), lax.axis_index("c")
    return x, y, c


def _other_chips(x, y):
    return [(1 - x, y), (x, 1 - y), (1 - x, 1 - y)]


def _remote(src, dst, send_sem, recv_sem, to):
    return pltpu.make_async_remote_copy(src_ref=src, dst_ref=dst, send_sem=send_sem, recv_sem=recv_sem,
                                        device_id=to, device_id_type=MESH)


ANY = pl.BlockSpec(memory_space=pl.ANY)


def _gather_weights(shards):
    n = len(shards)

    def body(*refs):
        send, forward, finish = _gather_phases(refs[:n], refs[n:2 * n], *refs[2 * n:])
        send()
        forward()
        finish()

    return pl.pallas_call(
        body,
        in_specs=[ANY] * n,
        out_specs=[ANY] * n,
        out_shape=_gather_out_shapes(shards),
        scratch_shapes=_gather_sems(n),
        name="gather_weights",
    )(*shards)


def _gather_out_shapes(shards):
    return [jax.ShapeDtypeStruct((N_CHIPS,) + s.shape, s.dtype) for s in shards]


def _gather_sems(n):
    return [pltpu.SemaphoreType.DMA((6 * n,)), pltpu.SemaphoreType.DMA((6 * n,))]


def _gather_phases(in_refs, out_refs, send_sems, recv_sems):
    x, y, c = _place()
    sibling = (x, y, 1 - c)
    chips = _other_chips(x, y)
    first, passed = [], []
    for k in range(len(in_refs)):
        hf = in_refs[k].shape[0] // 2

        def half(px, py, pc, k=k, hf=hf):
            return out_refs[k].at[2 * px + py, pl.ds(pc * hf, hf), :]

        src = in_refs[k].at[pl.ds(c * hf, hf), :]
        for j, chip in enumerate(chips):
            s = 6 * k + j
            first.append(_remote(src, half(x, y, c), send_sems.at[s], recv_sems.at[s], (*chip, c)))
            passed.append((_remote(src, half(*chip, c), send_sems.at[s], recv_sems.at[s], (*chip, c)),
                           _remote(half(*chip, c), half(*chip, c), send_sems.at[s + 3], recv_sems.at[s + 3], sibling),
                           _remote(src, half(*chip, 1 - c), send_sems.at[s + 3], recv_sems.at[s + 3], sibling)))

    def send():
        for cp in first:
            cp.start()

    def forward():
        for landed, fwd, _ in passed:
            landed.wait_recv()
            fwd.start()

    def finish():
        for _, _, from_sibling in passed:
            from_sibling.wait_recv()
        for cp in first:
            cp.wait_send()
        for _, fwd, _ in passed:
            fwd.wait_send()

    return send, forward, finish


def _pair_exchange(grads, *, name):
    n = len(grads)

    def body(*refs):
        g_refs, land_refs = refs[:n], refs[n:2 * n]
        send_sems, recv_sems = refs[2 * n:]
        x, y, c = _place()
        cps = []
        for k in range(n):
            hf = g_refs[k].shape[1] // 2
            src = g_refs[k].at[:, pl.ds((1 - c) * hf, hf), :]
            cps.append(_remote(src, land_refs[k], send_sems.at[k], recv_sems.at[k], (x, y, 1 - c)))
        for cp in cps:
            cp.start()
        for cp in cps:
            cp.wait()

    return pl.pallas_call(
        body,
        in_specs=[ANY] * n,
        out_specs=[ANY] * n,
        out_shape=[jax.ShapeDtypeStruct((N_CHIPS, g.shape[1] // 2, g.shape[2]), F32) for g in grads],
        scratch_shapes=[pltpu.SemaphoreType.DMA((n,)), pltpu.SemaphoreType.DMA((n,))],
        name=name,
    )(*grads)


def _pair_add(g, land, c_arr, *, name):
    _, a, b = g.shape
    hf = a // 2

    def body(c_ref, g_ref, l_ref, o_ref):
        o_ref[...] = (g_ref[...] + l_ref[...]).astype(BF16)

    return pl.pallas_call(
        body,
        grid_spec=pltpu.PrefetchScalarGridSpec(
            num_scalar_prefetch=1,
            grid=(N_CHIPS,),
            in_specs=[pl.BlockSpec((None, None, hf, b), lambda s, c_ref: (s, c_ref[0], 0, 0)),
                      pl.BlockSpec((None, hf, b), lambda s, c_ref: (s, 0, 0))],
            out_specs=pl.BlockSpec((None, hf, b), lambda s, c_ref: (s, 0, 0)),
        ),
        out_shape=jax.ShapeDtypeStruct((N_CHIPS, hf, b), BF16),
        compiler_params=pltpu.CompilerParams(dimension_semantics=("parallel",)),
        name=name,
    )(c_arr, g.reshape(N_CHIPS, 2, hf, b), land)


def _chip_exchange(parts):
    n = len(parts)

    def body(*refs):
        send, finish = _chip_exchange_phases(refs[:n], refs[n:2 * n], *refs[2 * n:])
        send()
        finish()

    return pl.pallas_call(
        body,
        in_specs=[ANY] * n,
        out_specs=[ANY] * n,
        out_shape=[jax.ShapeDtypeStruct(p.shape, p.dtype) for p in parts],
        scratch_shapes=_chip_exchange_sems(n),
        name="chip_exchange",
    )(*parts)


def _chip_exchange_sems(n):
    return [pltpu.SemaphoreType.DMA((3 * n,)), pltpu.SemaphoreType.DMA((3 * n,))]


def _chip_exchange_phases(p_refs, land_refs, send_sems, recv_sems):
    x, y, c = _place()
    me = 2 * x + y
    sends, recvs = [], []
    for k in range(len(p_refs)):
        for j, (cx, cy) in enumerate(_other_chips(x, y)):
            s = 3 * k + j
            sends.append(_remote(p_refs[k].at[2 * cx + cy], land_refs[k].at[me], send_sems.at[s], recv_sems.at[s], (cx, cy, c)))
            recvs.append(_remote(p_refs[k].at[me], land_refs[k].at[2 * cx + cy], send_sems.at[s], recv_sems.at[s], (cx, cy, c)))

    def send():
        for cp in sends:
            cp.start()

    def finish():
        for cp in recvs:
            cp.wait_recv()
        for cp in sends:
            cp.wait_send()

    return send, finish


def _chip_add(land, part, me_arr, *, name):
    _, r, b = land.shape

    def body(me_ref, p_ref, l1_ref, l2_ref, l3_ref, o_ref):
        o_ref[...] = ((p_ref[...].astype(F32) + l1_ref[...].astype(F32)) + l2_ref[...].astype(F32)) + l3_ref[...].astype(F32)

    tr = r // 2
    other = lambda j: pl.BlockSpec((None, tr, b), lambda i, me_ref: (jnp.bitwise_xor(me_ref[0], j), i, 0))
    return pl.pallas_call(
        body,
        grid_spec=pltpu.PrefetchScalarGridSpec(
            num_scalar_prefetch=1,
            grid=(r // tr,),
            in_specs=[pl.BlockSpec((None, tr, b), lambda i, me_ref: (me_ref[0], i, 0)), other(2), other(1), other(3)],
            out_specs=pl.BlockSpec((tr, b), lambda i, me_ref: (i, 0)),
        ),
        out_shape=jax.ShapeDtypeStruct((r, b), F32),
        compiler_params=pltpu.CompilerParams(dimension_semantics=("parallel",)),
        name=name,
    )(me_arr, part, land, land, land)


def _pair_share(halves):
    n = len(halves)

    def body(*refs):
        h_refs, out_refs = refs[:n], refs[n:2 * n]
        send_sems, recv_sems = refs[2 * n:]
        x, y, c = _place()
        cps = [_remote(h_refs[k], out_refs[k], send_sems.at[k], recv_sems.at[k], (x, y, 1 - c)) for k in range(n)]
        for cp in cps:
            cp.start()
        for cp in cps:
            cp.wait()

    return pl.pallas_call(
        body,
        in_specs=[ANY] * n,
        out_specs=[ANY] * n,
        out_shape=[jax.ShapeDtypeStruct(h.shape, F32) for h in halves],
        scratch_shapes=[pltpu.SemaphoreType.DMA((n,)), pltpu.SemaphoreType.DMA((n,))],
        name="pair_share",
    )(*halves)


def _all_sum_small(part):
    def body(p_ref, o_ref, slots, send_sems, recv_sems):
        x, y, c = _place()
        me = 4 * x + 2 * y + c
        slots[me] = p_ref[...]
        peers = [(x ^ dx, y ^ dy, c ^ dc) for dx in (0, 1) for dy in (0, 1) for dc in (0, 1)][1:]
        sends = [_remote(p_ref, slots.at[me], send_sems.at[k], recv_sems.at[k], peer) for k, peer in enumerate(peers)]
        for cp in sends:
            cp.start()
        for k, (px, py, pc) in enumerate(peers):
            _remote(p_ref, slots.at[4 * px + 2 * py + pc], send_sems.at[k], recv_sems.at[k], (px, py, pc)).wait_recv()
        for cp in sends:
            cp.wait_send()
        acc = slots[0]
        for d in range(1, 8):
            acc = acc + slots[d]
        o_ref[...] = acc

    vmem = pl.BlockSpec(memory_space=pltpu.VMEM)
    return pl.pallas_call(
        body,
        in_specs=[vmem],
        out_specs=vmem,
        out_shape=jax.ShapeDtypeStruct(part.shape, F32),
        scratch_shapes=[pltpu.VMEM((8,) + part.shape, F32), pltpu.SemaphoreType.DMA((7,)), pltpu.SemaphoreType.DMA((7,))],
        name="all_sum_small",
    )(part)


def _deinterleave(a, d):
    b, s, c = a.shape
    return a.reshape(b, s // d, d, c).transpose(0, 2, 1, 3).reshape(b * s // BAND, BAND, c)


def _reinterleave(a, d, b):
    c = a.shape[-1]
    return a.reshape(b, d, SEQ // d, c).transpose(0, 2, 1, 3).reshape(b, SEQ, c)


def _rope_tables():
    half = HEAD_DIM // 2
    inv_freq = np.float32(ROPE_THETA) ** (-np.arange(half, dtype=np.float32) * np.float32(2.0) / np.float32(HEAD_DIM))
    ang = np.arange(SEQ, dtype=np.float32)[:, None] * inv_freq[None, :].astype(np.float32)
    cos = np.tile(np.cos(ang).astype(np.float32), (1, 2 * BAND_HEADS))
    sin = np.tile(np.concatenate([-np.sin(ang), np.sin(ang)], axis=1).astype(np.float32), (1, BAND_HEADS))
    return jnp.asarray(cos), jnp.asarray(sin)


def _band_groups():
    out = []
    for d in DIL_D:
        rows = SEQ // d
        cls = max(1, 512 // rows) if d > 1 else 1
        out.append(dict(rows=rows, cls=cls, steps=d // cls))
    return out


def _local_step(x, mem, loss_target, g_pre_mix, g_post_mix, g_pre_ffn, g_post_ffn, g_mem, b_gate, w, comm=None):
    bl = x.shape[0]
    t = bl * SEQ
    chips = range(N_CHIPS)
    w_in_full = jnp.concatenate([w["w_in"][s] for s in chips], axis=1)
    half_ff = D_FF // 2

    def with_gathered(w, names, gathered, shards):
        return {**w, **{name: lax.dynamic_update_slice(g, s[None], (comm["me"][0], 0, 0))
                        for name, g, s in zip(names, gathered, shards)}}

    x2 = x.reshape(t, D_MODEL)
    tgt2 = loss_target.reshape(t, D_MODEL)
    mem2 = mem.reshape(bl * MEM_LEN, D_MODEL)

    h = _norm_fwd(x2, g_pre_mix, name="norm_x")
    proj = _mm([(h, w_in_full)], nt=False, tn=2176, out_dtypes=[BF16], name="proj", gather=comm["mid_shards"] if comm else ())
    if comm:
        w = with_gathered(w, comm["mid_names"], proj[1], comm["mid_shards"])
        proj = proj[0]
    w_mem_kv_full = w["w_mem_kv"].reshape(D_MODEL, 2 * MEM_W)
    gates = _mm([(h, w["w_gate"], None, "j")], nt=False,tn=w["w_gate"].shape[2], out_dtypes=[BF16], name="gates",
                bias=b_gate, epilogue=lambda acc: (_sigmoid(acc),))
    hm = _norm_fwd(mem2, g_mem, name="norm_mem")
    kv_m = _mm([(hm, w_mem_kv_full)], nt=False,tn=1024, out_dtypes=[BF16], name="mem_kv")
    proj3 = proj.reshape(bl, SEQ, D_IN)
    kv3 = kv_m.reshape(bl, MEM_LEN, 2 * MEM_W)

    o_a, o_a32, sb_weights, late_gathered = _sb_fwd(proj3, comm["late_shards"] if comm else [])
    if comm:
        w = with_gathered(w, comm["late_names"], late_gathered, comm["late_shards"])
    w_o_full = w["w_o"].reshape(D_MODEL, D_MODEL)
    w_ffn_out_full = w["w_ffn_out"].reshape(D_FF, D_MODEL)

    cos_t, sin_t = _rope_tables()
    dil0 = 3 * SB_W

    grp_w = 3 * DIL_W
    band = []
    for g, (d, cfg) in enumerate(zip(DIL_D, _band_groups())):
        a_g = proj3 if d == 1 else proj3[:, :, dil0 + g * grp_w:dil0 + (g + 1) * grp_w].reshape(bl, SEQ // d, d * grp_w)
        band.append(dict(cfg, a=a_g, col0=dil0 // grp_w if d == 1 else None, cos=cos_t.reshape(SEQ // d, d * DIL_W),
                         sin=sin_t.reshape(SEQ // d, d * DIL_W)))
    ols = [_band_group_fwd(b["a"], b["cos"], b["sin"], rows=b["rows"], cls=b["cls"], steps=b["steps"], col0=b["col0"],
                           name=f"band_fwd_{g}").reshape(t, 2 * DIL_W) for g, b in enumerate(band)]
    o_b, lse_b = _band_merge3(ols)

    o_c = _mem_fwd(proj3, kv3)

    o_a2, o_c2 = o_a.reshape(t, SB_W), o_c.reshape(t, MEM_W)
    y_a, y_b, y_c, merged = _branch_merge_fwd(o_a2, o_b, o_c2, w["w_br_sb"], w["w_br_dil"], w["w_br_mem"], gates)
    mix = _mm([(merged, w_o_full)], nt=False,tn=1024, out_dtypes=[F32], name="mix")
    x1, h2 = _mid_fwd(mix, x2, g_post_mix, g_pre_ffn)
    gg, uu, f = _ffn_in_fwd(h2, w["w_ffn_in"])
    f2 = _mm([(f, w_ffn_out_full)], nt=False,tn=1024, out_dtypes=[F32], name="ffn_out")

    dy, df2, dg_post_ffn, loss_row = _loss_bwd(f2, x1, g_post_ffn, tgt2)

    dg_ffn, du_ffn = _mm([(df2, w_ffn_out_full)], nt=True,tn=half_ff, out_dtypes=[BF16, BF16], name="d_ffn_act",
                         extras=(gg, uu), epilogue=_swiglu_bwd_epilogue)
    gw = {}
    gw["w_ffn_out"] = _mm_tn(f, df2, tm=half_ff, tn=1024, name="gw_ffn_out").reshape(N_CHIPS, D_FF // N_CHIPS, D_MODEL)
    gw_ffn_g = _mm_tn(h2, dg_ffn, tm=1024, tn=half_ff, name="gw_ffn_gate", out_shards=True)
    gw_ffn_u = _mm_tn(h2, du_ffn, tm=1024, tn=half_ff, name="gw_ffn_up", out_shards=True)
    gw["w_ffn_in"] = jnp.concatenate([gw_ffn_g, gw_ffn_u], axis=0)
    dh2 = _mm([(dg_ffn, w["w_ffn_in"], 0, 0), (dg_ffn, w["w_ffn_in"], 1, 1), (du_ffn, w["w_ffn_in"], 0, 2),
               (du_ffn, w["w_ffn_in"], 1, 3)], nt=True,tn=1024, out_dtypes=[F32], name="d_h2")
    dx1, dmix, dg_pre_ffn, dg_post_mix = _mid_bwd(dh2, x1, mix, g_pre_ffn, g_post_mix, dy)

    gw["w_o"] = _mm_tn(merged, dmix, tm=1024, tn=1024, name="gw_o").reshape(N_CHIPS, D_MODEL // N_CHIPS, D_MODEL)
    dmerged = _mm([(dmix, w_o_full)], nt=True,tn=1024, out_dtypes=[F32], name="d_merged")
    dy_a, dy_b, dy_c, dgpre, db_gate = _gate_bwd(dmerged, gates, y_a, y_b, y_c)
    br_cols = D_MODEL // N_CHIPS
    gw["w_br_sb"] = _mm_tn(o_a2, dy_a, tm=512, tn=br_cols, name="gw_br_sb", out_shards=True)
    gw["w_br_dil"] = _mm_tn(o_b, dy_b, tm=256, tn=br_cols, name="gw_br_dil", out_shards=True)
    gw["w_br_mem"] = _mm_tn(o_c2, dy_c, tm=512, tn=br_cols, name="gw_br_mem", out_shards=True)
    gw["w_gate"] = _mm_tn(h, dgpre, tm=1024, tn=w["w_gate"].shape[2], name="gw_gate", out_shards=True)
    do_a = _mm([(dy_a, w["w_br_sb"], s, s) for s in chips], nt=True,tn=SB_W, out_dtypes=[BF16], name="d_o_a")
    do_b = _mm([(dy_b, w["w_br_dil"], s, s) for s in chips], nt=True,tn=DIL_W, out_dtypes=[BF16], name="d_o_b")
    do_c = _mm([(dy_c, w["w_br_mem"], s, s) for s in chips], nt=True,tn=MEM_W, out_dtypes=[BF16], name="d_o_c")

    dq_c, dk_m, dv_m = _mem_bwd(proj3, kv3, do_c.reshape(bl, SEQ, MEM_W))
    dkv_m = jnp.concatenate([dk_m, dv_m], axis=-1).reshape(bl * MEM_LEN, 2 * MEM_W).astype(BF16)
    gw["w_mem_kv"] = _mm_tn(hm, dkv_m, tm=1024, tn=1024, name="gw_mem_kv").reshape(N_CHIPS, D_MODEL // N_CHIPS, 2 * MEM_W)
    dhm = _mm([(dkv_m, w_mem_kv_full)], nt=True,tn=1024, out_dtypes=[F32], name="d_hm")
    dg_mem = _mem_norm_bwd(dhm, mem2, g_mem)

    dcat = _band_delta(do_b, o_b, lse_b)
    d_dil = [_band_group_bwd(b["a"], dcat.reshape(bl, SEQ // d, d * grp_w), b["cos"], b["sin"], rows=b["rows"], cls=b["cls"],
                             steps=b["steps"], col0=b["col0"], name=f"band_bwd_{g}").reshape(bl, SEQ, grp_w)
             for g, (d, b) in enumerate(zip(DIL_D, band))]

    early, parts = [], []
    if comm:
        early = [name for name, _, _ in PACK if name != "w_in"]
        grads = [gw[name] for name in early]
        parts = [_pair_add(g, l, comm["c"], name="pair_add_" + name)
                 for name, g, l in zip(early, grads, _pair_exchange(grads, name="pair_exchange_early"))]
    dq_a, dk_a, dv_a, lands = _sb_bwd(proj3, o_a32, do_a.reshape(bl, SEQ, SB_W), sb_weights, parts)
    reduced = {name: (p, l) for name, p, l in zip(early, parts, lands)}

    dproj = jnp.concatenate([dq_a, dk_a, dv_a] + d_dil + [dq_c], axis=-1).reshape(t, D_IN)
    in_cols = D_IN // N_CHIPS
    dproj_s = jnp.stack([dproj[:, s * in_cols:(s + 1) * in_cols] for s in chips])
    gw["w_in"] = _mm_tn(h, dproj_s, tm=1024, tn=in_cols, name="gw_in")
    dh = _mm([(dproj_s, w["w_in"], s, s) for s in chips] + [(dgpre, w["w_gate"], s, s) for s in chips],
             nt=True,tn=1024, out_dtypes=[F32], name="d_h")
    grad_x, dg_pre_mix = _first_bwd(dh, x2, g_pre_mix, dx1)
    small = jnp.concatenate([dg_pre_mix, dg_post_mix, dg_pre_ffn, dg_post_ffn, dg_mem, db_gate.reshape(3, D_MODEL)], axis=0)
    return loss_row[0, 0], grad_x.reshape(bl, SEQ, D_MODEL), gw, small, reduced


def kernel(x, mem, g_pre_mix, g_post_mix, g_pre_ffn, g_post_ffn, g_mem, w_in, w_mem_kv, w_br_sb, w_br_dil, w_br_mem, w_gate, b_gate, w_o, w_ffn_in, w_ffn_out, loss_target, m_g_pre_mix, m_g_post_mix, m_g_pre_ffn, m_g_post_ffn, m_g_mem, m_w_in, m_w_mem_kv, m_w_br_sb, m_w_br_dil, m_w_br_mem, m_w_gate, m_b_gate, m_w_o, m_w_ffn_in, m_w_ffn_out, v_g_pre_mix, v_g_post_mix, v_g_pre_ffn, v_g_post_ffn, v_g_mem, v_w_in, v_w_mem_kv, v_w_br_sb, v_w_br_dil, v_w_br_mem, v_w_gate, v_b_gate, v_w_o, v_w_ffn_in, v_w_ffn_out):
    w_shards = dict(w_in=w_in[0], w_mem_kv=w_mem_kv[0], w_br_sb=w_br_sb[0], w_br_dil=w_br_dil[0], w_br_mem=w_br_mem[0],
                    w_gate=w_gate[0], w_o=w_o[0], w_ffn_in=w_ffn_in[0], w_ffn_out=w_ffn_out[0])
    m_shards = dict(w_in=m_w_in[0], w_mem_kv=m_w_mem_kv[0], w_br_sb=m_w_br_sb[0], w_br_dil=m_w_br_dil[0], w_br_mem=m_w_br_mem[0],
                    w_gate=m_w_gate[0], w_o=m_w_o[0], w_ffn_in=m_w_ffn_in[0], w_ffn_out=m_w_ffn_out[0])
    v_shards = dict(w_in=v_w_in[0], w_mem_kv=v_w_mem_kv[0], w_br_sb=v_w_br_sb[0], w_br_dil=v_w_br_dil[0], w_br_mem=v_w_br_mem[0],
                    w_gate=v_w_gate[0], w_o=v_w_o[0], w_ffn_in=v_w_ffn_in[0], w_ffn_out=v_w_ffn_out[0])

    names = [name for name, _, _ in PACK]
    c_arr = lax.axis_index("c").astype(jnp.int32).reshape(1)
    me_arr = (2 * lax.axis_index("x") + lax.axis_index("y")).astype(jnp.int32).reshape(1)
    mid_names = ["w_gate", "w_mem_kv"]
    late_names = [name for name in names if name not in ["w_in"] + mid_names]
    bf = {name: w_shards[name].astype(BF16) for name in names}
    w = {"w_in": lax.dynamic_update_slice(_gather_weights([bf["w_in"]])[0], bf["w_in"][None], (me_arr[0], 0, 0))}
    comm = dict(c=c_arr, me=me_arr, mid_names=mid_names, mid_shards=[bf[name] for name in mid_names],
                late_names=late_names, late_shards=[bf[name] for name in late_names])

    loss_local, grad_x, gw, small, reduced = _local_step(x, mem, loss_target, g_pre_mix, g_post_mix, g_pre_ffn, g_post_ffn,
                                                         g_mem, b_gate, w, comm)
    loss = lax.psum(loss_local, ("x", "y", "c"))

    part_in = _pair_add(gw["w_in"], _pair_exchange([gw["w_in"]], name="pair_exchange_w_in")[0], c_arr, name="pair_add_w_in")
    reduced["w_in"] = (part_in, _chip_exchange([part_in])[0])
    halves = [_chip_add(reduced[name][1], reduced[name][0], me_arr, name="chip_add_" + name) for name in names]
    theirs = _pair_share(halves)
    small = _all_sum_small(small)

    upd = {}
    for name, mine, other in zip(names, halves, theirs):
        upd[name] = _adamw_halves(w_shards[name], mine, other, m_shards[name], v_shards[name], c_arr, name="adamw_" + name)
    g_shards = {name: u[0] for name, u in upd.items()}

    def small8(gs, b):
        return jnp.concatenate(gs + [b.reshape(3, D_MODEL)], axis=0)

    sw = small8([g_pre_mix, g_post_mix, g_pre_ffn, g_post_ffn, g_mem], b_gate)
    sm = small8([m_g_pre_mix, m_g_post_mix, m_g_pre_ffn, m_g_post_ffn, m_g_mem], m_b_gate)
    sv = small8([v_g_pre_mix, v_g_post_mix, v_g_pre_ffn, v_g_post_ffn, v_g_mem], v_b_gate)
    s_upd = _adamw(sw, small, sm, sv, tm=8, name="adamw_small")

    def small_out(a):
        return [a[0:1], a[1:2], a[2:3], a[3:4], a[4:5]]

    order = ["w_in", "w_mem_kv", "w_br_sb", "w_br_dil", "w_br_mem", "w_gate", "b_gate", "w_o", "w_ffn_in", "w_ffn_out"]

    def leaves(small_arr, big):
        out = small_out(small_arr)
        for name in order:
            out.append(small_arr[5:8].reshape(1, 3 * D_MODEL) if name == "b_gate" else big[name][None])
        return out

    grads_out = leaves(small, g_shards)
    delta_out = leaves(s_upd[0], {n: u[1] for n, u in upd.items()})
    m_out = leaves(s_upd[1], {n: u[2] for n, u in upd.items()})
    v_out = leaves(s_upd[2], {n: u[3] for n, u in upd.items()})
    return (loss, grad_x, *grads_out, *delta_out, *m_out, *v_out)
```

```python
import jax
import jax.numpy as jnp
import numpy as np
from jax import lax
from jax.experimental import pallas as pl
from jax.experimental.pallas import tpu as pltpu

F32 = jnp.float32
BF16 = jnp.bfloat16
MESH = pl.DeviceIdType.MESH

D_MODEL = 1024
SEQ = 2048
HEAD_DIM = 64
SB_W = 512
DIL_W = 256
MEM_W = 512
MEM_LEN = 256
D_IN = 3 * SB_W + 9 * DIL_W + MEM_W
D_FF = 2816
DIL_D = (1, 4, 16)
ROPE_THETA = 10000.0
NORM_EPS = 1e-6
NEG_INF = -1e30
LANES = 128

ADAM_LR = 0.001
ADAM_B1 = 0.9
ADAM_B2 = 0.999
ADAM_EPS = 1e-08
ADAM_WD = 0.01
ADAM_STEP = 10

N_CHIPS = 4
PACK = (
    ("w_in", (1024, 1088), 1),
    ("w_mem_kv", (256, 1024), 0),
    ("w_br_sb", (512, 256), 1),
    ("w_br_dil", (256, 256), 1),
    ("w_br_mem", (512, 256), 1),
    ("w_gate", (1024, 768), 1),
    ("w_o", (256, 1024), 0),
    ("w_ffn_in", (1024, 1408), 1),
    ("w_ffn_out", (704, 1024), 0),
)
PACK_ROWS = sum(a * b for _, (a, b), _ in PACK) // D_MODEL
HALF_ROWS = PACK_ROWS // 2


def _dot(a, b):
    return lax.dot_general(a, b, (((1,), (0,)), ((), ())), preferred_element_type=F32)


def _dot_nt(a, b):
    return lax.dot_general(a, b, (((1,), (1,)), ((), ())), preferred_element_type=F32)


def _dot_tn(a, b):
    return lax.dot_general(a, b, (((0,), (0,)), ((), ())), preferred_element_type=F32)


def _split_dot(x, u):
    hi = x.astype(BF16)
    lo = (x - hi.astype(F32)).astype(BF16)
    return _dot(hi, u) + _dot(lo, u)


V7X_VMEM_BUDGET = 44 * 2 ** 20


def _rows_that_fit(m, row_bytes, fixed_bytes):
    for tm in (1024, 512, 256, 128):
        if m % tm == 0 and fixed_bytes + tm * row_bytes <= V7X_VMEM_BUDGET:
            return tm
    return min(m, 128)


def _mm(pairs, *, nt, tn, out_dtypes, name, bias=None, extras=(), epilogue=None, side=None):
    pairs = [p if len(p) == 4 else (p[0], p[1], None, None) for p in pairs]
    m = pairs[0][0].shape[-2]
    b0 = pairs[0][1]
    if nt:
        n = b0.shape[-2]
    else:
        n = b0.shape[-1] * (b0.shape[0] if b0.ndim == 3 else 1)
    n_pairs, n_extra, n_out = len(pairs), len(extras), len(out_dtypes)
    assert n % tn == 0
    one_col = n == tn
    ks = [(b.shape[-1] if nt else b.shape[-2]) for _, b, _, _ in pairs]
    fixed = sum(k * tn * 2 for k in ks) * (1 if one_col else 2)
    row_bytes = 2 * sum(k * 2 for k in ks) + 2 * tn * (sum(jnp.dtype(dt).itemsize for dt in out_dtypes) + 2 * n_extra) + 2 * tn * 4
    tm = _rows_that_fit(m, row_bytes, fixed)
    assert m % tm == 0
    b_mode = dict(pipeline_mode=pl.Buffered(1)) if one_col else {}
    has_bias = bias is not None
    side_arrays = side[1] if side else []
    n_side = len(side_arrays)
    n_main_in = 2 * n_pairs + has_bias + n_extra
    n_steps = (n // tn) * (m // tm)

    def body(*refs):
        if n_side:
            step = pl.program_id(0) * (m // tm) + pl.program_id(1)
            finish = _run_side(side, refs[n_main_in:n_main_in + n_side],
                               refs[n_main_in + n_side + n_out:n_main_in + 2 * n_side + n_out],
                               refs[n_main_in + 2 * n_side + n_out:], step, n_steps)
        acc = None
        for i in range(n_pairs):
            a, b = refs[2 * i][...], refs[2 * i + 1][...]
            p = _dot_nt(a, b) if nt else _dot(a, b)
            acc = p if acc is None else acc + p
        pos = 2 * n_pairs
        if has_bias:
            acc = acc + refs[pos][...]
            pos += 1
        ex = [r[...] for r in refs[pos:pos + n_extra]]
        outs = refs[n_main_in + n_side:n_main_in + n_side + n_out]
        vals = (acc,) if epilogue is None else epilogue(acc, *ex)
        for r, v, dt in zip(outs, vals, out_dtypes):
            r[...] = v.astype(dt)
        if n_side:
            finish()

    in_specs, args = [], []
    for a, b, a_col, b_sel in pairs:
        k = b.shape[-1] if nt else b.shape[-2]
        assert a_col is not None or a.shape[1] == k
        if a.ndim == 3:
            in_specs.append(pl.BlockSpec((None, tm, k), lambda j, i, c=a_col: (c, i, 0)))
        else:
            in_specs.append(pl.BlockSpec((tm, k), lambda j, i, c=a_col or 0: (i, c)))
        if b.ndim == 2:
            in_specs.append(pl.BlockSpec((tn, k), lambda j, i: (j, 0), **b_mode) if nt
                            else pl.BlockSpec((k, tn), lambda j, i: (0, j), **b_mode))
        elif nt:
            in_specs.append(pl.BlockSpec((None, tn, k), lambda j, i, s=b_sel: (s, j, 0), **b_mode))
        else:
            assert b_sel == "j" and b.shape[-1] == tn
            in_specs.append(pl.BlockSpec((None, k, tn), lambda j, i: (j, 0, 0), **b_mode))
        args += [a, b]
    if has_bias:
        in_specs.append(pl.BlockSpec((1, tn), lambda j, i: (0, j)))
        args.append(bias)
    for e in extras:
        in_specs.append(pl.BlockSpec((tm, tn), lambda j, i: (i, j)))
        args.append(e)
    out = pl.pallas_call(
        body,
        grid=(n // tn, m // tm),
        in_specs=in_specs + [ANY] * n_side,
        out_specs=[pl.BlockSpec((tm, tn), lambda j, i: (i, j)) for _ in range(n_out)] + [ANY] * n_side,
        out_shape=[jax.ShapeDtypeStruct((m, n), dt) for dt in out_dtypes] + (_side_out_shapes(side) if n_side else []),
        scratch_shapes=_side_sems(side) if n_side else [],
        compiler_params=pltpu.CompilerParams(dimension_semantics=("arbitrary", "arbitrary") if n_side else ("parallel", "parallel")),
        name=name,
    )(*args, *side_arrays)
    if n_side:
        return (out[0] if n_out == 1 else out[:n_out]), out[n_out:]
    return out[0] if n_out == 1 else out


def _mm_tn(a, b, *, tm, tn, name, out_shards=False):
    k, m = a.shape
    b_shards = b.ndim == 3
    out_shards = out_shards or b_shards
    n = b.shape[0] * b.shape[2] if b_shards else b.shape[1]
    tk = _rows_that_fit(k, 2 * 2 * (tm + tn), 3 * tm * tn * 4)
    assert m % tm == 0 and n % tn == 0 and k % tk == 0 and (not b_shards or b.shape[2] == tn)

    def body(a_ref, b_ref, o_ref):
        @pl.when(pl.program_id(2) == 0)
        def _():
            o_ref[...] = jnp.zeros_like(o_ref)

        o_ref[...] += _dot_tn(a_ref[...], b_ref[...])

    if b_shards:
        b_spec = pl.BlockSpec((None, tk, tn), lambda i, j, kk: (j, kk, 0))
    else:
        b_spec = pl.BlockSpec((tk, tn), lambda i, j, kk: (kk, j))
    if out_shards:
        out_spec = pl.BlockSpec((None, tm, tn), lambda i, j, kk: (j, i, 0))
        out_shape = jax.ShapeDtypeStruct((n // tn, m, tn), F32)
    else:
        out_spec = pl.BlockSpec((tm, tn), lambda i, j, kk: (i, j))
        out_shape = jax.ShapeDtypeStruct((m, n), F32)
    return pl.pallas_call(
        body,
        grid=(m // tm, n // tn, k // tk),
        in_specs=[pl.BlockSpec((tk, tm), lambda i, j, kk: (kk, i)), b_spec],
        out_specs=out_spec,
        out_shape=out_shape,
        compiler_params=pltpu.CompilerParams(dimension_semantics=("parallel", "parallel", "arbitrary")),
        name=name,
    )(a, b)


def _rowwise(fn, ins, outs, *, tm, name):
    rows = next(a.shape[0] for a, kind in ins if kind == "row")
    tm = min(tm, rows)
    assert rows % tm == 0
    n_in = len(ins)

    def body(*refs):
        vals = fn(*[r[...] for r in refs[:n_in]])
        for (_, dt, kind), r, v in zip(outs, refs[n_in:], vals):
            if kind == "row":
                r[...] = v.astype(dt)
            else:
                @pl.when(pl.program_id(0) == 0)
                def _(r=r):
                    r[...] = jnp.zeros_like(r)

                r[...] += v

    in_specs = [pl.BlockSpec((tm, a.shape[1]), lambda i: (i, 0)) if kind == "row" else pl.BlockSpec(a.shape, lambda i: (0, 0))
                for a, kind in ins]
    out_specs = [pl.BlockSpec((tm, c), lambda i: (i, 0)) if kind == "row" else pl.BlockSpec((1, c), lambda i: (0, 0))
                 for c, _, kind in outs]
    out_shape = [jax.ShapeDtypeStruct((rows if kind == "row" else 1, c), dt) for c, dt, kind in outs]
    has_acc = any(kind == "acc" for _, _, kind in outs)
    return pl.pallas_call(
        body,
        grid=(rows // tm,),
        in_specs=in_specs,
        out_specs=out_specs,
        out_shape=out_shape,
        compiler_params=pltpu.CompilerParams(dimension_semantics=("arbitrary" if has_acc else "parallel",)),
        name=name,
    )(*[a for a, _ in ins])


def _rstd(x):
    return lax.rsqrt(jnp.mean(x * x, axis=-1, keepdims=True) + NORM_EPS)


def _norm_bwd(dout, xin, g):
    r = _rstd(xin)
    n = xin * r
    dn = dout * g
    dg = jnp.sum(dout * n, axis=0, keepdims=True)
    dx = r * (dn - n * jnp.mean(dn * n, axis=-1, keepdims=True))
    return dx, dg


def _sigmoid(x):
    return 1.0 / (1.0 + jnp.exp(-x))


def _norm_fwd(x, g, *, name):
    def fn(x, g):
        return ((x * _rstd(x)) * g,)

    return _rowwise(fn, [(x, "row"), (g, "vec")], [(D_MODEL, BF16, "row")], tm=512, name=name)[0]


def _mid_fwd(mix, x, g_post_mix, g_pre_ffn):
    def fn(mix, x, g2, g3):
        x1 = x + (mix * _rstd(mix)) * g2
        return x1, (x1 * _rstd(x1)) * g3

    return _rowwise(fn, [(mix, "row"), (x, "row"), (g_post_mix, "vec"), (g_pre_ffn, "vec")],
                    [(D_MODEL, F32, "row"), (D_MODEL, BF16, "row")], tm=512, name="mid_fwd")


def _loss_bwd(f2, x1, g_post_ffn, tgt):
    def fn(f2, x1, g4, tgt):
        r = _rstd(f2)
        n = f2 * r
        err = x1 + n * g4 - tgt
        loss = 0.5 * jnp.sum(jnp.mean(err * err, axis=-1, keepdims=True), axis=0, keepdims=True)
        dy = err * (1.0 / D_MODEL)
        dn = dy * g4
        dg4 = jnp.sum(dy * n, axis=0, keepdims=True)
        df2 = r * (dn - n * jnp.mean(dn * n, axis=-1, keepdims=True))
        return dy, df2, dg4, jnp.broadcast_to(loss, (1, LANES))

    return _rowwise(fn, [(f2, "row"), (x1, "row"), (g_post_ffn, "vec"), (tgt, "row")],
                    [(D_MODEL, F32, "row"), (D_MODEL, BF16, "row"), (D_MODEL, F32, "acc"), (LANES, F32, "acc")],
                    tm=512, name="loss_bwd")


def _mid_bwd(dh2, x1, mix, g_pre_ffn, g_post_mix, dy):
    def fn(dh2, x1, mix, g3, g2, dy):
        d3, dg3 = _norm_bwd(dh2, x1, g3)
        dx1 = dy + d3
        dmix, dg2 = _norm_bwd(dx1, mix, g2)
        return dx1, dmix, dg3, dg2

    return _rowwise(fn, [(dh2, "row"), (x1, "row"), (mix, "row"), (g_pre_ffn, "vec"), (g_post_mix, "vec"), (dy, "row")],
                    [(D_MODEL, F32, "row"), (D_MODEL, BF16, "row"), (D_MODEL, F32, "acc"), (D_MODEL, F32, "acc")],
                    tm=256, name="mid_bwd")


def _first_bwd(dh, x, g_pre_mix, dx1):
    def fn(dh, x, g1, dx1):
        d1, dg1 = _norm_bwd(dh, x, g1)
        return dx1 + d1, dg1

    return _rowwise(fn, [(dh, "row"), (x, "row"), (g_pre_mix, "vec"), (dx1, "row")],
                    [(D_MODEL, F32, "row"), (D_MODEL, F32, "acc")], tm=512, name="first_bwd")


def _mem_norm_bwd(dhm, mem, g_mem):
    def fn(dhm, mem, g):
        return (jnp.sum(dhm * (mem * _rstd(mem)), axis=0, keepdims=True),)

    return _rowwise(fn, [(dhm, "row"), (mem, "row"), (g_mem, "vec")], [(D_MODEL, F32, "acc")], tm=512, name="mem_norm_bwd")[0]


def _gate_bwd(dmerged, gates, ya, yb, yc):
    def fn(dm, gt, ya, yb, yc):
        gt = gt.astype(F32)
        outs, dgp = [], []
        for i, y in enumerate((ya, yb, yc)):
            gi = gt[:, i * D_MODEL:(i + 1) * D_MODEL]
            outs.append(dm * gi)
            dgp.append(dm * y.astype(F32) * gi * (1.0 - gi))
        dgpre = jnp.concatenate(dgp, axis=1)
        return outs[0], outs[1], outs[2], dgpre, jnp.sum(dgpre, axis=0, keepdims=True)

    return _rowwise(fn, [(dmerged, "row"), (gates, "row"), (ya, "row"), (yb, "row"), (yc, "row")],
                    [(D_MODEL, BF16, "row")] * 3 + [(3 * D_MODEL, BF16, "row"), (3 * D_MODEL, F32, "acc")],
                    tm=256, name="gate_bwd")


def _adamw_math(w, g, m, v):
    m = ADAM_B1 * m + (1.0 - ADAM_B1) * g
    v = ADAM_B2 * v + (1.0 - ADAM_B2) * (g * g)
    m_hat = m / (1.0 - ADAM_B1 ** ADAM_STEP)
    v_hat = v / (1.0 - ADAM_B2 ** ADAM_STEP)
    delta = -ADAM_LR * (m_hat / (jnp.sqrt(v_hat) + ADAM_EPS) + ADAM_WD * w)
    return delta, m, v


def _adamw(w, g, m, v, *, tm, name):
    c = w.shape[1]
    return _rowwise(_adamw_math, [(w, "row"), (g, "row"), (m, "row"), (v, "row")], [(c, F32, "row")] * 3, tm=tm, name=name)


def _adamw_halves(w, g_mine, g_theirs, m, v, c_arr, *, name):
    a, b = w.shape
    hf = a // 2
    tr = hf // 4

    def body(c_ref, w_ref, gm_ref, gt_ref, m_ref, v_ref, g_out, d_out, m_out, v_out):
        g = jnp.where(pl.program_id(0) == c_ref[0], gm_ref[...], gt_ref[...])
        d, m_new, v_new = _adamw_math(w_ref[...], g, m_ref[...], v_ref[...])
        g_out[...] = g
        d_out[...] = d
        m_out[...] = m_new
        v_out[...] = v_new

    full = pl.BlockSpec((tr, b), lambda hh, i, c_ref: (hh * (hf // tr) + i, 0))
    half = pl.BlockSpec((tr, b), lambda hh, i, c_ref: (i, 0))
    return pl.pallas_call(
        body,
        grid_spec=pltpu.PrefetchScalarGridSpec(
            num_scalar_prefetch=1,
            grid=(2, hf // tr),
            in_specs=[full, half, half, full, full],
            out_specs=[full] * 4,
        ),
        out_shape=[jax.ShapeDtypeStruct((a, b), F32)] * 4,
        compiler_params=pltpu.CompilerParams(dimension_semantics=("parallel", "parallel")),
        name=name,
    )(c_arr, w, g_mine, g_theirs, m, v)


def _ffn_in_fwd(h2, w_ffn):
    m, tm, tn = h2.shape[0], 512, w_ffn.shape[2]
    assert 2 * tn == D_FF

    def body(h_ref, wg_ref, wu_ref, g_ref, u_ref, f_ref):
        h = h_ref[...]
        g = _dot(h, wg_ref[...])
        u = _dot(h, wu_ref[...])
        g_ref[...] = g.astype(BF16)
        u_ref[...] = u.astype(BF16)
        f_ref[...] = (g * _sigmoid(g) * u).astype(BF16)

    o_spec = pl.BlockSpec((tm, tn), lambda j, i: (i, j))
    return pl.pallas_call(
        body,
        grid=(D_FF // tn, m // tm),
        in_specs=[pl.BlockSpec((tm, D_MODEL), lambda j, i: (i, 0)),
                  pl.BlockSpec((None, D_MODEL, tn), lambda j, i: (j, 0, 0)),
                  pl.BlockSpec((None, D_MODEL, tn), lambda j, i: (j + 2, 0, 0))],
        out_specs=[o_spec, o_spec, o_spec],
        out_shape=[jax.ShapeDtypeStruct((m, D_FF), BF16)] * 3,
        compiler_params=pltpu.CompilerParams(dimension_semantics=("parallel", "parallel")),
        name="ffn_in_fwd",
    )(h2, w_ffn, w_ffn)


def _swiglu_bwd_epilogue(df, g, u):
    g = g.astype(F32)
    u = u.astype(F32)
    sg = _sigmoid(g)
    return df * u * (sg * (1.0 + g * (1.0 - sg))), df * (g * sg)


def _branch_merge_fwd(o_a, o_b, o_c, w_sb, w_dil, w_mem, gates):
    m, tm = o_a.shape[0], 256

    def body(oa_ref, ob_ref, oc_ref, wa_ref, wb_ref, wc_ref, gt_ref, ya_ref, yb_ref, yc_ref, mg_ref):
        def project(o_ref, w_ref):
            o = o_ref[...]
            return jnp.concatenate([_dot(o, w_ref[s]) for s in range(N_CHIPS)], axis=1)

        ya = project(oa_ref, wa_ref)
        yb = project(ob_ref, wb_ref)
        yc = project(oc_ref, wc_ref)
        gt = gt_ref[...].astype(F32)
        ya_ref[...] = ya.astype(BF16)
        yb_ref[...] = yb.astype(BF16)
        yc_ref[...] = yc.astype(BF16)
        mg_ref[...] = (gt[:, :D_MODEL] * ya + gt[:, D_MODEL:2 * D_MODEL] * yb + gt[:, 2 * D_MODEL:] * yc).astype(BF16)

    row = lambda c: pl.BlockSpec((tm, c), lambda i: (i, 0))
    full = lambda a: pl.BlockSpec(a.shape, lambda i: (0, 0, 0))
    return pl.pallas_call(
        body,
        grid=(m // tm,),
        in_specs=[row(SB_W), row(DIL_W), row(MEM_W), full(w_sb), full(w_dil), full(w_mem), row(3 * D_MODEL)],
        out_specs=[row(D_MODEL)] * 4,
        out_shape=[jax.ShapeDtypeStruct((m, D_MODEL), BF16)] * 4,
        compiler_params=pltpu.CompilerParams(dimension_semantics=("parallel",)),
        name="branch_merge_fwd",
    )(o_a, o_b, o_c, w_sb, w_dil, w_mem, gates)


SB_T = 256
SB_SCALE = HEAD_DIM ** -0.5


def _sb_masks():
    row = lax.broadcasted_iota(jnp.int32, (SB_T, SB_T), 0)
    col = lax.broadcasted_iota(jnp.int32, (SB_T, SB_T), 1)
    lane = lax.broadcasted_iota(jnp.int32, (1, LANES), 1)
    return row, col, lane


def _sb_logs(z):
    e = jnp.exp(-jnp.abs(z))
    lb = jnp.minimum(z, 0.0) - jnp.log(1.0 + e)
    return lb, lb - z, e


def _sb_specs(n_heads_pairs, col0):
    q = pl.BlockSpec((None, SB_T, LANES), lambda b, p, i: (b, i, col0 + p))
    k = pl.BlockSpec((None, SEQ, LANES), lambda b, p, i: (b, 0, col0 + n_heads_pairs + p))
    v = pl.BlockSpec((None, SEQ, LANES), lambda b, p, i: (b, 0, col0 + 2 * n_heads_pairs + p))
    return q, k, v


def _grid_step(n_pairs, nq):
    return (pl.program_id(0) * n_pairs + pl.program_id(1)) * nq + pl.program_id(2)


def _sb_fwd(proj3, late_shards):
    bl = proj3.shape[0]
    n_pairs = SB_W // LANES
    nq = SEQ // SB_T
    n_late = len(late_shards)
    n_steps = bl * n_pairs * nq

    def body(q_ref, k_ref, v_ref, *rest):
        late_in, (o_ref, o32_ref, w_ref), late_out = rest[:n_late], rest[n_late:n_late + 3], rest[n_late + 3:2 * n_late + 3]
        step = _grid_step(n_pairs, nq)
        if n_late:
            send, forward, finish = _gather_phases(late_in, late_out, *rest[2 * n_late + 3:])
            pl.when(step == 0)(send)
            pl.when(step == n_steps // 2)(forward)
        i = pl.program_id(2)
        row, col, lane = _sb_masks()
        causal = col < row
        u_excl = (row > col).astype(BF16)
        q = q_ref[...]
        heads = []
        for h in range(2):
            mh = (lane // HEAD_DIM) == h
            heads.append((mh, jnp.where(mh, q, jnp.zeros_like(q)) * SB_SCALE))

        def blocks(js, diags, carries, acc):
            ks = [k_ref[pl.ds(pl.multiple_of(j * SB_T, SB_T), SB_T), :] for j in js]
            vs = [v_ref[pl.ds(pl.multiple_of(j * SB_T, SB_T), SB_T), :] for j in js]
            chains = [(b, h) for b in range(len(js)) for h in range(2)]
            z = {c: _dot_nt(heads[c[1]][1], ks[c[0]]) for c in chains}
            lb, lk = {}, {}
            for c in chains:
                lb[c], lk[c], _ = _sb_logs(z[c])
                if diags[c[0]]:
                    lk[c] = jnp.where(causal, lk[c], 0.0)
            r = {c: _split_dot(lk[c], u_excl) for c in chains}
            carries = list(carries)
            w = {}
            for b, h in chains:
                w_c = jnp.exp(lb[b, h] + r[b, h] + carries[h])
                w[b, h] = (jnp.where(causal, w_c, 0.0) if diags[b] else w_c).astype(BF16)
                w_ref[h, js[b]] = w[b, h]
                carries[h] = carries[h] + (r[b, h][:, :1] + lk[b, h][:, :1])
            for b, h in chains:
                acc = acc + _dot(w[b, h], jnp.where(heads[h][0], vs[b], jnp.zeros_like(vs[b])))
            return tuple(carries), acc

        zero = jnp.zeros((SB_T, 1), F32)
        init = ((zero, zero), jnp.zeros((SB_T, LANES), F32))
        odd = i % 2
        carries, acc = lax.cond(odd == 1, lambda: blocks([i, i - 1], (True, False), *init), lambda: blocks([i], (True,), *init))
        rest = i - 1 - odd
        carries, acc = lax.fori_loop(
            0, i // 2, lambda jj, c: blocks([rest - 2 * jj, rest - 1 - 2 * jj], (False, False), c[0], c[1]), (carries, acc))
        o_ref[...] = acc.astype(BF16)
        o32_ref[...] = acc
        if n_late:
            pl.when(step == n_steps - 1)(finish)

    q_spec, k_spec, v_spec = _sb_specs(n_pairs, 0)
    blk = pl.BlockSpec((None, SB_T, LANES), lambda b, p, i: (b, i, p))
    out = pl.pallas_call(
        body,
        grid=(bl, n_pairs, nq),
        in_specs=[q_spec, k_spec, v_spec] + [ANY] * n_late,
        out_specs=[blk, blk, _sb_weight_spec(nq)] + [ANY] * n_late,
        out_shape=[jax.ShapeDtypeStruct((bl, SEQ, SB_W), BF16), jax.ShapeDtypeStruct((bl, SEQ, SB_W), F32),
                   jax.ShapeDtypeStruct((bl, n_pairs, nq, 2, nq, SB_T, SB_T), BF16)] + _gather_out_shapes(late_shards),
        scratch_shapes=_gather_sems(n_late) if n_late else [],
        compiler_params=pltpu.CompilerParams(dimension_semantics=("arbitrary", "arbitrary", "arbitrary")),
        name="sb_fwd",
    )(proj3, proj3, proj3, *late_shards)
    return out[0], out[1], out[2], out[3:]


def _sb_weight_spec(nq):
    return pl.BlockSpec((None, None, None, 2, nq, SB_T, SB_T), lambda b, p, i: (b, p, i, 0, 0, 0, 0))


def _sb_bwd(proj3, o_a, do_a, w_all, parts):
    bl = proj3.shape[0]
    n_pairs = SB_W // LANES
    nq = SEQ // SB_T
    n_parts = len(parts)
    n_steps = bl * n_pairs * nq

    def body(q_ref, k_ref, v_ref, o_ref, do_ref, w_ref, *rest):
        p_refs, (dq_ref, dk_ref, dv_ref), land_refs = rest[:n_parts], rest[n_parts:n_parts + 3], rest[n_parts + 3:2 * n_parts + 3]
        dk_acc, dv_acc = rest[2 * n_parts + 3:2 * n_parts + 5]
        step = _grid_step(n_pairs, nq)
        if n_parts:
            send, finish = _chip_exchange_phases(p_refs, land_refs, *rest[2 * n_parts + 5:])
            pl.when(step == 0)(send)
        i = pl.program_id(2)

        @pl.when(i == 0)
        def _():
            dk_acc[...] = jnp.zeros_like(dk_acc)
            dv_acc[...] = jnp.zeros_like(dv_acc)

        row, col, lane = _sb_masks()
        causal = col < row
        u_incl = (row >= col).astype(BF16)
        q = q_ref[...]
        do = do_ref[...]
        prod = do.astype(F32) * o_ref[...]
        heads = []
        for h in range(2):
            mh = (lane // HEAD_DIM) == h
            d_tot = jnp.sum(jnp.where(mh, prod, 0.0), axis=1, keepdims=True)
            heads.append((mh, jnp.where(mh, q, jnp.zeros_like(q)) * SB_SCALE, jnp.where(mh, do, jnp.zeros_like(do)), d_tot))

        def blocks(js, diags, c_das, dq):
            starts = [pl.multiple_of(j * SB_T, SB_T) for j in js]
            ks = [k_ref[pl.ds(s, SB_T), :] for s in starts]
            vs = [v_ref[pl.ds(s, SB_T), :] for s in starts]
            chains = [(b, h) for b in range(len(js)) for h in range(2)]
            z = {c: _dot_nt(heads[c[1]][1], ks[c[0]]) for c in chains}
            dw = {c: _dot_nt(heads[c[1]][2], vs[c[0]]) for c in chains}
            wb = {(b, h): w_ref[h, js[b]] for b, h in chains}
            da = {c: dw[c] * wb[c].astype(F32) for c in chains}
            sfx = {c: _split_dot(da[c], u_incl) for c in chains}
            c_das = list(c_das)
            dz = {}
            for b, h in chains:
                dlk = heads[h][3] - c_das[h] - sfx[b, h]
                if diags[b]:
                    dlk = jnp.where(causal, dlk, 0.0)
                c_das[h] = c_das[h] + sfx[b, h][:, :1]
                e = jnp.exp(-jnp.abs(z[b, h]))
                inv = 1.0 / (1.0 + e)
                pos = z[b, h] >= 0.0
                beta = jnp.where(pos, inv, e * inv)
                one_m_beta = jnp.where(pos, e * inv, inv)
                dz[b, h] = (da[b, h] * one_m_beta - dlk * beta).astype(BF16)
            for b, h in chains:
                dq = dq + _dot(dz[b, h], jnp.where(heads[h][0], ks[b], jnp.zeros_like(ks[b])))
            for b in range(len(js)):
                dk_acc[pl.ds(starts[b], SB_T), :] += _dot_tn(dz[b, 0], heads[0][1]) + _dot_tn(dz[b, 1], heads[1][1])
                dv_acc[pl.ds(starts[b], SB_T), :] += _dot_tn(wb[b, 0], heads[0][2]) + _dot_tn(wb[b, 1], heads[1][2])
            return tuple(c_das), dq

        zero = jnp.zeros((SB_T, 1), F32)
        init = ((zero, zero), jnp.zeros((SB_T, LANES), F32))
        odd = i % 2
        state = lax.cond(odd == 1, lambda: blocks([i, i - 1], (True, False), *init), lambda: blocks([i], (True,), *init))
        rest = i - 1 - odd
        state = lax.fori_loop(0, i // 2, lambda jj, c: blocks([rest - 2 * jj, rest - 1 - 2 * jj], (False, False), c[0], c[1]), state)
        dq_ref[...] = (state[1] * SB_SCALE).astype(BF16)

        @pl.when(i == nq - 1)
        def _():
            dk_ref[...] = dk_acc[...].astype(BF16)
            dv_ref[...] = dv_acc[...].astype(BF16)

        if n_parts:
            pl.when(step == n_steps - 1)(finish)

    q_spec, k_spec, v_spec = _sb_specs(n_pairs, 0)
    blk = pl.BlockSpec((None, SB_T, LANES), lambda b, p, i: (b, i, p))
    seq = pl.BlockSpec((None, SEQ, LANES), lambda b, p, i: (b, 0, p))
    shape = jax.ShapeDtypeStruct((bl, SEQ, SB_W), BF16)
    out = pl.pallas_call(
        body,
        grid=(bl, n_pairs, nq),
        in_specs=[q_spec, k_spec, v_spec, blk, blk, _sb_weight_spec(nq)] + [ANY] * n_parts,
        out_specs=[blk, seq, seq] + [ANY] * n_parts,
        out_shape=[shape, shape, shape] + [jax.ShapeDtypeStruct(p.shape, p.dtype) for p in parts],
        scratch_shapes=[pltpu.VMEM((SEQ, LANES), F32), pltpu.VMEM((SEQ, LANES), F32)]
        + (_chip_exchange_sems(n_parts) if n_parts else []),
        compiler_params=pltpu.CompilerParams(dimension_semantics=("arbitrary", "arbitrary", "arbitrary")),
        name="sb_bwd",
    )(proj3, proj3, proj3, o_a, do_a, w_all, *parts)
    return out[0], out[1], out[2], out[3:]


BAND = 128


BAND_CH = 4
BAND_HEADS = DIL_W // HEAD_DIM


def _swap_half(x):
    n = x.shape[-1]
    lane = lax.broadcasted_iota(jnp.int32, (1, n), 1)
    return jnp.where((lane % HEAD_DIM) < HEAD_DIM // 2, pltpu.roll(x, n - HEAD_DIM // 2, 1), pltpu.roll(x, HEAD_DIM // 2, 1))


def _rope(x, cos, sin_signed):
    x = x.astype(F32)
    return x * cos + _swap_half(x) * sin_signed


def _band_valid(g, blk):
    nb = jnp.where(g == 0, 16, jnp.where(g == 1, 4, 1))
    first_key = jnp.where(lax.rem(blk, nb) != 0, 0, BAND)
    qi = lax.broadcasted_iota(jnp.int32, (BAND, 2 * BAND), 0) + BAND
    kj = lax.broadcasted_iota(jnp.int32, (BAND, 2 * BAND), 1)
    dist = qi - kj
    return (dist >= 0) & (dist <= BAND) & (kj >= first_key)


def _band_specs():
    last_before = lambda i: jnp.maximum(i * BAND_CH - 1, 0)
    cur = lambda col: pl.BlockSpec((None, BAND_CH, BAND, DIL_W), lambda g, i: (g, i, 0, col))
    prev = lambda col: pl.BlockSpec((None, None, BAND, DIL_W), lambda g, i: (g, last_before(i), 0, col))
    tab = pl.BlockSpec((None, BAND_CH, BAND, DIL_W), lambda g, i: (g, lax.rem(i, 16 // BAND_CH), 0, 0))
    tab_prev = pl.BlockSpec((None, None, BAND, DIL_W), lambda g, i: (g, lax.rem(last_before(i), 16), 0, 0))
    return cur, prev, tab, tab_prev


def _band_load(q_ref, k_ref, kp_ref, v_ref, vp_ref, c_ref, s_ref, cp_ref, sp_ref):
    qs = [(_rope(q_ref[b], c_ref[b], s_ref[b]) * SB_SCALE).astype(BF16) for b in range(BAND_CH)]
    ks = [_rope(kp_ref[...], cp_ref[...], sp_ref[...]).astype(BF16)]
    ks += [_rope(k_ref[b], c_ref[b], s_ref[b]).astype(BF16) for b in range(BAND_CH)]
    vs = [vp_ref[...]] + [v_ref[b] for b in range(BAND_CH)]
    k2 = [jnp.concatenate([ks[b], ks[b + 1]], axis=0) for b in range(BAND_CH)]
    v2 = [jnp.concatenate([vs[b], vs[b + 1]], axis=0) for b in range(BAND_CH)]
    return qs, k2, v2


def _band_fwd(qkv_s, cos_t, sin_t):
    def body(q_ref, k_ref, kp_ref, v_ref, vp_ref, c_ref, s_ref, cp_ref, sp_ref, ol_ref):
        g, i = pl.program_id(0), pl.program_id(1)
        qs, k2, v2 = _band_load(q_ref, k_ref, kp_ref, v_ref, vp_ref, c_ref, s_ref, cp_ref, sp_ref)
        lane = lax.broadcasted_iota(jnp.int32, (1, DIL_W), 1)
        for b in range(BAND_CH):
            valid = _band_valid(g, i * BAND_CH + b)
            hs = range(BAND_HEADS)
            mh = [(lane // HEAD_DIM) == h for h in hs]
            s = [jnp.where(valid, _dot_nt(jnp.where(mh[h], qs[b], jnp.zeros_like(qs[b])), k2[b]), NEG_INF) for h in hs]
            m = [jnp.max(s[h], axis=1, keepdims=True) for h in hs]
            p = [jnp.exp(s[h] - m[h]) for h in hs]
            den = [jnp.sum(p[h], axis=1, keepdims=True) for h in hs]
            pv = [_dot(p[h].astype(BF16), jnp.where(mh[h], v2[b], jnp.zeros_like(v2[b]))) for h in hs]
            o = jnp.zeros((BAND, DIL_W), F32)
            lse = jnp.zeros((BAND, DIL_W), F32)
            for h in hs:
                o = o + pv[h] * (1.0 / den[h])
                lse = jnp.where(mh[h], m[h] + jnp.log(den[h]), lse)
            ol_ref[b, :, :DIL_W] = o
            ol_ref[b, :, DIL_W:] = lse

    cur, prev, tab, tab_prev = _band_specs()
    n_blk = qkv_s.shape[1]
    return pl.pallas_call(
        body,
        grid=(3, n_blk // BAND_CH),
        in_specs=[cur(0), cur(1), prev(1), cur(2), prev(2), tab, tab, tab_prev, tab_prev],
        out_specs=pl.BlockSpec((None, BAND_CH, BAND, 2 * DIL_W), lambda g, i: (g, i, 0, 0)),
        out_shape=jax.ShapeDtypeStruct((3, n_blk, BAND, 2 * DIL_W), F32),
        compiler_params=pltpu.CompilerParams(dimension_semantics=("parallel", "parallel")),
        name="band_fwd",
    )(qkv_s, qkv_s, qkv_s, qkv_s, qkv_s, cos_t, sin_t, cos_t, sin_t)


def _band_bwd(qkv_s, cos_t, sin_t, dcat_s):
    def body(q_ref, k_ref, kp_ref, v_ref, vp_ref, c_ref, s_ref, cp_ref, sp_ref, do_ref, lse_ref, dl_ref,
             dq_ref, dk_ref, dv_ref, dkf_ref, dvf_ref):
        g, i = pl.program_id(0), pl.program_id(1)
        qs, k2, v2 = _band_load(q_ref, k_ref, kp_ref, v_ref, vp_ref, c_ref, s_ref, cp_ref, sp_ref)
        lane = lax.broadcasted_iota(jnp.int32, (1, DIL_W), 1)
        dks, dvs = [], []
        for b in range(BAND_CH):
            valid = _band_valid(g, i * BAND_CH + b)
            do, lse, dl = do_ref[b].astype(BF16), lse_ref[b], dl_ref[b]
            hs = range(BAND_HEADS)
            mh = [(lane // HEAD_DIM) == h for h in hs]
            qh = [jnp.where(mh[h], qs[b], jnp.zeros_like(qs[b])) for h in hs]
            doh = [jnp.where(mh[h], do, jnp.zeros_like(do)) for h in hs]
            s = [_dot_nt(qh[h], k2[b]) for h in hs]
            dp = [_dot_nt(doh[h], v2[b]) for h in hs]
            p = [jnp.where(valid, jnp.exp(s[h] - lse[:, h * HEAD_DIM:h * HEAD_DIM + 1]), 0.0) for h in hs]
            ds = [(p[h] * (dp[h] - dl[:, h * HEAD_DIM:h * HEAD_DIM + 1])).astype(BF16) for h in hs]
            pb = [p[h].astype(BF16) for h in hs]
            dq = sum(_dot(ds[h], jnp.where(mh[h], k2[b], jnp.zeros_like(k2[b]))) for h in hs)
            dk2 = sum(_dot_tn(ds[h], qh[h]) for h in hs)
            dv2 = sum(_dot_tn(pb[h], doh[h]) for h in hs)
            dq_ref[b] = dq * SB_SCALE
            dks.append(dk2)
            dvs.append(dv2)
        dkf_ref[...] = dks[0][:BAND]
        dvf_ref[...] = dvs[0][:BAND]
        for b in range(BAND_CH):
            last = b == BAND_CH - 1
            dk_ref[b] = dks[b][BAND:] if last else dks[b][BAND:] + dks[b + 1][:BAND]
            dv_ref[b] = dvs[b][BAND:] if last else dvs[b][BAND:] + dvs[b + 1][:BAND]

    cur, prev, tab, tab_prev = _band_specs()
    first = pl.BlockSpec((None, None, BAND, DIL_W), lambda g, i: (g, i, 0, 0))
    n_blk = qkv_s.shape[1]
    n_chunks = n_blk // BAND_CH
    shape = jax.ShapeDtypeStruct((3, n_blk, BAND, DIL_W), F32)
    shape_first = jax.ShapeDtypeStruct((3, n_chunks, BAND, DIL_W), F32)
    return pl.pallas_call(
        body,
        grid=(3, n_chunks),
        in_specs=[cur(0), cur(1), prev(1), cur(2), prev(2), tab, tab, tab_prev, tab_prev, cur(0), cur(1), cur(2)],
        out_specs=[cur(0), cur(0), cur(0), first, first],
        out_shape=[shape, shape, shape, shape_first, shape_first],
        compiler_params=pltpu.CompilerParams(dimension_semantics=("parallel", "parallel")),
        name="band_bwd",
    )(qkv_s, qkv_s, qkv_s, qkv_s, qkv_s, cos_t, sin_t, cos_t, sin_t, dcat_s, dcat_s, dcat_s)


def _band_combine(dq, dk, dv, dk_first, dv_first, cos_t, sin_t):
    n_chunks = dk_first.shape[1]

    def body(dq_ref, dk_ref, dkn_ref, dv_ref, dvn_ref, c_ref, s_ref, out_ref):
        nxt = (pl.program_id(1) < n_chunks - 1).astype(F32)
        for b in range(BAND_CH):
            cos, sin = c_ref[b], s_ref[b]
            dq_b, dk_b, dv_b = dq_ref[b], dk_ref[b], dv_ref[b]
            if b == BAND_CH - 1:
                dk_b = dk_b + nxt * dkn_ref[...]
                dv_b = dv_b + nxt * dvn_ref[...]
            out_ref[b, :, :DIL_W] = (dq_b * cos - _swap_half(dq_b) * sin).astype(BF16)
            out_ref[b, :, DIL_W:2 * DIL_W] = (dk_b * cos - _swap_half(dk_b) * sin).astype(BF16)
            out_ref[b, :, 2 * DIL_W:] = dv_b.astype(BF16)

    cur, _, tab, _ = _band_specs()
    nxt = pl.BlockSpec((None, None, BAND, DIL_W), lambda g, i: (g, jnp.minimum(i + 1, n_chunks - 1), 0, 0))
    return pl.pallas_call(
        body,
        grid=(3, n_chunks),
        in_specs=[cur(0), cur(0), nxt, cur(0), nxt, tab, tab],
        out_specs=pl.BlockSpec((None, BAND_CH, BAND, 3 * DIL_W), lambda g, i: (g, i, 0, 0)),
        out_shape=jax.ShapeDtypeStruct(dq.shape[:3] + (3 * DIL_W,), BF16),
        compiler_params=pltpu.CompilerParams(dimension_semantics=("parallel", "parallel")),
        name="band_combine",
    )(dq, dk, dk_first, dv, dv_first, cos_t, sin_t)


def _band_merge(ol):
    t, tm = ol.shape[1], 512

    def body(o_ref, l_ref, ob_ref, lse_ref):
        l0, l1, l2 = l_ref[0], l_ref[1], l_ref[2]
        m = jnp.maximum(jnp.maximum(l0, l1), l2)
        lse = m + jnp.log(jnp.exp(l0 - m) + jnp.exp(l1 - m) + jnp.exp(l2 - m))
        ob_ref[...] = (jnp.exp(l0 - lse) * o_ref[0] + jnp.exp(l1 - lse) * o_ref[1] + jnp.exp(l2 - lse) * o_ref[2]).astype(BF16)
        lse_ref[...] = lse

    spec = pl.BlockSpec((tm, DIL_W), lambda i: (i, 0))
    return pl.pallas_call(
        body,
        grid=(t // tm,),
        in_specs=[pl.BlockSpec((3, tm, DIL_W), lambda i: (0, i, 0)), pl.BlockSpec((3, tm, DIL_W), lambda i: (0, i, 1))],
        out_specs=[spec, spec],
        out_shape=[jax.ShapeDtypeStruct((t, DIL_W), BF16), jax.ShapeDtypeStruct((t, DIL_W), F32)],
        compiler_params=pltpu.CompilerParams(dimension_semantics=("parallel",)),
        name="band_merge",
    )(ol, ol)


def _band_delta(do_b, o_b, lse_b):
    def fn(do, o, lse):
        r = lax.broadcasted_iota(jnp.int32, (DIL_W, DIL_W), 0) // HEAD_DIM
        c = lax.broadcasted_iota(jnp.int32, (DIL_W, DIL_W), 1) // HEAD_DIM
        do = do.astype(F32)
        delta = _split_dot(do * o.astype(F32), (r == c).astype(BF16))
        return (jnp.concatenate([do, lse, delta], axis=1),)

    return _rowwise(fn, [(do_b, "row"), (o_b, "row"), (lse_b, "row")], [(3 * DIL_W, F32, "row")], tm=512, name="band_delta")[0]


def _band_masks():
    qi = lax.broadcasted_iota(jnp.int32, (BAND, 2 * BAND), 0) + BAND
    kj = lax.broadcasted_iota(jnp.int32, (BAND, 2 * BAND), 1)
    dist = qi - kj
    row = lax.broadcasted_iota(jnp.int32, (BAND, BAND), 0)
    col = lax.broadcasted_iota(jnp.int32, (BAND, BAND), 1)
    return col <= row, (dist >= 0) & (dist <= BAND)


def _band_attend(q, k, v, valid):
    lane = lax.broadcasted_iota(jnp.int32, (1, DIL_W), 1)
    hs = range(BAND_HEADS)
    mh = [(lane // HEAD_DIM) == h for h in hs]
    s = [jnp.where(valid, _dot_nt(jnp.where(mh[h], q, jnp.zeros_like(q)), k), NEG_INF) for h in hs]
    m = [jnp.max(s[h], axis=1, keepdims=True) for h in hs]
    p = [jnp.exp(s[h] - m[h]) for h in hs]
    den = [jnp.sum(p[h], axis=1, keepdims=True) for h in hs]
    pv = [_dot(p[h].astype(BF16), jnp.where(mh[h], v, jnp.zeros_like(v))) for h in hs]
    o = jnp.zeros((BAND, DIL_W), F32)
    lse = jnp.zeros((BAND, DIL_W), F32)
    for h in hs:
        o = o + pv[h] * (1.0 / den[h])
        lse = jnp.where(mh[h], m[h] + jnp.log(den[h]), lse)
    return o, lse


def _band_attend_bwd(q, k, v, valid, do, lse, dl):
    lane = lax.broadcasted_iota(jnp.int32, (1, DIL_W), 1)
    hs = range(BAND_HEADS)
    mh = [(lane // HEAD_DIM) == h for h in hs]
    qh = [jnp.where(mh[h], q, jnp.zeros_like(q)) for h in hs]
    doh = [jnp.where(mh[h], do, jnp.zeros_like(do)) for h in hs]
    s = [_dot_nt(qh[h], k) for h in hs]
    dp = [_dot_nt(doh[h], v) for h in hs]
    p = [jnp.where(valid, jnp.exp(s[h] - lse[:, h * HEAD_DIM:h * HEAD_DIM + 1]), 0.0) for h in hs]
    ds = [(p[h] * (dp[h] - dl[:, h * HEAD_DIM:h * HEAD_DIM + 1])).astype(BF16) for h in hs]
    pb = [p[h].astype(BF16) for h in hs]
    dq = sum(_dot(ds[h], jnp.where(mh[h], k, jnp.zeros_like(k))) for h in hs)
    dk = sum(_dot_tn(ds[h], qh[h]) for h in hs)
    dv = sum(_dot_tn(pb[h], doh[h]) for h in hs)
    return dq, dk, dv


def _band_group_specs(lead, rows, cls, col0):
    def spec(width):
        if lead == "rows":
            return pl.BlockSpec((None, rows, width), lambda b, i: (b, 0, col0))
        return pl.BlockSpec((None, rows, cls * width), lambda b, i: (b, 0, i))
    return spec


def _band_group_fwd(a, cos_g, sin_g, *, rows, cls, steps, col0, name):
    bl = a.shape[0]
    nb = rows // BAND
    grp_w = 3 * DIL_W

    def body(a_ref, c_ref, s_ref, ol_ref, qr, kr):
        first_valid, later_valid = _band_masks()
        for j in range(cls):
            a0, t0, o0 = j * grp_w, j * DIL_W, j * 2 * DIL_W
            cos, sin = c_ref[:, t0:t0 + DIL_W], s_ref[:, t0:t0 + DIL_W]
            qr[...] = (_rope(a_ref[:, a0:a0 + DIL_W], cos, sin) * SB_SCALE).astype(BF16)
            kr[...] = _rope(a_ref[:, a0 + DIL_W:a0 + 2 * DIL_W], cos, sin).astype(BF16)

            def block(q0, k0, keys, valid, a0=a0, o0=o0):
                o, lse = _band_attend(qr[pl.ds(q0, BAND), :], kr[pl.ds(k0, keys), :],
                                      a_ref[pl.ds(k0, keys), a0 + 2 * DIL_W:a0 + grp_w], valid)
                ol_ref[pl.ds(q0, BAND), o0:o0 + DIL_W] = o
                ol_ref[pl.ds(q0, BAND), o0 + DIL_W:o0 + 2 * DIL_W] = lse

            block(0, 0, BAND, first_valid)
            if nb > 1:
                def later(b, carry, block=block):
                    block(pl.multiple_of(b * BAND, BAND), pl.multiple_of((b - 1) * BAND, BAND), 2 * BAND, later_valid)
                    return carry

                lax.fori_loop(1, nb, later, 0)

    lead = "rows" if col0 is not None else "cols"
    spec = _band_group_specs(lead, rows, cls, col0)
    tab = pl.BlockSpec((rows, cls * DIL_W), lambda b, i: (0, i))
    n_cls = cos_g.shape[1] // DIL_W
    return pl.pallas_call(
        body,
        grid=(bl, steps),
        in_specs=[spec(grp_w), tab, tab],
        out_specs=pl.BlockSpec((None, rows, cls * 2 * DIL_W), lambda b, i: (b, 0, i)),
        out_shape=jax.ShapeDtypeStruct((bl, rows, n_cls * 2 * DIL_W), F32),
        scratch_shapes=[pltpu.VMEM((rows, DIL_W), BF16), pltpu.VMEM((rows, DIL_W), BF16)],
        compiler_params=pltpu.CompilerParams(dimension_semantics=("parallel", "parallel")),
        name=name,
    )(a, cos_g, sin_g)


def _band_group_bwd(a, d, cos_g, sin_g, *, rows, cls, steps, col0, name, side=None):
    bl = a.shape[0]
    nb = rows // BAND
    grp_w = 3 * DIL_W
    side_arrays = side[1] if side else []
    n_side = len(side_arrays)

    def body(a_ref, d_ref, c_ref, s_ref, *rest):
        out_ref = rest[n_side]
        qr, kr, dk_acc, dv_acc = rest[2 * n_side + 1:2 * n_side + 5]
        if n_side:
            step = pl.program_id(0) * steps + pl.program_id(1)
            finish = _run_side(side, rest[:n_side], rest[n_side + 1:2 * n_side + 1], rest[2 * n_side + 5:], step, bl * steps)
        first_valid, later_valid = _band_masks()
        for j in range(cls):
            a0, t0 = j * grp_w, j * DIL_W
            cos, sin = c_ref[:, t0:t0 + DIL_W], s_ref[:, t0:t0 + DIL_W]
            qr[...] = (_rope(a_ref[:, a0:a0 + DIL_W], cos, sin) * SB_SCALE).astype(BF16)
            kr[...] = _rope(a_ref[:, a0 + DIL_W:a0 + 2 * DIL_W], cos, sin).astype(BF16)
            dk_acc[...] = jnp.zeros_like(dk_acc)
            dv_acc[...] = jnp.zeros_like(dv_acc)

            def block(q0, k0, keys, valid, a0=a0, t0=t0):
                qrows, krows = pl.ds(q0, BAND), pl.ds(k0, keys)
                dq, dk, dv = _band_attend_bwd(
                    qr[qrows, :], kr[krows, :], a_ref[krows, a0 + 2 * DIL_W:a0 + grp_w], valid,
                    d_ref[qrows, a0:a0 + DIL_W].astype(BF16), d_ref[qrows, a0 + DIL_W:a0 + 2 * DIL_W],
                    d_ref[qrows, a0 + 2 * DIL_W:a0 + grp_w])
                dq = dq * SB_SCALE
                out_ref[qrows, a0:a0 + DIL_W] = (dq * c_ref[qrows, t0:t0 + DIL_W]
                                                 - _swap_half(dq) * s_ref[qrows, t0:t0 + DIL_W]).astype(BF16)
                dk_acc[krows, :] += dk
                dv_acc[krows, :] += dv

            block(0, 0, BAND, first_valid)
            if nb > 1:
                def later(b, carry, block=block):
                    block(pl.multiple_of(b * BAND, BAND), pl.multiple_of((b - 1) * BAND, BAND), 2 * BAND, later_valid)
                    return carry

                lax.fori_loop(1, nb, later, 0)
            dk = dk_acc[...]
            out_ref[:, a0 + DIL_W:a0 + 2 * DIL_W] = (dk * cos - _swap_half(dk) * sin).astype(BF16)
            out_ref[:, a0 + 2 * DIL_W:a0 + grp_w] = dv_acc[...].astype(BF16)
        if n_side:
            finish()

    lead = "rows" if col0 is not None else "cols"
    spec = _band_group_specs(lead, rows, cls, col0)
    dspec = _band_group_specs(lead, rows, cls, 0 if col0 is not None else None)
    tab = pl.BlockSpec((rows, cls * DIL_W), lambda b, i: (0, i))
    n_cls = cos_g.shape[1] // DIL_W
    out = pl.pallas_call(
        body,
        grid=(bl, steps),
        in_specs=[spec(grp_w), dspec(grp_w), tab, tab] + [ANY] * n_side,
        out_specs=[pl.BlockSpec((None, rows, cls * grp_w), lambda b, i: (b, 0, i))] + [ANY] * n_side,
        out_shape=[jax.ShapeDtypeStruct((bl, rows, n_cls * grp_w), BF16)] + (_side_out_shapes(side) if n_side else []),
        scratch_shapes=[pltpu.VMEM((rows, DIL_W), BF16), pltpu.VMEM((rows, DIL_W), BF16),
                        pltpu.VMEM((rows, DIL_W), F32), pltpu.VMEM((rows, DIL_W), F32)] + (_side_sems(side) if n_side else []),
        compiler_params=pltpu.CompilerParams(dimension_semantics=("arbitrary", "arbitrary") if n_side else ("parallel", "parallel")),
        name=name,
    )(a, d, cos_g, sin_g, *side_arrays)
    return (out[0], out[1:]) if n_side else out[0]


def _band_merge3(ols):
    t, tm = ols[0].shape[0], 512

    def body(o0, l0, o1, l1, o2, l2, ob_ref, lse_ref):
        a, b, c = l0[...], l1[...], l2[...]
        m = jnp.maximum(jnp.maximum(a, b), c)
        lse = m + jnp.log(jnp.exp(a - m) + jnp.exp(b - m) + jnp.exp(c - m))
        ob_ref[...] = (jnp.exp(a - lse) * o0[...] + jnp.exp(b - lse) * o1[...] + jnp.exp(c - lse) * o2[...]).astype(BF16)
        lse_ref[...] = lse

    spec = pl.BlockSpec((tm, DIL_W), lambda i: (i, 0))
    spec_l = pl.BlockSpec((tm, DIL_W), lambda i: (i, 1))
    return pl.pallas_call(
        body,
        grid=(t // tm,),
        in_specs=[spec, spec_l] * 3,
        out_specs=[spec, spec],
        out_shape=[jax.ShapeDtypeStruct((t, DIL_W), BF16), jax.ShapeDtypeStruct((t, DIL_W), F32)],
        compiler_params=pltpu.CompilerParams(dimension_semantics=("parallel",)),
        name="band_merge",
    )(ols[0], ols[0], ols[1], ols[1], ols[2], ols[2])


MEM_T = 512
MEM_SCALE = 128 ** -0.5
MEM_Q_COL = (D_IN - MEM_W) // LANES


def _mem_specs():
    q = pl.BlockSpec((None, MEM_T, LANES), lambda b, h, i: (b, i, MEM_Q_COL + h))
    k = pl.BlockSpec((None, MEM_LEN, LANES), lambda b, h, i: (b, 0, h))
    v = pl.BlockSpec((None, MEM_LEN, LANES), lambda b, h, i: (b, 0, MEM_W // LANES + h))
    blk = pl.BlockSpec((None, MEM_T, LANES), lambda b, h, i: (b, i, h))
    return q, k, v, blk


def _mem_probs(q, k):
    s = _dot_nt(q, k) * MEM_SCALE
    p = jnp.exp(s - jnp.max(s, axis=1, keepdims=True))
    return p * (1.0 / jnp.sum(p, axis=1, keepdims=True))


def _mem_fwd(proj3, kv3):
    bl = proj3.shape[0]

    def body(q_ref, k_ref, v_ref, o_ref):
        p = _mem_probs(q_ref[...], k_ref[...])
        o_ref[...] = _dot(p.astype(BF16), v_ref[...]).astype(BF16)

    q, k, v, blk = _mem_specs()
    return pl.pallas_call(
        body,
        grid=(bl, MEM_W // LANES, SEQ // MEM_T),
        in_specs=[q, k, v],
        out_specs=blk,
        out_shape=jax.ShapeDtypeStruct((bl, SEQ, MEM_W), BF16),
        compiler_params=pltpu.CompilerParams(dimension_semantics=("parallel", "parallel", "parallel")),
        name="mem_fwd",
    )(proj3, kv3, kv3)


def _mem_bwd(proj3, kv3, do_c):
    bl = proj3.shape[0]

    def body(q_ref, k_ref, v_ref, do_ref, dq_ref, dk_ref, dv_ref):
        @pl.when(pl.program_id(2) == 0)
        def _():
            dk_ref[...] = jnp.zeros_like(dk_ref)
            dv_ref[...] = jnp.zeros_like(dv_ref)

        q, k, do = q_ref[...], k_ref[...], do_ref[...]
        p = _mem_probs(q, k)
        dp = _dot_nt(do, v_ref[...])
        ds = (p * (dp - jnp.sum(p * dp, axis=1, keepdims=True)) * MEM_SCALE).astype(BF16)
        dq_ref[...] = _dot(ds, k).astype(BF16)
        dk_ref[...] += _dot_tn(ds, q)
        dv_ref[...] += _dot_tn(p.astype(BF16), do)

    q, k, v, blk = _mem_specs()
    kv_out = pl.BlockSpec((None, MEM_LEN, LANES), lambda b, h, i: (b, 0, h))
    return pl.pallas_call(
        body,
        grid=(bl, MEM_W // LANES, SEQ // MEM_T),
        in_specs=[q, k, v, blk],
        out_specs=[blk, kv_out, kv_out],
        out_shape=[jax.ShapeDtypeStruct((bl, SEQ, MEM_W), BF16), jax.ShapeDtypeStruct((bl, MEM_LEN, MEM_W), F32),
                   jax.ShapeDtypeStruct((bl, MEM_LEN, MEM_W), F32)],
        compiler_params=pltpu.CompilerParams(dimension_semantics=("parallel", "parallel", "arbitrary")),
        name="mem_bwd",
    )(proj3, kv3, kv3, do_c)


def _place():
    x, y, c = lax.axis_index("x"), lax.axis_index("y"), lax.axis_index("c")
    return x, y, c


def _other_chips(x, y):
    return [(1 - x, y), (x, 1 - y), (1 - x, 1 - y)]


def _remote(src, dst, send_sem, recv_sem, to):
    return pltpu.make_async_remote_copy(src_ref=src, dst_ref=dst, send_sem=send_sem, recv_sem=recv_sem,
                                        device_id=to, device_id_type=MESH)


ANY = pl.BlockSpec(memory_space=pl.ANY)


def _gather_weights(shards):
    n = len(shards)

    def body(*refs):
        send, forward, finish = _gather_phases(refs[:n], refs[n:2 * n], *refs[2 * n:])
        send()
        forward()
        finish()

    return pl.pallas_call(
        body,
        in_specs=[ANY] * n,
        out_specs=[ANY] * n,
        out_shape=_gather_out_shapes(shards),
        scratch_shapes=_gather_sems(n),
        name="gather_weights",
    )(*shards)


def _gather_out_shapes(shards):
    return [jax.ShapeDtypeStruct((N_CHIPS,) + s.shape, s.dtype) for s in shards]


def _gather_sems(n):
    return [pltpu.SemaphoreType.DMA((6 * n,)), pltpu.SemaphoreType.DMA((6 * n,))]


def _gather_phases(in_refs, out_refs, send_sems, recv_sems):
    x, y, c = _place()
    sibling = (x, y, 1 - c)
    chips = _other_chips(x, y)
    first, passed = [], []
    for k in range(len(in_refs)):
        hf = in_refs[k].shape[0] // 2

        def half(px, py, pc, k=k, hf=hf):
            return out_refs[k].at[2 * px + py, pl.ds(pc * hf, hf), :]

        src = in_refs[k].at[pl.ds(c * hf, hf), :]
        for j, chip in enumerate(chips):
            s = 6 * k + j
            first.append(_remote(src, half(x, y, c), send_sems.at[s], recv_sems.at[s], (*chip, c)))
            passed.append((_remote(src, half(*chip, c), send_sems.at[s], recv_sems.at[s], (*chip, c)),
                           _remote(half(*chip, c), half(*chip, c), send_sems.at[s + 3], recv_sems.at[s + 3], sibling),
                           _remote(src, half(*chip, 1 - c), send_sems.at[s + 3], recv_sems.at[s + 3], sibling)))

    def send():
        for cp in first:
            cp.start()

    def forward():
        for landed, fwd, _ in passed:
            landed.wait_recv()
            fwd.start()

    def finish():
        for _, _, from_sibling in passed:
            from_sibling.wait_recv()
        for cp in first:
            cp.wait_send()
        for _, fwd, _ in passed:
            fwd.wait_send()

    return send, forward, finish


def _pair_exchange(grads, *, name):
    n = len(grads)
    side = ("pair", grads)

    def body(*refs):
        send, _, finish = _side_phases(side, refs[:n], refs[n:2 * n], refs[2 * n:])
        send()
        finish()

    return pl.pallas_call(
        body,
        in_specs=[ANY] * n,
        out_specs=[ANY] * n,
        out_shape=_side_out_shapes(side),
        scratch_shapes=_side_sems(side),
        name=name,
    )(*grads)


def _pair_exchange_phases(g_refs, land_refs, send_sems, recv_sems):
    x, y, c = _place()
    cps = []
    for k in range(len(g_refs)):
        hf = g_refs[k].shape[1] // 2
        src = g_refs[k].at[:, pl.ds((1 - c) * hf, hf), :]
        cps.append(_remote(src, land_refs[k], send_sems.at[k], recv_sems.at[k], (x, y, 1 - c)))

    def send():
        for cp in cps:
            cp.start()

    def finish():
        for cp in cps:
            cp.wait()

    return send, finish


def _side_out_shapes(side):
    kind, arrays = side
    if kind == "gather":
        return _gather_out_shapes(arrays)
    if kind == "pair":
        return [jax.ShapeDtypeStruct((N_CHIPS, g.shape[1] // 2, g.shape[2]), g.dtype) for g in arrays]
    return [jax.ShapeDtypeStruct(p.shape, p.dtype) for p in arrays]


def _side_sems(side):
    kind, arrays = side
    n = len(arrays)
    if kind == "gather":
        return _gather_sems(n)
    if kind == "pair":
        return [pltpu.SemaphoreType.DMA((n,)), pltpu.SemaphoreType.DMA((n,))]
    return _chip_exchange_sems(n)


def _side_phases(side, in_refs, out_refs, sems):
    kind = side[0]
    if kind == "gather":
        return _gather_phases(in_refs, out_refs, *sems)
    send, finish = (_pair_exchange_phases if kind == "pair" else _chip_exchange_phases)(in_refs, out_refs, *sems)
    return send, None, finish


def _run_side(side, in_refs, out_refs, sems, step, n_steps):
    first, mid, last = _side_phases(side, in_refs, out_refs, sems)
    pl.when(step == 0)(first)
    if mid is not None:
        pl.when(step == n_steps // 2)(mid)
    return lambda: pl.when(step == n_steps - 1)(last)


def _pair_add(g, land, c_arr, *, name):
    _, a, b = g.shape
    hf = a // 2

    def body(c_ref, g_ref, l_ref, o_ref):
        o_ref[...] = (g_ref[...] + l_ref[...]).astype(BF16)

    return pl.pallas_call(
        body,
        grid_spec=pltpu.PrefetchScalarGridSpec(
            num_scalar_prefetch=1,
            grid=(N_CHIPS,),
            in_specs=[pl.BlockSpec((None, None, hf, b), lambda s, c_ref: (s, c_ref[0], 0, 0)),
                      pl.BlockSpec((None, hf, b), lambda s, c_ref: (s, 0, 0))],
            out_specs=pl.BlockSpec((None, hf, b), lambda s, c_ref: (s, 0, 0)),
        ),
        out_shape=jax.ShapeDtypeStruct((N_CHIPS, hf, b), BF16),
        compiler_params=pltpu.CompilerParams(dimension_semantics=("parallel",)),
        name=name,
    )(c_arr, g.reshape(N_CHIPS, 2, hf, b), land)


def _chip_exchange(parts):
    n = len(parts)

    def body(*refs):
        send, finish = _chip_exchange_phases(refs[:n], refs[n:2 * n], *refs[2 * n:])
        send()
        finish()

    return pl.pallas_call(
        body,
        in_specs=[ANY] * n,
        out_specs=[ANY] * n,
        out_shape=[jax.ShapeDtypeStruct(p.shape, p.dtype) for p in parts],
        scratch_shapes=_chip_exchange_sems(n),
        name="chip_exchange",
    )(*parts)


def _chip_exchange_sems(n):
    return [pltpu.SemaphoreType.DMA((3 * n,)), pltpu.SemaphoreType.DMA((3 * n,))]


def _chip_exchange_phases(p_refs, land_refs, send_sems, recv_sems):
    x, y, c = _place()
    me = 2 * x + y
    sends, recvs = [], []
    for k in range(len(p_refs)):
        for j, (cx, cy) in enumerate(_other_chips(x, y)):
            s = 3 * k + j
            sends.append(_remote(p_refs[k].at[2 * cx + cy], land_refs[k].at[me], send_sems.at[s], recv_sems.at[s], (cx, cy, c)))
            recvs.append(_remote(p_refs[k].at[me], land_refs[k].at[2 * cx + cy], send_sems.at[s], recv_sems.at[s], (cx, cy, c)))

    def send():
        for cp in sends:
            cp.start()

    def finish():
        for cp in recvs:
            cp.wait_recv()
        for cp in sends:
            cp.wait_send()

    return send, finish


def _chip_add(land, part, me_arr, *, name):
    _, r, b = land.shape

    def body(me_ref, p_ref, l1_ref, l2_ref, l3_ref, o_ref):
        o_ref[...] = ((p_ref[...].astype(F32) + l1_ref[...].astype(F32)) + l2_ref[...].astype(F32)) + l3_ref[...].astype(F32)

    tr = r // 2
    other = lambda j: pl.BlockSpec((None, tr, b), lambda i, me_ref: (jnp.bitwise_xor(me_ref[0], j), i, 0))
    return pl.pallas_call(
        body,
        grid_spec=pltpu.PrefetchScalarGridSpec(
            num_scalar_prefetch=1,
            grid=(r // tr,),
            in_specs=[pl.BlockSpec((None, tr, b), lambda i, me_ref: (me_ref[0], i, 0)), other(2), other(1), other(3)],
            out_specs=pl.BlockSpec((tr, b), lambda i, me_ref: (i, 0)),
        ),
        out_shape=jax.ShapeDtypeStruct((r, b), F32),
        compiler_params=pltpu.CompilerParams(dimension_semantics=("parallel",)),
        name=name,
    )(me_arr, part, land, land, land)


def _pair_share(halves):
    n = len(halves)

    def body(*refs):
        h_refs, out_refs = refs[:n], refs[n:2 * n]
        send_sems, recv_sems = refs[2 * n:]
        x, y, c = _place()
        cps = [_remote(h_refs[k], out_refs[k], send_sems.at[k], recv_sems.at[k], (x, y, 1 - c)) for k in range(n)]
        for cp in cps:
            cp.start()
        for cp in cps:
            cp.wait()

    return pl.pallas_call(
        body,
        in_specs=[ANY] * n,
        out_specs=[ANY] * n,
        out_shape=[jax.ShapeDtypeStruct(h.shape, F32) for h in halves],
        scratch_shapes=[pltpu.SemaphoreType.DMA((n,)), pltpu.SemaphoreType.DMA((n,))],
        name="pair_share",
    )(*halves)


def _all_sum_small(part):
    def body(p_ref, o_ref, slots, send_sems, recv_sems):
        x, y, c = _place()
        me = 4 * x + 2 * y + c
        slots[me] = p_ref[...]
        peers = [(x ^ dx, y ^ dy, c ^ dc) for dx in (0, 1) for dy in (0, 1) for dc in (0, 1)][1:]
        sends = [_remote(p_ref, slots.at[me], send_sems.at[k], recv_sems.at[k], peer) for k, peer in enumerate(peers)]
        for cp in sends:
            cp.start()
        for k, (px, py, pc) in enumerate(peers):
            _remote(p_ref, slots.at[4 * px + 2 * py + pc], send_sems.at[k], recv_sems.at[k], (px, py, pc)).wait_recv()
        for cp in sends:
            cp.wait_send()
        acc = slots[0]
        for d in range(1, 8):
            acc = acc + slots[d]
        o_ref[...] = acc

    vmem = pl.BlockSpec(memory_space=pltpu.VMEM)
    return pl.pallas_call(
        body,
        in_specs=[vmem],
        out_specs=vmem,
        out_shape=jax.ShapeDtypeStruct(part.shape, F32),
        scratch_shapes=[pltpu.VMEM((8,) + part.shape, F32), pltpu.SemaphoreType.DMA((7,)), pltpu.SemaphoreType.DMA((7,))],
        name="all_sum_small",
    )(part)


def _deinterleave(a, d):
    b, s, c = a.shape
    return a.reshape(b, s // d, d, c).transpose(0, 2, 1, 3).reshape(b * s // BAND, BAND, c)


def _reinterleave(a, d, b):
    c = a.shape[-1]
    return a.reshape(b, d, SEQ // d, c).transpose(0, 2, 1, 3).reshape(b, SEQ, c)


def _rope_tables():
    half = HEAD_DIM // 2
    inv_freq = np.float32(ROPE_THETA) ** (-np.arange(half, dtype=np.float32) * np.float32(2.0) / np.float32(HEAD_DIM))
    ang = np.arange(SEQ, dtype=np.float32)[:, None] * inv_freq[None, :].astype(np.float32)
    cos = np.tile(np.cos(ang).astype(np.float32), (1, 2 * BAND_HEADS))
    sin = np.tile(np.concatenate([-np.sin(ang), np.sin(ang)], axis=1).astype(np.float32), (1, BAND_HEADS))
    return jnp.asarray(cos), jnp.asarray(sin)


def _band_groups():
    out = []
    for d in DIL_D:
        rows = SEQ // d
        cls = max(1, 512 // rows) if d > 1 else 1
        out.append(dict(rows=rows, cls=cls, steps=d // cls))
    return out


def _local_step(x, mem, loss_target, g_pre_mix, g_post_mix, g_pre_ffn, g_post_ffn, g_mem, b_gate, w, comm=None):
    bl = x.shape[0]
    t = bl * SEQ
    chips = range(N_CHIPS)
    w_in_full = jnp.concatenate([w["w_in"][s] for s in chips], axis=1)
    half_ff = D_FF // 2

    def with_gathered(w, names, gathered, shards):
        return {**w, **{name: lax.dynamic_update_slice(g, s[None], (comm["me"][0], 0, 0))
                        for name, g, s in zip(names, gathered, shards)}}

    x2 = x.reshape(t, D_MODEL)
    tgt2 = loss_target.reshape(t, D_MODEL)
    mem2 = mem.reshape(bl * MEM_LEN, D_MODEL)

    h = _norm_fwd(x2, g_pre_mix, name="norm_x")
    proj = _mm([(h, w_in_full)], nt=False, tn=2176, out_dtypes=[BF16], name="proj",
               side=("gather", comm["mid_shards"]) if comm else None)
    if comm:
        w = with_gathered(w, comm["mid_names"], proj[1], comm["mid_shards"])
        proj = proj[0]
    w_mem_kv_full = w["w_mem_kv"].reshape(D_MODEL, 2 * MEM_W)
    gates = _mm([(h, w["w_gate"], None, "j")], nt=False,tn=w["w_gate"].shape[2], out_dtypes=[BF16], name="gates",
                bias=b_gate, epilogue=lambda acc: (_sigmoid(acc),))
    hm = _norm_fwd(mem2, g_mem, name="norm_mem")
    kv_m = _mm([(hm, w_mem_kv_full)], nt=False,tn=1024, out_dtypes=[BF16], name="mem_kv")
    proj3 = proj.reshape(bl, SEQ, D_IN)
    kv3 = kv_m.reshape(bl, MEM_LEN, 2 * MEM_W)

    o_a, o_a32, sb_weights, late_gathered = _sb_fwd(proj3, comm["late_shards"] if comm else [])
    if comm:
        w = with_gathered(w, comm["late_names"], late_gathered, comm["late_shards"])
    w_o_full = w["w_o"].reshape(D_MODEL, D_MODEL)
    w_ffn_out_full = w["w_ffn_out"].reshape(D_FF, D_MODEL)

    cos_t, sin_t = _rope_tables()
    dil0 = 3 * SB_W

    grp_w = 3 * DIL_W
    band = []
    for g, (d, cfg) in enumerate(zip(DIL_D, _band_groups())):
        a_g = proj3 if d == 1 else proj3[:, :, dil0 + g * grp_w:dil0 + (g + 1) * grp_w].reshape(bl, SEQ // d, d * grp_w)
        band.append(dict(cfg, a=a_g, col0=dil0 // grp_w if d == 1 else None, cos=cos_t.reshape(SEQ // d, d * DIL_W),
                         sin=sin_t.reshape(SEQ // d, d * DIL_W)))
    ols = [_band_group_fwd(b["a"], b["cos"], b["sin"], rows=b["rows"], cls=b["cls"], steps=b["steps"], col0=b["col0"],
                           name=f"band_fwd_{g}").reshape(t, 2 * DIL_W) for g, b in enumerate(band)]
    o_b, lse_b = _band_merge3(ols)

    o_c = _mem_fwd(proj3, kv3)

    o_a2, o_c2 = o_a.reshape(t, SB_W), o_c.reshape(t, MEM_W)
    y_a, y_b, y_c, merged = _branch_merge_fwd(o_a2, o_b, o_c2, w["w_br_sb"], w["w_br_dil"], w["w_br_mem"], gates)
    mix = _mm([(merged, w_o_full)], nt=False,tn=1024, out_dtypes=[F32], name="mix")
    x1, h2 = _mid_fwd(mix, x2, g_post_mix, g_pre_ffn)
    gg, uu, f = _ffn_in_fwd(h2, w["w_ffn_in"])
    f2 = _mm([(f, w_ffn_out_full)], nt=False,tn=1024, out_dtypes=[F32], name="ffn_out")

    dy, df2, dg_post_ffn, loss_row = _loss_bwd(f2, x1, g_post_ffn, tgt2)

    dg_ffn, du_ffn = _mm([(df2, w_ffn_out_full)], nt=True,tn=half_ff, out_dtypes=[BF16, BF16], name="d_ffn_act",
                         extras=(gg, uu), epilogue=_swiglu_bwd_epilogue)
    gw = {}
    gw["w_ffn_out"] = _mm_tn(f, df2, tm=half_ff, tn=1024, name="gw_ffn_out").reshape(N_CHIPS, D_FF // N_CHIPS, D_MODEL)
    gw_ffn_g = _mm_tn(h2, dg_ffn, tm=1024, tn=half_ff, name="gw_ffn_gate", out_shards=True)
    gw_ffn_u = _mm_tn(h2, du_ffn, tm=1024, tn=half_ff, name="gw_ffn_up", out_shards=True)
    gw["w_ffn_in"] = jnp.concatenate([gw_ffn_g, gw_ffn_u], axis=0)
    dh2 = _mm([(dg_ffn, w["w_ffn_in"], 0, 0), (dg_ffn, w["w_ffn_in"], 1, 1), (du_ffn, w["w_ffn_in"], 0, 2),
               (du_ffn, w["w_ffn_in"], 1, 3)], nt=True,tn=1024, out_dtypes=[F32], name="d_h2")
    dx1, dmix, dg_pre_ffn, dg_post_mix = _mid_bwd(dh2, x1, mix, g_pre_ffn, g_post_mix, dy)

    gw["w_o"] = _mm_tn(merged, dmix, tm=1024, tn=1024, name="gw_o").reshape(N_CHIPS, D_MODEL // N_CHIPS, D_MODEL)
    dmerged = _mm([(dmix, w_o_full)], nt=True,tn=1024, out_dtypes=[F32], name="d_merged")
    dy_a, dy_b, dy_c, dgpre, db_gate = _gate_bwd(dmerged, gates, y_a, y_b, y_c)
    br_cols = D_MODEL // N_CHIPS
    gw["w_br_sb"] = _mm_tn(o_a2, dy_a, tm=512, tn=br_cols, name="gw_br_sb", out_shards=True)
    gw["w_br_dil"] = _mm_tn(o_b, dy_b, tm=256, tn=br_cols, name="gw_br_dil", out_shards=True)
    gw["w_br_mem"] = _mm_tn(o_c2, dy_c, tm=512, tn=br_cols, name="gw_br_mem", out_shards=True)
    gw["w_gate"] = _mm_tn(h, dgpre, tm=1024, tn=w["w_gate"].shape[2], name="gw_gate", out_shards=True)
    do_a = _mm([(dy_a, w["w_br_sb"], s, s) for s in chips], nt=True,tn=SB_W, out_dtypes=[BF16], name="d_o_a")
    do_b = _mm([(dy_b, w["w_br_dil"], s, s) for s in chips], nt=True,tn=DIL_W, out_dtypes=[BF16], name="d_o_b")
    do_c = _mm([(dy_c, w["w_br_mem"], s, s) for s in chips], nt=True,tn=MEM_W, out_dtypes=[BF16], name="d_o_c")

    dq_c, dk_m, dv_m = _mem_bwd(proj3, kv3, do_c.reshape(bl, SEQ, MEM_W))
    dkv_m = jnp.concatenate([dk_m, dv_m], axis=-1).reshape(bl * MEM_LEN, 2 * MEM_W).astype(BF16)
    gw["w_mem_kv"] = _mm_tn(hm, dkv_m, tm=1024, tn=1024, name="gw_mem_kv").reshape(N_CHIPS, D_MODEL // N_CHIPS, 2 * MEM_W)
    dhm = _mm([(dkv_m, w_mem_kv_full)], nt=True,tn=1024, out_dtypes=[F32], name="d_hm")
    dg_mem = _mem_norm_bwd(dhm, mem2, g_mem)

    dcat = _band_delta(do_b, o_b, lse_b)
    early = [name for name, _, _ in PACK if name != "w_in"] if comm else []
    grads = [gw[name] for name in early]
    d_dil = []
    for g, (d, b) in enumerate(zip(DIL_D, band)):
        out = _band_group_bwd(b["a"], dcat.reshape(bl, SEQ // d, d * grp_w), b["cos"], b["sin"], rows=b["rows"], cls=b["cls"],
                              steps=b["steps"], col0=b["col0"], name=f"band_bwd_{g}",
                              side=("pair", grads) if comm and g == 0 else None)
        if comm and g == 0:
            out, lands = out
        d_dil.append(out.reshape(bl, SEQ, grp_w))

    parts = [_pair_add(g, l, comm["c"], name="pair_add_" + name) for name, g, l in zip(early, grads, lands)] if comm else []
    dq_a, dk_a, dv_a, lands = _sb_bwd(proj3, o_a32, do_a.reshape(bl, SEQ, SB_W), sb_weights, parts)
    reduced = {name: (p, l) for name, p, l in zip(early, parts, lands)}

    dproj = jnp.concatenate([dq_a, dk_a, dv_a] + d_dil + [dq_c], axis=-1).reshape(t, D_IN)
    in_cols = D_IN // N_CHIPS
    dproj_s = jnp.stack([dproj[:, s * in_cols:(s + 1) * in_cols] for s in chips])
    gw["w_in"] = _mm_tn(h, dproj_s, tm=1024, tn=in_cols, name="gw_in")
    if comm:
        land = _pair_exchange([gw["w_in"]], name="pair_exchange_w_in")[0]
        part_in = _pair_add(gw["w_in"], land, comm["c"], name="pair_add_w_in")
    dh = _mm([(dproj_s, w["w_in"], s, s) for s in chips] + [(dgpre, w["w_gate"], s, s) for s in chips],
             nt=True, tn=1024, out_dtypes=[F32], name="d_h", side=("chip", [part_in]) if comm else None)
    if comm:
        dh, (land_in,) = dh
        reduced["w_in"] = (part_in, land_in)
    grad_x, dg_pre_mix = _first_bwd(dh, x2, g_pre_mix, dx1)
    small = jnp.concatenate([dg_pre_mix, dg_post_mix, dg_pre_ffn, dg_post_ffn, dg_mem, db_gate.reshape(3, D_MODEL)], axis=0)
    return loss_row[0, 0], grad_x.reshape(bl, SEQ, D_MODEL), gw, small, reduced


def kernel(x, mem, g_pre_mix, g_post_mix, g_pre_ffn, g_post_ffn, g_mem, w_in, w_mem_kv, w_br_sb, w_br_dil, w_br_mem, w_gate, b_gate, w_o, w_ffn_in, w_ffn_out, loss_target, m_g_pre_mix, m_g_post_mix, m_g_pre_ffn, m_g_post_ffn, m_g_mem, m_w_in, m_w_mem_kv, m_w_br_sb, m_w_br_dil, m_w_br_mem, m_w_gate, m_b_gate, m_w_o, m_w_ffn_in, m_w_ffn_out, v_g_pre_mix, v_g_post_mix, v_g_pre_ffn, v_g_post_ffn, v_g_mem, v_w_in, v_w_mem_kv, v_w_br_sb, v_w_br_dil, v_w_br_mem, v_w_gate, v_b_gate, v_w_o, v_w_ffn_in, v_w_ffn_out):
    w_shards = dict(w_in=w_in[0], w_mem_kv=w_mem_kv[0], w_br_sb=w_br_sb[0], w_br_dil=w_br_dil[0], w_br_mem=w_br_mem[0],
                    w_gate=w_gate[0], w_o=w_o[0], w_ffn_in=w_ffn_in[0], w_ffn_out=w_ffn_out[0])
    m_shards = dict(w_in=m_w_in[0], w_mem_kv=m_w_mem_kv[0], w_br_sb=m_w_br_sb[0], w_br_dil=m_w_br_dil[0], w_br_mem=m_w_br_mem[0],
                    w_gate=m_w_gate[0], w_o=m_w_o[0], w_ffn_in=m_w_ffn_in[0], w_ffn_out=m_w_ffn_out[0])
    v_shards = dict(w_in=v_w_in[0], w_mem_kv=v_w_mem_kv[0], w_br_sb=v_w_br_sb[0], w_br_dil=v_w_br_dil[0], w_br_mem=v_w_br_mem[0],
                    w_gate=v_w_gate[0], w_o=v_w_o[0], w_ffn_in=v_w_ffn_in[0], w_ffn_out=v_w_ffn_out[0])

    names = [name for name, _, _ in PACK]
    c_arr = lax.axis_index("c").astype(jnp.int32).reshape(1)
    me_arr = (2 * lax.axis_index("x") + lax.axis_index("y")).astype(jnp.int32).reshape(1)
    mid_names = ["w_gate", "w_mem_kv"]
    late_names = [name for name in names if name not in ["w_in"] + mid_names]
    bf = {name: w_shards[name].astype(BF16) for name in names}
    w = {"w_in": lax.dynamic_update_slice(_gather_weights([bf["w_in"]])[0], bf["w_in"][None], (me_arr[0], 0, 0))}
    comm = dict(c=c_arr, me=me_arr, mid_names=mid_names, mid_shards=[bf[name] for name in mid_names],
                late_names=late_names, late_shards=[bf[name] for name in late_names])

    loss_local, grad_x, gw, small, reduced = _local_step(x, mem, loss_target, g_pre_mix, g_post_mix, g_pre_ffn, g_post_ffn,
                                                         g_mem, b_gate, w, comm)
    loss = lax.psum(loss_local, ("x", "y", "c"))

    halves =[_chip_add(reduced[name][1], reduced[name][0], me_arr, name="chip_add_" + name) for name in names]
    theirs = _pair_share(halves)
    small = _all_sum_small(small)

    upd = {}
    for name, mine, other in zip(names, halves, theirs):
        upd[name] = _adamw_halves(w_shards[name], mine, other, m_shards[name], v_shards[name], c_arr, name="adamw_" + name)
    g_shards = {name: u[0] for name, u in upd.items()}

    def small8(gs, b):
        return jnp.concatenate(gs + [b.reshape(3, D_MODEL)], axis=0)

    sw = small8([g_pre_mix, g_post_mix, g_pre_ffn, g_post_ffn, g_mem], b_gate)
    sm = small8([m_g_pre_mix, m_g_post_mix, m_g_pre_ffn, m_g_post_ffn, m_g_mem], m_b_gate)
    sv = small8([v_g_pre_mix, v_g_post_mix, v_g_pre_ffn, v_g_post_ffn, v_g_mem], v_b_gate)
    s_upd = _adamw(sw, small, sm, sv, tm=8, name="adamw_small")

    def small_out(a):
        return [a[0:1], a[1:2], a[2:3], a[3:4], a[4:5]]

    order = ["w_in", "w_mem_kv", "w_br_sb", "w_br_dil", "w_br_mem", "w_gate", "b_gate", "w_o", "w_ffn_in", "w_ffn_out"]

    def leaves(small_arr, big):
        out = small_out(small_arr)
        for name in order:
            out.append(small_arr[5:8].reshape(1, 3 * D_MODEL) if name == "b_gate" else big[name][None])
        return out

    grads_out = leaves(small, g_shards)
    delta_out = leaves(s_upd[0], {n: u[1] for n, u in upd.items()})
    m_out = leaves(s_upd[1], {n: u[2] for n, u in upd.items()})
    v_out = leaves(s_upd[2], {n: u[3] for n, u in upd.items()})
    return (loss, grad_x, *grads_out, *delta_out, *m_out, *v_out)
```

```python
import jax
import jax.numpy as jnp
import numpy as np
from jax import lax
from jax.experimental import pallas as pl
from jax.experimental.pallas import tpu as pltpu

F32 = jnp.float32
BF16 = jnp.bfloat16
MESH = pl.DeviceIdType.MESH

D_MODEL = 1024
SEQ = 2048
HEAD_DIM = 64
SB_W = 512
DIL_W = 256
MEM_W = 512
MEM_LEN = 256
D_IN = 3 * SB_W + 9 * DIL_W + MEM_W
D_FF = 2816
DIL_D = (1, 4, 16)
ROPE_THETA = 10000.0
NORM_EPS = 1e-6
NEG_INF = -1e30
LANES = 128

ADAM_LR = 0.001
ADAM_B1 = 0.9
ADAM_B2 = 0.999
ADAM_EPS = 1e-08
ADAM_WD = 0.01
ADAM_STEP = 10

N_CHIPS = 4
PACK = (
    ("w_in", (1024, 1088), 1),
    ("w_mem_kv", (256, 1024), 0),
    ("w_br_sb", (512, 256), 1),
    ("w_br_dil", (256, 256), 1),
    ("w_br_mem", (512, 256), 1),
    ("w_gate", (1024, 768), 1),
    ("w_o", (256, 1024), 0),
    ("w_ffn_in", (1024, 1408), 1),
    ("w_ffn_out", (704, 1024), 0),
)
PACK_ROWS = sum(a * b for _, (a, b), _ in PACK) // D_MODEL
HALF_ROWS = PACK_ROWS // 2


def _dot(a, b):
    return lax.dot_general(a, b, (((1,), (0,)), ((), ())), preferred_element_type=F32)


def _dot_nt(a, b):
    return lax.dot_general(a, b, (((1,), (1,)), ((), ())), preferred_element_type=F32)


def _dot_tn(a, b):
    return lax.dot_general(a, b, (((0,), (0,)), ((), ())), preferred_element_type=F32)


def _split_dot(x, u):
    hi = x.astype(BF16)
    lo = (x - hi.astype(F32)).astype(BF16)
    return _dot(hi, u) + _dot(lo, u)


V7X_VMEM_BUDGET = 44 * 2 ** 20


def _rows_that_fit(m, row_bytes, fixed_bytes):
    for tm in (1024, 512, 256, 128):
        if m % tm == 0 and fixed_bytes + tm * row_bytes <= V7X_VMEM_BUDGET:
            return tm
    return min(m, 128)


def _mm(pairs, *, nt, tn, out_dtypes, name, bias=None, extras=(), epilogue=None, side=None):
    pairs = [p if len(p) == 4 else (p[0], p[1], None, None) for p in pairs]
    m = pairs[0][0].shape[-2]
    b0 = pairs[0][1]
    if nt:
        n = b0.shape[-2]
    else:
        n = b0.shape[-1] * (b0.shape[0] if b0.ndim == 3 else 1)
    n_pairs, n_extra, n_out = len(pairs), len(extras), len(out_dtypes)
    assert n % tn == 0
    one_col = n == tn
    ks = [(b.shape[-1] if nt else b.shape[-2]) for _, b, _, _ in pairs]
    fixed = sum(k * tn * 2 for k in ks) * (1 if one_col else 2)
    row_bytes = 2 * sum(k * 2 for k in ks) + 2 * tn * (sum(jnp.dtype(dt).itemsize for dt in out_dtypes) + 2 * n_extra) + 2 * tn * 4
    tm = _rows_that_fit(m, row_bytes, fixed)
    assert m % tm == 0
    b_mode = dict(pipeline_mode=pl.Buffered(1)) if one_col else {}
    has_bias = bias is not None
    side_arrays = side[1] if side else []
    n_side = len(side_arrays)
    n_main_in = 2 * n_pairs + has_bias + n_extra
    n_steps = (n // tn) * (m // tm)

    def body(*refs):
        if n_side:
            step = pl.program_id(0) * (m // tm) + pl.program_id(1)
            finish = _run_side(side, refs[n_main_in:n_main_in + n_side],
                               refs[n_main_in + n_side + n_out:n_main_in + 2 * n_side + n_out],
                               refs[n_main_in + 2 * n_side + n_out:], step, n_steps)
        outs = refs[n_main_in + n_side:n_main_in + n_side + n_out]
        half = (tn // 256) * LANES if epilogue is not None else 0
        for lo, hi in ([(0, half), (half, tn)] if half else [(0, tn)]):
            acc = None
            for i in range(n_pairs):
                a, b_ref = refs[2 * i][...], refs[2 * i + 1]
                p = _dot_nt(a, b_ref[lo:hi, :]) if nt else _dot(a, b_ref[:, lo:hi])
                acc = p if acc is None else acc + p
            pos = 2 * n_pairs
            if has_bias:
                acc = acc + refs[pos][:, lo:hi]
                pos += 1
            ex = [r[:, lo:hi] for r in refs[pos:pos + n_extra]]
            vals = (acc,) if epilogue is None else epilogue(acc, *ex)
            for r, v, dt in zip(outs, vals, out_dtypes):
                r[:, lo:hi] = v.astype(dt)
        if n_side:
            finish()

    in_specs, args = [], []
    for a, b, a_col, b_sel in pairs:
        k = b.shape[-1] if nt else b.shape[-2]
        assert a_col is not None or a.shape[1] == k
        if a.ndim == 3:
            in_specs.append(pl.BlockSpec((None, tm, k), lambda j, i, c=a_col: (c, i, 0)))
        else:
            in_specs.append(pl.BlockSpec((tm, k), lambda j, i, c=a_col or 0: (i, c)))
        if b.ndim == 2:
            in_specs.append(pl.BlockSpec((tn, k), lambda j, i: (j, 0), **b_mode) if nt
                            else pl.BlockSpec((k, tn), lambda j, i: (0, j), **b_mode))
        elif nt:
            in_specs.append(pl.BlockSpec((None, tn, k), lambda j, i, s=b_sel: (s, j, 0), **b_mode))
        else:
            assert b_sel == "j" and b.shape[-1] == tn
            in_specs.append(pl.BlockSpec((None, k, tn), lambda j, i: (j, 0, 0), **b_mode))
        args += [a, b]
    if has_bias:
        in_specs.append(pl.BlockSpec((1, tn), lambda j, i: (0, j)))
        args.append(bias)
    for e in extras:
        in_specs.append(pl.BlockSpec((tm, tn), lambda j, i: (i, j)))
        args.append(e)
    out = pl.pallas_call(
        body,
        grid=(n // tn, m // tm),
        in_specs=in_specs + [ANY] * n_side,
        out_specs=[pl.BlockSpec((tm, tn), lambda j, i: (i, j)) for _ in range(n_out)] + [ANY] * n_side,
        out_shape=[jax.ShapeDtypeStruct((m, n), dt) for dt in out_dtypes] + (_side_out_shapes(side) if n_side else []),
        scratch_shapes=_side_sems(side) if n_side else [],
        compiler_params=pltpu.CompilerParams(dimension_semantics=("arbitrary", "arbitrary") if n_side else ("parallel", "parallel")),
        name=name,
    )(*args, *side_arrays)
    if n_side:
        return (out[0] if n_out == 1 else out[:n_out]), out[n_out:]
    return out[0] if n_out == 1 else out


def _mm_tn(a, b, *, tm, tn, name, out_shards=False):
    k, m = a.shape
    b_shards = b.ndim == 3
    out_shards = out_shards or b_shards
    n = b.shape[0] * b.shape[2] if b_shards else b.shape[1]
    tk = _rows_that_fit(k, 2 * 2 * (tm + tn), 3 * tm * tn * 4)
    assert m % tm == 0 and n % tn == 0 and k % tk == 0 and (not b_shards or b.shape[2] == tn)

    def body(a_ref, b_ref, o_ref):
        @pl.when(pl.program_id(2) == 0)
        def _():
            o_ref[...] = jnp.zeros_like(o_ref)

        o_ref[...] += _dot_tn(a_ref[...], b_ref[...])

    if b_shards:
        b_spec = pl.BlockSpec((None, tk, tn), lambda i, j, kk: (j, kk, 0))
    else:
        b_spec = pl.BlockSpec((tk, tn), lambda i, j, kk: (kk, j))
    if out_shards:
        out_spec = pl.BlockSpec((None, tm, tn), lambda i, j, kk: (j, i, 0))
        out_shape = jax.ShapeDtypeStruct((n // tn, m, tn), F32)
    else:
        out_spec = pl.BlockSpec((tm, tn), lambda i, j, kk: (i, j))
        out_shape = jax.ShapeDtypeStruct((m, n), F32)
    return pl.pallas_call(
        body,
        grid=(m // tm, n // tn, k // tk),
        in_specs=[pl.BlockSpec((tk, tm), lambda i, j, kk: (kk, i)), b_spec],
        out_specs=out_spec,
        out_shape=out_shape,
        compiler_params=pltpu.CompilerParams(dimension_semantics=("parallel", "parallel", "arbitrary")),
        name=name,
    )(a, b)


def _rowwise(fn, ins, outs, *, tm, name, side=None):
    rows = next(a.shape[0] for a, kind in ins if kind == "row")
    tm = min(tm, rows)
    assert rows % tm == 0
    n_in, n_out = len(ins), len(outs)
    side_arrays = side[1] if side else []
    n_side = len(side_arrays)

    def body(*refs):
        if n_side:
            finish = _run_side(side, refs[n_in:n_in + n_side], refs[n_in + n_side + n_out:n_in + 2 * n_side + n_out],
                               refs[n_in + 2 * n_side + n_out:], pl.program_id(0), rows // tm)
        vals = fn(*[r[...] for r in refs[:n_in]])
        for (_, dt, kind), r, v in zip(outs, refs[n_in + n_side:n_in + n_side + n_out], vals):
            if kind == "row":
                r[...] = v.astype(dt)
            else:
                @pl.when(pl.program_id(0) == 0)
                def _(r=r):
                    r[...] = jnp.zeros_like(r)

                r[...] += v
        if n_side:
            finish()

    in_specs = [pl.BlockSpec((tm, a.shape[1]), lambda i: (i, 0)) if kind == "row" else pl.BlockSpec(a.shape, lambda i: (0, 0))
                for a, kind in ins]
    out_specs = [pl.BlockSpec((tm, c), lambda i: (i, 0)) if kind == "row" else pl.BlockSpec((1, c), lambda i: (0, 0))
                 for c, _, kind in outs]
    out_shape = [jax.ShapeDtypeStruct((rows if kind == "row" else 1, c), dt) for c, dt, kind in outs]
    ordered = n_side or any(kind == "acc" for _, _, kind in outs)
    return pl.pallas_call(
        body,
        grid=(rows // tm,),
        in_specs=in_specs + [ANY] * n_side,
        out_specs=out_specs + [ANY] * n_side,
        out_shape=out_shape + (_side_out_shapes(side) if n_side else []),
        scratch_shapes=_side_sems(side) if n_side else [],
        compiler_params=pltpu.CompilerParams(dimension_semantics=("arbitrary" if ordered else "parallel",)),
        name=name,
    )(*[a for a, _ in ins], *side_arrays)


def _rstd(x):
    return lax.rsqrt(jnp.mean(x * x, axis=-1, keepdims=True) + NORM_EPS)


def _norm_bwd(dout, xin, g):
    r = _rstd(xin)
    n = xin * r
    dn = dout * g
    dg = jnp.sum(dout * n, axis=0, keepdims=True)
    dx = r * (dn - n * jnp.mean(dn * n, axis=-1, keepdims=True))
    return dx, dg


def _sigmoid(x):
    return 1.0 / (1.0 + jnp.exp(-x))


def _norm_fwd(x, g, *, name, side=None):
    def fn(x, g):
        return ((x * _rstd(x)) * g,)

    out = _rowwise(fn, [(x, "row"), (g, "vec")], [(D_MODEL, BF16, "row")], tm=512, name=name, side=side)
    return (out[0], out[1:]) if side else out[0]


def _mid_fwd(mix, x, g_post_mix, g_pre_ffn):
    def fn(mix, x, g2, g3):
        x1 = x + (mix * _rstd(mix)) * g2
        return x1, (x1 * _rstd(x1)) * g3

    return _rowwise(fn, [(mix, "row"), (x, "row"), (g_post_mix, "vec"), (g_pre_ffn, "vec")],
                    [(D_MODEL, F32, "row"), (D_MODEL, BF16, "row")], tm=512, name="mid_fwd")


def _loss_bwd(f2, x1, g_post_ffn, tgt):
    def fn(f2, x1, g4, tgt):
        r = _rstd(f2)
        n = f2 * r
        err = x1 + n * g4 - tgt
        loss = 0.5 * jnp.sum(jnp.mean(err * err, axis=-1, keepdims=True), axis=0, keepdims=True)
        dy = err * (1.0 / D_MODEL)
        dn = dy * g4
        dg4 = jnp.sum(dy * n, axis=0, keepdims=True)
        df2 = r * (dn - n * jnp.mean(dn * n, axis=-1, keepdims=True))
        return dy, df2, dg4, jnp.broadcast_to(loss, (1, LANES))

    return _rowwise(fn, [(f2, "row"), (x1, "row"), (g_post_ffn, "vec"), (tgt, "row")],
                    [(D_MODEL, F32, "row"), (D_MODEL, BF16, "row"), (D_MODEL, F32, "acc"), (LANES, F32, "acc")],
                    tm=512, name="loss_bwd")


def _mid_bwd(dh2, x1, mix, g_pre_ffn, g_post_mix, dy):
    def fn(dh2, x1, mix, g3, g2, dy):
        d3, dg3 = _norm_bwd(dh2, x1, g3)
        dx1 = dy + d3
        dmix, dg2 = _norm_bwd(dx1, mix, g2)
        return dx1, dmix, dg3, dg2

    return _rowwise(fn, [(dh2, "row"), (x1, "row"), (mix, "row"), (g_pre_ffn, "vec"), (g_post_mix, "vec"), (dy, "row")],
                    [(D_MODEL, F32, "row"), (D_MODEL, BF16, "row"), (D_MODEL, F32, "acc"), (D_MODEL, F32, "acc")],
                    tm=256, name="mid_bwd")


def _first_bwd(dh, x, g_pre_mix, dx1):
    def fn(dh, x, g1, dx1):
        d1, dg1 = _norm_bwd(dh, x, g1)
        return dx1 + d1, dg1

    return _rowwise(fn, [(dh, "row"), (x, "row"), (g_pre_mix, "vec"), (dx1, "row")],
                    [(D_MODEL, F32, "row"), (D_MODEL, F32, "acc")], tm=512, name="first_bwd")


def _mem_norm_bwd(dhm, mem, g_mem):
    def fn(dhm, mem, g):
        return (jnp.sum(dhm * (mem * _rstd(mem)), axis=0, keepdims=True),)

    return _rowwise(fn, [(dhm, "row"), (mem, "row"), (g_mem, "vec")], [(D_MODEL, F32, "acc")], tm=512, name="mem_norm_bwd")[0]


def _gate_bwd(dmerged, gates, ya, yb, yc):
    def fn(dm, gt, ya, yb, yc):
        gt = gt.astype(F32)
        outs, dgp = [], []
        for i, y in enumerate((ya, yb, yc)):
            gi = gt[:, i * D_MODEL:(i + 1) * D_MODEL]
            outs.append(dm * gi)
            dgp.append(dm * y.astype(F32) * gi * (1.0 - gi))
        dgpre = jnp.concatenate(dgp, axis=1)
        return outs[0], outs[1], outs[2], dgpre, jnp.sum(dgpre, axis=0, keepdims=True)

    return _rowwise(fn, [(dmerged, "row"), (gates, "row"), (ya, "row"), (yb, "row"), (yc, "row")],
                    [(D_MODEL, BF16, "row")] * 3 + [(3 * D_MODEL, BF16, "row"), (3 * D_MODEL, F32, "acc")],
                    tm=256, name="gate_bwd")


def _adamw_math(w, g, m, v):
    m = ADAM_B1 * m + (1.0 - ADAM_B1) * g
    v = ADAM_B2 * v + (1.0 - ADAM_B2) * (g * g)
    m_hat = m / (1.0 - ADAM_B1 ** ADAM_STEP)
    v_hat = v / (1.0 - ADAM_B2 ** ADAM_STEP)
    delta = -ADAM_LR * (m_hat / (jnp.sqrt(v_hat) + ADAM_EPS) + ADAM_WD * w)
    return delta, m, v


def _adamw(w, g, m, v, *, tm, name):
    c = w.shape[1]
    return _rowwise(_adamw_math, [(w, "row"), (g, "row"), (m, "row"), (v, "row")], [(c, F32, "row")] * 3, tm=tm, name=name)


def _adamw_halves(w, g_mine, g_theirs, m, v, c_arr, *, name):
    a, b = w.shape
    hf = a // 2
    tr = hf // 4

    def body(c_ref, w_ref, gm_ref, gt_ref, m_ref, v_ref, g_out, d_out, m_out, v_out):
        g = jnp.where(pl.program_id(0) == c_ref[0], gm_ref[...], gt_ref[...])
        d, m_new, v_new = _adamw_math(w_ref[...], g, m_ref[...], v_ref[...])
        g_out[...] = g
        d_out[...] = d
        m_out[...] = m_new
        v_out[...] = v_new

    full = pl.BlockSpec((tr, b), lambda hh, i, c_ref: (hh * (hf // tr) + i, 0))
    half = pl.BlockSpec((tr, b), lambda hh, i, c_ref: (i, 0))
    return pl.pallas_call(
        body,
        grid_spec=pltpu.PrefetchScalarGridSpec(
            num_scalar_prefetch=1,
            grid=(2, hf // tr),
            in_specs=[full, half, half, full, full],
            out_specs=[full] * 4,
        ),
        out_shape=[jax.ShapeDtypeStruct((a, b), F32)] * 4,
        compiler_params=pltpu.CompilerParams(dimension_semantics=("parallel", "parallel")),
        name=name,
    )(c_arr, w, g_mine, g_theirs, m, v)


def _ffn_in_fwd(h2, w_ffn):
    m, tm, tn = h2.shape[0], 512, w_ffn.shape[2]
    assert 2 * tn == D_FF

    def body(h_ref, wg_ref, wu_ref, g_ref, u_ref, f_ref):
        h = h_ref[...]
        g = _dot(h, wg_ref[...])
        u = _dot(h, wu_ref[...])
        g_ref[...] = g.astype(BF16)
        u_ref[...] = u.astype(BF16)
        f_ref[...] = (g * _sigmoid(g) * u).astype(BF16)

    o_spec = pl.BlockSpec((tm, tn), lambda j, i: (i, j))
    return pl.pallas_call(
        body,
        grid=(D_FF // tn, m // tm),
        in_specs=[pl.BlockSpec((tm, D_MODEL), lambda j, i: (i, 0)),
                  pl.BlockSpec((None, D_MODEL, tn), lambda j, i: (j, 0, 0)),
                  pl.BlockSpec((None, D_MODEL, tn), lambda j, i: (j + 2, 0, 0))],
        out_specs=[o_spec, o_spec, o_spec],
        out_shape=[jax.ShapeDtypeStruct((m, D_FF), BF16)] * 3,
        compiler_params=pltpu.CompilerParams(dimension_semantics=("parallel", "parallel")),
        name="ffn_in_fwd",
    )(h2, w_ffn, w_ffn)


def _join_shards(w4):
    _, rows, cols = w4.shape
    tr = rows // 4

    def body(w_ref, o_ref):
        for s in range(N_CHIPS):
            o_ref[:, s * cols:(s + 1) * cols] = w_ref[s]

    return pl.pallas_call(
        body,
        grid=(rows // tr,),
        in_specs=[pl.BlockSpec((N_CHIPS, tr, cols), lambda i: (0, i, 0))],
        out_specs=pl.BlockSpec((tr, N_CHIPS * cols), lambda i: (i, 0)),
        out_shape=jax.ShapeDtypeStruct((rows, N_CHIPS * cols), w4.dtype),
        compiler_params=pltpu.CompilerParams(dimension_semantics=("parallel",)),
        name="join_shards",
    )(w4)


def _split_to_shards(pieces, *, name):
    t = pieces[0].shape[0]
    widths = [p.shape[1] for p in pieces]
    cols = sum(widths) // N_CHIPS
    tm = 512
    plan, start = [], 0
    for p, wd in enumerate(widths):
        for s in range(N_CHIPS):
            lo, hi = max(start, s * cols), min(start + wd, (s + 1) * cols)
            if lo < hi:
                plan.append((s, p, lo - s * cols, hi - s * cols, lo - start, hi - start))
        start += wd

    def body(*refs):
        o_ref = refs[-1]
        for s, p, o_lo, o_hi, p_lo, p_hi in plan:
            o_ref[s, :, o_lo:o_hi] = refs[p][:, p_lo:p_hi]

    return pl.pallas_call(
        body,
        grid=(t // tm,),
        in_specs=[pl.BlockSpec((tm, wd), lambda i: (i, 0)) for wd in widths],
        out_specs=pl.BlockSpec((N_CHIPS, tm, cols), lambda i: (0, i, 0)),
        out_shape=jax.ShapeDtypeStruct((N_CHIPS, t, cols), pieces[0].dtype),
        compiler_params=pltpu.CompilerParams(dimension_semantics=("parallel",)),
        name=name,
    )(*pieces)


def _swiglu_bwd_epilogue(df, g, u):
    g = g.astype(F32)
    u = u.astype(F32)
    sg = _sigmoid(g)
    return df * u * (sg * (1.0 + g * (1.0 - sg))), df * (g * sg)


def _branch_merge_fwd(o_a, o_b, o_c, w_sb, w_dil, w_mem, gates):
    m, tm = o_a.shape[0], 256

    def body(oa_ref, ob_ref, oc_ref, wa_ref, wb_ref, wc_ref, gt_ref, ya_ref, yb_ref, yc_ref, mg_ref):
        def project(o_ref, w_ref):
            o = o_ref[...]
            return jnp.concatenate([_dot(o, w_ref[s]) for s in range(N_CHIPS)], axis=1)

        ya = project(oa_ref, wa_ref)
        yb = project(ob_ref, wb_ref)
        yc = project(oc_ref, wc_ref)
        gt = gt_ref[...].astype(F32)
        ya_ref[...] = ya.astype(BF16)
        yb_ref[...] = yb.astype(BF16)
        yc_ref[...] = yc.astype(BF16)
        mg_ref[...] = (gt[:, :D_MODEL] * ya + gt[:, D_MODEL:2 * D_MODEL] * yb + gt[:, 2 * D_MODEL:] * yc).astype(BF16)

    row = lambda c: pl.BlockSpec((tm, c), lambda i: (i, 0))
    full = lambda a: pl.BlockSpec(a.shape, lambda i: (0, 0, 0))
    return pl.pallas_call(
        body,
        grid=(m // tm,),
        in_specs=[row(SB_W), row(DIL_W), row(MEM_W), full(w_sb), full(w_dil), full(w_mem), row(3 * D_MODEL)],
        out_specs=[row(D_MODEL)] * 4,
        out_shape=[jax.ShapeDtypeStruct((m, D_MODEL), BF16)] * 4,
        compiler_params=pltpu.CompilerParams(dimension_semantics=("parallel",)),
        name="branch_merge_fwd",
    )(o_a, o_b, o_c, w_sb, w_dil, w_mem, gates)


SB_T = 256
SB_SCALE = HEAD_DIM ** -0.5


def _sb_masks():
    row = lax.broadcasted_iota(jnp.int32, (SB_T, SB_T), 0)
    col = lax.broadcasted_iota(jnp.int32, (SB_T, SB_T), 1)
    lane = lax.broadcasted_iota(jnp.int32, (1, LANES), 1)
    return row, col, lane


def _sb_logs(z):
    e = jnp.exp(-jnp.abs(z))
    lb = jnp.minimum(z, 0.0) - jnp.log(1.0 + e)
    return lb, lb - z, e


def _sb_specs(n_heads_pairs, col0):
    q = pl.BlockSpec((None, SB_T, LANES), lambda b, p, i: (b, i, col0 + p))
    k = pl.BlockSpec((None, SEQ, LANES), lambda b, p, i: (b, 0, col0 + n_heads_pairs + p))
    v = pl.BlockSpec((None, SEQ, LANES), lambda b, p, i: (b, 0, col0 + 2 * n_heads_pairs + p))
    return q, k, v


def _grid_step(n_pairs, nq):
    return (pl.program_id(0) * n_pairs + pl.program_id(1)) * nq + pl.program_id(2)


def _sb_fwd(proj3, late_shards):
    bl = proj3.shape[0]
    n_pairs = SB_W // LANES
    nq = SEQ // SB_T
    n_late = len(late_shards)
    n_steps = bl * n_pairs * nq

    def body(q_ref, k_ref, v_ref, *rest):
        late_in, (o_ref, o32_ref, w_ref), late_out = rest[:n_late], rest[n_late:n_late + 3], rest[n_late + 3:2 * n_late + 3]
        step = _grid_step(n_pairs, nq)
        if n_late:
            send, forward, finish = _gather_phases(late_in, late_out, *rest[2 * n_late + 3:])
            pl.when(step == 0)(send)
            pl.when(step == n_steps // 2)(forward)
        i = pl.program_id(2)
        row, col, lane = _sb_masks()
        causal = col < row
        u_excl = (row > col).astype(BF16)
        q = q_ref[...]
        heads = []
        for h in range(2):
            mh = (lane // HEAD_DIM) == h
            heads.append((mh, jnp.where(mh, q, jnp.zeros_like(q)) * SB_SCALE))

        def blocks(js, diags, carries, acc):
            ks = [k_ref[pl.ds(pl.multiple_of(j * SB_T, SB_T), SB_T), :] for j in js]
            vs = [v_ref[pl.ds(pl.multiple_of(j * SB_T, SB_T), SB_T), :] for j in js]
            chains = [(b, h) for b in range(len(js)) for h in range(2)]
            z = {c: _dot_nt(heads[c[1]][1], ks[c[0]]) for c in chains}
            lb, lk = {}, {}
            for c in chains:
                lb[c], lk[c], _ = _sb_logs(z[c])
                if diags[c[0]]:
                    lk[c] = jnp.where(causal, lk[c], 0.0)
            r = {c: _split_dot(lk[c], u_excl) for c in chains}
            carries = list(carries)
            w = {}
            for b, h in chains:
                w_c = jnp.exp(lb[b, h] + r[b, h] + carries[h])
                w[b, h] = (jnp.where(causal, w_c, 0.0) if diags[b] else w_c).astype(BF16)
                w_ref[h, js[b]] = w[b, h]
                carries[h] = carries[h] + (r[b, h][:, :1] + lk[b, h][:, :1])
            for b, h in chains:
                acc = acc + _dot(w[b, h], jnp.where(heads[h][0], vs[b], jnp.zeros_like(vs[b])))
            return tuple(carries), acc

        zero = jnp.zeros((SB_T, 1), F32)
        init = ((zero, zero), jnp.zeros((SB_T, LANES), F32))
        odd = i % 2
        carries, acc = lax.cond(odd == 1, lambda: blocks([i, i - 1], (True, False), *init), lambda: blocks([i], (True,), *init))
        rest = i - 1 - odd
        carries, acc = lax.fori_loop(
            0, i // 2, lambda jj, c: blocks([rest - 2 * jj, rest - 1 - 2 * jj], (False, False), c[0], c[1]), (carries, acc))
        o_ref[...] = acc.astype(BF16)
        o32_ref[...] = acc
        if n_late:
            pl.when(step == n_steps - 1)(finish)

    q_spec, k_spec, v_spec = _sb_specs(n_pairs, 0)
    blk = pl.BlockSpec((None, SB_T, LANES), lambda b, p, i: (b, i, p))
    out = pl.pallas_call(
        body,
        grid=(bl, n_pairs, nq),
        in_specs=[q_spec, k_spec, v_spec] + [ANY] * n_late,
        out_specs=[blk, blk, _sb_weight_spec(nq)] + [ANY] * n_late,
        out_shape=[jax.ShapeDtypeStruct((bl, SEQ, SB_W), BF16), jax.ShapeDtypeStruct((bl, SEQ, SB_W), F32),
                   jax.ShapeDtypeStruct((bl, n_pairs, nq, 2, nq, SB_T, SB_T), BF16)] + _gather_out_shapes(late_shards),
        scratch_shapes=_gather_sems(n_late) if n_late else [],
        compiler_params=pltpu.CompilerParams(dimension_semantics=("arbitrary", "arbitrary", "arbitrary")),
        name="sb_fwd",
    )(proj3, proj3, proj3, *late_shards)
    return out[0], out[1], out[2], out[3:]


def _sb_weight_spec(nq):
    return pl.BlockSpec((None, None, None, 2, nq, SB_T, SB_T), lambda b, p, i: (b, p, i, 0, 0, 0, 0))


def _sb_bwd(proj3, o_a, do_a, w_all, parts):
    bl = proj3.shape[0]
    n_pairs = SB_W // LANES
    nq = SEQ // SB_T
    n_parts = len(parts)
    n_steps = bl * n_pairs * nq

    def body(q_ref, k_ref, v_ref, o_ref, do_ref, w_ref, *rest):
        p_refs, (dq_ref, dk_ref, dv_ref), land_refs = rest[:n_parts], rest[n_parts:n_parts + 3], rest[n_parts + 3:2 * n_parts + 3]
        dk_acc, dv_acc = rest[2 * n_parts + 3:2 * n_parts + 5]
        step = _grid_step(n_pairs, nq)
        if n_parts:
            send, finish = _chip_exchange_phases(p_refs, land_refs, *rest[2 * n_parts + 5:])
            pl.when(step == 0)(send)
        i = pl.program_id(2)

        @pl.when(i == 0)
        def _():
            dk_acc[...] = jnp.zeros_like(dk_acc)
            dv_acc[...] = jnp.zeros_like(dv_acc)

        row, col, lane = _sb_masks()
        causal = col < row
        u_incl = (row >= col).astype(BF16)
        q = q_ref[...]
        do = do_ref[...]
        prod = do.astype(F32) * o_ref[...]
        heads = []
        for h in range(2):
            mh = (lane // HEAD_DIM) == h
            d_tot = jnp.sum(jnp.where(mh, prod, 0.0), axis=1, keepdims=True)
            heads.append((mh, jnp.where(mh, q, jnp.zeros_like(q)) * SB_SCALE, jnp.where(mh, do, jnp.zeros_like(do)), d_tot))

        def blocks(js, diags, c_das, dq):
            starts = [pl.multiple_of(j * SB_T, SB_T) for j in js]
            ks = [k_ref[pl.ds(s, SB_T), :] for s in starts]
            vs = [v_ref[pl.ds(s, SB_T), :] for s in starts]
            chains = [(b, h) for b in range(len(js)) for h in range(2)]
            z = {c: _dot_nt(heads[c[1]][1], ks[c[0]]) for c in chains}
            dw = {c: _dot_nt(heads[c[1]][2], vs[c[0]]) for c in chains}
            wb = {(b, h): w_ref[h, js[b]] for b, h in chains}
            da = {c: dw[c] * wb[c].astype(F32) for c in chains}
            sfx = {c: _split_dot(da[c], u_incl) for c in chains}
            c_das = list(c_das)
            dz = {}
            for b, h in chains:
                dlk = heads[h][3] - c_das[h] - sfx[b, h]
                if diags[b]:
                    dlk = jnp.where(causal, dlk, 0.0)
                c_das[h] = c_das[h] + sfx[b, h][:, :1]
                e = jnp.exp(-jnp.abs(z[b, h]))
                inv = 1.0 / (1.0 + e)
                pos = z[b, h] >= 0.0
                beta = jnp.where(pos, inv, e * inv)
                one_m_beta = jnp.where(pos, e * inv, inv)
                dz[b, h] = (da[b, h] * one_m_beta - dlk * beta).astype(BF16)
            for b, h in chains:
                dq = dq + _dot(dz[b, h], jnp.where(heads[h][0], ks[b], jnp.zeros_like(ks[b])))
            for b in range(len(js)):
                dk_acc[pl.ds(starts[b], SB_T), :] += _dot_tn(dz[b, 0], heads[0][1]) + _dot_tn(dz[b, 1], heads[1][1])
                dv_acc[pl.ds(starts[b], SB_T), :] += _dot_tn(wb[b, 0], heads[0][2]) + _dot_tn(wb[b, 1], heads[1][2])
            return tuple(c_das), dq

        zero = jnp.zeros((SB_T, 1), F32)
        init = ((zero, zero), jnp.zeros((SB_T, LANES), F32))
        odd = i % 2
        state = lax.cond(odd == 1, lambda: blocks([i, i - 1], (True, False), *init), lambda: blocks([i], (True,), *init))
        rest = i - 1 - odd
        state = lax.fori_loop(0, i // 2, lambda jj, c: blocks([rest - 2 * jj, rest - 1 - 2 * jj], (False, False), c[0], c[1]), state)
        dq_ref[...] = (state[1] * SB_SCALE).astype(BF16)

        @pl.when(i == nq - 1)
        def _():
            dk_ref[...] = dk_acc[...].astype(BF16)
            dv_ref[...] = dv_acc[...].astype(BF16)

        if n_parts:
            pl.when(step == n_steps - 1)(finish)

    q_spec, k_spec, v_spec = _sb_specs(n_pairs, 0)
    blk = pl.BlockSpec((None, SB_T, LANES), lambda b, p, i: (b, i, p))
    seq = pl.BlockSpec((None, SEQ, LANES), lambda b, p, i: (b, 0, p))
    shape = jax.ShapeDtypeStruct((bl, SEQ, SB_W), BF16)
    out = pl.pallas_call(
        body,
        grid=(bl, n_pairs, nq),
        in_specs=[q_spec, k_spec, v_spec, blk, blk, _sb_weight_spec(nq)] + [ANY] * n_parts,
        out_specs=[blk, seq, seq] + [ANY] * n_parts,
        out_shape=[shape, shape, shape] + [jax.ShapeDtypeStruct(p.shape, p.dtype) for p in parts],
        scratch_shapes=[pltpu.VMEM((SEQ, LANES), F32), pltpu.VMEM((SEQ, LANES), F32)]
        + (_chip_exchange_sems(n_parts) if n_parts else []),
        compiler_params=pltpu.CompilerParams(dimension_semantics=("arbitrary", "arbitrary", "arbitrary")),
        name="sb_bwd",
    )(proj3, proj3, proj3, o_a, do_a, w_all, *parts)
    return out[0], out[1], out[2], out[3:]


BAND = 128


BAND_CH = 4
BAND_HEADS = DIL_W // HEAD_DIM


def _swap_half(x):
    n = x.shape[-1]
    lane = lax.broadcasted_iota(jnp.int32, (1, n), 1)
    return jnp.where((lane % HEAD_DIM) < HEAD_DIM // 2, pltpu.roll(x, n - HEAD_DIM // 2, 1), pltpu.roll(x, HEAD_DIM // 2, 1))


def _rope(x, cos, sin_signed):
    x = x.astype(F32)
    return x * cos + _swap_half(x) * sin_signed


def _band_valid(g, blk):
    nb = jnp.where(g == 0, 16, jnp.where(g == 1, 4, 1))
    first_key = jnp.where(lax.rem(blk, nb) != 0, 0, BAND)
    qi = lax.broadcasted_iota(jnp.int32, (BAND, 2 * BAND), 0) + BAND
    kj = lax.broadcasted_iota(jnp.int32, (BAND, 2 * BAND), 1)
    dist = qi - kj
    return (dist >= 0) & (dist <= BAND) & (kj >= first_key)


def _band_specs():
    last_before = lambda i: jnp.maximum(i * BAND_CH - 1, 0)
    cur = lambda col: pl.BlockSpec((None, BAND_CH, BAND, DIL_W), lambda g, i: (g, i, 0, col))
    prev = lambda col: pl.BlockSpec((None, None, BAND, DIL_W), lambda g, i: (g, last_before(i), 0, col))
    tab = pl.BlockSpec((None, BAND_CH, BAND, DIL_W), lambda g, i: (g, lax.rem(i, 16 // BAND_CH), 0, 0))
    tab_prev = pl.BlockSpec((None, None, BAND, DIL_W), lambda g, i: (g, lax.rem(last_before(i), 16), 0, 0))
    return cur, prev, tab, tab_prev


def _band_load(q_ref, k_ref, kp_ref, v_ref, vp_ref, c_ref, s_ref, cp_ref, sp_ref):
    qs = [(_rope(q_ref[b], c_ref[b], s_ref[b]) * SB_SCALE).astype(BF16) for b in range(BAND_CH)]
    ks = [_rope(kp_ref[...], cp_ref[...], sp_ref[...]).astype(BF16)]
    ks += [_rope(k_ref[b], c_ref[b], s_ref[b]).astype(BF16) for b in range(BAND_CH)]
    vs = [vp_ref[...]] + [v_ref[b] for b in range(BAND_CH)]
    k2 = [jnp.concatenate([ks[b], ks[b + 1]], axis=0) for b in range(BAND_CH)]
    v2 = [jnp.concatenate([vs[b], vs[b + 1]], axis=0) for b in range(BAND_CH)]
    return qs, k2, v2


def _band_fwd(qkv_s, cos_t, sin_t):
    def body(q_ref, k_ref, kp_ref, v_ref, vp_ref, c_ref, s_ref, cp_ref, sp_ref, ol_ref):
        g, i = pl.program_id(0), pl.program_id(1)
        qs, k2, v2 = _band_load(q_ref, k_ref, kp_ref, v_ref, vp_ref, c_ref, s_ref, cp_ref, sp_ref)
        lane = lax.broadcasted_iota(jnp.int32, (1, DIL_W), 1)
        for b in range(BAND_CH):
            valid = _band_valid(g, i * BAND_CH + b)
            hs = range(BAND_HEADS)
            mh = [(lane // HEAD_DIM) == h for h in hs]
            s = [jnp.where(valid, _dot_nt(jnp.where(mh[h], qs[b], jnp.zeros_like(qs[b])), k2[b]), NEG_INF) for h in hs]
            m = [jnp.max(s[h], axis=1, keepdims=True) for h in hs]
            p = [jnp.exp(s[h] - m[h]) for h in hs]
            den = [jnp.sum(p[h], axis=1, keepdims=True) for h in hs]
            pv = [_dot(p[h].astype(BF16), jnp.where(mh[h], v2[b], jnp.zeros_like(v2[b]))) for h in hs]
            o = jnp.zeros((BAND, DIL_W), F32)
            lse = jnp.zeros((BAND, DIL_W), F32)
            for h in hs:
                o = o + pv[h] * (1.0 / den[h])
                lse = jnp.where(mh[h], m[h] + jnp.log(den[h]), lse)
            ol_ref[b, :, :DIL_W] = o
            ol_ref[b, :, DIL_W:] = lse

    cur, prev, tab, tab_prev = _band_specs()
    n_blk = qkv_s.shape[1]
    return pl.pallas_call(
        body,
        grid=(3, n_blk // BAND_CH),
        in_specs=[cur(0), cur(1), prev(1), cur(2), prev(2), tab, tab, tab_prev, tab_prev],
        out_specs=pl.BlockSpec((None, BAND_CH, BAND, 2 * DIL_W), lambda g, i: (g, i, 0, 0)),
        out_shape=jax.ShapeDtypeStruct((3, n_blk, BAND, 2 * DIL_W), F32),
        compiler_params=pltpu.CompilerParams(dimension_semantics=("parallel", "parallel")),
        name="band_fwd",
    )(qkv_s, qkv_s, qkv_s, qkv_s, qkv_s, cos_t, sin_t, cos_t, sin_t)


def _band_bwd(qkv_s, cos_t, sin_t, dcat_s):
    def body(q_ref, k_ref, kp_ref, v_ref, vp_ref, c_ref, s_ref, cp_ref, sp_ref, do_ref, lse_ref, dl_ref,
             dq_ref, dk_ref, dv_ref, dkf_ref, dvf_ref):
        g, i = pl.program_id(0), pl.program_id(1)
        qs, k2, v2 = _band_load(q_ref, k_ref, kp_ref, v_ref, vp_ref, c_ref, s_ref, cp_ref, sp_ref)
        lane = lax.broadcasted_iota(jnp.int32, (1, DIL_W), 1)
        dks, dvs = [], []
        for b in range(BAND_CH):
            valid = _band_valid(g, i * BAND_CH + b)
            do, lse, dl = do_ref[b].astype(BF16), lse_ref[b], dl_ref[b]
            hs = range(BAND_HEADS)
            mh = [(lane // HEAD_DIM) == h for h in hs]
            qh = [jnp.where(mh[h], qs[b], jnp.zeros_like(qs[b])) for h in hs]
            doh = [jnp.where(mh[h], do, jnp.zeros_like(do)) for h in hs]
            s = [_dot_nt(qh[h], k2[b]) for h in hs]
            dp = [_dot_nt(doh[h], v2[b]) for h in hs]
            p = [jnp.where(valid, jnp.exp(s[h] - lse[:, h * HEAD_DIM:h * HEAD_DIM + 1]), 0.0) for h in hs]
            ds = [(p[h] * (dp[h] - dl[:, h * HEAD_DIM:h * HEAD_DIM + 1])).astype(BF16) for h in hs]
            pb = [p[h].astype(BF16) for h in hs]
            dq = sum(_dot(ds[h], jnp.where(mh[h], k2[b], jnp.zeros_like(k2[b]))) for h in hs)
            dk2 = sum(_dot_tn(ds[h], qh[h]) for h in hs)
            dv2 = sum(_dot_tn(pb[h], doh[h]) for h in hs)
            dq_ref[b] = dq * SB_SCALE
            dks.append(dk2)
            dvs.append(dv2)
        dkf_ref[...] = dks[0][:BAND]
        dvf_ref[...] = dvs[0][:BAND]
        for b in range(BAND_CH):
            last = b == BAND_CH - 1
            dk_ref[b] = dks[b][BAND:] if last else dks[b][BAND:] + dks[b + 1][:BAND]
            dv_ref[b] = dvs[b][BAND:] if last else dvs[b][BAND:] + dvs[b + 1][:BAND]

    cur, prev, tab, tab_prev = _band_specs()
    first = pl.BlockSpec((None, None, BAND, DIL_W), lambda g, i: (g, i, 0, 0))
    n_blk = qkv_s.shape[1]
    n_chunks = n_blk // BAND_CH
    shape = jax.ShapeDtypeStruct((3, n_blk, BAND, DIL_W), F32)
    shape_first = jax.ShapeDtypeStruct((3, n_chunks, BAND, DIL_W), F32)
    return pl.pallas_call(
        body,
        grid=(3, n_chunks),
        in_specs=[cur(0), cur(1), prev(1), cur(2), prev(2), tab, tab, tab_prev, tab_prev, cur(0), cur(1), cur(2)],
        out_specs=[cur(0), cur(0), cur(0), first, first],
        out_shape=[shape, shape, shape, shape_first, shape_first],
        compiler_params=pltpu.CompilerParams(dimension_semantics=("parallel", "parallel")),
        name="band_bwd",
    )(qkv_s, qkv_s, qkv_s, qkv_s, qkv_s, cos_t, sin_t, cos_t, sin_t, dcat_s, dcat_s, dcat_s)


def _band_combine(dq, dk, dv, dk_first, dv_first, cos_t, sin_t):
    n_chunks = dk_first.shape[1]

    def body(dq_ref, dk_ref, dkn_ref, dv_ref, dvn_ref, c_ref, s_ref, out_ref):
        nxt = (pl.program_id(1) < n_chunks - 1).astype(F32)
        for b in range(BAND_CH):
            cos, sin = c_ref[b], s_ref[b]
            dq_b, dk_b, dv_b = dq_ref[b], dk_ref[b], dv_ref[b]
            if b == BAND_CH - 1:
                dk_b = dk_b + nxt * dkn_ref[...]
                dv_b = dv_b + nxt * dvn_ref[...]
            out_ref[b, :, :DIL_W] = (dq_b * cos - _swap_half(dq_b) * sin).astype(BF16)
            out_ref[b, :, DIL_W:2 * DIL_W] = (dk_b * cos - _swap_half(dk_b) * sin).astype(BF16)
            out_ref[b, :, 2 * DIL_W:] = dv_b.astype(BF16)

    cur, _, tab, _ = _band_specs()
    nxt = pl.BlockSpec((None, None, BAND, DIL_W), lambda g, i: (g, jnp.minimum(i + 1, n_chunks - 1), 0, 0))
    return pl.pallas_call(
        body,
        grid=(3, n_chunks),
        in_specs=[cur(0), cur(0), nxt, cur(0), nxt, tab, tab],
        out_specs=pl.BlockSpec((None, BAND_CH, BAND, 3 * DIL_W), lambda g, i: (g, i, 0, 0)),
        out_shape=jax.ShapeDtypeStruct(dq.shape[:3] + (3 * DIL_W,), BF16),
        compiler_params=pltpu.CompilerParams(dimension_semantics=("parallel", "parallel")),
        name="band_combine",
    )(dq, dk, dk_first, dv, dv_first, cos_t, sin_t)


def _band_merge(ol):
    t, tm = ol.shape[1], 512

    def body(o_ref, l_ref, ob_ref, lse_ref):
        l0, l1, l2 = l_ref[0], l_ref[1], l_ref[2]
        m = jnp.maximum(jnp.maximum(l0, l1), l2)
        lse = m + jnp.log(jnp.exp(l0 - m) + jnp.exp(l1 - m) + jnp.exp(l2 - m))
        ob_ref[...] = (jnp.exp(l0 - lse) * o_ref[0] + jnp.exp(l1 - lse) * o_ref[1] + jnp.exp(l2 - lse) * o_ref[2]).astype(BF16)
        lse_ref[...] = lse

    spec = pl.BlockSpec((tm, DIL_W), lambda i: (i, 0))
    return pl.pallas_call(
        body,
        grid=(t // tm,),
        in_specs=[pl.BlockSpec((3, tm, DIL_W), lambda i: (0, i, 0)), pl.BlockSpec((3, tm, DIL_W), lambda i: (0, i, 1))],
        out_specs=[spec, spec],
        out_shape=[jax.ShapeDtypeStruct((t, DIL_W), BF16), jax.ShapeDtypeStruct((t, DIL_W), F32)],
        compiler_params=pltpu.CompilerParams(dimension_semantics=("parallel",)),
        name="band_merge",
    )(ol, ol)


def _band_delta(do_b, o_b, lse_b):
    def fn(do, o, lse):
        r = lax.broadcasted_iota(jnp.int32, (DIL_W, DIL_W), 0) // HEAD_DIM
        c = lax.broadcasted_iota(jnp.int32, (DIL_W, DIL_W), 1) // HEAD_DIM
        do = do.astype(F32)
        delta = _split_dot(do * o.astype(F32), (r == c).astype(BF16))
        return (jnp.concatenate([do, lse, delta], axis=1),)

    return _rowwise(fn, [(do_b, "row"), (o_b, "row"), (lse_b, "row")], [(3 * DIL_W, F32, "row")], tm=512, name="band_delta")[0]


def _band_masks():
    qi = lax.broadcasted_iota(jnp.int32, (BAND, 2 * BAND), 0) + BAND
    kj = lax.broadcasted_iota(jnp.int32, (BAND, 2 * BAND), 1)
    dist = qi - kj
    row = lax.broadcasted_iota(jnp.int32, (BAND, BAND), 0)
    col = lax.broadcasted_iota(jnp.int32, (BAND, BAND), 1)
    return col <= row, (dist >= 0) & (dist <= BAND)


def _band_attend(q, k, v, valid):
    lane = lax.broadcasted_iota(jnp.int32, (1, DIL_W), 1)
    hs = range(BAND_HEADS)
    mh = [(lane // HEAD_DIM) == h for h in hs]
    s = [jnp.where(valid, _dot_nt(jnp.where(mh[h], q, jnp.zeros_like(q)), k), NEG_INF) for h in hs]
    m = [jnp.max(s[h], axis=1, keepdims=True) for h in hs]
    p = [jnp.exp(s[h] - m[h]) for h in hs]
    den = [jnp.sum(p[h], axis=1, keepdims=True) for h in hs]
    pv = [_dot(p[h].astype(BF16), jnp.where(mh[h], v, jnp.zeros_like(v))) for h in hs]
    o = jnp.zeros((BAND, DIL_W), F32)
    lse = jnp.zeros((BAND, DIL_W), F32)
    for h in hs:
        o = o + pv[h] * (1.0 / den[h])
        lse = jnp.where(mh[h], m[h] + jnp.log(den[h]), lse)
    return o, lse


def _band_attend_bwd(q, k, v, valid, do, lse, dl):
    lane = lax.broadcasted_iota(jnp.int32, (1, DIL_W), 1)
    hs = range(BAND_HEADS)
    mh = [(lane // HEAD_DIM) == h for h in hs]
    qh = [jnp.where(mh[h], q, jnp.zeros_like(q)) for h in hs]
    doh = [jnp.where(mh[h], do, jnp.zeros_like(do)) for h in hs]
    s = [_dot_nt(qh[h], k) for h in hs]
    dp = [_dot_nt(doh[h], v) for h in hs]
    p = [jnp.where(valid, jnp.exp(s[h] - lse[:, h * HEAD_DIM:h * HEAD_DIM + 1]), 0.0) for h in hs]
    ds = [(p[h] * (dp[h] - dl[:, h * HEAD_DIM:h * HEAD_DIM + 1])).astype(BF16) for h in hs]
    pb = [p[h].astype(BF16) for h in hs]
    dq = sum(_dot(ds[h], jnp.where(mh[h], k, jnp.zeros_like(k))) for h in hs)
    dk = sum(_dot_tn(ds[h], qh[h]) for h in hs)
    dv = sum(_dot_tn(pb[h], doh[h]) for h in hs)
    return dq, dk, dv


def _band_group_specs(lead, rows, cls, col0):
    def spec(width):
        if lead == "rows":
            return pl.BlockSpec((None, rows, width), lambda b, i: (b, 0, col0))
        return pl.BlockSpec((None, rows, cls * width), lambda b, i: (b, 0, i))
    return spec


def _band_group_fwd(a, cos_g, sin_g, *, rows, cls, steps, col0, name):
    bl = a.shape[0]
    nb = rows // BAND
    grp_w = 3 * DIL_W

    def body(a_ref, c_ref, s_ref, ol_ref, qr, kr):
        first_valid, later_valid = _band_masks()
        for j in range(cls):
            a0, t0, o0 = j * grp_w, j * DIL_W, j * 2 * DIL_W
            cos, sin = c_ref[:, t0:t0 + DIL_W], s_ref[:, t0:t0 + DIL_W]
            qr[...] = (_rope(a_ref[:, a0:a0 + DIL_W], cos, sin) * SB_SCALE).astype(BF16)
            kr[...] = _rope(a_ref[:, a0 + DIL_W:a0 + 2 * DIL_W], cos, sin).astype(BF16)

            def block(q0, k0, keys, valid, a0=a0, o0=o0):
                o, lse = _band_attend(qr[pl.ds(q0, BAND), :], kr[pl.ds(k0, keys), :],
                                      a_ref[pl.ds(k0, keys), a0 + 2 * DIL_W:a0 + grp_w], valid)
                ol_ref[pl.ds(q0, BAND), o0:o0 + DIL_W] = o
                ol_ref[pl.ds(q0, BAND), o0 + DIL_W:o0 + 2 * DIL_W] = lse

            block(0, 0, BAND, first_valid)
            if nb > 1:
                def later(b, carry, block=block):
                    block(pl.multiple_of(b * BAND, BAND), pl.multiple_of((b - 1) * BAND, BAND), 2 * BAND, later_valid)
                    return carry

                lax.fori_loop(1, nb, later, 0)

    lead = "rows" if col0 is not None else "cols"
    spec = _band_group_specs(lead, rows, cls, col0)
    tab = pl.BlockSpec((rows, cls * DIL_W), lambda b, i: (0, i))
    n_cls = cos_g.shape[1] // DIL_W
    return pl.pallas_call(
        body,
        grid=(bl, steps),
        in_specs=[spec(grp_w), tab, tab],
        out_specs=pl.BlockSpec((None, rows, cls * 2 * DIL_W), lambda b, i: (b, 0, i)),
        out_shape=jax.ShapeDtypeStruct((bl, rows, n_cls * 2 * DIL_W), F32),
        scratch_shapes=[pltpu.VMEM((rows, DIL_W), BF16), pltpu.VMEM((rows, DIL_W), BF16)],
        compiler_params=pltpu.CompilerParams(dimension_semantics=("parallel", "parallel")),
        name=name,
    )(a, cos_g, sin_g)


def _band_group_bwd(a, d, cos_g, sin_g, *, rows, cls, steps, col0, name, side=None):
    bl = a.shape[0]
    nb = rows // BAND
    grp_w = 3 * DIL_W
    side_arrays = side[1] if side else []
    n_side = len(side_arrays)

    def body(a_ref, d_ref, c_ref, s_ref, *rest):
        out_ref = rest[n_side]
        qr, kr, dk_acc, dv_acc = rest[2 * n_side + 1:2 * n_side + 5]
        if n_side:
            step = pl.program_id(0) * steps + pl.program_id(1)
            finish = _run_side(side, rest[:n_side], rest[n_side + 1:2 * n_side + 1], rest[2 * n_side + 5:], step, bl * steps)
        first_valid, later_valid = _band_masks()
        for j in range(cls):
            a0, t0 = j * grp_w, j * DIL_W
            cos, sin = c_ref[:, t0:t0 + DIL_W], s_ref[:, t0:t0 + DIL_W]
            qr[...] = (_rope(a_ref[:, a0:a0 + DIL_W], cos, sin) * SB_SCALE).astype(BF16)
            kr[...] = _rope(a_ref[:, a0 + DIL_W:a0 + 2 * DIL_W], cos, sin).astype(BF16)
            dk_acc[...] = jnp.zeros_like(dk_acc)
            dv_acc[...] = jnp.zeros_like(dv_acc)

            def block(q0, k0, keys, valid, a0=a0, t0=t0):
                qrows, krows = pl.ds(q0, BAND), pl.ds(k0, keys)
                dq, dk, dv = _band_attend_bwd(
                    qr[qrows, :], kr[krows, :], a_ref[krows, a0 + 2 * DIL_W:a0 + grp_w], valid,
                    d_ref[qrows, a0:a0 + DIL_W].astype(BF16), d_ref[qrows, a0 + DIL_W:a0 + 2 * DIL_W],
                    d_ref[qrows, a0 + 2 * DIL_W:a0 + grp_w])
                dq = dq * SB_SCALE
                out_ref[qrows, a0:a0 + DIL_W] = (dq * c_ref[qrows, t0:t0 + DIL_W]
                                                 - _swap_half(dq) * s_ref[qrows, t0:t0 + DIL_W]).astype(BF16)
                dk_acc[krows, :] += dk
                dv_acc[krows, :] += dv

            block(0, 0, BAND, first_valid)
            if nb > 1:
                def later(b, carry, block=block):
                    block(pl.multiple_of(b * BAND, BAND), pl.multiple_of((b - 1) * BAND, BAND), 2 * BAND, later_valid)
                    return carry

                lax.fori_loop(1, nb, later, 0)
            dk = dk_acc[...]
            out_ref[:, a0 + DIL_W:a0 + 2 * DIL_W] = (dk * cos - _swap_half(dk) * sin).astype(BF16)
            out_ref[:, a0 + 2 * DIL_W:a0 + grp_w] = dv_acc[...].astype(BF16)
        if n_side:
            finish()

    lead = "rows" if col0 is not None else "cols"
    spec = _band_group_specs(lead, rows, cls, col0)
    dspec = _band_group_specs(lead, rows, cls, 0 if col0 is not None else None)
    tab = pl.BlockSpec((rows, cls * DIL_W), lambda b, i: (0, i))
    n_cls = cos_g.shape[1] // DIL_W
    out = pl.pallas_call(
        body,
        grid=(bl, steps),
        in_specs=[spec(grp_w), dspec(grp_w), tab, tab] + [ANY] * n_side,
        out_specs=[pl.BlockSpec((None, rows, cls * grp_w), lambda b, i: (b, 0, i))] + [ANY] * n_side,
        out_shape=[jax.ShapeDtypeStruct((bl, rows, n_cls * grp_w), BF16)] + (_side_out_shapes(side) if n_side else []),
        scratch_shapes=[pltpu.VMEM((rows, DIL_W), BF16), pltpu.VMEM((rows, DIL_W), BF16),
                        pltpu.VMEM((rows, DIL_W), F32), pltpu.VMEM((rows, DIL_W), F32)] + (_side_sems(side) if n_side else []),
        compiler_params=pltpu.CompilerParams(dimension_semantics=("arbitrary", "arbitrary") if n_side else ("parallel", "parallel")),
        name=name,
    )(a, d, cos_g, sin_g, *side_arrays)
    return (out[0], out[1:]) if n_side else out[0]


def _band_merge3(ols):
    t, tm = ols[0].shape[0], 512

    def body(o0, l0, o1, l1, o2, l2, ob_ref, lse_ref):
        a, b, c = l0[...], l1[...], l2[...]
        m = jnp.maximum(jnp.maximum(a, b), c)
        lse = m + jnp.log(jnp.exp(a - m) + jnp.exp(b - m) + jnp.exp(c - m))
        ob_ref[...] = (jnp.exp(a - lse) * o0[...] + jnp.exp(b - lse) * o1[...] + jnp.exp(c - lse) * o2[...]).astype(BF16)
        lse_ref[...] = lse

    spec = pl.BlockSpec((tm, DIL_W), lambda i: (i, 0))
    spec_l = pl.BlockSpec((tm, DIL_W), lambda i: (i, 1))
    return pl.pallas_call(
        body,
        grid=(t // tm,),
        in_specs=[spec, spec_l] * 3,
        out_specs=[spec, spec],
        out_shape=[jax.ShapeDtypeStruct((t, DIL_W), BF16), jax.ShapeDtypeStruct((t, DIL_W), F32)],
        compiler_params=pltpu.CompilerParams(dimension_semantics=("parallel",)),
        name="band_merge",
    )(ols[0], ols[0], ols[1], ols[1], ols[2], ols[2])


MEM_T = 512
MEM_SCALE = 128 ** -0.5
MEM_Q_COL = (D_IN - MEM_W) // LANES


def _mem_specs():
    q = pl.BlockSpec((None, MEM_T, LANES), lambda b, h, i: (b, i, MEM_Q_COL + h))
    k = pl.BlockSpec((None, MEM_LEN, LANES), lambda b, h, i: (b, 0, h))
    v = pl.BlockSpec((None, MEM_LEN, LANES), lambda b, h, i: (b, 0, MEM_W // LANES + h))
    blk = pl.BlockSpec((None, MEM_T, LANES), lambda b, h, i: (b, i, h))
    return q, k, v, blk


def _mem_probs(q, k):
    s = _dot_nt(q, k) * MEM_SCALE
    p = jnp.exp(s - jnp.max(s, axis=1, keepdims=True))
    return p * (1.0 / jnp.sum(p, axis=1, keepdims=True))


def _mem_fwd(proj3, kv3):
    bl = proj3.shape[0]

    def body(q_ref, k_ref, v_ref, o_ref):
        p = _mem_probs(q_ref[...], k_ref[...])
        o_ref[...] = _dot(p.astype(BF16), v_ref[...]).astype(BF16)

    q, k, v, blk = _mem_specs()
    return pl.pallas_call(
        body,
        grid=(bl, MEM_W // LANES, SEQ // MEM_T),
        in_specs=[q, k, v],
        out_specs=blk,
        out_shape=jax.ShapeDtypeStruct((bl, SEQ, MEM_W), BF16),
        compiler_params=pltpu.CompilerParams(dimension_semantics=("parallel", "parallel", "parallel")),
        name="mem_fwd",
    )(proj3, kv3, kv3)


def _mem_bwd(proj3, kv3, do_c):
    bl = proj3.shape[0]

    def body(q_ref, k_ref, v_ref, do_ref, dq_ref, dk_ref, dv_ref):
        @pl.when(pl.program_id(2) == 0)
        def _():
            dk_ref[...] = jnp.zeros_like(dk_ref)
            dv_ref[...] = jnp.zeros_like(dv_ref)

        q, k, do = q_ref[...], k_ref[...], do_ref[...]
        p = _mem_probs(q, k)
        dp = _dot_nt(do, v_ref[...])
        ds = (p * (dp - jnp.sum(p * dp, axis=1, keepdims=True)) * MEM_SCALE).astype(BF16)
        dq_ref[...] = _dot(ds, k).astype(BF16)
        dk_ref[...] += _dot_tn(ds, q)
        dv_ref[...] += _dot_tn(p.astype(BF16), do)

    q, k, v, blk = _mem_specs()
    kv_out = pl.BlockSpec((None, MEM_LEN, LANES), lambda b, h, i: (b, 0, h))
    return pl.pallas_call(
        body,
        grid=(bl, MEM_W // LANES, SEQ // MEM_T),
        in_specs=[q, k, v, blk],
        out_specs=[blk, kv_out, kv_out],
        out_shape=[jax.ShapeDtypeStruct((bl, SEQ, MEM_W), BF16), jax.ShapeDtypeStruct((bl, MEM_LEN, MEM_W), F32),
                   jax.ShapeDtypeStruct((bl, MEM_LEN, MEM_W), F32)],
        compiler_params=pltpu.CompilerParams(dimension_semantics=("parallel", "parallel", "arbitrary")),
        name="mem_bwd",
    )(proj3, kv3, kv3, do_c)


def _place():
    x, y, c = lax.axis_index("x"), lax.axis_index("y"), lax.axis_index("c")
    return x, y, c


def _other_chips(x, y):
    return [(1 - x, y), (x, 1 - y), (1 - x, 1 - y)]


def _remote(src, dst, send_sem, recv_sem, to):
    return pltpu.make_async_remote_copy(src_ref=src, dst_ref=dst, send_sem=send_sem, recv_sem=recv_sem,
                                        device_id=to, device_id_type=MESH)


ANY = pl.BlockSpec(memory_space=pl.ANY)


def _gather_weights(shards):
    n = len(shards)

    def body(*refs):
        send, forward, finish = _gather_phases(refs[:n], refs[n:2 * n], *refs[2 * n:])
        send()
        forward()
        finish()

    return pl.pallas_call(
        body,
        in_specs=[ANY] * n,
        out_specs=[ANY] * n,
        out_shape=_gather_out_shapes(shards),
        scratch_shapes=_gather_sems(n),
        name="gather_weights",
    )(*shards)


def _gather_out_shapes(shards):
    return [jax.ShapeDtypeStruct((N_CHIPS,) + s.shape, s.dtype) for s in shards]


def _gather_sems(n):
    return [pltpu.SemaphoreType.DMA((6 * n,)), pltpu.SemaphoreType.DMA((6 * n,))]


def _gather_phases(in_refs, out_refs, send_sems, recv_sems):
    x, y, c = _place()
    sibling = (x, y, 1 - c)
    chips = _other_chips(x, y)
    first, passed = [], []
    for k in range(len(in_refs)):
        hf = in_refs[k].shape[0] // 2

        def half(px, py, pc, k=k, hf=hf):
            return out_refs[k].at[2 * px + py, pl.ds(pc * hf, hf), :]

        src = in_refs[k].at[pl.ds(c * hf, hf), :]
        for j, chip in enumerate(chips):
            s = 6 * k + j
            first.append(_remote(src, half(x, y, c), send_sems.at[s], recv_sems.at[s], (*chip, c)))
            passed.append((_remote(src, half(*chip, c), send_sems.at[s], recv_sems.at[s], (*chip, c)),
                           _remote(half(*chip, c), half(*chip, c), send_sems.at[s + 3], recv_sems.at[s + 3], sibling),
                           _remote(src, half(*chip, 1 - c), send_sems.at[s + 3], recv_sems.at[s + 3], sibling)))

    def send():
        for cp in first:
            cp.start()

    def forward():
        for landed, fwd, _ in passed:
            landed.wait_recv()
            fwd.start()

    def finish():
        for _, _, from_sibling in passed:
            from_sibling.wait_recv()
        for cp in first:
            cp.wait_send()
        for _, fwd, _ in passed:
            fwd.wait_send()

    return send, forward, finish


def _pair_exchange(grads, *, name):
    n = len(grads)
    side = ("pair", grads)

    def body(*refs):
        send, _, finish = _side_phases(side, refs[:n], refs[n:2 * n], refs[2 * n:])
        send()
        finish()

    return pl.pallas_call(
        body,
        in_specs=[ANY] * n,
        out_specs=[ANY] * n,
        out_shape=_side_out_shapes(side),
        scratch_shapes=_side_sems(side),
        name=name,
    )(*grads)


def _pair_exchange_phases(g_refs, land_refs, send_sems, recv_sems):
    x, y, c = _place()
    cps = []
    for k in range(len(g_refs)):
        hf = g_refs[k].shape[1] // 2
        src = g_refs[k].at[:, pl.ds((1 - c) * hf, hf), :]
        cps.append(_remote(src, land_refs[k], send_sems.at[k], recv_sems.at[k], (x, y, 1 - c)))

    def send():
        for cp in cps:
            cp.start()

    def finish():
        for cp in cps:
            cp.wait()

    return send, finish


def _side_out_shapes(side):
    kind, arrays = side
    if kind == "gather":
        return _gather_out_shapes(arrays)
    if kind == "pair":
        return [jax.ShapeDtypeStruct((N_CHIPS, g.shape[1] // 2, g.shape[2]), g.dtype) for g in arrays]
    return [jax.ShapeDtypeStruct(p.shape, p.dtype) for p in arrays]


def _side_sems(side):
    kind, arrays = side
    n = len(arrays)
    if kind == "gather":
        return _gather_sems(n)
    if kind == "pair":
        return [pltpu.SemaphoreType.DMA((n,)), pltpu.SemaphoreType.DMA((n,))]
    return _chip_exchange_sems(n)


def _side_phases(side, in_refs, out_refs, sems):
    kind = side[0]
    if kind == "gather":
        return _gather_phases(in_refs, out_refs, *sems)
    send, finish = (_pair_exchange_phases if kind == "pair" else _chip_exchange_phases)(in_refs, out_refs, *sems)
    return send, None, finish


def _run_side(side, in_refs, out_refs, sems, step, n_steps):
    first, mid, last = _side_phases(side, in_refs, out_refs, sems)
    pl.when(step == 0)(first)
    if mid is not None:
        pl.when(step == n_steps // 2)(mid)
    return lambda: pl.when(step == n_steps - 1)(last)


def _pair_add(g, land, c_arr, *, name):
    _, a, b = g.shape
    hf = a // 2

    def body(c_ref, g_ref, l_ref, o_ref):
        o_ref[...] = (g_ref[...] + l_ref[...]).astype(BF16)

    return pl.pallas_call(
        body,
        grid_spec=pltpu.PrefetchScalarGridSpec(
            num_scalar_prefetch=1,
            grid=(N_CHIPS,),
            in_specs=[pl.BlockSpec((None, None, hf, b), lambda s, c_ref: (s, c_ref[0], 0, 0)),
                      pl.BlockSpec((None, hf, b), lambda s, c_ref: (s, 0, 0))],
            out_specs=pl.BlockSpec((None, hf, b), lambda s, c_ref: (s, 0, 0)),
        ),
        out_shape=jax.ShapeDtypeStruct((N_CHIPS, hf, b), BF16),
        compiler_params=pltpu.CompilerParams(dimension_semantics=("parallel",)),
        name=name,
    )(c_arr, g.reshape(N_CHIPS, 2, hf, b), land)


def _chip_exchange(parts):
    n = len(parts)

    def body(*refs):
        send, finish = _chip_exchange_phases(refs[:n], refs[n:2 * n], *refs[2 * n:])
        send()
        finish()

    return pl.pallas_call(
        body,
        in_specs=[ANY] * n,
        out_specs=[ANY] * n,
        out_shape=[jax.ShapeDtypeStruct(p.shape, p.dtype) for p in parts],
        scratch_shapes=_chip_exchange_sems(n),
        name="chip_exchange",
    )(*parts)


def _chip_exchange_sems(n):
    return [pltpu.SemaphoreType.DMA((3 * n,)), pltpu.SemaphoreType.DMA((3 * n,))]


def _chip_exchange_phases(p_refs, land_refs, send_sems, recv_sems):
    x, y, c = _place()
    me = 2 * x + y
    sends, recvs = [], []
    for k in range(len(p_refs)):
        for j, (cx, cy) in enumerate(_other_chips(x, y)):
            s = 3 * k + j
            sends.append(_remote(p_refs[k].at[2 * cx + cy], land_refs[k].at[me], send_sems.at[s], recv_sems.at[s], (cx, cy, c)))
            recvs.append(_remote(p_refs[k].at[me], land_refs[k].at[2 * cx + cy], send_sems.at[s], recv_sems.at[s], (cx, cy, c)))

    def send():
        for cp in sends:
            cp.start()

    def finish():
        for cp in recvs:
            cp.wait_recv()
        for cp in sends:
            cp.wait_send()

    return send, finish


def _chip_add(land, part, me_arr, *, name):
    _, r, b = land.shape

    def body(me_ref, p_ref, l1_ref, l2_ref, l3_ref, o_ref):
        o_ref[...] = ((p_ref[...].astype(F32) + l1_ref[...].astype(F32)) + l2_ref[...].astype(F32)) + l3_ref[...].astype(F32)

    tr = r // 2
    other = lambda j: pl.BlockSpec((None, tr, b), lambda i, me_ref: (jnp.bitwise_xor(me_ref[0], j), i, 0))
    return pl.pallas_call(
        body,
        grid_spec=pltpu.PrefetchScalarGridSpec(
            num_scalar_prefetch=1,
            grid=(r // tr,),
            in_specs=[pl.BlockSpec((None, tr, b), lambda i, me_ref: (me_ref[0], i, 0)), other(2), other(1), other(3)],
            out_specs=pl.BlockSpec((tr, b), lambda i, me_ref: (i, 0)),
        ),
        out_shape=jax.ShapeDtypeStruct((r, b), F32),
        compiler_params=pltpu.CompilerParams(dimension_semantics=("parallel",)),
        name=name,
    )(me_arr, part, land, land, land)


def _pair_share(halves):
    n = len(halves)

    def body(*refs):
        h_refs, out_refs = refs[:n], refs[n:2 * n]
        send_sems, recv_sems = refs[2 * n:]
        x, y, c = _place()
        cps = [_remote(h_refs[k], out_refs[k], send_sems.at[k], recv_sems.at[k], (x, y, 1 - c)) for k in range(n)]
        for cp in cps:
            cp.start()
        for cp in cps:
            cp.wait()

    return pl.pallas_call(
        body,
        in_specs=[ANY] * n,
        out_specs=[ANY] * n,
        out_shape=[jax.ShapeDtypeStruct(h.shape, F32) for h in halves],
        scratch_shapes=[pltpu.SemaphoreType.DMA((n,)), pltpu.SemaphoreType.DMA((n,))],
        name="pair_share",
    )(*halves)


def _all_sum_small(part):
    def body(p_ref, o_ref, slots, send_sems, recv_sems):
        x, y, c = _place()
        me = 4 * x + 2 * y + c
        slots[me] = p_ref[...]
        peers = [(x ^ dx, y ^ dy, c ^ dc) for dx in (0, 1) for dy in (0, 1) for dc in (0, 1)][1:]
        sends = [_remote(p_ref, slots.at[me], send_sems.at[k], recv_sems.at[k], peer) for k, peer in enumerate(peers)]
        for cp in sends:
            cp.start()
        for k, (px, py, pc) in enumerate(peers):
            _remote(p_ref, slots.at[4 * px + 2 * py + pc], send_sems.at[k], recv_sems.at[k], (px, py, pc)).wait_recv()
        for cp in sends:
            cp.wait_send()
        acc = slots[0]
        for d in range(1, 8):
            acc = acc + slots[d]
        o_ref[...] = acc

    vmem = pl.BlockSpec(memory_space=pltpu.VMEM)
    return pl.pallas_call(
        body,
        in_specs=[vmem],
        out_specs=vmem,
        out_shape=jax.ShapeDtypeStruct(part.shape, F32),
        scratch_shapes=[pltpu.VMEM((8,) + part.shape, F32), pltpu.SemaphoreType.DMA((7,)), pltpu.SemaphoreType.DMA((7,))],
        name="all_sum_small",
    )(part)


def _deinterleave(a, d):
    b, s, c = a.shape
    return a.reshape(b, s // d, d, c).transpose(0, 2, 1, 3).reshape(b * s // BAND, BAND, c)


def _reinterleave(a, d, b):
    c = a.shape[-1]
    return a.reshape(b, d, SEQ // d, c).transpose(0, 2, 1, 3).reshape(b, SEQ, c)


def _rope_tables():
    half = HEAD_DIM // 2
    inv_freq = np.float32(ROPE_THETA) ** (-np.arange(half, dtype=np.float32) * np.float32(2.0) / np.float32(HEAD_DIM))
    ang = np.arange(SEQ, dtype=np.float32)[:, None] * inv_freq[None, :].astype(np.float32)
    cos = np.tile(np.cos(ang).astype(np.float32), (1, 2 * BAND_HEADS))
    sin = np.tile(np.concatenate([-np.sin(ang), np.sin(ang)], axis=1).astype(np.float32), (1, BAND_HEADS))
    return jnp.asarray(cos), jnp.asarray(sin)


def _band_groups():
    out = []
    for d in DIL_D:
        rows = SEQ // d
        cls = max(1, 512 // rows) if d > 1 else 1
        out.append(dict(rows=rows, cls=cls, steps=d // cls))
    return out


def _local_step(x, mem, loss_target, g_pre_mix, g_post_mix, g_pre_ffn, g_post_ffn, g_mem, b_gate, w, comm=None):
    bl = x.shape[0]
    t = bl * SEQ
    chips = range(N_CHIPS)
    half_ff = D_FF // 2

    def with_gathered(w, names, gathered, shards):
        return {**w, **{name: lax.dynamic_update_slice(g, s[None], (comm["me"][0], 0, 0))
                        for name, g, s in zip(names, gathered, shards)}}

    x2 = x.reshape(t, D_MODEL)
    tgt2 = loss_target.reshape(t, D_MODEL)
    mem2 = mem.reshape(bl * MEM_LEN, D_MODEL)

    h = _norm_fwd(x2, g_pre_mix, name="norm_x", side=("gather", comm["first_shards"]) if comm else None)
    if comm:
        w = with_gathered(w, comm["first_names"], h[1], comm["first_shards"])
        h = h[0]
    w_in_full = _join_shards(w["w_in"])
    proj = _mm([(h, w_in_full)], nt=False, tn=2176, out_dtypes=[BF16], name="proj",
               side=("gather", comm["mid_shards"]) if comm else None)
    if comm:
        w = with_gathered(w, comm["mid_names"], proj[1], comm["mid_shards"])
        proj = proj[0]
    w_mem_kv_full = w["w_mem_kv"].reshape(D_MODEL, 2 * MEM_W)
    gates = _mm([(h, w["w_gate"], None, "j")], nt=False,tn=w["w_gate"].shape[2], out_dtypes=[BF16], name="gates",
                bias=b_gate, epilogue=lambda acc: (_sigmoid(acc),))
    hm = _norm_fwd(mem2, g_mem, name="norm_mem")
    kv_m = _mm([(hm, w_mem_kv_full)], nt=False,tn=1024, out_dtypes=[BF16], name="mem_kv")
    proj3 = proj.reshape(bl, SEQ, D_IN)
    kv3 = kv_m.reshape(bl, MEM_LEN, 2 * MEM_W)

    o_a, o_a32, sb_weights, late_gathered = _sb_fwd(proj3, comm["late_shards"] if comm else [])
    if comm:
        w = with_gathered(w, comm["late_names"], late_gathered, comm["late_shards"])
    w_o_full = w["w_o"].reshape(D_MODEL, D_MODEL)
    w_ffn_out_full = w["w_ffn_out"].reshape(D_FF, D_MODEL)

    cos_t, sin_t = _rope_tables()
    dil0 = 3 * SB_W

    grp_w = 3 * DIL_W
    band = []
    for g, (d, cfg) in enumerate(zip(DIL_D, _band_groups())):
        a_g = proj3 if d == 1 else proj3[:, :, dil0 + g * grp_w:dil0 + (g + 1) * grp_w].reshape(bl, SEQ // d, d * grp_w)
        band.append(dict(cfg, a=a_g, col0=dil0 // grp_w if d == 1 else None, cos=cos_t.reshape(SEQ // d, d * DIL_W),
                         sin=sin_t.reshape(SEQ // d, d * DIL_W)))
    ols = [_band_group_fwd(b["a"], b["cos"], b["sin"], rows=b["rows"], cls=b["cls"], steps=b["steps"], col0=b["col0"],
                           name=f"band_fwd_{g}").reshape(t, 2 * DIL_W) for g, b in enumerate(band)]
    o_b, lse_b = _band_merge3(ols)

    o_c = _mem_fwd(proj3, kv3)

    o_a2, o_c2 = o_a.reshape(t, SB_W), o_c.reshape(t, MEM_W)
    y_a, y_b, y_c, merged = _branch_merge_fwd(o_a2, o_b, o_c2, w["w_br_sb"], w["w_br_dil"], w["w_br_mem"], gates)
    mix = _mm([(merged, w_o_full)], nt=False,tn=1024, out_dtypes=[F32], name="mix")
    x1, h2 = _mid_fwd(mix, x2, g_post_mix, g_pre_ffn)
    gg, uu, f = _ffn_in_fwd(h2, w["w_ffn_in"])
    f2 = _mm([(f, w_ffn_out_full)], nt=False,tn=1024, out_dtypes=[F32], name="ffn_out")

    dy, df2, dg_post_ffn, loss_row = _loss_bwd(f2, x1, g_post_ffn, tgt2)

    dg_ffn, du_ffn = _mm([(df2, w_ffn_out_full)], nt=True,tn=half_ff, out_dtypes=[BF16, BF16], name="d_ffn_act",
                         extras=(gg, uu), epilogue=_swiglu_bwd_epilogue)
    gw = {}
    gw["w_ffn_out"] = _mm_tn(f, df2, tm=half_ff, tn=1024, name="gw_ffn_out").reshape(N_CHIPS, D_FF // N_CHIPS, D_MODEL)
    gw_ffn_g = _mm_tn(h2, dg_ffn, tm=1024, tn=half_ff, name="gw_ffn_gate", out_shards=True)
    gw_ffn_u = _mm_tn(h2, du_ffn, tm=1024, tn=half_ff, name="gw_ffn_up", out_shards=True)
    gw["w_ffn_in"] = jnp.concatenate([gw_ffn_g, gw_ffn_u], axis=0)
    dh2 = _mm([(dg_ffn, w["w_ffn_in"], 0, 0), (dg_ffn, w["w_ffn_in"], 1, 1), (du_ffn, w["w_ffn_in"], 0, 2),
               (du_ffn, w["w_ffn_in"], 1, 3)], nt=True,tn=1024, out_dtypes=[F32], name="d_h2")
    dx1, dmix, dg_pre_ffn, dg_post_mix = _mid_bwd(dh2, x1, mix, g_pre_ffn, g_post_mix, dy)

    gw["w_o"] = _mm_tn(merged, dmix, tm=1024, tn=1024, name="gw_o").reshape(N_CHIPS, D_MODEL // N_CHIPS, D_MODEL)
    dmerged = _mm([(dmix, w_o_full)], nt=True,tn=1024, out_dtypes=[F32], name="d_merged")
    dy_a, dy_b, dy_c, dgpre, db_gate = _gate_bwd(dmerged, gates, y_a, y_b, y_c)
    br_cols = D_MODEL // N_CHIPS
    gw["w_br_sb"] = _mm_tn(o_a2, dy_a, tm=512, tn=br_cols, name="gw_br_sb", out_shards=True)
    gw["w_br_dil"] = _mm_tn(o_b, dy_b, tm=256, tn=br_cols, name="gw_br_dil", out_shards=True)
    gw["w_br_mem"] = _mm_tn(o_c2, dy_c, tm=512, tn=br_cols, name="gw_br_mem", out_shards=True)
    gw["w_gate"] = _mm_tn(h, dgpre, tm=1024, tn=w["w_gate"].shape[2], name="gw_gate", out_shards=True)
    do_a = _mm([(dy_a, w["w_br_sb"], s, s) for s in chips], nt=True,tn=SB_W, out_dtypes=[BF16], name="d_o_a")
    do_b = _mm([(dy_b, w["w_br_dil"], s, s) for s in chips], nt=True,tn=DIL_W, out_dtypes=[BF16], name="d_o_b")
    do_c = _mm([(dy_c, w["w_br_mem"], s, s) for s in chips], nt=True,tn=MEM_W, out_dtypes=[BF16], name="d_o_c")

    dq_c, dk_m, dv_m = _mem_bwd(proj3, kv3, do_c.reshape(bl, SEQ, MEM_W))
    dkv_m = jnp.concatenate([dk_m, dv_m], axis=-1).reshape(bl * MEM_LEN, 2 * MEM_W).astype(BF16)
    gw["w_mem_kv"] = _mm_tn(hm, dkv_m, tm=1024, tn=1024, name="gw_mem_kv").reshape(N_CHIPS, D_MODEL // N_CHIPS, 2 * MEM_W)
    dhm = _mm([(dkv_m, w_mem_kv_full)], nt=True,tn=1024, out_dtypes=[F32], name="d_hm")
    dg_mem = _mem_norm_bwd(dhm, mem2, g_mem)

    dcat = _band_delta(do_b, o_b, lse_b)
    early = [name for name, _, _ in PACK if name != "w_in"] if comm else []
    grads = [gw[name] for name in early]
    d_dil = []
    for g, (d, b) in enumerate(zip(DIL_D, band)):
        out = _band_group_bwd(b["a"], dcat.reshape(bl, SEQ // d, d * grp_w), b["cos"], b["sin"], rows=b["rows"], cls=b["cls"],
                              steps=b["steps"], col0=b["col0"], name=f"band_bwd_{g}",
                              side=("pair", grads) if comm and g == 0 else None)
        if comm and g == 0:
            out, lands = out
        d_dil.append(out.reshape(bl, SEQ, grp_w))

    parts = [_pair_add(g, l, comm["c"], name="pair_add_" + name) for name, g, l in zip(early, grads, lands)] if comm else []
    dq_a, dk_a, dv_a, lands = _sb_bwd(proj3, o_a32, do_a.reshape(bl, SEQ, SB_W), sb_weights, parts)
    reduced = {name: (p, l) for name, p, l in zip(early, parts, lands)}

    in_cols = D_IN // N_CHIPS
    dproj_s = _split_to_shards([a.reshape(t, a.shape[-1]) for a in [dq_a, dk_a, dv_a] + d_dil + [dq_c]], name="dproj_shards")
    gw["w_in"] = _mm_tn(h, dproj_s, tm=1024, tn=in_cols, name="gw_in")
    if comm:
        land = _pair_exchange([gw["w_in"]], name="pair_exchange_w_in")[0]
        part_in = _pair_add(gw["w_in"], land, comm["c"], name="pair_add_w_in")
    dh = _mm([(dproj_s, w["w_in"], s, s) for s in chips] + [(dgpre, w["w_gate"], s, s) for s in chips],
             nt=True, tn=1024, out_dtypes=[F32], name="d_h", side=("chip", [part_in]) if comm else None)
    if comm:
        dh, (land_in,) = dh
        reduced["w_in"] = (part_in, land_in)
    grad_x, dg_pre_mix = _first_bwd(dh, x2, g_pre_mix, dx1)
    small = jnp.concatenate([dg_pre_mix, dg_post_mix, dg_pre_ffn, dg_post_ffn, dg_mem, db_gate.reshape(3, D_MODEL)], axis=0)
    return loss_row[0, 0], grad_x.reshape(bl, SEQ, D_MODEL), gw, small, reduced


def kernel(x, mem, g_pre_mix, g_post_mix, g_pre_ffn, g_post_ffn, g_mem, w_in, w_mem_kv, w_br_sb, w_br_dil, w_br_mem, w_gate, b_gate, w_o, w_ffn_in, w_ffn_out, loss_target, m_g_pre_mix, m_g_post_mix, m_g_pre_ffn, m_g_post_ffn, m_g_mem, m_w_in, m_w_mem_kv, m_w_br_sb, m_w_br_dil, m_w_br_mem, m_w_gate, m_b_gate, m_w_o, m_w_ffn_in, m_w_ffn_out, v_g_pre_mix, v_g_post_mix, v_g_pre_ffn, v_g_post_ffn, v_g_mem, v_w_in, v_w_mem_kv, v_w_br_sb, v_w_br_dil, v_w_br_mem, v_w_gate, v_b_gate, v_w_o, v_w_ffn_in, v_w_ffn_out):
    w_shards = dict(w_in=w_in[0], w_mem_kv=w_mem_kv[0], w_br_sb=w_br_sb[0], w_br_dil=w_br_dil[0], w_br_mem=w_br_mem[0],
                    w_gate=w_gate[0], w_o=w_o[0], w_ffn_in=w_ffn_in[0], w_ffn_out=w_ffn_out[0])
    m_shards = dict(w_in=m_w_in[0], w_mem_kv=m_w_mem_kv[0], w_br_sb=m_w_br_sb[0], w_br_dil=m_w_br_dil[0], w_br_mem=m_w_br_mem[0],
                    w_gate=m_w_gate[0], w_o=m_w_o[0], w_ffn_in=m_w_ffn_in[0], w_ffn_out=m_w_ffn_out[0])
    v_shards = dict(w_in=v_w_in[0], w_mem_kv=v_w_mem_kv[0], w_br_sb=v_w_br_sb[0], w_br_dil=v_w_br_dil[0], w_br_mem=v_w_br_mem[0],
                    w_gate=v_w_gate[0], w_o=v_w_o[0], w_ffn_in=v_w_ffn_in[0], w_ffn_out=v_w_ffn_out[0])

    names = [name for name, _, _ in PACK]
    c_arr = lax.axis_index("c").astype(jnp.int32).reshape(1)
    me_arr = (2 * lax.axis_index("x") + lax.axis_index("y")).astype(jnp.int32).reshape(1)
    mid_names = ["w_gate", "w_mem_kv"]
    late_names = [name for name in names if name not in ["w_in"] + mid_names]
    bf = {name: w_shards[name].astype(BF16) for name in names}
    comm = dict(c=c_arr, me=me_arr, first_names=["w_in"], first_shards=[bf["w_in"]],
                mid_names=mid_names, mid_shards=[bf[name] for name in mid_names],
                late_names=late_names, late_shards=[bf[name] for name in late_names])

    loss_local, grad_x, gw, small, reduced = _local_step(x, mem, loss_target, g_pre_mix, g_post_mix, g_pre_ffn, g_post_ffn,
                                                         g_mem, b_gate, {}, comm)
    loss = lax.psum(loss_local, ("x", "y", "c"))

    halves =[_chip_add(reduced[name][1], reduced[name][0], me_arr, name="chip_add_" + name) for name in names]
    theirs = _pair_share(halves)
    small = _all_sum_small(small)

    upd = {}
    for name, mine, other in zip(names, halves, theirs):
        upd[name] = _adamw_halves(w_shards[name], mine, other, m_shards[name], v_shards[name], c_arr, name="adamw_" + name)
    g_shards = {name: u[0] for name, u in upd.items()}

    def small8(gs, b):
        return jnp.concatenate(gs + [b.reshape(3, D_MODEL)], axis=0)

    sw = small8([g_pre_mix, g_post_mix, g_pre_ffn, g_post_ffn, g_mem], b_gate)
    sm = small8([m_g_pre_mix, m_g_post_mix, m_g_pre_ffn, m_g_post_ffn, m_g_mem], m_b_gate)
    sv = small8([v_g_pre_mix, v_g_post_mix, v_g_pre_ffn, v_g_post_ffn, v_g_mem], v_b_gate)
    s_upd = _adamw(sw, small, sm, sv, tm=8, name="adamw_small")

    def small_out(a):
        return [a[0:1], a[1:2], a[2:3], a[3:4], a[4:5]]

    order = ["w_in", "w_mem_kv", "w_br_sb", "w_br_dil", "w_br_mem", "w_gate", "b_gate", "w_o", "w_ffn_in", "w_ffn_out"]

    def leaves(small_arr, big):
        out = small_out(small_arr)
        for name in order:
            out.append(small_arr[5:8].reshape(1, 3 * D_MODEL) if name == "b_gate" else big[name][None])
        return out

    grads_out = leaves(small, g_shards)
    delta_out = leaves(s_upd[0], {n: u[1] for n, u in upd.items()})
    m_out = leaves(s_upd[1], {n: u[2] for n, u in upd.items()})
    v_out = leaves(s_upd[2], {n: u[3] for n, u in upd.items()})
    return (loss, grad_x, *grads_out, *delta_out, *m_out, *v_out)
```

```python
import jax
import jax.numpy as jnp
import numpy as np
from jax import lax
from jax.experimental import pallas as pl
from jax.experimental.pallas import tpu as pltpu

F32 = jnp.float32
BF16 = jnp.bfloat16
MESH = pl.DeviceIdType.MESH

D_MODEL = 1024
SEQ = 2048
HEAD_DIM = 64
SB_W = 512
DIL_W = 256
MEM_W = 512
MEM_LEN = 256
D_IN = 3 * SB_W + 9 * DIL_W + MEM_W
D_FF = 2816
DIL_D = (1, 4, 16)
ROPE_THETA = 10000.0
NORM_EPS = 1e-6
NEG_INF = -1e30
LANES = 128

ADAM_LR = 0.001
ADAM_B1 = 0.9
ADAM_B2 = 0.999
ADAM_EPS = 1e-08
ADAM_WD = 0.01
ADAM_STEP = 10

N_CHIPS = 4
PACK = (
    ("w_in", (1024, 1088), 1),
    ("w_mem_kv", (256, 1024), 0),
    ("w_br_sb", (512, 256), 1),
    ("w_br_dil", (256, 256), 1),
    ("w_br_mem", (512, 256), 1),
    ("w_gate", (1024, 768), 1),
    ("w_o", (256, 1024), 0),
    ("w_ffn_in", (1024, 1408), 1),
    ("w_ffn_out", (704, 1024), 0),
)
PACK_ROWS = sum(a * b for _, (a, b), _ in PACK) // D_MODEL
HALF_ROWS = PACK_ROWS // 2


def _dot(a, b):
    return lax.dot_general(a, b, (((1,), (0,)), ((), ())), preferred_element_type=F32)


def _dot_nt(a, b):
    return lax.dot_general(a, b, (((1,), (1,)), ((), ())), preferred_element_type=F32)


def _dot_tn(a, b):
    return lax.dot_general(a, b, (((0,), (0,)), ((), ())), preferred_element_type=F32)


def _split_dot(x, u):
    hi = x.astype(BF16)
    lo = (x - hi.astype(F32)).astype(BF16)
    return _dot(hi, u) + _dot(lo, u)


V7X_VMEM_BUDGET = 44 * 2 ** 20


def _rows_that_fit(m, row_bytes, fixed_bytes):
    for tm in (1024, 512, 256, 128):
        if m % tm == 0 and fixed_bytes + tm * row_bytes <= V7X_VMEM_BUDGET:
            return tm
    return min(m, 128)


def _mm(pairs, *, nt, tn, out_dtypes, name, bias=None, extras=(), epilogue=None, side=None):
    pairs = [p if len(p) == 4 else (p[0], p[1], None, None) for p in pairs]
    m = pairs[0][0].shape[-2]
    b0 = pairs[0][1]
    if nt:
        n = b0.shape[-2]
    else:
        n = b0.shape[-1] * (b0.shape[0] if b0.ndim == 3 else 1)
    n_pairs, n_extra, n_out = len(pairs), len(extras), len(out_dtypes)
    assert n % tn == 0
    one_col = n == tn
    ks = [(b.shape[-1] if nt else b.shape[-2]) for _, b, _, _ in pairs]
    fixed = sum(k * tn * 2 for k in ks) * (1 if one_col else 2)
    row_bytes = 2 * sum(k * 2 for k in ks) + 2 * tn * (sum(jnp.dtype(dt).itemsize for dt in out_dtypes) + 2 * n_extra) + 2 * tn * 4
    tm = _rows_that_fit(m, row_bytes, fixed)
    assert m % tm == 0
    b_mode = dict(pipeline_mode=pl.Buffered(1)) if one_col else {}
    has_bias = bias is not None
    side_arrays = side[1] if side else []
    n_side = len(side_arrays)
    n_main_in = 2 * n_pairs + has_bias + n_extra
    n_steps = (n // tn) * (m // tm)

    def body(*refs):
        if n_side:
            step = pl.program_id(0) * (m // tm) + pl.program_id(1)
            finish = _run_side(side, refs[n_main_in:n_main_in + n_side],
                               refs[n_main_in + n_side + n_out:n_main_in + 2 * n_side + n_out],
                               refs[n_main_in + 2 * n_side + n_out:], step, n_steps)
        outs = refs[n_main_in + n_side:n_main_in + n_side + n_out]
        acc = None
        for i in range(n_pairs):
            a, b = refs[2 * i][...], refs[2 * i + 1][...]
            p = _dot_nt(a, b) if nt else _dot(a, b)
            acc = p if acc is None else acc + p
        pos = 2 * n_pairs
        if has_bias:
            acc = acc + refs[pos][...]
            pos += 1
        ex = [r[...] for r in refs[pos:pos + n_extra]]
        vals = (acc,) if epilogue is None else epilogue(acc, *ex)
        for r, v, dt in zip(outs, vals, out_dtypes):
            r[...] = v.astype(dt)
        if n_side:
            finish()

    in_specs, args = [], []
    for a, b, a_col, b_sel in pairs:
        k = b.shape[-1] if nt else b.shape[-2]
        assert a_col is not None or a.shape[1] == k
        if a.ndim == 3:
            in_specs.append(pl.BlockSpec((None, tm, k), lambda j, i, c=a_col: (c, i, 0)))
        else:
            in_specs.append(pl.BlockSpec((tm, k), lambda j, i, c=a_col or 0: (i, c)))
        if b.ndim == 2:
            in_specs.append(pl.BlockSpec((tn, k), lambda j, i: (j, 0), **b_mode) if nt
                            else pl.BlockSpec((k, tn), lambda j, i: (0, j), **b_mode))
        elif nt:
            in_specs.append(pl.BlockSpec((None, tn, k), lambda j, i, s=b_sel: (s, j, 0), **b_mode))
        else:
            assert b_sel == "j" and b.shape[-1] == tn
            in_specs.append(pl.BlockSpec((None, k, tn), lambda j, i: (j, 0, 0), **b_mode))
        args += [a, b]
    if has_bias:
        in_specs.append(pl.BlockSpec((1, tn), lambda j, i: (0, j)))
        args.append(bias)
    for e in extras:
        in_specs.append(pl.BlockSpec((tm, tn), lambda j, i: (i, j)))
        args.append(e)
    out = pl.pallas_call(
        body,
        grid=(n // tn, m // tm),
        in_specs=in_specs + [ANY] * n_side,
        out_specs=[pl.BlockSpec((tm, tn), lambda j, i: (i, j)) for _ in range(n_out)] + [ANY] * n_side,
        out_shape=[jax.ShapeDtypeStruct((m, n), dt) for dt in out_dtypes] + (_side_out_shapes(side) if n_side else []),
        scratch_shapes=_side_sems(side) if n_side else [],
        compiler_params=pltpu.CompilerParams(dimension_semantics=("arbitrary", "arbitrary") if n_side else ("parallel", "parallel")),
        name=name,
    )(*args, *side_arrays)
    if n_side:
        return (out[0] if n_out == 1 else out[:n_out]), out[n_out:]
    return out[0] if n_out == 1 else out


def _mm_tn(a, b, *, tm, tn, name, out_shards=False):
    k, m = a.shape
    b_shards = b.ndim == 3
    out_shards = out_shards or b_shards
    n = b.shape[0] * b.shape[2] if b_shards else b.shape[1]
    tk = _rows_that_fit(k, 2 * 2 * (tm + tn), 3 * tm * tn * 4)
    assert m % tm == 0 and n % tn == 0 and k % tk == 0 and (not b_shards or b.shape[2] == tn)

    def body(a_ref, b_ref, o_ref):
        @pl.when(pl.program_id(2) == 0)
        def _():
            o_ref[...] = jnp.zeros_like(o_ref)

        o_ref[...] += _dot_tn(a_ref[...], b_ref[...])

    if b_shards:
        b_spec = pl.BlockSpec((None, tk, tn), lambda i, j, kk: (j, kk, 0))
    else:
        b_spec = pl.BlockSpec((tk, tn), lambda i, j, kk: (kk, j))
    if out_shards:
        out_spec = pl.BlockSpec((None, tm, tn), lambda i, j, kk: (j, i, 0))
        out_shape = jax.ShapeDtypeStruct((n // tn, m, tn), F32)
    else:
        out_spec = pl.BlockSpec((tm, tn), lambda i, j, kk: (i, j))
        out_shape = jax.ShapeDtypeStruct((m, n), F32)
    return pl.pallas_call(
        body,
        grid=(m // tm, n // tn, k // tk),
        in_specs=[pl.BlockSpec((tk, tm), lambda i, j, kk: (kk, i)), b_spec],
        out_specs=out_spec,
        out_shape=out_shape,
        compiler_params=pltpu.CompilerParams(dimension_semantics=("parallel", "parallel", "arbitrary")),
        name=name,
    )(a, b)


def _rowwise(fn, ins, outs, *, tm, name, side=None):
    rows = next(a.shape[0] for a, kind in ins if kind == "row")
    tm = min(tm, rows)
    assert rows % tm == 0
    n_in, n_out = len(ins), len(outs)
    side_arrays = side[1] if side else []
    n_side = len(side_arrays)

    def body(*refs):
        if n_side:
            finish = _run_side(side, refs[n_in:n_in + n_side], refs[n_in + n_side + n_out:n_in + 2 * n_side + n_out],
                               refs[n_in + 2 * n_side + n_out:], pl.program_id(0), rows // tm)
        vals = fn(*[r[...] for r in refs[:n_in]])
        for (_, dt, kind), r, v in zip(outs, refs[n_in + n_side:n_in + n_side + n_out], vals):
            if kind == "row":
                r[...] = v.astype(dt)
            else:
                @pl.when(pl.program_id(0) == 0)
                def _(r=r):
                    r[...] = jnp.zeros_like(r)

                r[...] += v
        if n_side:
            finish()

    in_specs = [pl.BlockSpec((tm, a.shape[1]), lambda i: (i, 0)) if kind == "row" else pl.BlockSpec(a.shape, lambda i: (0, 0))
                for a, kind in ins]
    out_specs = [pl.BlockSpec((tm, c), lambda i: (i, 0)) if kind == "row" else pl.BlockSpec((1, c), lambda i: (0, 0))
                 for c, _, kind in outs]
    out_shape = [jax.ShapeDtypeStruct((rows if kind == "row" else 1, c), dt) for c, dt, kind in outs]
    ordered = n_side or any(kind == "acc" for _, _, kind in outs)
    return pl.pallas_call(
        body,
        grid=(rows // tm,),
        in_specs=in_specs + [ANY] * n_side,
        out_specs=out_specs + [ANY] * n_side,
        out_shape=out_shape + (_side_out_shapes(side) if n_side else []),
        scratch_shapes=_side_sems(side) if n_side else [],
        compiler_params=pltpu.CompilerParams(dimension_semantics=("arbitrary" if ordered else "parallel",)),
        name=name,
    )(*[a for a, _ in ins], *side_arrays)


def _rstd(x):
    return lax.rsqrt(jnp.mean(x * x, axis=-1, keepdims=True) + NORM_EPS)


def _norm_bwd(dout, xin, g):
    r = _rstd(xin)
    n = xin * r
    dn = dout * g
    dg = jnp.sum(dout * n, axis=0, keepdims=True)
    dx = r * (dn - n * jnp.mean(dn * n, axis=-1, keepdims=True))
    return dx, dg


def _sigmoid(x):
    return 1.0 / (1.0 + jnp.exp(-x))


def _norm_fwd(x, g, *, name, side=None):
    def fn(x, g):
        return ((x * _rstd(x)) * g,)

    out = _rowwise(fn, [(x, "row"), (g, "vec")], [(D_MODEL, BF16, "row")], tm=512, name=name, side=side)
    return (out[0], out[1:]) if side else out[0]


def _mid_fwd(mix, x, g_post_mix, g_pre_ffn):
    def fn(mix, x, g2, g3):
        x1 = x + (mix * _rstd(mix)) * g2
        return x1, (x1 * _rstd(x1)) * g3

    return _rowwise(fn, [(mix, "row"), (x, "row"), (g_post_mix, "vec"), (g_pre_ffn, "vec")],
                    [(D_MODEL, F32, "row"), (D_MODEL, BF16, "row")], tm=512, name="mid_fwd")


def _loss_bwd(f2, x1, g_post_ffn, tgt):
    def fn(f2, x1, g4, tgt):
        r = _rstd(f2)
        n = f2 * r
        err = x1 + n * g4 - tgt
        loss = 0.5 * jnp.sum(jnp.mean(err * err, axis=-1, keepdims=True), axis=0, keepdims=True)
        dy = err * (1.0 / D_MODEL)
        dn = dy * g4
        dg4 = jnp.sum(dy * n, axis=0, keepdims=True)
        df2 = r * (dn - n * jnp.mean(dn * n, axis=-1, keepdims=True))
        return dy, df2, dg4, jnp.broadcast_to(loss, (1, LANES))

    return _rowwise(fn, [(f2, "row"), (x1, "row"), (g_post_ffn, "vec"), (tgt, "row")],
                    [(D_MODEL, F32, "row"), (D_MODEL, BF16, "row"), (D_MODEL, F32, "acc"), (LANES, F32, "acc")],
                    tm=512, name="loss_bwd")


def _mid_bwd(dh2, x1, mix, g_pre_ffn, g_post_mix, dy):
    def fn(dh2, x1, mix, g3, g2, dy):
        d3, dg3 = _norm_bwd(dh2, x1, g3)
        dx1 = dy + d3
        dmix, dg2 = _norm_bwd(dx1, mix, g2)
        return dx1, dmix, dg3, dg2

    return _rowwise(fn, [(dh2, "row"), (x1, "row"), (mix, "row"), (g_pre_ffn, "vec"), (g_post_mix, "vec"), (dy, "row")],
                    [(D_MODEL, F32, "row"), (D_MODEL, BF16, "row"), (D_MODEL, F32, "acc"), (D_MODEL, F32, "acc")],
                    tm=256, name="mid_bwd")


def _first_bwd(dh, x, g_pre_mix, dx1):
    def fn(dh, x, g1, dx1):
        d1, dg1 = _norm_bwd(dh, x, g1)
        return dx1 + d1, dg1

    return _rowwise(fn, [(dh, "row"), (x, "row"), (g_pre_mix, "vec"), (dx1, "row")],
                    [(D_MODEL, F32, "row"), (D_MODEL, F32, "acc")], tm=512, name="first_bwd")


def _mem_norm_bwd(dhm, mem, g_mem):
    def fn(dhm, mem, g):
        return (jnp.sum(dhm * (mem * _rstd(mem)), axis=0, keepdims=True),)

    return _rowwise(fn, [(dhm, "row"), (mem, "row"), (g_mem, "vec")], [(D_MODEL, F32, "acc")], tm=512, name="mem_norm_bwd")[0]


def _gate_bwd(dmerged, gates, ya, yb, yc):
    def fn(dm, gt, ya, yb, yc):
        gt = gt.astype(F32)
        outs, dgp = [], []
        for i, y in enumerate((ya, yb, yc)):
            gi = gt[:, i * D_MODEL:(i + 1) * D_MODEL]
            outs.append(dm * gi)
            dgp.append(dm * y.astype(F32) * gi * (1.0 - gi))
        dgpre = jnp.concatenate(dgp, axis=1)
        return outs[0], outs[1], outs[2], dgpre, jnp.sum(dgpre, axis=0, keepdims=True)

    return _rowwise(fn, [(dmerged, "row"), (gates, "row"), (ya, "row"), (yb, "row"), (yc, "row")],
                    [(D_MODEL, BF16, "row")] * 3 + [(3 * D_MODEL, BF16, "row"), (3 * D_MODEL, F32, "acc")],
                    tm=256, name="gate_bwd")


def _adamw_math(w, g, m, v):
    m = ADAM_B1 * m + (1.0 - ADAM_B1) * g
    v = ADAM_B2 * v + (1.0 - ADAM_B2) * (g * g)
    m_hat = m / (1.0 - ADAM_B1 ** ADAM_STEP)
    v_hat = v / (1.0 - ADAM_B2 ** ADAM_STEP)
    delta = -ADAM_LR * (m_hat / (jnp.sqrt(v_hat) + ADAM_EPS) + ADAM_WD * w)
    return delta, m, v


def _adamw(w, g, m, v, *, tm, name):
    c = w.shape[1]
    return _rowwise(_adamw_math, [(w, "row"), (g, "row"), (m, "row"), (v, "row")], [(c, F32, "row")] * 3, tm=tm, name=name)


def _adamw_halves(w, g_mine, g_theirs, m, v, c_arr, *, name):
    a, b = w.shape
    hf = a // 2
    tr = hf // 4

    def body(c_ref, w_ref, gm_ref, gt_ref, m_ref, v_ref, g_out, d_out, m_out, v_out):
        g = jnp.where(pl.program_id(0) == c_ref[0], gm_ref[...], gt_ref[...])
        d, m_new, v_new = _adamw_math(w_ref[...], g, m_ref[...], v_ref[...])
        g_out[...] = g
        d_out[...] = d
        m_out[...] = m_new
        v_out[...] = v_new

    full = pl.BlockSpec((tr, b), lambda hh, i, c_ref: (hh * (hf // tr) + i, 0))
    half = pl.BlockSpec((tr, b), lambda hh, i, c_ref: (i, 0))
    return pl.pallas_call(
        body,
        grid_spec=pltpu.PrefetchScalarGridSpec(
            num_scalar_prefetch=1,
            grid=(2, hf // tr),
            in_specs=[full, half, half, full, full],
            out_specs=[full] * 4,
        ),
        out_shape=[jax.ShapeDtypeStruct((a, b), F32)] * 4,
        compiler_params=pltpu.CompilerParams(dimension_semantics=("parallel", "parallel")),
        name=name,
    )(c_arr, w, g_mine, g_theirs, m, v)


def _ffn_in_fwd(h2, w_ffn):
    m, tm, tn = h2.shape[0], 512, w_ffn.shape[2]
    assert 2 * tn == D_FF

    def body(h_ref, wg_ref, wu_ref, g_ref, u_ref, f_ref):
        h = h_ref[...]
        g = _dot(h, wg_ref[...])
        u = _dot(h, wu_ref[...])
        g_ref[...] = g.astype(BF16)
        u_ref[...] = u.astype(BF16)
        f_ref[...] = (g * _sigmoid(g) * u).astype(BF16)

    o_spec = pl.BlockSpec((tm, tn), lambda j, i: (i, j))
    return pl.pallas_call(
        body,
        grid=(D_FF // tn, m // tm),
        in_specs=[pl.BlockSpec((tm, D_MODEL), lambda j, i: (i, 0)),
                  pl.BlockSpec((None, D_MODEL, tn), lambda j, i: (j, 0, 0)),
                  pl.BlockSpec((None, D_MODEL, tn), lambda j, i: (j + 2, 0, 0))],
        out_specs=[o_spec, o_spec, o_spec],
        out_shape=[jax.ShapeDtypeStruct((m, D_FF), BF16)] * 3,
        compiler_params=pltpu.CompilerParams(dimension_semantics=("parallel", "parallel")),
        name="ffn_in_fwd",
    )(h2, w_ffn, w_ffn)


def _join_shards(w4):
    _, rows, cols = w4.shape
    tr = rows // 4

    def body(w_ref, o_ref):
        for s in range(N_CHIPS):
            o_ref[:, s * cols:(s + 1) * cols] = w_ref[s]

    return pl.pallas_call(
        body,
        grid=(rows // tr,),
        in_specs=[pl.BlockSpec((N_CHIPS, tr, cols), lambda i: (0, i, 0))],
        out_specs=pl.BlockSpec((tr, N_CHIPS * cols), lambda i: (i, 0)),
        out_shape=jax.ShapeDtypeStruct((rows, N_CHIPS * cols), w4.dtype),
        compiler_params=pltpu.CompilerParams(dimension_semantics=("parallel",)),
        name="join_shards",
    )(w4)


def _split_to_shards(pieces, *, name):
    t = pieces[0].shape[0]
    widths = [p.shape[1] for p in pieces]
    cols = sum(widths) // N_CHIPS
    tm = 512
    plan, start = [], 0
    for p, wd in enumerate(widths):
        for s in range(N_CHIPS):
            lo, hi = max(start, s * cols), min(start + wd, (s + 1) * cols)
            if lo < hi:
                plan.append((s, p, lo - s * cols, hi - s * cols, lo - start, hi - start))
        start += wd

    def body(*refs):
        o_ref = refs[-1]
        for s, p, o_lo, o_hi, p_lo, p_hi in plan:
            o_ref[s, :, o_lo:o_hi] = refs[p][:, p_lo:p_hi]

    return pl.pallas_call(
        body,
        grid=(t // tm,),
        in_specs=[pl.BlockSpec((tm, wd), lambda i: (i, 0)) for wd in widths],
        out_specs=pl.BlockSpec((N_CHIPS, tm, cols), lambda i: (0, i, 0)),
        out_shape=jax.ShapeDtypeStruct((N_CHIPS, t, cols), pieces[0].dtype),
        compiler_params=pltpu.CompilerParams(dimension_semantics=("parallel",)),
        name=name,
    )(*pieces)


def _swiglu_bwd_epilogue(df, g, u):
    g = g.astype(F32)
    u = u.astype(F32)
    sg = _sigmoid(g)
    return df * u * (sg * (1.0 + g * (1.0 - sg))), df * (g * sg)


def _branch_merge_fwd(o_a, o_b, o_c, w_sb, w_dil, w_mem, gates):
    m, tm = o_a.shape[0], 256

    def body(oa_ref, ob_ref, oc_ref, wa_ref, wb_ref, wc_ref, gt_ref, ya_ref, yb_ref, yc_ref, mg_ref):
        def project(o_ref, w_ref):
            o = o_ref[...]
            return jnp.concatenate([_dot(o, w_ref[s]) for s in range(N_CHIPS)], axis=1)

        ya = project(oa_ref, wa_ref)
        yb = project(ob_ref, wb_ref)
        yc = project(oc_ref, wc_ref)
        gt = gt_ref[...].astype(F32)
        ya_ref[...] = ya.astype(BF16)
        yb_ref[...] = yb.astype(BF16)
        yc_ref[...] = yc.astype(BF16)
        mg_ref[...] = (gt[:, :D_MODEL] * ya + gt[:, D_MODEL:2 * D_MODEL] * yb + gt[:, 2 * D_MODEL:] * yc).astype(BF16)

    row = lambda c: pl.BlockSpec((tm, c), lambda i: (i, 0))
    full = lambda a: pl.BlockSpec(a.shape, lambda i: (0, 0, 0))
    return pl.pallas_call(
        body,
        grid=(m // tm,),
        in_specs=[row(SB_W), row(DIL_W), row(MEM_W), full(w_sb), full(w_dil), full(w_mem), row(3 * D_MODEL)],
        out_specs=[row(D_MODEL)] * 4,
        out_shape=[jax.ShapeDtypeStruct((m, D_MODEL), BF16)] * 4,
        compiler_params=pltpu.CompilerParams(dimension_semantics=("parallel",)),
        name="branch_merge_fwd",
    )(o_a, o_b, o_c, w_sb, w_dil, w_mem, gates)


SB_T = 256
SB_SCALE = HEAD_DIM ** -0.5
LOG2_E = 1.4426950408889634


def _sb_masks():
    row = lax.broadcasted_iota(jnp.int32, (SB_T, SB_T), 0)
    col = lax.broadcasted_iota(jnp.int32, (SB_T, SB_T), 1)
    lane = lax.broadcasted_iota(jnp.int32, (1, LANES), 1)
    return row, col, lane


def _sb_logs(z):
    z2 = z * LOG2_E
    lb = jnp.minimum(z2, 0.0) - jnp.log2(1.0 + jnp.exp2(-jnp.abs(z2)))
    return lb, lb - z2


def _sb_specs(n_heads_pairs, col0):
    q = pl.BlockSpec((None, SB_T, LANES), lambda b, p, i: (b, i, col0 + p))
    k = pl.BlockSpec((None, SEQ, LANES), lambda b, p, i: (b, 0, col0 + n_heads_pairs + p))
    v = pl.BlockSpec((None, SEQ, LANES), lambda b, p, i: (b, 0, col0 + 2 * n_heads_pairs + p))
    return q, k, v


def _grid_step(n_pairs, nq):
    return (pl.program_id(0) * n_pairs + pl.program_id(1)) * nq + pl.program_id(2)


def _sb_fwd(proj3, late_shards):
    bl = proj3.shape[0]
    n_pairs = SB_W // LANES
    nq = SEQ // SB_T
    n_late = len(late_shards)
    n_steps = bl * n_pairs * nq

    def body(q_ref, k_ref, v_ref, *rest):
        late_in, (o_ref, o32_ref, w_ref), late_out = rest[:n_late], rest[n_late:n_late + 3], rest[n_late + 3:2 * n_late + 3]
        step = _grid_step(n_pairs, nq)
        if n_late:
            send, forward, finish = _gather_phases(late_in, late_out, *rest[2 * n_late + 3:])
            pl.when(step == 0)(send)
            pl.when(step == n_steps // 2)(forward)
        i = pl.program_id(2)
        row, col, lane = _sb_masks()
        causal = col < row
        u_excl = (row > col).astype(BF16)
        q = q_ref[...]
        heads = []
        for h in range(2):
            mh = (lane // HEAD_DIM) == h
            heads.append((mh, jnp.where(mh, q, jnp.zeros_like(q)) * SB_SCALE))

        def blocks(js, diags, carries, acc):
            ks = [k_ref[pl.ds(pl.multiple_of(j * SB_T, SB_T), SB_T), :] for j in js]
            vs = [v_ref[pl.ds(pl.multiple_of(j * SB_T, SB_T), SB_T), :] for j in js]
            chains = [(b, h) for b in range(len(js)) for h in range(2)]
            z = {c: _dot_nt(heads[c[1]][1], ks[c[0]]) for c in chains}
            lb, lk = {}, {}
            for c in chains:
                lb[c], lk[c] = _sb_logs(z[c])
                if diags[c[0]]:
                    lk[c] = jnp.where(causal, lk[c], 0.0)
            r = {c: _split_dot(lk[c], u_excl) for c in chains}
            carries = list(carries)
            w = {}
            for b, h in chains:
                w_c = jnp.exp2(lb[b, h] + r[b, h] + carries[h])
                w[b, h] = (jnp.where(causal, w_c, 0.0) if diags[b] else w_c).astype(BF16)
                w_ref[h, js[b]] = w[b, h]
                carries[h] = carries[h] + (r[b, h][:, :1] + lk[b, h][:, :1])
            for b, h in chains:
                acc = acc + _dot(w[b, h], jnp.where(heads[h][0], vs[b], jnp.zeros_like(vs[b])))
            return tuple(carries), acc

        zero = jnp.zeros((SB_T, 1), F32)
        init = ((zero, zero), jnp.zeros((SB_T, LANES), F32))
        odd = i % 2
        carries, acc = lax.cond(odd == 1, lambda: blocks([i, i - 1], (True, False), *init), lambda: blocks([i], (True,), *init))
        rest = i - 1 - odd
        carries, acc = lax.fori_loop(
            0, i // 2, lambda jj, c: blocks([rest - 2 * jj, rest - 1 - 2 * jj], (False, False), c[0], c[1]), (carries, acc))
        o_ref[...] = acc.astype(BF16)
        o32_ref[...] = acc
        if n_late:
            pl.when(step == n_steps - 1)(finish)

    q_spec, k_spec, v_spec = _sb_specs(n_pairs, 0)
    blk = pl.BlockSpec((None, SB_T, LANES), lambda b, p, i: (b, i, p))
    out = pl.pallas_call(
        body,
        grid=(bl, n_pairs, nq),
        in_specs=[q_spec, k_spec, v_spec] + [ANY] * n_late,
        out_specs=[blk, blk, _sb_weight_spec(nq)] + [ANY] * n_late,
        out_shape=[jax.ShapeDtypeStruct((bl, SEQ, SB_W), BF16), jax.ShapeDtypeStruct((bl, SEQ, SB_W), F32),
                   jax.ShapeDtypeStruct((bl, n_pairs, nq, 2, nq, SB_T, SB_T), BF16)] + _gather_out_shapes(late_shards),
        scratch_shapes=_gather_sems(n_late) if n_late else [],
        compiler_params=pltpu.CompilerParams(dimension_semantics=("arbitrary", "arbitrary", "arbitrary")),
        name="sb_fwd",
    )(proj3, proj3, proj3, *late_shards)
    return out[0], out[1], out[2], out[3:]


def _sb_weight_spec(nq):
    return pl.BlockSpec((None, None, None, 2, nq, SB_T, SB_T), lambda b, p, i: (b, p, i, 0, 0, 0, 0))


def _sb_bwd(proj3, o_a, do_a, w_all, parts):
    bl = proj3.shape[0]
    n_pairs = SB_W // LANES
    nq = SEQ // SB_T
    n_parts = len(parts)
    n_steps = bl * n_pairs * nq

    def body(q_ref, k_ref, v_ref, o_ref, do_ref, w_ref, *rest):
        p_refs, (dq_ref, dk_ref, dv_ref), land_refs = rest[:n_parts], rest[n_parts:n_parts + 3], rest[n_parts + 3:2 * n_parts + 3]
        dk_acc, dv_acc = rest[2 * n_parts + 3:2 * n_parts + 5]
        step = _grid_step(n_pairs, nq)
        if n_parts:
            send, finish = _chip_exchange_phases(p_refs, land_refs, *rest[2 * n_parts + 5:])
            pl.when(step == 0)(send)
        i = pl.program_id(2)

        @pl.when(i == 0)
        def _():
            dk_acc[...] = jnp.zeros_like(dk_acc)
            dv_acc[...] = jnp.zeros_like(dv_acc)

        row, col, lane = _sb_masks()
        causal = col < row
        u_incl = (row >= col).astype(BF16)
        q = q_ref[...]
        do = do_ref[...]
        prod = do.astype(F32) * o_ref[...]
        heads = []
        for h in range(2):
            mh = (lane // HEAD_DIM) == h
            d_tot = jnp.sum(jnp.where(mh, prod, 0.0), axis=1, keepdims=True)
            heads.append((mh, jnp.where(mh, q, jnp.zeros_like(q)) * SB_SCALE, jnp.where(mh, do, jnp.zeros_like(do)), d_tot))

        def blocks(js, diags, c_das, dq):
            starts = [pl.multiple_of(j * SB_T, SB_T) for j in js]
            ks = [k_ref[pl.ds(s, SB_T), :] for s in starts]
            vs = [v_ref[pl.ds(s, SB_T), :] for s in starts]
            chains = [(b, h) for b in range(len(js)) for h in range(2)]
            z = {c: _dot_nt(heads[c[1]][1], ks[c[0]]) for c in chains}
            dw = {c: _dot_nt(heads[c[1]][2], vs[c[0]]) for c in chains}
            wb = {(b, h): w_ref[h, js[b]] for b, h in chains}
            da = {c: dw[c] * wb[c].astype(F32) for c in chains}
            sfx = {c: _split_dot(da[c], u_incl) for c in chains}
            c_das = list(c_das)
            dz = {}
            for b, h in chains:
                dlk = heads[h][3] - c_das[h] - sfx[b, h]
                if diags[b]:
                    dlk = jnp.where(causal, dlk, 0.0)
                c_das[h] = c_das[h] + sfx[b, h][:, :1]
                e = jnp.exp(-jnp.abs(z[b, h]))
                inv = 1.0 / (1.0 + e)
                pos = z[b, h] >= 0.0
                beta = jnp.where(pos, inv, e * inv)
                one_m_beta = jnp.where(pos, e * inv, inv)
                dz[b, h] = (da[b, h] * one_m_beta - dlk * beta).astype(BF16)
            for b, h in chains:
                dq = dq + _dot(dz[b, h], jnp.where(heads[h][0], ks[b], jnp.zeros_like(ks[b])))
            for b in range(len(js)):
                dk_acc[pl.ds(starts[b], SB_T), :] += _dot_tn(dz[b, 0], heads[0][1]) + _dot_tn(dz[b, 1], heads[1][1])
                dv_acc[pl.ds(starts[b], SB_T), :] += _dot_tn(wb[b, 0], heads[0][2]) + _dot_tn(wb[b, 1], heads[1][2])
            return tuple(c_das), dq

        zero = jnp.zeros((SB_T, 1), F32)
        init = ((zero, zero), jnp.zeros((SB_T, LANES), F32))
        odd = i % 2
        state = lax.cond(odd == 1, lambda: blocks([i, i - 1], (True, False), *init), lambda: blocks([i], (True,), *init))
        rest = i - 1 - odd
        state = lax.fori_loop(0, i // 2, lambda jj, c: blocks([rest - 2 * jj, rest - 1 - 2 * jj], (False, False), c[0], c[1]), state)
        dq_ref[...] = (state[1] * SB_SCALE).astype(BF16)

        @pl.when(i == nq - 1)
        def _():
            dk_ref[...] = dk_acc[...].astype(BF16)
            dv_ref[...] = dv_acc[...].astype(BF16)

        if n_parts:
            pl.when(step == n_steps - 1)(finish)

    q_spec, k_spec, v_spec = _sb_specs(n_pairs, 0)
    blk = pl.BlockSpec((None, SB_T, LANES), lambda b, p, i: (b, i, p))
    seq = pl.BlockSpec((None, SEQ, LANES), lambda b, p, i: (b, 0, p))
    shape = jax.ShapeDtypeStruct((bl, SEQ, SB_W), BF16)
    out = pl.pallas_call(
        body,
        grid=(bl, n_pairs, nq),
        in_specs=[q_spec, k_spec, v_spec, blk, blk, _sb_weight_spec(nq)] + [ANY] * n_parts,
        out_specs=[blk, seq, seq] + [ANY] * n_parts,
        out_shape=[shape, shape, shape] + [jax.ShapeDtypeStruct(p.shape, p.dtype) for p in parts],
        scratch_shapes=[pltpu.VMEM((SEQ, LANES), F32), pltpu.VMEM((SEQ, LANES), F32)]
        + (_chip_exchange_sems(n_parts) if n_parts else []),
        compiler_params=pltpu.CompilerParams(dimension_semantics=("arbitrary", "arbitrary", "arbitrary")),
        name="sb_bwd",
    )(proj3, proj3, proj3, o_a, do_a, w_all, *parts)
    return out[0], out[1], out[2], out[3:]


BAND = 128


BAND_CH = 4
BAND_HEADS = DIL_W // HEAD_DIM


def _swap_half(x):
    n = x.shape[-1]
    lane = lax.broadcasted_iota(jnp.int32, (1, n), 1)
    return jnp.where((lane % HEAD_DIM) < HEAD_DIM // 2, pltpu.roll(x, n - HEAD_DIM // 2, 1), pltpu.roll(x, HEAD_DIM // 2, 1))


def _rope(x, cos, sin_signed):
    x = x.astype(F32)
    return x * cos + _swap_half(x) * sin_signed


def _band_valid(g, blk):
    nb = jnp.where(g == 0, 16, jnp.where(g == 1, 4, 1))
    first_key = jnp.where(lax.rem(blk, nb) != 0, 0, BAND)
    qi = lax.broadcasted_iota(jnp.int32, (BAND, 2 * BAND), 0) + BAND
    kj = lax.broadcasted_iota(jnp.int32, (BAND, 2 * BAND), 1)
    dist = qi - kj
    return (dist >= 0) & (dist <= BAND) & (kj >= first_key)


def _band_specs():
    last_before = lambda i: jnp.maximum(i * BAND_CH - 1, 0)
    cur = lambda col: pl.BlockSpec((None, BAND_CH, BAND, DIL_W), lambda g, i: (g, i, 0, col))
    prev = lambda col: pl.BlockSpec((None, None, BAND, DIL_W), lambda g, i: (g, last_before(i), 0, col))
    tab = pl.BlockSpec((None, BAND_CH, BAND, DIL_W), lambda g, i: (g, lax.rem(i, 16 // BAND_CH), 0, 0))
    tab_prev = pl.BlockSpec((None, None, BAND, DIL_W), lambda g, i: (g, lax.rem(last_before(i), 16), 0, 0))
    return cur, prev, tab, tab_prev


def _band_load(q_ref, k_ref, kp_ref, v_ref, vp_ref, c_ref, s_ref, cp_ref, sp_ref):
    qs = [(_rope(q_ref[b], c_ref[b], s_ref[b]) * SB_SCALE).astype(BF16) for b in range(BAND_CH)]
    ks = [_rope(kp_ref[...], cp_ref[...], sp_ref[...]).astype(BF16)]
    ks += [_rope(k_ref[b], c_ref[b], s_ref[b]).astype(BF16) for b in range(BAND_CH)]
    vs = [vp_ref[...]] + [v_ref[b] for b in range(BAND_CH)]
    k2 = [jnp.concatenate([ks[b], ks[b + 1]], axis=0) for b in range(BAND_CH)]
    v2 = [jnp.concatenate([vs[b], vs[b + 1]], axis=0) for b in range(BAND_CH)]
    return qs, k2, v2


def _band_fwd(qkv_s, cos_t, sin_t):
    def body(q_ref, k_ref, kp_ref, v_ref, vp_ref, c_ref, s_ref, cp_ref, sp_ref, ol_ref):
        g, i = pl.program_id(0), pl.program_id(1)
        qs, k2, v2 = _band_load(q_ref, k_ref, kp_ref, v_ref, vp_ref, c_ref, s_ref, cp_ref, sp_ref)
        lane = lax.broadcasted_iota(jnp.int32, (1, DIL_W), 1)
        for b in range(BAND_CH):
            valid = _band_valid(g, i * BAND_CH + b)
            hs = range(BAND_HEADS)
            mh = [(lane // HEAD_DIM) == h for h in hs]
            s = [jnp.where(valid, _dot_nt(jnp.where(mh[h], qs[b], jnp.zeros_like(qs[b])), k2[b]), NEG_INF) for h in hs]
            m = [jnp.max(s[h], axis=1, keepdims=True) for h in hs]
            p = [jnp.exp(s[h] - m[h]) for h in hs]
            den = [jnp.sum(p[h], axis=1, keepdims=True) for h in hs]
            pv = [_dot(p[h].astype(BF16), jnp.where(mh[h], v2[b], jnp.zeros_like(v2[b]))) for h in hs]
            o = jnp.zeros((BAND, DIL_W), F32)
            lse = jnp.zeros((BAND, DIL_W), F32)
            for h in hs:
                o = o + pv[h] * (1.0 / den[h])
                lse = jnp.where(mh[h], m[h] + jnp.log(den[h]), lse)
            ol_ref[b, :, :DIL_W] = o
            ol_ref[b, :, DIL_W:] = lse

    cur, prev, tab, tab_prev = _band_specs()
    n_blk = qkv_s.shape[1]
    return pl.pallas_call(
        body,
        grid=(3, n_blk // BAND_CH),
        in_specs=[cur(0), cur(1), prev(1), cur(2), prev(2), tab, tab, tab_prev, tab_prev],
        out_specs=pl.BlockSpec((None, BAND_CH, BAND, 2 * DIL_W), lambda g, i: (g, i, 0, 0)),
        out_shape=jax.ShapeDtypeStruct((3, n_blk, BAND, 2 * DIL_W), F32),
        compiler_params=pltpu.CompilerParams(dimension_semantics=("parallel", "parallel")),
        name="band_fwd",
    )(qkv_s, qkv_s, qkv_s, qkv_s, qkv_s, cos_t, sin_t, cos_t, sin_t)


def _band_bwd(qkv_s, cos_t, sin_t, dcat_s):
    def body(q_ref, k_ref, kp_ref, v_ref, vp_ref, c_ref, s_ref, cp_ref, sp_ref, do_ref, lse_ref, dl_ref,
             dq_ref, dk_ref, dv_ref, dkf_ref, dvf_ref):
        g, i = pl.program_id(0), pl.program_id(1)
        qs, k2, v2 = _band_load(q_ref, k_ref, kp_ref, v_ref, vp_ref, c_ref, s_ref, cp_ref, sp_ref)
        lane = lax.broadcasted_iota(jnp.int32, (1, DIL_W), 1)
        dks, dvs = [], []
        for b in range(BAND_CH):
            valid = _band_valid(g, i * BAND_CH + b)
            do, lse, dl = do_ref[b].astype(BF16), lse_ref[b], dl_ref[b]
            hs = range(BAND_HEADS)
            mh = [(lane // HEAD_DIM) == h for h in hs]
            qh = [jnp.where(mh[h], qs[b], jnp.zeros_like(qs[b])) for h in hs]
            doh = [jnp.where(mh[h], do, jnp.zeros_like(do)) for h in hs]
            s = [_dot_nt(qh[h], k2[b]) for h in hs]
            dp = [_dot_nt(doh[h], v2[b]) for h in hs]
            p = [jnp.where(valid, jnp.exp(s[h] - lse[:, h * HEAD_DIM:h * HEAD_DIM + 1]), 0.0) for h in hs]
            ds = [(p[h] * (dp[h] - dl[:, h * HEAD_DIM:h * HEAD_DIM + 1])).astype(BF16) for h in hs]
            pb = [p[h].astype(BF16) for h in hs]
            dq = sum(_dot(ds[h], jnp.where(mh[h], k2[b], jnp.zeros_like(k2[b]))) for h in hs)
            dk2 = sum(_dot_tn(ds[h], qh[h]) for h in hs)
            dv2 = sum(_dot_tn(pb[h], doh[h]) for h in hs)
            dq_ref[b] = dq * SB_SCALE
            dks.append(dk2)
            dvs.append(dv2)
        dkf_ref[...] = dks[0][:BAND]
        dvf_ref[...] = dvs[0][:BAND]
        for b in range(BAND_CH):
            last = b == BAND_CH - 1
            dk_ref[b] = dks[b][BAND:] if last else dks[b][BAND:] + dks[b + 1][:BAND]
            dv_ref[b] = dvs[b][BAND:] if last else dvs[b][BAND:] + dvs[b + 1][:BAND]

    cur, prev, tab, tab_prev = _band_specs()
    first = pl.BlockSpec((None, None, BAND, DIL_W), lambda g, i: (g, i, 0, 0))
    n_blk = qkv_s.shape[1]
    n_chunks = n_blk // BAND_CH
    shape = jax.ShapeDtypeStruct((3, n_blk, BAND, DIL_W), F32)
    shape_first = jax.ShapeDtypeStruct((3, n_chunks, BAND, DIL_W), F32)
    return pl.pallas_call(
        body,
        grid=(3, n_chunks),
        in_specs=[cur(0), cur(1), prev(1), cur(2), prev(2), tab, tab, tab_prev, tab_prev, cur(0), cur(1), cur(2)],
        out_specs=[cur(0), cur(0), cur(0), first, first],
        out_shape=[shape, shape, shape, shape_first, shape_first],
        compiler_params=pltpu.CompilerParams(dimension_semantics=("parallel", "parallel")),
        name="band_bwd",
    )(qkv_s, qkv_s, qkv_s, qkv_s, qkv_s, cos_t, sin_t, cos_t, sin_t, dcat_s, dcat_s, dcat_s)


def _band_combine(dq, dk, dv, dk_first, dv_first, cos_t, sin_t):
    n_chunks = dk_first.shape[1]

    def body(dq_ref, dk_ref, dkn_ref, dv_ref, dvn_ref, c_ref, s_ref, out_ref):
        nxt = (pl.program_id(1) < n_chunks - 1).astype(F32)
        for b in range(BAND_CH):
            cos, sin = c_ref[b], s_ref[b]
            dq_b, dk_b, dv_b = dq_ref[b], dk_ref[b], dv_ref[b]
            if b == BAND_CH - 1:
                dk_b = dk_b + nxt * dkn_ref[...]
                dv_b = dv_b + nxt * dvn_ref[...]
            out_ref[b, :, :DIL_W] = (dq_b * cos - _swap_half(dq_b) * sin).astype(BF16)
            out_ref[b, :, DIL_W:2 * DIL_W] = (dk_b * cos - _swap_half(dk_b) * sin).astype(BF16)
            out_ref[b, :, 2 * DIL_W:] = dv_b.astype(BF16)

    cur, _, tab, _ = _band_specs()
    nxt = pl.BlockSpec((None, None, BAND, DIL_W), lambda g, i: (g, jnp.minimum(i + 1, n_chunks - 1), 0, 0))
    return pl.pallas_call(
        body,
        grid=(3, n_chunks),
        in_specs=[cur(0), cur(0), nxt, cur(0), nxt, tab, tab],
        out_specs=pl.BlockSpec((None, BAND_CH, BAND, 3 * DIL_W), lambda g, i: (g, i, 0, 0)),
        out_shape=jax.ShapeDtypeStruct(dq.shape[:3] + (3 * DIL_W,), BF16),
        compiler_params=pltpu.CompilerParams(dimension_semantics=("parallel", "parallel")),
        name="band_combine",
    )(dq, dk, dk_first, dv, dv_first, cos_t, sin_t)


def _band_merge(ol):
    t, tm = ol.shape[1], 512

    def body(o_ref, l_ref, ob_ref, lse_ref):
        l0, l1, l2 = l_ref[0], l_ref[1], l_ref[2]
        m = jnp.maximum(jnp.maximum(l0, l1), l2)
        lse = m + jnp.log(jnp.exp(l0 - m) + jnp.exp(l1 - m) + jnp.exp(l2 - m))
        ob_ref[...] = (jnp.exp(l0 - lse) * o_ref[0] + jnp.exp(l1 - lse) * o_ref[1] + jnp.exp(l2 - lse) * o_ref[2]).astype(BF16)
        lse_ref[...] = lse

    spec = pl.BlockSpec((tm, DIL_W), lambda i: (i, 0))
    return pl.pallas_call(
        body,
        grid=(t // tm,),
        in_specs=[pl.BlockSpec((3, tm, DIL_W), lambda i: (0, i, 0)), pl.BlockSpec((3, tm, DIL_W), lambda i: (0, i, 1))],
        out_specs=[spec, spec],
        out_shape=[jax.ShapeDtypeStruct((t, DIL_W), BF16), jax.ShapeDtypeStruct((t, DIL_W), F32)],
        compiler_params=pltpu.CompilerParams(dimension_semantics=("parallel",)),
        name="band_merge",
    )(ol, ol)


def _band_delta(do_b, o_b, lse_b):
    def fn(do, o, lse):
        r = lax.broadcasted_iota(jnp.int32, (DIL_W, DIL_W), 0) // HEAD_DIM
        c = lax.broadcasted_iota(jnp.int32, (DIL_W, DIL_W), 1) // HEAD_DIM
        do = do.astype(F32)
        delta = _split_dot(do * o.astype(F32), (r == c).astype(BF16))
        return (jnp.concatenate([do, lse, delta], axis=1),)

    return _rowwise(fn, [(do_b, "row"), (o_b, "row"), (lse_b, "row")], [(3 * DIL_W, F32, "row")], tm=512, name="band_delta")[0]


def _band_masks():
    qi = lax.broadcasted_iota(jnp.int32, (BAND, 2 * BAND), 0) + BAND
    kj = lax.broadcasted_iota(jnp.int32, (BAND, 2 * BAND), 1)
    dist = qi - kj
    row = lax.broadcasted_iota(jnp.int32, (BAND, BAND), 0)
    col = lax.broadcasted_iota(jnp.int32, (BAND, BAND), 1)
    return col <= row, (dist >= 0) & (dist <= BAND)


def _band_attend(q, k, v, valid):
    lane = lax.broadcasted_iota(jnp.int32, (1, DIL_W), 1)
    hs = range(BAND_HEADS)
    mh = [(lane // HEAD_DIM) == h for h in hs]
    s = [jnp.where(valid, _dot_nt(jnp.where(mh[h], q, jnp.zeros_like(q)), k), NEG_INF) for h in hs]
    m = [jnp.max(s[h], axis=1, keepdims=True) for h in hs]
    p = [jnp.exp(s[h] - m[h]) for h in hs]
    den = [jnp.sum(p[h], axis=1, keepdims=True) for h in hs]
    pv = [_dot(p[h].astype(BF16), jnp.where(mh[h], v, jnp.zeros_like(v))) for h in hs]
    o = jnp.zeros((BAND, DIL_W), F32)
    lse = jnp.zeros((BAND, DIL_W), F32)
    for h in hs:
        o = o + pv[h] * (1.0 / den[h])
        lse = jnp.where(mh[h], m[h] + jnp.log(den[h]), lse)
    return o, lse


def _band_attend_bwd(q, k, v, valid, do, lse, dl):
    lane = lax.broadcasted_iota(jnp.int32, (1, DIL_W), 1)
    hs = range(BAND_HEADS)
    mh = [(lane // HEAD_DIM) == h for h in hs]
    qh = [jnp.where(mh[h], q, jnp.zeros_like(q)) for h in hs]
    doh = [jnp.where(mh[h], do, jnp.zeros_like(do)) for h in hs]
    s = [_dot_nt(qh[h], k) for h in hs]
    dp = [_dot_nt(doh[h], v) for h in hs]
    p = [jnp.where(valid, jnp.exp(s[h] - lse[:, h * HEAD_DIM:h * HEAD_DIM + 1]), 0.0) for h in hs]
    ds = [(p[h] * (dp[h] - dl[:, h * HEAD_DIM:h * HEAD_DIM + 1])).astype(BF16) for h in hs]
    pb = [p[h].astype(BF16) for h in hs]
    dq = sum(_dot(ds[h], jnp.where(mh[h], k, jnp.zeros_like(k))) for h in hs)
    dk = sum(_dot_tn(ds[h], qh[h]) for h in hs)
    dv = sum(_dot_tn(pb[h], doh[h]) for h in hs)
    return dq, dk, dv


def _band_group_specs(lead, rows, cls, col0):
    def spec(width):
        if lead == "rows":
            return pl.BlockSpec((None, rows, width), lambda b, i: (b, 0, col0))
        return pl.BlockSpec((None, rows, cls * width), lambda b, i: (b, 0, i))
    return spec


def _band_group_fwd(a, cos_g, sin_g, *, rows, cls, steps, col0, name):
    bl = a.shape[0]
    nb = rows // BAND
    grp_w = 3 * DIL_W

    def body(a_ref, c_ref, s_ref, ol_ref, qr, kr):
        first_valid, later_valid = _band_masks()
        for j in range(cls):
            a0, t0, o0 = j * grp_w, j * DIL_W, j * 2 * DIL_W
            cos, sin = c_ref[:, t0:t0 + DIL_W], s_ref[:, t0:t0 + DIL_W]
            qr[...] = (_rope(a_ref[:, a0:a0 + DIL_W], cos, sin) * SB_SCALE).astype(BF16)
            kr[...] = _rope(a_ref[:, a0 + DIL_W:a0 + 2 * DIL_W], cos, sin).astype(BF16)

            def block(q0, k0, keys, valid, a0=a0, o0=o0):
                o, lse = _band_attend(qr[pl.ds(q0, BAND), :], kr[pl.ds(k0, keys), :],
                                      a_ref[pl.ds(k0, keys), a0 + 2 * DIL_W:a0 + grp_w], valid)
                ol_ref[pl.ds(q0, BAND), o0:o0 + DIL_W] = o
                ol_ref[pl.ds(q0, BAND), o0 + DIL_W:o0 + 2 * DIL_W] = lse

            block(0, 0, BAND, first_valid)
            if nb > 1:
                def later(b, carry, block=block):
                    block(pl.multiple_of(b * BAND, BAND), pl.multiple_of((b - 1) * BAND, BAND), 2 * BAND, later_valid)
                    return carry

                lax.fori_loop(1, nb, later, 0)

    lead = "rows" if col0 is not None else "cols"
    spec = _band_group_specs(lead, rows, cls, col0)
    tab = pl.BlockSpec((rows, cls * DIL_W), lambda b, i: (0, i))
    n_cls = cos_g.shape[1] // DIL_W
    return pl.pallas_call(
        body,
        grid=(bl, steps),
        in_specs=[spec(grp_w), tab, tab],
        out_specs=pl.BlockSpec((None, rows, cls * 2 * DIL_W), lambda b, i: (b, 0, i)),
        out_shape=jax.ShapeDtypeStruct((bl, rows, n_cls * 2 * DIL_W), F32),
        scratch_shapes=[pltpu.VMEM((rows, DIL_W), BF16), pltpu.VMEM((rows, DIL_W), BF16)],
        compiler_params=pltpu.CompilerParams(dimension_semantics=("parallel", "parallel")),
        name=name,
    )(a, cos_g, sin_g)


def _band_group_bwd(a, d, cos_g, sin_g, *, rows, cls, steps, col0, name, side=None):
    bl = a.shape[0]
    nb = rows // BAND
    grp_w = 3 * DIL_W
    side_arrays = side[1] if side else []
    n_side = len(side_arrays)

    def body(a_ref, d_ref, c_ref, s_ref, *rest):
        out_ref = rest[n_side]
        qr, kr, dk_acc, dv_acc = rest[2 * n_side + 1:2 * n_side + 5]
        if n_side:
            step = pl.program_id(0) * steps + pl.program_id(1)
            finish = _run_side(side, rest[:n_side], rest[n_side + 1:2 * n_side + 1], rest[2 * n_side + 5:], step, bl * steps)
        first_valid, later_valid = _band_masks()
        for j in range(cls):
            a0, t0 = j * grp_w, j * DIL_W
            cos, sin = c_ref[:, t0:t0 + DIL_W], s_ref[:, t0:t0 + DIL_W]
            qr[...] = (_rope(a_ref[:, a0:a0 + DIL_W], cos, sin) * SB_SCALE).astype(BF16)
            kr[...] = _rope(a_ref[:, a0 + DIL_W:a0 + 2 * DIL_W], cos, sin).astype(BF16)
            dk_acc[...] = jnp.zeros_like(dk_acc)
            dv_acc[...] = jnp.zeros_like(dv_acc)

            def block(q0, k0, keys, valid, a0=a0, t0=t0):
                qrows, krows = pl.ds(q0, BAND), pl.ds(k0, keys)
                dq, dk, dv = _band_attend_bwd(
                    qr[qrows, :], kr[krows, :], a_ref[krows, a0 + 2 * DIL_W:a0 + grp_w], valid,
                    d_ref[qrows, a0:a0 + DIL_W].astype(BF16), d_ref[qrows, a0 + DIL_W:a0 + 2 * DIL_W],
                    d_ref[qrows, a0 + 2 * DIL_W:a0 + grp_w])
                dq = dq * SB_SCALE
                out_ref[qrows, a0:a0 + DIL_W] = (dq * c_ref[qrows, t0:t0 + DIL_W]
                                                 - _swap_half(dq) * s_ref[qrows, t0:t0 + DIL_W]).astype(BF16)
                dk_acc[krows, :] += dk
                dv_acc[krows, :] += dv

            block(0, 0, BAND, first_valid)
            if nb > 1:
                def later(b, carry, block=block):
                    block(pl.multiple_of(b * BAND, BAND), pl.multiple_of((b - 1) * BAND, BAND), 2 * BAND, later_valid)
                    return carry

                lax.fori_loop(1, nb, later, 0)
            dk = dk_acc[...]
            out_ref[:, a0 + DIL_W:a0 + 2 * DIL_W] = (dk * cos - _swap_half(dk) * sin).astype(BF16)
            out_ref[:, a0 + 2 * DIL_W:a0 + grp_w] = dv_acc[...].astype(BF16)
        if n_side:
            finish()

    lead = "rows" if col0 is not None else "cols"
    spec = _band_group_specs(lead, rows, cls, col0)
    dspec = _band_group_specs(lead, rows, cls, 0 if col0 is not None else None)
    tab = pl.BlockSpec((rows, cls * DIL_W), lambda b, i: (0, i))
    n_cls = cos_g.shape[1] // DIL_W
    out = pl.pallas_call(
        body,
        grid=(bl, steps),
        in_specs=[spec(grp_w), dspec(grp_w), tab, tab] + [ANY] * n_side,
        out_specs=[pl.BlockSpec((None, rows, cls * grp_w), lambda b, i: (b, 0, i))] + [ANY] * n_side,
        out_shape=[jax.ShapeDtypeStruct((bl, rows, n_cls * grp_w), BF16)] + (_side_out_shapes(side) if n_side else []),
        scratch_shapes=[pltpu.VMEM((rows, DIL_W), BF16), pltpu.VMEM((rows, DIL_W), BF16),
                        pltpu.VMEM((rows, DIL_W), F32), pltpu.VMEM((rows, DIL_W), F32)] + (_side_sems(side) if n_side else []),
        compiler_params=pltpu.CompilerParams(dimension_semantics=("arbitrary", "arbitrary") if n_side else ("parallel", "parallel")),
        name=name,
    )(a, d, cos_g, sin_g, *side_arrays)
    return (out[0], out[1:]) if n_side else out[0]


def _band_merge3(ols):
    t, tm = ols[0].shape[0], 512

    def body(o0, l0, o1, l1, o2, l2, ob_ref, lse_ref):
        a, b, c = l0[...], l1[...], l2[...]
        m = jnp.maximum(jnp.maximum(a, b), c)
        lse = m + jnp.log(jnp.exp(a - m) + jnp.exp(b - m) + jnp.exp(c - m))
        ob_ref[...] = (jnp.exp(a - lse) * o0[...] + jnp.exp(b - lse) * o1[...] + jnp.exp(c - lse) * o2[...]).astype(BF16)
        lse_ref[...] = lse

    spec = pl.BlockSpec((tm, DIL_W), lambda i: (i, 0))
    spec_l = pl.BlockSpec((tm, DIL_W), lambda i: (i, 1))
    return pl.pallas_call(
        body,
        grid=(t // tm,),
        in_specs=[spec, spec_l] * 3,
        out_specs=[spec, spec],
        out_shape=[jax.ShapeDtypeStruct((t, DIL_W), BF16), jax.ShapeDtypeStruct((t, DIL_W), F32)],
        compiler_params=pltpu.CompilerParams(dimension_semantics=("parallel",)),
        name="band_merge",
    )(ols[0], ols[0], ols[1], ols[1], ols[2], ols[2])


MEM_T = 512
MEM_SCALE = 128 ** -0.5
MEM_Q_COL = (D_IN - MEM_W) // LANES


def _mem_specs():
    q = pl.BlockSpec((None, MEM_T, LANES), lambda b, h, i: (b, i, MEM_Q_COL + h))
    k = pl.BlockSpec((None, MEM_LEN, LANES), lambda b, h, i: (b, 0, h))
    v = pl.BlockSpec((None, MEM_LEN, LANES), lambda b, h, i: (b, 0, MEM_W // LANES + h))
    blk = pl.BlockSpec((None, MEM_T, LANES), lambda b, h, i: (b, i, h))
    return q, k, v, blk


def _mem_probs(q, k):
    s = _dot_nt(q, k) * MEM_SCALE
    p = jnp.exp(s - jnp.max(s, axis=1, keepdims=True))
    return p * (1.0 / jnp.sum(p, axis=1, keepdims=True))


def _mem_fwd(proj3, kv3):
    bl = proj3.shape[0]

    def body(q_ref, k_ref, v_ref, o_ref):
        p = _mem_probs(q_ref[...], k_ref[...])
        o_ref[...] = _dot(p.astype(BF16), v_ref[...]).astype(BF16)

    q, k, v, blk = _mem_specs()
    return pl.pallas_call(
        body,
        grid=(bl, MEM_W // LANES, SEQ // MEM_T),
        in_specs=[q, k, v],
        out_specs=blk,
        out_shape=jax.ShapeDtypeStruct((bl, SEQ, MEM_W), BF16),
        compiler_params=pltpu.CompilerParams(dimension_semantics=("parallel", "parallel", "parallel")),
        name="mem_fwd",
    )(proj3, kv3, kv3)


def _mem_bwd(proj3, kv3, do_c):
    bl = proj3.shape[0]

    def body(q_ref, k_ref, v_ref, do_ref, dq_ref, dk_ref, dv_ref):
        @pl.when(pl.program_id(2) == 0)
        def _():
            dk_ref[...] = jnp.zeros_like(dk_ref)
            dv_ref[...] = jnp.zeros_like(dv_ref)

        q, k, do = q_ref[...], k_ref[...], do_ref[...]
        p = _mem_probs(q, k)
        dp = _dot_nt(do, v_ref[...])
        ds = (p * (dp - jnp.sum(p * dp, axis=1, keepdims=True)) * MEM_SCALE).astype(BF16)
        dq_ref[...] = _dot(ds, k).astype(BF16)
        dk_ref[...] += _dot_tn(ds, q)
        dv_ref[...] += _dot_tn(p.astype(BF16), do)

    q, k, v, blk = _mem_specs()
    kv_out = pl.BlockSpec((None, MEM_LEN, LANES), lambda b, h, i: (b, 0, h))
    return pl.pallas_call(
        body,
        grid=(bl, MEM_W // LANES, SEQ // MEM_T),
        in_specs=[q, k, v, blk],
        out_specs=[blk, kv_out, kv_out],
        out_shape=[jax.ShapeDtypeStruct((bl, SEQ, MEM_W), BF16), jax.ShapeDtypeStruct((bl, MEM_LEN, MEM_W), F32),
                   jax.ShapeDtypeStruct((bl, MEM_LEN, MEM_W), F32)],
        compiler_params=pltpu.CompilerParams(dimension_semantics=("parallel", "parallel", "arbitrary")),
        name="mem_bwd",
    )(proj3, kv3, kv3, do_c)


def _place():
    x, y, c = lax.axis_index("x"), lax.axis_index("y"), lax.axis_index("c")
    return x, y, c


def _other_chips(x, y):
    return [(1 - x, y), (x, 1 - y), (1 - x, 1 - y)]


def _remote(src, dst, send_sem, recv_sem, to):
    return pltpu.make_async_remote_copy(src_ref=src, dst_ref=dst, send_sem=send_sem, recv_sem=recv_sem,
                                        device_id=to, device_id_type=MESH)


ANY = pl.BlockSpec(memory_space=pl.ANY)


def _gather_weights(shards):
    n = len(shards)

    def body(*refs):
        send, forward, finish = _gather_phases(refs[:n], refs[n:2 * n], *refs[2 * n:])
        send()
        forward()
        finish()

    return pl.pallas_call(
        body,
        in_specs=[ANY] * n,
        out_specs=[ANY] * n,
        out_shape=_gather_out_shapes(shards),
        scratch_shapes=_gather_sems(n),
        name="gather_weights",
    )(*shards)


def _gather_out_shapes(shards):
    return [jax.ShapeDtypeStruct((N_CHIPS,) + s.shape, s.dtype) for s in shards]


def _gather_sems(n):
    return [pltpu.SemaphoreType.DMA((6 * n,)), pltpu.SemaphoreType.DMA((6 * n,))]


def _gather_phases(in_refs, out_refs, send_sems, recv_sems):
    x, y, c = _place()
    sibling = (x, y, 1 - c)
    chips = _other_chips(x, y)
    first, passed = [], []
    for k in range(len(in_refs)):
        hf = in_refs[k].shape[0] // 2

        def half(px, py, pc, k=k, hf=hf):
            return out_refs[k].at[2 * px + py, pl.ds(pc * hf, hf), :]

        src = in_refs[k].at[pl.ds(c * hf, hf), :]
        for j, chip in enumerate(chips):
            s = 6 * k + j
            first.append(_remote(src, half(x, y, c), send_sems.at[s], recv_sems.at[s], (*chip, c)))
            passed.append((_remote(src, half(*chip, c), send_sems.at[s], recv_sems.at[s], (*chip, c)),
                           _remote(half(*chip, c), half(*chip, c), send_sems.at[s + 3], recv_sems.at[s + 3], sibling),
                           _remote(src, half(*chip, 1 - c), send_sems.at[s + 3], recv_sems.at[s + 3], sibling)))

    def send():
        for cp in first:
            cp.start()

    def forward():
        for landed, fwd, _ in passed:
            landed.wait_recv()
            fwd.start()

    def finish():
        for _, _, from_sibling in passed:
            from_sibling.wait_recv()
        for cp in first:
            cp.wait_send()
        for _, fwd, _ in passed:
            fwd.wait_send()

    return send, forward, finish


def _pair_exchange(grads, *, name):
    n = len(grads)
    side = ("pair", grads)

    def body(*refs):
        send, _, finish = _side_phases(side, refs[:n], refs[n:2 * n], refs[2 * n:])
        send()
        finish()

    return pl.pallas_call(
        body,
        in_specs=[ANY] * n,
        out_specs=[ANY] * n,
        out_shape=_side_out_shapes(side),
        scratch_shapes=_side_sems(side),
        name=name,
    )(*grads)


def _pair_exchange_phases(g_refs, land_refs, send_sems, recv_sems):
    x, y, c = _place()
    cps = []
    for k in range(len(g_refs)):
        hf = g_refs[k].shape[1] // 2
        src = g_refs[k].at[:, pl.ds((1 - c) * hf, hf), :]
        cps.append(_remote(src, land_refs[k], send_sems.at[k], recv_sems.at[k], (x, y, 1 - c)))

    def send():
        for cp in cps:
            cp.start()

    def finish():
        for cp in cps:
            cp.wait()

    return send, finish


def _side_out_shapes(side):
    kind, arrays = side
    if kind == "gather":
        return _gather_out_shapes(arrays)
    if kind == "pair":
        return [jax.ShapeDtypeStruct((N_CHIPS, g.shape[1] // 2, g.shape[2]), g.dtype) for g in arrays]
    return [jax.ShapeDtypeStruct(p.shape, p.dtype) for p in arrays]


def _side_sems(side):
    kind, arrays = side
    n = len(arrays)
    if kind == "gather":
        return _gather_sems(n)
    if kind == "pair":
        return [pltpu.SemaphoreType.DMA((n,)), pltpu.SemaphoreType.DMA((n,))]
    return _chip_exchange_sems(n)


def _side_phases(side, in_refs, out_refs, sems):
    kind = side[0]
    if kind == "gather":
        return _gather_phases(in_refs, out_refs, *sems)
    send, finish = (_pair_exchange_phases if kind == "pair" else _chip_exchange_phases)(in_refs, out_refs, *sems)
    return send, None, finish


def _run_side(side, in_refs, out_refs, sems, step, n_steps):
    first, mid, last = _side_phases(side, in_refs, out_refs, sems)
    pl.when(step == 0)(first)
    if mid is not None:
        pl.when(step == n_steps // 2)(mid)
    return lambda: pl.when(step == n_steps - 1)(last)


def _pair_add(g, land, c_arr, *, name):
    _, a, b = g.shape
    hf = a // 2

    def body(c_ref, g_ref, l_ref, o_ref):
        o_ref[...] = (g_ref[...] + l_ref[...]).astype(BF16)

    return pl.pallas_call(
        body,
        grid_spec=pltpu.PrefetchScalarGridSpec(
            num_scalar_prefetch=1,
            grid=(N_CHIPS,),
            in_specs=[pl.BlockSpec((None, None, hf, b), lambda s, c_ref: (s, c_ref[0], 0, 0)),
                      pl.BlockSpec((None, hf, b), lambda s, c_ref: (s, 0, 0))],
            out_specs=pl.BlockSpec((None, hf, b), lambda s, c_ref: (s, 0, 0)),
        ),
        out_shape=jax.ShapeDtypeStruct((N_CHIPS, hf, b), BF16),
        compiler_params=pltpu.CompilerParams(dimension_semantics=("parallel",)),
        name=name,
    )(c_arr, g.reshape(N_CHIPS, 2, hf, b), land)


def _chip_exchange(parts):
    n = len(parts)

    def body(*refs):
        send, finish = _chip_exchange_phases(refs[:n], refs[n:2 * n], *refs[2 * n:])
        send()
        finish()

    return pl.pallas_call(
        body,
        in_specs=[ANY] * n,
        out_specs=[ANY] * n,
        out_shape=[jax.ShapeDtypeStruct(p.shape, p.dtype) for p in parts],
        scratch_shapes=_chip_exchange_sems(n),
        name="chip_exchange",
    )(*parts)


def _chip_exchange_sems(n):
    return [pltpu.SemaphoreType.DMA((3 * n,)), pltpu.SemaphoreType.DMA((3 * n,))]


def _chip_exchange_phases(p_refs, land_refs, send_sems, recv_sems):
    x, y, c = _place()
    me = 2 * x + y
    sends, recvs = [], []
    for k in range(len(p_refs)):
        for j, (cx, cy) in enumerate(_other_chips(x, y)):
            s = 3 * k + j
            sends.append(_remote(p_refs[k].at[2 * cx + cy], land_refs[k].at[me], send_sems.at[s], recv_sems.at[s], (cx, cy, c)))
            recvs.append(_remote(p_refs[k].at[me], land_refs[k].at[2 * cx + cy], send_sems.at[s], recv_sems.at[s], (cx, cy, c)))

    def send():
        for cp in sends:
            cp.start()

    def finish():
        for cp in recvs:
            cp.wait_recv()
        for cp in sends:
            cp.wait_send()

    return send, finish


def _chip_add(land, part, me_arr, *, name):
    _, r, b = land.shape

    def body(me_ref, p_ref, l1_ref, l2_ref, l3_ref, o_ref):
        o_ref[...] = ((p_ref[...].astype(F32) + l1_ref[...].astype(F32)) + l2_ref[...].astype(F32)) + l3_ref[...].astype(F32)

    tr = r // 2
    other = lambda j: pl.BlockSpec((None, tr, b), lambda i, me_ref: (jnp.bitwise_xor(me_ref[0], j), i, 0))
    return pl.pallas_call(
        body,
        grid_spec=pltpu.PrefetchScalarGridSpec(
            num_scalar_prefetch=1,
            grid=(r // tr,),
            in_specs=[pl.BlockSpec((None, tr, b), lambda i, me_ref: (me_ref[0], i, 0)), other(2), other(1), other(3)],
            out_specs=pl.BlockSpec((tr, b), lambda i, me_ref: (i, 0)),
        ),
        out_shape=jax.ShapeDtypeStruct((r, b), F32),
        compiler_params=pltpu.CompilerParams(dimension_semantics=("parallel",)),
        name=name,
    )(me_arr, part, land, land, land)


def _pair_share(halves):
    n = len(halves)

    def body(*refs):
        h_refs, out_refs = refs[:n], refs[n:2 * n]
        send_sems, recv_sems = refs[2 * n:]
        x, y, c = _place()
        cps = [_remote(h_refs[k], out_refs[k], send_sems.at[k], recv_sems.at[k], (x, y, 1 - c)) for k in range(n)]
        for cp in cps:
            cp.start()
        for cp in cps:
            cp.wait()

    return pl.pallas_call(
        body,
        in_specs=[ANY] * n,
        out_specs=[ANY] * n,
        out_shape=[jax.ShapeDtypeStruct(h.shape, F32) for h in halves],
        scratch_shapes=[pltpu.SemaphoreType.DMA((n,)), pltpu.SemaphoreType.DMA((n,))],
        name="pair_share",
    )(*halves)


def _all_sum_small(part):
    def body(p_ref, o_ref, slots, send_sems, recv_sems):
        x, y, c = _place()
        me = 4 * x + 2 * y + c
        slots[me] = p_ref[...]
        peers = [(x ^ dx, y ^ dy, c ^ dc) for dx in (0, 1) for dy in (0, 1) for dc in (0, 1)][1:]
        sends = [_remote(p_ref, slots.at[me], send_sems.at[k], recv_sems.at[k], peer) for k, peer in enumerate(peers)]
        for cp in sends:
            cp.start()
        for k, (px, py, pc) in enumerate(peers):
            _remote(p_ref, slots.at[4 * px + 2 * py + pc], send_sems.at[k], recv_sems.at[k], (px, py, pc)).wait_recv()
        for cp in sends:
            cp.wait_send()
        acc = slots[0]
        for d in range(1, 8):
            acc = acc + slots[d]
        o_ref[...] = acc

    vmem = pl.BlockSpec(memory_space=pltpu.VMEM)
    return pl.pallas_call(
        body,
        in_specs=[vmem],
        out_specs=vmem,
        out_shape=jax.ShapeDtypeStruct(part.shape, F32),
        scratch_shapes=[pltpu.VMEM((8,) + part.shape, F32), pltpu.SemaphoreType.DMA((7,)), pltpu.SemaphoreType.DMA((7,))],
        name="all_sum_small",
    )(part)


def _deinterleave(a, d):
    b, s, c = a.shape
    return a.reshape(b, s // d, d, c).transpose(0, 2, 1, 3).reshape(b * s // BAND, BAND, c)


def _reinterleave(a, d, b):
    c = a.shape[-1]
    return a.reshape(b, d, SEQ // d, c).transpose(0, 2, 1, 3).reshape(b, SEQ, c)


def _rope_tables():
    half = HEAD_DIM // 2
    inv_freq = np.float32(ROPE_THETA) ** (-np.arange(half, dtype=np.float32) * np.float32(2.0) / np.float32(HEAD_DIM))
    ang = np.arange(SEQ, dtype=np.float32)[:, None] * inv_freq[None, :].astype(np.float32)
    cos = np.tile(np.cos(ang).astype(np.float32), (1, 2 * BAND_HEADS))
    sin = np.tile(np.concatenate([-np.sin(ang), np.sin(ang)], axis=1).astype(np.float32), (1, BAND_HEADS))
    return jnp.asarray(cos), jnp.asarray(sin)


def _band_groups():
    out = []
    for d in DIL_D:
        rows = SEQ // d
        cls = max(1, 512 // rows) if d > 1 else 1
        out.append(dict(rows=rows, cls=cls, steps=d // cls))
    return out


def _local_step(x, mem, loss_target, g_pre_mix, g_post_mix, g_pre_ffn, g_post_ffn, g_mem, b_gate, w, comm=None):
    bl = x.shape[0]
    t = bl * SEQ
    chips = range(N_CHIPS)
    half_ff = D_FF // 2

    def with_gathered(w, names, gathered, shards):
        return {**w, **{name: lax.dynamic_update_slice(g, s[None], (comm["me"][0], 0, 0))
                        for name, g, s in zip(names, gathered, shards)}}

    x2 = x.reshape(t, D_MODEL)
    tgt2 = loss_target.reshape(t, D_MODEL)
    mem2 = mem.reshape(bl * MEM_LEN, D_MODEL)

    h = _norm_fwd(x2, g_pre_mix, name="norm_x", side=("gather", comm["first_shards"]) if comm else None)
    if comm:
        w = with_gathered(w, comm["first_names"], h[1], comm["first_shards"])
        h = h[0]
    w_in_full = _join_shards(w["w_in"])
    proj = _mm([(h, w_in_full)], nt=False, tn=2176, out_dtypes=[BF16], name="proj",
               side=("gather", comm["mid_shards"]) if comm else None)
    if comm:
        w = with_gathered(w, comm["mid_names"], proj[1], comm["mid_shards"])
        proj = proj[0]
    w_mem_kv_full = w["w_mem_kv"].reshape(D_MODEL, 2 * MEM_W)
    gates = _mm([(h, w["w_gate"], None, "j")], nt=False,tn=w["w_gate"].shape[2], out_dtypes=[BF16], name="gates",
                bias=b_gate, epilogue=lambda acc: (_sigmoid(acc),))
    hm = _norm_fwd(mem2, g_mem, name="norm_mem")
    kv_m = _mm([(hm, w_mem_kv_full)], nt=False,tn=1024, out_dtypes=[BF16], name="mem_kv")
    proj3 = proj.reshape(bl, SEQ, D_IN)
    kv3 = kv_m.reshape(bl, MEM_LEN, 2 * MEM_W)

    o_a, o_a32, sb_weights, late_gathered = _sb_fwd(proj3, comm["late_shards"] if comm else [])
    if comm:
        w = with_gathered(w, comm["late_names"], late_gathered, comm["late_shards"])
    w_o_full = w["w_o"].reshape(D_MODEL, D_MODEL)
    w_ffn_out_full = w["w_ffn_out"].reshape(D_FF, D_MODEL)

    cos_t, sin_t = _rope_tables()
    dil0 = 3 * SB_W

    grp_w = 3 * DIL_W
    band = []
    for g, (d, cfg) in enumerate(zip(DIL_D, _band_groups())):
        a_g = proj3 if d == 1 else proj3[:, :, dil0 + g * grp_w:dil0 + (g + 1) * grp_w].reshape(bl, SEQ // d, d * grp_w)
        band.append(dict(cfg, a=a_g, col0=dil0 // grp_w if d == 1 else None, cos=cos_t.reshape(SEQ // d, d * DIL_W),
                         sin=sin_t.reshape(SEQ // d, d * DIL_W)))
    ols = [_band_group_fwd(b["a"], b["cos"], b["sin"], rows=b["rows"], cls=b["cls"], steps=b["steps"], col0=b["col0"],
                           name=f"band_fwd_{g}").reshape(t, 2 * DIL_W) for g, b in enumerate(band)]
    o_b, lse_b = _band_merge3(ols)

    o_c = _mem_fwd(proj3, kv3)

    o_a2, o_c2 = o_a.reshape(t, SB_W), o_c.reshape(t, MEM_W)
    y_a, y_b, y_c, merged = _branch_merge_fwd(o_a2, o_b, o_c2, w["w_br_sb"], w["w_br_dil"], w["w_br_mem"], gates)
    mix = _mm([(merged, w_o_full)], nt=False,tn=1024, out_dtypes=[F32], name="mix")
    x1, h2 = _mid_fwd(mix, x2, g_post_mix, g_pre_ffn)
    gg, uu, f = _ffn_in_fwd(h2, w["w_ffn_in"])
    f2 = _mm([(f, w_ffn_out_full)], nt=False,tn=1024, out_dtypes=[F32], name="ffn_out")

    dy, df2, dg_post_ffn, loss_row = _loss_bwd(f2, x1, g_post_ffn, tgt2)

    dg_ffn, du_ffn = _mm([(df2, w_ffn_out_full)], nt=True,tn=half_ff, out_dtypes=[BF16, BF16], name="d_ffn_act",
                         extras=(gg, uu), epilogue=_swiglu_bwd_epilogue)
    gw = {}
    gw["w_ffn_out"] = _mm_tn(f, df2, tm=half_ff, tn=1024, name="gw_ffn_out").reshape(N_CHIPS, D_FF // N_CHIPS, D_MODEL)
    gw_ffn_g = _mm_tn(h2, dg_ffn, tm=1024, tn=half_ff, name="gw_ffn_gate", out_shards=True)
    gw_ffn_u = _mm_tn(h2, du_ffn, tm=1024, tn=half_ff, name="gw_ffn_up", out_shards=True)
    gw["w_ffn_in"] = jnp.concatenate([gw_ffn_g, gw_ffn_u], axis=0)
    dh2 = _mm([(dg_ffn, w["w_ffn_in"], 0, 0), (dg_ffn, w["w_ffn_in"], 1, 1), (du_ffn, w["w_ffn_in"], 0, 2),
               (du_ffn, w["w_ffn_in"], 1, 3)], nt=True,tn=1024, out_dtypes=[F32], name="d_h2")
    dx1, dmix, dg_pre_ffn, dg_post_mix = _mid_bwd(dh2, x1, mix, g_pre_ffn, g_post_mix, dy)

    gw["w_o"] = _mm_tn(merged, dmix, tm=1024, tn=1024, name="gw_o").reshape(N_CHIPS, D_MODEL // N_CHIPS, D_MODEL)
    dmerged = _mm([(dmix, w_o_full)], nt=True,tn=1024, out_dtypes=[F32], name="d_merged")
    dy_a, dy_b, dy_c, dgpre, db_gate = _gate_bwd(dmerged, gates, y_a, y_b, y_c)
    br_cols = D_MODEL // N_CHIPS
    gw["w_br_sb"] = _mm_tn(o_a2, dy_a, tm=512, tn=br_cols, name="gw_br_sb", out_shards=True)
    gw["w_br_dil"] = _mm_tn(o_b, dy_b, tm=256, tn=br_cols, name="gw_br_dil", out_shards=True)
    gw["w_br_mem"] = _mm_tn(o_c2, dy_c, tm=512, tn=br_cols, name="gw_br_mem", out_shards=True)
    gw["w_gate"] = _mm_tn(h, dgpre, tm=1024, tn=w["w_gate"].shape[2], name="gw_gate", out_shards=True)
    do_a = _mm([(dy_a, w["w_br_sb"], s, s) for s in chips], nt=True,tn=SB_W, out_dtypes=[BF16], name="d_o_a")
    do_b = _mm([(dy_b, w["w_br_dil"], s, s) for s in chips], nt=True,tn=DIL_W, out_dtypes=[BF16], name="d_o_b")
    do_c = _mm([(dy_c, w["w_br_mem"], s, s) for s in chips], nt=True,tn=MEM_W, out_dtypes=[BF16], name="d_o_c")

    dq_c, dk_m, dv_m = _mem_bwd(proj3, kv3, do_c.reshape(bl, SEQ, MEM_W))
    dkv_m = jnp.concatenate([dk_m, dv_m], axis=-1).reshape(bl * MEM_LEN, 2 * MEM_W).astype(BF16)
    gw["w_mem_kv"] = _mm_tn(hm, dkv_m, tm=1024, tn=1024, name="gw_mem_kv").reshape(N_CHIPS, D_MODEL // N_CHIPS, 2 * MEM_W)
    dhm = _mm([(dkv_m, w_mem_kv_full)], nt=True,tn=1024, out_dtypes=[F32], name="d_hm")
    dg_mem = _mem_norm_bwd(dhm, mem2, g_mem)

    dcat = _band_delta(do_b, o_b, lse_b)
    early = [name for name, _, _ in PACK if name != "w_in"] if comm else []
    grads = [gw[name] for name in early]
    d_dil = []
    for g, (d, b) in enumerate(zip(DIL_D, band)):
        out = _band_group_bwd(b["a"], dcat.reshape(bl, SEQ // d, d * grp_w), b["cos"], b["sin"], rows=b["rows"], cls=b["cls"],
                              steps=b["steps"], col0=b["col0"], name=f"band_bwd_{g}",
                              side=("pair", grads) if comm and g == 0 else None)
        if comm and g == 0:
            out, lands = out
        d_dil.append(out.reshape(bl, SEQ, grp_w))

    parts = [_pair_add(g, l, comm["c"], name="pair_add_" + name) for name, g, l in zip(early, grads, lands)] if comm else []
    dq_a, dk_a, dv_a, lands = _sb_bwd(proj3, o_a32, do_a.reshape(bl, SEQ, SB_W), sb_weights, parts)
    reduced = {name: (p, l) for name, p, l in zip(early, parts, lands)}

    in_cols = D_IN // N_CHIPS
    dproj_s = _split_to_shards([a.reshape(t, a.shape[-1]) for a in [dq_a, dk_a, dv_a] + d_dil + [dq_c]], name="dproj_shards")
    gw["w_in"] = _mm_tn(h, dproj_s, tm=1024, tn=in_cols, name="gw_in")
    if comm:
        land = _pair_exchange([gw["w_in"]], name="pair_exchange_w_in")[0]
        part_in = _pair_add(gw["w_in"], land, comm["c"], name="pair_add_w_in")
    dh = _mm([(dproj_s, w["w_in"], s, s) for s in chips] + [(dgpre, w["w_gate"], s, s) for s in chips],
             nt=True, tn=1024, out_dtypes=[F32], name="d_h", side=("chip", [part_in]) if comm else None)
    if comm:
        dh, (land_in,) = dh
        reduced["w_in"] = (part_in, land_in)
    grad_x, dg_pre_mix = _first_bwd(dh, x2, g_pre_mix, dx1)
    small = jnp.concatenate([dg_pre_mix, dg_post_mix, dg_pre_ffn, dg_post_ffn, dg_mem, db_gate.reshape(3, D_MODEL)], axis=0)
    return loss_row[0, 0], grad_x.reshape(bl, SEQ, D_MODEL), gw, small, reduced


def kernel(x, mem, g_pre_mix, g_post_mix, g_pre_ffn, g_post_ffn, g_mem, w_in, w_mem_kv, w_br_sb, w_br_dil, w_br_mem, w_gate, b_gate, w_o, w_ffn_in, w_ffn_out, loss_target, m_g_pre_mix, m_g_post_mix, m_g_pre_ffn, m_g_post_ffn, m_g_mem, m_w_in, m_w_mem_kv, m_w_br_sb, m_w_br_dil, m_w_br_mem, m_w_gate, m_b_gate, m_w_o, m_w_ffn_in, m_w_ffn_out, v_g_pre_mix, v_g_post_mix, v_g_pre_ffn, v_g_post_ffn, v_g_mem, v_w_in, v_w_mem_kv, v_w_br_sb, v_w_br_dil, v_w_br_mem, v_w_gate, v_b_gate, v_w_o, v_w_ffn_in, v_w_ffn_out):
    w_shards = dict(w_in=w_in[0], w_mem_kv=w_mem_kv[0], w_br_sb=w_br_sb[0], w_br_dil=w_br_dil[0], w_br_mem=w_br_mem[0],
                    w_gate=w_gate[0], w_o=w_o[0], w_ffn_in=w_ffn_in[0], w_ffn_out=w_ffn_out[0])
    m_shards = dict(w_in=m_w_in[0], w_mem_kv=m_w_mem_kv[0], w_br_sb=m_w_br_sb[0], w_br_dil=m_w_br_dil[0], w_br_mem=m_w_br_mem[0],
                    w_gate=m_w_gate[0], w_o=m_w_o[0], w_ffn_in=m_w_ffn_in[0], w_ffn_out=m_w_ffn_out[0])
    v_shards = dict(w_in=v_w_in[0], w_mem_kv=v_w_mem_kv[0], w_br_sb=v_w_br_sb[0], w_br_dil=v_w_br_dil[0], w_br_mem=v_w_br_mem[0],
                    w_gate=v_w_gate[0], w_o=v_w_o[0], w_ffn_in=v_w_ffn_in[0], w_ffn_out=v_w_ffn_out[0])

    names = [name for name, _, _ in PACK]
    c_arr = lax.axis_index("c").astype(jnp.int32).reshape(1)
    me_arr = (2 * lax.axis_index("x") + lax.axis_index("y")).astype(jnp.int32).reshape(1)
    mid_names = ["w_gate", "w_mem_kv"]
    late_names = [name for name in names if name not in ["w_in"] + mid_names]
    bf = {name: w_shards[name].astype(BF16) for name in names}
    comm = dict(c=c_arr, me=me_arr, first_names=["w_in"], first_shards=[bf["w_in"]],
                mid_names=mid_names, mid_shards=[bf[name] for name in mid_names],
                late_names=late_names, late_shards=[bf[name] for name in late_names])

    loss_local, grad_x, gw, small, reduced = _local_step(x, mem, loss_target, g_pre_mix, g_post_mix, g_pre_ffn, g_post_ffn,
                                                         g_mem, b_gate, {}, comm)
    loss = lax.psum(loss_local, ("x", "y", "c"))

    halves =[_chip_add(reduced[name][1], reduced[name][0], me_arr, name="chip_add_" + name) for name in names]
    theirs = _pair_share(halves)
    small = _all_sum_small(small)

    upd = {}
    for name, mine, other in zip(names, halves, theirs):
        upd[name] = _adamw_halves(w_shards[name], mine, other, m_shards[name], v_shards[name], c_arr, name="adamw_" + name)
    g_shards = {name: u[0] for name, u in upd.items()}

    def small8(gs, b):
        return jnp.concatenate(gs + [b.reshape(3, D_MODEL)], axis=0)

    sw = small8([g_pre_mix, g_post_mix, g_pre_ffn, g_post_ffn, g_mem], b_gate)
    sm = small8([m_g_pre_mix, m_g_post_mix, m_g_pre_ffn, m_g_post_ffn, m_g_mem], m_b_gate)
    sv = small8([v_g_pre_mix, v_g_post_mix, v_g_pre_ffn, v_g_post_ffn, v_g_mem], v_b_gate)
    s_upd = _adamw(sw, small, sm, sv, tm=8, name="adamw_small")

    def small_out(a):
        return [a[0:1], a[1:2], a[2:3], a[3:4], a[4:5]]

    order = ["w_in", "w_mem_kv", "w_br_sb", "w_br_dil", "w_br_mem", "w_gate", "b_gate", "w_o", "w_ffn_in", "w_ffn_out"]

    def leaves(small_arr, big):
        out = small_out(small_arr)
        for name in order:
            out.append(small_arr[5:8].reshape(1, 3 * D_MODEL) if name == "b_gate" else big[name][None])
        return out

    grads_out = leaves(small, g_shards)
    delta_out = leaves(s_upd[0], {n: u[1] for n, u in upd.items()})
    m_out = leaves(s_upd[1], {n: u[2] for n, u in upd.items()})
    v_out = leaves(s_upd[2], {n: u[3] for n, u in upd.items()})
    return (loss, grad_x, *grads_out, *delta_out, *m_out, *v_out)
```

```python
import jax
import jax.numpy as jnp
import numpy as np
from jax import lax
from jax.experimental import pallas as pl
from jax.experimental.pallas import tpu as pltpu

F32 = jnp.float32
BF16 = jnp.bfloat16
MESH = pl.DeviceIdType.MESH

D_MODEL = 1024
SEQ = 2048
HEAD_DIM = 64
SB_W = 512
DIL_W = 256
MEM_W = 512
MEM_LEN = 256
D_IN = 3 * SB_W + 9 * DIL_W + MEM_W
D_FF = 2816
DIL_D = (1, 4, 16)
ROPE_THETA = 10000.0
NORM_EPS = 1e-6
NEG_INF = -1e30
LANES = 128

ADAM_LR = 0.001
ADAM_B1 = 0.9
ADAM_B2 = 0.999
ADAM_EPS = 1e-08
ADAM_WD = 0.01
ADAM_STEP = 10

N_CHIPS = 4
PACK = (
    ("w_in", (1024, 1088), 1),
    ("w_mem_kv", (256, 1024), 0),
    ("w_br_sb", (512, 256), 1),
    ("w_br_dil", (256, 256), 1),
    ("w_br_mem", (512, 256), 1),
    ("w_gate", (1024, 768), 1),
    ("w_o", (256, 1024), 0),
    ("w_ffn_in", (1024, 1408), 1),
    ("w_ffn_out", (704, 1024), 0),
)
PACK_ROWS = sum(a * b for _, (a, b), _ in PACK) // D_MODEL
HALF_ROWS = PACK_ROWS // 2


def _dot(a, b):
    return lax.dot_general(a, b, (((1,), (0,)), ((), ())), preferred_element_type=F32)


def _dot_nt(a, b):
    return lax.dot_general(a, b, (((1,), (1,)), ((), ())), preferred_element_type=F32)


def _dot_tn(a, b):
    return lax.dot_general(a, b, (((0,), (0,)), ((), ())), preferred_element_type=F32)


def _split_dot(x, u):
    hi = x.astype(BF16)
    lo = (x - hi.astype(F32)).astype(BF16)
    return _dot(hi, u) + _dot(lo, u)


V7X_VMEM_BUDGET = 44 * 2 ** 20


def _rows_that_fit(m, row_bytes, fixed_bytes):
    for tm in (1024, 512, 256, 128):
        if m % tm == 0 and fixed_bytes + tm * row_bytes <= V7X_VMEM_BUDGET:
            return tm
    return min(m, 128)


def _mm(pairs, *, nt, tn, out_dtypes, name, bias=None, extras=(), epilogue=None, side=None):
    pairs = [p if len(p) == 4 else (p[0], p[1], None, None) for p in pairs]
    m = pairs[0][0].shape[-2]
    b0 = pairs[0][1]
    if nt:
        n = b0.shape[-2]
    else:
        n = b0.shape[-1] * (b0.shape[0] if b0.ndim == 3 else 1)
    n_pairs, n_extra, n_out = len(pairs), len(extras), len(out_dtypes)
    assert n % tn == 0
    one_col = n == tn
    ks = [(b.shape[-1] if nt else b.shape[-2]) for _, b, _, _ in pairs]
    fixed = sum(k * tn * 2 for k in ks) * (1 if one_col else 2)
    row_bytes = 2 * sum(k * 2 for k in ks) + 2 * tn * (sum(jnp.dtype(dt).itemsize for dt in out_dtypes) + 2 * n_extra) + 2 * tn * 4
    tm = _rows_that_fit(m, row_bytes, fixed)
    assert m % tm == 0
    b_mode = dict(pipeline_mode=pl.Buffered(1)) if one_col else {}
    has_bias = bias is not None
    side_arrays = side[1] if side else []
    n_side = len(side_arrays)
    n_main_in = 2 * n_pairs + has_bias + n_extra
    n_steps = (n // tn) * (m // tm)

    def body(*refs):
        if n_side:
            step = pl.program_id(0) * (m // tm) + pl.program_id(1)
            finish = _run_side(side, refs[n_main_in:n_main_in + n_side],
                               refs[n_main_in + n_side + n_out:n_main_in + 2 * n_side + n_out],
                               refs[n_main_in + 2 * n_side + n_out:], step, n_steps)
        outs = refs[n_main_in + n_side:n_main_in + n_side + n_out]
        acc = None
        for i in range(n_pairs):
            a, b = refs[2 * i][...], refs[2 * i + 1][...]
            p = _dot_nt(a, b) if nt else _dot(a, b)
            acc = p if acc is None else acc + p
        pos = 2 * n_pairs
        if has_bias:
            acc = acc + refs[pos][...]
            pos += 1
        ex = [r[...] for r in refs[pos:pos + n_extra]]
        vals = (acc,) if epilogue is None else epilogue(acc, *ex)
        for r, v, dt in zip(outs, vals, out_dtypes):
            r[...] = v.astype(dt)
        if n_side:
            finish()

    in_specs, args = [], []
    for a, b, a_col, b_sel in pairs:
        k = b.shape[-1] if nt else b.shape[-2]
        assert a_col is not None or a.shape[1] == k
        if a.ndim == 3:
            in_specs.append(pl.BlockSpec((None, tm, k), lambda j, i, c=a_col: (c, i, 0)))
        else:
            in_specs.append(pl.BlockSpec((tm, k), lambda j, i, c=a_col or 0: (i, c)))
        if b.ndim == 2:
            in_specs.append(pl.BlockSpec((tn, k), lambda j, i: (j, 0), **b_mode) if nt
                            else pl.BlockSpec((k, tn), lambda j, i: (0, j), **b_mode))
        elif nt:
            in_specs.append(pl.BlockSpec((None, tn, k), lambda j, i, s=b_sel: (s, j, 0), **b_mode))
        else:
            assert b_sel == "j" and b.shape[-1] == tn
            in_specs.append(pl.BlockSpec((None, k, tn), lambda j, i: (j, 0, 0), **b_mode))
        args += [a, b]
    if has_bias:
        in_specs.append(pl.BlockSpec((1, tn), lambda j, i: (0, j)))
        args.append(bias)
    for e in extras:
        in_specs.append(pl.BlockSpec((tm, tn), lambda j, i: (i, j)))
        args.append(e)
    out = pl.pallas_call(
        body,
        grid=(n // tn, m // tm),
        in_specs=in_specs + [ANY] * n_side,
        out_specs=[pl.BlockSpec((tm, tn), lambda j, i: (i, j)) for _ in range(n_out)] + [ANY] * n_side,
        out_shape=[jax.ShapeDtypeStruct((m, n), dt) for dt in out_dtypes] + (_side_out_shapes(side) if n_side else []),
        scratch_shapes=_side_sems(side) if n_side else [],
        compiler_params=pltpu.CompilerParams(dimension_semantics=("arbitrary", "arbitrary") if n_side else ("parallel", "parallel")),
        name=name,
    )(*args, *side_arrays)
    if n_side:
        return (out[0] if n_out == 1 else out[:n_out]), out[n_out:]
    return out[0] if n_out == 1 else out


def _mm_tn(a, b, *, tm, tn, name, out_shards=False):
    k, m = a.shape
    b_shards = b.ndim == 3
    out_shards = out_shards or b_shards
    n = b.shape[0] * b.shape[2] if b_shards else b.shape[1]
    tk = _rows_that_fit(k, 2 * 2 * (tm + tn), 3 * tm * tn * 4)
    assert m % tm == 0 and n % tn == 0 and k % tk == 0 and (not b_shards or b.shape[2] == tn)

    def body(a_ref, b_ref, o_ref):
        @pl.when(pl.program_id(2) == 0)
        def _():
            o_ref[...] = jnp.zeros_like(o_ref)

        o_ref[...] += _dot_tn(a_ref[...], b_ref[...])

    if b_shards:
        b_spec = pl.BlockSpec((None, tk, tn), lambda i, j, kk: (j, kk, 0))
    else:
        b_spec = pl.BlockSpec((tk, tn), lambda i, j, kk: (kk, j))
    if out_shards:
        out_spec = pl.BlockSpec((None, tm, tn), lambda i, j, kk: (j, i, 0))
        out_shape = jax.ShapeDtypeStruct((n // tn, m, tn), F32)
    else:
        out_spec = pl.BlockSpec((tm, tn), lambda i, j, kk: (i, j))
        out_shape = jax.ShapeDtypeStruct((m, n), F32)
    return pl.pallas_call(
        body,
        grid=(m // tm, n // tn, k // tk),
        in_specs=[pl.BlockSpec((tk, tm), lambda i, j, kk: (kk, i)), b_spec],
        out_specs=out_spec,
        out_shape=out_shape,
        compiler_params=pltpu.CompilerParams(dimension_semantics=("parallel", "parallel", "arbitrary")),
        name=name,
    )(a, b)


def _rowwise(fn, ins, outs, *, tm, name, side=None):
    rows = next(a.shape[0] for a, kind in ins if kind == "row")
    tm = min(tm, rows)
    assert rows % tm == 0
    n_in, n_out = len(ins), len(outs)
    side_arrays = side[1] if side else []
    n_side = len(side_arrays)

    def body(*refs):
        if n_side:
            finish = _run_side(side, refs[n_in:n_in + n_side], refs[n_in + n_side + n_out:n_in + 2 * n_side + n_out],
                               refs[n_in + 2 * n_side + n_out:], pl.program_id(0), rows // tm)
        vals = fn(*[r[...] for r in refs[:n_in]])
        for (_, dt, kind), r, v in zip(outs, refs[n_in + n_side:n_in + n_side + n_out], vals):
            if kind == "row":
                r[...] = v.astype(dt)
            else:
                @pl.when(pl.program_id(0) == 0)
                def _(r=r):
                    r[...] = jnp.zeros_like(r)

                r[...] += v
        if n_side:
            finish()

    in_specs = [pl.BlockSpec((tm, a.shape[1]), lambda i: (i, 0)) if kind == "row" else pl.BlockSpec(a.shape, lambda i: (0, 0))
                for a, kind in ins]
    out_specs = [pl.BlockSpec((tm, c), lambda i: (i, 0)) if kind == "row" else pl.BlockSpec((1, c), lambda i: (0, 0))
                 for c, _, kind in outs]
    out_shape = [jax.ShapeDtypeStruct((rows if kind == "row" else 1, c), dt) for c, dt, kind in outs]
    ordered = n_side or any(kind == "acc" for _, _, kind in outs)
    return pl.pallas_call(
        body,
        grid=(rows // tm,),
        in_specs=in_specs + [ANY] * n_side,
        out_specs=out_specs + [ANY] * n_side,
        out_shape=out_shape + (_side_out_shapes(side) if n_side else []),
        scratch_shapes=_side_sems(side) if n_side else [],
        compiler_params=pltpu.CompilerParams(dimension_semantics=("arbitrary" if ordered else "parallel",)),
        name=name,
    )(*[a for a, _ in ins], *side_arrays)


def _rstd(x):
    return lax.rsqrt(jnp.mean(x * x, axis=-1, keepdims=True) + NORM_EPS)


def _norm_bwd(dout, xin, g):
    r = _rstd(xin)
    n = xin * r
    dn = dout * g
    dg = jnp.sum(dout * n, axis=0, keepdims=True)
    dx = r * (dn - n * jnp.mean(dn * n, axis=-1, keepdims=True))
    return dx, dg


def _sigmoid(x):
    return 0.5 * jnp.tanh(0.5 * x) + 0.5


def _norm_fwd(x, g, *, name, side=None):
    def fn(x, g):
        return ((x * _rstd(x)) * g,)

    out = _rowwise(fn, [(x, "row"), (g, "vec")], [(D_MODEL, BF16, "row")], tm=512, name=name, side=side)
    return (out[0], out[1:]) if side else out[0]


def _mid_fwd(mix, x, g_post_mix, g_pre_ffn):
    def fn(mix, x, g2, g3):
        x1 = x + (mix * _rstd(mix)) * g2
        return x1, (x1 * _rstd(x1)) * g3

    return _rowwise(fn, [(mix, "row"), (x, "row"), (g_post_mix, "vec"), (g_pre_ffn, "vec")],
                    [(D_MODEL, F32, "row"), (D_MODEL, BF16, "row")], tm=512, name="mid_fwd")


def _loss_bwd(f2, x1, g_post_ffn, tgt):
    def fn(f2, x1, g4, tgt):
        r = _rstd(f2)
        n = f2 * r
        err = x1 + n * g4 - tgt
        loss = 0.5 * jnp.sum(jnp.mean(err * err, axis=-1, keepdims=True), axis=0, keepdims=True)
        dy = err * (1.0 / D_MODEL)
        dn = dy * g4
        dg4 = jnp.sum(dy * n, axis=0, keepdims=True)
        df2 = r * (dn - n * jnp.mean(dn * n, axis=-1, keepdims=True))
        return dy, df2, dg4, jnp.broadcast_to(loss, (1, LANES))

    return _rowwise(fn, [(f2, "row"), (x1, "row"), (g_post_ffn, "vec"), (tgt, "row")],
                    [(D_MODEL, F32, "row"), (D_MODEL, BF16, "row"), (D_MODEL, F32, "acc"), (LANES, F32, "acc")],
                    tm=512, name="loss_bwd")


def _mid_bwd(dh2, x1, mix, g_pre_ffn, g_post_mix, dy):
    def fn(dh2, x1, mix, g3, g2, dy):
        d3, dg3 = _norm_bwd(dh2, x1, g3)
        dx1 = dy + d3
        dmix, dg2 = _norm_bwd(dx1, mix, g2)
        return dx1, dmix, dg3, dg2

    return _rowwise(fn, [(dh2, "row"), (x1, "row"), (mix, "row"), (g_pre_ffn, "vec"), (g_post_mix, "vec"), (dy, "row")],
                    [(D_MODEL, F32, "row"), (D_MODEL, BF16, "row"), (D_MODEL, F32, "acc"), (D_MODEL, F32, "acc")],
                    tm=256, name="mid_bwd")


def _first_bwd(dh, x, g_pre_mix, dx1):
    def fn(dh, x, g1, dx1):
        d1, dg1 = _norm_bwd(dh, x, g1)
        return dx1 + d1, dg1

    return _rowwise(fn, [(dh, "row"), (x, "row"), (g_pre_mix, "vec"), (dx1, "row")],
                    [(D_MODEL, F32, "row"), (D_MODEL, F32, "acc")], tm=512, name="first_bwd")


def _mem_norm_bwd(dhm, mem, g_mem):
    def fn(dhm, mem, g):
        return (jnp.sum(dhm * (mem * _rstd(mem)), axis=0, keepdims=True),)

    return _rowwise(fn, [(dhm, "row"), (mem, "row"), (g_mem, "vec")], [(D_MODEL, F32, "acc")], tm=512, name="mem_norm_bwd")[0]


def _gate_bwd(dmerged, gates, ya, yb, yc):
    def fn(dm, gt, ya, yb, yc):
        gt = gt.astype(F32)
        outs, dgp = [], []
        for i, y in enumerate((ya, yb, yc)):
            gi = gt[:, i * D_MODEL:(i + 1) * D_MODEL]
            outs.append(dm * gi)
            dgp.append(dm * y.astype(F32) * gi * (1.0 - gi))
        dgpre = jnp.concatenate(dgp, axis=1)
        return outs[0], outs[1], outs[2], dgpre, jnp.sum(dgpre, axis=0, keepdims=True)

    return _rowwise(fn, [(dmerged, "row"), (gates, "row"), (ya, "row"), (yb, "row"), (yc, "row")],
                    [(D_MODEL, BF16, "row")] * 3 + [(3 * D_MODEL, BF16, "row"), (3 * D_MODEL, F32, "acc")],
                    tm=256, name="gate_bwd")


def _adamw_math(w, g, m, v):
    m = ADAM_B1 * m + (1.0 - ADAM_B1) * g
    v = ADAM_B2 * v + (1.0 - ADAM_B2) * (g * g)
    m_hat = m / (1.0 - ADAM_B1 ** ADAM_STEP)
    v_hat = v / (1.0 - ADAM_B2 ** ADAM_STEP)
    delta = -ADAM_LR * (m_hat / (jnp.sqrt(v_hat) + ADAM_EPS) + ADAM_WD * w)
    return delta, m, v


def _adamw(w, g, m, v, *, tm, name):
    c = w.shape[1]
    return _rowwise(_adamw_math, [(w, "row"), (g, "row"), (m, "row"), (v, "row")], [(c, F32, "row")] * 3, tm=tm, name=name)


def _adamw_halves(w, g_mine, g_theirs, m, v, c_arr, *, name):
    a, b = w.shape
    hf = a // 2
    tr = hf // 4

    def body(c_ref, w_ref, gm_ref, gt_ref, m_ref, v_ref, g_out, d_out, m_out, v_out):
        g = jnp.where(pl.program_id(0) == c_ref[0], gm_ref[...], gt_ref[...])
        d, m_new, v_new = _adamw_math(w_ref[...], g, m_ref[...], v_ref[...])
        g_out[...] = g
        d_out[...] = d
        m_out[...] = m_new
        v_out[...] = v_new

    full = pl.BlockSpec((tr, b), lambda hh, i, c_ref: (hh * (hf // tr) + i, 0))
    half = pl.BlockSpec((tr, b), lambda hh, i, c_ref: (i, 0))
    return pl.pallas_call(
        body,
        grid_spec=pltpu.PrefetchScalarGridSpec(
            num_scalar_prefetch=1,
            grid=(2, hf // tr),
            in_specs=[full, half, half, full, full],
            out_specs=[full] * 4,
        ),
        out_shape=[jax.ShapeDtypeStruct((a, b), F32)] * 4,
        compiler_params=pltpu.CompilerParams(dimension_semantics=("parallel", "parallel")),
        name=name,
    )(c_arr, w, g_mine, g_theirs, m, v)


def _ffn_in_fwd(h2, w_ffn):
    m, tm, tn = h2.shape[0], 512, w_ffn.shape[2]
    assert 2 * tn == D_FF

    def body(h_ref, wg_ref, wu_ref, g_ref, u_ref, f_ref):
        h = h_ref[...]
        g = _dot(h, wg_ref[...])
        u = _dot(h, wu_ref[...])
        g_ref[...] = g.astype(BF16)
        u_ref[...] = u.astype(BF16)
        f_ref[...] = (g * _sigmoid(g) * u).astype(BF16)

    o_spec = pl.BlockSpec((tm, tn), lambda j, i: (i, j))
    return pl.pallas_call(
        body,
        grid=(D_FF // tn, m // tm),
        in_specs=[pl.BlockSpec((tm, D_MODEL), lambda j, i: (i, 0)),
                  pl.BlockSpec((None, D_MODEL, tn), lambda j, i: (j, 0, 0)),
                  pl.BlockSpec((None, D_MODEL, tn), lambda j, i: (j + 2, 0, 0))],
        out_specs=[o_spec, o_spec, o_spec],
        out_shape=[jax.ShapeDtypeStruct((m, D_FF), BF16)] * 3,
        compiler_params=pltpu.CompilerParams(dimension_semantics=("parallel", "parallel")),
        name="ffn_in_fwd",
    )(h2, w_ffn, w_ffn)


def _join_shards(w4):
    _, rows, cols = w4.shape
    tr = rows // 4

    def body(w_ref, o_ref):
        for s in range(N_CHIPS):
            o_ref[:, s * cols:(s + 1) * cols] = w_ref[s]

    return pl.pallas_call(
        body,
        grid=(rows // tr,),
        in_specs=[pl.BlockSpec((N_CHIPS, tr, cols), lambda i: (0, i, 0))],
        out_specs=pl.BlockSpec((tr, N_CHIPS * cols), lambda i: (i, 0)),
        out_shape=jax.ShapeDtypeStruct((rows, N_CHIPS * cols), w4.dtype),
        compiler_params=pltpu.CompilerParams(dimension_semantics=("parallel",)),
        name="join_shards",
    )(w4)


def _split_to_shards(pieces, *, name):
    t = pieces[0].shape[0]
    widths = [p.shape[1] for p in pieces]
    cols = sum(widths) // N_CHIPS
    tm = 512
    plan, start = [], 0
    for p, wd in enumerate(widths):
        for s in range(N_CHIPS):
            lo, hi = max(start, s * cols), min(start + wd, (s + 1) * cols)
            if lo < hi:
                plan.append((s, p, lo - s * cols, hi - s * cols, lo - start, hi - start))
        start += wd

    def body(*refs):
        o_ref = refs[-1]
        for s, p, o_lo, o_hi, p_lo, p_hi in plan:
            o_ref[s, :, o_lo:o_hi] = refs[p][:, p_lo:p_hi]

    return pl.pallas_call(
        body,
        grid=(t // tm,),
        in_specs=[pl.BlockSpec((tm, wd), lambda i: (i, 0)) for wd in widths],
        out_specs=pl.BlockSpec((N_CHIPS, tm, cols), lambda i: (0, i, 0)),
        out_shape=jax.ShapeDtypeStruct((N_CHIPS, t, cols), pieces[0].dtype),
        compiler_params=pltpu.CompilerParams(dimension_semantics=("parallel",)),
        name=name,
    )(*pieces)


def _swiglu_bwd_epilogue(df, g, u):
    g = g.astype(F32)
    u = u.astype(F32)
    sg = _sigmoid(g)
    return df * u * (sg * (1.0 + g * (1.0 - sg))), df * (g * sg)


def _branch_merge_fwd(o_a, o_b, o_c, w_sb, w_dil, w_mem, gates):
    m, tm = o_a.shape[0], 256

    def body(oa_ref, ob_ref, oc_ref, wa_ref, wb_ref, wc_ref, gt_ref, ya_ref, yb_ref, yc_ref, mg_ref):
        def project(o_ref, w_ref):
            o = o_ref[...]
            return jnp.concatenate([_dot(o, w_ref[s]) for s in range(N_CHIPS)], axis=1)

        ya = project(oa_ref, wa_ref)
        yb = project(ob_ref, wb_ref)
        yc = project(oc_ref, wc_ref)
        gt = gt_ref[...].astype(F32)
        ya_ref[...] = ya.astype(BF16)
        yb_ref[...] = yb.astype(BF16)
        yc_ref[...] = yc.astype(BF16)
        mg_ref[...] = (gt[:, :D_MODEL] * ya + gt[:, D_MODEL:2 * D_MODEL] * yb + gt[:, 2 * D_MODEL:] * yc).astype(BF16)

    row = lambda c: pl.BlockSpec((tm, c), lambda i: (i, 0))
    full = lambda a: pl.BlockSpec(a.shape, lambda i: (0, 0, 0))
    return pl.pallas_call(
        body,
        grid=(m // tm,),
        in_specs=[row(SB_W), row(DIL_W), row(MEM_W), full(w_sb), full(w_dil), full(w_mem), row(3 * D_MODEL)],
        out_specs=[row(D_MODEL)] * 4,
        out_shape=[jax.ShapeDtypeStruct((m, D_MODEL), BF16)] * 4,
        compiler_params=pltpu.CompilerParams(dimension_semantics=("parallel",)),
        name="branch_merge_fwd",
    )(o_a, o_b, o_c, w_sb, w_dil, w_mem, gates)


SB_T = 256
SB_SCALE = HEAD_DIM ** -0.5


def _sb_masks():
    row = lax.broadcasted_iota(jnp.int32, (SB_T, SB_T), 0)
    col = lax.broadcasted_iota(jnp.int32, (SB_T, SB_T), 1)
    lane = lax.broadcasted_iota(jnp.int32, (1, LANES), 1)
    return row, col, lane


def _sb_logs(z):
    lb = jnp.minimum(z, 0.0) - jnp.log(1.0 + jnp.exp(-jnp.abs(z)))
    return lb, lb - z


def _sb_specs(n_heads_pairs, col0):
    q = pl.BlockSpec((None, SB_T, LANES), lambda b, p, i: (b, i, col0 + p))
    k = pl.BlockSpec((None, SEQ, LANES), lambda b, p, i: (b, 0, col0 + n_heads_pairs + p))
    v = pl.BlockSpec((None, SEQ, LANES), lambda b, p, i: (b, 0, col0 + 2 * n_heads_pairs + p))
    return q, k, v


def _grid_step(n_pairs, nq):
    return (pl.program_id(0) * n_pairs + pl.program_id(1)) * nq + pl.program_id(2)


def _sb_fwd(proj3, late_shards):
    bl = proj3.shape[0]
    n_pairs = SB_W // LANES
    nq = SEQ // SB_T
    n_late = len(late_shards)
    n_steps = bl * n_pairs * nq

    def body(q_ref, k_ref, v_ref, *rest):
        late_in, (o_ref, o32_ref, w_ref), late_out = rest[:n_late], rest[n_late:n_late + 3], rest[n_late + 3:2 * n_late + 3]
        step = _grid_step(n_pairs, nq)
        if n_late:
            send, forward, finish = _gather_phases(late_in, late_out, *rest[2 * n_late + 3:])
            pl.when(step == 0)(send)
            pl.when(step == n_steps // 2)(forward)
        i = pl.program_id(2)
        row, col, lane = _sb_masks()
        causal = col < row
        u_excl = (row > col).astype(BF16)
        q = q_ref[...]
        heads = []
        for h in range(2):
            mh = (lane // HEAD_DIM) == h
            heads.append((mh, jnp.where(mh, q, jnp.zeros_like(q)) * SB_SCALE))

        def blocks(js, diags, carries, acc):
            ks = [k_ref[pl.ds(pl.multiple_of(j * SB_T, SB_T), SB_T), :] for j in js]
            vs = [v_ref[pl.ds(pl.multiple_of(j * SB_T, SB_T), SB_T), :] for j in js]
            chains = [(b, h) for b in range(len(js)) for h in range(2)]
            z = {c: _dot_nt(heads[c[1]][1], ks[c[0]]) for c in chains}
            lb, lk = {}, {}
            for c in chains:
                lb[c], lk[c] = _sb_logs(z[c])
                if diags[c[0]]:
                    lk[c] = jnp.where(causal, lk[c], 0.0)
            r = {c: _split_dot(lk[c], u_excl) for c in chains}
            carries = list(carries)
            w = {}
            for b, h in chains:
                w_c = jnp.exp(lb[b, h] + r[b, h] + carries[h])
                w[b, h] = (jnp.where(causal, w_c, 0.0) if diags[b] else w_c).astype(BF16)
                w_ref[h, js[b]] = w[b, h]
                carries[h] = carries[h] + (r[b, h][:, :1] + lk[b, h][:, :1])
            for b, h in chains:
                acc = acc + _dot(w[b, h], jnp.where(heads[h][0], vs[b], jnp.zeros_like(vs[b])))
            return tuple(carries), acc

        zero = jnp.zeros((SB_T, 1), F32)
        init = ((zero, zero), jnp.zeros((SB_T, LANES), F32))
        odd = i % 2
        carries, acc = lax.cond(odd == 1, lambda: blocks([i, i - 1], (True, False), *init), lambda: blocks([i], (True,), *init))
        rest = i - 1 - odd
        carries, acc = lax.fori_loop(
            0, i // 2, lambda jj, c: blocks([rest - 2 * jj, rest - 1 - 2 * jj], (False, False), c[0], c[1]), (carries, acc))
        o_ref[...] = acc.astype(BF16)
        o32_ref[...] = acc
        if n_late:
            pl.when(step == n_steps - 1)(finish)

    q_spec, k_spec, v_spec = _sb_specs(n_pairs, 0)
    blk = pl.BlockSpec((None, SB_T, LANES), lambda b, p, i: (b, i, p))
    out = pl.pallas_call(
        body,
        grid=(bl, n_pairs, nq),
        in_specs=[q_spec, k_spec, v_spec] + [ANY] * n_late,
        out_specs=[blk, blk, _sb_weight_spec(nq)] + [ANY] * n_late,
        out_shape=[jax.ShapeDtypeStruct((bl, SEQ, SB_W), BF16), jax.ShapeDtypeStruct((bl, SEQ, SB_W), F32),
                   jax.ShapeDtypeStruct((bl, n_pairs, nq, 2, nq, SB_T, SB_T), BF16)] + _gather_out_shapes(late_shards),
        scratch_shapes=_gather_sems(n_late) if n_late else [],
        compiler_params=pltpu.CompilerParams(dimension_semantics=("arbitrary", "arbitrary", "arbitrary")),
        name="sb_fwd",
    )(proj3, proj3, proj3, *late_shards)
    return out[0], out[1], out[2], out[3:]


def _sb_weight_spec(nq):
    return pl.BlockSpec((None, None, None, 2, nq, SB_T, SB_T), lambda b, p, i: (b, p, i, 0, 0, 0, 0))


def _sb_bwd(proj3, o_a, do_a, w_all, parts):
    bl = proj3.shape[0]
    n_pairs = SB_W // LANES
    nq = SEQ // SB_T
    n_parts = len(parts)
    n_steps = bl * n_pairs * nq

    def body(q_ref, k_ref, v_ref, o_ref, do_ref, w_ref, *rest):
        p_refs, (dq_ref, dk_ref, dv_ref), land_refs = rest[:n_parts], rest[n_parts:n_parts + 3], rest[n_parts + 3:2 * n_parts + 3]
        dk_acc, dv_acc = rest[2 * n_parts + 3:2 * n_parts + 5]
        step = _grid_step(n_pairs, nq)
        if n_parts:
            send, finish = _chip_exchange_phases(p_refs, land_refs, *rest[2 * n_parts + 5:])
            pl.when(step == 0)(send)
        i = pl.program_id(2)

        @pl.when(i == 0)
        def _():
            dk_acc[...] = jnp.zeros_like(dk_acc)
            dv_acc[...] = jnp.zeros_like(dv_acc)

        row, col, lane = _sb_masks()
        causal = col < row
        u_incl = (row >= col).astype(BF16)
        q = q_ref[...]
        do = do_ref[...]
        prod = do.astype(F32) * o_ref[...]
        heads = []
        for h in range(2):
            mh = (lane // HEAD_DIM) == h
            d_tot = jnp.sum(jnp.where(mh, prod, 0.0), axis=1, keepdims=True)
            heads.append((mh, jnp.where(mh, q, jnp.zeros_like(q)) * SB_SCALE, jnp.where(mh, do, jnp.zeros_like(do)), d_tot))

        def blocks(js, diags, c_das, dq):
            starts = [pl.multiple_of(j * SB_T, SB_T) for j in js]
            ks = [k_ref[pl.ds(s, SB_T), :] for s in starts]
            vs = [v_ref[pl.ds(s, SB_T), :] for s in starts]
            chains = [(b, h) for b in range(len(js)) for h in range(2)]
            z = {c: _dot_nt(heads[c[1]][1], ks[c[0]]) for c in chains}
            dw = {c: _dot_nt(heads[c[1]][2], vs[c[0]]) for c in chains}
            wb = {(b, h): w_ref[h, js[b]] for b, h in chains}
            da = {c: dw[c] * wb[c].astype(F32) for c in chains}
            sfx = {c: _split_dot(da[c], u_incl) for c in chains}
            c_das = list(c_das)
            dz = {}
            for b, h in chains:
                dlk = heads[h][3] - c_das[h] - sfx[b, h]
                if diags[b]:
                    dlk = jnp.where(causal, dlk, 0.0)
                c_das[h] = c_das[h] + sfx[b, h][:, :1]
                e = jnp.exp(-jnp.abs(z[b, h]))
                inv = 1.0 / (1.0 + e)
                pos = z[b, h] >= 0.0
                beta = jnp.where(pos, inv, e * inv)
                one_m_beta = jnp.where(pos, e * inv, inv)
                dz[b, h] = (da[b, h] * one_m_beta - dlk * beta).astype(BF16)
            for b, h in chains:
                dq = dq + _dot(dz[b, h], jnp.where(heads[h][0], ks[b], jnp.zeros_like(ks[b])))
            for b in range(len(js)):
                dk_acc[pl.ds(starts[b], SB_T), :] += _dot_tn(dz[b, 0], heads[0][1]) + _dot_tn(dz[b, 1], heads[1][1])
                dv_acc[pl.ds(starts[b], SB_T), :] += _dot_tn(wb[b, 0], heads[0][2]) + _dot_tn(wb[b, 1], heads[1][2])
            return tuple(c_das), dq

        zero = jnp.zeros((SB_T, 1), F32)
        init = ((zero, zero), jnp.zeros((SB_T, LANES), F32))
        odd = i % 2
        state = lax.cond(odd == 1, lambda: blocks([i, i - 1], (True, False), *init), lambda: blocks([i], (True,), *init))
        rest = i - 1 - odd
        state = lax.fori_loop(0, i // 2, lambda jj, c: blocks([rest - 2 * jj, rest - 1 - 2 * jj], (False, False), c[0], c[1]), state)
        dq_ref[...] = (state[1] * SB_SCALE).astype(BF16)

        @pl.when(i == nq - 1)
        def _():
            dk_ref[...] = dk_acc[...].astype(BF16)
            dv_ref[...] = dv_acc[...].astype(BF16)

        if n_parts:
            pl.when(step == n_steps - 1)(finish)

    q_spec, k_spec, v_spec = _sb_specs(n_pairs, 0)
    blk = pl.BlockSpec((None, SB_T, LANES), lambda b, p, i: (b, i, p))
    seq = pl.BlockSpec((None, SEQ, LANES), lambda b, p, i: (b, 0, p))
    shape = jax.ShapeDtypeStruct((bl, SEQ, SB_W), BF16)
    out = pl.pallas_call(
        body,
        grid=(bl, n_pairs, nq),
        in_specs=[q_spec, k_spec, v_spec, blk, blk, _sb_weight_spec(nq)] + [ANY] * n_parts,
        out_specs=[blk, seq, seq] + [ANY] * n_parts,
        out_shape=[shape, shape, shape] + [jax.ShapeDtypeStruct(p.shape, p.dtype) for p in parts],
        scratch_shapes=[pltpu.VMEM((SEQ, LANES), F32), pltpu.VMEM((SEQ, LANES), F32)]
        + (_chip_exchange_sems(n_parts) if n_parts else []),
        compiler_params=pltpu.CompilerParams(dimension_semantics=("arbitrary", "arbitrary", "arbitrary")),
        name="sb_bwd",
    )(proj3, proj3, proj3, o_a, do_a, w_all, *parts)
    return out[0], out[1], out[2], out[3:]


BAND = 128


BAND_CH = 4
BAND_HEADS = DIL_W // HEAD_DIM


def _swap_half(x):
    n = x.shape[-1]
    lane = lax.broadcasted_iota(jnp.int32, (1, n), 1)
    return jnp.where((lane % HEAD_DIM) < HEAD_DIM // 2, pltpu.roll(x, n - HEAD_DIM // 2, 1), pltpu.roll(x, HEAD_DIM // 2, 1))


def _rope(x, cos, sin_signed):
    x = x.astype(F32)
    return x * cos + _swap_half(x) * sin_signed


def _band_valid(g, blk):
    nb = jnp.where(g == 0, 16, jnp.where(g == 1, 4, 1))
    first_key = jnp.where(lax.rem(blk, nb) != 0, 0, BAND)
    qi = lax.broadcasted_iota(jnp.int32, (BAND, 2 * BAND), 0) + BAND
    kj = lax.broadcasted_iota(jnp.int32, (BAND, 2 * BAND), 1)
    dist = qi - kj
    return (dist >= 0) & (dist <= BAND) & (kj >= first_key)


def _band_specs():
    last_before = lambda i: jnp.maximum(i * BAND_CH - 1, 0)
    cur = lambda col: pl.BlockSpec((None, BAND_CH, BAND, DIL_W), lambda g, i: (g, i, 0, col))
    prev = lambda col: pl.BlockSpec((None, None, BAND, DIL_W), lambda g, i: (g, last_before(i), 0, col))
    tab = pl.BlockSpec((None, BAND_CH, BAND, DIL_W), lambda g, i: (g, lax.rem(i, 16 // BAND_CH), 0, 0))
    tab_prev = pl.BlockSpec((None, None, BAND, DIL_W), lambda g, i: (g, lax.rem(last_before(i), 16), 0, 0))
    return cur, prev, tab, tab_prev


def _band_load(q_ref, k_ref, kp_ref, v_ref, vp_ref, c_ref, s_ref, cp_ref, sp_ref):
    qs = [(_rope(q_ref[b], c_ref[b], s_ref[b]) * SB_SCALE).astype(BF16) for b in range(BAND_CH)]
    ks = [_rope(kp_ref[...], cp_ref[...], sp_ref[...]).astype(BF16)]
    ks += [_rope(k_ref[b], c_ref[b], s_ref[b]).astype(BF16) for b in range(BAND_CH)]
    vs = [vp_ref[...]] + [v_ref[b] for b in range(BAND_CH)]
    k2 = [jnp.concatenate([ks[b], ks[b + 1]], axis=0) for b in range(BAND_CH)]
    v2 = [jnp.concatenate([vs[b], vs[b + 1]], axis=0) for b in range(BAND_CH)]
    return qs, k2, v2


def _band_fwd(qkv_s, cos_t, sin_t):
    def body(q_ref, k_ref, kp_ref, v_ref, vp_ref, c_ref, s_ref, cp_ref, sp_ref, ol_ref):
        g, i = pl.program_id(0), pl.program_id(1)
        qs, k2, v2 = _band_load(q_ref, k_ref, kp_ref, v_ref, vp_ref, c_ref, s_ref, cp_ref, sp_ref)
        lane = lax.broadcasted_iota(jnp.int32, (1, DIL_W), 1)
        for b in range(BAND_CH):
            valid = _band_valid(g, i * BAND_CH + b)
            hs = range(BAND_HEADS)
            mh = [(lane // HEAD_DIM) == h for h in hs]
            s = [jnp.where(valid, _dot_nt(jnp.where(mh[h], qs[b], jnp.zeros_like(qs[b])), k2[b]), NEG_INF) for h in hs]
            m = [jnp.max(s[h], axis=1, keepdims=True) for h in hs]
            p = [jnp.exp(s[h] - m[h]) for h in hs]
            den = [jnp.sum(p[h], axis=1, keepdims=True) for h in hs]
            pv = [_dot(p[h].astype(BF16), jnp.where(mh[h], v2[b], jnp.zeros_like(v2[b]))) for h in hs]
            o = jnp.zeros((BAND, DIL_W), F32)
            lse = jnp.zeros((BAND, DIL_W), F32)
            for h in hs:
                o = o + pv[h] * (1.0 / den[h])
                lse = jnp.where(mh[h], m[h] + jnp.log(den[h]), lse)
            ol_ref[b, :, :DIL_W] = o
            ol_ref[b, :, DIL_W:] = lse

    cur, prev, tab, tab_prev = _band_specs()
    n_blk = qkv_s.shape[1]
    return pl.pallas_call(
        body,
        grid=(3, n_blk // BAND_CH),
        in_specs=[cur(0), cur(1), prev(1), cur(2), prev(2), tab, tab, tab_prev, tab_prev],
        out_specs=pl.BlockSpec((None, BAND_CH, BAND, 2 * DIL_W), lambda g, i: (g, i, 0, 0)),
        out_shape=jax.ShapeDtypeStruct((3, n_blk, BAND, 2 * DIL_W), F32),
        compiler_params=pltpu.CompilerParams(dimension_semantics=("parallel", "parallel")),
        name="band_fwd",
    )(qkv_s, qkv_s, qkv_s, qkv_s, qkv_s, cos_t, sin_t, cos_t, sin_t)


def _band_bwd(qkv_s, cos_t, sin_t, dcat_s):
    def body(q_ref, k_ref, kp_ref, v_ref, vp_ref, c_ref, s_ref, cp_ref, sp_ref, do_ref, lse_ref, dl_ref,
             dq_ref, dk_ref, dv_ref, dkf_ref, dvf_ref):
        g, i = pl.program_id(0), pl.program_id(1)
        qs, k2, v2 = _band_load(q_ref, k_ref, kp_ref, v_ref, vp_ref, c_ref, s_ref, cp_ref, sp_ref)
        lane = lax.broadcasted_iota(jnp.int32, (1, DIL_W), 1)
        dks, dvs = [], []
        for b in range(BAND_CH):
            valid = _band_valid(g, i * BAND_CH + b)
            do, lse, dl = do_ref[b].astype(BF16), lse_ref[b], dl_ref[b]
            hs = range(BAND_HEADS)
            mh = [(lane // HEAD_DIM) == h for h in hs]
            qh = [jnp.where(mh[h], qs[b], jnp.zeros_like(qs[b])) for h in hs]
            doh = [jnp.where(mh[h], do, jnp.zeros_like(do)) for h in hs]
            s = [_dot_nt(qh[h], k2[b]) for h in hs]
            dp = [_dot_nt(doh[h], v2[b]) for h in hs]
            p = [jnp.where(valid, jnp.exp(s[h] - lse[:, h * HEAD_DIM:h * HEAD_DIM + 1]), 0.0) for h in hs]
            ds = [(p[h] * (dp[h] - dl[:, h * HEAD_DIM:h * HEAD_DIM + 1])).astype(BF16) for h in hs]
            pb = [p[h].astype(BF16) for h in hs]
            dq = sum(_dot(ds[h], jnp.where(mh[h], k2[b], jnp.zeros_like(k2[b]))) for h in hs)
            dk2 = sum(_dot_tn(ds[h], qh[h]) for h in hs)
            dv2 = sum(_dot_tn(pb[h], doh[h]) for h in hs)
            dq_ref[b] = dq * SB_SCALE
            dks.append(dk2)
            dvs.append(dv2)
        dkf_ref[...] = dks[0][:BAND]
        dvf_ref[...] = dvs[0][:BAND]
        for b in range(BAND_CH):
            last = b == BAND_CH - 1
            dk_ref[b] = dks[b][BAND:] if last else dks[b][BAND:] + dks[b + 1][:BAND]
            dv_ref[b] = dvs[b][BAND:] if last else dvs[b][BAND:] + dvs[b + 1][:BAND]

    cur, prev, tab, tab_prev = _band_specs()
    first = pl.BlockSpec((None, None, BAND, DIL_W), lambda g, i: (g, i, 0, 0))
    n_blk = qkv_s.shape[1]
    n_chunks = n_blk // BAND_CH
    shape = jax.ShapeDtypeStruct((3, n_blk, BAND, DIL_W), F32)
    shape_first = jax.ShapeDtypeStruct((3, n_chunks, BAND, DIL_W), F32)
    return pl.pallas_call(
        body,
        grid=(3, n_chunks),
        in_specs=[cur(0), cur(1), prev(1), cur(2), prev(2), tab, tab, tab_prev, tab_prev, cur(0), cur(1), cur(2)],
        out_specs=[cur(0), cur(0), cur(0), first, first],
        out_shape=[shape, shape, shape, shape_first, shape_first],
        compiler_params=pltpu.CompilerParams(dimension_semantics=("parallel", "parallel")),
        name="band_bwd",
    )(qkv_s, qkv_s, qkv_s, qkv_s, qkv_s, cos_t, sin_t, cos_t, sin_t, dcat_s, dcat_s, dcat_s)


def _band_combine(dq, dk, dv, dk_first, dv_first, cos_t, sin_t):
    n_chunks = dk_first.shape[1]

    def body(dq_ref, dk_ref, dkn_ref, dv_ref, dvn_ref, c_ref, s_ref, out_ref):
        nxt = (pl.program_id(1) < n_chunks - 1).astype(F32)
        for b in range(BAND_CH):
            cos, sin = c_ref[b], s_ref[b]
            dq_b, dk_b, dv_b = dq_ref[b], dk_ref[b], dv_ref[b]
            if b == BAND_CH - 1:
                dk_b = dk_b + nxt * dkn_ref[...]
                dv_b = dv_b + nxt * dvn_ref[...]
            out_ref[b, :, :DIL_W] = (dq_b * cos - _swap_half(dq_b) * sin).astype(BF16)
            out_ref[b, :, DIL_W:2 * DIL_W] = (dk_b * cos - _swap_half(dk_b) * sin).astype(BF16)
            out_ref[b, :, 2 * DIL_W:] = dv_b.astype(BF16)

    cur, _, tab, _ = _band_specs()
    nxt = pl.BlockSpec((None, None, BAND, DIL_W), lambda g, i: (g, jnp.minimum(i + 1, n_chunks - 1), 0, 0))
    return pl.pallas_call(
        body,
        grid=(3, n_chunks),
        in_specs=[cur(0), cur(0), nxt, cur(0), nxt, tab, tab],
        out_specs=pl.BlockSpec((None, BAND_CH, BAND, 3 * DIL_W), lambda g, i: (g, i, 0, 0)),
        out_shape=jax.ShapeDtypeStruct(dq.shape[:3] + (3 * DIL_W,), BF16),
        compiler_params=pltpu.CompilerParams(dimension_semantics=("parallel", "parallel")),
        name="band_combine",
    )(dq, dk, dk_first, dv, dv_first, cos_t, sin_t)


def _band_merge(ol):
    t, tm = ol.shape[1], 512

    def body(o_ref, l_ref, ob_ref, lse_ref):
        l0, l1, l2 = l_ref[0], l_ref[1], l_ref[2]
        m = jnp.maximum(jnp.maximum(l0, l1), l2)
        lse = m + jnp.log(jnp.exp(l0 - m) + jnp.exp(l1 - m) + jnp.exp(l2 - m))
        ob_ref[...] = (jnp.exp(l0 - lse) * o_ref[0] + jnp.exp(l1 - lse) * o_ref[1] + jnp.exp(l2 - lse) * o_ref[2]).astype(BF16)
        lse_ref[...] = lse

    spec = pl.BlockSpec((tm, DIL_W), lambda i: (i, 0))
    return pl.pallas_call(
        body,
        grid=(t // tm,),
        in_specs=[pl.BlockSpec((3, tm, DIL_W), lambda i: (0, i, 0)), pl.BlockSpec((3, tm, DIL_W), lambda i: (0, i, 1))],
        out_specs=[spec, spec],
        out_shape=[jax.ShapeDtypeStruct((t, DIL_W), BF16), jax.ShapeDtypeStruct((t, DIL_W), F32)],
        compiler_params=pltpu.CompilerParams(dimension_semantics=("parallel",)),
        name="band_merge",
    )(ol, ol)


def _band_delta(do_b, o_b, lse_b):
    def fn(do, o, lse):
        r = lax.broadcasted_iota(jnp.int32, (DIL_W, DIL_W), 0) // HEAD_DIM
        c = lax.broadcasted_iota(jnp.int32, (DIL_W, DIL_W), 1) // HEAD_DIM
        do = do.astype(F32)
        delta = _split_dot(do * o.astype(F32), (r == c).astype(BF16))
        return (jnp.concatenate([do, lse, delta], axis=1),)

    return _rowwise(fn, [(do_b, "row"), (o_b, "row"), (lse_b, "row")], [(3 * DIL_W, F32, "row")], tm=512, name="band_delta")[0]


def _band_masks():
    qi = lax.broadcasted_iota(jnp.int32, (BAND, 2 * BAND), 0) + BAND
    kj = lax.broadcasted_iota(jnp.int32, (BAND, 2 * BAND), 1)
    dist = qi - kj
    row = lax.broadcasted_iota(jnp.int32, (BAND, BAND), 0)
    col = lax.broadcasted_iota(jnp.int32, (BAND, BAND), 1)
    return col <= row, (dist >= 0) & (dist <= BAND)


def _band_attend(q, k, v, valid):
    lane = lax.broadcasted_iota(jnp.int32, (1, DIL_W), 1)
    hs = range(BAND_HEADS)
    mh = [(lane // HEAD_DIM) == h for h in hs]
    s = [jnp.where(valid, _dot_nt(jnp.where(mh[h], q, jnp.zeros_like(q)), k), NEG_INF) for h in hs]
    m = [jnp.max(s[h], axis=1, keepdims=True) for h in hs]
    p = [jnp.exp(s[h] - m[h]) for h in hs]
    den = [jnp.sum(p[h], axis=1, keepdims=True) for h in hs]
    pv = [_dot(p[h].astype(BF16), jnp.where(mh[h], v, jnp.zeros_like(v))) for h in hs]
    o = jnp.zeros((BAND, DIL_W), F32)
    lse = jnp.zeros((BAND, DIL_W), F32)
    for h in hs:
        o = o + pv[h] * (1.0 / den[h])
        lse = jnp.where(mh[h], m[h] + jnp.log(den[h]), lse)
    return o, lse


def _band_attend_bwd(q, k, v, valid, do, lse, dl):
    lane = lax.broadcasted_iota(jnp.int32, (1, DIL_W), 1)
    hs = range(BAND_HEADS)
    mh = [(lane // HEAD_DIM) == h for h in hs]
    qh = [jnp.where(mh[h], q, jnp.zeros_like(q)) for h in hs]
    doh = [jnp.where(mh[h], do, jnp.zeros_like(do)) for h in hs]
    s = [_dot_nt(qh[h], k) for h in hs]
    dp = [_dot_nt(doh[h], v) for h in hs]
    p = [jnp.where(valid, jnp.exp(s[h] - lse[:, h * HEAD_DIM:h * HEAD_DIM + 1]), 0.0) for h in hs]
    ds = [(p[h] * (dp[h] - dl[:, h * HEAD_DIM:h * HEAD_DIM + 1])).astype(BF16) for h in hs]
    pb = [p[h].astype(BF16) for h in hs]
    dq = sum(_dot(ds[h], jnp.where(mh[h], k, jnp.zeros_like(k))) for h in hs)
    dk = sum(_dot_tn(ds[h], qh[h]) for h in hs)
    dv = sum(_dot_tn(pb[h], doh[h]) for h in hs)
    return dq, dk, dv


def _band_group_specs(lead, rows, cls, col0):
    def spec(width):
        if lead == "rows":
            return pl.BlockSpec((None, rows, width), lambda b, i: (b, 0, col0))
        return pl.BlockSpec((None, rows, cls * width), lambda b, i: (b, 0, i))
    return spec


def _band_group_fwd(a, cos_g, sin_g, *, rows, cls, steps, col0, name):
    bl = a.shape[0]
    nb = rows // BAND
    grp_w = 3 * DIL_W

    def body(a_ref, c_ref, s_ref, ol_ref, qr, kr):
        first_valid, later_valid = _band_masks()
        for j in range(cls):
            a0, t0, o0 = j * grp_w, j * DIL_W, j * 2 * DIL_W
            cos, sin = c_ref[:, t0:t0 + DIL_W], s_ref[:, t0:t0 + DIL_W]
            qr[...] = (_rope(a_ref[:, a0:a0 + DIL_W], cos, sin) * SB_SCALE).astype(BF16)
            kr[...] = _rope(a_ref[:, a0 + DIL_W:a0 + 2 * DIL_W], cos, sin).astype(BF16)

            def block(q0, k0, keys, valid, a0=a0, o0=o0):
                o, lse = _band_attend(qr[pl.ds(q0, BAND), :], kr[pl.ds(k0, keys), :],
                                      a_ref[pl.ds(k0, keys), a0 + 2 * DIL_W:a0 + grp_w], valid)
                ol_ref[pl.ds(q0, BAND), o0:o0 + DIL_W] = o
                ol_ref[pl.ds(q0, BAND), o0 + DIL_W:o0 + 2 * DIL_W] = lse

            block(0, 0, BAND, first_valid)
            if nb > 1:
                def later(b, carry, block=block):
                    block(pl.multiple_of(b * BAND, BAND), pl.multiple_of((b - 1) * BAND, BAND), 2 * BAND, later_valid)
                    return carry

                lax.fori_loop(1, nb, later, 0)

    lead = "rows" if col0 is not None else "cols"
    spec = _band_group_specs(lead, rows, cls, col0)
    tab = pl.BlockSpec((rows, cls * DIL_W), lambda b, i: (0, i))
    n_cls = cos_g.shape[1] // DIL_W
    return pl.pallas_call(
        body,
        grid=(bl, steps),
        in_specs=[spec(grp_w), tab, tab],
        out_specs=pl.BlockSpec((None, rows, cls * 2 * DIL_W), lambda b, i: (b, 0, i)),
        out_shape=jax.ShapeDtypeStruct((bl, rows, n_cls * 2 * DIL_W), F32),
        scratch_shapes=[pltpu.VMEM((rows, DIL_W), BF16), pltpu.VMEM((rows, DIL_W), BF16)],
        compiler_params=pltpu.CompilerParams(dimension_semantics=("parallel", "parallel")),
        name=name,
    )(a, cos_g, sin_g)


def _band_group_bwd(a, d, cos_g, sin_g, *, rows, cls, steps, col0, name, side=None):
    bl = a.shape[0]
    nb = rows // BAND
    grp_w = 3 * DIL_W
    side_arrays = side[1] if side else []
    n_side = len(side_arrays)

    def body(a_ref, d_ref, c_ref, s_ref, *rest):
        out_ref = rest[n_side]
        qr, kr, dk_acc, dv_acc = rest[2 * n_side + 1:2 * n_side + 5]
        if n_side:
            step = pl.program_id(0) * steps + pl.program_id(1)
            finish = _run_side(side, rest[:n_side], rest[n_side + 1:2 * n_side + 1], rest[2 * n_side + 5:], step, bl * steps)
        first_valid, later_valid = _band_masks()
        for j in range(cls):
            a0, t0 = j * grp_w, j * DIL_W
            cos, sin = c_ref[:, t0:t0 + DIL_W], s_ref[:, t0:t0 + DIL_W]
            qr[...] = (_rope(a_ref[:, a0:a0 + DIL_W], cos, sin) * SB_SCALE).astype(BF16)
            kr[...] = _rope(a_ref[:, a0 + DIL_W:a0 + 2 * DIL_W], cos, sin).astype(BF16)
            dk_acc[...] = jnp.zeros_like(dk_acc)
            dv_acc[...] = jnp.zeros_like(dv_acc)

            def block(q0, k0, keys, valid, a0=a0, t0=t0):
                qrows, krows = pl.ds(q0, BAND), pl.ds(k0, keys)
                dq, dk, dv = _band_attend_bwd(
                    qr[qrows, :], kr[krows, :], a_ref[krows, a0 + 2 * DIL_W:a0 + grp_w], valid,
                    d_ref[qrows, a0:a0 + DIL_W].astype(BF16), d_ref[qrows, a0 + DIL_W:a0 + 2 * DIL_W],
                    d_ref[qrows, a0 + 2 * DIL_W:a0 + grp_w])
                dq = dq * SB_SCALE
                out_ref[qrows, a0:a0 + DIL_W] = (dq * c_ref[qrows, t0:t0 + DIL_W]
                                                 - _swap_half(dq) * s_ref[qrows, t0:t0 + DIL_W]).astype(BF16)
                dk_acc[krows, :] += dk
                dv_acc[krows, :] += dv

            block(0, 0, BAND, first_valid)
            if nb > 1:
                def later(b, carry, block=block):
                    block(pl.multiple_of(b * BAND, BAND), pl.multiple_of((b - 1) * BAND, BAND), 2 * BAND, later_valid)
                    return carry

                lax.fori_loop(1, nb, later, 0)
            dk = dk_acc[...]
            out_ref[:, a0 + DIL_W:a0 + 2 * DIL_W] = (dk * cos - _swap_half(dk) * sin).astype(BF16)
            out_ref[:, a0 + 2 * DIL_W:a0 + grp_w] = dv_acc[...].astype(BF16)
        if n_side:
            finish()

    lead = "rows" if col0 is not None else "cols"
    spec = _band_group_specs(lead, rows, cls, col0)
    dspec = _band_group_specs(lead, rows, cls, 0 if col0 is not None else None)
    tab = pl.BlockSpec((rows, cls * DIL_W), lambda b, i: (0, i))
    n_cls = cos_g.shape[1] // DIL_W
    out = pl.pallas_call(
        body,
        grid=(bl, steps),
        in_specs=[spec(grp_w), dspec(grp_w), tab, tab] + [ANY] * n_side,
        out_specs=[pl.BlockSpec((None, rows, cls * grp_w), lambda b, i: (b, 0, i))] + [ANY] * n_side,
        out_shape=[jax.ShapeDtypeStruct((bl, rows, n_cls * grp_w), BF16)] + (_side_out_shapes(side) if n_side else []),
        scratch_shapes=[pltpu.VMEM((rows, DIL_W), BF16), pltpu.VMEM((rows, DIL_W), BF16),
                        pltpu.VMEM((rows, DIL_W), F32), pltpu.VMEM((rows, DIL_W), F32)] + (_side_sems(side) if n_side else []),
        compiler_params=pltpu.CompilerParams(dimension_semantics=("arbitrary", "arbitrary") if n_side else ("parallel", "parallel")),
        name=name,
    )(a, d, cos_g, sin_g, *side_arrays)
    return (out[0], out[1:]) if n_side else out[0]


def _band_merge3(ols):
    t, tm = ols[0].shape[0], 512

    def body(o0, l0, o1, l1, o2, l2, ob_ref, lse_ref):
        a, b, c = l0[...], l1[...], l2[...]
        m = jnp.maximum(jnp.maximum(a, b), c)
        lse = m + jnp.log(jnp.exp(a - m) + jnp.exp(b - m) + jnp.exp(c - m))
        ob_ref[...] = (jnp.exp(a - lse) * o0[...] + jnp.exp(b - lse) * o1[...] + jnp.exp(c - lse) * o2[...]).astype(BF16)
        lse_ref[...] = lse

    spec = pl.BlockSpec((tm, DIL_W), lambda i: (i, 0))
    spec_l = pl.BlockSpec((tm, DIL_W), lambda i: (i, 1))
    return pl.pallas_call(
        body,
        grid=(t // tm,),
        in_specs=[spec, spec_l] * 3,
        out_specs=[spec, spec],
        out_shape=[jax.ShapeDtypeStruct((t, DIL_W), BF16), jax.ShapeDtypeStruct((t, DIL_W), F32)],
        compiler_params=pltpu.CompilerParams(dimension_semantics=("parallel",)),
        name="band_merge",
    )(ols[0], ols[0], ols[1], ols[1], ols[2], ols[2])


MEM_T = 512
MEM_SCALE = 128 ** -0.5
MEM_Q_COL = (D_IN - MEM_W) // LANES


def _mem_specs():
    q = pl.BlockSpec((None, MEM_T, LANES), lambda b, h, i: (b, i, MEM_Q_COL + h))
    k = pl.BlockSpec((None, MEM_LEN, LANES), lambda b, h, i: (b, 0, h))
    v = pl.BlockSpec((None, MEM_LEN, LANES), lambda b, h, i: (b, 0, MEM_W // LANES + h))
    blk = pl.BlockSpec((None, MEM_T, LANES), lambda b, h, i: (b, i, h))
    return q, k, v, blk


def _mem_probs(q, k):
    s = _dot_nt(q, k) * MEM_SCALE
    p = jnp.exp(s - jnp.max(s, axis=1, keepdims=True))
    return p * (1.0 / jnp.sum(p, axis=1, keepdims=True))


def _mem_fwd(proj3, kv3):
    bl = proj3.shape[0]

    def body(q_ref, k_ref, v_ref, o_ref):
        p = _mem_probs(q_ref[...], k_ref[...])
        o_ref[...] = _dot(p.astype(BF16), v_ref[...]).astype(BF16)

    q, k, v, blk = _mem_specs()
    return pl.pallas_call(
        body,
        grid=(bl, MEM_W // LANES, SEQ // MEM_T),
        in_specs=[q, k, v],
        out_specs=blk,
        out_shape=jax.ShapeDtypeStruct((bl, SEQ, MEM_W), BF16),
        compiler_params=pltpu.CompilerParams(dimension_semantics=("parallel", "parallel", "parallel")),
        name="mem_fwd",
    )(proj3, kv3, kv3)


def _mem_bwd(proj3, kv3, do_c):
    bl = proj3.shape[0]

    def body(q_ref, k_ref, v_ref, do_ref, dq_ref, dk_ref, dv_ref):
        @pl.when(pl.program_id(2) == 0)
        def _():
            dk_ref[...] = jnp.zeros_like(dk_ref)
            dv_ref[...] = jnp.zeros_like(dv_ref)

        q, k, do = q_ref[...], k_ref[...], do_ref[...]
        p = _mem_probs(q, k)
        dp = _dot_nt(do, v_ref[...])
        ds = (p * (dp - jnp.sum(p * dp, axis=1, keepdims=True)) * MEM_SCALE).astype(BF16)
        dq_ref[...] = _dot(ds, k).astype(BF16)
        dk_ref[...] += _dot_tn(ds, q)
        dv_ref[...] += _dot_tn(p.astype(BF16), do)

    q, k, v, blk = _mem_specs()
    kv_out = pl.BlockSpec((None, MEM_LEN, LANES), lambda b, h, i: (b, 0, h))
    return pl.pallas_call(
        body,
        grid=(bl, MEM_W // LANES, SEQ // MEM_T),
        in_specs=[q, k, v, blk],
        out_specs=[blk, kv_out, kv_out],
        out_shape=[jax.ShapeDtypeStruct((bl, SEQ, MEM_W), BF16), jax.ShapeDtypeStruct((bl, MEM_LEN, MEM_W), F32),
                   jax.ShapeDtypeStruct((bl, MEM_LEN, MEM_W), F32)],
        compiler_params=pltpu.CompilerParams(dimension_semantics=("parallel", "parallel", "arbitrary")),
        name="mem_bwd",
    )(proj3, kv3, kv3, do_c)


def _place():
    x, y, c = lax.axis_index("x"), lax.axis_index("y"), lax.axis_index("c")
    return x, y, c


def _other_chips(x, y):
    return [(1 - x, y), (x, 1 - y), (1 - x, 1 - y)]


def _remote(src, dst, send_sem, recv_sem, to):
    return pltpu.make_async_remote_copy(src_ref=src, dst_ref=dst, send_sem=send_sem, recv_sem=recv_sem,
                                        device_id=to, device_id_type=MESH)


ANY = pl.BlockSpec(memory_space=pl.ANY)


def _gather_weights(shards):
    n = len(shards)

    def body(*refs):
        send, forward, finish = _gather_phases(refs[:n], refs[n:2 * n], *refs[2 * n:])
        send()
        forward()
        finish()

    return pl.pallas_call(
        body,
        in_specs=[ANY] * n,
        out_specs=[ANY] * n,
        out_shape=_gather_out_shapes(shards),
        scratch_shapes=_gather_sems(n),
        name="gather_weights",
    )(*shards)


def _gather_out_shapes(shards):
    return [jax.ShapeDtypeStruct((N_CHIPS,) + s.shape, s.dtype) for s in shards]


def _gather_sems(n):
    return [pltpu.SemaphoreType.DMA((6 * n,)), pltpu.SemaphoreType.DMA((6 * n,))]


def _gather_phases(in_refs, out_refs, send_sems, recv_sems):
    x, y, c = _place()
    sibling = (x, y, 1 - c)
    chips = _other_chips(x, y)
    first, passed = [], []
    for k in range(len(in_refs)):
        hf = in_refs[k].shape[0] // 2

        def half(px, py, pc, k=k, hf=hf):
            return out_refs[k].at[2 * px + py, pl.ds(pc * hf, hf), :]

        src = in_refs[k].at[pl.ds(c * hf, hf), :]
        for j, chip in enumerate(chips):
            s = 6 * k + j
            first.append(_remote(src, half(x, y, c), send_sems.at[s], recv_sems.at[s], (*chip, c)))
            passed.append((_remote(src, half(*chip, c), send_sems.at[s], recv_sems.at[s], (*chip, c)),
                           _remote(half(*chip, c), half(*chip, c), send_sems.at[s + 3], recv_sems.at[s + 3], sibling),
                           _remote(src, half(*chip, 1 - c), send_sems.at[s + 3], recv_sems.at[s + 3], sibling)))

    def send():
        for cp in first:
            cp.start()

    def forward():
        for landed, fwd, _ in passed:
            landed.wait_recv()
            fwd.start()

    def finish():
        for _, _, from_sibling in passed:
            from_sibling.wait_recv()
        for cp in first:
            cp.wait_send()
        for _, fwd, _ in passed:
            fwd.wait_send()

    return send, forward, finish


def _pair_exchange(grads, *, name):
    n = len(grads)
    side = ("pair", grads)

    def body(*refs):
        send, _, finish = _side_phases(side, refs[:n], refs[n:2 * n], refs[2 * n:])
        send()
        finish()

    return pl.pallas_call(
        body,
        in_specs=[ANY] * n,
        out_specs=[ANY] * n,
        out_shape=_side_out_shapes(side),
        scratch_shapes=_side_sems(side),
        name=name,
    )(*grads)


def _pair_exchange_phases(g_refs, land_refs, send_sems, recv_sems):
    x, y, c = _place()
    cps = []
    for k in range(len(g_refs)):
        hf = g_refs[k].shape[1] // 2
        src = g_refs[k].at[:, pl.ds((1 - c) * hf, hf), :]
        cps.append(_remote(src, land_refs[k], send_sems.at[k], recv_sems.at[k], (x, y, 1 - c)))

    def send():
        for cp in cps:
            cp.start()

    def finish():
        for cp in cps:
            cp.wait()

    return send, finish


def _side_out_shapes(side):
    kind, arrays = side
    if kind == "gather":
        return _gather_out_shapes(arrays)
    if kind == "pair":
        return [jax.ShapeDtypeStruct((N_CHIPS, g.shape[1] // 2, g.shape[2]), g.dtype) for g in arrays]
    return [jax.ShapeDtypeStruct(p.shape, p.dtype) for p in arrays]


def _side_sems(side):
    kind, arrays = side
    n = len(arrays)
    if kind == "gather":
        return _gather_sems(n)
    if kind == "pair":
        return [pltpu.SemaphoreType.DMA((n,)), pltpu.SemaphoreType.DMA((n,))]
    return _chip_exchange_sems(n)


def _side_phases(side, in_refs, out_refs, sems):
    kind = side[0]
    if kind == "gather":
        return _gather_phases(in_refs, out_refs, *sems)
    send, finish = (_pair_exchange_phases if kind == "pair" else _chip_exchange_phases)(in_refs, out_refs, *sems)
    return send, None, finish


def _run_side(side, in_refs, out_refs, sems, step, n_steps):
    first, mid, last = _side_phases(side, in_refs, out_refs, sems)
    pl.when(step == 0)(first)
    if mid is not None:
        pl.when(step == n_steps // 2)(mid)
    return lambda: pl.when(step == n_steps - 1)(last)


def _pair_add(g, land, c_arr, *, name):
    _, a, b = g.shape
    hf = a // 2

    def body(c_ref, g_ref, l_ref, o_ref):
        o_ref[...] = (g_ref[...] + l_ref[...]).astype(BF16)

    return pl.pallas_call(
        body,
        grid_spec=pltpu.PrefetchScalarGridSpec(
            num_scalar_prefetch=1,
            grid=(N_CHIPS,),
            in_specs=[pl.BlockSpec((None, None, hf, b), lambda s, c_ref: (s, c_ref[0], 0, 0)),
                      pl.BlockSpec((None, hf, b), lambda s, c_ref: (s, 0, 0))],
            out_specs=pl.BlockSpec((None, hf, b), lambda s, c_ref: (s, 0, 0)),
        ),
        out_shape=jax.ShapeDtypeStruct((N_CHIPS, hf, b), BF16),
        compiler_params=pltpu.CompilerParams(dimension_semantics=("parallel",)),
        name=name,
    )(c_arr, g.reshape(N_CHIPS, 2, hf, b), land)


def _chip_exchange(parts):
    n = len(parts)

    def body(*refs):
        send, finish = _chip_exchange_phases(refs[:n], refs[n:2 * n], *refs[2 * n:])
        send()
        finish()

    return pl.pallas_call(
        body,
        in_specs=[ANY] * n,
        out_specs=[ANY] * n,
        out_shape=[jax.ShapeDtypeStruct(p.shape, p.dtype) for p in parts],
        scratch_shapes=_chip_exchange_sems(n),
        name="chip_exchange",
    )(*parts)


def _chip_exchange_sems(n):
    return [pltpu.SemaphoreType.DMA((3 * n,)), pltpu.SemaphoreType.DMA((3 * n,))]


def _chip_exchange_phases(p_refs, land_refs, send_sems, recv_sems):
    x, y, c = _place()
    me = 2 * x + y
    sends, recvs = [], []
    for k in range(len(p_refs)):
        for j, (cx, cy) in enumerate(_other_chips(x, y)):
            s = 3 * k + j
            sends.append(_remote(p_refs[k].at[2 * cx + cy], land_refs[k].at[me], send_sems.at[s], recv_sems.at[s], (cx, cy, c)))
            recvs.append(_remote(p_refs[k].at[me], land_refs[k].at[2 * cx + cy], send_sems.at[s], recv_sems.at[s], (cx, cy, c)))

    def send():
        for cp in sends:
            cp.start()

    def finish():
        for cp in recvs:
            cp.wait_recv()
        for cp in sends:
            cp.wait_send()

    return send, finish


def _chip_add(land, part, me_arr, *, name):
    _, r, b = land.shape

    def body(me_ref, p_ref, l1_ref, l2_ref, l3_ref, o_ref):
        o_ref[...] = ((p_ref[...].astype(F32) + l1_ref[...].astype(F32)) + l2_ref[...].astype(F32)) + l3_ref[...].astype(F32)

    tr = r // 2
    other = lambda j: pl.BlockSpec((None, tr, b), lambda i, me_ref: (jnp.bitwise_xor(me_ref[0], j), i, 0))
    return pl.pallas_call(
        body,
        grid_spec=pltpu.PrefetchScalarGridSpec(
            num_scalar_prefetch=1,
            grid=(r // tr,),
            in_specs=[pl.BlockSpec((None, tr, b), lambda i, me_ref: (me_ref[0], i, 0)), other(2), other(1), other(3)],
            out_specs=pl.BlockSpec((tr, b), lambda i, me_ref: (i, 0)),
        ),
        out_shape=jax.ShapeDtypeStruct((r, b), F32),
        compiler_params=pltpu.CompilerParams(dimension_semantics=("parallel",)),
        name=name,
    )(me_arr, part, land, land, land)


def _pair_share(halves):
    n = len(halves)

    def body(*refs):
        h_refs, out_refs = refs[:n], refs[n:2 * n]
        send_sems, recv_sems = refs[2 * n:]
        x, y, c = _place()
        cps = [_remote(h_refs[k], out_refs[k], send_sems.at[k], recv_sems.at[k], (x, y, 1 - c)) for k in range(n)]
        for cp in cps:
            cp.start()
        for cp in cps:
            cp.wait()

    return pl.pallas_call(
        body,
        in_specs=[ANY] * n,
        out_specs=[ANY] * n,
        out_shape=[jax.ShapeDtypeStruct(h.shape, F32) for h in halves],
        scratch_shapes=[pltpu.SemaphoreType.DMA((n,)), pltpu.SemaphoreType.DMA((n,))],
        name="pair_share",
    )(*halves)


def _all_sum_small(part):
    def body(p_ref, o_ref, slots, send_sems, recv_sems):
        x, y, c = _place()
        me = 4 * x + 2 * y + c
        slots[me] = p_ref[...]
        peers = [(x ^ dx, y ^ dy, c ^ dc) for dx in (0, 1) for dy in (0, 1) for dc in (0, 1)][1:]
        sends = [_remote(p_ref, slots.at[me], send_sems.at[k], recv_sems.at[k], peer) for k, peer in enumerate(peers)]
        for cp in sends:
            cp.start()
        for k, (px, py, pc) in enumerate(peers):
            _remote(p_ref, slots.at[4 * px + 2 * py + pc], send_sems.at[k], recv_sems.at[k], (px, py, pc)).wait_recv()
        for cp in sends:
            cp.wait_send()
        acc = slots[0]
        for d in range(1, 8):
            acc = acc + slots[d]
        o_ref[...] = acc

    vmem = pl.BlockSpec(memory_space=pltpu.VMEM)
    return pl.pallas_call(
        body,
        in_specs=[vmem],
        out_specs=vmem,
        out_shape=jax.ShapeDtypeStruct(part.shape, F32),
        scratch_shapes=[pltpu.VMEM((8,) + part.shape, F32), pltpu.SemaphoreType.DMA((7,)), pltpu.SemaphoreType.DMA((7,))],
        name="all_sum_small",
    )(part)


def _deinterleave(a, d):
    b, s, c = a.shape
    return a.reshape(b, s // d, d, c).transpose(0, 2, 1, 3).reshape(b * s // BAND, BAND, c)


def _reinterleave(a, d, b):
    c = a.shape[-1]
    return a.reshape(b, d, SEQ // d, c).transpose(0, 2, 1, 3).reshape(b, SEQ, c)


def _rope_tables():
    half = HEAD_DIM // 2
    inv_freq = np.float32(ROPE_THETA) ** (-np.arange(half, dtype=np.float32) * np.float32(2.0) / np.float32(HEAD_DIM))
    ang = np.arange(SEQ, dtype=np.float32)[:, None] * inv_freq[None, :].astype(np.float32)
    cos = np.tile(np.cos(ang).astype(np.float32), (1, 2 * BAND_HEADS))
    sin = np.tile(np.concatenate([-np.sin(ang), np.sin(ang)], axis=1).astype(np.float32), (1, BAND_HEADS))
    return jnp.asarray(cos), jnp.asarray(sin)


def _band_groups():
    out = []
    for d in DIL_D:
        rows = SEQ // d
        cls = max(1, 512 // rows) if d > 1 else 1
        out.append(dict(rows=rows, cls=cls, steps=d // cls))
    return out


def _local_step(x, mem, loss_target, g_pre_mix, g_post_mix, g_pre_ffn, g_post_ffn, g_mem, b_gate, w, comm=None):
    bl = x.shape[0]
    t = bl * SEQ
    chips = range(N_CHIPS)
    half_ff = D_FF // 2

    def with_gathered(w, names, gathered, shards):
        return {**w, **{name: lax.dynamic_update_slice(g, s[None], (comm["me"][0], 0, 0))
                        for name, g, s in zip(names, gathered, shards)}}

    x2 = x.reshape(t, D_MODEL)
    tgt2 = loss_target.reshape(t, D_MODEL)
    mem2 = mem.reshape(bl * MEM_LEN, D_MODEL)

    h = _norm_fwd(x2, g_pre_mix, name="norm_x", side=("gather", comm["first_shards"]) if comm else None)
    if comm:
        w = with_gathered(w, comm["first_names"], h[1], comm["first_shards"])
        h = h[0]
    w_in_full = _join_shards(w["w_in"])
    proj = _mm([(h, w_in_full)], nt=False, tn=2176, out_dtypes=[BF16], name="proj",
               side=("gather", comm["mid_shards"]) if comm else None)
    if comm:
        w = with_gathered(w, comm["mid_names"], proj[1], comm["mid_shards"])
        proj = proj[0]
    w_mem_kv_full = w["w_mem_kv"].reshape(D_MODEL, 2 * MEM_W)
    gates = _mm([(h, w["w_gate"], None, "j")], nt=False,tn=w["w_gate"].shape[2], out_dtypes=[BF16], name="gates",
                bias=b_gate, epilogue=lambda acc: (_sigmoid(acc),))
    hm = _norm_fwd(mem2, g_mem, name="norm_mem")
    kv_m = _mm([(hm, w_mem_kv_full)], nt=False,tn=1024, out_dtypes=[BF16], name="mem_kv")
    proj3 = proj.reshape(bl, SEQ, D_IN)
    kv3 = kv_m.reshape(bl, MEM_LEN, 2 * MEM_W)

    o_a, o_a32, sb_weights, late_gathered = _sb_fwd(proj3, comm["late_shards"] if comm else [])
    if comm:
        w = with_gathered(w, comm["late_names"], late_gathered, comm["late_shards"])
    w_o_full = w["w_o"].reshape(D_MODEL, D_MODEL)
    w_ffn_out_full = w["w_ffn_out"].reshape(D_FF, D_MODEL)

    cos_t, sin_t = _rope_tables()
    dil0 = 3 * SB_W

    grp_w = 3 * DIL_W
    band = []
    for g, (d, cfg) in enumerate(zip(DIL_D, _band_groups())):
        a_g = proj3 if d == 1 else proj3[:, :, dil0 + g * grp_w:dil0 + (g + 1) * grp_w].reshape(bl, SEQ // d, d * grp_w)
        band.append(dict(cfg, a=a_g, col0=dil0 // grp_w if d == 1 else None, cos=cos_t.reshape(SEQ // d, d * DIL_W),
                         sin=sin_t.reshape(SEQ // d, d * DIL_W)))
    ols = [_band_group_fwd(b["a"], b["cos"], b["sin"], rows=b["rows"], cls=b["cls"], steps=b["steps"], col0=b["col0"],
                           name=f"band_fwd_{g}").reshape(t, 2 * DIL_W) for g, b in enumerate(band)]
    o_b, lse_b = _band_merge3(ols)

    o_c = _mem_fwd(proj3, kv3)

    o_a2, o_c2 = o_a.reshape(t, SB_W), o_c.reshape(t, MEM_W)
    y_a, y_b, y_c, merged = _branch_merge_fwd(o_a2, o_b, o_c2, w["w_br_sb"], w["w_br_dil"], w["w_br_mem"], gates)
    mix = _mm([(merged, w_o_full)], nt=False,tn=1024, out_dtypes=[F32], name="mix")
    x1, h2 = _mid_fwd(mix, x2, g_post_mix, g_pre_ffn)
    gg, uu, f = _ffn_in_fwd(h2, w["w_ffn_in"])
    f2 = _mm([(f, w_ffn_out_full)], nt=False,tn=1024, out_dtypes=[F32], name="ffn_out")

    dy, df2, dg_post_ffn, loss_row = _loss_bwd(f2, x1, g_post_ffn, tgt2)

    dg_ffn, du_ffn = _mm([(df2, w_ffn_out_full)], nt=True,tn=half_ff, out_dtypes=[BF16, BF16], name="d_ffn_act",
                         extras=(gg, uu), epilogue=_swiglu_bwd_epilogue)
    gw = {}
    gw["w_ffn_out"] = _mm_tn(f, df2, tm=half_ff, tn=1024, name="gw_ffn_out").reshape(N_CHIPS, D_FF // N_CHIPS, D_MODEL)
    gw_ffn_g = _mm_tn(h2, dg_ffn, tm=1024, tn=half_ff, name="gw_ffn_gate", out_shards=True)
    gw_ffn_u = _mm_tn(h2, du_ffn, tm=1024, tn=half_ff, name="gw_ffn_up", out_shards=True)
    gw["w_ffn_in"] = jnp.concatenate([gw_ffn_g, gw_ffn_u], axis=0)
    dh2 = _mm([(dg_ffn, w["w_ffn_in"], 0, 0), (dg_ffn, w["w_ffn_in"], 1, 1), (du_ffn, w["w_ffn_in"], 0, 2),
               (du_ffn, w["w_ffn_in"], 1, 3)], nt=True,tn=1024, out_dtypes=[F32], name="d_h2")
    dx1, dmix, dg_pre_ffn, dg_post_mix = _mid_bwd(dh2, x1, mix, g_pre_ffn, g_post_mix, dy)

    gw["w_o"] = _mm_tn(merged, dmix, tm=1024, tn=1024, name="gw_o").reshape(N_CHIPS, D_MODEL // N_CHIPS, D_MODEL)
    dmerged = _mm([(dmix, w_o_full)], nt=True,tn=1024, out_dtypes=[F32], name="d_merged")
    dy_a, dy_b, dy_c, dgpre, db_gate = _gate_bwd(dmerged, gates, y_a, y_b, y_c)
    br_cols = D_MODEL // N_CHIPS
    gw["w_br_sb"] = _mm_tn(o_a2, dy_a, tm=512, tn=br_cols, name="gw_br_sb", out_shards=True)
    gw["w_br_dil"] = _mm_tn(o_b, dy_b, tm=256, tn=br_cols, name="gw_br_dil", out_shards=True)
    gw["w_br_mem"] = _mm_tn(o_c2, dy_c, tm=512, tn=br_cols, name="gw_br_mem", out_shards=True)
    gw["w_gate"] = _mm_tn(h, dgpre, tm=1024, tn=w["w_gate"].shape[2], name="gw_gate", out_shards=True)
    do_a = _mm([(dy_a, w["w_br_sb"], s, s) for s in chips], nt=True,tn=SB_W, out_dtypes=[BF16], name="d_o_a")
    do_b = _mm([(dy_b, w["w_br_dil"], s, s) for s in chips], nt=True,tn=DIL_W, out_dtypes=[BF16], name="d_o_b")
    do_c = _mm([(dy_c, w["w_br_mem"], s, s) for s in chips], nt=True,tn=MEM_W, out_dtypes=[BF16], name="d_o_c")

    dq_c, dk_m, dv_m = _mem_bwd(proj3, kv3, do_c.reshape(bl, SEQ, MEM_W))
    dkv_m = jnp.concatenate([dk_m, dv_m], axis=-1).reshape(bl * MEM_LEN, 2 * MEM_W).astype(BF16)
    gw["w_mem_kv"] = _mm_tn(hm, dkv_m, tm=1024, tn=1024, name="gw_mem_kv").reshape(N_CHIPS, D_MODEL // N_CHIPS, 2 * MEM_W)
    dhm = _mm([(dkv_m, w_mem_kv_full)], nt=True,tn=1024, out_dtypes=[F32], name="d_hm")
    dg_mem = _mem_norm_bwd(dhm, mem2, g_mem)

    dcat = _band_delta(do_b, o_b, lse_b)
    early = [name for name, _, _ in PACK if name != "w_in"] if comm else []
    grads = [gw[name] for name in early]
    d_dil = []
    for g, (d, b) in enumerate(zip(DIL_D, band)):
        out = _band_group_bwd(b["a"], dcat.reshape(bl, SEQ // d, d * grp_w), b["cos"], b["sin"], rows=b["rows"], cls=b["cls"],
                              steps=b["steps"], col0=b["col0"], name=f"band_bwd_{g}",
                              side=("pair", grads) if comm and g == 0 else None)
        if comm and g == 0:
            out, lands = out
        d_dil.append(out.reshape(bl, SEQ, grp_w))

    parts = [_pair_add(g, l, comm["c"], name="pair_add_" + name) for name, g, l in zip(early, grads, lands)] if comm else []
    dq_a, dk_a, dv_a, lands = _sb_bwd(proj3, o_a32, do_a.reshape(bl, SEQ, SB_W), sb_weights, parts)
    reduced = {name: (p, l) for name, p, l in zip(early, parts, lands)}

    in_cols = D_IN // N_CHIPS
    dproj_s = _split_to_shards([a.reshape(t, a.shape[-1]) for a in [dq_a, dk_a, dv_a] + d_dil + [dq_c]], name="dproj_shards")
    gw["w_in"] = _mm_tn(h, dproj_s, tm=1024, tn=in_cols, name="gw_in")
    if comm:
        land = _pair_exchange([gw["w_in"]], name="pair_exchange_w_in")[0]
        part_in = _pair_add(gw["w_in"], land, comm["c"], name="pair_add_w_in")
    dh = _mm([(dproj_s, w["w_in"], s, s) for s in chips] + [(dgpre, w["w_gate"], s, s) for s in chips],
             nt=True, tn=1024, out_dtypes=[F32], name="d_h", side=("chip", [part_in]) if comm else None)
    if comm:
        dh, (land_in,) = dh
        reduced["w_in"] = (part_in, land_in)
    grad_x, dg_pre_mix = _first_bwd(dh, x2, g_pre_mix, dx1)
    small = jnp.concatenate([dg_pre_mix, dg_post_mix, dg_pre_ffn, dg_post_ffn, dg_mem, db_gate.reshape(3, D_MODEL)], axis=0)
    return loss_row[0, 0], grad_x.reshape(bl, SEQ, D_MODEL), gw, small, reduced


def kernel(x, mem, g_pre_mix, g_post_mix, g_pre_ffn, g_post_ffn, g_mem, w_in, w_mem_kv, w_br_sb, w_br_dil, w_br_mem, w_gate, b_gate, w_o, w_ffn_in, w_ffn_out, loss_target, m_g_pre_mix, m_g_post_mix, m_g_pre_ffn, m_g_post_ffn, m_g_mem, m_w_in, m_w_mem_kv, m_w_br_sb, m_w_br_dil, m_w_br_mem, m_w_gate, m_b_gate, m_w_o, m_w_ffn_in, m_w_ffn_out, v_g_pre_mix, v_g_post_mix, v_g_pre_ffn, v_g_post_ffn, v_g_mem, v_w_in, v_w_mem_kv, v_w_br_sb, v_w_br_dil, v_w_br_mem, v_w_gate, v_b_gate, v_w_o, v_w_ffn_in, v_w_ffn_out):
    w_shards = dict(w_in=w_in[0], w_mem_kv=w_mem_kv[0], w_br_sb=w_br_sb[0], w_br_dil=w_br_dil[0], w_br_mem=w_br_mem[0],
                    w_gate=w_gate[0], w_o=w_o[0], w_ffn_in=w_ffn_in[0], w_ffn_out=w_ffn_out[0])
    m_shards = dict(w_in=m_w_in[0], w_mem_kv=m_w_mem_kv[0], w_br_sb=m_w_br_sb[0], w_br_dil=m_w_br_dil[0], w_br_mem=m_w_br_mem[0],
                    w_gate=m_w_gate[0], w_o=m_w_o[0], w_ffn_in=m_w_ffn_in[0], w_ffn_out=m_w_ffn_out[0])
    v_shards = dict(w_in=v_w_in[0], w_mem_kv=v_w_mem_kv[0], w_br_sb=v_w_br_sb[0], w_br_dil=v_w_br_dil[0], w_br_mem=v_w_br_mem[0],
                    w_gate=v_w_gate[0], w_o=v_w_o[0], w_ffn_in=v_w_ffn_in[0], w_ffn_out=v_w_ffn_out[0])

    names = [name for name, _, _ in PACK]
    c_arr = lax.axis_index("c").astype(jnp.int32).reshape(1)
    me_arr = (2 * lax.axis_index("x") + lax.axis_index("y")).astype(jnp.int32).reshape(1)
    mid_names = ["w_gate", "w_mem_kv"]
    late_names = [name for name in names if name not in ["w_in"] + mid_names]
    bf = {name: w_shards[name].astype(BF16) for name in names}
    comm = dict(c=c_arr, me=me_arr, first_names=["w_in"], first_shards=[bf["w_in"]],
                mid_names=mid_names, mid_shards=[bf[name] for name in mid_names],
                late_names=late_names, late_shards=[bf[name] for name in late_names])

    loss_local, grad_x, gw, small, reduced = _local_step(x, mem, loss_target, g_pre_mix, g_post_mix, g_pre_ffn, g_post_ffn,
                                                         g_mem, b_gate, {}, comm)
    loss = lax.psum(loss_local, ("x", "y", "c"))

    halves =[_chip_add(reduced[name][1], reduced[name][0], me_arr, name="chip_add_" + name) for name in names]
    theirs = _pair_share(halves)
    small = _all_sum_small(small)

    upd = {}
    for name, mine, other in zip(names, halves, theirs):
        upd[name] = _adamw_halves(w_shards[name], mine, other, m_shards[name], v_shards[name], c_arr, name="adamw_" + name)
    g_shards = {name: u[0] for name, u in upd.items()}

    def small8(gs, b):
        return jnp.concatenate(gs + [b.reshape(3, D_MODEL)], axis=0)

    sw = small8([g_pre_mix, g_post_mix, g_pre_ffn, g_post_ffn, g_mem], b_gate)
    sm = small8([m_g_pre_mix, m_g_post_mix, m_g_pre_ffn, m_g_post_ffn, m_g_mem], m_b_gate)
    sv = small8([v_g_pre_mix, v_g_post_mix, v_g_pre_ffn, v_g_post_ffn, v_g_mem], v_b_gate)
    s_upd = _adamw(sw, small, sm, sv, tm=8, name="adamw_small")

    def small_out(a):
        return [a[0:1], a[1:2], a[2:3], a[3:4], a[4:5]]

    order = ["w_in", "w_mem_kv", "w_br_sb", "w_br_dil", "w_br_mem", "w_gate", "b_gate", "w_o", "w_ffn_in", "w_ffn_out"]

    def leaves(small_arr, big):
        out = small_out(small_arr)
        for name in order:
            out.append(small_arr[5:8].reshape(1, 3 * D_MODEL) if name == "b_gate" else big[name][None])
        return out

    grads_out = leaves(small, g_shards)
    delta_out = leaves(s_upd[0], {n: u[1] for n, u in upd.items()})
    m_out = leaves(s_upd[1], {n: u[2] for n, u in upd.items()})
    v_out = leaves(s_upd[2], {n: u[3] for n, u in upd.items()})
    return (loss, grad_x, *grads_out, *delta_out, *m_out, *v_out)
```

```python
import jax
import jax.numpy as jnp
import numpy as np
from jax import lax
from jax.experimental import pallas as pl
from jax.experimental.pallas import tpu as pltpu

F32 = jnp.float32
BF16 = jnp.bfloat16
MESH = pl.DeviceIdType.MESH

D_MODEL = 1024
SEQ = 2048
HEAD_DIM = 64
SB_W = 512
DIL_W = 256
MEM_W = 512
MEM_LEN = 256
D_IN = 3 * SB_W + 9 * DIL_W + MEM_W
D_FF = 2816
DIL_D = (1, 4, 16)
ROPE_THETA = 10000.0
NORM_EPS = 1e-6
NEG_INF = -1e30
LANES = 128

ADAM_LR = 0.001
ADAM_B1 = 0.9
ADAM_B2 = 0.999
ADAM_EPS = 1e-08
ADAM_WD = 0.01
ADAM_STEP = 10

N_CHIPS = 4
PACK = (
    ("w_in", (1024, 1088), 1),
    ("w_mem_kv", (256, 1024), 0),
    ("w_br_sb", (512, 256), 1),
    ("w_br_dil", (256, 256), 1),
    ("w_br_mem", (512, 256), 1),
    ("w_gate", (1024, 768), 1),
    ("w_o", (256, 1024), 0),
    ("w_ffn_in", (1024, 1408), 1),
    ("w_ffn_out", (704, 1024), 0),
)
PACK_ROWS = sum(a * b for _, (a, b), _ in PACK) // D_MODEL
HALF_ROWS = PACK_ROWS // 2


def _dot(a, b):
    return lax.dot_general(a, b, (((1,), (0,)), ((), ())), preferred_element_type=F32)


def _dot_nt(a, b):
    return lax.dot_general(a, b, (((1,), (1,)), ((), ())), preferred_element_type=F32)


def _dot_tn(a, b):
    return lax.dot_general(a, b, (((0,), (0,)), ((), ())), preferred_element_type=F32)


def _split_dot(x, u):
    hi = x.astype(BF16)
    lo = (x - hi.astype(F32)).astype(BF16)
    return _dot(hi, u) + _dot(lo, u)


V7X_VMEM_BUDGET = 44 * 2 ** 20


def _rows_that_fit(m, row_bytes, fixed_bytes):
    for tm in (1024, 512, 256, 128):
        if m % tm == 0 and fixed_bytes + tm * row_bytes <= V7X_VMEM_BUDGET:
            return tm
    return min(m, 128)


def _mm(pairs, *, nt, tn, out_dtypes, name, bias=None, extras=(), epilogue=None, side=None):
    pairs = [p if len(p) == 4 else (p[0], p[1], None, None) for p in pairs]
    m = pairs[0][0].shape[-2]
    b0 = pairs[0][1]
    if nt:
        n = b0.shape[-2]
    else:
        n = b0.shape[-1] * (b0.shape[0] if b0.ndim == 3 else 1)
    n_pairs, n_extra, n_out = len(pairs), len(extras), len(out_dtypes)
    assert n % tn == 0
    one_col = n == tn
    ks = [(b.shape[-1] if nt else b.shape[-2]) for _, b, _, _ in pairs]
    fixed = sum(k * tn * 2 for k in ks) * (1 if one_col else 2)
    row_bytes = 2 * sum(k * 2 for k in ks) + 2 * tn * (sum(jnp.dtype(dt).itemsize for dt in out_dtypes) + 2 * n_extra) + 2 * tn * 4
    tm = _rows_that_fit(m, row_bytes, fixed)
    assert m % tm == 0
    b_mode = dict(pipeline_mode=pl.Buffered(1)) if one_col else {}
    has_bias = bias is not None
    side_arrays = side[1] if side else []
    n_side = len(side_arrays)
    n_main_in = 2 * n_pairs + has_bias + n_extra
    n_steps = (n // tn) * (m // tm)

    def body(*refs):
        if n_side:
            step = pl.program_id(0) * (m // tm) + pl.program_id(1)
            finish = _run_side(side, refs[n_main_in:n_main_in + n_side],
                               refs[n_main_in + n_side + n_out:n_main_in + 2 * n_side + n_out],
                               refs[n_main_in + 2 * n_side + n_out:], step, n_steps)
        outs = refs[n_main_in + n_side:n_main_in + n_side + n_out]
        acc = None
        for i in range(n_pairs):
            a, b = refs[2 * i][...], refs[2 * i + 1][...]
            p = _dot_nt(a, b) if nt else _dot(a, b)
            acc = p if acc is None else acc + p
        pos = 2 * n_pairs
        if has_bias:
            acc = acc + refs[pos][...]
            pos += 1
        ex = [r[...] for r in refs[pos:pos + n_extra]]
        vals = (acc,) if epilogue is None else epilogue(acc, *ex)
        for r, v, dt in zip(outs, vals, out_dtypes):
            r[...] = v.astype(dt)
        if n_side:
            finish()

    in_specs, args = [], []
    for a, b, a_col, b_sel in pairs:
        k = b.shape[-1] if nt else b.shape[-2]
        assert a_col is not None or a.shape[1] == k
        if a.ndim == 3:
            in_specs.append(pl.BlockSpec((None, tm, k), lambda j, i, c=a_col: (c, i, 0)))
        else:
            in_specs.append(pl.BlockSpec((tm, k), lambda j, i, c=a_col or 0: (i, c)))
        if b.ndim == 2:
            in_specs.append(pl.BlockSpec((tn, k), lambda j, i: (j, 0), **b_mode) if nt
                            else pl.BlockSpec((k, tn), lambda j, i: (0, j), **b_mode))
        elif nt:
            in_specs.append(pl.BlockSpec((None, tn, k), lambda j, i, s=b_sel: (s, j, 0), **b_mode))
        else:
            assert b_sel == "j" and b.shape[-1] == tn
            in_specs.append(pl.BlockSpec((None, k, tn), lambda j, i: (j, 0, 0), **b_mode))
        args += [a, b]
    if has_bias:
        in_specs.append(pl.BlockSpec((1, tn), lambda j, i: (0, j)))
        args.append(bias)
    for e in extras:
        in_specs.append(pl.BlockSpec((tm, tn), lambda j, i: (i, j)))
        args.append(e)
    out = pl.pallas_call(
        body,
        grid=(n // tn, m // tm),
        in_specs=in_specs + [ANY] * n_side,
        out_specs=[pl.BlockSpec((tm, tn), lambda j, i: (i, j)) for _ in range(n_out)] + [ANY] * n_side,
        out_shape=[jax.ShapeDtypeStruct((m, n), dt) for dt in out_dtypes] + (_side_out_shapes(side) if n_side else []),
        scratch_shapes=_side_sems(side) if n_side else [],
        compiler_params=pltpu.CompilerParams(dimension_semantics=("arbitrary", "arbitrary") if n_side else ("parallel", "parallel")),
        name=name,
    )(*args, *side_arrays)
    if n_side:
        return (out[0] if n_out == 1 else out[:n_out]), out[n_out:]
    return out[0] if n_out == 1 else out


def _mm_tn(a, b, *, tm, tn, name, out_shards=False):
    k, m = a.shape
    b_shards = b.ndim == 3
    out_shards = out_shards or b_shards
    n = b.shape[0] * b.shape[2] if b_shards else b.shape[1]
    tk = _rows_that_fit(k, 2 * 2 * (tm + tn), 3 * tm * tn * 4)
    assert m % tm == 0 and n % tn == 0 and k % tk == 0 and (not b_shards or b.shape[2] == tn)

    def body(a_ref, b_ref, o_ref):
        @pl.when(pl.program_id(2) == 0)
        def _():
            o_ref[...] = jnp.zeros_like(o_ref)

        o_ref[...] += _dot_tn(a_ref[...], b_ref[...])

    if b_shards:
        b_spec = pl.BlockSpec((None, tk, tn), lambda i, j, kk: (j, kk, 0))
    else:
        b_spec = pl.BlockSpec((tk, tn), lambda i, j, kk: (kk, j))
    if out_shards:
        out_spec = pl.BlockSpec((None, tm, tn), lambda i, j, kk: (j, i, 0))
        out_shape = jax.ShapeDtypeStruct((n // tn, m, tn), F32)
    else:
        out_spec = pl.BlockSpec((tm, tn), lambda i, j, kk: (i, j))
        out_shape = jax.ShapeDtypeStruct((m, n), F32)
    return pl.pallas_call(
        body,
        grid=(m // tm, n // tn, k // tk),
        in_specs=[pl.BlockSpec((tk, tm), lambda i, j, kk: (kk, i)), b_spec],
        out_specs=out_spec,
        out_shape=out_shape,
        compiler_params=pltpu.CompilerParams(dimension_semantics=("parallel", "parallel", "arbitrary")),
        name=name,
    )(a, b)


def _rowwise(fn, ins, outs, *, tm, name, side=None):
    rows = next(a.shape[0] for a, kind in ins if kind == "row")
    tm = min(tm, rows)
    assert rows % tm == 0
    n_in, n_out = len(ins), len(outs)
    side_arrays = side[1] if side else []
    n_side = len(side_arrays)

    def body(*refs):
        if n_side:
            finish = _run_side(side, refs[n_in:n_in + n_side], refs[n_in + n_side + n_out:n_in + 2 * n_side + n_out],
                               refs[n_in + 2 * n_side + n_out:], pl.program_id(0), rows // tm)
        vals = fn(*[r[...] for r in refs[:n_in]])
        for (_, dt, kind), r, v in zip(outs, refs[n_in + n_side:n_in + n_side + n_out], vals):
            if kind == "row":
                r[...] = v.astype(dt)
            else:
                @pl.when(pl.program_id(0) == 0)
                def _(r=r):
                    r[...] = jnp.zeros_like(r)

                r[...] += v
        if n_side:
            finish()

    in_specs = [pl.BlockSpec((tm, a.shape[1]), lambda i: (i, 0)) if kind == "row" else pl.BlockSpec(a.shape, lambda i: (0, 0))
                for a, kind in ins]
    out_specs = [pl.BlockSpec((tm, c), lambda i: (i, 0)) if kind == "row" else pl.BlockSpec((1, c), lambda i: (0, 0))
                 for c, _, kind in outs]
    out_shape = [jax.ShapeDtypeStruct((rows if kind == "row" else 1, c), dt) for c, dt, kind in outs]
    ordered = n_side or any(kind == "acc" for _, _, kind in outs)
    return pl.pallas_call(
        body,
        grid=(rows // tm,),
        in_specs=in_specs + [ANY] * n_side,
        out_specs=out_specs + [ANY] * n_side,
        out_shape=out_shape + (_side_out_shapes(side) if n_side else []),
        scratch_shapes=_side_sems(side) if n_side else [],
        compiler_params=pltpu.CompilerParams(dimension_semantics=("arbitrary" if ordered else "parallel",)),
        name=name,
    )(*[a for a, _ in ins], *side_arrays)


def _rstd(x):
    return lax.rsqrt(jnp.mean(x * x, axis=-1, keepdims=True) + NORM_EPS)


def _norm_bwd(dout, xin, g):
    r = _rstd(xin)
    n = xin * r
    dn = dout * g
    dg = jnp.sum(dout * n, axis=0, keepdims=True)
    dx = r * (dn - n * jnp.mean(dn * n, axis=-1, keepdims=True))
    return dx, dg


def _sigmoid(x):
    return 0.5 * jnp.tanh(0.5 * x) + 0.5


def _norm_fwd(x, g, *, name, side=None):
    def fn(x, g):
        return ((x * _rstd(x)) * g,)

    out = _rowwise(fn, [(x, "row"), (g, "vec")], [(D_MODEL, BF16, "row")], tm=512, name=name, side=side)
    return (out[0], out[1:]) if side else out[0]


def _mid_fwd(mix, x, g_post_mix, g_pre_ffn):
    def fn(mix, x, g2, g3):
        x1 = x + (mix * _rstd(mix)) * g2
        return x1, (x1 * _rstd(x1)) * g3

    return _rowwise(fn, [(mix, "row"), (x, "row"), (g_post_mix, "vec"), (g_pre_ffn, "vec")],
                    [(D_MODEL, F32, "row"), (D_MODEL, BF16, "row")], tm=512, name="mid_fwd")


def _loss_bwd(f2, x1, g_post_ffn, tgt):
    def fn(f2, x1, g4, tgt):
        r = _rstd(f2)
        n = f2 * r
        err = x1 + n * g4 - tgt
        loss = 0.5 * jnp.sum(jnp.mean(err * err, axis=-1, keepdims=True), axis=0, keepdims=True)
        dy = err * (1.0 / D_MODEL)
        dn = dy * g4
        dg4 = jnp.sum(dy * n, axis=0, keepdims=True)
        df2 = r * (dn - n * jnp.mean(dn * n, axis=-1, keepdims=True))
        return dy, df2, dg4, jnp.broadcast_to(loss, (1, LANES))

    return _rowwise(fn, [(f2, "row"), (x1, "row"), (g_post_ffn, "vec"), (tgt, "row")],
                    [(D_MODEL, BF16, "row"), (D_MODEL, BF16, "row"), (D_MODEL, F32, "acc"), (LANES, F32, "acc")],
                    tm=512, name="loss_bwd")


def _mid_bwd(dh2, x1, mix, g_pre_ffn, g_post_mix, dy):
    def fn(dh2, x1, mix, g3, g2, dy):
        d3, dg3 = _norm_bwd(dh2.astype(F32), x1, g3)
        dx1 = dy.astype(F32) + d3
        dmix, dg2 = _norm_bwd(dx1, mix, g2)
        return dx1, dmix, dg3, dg2

    return _rowwise(fn, [(dh2, "row"), (x1, "row"), (mix, "row"), (g_pre_ffn, "vec"), (g_post_mix, "vec"), (dy, "row")],
                    [(D_MODEL, BF16, "row"), (D_MODEL, BF16, "row"), (D_MODEL, F32, "acc"), (D_MODEL, F32, "acc")],
                    tm=256, name="mid_bwd")


def _first_bwd(dh, x, g_pre_mix, dx1):
    def fn(dh, x, g1, dx1):
        d1, dg1 = _norm_bwd(dh.astype(F32), x, g1)
        return dx1.astype(F32) + d1, dg1

    return _rowwise(fn, [(dh, "row"), (x, "row"), (g_pre_mix, "vec"), (dx1, "row")],
                    [(D_MODEL, F32, "row"), (D_MODEL, F32, "acc")], tm=512, name="first_bwd")


def _mem_norm_bwd(dhm, mem, g_mem):
    def fn(dhm, mem, g):
        return (jnp.sum(dhm * (mem * _rstd(mem)), axis=0, keepdims=True),)

    return _rowwise(fn, [(dhm, "row"), (mem, "row"), (g_mem, "vec")], [(D_MODEL, F32, "acc")], tm=512, name="mem_norm_bwd")[0]


def _gate_bwd(dmerged, gates, ya, yb, yc):
    def fn(dm, gt, ya, yb, yc):
        dm = dm.astype(F32)
        gt = gt.astype(F32)
        outs, dgp = [], []
        for i, y in enumerate((ya, yb, yc)):
            gi = gt[:, i * D_MODEL:(i + 1) * D_MODEL]
            outs.append(dm * gi)
            dgp.append(dm * y.astype(F32) * gi * (1.0 - gi))
        dgpre = jnp.concatenate(dgp, axis=1)
        return outs[0], outs[1], outs[2], dgpre, jnp.sum(dgpre, axis=0, keepdims=True)

    return _rowwise(fn, [(dmerged, "row"), (gates, "row"), (ya, "row"), (yb, "row"), (yc, "row")],
                    [(D_MODEL, BF16, "row")] * 3 + [(3 * D_MODEL, BF16, "row"), (3 * D_MODEL, F32, "acc")],
                    tm=256, name="gate_bwd")


def _adamw_math(w, g, m, v):
    m = ADAM_B1 * m + (1.0 - ADAM_B1) * g
    v = ADAM_B2 * v + (1.0 - ADAM_B2) * (g * g)
    m_hat = m / (1.0 - ADAM_B1 ** ADAM_STEP)
    v_hat = v / (1.0 - ADAM_B2 ** ADAM_STEP)
    delta = -ADAM_LR * (m_hat / (jnp.sqrt(v_hat) + ADAM_EPS) + ADAM_WD * w)
    return delta, m, v


def _adamw(w, g, m, v, *, tm, name):
    c = w.shape[1]
    return _rowwise(_adamw_math, [(w, "row"), (g, "row"), (m, "row"), (v, "row")], [(c, F32, "row")] * 3, tm=tm, name=name)


def _adamw_halves(w, g_mine, g_theirs, m, v, c_arr, *, name):
    a, b = w.shape
    hf = a // 2
    tr = hf // 4

    def body(c_ref, w_ref, gm_ref, gt_ref, m_ref, v_ref, g_out, d_out, m_out, v_out):
        g = jnp.where(pl.program_id(0) == c_ref[0], gm_ref[...], gt_ref[...])
        d, m_new, v_new = _adamw_math(w_ref[...], g, m_ref[...], v_ref[...])
        g_out[...] = g
        d_out[...] = d
        m_out[...] = m_new
        v_out[...] = v_new

    full = pl.BlockSpec((tr, b), lambda hh, i, c_ref: (hh * (hf // tr) + i, 0))
    half = pl.BlockSpec((tr, b), lambda hh, i, c_ref: (i, 0))
    return pl.pallas_call(
        body,
        grid_spec=pltpu.PrefetchScalarGridSpec(
            num_scalar_prefetch=1,
            grid=(2, hf // tr),
            in_specs=[full, half, half, full, full],
            out_specs=[full] * 4,
        ),
        out_shape=[jax.ShapeDtypeStruct((a, b), F32)] * 4,
        compiler_params=pltpu.CompilerParams(dimension_semantics=("parallel", "parallel")),
        name=name,
    )(c_arr, w, g_mine, g_theirs, m, v)


def _ffn_in_fwd(h2, w_ffn):
    m, tm, tn = h2.shape[0], 512, w_ffn.shape[2]
    assert 2 * tn == D_FF

    def body(h_ref, wg_ref, wu_ref, g_ref, u_ref, f_ref):
        h = h_ref[...]
        g = _dot(h, wg_ref[...])
        u = _dot(h, wu_ref[...])
        g_ref[...] = g.astype(BF16)
        u_ref[...] = u.astype(BF16)
        f_ref[...] = (g * _sigmoid(g) * u).astype(BF16)

    o_spec = pl.BlockSpec((tm, tn), lambda j, i: (i, j))
    return pl.pallas_call(
        body,
        grid=(D_FF // tn, m // tm),
        in_specs=[pl.BlockSpec((tm, D_MODEL), lambda j, i: (i, 0)),
                  pl.BlockSpec((None, D_MODEL, tn), lambda j, i: (j, 0, 0)),
                  pl.BlockSpec((None, D_MODEL, tn), lambda j, i: (j + 2, 0, 0))],
        out_specs=[o_spec, o_spec, o_spec],
        out_shape=[jax.ShapeDtypeStruct((m, D_FF), BF16)] * 3,
        compiler_params=pltpu.CompilerParams(dimension_semantics=("parallel", "parallel")),
        name="ffn_in_fwd",
    )(h2, w_ffn, w_ffn)


def _join_shards(w4):
    _, rows, cols = w4.shape
    tr = rows // 4

    def body(w_ref, o_ref):
        for s in range(N_CHIPS):
            o_ref[:, s * cols:(s + 1) * cols] = w_ref[s]

    return pl.pallas_call(
        body,
        grid=(rows // tr,),
        in_specs=[pl.BlockSpec((N_CHIPS, tr, cols), lambda i: (0, i, 0))],
        out_specs=pl.BlockSpec((tr, N_CHIPS * cols), lambda i: (i, 0)),
        out_shape=jax.ShapeDtypeStruct((rows, N_CHIPS * cols), w4.dtype),
        compiler_params=pltpu.CompilerParams(dimension_semantics=("parallel",)),
        name="join_shards",
    )(w4)


def _split_to_shards(pieces, *, name):
    t = pieces[0].shape[0]
    widths = [p.shape[1] for p in pieces]
    cols = sum(widths) // N_CHIPS
    tm = 512
    plan, start = [], 0
    for p, wd in enumerate(widths):
        for s in range(N_CHIPS):
            lo, hi = max(start, s * cols), min(start + wd, (s + 1) * cols)
            if lo < hi:
                plan.append((s, p, lo - s * cols, hi - s * cols, lo - start, hi - start))
        start += wd

    def body(*refs):
        o_ref = refs[-1]
        for s, p, o_lo, o_hi, p_lo, p_hi in plan:
            o_ref[s, :, o_lo:o_hi] = refs[p][:, p_lo:p_hi]

    return pl.pallas_call(
        body,
        grid=(t // tm,),
        in_specs=[pl.BlockSpec((tm, wd), lambda i: (i, 0)) for wd in widths],
        out_specs=pl.BlockSpec((N_CHIPS, tm, cols), lambda i: (0, i, 0)),
        out_shape=jax.ShapeDtypeStruct((N_CHIPS, t, cols), pieces[0].dtype),
        compiler_params=pltpu.CompilerParams(dimension_semantics=("parallel",)),
        name=name,
    )(*pieces)


def _swiglu_bwd_epilogue(df, g, u):
    g = g.astype(F32)
    u = u.astype(F32)
    sg = _sigmoid(g)
    return df * u * (sg * (1.0 + g * (1.0 - sg))), df * (g * sg)


def _branch_merge_fwd(o_a, o_b, o_c, w_sb, w_dil, w_mem, gates):
    m, tm = o_a.shape[0], 256

    def body(oa_ref, ob_ref, oc_ref, wa_ref, wb_ref, wc_ref, gt_ref, ya_ref, yb_ref, yc_ref, mg_ref):
        def project(o_ref, w_ref):
            o = o_ref[...]
            return jnp.concatenate([_dot(o, w_ref[s]) for s in range(N_CHIPS)], axis=1)

        ya = project(oa_ref, wa_ref)
        yb = project(ob_ref, wb_ref)
        yc = project(oc_ref, wc_ref)
        gt = gt_ref[...].astype(F32)
        ya_ref[...] = ya.astype(BF16)
        yb_ref[...] = yb.astype(BF16)
        yc_ref[...] = yc.astype(BF16)
        mg_ref[...] = (gt[:, :D_MODEL] * ya + gt[:, D_MODEL:2 * D_MODEL] * yb + gt[:, 2 * D_MODEL:] * yc).astype(BF16)

    row = lambda c: pl.BlockSpec((tm, c), lambda i: (i, 0))
    full = lambda a: pl.BlockSpec(a.shape, lambda i: (0, 0, 0))
    return pl.pallas_call(
        body,
        grid=(m // tm,),
        in_specs=[row(SB_W), row(DIL_W), row(MEM_W), full(w_sb), full(w_dil), full(w_mem), row(3 * D_MODEL)],
        out_specs=[row(D_MODEL)] * 4,
        out_shape=[jax.ShapeDtypeStruct((m, D_MODEL), BF16)] * 4,
        compiler_params=pltpu.CompilerParams(dimension_semantics=("parallel",)),
        name="branch_merge_fwd",
    )(o_a, o_b, o_c, w_sb, w_dil, w_mem, gates)


SB_T = 256
SB_SCALE = HEAD_DIM ** -0.5


def _sb_masks():
    row = lax.broadcasted_iota(jnp.int32, (SB_T, SB_T), 0)
    col = lax.broadcasted_iota(jnp.int32, (SB_T, SB_T), 1)
    lane = lax.broadcasted_iota(jnp.int32, (1, LANES), 1)
    return row, col, lane


def _sb_logs(z):
    lb = jnp.minimum(z, 0.0) - jnp.log(1.0 + jnp.exp(-jnp.abs(z)))
    return lb, lb - z


def _sb_specs(n_heads_pairs, col0):
    q = pl.BlockSpec((None, SB_T, LANES), lambda b, p, i: (b, i, col0 + p))
    k = pl.BlockSpec((None, SEQ, LANES), lambda b, p, i: (b, 0, col0 + n_heads_pairs + p))
    v = pl.BlockSpec((None, SEQ, LANES), lambda b, p, i: (b, 0, col0 + 2 * n_heads_pairs + p))
    return q, k, v


def _grid_step(n_pairs, nq):
    return (pl.program_id(0) * n_pairs + pl.program_id(1)) * nq + pl.program_id(2)


def _sb_fwd(proj3, late_shards):
    bl = proj3.shape[0]
    n_pairs = SB_W // LANES
    nq = SEQ // SB_T
    n_late = len(late_shards)
    n_steps = bl * n_pairs * nq

    def body(q_ref, k_ref, v_ref, *rest):
        late_in, (o_ref, o32_ref, w_ref), late_out = rest[:n_late], rest[n_late:n_late + 3], rest[n_late + 3:2 * n_late + 3]
        step = _grid_step(n_pairs, nq)
        if n_late:
            send, forward, finish = _gather_phases(late_in, late_out, *rest[2 * n_late + 3:])
            pl.when(step == 0)(send)
            pl.when(step == n_steps // 2)(forward)
        i = pl.program_id(2)
        row, col, lane = _sb_masks()
        causal = col < row
        u_excl = (row > col).astype(BF16)
        q = q_ref[...]
        heads = []
        for h in range(2):
            mh = (lane // HEAD_DIM) == h
            heads.append((mh, jnp.where(mh, q, jnp.zeros_like(q)) * SB_SCALE))

        def blocks(js, diags, carries, acc):
            ks = [k_ref[pl.ds(pl.multiple_of(j * SB_T, SB_T), SB_T), :] for j in js]
            vs = [v_ref[pl.ds(pl.multiple_of(j * SB_T, SB_T), SB_T), :] for j in js]
            chains = [(b, h) for b in range(len(js)) for h in range(2)]
            z = {c: _dot_nt(heads[c[1]][1], ks[c[0]]) for c in chains}
            lb, lk = {}, {}
            for c in chains:
                lb[c], lk[c] = _sb_logs(z[c])
                if diags[c[0]]:
                    lk[c] = jnp.where(causal, lk[c], 0.0)
            r = {c: _split_dot(lk[c], u_excl) for c in chains}
            carries = list(carries)
            w = {}
            for b, h in chains:
                w_c = jnp.exp(lb[b, h] + r[b, h] + carries[h])
                w[b, h] = (jnp.where(causal, w_c, 0.0) if diags[b] else w_c).astype(BF16)
                w_ref[h, js[b]] = w[b, h]
                carries[h] = carries[h] + (r[b, h][:, :1] + lk[b, h][:, :1])
            for b, h in chains:
                acc = acc + _dot(w[b, h], jnp.where(heads[h][0], vs[b], jnp.zeros_like(vs[b])))
            return tuple(carries), acc

        zero = jnp.zeros((SB_T, 1), F32)
        init = ((zero, zero), jnp.zeros((SB_T, LANES), F32))
        odd = i % 2
        carries, acc = lax.cond(odd == 1, lambda: blocks([i, i - 1], (True, False), *init), lambda: blocks([i], (True,), *init))
        rest = i - 1 - odd
        carries, acc = lax.fori_loop(
            0, i // 2, lambda jj, c: blocks([rest - 2 * jj, rest - 1 - 2 * jj], (False, False), c[0], c[1]), (carries, acc))
        o_ref[...] = acc.astype(BF16)
        o32_ref[...] = acc
        if n_late:
            pl.when(step == n_steps - 1)(finish)

    q_spec, k_spec, v_spec = _sb_specs(n_pairs, 0)
    blk = pl.BlockSpec((None, SB_T, LANES), lambda b, p, i: (b, i, p))
    out = pl.pallas_call(
        body,
        grid=(bl, n_pairs, nq),
        in_specs=[q_spec, k_spec, v_spec] + [ANY] * n_late,
        out_specs=[blk, blk, _sb_weight_spec(nq)] + [ANY] * n_late,
        out_shape=[jax.ShapeDtypeStruct((bl, SEQ, SB_W), BF16), jax.ShapeDtypeStruct((bl, SEQ, SB_W), F32),
                   jax.ShapeDtypeStruct((bl, n_pairs, nq, 2, nq, SB_T, SB_T), BF16)] + _gather_out_shapes(late_shards),
        scratch_shapes=_gather_sems(n_late) if n_late else [],
        compiler_params=pltpu.CompilerParams(dimension_semantics=("arbitrary", "arbitrary", "arbitrary")),
        name="sb_fwd",
    )(proj3, proj3, proj3, *late_shards)
    return out[0], out[1], out[2], out[3:]


def _sb_weight_spec(nq):
    return pl.BlockSpec((None, None, None, 2, nq, SB_T, SB_T), lambda b, p, i: (b, p, i, 0, 0, 0, 0))


def _sb_bwd(proj3, o_a, do_a, w_all, parts):
    bl = proj3.shape[0]
    n_pairs = SB_W // LANES
    nq = SEQ // SB_T
    n_parts = len(parts)
    n_steps = bl * n_pairs * nq

    def body(q_ref, k_ref, v_ref, o_ref, do_ref, w_ref, *rest):
        p_refs, (dq_ref, dk_ref, dv_ref), land_refs = rest[:n_parts], rest[n_parts:n_parts + 3], rest[n_parts + 3:2 * n_parts + 3]
        dk_acc, dv_acc = rest[2 * n_parts + 3:2 * n_parts + 5]
        step = _grid_step(n_pairs, nq)
        if n_parts:
            send, finish = _chip_exchange_phases(p_refs, land_refs, *rest[2 * n_parts + 5:])
            pl.when(step == 0)(send)
        i = pl.program_id(2)

        @pl.when(i == 0)
        def _():
            dk_acc[...] = jnp.zeros_like(dk_acc)
            dv_acc[...] = jnp.zeros_like(dv_acc)

        row, col, lane = _sb_masks()
        causal = col < row
        u_incl = (row >= col).astype(BF16)
        q = q_ref[...]
        do = do_ref[...]
        prod = do.astype(F32) * o_ref[...]
        heads = []
        for h in range(2):
            mh = (lane // HEAD_DIM) == h
            d_tot = jnp.sum(jnp.where(mh, prod, 0.0), axis=1, keepdims=True)
            heads.append((mh, jnp.where(mh, q, jnp.zeros_like(q)) * SB_SCALE, jnp.where(mh, do, jnp.zeros_like(do)), d_tot))

        def blocks(js, diags, c_das, dq):
            starts = [pl.multiple_of(j * SB_T, SB_T) for j in js]
            ks = [k_ref[pl.ds(s, SB_T), :] for s in starts]
            vs = [v_ref[pl.ds(s, SB_T), :] for s in starts]
            chains = [(b, h) for b in range(len(js)) for h in range(2)]
            z = {c: _dot_nt(heads[c[1]][1], ks[c[0]]) for c in chains}
            dw = {c: _dot_nt(heads[c[1]][2], vs[c[0]]) for c in chains}
            wb = {(b, h): w_ref[h, js[b]] for b, h in chains}
            da = {c: dw[c] * wb[c].astype(F32) for c in chains}
            sfx = {c: _split_dot(da[c], u_incl) for c in chains}
            c_das = list(c_das)
            dz = {}
            for b, h in chains:
                dlk = heads[h][3] - c_das[h] - sfx[b, h]
                if diags[b]:
                    dlk = jnp.where(causal, dlk, 0.0)
                c_das[h] = c_das[h] + sfx[b, h][:, :1]
                e = jnp.exp(-jnp.abs(z[b, h]))
                inv = 1.0 / (1.0 + e)
                pos = z[b, h] >= 0.0
                beta = jnp.where(pos, inv, e * inv)
                one_m_beta = jnp.where(pos, e * inv, inv)
                dz[b, h] = (da[b, h] * one_m_beta - dlk * beta).astype(BF16)
            for b, h in chains:
                dq = dq + _dot(dz[b, h], jnp.where(heads[h][0], ks[b], jnp.zeros_like(ks[b])))
            for b in range(len(js)):
                dk_acc[pl.ds(starts[b], SB_T), :] += _dot_tn(dz[b, 0], heads[0][1]) + _dot_tn(dz[b, 1], heads[1][1])
                dv_acc[pl.ds(starts[b], SB_T), :] += _dot_tn(wb[b, 0], heads[0][2]) + _dot_tn(wb[b, 1], heads[1][2])
            return tuple(c_das), dq

        zero = jnp.zeros((SB_T, 1), F32)
        init = ((zero, zero), jnp.zeros((SB_T, LANES), F32))
        odd = i % 2
        state = lax.cond(odd == 1, lambda: blocks([i, i - 1], (True, False), *init), lambda: blocks([i], (True,), *init))
        rest = i - 1 - odd
        state = lax.fori_loop(0, i // 2, lambda jj, c: blocks([rest - 2 * jj, rest - 1 - 2 * jj], (False, False), c[0], c[1]), state)
        dq_ref[...] = (state[1] * SB_SCALE).astype(BF16)

        @pl.when(i == nq - 1)
        def _():
            dk_ref[...] = dk_acc[...].astype(BF16)
            dv_ref[...] = dv_acc[...].astype(BF16)

        if n_parts:
            pl.when(step == n_steps - 1)(finish)

    q_spec, k_spec, v_spec = _sb_specs(n_pairs, 0)
    blk = pl.BlockSpec((None, SB_T, LANES), lambda b, p, i: (b, i, p))
    seq = pl.BlockSpec((None, SEQ, LANES), lambda b, p, i: (b, 0, p))
    shape = jax.ShapeDtypeStruct((bl, SEQ, SB_W), BF16)
    out = pl.pallas_call(
        body,
        grid=(bl, n_pairs, nq),
        in_specs=[q_spec, k_spec, v_spec, blk, blk, _sb_weight_spec(nq)] + [ANY] * n_parts,
        out_specs=[blk, seq, seq] + [ANY] * n_parts,
        out_shape=[shape, shape, shape] + [jax.ShapeDtypeStruct(p.shape, p.dtype) for p in parts],
        scratch_shapes=[pltpu.VMEM((SEQ, LANES), F32), pltpu.VMEM((SEQ, LANES), F32)]
        + (_chip_exchange_sems(n_parts) if n_parts else []),
        compiler_params=pltpu.CompilerParams(dimension_semantics=("arbitrary", "arbitrary", "arbitrary")),
        name="sb_bwd",
    )(proj3, proj3, proj3, o_a, do_a, w_all, *parts)
    return out[0], out[1], out[2], out[3:]


BAND = 128


BAND_CH = 4
BAND_HEADS = DIL_W // HEAD_DIM


def _swap_half(x):
    n = x.shape[-1]
    lane = lax.broadcasted_iota(jnp.int32, (1, n), 1)
    return jnp.where((lane % HEAD_DIM) < HEAD_DIM // 2, pltpu.roll(x, n - HEAD_DIM // 2, 1), pltpu.roll(x, HEAD_DIM // 2, 1))


def _rope(x, cos, sin_signed):
    x = x.astype(F32)
    return x * cos + _swap_half(x) * sin_signed


def _band_valid(g, blk):
    nb = jnp.where(g == 0, 16, jnp.where(g == 1, 4, 1))
    first_key = jnp.where(lax.rem(blk, nb) != 0, 0, BAND)
    qi = lax.broadcasted_iota(jnp.int32, (BAND, 2 * BAND), 0) + BAND
    kj = lax.broadcasted_iota(jnp.int32, (BAND, 2 * BAND), 1)
    dist = qi - kj
    return (dist >= 0) & (dist <= BAND) & (kj >= first_key)


def _band_specs():
    last_before = lambda i: jnp.maximum(i * BAND_CH - 1, 0)
    cur = lambda col: pl.BlockSpec((None, BAND_CH, BAND, DIL_W), lambda g, i: (g, i, 0, col))
    prev = lambda col: pl.BlockSpec((None, None, BAND, DIL_W), lambda g, i: (g, last_before(i), 0, col))
    tab = pl.BlockSpec((None, BAND_CH, BAND, DIL_W), lambda g, i: (g, lax.rem(i, 16 // BAND_CH), 0, 0))
    tab_prev = pl.BlockSpec((None, None, BAND, DIL_W), lambda g, i: (g, lax.rem(last_before(i), 16), 0, 0))
    return cur, prev, tab, tab_prev


def _band_load(q_ref, k_ref, kp_ref, v_ref, vp_ref, c_ref, s_ref, cp_ref, sp_ref):
    qs = [(_rope(q_ref[b], c_ref[b], s_ref[b]) * SB_SCALE).astype(BF16) for b in range(BAND_CH)]
    ks = [_rope(kp_ref[...], cp_ref[...], sp_ref[...]).astype(BF16)]
    ks += [_rope(k_ref[b], c_ref[b], s_ref[b]).astype(BF16) for b in range(BAND_CH)]
    vs = [vp_ref[...]] + [v_ref[b] for b in range(BAND_CH)]
    k2 = [jnp.concatenate([ks[b], ks[b + 1]], axis=0) for b in range(BAND_CH)]
    v2 = [jnp.concatenate([vs[b], vs[b + 1]], axis=0) for b in range(BAND_CH)]
    return qs, k2, v2


def _band_fwd(qkv_s, cos_t, sin_t):
    def body(q_ref, k_ref, kp_ref, v_ref, vp_ref, c_ref, s_ref, cp_ref, sp_ref, ol_ref):
        g, i = pl.program_id(0), pl.program_id(1)
        qs, k2, v2 = _band_load(q_ref, k_ref, kp_ref, v_ref, vp_ref, c_ref, s_ref, cp_ref, sp_ref)
        lane = lax.broadcasted_iota(jnp.int32, (1, DIL_W), 1)
        for b in range(BAND_CH):
            valid = _band_valid(g, i * BAND_CH + b)
            hs = range(BAND_HEADS)
            mh = [(lane // HEAD_DIM) == h for h in hs]
            s = [jnp.where(valid, _dot_nt(jnp.where(mh[h], qs[b], jnp.zeros_like(qs[b])), k2[b]), NEG_INF) for h in hs]
            m = [jnp.max(s[h], axis=1, keepdims=True) for h in hs]
            p = [jnp.exp(s[h] - m[h]) for h in hs]
            den = [jnp.sum(p[h], axis=1, keepdims=True) for h in hs]
            pv = [_dot(p[h].astype(BF16), jnp.where(mh[h], v2[b], jnp.zeros_like(v2[b]))) for h in hs]
            o = jnp.zeros((BAND, DIL_W), F32)
            lse = jnp.zeros((BAND, DIL_W), F32)
            for h in hs:
                o = o + pv[h] * (1.0 / den[h])
                lse = jnp.where(mh[h], m[h] + jnp.log(den[h]), lse)
            ol_ref[b, :, :DIL_W] = o
            ol_ref[b, :, DIL_W:] = lse

    cur, prev, tab, tab_prev = _band_specs()
    n_blk = qkv_s.shape[1]
    return pl.pallas_call(
        body,
        grid=(3, n_blk // BAND_CH),
        in_specs=[cur(0), cur(1), prev(1), cur(2), prev(2), tab, tab, tab_prev, tab_prev],
        out_specs=pl.BlockSpec((None, BAND_CH, BAND, 2 * DIL_W), lambda g, i: (g, i, 0, 0)),
        out_shape=jax.ShapeDtypeStruct((3, n_blk, BAND, 2 * DIL_W), F32),
        compiler_params=pltpu.CompilerParams(dimension_semantics=("parallel", "parallel")),
        name="band_fwd",
    )(qkv_s, qkv_s, qkv_s, qkv_s, qkv_s, cos_t, sin_t, cos_t, sin_t)


def _band_bwd(qkv_s, cos_t, sin_t, dcat_s):
    def body(q_ref, k_ref, kp_ref, v_ref, vp_ref, c_ref, s_ref, cp_ref, sp_ref, do_ref, lse_ref, dl_ref,
             dq_ref, dk_ref, dv_ref, dkf_ref, dvf_ref):
        g, i = pl.program_id(0), pl.program_id(1)
        qs, k2, v2 = _band_load(q_ref, k_ref, kp_ref, v_ref, vp_ref, c_ref, s_ref, cp_ref, sp_ref)
        lane = lax.broadcasted_iota(jnp.int32, (1, DIL_W), 1)
        dks, dvs = [], []
        for b in range(BAND_CH):
            valid = _band_valid(g, i * BAND_CH + b)
            do, lse, dl = do_ref[b].astype(BF16), lse_ref[b], dl_ref[b]
            hs = range(BAND_HEADS)
            mh = [(lane // HEAD_DIM) == h for h in hs]
            qh = [jnp.where(mh[h], qs[b], jnp.zeros_like(qs[b])) for h in hs]
            doh = [jnp.where(mh[h], do, jnp.zeros_like(do)) for h in hs]
            s = [_dot_nt(qh[h], k2[b]) for h in hs]
            dp = [_dot_nt(doh[h], v2[b]) for h in hs]
            p = [jnp.where(valid, jnp.exp(s[h] - lse[:, h * HEAD_DIM:h * HEAD_DIM + 1]), 0.0) for h in hs]
            ds = [(p[h] * (dp[h] - dl[:, h * HEAD_DIM:h * HEAD_DIM + 1])).astype(BF16) for h in hs]
            pb = [p[h].astype(BF16) for h in hs]
            dq = sum(_dot(ds[h], jnp.where(mh[h], k2[b], jnp.zeros_like(k2[b]))) for h in hs)
            dk2 = sum(_dot_tn(ds[h], qh[h]) for h in hs)
            dv2 = sum(_dot_tn(pb[h], doh[h]) for h in hs)
            dq_ref[b] = dq * SB_SCALE
            dks.append(dk2)
            dvs.append(dv2)
        dkf_ref[...] = dks[0][:BAND]
        dvf_ref[...] = dvs[0][:BAND]
        for b in range(BAND_CH):
            last = b == BAND_CH - 1
            dk_ref[b] = dks[b][BAND:] if last else dks[b][BAND:] + dks[b + 1][:BAND]
            dv_ref[b] = dvs[b][BAND:] if last else dvs[b][BAND:] + dvs[b + 1][:BAND]

    cur, prev, tab, tab_prev = _band_specs()
    first = pl.BlockSpec((None, None, BAND, DIL_W), lambda g, i: (g, i, 0, 0))
    n_blk = qkv_s.shape[1]
    n_chunks = n_blk // BAND_CH
    shape = jax.ShapeDtypeStruct((3, n_blk, BAND, DIL_W), F32)
    shape_first = jax.ShapeDtypeStruct((3, n_chunks, BAND, DIL_W), F32)
    return pl.pallas_call(
        body,
        grid=(3, n_chunks),
        in_specs=[cur(0), cur(1), prev(1), cur(2), prev(2), tab, tab, tab_prev, tab_prev, cur(0), cur(1), cur(2)],
        out_specs=[cur(0), cur(0), cur(0), first, first],
        out_shape=[shape, shape, shape, shape_first, shape_first],
        compiler_params=pltpu.CompilerParams(dimension_semantics=("parallel", "parallel")),
        name="band_bwd",
    )(qkv_s, qkv_s, qkv_s, qkv_s, qkv_s, cos_t, sin_t, cos_t, sin_t, dcat_s, dcat_s, dcat_s)


def _band_combine(dq, dk, dv, dk_first, dv_first, cos_t, sin_t):
    n_chunks = dk_first.shape[1]

    def body(dq_ref, dk_ref, dkn_ref, dv_ref, dvn_ref, c_ref, s_ref, out_ref):
        nxt = (pl.program_id(1) < n_chunks - 1).astype(F32)
        for b in range(BAND_CH):
            cos, sin = c_ref[b], s_ref[b]
            dq_b, dk_b, dv_b = dq_ref[b], dk_ref[b], dv_ref[b]
            if b == BAND_CH - 1:
                dk_b = dk_b + nxt * dkn_ref[...]
                dv_b = dv_b + nxt * dvn_ref[...]
            out_ref[b, :, :DIL_W] = (dq_b * cos - _swap_half(dq_b) * sin).astype(BF16)
            out_ref[b, :, DIL_W:2 * DIL_W] = (dk_b * cos - _swap_half(dk_b) * sin).astype(BF16)
            out_ref[b, :, 2 * DIL_W:] = dv_b.astype(BF16)

    cur, _, tab, _ = _band_specs()
    nxt = pl.BlockSpec((None, None, BAND, DIL_W), lambda g, i: (g, jnp.minimum(i + 1, n_chunks - 1), 0, 0))
    return pl.pallas_call(
        body,
        grid=(3, n_chunks),
        in_specs=[cur(0), cur(0), nxt, cur(0), nxt, tab, tab],
        out_specs=pl.BlockSpec((None, BAND_CH, BAND, 3 * DIL_W), lambda g, i: (g, i, 0, 0)),
        out_shape=jax.ShapeDtypeStruct(dq.shape[:3] + (3 * DIL_W,), BF16),
        compiler_params=pltpu.CompilerParams(dimension_semantics=("parallel", "parallel")),
        name="band_combine",
    )(dq, dk, dk_first, dv, dv_first, cos_t, sin_t)


def _band_merge(ol):
    t, tm = ol.shape[1], 512

    def body(o_ref, l_ref, ob_ref, lse_ref):
        l0, l1, l2 = l_ref[0], l_ref[1], l_ref[2]
        m = jnp.maximum(jnp.maximum(l0, l1), l2)
        lse = m + jnp.log(jnp.exp(l0 - m) + jnp.exp(l1 - m) + jnp.exp(l2 - m))
        ob_ref[...] = (jnp.exp(l0 - lse) * o_ref[0] + jnp.exp(l1 - lse) * o_ref[1] + jnp.exp(l2 - lse) * o_ref[2]).astype(BF16)
        lse_ref[...] = lse

    spec = pl.BlockSpec((tm, DIL_W), lambda i: (i, 0))
    return pl.pallas_call(
        body,
        grid=(t // tm,),
        in_specs=[pl.BlockSpec((3, tm, DIL_W), lambda i: (0, i, 0)), pl.BlockSpec((3, tm, DIL_W), lambda i: (0, i, 1))],
        out_specs=[spec, spec],
        out_shape=[jax.ShapeDtypeStruct((t, DIL_W), BF16), jax.ShapeDtypeStruct((t, DIL_W), F32)],
        compiler_params=pltpu.CompilerParams(dimension_semantics=("parallel",)),
        name="band_merge",
    )(ol, ol)


def _band_delta(do_b, o_b, lse_b):
    def fn(do, o, lse):
        lane_in = lax.broadcasted_iota(jnp.int32, (DIL_W, LANES), 0)
        col = lax.broadcasted_iota(jnp.int32, (DIL_W, LANES), 1)
        pick_lse = ((lane_in == col * HEAD_DIM) & (col < BAND_HEADS)).astype(BF16)
        sum_head = ((lane_in // HEAD_DIM == col - BAND_HEADS) & (col >= BAND_HEADS) & (col < 2 * BAND_HEADS)).astype(BF16)
        hi = lse.astype(BF16)
        rest = lse - hi.astype(F32)
        return (_dot(hi, pick_lse) + _split_dot(rest, pick_lse) + _split_dot(do.astype(F32) * o.astype(F32), sum_head),)

    return _rowwise(fn, [(do_b, "row"), (o_b, "row"), (lse_b, "row")], [(LANES, F32, "row")], tm=512, name="band_delta")[0]


def _band_masks():
    qi = lax.broadcasted_iota(jnp.int32, (BAND, 2 * BAND), 0) + BAND
    kj = lax.broadcasted_iota(jnp.int32, (BAND, 2 * BAND), 1)
    dist = qi - kj
    row = lax.broadcasted_iota(jnp.int32, (BAND, BAND), 0)
    col = lax.broadcasted_iota(jnp.int32, (BAND, BAND), 1)
    return col <= row, (dist >= 0) & (dist <= BAND)


def _band_attend(q, k, v, valid):
    lane = lax.broadcasted_iota(jnp.int32, (1, DIL_W), 1)
    hs = range(BAND_HEADS)
    mh = [(lane // HEAD_DIM) == h for h in hs]
    s = [jnp.where(valid, _dot_nt(jnp.where(mh[h], q, jnp.zeros_like(q)), k), NEG_INF) for h in hs]
    m = [jnp.max(s[h], axis=1, keepdims=True) for h in hs]
    p = [jnp.exp(s[h] - m[h]) for h in hs]
    den = [jnp.sum(p[h], axis=1, keepdims=True) for h in hs]
    pv = [_dot(p[h].astype(BF16), jnp.where(mh[h], v, jnp.zeros_like(v))) for h in hs]
    o = jnp.zeros((BAND, DIL_W), F32)
    lse = jnp.zeros((BAND, DIL_W), F32)
    for h in hs:
        o = o + pv[h] * (1.0 / den[h])
        lse = jnp.where(mh[h], m[h] + jnp.log(den[h]), lse)
    return o, lse


def _band_attend_bwd(q, k, v, valid, do, st):
    lane = lax.broadcasted_iota(jnp.int32, (1, DIL_W), 1)
    hs = range(BAND_HEADS)
    mh = [(lane // HEAD_DIM) == h for h in hs]
    qh = [jnp.where(mh[h], q, jnp.zeros_like(q)) for h in hs]
    doh = [jnp.where(mh[h], do, jnp.zeros_like(do)) for h in hs]
    s = [_dot_nt(qh[h], k) for h in hs]
    dp = [_dot_nt(doh[h], v) for h in hs]
    p = [jnp.where(valid, jnp.exp(s[h] - st[:, h:h + 1]), 0.0) for h in hs]
    ds = [(p[h] * (dp[h] - st[:, BAND_HEADS + h:BAND_HEADS + h + 1])).astype(BF16) for h in hs]
    pb = [p[h].astype(BF16) for h in hs]
    dq = sum(_dot(ds[h], jnp.where(mh[h], k, jnp.zeros_like(k))) for h in hs)
    dk = sum(_dot_tn(ds[h], qh[h]) for h in hs)
    dv = sum(_dot_tn(pb[h], doh[h]) for h in hs)
    return dq, dk, dv


def _band_group_specs(lead, rows, cls, col0):
    def spec(width):
        if lead == "rows":
            return pl.BlockSpec((None, rows, width), lambda b, i: (b, 0, col0))
        return pl.BlockSpec((None, rows, cls * width), lambda b, i: (b, 0, i))
    return spec


def _band_group_fwd(a, cos_g, sin_g, *, rows, cls, steps, col0, name):
    bl = a.shape[0]
    nb = rows // BAND
    grp_w = 3 * DIL_W

    def body(a_ref, c_ref, s_ref, ol_ref, qr, kr):
        first_valid, later_valid = _band_masks()
        for j in range(cls):
            a0, t0, o0 = j * grp_w, j * DIL_W, j * 2 * DIL_W
            cos, sin = c_ref[:, t0:t0 + DIL_W], s_ref[:, t0:t0 + DIL_W]
            qr[...] = (_rope(a_ref[:, a0:a0 + DIL_W], cos, sin) * SB_SCALE).astype(BF16)
            kr[...] = _rope(a_ref[:, a0 + DIL_W:a0 + 2 * DIL_W], cos, sin).astype(BF16)

            def block(q0, k0, keys, valid, a0=a0, o0=o0):
                o, lse = _band_attend(qr[pl.ds(q0, BAND), :], kr[pl.ds(k0, keys), :],
                                      a_ref[pl.ds(k0, keys), a0 + 2 * DIL_W:a0 + grp_w], valid)
                ol_ref[pl.ds(q0, BAND), o0:o0 + DIL_W] = o
                ol_ref[pl.ds(q0, BAND), o0 + DIL_W:o0 + 2 * DIL_W] = lse

            block(0, 0, BAND, first_valid)
            if nb > 1:
                def later(b, carry, block=block):
                    block(pl.multiple_of(b * BAND, BAND), pl.multiple_of((b - 1) * BAND, BAND), 2 * BAND, later_valid)
                    return carry

                lax.fori_loop(1, nb, later, 0)

    lead = "rows" if col0 is not None else "cols"
    spec = _band_group_specs(lead, rows, cls, col0)
    tab = pl.BlockSpec((rows, cls * DIL_W), lambda b, i: (0, i))
    n_cls = cos_g.shape[1] // DIL_W
    return pl.pallas_call(
        body,
        grid=(bl, steps),
        in_specs=[spec(grp_w), tab, tab],
        out_specs=pl.BlockSpec((None, rows, cls * 2 * DIL_W), lambda b, i: (b, 0, i)),
        out_shape=jax.ShapeDtypeStruct((bl, rows, n_cls * 2 * DIL_W), F32),
        scratch_shapes=[pltpu.VMEM((rows, DIL_W), BF16), pltpu.VMEM((rows, DIL_W), BF16)],
        compiler_params=pltpu.CompilerParams(dimension_semantics=("parallel", "parallel")),
        name=name,
    )(a, cos_g, sin_g)


def _band_group_bwd(a, do, st, cos_g, sin_g, *, rows, cls, steps, col0, name, side=None):
    bl = a.shape[0]
    nb = rows // BAND
    grp_w = 3 * DIL_W
    side_arrays = side[1] if side else []
    n_side = len(side_arrays)

    def body(a_ref, do_ref, st_ref, c_ref, s_ref, *rest):
        out_ref = rest[n_side]
        qr, kr, dk_acc, dv_acc = rest[2 * n_side + 1:2 * n_side + 5]
        if n_side:
            step = pl.program_id(0) * steps + pl.program_id(1)
            finish = _run_side(side, rest[:n_side], rest[n_side + 1:2 * n_side + 1], rest[2 * n_side + 5:], step, bl * steps)
        first_valid, later_valid = _band_masks()
        for j in range(cls):
            a0, t0 = j * grp_w, j * DIL_W
            cos, sin = c_ref[:, t0:t0 + DIL_W], s_ref[:, t0:t0 + DIL_W]
            qr[...] = (_rope(a_ref[:, a0:a0 + DIL_W], cos, sin) * SB_SCALE).astype(BF16)
            kr[...] = _rope(a_ref[:, a0 + DIL_W:a0 + 2 * DIL_W], cos, sin).astype(BF16)
            dk_acc[...] = jnp.zeros_like(dk_acc)
            dv_acc[...] = jnp.zeros_like(dv_acc)

            def block(q0, k0, keys, valid, a0=a0, t0=t0, s0=j * LANES):
                qrows, krows = pl.ds(q0, BAND), pl.ds(k0, keys)
                dq, dk, dv = _band_attend_bwd(
                    qr[qrows, :], kr[krows, :], a_ref[krows, a0 + 2 * DIL_W:a0 + grp_w], valid,
                    do_ref[qrows, t0:t0 + DIL_W], st_ref[qrows, s0:s0 + LANES])
                dq = dq * SB_SCALE
                out_ref[qrows, a0:a0 + DIL_W] = (dq * c_ref[qrows, t0:t0 + DIL_W]
                                                 - _swap_half(dq) * s_ref[qrows, t0:t0 + DIL_W]).astype(BF16)
                dk_acc[krows, :] += dk
                dv_acc[krows, :] += dv

            block(0, 0, BAND, first_valid)
            if nb > 1:
                def later(b, carry, block=block):
                    block(pl.multiple_of(b * BAND, BAND), pl.multiple_of((b - 1) * BAND, BAND), 2 * BAND, later_valid)
                    return carry

                lax.fori_loop(1, nb, later, 0)
            dk = dk_acc[...]
            out_ref[:, a0 + DIL_W:a0 + 2 * DIL_W] = (dk * cos - _swap_half(dk) * sin).astype(BF16)
            out_ref[:, a0 + 2 * DIL_W:a0 + grp_w] = dv_acc[...].astype(BF16)
        if n_side:
            finish()

    lead = "rows" if col0 is not None else "cols"
    spec = _band_group_specs(lead, rows, cls, col0)
    dspec = _band_group_specs(lead, rows, cls, 0 if col0 is not None else None)
    tab = pl.BlockSpec((rows, cls * DIL_W), lambda b, i: (0, i))
    n_cls = cos_g.shape[1] // DIL_W
    out = pl.pallas_call(
        body,
        grid=(bl, steps),
        in_specs=[spec(grp_w), dspec(DIL_W), dspec(LANES), tab, tab] + [ANY] * n_side,
        out_specs=[pl.BlockSpec((None, rows, cls * grp_w), lambda b, i: (b, 0, i))] + [ANY] * n_side,
        out_shape=[jax.ShapeDtypeStruct((bl, rows, n_cls * grp_w), BF16)] + (_side_out_shapes(side) if n_side else []),
        scratch_shapes=[pltpu.VMEM((rows, DIL_W), BF16), pltpu.VMEM((rows, DIL_W), BF16),
                        pltpu.VMEM((rows, DIL_W), F32), pltpu.VMEM((rows, DIL_W), F32)] + (_side_sems(side) if n_side else []),
        compiler_params=pltpu.CompilerParams(dimension_semantics=("arbitrary", "arbitrary") if n_side else ("parallel", "parallel")),
        name=name,
    )(a, do, st, cos_g, sin_g, *side_arrays)
    return (out[0], out[1:]) if n_side else out[0]


def _band_merge3(ols):
    t, tm = ols[0].shape[0], 512

    def body(o0, l0, o1, l1, o2, l2, ob_ref, lse_ref):
        a, b, c = l0[...], l1[...], l2[...]
        m = jnp.maximum(jnp.maximum(a, b), c)
        lse = m + jnp.log(jnp.exp(a - m) + jnp.exp(b - m) + jnp.exp(c - m))
        ob_ref[...] = (jnp.exp(a - lse) * o0[...] + jnp.exp(b - lse) * o1[...] + jnp.exp(c - lse) * o2[...]).astype(BF16)
        lse_ref[...] = lse

    spec = pl.BlockSpec((tm, DIL_W), lambda i: (i, 0))
    spec_l = pl.BlockSpec((tm, DIL_W), lambda i: (i, 1))
    return pl.pallas_call(
        body,
        grid=(t // tm,),
        in_specs=[spec, spec_l] * 3,
        out_specs=[spec, spec],
        out_shape=[jax.ShapeDtypeStruct((t, DIL_W), BF16), jax.ShapeDtypeStruct((t, DIL_W), F32)],
        compiler_params=pltpu.CompilerParams(dimension_semantics=("parallel",)),
        name="band_merge",
    )(ols[0], ols[0], ols[1], ols[1], ols[2], ols[2])


MEM_T = 512
MEM_SCALE = 128 ** -0.5
MEM_Q_COL = (D_IN - MEM_W) // LANES


def _mem_specs():
    q = pl.BlockSpec((None, MEM_T, LANES), lambda b, h, i: (b, i, MEM_Q_COL + h))
    k = pl.BlockSpec((None, MEM_LEN, LANES), lambda b, h, i: (b, 0, h))
    v = pl.BlockSpec((None, MEM_LEN, LANES), lambda b, h, i: (b, 0, MEM_W // LANES + h))
    blk = pl.BlockSpec((None, MEM_T, LANES), lambda b, h, i: (b, i, h))
    return q, k, v, blk


def _mem_probs(q, k):
    s = _dot_nt(q, k) * MEM_SCALE
    p = jnp.exp(s - jnp.max(s, axis=1, keepdims=True))
    return p * (1.0 / jnp.sum(p, axis=1, keepdims=True))


def _mem_fwd(proj3, kv3):
    bl = proj3.shape[0]

    def body(q_ref, k_ref, v_ref, o_ref):
        p = _mem_probs(q_ref[...], k_ref[...])
        o_ref[...] = _dot(p.astype(BF16), v_ref[...]).astype(BF16)

    q, k, v, blk = _mem_specs()
    return pl.pallas_call(
        body,
        grid=(bl, MEM_W // LANES, SEQ // MEM_T),
        in_specs=[q, k, v],
        out_specs=blk,
        out_shape=jax.ShapeDtypeStruct((bl, SEQ, MEM_W), BF16),
        compiler_params=pltpu.CompilerParams(dimension_semantics=("parallel", "parallel", "parallel")),
        name="mem_fwd",
    )(proj3, kv3, kv3)


def _mem_bwd(proj3, kv3, do_c):
    bl = proj3.shape[0]

    def body(q_ref, k_ref, v_ref, do_ref, dq_ref, dk_ref, dv_ref):
        @pl.when(pl.program_id(2) == 0)
        def _():
            dk_ref[...] = jnp.zeros_like(dk_ref)
            dv_ref[...] = jnp.zeros_like(dv_ref)

        q, k, do = q_ref[...], k_ref[...], do_ref[...]
        p = _mem_probs(q, k)
        dp = _dot_nt(do, v_ref[...])
        ds = (p * (dp - jnp.sum(p * dp, axis=1, keepdims=True)) * MEM_SCALE).astype(BF16)
        dq_ref[...] = _dot(ds, k).astype(BF16)
        dk_ref[...] += _dot_tn(ds, q)
        dv_ref[...] += _dot_tn(p.astype(BF16), do)

    q, k, v, blk = _mem_specs()
    kv_out = pl.BlockSpec((None, MEM_LEN, LANES), lambda b, h, i: (b, 0, h))
    return pl.pallas_call(
        body,
        grid=(bl, MEM_W // LANES, SEQ // MEM_T),
        in_specs=[q, k, v, blk],
        out_specs=[blk, kv_out, kv_out],
        out_shape=[jax.ShapeDtypeStruct((bl, SEQ, MEM_W), BF16), jax.ShapeDtypeStruct((bl, MEM_LEN, MEM_W), F32),
                   jax.ShapeDtypeStruct((bl, MEM_LEN, MEM_W), F32)],
        compiler_params=pltpu.CompilerParams(dimension_semantics=("parallel", "parallel", "arbitrary")),
        name="mem_bwd",
    )(proj3, kv3, kv3, do_c)


def _place():
    x, y, c = lax.axis_index("x"), lax.axis_index("y"), lax.axis_index("c")
    return x, y, c


def _other_chips(x, y):
    return [(1 - x, y), (x, 1 - y), (1 - x, 1 - y)]


def _remote(src, dst, send_sem, recv_sem, to):
    return pltpu.make_async_remote_copy(src_ref=src, dst_ref=dst, send_sem=send_sem, recv_sem=recv_sem,
                                        device_id=to, device_id_type=MESH)


ANY = pl.BlockSpec(memory_space=pl.ANY)


def _gather_weights(shards):
    n = len(shards)

    def body(*refs):
        send, forward, finish = _gather_phases(refs[:n], refs[n:2 * n], *refs[2 * n:])
        send()
        forward()
        finish()

    return pl.pallas_call(
        body,
        in_specs=[ANY] * n,
        out_specs=[ANY] * n,
        out_shape=_gather_out_shapes(shards),
        scratch_shapes=_gather_sems(n),
        name="gather_weights",
    )(*shards)


def _gather_out_shapes(shards):
    return [jax.ShapeDtypeStruct((N_CHIPS,) + s.shape, s.dtype) for s in shards]


def _gather_sems(n):
    return [pltpu.SemaphoreType.DMA((6 * n,)), pltpu.SemaphoreType.DMA((6 * n,))]


def _gather_phases(in_refs, out_refs, send_sems, recv_sems):
    x, y, c = _place()
    sibling = (x, y, 1 - c)
    chips = _other_chips(x, y)
    first, passed = [], []
    for k in range(len(in_refs)):
        hf = in_refs[k].shape[0] // 2

        def half(px, py, pc, k=k, hf=hf):
            return out_refs[k].at[2 * px + py, pl.ds(pc * hf, hf), :]

        src = in_refs[k].at[pl.ds(c * hf, hf), :]
        for j, chip in enumerate(chips):
            s = 6 * k + j
            first.append(_remote(src, half(x, y, c), send_sems.at[s], recv_sems.at[s], (*chip, c)))
            passed.append((_remote(src, half(*chip, c), send_sems.at[s], recv_sems.at[s], (*chip, c)),
                           _remote(half(*chip, c), half(*chip, c), send_sems.at[s + 3], recv_sems.at[s + 3], sibling),
                           _remote(src, half(*chip, 1 - c), send_sems.at[s + 3], recv_sems.at[s + 3], sibling)))

    def send():
        for cp in first:
            cp.start()

    def forward():
        for landed, fwd, _ in passed:
            landed.wait_recv()
            fwd.start()

    def finish():
        for _, _, from_sibling in passed:
            from_sibling.wait_recv()
        for cp in first:
            cp.wait_send()
        for _, fwd, _ in passed:
            fwd.wait_send()

    return send, forward, finish


def _pair_exchange(grads, *, name):
    n = len(grads)
    side = ("pair", grads)

    def body(*refs):
        send, _, finish = _side_phases(side, refs[:n], refs[n:2 * n], refs[2 * n:])
        send()
        finish()

    return pl.pallas_call(
        body,
        in_specs=[ANY] * n,
        out_specs=[ANY] * n,
        out_shape=_side_out_shapes(side),
        scratch_shapes=_side_sems(side),
        name=name,
    )(*grads)


def _pair_exchange_phases(g_refs, land_refs, send_sems, recv_sems):
    x, y, c = _place()
    cps = []
    for k in range(len(g_refs)):
        hf = g_refs[k].shape[1] // 2
        src = g_refs[k].at[:, pl.ds((1 - c) * hf, hf), :]
        cps.append(_remote(src, land_refs[k], send_sems.at[k], recv_sems.at[k], (x, y, 1 - c)))

    def send():
        for cp in cps:
            cp.start()

    def finish():
        for cp in cps:
            cp.wait()

    return send, finish


def _side_out_shapes(side):
    kind, arrays = side
    if kind == "gather":
        return _gather_out_shapes(arrays)
    if kind == "pair":
        return [jax.ShapeDtypeStruct((N_CHIPS, g.shape[1] // 2, g.shape[2]), g.dtype) for g in arrays]
    return [jax.ShapeDtypeStruct(p.shape, p.dtype) for p in arrays]


def _side_sems(side):
    kind, arrays = side
    n = len(arrays)
    if kind == "gather":
        return _gather_sems(n)
    if kind == "pair":
        return [pltpu.SemaphoreType.DMA((n,)), pltpu.SemaphoreType.DMA((n,))]
    return _chip_exchange_sems(n)


def _side_phases(side, in_refs, out_refs, sems):
    kind = side[0]
    if kind == "gather":
        return _gather_phases(in_refs, out_refs, *sems)
    send, finish = (_pair_exchange_phases if kind == "pair" else _chip_exchange_phases)(in_refs, out_refs, *sems)
    return send, None, finish


def _run_side(side, in_refs, out_refs, sems, step, n_steps):
    first, mid, last = _side_phases(side, in_refs, out_refs, sems)
    pl.when(step == 0)(first)
    if mid is not None:
        pl.when(step == n_steps // 2)(mid)
    return lambda: pl.when(step == n_steps - 1)(last)


def _pair_add(g, land, c_arr, *, name):
    _, a, b = g.shape
    hf = a // 2

    def body(c_ref, g_ref, l_ref, o_ref):
        o_ref[...] = (g_ref[...] + l_ref[...]).astype(BF16)

    return pl.pallas_call(
        body,
        grid_spec=pltpu.PrefetchScalarGridSpec(
            num_scalar_prefetch=1,
            grid=(N_CHIPS,),
            in_specs=[pl.BlockSpec((None, None, hf, b), lambda s, c_ref: (s, c_ref[0], 0, 0)),
                      pl.BlockSpec((None, hf, b), lambda s, c_ref: (s, 0, 0))],
            out_specs=pl.BlockSpec((None, hf, b), lambda s, c_ref: (s, 0, 0)),
        ),
        out_shape=jax.ShapeDtypeStruct((N_CHIPS, hf, b), BF16),
        compiler_params=pltpu.CompilerParams(dimension_semantics=("parallel",)),
        name=name,
    )(c_arr, g.reshape(N_CHIPS, 2, hf, b), land)


def _chip_exchange(parts):
    n = len(parts)

    def body(*refs):
        send, finish = _chip_exchange_phases(refs[:n], refs[n:2 * n], *refs[2 * n:])
        send()
        finish()

    return pl.pallas_call(
        body,
        in_specs=[ANY] * n,
        out_specs=[ANY] * n,
        out_shape=[jax.ShapeDtypeStruct(p.shape, p.dtype) for p in parts],
        scratch_shapes=_chip_exchange_sems(n),
        name="chip_exchange",
    )(*parts)


def _chip_exchange_sems(n):
    return [pltpu.SemaphoreType.DMA((3 * n,)), pltpu.SemaphoreType.DMA((3 * n,))]


def _chip_exchange_phases(p_refs, land_refs, send_sems, recv_sems):
    x, y, c = _place()
    me = 2 * x + y
    sends, recvs = [], []
    for k in range(len(p_refs)):
        for j, (cx, cy) in enumerate(_other_chips(x, y)):
            s = 3 * k + j
            sends.append(_remote(p_refs[k].at[2 * cx + cy], land_refs[k].at[me], send_sems.at[s], recv_sems.at[s], (cx, cy, c)))
            recvs.append(_remote(p_refs[k].at[me], land_refs[k].at[2 * cx + cy], send_sems.at[s], recv_sems.at[s], (cx, cy, c)))

    def send():
        for cp in sends:
            cp.start()

    def finish():
        for cp in recvs:
            cp.wait_recv()
        for cp in sends:
            cp.wait_send()

    return send, finish


def _chip_add(land, part, me_arr, *, name):
    _, r, b = land.shape

    def body(me_ref, p_ref, l1_ref, l2_ref, l3_ref, o_ref):
        o_ref[...] = ((p_ref[...].astype(F32) + l1_ref[...].astype(F32)) + l2_ref[...].astype(F32)) + l3_ref[...].astype(F32)

    tr = r // 2
    other = lambda j: pl.BlockSpec((None, tr, b), lambda i, me_ref: (jnp.bitwise_xor(me_ref[0], j), i, 0))
    return pl.pallas_call(
        body,
        grid_spec=pltpu.PrefetchScalarGridSpec(
            num_scalar_prefetch=1,
            grid=(r // tr,),
            in_specs=[pl.BlockSpec((None, tr, b), lambda i, me_ref: (me_ref[0], i, 0)), other(2), other(1), other(3)],
            out_specs=pl.BlockSpec((tr, b), lambda i, me_ref: (i, 0)),
        ),
        out_shape=jax.ShapeDtypeStruct((r, b), F32),
        compiler_params=pltpu.CompilerParams(dimension_semantics=("parallel",)),
        name=name,
    )(me_arr, part, land, land, land)


def _pair_share(halves):
    n = len(halves)

    def body(*refs):
        h_refs, out_refs = refs[:n], refs[n:2 * n]
        send_sems, recv_sems = refs[2 * n:]
        x, y, c = _place()
        cps = [_remote(h_refs[k], out_refs[k], send_sems.at[k], recv_sems.at[k], (x, y, 1 - c)) for k in range(n)]
        for cp in cps:
            cp.start()
        for cp in cps:
            cp.wait()

    return pl.pallas_call(
        body,
        in_specs=[ANY] * n,
        out_specs=[ANY] * n,
        out_shape=[jax.ShapeDtypeStruct(h.shape, F32) for h in halves],
        scratch_shapes=[pltpu.SemaphoreType.DMA((n,)), pltpu.SemaphoreType.DMA((n,))],
        name="pair_share",
    )(*halves)


def _all_sum_small(part):
    def body(p_ref, o_ref, slots, send_sems, recv_sems):
        x, y, c = _place()
        me = 4 * x + 2 * y + c
        slots[me] = p_ref[...]
        peers = [(x ^ dx, y ^ dy, c ^ dc) for dx in (0, 1) for dy in (0, 1) for dc in (0, 1)][1:]
        sends = [_remote(p_ref, slots.at[me], send_sems.at[k], recv_sems.at[k], peer) for k, peer in enumerate(peers)]
        for cp in sends:
            cp.start()
        for k, (px, py, pc) in enumerate(peers):
            _remote(p_ref, slots.at[4 * px + 2 * py + pc], send_sems.at[k], recv_sems.at[k], (px, py, pc)).wait_recv()
        for cp in sends:
            cp.wait_send()
        acc = slots[0]
        for d in range(1, 8):
            acc = acc + slots[d]
        o_ref[...] = acc

    vmem = pl.BlockSpec(memory_space=pltpu.VMEM)
    return pl.pallas_call(
        body,
        in_specs=[vmem],
        out_specs=vmem,
        out_shape=jax.ShapeDtypeStruct(part.shape, F32),
        scratch_shapes=[pltpu.VMEM((8,) + part.shape, F32), pltpu.SemaphoreType.DMA((7,)), pltpu.SemaphoreType.DMA((7,))],
        name="all_sum_small",
    )(part)


def _deinterleave(a, d):
    b, s, c = a.shape
    return a.reshape(b, s // d, d, c).transpose(0, 2, 1, 3).reshape(b * s // BAND, BAND, c)


def _reinterleave(a, d, b):
    c = a.shape[-1]
    return a.reshape(b, d, SEQ // d, c).transpose(0, 2, 1, 3).reshape(b, SEQ, c)


def _rope_tables():
    half = HEAD_DIM // 2
    inv_freq = np.float32(ROPE_THETA) ** (-np.arange(half, dtype=np.float32) * np.float32(2.0) / np.float32(HEAD_DIM))
    ang = np.arange(SEQ, dtype=np.float32)[:, None] * inv_freq[None, :].astype(np.float32)
    cos = np.tile(np.cos(ang).astype(np.float32), (1, 2 * BAND_HEADS))
    sin = np.tile(np.concatenate([-np.sin(ang), np.sin(ang)], axis=1).astype(np.float32), (1, BAND_HEADS))
    return jnp.asarray(cos), jnp.asarray(sin)


def _band_groups():
    out = []
    for d in DIL_D:
        rows = SEQ // d
        cls = max(1, 512 // rows) if d > 1 else 1
        out.append(dict(rows=rows, cls=cls, steps=d // cls))
    return out


def _local_step(x, mem, loss_target, g_pre_mix, g_post_mix, g_pre_ffn, g_post_ffn, g_mem, b_gate, w, comm=None):
    bl = x.shape[0]
    t = bl * SEQ
    chips = range(N_CHIPS)
    half_ff = D_FF // 2

    def with_gathered(w, names, gathered, shards):
        return {**w, **{name: lax.dynamic_update_slice(g, s[None], (comm["me"][0], 0, 0))
                        for name, g, s in zip(names, gathered, shards)}}

    x2 = x.reshape(t, D_MODEL)
    tgt2 = loss_target.reshape(t, D_MODEL)
    mem2 = mem.reshape(bl * MEM_LEN, D_MODEL)

    h = _norm_fwd(x2, g_pre_mix, name="norm_x", side=("gather", comm["first_shards"]) if comm else None)
    if comm:
        w = with_gathered(w, comm["first_names"], h[1], comm["first_shards"])
        h = h[0]
    w_in_full = _join_shards(w["w_in"])
    proj = _mm([(h, w_in_full)], nt=False, tn=2176, out_dtypes=[BF16], name="proj",
               side=("gather", comm["mid_shards"]) if comm else None)
    if comm:
        w = with_gathered(w, comm["mid_names"], proj[1], comm["mid_shards"])
        proj = proj[0]
    w_mem_kv_full = w["w_mem_kv"].reshape(D_MODEL, 2 * MEM_W)
    gates = _mm([(h, w["w_gate"], None, "j")], nt=False,tn=w["w_gate"].shape[2], out_dtypes=[BF16], name="gates",
                bias=b_gate, epilogue=lambda acc: (_sigmoid(acc),))
    hm = _norm_fwd(mem2, g_mem, name="norm_mem")
    kv_m = _mm([(hm, w_mem_kv_full)], nt=False,tn=1024, out_dtypes=[BF16], name="mem_kv")
    proj3 = proj.reshape(bl, SEQ, D_IN)
    kv3 = kv_m.reshape(bl, MEM_LEN, 2 * MEM_W)

    o_a, o_a32, sb_weights, late_gathered = _sb_fwd(proj3, comm["late_shards"] if comm else [])
    if comm:
        w = with_gathered(w, comm["late_names"], late_gathered, comm["late_shards"])
    w_o_full = w["w_o"].reshape(D_MODEL, D_MODEL)
    w_ffn_out_full = w["w_ffn_out"].reshape(D_FF, D_MODEL)

    cos_t, sin_t = _rope_tables()
    dil0 = 3 * SB_W

    grp_w = 3 * DIL_W
    band = []
    for g, (d, cfg) in enumerate(zip(DIL_D, _band_groups())):
        a_g = proj3 if d == 1 else proj3[:, :, dil0 + g * grp_w:dil0 + (g + 1) * grp_w].reshape(bl, SEQ // d, d * grp_w)
        band.append(dict(cfg, a=a_g, col0=dil0 // grp_w if d == 1 else None, cos=cos_t.reshape(SEQ // d, d * DIL_W),
                         sin=sin_t.reshape(SEQ // d, d * DIL_W)))
    ols = [_band_group_fwd(b["a"], b["cos"], b["sin"], rows=b["rows"], cls=b["cls"], steps=b["steps"], col0=b["col0"],
                           name=f"band_fwd_{g}").reshape(t, 2 * DIL_W) for g, b in enumerate(band)]
    o_b, lse_b = _band_merge3(ols)

    o_c = _mem_fwd(proj3, kv3)

    o_a2, o_c2 = o_a.reshape(t, SB_W), o_c.reshape(t, MEM_W)
    y_a, y_b, y_c, merged = _branch_merge_fwd(o_a2, o_b, o_c2, w["w_br_sb"], w["w_br_dil"], w["w_br_mem"], gates)
    mix = _mm([(merged, w_o_full)], nt=False,tn=1024, out_dtypes=[F32], name="mix")
    x1, h2 = _mid_fwd(mix, x2, g_post_mix, g_pre_ffn)
    gg, uu, f = _ffn_in_fwd(h2, w["w_ffn_in"])
    f2 = _mm([(f, w_ffn_out_full)], nt=False,tn=1024, out_dtypes=[F32], name="ffn_out")

    dy, df2, dg_post_ffn, loss_row = _loss_bwd(f2, x1, g_post_ffn, tgt2)

    dg_ffn, du_ffn = _mm([(df2, w_ffn_out_full)], nt=True,tn=half_ff, out_dtypes=[BF16, BF16], name="d_ffn_act",
                         extras=(gg, uu), epilogue=_swiglu_bwd_epilogue)
    gw = {}
    gw["w_ffn_out"] = _mm_tn(f, df2, tm=half_ff, tn=1024, name="gw_ffn_out").reshape(N_CHIPS, D_FF // N_CHIPS, D_MODEL)
    gw_ffn_g = _mm_tn(h2, dg_ffn, tm=1024, tn=half_ff, name="gw_ffn_gate", out_shards=True)
    gw_ffn_u = _mm_tn(h2, du_ffn, tm=1024, tn=half_ff, name="gw_ffn_up", out_shards=True)
    gw["w_ffn_in"] = jnp.concatenate([gw_ffn_g, gw_ffn_u], axis=0)
    dh2 = _mm([(dg_ffn, w["w_ffn_in"], 0, 0), (dg_ffn, w["w_ffn_in"], 1, 1), (du_ffn, w["w_ffn_in"], 0, 2),
               (du_ffn, w["w_ffn_in"], 1, 3)], nt=True, tn=1024, out_dtypes=[BF16], name="d_h2")
    dx1, dmix, dg_pre_ffn, dg_post_mix = _mid_bwd(dh2, x1, mix, g_pre_ffn, g_post_mix, dy)

    gw["w_o"] = _mm_tn(merged, dmix, tm=1024, tn=1024, name="gw_o").reshape(N_CHIPS, D_MODEL // N_CHIPS, D_MODEL)
    dmerged = _mm([(dmix, w_o_full)], nt=True, tn=1024, out_dtypes=[BF16], name="d_merged")
    dy_a, dy_b, dy_c, dgpre, db_gate = _gate_bwd(dmerged, gates, y_a, y_b, y_c)
    br_cols = D_MODEL // N_CHIPS
    gw["w_br_sb"] = _mm_tn(o_a2, dy_a, tm=512, tn=br_cols, name="gw_br_sb", out_shards=True)
    gw["w_br_dil"] = _mm_tn(o_b, dy_b, tm=256, tn=br_cols, name="gw_br_dil", out_shards=True)
    gw["w_br_mem"] = _mm_tn(o_c2, dy_c, tm=512, tn=br_cols, name="gw_br_mem", out_shards=True)
    gw["w_gate"] = _mm_tn(h, dgpre, tm=1024, tn=w["w_gate"].shape[2], name="gw_gate", out_shards=True)
    do_a = _mm([(dy_a, w["w_br_sb"], s, s) for s in chips], nt=True,tn=SB_W, out_dtypes=[BF16], name="d_o_a")
    do_b = _mm([(dy_b, w["w_br_dil"], s, s) for s in chips], nt=True,tn=DIL_W, out_dtypes=[BF16], name="d_o_b")
    do_c = _mm([(dy_c, w["w_br_mem"], s, s) for s in chips], nt=True,tn=MEM_W, out_dtypes=[BF16], name="d_o_c")

    dq_c, dk_m, dv_m = _mem_bwd(proj3, kv3, do_c.reshape(bl, SEQ, MEM_W))
    dkv_m = jnp.concatenate([dk_m, dv_m], axis=-1).reshape(bl * MEM_LEN, 2 * MEM_W).astype(BF16)
    gw["w_mem_kv"] = _mm_tn(hm, dkv_m, tm=1024, tn=1024, name="gw_mem_kv").reshape(N_CHIPS, D_MODEL // N_CHIPS, 2 * MEM_W)
    dhm = _mm([(dkv_m, w_mem_kv_full)], nt=True,tn=1024, out_dtypes=[F32], name="d_hm")
    dg_mem = _mem_norm_bwd(dhm, mem2, g_mem)

    stats = _band_delta(do_b, o_b, lse_b)
    early = [name for name, _, _ in PACK if name != "w_in"] if comm else []
    grads = [gw[name] for name in early]
    d_dil = []
    for g, (d, b) in enumerate(zip(DIL_D, band)):
        out = _band_group_bwd(b["a"], do_b.reshape(bl, SEQ // d, d * DIL_W), stats.reshape(bl, SEQ // d, d * LANES),
                              b["cos"], b["sin"], rows=b["rows"], cls=b["cls"], steps=b["steps"], col0=b["col0"],
                              name=f"band_bwd_{g}", side=("pair", grads) if comm and g == 0 else None)
        if comm and g == 0:
            out, lands = out
        d_dil.append(out.reshape(bl, SEQ, grp_w))

    parts = [_pair_add(g, l, comm["c"], name="pair_add_" + name) for name, g, l in zip(early, grads, lands)] if comm else []
    dq_a, dk_a, dv_a, lands = _sb_bwd(proj3, o_a32, do_a.reshape(bl, SEQ, SB_W), sb_weights, parts)
    reduced = {name: (p, l) for name, p, l in zip(early, parts, lands)}

    in_cols = D_IN // N_CHIPS
    dproj_s = _split_to_shards([a.reshape(t, a.shape[-1]) for a in [dq_a, dk_a, dv_a] + d_dil + [dq_c]], name="dproj_shards")
    gw["w_in"] = _mm_tn(h, dproj_s, tm=1024, tn=in_cols, name="gw_in")
    if comm:
        land = _pair_exchange([gw["w_in"]], name="pair_exchange_w_in")[0]
        part_in = _pair_add(gw["w_in"], land, comm["c"], name="pair_add_w_in")
    dh = _mm([(dproj_s, w["w_in"], s, s) for s in chips] + [(dgpre, w["w_gate"], s, s) for s in chips],
             nt=True, tn=1024, out_dtypes=[BF16], name="d_h", side=("chip", [part_in]) if comm else None)
    if comm:
        dh, (land_in,) = dh
        reduced["w_in"] = (part_in, land_in)
    grad_x, dg_pre_mix = _first_bwd(dh, x2, g_pre_mix, dx1)
    small = jnp.concatenate([dg_pre_mix, dg_post_mix, dg_pre_ffn, dg_post_ffn, dg_mem, db_gate.reshape(3, D_MODEL)], axis=0)
    return loss_row[0, 0], grad_x.reshape(bl, SEQ, D_MODEL), gw, small, reduced


def kernel(x, mem, g_pre_mix, g_post_mix, g_pre_ffn, g_post_ffn, g_mem, w_in, w_mem_kv, w_br_sb, w_br_dil, w_br_mem, w_gate, b_gate, w_o, w_ffn_in, w_ffn_out, loss_target, m_g_pre_mix, m_g_post_mix, m_g_pre_ffn, m_g_post_ffn, m_g_mem, m_w_in, m_w_mem_kv, m_w_br_sb, m_w_br_dil, m_w_br_mem, m_w_gate, m_b_gate, m_w_o, m_w_ffn_in, m_w_ffn_out, v_g_pre_mix, v_g_post_mix, v_g_pre_ffn, v_g_post_ffn, v_g_mem, v_w_in, v_w_mem_kv, v_w_br_sb, v_w_br_dil, v_w_br_mem, v_w_gate, v_b_gate, v_w_o, v_w_ffn_in, v_w_ffn_out):
    w_shards = dict(w_in=w_in[0], w_mem_kv=w_mem_kv[0], w_br_sb=w_br_sb[0], w_br_dil=w_br_dil[0], w_br_mem=w_br_mem[0],
                    w_gate=w_gate[0], w_o=w_o[0], w_ffn_in=w_ffn_in[0], w_ffn_out=w_ffn_out[0])
    m_shards = dict(w_in=m_w_in[0], w_mem_kv=m_w_mem_kv[0], w_br_sb=m_w_br_sb[0], w_br_dil=m_w_br_dil[0], w_br_mem=m_w_br_mem[0],
                    w_gate=m_w_gate[0], w_o=m_w_o[0], w_ffn_in=m_w_ffn_in[0], w_ffn_out=m_w_ffn_out[0])
    v_shards = dict(w_in=v_w_in[0], w_mem_kv=v_w_mem_kv[0], w_br_sb=v_w_br_sb[0], w_br_dil=v_w_br_dil[0], w_br_mem=v_w_br_mem[0],
                    w_gate=v_w_gate[0], w_o=v_w_o[0], w_ffn_in=v_w_ffn_in[0], w_ffn_out=v_w_ffn_out[0])

    names = [name for name, _, _ in PACK]
    c_arr = lax.axis_index("c").astype(jnp.int32).reshape(1)
    me_arr = (2 * lax.axis_index("x") + lax.axis_index("y")).astype(jnp.int32).reshape(1)
    mid_names = ["w_gate", "w_mem_kv"]
    late_names = [name for name in names if name not in ["w_in"] + mid_names]
    bf = {name: w_shards[name].astype(BF16) for name in names}
    comm = dict(c=c_arr, me=me_arr, first_names=["w_in"], first_shards=[bf["w_in"]],
                mid_names=mid_names, mid_shards=[bf[name] for name in mid_names],
                late_names=late_names, late_shards=[bf[name] for name in late_names])

    loss_local, grad_x, gw, small, reduced = _local_step(x, mem, loss_target, g_pre_mix, g_post_mix, g_pre_ffn, g_post_ffn,
                                                         g_mem, b_gate, {}, comm)
    loss = lax.psum(loss_local, ("x", "y", "c"))

    halves =[_chip_add(reduced[name][1], reduced[name][0], me_arr, name="chip_add_" + name) for name in names]
    theirs = _pair_share(halves)
    small = _all_sum_small(small)

    upd = {}
    for name, mine, other in zip(names, halves, theirs):
        upd[name] = _adamw_halves(w_shards[name], mine, other, m_shards[name], v_shards[name], c_arr, name="adamw_" + name)
    g_shards = {name: u[0] for name, u in upd.items()}

    def small8(gs, b):
        return jnp.concatenate(gs + [b.reshape(3, D_MODEL)], axis=0)

    sw = small8([g_pre_mix, g_post_mix, g_pre_ffn, g_post_ffn, g_mem], b_gate)
    sm = small8([m_g_pre_mix, m_g_post_mix, m_g_pre_ffn, m_g_post_ffn, m_g_mem], m_b_gate)
    sv = small8([v_g_pre_mix, v_g_post_mix, v_g_pre_ffn, v_g_post_ffn, v_g_mem], v_b_gate)
    s_upd = _adamw(sw, small, sm, sv, tm=8, name="adamw_small")

    def small_out(a):
        return [a[0:1], a[1:2], a[2:3], a[3:4], a[4:5]]

    order = ["w_in", "w_mem_kv", "w_br_sb", "w_br_dil", "w_br_mem", "w_gate", "b_gate", "w_o", "w_ffn_in", "w_ffn_out"]

    def leaves(small_arr, big):
        out = small_out(small_arr)
        for name in order:
            out.append(small_arr[5:8].reshape(1, 3 * D_MODEL) if name == "b_gate" else big[name][None])
        return out

    grads_out = leaves(small, g_shards)
    delta_out = leaves(s_upd[0], {n: u[1] for n, u in upd.items()})
    m_out = leaves(s_upd[1], {n: u[2] for n, u in upd.items()})
    v_out = leaves(s_upd[2], {n: u[3] for n, u in upd.items()})
    return (loss, grad_x, *grads_out, *delta_out, *m_out, *v_out)
```

```python
import jax
import jax.numpy as jnp
import numpy as np
from jax import lax
from jax.experimental import pallas as pl
from jax.experimental.pallas import tpu as pltpu

F32 = jnp.float32
BF16 = jnp.bfloat16
MESH = pl.DeviceIdType.MESH

D_MODEL = 1024
SEQ = 2048
HEAD_DIM = 64
SB_W = 512
DIL_W = 256
MEM_W = 512
MEM_LEN = 256
D_IN = 3 * SB_W + 9 * DIL_W + MEM_W
D_FF = 2816
DIL_D = (1, 4, 16)
ROPE_THETA = 10000.0
NORM_EPS = 1e-6
NEG_INF = -1e30
LANES = 128

ADAM_LR = 0.001
ADAM_B1 = 0.9
ADAM_B2 = 0.999
ADAM_EPS = 1e-08
ADAM_WD = 0.01
ADAM_STEP = 10

N_CHIPS = 4
PACK = (
    ("w_in", (1024, 1088), 1),
    ("w_mem_kv", (256, 1024), 0),
    ("w_br_sb", (512, 256), 1),
    ("w_br_dil", (256, 256), 1),
    ("w_br_mem", (512, 256), 1),
    ("w_gate", (1024, 768), 1),
    ("w_o", (256, 1024), 0),
    ("w_ffn_in", (1024, 1408), 1),
    ("w_ffn_out", (704, 1024), 0),
)
PACK_ROWS = sum(a * b for _, (a, b), _ in PACK) // D_MODEL
HALF_ROWS = PACK_ROWS // 2


def _dot(a, b):
    return lax.dot_general(a, b, (((1,), (0,)), ((), ())), preferred_element_type=F32)


def _dot_nt(a, b):
    return lax.dot_general(a, b, (((1,), (1,)), ((), ())), preferred_element_type=F32)


def _dot_tn(a, b):
    return lax.dot_general(a, b, (((0,), (0,)), ((), ())), preferred_element_type=F32)


def _split_dot(x, u):
    hi = x.astype(BF16)
    lo = (x - hi.astype(F32)).astype(BF16)
    return _dot(hi, u) + _dot(lo, u)


V7X_VMEM_BUDGET = 44 * 2 ** 20


def _rows_that_fit(m, row_bytes, fixed_bytes):
    for tm in (1024, 512, 256, 128):
        if m % tm == 0 and fixed_bytes + tm * row_bytes <= V7X_VMEM_BUDGET:
            return tm
    return min(m, 128)


def _mm(pairs, *, nt, tn, out_dtypes, name, bias=None, extras=(), epilogue=None, side=None):
    pairs = [p if len(p) == 4 else (p[0], p[1], None, None) for p in pairs]
    m = pairs[0][0].shape[-2]
    b0 = pairs[0][1]
    if nt:
        n = b0.shape[-2]
    else:
        n = b0.shape[-1] * (b0.shape[0] if b0.ndim == 3 else 1)
    n_pairs, n_extra, n_out = len(pairs), len(extras), len(out_dtypes)
    assert n % tn == 0
    one_col = n == tn
    ks = [(b.shape[-1] if nt else b.shape[-2]) for _, b, _, _ in pairs]
    fixed = sum(k * tn * 2 for k in ks) * (1 if one_col else 2)
    row_bytes = 2 * sum(k * 2 for k in ks) + 2 * tn * (sum(jnp.dtype(dt).itemsize for dt in out_dtypes) + 2 * n_extra) + 2 * tn * 4
    tm = _rows_that_fit(m, row_bytes, fixed)
    assert m % tm == 0
    b_mode = dict(pipeline_mode=pl.Buffered(1)) if one_col else {}
    has_bias = bias is not None
    side_arrays = side[1] if side else []
    n_side = len(side_arrays)
    n_main_in = 2 * n_pairs + has_bias + n_extra
    n_steps = (n // tn) * (m // tm)

    def body(*refs):
        if n_side:
            step = pl.program_id(0) * (m // tm) + pl.program_id(1)
            finish = _run_side(side, refs[n_main_in:n_main_in + n_side],
                               refs[n_main_in + n_side + n_out:n_main_in + 2 * n_side + n_out],
                               refs[n_main_in + 2 * n_side + n_out:], step, n_steps)
        outs = refs[n_main_in + n_side:n_main_in + n_side + n_out]
        acc = None
        for i in range(n_pairs):
            a, b = refs[2 * i][...], refs[2 * i + 1][...]
            p = _dot_nt(a, b) if nt else _dot(a, b)
            acc = p if acc is None else acc + p
        pos = 2 * n_pairs
        if has_bias:
            acc = acc + refs[pos][...]
            pos += 1
        ex = [r[...] for r in refs[pos:pos + n_extra]]
        vals = (acc,) if epilogue is None else epilogue(acc, *ex)
        for r, v, dt in zip(outs, vals, out_dtypes):
            r[...] = v.astype(dt)
        if n_side:
            finish()

    in_specs, args = [], []
    for a, b, a_col, b_sel in pairs:
        k = b.shape[-1] if nt else b.shape[-2]
        assert a_col is not None or a.shape[1] == k
        if a.ndim == 3:
            in_specs.append(pl.BlockSpec((None, tm, k), lambda j, i, c=a_col: (c, i, 0)))
        else:
            in_specs.append(pl.BlockSpec((tm, k), lambda j, i, c=a_col or 0: (i, c)))
        if b.ndim == 2:
            in_specs.append(pl.BlockSpec((tn, k), lambda j, i: (j, 0), **b_mode) if nt
                            else pl.BlockSpec((k, tn), lambda j, i: (0, j), **b_mode))
        elif nt:
            in_specs.append(pl.BlockSpec((None, tn, k), lambda j, i, s=b_sel: (s, j, 0), **b_mode))
        else:
            assert b_sel == "j" and b.shape[-1] == tn
            in_specs.append(pl.BlockSpec((None, k, tn), lambda j, i: (j, 0, 0), **b_mode))
        args += [a, b]
    if has_bias:
        in_specs.append(pl.BlockSpec((1, tn), lambda j, i: (0, j)))
        args.append(bias)
    for e in extras:
        in_specs.append(pl.BlockSpec((tm, tn), lambda j, i: (i, j)))
        args.append(e)
    out = pl.pallas_call(
        body,
        grid=(n // tn, m // tm),
        in_specs=in_specs + [ANY] * n_side,
        out_specs=[pl.BlockSpec((tm, tn), lambda j, i: (i, j)) for _ in range(n_out)] + [ANY] * n_side,
        out_shape=[jax.ShapeDtypeStruct((m, n), dt) for dt in out_dtypes] + (_side_out_shapes(side) if n_side else []),
        scratch_shapes=_side_sems(side) if n_side else [],
        compiler_params=pltpu.CompilerParams(dimension_semantics=("arbitrary", "arbitrary") if n_side else ("parallel", "parallel")),
        name=name,
    )(*args, *side_arrays)
    if n_side:
        return (out[0] if n_out == 1 else out[:n_out]), out[n_out:]
    return out[0] if n_out == 1 else out


def _mm_tn(a, b, *, tm, tn, name, out_shards=False, slots=None, into=None):
    k, m = a.shape
    b_shards = b.ndim == 3
    out_shards = out_shards or b_shards
    n = b.shape[0] * b.shape[2] if b_shards else b.shape[1]
    tk = _rows_that_fit(k, 2 * 2 * (tm + tn), 3 * tm * tn * 4)
    assert m % tm == 0 and n % tn == 0 and k % tk == 0 and (not b_shards or b.shape[2] == tn)
    total, first = slots if slots else (n // tn, 0)

    def body(a_ref, b_ref, *rest):
        o_ref = rest[-1]

        @pl.when(pl.program_id(2) == 0)
        def _():
            o_ref[...] = jnp.zeros_like(o_ref)

        o_ref[...] += _dot_tn(a_ref[...], b_ref[...])

    if b_shards:
        b_spec = pl.BlockSpec((None, tk, tn), lambda i, j, kk: (j, kk, 0))
    else:
        b_spec = pl.BlockSpec((tk, tn), lambda i, j, kk: (kk, j))
    if out_shards:
        out_spec = pl.BlockSpec((None, tm, tn), lambda i, j, kk: (j + first, i, 0))
        out_shape = jax.ShapeDtypeStruct((total, m, tn), F32)
    else:
        out_spec = pl.BlockSpec((tm, tn), lambda i, j, kk: (i, j))
        out_shape = jax.ShapeDtypeStruct((m, n), F32)
    return pl.pallas_call(
        body,
        grid=(m // tm, n // tn, k // tk),
        in_specs=[pl.BlockSpec((tk, tm), lambda i, j, kk: (kk, i)), b_spec] + ([ANY] if into is not None else []),
        out_specs=out_spec,
        out_shape=out_shape,
        input_output_aliases={2: 0} if into is not None else {},
        compiler_params=pltpu.CompilerParams(dimension_semantics=("parallel", "parallel", "arbitrary")),
        name=name,
    )(*([a, b] + ([into] if into is not None else [])))


def _rowwise(fn, ins, outs, *, tm, name, side=None):
    rows = next(a.shape[0] for a, kind in ins if kind == "row")
    tm = min(tm, rows)
    assert rows % tm == 0
    n_in, n_out = len(ins), len(outs)
    side_arrays = side[1] if side else []
    n_side = len(side_arrays)

    def body(*refs):
        if n_side:
            finish = _run_side(side, refs[n_in:n_in + n_side], refs[n_in + n_side + n_out:n_in + 2 * n_side + n_out],
                               refs[n_in + 2 * n_side + n_out:], pl.program_id(0), rows // tm)
        vals = fn(*[r[...] for r in refs[:n_in]])
        for (_, dt, kind), r, v in zip(outs, refs[n_in + n_side:n_in + n_side + n_out], vals):
            if kind == "row":
                r[...] = v.astype(dt)
            else:
                @pl.when(pl.program_id(0) == 0)
                def _(r=r):
                    r[...] = jnp.zeros_like(r)

                r[...] += v
        if n_side:
            finish()

    in_specs = [pl.BlockSpec((tm, a.shape[1]), lambda i: (i, 0)) if kind == "row" else pl.BlockSpec(a.shape, lambda i: (0, 0))
                for a, kind in ins]
    out_specs = [pl.BlockSpec((tm, c), lambda i: (i, 0)) if kind == "row" else pl.BlockSpec((1, c), lambda i: (0, 0))
                 for c, _, kind in outs]
    out_shape = [jax.ShapeDtypeStruct((rows if kind == "row" else 1, c), dt) for c, dt, kind in outs]
    ordered = n_side or any(kind == "acc" for _, _, kind in outs)
    return pl.pallas_call(
        body,
        grid=(rows // tm,),
        in_specs=in_specs + [ANY] * n_side,
        out_specs=out_specs + [ANY] * n_side,
        out_shape=out_shape + (_side_out_shapes(side) if n_side else []),
        scratch_shapes=_side_sems(side) if n_side else [],
        compiler_params=pltpu.CompilerParams(dimension_semantics=("arbitrary" if ordered else "parallel",)),
        name=name,
    )(*[a for a, _ in ins], *side_arrays)


def _rstd(x):
    return lax.rsqrt(jnp.mean(x * x, axis=-1, keepdims=True) + NORM_EPS)


def _norm_bwd(dout, xin, g):
    r = _rstd(xin)
    n = xin * r
    dn = dout * g
    dg = jnp.sum(dout * n, axis=0, keepdims=True)
    dx = r * (dn - n * jnp.mean(dn * n, axis=-1, keepdims=True))
    return dx, dg


def _sigmoid(x):
    return 0.5 * jnp.tanh(0.5 * x) + 0.5


def _norm_fwd(x, g, *, name, side=None):
    def fn(x, g):
        return ((x * _rstd(x)) * g,)

    out = _rowwise(fn, [(x, "row"), (g, "vec")], [(D_MODEL, BF16, "row")], tm=512, name=name, side=side)
    return (out[0], out[1:]) if side else out[0]


def _mid_fwd(mix, x, g_post_mix, g_pre_ffn):
    def fn(mix, x, g2, g3):
        x1 = x + (mix * _rstd(mix)) * g2
        return x1, (x1 * _rstd(x1)) * g3

    return _rowwise(fn, [(mix, "row"), (x, "row"), (g_post_mix, "vec"), (g_pre_ffn, "vec")],
                    [(D_MODEL, F32, "row"), (D_MODEL, BF16, "row")], tm=512, name="mid_fwd")


def _loss_bwd(f2, x1, g_post_ffn, tgt):
    def fn(f2, x1, g4, tgt):
        r = _rstd(f2)
        n = f2 * r
        err = x1 + n * g4 - tgt
        loss = 0.5 * jnp.sum(jnp.mean(err * err, axis=-1, keepdims=True), axis=0, keepdims=True)
        dy = err * (1.0 / D_MODEL)
        dn = dy * g4
        dg4 = jnp.sum(dy * n, axis=0, keepdims=True)
        df2 = r * (dn - n * jnp.mean(dn * n, axis=-1, keepdims=True))
        return dy, df2, dg4, jnp.broadcast_to(loss, (1, LANES))

    return _rowwise(fn, [(f2, "row"), (x1, "row"), (g_post_ffn, "vec"), (tgt, "row")],
                    [(D_MODEL, BF16, "row"), (D_MODEL, BF16, "row"), (D_MODEL, F32, "acc"), (LANES, F32, "acc")],
                    tm=512, name="loss_bwd")


def _mid_bwd(dh2, x1, mix, g_pre_ffn, g_post_mix, dy):
    def fn(dh2, x1, mix, g3, g2, dy):
        d3, dg3 = _norm_bwd(dh2.astype(F32), x1, g3)
        dx1 = dy.astype(F32) + d3
        dmix, dg2 = _norm_bwd(dx1, mix, g2)
        return dx1, dmix, dg3, dg2

    return _rowwise(fn, [(dh2, "row"), (x1, "row"), (mix, "row"), (g_pre_ffn, "vec"), (g_post_mix, "vec"), (dy, "row")],
                    [(D_MODEL, BF16, "row"), (D_MODEL, BF16, "row"), (D_MODEL, F32, "acc"), (D_MODEL, F32, "acc")],
                    tm=256, name="mid_bwd")


def _first_bwd(dh, x, g_pre_mix, dx1):
    def fn(dh, x, g1, dx1):
        d1, dg1 = _norm_bwd(dh.astype(F32), x, g1)
        return dx1.astype(F32) + d1, dg1

    return _rowwise(fn, [(dh, "row"), (x, "row"), (g_pre_mix, "vec"), (dx1, "row")],
                    [(D_MODEL, F32, "row"), (D_MODEL, F32, "acc")], tm=512, name="first_bwd")


def _mem_norm_bwd(dhm, mem, g_mem):
    def fn(dhm, mem, g):
        return (jnp.sum(dhm * (mem * _rstd(mem)), axis=0, keepdims=True),)

    return _rowwise(fn, [(dhm, "row"), (mem, "row"), (g_mem, "vec")], [(D_MODEL, F32, "acc")], tm=512, name="mem_norm_bwd")[0]


def _gate_bwd(dmerged, gates, ya, yb, yc):
    def fn(dm, gt, ya, yb, yc):
        dm = dm.astype(F32)
        gt = gt.astype(F32)
        outs, dgp = [], []
        for i, y in enumerate((ya, yb, yc)):
            gi = gt[:, i * D_MODEL:(i + 1) * D_MODEL]
            outs.append(dm * gi)
            dgp.append(dm * y.astype(F32) * gi * (1.0 - gi))
        dgpre = jnp.concatenate(dgp, axis=1)
        return outs[0], outs[1], outs[2], dgpre, jnp.sum(dgpre, axis=0, keepdims=True)

    return _rowwise(fn, [(dmerged, "row"), (gates, "row"), (ya, "row"), (yb, "row"), (yc, "row")],
                    [(D_MODEL, BF16, "row")] * 3 + [(3 * D_MODEL, BF16, "row"), (3 * D_MODEL, F32, "acc")],
                    tm=256, name="gate_bwd")


def _adamw_math(w, g, m, v):
    m = ADAM_B1 * m + (1.0 - ADAM_B1) * g
    v = ADAM_B2 * v + (1.0 - ADAM_B2) * (g * g)
    m_hat = m / (1.0 - ADAM_B1 ** ADAM_STEP)
    v_hat = v / (1.0 - ADAM_B2 ** ADAM_STEP)
    delta = -ADAM_LR * (m_hat / (jnp.sqrt(v_hat) + ADAM_EPS) + ADAM_WD * w)
    return delta, m, v


def _adamw(w, g, m, v, *, tm, name):
    c = w.shape[1]
    return _rowwise(_adamw_math, [(w, "row"), (g, "row"), (m, "row"), (v, "row")], [(c, F32, "row")] * 3, tm=tm, name=name)


def _adamw_halves(w, g_mine, g_theirs, m, v, c_arr, *, name):
    a, b = w.shape
    hf = a // 2
    tr = hf // 4

    def body(c_ref, w_ref, gm_ref, gt_ref, m_ref, v_ref, g_out, d_out, m_out, v_out):
        g = jnp.where(pl.program_id(0) == c_ref[0], gm_ref[...], gt_ref[...])
        d, m_new, v_new = _adamw_math(w_ref[...], g, m_ref[...], v_ref[...])
        g_out[...] = g
        d_out[...] = d
        m_out[...] = m_new
        v_out[...] = v_new

    full = pl.BlockSpec((tr, b), lambda hh, i, c_ref: (hh * (hf // tr) + i, 0))
    half = pl.BlockSpec((tr, b), lambda hh, i, c_ref: (i, 0))
    return pl.pallas_call(
        body,
        grid_spec=pltpu.PrefetchScalarGridSpec(
            num_scalar_prefetch=1,
            grid=(2, hf // tr),
            in_specs=[full, half, half, full, full],
            out_specs=[full] * 4,
        ),
        out_shape=[jax.ShapeDtypeStruct((a, b), F32)] * 4,
        compiler_params=pltpu.CompilerParams(dimension_semantics=("parallel", "parallel")),
        name=name,
    )(c_arr, w, g_mine, g_theirs, m, v)


def _ffn_in_fwd(h2, w_ffn):
    m, tm, tn = h2.shape[0], 512, w_ffn.shape[2]
    assert 2 * tn == D_FF

    def body(h_ref, wg_ref, wu_ref, g_ref, u_ref, f_ref):
        h = h_ref[...]
        g = _dot(h, wg_ref[...])
        u = _dot(h, wu_ref[...])
        g_ref[...] = g.astype(BF16)
        u_ref[...] = u.astype(BF16)
        f_ref[...] = (g * _sigmoid(g) * u).astype(BF16)

    o_spec = pl.BlockSpec((tm, tn), lambda j, i: (i, j))
    return pl.pallas_call(
        body,
        grid=(D_FF // tn, m // tm),
        in_specs=[pl.BlockSpec((tm, D_MODEL), lambda j, i: (i, 0)),
                  pl.BlockSpec((None, D_MODEL, tn), lambda j, i: (j, 0, 0)),
                  pl.BlockSpec((None, D_MODEL, tn), lambda j, i: (j + 2, 0, 0))],
        out_specs=[o_spec, o_spec, o_spec],
        out_shape=[jax.ShapeDtypeStruct((m, D_FF), BF16)] * 3,
        compiler_params=pltpu.CompilerParams(dimension_semantics=("parallel", "parallel")),
        name="ffn_in_fwd",
    )(h2, w_ffn, w_ffn)


def _join_shards(w4):
    _, rows, cols = w4.shape
    tr = rows // 4

    def body(w_ref, o_ref):
        for s in range(N_CHIPS):
            o_ref[:, s * cols:(s + 1) * cols] = w_ref[s]

    return pl.pallas_call(
        body,
        grid=(rows // tr,),
        in_specs=[pl.BlockSpec((N_CHIPS, tr, cols), lambda i: (0, i, 0))],
        out_specs=pl.BlockSpec((tr, N_CHIPS * cols), lambda i: (i, 0)),
        out_shape=jax.ShapeDtypeStruct((rows, N_CHIPS * cols), w4.dtype),
        compiler_params=pltpu.CompilerParams(dimension_semantics=("parallel",)),
        name="join_shards",
    )(w4)


def _split_to_shards(pieces, *, name):
    t = pieces[0].shape[0]
    widths = [p.shape[1] for p in pieces]
    cols = sum(widths) // N_CHIPS
    tm = 512
    plan, start = [], 0
    for p, wd in enumerate(widths):
        for s in range(N_CHIPS):
            lo, hi = max(start, s * cols), min(start + wd, (s + 1) * cols)
            if lo < hi:
                plan.append((s, p, lo - s * cols, hi - s * cols, lo - start, hi - start))
        start += wd

    def body(*refs):
        o_ref = refs[-1]
        for s, p, o_lo, o_hi, p_lo, p_hi in plan:
            o_ref[s, :, o_lo:o_hi] = refs[p][:, p_lo:p_hi]

    return pl.pallas_call(
        body,
        grid=(t // tm,),
        in_specs=[pl.BlockSpec((tm, wd), lambda i: (i, 0)) for wd in widths],
        out_specs=pl.BlockSpec((N_CHIPS, tm, cols), lambda i: (0, i, 0)),
        out_shape=jax.ShapeDtypeStruct((N_CHIPS, t, cols), pieces[0].dtype),
        compiler_params=pltpu.CompilerParams(dimension_semantics=("parallel",)),
        name=name,
    )(*pieces)


def _swiglu_bwd_epilogue(df, g, u):
    g = g.astype(F32)
    u = u.astype(F32)
    sg = _sigmoid(g)
    return df * u * (sg * (1.0 + g * (1.0 - sg))), df * (g * sg)


def _branch_merge_fwd(o_a, o_b, o_c, w_sb, w_dil, w_mem, gates):
    m, tm = o_a.shape[0], 256

    def body(oa_ref, ob_ref, oc_ref, wa_ref, wb_ref, wc_ref, gt_ref, ya_ref, yb_ref, yc_ref, mg_ref):
        def project(o_ref, w_ref):
            o = o_ref[...]
            return jnp.concatenate([_dot(o, w_ref[s]) for s in range(N_CHIPS)], axis=1)

        ya = project(oa_ref, wa_ref)
        yb = project(ob_ref, wb_ref)
        yc = project(oc_ref, wc_ref)
        gt = gt_ref[...].astype(F32)
        ya_ref[...] = ya.astype(BF16)
        yb_ref[...] = yb.astype(BF16)
        yc_ref[...] = yc.astype(BF16)
        mg_ref[...] = (gt[:, :D_MODEL] * ya + gt[:, D_MODEL:2 * D_MODEL] * yb + gt[:, 2 * D_MODEL:] * yc).astype(BF16)

    row = lambda c: pl.BlockSpec((tm, c), lambda i: (i, 0))
    full = lambda a: pl.BlockSpec(a.shape, lambda i: (0, 0, 0))
    return pl.pallas_call(
        body,
        grid=(m // tm,),
        in_specs=[row(SB_W), row(DIL_W), row(MEM_W), full(w_sb), full(w_dil), full(w_mem), row(3 * D_MODEL)],
        out_specs=[row(D_MODEL)] * 4,
        out_shape=[jax.ShapeDtypeStruct((m, D_MODEL), BF16)] * 4,
        compiler_params=pltpu.CompilerParams(dimension_semantics=("parallel",)),
        name="branch_merge_fwd",
    )(o_a, o_b, o_c, w_sb, w_dil, w_mem, gates)


SB_T = 256
SB_SCALE = HEAD_DIM ** -0.5


def _sb_masks():
    row = lax.broadcasted_iota(jnp.int32, (SB_T, SB_T), 0)
    col = lax.broadcasted_iota(jnp.int32, (SB_T, SB_T), 1)
    lane = lax.broadcasted_iota(jnp.int32, (1, LANES), 1)
    return row, col, lane


def _sb_logs(z):
    lb = jnp.minimum(z, 0.0) - jnp.log(1.0 + jnp.exp(-jnp.abs(z)))
    return lb, lb - z


def _sb_specs(n_heads_pairs, col0):
    q = pl.BlockSpec((None, SB_T, LANES), lambda b, p, i: (b, i, col0 + p))
    k = pl.BlockSpec((None, SEQ, LANES), lambda b, p, i: (b, 0, col0 + n_heads_pairs + p))
    v = pl.BlockSpec((None, SEQ, LANES), lambda b, p, i: (b, 0, col0 + 2 * n_heads_pairs + p))
    return q, k, v


def _grid_step(n_pairs, nq):
    return (pl.program_id(0) * n_pairs + pl.program_id(1)) * nq + pl.program_id(2)


def _sb_fwd(proj3, late_shards):
    bl = proj3.shape[0]
    n_pairs = SB_W // LANES
    nq = SEQ // SB_T
    n_late = len(late_shards)
    n_steps = bl * n_pairs * nq

    def body(q_ref, k_ref, v_ref, *rest):
        late_in, (o_ref, o32_ref, w_ref), late_out = rest[:n_late], rest[n_late:n_late + 3], rest[n_late + 3:2 * n_late + 3]
        step = _grid_step(n_pairs, nq)
        if n_late:
            send, forward, finish = _gather_phases(late_in, late_out, *rest[2 * n_late + 3:])
            pl.when(step == 0)(send)
            pl.when(step == n_steps // 2)(forward)
        i = pl.program_id(2)
        row, col, lane = _sb_masks()
        causal = col < row
        u_excl = (row > col).astype(BF16)
        q = q_ref[...]
        heads = []
        for h in range(2):
            mh = (lane // HEAD_DIM) == h
            heads.append((mh, jnp.where(mh, q, jnp.zeros_like(q)) * SB_SCALE))

        def blocks(js, diags, carries, acc):
            ks = [k_ref[pl.ds(pl.multiple_of(j * SB_T, SB_T), SB_T), :] for j in js]
            vs = [v_ref[pl.ds(pl.multiple_of(j * SB_T, SB_T), SB_T), :] for j in js]
            chains = [(b, h) for b in range(len(js)) for h in range(2)]
            z = {c: _dot_nt(heads[c[1]][1], ks[c[0]]) for c in chains}
            lb, lk = {}, {}
            for c in chains:
                lb[c], lk[c] = _sb_logs(z[c])
                if diags[c[0]]:
                    lk[c] = jnp.where(causal, lk[c], 0.0)
            r = {c: _split_dot(lk[c], u_excl) for c in chains}
            carries = list(carries)
            w = {}
            for b, h in chains:
                w_c = jnp.exp(lb[b, h] + r[b, h] + carries[h])
                w[b, h] = (jnp.where(causal, w_c, 0.0) if diags[b] else w_c).astype(BF16)
                w_ref[h, js[b]] = w[b, h]
                carries[h] = carries[h] + (r[b, h][:, :1] + lk[b, h][:, :1])
            for b, h in chains:
                acc = acc + _dot(w[b, h], jnp.where(heads[h][0], vs[b], jnp.zeros_like(vs[b])))
            return tuple(carries), acc

        zero = jnp.zeros((SB_T, 1), F32)
        init = ((zero, zero), jnp.zeros((SB_T, LANES), F32))
        odd = i % 2
        carries, acc = lax.cond(odd == 1, lambda: blocks([i, i - 1], (True, False), *init), lambda: blocks([i], (True,), *init))
        rest = i - 1 - odd
        carries, acc = lax.fori_loop(
            0, i // 2, lambda jj, c: blocks([rest - 2 * jj, rest - 1 - 2 * jj], (False, False), c[0], c[1]), (carries, acc))
        o_ref[...] = acc.astype(BF16)
        o32_ref[...] = acc
        if n_late:
            pl.when(step == n_steps - 1)(finish)

    q_spec, k_spec, v_spec = _sb_specs(n_pairs, 0)
    blk = pl.BlockSpec((None, SB_T, LANES), lambda b, p, i: (b, i, p))
    out = pl.pallas_call(
        body,
        grid=(bl, n_pairs, nq),
        in_specs=[q_spec, k_spec, v_spec] + [ANY] * n_late,
        out_specs=[blk, blk, _sb_weight_spec(nq)] + [ANY] * n_late,
        out_shape=[jax.ShapeDtypeStruct((bl, SEQ, SB_W), BF16), jax.ShapeDtypeStruct((bl, SEQ, SB_W), F32),
                   jax.ShapeDtypeStruct((bl, n_pairs, nq, 2, nq, SB_T, SB_T), BF16)] + _gather_out_shapes(late_shards),
        scratch_shapes=_gather_sems(n_late) if n_late else [],
        compiler_params=pltpu.CompilerParams(dimension_semantics=("arbitrary", "arbitrary", "arbitrary")),
        name="sb_fwd",
    )(proj3, proj3, proj3, *late_shards)
    return out[0], out[1], out[2], out[3:]


def _sb_weight_spec(nq):
    return pl.BlockSpec((None, None, None, 2, nq, SB_T, SB_T), lambda b, p, i: (b, p, i, 0, 0, 0, 0))


def _sb_bwd(proj3, o_a, do_a, w_all, parts):
    bl = proj3.shape[0]
    n_pairs = SB_W // LANES
    nq = SEQ // SB_T
    n_parts = len(parts)
    n_steps = bl * n_pairs * nq

    def body(q_ref, k_ref, v_ref, o_ref, do_ref, w_ref, *rest):
        p_refs, (dq_ref, dk_ref, dv_ref), land_refs = rest[:n_parts], rest[n_parts:n_parts + 3], rest[n_parts + 3:2 * n_parts + 3]
        dk_acc, dv_acc = rest[2 * n_parts + 3:2 * n_parts + 5]
        step = _grid_step(n_pairs, nq)
        if n_parts:
            send, finish = _chip_exchange_phases(p_refs, land_refs, *rest[2 * n_parts + 5:])
            pl.when(step == 0)(send)
        i = pl.program_id(2)

        @pl.when(i == 0)
        def _():
            dk_acc[...] = jnp.zeros_like(dk_acc)
            dv_acc[...] = jnp.zeros_like(dv_acc)

        row, col, lane = _sb_masks()
        causal = col < row
        u_incl = (row >= col).astype(BF16)
        q = q_ref[...]
        do = do_ref[...]
        prod = do.astype(F32) * o_ref[...]
        heads = []
        for h in range(2):
            mh = (lane // HEAD_DIM) == h
            d_tot = jnp.sum(jnp.where(mh, prod, 0.0), axis=1, keepdims=True)
            heads.append((mh, jnp.where(mh, q, jnp.zeros_like(q)) * SB_SCALE, jnp.where(mh, do, jnp.zeros_like(do)), d_tot))

        def blocks(js, diags, c_das, dq):
            starts = [pl.multiple_of(j * SB_T, SB_T) for j in js]
            ks = [k_ref[pl.ds(s, SB_T), :] for s in starts]
            vs = [v_ref[pl.ds(s, SB_T), :] for s in starts]
            chains = [(b, h) for b in range(len(js)) for h in range(2)]
            z = {c: _dot_nt(heads[c[1]][1], ks[c[0]]) for c in chains}
            dw = {c: _dot_nt(heads[c[1]][2], vs[c[0]]) for c in chains}
            wb = {(b, h): w_ref[h, js[b]] for b, h in chains}
            da = {c: dw[c] * wb[c].astype(F32) for c in chains}
            sfx = {c: _split_dot(da[c], u_incl) for c in chains}
            c_das = list(c_das)
            dz = {}
            for b, h in chains:
                dlk = heads[h][3] - c_das[h] - sfx[b, h]
                if diags[b]:
                    dlk = jnp.where(causal, dlk, 0.0)
                c_das[h] = c_das[h] + sfx[b, h][:, :1]
                e = jnp.exp(-jnp.abs(z[b, h]))
                inv = 1.0 / (1.0 + e)
                pos = z[b, h] >= 0.0
                beta = jnp.where(pos, inv, e * inv)
                one_m_beta = jnp.where(pos, e * inv, inv)
                dz[b, h] = (da[b, h] * one_m_beta - dlk * beta).astype(BF16)
            for b, h in chains:
                dq = dq + _dot(dz[b, h], jnp.where(heads[h][0], ks[b], jnp.zeros_like(ks[b])))
            for b in range(len(js)):
                dk_acc[pl.ds(starts[b], SB_T), :] += _dot_tn(dz[b, 0], heads[0][1]) + _dot_tn(dz[b, 1], heads[1][1])
                dv_acc[pl.ds(starts[b], SB_T), :] += _dot_tn(wb[b, 0], heads[0][2]) + _dot_tn(wb[b, 1], heads[1][2])
            return tuple(c_das), dq

        zero = jnp.zeros((SB_T, 1), F32)
        init = ((zero, zero), jnp.zeros((SB_T, LANES), F32))
        odd = i % 2
        state = lax.cond(odd == 1, lambda: blocks([i, i - 1], (True, False), *init), lambda: blocks([i], (True,), *init))
        rest = i - 1 - odd
        state = lax.fori_loop(0, i // 2, lambda jj, c: blocks([rest - 2 * jj, rest - 1 - 2 * jj], (False, False), c[0], c[1]), state)
        dq_ref[...] = (state[1] * SB_SCALE).astype(BF16)

        @pl.when(i == nq - 1)
        def _():
            dk_ref[...] = dk_acc[...].astype(BF16)
            dv_ref[...] = dv_acc[...].astype(BF16)

        if n_parts:
            pl.when(step == n_steps - 1)(finish)

    q_spec, k_spec, v_spec = _sb_specs(n_pairs, 0)
    blk = pl.BlockSpec((None, SB_T, LANES), lambda b, p, i: (b, i, p))
    seq = pl.BlockSpec((None, SEQ, LANES), lambda b, p, i: (b, 0, p))
    shape = jax.ShapeDtypeStruct((bl, SEQ, SB_W), BF16)
    out = pl.pallas_call(
        body,
        grid=(bl, n_pairs, nq),
        in_specs=[q_spec, k_spec, v_spec, blk, blk, _sb_weight_spec(nq)] + [ANY] * n_parts,
        out_specs=[blk, seq, seq] + [ANY] * n_parts,
        out_shape=[shape, shape, shape] + [jax.ShapeDtypeStruct(p.shape, p.dtype) for p in parts],
        scratch_shapes=[pltpu.VMEM((SEQ, LANES), F32), pltpu.VMEM((SEQ, LANES), F32)]
        + (_chip_exchange_sems(n_parts) if n_parts else []),
        compiler_params=pltpu.CompilerParams(dimension_semantics=("arbitrary", "arbitrary", "arbitrary")),
        name="sb_bwd",
    )(proj3, proj3, proj3, o_a, do_a, w_all, *parts)
    return out[0], out[1], out[2], out[3:]


BAND = 128


BAND_CH = 4
BAND_HEADS = DIL_W // HEAD_DIM


def _swap_half(x):
    n = x.shape[-1]
    lane = lax.broadcasted_iota(jnp.int32, (1, n), 1)
    return jnp.where((lane % HEAD_DIM) < HEAD_DIM // 2, pltpu.roll(x, n - HEAD_DIM // 2, 1), pltpu.roll(x, HEAD_DIM // 2, 1))


def _rope(x, cos, sin_signed):
    x = x.astype(F32)
    return x * cos + _swap_half(x) * sin_signed


def _band_valid(g, blk):
    nb = jnp.where(g == 0, 16, jnp.where(g == 1, 4, 1))
    first_key = jnp.where(lax.rem(blk, nb) != 0, 0, BAND)
    qi = lax.broadcasted_iota(jnp.int32, (BAND, 2 * BAND), 0) + BAND
    kj = lax.broadcasted_iota(jnp.int32, (BAND, 2 * BAND), 1)
    dist = qi - kj
    return (dist >= 0) & (dist <= BAND) & (kj >= first_key)


def _band_specs():
    last_before = lambda i: jnp.maximum(i * BAND_CH - 1, 0)
    cur = lambda col: pl.BlockSpec((None, BAND_CH, BAND, DIL_W), lambda g, i: (g, i, 0, col))
    prev = lambda col: pl.BlockSpec((None, None, BAND, DIL_W), lambda g, i: (g, last_before(i), 0, col))
    tab = pl.BlockSpec((None, BAND_CH, BAND, DIL_W), lambda g, i: (g, lax.rem(i, 16 // BAND_CH), 0, 0))
    tab_prev = pl.BlockSpec((None, None, BAND, DIL_W), lambda g, i: (g, lax.rem(last_before(i), 16), 0, 0))
    return cur, prev, tab, tab_prev


def _band_load(q_ref, k_ref, kp_ref, v_ref, vp_ref, c_ref, s_ref, cp_ref, sp_ref):
    qs = [(_rope(q_ref[b], c_ref[b], s_ref[b]) * SB_SCALE).astype(BF16) for b in range(BAND_CH)]
    ks = [_rope(kp_ref[...], cp_ref[...], sp_ref[...]).astype(BF16)]
    ks += [_rope(k_ref[b], c_ref[b], s_ref[b]).astype(BF16) for b in range(BAND_CH)]
    vs = [vp_ref[...]] + [v_ref[b] for b in range(BAND_CH)]
    k2 = [jnp.concatenate([ks[b], ks[b + 1]], axis=0) for b in range(BAND_CH)]
    v2 = [jnp.concatenate([vs[b], vs[b + 1]], axis=0) for b in range(BAND_CH)]
    return qs, k2, v2


def _band_fwd(qkv_s, cos_t, sin_t):
    def body(q_ref, k_ref, kp_ref, v_ref, vp_ref, c_ref, s_ref, cp_ref, sp_ref, ol_ref):
        g, i = pl.program_id(0), pl.program_id(1)
        qs, k2, v2 = _band_load(q_ref, k_ref, kp_ref, v_ref, vp_ref, c_ref, s_ref, cp_ref, sp_ref)
        lane = lax.broadcasted_iota(jnp.int32, (1, DIL_W), 1)
        for b in range(BAND_CH):
            valid = _band_valid(g, i * BAND_CH + b)
            hs = range(BAND_HEADS)
            mh = [(lane // HEAD_DIM) == h for h in hs]
            s = [jnp.where(valid, _dot_nt(jnp.where(mh[h], qs[b], jnp.zeros_like(qs[b])), k2[b]), NEG_INF) for h in hs]
            m = [jnp.max(s[h], axis=1, keepdims=True) for h in hs]
            p = [jnp.exp(s[h] - m[h]) for h in hs]
            den = [jnp.sum(p[h], axis=1, keepdims=True) for h in hs]
            pv = [_dot(p[h].astype(BF16), jnp.where(mh[h], v2[b], jnp.zeros_like(v2[b]))) for h in hs]
            o = jnp.zeros((BAND, DIL_W), F32)
            lse = jnp.zeros((BAND, DIL_W), F32)
            for h in hs:
                o = o + pv[h] * (1.0 / den[h])
                lse = jnp.where(mh[h], m[h] + jnp.log(den[h]), lse)
            ol_ref[b, :, :DIL_W] = o
            ol_ref[b, :, DIL_W:] = lse

    cur, prev, tab, tab_prev = _band_specs()
    n_blk = qkv_s.shape[1]
    return pl.pallas_call(
        body,
        grid=(3, n_blk // BAND_CH),
        in_specs=[cur(0), cur(1), prev(1), cur(2), prev(2), tab, tab, tab_prev, tab_prev],
        out_specs=pl.BlockSpec((None, BAND_CH, BAND, 2 * DIL_W), lambda g, i: (g, i, 0, 0)),
        out_shape=jax.ShapeDtypeStruct((3, n_blk, BAND, 2 * DIL_W), F32),
        compiler_params=pltpu.CompilerParams(dimension_semantics=("parallel", "parallel")),
        name="band_fwd",
    )(qkv_s, qkv_s, qkv_s, qkv_s, qkv_s, cos_t, sin_t, cos_t, sin_t)


def _band_bwd(qkv_s, cos_t, sin_t, dcat_s):
    def body(q_ref, k_ref, kp_ref, v_ref, vp_ref, c_ref, s_ref, cp_ref, sp_ref, do_ref, lse_ref, dl_ref,
             dq_ref, dk_ref, dv_ref, dkf_ref, dvf_ref):
        g, i = pl.program_id(0), pl.program_id(1)
        qs, k2, v2 = _band_load(q_ref, k_ref, kp_ref, v_ref, vp_ref, c_ref, s_ref, cp_ref, sp_ref)
        lane = lax.broadcasted_iota(jnp.int32, (1, DIL_W), 1)
        dks, dvs = [], []
        for b in range(BAND_CH):
            valid = _band_valid(g, i * BAND_CH + b)
            do, lse, dl = do_ref[b].astype(BF16), lse_ref[b], dl_ref[b]
            hs = range(BAND_HEADS)
            mh = [(lane // HEAD_DIM) == h for h in hs]
            qh = [jnp.where(mh[h], qs[b], jnp.zeros_like(qs[b])) for h in hs]
            doh = [jnp.where(mh[h], do, jnp.zeros_like(do)) for h in hs]
            s = [_dot_nt(qh[h], k2[b]) for h in hs]
            dp = [_dot_nt(doh[h], v2[b]) for h in hs]
            p = [jnp.where(valid, jnp.exp(s[h] - lse[:, h * HEAD_DIM:h * HEAD_DIM + 1]), 0.0) for h in hs]
            ds = [(p[h] * (dp[h] - dl[:, h * HEAD_DIM:h * HEAD_DIM + 1])).astype(BF16) for h in hs]
            pb = [p[h].astype(BF16) for h in hs]
            dq = sum(_dot(ds[h], jnp.where(mh[h], k2[b], jnp.zeros_like(k2[b]))) for h in hs)
            dk2 = sum(_dot_tn(ds[h], qh[h]) for h in hs)
            dv2 = sum(_dot_tn(pb[h], doh[h]) for h in hs)
            dq_ref[b] = dq * SB_SCALE
            dks.append(dk2)
            dvs.append(dv2)
        dkf_ref[...] = dks[0][:BAND]
        dvf_ref[...] = dvs[0][:BAND]
        for b in range(BAND_CH):
            last = b == BAND_CH - 1
            dk_ref[b] = dks[b][BAND:] if last else dks[b][BAND:] + dks[b + 1][:BAND]
            dv_ref[b] = dvs[b][BAND:] if last else dvs[b][BAND:] + dvs[b + 1][:BAND]

    cur, prev, tab, tab_prev = _band_specs()
    first = pl.BlockSpec((None, None, BAND, DIL_W), lambda g, i: (g, i, 0, 0))
    n_blk = qkv_s.shape[1]
    n_chunks = n_blk // BAND_CH
    shape = jax.ShapeDtypeStruct((3, n_blk, BAND, DIL_W), F32)
    shape_first = jax.ShapeDtypeStruct((3, n_chunks, BAND, DIL_W), F32)
    return pl.pallas_call(
        body,
        grid=(3, n_chunks),
        in_specs=[cur(0), cur(1), prev(1), cur(2), prev(2), tab, tab, tab_prev, tab_prev, cur(0), cur(1), cur(2)],
        out_specs=[cur(0), cur(0), cur(0), first, first],
        out_shape=[shape, shape, shape, shape_first, shape_first],
        compiler_params=pltpu.CompilerParams(dimension_semantics=("parallel", "parallel")),
        name="band_bwd",
    )(qkv_s, qkv_s, qkv_s, qkv_s, qkv_s, cos_t, sin_t, cos_t, sin_t, dcat_s, dcat_s, dcat_s)


def _band_combine(dq, dk, dv, dk_first, dv_first, cos_t, sin_t):
    n_chunks = dk_first.shape[1]

    def body(dq_ref, dk_ref, dkn_ref, dv_ref, dvn_ref, c_ref, s_ref, out_ref):
        nxt = (pl.program_id(1) < n_chunks - 1).astype(F32)
        for b in range(BAND_CH):
            cos, sin = c_ref[b], s_ref[b]
            dq_b, dk_b, dv_b = dq_ref[b], dk_ref[b], dv_ref[b]
            if b == BAND_CH - 1:
                dk_b = dk_b + nxt * dkn_ref[...]
                dv_b = dv_b + nxt * dvn_ref[...]
            out_ref[b, :, :DIL_W] = (dq_b * cos - _swap_half(dq_b) * sin).astype(BF16)
            out_ref[b, :, DIL_W:2 * DIL_W] = (dk_b * cos - _swap_half(dk_b) * sin).astype(BF16)
            out_ref[b, :, 2 * DIL_W:] = dv_b.astype(BF16)

    cur, _, tab, _ = _band_specs()
    nxt = pl.BlockSpec((None, None, BAND, DIL_W), lambda g, i: (g, jnp.minimum(i + 1, n_chunks - 1), 0, 0))
    return pl.pallas_call(
        body,
        grid=(3, n_chunks),
        in_specs=[cur(0), cur(0), nxt, cur(0), nxt, tab, tab],
        out_specs=pl.BlockSpec((None, BAND_CH, BAND, 3 * DIL_W), lambda g, i: (g, i, 0, 0)),
        out_shape=jax.ShapeDtypeStruct(dq.shape[:3] + (3 * DIL_W,), BF16),
        compiler_params=pltpu.CompilerParams(dimension_semantics=("parallel", "parallel")),
        name="band_combine",
    )(dq, dk, dk_first, dv, dv_first, cos_t, sin_t)


def _band_merge(ol):
    t, tm = ol.shape[1], 512

    def body(o_ref, l_ref, ob_ref, lse_ref):
        l0, l1, l2 = l_ref[0], l_ref[1], l_ref[2]
        m = jnp.maximum(jnp.maximum(l0, l1), l2)
        lse = m + jnp.log(jnp.exp(l0 - m) + jnp.exp(l1 - m) + jnp.exp(l2 - m))
        ob_ref[...] = (jnp.exp(l0 - lse) * o_ref[0] + jnp.exp(l1 - lse) * o_ref[1] + jnp.exp(l2 - lse) * o_ref[2]).astype(BF16)
        lse_ref[...] = lse

    spec = pl.BlockSpec((tm, DIL_W), lambda i: (i, 0))
    return pl.pallas_call(
        body,
        grid=(t // tm,),
        in_specs=[pl.BlockSpec((3, tm, DIL_W), lambda i: (0, i, 0)), pl.BlockSpec((3, tm, DIL_W), lambda i: (0, i, 1))],
        out_specs=[spec, spec],
        out_shape=[jax.ShapeDtypeStruct((t, DIL_W), BF16), jax.ShapeDtypeStruct((t, DIL_W), F32)],
        compiler_params=pltpu.CompilerParams(dimension_semantics=("parallel",)),
        name="band_merge",
    )(ol, ol)


def _band_delta(do_b, o_b, lse_b):
    def fn(do, o, lse):
        lane_in = lax.broadcasted_iota(jnp.int32, (DIL_W, LANES), 0)
        col = lax.broadcasted_iota(jnp.int32, (DIL_W, LANES), 1)
        sum_head = ((lane_in // HEAD_DIM == col - BAND_HEADS) & (col >= BAND_HEADS) & (col < 2 * BAND_HEADS)).astype(BF16)
        return (lse + _split_dot(do.astype(F32) * o.astype(F32), sum_head),)

    return _rowwise(fn, [(do_b, "row"), (o_b, "row"), (lse_b, "row")], [(LANES, F32, "row")], tm=512, name="band_delta")[0]


def _band_masks():
    qi = lax.broadcasted_iota(jnp.int32, (BAND, 2 * BAND), 0) + BAND
    kj = lax.broadcasted_iota(jnp.int32, (BAND, 2 * BAND), 1)
    dist = qi - kj
    row = lax.broadcasted_iota(jnp.int32, (BAND, BAND), 0)
    col = lax.broadcasted_iota(jnp.int32, (BAND, BAND), 1)
    return col <= row, (dist >= 0) & (dist <= BAND)


def _band_attend(q, k, v, valid):
    lane = lax.broadcasted_iota(jnp.int32, (1, DIL_W), 1)
    stat_lane = lax.broadcasted_iota(jnp.int32, (1, LANES), 1)
    hs = range(BAND_HEADS)
    mh = [(lane // HEAD_DIM) == h for h in hs]
    s = [jnp.where(valid, _dot_nt(jnp.where(mh[h], q, jnp.zeros_like(q)), k), NEG_INF) for h in hs]
    m = [jnp.max(s[h], axis=1, keepdims=True) for h in hs]
    p = [jnp.exp(s[h] - m[h]) for h in hs]
    den = [jnp.sum(p[h], axis=1, keepdims=True) for h in hs]
    pv = [_dot(p[h].astype(BF16), jnp.where(mh[h], v, jnp.zeros_like(v))) for h in hs]
    o = jnp.zeros((BAND, DIL_W), F32)
    lse = jnp.zeros((BAND, LANES), F32)
    for h in hs:
        o = o + pv[h] * (1.0 / den[h])
        lse = jnp.where(stat_lane == h, m[h] + jnp.log(den[h]), lse)
    return o, lse


def _band_attend_bwd(q, k, v, valid, do, st):
    lane = lax.broadcasted_iota(jnp.int32, (1, DIL_W), 1)
    hs = range(BAND_HEADS)
    mh = [(lane // HEAD_DIM) == h for h in hs]
    qh = [jnp.where(mh[h], q, jnp.zeros_like(q)) for h in hs]
    doh = [jnp.where(mh[h], do, jnp.zeros_like(do)) for h in hs]
    s = [_dot_nt(qh[h], k) for h in hs]
    dp = [_dot_nt(doh[h], v) for h in hs]
    p = [jnp.where(valid, jnp.exp(s[h] - st[:, h:h + 1]), 0.0) for h in hs]
    ds = [(p[h] * (dp[h] - st[:, BAND_HEADS + h:BAND_HEADS + h + 1])).astype(BF16) for h in hs]
    pb = [p[h].astype(BF16) for h in hs]
    dq = sum(_dot(ds[h], jnp.where(mh[h], k, jnp.zeros_like(k))) for h in hs)
    dk = sum(_dot_tn(ds[h], qh[h]) for h in hs)
    dv = sum(_dot_tn(pb[h], doh[h]) for h in hs)
    return dq, dk, dv


def _band_group_specs(lead, rows, cls, col0):
    def spec(width):
        if lead == "rows":
            return pl.BlockSpec((None, rows, width), lambda b, i: (b, 0, col0))
        return pl.BlockSpec((None, rows, cls * width), lambda b, i: (b, 0, i))
    return spec


def _band_group_fwd(a, cos_g, sin_g, *, rows, cls, steps, col0, name):
    bl = a.shape[0]
    nb = rows // BAND
    grp_w = 3 * DIL_W

    def body(a_ref, c_ref, s_ref, o_ref, l_ref, qr, kr):
        first_valid, later_valid = _band_masks()
        for j in range(cls):
            a0, t0, s0 = j * grp_w, j * DIL_W, j * LANES
            cos, sin = c_ref[:, t0:t0 + DIL_W], s_ref[:, t0:t0 + DIL_W]
            qr[...] = (_rope(a_ref[:, a0:a0 + DIL_W], cos, sin) * SB_SCALE).astype(BF16)
            kr[...] = _rope(a_ref[:, a0 + DIL_W:a0 + 2 * DIL_W], cos, sin).astype(BF16)

            def block(q0, k0, keys, valid, a0=a0, t0=t0, s0=s0):
                o, lse = _band_attend(qr[pl.ds(q0, BAND), :], kr[pl.ds(k0, keys), :],
                                      a_ref[pl.ds(k0, keys), a0 + 2 * DIL_W:a0 + grp_w], valid)
                o_ref[pl.ds(q0, BAND), t0:t0 + DIL_W] = o.astype(BF16)
                l_ref[pl.ds(q0, BAND), s0:s0 + LANES] = lse

            block(0, 0, BAND, first_valid)
            if nb > 1:
                def later(b, carry, block=block):
                    block(pl.multiple_of(b * BAND, BAND), pl.multiple_of((b - 1) * BAND, BAND), 2 * BAND, later_valid)
                    return carry

                lax.fori_loop(1, nb, later, 0)

    lead = "rows" if col0 is not None else "cols"
    spec = _band_group_specs(lead, rows, cls, col0)
    tab = pl.BlockSpec((rows, cls * DIL_W), lambda b, i: (0, i))
    n_cls = cos_g.shape[1] // DIL_W
    return pl.pallas_call(
        body,
        grid=(bl, steps),
        in_specs=[spec(grp_w), tab, tab],
        out_specs=[pl.BlockSpec((None, rows, cls * DIL_W), lambda b, i: (b, 0, i)),
                   pl.BlockSpec((None, rows, cls * LANES), lambda b, i: (b, 0, i))],
        out_shape=[jax.ShapeDtypeStruct((bl, rows, n_cls * DIL_W), BF16), jax.ShapeDtypeStruct((bl, rows, n_cls * LANES), F32)],
        scratch_shapes=[pltpu.VMEM((rows, DIL_W), BF16), pltpu.VMEM((rows, DIL_W), BF16)],
        compiler_params=pltpu.CompilerParams(dimension_semantics=("parallel", "parallel")),
        name=name,
    )(a, cos_g, sin_g)


def _band_group_bwd(a, do, st, cos_g, sin_g, *, rows, cls, steps, col0, name, side=None):
    bl = a.shape[0]
    nb = rows // BAND
    grp_w = 3 * DIL_W
    side_arrays = side[1] if side else []
    n_side = len(side_arrays)

    def body(a_ref, do_ref, st_ref, c_ref, s_ref, *rest):
        out_ref = rest[n_side]
        qr, kr, dk_acc, dv_acc = rest[2 * n_side + 1:2 * n_side + 5]
        if n_side:
            step = pl.program_id(0) * steps + pl.program_id(1)
            finish = _run_side(side, rest[:n_side], rest[n_side + 1:2 * n_side + 1], rest[2 * n_side + 5:], step, bl * steps)
        first_valid, later_valid = _band_masks()
        for j in range(cls):
            a0, t0 = j * grp_w, j * DIL_W
            cos, sin = c_ref[:, t0:t0 + DIL_W], s_ref[:, t0:t0 + DIL_W]
            qr[...] = (_rope(a_ref[:, a0:a0 + DIL_W], cos, sin) * SB_SCALE).astype(BF16)
            kr[...] = _rope(a_ref[:, a0 + DIL_W:a0 + 2 * DIL_W], cos, sin).astype(BF16)
            dk_acc[...] = jnp.zeros_like(dk_acc)
            dv_acc[...] = jnp.zeros_like(dv_acc)

            def block(q0, k0, keys, valid, a0=a0, t0=t0, s0=j * LANES):
                qrows, krows = pl.ds(q0, BAND), pl.ds(k0, keys)
                dq, dk, dv = _band_attend_bwd(
                    qr[qrows, :], kr[krows, :], a_ref[krows, a0 + 2 * DIL_W:a0 + grp_w], valid,
                    do_ref[qrows, t0:t0 + DIL_W], st_ref[qrows, s0:s0 + LANES])
                dq = dq * SB_SCALE
                out_ref[qrows, a0:a0 + DIL_W] = (dq * c_ref[qrows, t0:t0 + DIL_W]
                                                 - _swap_half(dq) * s_ref[qrows, t0:t0 + DIL_W]).astype(BF16)
                dk_acc[krows, :] += dk
                dv_acc[krows, :] += dv

            block(0, 0, BAND, first_valid)
            if nb > 1:
                def later(b, carry, block=block):
                    block(pl.multiple_of(b * BAND, BAND), pl.multiple_of((b - 1) * BAND, BAND), 2 * BAND, later_valid)
                    return carry

                lax.fori_loop(1, nb, later, 0)
            dk = dk_acc[...]
            out_ref[:, a0 + DIL_W:a0 + 2 * DIL_W] = (dk * cos - _swap_half(dk) * sin).astype(BF16)
            out_ref[:, a0 + 2 * DIL_W:a0 + grp_w] = dv_acc[...].astype(BF16)
        if n_side:
            finish()

    lead = "rows" if col0 is not None else "cols"
    spec = _band_group_specs(lead, rows, cls, col0)
    dspec = _band_group_specs(lead, rows, cls, 0 if col0 is not None else None)
    tab = pl.BlockSpec((rows, cls * DIL_W), lambda b, i: (0, i))
    n_cls = cos_g.shape[1] // DIL_W
    out = pl.pallas_call(
        body,
        grid=(bl, steps),
        in_specs=[spec(grp_w), dspec(DIL_W), dspec(LANES), tab, tab] + [ANY] * n_side,
        out_specs=[pl.BlockSpec((None, rows, cls * grp_w), lambda b, i: (b, 0, i))] + [ANY] * n_side,
        out_shape=[jax.ShapeDtypeStruct((bl, rows, n_cls * grp_w), BF16)] + (_side_out_shapes(side) if n_side else []),
        scratch_shapes=[pltpu.VMEM((rows, DIL_W), BF16), pltpu.VMEM((rows, DIL_W), BF16),
                        pltpu.VMEM((rows, DIL_W), F32), pltpu.VMEM((rows, DIL_W), F32)] + (_side_sems(side) if n_side else []),
        compiler_params=pltpu.CompilerParams(dimension_semantics=("arbitrary", "arbitrary") if n_side else ("parallel", "parallel")),
        name=name,
    )(a, do, st, cos_g, sin_g, *side_arrays)
    return (out[0], out[1:]) if n_side else out[0]


def _band_merge3(groups):
    t, tm = groups[0][0].shape[0], 512

    def body(o0, l0, o1, l1, o2, l2, ob_ref, lse_ref):
        a, b, c = l0[...], l1[...], l2[...]
        m = jnp.maximum(jnp.maximum(a, b), c)
        lse = m + jnp.log(jnp.exp(a - m) + jnp.exp(b - m) + jnp.exp(c - m))
        lane = lax.broadcasted_iota(jnp.int32, (1, DIL_W), 1)
        acc = jnp.zeros((tm, DIL_W), F32)
        for o_ref, l in ((o0, a), (o1, b), (o2, c)):
            share = jnp.exp(l - lse)
            spread = jnp.zeros((tm, DIL_W), F32)
            for h in range(BAND_HEADS):
                spread = jnp.where(lane // HEAD_DIM == h, share[:, h:h + 1], spread)
            acc = acc + spread * o_ref[...].astype(F32)
        ob_ref[...] = acc.astype(BF16)
        stat_lane = lax.broadcasted_iota(jnp.int32, (1, LANES), 1)
        lse_ref[...] = jnp.where(stat_lane < BAND_HEADS, lse, 0.0)

    spec = pl.BlockSpec((tm, DIL_W), lambda i: (i, 0))
    spec_l = pl.BlockSpec((tm, LANES), lambda i: (i, 0))
    return pl.pallas_call(
        body,
        grid=(t // tm,),
        in_specs=[spec, spec_l] * 3,
        out_specs=[spec, spec_l],
        out_shape=[jax.ShapeDtypeStruct((t, DIL_W), BF16), jax.ShapeDtypeStruct((t, LANES), F32)],
        compiler_params=pltpu.CompilerParams(dimension_semantics=("parallel",)),
        name="band_merge",
    )(*[a for g in groups for a in g])


MEM_T = 512
MEM_SCALE = 128 ** -0.5
MEM_Q_COL = (D_IN - MEM_W) // LANES


def _mem_specs():
    q = pl.BlockSpec((None, MEM_T, LANES), lambda b, h, i: (b, i, MEM_Q_COL + h))
    k = pl.BlockSpec((None, MEM_LEN, LANES), lambda b, h, i: (b, 0, h))
    v = pl.BlockSpec((None, MEM_LEN, LANES), lambda b, h, i: (b, 0, MEM_W // LANES + h))
    blk = pl.BlockSpec((None, MEM_T, LANES), lambda b, h, i: (b, i, h))
    return q, k, v, blk


def _mem_probs(q, k):
    s = _dot_nt(q, k) * MEM_SCALE
    p = jnp.exp(s - jnp.max(s, axis=1, keepdims=True))
    return p * (1.0 / jnp.sum(p, axis=1, keepdims=True))


def _mem_fwd(proj3, kv3):
    bl = proj3.shape[0]

    def body(q_ref, k_ref, v_ref, o_ref):
        p = _mem_probs(q_ref[...], k_ref[...])
        o_ref[...] = _dot(p.astype(BF16), v_ref[...]).astype(BF16)

    q, k, v, blk = _mem_specs()
    return pl.pallas_call(
        body,
        grid=(bl, MEM_W // LANES, SEQ // MEM_T),
        in_specs=[q, k, v],
        out_specs=blk,
        out_shape=jax.ShapeDtypeStruct((bl, SEQ, MEM_W), BF16),
        compiler_params=pltpu.CompilerParams(dimension_semantics=("parallel", "parallel", "parallel")),
        name="mem_fwd",
    )(proj3, kv3, kv3)


def _mem_bwd(proj3, kv3, do_c):
    bl = proj3.shape[0]

    def body(q_ref, k_ref, v_ref, do_ref, dq_ref, dk_ref, dv_ref):
        @pl.when(pl.program_id(2) == 0)
        def _():
            dk_ref[...] = jnp.zeros_like(dk_ref)
            dv_ref[...] = jnp.zeros_like(dv_ref)

        q, k, do = q_ref[...], k_ref[...], do_ref[...]
        p = _mem_probs(q, k)
        dp = _dot_nt(do, v_ref[...])
        ds = (p * (dp - jnp.sum(p * dp, axis=1, keepdims=True)) * MEM_SCALE).astype(BF16)
        dq_ref[...] = _dot(ds, k).astype(BF16)
        dk_ref[...] += _dot_tn(ds, q)
        dv_ref[...] += _dot_tn(p.astype(BF16), do)

    q, k, v, blk = _mem_specs()
    kv_out = pl.BlockSpec((None, MEM_LEN, LANES), lambda b, h, i: (b, 0, h))
    return pl.pallas_call(
        body,
        grid=(bl, MEM_W // LANES, SEQ // MEM_T),
        in_specs=[q, k, v, blk],
        out_specs=[blk, kv_out, kv_out],
        out_shape=[jax.ShapeDtypeStruct((bl, SEQ, MEM_W), BF16), jax.ShapeDtypeStruct((bl, MEM_LEN, MEM_W), F32),
                   jax.ShapeDtypeStruct((bl, MEM_LEN, MEM_W), F32)],
        compiler_params=pltpu.CompilerParams(dimension_semantics=("parallel", "parallel", "arbitrary")),
        name="mem_bwd",
    )(proj3, kv3, kv3, do_c)


def _place():
    x, y, c = lax.axis_index("x"), lax.axis_index("y"), lax.axis_index("c")
    return x, y, c


def _other_chips(x, y):
    return [(1 - x, y), (x, 1 - y), (1 - x, 1 - y)]


def _remote(src, dst, send_sem, recv_sem, to):
    return pltpu.make_async_remote_copy(src_ref=src, dst_ref=dst, send_sem=send_sem, recv_sem=recv_sem,
                                        device_id=to, device_id_type=MESH)


ANY = pl.BlockSpec(memory_space=pl.ANY)


def _gather_weights(shards):
    n = len(shards)

    def body(*refs):
        send, forward, finish = _gather_phases(refs[:n], refs[n:2 * n], *refs[2 * n:])
        send()
        forward()
        finish()

    return pl.pallas_call(
        body,
        in_specs=[ANY] * n,
        out_specs=[ANY] * n,
        out_shape=_gather_out_shapes(shards),
        scratch_shapes=_gather_sems(n),
        name="gather_weights",
    )(*shards)


def _gather_out_shapes(shards):
    return [jax.ShapeDtypeStruct((N_CHIPS,) + s.shape, s.dtype) for s in shards]


def _gather_sems(n):
    return [pltpu.SemaphoreType.DMA((6 * n,)), pltpu.SemaphoreType.DMA((6 * n,))]


def _gather_phases(in_refs, out_refs, send_sems, recv_sems):
    x, y, c = _place()
    sibling = (x, y, 1 - c)
    chips = _other_chips(x, y)
    first, passed = [], []
    for k in range(len(in_refs)):
        hf = in_refs[k].shape[0] // 2

        def half(px, py, pc, k=k, hf=hf):
            return out_refs[k].at[2 * px + py, pl.ds(pc * hf, hf), :]

        src = in_refs[k].at[pl.ds(c * hf, hf), :]
        for j, chip in enumerate(chips):
            s = 6 * k + j
            first.append(_remote(src, half(x, y, c), send_sems.at[s], recv_sems.at[s], (*chip, c)))
            passed.append((_remote(src, half(*chip, c), send_sems.at[s], recv_sems.at[s], (*chip, c)),
                           _remote(half(*chip, c), half(*chip, c), send_sems.at[s + 3], recv_sems.at[s + 3], sibling),
                           _remote(src, half(*chip, 1 - c), send_sems.at[s + 3], recv_sems.at[s + 3], sibling)))

    def send():
        for cp in first:
            cp.start()

    def forward():
        for landed, fwd, _ in passed:
            landed.wait_recv()
            fwd.start()

    def finish():
        for _, _, from_sibling in passed:
            from_sibling.wait_recv()
        for cp in first:
            cp.wait_send()
        for _, fwd, _ in passed:
            fwd.wait_send()

    return send, forward, finish


def _pair_exchange(grads, *, name):
    n = len(grads)
    side = ("pair", grads)

    def body(*refs):
        send, _, finish = _side_phases(side, refs[:n], refs[n:2 * n], refs[2 * n:])
        send()
        finish()

    return pl.pallas_call(
        body,
        in_specs=[ANY] * n,
        out_specs=[ANY] * n,
        out_shape=_side_out_shapes(side),
        scratch_shapes=_side_sems(side),
        name=name,
    )(*grads)


def _pair_exchange_phases(g_refs, land_refs, send_sems, recv_sems):
    x, y, c = _place()
    cps = []
    for k in range(len(g_refs)):
        hf = g_refs[k].shape[1] // 2
        src = g_refs[k].at[:, pl.ds((1 - c) * hf, hf), :]
        cps.append(_remote(src, land_refs[k], send_sems.at[k], recv_sems.at[k], (x, y, 1 - c)))

    def send():
        for cp in cps:
            cp.start()

    def finish():
        for cp in cps:
            cp.wait()

    return send, finish


def _side_out_shapes(side):
    kind, arrays = side
    if kind == "gather":
        return _gather_out_shapes(arrays)
    if kind == "pair":
        return [jax.ShapeDtypeStruct((N_CHIPS, g.shape[1] // 2, g.shape[2]), g.dtype) for g in arrays]
    return [jax.ShapeDtypeStruct(p.shape, p.dtype) for p in arrays]


def _side_sems(side):
    kind, arrays = side
    n = len(arrays)
    if kind == "gather":
        return _gather_sems(n)
    if kind == "pair":
        return [pltpu.SemaphoreType.DMA((n,)), pltpu.SemaphoreType.DMA((n,))]
    return _chip_exchange_sems(n)


def _side_phases(side, in_refs, out_refs, sems):
    kind = side[0]
    if kind == "gather":
        return _gather_phases(in_refs, out_refs, *sems)
    send, finish = (_pair_exchange_phases if kind == "pair" else _chip_exchange_phases)(in_refs, out_refs, *sems)
    return send, None, finish


def _run_side(side, in_refs, out_refs, sems, step, n_steps):
    first, mid, last = _side_phases(side, in_refs, out_refs, sems)
    pl.when(step == 0)(first)
    if mid is not None:
        pl.when(step == n_steps // 2)(mid)
    return lambda: pl.when(step == n_steps - 1)(last)


def _pair_add(g, land, c_arr, *, name):
    _, a, b = g.shape
    hf = a // 2

    def body(c_ref, g_ref, l_ref, o_ref):
        o_ref[...] = (g_ref[...] + l_ref[...]).astype(BF16)

    return pl.pallas_call(
        body,
        grid_spec=pltpu.PrefetchScalarGridSpec(
            num_scalar_prefetch=1,
            grid=(N_CHIPS,),
            in_specs=[pl.BlockSpec((None, None, hf, b), lambda s, c_ref: (s, c_ref[0], 0, 0)),
                      pl.BlockSpec((None, hf, b), lambda s, c_ref: (s, 0, 0))],
            out_specs=pl.BlockSpec((None, hf, b), lambda s, c_ref: (s, 0, 0)),
        ),
        out_shape=jax.ShapeDtypeStruct((N_CHIPS, hf, b), BF16),
        compiler_params=pltpu.CompilerParams(dimension_semantics=("parallel",)),
        name=name,
    )(c_arr, g.reshape(N_CHIPS, 2, hf, b), land)


def _chip_exchange(parts):
    n = len(parts)

    def body(*refs):
        send, finish = _chip_exchange_phases(refs[:n], refs[n:2 * n], *refs[2 * n:])
        send()
        finish()

    return pl.pallas_call(
        body,
        in_specs=[ANY] * n,
        out_specs=[ANY] * n,
        out_shape=[jax.ShapeDtypeStruct(p.shape, p.dtype) for p in parts],
        scratch_shapes=_chip_exchange_sems(n),
        name="chip_exchange",
    )(*parts)


def _chip_exchange_sems(n):
    return [pltpu.SemaphoreType.DMA((3 * n,)), pltpu.SemaphoreType.DMA((3 * n,))]


def _chip_exchange_phases(p_refs, land_refs, send_sems, recv_sems):
    x, y, c = _place()
    me = 2 * x + y
    sends, recvs = [], []
    for k in range(len(p_refs)):
        for j, (cx, cy) in enumerate(_other_chips(x, y)):
            s = 3 * k + j
            sends.append(_remote(p_refs[k].at[2 * cx + cy], land_refs[k].at[me], send_sems.at[s], recv_sems.at[s], (cx, cy, c)))
            recvs.append(_remote(p_refs[k].at[me], land_refs[k].at[2 * cx + cy], send_sems.at[s], recv_sems.at[s], (cx, cy, c)))

    def send():
        for cp in sends:
            cp.start()

    def finish():
        for cp in recvs:
            cp.wait_recv()
        for cp in sends:
            cp.wait_send()

    return send, finish


def _chip_add(land, part, me_arr, *, name):
    _, r, b = land.shape

    def body(me_ref, p_ref, l1_ref, l2_ref, l3_ref, o_ref):
        o_ref[...] = ((p_ref[...].astype(F32) + l1_ref[...].astype(F32)) + l2_ref[...].astype(F32)) + l3_ref[...].astype(F32)

    tr = r // 2
    other = lambda j: pl.BlockSpec((None, tr, b), lambda i, me_ref: (jnp.bitwise_xor(me_ref[0], j), i, 0))
    return pl.pallas_call(
        body,
        grid_spec=pltpu.PrefetchScalarGridSpec(
            num_scalar_prefetch=1,
            grid=(r // tr,),
            in_specs=[pl.BlockSpec((None, tr, b), lambda i, me_ref: (me_ref[0], i, 0)), other(2), other(1), other(3)],
            out_specs=pl.BlockSpec((tr, b), lambda i, me_ref: (i, 0)),
        ),
        out_shape=jax.ShapeDtypeStruct((r, b), F32),
        compiler_params=pltpu.CompilerParams(dimension_semantics=("parallel",)),
        name=name,
    )(me_arr, part, land, land, land)


def _pair_share(halves):
    n = len(halves)

    def body(*refs):
        h_refs, out_refs = refs[:n], refs[n:2 * n]
        send_sems, recv_sems = refs[2 * n:]
        x, y, c = _place()
        cps = [_remote(h_refs[k], out_refs[k], send_sems.at[k], recv_sems.at[k], (x, y, 1 - c)) for k in range(n)]
        for cp in cps:
            cp.start()
        for cp in cps:
            cp.wait()

    return pl.pallas_call(
        body,
        in_specs=[ANY] * n,
        out_specs=[ANY] * n,
        out_shape=[jax.ShapeDtypeStruct(h.shape, F32) for h in halves],
        scratch_shapes=[pltpu.SemaphoreType.DMA((n,)), pltpu.SemaphoreType.DMA((n,))],
        name="pair_share",
    )(*halves)


def _all_sum_small(part):
    def body(p_ref, o_ref, slots, send_sems, recv_sems):
        x, y, c = _place()
        me = 4 * x + 2 * y + c
        slots[me] = p_ref[...]
        peers = [(x ^ dx, y ^ dy, c ^ dc) for dx in (0, 1) for dy in (0, 1) for dc in (0, 1)][1:]
        sends = [_remote(p_ref, slots.at[me], send_sems.at[k], recv_sems.at[k], peer) for k, peer in enumerate(peers)]
        for cp in sends:
            cp.start()
        for k, (px, py, pc) in enumerate(peers):
            _remote(p_ref, slots.at[4 * px + 2 * py + pc], send_sems.at[k], recv_sems.at[k], (px, py, pc)).wait_recv()
        for cp in sends:
            cp.wait_send()
        acc = slots[0]
        for d in range(1, 8):
            acc = acc + slots[d]
        o_ref[...] = acc

    vmem = pl.BlockSpec(memory_space=pltpu.VMEM)
    return pl.pallas_call(
        body,
        in_specs=[vmem],
        out_specs=vmem,
        out_shape=jax.ShapeDtypeStruct(part.shape, F32),
        scratch_shapes=[pltpu.VMEM((8,) + part.shape, F32), pltpu.SemaphoreType.DMA((7,)), pltpu.SemaphoreType.DMA((7,))],
        name="all_sum_small",
    )(part)


def _deinterleave(a, d):
    b, s, c = a.shape
    return a.reshape(b, s // d, d, c).transpose(0, 2, 1, 3).reshape(b * s // BAND, BAND, c)


def _reinterleave(a, d, b):
    c = a.shape[-1]
    return a.reshape(b, d, SEQ // d, c).transpose(0, 2, 1, 3).reshape(b, SEQ, c)


def _rope_tables():
    half = HEAD_DIM // 2
    inv_freq = np.float32(ROPE_THETA) ** (-np.arange(half, dtype=np.float32) * np.float32(2.0) / np.float32(HEAD_DIM))
    ang = np.arange(SEQ, dtype=np.float32)[:, None] * inv_freq[None, :].astype(np.float32)
    cos = np.tile(np.cos(ang).astype(np.float32), (1, 2 * BAND_HEADS))
    sin = np.tile(np.concatenate([-np.sin(ang), np.sin(ang)], axis=1).astype(np.float32), (1, BAND_HEADS))
    return jnp.asarray(cos), jnp.asarray(sin)


def _band_groups():
    out = []
    for d in DIL_D:
        rows = SEQ // d
        cls = max(1, 512 // rows) if d > 1 else 1
        out.append(dict(rows=rows, cls=cls, steps=d // cls))
    return out


def _local_step(x, mem, loss_target, g_pre_mix, g_post_mix, g_pre_ffn, g_post_ffn, g_mem, b_gate, w, comm=None):
    bl = x.shape[0]
    t = bl * SEQ
    chips = range(N_CHIPS)
    half_ff = D_FF // 2

    def with_gathered(w, names, gathered, shards):
        return {**w, **{name: lax.dynamic_update_slice(g, s[None], (comm["me"][0], 0, 0))
                        for name, g, s in zip(names, gathered, shards)}}

    x2 = x.reshape(t, D_MODEL)
    tgt2 = loss_target.reshape(t, D_MODEL)
    mem2 = mem.reshape(bl * MEM_LEN, D_MODEL)

    h = _norm_fwd(x2, g_pre_mix, name="norm_x", side=("gather", comm["first_shards"]) if comm else None)
    if comm:
        w = with_gathered(w, comm["first_names"], h[1], comm["first_shards"])
        h = h[0]
    w_in_full = _join_shards(w["w_in"])
    proj = _mm([(h, w_in_full)], nt=False, tn=2176, out_dtypes=[BF16], name="proj",
               side=("gather", comm["mid_shards"]) if comm else None)
    if comm:
        w = with_gathered(w, comm["mid_names"], proj[1], comm["mid_shards"])
        proj = proj[0]
    w_mem_kv_full = w["w_mem_kv"].reshape(D_MODEL, 2 * MEM_W)
    gates = _mm([(h, w["w_gate"], None, "j")], nt=False,tn=w["w_gate"].shape[2], out_dtypes=[BF16], name="gates",
                bias=b_gate, epilogue=lambda acc: (_sigmoid(acc),))
    hm = _norm_fwd(mem2, g_mem, name="norm_mem")
    kv_m = _mm([(hm, w_mem_kv_full)], nt=False,tn=1024, out_dtypes=[BF16], name="mem_kv")
    proj3 = proj.reshape(bl, SEQ, D_IN)
    kv3 = kv_m.reshape(bl, MEM_LEN, 2 * MEM_W)

    o_a, o_a32, sb_weights, late_gathered = _sb_fwd(proj3, comm["late_shards"] if comm else [])
    if comm:
        w = with_gathered(w, comm["late_names"], late_gathered, comm["late_shards"])
    w_o_full = w["w_o"].reshape(D_MODEL, D_MODEL)
    w_ffn_out_full = w["w_ffn_out"].reshape(D_FF, D_MODEL)

    cos_t, sin_t = _rope_tables()
    dil0 = 3 * SB_W

    grp_w = 3 * DIL_W
    band = []
    for g, (d, cfg) in enumerate(zip(DIL_D, _band_groups())):
        a_g = proj3 if d == 1 else proj3[:, :, dil0 + g * grp_w:dil0 + (g + 1) * grp_w].reshape(bl, SEQ // d, d * grp_w)
        band.append(dict(cfg, a=a_g, col0=dil0 // grp_w if d == 1 else None, cos=cos_t.reshape(SEQ // d, d * DIL_W),
                         sin=sin_t.reshape(SEQ // d, d * DIL_W)))
    outs = [_band_group_fwd(b["a"], b["cos"], b["sin"], rows=b["rows"], cls=b["cls"], steps=b["steps"], col0=b["col0"],
                            name=f"band_fwd_{g}") for g, b in enumerate(band)]
    o_b, lse_b = _band_merge3([(o.reshape(t, DIL_W), l.reshape(t, LANES)) for o, l in outs])

    o_c = _mem_fwd(proj3, kv3)

    o_a2, o_c2 = o_a.reshape(t, SB_W), o_c.reshape(t, MEM_W)
    y_a, y_b, y_c, merged = _branch_merge_fwd(o_a2, o_b, o_c2, w["w_br_sb"], w["w_br_dil"], w["w_br_mem"], gates)
    mix = _mm([(merged, w_o_full)], nt=False,tn=1024, out_dtypes=[F32], name="mix")
    x1, h2 = _mid_fwd(mix, x2, g_post_mix, g_pre_ffn)
    gg, uu, f = _ffn_in_fwd(h2, w["w_ffn_in"])
    f2 = _mm([(f, w_ffn_out_full)], nt=False,tn=1024, out_dtypes=[F32], name="ffn_out")

    dy, df2, dg_post_ffn, loss_row = _loss_bwd(f2, x1, g_post_ffn, tgt2)

    dg_ffn, du_ffn = _mm([(df2, w_ffn_out_full)], nt=True,tn=half_ff, out_dtypes=[BF16, BF16], name="d_ffn_act",
                         extras=(gg, uu), epilogue=_swiglu_bwd_epilogue)
    gw = {}
    gw["w_ffn_out"] = _mm_tn(f, df2, tm=half_ff, tn=1024, name="gw_ffn_out").reshape(N_CHIPS, D_FF // N_CHIPS, D_MODEL)
    gw_ffn_g = _mm_tn(h2, dg_ffn, tm=1024, tn=half_ff, name="gw_ffn_gate", out_shards=True, slots=(N_CHIPS, 0))
    gw["w_ffn_in"] = _mm_tn(h2, du_ffn, tm=1024, tn=half_ff, name="gw_ffn_up", out_shards=True, slots=(N_CHIPS, 2), into=gw_ffn_g)
    dh2 = _mm([(dg_ffn, w["w_ffn_in"], 0, 0), (dg_ffn, w["w_ffn_in"], 1, 1), (du_ffn, w["w_ffn_in"], 0, 2),
               (du_ffn, w["w_ffn_in"], 1, 3)], nt=True, tn=1024, out_dtypes=[BF16], name="d_h2")
    dx1, dmix, dg_pre_ffn, dg_post_mix = _mid_bwd(dh2, x1, mix, g_pre_ffn, g_post_mix, dy)

    gw["w_o"] = _mm_tn(merged, dmix, tm=1024, tn=1024, name="gw_o").reshape(N_CHIPS, D_MODEL // N_CHIPS, D_MODEL)
    dmerged = _mm([(dmix, w_o_full)], nt=True, tn=1024, out_dtypes=[BF16], name="d_merged")
    dy_a, dy_b, dy_c, dgpre, db_gate = _gate_bwd(dmerged, gates, y_a, y_b, y_c)
    br_cols = D_MODEL // N_CHIPS
    gw["w_br_sb"] = _mm_tn(o_a2, dy_a, tm=512, tn=br_cols, name="gw_br_sb", out_shards=True)
    gw["w_br_dil"] = _mm_tn(o_b, dy_b, tm=256, tn=br_cols, name="gw_br_dil", out_shards=True)
    gw["w_br_mem"] = _mm_tn(o_c2, dy_c, tm=512, tn=br_cols, name="gw_br_mem", out_shards=True)
    gw["w_gate"] = _mm_tn(h, dgpre, tm=1024, tn=w["w_gate"].shape[2], name="gw_gate", out_shards=True)
    do_a = _mm([(dy_a, w["w_br_sb"], s, s) for s in chips], nt=True,tn=SB_W, out_dtypes=[BF16], name="d_o_a")
    do_b = _mm([(dy_b, w["w_br_dil"], s, s) for s in chips], nt=True,tn=DIL_W, out_dtypes=[BF16], name="d_o_b")
    do_c = _mm([(dy_c, w["w_br_mem"], s, s) for s in chips], nt=True,tn=MEM_W, out_dtypes=[BF16], name="d_o_c")

    dq_c, dk_m, dv_m = _mem_bwd(proj3, kv3, do_c.reshape(bl, SEQ, MEM_W))
    dkv_m = jnp.concatenate([dk_m, dv_m], axis=-1).reshape(bl * MEM_LEN, 2 * MEM_W).astype(BF16)
    gw["w_mem_kv"] = _mm_tn(hm, dkv_m, tm=1024, tn=1024, name="gw_mem_kv").reshape(N_CHIPS, D_MODEL // N_CHIPS, 2 * MEM_W)
    dhm = _mm([(dkv_m, w_mem_kv_full)], nt=True,tn=1024, out_dtypes=[F32], name="d_hm")
    dg_mem = _mem_norm_bwd(dhm, mem2, g_mem)

    stats = _band_delta(do_b, o_b, lse_b)
    early = [name for name, _, _ in PACK if name != "w_in"] if comm else []
    grads = [gw[name] for name in early]
    d_dil = []
    for g, (d, b) in enumerate(zip(DIL_D, band)):
        out = _band_group_bwd(b["a"], do_b.reshape(bl, SEQ // d, d * DIL_W), stats.reshape(bl, SEQ // d, d * LANES),
                              b["cos"], b["sin"], rows=b["rows"], cls=b["cls"], steps=b["steps"], col0=b["col0"],
                              name=f"band_bwd_{g}", side=("pair", grads) if comm and g == 0 else None)
        if comm and g == 0:
            out, lands = out
        d_dil.append(out.reshape(bl, SEQ, grp_w))

    parts = [_pair_add(g, l, comm["c"], name="pair_add_" + name) for name, g, l in zip(early, grads, lands)] if comm else []
    dq_a, dk_a, dv_a, lands = _sb_bwd(proj3, o_a32, do_a.reshape(bl, SEQ, SB_W), sb_weights, parts)
    reduced = {name: (p, l) for name, p, l in zip(early, parts, lands)}

    in_cols = D_IN // N_CHIPS
    dproj_s = _split_to_shards([a.reshape(t, a.shape[-1]) for a in [dq_a, dk_a, dv_a] + d_dil + [dq_c]], name="dproj_shards")
    gw["w_in"] = _mm_tn(h, dproj_s, tm=1024, tn=in_cols, name="gw_in")
    if comm:
        land = _pair_exchange([gw["w_in"]], name="pair_exchange_w_in")[0]
        part_in = _pair_add(gw["w_in"], land, comm["c"], name="pair_add_w_in")
    dh = _mm([(dproj_s, w["w_in"], s, s) for s in chips] + [(dgpre, w["w_gate"], s, s) for s in chips],
             nt=True, tn=1024, out_dtypes=[BF16], name="d_h", side=("chip", [part_in]) if comm else None)
    if comm:
        dh, (land_in,) = dh
        reduced["w_in"] = (part_in, land_in)
    grad_x, dg_pre_mix = _first_bwd(dh, x2, g_pre_mix, dx1)
    small = jnp.concatenate([dg_pre_mix, dg_post_mix, dg_pre_ffn, dg_post_ffn, dg_mem, db_gate.reshape(3, D_MODEL)], axis=0)
    return loss_row[0, 0], grad_x.reshape(bl, SEQ, D_MODEL), gw, small, reduced


def kernel(x, mem, g_pre_mix, g_post_mix, g_pre_ffn, g_post_ffn, g_mem, w_in, w_mem_kv, w_br_sb, w_br_dil, w_br_mem, w_gate, b_gate, w_o, w_ffn_in, w_ffn_out, loss_target, m_g_pre_mix, m_g_post_mix, m_g_pre_ffn, m_g_post_ffn, m_g_mem, m_w_in, m_w_mem_kv, m_w_br_sb, m_w_br_dil, m_w_br_mem, m_w_gate, m_b_gate, m_w_o, m_w_ffn_in, m_w_ffn_out, v_g_pre_mix, v_g_post_mix, v_g_pre_ffn, v_g_post_ffn, v_g_mem, v_w_in, v_w_mem_kv, v_w_br_sb, v_w_br_dil, v_w_br_mem, v_w_gate, v_b_gate, v_w_o, v_w_ffn_in, v_w_ffn_out):
    w_shards = dict(w_in=w_in[0], w_mem_kv=w_mem_kv[0], w_br_sb=w_br_sb[0], w_br_dil=w_br_dil[0], w_br_mem=w_br_mem[0],
                    w_gate=w_gate[0], w_o=w_o[0], w_ffn_in=w_ffn_in[0], w_ffn_out=w_ffn_out[0])
    m_shards = dict(w_in=m_w_in[0], w_mem_kv=m_w_mem_kv[0], w_br_sb=m_w_br_sb[0], w_br_dil=m_w_br_dil[0], w_br_mem=m_w_br_mem[0],
                    w_gate=m_w_gate[0], w_o=m_w_o[0], w_ffn_in=m_w_ffn_in[0], w_ffn_out=m_w_ffn_out[0])
    v_shards = dict(w_in=v_w_in[0], w_mem_kv=v_w_mem_kv[0], w_br_sb=v_w_br_sb[0], w_br_dil=v_w_br_dil[0], w_br_mem=v_w_br_mem[0],
                    w_gate=v_w_gate[0], w_o=v_w_o[0], w_ffn_in=v_w_ffn_in[0], w_ffn_out=v_w_ffn_out[0])

    names = [name for name, _, _ in PACK]
    c_arr = lax.axis_index("c").astype(jnp.int32).reshape(1)
    me_arr = (2 * lax.axis_index("x") + lax.axis_index("y")).astype(jnp.int32).reshape(1)
    mid_names = ["w_gate", "w_mem_kv"]
    late_names = [name for name in names if name not in ["w_in"] + mid_names]
    bf = {name: w_shards[name].astype(BF16) for name in names}
    comm = dict(c=c_arr, me=me_arr, first_names=["w_in"], first_shards=[bf["w_in"]],
                mid_names=mid_names, mid_shards=[bf[name] for name in mid_names],
                late_names=late_names, late_shards=[bf[name] for name in late_names])

    loss_local, grad_x, gw, small, reduced = _local_step(x, mem, loss_target, g_pre_mix, g_post_mix, g_pre_ffn, g_post_ffn,
                                                         g_mem, b_gate, {}, comm)
    loss = lax.psum(loss_local, ("x", "y", "c"))

    halves =[_chip_add(reduced[name][1], reduced[name][0], me_arr, name="chip_add_" + name) for name in names]
    theirs = _pair_share(halves)
    small = _all_sum_small(small)

    upd = {}
    for name, mine, other in zip(names, halves, theirs):
        upd[name] = _adamw_halves(w_shards[name], mine, other, m_shards[name], v_shards[name], c_arr, name="adamw_" + name)
    g_shards = {name: u[0] for name, u in upd.items()}

    def small8(gs, b):
        return jnp.concatenate(gs + [b.reshape(3, D_MODEL)], axis=0)

    sw = small8([g_pre_mix, g_post_mix, g_pre_ffn, g_post_ffn, g_mem], b_gate)
    sm = small8([m_g_pre_mix, m_g_post_mix, m_g_pre_ffn, m_g_post_ffn, m_g_mem], m_b_gate)
    sv = small8([v_g_pre_mix, v_g_post_mix, v_g_pre_ffn, v_g_post_ffn, v_g_mem], v_b_gate)
    s_upd = _adamw(sw, small, sm, sv, tm=8, name="adamw_small")

    def small_out(a):
        return [a[0:1], a[1:2], a[2:3], a[3:4], a[4:5]]

    order = ["w_in", "w_mem_kv", "w_br_sb", "w_br_dil", "w_br_mem", "w_gate", "b_gate", "w_o", "w_ffn_in", "w_ffn_out"]

    def leaves(small_arr, big):
        out = small_out(small_arr)
        for name in order:
            out.append(small_arr[5:8].reshape(1, 3 * D_MODEL) if name == "b_gate" else big[name][None])
        return out

    grads_out = leaves(small, g_shards)
    delta_out = leaves(s_upd[0], {n: u[1] for n, u in upd.items()})
    m_out = leaves(s_upd[1], {n: u[2] for n, u in upd.items()})
    v_out = leaves(s_upd[2], {n: u[3] for n, u in upd.items()})
    return (loss, grad_x, *grads_out, *delta_out, *m_out, *v_out)
```

```python
import jax
import jax.numpy as jnp
import numpy as np
from jax import lax
from jax.experimental import pallas as pl
from jax.experimental.pallas import tpu as pltpu

F32 = jnp.float32
BF16 = jnp.bfloat16
MESH = pl.DeviceIdType.MESH

D_MODEL = 1024
SEQ = 2048
HEAD_DIM = 64
SB_W = 512
DIL_W = 256
MEM_W = 512
MEM_LEN = 256
D_IN = 3 * SB_W + 9 * DIL_W + MEM_W
D_FF = 2816
DIL_D = (1, 4, 16)
ROPE_THETA = 10000.0
NORM_EPS = 1e-6
NEG_INF = -1e30
LANES = 128

ADAM_LR = 0.001
ADAM_B1 = 0.9
ADAM_B2 = 0.999
ADAM_EPS = 1e-08
ADAM_WD = 0.01
ADAM_STEP = 10

N_CHIPS = 4
PACK = (
    ("w_in", (1024, 1088), 1),
    ("w_mem_kv", (256, 1024), 0),
    ("w_br_sb", (512, 256), 1),
    ("w_br_dil", (256, 256), 1),
    ("w_br_mem", (512, 256), 1),
    ("w_gate", (1024, 768), 1),
    ("w_o", (256, 1024), 0),
    ("w_ffn_in", (1024, 1408), 1),
    ("w_ffn_out", (704, 1024), 0),
)
PACK_ROWS = sum(a * b for _, (a, b), _ in PACK) // D_MODEL
HALF_ROWS = PACK_ROWS // 2


def _dot(a, b):
    return lax.dot_general(a, b, (((1,), (0,)), ((), ())), preferred_element_type=F32)


def _dot_nt(a, b):
    return lax.dot_general(a, b, (((1,), (1,)), ((), ())), preferred_element_type=F32)


def _dot_tn(a, b):
    return lax.dot_general(a, b, (((0,), (0,)), ((), ())), preferred_element_type=F32)


def _split_dot(x, u):
    hi = x.astype(BF16)
    lo = (x - hi.astype(F32)).astype(BF16)
    return _dot(hi, u) + _dot(lo, u)


V7X_VMEM_BUDGET = 44 * 2 ** 20


def _rows_that_fit(m, row_bytes, fixed_bytes):
    for tm in (1024, 512, 256, 128):
        if m % tm == 0 and fixed_bytes + tm * row_bytes <= V7X_VMEM_BUDGET:
            return tm
    return min(m, 128)


def _mm(pairs, *, nt, tn, out_dtypes, name, bias=None, extras=(), epilogue=None, side=None):
    pairs = [p if len(p) == 4 else (p[0], p[1], None, None) for p in pairs]
    m = pairs[0][0].shape[-2]
    b0 = pairs[0][1]
    if nt:
        n = b0.shape[-2]
    else:
        n = b0.shape[-1] * (b0.shape[0] if b0.ndim == 3 else 1)
    n_pairs, n_extra, n_out = len(pairs), len(extras), len(out_dtypes)
    assert n % tn == 0
    one_col = n == tn
    ks = [(b.shape[-1] if nt else b.shape[-2]) for _, b, _, _ in pairs]
    fixed = sum(k * tn * 2 for k in ks) * (1 if one_col else 2)
    row_bytes = 2 * sum(k * 2 for k in ks) + 2 * tn * (sum(jnp.dtype(dt).itemsize for dt in out_dtypes) + 2 * n_extra) + 2 * tn * 4
    tm = _rows_that_fit(m, row_bytes, fixed)
    assert m % tm == 0
    b_mode = dict(pipeline_mode=pl.Buffered(1)) if one_col else {}
    has_bias = bias is not None
    side_arrays = side[1] if side else []
    n_side = len(side_arrays)
    n_main_in = 2 * n_pairs + has_bias + n_extra
    n_steps = (n // tn) * (m // tm)

    def body(*refs):
        if n_side:
            step = pl.program_id(0) * (m // tm) + pl.program_id(1)
            finish = _run_side(side, refs[n_main_in:n_main_in + n_side],
                               refs[n_main_in + n_side + n_out:n_main_in + 2 * n_side + n_out],
                               refs[n_main_in + 2 * n_side + n_out:], step, n_steps)
        outs = refs[n_main_in + n_side:n_main_in + n_side + n_out]
        acc = None
        for i in range(n_pairs):
            a, b = refs[2 * i][...], refs[2 * i + 1][...]
            p = _dot_nt(a, b) if nt else _dot(a, b)
            acc = p if acc is None else acc + p
        pos = 2 * n_pairs
        if has_bias:
            acc = acc + refs[pos][...]
            pos += 1
        ex = [r[...] for r in refs[pos:pos + n_extra]]
        vals = (acc,) if epilogue is None else epilogue(acc, *ex)
        for r, v, dt in zip(outs, vals, out_dtypes):
            r[...] = v.astype(dt)
        if n_side:
            finish()

    in_specs, args = [], []
    for a, b, a_col, b_sel in pairs:
        k = b.shape[-1] if nt else b.shape[-2]
        assert a_col is not None or a.shape[1] == k
        if a.ndim == 3:
            in_specs.append(pl.BlockSpec((None, tm, k), lambda j, i, c=a_col: (c, i, 0)))
        else:
            in_specs.append(pl.BlockSpec((tm, k), lambda j, i, c=a_col or 0: (i, c)))
        if b.ndim == 2:
            in_specs.append(pl.BlockSpec((tn, k), lambda j, i: (j, 0), **b_mode) if nt
                            else pl.BlockSpec((k, tn), lambda j, i: (0, j), **b_mode))
        elif nt:
            in_specs.append(pl.BlockSpec((None, tn, k), lambda j, i, s=b_sel: (s, j, 0), **b_mode))
        else:
            assert b_sel == "j" and b.shape[-1] == tn
            in_specs.append(pl.BlockSpec((None, k, tn), lambda j, i: (j, 0, 0), **b_mode))
        args += [a, b]
    if has_bias:
        in_specs.append(pl.BlockSpec((1, tn), lambda j, i: (0, j)))
        args.append(bias)
    for e in extras:
        in_specs.append(pl.BlockSpec((tm, tn), lambda j, i: (i, j)))
        args.append(e)
    out = pl.pallas_call(
        body,
        grid=(n // tn, m // tm),
        in_specs=in_specs + [ANY] * n_side,
        out_specs=[pl.BlockSpec((tm, tn), lambda j, i: (i, j)) for _ in range(n_out)] + [ANY] * n_side,
        out_shape=[jax.ShapeDtypeStruct((m, n), dt) for dt in out_dtypes] + (_side_out_shapes(side) if n_side else []),
        scratch_shapes=_side_sems(side) if n_side else [],
        compiler_params=pltpu.CompilerParams(dimension_semantics=("arbitrary", "arbitrary") if n_side else ("parallel", "parallel")),
        name=name,
    )(*args, *side_arrays)
    if n_side:
        return (out[0] if n_out == 1 else out[:n_out]), out[n_out:]
    return out[0] if n_out == 1 else out


def _mm_tn(a, b, *, tm, tn, name, out_shards=False, slots=None, into=None):
    k, m = a.shape
    b_shards = b.ndim == 3
    out_shards = out_shards or b_shards
    n = b.shape[0] * b.shape[2] if b_shards else b.shape[1]
    tk = _rows_that_fit(k, 2 * 2 * (tm + tn), 3 * tm * tn * 4)
    assert m % tm == 0 and n % tn == 0 and k % tk == 0 and (not b_shards or b.shape[2] == tn)
    total, first = slots if slots else (n // tn, 0)

    def body(a_ref, b_ref, *rest):
        o_ref = rest[-1]

        @pl.when(pl.program_id(2) == 0)
        def _():
            o_ref[...] = jnp.zeros_like(o_ref)

        o_ref[...] += _dot_tn(a_ref[...], b_ref[...])

    if b_shards:
        b_spec = pl.BlockSpec((None, tk, tn), lambda i, j, kk: (j, kk, 0))
    else:
        b_spec = pl.BlockSpec((tk, tn), lambda i, j, kk: (kk, j))
    if out_shards:
        out_spec = pl.BlockSpec((None, tm, tn), lambda i, j, kk: (j + first, i, 0))
        out_shape = jax.ShapeDtypeStruct((total, m, tn), F32)
    else:
        out_spec = pl.BlockSpec((tm, tn), lambda i, j, kk: (i, j))
        out_shape = jax.ShapeDtypeStruct((m, n), F32)
    return pl.pallas_call(
        body,
        grid=(m // tm, n // tn, k // tk),
        in_specs=[pl.BlockSpec((tk, tm), lambda i, j, kk: (kk, i)), b_spec] + ([ANY] if into is not None else []),
        out_specs=out_spec,
        out_shape=out_shape,
        input_output_aliases={2: 0} if into is not None else {},
        compiler_params=pltpu.CompilerParams(dimension_semantics=("parallel", "parallel", "arbitrary")),
        name=name,
    )(*([a, b] + ([into] if into is not None else [])))


def _rowwise(fn, ins, outs, *, tm, name, side=None):
    rows = next(a.shape[0] for a, kind in ins if kind == "row")
    tm = min(tm, rows)
    assert rows % tm == 0
    n_in, n_out = len(ins), len(outs)
    side_arrays = side[1] if side else []
    n_side = len(side_arrays)

    def body(*refs):
        if n_side:
            finish = _run_side(side, refs[n_in:n_in + n_side], refs[n_in + n_side + n_out:n_in + 2 * n_side + n_out],
                               refs[n_in + 2 * n_side + n_out:], pl.program_id(0), rows // tm)
        vals = fn(*[r[...] for r in refs[:n_in]])
        for (_, dt, kind), r, v in zip(outs, refs[n_in + n_side:n_in + n_side + n_out], vals):
            if kind == "row":
                r[...] = v.astype(dt)
            else:
                @pl.when(pl.program_id(0) == 0)
                def _(r=r):
                    r[...] = jnp.zeros_like(r)

                r[...] += v
        if n_side:
            finish()

    in_specs = [pl.BlockSpec((tm, a.shape[1]), lambda i: (i, 0)) if kind == "row" else pl.BlockSpec(a.shape, lambda i: (0, 0))
                for a, kind in ins]
    out_specs = [pl.BlockSpec((tm, c), lambda i: (i, 0)) if kind == "row" else pl.BlockSpec((1, c), lambda i: (0, 0))
                 for c, _, kind in outs]
    out_shape = [jax.ShapeDtypeStruct((rows if kind == "row" else 1, c), dt) for c, dt, kind in outs]
    ordered = n_side or any(kind == "acc" for _, _, kind in outs)
    return pl.pallas_call(
        body,
        grid=(rows // tm,),
        in_specs=in_specs + [ANY] * n_side,
        out_specs=out_specs + [ANY] * n_side,
        out_shape=out_shape + (_side_out_shapes(side) if n_side else []),
        scratch_shapes=_side_sems(side) if n_side else [],
        compiler_params=pltpu.CompilerParams(dimension_semantics=("arbitrary" if ordered else "parallel",)),
        name=name,
    )(*[a for a, _ in ins], *side_arrays)


def _rstd(x):
    return lax.rsqrt(jnp.mean(x * x, axis=-1, keepdims=True) + NORM_EPS)


def _norm_bwd(dout, xin, g):
    r = _rstd(xin)
    n = xin * r
    dn = dout * g
    dg = jnp.sum(dout * n, axis=0, keepdims=True)
    dx = r * (dn - n * jnp.mean(dn * n, axis=-1, keepdims=True))
    return dx, dg


def _sigmoid(x):
    return 0.5 * jnp.tanh(0.5 * x) + 0.5


def _norm_fwd(x, g, *, name, side=None):
    def fn(x, g):
        return ((x * _rstd(x)) * g,)

    out = _rowwise(fn, [(x, "row"), (g, "vec")], [(D_MODEL, BF16, "row")], tm=512, name=name, side=side)
    return (out[0], out[1:]) if side else out[0]


def _mid_fwd(mix, x, g_post_mix, g_pre_ffn):
    def fn(mix, x, g2, g3):
        x1 = x + (mix * _rstd(mix)) * g2
        return x1, (x1 * _rstd(x1)) * g3

    return _rowwise(fn, [(mix, "row"), (x, "row"), (g_post_mix, "vec"), (g_pre_ffn, "vec")],
                    [(D_MODEL, F32, "row"), (D_MODEL, BF16, "row")], tm=512, name="mid_fwd")


def _loss_bwd(f2, x1, g_post_ffn, tgt):
    def fn(f2, x1, g4, tgt):
        r = _rstd(f2)
        n = f2 * r
        err = x1 + n * g4 - tgt
        loss = 0.5 * jnp.sum(jnp.mean(err * err, axis=-1, keepdims=True), axis=0, keepdims=True)
        dy = err * (1.0 / D_MODEL)
        dn = dy * g4
        dg4 = jnp.sum(dy * n, axis=0, keepdims=True)
        df2 = r * (dn - n * jnp.mean(dn * n, axis=-1, keepdims=True))
        return dy, df2, dg4, jnp.broadcast_to(loss, (1, LANES))

    return _rowwise(fn, [(f2, "row"), (x1, "row"), (g_post_ffn, "vec"), (tgt, "row")],
                    [(D_MODEL, BF16, "row"), (D_MODEL, BF16, "row"), (D_MODEL, F32, "acc"), (LANES, F32, "acc")],
                    tm=512, name="loss_bwd")


def _mid_bwd(dh2, x1, mix, g_pre_ffn, g_post_mix, dy):
    def fn(dh2, x1, mix, g3, g2, dy):
        d3, dg3 = _norm_bwd(dh2.astype(F32), x1, g3)
        dx1 = dy.astype(F32) + d3
        dmix, dg2 = _norm_bwd(dx1, mix, g2)
        return dx1, dmix, dg3, dg2

    return _rowwise(fn, [(dh2, "row"), (x1, "row"), (mix, "row"), (g_pre_ffn, "vec"), (g_post_mix, "vec"), (dy, "row")],
                    [(D_MODEL, BF16, "row"), (D_MODEL, BF16, "row"), (D_MODEL, F32, "acc"), (D_MODEL, F32, "acc")],
                    tm=256, name="mid_bwd")


def _first_bwd(dh, x, g_pre_mix, dx1):
    def fn(dh, x, g1, dx1):
        d1, dg1 = _norm_bwd(dh.astype(F32), x, g1)
        return dx1.astype(F32) + d1, dg1

    return _rowwise(fn, [(dh, "row"), (x, "row"), (g_pre_mix, "vec"), (dx1, "row")],
                    [(D_MODEL, F32, "row"), (D_MODEL, F32, "acc")], tm=512, name="first_bwd")


def _mem_norm_bwd(dhm, mem, g_mem):
    def fn(dhm, mem, g):
        return (jnp.sum(dhm * (mem * _rstd(mem)), axis=0, keepdims=True),)

    return _rowwise(fn, [(dhm, "row"), (mem, "row"), (g_mem, "vec")], [(D_MODEL, F32, "acc")], tm=512, name="mem_norm_bwd")[0]


def _gate_bwd(dmerged, gates, ya, yb, yc):
    def fn(dm, gt, ya, yb, yc):
        dm = dm.astype(F32)
        gt = gt.astype(F32)
        outs, dgp = [], []
        for i, y in enumerate((ya, yb, yc)):
            gi = gt[:, i * D_MODEL:(i + 1) * D_MODEL]
            outs.append(dm * gi)
            dgp.append(dm * y.astype(F32) * gi * (1.0 - gi))
        dgpre = jnp.concatenate(dgp, axis=1)
        return outs[0], outs[1], outs[2], dgpre, jnp.sum(dgpre, axis=0, keepdims=True)

    return _rowwise(fn, [(dmerged, "row"), (gates, "row"), (ya, "row"), (yb, "row"), (yc, "row")],
                    [(D_MODEL, BF16, "row")] * 3 + [(3 * D_MODEL, BF16, "row"), (3 * D_MODEL, F32, "acc")],
                    tm=256, name="gate_bwd")


def _adamw_math(w, g, m, v):
    m = ADAM_B1 * m + (1.0 - ADAM_B1) * g
    v = ADAM_B2 * v + (1.0 - ADAM_B2) * (g * g)
    m_hat = m / (1.0 - ADAM_B1 ** ADAM_STEP)
    v_hat = v / (1.0 - ADAM_B2 ** ADAM_STEP)
    delta = -ADAM_LR * (m_hat / (jnp.sqrt(v_hat) + ADAM_EPS) + ADAM_WD * w)
    return delta, m, v


def _adamw(w, g, m, v, *, tm, name):
    c = w.shape[1]
    return _rowwise(_adamw_math, [(w, "row"), (g, "row"), (m, "row"), (v, "row")], [(c, F32, "row")] * 3, tm=tm, name=name)


def _adamw_halves(w, g_mine, g_theirs, m, v, c_arr, *, name):
    a, b = w.shape
    hf = a // 2
    tr = hf // 4

    def body(c_ref, w_ref, gm_ref, gt_ref, m_ref, v_ref, g_out, d_out, m_out, v_out):
        g = jnp.where(pl.program_id(0) == c_ref[0], gm_ref[...], gt_ref[...])
        d, m_new, v_new = _adamw_math(w_ref[...], g, m_ref[...], v_ref[...])
        g_out[...] = g
        d_out[...] = d
        m_out[...] = m_new
        v_out[...] = v_new

    full = pl.BlockSpec((tr, b), lambda hh, i, c_ref: (hh * (hf // tr) + i, 0))
    half = pl.BlockSpec((tr, b), lambda hh, i, c_ref: (i, 0))
    return pl.pallas_call(
        body,
        grid_spec=pltpu.PrefetchScalarGridSpec(
            num_scalar_prefetch=1,
            grid=(2, hf // tr),
            in_specs=[full, half, half, full, full],
            out_specs=[full] * 4,
        ),
        out_shape=[jax.ShapeDtypeStruct((a, b), F32)] * 4,
        compiler_params=pltpu.CompilerParams(dimension_semantics=("parallel", "parallel")),
        name=name,
    )(c_arr, w, g_mine, g_theirs, m, v)


def _ffn_in_fwd(h2, w_ffn):
    m, tm, tn = h2.shape[0], 512, w_ffn.shape[2]
    assert 2 * tn == D_FF

    def body(h_ref, wg_ref, wu_ref, g_ref, u_ref, f_ref):
        h = h_ref[...]
        g = _dot(h, wg_ref[...])
        u = _dot(h, wu_ref[...])
        g_ref[...] = g.astype(BF16)
        u_ref[...] = u.astype(BF16)
        f_ref[...] = (g * _sigmoid(g) * u).astype(BF16)

    o_spec = pl.BlockSpec((tm, tn), lambda j, i: (i, j))
    return pl.pallas_call(
        body,
        grid=(D_FF // tn, m // tm),
        in_specs=[pl.BlockSpec((tm, D_MODEL), lambda j, i: (i, 0)),
                  pl.BlockSpec((None, D_MODEL, tn), lambda j, i: (j, 0, 0)),
                  pl.BlockSpec((None, D_MODEL, tn), lambda j, i: (j + 2, 0, 0))],
        out_specs=[o_spec, o_spec, o_spec],
        out_shape=[jax.ShapeDtypeStruct((m, D_FF), BF16)] * 3,
        compiler_params=pltpu.CompilerParams(dimension_semantics=("parallel", "parallel")),
        name="ffn_in_fwd",
    )(h2, w_ffn, w_ffn)


def _join_shards(w4):
    _, rows, cols = w4.shape
    tr = rows // 4

    def body(w_ref, o_ref):
        for s in range(N_CHIPS):
            o_ref[:, s * cols:(s + 1) * cols] = w_ref[s]

    return pl.pallas_call(
        body,
        grid=(rows // tr,),
        in_specs=[pl.BlockSpec((N_CHIPS, tr, cols), lambda i: (0, i, 0))],
        out_specs=pl.BlockSpec((tr, N_CHIPS * cols), lambda i: (i, 0)),
        out_shape=jax.ShapeDtypeStruct((rows, N_CHIPS * cols), w4.dtype),
        compiler_params=pltpu.CompilerParams(dimension_semantics=("parallel",)),
        name="join_shards",
    )(w4)


def _split_to_shards(pieces, *, name):
    t = pieces[0].shape[0]
    widths = [p.shape[1] for p in pieces]
    cols = sum(widths) // N_CHIPS
    tm = 512
    plan, start = [], 0
    for p, wd in enumerate(widths):
        for s in range(N_CHIPS):
            lo, hi = max(start, s * cols), min(start + wd, (s + 1) * cols)
            if lo < hi:
                plan.append((s, p, lo - s * cols, hi - s * cols, lo - start, hi - start))
        start += wd

    def body(*refs):
        o_ref = refs[-1]
        for s, p, o_lo, o_hi, p_lo, p_hi in plan:
            o_ref[s, :, o_lo:o_hi] = refs[p][:, p_lo:p_hi]

    return pl.pallas_call(
        body,
        grid=(t // tm,),
        in_specs=[pl.BlockSpec((tm, wd), lambda i: (i, 0)) for wd in widths],
        out_specs=pl.BlockSpec((N_CHIPS, tm, cols), lambda i: (0, i, 0)),
        out_shape=jax.ShapeDtypeStruct((N_CHIPS, t, cols), pieces[0].dtype),
        compiler_params=pltpu.CompilerParams(dimension_semantics=("parallel",)),
        name=name,
    )(*pieces)


def _swiglu_bwd_epilogue(df, g, u):
    g = g.astype(F32)
    u = u.astype(F32)
    sg = _sigmoid(g)
    return df * u * (sg * (1.0 + g * (1.0 - sg))), df * (g * sg)


def _branch_merge_fwd(o_a, o_b, o_c, w_sb, w_dil, w_mem, gates):
    m, tm = o_a.shape[0], 256

    def body(oa_ref, ob_ref, oc_ref, wa_ref, wb_ref, wc_ref, gt_ref, ya_ref, yb_ref, yc_ref, mg_ref):
        def project(o_ref, w_ref):
            o = o_ref[...]
            return jnp.concatenate([_dot(o, w_ref[s]) for s in range(N_CHIPS)], axis=1)

        ya = project(oa_ref, wa_ref)
        yb = project(ob_ref, wb_ref)
        yc = project(oc_ref, wc_ref)
        gt = gt_ref[...].astype(F32)
        ya_ref[...] = ya.astype(BF16)
        yb_ref[...] = yb.astype(BF16)
        yc_ref[...] = yc.astype(BF16)
        mg_ref[...] = (gt[:, :D_MODEL] * ya + gt[:, D_MODEL:2 * D_MODEL] * yb + gt[:, 2 * D_MODEL:] * yc).astype(BF16)

    row = lambda c: pl.BlockSpec((tm, c), lambda i: (i, 0))
    full = lambda a: pl.BlockSpec(a.shape, lambda i: (0, 0, 0))
    return pl.pallas_call(
        body,
        grid=(m // tm,),
        in_specs=[row(SB_W), row(DIL_W), row(MEM_W), full(w_sb), full(w_dil), full(w_mem), row(3 * D_MODEL)],
        out_specs=[row(D_MODEL)] * 4,
        out_shape=[jax.ShapeDtypeStruct((m, D_MODEL), BF16)] * 4,
        compiler_params=pltpu.CompilerParams(dimension_semantics=("parallel",)),
        name="branch_merge_fwd",
    )(o_a, o_b, o_c, w_sb, w_dil, w_mem, gates)


SB_T = 256
SB_SCALE = HEAD_DIM ** -0.5


def _sb_masks():
    row = lax.broadcasted_iota(jnp.int32, (SB_T, SB_T), 0)
    col = lax.broadcasted_iota(jnp.int32, (SB_T, SB_T), 1)
    lane = lax.broadcasted_iota(jnp.int32, (1, LANES), 1)
    return row, col, lane


def _sb_logs(z):
    lb = jnp.minimum(z, 0.0) - jnp.log(1.0 + jnp.exp(-jnp.abs(z)))
    return lb, lb - z


def _sb_specs(n_heads_pairs, col0):
    q = pl.BlockSpec((None, SB_T, LANES), lambda b, p, i: (b, i, col0 + p))
    k = pl.BlockSpec((None, SEQ, LANES), lambda b, p, i: (b, 0, col0 + n_heads_pairs + p))
    v = pl.BlockSpec((None, SEQ, LANES), lambda b, p, i: (b, 0, col0 + 2 * n_heads_pairs + p))
    return q, k, v


def _grid_step(n_pairs, nq):
    return (pl.program_id(0) * n_pairs + pl.program_id(1)) * nq + pl.program_id(2)


def _sb_fwd(proj3, late_shards):
    bl = proj3.shape[0]
    n_pairs = SB_W // LANES
    nq = SEQ // SB_T
    n_late = len(late_shards)
    n_steps = bl * n_pairs * nq

    def body(q_ref, k_ref, v_ref, *rest):
        late_in, (o_ref, o32_ref, w_ref), late_out = rest[:n_late], rest[n_late:n_late + 3], rest[n_late + 3:2 * n_late + 3]
        step = _grid_step(n_pairs, nq)
        if n_late:
            send, forward, finish = _gather_phases(late_in, late_out, *rest[2 * n_late + 3:])
            pl.when(step == 0)(send)
            pl.when(step == n_steps // 2)(forward)
        i = pl.program_id(2)
        row, col, lane = _sb_masks()
        causal = col < row
        u_excl = (row > col).astype(BF16)
        q = q_ref[...]
        heads = []
        for h in range(2):
            mh = (lane // HEAD_DIM) == h
            heads.append((mh, jnp.where(mh, q, jnp.zeros_like(q)) * SB_SCALE))

        def blocks(js, diags, carries, acc):
            ks = [k_ref[pl.ds(pl.multiple_of(j * SB_T, SB_T), SB_T), :] for j in js]
            vs = [v_ref[pl.ds(pl.multiple_of(j * SB_T, SB_T), SB_T), :] for j in js]
            chains = [(b, h) for b in range(len(js)) for h in range(2)]
            z = {c: _dot_nt(heads[c[1]][1], ks[c[0]]) for c in chains}
            lb, lk = {}, {}
            for c in chains:
                lb[c], lk[c] = _sb_logs(z[c])
                if diags[c[0]]:
                    lk[c] = jnp.where(causal, lk[c], 0.0)
            r = {c: _split_dot(lk[c], u_excl) for c in chains}
            carries = list(carries)
            w = {}
            for b, h in chains:
                w_c = jnp.exp(lb[b, h] + r[b, h] + carries[h])
                w[b, h] = (jnp.where(causal, w_c, 0.0) if diags[b] else w_c).astype(BF16)
                w_ref[h, js[b]] = w[b, h]
                carries[h] = carries[h] + (r[b, h][:, :1] + lk[b, h][:, :1])
            for b, h in chains:
                acc = acc + _dot(w[b, h], jnp.where(heads[h][0], vs[b], jnp.zeros_like(vs[b])))
            return tuple(carries), acc

        zero = jnp.zeros((SB_T, 1), F32)
        init = ((zero, zero), jnp.zeros((SB_T, LANES), F32))
        odd = i % 2
        carries, acc = lax.cond(odd == 1, lambda: blocks([i, i - 1], (True, False), *init), lambda: blocks([i], (True,), *init))
        rest = i - 1 - odd
        carries, acc = lax.fori_loop(
            0, i // 2, lambda jj, c: blocks([rest - 2 * jj, rest - 1 - 2 * jj], (False, False), c[0], c[1]), (carries, acc))
        o_ref[...] = acc.astype(BF16)
        o32_ref[...] = acc
        if n_late:
            pl.when(step == n_steps - 1)(finish)

    q_spec, k_spec, v_spec = _sb_specs(n_pairs, 0)
    blk = pl.BlockSpec((None, SB_T, LANES), lambda b, p, i: (b, i, p))
    out = pl.pallas_call(
        body,
        grid=(bl, n_pairs, nq),
        in_specs=[q_spec, k_spec, v_spec] + [ANY] * n_late,
        out_specs=[blk, blk, _sb_weight_spec(nq)] + [ANY] * n_late,
        out_shape=[jax.ShapeDtypeStruct((bl, SEQ, SB_W), BF16), jax.ShapeDtypeStruct((bl, SEQ, SB_W), F32),
                   jax.ShapeDtypeStruct((bl, n_pairs, nq, 2, nq, SB_T, SB_T), BF16)] + _gather_out_shapes(late_shards),
        scratch_shapes=_gather_sems(n_late) if n_late else [],
        compiler_params=pltpu.CompilerParams(dimension_semantics=("arbitrary", "arbitrary", "arbitrary")),
        name="sb_fwd",
    )(proj3, proj3, proj3, *late_shards)
    return out[0], out[1], out[2], out[3:]


def _sb_weight_spec(nq):
    return pl.BlockSpec((None, None, None, 2, nq, SB_T, SB_T), lambda b, p, i: (b, p, i, 0, 0, 0, 0))


def _sb_bwd(proj3, o_a, do_a, w_all, parts):
    bl = proj3.shape[0]
    n_pairs = SB_W // LANES
    nq = SEQ // SB_T
    n_parts = len(parts)
    n_steps = bl * n_pairs * nq

    def body(q_ref, k_ref, v_ref, o_ref, do_ref, w_ref, *rest):
        p_refs, (dq_ref, dk_ref, dv_ref), land_refs = rest[:n_parts], rest[n_parts:n_parts + 3], rest[n_parts + 3:2 * n_parts + 3]
        dk_acc, dv_acc = rest[2 * n_parts + 3:2 * n_parts + 5]
        step = _grid_step(n_pairs, nq)
        if n_parts:
            send, finish = _chip_exchange_phases(p_refs, land_refs, *rest[2 * n_parts + 5:])
            pl.when(step == 0)(send)
        i = pl.program_id(2)

        @pl.when(i == 0)
        def _():
            dk_acc[...] = jnp.zeros_like(dk_acc)
            dv_acc[...] = jnp.zeros_like(dv_acc)

        row, col, lane = _sb_masks()
        causal = col < row
        u_incl = (row >= col).astype(BF16)
        q = q_ref[...]
        do = do_ref[...]
        prod = do.astype(F32) * o_ref[...]
        heads = []
        for h in range(2):
            mh = (lane // HEAD_DIM) == h
            d_tot = jnp.sum(jnp.where(mh, prod, 0.0), axis=1, keepdims=True)
            heads.append((mh, jnp.where(mh, q, jnp.zeros_like(q)) * SB_SCALE, jnp.where(mh, do, jnp.zeros_like(do)), d_tot))

        def blocks(js, diags, c_das, dq):
            starts = [pl.multiple_of(j * SB_T, SB_T) for j in js]
            ks = [k_ref[pl.ds(s, SB_T), :] for s in starts]
            vs = [v_ref[pl.ds(s, SB_T), :] for s in starts]
            chains = [(b, h) for b in range(len(js)) for h in range(2)]
            z = {c: _dot_nt(heads[c[1]][1], ks[c[0]]) for c in chains}
            dw = {c: _dot_nt(heads[c[1]][2], vs[c[0]]) for c in chains}
            wb = {(b, h): w_ref[h, js[b]] for b, h in chains}
            da = {c: dw[c] * wb[c].astype(F32) for c in chains}
            sfx = {c: _split_dot(da[c], u_incl) for c in chains}
            c_das = list(c_das)
            dz = {}
            for b, h in chains:
                dlk = heads[h][3] - c_das[h] - sfx[b, h]
                if diags[b]:
                    dlk = jnp.where(causal, dlk, 0.0)
                c_das[h] = c_das[h] + sfx[b, h][:, :1]
                e = jnp.exp(-jnp.abs(z[b, h]))
                inv = 1.0 / (1.0 + e)
                pos = z[b, h] >= 0.0
                beta = jnp.where(pos, inv, e * inv)
                one_m_beta = jnp.where(pos, e * inv, inv)
                dz[b, h] = (da[b, h] * one_m_beta - dlk * beta).astype(BF16)
            for b, h in chains:
                dq = dq + _dot(dz[b, h], jnp.where(heads[h][0], ks[b], jnp.zeros_like(ks[b])))
            for b in range(len(js)):
                dk_acc[pl.ds(starts[b], SB_T), :] += _dot_tn(dz[b, 0], heads[0][1]) + _dot_tn(dz[b, 1], heads[1][1])
                dv_acc[pl.ds(starts[b], SB_T), :] += _dot_tn(wb[b, 0], heads[0][2]) + _dot_tn(wb[b, 1], heads[1][2])
            return tuple(c_das), dq

        zero = jnp.zeros((SB_T, 1), F32)
        init = ((zero, zero), jnp.zeros((SB_T, LANES), F32))
        odd = i % 2
        state = lax.cond(odd == 1, lambda: blocks([i, i - 1], (True, False), *init), lambda: blocks([i], (True,), *init))
        rest = i - 1 - odd
        state = lax.fori_loop(0, i // 2, lambda jj, c: blocks([rest - 2 * jj, rest - 1 - 2 * jj], (False, False), c[0], c[1]), state)
        dq_ref[...] = (state[1] * SB_SCALE).astype(BF16)

        @pl.when(i == nq - 1)
        def _():
            dk_ref[...] = dk_acc[...].astype(BF16)
            dv_ref[...] = dv_acc[...].astype(BF16)

        if n_parts:
            pl.when(step == n_steps - 1)(finish)

    q_spec, k_spec, v_spec = _sb_specs(n_pairs, 0)
    blk = pl.BlockSpec((None, SB_T, LANES), lambda b, p, i: (b, i, p))
    seq = pl.BlockSpec((None, SEQ, LANES), lambda b, p, i: (b, 0, p))
    shape = jax.ShapeDtypeStruct((bl, SEQ, SB_W), BF16)
    out = pl.pallas_call(
        body,
        grid=(bl, n_pairs, nq),
        in_specs=[q_spec, k_spec, v_spec, blk, blk, _sb_weight_spec(nq)] + [ANY] * n_parts,
        out_specs=[blk, seq, seq] + [ANY] * n_parts,
        out_shape=[shape, shape, shape] + [jax.ShapeDtypeStruct(p.shape, p.dtype) for p in parts],
        scratch_shapes=[pltpu.VMEM((SEQ, LANES), F32), pltpu.VMEM((SEQ, LANES), F32)]
        + (_chip_exchange_sems(n_parts) if n_parts else []),
        compiler_params=pltpu.CompilerParams(dimension_semantics=("arbitrary", "arbitrary", "arbitrary")),
        name="sb_bwd",
    )(proj3, proj3, proj3, o_a, do_a, w_all, *parts)
    return out[0], out[1], out[2], out[3:]


BAND = 128


BAND_CH = 4
BAND_HEADS = DIL_W // HEAD_DIM


def _swap_half(x):
    n = x.shape[-1]
    lane = lax.broadcasted_iota(jnp.int32, (1, n), 1)
    return jnp.where((lane % HEAD_DIM) < HEAD_DIM // 2, pltpu.roll(x, n - HEAD_DIM // 2, 1), pltpu.roll(x, HEAD_DIM // 2, 1))


def _rope(x, cos, sin_signed):
    x = x.astype(F32)
    return x * cos + _swap_half(x) * sin_signed


def _band_valid(g, blk):
    nb = jnp.where(g == 0, 16, jnp.where(g == 1, 4, 1))
    first_key = jnp.where(lax.rem(blk, nb) != 0, 0, BAND)
    qi = lax.broadcasted_iota(jnp.int32, (BAND, 2 * BAND), 0) + BAND
    kj = lax.broadcasted_iota(jnp.int32, (BAND, 2 * BAND), 1)
    dist = qi - kj
    return (dist >= 0) & (dist <= BAND) & (kj >= first_key)


def _band_specs():
    last_before = lambda i: jnp.maximum(i * BAND_CH - 1, 0)
    cur = lambda col: pl.BlockSpec((None, BAND_CH, BAND, DIL_W), lambda g, i: (g, i, 0, col))
    prev = lambda col: pl.BlockSpec((None, None, BAND, DIL_W), lambda g, i: (g, last_before(i), 0, col))
    tab = pl.BlockSpec((None, BAND_CH, BAND, DIL_W), lambda g, i: (g, lax.rem(i, 16 // BAND_CH), 0, 0))
    tab_prev = pl.BlockSpec((None, None, BAND, DIL_W), lambda g, i: (g, lax.rem(last_before(i), 16), 0, 0))
    return cur, prev, tab, tab_prev


def _band_load(q_ref, k_ref, kp_ref, v_ref, vp_ref, c_ref, s_ref, cp_ref, sp_ref):
    qs = [(_rope(q_ref[b], c_ref[b], s_ref[b]) * SB_SCALE).astype(BF16) for b in range(BAND_CH)]
    ks = [_rope(kp_ref[...], cp_ref[...], sp_ref[...]).astype(BF16)]
    ks += [_rope(k_ref[b], c_ref[b], s_ref[b]).astype(BF16) for b in range(BAND_CH)]
    vs = [vp_ref[...]] + [v_ref[b] for b in range(BAND_CH)]
    k2 = [jnp.concatenate([ks[b], ks[b + 1]], axis=0) for b in range(BAND_CH)]
    v2 = [jnp.concatenate([vs[b], vs[b + 1]], axis=0) for b in range(BAND_CH)]
    return qs, k2, v2


def _band_fwd(qkv_s, cos_t, sin_t):
    def body(q_ref, k_ref, kp_ref, v_ref, vp_ref, c_ref, s_ref, cp_ref, sp_ref, ol_ref):
        g, i = pl.program_id(0), pl.program_id(1)
        qs, k2, v2 = _band_load(q_ref, k_ref, kp_ref, v_ref, vp_ref, c_ref, s_ref, cp_ref, sp_ref)
        lane = lax.broadcasted_iota(jnp.int32, (1, DIL_W), 1)
        for b in range(BAND_CH):
            valid = _band_valid(g, i * BAND_CH + b)
            hs = range(BAND_HEADS)
            mh = [(lane // HEAD_DIM) == h for h in hs]
            s = [jnp.where(valid, _dot_nt(jnp.where(mh[h], qs[b], jnp.zeros_like(qs[b])), k2[b]), NEG_INF) for h in hs]
            m = [jnp.max(s[h], axis=1, keepdims=True) for h in hs]
            p = [jnp.exp(s[h] - m[h]) for h in hs]
            den = [jnp.sum(p[h], axis=1, keepdims=True) for h in hs]
            pv = [_dot(p[h].astype(BF16), jnp.where(mh[h], v2[b], jnp.zeros_like(v2[b]))) for h in hs]
            o = jnp.zeros((BAND, DIL_W), F32)
            lse = jnp.zeros((BAND, DIL_W), F32)
            for h in hs:
                o = o + pv[h] * (1.0 / den[h])
                lse = jnp.where(mh[h], m[h] + jnp.log(den[h]), lse)
            ol_ref[b, :, :DIL_W] = o
            ol_ref[b, :, DIL_W:] = lse

    cur, prev, tab, tab_prev = _band_specs()
    n_blk = qkv_s.shape[1]
    return pl.pallas_call(
        body,
        grid=(3, n_blk // BAND_CH),
        in_specs=[cur(0), cur(1), prev(1), cur(2), prev(2), tab, tab, tab_prev, tab_prev],
        out_specs=pl.BlockSpec((None, BAND_CH, BAND, 2 * DIL_W), lambda g, i: (g, i, 0, 0)),
        out_shape=jax.ShapeDtypeStruct((3, n_blk, BAND, 2 * DIL_W), F32),
        compiler_params=pltpu.CompilerParams(dimension_semantics=("parallel", "parallel")),
        name="band_fwd",
    )(qkv_s, qkv_s, qkv_s, qkv_s, qkv_s, cos_t, sin_t, cos_t, sin_t)


def _band_bwd(qkv_s, cos_t, sin_t, dcat_s):
    def body(q_ref, k_ref, kp_ref, v_ref, vp_ref, c_ref, s_ref, cp_ref, sp_ref, do_ref, lse_ref, dl_ref,
             dq_ref, dk_ref, dv_ref, dkf_ref, dvf_ref):
        g, i = pl.program_id(0), pl.program_id(1)
        qs, k2, v2 = _band_load(q_ref, k_ref, kp_ref, v_ref, vp_ref, c_ref, s_ref, cp_ref, sp_ref)
        lane = lax.broadcasted_iota(jnp.int32, (1, DIL_W), 1)
        dks, dvs = [], []
        for b in range(BAND_CH):
            valid = _band_valid(g, i * BAND_CH + b)
            do, lse, dl = do_ref[b].astype(BF16), lse_ref[b], dl_ref[b]
            hs = range(BAND_HEADS)
            mh = [(lane // HEAD_DIM) == h for h in hs]
            qh = [jnp.where(mh[h], qs[b], jnp.zeros_like(qs[b])) for h in hs]
            doh = [jnp.where(mh[h], do, jnp.zeros_like(do)) for h in hs]
            s = [_dot_nt(qh[h], k2[b]) for h in hs]
            dp = [_dot_nt(doh[h], v2[b]) for h in hs]
            p = [jnp.where(valid, jnp.exp(s[h] - lse[:, h * HEAD_DIM:h * HEAD_DIM + 1]), 0.0) for h in hs]
            ds = [(p[h] * (dp[h] - dl[:, h * HEAD_DIM:h * HEAD_DIM + 1])).astype(BF16) for h in hs]
            pb = [p[h].astype(BF16) for h in hs]
            dq = sum(_dot(ds[h], jnp.where(mh[h], k2[b], jnp.zeros_like(k2[b]))) for h in hs)
            dk2 = sum(_dot_tn(ds[h], qh[h]) for h in hs)
            dv2 = sum(_dot_tn(pb[h], doh[h]) for h in hs)
            dq_ref[b] = dq * SB_SCALE
            dks.append(dk2)
            dvs.append(dv2)
        dkf_ref[...] = dks[0][:BAND]
        dvf_ref[...] = dvs[0][:BAND]
        for b in range(BAND_CH):
            last = b == BAND_CH - 1
            dk_ref[b] = dks[b][BAND:] if last else dks[b][BAND:] + dks[b + 1][:BAND]
            dv_ref[b] = dvs[b][BAND:] if last else dvs[b][BAND:] + dvs[b + 1][:BAND]

    cur, prev, tab, tab_prev = _band_specs()
    first = pl.BlockSpec((None, None, BAND, DIL_W), lambda g, i: (g, i, 0, 0))
    n_blk = qkv_s.shape[1]
    n_chunks = n_blk // BAND_CH
    shape = jax.ShapeDtypeStruct((3, n_blk, BAND, DIL_W), F32)
    shape_first = jax.ShapeDtypeStruct((3, n_chunks, BAND, DIL_W), F32)
    return pl.pallas_call(
        body,
        grid=(3, n_chunks),
        in_specs=[cur(0), cur(1), prev(1), cur(2), prev(2), tab, tab, tab_prev, tab_prev, cur(0), cur(1), cur(2)],
        out_specs=[cur(0), cur(0), cur(0), first, first],
        out_shape=[shape, shape, shape, shape_first, shape_first],
        compiler_params=pltpu.CompilerParams(dimension_semantics=("parallel", "parallel")),
        name="band_bwd",
    )(qkv_s, qkv_s, qkv_s, qkv_s, qkv_s, cos_t, sin_t, cos_t, sin_t, dcat_s, dcat_s, dcat_s)


def _band_combine(dq, dk, dv, dk_first, dv_first, cos_t, sin_t):
    n_chunks = dk_first.shape[1]

    def body(dq_ref, dk_ref, dkn_ref, dv_ref, dvn_ref, c_ref, s_ref, out_ref):
        nxt = (pl.program_id(1) < n_chunks - 1).astype(F32)
        for b in range(BAND_CH):
            cos, sin = c_ref[b], s_ref[b]
            dq_b, dk_b, dv_b = dq_ref[b], dk_ref[b], dv_ref[b]
            if b == BAND_CH - 1:
                dk_b = dk_b + nxt * dkn_ref[...]
                dv_b = dv_b + nxt * dvn_ref[...]
            out_ref[b, :, :DIL_W] = (dq_b * cos - _swap_half(dq_b) * sin).astype(BF16)
            out_ref[b, :, DIL_W:2 * DIL_W] = (dk_b * cos - _swap_half(dk_b) * sin).astype(BF16)
            out_ref[b, :, 2 * DIL_W:] = dv_b.astype(BF16)

    cur, _, tab, _ = _band_specs()
    nxt = pl.BlockSpec((None, None, BAND, DIL_W), lambda g, i: (g, jnp.minimum(i + 1, n_chunks - 1), 0, 0))
    return pl.pallas_call(
        body,
        grid=(3, n_chunks),
        in_specs=[cur(0), cur(0), nxt, cur(0), nxt, tab, tab],
        out_specs=pl.BlockSpec((None, BAND_CH, BAND, 3 * DIL_W), lambda g, i: (g, i, 0, 0)),
        out_shape=jax.ShapeDtypeStruct(dq.shape[:3] + (3 * DIL_W,), BF16),
        compiler_params=pltpu.CompilerParams(dimension_semantics=("parallel", "parallel")),
        name="band_combine",
    )(dq, dk, dk_first, dv, dv_first, cos_t, sin_t)


def _band_merge(ol):
    t, tm = ol.shape[1], 512

    def body(o_ref, l_ref, ob_ref, lse_ref):
        l0, l1, l2 = l_ref[0], l_ref[1], l_ref[2]
        m = jnp.maximum(jnp.maximum(l0, l1), l2)
        lse = m + jnp.log(jnp.exp(l0 - m) + jnp.exp(l1 - m) + jnp.exp(l2 - m))
        ob_ref[...] = (jnp.exp(l0 - lse) * o_ref[0] + jnp.exp(l1 - lse) * o_ref[1] + jnp.exp(l2 - lse) * o_ref[2]).astype(BF16)
        lse_ref[...] = lse

    spec = pl.BlockSpec((tm, DIL_W), lambda i: (i, 0))
    return pl.pallas_call(
        body,
        grid=(t // tm,),
        in_specs=[pl.BlockSpec((3, tm, DIL_W), lambda i: (0, i, 0)), pl.BlockSpec((3, tm, DIL_W), lambda i: (0, i, 1))],
        out_specs=[spec, spec],
        out_shape=[jax.ShapeDtypeStruct((t, DIL_W), BF16), jax.ShapeDtypeStruct((t, DIL_W), F32)],
        compiler_params=pltpu.CompilerParams(dimension_semantics=("parallel",)),
        name="band_merge",
    )(ol, ol)


def _band_delta(do_b, o_b, lse_b):
    def fn(do, o, lse):
        lane_in = lax.broadcasted_iota(jnp.int32, (DIL_W, LANES), 0)
        col = lax.broadcasted_iota(jnp.int32, (DIL_W, LANES), 1)
        sum_head = ((lane_in // HEAD_DIM == col - BAND_HEADS) & (col >= BAND_HEADS) & (col < 2 * BAND_HEADS)).astype(BF16)
        return (lse + _split_dot(do.astype(F32) * o.astype(F32), sum_head),)

    return _rowwise(fn, [(do_b, "row"), (o_b, "row"), (lse_b, "row")], [(LANES, F32, "row")], tm=512, name="band_delta")[0]


def _band_masks():
    qi = lax.broadcasted_iota(jnp.int32, (BAND, 2 * BAND), 0) + BAND
    kj = lax.broadcasted_iota(jnp.int32, (BAND, 2 * BAND), 1)
    dist = qi - kj
    row = lax.broadcasted_iota(jnp.int32, (BAND, BAND), 0)
    col = lax.broadcasted_iota(jnp.int32, (BAND, BAND), 1)
    return col <= row, (dist >= 0) & (dist <= BAND)


def _band_attend(q, k, v, valid):
    lane = lax.broadcasted_iota(jnp.int32, (1, DIL_W), 1)
    stat_lane = lax.broadcasted_iota(jnp.int32, (1, LANES), 1)
    hs = range(BAND_HEADS)
    mh = [(lane // HEAD_DIM) == h for h in hs]
    s = [jnp.where(valid, _dot_nt(jnp.where(mh[h], q, jnp.zeros_like(q)), k), NEG_INF) for h in hs]
    m = [jnp.max(s[h], axis=1, keepdims=True) for h in hs]
    p = [jnp.exp(s[h] - m[h]) for h in hs]
    den = [jnp.sum(p[h], axis=1, keepdims=True) for h in hs]
    pv = [_dot(p[h].astype(BF16), jnp.where(mh[h], v, jnp.zeros_like(v))) for h in hs]
    o = jnp.zeros((BAND, DIL_W), F32)
    lse = jnp.zeros((BAND, LANES), F32)
    for h in hs:
        o = o + pv[h] * (1.0 / den[h])
        lse = jnp.where(stat_lane == h, m[h] + jnp.log(den[h]), lse)
    return o, lse


def _band_attend_bwd(q, k, v, valid, do, st):
    lane = lax.broadcasted_iota(jnp.int32, (1, DIL_W), 1)
    hs = range(BAND_HEADS)
    mh = [(lane // HEAD_DIM) == h for h in hs]
    qh = [jnp.where(mh[h], q, jnp.zeros_like(q)) for h in hs]
    doh = [jnp.where(mh[h], do, jnp.zeros_like(do)) for h in hs]
    s = [_dot_nt(qh[h], k) for h in hs]
    dp = [_dot_nt(doh[h], v) for h in hs]
    p = [jnp.where(valid, jnp.exp(s[h] - st[:, h:h + 1]), 0.0) for h in hs]
    ds = [(p[h] * (dp[h] - st[:, BAND_HEADS + h:BAND_HEADS + h + 1])).astype(BF16) for h in hs]
    pb = [p[h].astype(BF16) for h in hs]
    dq = sum(_dot(ds[h], jnp.where(mh[h], k, jnp.zeros_like(k))) for h in hs)
    dk = sum(_dot_tn(ds[h], qh[h]) for h in hs)
    dv = sum(_dot_tn(pb[h], doh[h]) for h in hs)
    return dq, dk, dv


def _band_group_specs(lead, rows, cls, col0):
    def spec(width):
        if lead == "rows":
            return pl.BlockSpec((None, rows, width), lambda b, i: (b, 0, col0))
        return pl.BlockSpec((None, rows, cls * width), lambda b, i: (b, 0, i))
    return spec


def _band_group_fwd(a, cos_g, sin_g, *, rows, cls, steps, col0, name):
    bl = a.shape[0]
    nb = rows // BAND
    grp_w = 3 * DIL_W

    def body(a_ref, c_ref, s_ref, o_ref, l_ref, qr, kr):
        first_valid, later_valid = _band_masks()
        for j in range(cls):
            a0, t0, s0 = j * grp_w, j * DIL_W, j * LANES
            cos, sin = c_ref[:, t0:t0 + DIL_W], s_ref[:, t0:t0 + DIL_W]
            qr[...] = (_rope(a_ref[:, a0:a0 + DIL_W], cos, sin) * SB_SCALE).astype(BF16)
            kr[...] = _rope(a_ref[:, a0 + DIL_W:a0 + 2 * DIL_W], cos, sin).astype(BF16)

            def block(q0, k0, keys, valid, a0=a0, t0=t0, s0=s0):
                o, lse = _band_attend(qr[pl.ds(q0, BAND), :], kr[pl.ds(k0, keys), :],
                                      a_ref[pl.ds(k0, keys), a0 + 2 * DIL_W:a0 + grp_w], valid)
                o_ref[pl.ds(q0, BAND), t0:t0 + DIL_W] = o.astype(BF16)
                l_ref[pl.ds(q0, BAND), s0:s0 + LANES] = lse

            block(0, 0, BAND, first_valid)
            if nb > 1:
                def later(b, carry, block=block):
                    block(pl.multiple_of(b * BAND, BAND), pl.multiple_of((b - 1) * BAND, BAND), 2 * BAND, later_valid)
                    return carry

                lax.fori_loop(1, nb, later, 0, unroll=3)

    lead = "rows" if col0 is not None else "cols"
    spec = _band_group_specs(lead, rows, cls, col0)
    tab = pl.BlockSpec((rows, cls * DIL_W), lambda b, i: (0, i))
    n_cls = cos_g.shape[1] // DIL_W
    return pl.pallas_call(
        body,
        grid=(bl, steps),
        in_specs=[spec(grp_w), tab, tab],
        out_specs=[pl.BlockSpec((None, rows, cls * DIL_W), lambda b, i: (b, 0, i)),
                   pl.BlockSpec((None, rows, cls * LANES), lambda b, i: (b, 0, i))],
        out_shape=[jax.ShapeDtypeStruct((bl, rows, n_cls * DIL_W), BF16), jax.ShapeDtypeStruct((bl, rows, n_cls * LANES), F32)],
        scratch_shapes=[pltpu.VMEM((rows, DIL_W), BF16), pltpu.VMEM((rows, DIL_W), BF16)],
        compiler_params=pltpu.CompilerParams(dimension_semantics=("parallel", "parallel")),
        name=name,
    )(a, cos_g, sin_g)


def _band_group_bwd(a, do, st, cos_g, sin_g, *, rows, cls, steps, col0, name, side=None):
    bl = a.shape[0]
    nb = rows // BAND
    grp_w = 3 * DIL_W
    side_arrays = side[1] if side else []
    n_side = len(side_arrays)

    def body(a_ref, do_ref, st_ref, c_ref, s_ref, *rest):
        out_ref = rest[n_side]
        qr, kr, dk_acc, dv_acc = rest[2 * n_side + 1:2 * n_side + 5]
        if n_side:
            step = pl.program_id(0) * steps + pl.program_id(1)
            finish = _run_side(side, rest[:n_side], rest[n_side + 1:2 * n_side + 1], rest[2 * n_side + 5:], step, bl * steps)
        first_valid, later_valid = _band_masks()
        for j in range(cls):
            a0, t0 = j * grp_w, j * DIL_W
            cos, sin = c_ref[:, t0:t0 + DIL_W], s_ref[:, t0:t0 + DIL_W]
            qr[...] = (_rope(a_ref[:, a0:a0 + DIL_W], cos, sin) * SB_SCALE).astype(BF16)
            kr[...] = _rope(a_ref[:, a0 + DIL_W:a0 + 2 * DIL_W], cos, sin).astype(BF16)
            dk_acc[...] = jnp.zeros_like(dk_acc)
            dv_acc[...] = jnp.zeros_like(dv_acc)

            def block(q0, k0, keys, valid, a0=a0, t0=t0, s0=j * LANES):
                qrows, krows = pl.ds(q0, BAND), pl.ds(k0, keys)
                dq, dk, dv = _band_attend_bwd(
                    qr[qrows, :], kr[krows, :], a_ref[krows, a0 + 2 * DIL_W:a0 + grp_w], valid,
                    do_ref[qrows, t0:t0 + DIL_W], st_ref[qrows, s0:s0 + LANES])
                dq = dq * SB_SCALE
                out_ref[qrows, a0:a0 + DIL_W] = (dq * c_ref[qrows, t0:t0 + DIL_W]
                                                 - _swap_half(dq) * s_ref[qrows, t0:t0 + DIL_W]).astype(BF16)
                dk_acc[krows, :] += dk
                dv_acc[krows, :] += dv

            block(0, 0, BAND, first_valid)
            if nb > 1:
                def later(b, carry, block=block):
                    block(pl.multiple_of(b * BAND, BAND), pl.multiple_of((b - 1) * BAND, BAND), 2 * BAND, later_valid)
                    return carry

                lax.fori_loop(1, nb, later, 0, unroll=3)
            dk = dk_acc[...]
            out_ref[:, a0 + DIL_W:a0 + 2 * DIL_W] = (dk * cos - _swap_half(dk) * sin).astype(BF16)
            out_ref[:, a0 + 2 * DIL_W:a0 + grp_w] = dv_acc[...].astype(BF16)
        if n_side:
            finish()

    lead = "rows" if col0 is not None else "cols"
    spec = _band_group_specs(lead, rows, cls, col0)
    dspec = _band_group_specs(lead, rows, cls, 0 if col0 is not None else None)
    tab = pl.BlockSpec((rows, cls * DIL_W), lambda b, i: (0, i))
    n_cls = cos_g.shape[1] // DIL_W
    out = pl.pallas_call(
        body,
        grid=(bl, steps),
        in_specs=[spec(grp_w), dspec(DIL_W), dspec(LANES), tab, tab] + [ANY] * n_side,
        out_specs=[pl.BlockSpec((None, rows, cls * grp_w), lambda b, i: (b, 0, i))] + [ANY] * n_side,
        out_shape=[jax.ShapeDtypeStruct((bl, rows, n_cls * grp_w), BF16)] + (_side_out_shapes(side) if n_side else []),
        scratch_shapes=[pltpu.VMEM((rows, DIL_W), BF16), pltpu.VMEM((rows, DIL_W), BF16),
                        pltpu.VMEM((rows, DIL_W), F32), pltpu.VMEM((rows, DIL_W), F32)] + (_side_sems(side) if n_side else []),
        compiler_params=pltpu.CompilerParams(dimension_semantics=("arbitrary", "arbitrary") if n_side else ("parallel", "parallel")),
        name=name,
    )(a, do, st, cos_g, sin_g, *side_arrays)
    return (out[0], out[1:]) if n_side else out[0]


def _band_merge3(groups):
    t, tm = groups[0][0].shape[0], 512

    def body(o0, l0, o1, l1, o2, l2, ob_ref, lse_ref):
        a, b, c = l0[...], l1[...], l2[...]
        m = jnp.maximum(jnp.maximum(a, b), c)
        lse = m + jnp.log(jnp.exp(a - m) + jnp.exp(b - m) + jnp.exp(c - m))
        lane = lax.broadcasted_iota(jnp.int32, (1, DIL_W), 1)
        acc = jnp.zeros((tm, DIL_W), F32)
        for o_ref, l in ((o0, a), (o1, b), (o2, c)):
            share = jnp.exp(l - lse)
            spread = jnp.zeros((tm, DIL_W), F32)
            for h in range(BAND_HEADS):
                spread = jnp.where(lane // HEAD_DIM == h, share[:, h:h + 1], spread)
            acc = acc + spread * o_ref[...].astype(F32)
        ob_ref[...] = acc.astype(BF16)
        stat_lane = lax.broadcasted_iota(jnp.int32, (1, LANES), 1)
        lse_ref[...] = jnp.where(stat_lane < BAND_HEADS, lse, 0.0)

    spec = pl.BlockSpec((tm, DIL_W), lambda i: (i, 0))
    spec_l = pl.BlockSpec((tm, LANES), lambda i: (i, 0))
    return pl.pallas_call(
        body,
        grid=(t // tm,),
        in_specs=[spec, spec_l] * 3,
        out_specs=[spec, spec_l],
        out_shape=[jax.ShapeDtypeStruct((t, DIL_W), BF16), jax.ShapeDtypeStruct((t, LANES), F32)],
        compiler_params=pltpu.CompilerParams(dimension_semantics=("parallel",)),
        name="band_merge",
    )(*[a for g in groups for a in g])


MEM_T = 512
MEM_SCALE = 128 ** -0.5
MEM_Q_COL = (D_IN - MEM_W) // LANES


MEM_HEADS = MEM_W // LANES


def _mem_specs():
    qs = [pl.BlockSpec((None, MEM_T, LANES), lambda b, i, h=h: (b, i, MEM_Q_COL + h)) for h in range(MEM_HEADS)]
    kv = pl.BlockSpec((None, MEM_LEN, 2 * MEM_W), lambda b, i: (b, 0, 0))
    blk = pl.BlockSpec((None, MEM_T, MEM_W), lambda b, i: (b, i, 0))
    return qs, kv, blk


def _mem_probs(q, k):
    s = _dot_nt(q, k) * MEM_SCALE
    p = jnp.exp(s - jnp.max(s, axis=1, keepdims=True))
    return p * (1.0 / jnp.sum(p, axis=1, keepdims=True))


def _head_cols(h, base=0):
    return slice(base + h * LANES, base + (h + 1) * LANES)


def _mem_fwd(proj3, kv3):
    bl = proj3.shape[0]
    hs = range(MEM_HEADS)

    def body(*refs):
        q_refs, kv_ref, o_ref = refs[:MEM_HEADS], refs[MEM_HEADS], refs[MEM_HEADS + 1]
        p = [_mem_probs(q_refs[h][...], kv_ref[:, _head_cols(h)]) for h in hs]
        for h in hs:
            o_ref[:, _head_cols(h)] = _dot(p[h].astype(BF16), kv_ref[:, _head_cols(h, MEM_W)]).astype(BF16)

    qs, kv, blk = _mem_specs()
    return pl.pallas_call(
        body,
        grid=(bl, SEQ // MEM_T),
        in_specs=qs + [kv],
        out_specs=blk,
        out_shape=jax.ShapeDtypeStruct((bl, SEQ, MEM_W), BF16),
        compiler_params=pltpu.CompilerParams(dimension_semantics=("parallel", "parallel")),
        name="mem_fwd",
    )(*([proj3] * MEM_HEADS), kv3)


def _mem_bwd(proj3, kv3, do_c):
    bl = proj3.shape[0]
    hs = range(MEM_HEADS)

    def body(*refs):
        q_refs, kv_ref, do_ref, dq_ref, dkv_ref = refs[:MEM_HEADS], *refs[MEM_HEADS:MEM_HEADS + 4]

        @pl.when(pl.program_id(1) == 0)
        def _():
            dkv_ref[...] = jnp.zeros_like(dkv_ref)

        q = [q_refs[h][...] for h in hs]
        do = [do_ref[:, _head_cols(h)] for h in hs]
        p = [_mem_probs(q[h], kv_ref[:, _head_cols(h)]) for h in hs]
        dp = [_dot_nt(do[h], kv_ref[:, _head_cols(h, MEM_W)]) for h in hs]
        ds = [(p[h] * (dp[h] - jnp.sum(p[h] * dp[h], axis=1, keepdims=True)) * MEM_SCALE).astype(BF16) for h in hs]
        for h in hs:
            dq_ref[:, _head_cols(h)] = _dot(ds[h], kv_ref[:, _head_cols(h)]).astype(BF16)
            dkv_ref[:, _head_cols(h)] += _dot_tn(ds[h], q[h])
            dkv_ref[:, _head_cols(h, MEM_W)] += _dot_tn(p[h].astype(BF16), do[h])

    qs, kv, blk = _mem_specs()
    return pl.pallas_call(
        body,
        grid=(bl, SEQ // MEM_T),
        in_specs=qs + [kv, blk],
        out_specs=[blk, kv],
        out_shape=[jax.ShapeDtypeStruct((bl, SEQ, MEM_W), BF16), jax.ShapeDtypeStruct((bl, MEM_LEN, 2 * MEM_W), F32)],
        compiler_params=pltpu.CompilerParams(dimension_semantics=("parallel", "arbitrary")),
        name="mem_bwd",
    )(*([proj3] * MEM_HEADS), kv3, do_c)


def _place():
    x, y, c = lax.axis_index("x"), lax.axis_index("y"), lax.axis_index("c")
    return x, y, c


def _other_chips(x, y):
    return [(1 - x, y), (x, 1 - y), (1 - x, 1 - y)]


def _remote(src, dst, send_sem, recv_sem, to):
    return pltpu.make_async_remote_copy(src_ref=src, dst_ref=dst, send_sem=send_sem, recv_sem=recv_sem,
                                        device_id=to, device_id_type=MESH)


ANY = pl.BlockSpec(memory_space=pl.ANY)


def _gather_weights(shards):
    n = len(shards)

    def body(*refs):
        send, forward, finish = _gather_phases(refs[:n], refs[n:2 * n], *refs[2 * n:])
        send()
        forward()
        finish()

    return pl.pallas_call(
        body,
        in_specs=[ANY] * n,
        out_specs=[ANY] * n,
        out_shape=_gather_out_shapes(shards),
        scratch_shapes=_gather_sems(n),
        name="gather_weights",
    )(*shards)


def _gather_out_shapes(shards):
    return [jax.ShapeDtypeStruct((N_CHIPS,) + s.shape, s.dtype) for s in shards]


def _gather_sems(n):
    return [pltpu.SemaphoreType.DMA((6 * n,)), pltpu.SemaphoreType.DMA((6 * n,))]


def _gather_phases(in_refs, out_refs, send_sems, recv_sems):
    x, y, c = _place()
    sibling = (x, y, 1 - c)
    chips = _other_chips(x, y)
    first, passed = [], []
    for k in range(len(in_refs)):
        hf = in_refs[k].shape[0] // 2

        def half(px, py, pc, k=k, hf=hf):
            return out_refs[k].at[2 * px + py, pl.ds(pc * hf, hf), :]

        src = in_refs[k].at[pl.ds(c * hf, hf), :]
        for j, chip in enumerate(chips):
            s = 6 * k + j
            first.append(_remote(src, half(x, y, c), send_sems.at[s], recv_sems.at[s], (*chip, c)))
            passed.append((_remote(src, half(*chip, c), send_sems.at[s], recv_sems.at[s], (*chip, c)),
                           _remote(half(*chip, c), half(*chip, c), send_sems.at[s + 3], recv_sems.at[s + 3], sibling),
                           _remote(src, half(*chip, 1 - c), send_sems.at[s + 3], recv_sems.at[s + 3], sibling)))

    def send():
        for cp in first:
            cp.start()

    def forward():
        for landed, fwd, _ in passed:
            landed.wait_recv()
            fwd.start()

    def finish():
        for _, _, from_sibling in passed:
            from_sibling.wait_recv()
        for cp in first:
            cp.wait_send()
        for _, fwd, _ in passed:
            fwd.wait_send()

    return send, forward, finish


def _pair_exchange(grads, *, name):
    n = len(grads)
    side = ("pair", grads)

    def body(*refs):
        send, _, finish = _side_phases(side, refs[:n], refs[n:2 * n], refs[2 * n:])
        send()
        finish()

    return pl.pallas_call(
        body,
        in_specs=[ANY] * n,
        out_specs=[ANY] * n,
        out_shape=_side_out_shapes(side),
        scratch_shapes=_side_sems(side),
        name=name,
    )(*grads)


def _pair_exchange_phases(g_refs, land_refs, send_sems, recv_sems):
    x, y, c = _place()
    cps = []
    for k in range(len(g_refs)):
        hf = g_refs[k].shape[1] // 2
        src = g_refs[k].at[:, pl.ds((1 - c) * hf, hf), :]
        cps.append(_remote(src, land_refs[k], send_sems.at[k], recv_sems.at[k], (x, y, 1 - c)))

    def send():
        for cp in cps:
            cp.start()

    def finish():
        for cp in cps:
            cp.wait()

    return send, finish


def _side_out_shapes(side):
    kind, arrays = side
    if kind == "gather":
        return _gather_out_shapes(arrays)
    if kind == "pair":
        return [jax.ShapeDtypeStruct((N_CHIPS, g.shape[1] // 2, g.shape[2]), g.dtype) for g in arrays]
    return [jax.ShapeDtypeStruct(p.shape, p.dtype) for p in arrays]


def _side_sems(side):
    kind, arrays = side
    n = len(arrays)
    if kind == "gather":
        return _gather_sems(n)
    if kind == "pair":
        return [pltpu.SemaphoreType.DMA((n,)), pltpu.SemaphoreType.DMA((n,))]
    return _chip_exchange_sems(n)


def _side_phases(side, in_refs, out_refs, sems):
    kind = side[0]
    if kind == "gather":
        return _gather_phases(in_refs, out_refs, *sems)
    send, finish = (_pair_exchange_phases if kind == "pair" else _chip_exchange_phases)(in_refs, out_refs, *sems)
    return send, None, finish


def _run_side(side, in_refs, out_refs, sems, step, n_steps):
    first, mid, last = _side_phases(side, in_refs, out_refs, sems)
    pl.when(step == 0)(first)
    if mid is not None:
        pl.when(step == n_steps // 2)(mid)
    return lambda: pl.when(step == n_steps - 1)(last)


def _pair_add(g, land, c_arr, *, name):
    _, a, b = g.shape
    hf = a // 2

    def body(c_ref, g_ref, l_ref, o_ref):
        o_ref[...] = (g_ref[...] + l_ref[...]).astype(BF16)

    return pl.pallas_call(
        body,
        grid_spec=pltpu.PrefetchScalarGridSpec(
            num_scalar_prefetch=1,
            grid=(N_CHIPS,),
            in_specs=[pl.BlockSpec((None, None, hf, b), lambda s, c_ref: (s, c_ref[0], 0, 0)),
                      pl.BlockSpec((None, hf, b), lambda s, c_ref: (s, 0, 0))],
            out_specs=pl.BlockSpec((None, hf, b), lambda s, c_ref: (s, 0, 0)),
        ),
        out_shape=jax.ShapeDtypeStruct((N_CHIPS, hf, b), BF16),
        compiler_params=pltpu.CompilerParams(dimension_semantics=("parallel",)),
        name=name,
    )(c_arr, g.reshape(N_CHIPS, 2, hf, b), land)


def _chip_exchange(parts):
    n = len(parts)

    def body(*refs):
        send, finish = _chip_exchange_phases(refs[:n], refs[n:2 * n], *refs[2 * n:])
        send()
        finish()

    return pl.pallas_call(
        body,
        in_specs=[ANY] * n,
        out_specs=[ANY] * n,
        out_shape=[jax.ShapeDtypeStruct(p.shape, p.dtype) for p in parts],
        scratch_shapes=_chip_exchange_sems(n),
        name="chip_exchange",
    )(*parts)


def _chip_exchange_sems(n):
    return [pltpu.SemaphoreType.DMA((3 * n,)), pltpu.SemaphoreType.DMA((3 * n,))]


def _chip_exchange_phases(p_refs, land_refs, send_sems, recv_sems):
    x, y, c = _place()
    me = 2 * x + y
    sends, recvs = [], []
    for k in range(len(p_refs)):
        for j, (cx, cy) in enumerate(_other_chips(x, y)):
            s = 3 * k + j
            sends.append(_remote(p_refs[k].at[2 * cx + cy], land_refs[k].at[me], send_sems.at[s], recv_sems.at[s], (cx, cy, c)))
            recvs.append(_remote(p_refs[k].at[me], land_refs[k].at[2 * cx + cy], send_sems.at[s], recv_sems.at[s], (cx, cy, c)))

    def send():
        for cp in sends:
            cp.start()

    def finish():
        for cp in recvs:
            cp.wait_recv()
        for cp in sends:
            cp.wait_send()

    return send, finish


def _chip_add(land, part, me_arr, *, name):
    _, r, b = land.shape

    def body(me_ref, p_ref, l1_ref, l2_ref, l3_ref, o_ref):
        o_ref[...] = ((p_ref[...].astype(F32) + l1_ref[...].astype(F32)) + l2_ref[...].astype(F32)) + l3_ref[...].astype(F32)

    tr = r // 2
    other = lambda j: pl.BlockSpec((None, tr, b), lambda i, me_ref: (jnp.bitwise_xor(me_ref[0], j), i, 0))
    return pl.pallas_call(
        body,
        grid_spec=pltpu.PrefetchScalarGridSpec(
            num_scalar_prefetch=1,
            grid=(r // tr,),
            in_specs=[pl.BlockSpec((None, tr, b), lambda i, me_ref: (me_ref[0], i, 0)), other(2), other(1), other(3)],
            out_specs=pl.BlockSpec((tr, b), lambda i, me_ref: (i, 0)),
        ),
        out_shape=jax.ShapeDtypeStruct((r, b), F32),
        compiler_params=pltpu.CompilerParams(dimension_semantics=("parallel",)),
        name=name,
    )(me_arr, part, land, land, land)


def _pair_share(halves):
    n = len(halves)

    def body(*refs):
        h_refs, out_refs = refs[:n], refs[n:2 * n]
        send_sems, recv_sems = refs[2 * n:]
        x, y, c = _place()
        cps = [_remote(h_refs[k], out_refs[k], send_sems.at[k], recv_sems.at[k], (x, y, 1 - c)) for k in range(n)]
        for cp in cps:
            cp.start()
        for cp in cps:
            cp.wait()

    return pl.pallas_call(
        body,
        in_specs=[ANY] * n,
        out_specs=[ANY] * n,
        out_shape=[jax.ShapeDtypeStruct(h.shape, F32) for h in halves],
        scratch_shapes=[pltpu.SemaphoreType.DMA((n,)), pltpu.SemaphoreType.DMA((n,))],
        name="pair_share",
    )(*halves)


def _all_sum_small(part):
    def body(p_ref, o_ref, slots, send_sems, recv_sems):
        x, y, c = _place()
        me = 4 * x + 2 * y + c
        slots[me] = p_ref[...]
        peers = [(x ^ dx, y ^ dy, c ^ dc) for dx in (0, 1) for dy in (0, 1) for dc in (0, 1)][1:]
        sends = [_remote(p_ref, slots.at[me], send_sems.at[k], recv_sems.at[k], peer) for k, peer in enumerate(peers)]
        for cp in sends:
            cp.start()
        for k, (px, py, pc) in enumerate(peers):
            _remote(p_ref, slots.at[4 * px + 2 * py + pc], send_sems.at[k], recv_sems.at[k], (px, py, pc)).wait_recv()
        for cp in sends:
            cp.wait_send()
        acc = slots[0]
        for d in range(1, 8):
            acc = acc + slots[d]
        o_ref[...] = acc

    vmem = pl.BlockSpec(memory_space=pltpu.VMEM)
    return pl.pallas_call(
        body,
        in_specs=[vmem],
        out_specs=vmem,
        out_shape=jax.ShapeDtypeStruct(part.shape, F32),
        scratch_shapes=[pltpu.VMEM((8,) + part.shape, F32), pltpu.SemaphoreType.DMA((7,)), pltpu.SemaphoreType.DMA((7,))],
        name="all_sum_small",
    )(part)


def _deinterleave(a, d):
    b, s, c = a.shape
    return a.reshape(b, s // d, d, c).transpose(0, 2, 1, 3).reshape(b * s // BAND, BAND, c)


def _reinterleave(a, d, b):
    c = a.shape[-1]
    return a.reshape(b, d, SEQ // d, c).transpose(0, 2, 1, 3).reshape(b, SEQ, c)


def _rope_tables():
    half = HEAD_DIM // 2
    inv_freq = np.float32(ROPE_THETA) ** (-np.arange(half, dtype=np.float32) * np.float32(2.0) / np.float32(HEAD_DIM))
    ang = np.arange(SEQ, dtype=np.float32)[:, None] * inv_freq[None, :].astype(np.float32)
    cos = np.tile(np.cos(ang).astype(np.float32), (1, 2 * BAND_HEADS))
    sin = np.tile(np.concatenate([-np.sin(ang), np.sin(ang)], axis=1).astype(np.float32), (1, BAND_HEADS))
    return jnp.asarray(cos), jnp.asarray(sin)


def _band_groups():
    out = []
    for d in DIL_D:
        rows = SEQ // d
        cls = max(1, 512 // rows) if d > 1 else 1
        out.append(dict(rows=rows, cls=cls, steps=d // cls))
    return out


def _local_step(x, mem, loss_target, g_pre_mix, g_post_mix, g_pre_ffn, g_post_ffn, g_mem, b_gate, w, comm=None):
    bl = x.shape[0]
    t = bl * SEQ
    chips = range(N_CHIPS)
    half_ff = D_FF // 2

    def with_gathered(w, names, gathered, shards):
        return {**w, **{name: lax.dynamic_update_slice(g, s[None], (comm["me"][0], 0, 0))
                        for name, g, s in zip(names, gathered, shards)}}

    x2 = x.reshape(t, D_MODEL)
    tgt2 = loss_target.reshape(t, D_MODEL)
    mem2 = mem.reshape(bl * MEM_LEN, D_MODEL)

    h = _norm_fwd(x2, g_pre_mix, name="norm_x", side=("gather", comm["first_shards"]) if comm else None)
    if comm:
        w = with_gathered(w, comm["first_names"], h[1], comm["first_shards"])
        h = h[0]
    w_in_full = _join_shards(w["w_in"])
    proj = _mm([(h, w_in_full)], nt=False, tn=2176, out_dtypes=[BF16], name="proj",
               side=("gather", comm["mid_shards"]) if comm else None)
    if comm:
        w = with_gathered(w, comm["mid_names"], proj[1], comm["mid_shards"])
        proj = proj[0]
    w_mem_kv_full = w["w_mem_kv"].reshape(D_MODEL, 2 * MEM_W)
    gates = _mm([(h, w["w_gate"], None, "j")], nt=False,tn=w["w_gate"].shape[2], out_dtypes=[BF16], name="gates",
                bias=b_gate, epilogue=lambda acc: (_sigmoid(acc),))
    hm = _norm_fwd(mem2, g_mem, name="norm_mem")
    kv_m = _mm([(hm, w_mem_kv_full)], nt=False,tn=1024, out_dtypes=[BF16], name="mem_kv")
    proj3 = proj.reshape(bl, SEQ, D_IN)
    kv3 = kv_m.reshape(bl, MEM_LEN, 2 * MEM_W)

    o_a, o_a32, sb_weights, late_gathered = _sb_fwd(proj3, comm["late_shards"] if comm else [])
    if comm:
        w = with_gathered(w, comm["late_names"], late_gathered, comm["late_shards"])
    w_o_full = w["w_o"].reshape(D_MODEL, D_MODEL)
    w_ffn_out_full = w["w_ffn_out"].reshape(D_FF, D_MODEL)

    cos_t, sin_t = _rope_tables()
    dil0 = 3 * SB_W

    grp_w = 3 * DIL_W
    band = []
    for g, (d, cfg) in enumerate(zip(DIL_D, _band_groups())):
        a_g = proj3 if d == 1 else proj3[:, :, dil0 + g * grp_w:dil0 + (g + 1) * grp_w].reshape(bl, SEQ // d, d * grp_w)
        band.append(dict(cfg, a=a_g, col0=dil0 // grp_w if d == 1 else None, cos=cos_t.reshape(SEQ // d, d * DIL_W),
                         sin=sin_t.reshape(SEQ // d, d * DIL_W)))
    outs = [_band_group_fwd(b["a"], b["cos"], b["sin"], rows=b["rows"], cls=b["cls"], steps=b["steps"], col0=b["col0"],
                            name=f"band_fwd_{g}") for g, b in enumerate(band)]
    o_b, lse_b = _band_merge3([(o.reshape(t, DIL_W), l.reshape(t, LANES)) for o, l in outs])

    o_c = _mem_fwd(proj3, kv3)

    o_a2, o_c2 = o_a.reshape(t, SB_W), o_c.reshape(t, MEM_W)
    y_a, y_b, y_c, merged = _branch_merge_fwd(o_a2, o_b, o_c2, w["w_br_sb"], w["w_br_dil"], w["w_br_mem"], gates)
    mix = _mm([(merged, w_o_full)], nt=False,tn=1024, out_dtypes=[F32], name="mix")
    x1, h2 = _mid_fwd(mix, x2, g_post_mix, g_pre_ffn)
    gg, uu, f = _ffn_in_fwd(h2, w["w_ffn_in"])
    f2 = _mm([(f, w_ffn_out_full)], nt=False,tn=1024, out_dtypes=[F32], name="ffn_out")

    dy, df2, dg_post_ffn, loss_row = _loss_bwd(f2, x1, g_post_ffn, tgt2)

    dg_ffn, du_ffn = _mm([(df2, w_ffn_out_full)], nt=True,tn=half_ff, out_dtypes=[BF16, BF16], name="d_ffn_act",
                         extras=(gg, uu), epilogue=_swiglu_bwd_epilogue)
    gw = {}
    gw["w_ffn_out"] = _mm_tn(f, df2, tm=half_ff, tn=1024, name="gw_ffn_out").reshape(N_CHIPS, D_FF // N_CHIPS, D_MODEL)
    gw_ffn_g = _mm_tn(h2, dg_ffn, tm=1024, tn=half_ff, name="gw_ffn_gate", out_shards=True, slots=(N_CHIPS, 0))
    gw["w_ffn_in"] = _mm_tn(h2, du_ffn, tm=1024, tn=half_ff, name="gw_ffn_up", out_shards=True, slots=(N_CHIPS, 2), into=gw_ffn_g)
    dh2 = _mm([(dg_ffn, w["w_ffn_in"], 0, 0), (dg_ffn, w["w_ffn_in"], 1, 1), (du_ffn, w["w_ffn_in"], 0, 2),
               (du_ffn, w["w_ffn_in"], 1, 3)], nt=True, tn=1024, out_dtypes=[BF16], name="d_h2")
    dx1, dmix, dg_pre_ffn, dg_post_mix = _mid_bwd(dh2, x1, mix, g_pre_ffn, g_post_mix, dy)

    gw["w_o"] = _mm_tn(merged, dmix, tm=1024, tn=1024, name="gw_o").reshape(N_CHIPS, D_MODEL // N_CHIPS, D_MODEL)
    dmerged = _mm([(dmix, w_o_full)], nt=True, tn=1024, out_dtypes=[BF16], name="d_merged")
    dy_a, dy_b, dy_c, dgpre, db_gate = _gate_bwd(dmerged, gates, y_a, y_b, y_c)
    br_cols = D_MODEL // N_CHIPS
    gw["w_br_sb"] = _mm_tn(o_a2, dy_a, tm=512, tn=br_cols, name="gw_br_sb", out_shards=True)
    gw["w_br_dil"] = _mm_tn(o_b, dy_b, tm=256, tn=br_cols, name="gw_br_dil", out_shards=True)
    gw["w_br_mem"] = _mm_tn(o_c2, dy_c, tm=512, tn=br_cols, name="gw_br_mem", out_shards=True)
    gw["w_gate"] = _mm_tn(h, dgpre, tm=1024, tn=w["w_gate"].shape[2], name="gw_gate", out_shards=True)
    do_a = _mm([(dy_a, w["w_br_sb"], s, s) for s in chips], nt=True,tn=SB_W, out_dtypes=[BF16], name="d_o_a")
    do_b = _mm([(dy_b, w["w_br_dil"], s, s) for s in chips], nt=True,tn=DIL_W, out_dtypes=[BF16], name="d_o_b")
    do_c = _mm([(dy_c, w["w_br_mem"], s, s) for s in chips], nt=True,tn=MEM_W, out_dtypes=[BF16], name="d_o_c")

    dq_c, dkv_m = _mem_bwd(proj3, kv3, do_c.reshape(bl, SEQ, MEM_W))
    dkv_m = dkv_m.reshape(bl * MEM_LEN, 2 * MEM_W).astype(BF16)
    gw["w_mem_kv"] = _mm_tn(hm, dkv_m, tm=1024, tn=1024, name="gw_mem_kv").reshape(N_CHIPS, D_MODEL // N_CHIPS, 2 * MEM_W)
    dhm = _mm([(dkv_m, w_mem_kv_full)], nt=True,tn=1024, out_dtypes=[F32], name="d_hm")
    dg_mem = _mem_norm_bwd(dhm, mem2, g_mem)

    stats = _band_delta(do_b, o_b, lse_b)
    early = [name for name, _, _ in PACK if name != "w_in"] if comm else []
    grads = [gw[name] for name in early]
    d_dil = []
    for g, (d, b) in enumerate(zip(DIL_D, band)):
        out = _band_group_bwd(b["a"], do_b.reshape(bl, SEQ // d, d * DIL_W), stats.reshape(bl, SEQ // d, d * LANES),
                              b["cos"], b["sin"], rows=b["rows"], cls=b["cls"], steps=b["steps"], col0=b["col0"],
                              name=f"band_bwd_{g}", side=("pair", grads) if comm and g == 0 else None)
        if comm and g == 0:
            out, lands = out
        d_dil.append(out.reshape(bl, SEQ, grp_w))

    parts = [_pair_add(g, l, comm["c"], name="pair_add_" + name) for name, g, l in zip(early, grads, lands)] if comm else []
    dq_a, dk_a, dv_a, lands = _sb_bwd(proj3, o_a32, do_a.reshape(bl, SEQ, SB_W), sb_weights, parts)
    reduced = {name: (p, l) for name, p, l in zip(early, parts, lands)}

    in_cols = D_IN // N_CHIPS
    dproj_s = _split_to_shards([a.reshape(t, a.shape[-1]) for a in [dq_a, dk_a, dv_a] + d_dil + [dq_c]], name="dproj_shards")
    gw["w_in"] = _mm_tn(h, dproj_s, tm=1024, tn=in_cols, name="gw_in")
    if comm:
        land = _pair_exchange([gw["w_in"]], name="pair_exchange_w_in")[0]
        part_in = _pair_add(gw["w_in"], land, comm["c"], name="pair_add_w_in")
    dh = _mm([(dproj_s, w["w_in"], s, s) for s in chips] + [(dgpre, w["w_gate"], s, s) for s in chips],
             nt=True, tn=1024, out_dtypes=[BF16], name="d_h", side=("chip", [part_in]) if comm else None)
    if comm:
        dh, (land_in,) = dh
        reduced["w_in"] = (part_in, land_in)
    grad_x, dg_pre_mix = _first_bwd(dh, x2, g_pre_mix, dx1)
    small = jnp.concatenate([dg_pre_mix, dg_post_mix, dg_pre_ffn, dg_post_ffn, dg_mem, db_gate.reshape(3, D_MODEL)], axis=0)
    return loss_row[0, 0], grad_x.reshape(bl, SEQ, D_MODEL), gw, small, reduced


def kernel(x, mem, g_pre_mix, g_post_mix, g_pre_ffn, g_post_ffn, g_mem, w_in, w_mem_kv, w_br_sb, w_br_dil, w_br_mem, w_gate, b_gate, w_o, w_ffn_in, w_ffn_out, loss_target, m_g_pre_mix, m_g_post_mix, m_g_pre_ffn, m_g_post_ffn, m_g_mem, m_w_in, m_w_mem_kv, m_w_br_sb, m_w_br_dil, m_w_br_mem, m_w_gate, m_b_gate, m_w_o, m_w_ffn_in, m_w_ffn_out, v_g_pre_mix, v_g_post_mix, v_g_pre_ffn, v_g_post_ffn, v_g_mem, v_w_in, v_w_mem_kv, v_w_br_sb, v_w_br_dil, v_w_br_mem, v_w_gate, v_b_gate, v_w_o, v_w_ffn_in, v_w_ffn_out):
    w_shards = dict(w_in=w_in[0], w_mem_kv=w_mem_kv[0], w_br_sb=w_br_sb[0], w_br_dil=w_br_dil[0], w_br_mem=w_br_mem[0],
                    w_gate=w_gate[0], w_o=w_o[0], w_ffn_in=w_ffn_in[0], w_ffn_out=w_ffn_out[0])
    m_shards = dict(w_in=m_w_in[0], w_mem_kv=m_w_mem_kv[0], w_br_sb=m_w_br_sb[0], w_br_dil=m_w_br_dil[0], w_br_mem=m_w_br_mem[0],
                    w_gate=m_w_gate[0], w_o=m_w_o[0], w_ffn_in=m_w_ffn_in[0], w_ffn_out=m_w_ffn_out[0])
    v_shards = dict(w_in=v_w_in[0], w_mem_kv=v_w_mem_kv[0], w_br_sb=v_w_br_sb[0], w_br_dil=v_w_br_dil[0], w_br_mem=v_w_br_mem[0],
                    w_gate=v_w_gate[0], w_o=v_w_o[0], w_ffn_in=v_w_ffn_in[0], w_ffn_out=v_w_ffn_out[0])

    names = [name for name, _, _ in PACK]
    c_arr = lax.axis_index("c").astype(jnp.int32).reshape(1)
    me_arr = (2 * lax.axis_index("x") + lax.axis_index("y")).astype(jnp.int32).reshape(1)
    mid_names = ["w_gate", "w_mem_kv"]
    late_names = [name for name in names if name not in ["w_in"] + mid_names]
    bf = {name: w_shards[name].astype(BF16) for name in names}
    comm = dict(c=c_arr, me=me_arr, first_names=["w_in"], first_shards=[bf["w_in"]],
                mid_names=mid_names, mid_shards=[bf[name] for name in mid_names],
                late_names=late_names, late_shards=[bf[name] for name in late_names])

    loss_local, grad_x, gw, small, reduced = _local_step(x, mem, loss_target, g_pre_mix, g_post_mix, g_pre_ffn, g_post_ffn,
                                                         g_mem, b_gate, {}, comm)
    loss = lax.psum(loss_local, ("x", "y", "c"))

    halves =[_chip_add(reduced[name][1], reduced[name][0], me_arr, name="chip_add_" + name) for name in names]
    theirs = _pair_share(halves)
    small = _all_sum_small(small)

    upd = {}
    for name, mine, other in zip(names, halves, theirs):
        upd[name] = _adamw_halves(w_shards[name], mine, other, m_shards[name], v_shards[name], c_arr, name="adamw_" + name)
    g_shards = {name: u[0] for name, u in upd.items()}

    def small8(gs, b):
        return jnp.concatenate(gs + [b.reshape(3, D_MODEL)], axis=0)

    sw = small8([g_pre_mix, g_post_mix, g_pre_ffn, g_post_ffn, g_mem], b_gate)
    sm = small8([m_g_pre_mix, m_g_post_mix, m_g_pre_ffn, m_g_post_ffn, m_g_mem], m_b_gate)
    sv = small8([v_g_pre_mix, v_g_post_mix, v_g_pre_ffn, v_g_post_ffn, v_g_mem], v_b_gate)
    s_upd = _adamw(sw, small, sm, sv, tm=8, name="adamw_small")

    def small_out(a):
        return [a[0:1], a[1:2], a[2:3], a[3:4], a[4:5]]

    order = ["w_in", "w_mem_kv", "w_br_sb", "w_br_dil", "w_br_mem", "w_gate", "b_gate", "w_o", "w_ffn_in", "w_ffn_out"]

    def leaves(small_arr, big):
        out = small_out(small_arr)
        for name in order:
            out.append(small_arr[5:8].reshape(1, 3 * D_MODEL) if name == "b_gate" else big[name][None])
        return out

    grads_out = leaves(small, g_shards)
    delta_out = leaves(s_upd[0], {n: u[1] for n, u in upd.items()})
    m_out = leaves(s_upd[1], {n: u[2] for n, u in upd.items()})
    v_out = leaves(s_upd[2], {n: u[3] for n, u in upd.items()})
    return (loss, grad_x, *grads_out, *delta_out, *m_out, *v_out)
```

```python
import jax
import jax.numpy as jnp
import numpy as np
from jax import lax
from jax.experimental import pallas as pl
from jax.experimental.pallas import tpu as pltpu

F32 = jnp.float32
BF16 = jnp.bfloat16
MESH = pl.DeviceIdType.MESH

D_MODEL = 1024
SEQ = 2048
HEAD_DIM = 64
SB_W = 512
DIL_W = 256
MEM_W = 512
MEM_LEN = 256
D_IN = 3 * SB_W + 9 * DIL_W + MEM_W
D_FF = 2816
DIL_D = (1, 4, 16)
ROPE_THETA = 10000.0
NORM_EPS = 1e-6
NEG_INF = -1e30
LANES = 128

ADAM_LR = 0.001
ADAM_B1 = 0.9
ADAM_B2 = 0.999
ADAM_EPS = 1e-08
ADAM_WD = 0.01
ADAM_STEP = 10

N_CHIPS = 4
PACK = (
    ("w_in", (1024, 1088), 1),
    ("w_mem_kv", (256, 1024), 0),
    ("w_br_sb", (512, 256), 1),
    ("w_br_dil", (256, 256), 1),
    ("w_br_mem", (512, 256), 1),
    ("w_gate", (1024, 768), 1),
    ("w_o", (256, 1024), 0),
    ("w_ffn_in", (1024, 1408), 1),
    ("w_ffn_out", (704, 1024), 0),
)
PACK_ROWS = sum(a * b for _, (a, b), _ in PACK) // D_MODEL
HALF_ROWS = PACK_ROWS // 2


def _dot(a, b):
    return lax.dot_general(a, b, (((1,), (0,)), ((), ())), preferred_element_type=F32)


def _dot_nt(a, b):
    return lax.dot_general(a, b, (((1,), (1,)), ((), ())), preferred_element_type=F32)


def _dot_tn(a, b):
    return lax.dot_general(a, b, (((0,), (0,)), ((), ())), preferred_element_type=F32)


def _split_dot(x, u):
    hi = x.astype(BF16)
    lo = (x - hi.astype(F32)).astype(BF16)
    return _dot(hi, u) + _dot(lo, u)


V7X_VMEM_BUDGET = 44 * 2 ** 20


def _rows_that_fit(m, row_bytes, fixed_bytes):
    for tm in (1024, 512, 256, 128):
        if m % tm == 0 and fixed_bytes + tm * row_bytes <= V7X_VMEM_BUDGET:
            return tm
    return min(m, 128)


def _mm(pairs, *, nt, tn, out_dtypes, name, bias=None, extras=(), epilogue=None, side=None):
    pairs = [p if len(p) == 4 else (p[0], p[1], None, None) for p in pairs]
    m = pairs[0][0].shape[-2]
    b0 = pairs[0][1]
    if nt:
        n = b0.shape[-2]
    else:
        n = b0.shape[-1] * (b0.shape[0] if b0.ndim == 3 else 1)
    n_pairs, n_extra, n_out = len(pairs), len(extras), len(out_dtypes)
    assert n % tn == 0
    one_col = n == tn
    ks = [(b.shape[-1] if nt else b.shape[-2]) for _, b, _, _ in pairs]
    fixed = sum(k * tn * 2 for k in ks) * (1 if one_col else 2)
    row_bytes = 2 * sum(k * 2 for k in ks) + 2 * tn * (sum(jnp.dtype(dt).itemsize for dt in out_dtypes) + 2 * n_extra) + 2 * tn * 4
    tm = _rows_that_fit(m, row_bytes, fixed)
    assert m % tm == 0
    b_mode = dict(pipeline_mode=pl.Buffered(1)) if one_col else {}
    has_bias = bias is not None
    side_arrays = side[1] if side else []
    n_side = len(side_arrays)
    n_main_in = 2 * n_pairs + has_bias + n_extra
    n_steps = (n // tn) * (m // tm)

    def body(*refs):
        if n_side:
            step = pl.program_id(0) * (m // tm) + pl.program_id(1)
            finish = _run_side(side, refs[n_main_in:n_main_in + n_side],
                               refs[n_main_in + n_side + n_out:n_main_in + 2 * n_side + n_out],
                               refs[n_main_in + 2 * n_side + n_out:], step, n_steps)
        outs = refs[n_main_in + n_side:n_main_in + n_side + n_out]
        acc = None
        for i in range(n_pairs):
            a, b = refs[2 * i][...], refs[2 * i + 1][...]
            p = _dot_nt(a, b) if nt else _dot(a, b)
            acc = p if acc is None else acc + p
        pos = 2 * n_pairs
        if has_bias:
            acc = acc + refs[pos][...]
            pos += 1
        ex = [r[...] for r in refs[pos:pos + n_extra]]
        vals = (acc,) if epilogue is None else epilogue(acc, *ex)
        for r, v, dt in zip(outs, vals, out_dtypes):
            r[...] = v.astype(dt)
        if n_side:
            finish()

    in_specs, args = [], []
    for a, b, a_col, b_sel in pairs:
        k = b.shape[-1] if nt else b.shape[-2]
        assert a_col is not None or a.shape[1] == k
        if a.ndim == 3:
            in_specs.append(pl.BlockSpec((None, tm, k), lambda j, i, c=a_col: (c, i, 0)))
        else:
            in_specs.append(pl.BlockSpec((tm, k), lambda j, i, c=a_col or 0: (i, c)))
        if b.ndim == 2:
            in_specs.append(pl.BlockSpec((tn, k), lambda j, i: (j, 0), **b_mode) if nt
                            else pl.BlockSpec((k, tn), lambda j, i: (0, j), **b_mode))
        elif nt:
            in_specs.append(pl.BlockSpec((None, tn, k), lambda j, i, s=b_sel: (s, j, 0), **b_mode))
        else:
            assert b_sel == "j" and b.shape[-1] == tn
            in_specs.append(pl.BlockSpec((None, k, tn), lambda j, i: (j, 0, 0), **b_mode))
        args += [a, b]
    if has_bias:
        in_specs.append(pl.BlockSpec((1, tn), lambda j, i: (0, j)))
        args.append(bias)
    for e in extras:
        in_specs.append(pl.BlockSpec((tm, tn), lambda j, i: (i, j)))
        args.append(e)
    out = pl.pallas_call(
        body,
        grid=(n // tn, m // tm),
        in_specs=in_specs + [ANY] * n_side,
        out_specs=[pl.BlockSpec((tm, tn), lambda j, i: (i, j)) for _ in range(n_out)] + [ANY] * n_side,
        out_shape=[jax.ShapeDtypeStruct((m, n), dt) for dt in out_dtypes] + (_side_out_shapes(side) if n_side else []),
        scratch_shapes=_side_sems(side) if n_side else [],
        compiler_params=pltpu.CompilerParams(dimension_semantics=("arbitrary", "arbitrary") if n_side else ("parallel", "parallel")),
        name=name,
    )(*args, *side_arrays)
    if n_side:
        return (out[0] if n_out == 1 else out[:n_out]), out[n_out:]
    return out[0] if n_out == 1 else out


def _mm_tn(a, b, *, tm, tn, name, out_shards=False, slots=None, into=None):
    k, m = a.shape
    b_shards = b.ndim == 3
    out_shards = out_shards or b_shards
    n = b.shape[0] * b.shape[2] if b_shards else b.shape[1]
    tk = _rows_that_fit(k, 2 * 2 * (tm + tn), 3 * tm * tn * 4)
    assert m % tm == 0 and n % tn == 0 and k % tk == 0 and (not b_shards or b.shape[2] == tn)
    total, first = slots if slots else (n // tn, 0)

    def body(a_ref, b_ref, *rest):
        o_ref = rest[-1]

        @pl.when(pl.program_id(2) == 0)
        def _():
            o_ref[...] = jnp.zeros_like(o_ref)

        o_ref[...] += _dot_tn(a_ref[...], b_ref[...])

    if b_shards:
        b_spec = pl.BlockSpec((None, tk, tn), lambda i, j, kk: (j, kk, 0))
    else:
        b_spec = pl.BlockSpec((tk, tn), lambda i, j, kk: (kk, j))
    if out_shards:
        out_spec = pl.BlockSpec((None, tm, tn), lambda i, j, kk: (j + first, i, 0))
        out_shape = jax.ShapeDtypeStruct((total, m, tn), F32)
    else:
        out_spec = pl.BlockSpec((tm, tn), lambda i, j, kk: (i, j))
        out_shape = jax.ShapeDtypeStruct((m, n), F32)
    return pl.pallas_call(
        body,
        grid=(m // tm, n // tn, k // tk),
        in_specs=[pl.BlockSpec((tk, tm), lambda i, j, kk: (kk, i)), b_spec] + ([ANY] if into is not None else []),
        out_specs=out_spec,
        out_shape=out_shape,
        input_output_aliases={2: 0} if into is not None else {},
        compiler_params=pltpu.CompilerParams(dimension_semantics=("parallel", "parallel", "arbitrary")),
        name=name,
    )(*([a, b] + ([into] if into is not None else [])))


def _rowwise(fn, ins, outs, *, tm, name, side=None):
    rows = next(a.shape[0] for a, kind in ins if kind == "row")
    tm = min(tm, rows)
    assert rows % tm == 0
    n_in, n_out = len(ins), len(outs)
    side_arrays = side[1] if side else []
    n_side = len(side_arrays)

    def body(*refs):
        if n_side:
            finish = _run_side(side, refs[n_in:n_in + n_side], refs[n_in + n_side + n_out:n_in + 2 * n_side + n_out],
                               refs[n_in + 2 * n_side + n_out:], pl.program_id(0), rows // tm)
        vals = fn(*[r[...] for r in refs[:n_in]])
        for (_, dt, kind), r, v in zip(outs, refs[n_in + n_side:n_in + n_side + n_out], vals):
            if kind == "row":
                r[...] = v.astype(dt)
            else:
                @pl.when(pl.program_id(0) == 0)
                def _(r=r):
                    r[...] = jnp.zeros_like(r)

                r[...] += v
        if n_side:
            finish()

    in_specs = [pl.BlockSpec((tm, a.shape[1]), lambda i: (i, 0)) if kind == "row" else pl.BlockSpec(a.shape, lambda i: (0, 0))
                for a, kind in ins]
    out_specs = [pl.BlockSpec((tm, c), lambda i: (i, 0)) if kind == "row" else pl.BlockSpec((1, c), lambda i: (0, 0))
                 for c, _, kind in outs]
    out_shape = [jax.ShapeDtypeStruct((rows if kind == "row" else 1, c), dt) for c, dt, kind in outs]
    ordered = n_side or any(kind == "acc" for _, _, kind in outs)
    return pl.pallas_call(
        body,
        grid=(rows // tm,),
        in_specs=in_specs + [ANY] * n_side,
        out_specs=out_specs + [ANY] * n_side,
        out_shape=out_shape + (_side_out_shapes(side) if n_side else []),
        scratch_shapes=_side_sems(side) if n_side else [],
        compiler_params=pltpu.CompilerParams(dimension_semantics=("arbitrary" if ordered else "parallel",)),
        name=name,
    )(*[a for a, _ in ins], *side_arrays)


def _rstd(x):
    return lax.rsqrt(jnp.mean(x * x, axis=-1, keepdims=True) + NORM_EPS)


def _norm_bwd(dout, xin, g):
    r = _rstd(xin)
    n = xin * r
    dn = dout * g
    dg = jnp.sum(dout * n, axis=0, keepdims=True)
    dx = r * (dn - n * jnp.mean(dn * n, axis=-1, keepdims=True))
    return dx, dg


def _sigmoid(x):
    return 0.5 * jnp.tanh(0.5 * x) + 0.5


def _norm_fwd(x, g, *, name, side=None):
    def fn(x, g):
        return ((x * _rstd(x)) * g,)

    out = _rowwise(fn, [(x, "row"), (g, "vec")], [(D_MODEL, BF16, "row")], tm=512, name=name, side=side)
    return (out[0], out[1:]) if side else out[0]


def _mid_fwd(mix, x, g_post_mix, g_pre_ffn):
    def fn(mix, x, g2, g3):
        x1 = x + (mix * _rstd(mix)) * g2
        return x1, (x1 * _rstd(x1)) * g3

    return _rowwise(fn, [(mix, "row"), (x, "row"), (g_post_mix, "vec"), (g_pre_ffn, "vec")],
                    [(D_MODEL, F32, "row"), (D_MODEL, BF16, "row")], tm=512, name="mid_fwd")


def _loss_bwd(f2, x1, g_post_ffn, tgt):
    def fn(f2, x1, g4, tgt):
        r = _rstd(f2)
        n = f2 * r
        err = x1 + n * g4 - tgt
        loss = 0.5 * jnp.sum(jnp.mean(err * err, axis=-1, keepdims=True), axis=0, keepdims=True)
        dy = err * (1.0 / D_MODEL)
        dn = dy * g4
        dg4 = jnp.sum(dy * n, axis=0, keepdims=True)
        df2 = r * (dn - n * jnp.mean(dn * n, axis=-1, keepdims=True))
        return dy, df2, dg4, jnp.broadcast_to(loss, (1, LANES))

    return _rowwise(fn, [(f2, "row"), (x1, "row"), (g_post_ffn, "vec"), (tgt, "row")],
                    [(D_MODEL, BF16, "row"), (D_MODEL, BF16, "row"), (D_MODEL, F32, "acc"), (LANES, F32, "acc")],
                    tm=512, name="loss_bwd")


def _mid_bwd(dh2, x1, mix, g_pre_ffn, g_post_mix, dy):
    def fn(dh2, x1, mix, g3, g2, dy):
        d3, dg3 = _norm_bwd(dh2.astype(F32), x1, g3)
        dx1 = dy.astype(F32) + d3
        dmix, dg2 = _norm_bwd(dx1, mix, g2)
        return dx1, dmix, dg3, dg2

    return _rowwise(fn, [(dh2, "row"), (x1, "row"), (mix, "row"), (g_pre_ffn, "vec"), (g_post_mix, "vec"), (dy, "row")],
                    [(D_MODEL, BF16, "row"), (D_MODEL, BF16, "row"), (D_MODEL, F32, "acc"), (D_MODEL, F32, "acc")],
                    tm=256, name="mid_bwd")


def _first_bwd(dh, x, g_pre_mix, dx1):
    def fn(dh, x, g1, dx1):
        d1, dg1 = _norm_bwd(dh.astype(F32), x, g1)
        return dx1.astype(F32) + d1, dg1

    return _rowwise(fn, [(dh, "row"), (x, "row"), (g_pre_mix, "vec"), (dx1, "row")],
                    [(D_MODEL, F32, "row"), (D_MODEL, F32, "acc")], tm=512, name="first_bwd")


def _mem_norm_bwd(dhm, mem, g_mem):
    def fn(dhm, mem, g):
        return (jnp.sum(dhm * (mem * _rstd(mem)), axis=0, keepdims=True),)

    return _rowwise(fn, [(dhm, "row"), (mem, "row"), (g_mem, "vec")], [(D_MODEL, F32, "acc")], tm=512, name="mem_norm_bwd")[0]


def _gate_bwd(dmerged, gates, ya, yb, yc):
    def fn(dm, gt, ya, yb, yc):
        dm = dm.astype(F32)
        gt = gt.astype(F32)
        outs, dgp = [], []
        for i, y in enumerate((ya, yb, yc)):
            gi = gt[:, i * D_MODEL:(i + 1) * D_MODEL]
            outs.append(dm * gi)
            dgp.append(dm * y.astype(F32) * gi * (1.0 - gi))
        dgpre = jnp.concatenate(dgp, axis=1)
        return outs[0], outs[1], outs[2], dgpre, jnp.sum(dgpre, axis=0, keepdims=True)

    return _rowwise(fn, [(dmerged, "row"), (gates, "row"), (ya, "row"), (yb, "row"), (yc, "row")],
                    [(D_MODEL, BF16, "row")] * 3 + [(3 * D_MODEL, BF16, "row"), (3 * D_MODEL, F32, "acc")],
                    tm=256, name="gate_bwd")


def _adamw_math(w, g, m, v):
    m = ADAM_B1 * m + (1.0 - ADAM_B1) * g
    v = ADAM_B2 * v + (1.0 - ADAM_B2) * (g * g)
    m_hat = m / (1.0 - ADAM_B1 ** ADAM_STEP)
    v_hat = v / (1.0 - ADAM_B2 ** ADAM_STEP)
    delta = -ADAM_LR * (m_hat / (jnp.sqrt(v_hat) + ADAM_EPS) + ADAM_WD * w)
    return delta, m, v


def _adamw(w, g, m, v, *, tm, name):
    c = w.shape[1]
    return _rowwise(_adamw_math, [(w, "row"), (g, "row"), (m, "row"), (v, "row")], [(c, F32, "row")] * 3, tm=tm, name=name)


def _adamw_halves(w, g_mine, g_theirs, m, v, c_arr, *, name):
    a, b = w.shape
    hf = a // 2
    tr = hf // 4

    def body(c_ref, w_ref, gm_ref, gt_ref, m_ref, v_ref, g_out, d_out, m_out, v_out):
        g = jnp.where(pl.program_id(0) == c_ref[0], gm_ref[...], gt_ref[...])
        d, m_new, v_new = _adamw_math(w_ref[...], g, m_ref[...], v_ref[...])
        g_out[...] = g
        d_out[...] = d
        m_out[...] = m_new
        v_out[...] = v_new

    full = pl.BlockSpec((tr, b), lambda hh, i, c_ref: (hh * (hf // tr) + i, 0))
    half = pl.BlockSpec((tr, b), lambda hh, i, c_ref: (i, 0))
    return pl.pallas_call(
        body,
        grid_spec=pltpu.PrefetchScalarGridSpec(
            num_scalar_prefetch=1,
            grid=(2, hf // tr),
            in_specs=[full, half, half, full, full],
            out_specs=[full] * 4,
        ),
        out_shape=[jax.ShapeDtypeStruct((a, b), F32)] * 4,
        compiler_params=pltpu.CompilerParams(dimension_semantics=("parallel", "parallel")),
        name=name,
    )(c_arr, w, g_mine, g_theirs, m, v)


def _ffn_in_fwd(h2, w_ffn):
    m, tm, tn = h2.shape[0], 512, w_ffn.shape[2]
    assert 2 * tn == D_FF

    def body(h_ref, wg_ref, wu_ref, g_ref, u_ref, f_ref):
        h = h_ref[...]
        g = _dot(h, wg_ref[...])
        u = _dot(h, wu_ref[...])
        g_ref[...] = g.astype(BF16)
        u_ref[...] = u.astype(BF16)
        f_ref[...] = (g * _sigmoid(g) * u).astype(BF16)

    o_spec = pl.BlockSpec((tm, tn), lambda j, i: (i, j))
    return pl.pallas_call(
        body,
        grid=(D_FF // tn, m // tm),
        in_specs=[pl.BlockSpec((tm, D_MODEL), lambda j, i: (i, 0)),
                  pl.BlockSpec((None, D_MODEL, tn), lambda j, i: (j, 0, 0)),
                  pl.BlockSpec((None, D_MODEL, tn), lambda j, i: (j + 2, 0, 0))],
        out_specs=[o_spec, o_spec, o_spec],
        out_shape=[jax.ShapeDtypeStruct((m, D_FF), BF16)] * 3,
        compiler_params=pltpu.CompilerParams(dimension_semantics=("parallel", "parallel")),
        name="ffn_in_fwd",
    )(h2, w_ffn, w_ffn)


def _join_shards(w4):
    _, rows, cols = w4.shape
    tr = rows // 4

    def body(w_ref, o_ref):
        for s in range(N_CHIPS):
            o_ref[:, s * cols:(s + 1) * cols] = w_ref[s]

    return pl.pallas_call(
        body,
        grid=(rows // tr,),
        in_specs=[pl.BlockSpec((N_CHIPS, tr, cols), lambda i: (0, i, 0))],
        out_specs=pl.BlockSpec((tr, N_CHIPS * cols), lambda i: (i, 0)),
        out_shape=jax.ShapeDtypeStruct((rows, N_CHIPS * cols), w4.dtype),
        compiler_params=pltpu.CompilerParams(dimension_semantics=("parallel",)),
        name="join_shards",
    )(w4)


def _split_to_shards(pieces, *, name):
    t = pieces[0].shape[0]
    widths = [p.shape[1] for p in pieces]
    cols = sum(widths) // N_CHIPS
    tm = 512
    plan, start = [], 0
    for p, wd in enumerate(widths):
        for s in range(N_CHIPS):
            lo, hi = max(start, s * cols), min(start + wd, (s + 1) * cols)
            if lo < hi:
                plan.append((s, p, lo - s * cols, hi - s * cols, lo - start, hi - start))
        start += wd

    def body(*refs):
        o_ref = refs[-1]
        for s, p, o_lo, o_hi, p_lo, p_hi in plan:
            o_ref[s, :, o_lo:o_hi] = refs[p][:, p_lo:p_hi]

    return pl.pallas_call(
        body,
        grid=(t // tm,),
        in_specs=[pl.BlockSpec((tm, wd), lambda i: (i, 0)) for wd in widths],
        out_specs=pl.BlockSpec((N_CHIPS, tm, cols), lambda i: (0, i, 0)),
        out_shape=jax.ShapeDtypeStruct((N_CHIPS, t, cols), pieces[0].dtype),
        compiler_params=pltpu.CompilerParams(dimension_semantics=("parallel",)),
        name=name,
    )(*pieces)


def _swiglu_bwd_epilogue(df, g, u):
    g = g.astype(F32)
    u = u.astype(F32)
    sg = _sigmoid(g)
    return df * u * (sg * (1.0 + g * (1.0 - sg))), df * (g * sg)


def _branch_merge_fwd(o_a, o_b, o_c, w_sb, w_dil, w_mem, gates):
    m, tm = o_a.shape[0], 256

    def body(oa_ref, ob_ref, oc_ref, wa_ref, wb_ref, wc_ref, gt_ref, ya_ref, yb_ref, yc_ref, mg_ref):
        def project(o_ref, w_ref):
            o = o_ref[...]
            return jnp.concatenate([_dot(o, w_ref[s]) for s in range(N_CHIPS)], axis=1)

        ya = project(oa_ref, wa_ref)
        yb = project(ob_ref, wb_ref)
        yc = project(oc_ref, wc_ref)
        gt = gt_ref[...].astype(F32)
        ya_ref[...] = ya.astype(BF16)
        yb_ref[...] = yb.astype(BF16)
        yc_ref[...] = yc.astype(BF16)
        mg_ref[...] = (gt[:, :D_MODEL] * ya + gt[:, D_MODEL:2 * D_MODEL] * yb + gt[:, 2 * D_MODEL:] * yc).astype(BF16)

    row = lambda c: pl.BlockSpec((tm, c), lambda i: (i, 0))
    full = lambda a: pl.BlockSpec(a.shape, lambda i: (0, 0, 0))
    return pl.pallas_call(
        body,
        grid=(m // tm,),
        in_specs=[row(SB_W), row(DIL_W), row(MEM_W), full(w_sb), full(w_dil), full(w_mem), row(3 * D_MODEL)],
        out_specs=[row(D_MODEL)] * 4,
        out_shape=[jax.ShapeDtypeStruct((m, D_MODEL), BF16)] * 4,
        compiler_params=pltpu.CompilerParams(dimension_semantics=("parallel",)),
        name="branch_merge_fwd",
    )(o_a, o_b, o_c, w_sb, w_dil, w_mem, gates)


SB_T = 256
SB_SCALE = HEAD_DIM ** -0.5


def _sb_masks():
    row = lax.broadcasted_iota(jnp.int32, (SB_T, SB_T), 0)
    col = lax.broadcasted_iota(jnp.int32, (SB_T, SB_T), 1)
    lane = lax.broadcasted_iota(jnp.int32, (1, LANES), 1)
    return row, col, lane


def _sb_logs(z):
    lb = jnp.minimum(z, 0.0) - jnp.log(1.0 + jnp.exp(-jnp.abs(z)))
    return lb, lb - z


def _sb_specs(n_heads_pairs, col0):
    q = pl.BlockSpec((None, SB_T, LANES), lambda b, p, i: (b, i, col0 + p))
    k = pl.BlockSpec((None, SEQ, LANES), lambda b, p, i: (b, 0, col0 + n_heads_pairs + p))
    v = pl.BlockSpec((None, SEQ, LANES), lambda b, p, i: (b, 0, col0 + 2 * n_heads_pairs + p))
    return q, k, v


def _sb_first_blocks(i):
    rem = lax.rem(i + 1, 3)
    return jnp.where(rem == 0, 3, rem)


def _grid_step(n_pairs, nq):
    return (pl.program_id(0) * n_pairs + pl.program_id(1)) * nq + pl.program_id(2)


def _sb_fwd(proj3, late_shards):
    bl = proj3.shape[0]
    n_pairs = SB_W // LANES
    nq = SEQ // SB_T
    n_late = len(late_shards)
    n_steps = bl * n_pairs * nq

    def body(q_ref, k_ref, v_ref, *rest):
        late_in, (o_ref, o32_ref, w_ref), late_out = rest[:n_late], rest[n_late:n_late + 3], rest[n_late + 3:2 * n_late + 3]
        step = _grid_step(n_pairs, nq)
        if n_late:
            send, forward, finish = _gather_phases(late_in, late_out, *rest[2 * n_late + 3:])
            pl.when(step == 0)(send)
            pl.when(step == n_steps // 2)(forward)
        i = pl.program_id(2)
        row, col, lane = _sb_masks()
        causal = col < row
        u_excl = (row > col).astype(BF16)
        q = q_ref[...]
        heads = []
        for h in range(2):
            mh = (lane // HEAD_DIM) == h
            heads.append((mh, jnp.where(mh, q, jnp.zeros_like(q)) * SB_SCALE))

        def blocks(js, diags, carries, acc):
            ks = [k_ref[pl.ds(pl.multiple_of(j * SB_T, SB_T), SB_T), :] for j in js]
            vs = [v_ref[pl.ds(pl.multiple_of(j * SB_T, SB_T), SB_T), :] for j in js]
            chains = [(b, h) for b in range(len(js)) for h in range(2)]
            z = {c: _dot_nt(heads[c[1]][1], ks[c[0]]) for c in chains}
            lb, lk = {}, {}
            for c in chains:
                lb[c], lk[c] = _sb_logs(z[c])
                if diags[c[0]]:
                    lk[c] = jnp.where(causal, lk[c], 0.0)
            r = {c: _split_dot(lk[c], u_excl) for c in chains}
            carries = list(carries)
            w = {}
            for b, h in chains:
                w_c = jnp.exp(lb[b, h] + r[b, h] + carries[h])
                w[b, h] = (jnp.where(causal, w_c, 0.0) if diags[b] else w_c).astype(BF16)
                w_ref[h, js[b]] = w[b, h]
                carries[h] = carries[h] + (r[b, h][:, :1] + lk[b, h][:, :1])
            for b, h in chains:
                acc = acc + _dot(w[b, h], jnp.where(heads[h][0], vs[b], jnp.zeros_like(vs[b])))
            return tuple(carries), acc

        zero = jnp.zeros((SB_T, 1), F32)
        init = ((zero, zero), jnp.zeros((SB_T, LANES), F32))
        first = _sb_first_blocks(i)
        carries, acc = lax.cond(
            first == 1, lambda: blocks([i], (True,), *init),
            lambda: lax.cond(first == 2, lambda: blocks([i, i - 1], (True, False), *init),
                             lambda: blocks([i, i - 1, i - 2], (True, False, False), *init)))
        rest = i - first
        carries, acc = lax.fori_loop(
            0, (rest + 1) // 3,
            lambda jj, c: blocks([rest - 3 * jj, rest - 1 - 3 * jj, rest - 2 - 3 * jj], (False, False, False), c[0], c[1]),
            (carries, acc))
        o_ref[...] = acc.astype(BF16)
        o32_ref[...] = acc
        if n_late:
            pl.when(step == n_steps - 1)(finish)

    q_spec, k_spec, v_spec = _sb_specs(n_pairs, 0)
    blk = pl.BlockSpec((None, SB_T, LANES), lambda b, p, i: (b, i, p))
    out = pl.pallas_call(
        body,
        grid=(bl, n_pairs, nq),
        in_specs=[q_spec, k_spec, v_spec] + [ANY] * n_late,
        out_specs=[blk, blk, _sb_weight_spec(nq)] + [ANY] * n_late,
        out_shape=[jax.ShapeDtypeStruct((bl, SEQ, SB_W), BF16), jax.ShapeDtypeStruct((bl, SEQ, SB_W), F32),
                   jax.ShapeDtypeStruct((bl, n_pairs, nq, 2, nq, SB_T, SB_T), BF16)] + _gather_out_shapes(late_shards),
        scratch_shapes=_gather_sems(n_late) if n_late else [],
        compiler_params=pltpu.CompilerParams(dimension_semantics=("arbitrary", "arbitrary", "arbitrary")),
        name="sb_fwd",
    )(proj3, proj3, proj3, *late_shards)
    return out[0], out[1], out[2], out[3:]


def _sb_weight_spec(nq):
    return pl.BlockSpec((None, None, None, 2, nq, SB_T, SB_T), lambda b, p, i: (b, p, i, 0, 0, 0, 0))


def _sb_bwd(proj3, o_a, do_a, w_all, parts):
    bl = proj3.shape[0]
    n_pairs = SB_W // LANES
    nq = SEQ // SB_T
    n_parts = len(parts)
    n_steps = bl * n_pairs * nq

    def body(q_ref, k_ref, v_ref, o_ref, do_ref, w_ref, *rest):
        p_refs, (dq_ref, dk_ref, dv_ref), land_refs = rest[:n_parts], rest[n_parts:n_parts + 3], rest[n_parts + 3:2 * n_parts + 3]
        dk_acc, dv_acc = rest[2 * n_parts + 3:2 * n_parts + 5]
        step = _grid_step(n_pairs, nq)
        if n_parts:
            send, finish = _chip_exchange_phases(p_refs, land_refs, *rest[2 * n_parts + 5:])
            pl.when(step == 0)(send)
        i = pl.program_id(2)

        @pl.when(i == 0)
        def _():
            dk_acc[...] = jnp.zeros_like(dk_acc)
            dv_acc[...] = jnp.zeros_like(dv_acc)

        row, col, lane = _sb_masks()
        causal = col < row
        u_incl = (row >= col).astype(BF16)
        q = q_ref[...]
        do = do_ref[...]
        prod = do.astype(F32) * o_ref[...]
        heads = []
        for h in range(2):
            mh = (lane // HEAD_DIM) == h
            d_tot = jnp.sum(jnp.where(mh, prod, 0.0), axis=1, keepdims=True)
            heads.append((mh, jnp.where(mh, q, jnp.zeros_like(q)) * SB_SCALE, jnp.where(mh, do, jnp.zeros_like(do)), d_tot))

        def blocks(js, diags, c_das, dq):
            starts = [pl.multiple_of(j * SB_T, SB_T) for j in js]
            ks = [k_ref[pl.ds(s, SB_T), :] for s in starts]
            vs = [v_ref[pl.ds(s, SB_T), :] for s in starts]
            chains = [(b, h) for b in range(len(js)) for h in range(2)]
            z = {c: _dot_nt(heads[c[1]][1], ks[c[0]]) for c in chains}
            dw = {c: _dot_nt(heads[c[1]][2], vs[c[0]]) for c in chains}
            wb = {(b, h): w_ref[h, js[b]] for b, h in chains}
            da = {c: dw[c] * wb[c].astype(F32) for c in chains}
            sfx = {c: _split_dot(da[c], u_incl) for c in chains}
            c_das = list(c_das)
            dz = {}
            for b, h in chains:
                dlk = heads[h][3] - c_das[h] - sfx[b, h]
                if diags[b]:
                    dlk = jnp.where(causal, dlk, 0.0)
                c_das[h] = c_das[h] + sfx[b, h][:, :1]
                e = jnp.exp(-jnp.abs(z[b, h]))
                inv = 1.0 / (1.0 + e)
                pos = z[b, h] >= 0.0
                beta = jnp.where(pos, inv, e * inv)
                one_m_beta = jnp.where(pos, e * inv, inv)
                dz[b, h] = (da[b, h] * one_m_beta - dlk * beta).astype(BF16)
            for b, h in chains:
                dq = dq + _dot(dz[b, h], jnp.where(heads[h][0], ks[b], jnp.zeros_like(ks[b])))
            for b in range(len(js)):
                dk_acc[pl.ds(starts[b], SB_T), :] += _dot_tn(dz[b, 0], heads[0][1]) + _dot_tn(dz[b, 1], heads[1][1])
                dv_acc[pl.ds(starts[b], SB_T), :] += _dot_tn(wb[b, 0], heads[0][2]) + _dot_tn(wb[b, 1], heads[1][2])
            return tuple(c_das), dq

        zero = jnp.zeros((SB_T, 1), F32)
        init = ((zero, zero), jnp.zeros((SB_T, LANES), F32))
        first = _sb_first_blocks(i)
        state = lax.cond(
            first == 1, lambda: blocks([i], (True,), *init),
            lambda: lax.cond(first == 2, lambda: blocks([i, i - 1], (True, False), *init),
                             lambda: blocks([i, i - 1, i - 2], (True, False, False), *init)))
        rest = i - first
        state = lax.fori_loop(
            0, (rest + 1) // 3,
            lambda jj, c: blocks([rest - 3 * jj, rest - 1 - 3 * jj, rest - 2 - 3 * jj], (False, False, False), c[0], c[1]), state)
        dq_ref[...] = (state[1] * SB_SCALE).astype(BF16)

        @pl.when(i == nq - 1)
        def _():
            dk_ref[...] = dk_acc[...].astype(BF16)
            dv_ref[...] = dv_acc[...].astype(BF16)

        if n_parts:
            pl.when(step == n_steps - 1)(finish)

    q_spec, k_spec, v_spec = _sb_specs(n_pairs, 0)
    blk = pl.BlockSpec((None, SB_T, LANES), lambda b, p, i: (b, i, p))
    seq = pl.BlockSpec((None, SEQ, LANES), lambda b, p, i: (b, 0, p))
    shape = jax.ShapeDtypeStruct((bl, SEQ, SB_W), BF16)
    out = pl.pallas_call(
        body,
        grid=(bl, n_pairs, nq),
        in_specs=[q_spec, k_spec, v_spec, blk, blk, _sb_weight_spec(nq)] + [ANY] * n_parts,
        out_specs=[blk, seq, seq] + [ANY] * n_parts,
        out_shape=[shape, shape, shape] + [jax.ShapeDtypeStruct(p.shape, p.dtype) for p in parts],
        scratch_shapes=[pltpu.VMEM((SEQ, LANES), F32), pltpu.VMEM((SEQ, LANES), F32)]
        + (_chip_exchange_sems(n_parts) if n_parts else []),
        compiler_params=pltpu.CompilerParams(dimension_semantics=("arbitrary", "arbitrary", "arbitrary")),
        name="sb_bwd",
    )(proj3, proj3, proj3, o_a, do_a, w_all, *parts)
    return out[0], out[1], out[2], out[3:]


BAND = 128


BAND_CH = 4
BAND_HEADS = DIL_W // HEAD_DIM


def _swap_half(x):
    n = x.shape[-1]
    lane = lax.broadcasted_iota(jnp.int32, (1, n), 1)
    return jnp.where((lane % HEAD_DIM) < HEAD_DIM // 2, pltpu.roll(x, n - HEAD_DIM // 2, 1), pltpu.roll(x, HEAD_DIM // 2, 1))


def _rope(x, cos, sin_signed):
    x = x.astype(F32)
    return x * cos + _swap_half(x) * sin_signed


def _band_valid(g, blk):
    nb = jnp.where(g == 0, 16, jnp.where(g == 1, 4, 1))
    first_key = jnp.where(lax.rem(blk, nb) != 0, 0, BAND)
    qi = lax.broadcasted_iota(jnp.int32, (BAND, 2 * BAND), 0) + BAND
    kj = lax.broadcasted_iota(jnp.int32, (BAND, 2 * BAND), 1)
    dist = qi - kj
    return (dist >= 0) & (dist <= BAND) & (kj >= first_key)


def _band_specs():
    last_before = lambda i: jnp.maximum(i * BAND_CH - 1, 0)
    cur = lambda col: pl.BlockSpec((None, BAND_CH, BAND, DIL_W), lambda g, i: (g, i, 0, col))
    prev = lambda col: pl.BlockSpec((None, None, BAND, DIL_W), lambda g, i: (g, last_before(i), 0, col))
    tab = pl.BlockSpec((None, BAND_CH, BAND, DIL_W), lambda g, i: (g, lax.rem(i, 16 // BAND_CH), 0, 0))
    tab_prev = pl.BlockSpec((None, None, BAND, DIL_W), lambda g, i: (g, lax.rem(last_before(i), 16), 0, 0))
    return cur, prev, tab, tab_prev


def _band_load(q_ref, k_ref, kp_ref, v_ref, vp_ref, c_ref, s_ref, cp_ref, sp_ref):
    qs = [(_rope(q_ref[b], c_ref[b], s_ref[b]) * SB_SCALE).astype(BF16) for b in range(BAND_CH)]
    ks = [_rope(kp_ref[...], cp_ref[...], sp_ref[...]).astype(BF16)]
    ks += [_rope(k_ref[b], c_ref[b], s_ref[b]).astype(BF16) for b in range(BAND_CH)]
    vs = [vp_ref[...]] + [v_ref[b] for b in range(BAND_CH)]
    k2 = [jnp.concatenate([ks[b], ks[b + 1]], axis=0) for b in range(BAND_CH)]
    v2 = [jnp.concatenate([vs[b], vs[b + 1]], axis=0) for b in range(BAND_CH)]
    return qs, k2, v2


def _band_fwd(qkv_s, cos_t, sin_t):
    def body(q_ref, k_ref, kp_ref, v_ref, vp_ref, c_ref, s_ref, cp_ref, sp_ref, ol_ref):
        g, i = pl.program_id(0), pl.program_id(1)
        qs, k2, v2 = _band_load(q_ref, k_ref, kp_ref, v_ref, vp_ref, c_ref, s_ref, cp_ref, sp_ref)
        lane = lax.broadcasted_iota(jnp.int32, (1, DIL_W), 1)
        for b in range(BAND_CH):
            valid = _band_valid(g, i * BAND_CH + b)
            hs = range(BAND_HEADS)
            mh = [(lane // HEAD_DIM) == h for h in hs]
            s = [jnp.where(valid, _dot_nt(jnp.where(mh[h], qs[b], jnp.zeros_like(qs[b])), k2[b]), NEG_INF) for h in hs]
            m = [jnp.max(s[h], axis=1, keepdims=True) for h in hs]
            p = [jnp.exp(s[h] - m[h]) for h in hs]
            den = [jnp.sum(p[h], axis=1, keepdims=True) for h in hs]
            pv = [_dot(p[h].astype(BF16), jnp.where(mh[h], v2[b], jnp.zeros_like(v2[b]))) for h in hs]
            o = jnp.zeros((BAND, DIL_W), F32)
            lse = jnp.zeros((BAND, DIL_W), F32)
            for h in hs:
                o = o + pv[h] * (1.0 / den[h])
                lse = jnp.where(mh[h], m[h] + jnp.log(den[h]), lse)
            ol_ref[b, :, :DIL_W] = o
            ol_ref[b, :, DIL_W:] = lse

    cur, prev, tab, tab_prev = _band_specs()
    n_blk = qkv_s.shape[1]
    return pl.pallas_call(
        body,
        grid=(3, n_blk // BAND_CH),
        in_specs=[cur(0), cur(1), prev(1), cur(2), prev(2), tab, tab, tab_prev, tab_prev],
        out_specs=pl.BlockSpec((None, BAND_CH, BAND, 2 * DIL_W), lambda g, i: (g, i, 0, 0)),
        out_shape=jax.ShapeDtypeStruct((3, n_blk, BAND, 2 * DIL_W), F32),
        compiler_params=pltpu.CompilerParams(dimension_semantics=("parallel", "parallel")),
        name="band_fwd",
    )(qkv_s, qkv_s, qkv_s, qkv_s, qkv_s, cos_t, sin_t, cos_t, sin_t)


def _band_bwd(qkv_s, cos_t, sin_t, dcat_s):
    def body(q_ref, k_ref, kp_ref, v_ref, vp_ref, c_ref, s_ref, cp_ref, sp_ref, do_ref, lse_ref, dl_ref,
             dq_ref, dk_ref, dv_ref, dkf_ref, dvf_ref):
        g, i = pl.program_id(0), pl.program_id(1)
        qs, k2, v2 = _band_load(q_ref, k_ref, kp_ref, v_ref, vp_ref, c_ref, s_ref, cp_ref, sp_ref)
        lane = lax.broadcasted_iota(jnp.int32, (1, DIL_W), 1)
        dks, dvs = [], []
        for b in range(BAND_CH):
            valid = _band_valid(g, i * BAND_CH + b)
            do, lse, dl = do_ref[b].astype(BF16), lse_ref[b], dl_ref[b]
            hs = range(BAND_HEADS)
            mh = [(lane // HEAD_DIM) == h for h in hs]
            qh = [jnp.where(mh[h], qs[b], jnp.zeros_like(qs[b])) for h in hs]
            doh = [jnp.where(mh[h], do, jnp.zeros_like(do)) for h in hs]
            s = [_dot_nt(qh[h], k2[b]) for h in hs]
            dp = [_dot_nt(doh[h], v2[b]) for h in hs]
            p = [jnp.where(valid, jnp.exp(s[h] - lse[:, h * HEAD_DIM:h * HEAD_DIM + 1]), 0.0) for h in hs]
            ds = [(p[h] * (dp[h] - dl[:, h * HEAD_DIM:h * HEAD_DIM + 1])).astype(BF16) for h in hs]
            pb = [p[h].astype(BF16) for h in hs]
            dq = sum(_dot(ds[h], jnp.where(mh[h], k2[b], jnp.zeros_like(k2[b]))) for h in hs)
            dk2 = sum(_dot_tn(ds[h], qh[h]) for h in hs)
            dv2 = sum(_dot_tn(pb[h], doh[h]) for h in hs)
            dq_ref[b] = dq * SB_SCALE
            dks.append(dk2)
            dvs.append(dv2)
        dkf_ref[...] = dks[0][:BAND]
        dvf_ref[...] = dvs[0][:BAND]
        for b in range(BAND_CH):
            last = b == BAND_CH - 1
            dk_ref[b] = dks[b][BAND:] if last else dks[b][BAND:] + dks[b + 1][:BAND]
            dv_ref[b] = dvs[b][BAND:] if last else dvs[b][BAND:] + dvs[b + 1][:BAND]

    cur, prev, tab, tab_prev = _band_specs()
    first = pl.BlockSpec((None, None, BAND, DIL_W), lambda g, i: (g, i, 0, 0))
    n_blk = qkv_s.shape[1]
    n_chunks = n_blk // BAND_CH
    shape = jax.ShapeDtypeStruct((3, n_blk, BAND, DIL_W), F32)
    shape_first = jax.ShapeDtypeStruct((3, n_chunks, BAND, DIL_W), F32)
    return pl.pallas_call(
        body,
        grid=(3, n_chunks),
        in_specs=[cur(0), cur(1), prev(1), cur(2), prev(2), tab, tab, tab_prev, tab_prev, cur(0), cur(1), cur(2)],
        out_specs=[cur(0), cur(0), cur(0), first, first],
        out_shape=[shape, shape, shape, shape_first, shape_first],
        compiler_params=pltpu.CompilerParams(dimension_semantics=("parallel", "parallel")),
        name="band_bwd",
    )(qkv_s, qkv_s, qkv_s, qkv_s, qkv_s, cos_t, sin_t, cos_t, sin_t, dcat_s, dcat_s, dcat_s)


def _band_combine(dq, dk, dv, dk_first, dv_first, cos_t, sin_t):
    n_chunks = dk_first.shape[1]

    def body(dq_ref, dk_ref, dkn_ref, dv_ref, dvn_ref, c_ref, s_ref, out_ref):
        nxt = (pl.program_id(1) < n_chunks - 1).astype(F32)
        for b in range(BAND_CH):
            cos, sin = c_ref[b], s_ref[b]
            dq_b, dk_b, dv_b = dq_ref[b], dk_ref[b], dv_ref[b]
            if b == BAND_CH - 1:
                dk_b = dk_b + nxt * dkn_ref[...]
                dv_b = dv_b + nxt * dvn_ref[...]
            out_ref[b, :, :DIL_W] = (dq_b * cos - _swap_half(dq_b) * sin).astype(BF16)
            out_ref[b, :, DIL_W:2 * DIL_W] = (dk_b * cos - _swap_half(dk_b) * sin).astype(BF16)
            out_ref[b, :, 2 * DIL_W:] = dv_b.astype(BF16)

    cur, _, tab, _ = _band_specs()
    nxt = pl.BlockSpec((None, None, BAND, DIL_W), lambda g, i: (g, jnp.minimum(i + 1, n_chunks - 1), 0, 0))
    return pl.pallas_call(
        body,
        grid=(3, n_chunks),
        in_specs=[cur(0), cur(0), nxt, cur(0), nxt, tab, tab],
        out_specs=pl.BlockSpec((None, BAND_CH, BAND, 3 * DIL_W), lambda g, i: (g, i, 0, 0)),
        out_shape=jax.ShapeDtypeStruct(dq.shape[:3] + (3 * DIL_W,), BF16),
        compiler_params=pltpu.CompilerParams(dimension_semantics=("parallel", "parallel")),
        name="band_combine",
    )(dq, dk, dk_first, dv, dv_first, cos_t, sin_t)


def _band_merge(ol):
    t, tm = ol.shape[1], 512

    def body(o_ref, l_ref, ob_ref, lse_ref):
        l0, l1, l2 = l_ref[0], l_ref[1], l_ref[2]
        m = jnp.maximum(jnp.maximum(l0, l1), l2)
        lse = m + jnp.log(jnp.exp(l0 - m) + jnp.exp(l1 - m) + jnp.exp(l2 - m))
        ob_ref[...] = (jnp.exp(l0 - lse) * o_ref[0] + jnp.exp(l1 - lse) * o_ref[1] + jnp.exp(l2 - lse) * o_ref[2]).astype(BF16)
        lse_ref[...] = lse

    spec = pl.BlockSpec((tm, DIL_W), lambda i: (i, 0))
    return pl.pallas_call(
        body,
        grid=(t // tm,),
        in_specs=[pl.BlockSpec((3, tm, DIL_W), lambda i: (0, i, 0)), pl.BlockSpec((3, tm, DIL_W), lambda i: (0, i, 1))],
        out_specs=[spec, spec],
        out_shape=[jax.ShapeDtypeStruct((t, DIL_W), BF16), jax.ShapeDtypeStruct((t, DIL_W), F32)],
        compiler_params=pltpu.CompilerParams(dimension_semantics=("parallel",)),
        name="band_merge",
    )(ol, ol)


def _band_delta(do_b, o_b, lse_b):
    def fn(do, o, lse):
        lane_in = lax.broadcasted_iota(jnp.int32, (DIL_W, LANES), 0)
        col = lax.broadcasted_iota(jnp.int32, (DIL_W, LANES), 1)
        sum_head = ((lane_in // HEAD_DIM == col - BAND_HEADS) & (col >= BAND_HEADS) & (col < 2 * BAND_HEADS)).astype(BF16)
        return (lse + _split_dot(do.astype(F32) * o.astype(F32), sum_head),)

    return _rowwise(fn, [(do_b, "row"), (o_b, "row"), (lse_b, "row")], [(LANES, F32, "row")], tm=512, name="band_delta")[0]


def _band_masks():
    qi = lax.broadcasted_iota(jnp.int32, (BAND, 2 * BAND), 0) + BAND
    kj = lax.broadcasted_iota(jnp.int32, (BAND, 2 * BAND), 1)
    dist = qi - kj
    row = lax.broadcasted_iota(jnp.int32, (BAND, BAND), 0)
    col = lax.broadcasted_iota(jnp.int32, (BAND, BAND), 1)
    return col <= row, (dist >= 0) & (dist <= BAND)


def _band_attend(q, k, v, valid):
    lane = lax.broadcasted_iota(jnp.int32, (1, DIL_W), 1)
    stat_lane = lax.broadcasted_iota(jnp.int32, (1, LANES), 1)
    hs = range(BAND_HEADS)
    mh = [(lane // HEAD_DIM) == h for h in hs]
    s = [jnp.where(valid, _dot_nt(jnp.where(mh[h], q, jnp.zeros_like(q)), k), NEG_INF) for h in hs]
    m = [jnp.max(s[h], axis=1, keepdims=True) for h in hs]
    p = [jnp.exp(s[h] - m[h]) for h in hs]
    den = [jnp.sum(p[h], axis=1, keepdims=True) for h in hs]
    pv = [_dot(p[h].astype(BF16), jnp.where(mh[h], v, jnp.zeros_like(v))) for h in hs]
    o = jnp.zeros((BAND, DIL_W), F32)
    lse = jnp.zeros((BAND, LANES), F32)
    for h in hs:
        o = o + pv[h] * (1.0 / den[h])
        lse = jnp.where(stat_lane == h, m[h] + jnp.log(den[h]), lse)
    return o, lse


def _band_attend_bwd(q, k, v, valid, do, st):
    lane = lax.broadcasted_iota(jnp.int32, (1, DIL_W), 1)
    hs = range(BAND_HEADS)
    mh = [(lane // HEAD_DIM) == h for h in hs]
    qh = [jnp.where(mh[h], q, jnp.zeros_like(q)) for h in hs]
    doh = [jnp.where(mh[h], do, jnp.zeros_like(do)) for h in hs]
    s = [_dot_nt(qh[h], k) for h in hs]
    dp = [_dot_nt(doh[h], v) for h in hs]
    p = [jnp.where(valid, jnp.exp(s[h] - st[:, h:h + 1]), 0.0) for h in hs]
    ds = [(p[h] * (dp[h] - st[:, BAND_HEADS + h:BAND_HEADS + h + 1])).astype(BF16) for h in hs]
    pb = [p[h].astype(BF16) for h in hs]
    dq = sum(_dot(ds[h], jnp.where(mh[h], k, jnp.zeros_like(k))) for h in hs)
    dk = sum(_dot_tn(ds[h], qh[h]) for h in hs)
    dv = sum(_dot_tn(pb[h], doh[h]) for h in hs)
    return dq, dk, dv


def _band_group_specs(lead, rows, cls, col0):
    def spec(width):
        if lead == "rows":
            return pl.BlockSpec((None, rows, width), lambda b, i: (b, 0, col0))
        return pl.BlockSpec((None, rows, cls * width), lambda b, i: (b, 0, i))
    return spec


def _band_group_fwd(a, cos_g, sin_g, *, rows, cls, steps, col0, name):
    bl = a.shape[0]
    nb = rows // BAND
    grp_w = 3 * DIL_W

    def body(a_ref, c_ref, s_ref, o_ref, l_ref, qr_all, kr_all):
        first_valid, later_valid = _band_masks()
        for j in range(cls):
            qr, kr = qr_all.at[j], kr_all.at[j]
            a0, t0, s0 = j * grp_w, j * DIL_W, j * LANES
            cos, sin = c_ref[:, t0:t0 + DIL_W], s_ref[:, t0:t0 + DIL_W]
            qr[...] = (_rope(a_ref[:, a0:a0 + DIL_W], cos, sin) * SB_SCALE).astype(BF16)
            kr[...] = _rope(a_ref[:, a0 + DIL_W:a0 + 2 * DIL_W], cos, sin).astype(BF16)

            def block(q0, k0, keys, valid, a0=a0, t0=t0, s0=s0):
                o, lse = _band_attend(qr[pl.ds(q0, BAND), :], kr[pl.ds(k0, keys), :],
                                      a_ref[pl.ds(k0, keys), a0 + 2 * DIL_W:a0 + grp_w], valid)
                o_ref[pl.ds(q0, BAND), t0:t0 + DIL_W] = o.astype(BF16)
                l_ref[pl.ds(q0, BAND), s0:s0 + LANES] = lse

            block(0, 0, BAND, first_valid)
            if nb > 1:
                def later(b, carry, block=block):
                    block(pl.multiple_of(b * BAND, BAND), pl.multiple_of((b - 1) * BAND, BAND), 2 * BAND, later_valid)
                    return carry

                lax.fori_loop(1, nb, later, 0, unroll=3)

    lead = "rows" if col0 is not None else "cols"
    spec = _band_group_specs(lead, rows, cls, col0)
    tab = pl.BlockSpec((rows, cls * DIL_W), lambda b, i: (0, i))
    n_cls = cos_g.shape[1] // DIL_W
    return pl.pallas_call(
        body,
        grid=(bl, steps),
        in_specs=[spec(grp_w), tab, tab],
        out_specs=[pl.BlockSpec((None, rows, cls * DIL_W), lambda b, i: (b, 0, i)),
                   pl.BlockSpec((None, rows, cls * LANES), lambda b, i: (b, 0, i))],
        out_shape=[jax.ShapeDtypeStruct((bl, rows, n_cls * DIL_W), BF16), jax.ShapeDtypeStruct((bl, rows, n_cls * LANES), F32)],
        scratch_shapes=[pltpu.VMEM((cls, rows, DIL_W), BF16), pltpu.VMEM((cls, rows, DIL_W), BF16)],
        compiler_params=pltpu.CompilerParams(dimension_semantics=("parallel", "parallel")),
        name=name,
    )(a, cos_g, sin_g)


def _band_group_bwd(a, do, st, cos_g, sin_g, *, rows, cls, steps, col0, name, side=None):
    bl = a.shape[0]
    nb = rows // BAND
    grp_w = 3 * DIL_W
    side_arrays = side[1] if side else []
    n_side = len(side_arrays)

    def body(a_ref, do_ref, st_ref, c_ref, s_ref, *rest):
        out_ref = rest[n_side]
        scratch = rest[2 * n_side + 1:2 * n_side + 5]
        if n_side:
            step = pl.program_id(0) * steps + pl.program_id(1)
            finish = _run_side(side, rest[:n_side], rest[n_side + 1:2 * n_side + 1], rest[2 * n_side + 5:], step, bl * steps)
        first_valid, later_valid = _band_masks()
        for j in range(cls):
            qr, kr, dk_acc, dv_acc = (s.at[j] for s in scratch)
            a0, t0 = j * grp_w, j * DIL_W
            cos, sin = c_ref[:, t0:t0 + DIL_W], s_ref[:, t0:t0 + DIL_W]
            qr[...] = (_rope(a_ref[:, a0:a0 + DIL_W], cos, sin) * SB_SCALE).astype(BF16)
            kr[...] = _rope(a_ref[:, a0 + DIL_W:a0 + 2 * DIL_W], cos, sin).astype(BF16)
            dk_acc[...] = jnp.zeros_like(dk_acc)
            dv_acc[...] = jnp.zeros_like(dv_acc)

            def block(q0, k0, keys, valid, a0=a0, t0=t0, s0=j * LANES):
                qrows, krows = pl.ds(q0, BAND), pl.ds(k0, keys)
                dq, dk, dv = _band_attend_bwd(
                    qr[qrows, :], kr[krows, :], a_ref[krows, a0 + 2 * DIL_W:a0 + grp_w], valid,
                    do_ref[qrows, t0:t0 + DIL_W], st_ref[qrows, s0:s0 + LANES])
                dq = dq * SB_SCALE
                out_ref[qrows, a0:a0 + DIL_W] = (dq * c_ref[qrows, t0:t0 + DIL_W]
                                                 - _swap_half(dq) * s_ref[qrows, t0:t0 + DIL_W]).astype(BF16)
                dk_acc[krows, :] += dk
                dv_acc[krows, :] += dv

            block(0, 0, BAND, first_valid)
            if nb > 1:
                def later(b, carry, block=block):
                    block(pl.multiple_of(b * BAND, BAND), pl.multiple_of((b - 1) * BAND, BAND), 2 * BAND, later_valid)
                    return carry

                lax.fori_loop(1, nb, later, 0, unroll=3)
            dk = dk_acc[...]
            out_ref[:, a0 + DIL_W:a0 + 2 * DIL_W] = (dk * cos - _swap_half(dk) * sin).astype(BF16)
            out_ref[:, a0 + 2 * DIL_W:a0 + grp_w] = dv_acc[...].astype(BF16)
        if n_side:
            finish()

    lead = "rows" if col0 is not None else "cols"
    spec = _band_group_specs(lead, rows, cls, col0)
    dspec = _band_group_specs(lead, rows, cls, 0 if col0 is not None else None)
    tab = pl.BlockSpec((rows, cls * DIL_W), lambda b, i: (0, i))
    n_cls = cos_g.shape[1] // DIL_W
    out = pl.pallas_call(
        body,
        grid=(bl, steps),
        in_specs=[spec(grp_w), dspec(DIL_W), dspec(LANES), tab, tab] + [ANY] * n_side,
        out_specs=[pl.BlockSpec((None, rows, cls * grp_w), lambda b, i: (b, 0, i))] + [ANY] * n_side,
        out_shape=[jax.ShapeDtypeStruct((bl, rows, n_cls * grp_w), BF16)] + (_side_out_shapes(side) if n_side else []),
        scratch_shapes=[pltpu.VMEM((cls, rows, DIL_W), BF16), pltpu.VMEM((cls, rows, DIL_W), BF16),
                        pltpu.VMEM((cls, rows, DIL_W), F32), pltpu.VMEM((cls, rows, DIL_W), F32)] + (_side_sems(side) if n_side else []),
        compiler_params=pltpu.CompilerParams(dimension_semantics=("arbitrary", "arbitrary") if n_side else ("parallel", "parallel")),
        name=name,
    )(a, do, st, cos_g, sin_g, *side_arrays)
    return (out[0], out[1:]) if n_side else out[0]


def _band_merge3(groups):
    t, tm = groups[0][0].shape[0], 512

    def body(o0, l0, o1, l1, o2, l2, ob_ref, lse_ref):
        a, b, c = l0[...], l1[...], l2[...]
        m = jnp.maximum(jnp.maximum(a, b), c)
        lse = m + jnp.log(jnp.exp(a - m) + jnp.exp(b - m) + jnp.exp(c - m))
        lane = lax.broadcasted_iota(jnp.int32, (1, DIL_W), 1)
        acc = jnp.zeros((tm, DIL_W), F32)
        for o_ref, l in ((o0, a), (o1, b), (o2, c)):
            share = jnp.exp(l - lse)
            spread = jnp.zeros((tm, DIL_W), F32)
            for h in range(BAND_HEADS):
                spread = jnp.where(lane // HEAD_DIM == h, share[:, h:h + 1], spread)
            acc = acc + spread * o_ref[...].astype(F32)
        ob_ref[...] = acc.astype(BF16)
        stat_lane = lax.broadcasted_iota(jnp.int32, (1, LANES), 1)
        lse_ref[...] = jnp.where(stat_lane < BAND_HEADS, lse, 0.0)

    spec = pl.BlockSpec((tm, DIL_W), lambda i: (i, 0))
    spec_l = pl.BlockSpec((tm, LANES), lambda i: (i, 0))
    return pl.pallas_call(
        body,
        grid=(t // tm,),
        in_specs=[spec, spec_l] * 3,
        out_specs=[spec, spec_l],
        out_shape=[jax.ShapeDtypeStruct((t, DIL_W), BF16), jax.ShapeDtypeStruct((t, LANES), F32)],
        compiler_params=pltpu.CompilerParams(dimension_semantics=("parallel",)),
        name="band_merge",
    )(*[a for g in groups for a in g])


MEM_T = 512
MEM_SCALE = 128 ** -0.5
MEM_Q_COL = (D_IN - MEM_W) // LANES


MEM_HEADS = MEM_W // LANES


def _mem_specs():
    qs = [pl.BlockSpec((None, MEM_T, LANES), lambda b, i, h=h: (b, i, MEM_Q_COL + h)) for h in range(MEM_HEADS)]
    kv = pl.BlockSpec((None, MEM_LEN, 2 * MEM_W), lambda b, i: (b, 0, 0))
    blk = pl.BlockSpec((None, MEM_T, MEM_W), lambda b, i: (b, i, 0))
    return qs, kv, blk


def _mem_probs(q, k):
    s = _dot_nt(q, k) * MEM_SCALE
    p = jnp.exp(s - jnp.max(s, axis=1, keepdims=True))
    return p * (1.0 / jnp.sum(p, axis=1, keepdims=True))


def _head_cols(h, base=0):
    return slice(base + h * LANES, base + (h + 1) * LANES)


def _mem_fwd(proj3, kv3):
    bl = proj3.shape[0]
    hs = range(MEM_HEADS)

    def body(*refs):
        q_refs, kv_ref, o_ref = refs[:MEM_HEADS], refs[MEM_HEADS], refs[MEM_HEADS + 1]
        p = [_mem_probs(q_refs[h][...], kv_ref[:, _head_cols(h)]) for h in hs]
        for h in hs:
            o_ref[:, _head_cols(h)] = _dot(p[h].astype(BF16), kv_ref[:, _head_cols(h, MEM_W)]).astype(BF16)

    qs, kv, blk = _mem_specs()
    return pl.pallas_call(
        body,
        grid=(bl, SEQ // MEM_T),
        in_specs=qs + [kv],
        out_specs=blk,
        out_shape=jax.ShapeDtypeStruct((bl, SEQ, MEM_W), BF16),
        compiler_params=pltpu.CompilerParams(dimension_semantics=("parallel", "parallel")),
        name="mem_fwd",
    )(*([proj3] * MEM_HEADS), kv3)


def _mem_bwd(proj3, kv3, do_c):
    bl = proj3.shape[0]
    hs = range(MEM_HEADS)

    def body(*refs):
        q_refs, kv_ref, do_ref, dq_ref, dkv_ref = refs[:MEM_HEADS], *refs[MEM_HEADS:MEM_HEADS + 4]

        @pl.when(pl.program_id(1) == 0)
        def _():
            dkv_ref[...] = jnp.zeros_like(dkv_ref)

        q = [q_refs[h][...] for h in hs]
        do = [do_ref[:, _head_cols(h)] for h in hs]
        p = [_mem_probs(q[h], kv_ref[:, _head_cols(h)]) for h in hs]
        dp = [_dot_nt(do[h], kv_ref[:, _head_cols(h, MEM_W)]) for h in hs]
        ds = [(p[h] * (dp[h] - jnp.sum(p[h] * dp[h], axis=1, keepdims=True)) * MEM_SCALE).astype(BF16) for h in hs]
        for h in hs:
            dq_ref[:, _head_cols(h)] = _dot(ds[h], kv_ref[:, _head_cols(h)]).astype(BF16)
            dkv_ref[:, _head_cols(h)] += _dot_tn(ds[h], q[h])
            dkv_ref[:, _head_cols(h, MEM_W)] += _dot_tn(p[h].astype(BF16), do[h])

    qs, kv, blk = _mem_specs()
    return pl.pallas_call(
        body,
        grid=(bl, SEQ // MEM_T),
        in_specs=qs + [kv, blk],
        out_specs=[blk, kv],
        out_shape=[jax.ShapeDtypeStruct((bl, SEQ, MEM_W), BF16), jax.ShapeDtypeStruct((bl, MEM_LEN, 2 * MEM_W), F32)],
        compiler_params=pltpu.CompilerParams(dimension_semantics=("parallel", "arbitrary")),
        name="mem_bwd",
    )(*([proj3] * MEM_HEADS), kv3, do_c)


def _place():
    x, y, c = lax.axis_index("x"), lax.axis_index("y"), lax.axis_index("c")
    return x, y, c


def _other_chips(x, y):
    return [(1 - x, y), (x, 1 - y), (1 - x, 1 - y)]


def _remote(src, dst, send_sem, recv_sem, to):
    return pltpu.make_async_remote_copy(src_ref=src, dst_ref=dst, send_sem=send_sem, recv_sem=recv_sem,
                                        device_id=to, device_id_type=MESH)


ANY = pl.BlockSpec(memory_space=pl.ANY)


def _gather_weights(shards):
    n = len(shards)

    def body(*refs):
        send, forward, finish = _gather_phases(refs[:n], refs[n:2 * n], *refs[2 * n:])
        send()
        forward()
        finish()

    return pl.pallas_call(
        body,
        in_specs=[ANY] * n,
        out_specs=[ANY] * n,
        out_shape=_gather_out_shapes(shards),
        scratch_shapes=_gather_sems(n),
        name="gather_weights",
    )(*shards)


def _gather_out_shapes(shards):
    return [jax.ShapeDtypeStruct((N_CHIPS,) + s.shape, s.dtype) for s in shards]


def _gather_sems(n):
    return [pltpu.SemaphoreType.DMA((6 * n,)), pltpu.SemaphoreType.DMA((6 * n,))]


def _gather_phases(in_refs, out_refs, send_sems, recv_sems):
    x, y, c = _place()
    sibling = (x, y, 1 - c)
    chips = _other_chips(x, y)
    first, passed = [], []
    for k in range(len(in_refs)):
        hf = in_refs[k].shape[0] // 2

        def half(px, py, pc, k=k, hf=hf):
            return out_refs[k].at[2 * px + py, pl.ds(pc * hf, hf), :]

        src = in_refs[k].at[pl.ds(c * hf, hf), :]
        for j, chip in enumerate(chips):
            s = 6 * k + j
            first.append(_remote(src, half(x, y, c), send_sems.at[s], recv_sems.at[s], (*chip, c)))
            passed.append((_remote(src, half(*chip, c), send_sems.at[s], recv_sems.at[s], (*chip, c)),
                           _remote(half(*chip, c), half(*chip, c), send_sems.at[s + 3], recv_sems.at[s + 3], sibling),
                           _remote(src, half(*chip, 1 - c), send_sems.at[s + 3], recv_sems.at[s + 3], sibling)))

    def send():
        for cp in first:
            cp.start()

    def forward():
        for landed, fwd, _ in passed:
            landed.wait_recv()
            fwd.start()

    def finish():
        for _, _, from_sibling in passed:
            from_sibling.wait_recv()
        for cp in first:
            cp.wait_send()
        for _, fwd, _ in passed:
            fwd.wait_send()

    return send, forward, finish


def _pair_exchange(grads, *, name):
    n = len(grads)
    side = ("pair", grads)

    def body(*refs):
        send, _, finish = _side_phases(side, refs[:n], refs[n:2 * n], refs[2 * n:])
        send()
        finish()

    return pl.pallas_call(
        body,
        in_specs=[ANY] * n,
        out_specs=[ANY] * n,
        out_shape=_side_out_shapes(side),
        scratch_shapes=_side_sems(side),
        name=name,
    )(*grads)


def _pair_exchange_phases(g_refs, land_refs, send_sems, recv_sems):
    x, y, c = _place()
    cps = []
    for k in range(len(g_refs)):
        hf = g_refs[k].shape[1] // 2
        src = g_refs[k].at[:, pl.ds((1 - c) * hf, hf), :]
        cps.append(_remote(src, land_refs[k], send_sems.at[k], recv_sems.at[k], (x, y, 1 - c)))

    def send():
        for cp in cps:
            cp.start()

    def finish():
        for cp in cps:
            cp.wait()

    return send, finish


def _side_out_shapes(side):
    kind, arrays = side
    if kind == "gather":
        return _gather_out_shapes(arrays)
    if kind == "pair":
        return [jax.ShapeDtypeStruct((N_CHIPS, g.shape[1] // 2, g.shape[2]), g.dtype) for g in arrays]
    return [jax.ShapeDtypeStruct(p.shape, p.dtype) for p in arrays]


def _side_sems(side):
    kind, arrays = side
    n = len(arrays)
    if kind == "gather":
        return _gather_sems(n)
    if kind == "pair":
        return [pltpu.SemaphoreType.DMA((n,)), pltpu.SemaphoreType.DMA((n,))]
    return _chip_exchange_sems(n)


def _side_phases(side, in_refs, out_refs, sems):
    kind = side[0]
    if kind == "gather":
        return _gather_phases(in_refs, out_refs, *sems)
    send, finish = (_pair_exchange_phases if kind == "pair" else _chip_exchange_phases)(in_refs, out_refs, *sems)
    return send, None, finish


def _run_side(side, in_refs, out_refs, sems, step, n_steps):
    first, mid, last = _side_phases(side, in_refs, out_refs, sems)
    pl.when(step == 0)(first)
    if mid is not None:
        pl.when(step == n_steps // 2)(mid)
    return lambda: pl.when(step == n_steps - 1)(last)


def _pair_add(g, land, c_arr, *, name):
    _, a, b = g.shape
    hf = a // 2

    def body(c_ref, g_ref, l_ref, o_ref):
        o_ref[...] = (g_ref[...] + l_ref[...]).astype(BF16)

    return pl.pallas_call(
        body,
        grid_spec=pltpu.PrefetchScalarGridSpec(
            num_scalar_prefetch=1,
            grid=(N_CHIPS,),
            in_specs=[pl.BlockSpec((None, None, hf, b), lambda s, c_ref: (s, c_ref[0], 0, 0)),
                      pl.BlockSpec((None, hf, b), lambda s, c_ref: (s, 0, 0))],
            out_specs=pl.BlockSpec((None, hf, b), lambda s, c_ref: (s, 0, 0)),
        ),
        out_shape=jax.ShapeDtypeStruct((N_CHIPS, hf, b), BF16),
        compiler_params=pltpu.CompilerParams(dimension_semantics=("parallel",)),
        name=name,
    )(c_arr, g.reshape(N_CHIPS, 2, hf, b), land)


def _chip_exchange(parts):
    n = len(parts)

    def body(*refs):
        send, finish = _chip_exchange_phases(refs[:n], refs[n:2 * n], *refs[2 * n:])
        send()
        finish()

    return pl.pallas_call(
        body,
        in_specs=[ANY] * n,
        out_specs=[ANY] * n,
        out_shape=[jax.ShapeDtypeStruct(p.shape, p.dtype) for p in parts],
        scratch_shapes=_chip_exchange_sems(n),
        name="chip_exchange",
    )(*parts)


def _chip_exchange_sems(n):
    return [pltpu.SemaphoreType.DMA((3 * n,)), pltpu.SemaphoreType.DMA((3 * n,))]


def _chip_exchange_phases(p_refs, land_refs, send_sems, recv_sems):
    x, y, c = _place()
    me = 2 * x + y
    sends, recvs = [], []
    for k in range(len(p_refs)):
        for j, (cx, cy) in enumerate(_other_chips(x, y)):
            s = 3 * k + j
            sends.append(_remote(p_refs[k].at[2 * cx + cy], land_refs[k].at[me], send_sems.at[s], recv_sems.at[s], (cx, cy, c)))
            recvs.append(_remote(p_refs[k].at[me], land_refs[k].at[2 * cx + cy], send_sems.at[s], recv_sems.at[s], (cx, cy, c)))

    def send():
        for cp in sends:
            cp.start()

    def finish():
        for cp in recvs:
            cp.wait_recv()
        for cp in sends:
            cp.wait_send()

    return send, finish


def _chip_add(land, part, me_arr, *, name):
    _, r, b = land.shape

    def body(me_ref, p_ref, l1_ref, l2_ref, l3_ref, o_ref):
        o_ref[...] = ((p_ref[...].astype(F32) + l1_ref[...].astype(F32)) + l2_ref[...].astype(F32)) + l3_ref[...].astype(F32)

    tr = r // 2
    other = lambda j: pl.BlockSpec((None, tr, b), lambda i, me_ref: (jnp.bitwise_xor(me_ref[0], j), i, 0))
    return pl.pallas_call(
        body,
        grid_spec=pltpu.PrefetchScalarGridSpec(
            num_scalar_prefetch=1,
            grid=(r // tr,),
            in_specs=[pl.BlockSpec((None, tr, b), lambda i, me_ref: (me_ref[0], i, 0)), other(2), other(1), other(3)],
            out_specs=pl.BlockSpec((tr, b), lambda i, me_ref: (i, 0)),
        ),
        out_shape=jax.ShapeDtypeStruct((r, b), F32),
        compiler_params=pltpu.CompilerParams(dimension_semantics=("parallel",)),
        name=name,
    )(me_arr, part, land, land, land)


def _pair_share(halves):
    n = len(halves)

    def body(*refs):
        h_refs, out_refs = refs[:n], refs[n:2 * n]
        send_sems, recv_sems = refs[2 * n:]
        x, y, c = _place()
        cps = [_remote(h_refs[k], out_refs[k], send_sems.at[k], recv_sems.at[k], (x, y, 1 - c)) for k in range(n)]
        for cp in cps:
            cp.start()
        for cp in cps:
            cp.wait()

    return pl.pallas_call(
        body,
        in_specs=[ANY] * n,
        out_specs=[ANY] * n,
        out_shape=[jax.ShapeDtypeStruct(h.shape, F32) for h in halves],
        scratch_shapes=[pltpu.SemaphoreType.DMA((n,)), pltpu.SemaphoreType.DMA((n,))],
        name="pair_share",
    )(*halves)


def _all_sum_small(part):
    def body(p_ref, o_ref, slots, send_sems, recv_sems):
        x, y, c = _place()
        me = 4 * x + 2 * y + c
        slots[me] = p_ref[...]
        peers = [(x ^ dx, y ^ dy, c ^ dc) for dx in (0, 1) for dy in (0, 1) for dc in (0, 1)][1:]
        sends = [_remote(p_ref, slots.at[me], send_sems.at[k], recv_sems.at[k], peer) for k, peer in enumerate(peers)]
        for cp in sends:
            cp.start()
        for k, (px, py, pc) in enumerate(peers):
            _remote(p_ref, slots.at[4 * px + 2 * py + pc], send_sems.at[k], recv_sems.at[k], (px, py, pc)).wait_recv()
        for cp in sends:
            cp.wait_send()
        acc = slots[0]
        for d in range(1, 8):
            acc = acc + slots[d]
        o_ref[...] = acc

    vmem = pl.BlockSpec(memory_space=pltpu.VMEM)
    return pl.pallas_call(
        body,
        in_specs=[vmem],
        out_specs=vmem,
        out_shape=jax.ShapeDtypeStruct(part.shape, F32),
        scratch_shapes=[pltpu.VMEM((8,) + part.shape, F32), pltpu.SemaphoreType.DMA((7,)), pltpu.SemaphoreType.DMA((7,))],
        name="all_sum_small",
    )(part)


def _deinterleave(a, d):
    b, s, c = a.shape
    return a.reshape(b, s // d, d, c).transpose(0, 2, 1, 3).reshape(b * s // BAND, BAND, c)


def _reinterleave(a, d, b):
    c = a.shape[-1]
    return a.reshape(b, d, SEQ // d, c).transpose(0, 2, 1, 3).reshape(b, SEQ, c)


def _rope_tables():
    half = HEAD_DIM // 2
    inv_freq = np.float32(ROPE_THETA) ** (-np.arange(half, dtype=np.float32) * np.float32(2.0) / np.float32(HEAD_DIM))
    ang = np.arange(SEQ, dtype=np.float32)[:, None] * inv_freq[None, :].astype(np.float32)
    cos = np.tile(np.cos(ang).astype(np.float32), (1, 2 * BAND_HEADS))
    sin = np.tile(np.concatenate([-np.sin(ang), np.sin(ang)], axis=1).astype(np.float32), (1, BAND_HEADS))
    return jnp.asarray(cos), jnp.asarray(sin)


def _band_groups():
    out = []
    for d in DIL_D:
        rows = SEQ // d
        cls = max(1, 512 // rows) if d > 1 else 1
        out.append(dict(rows=rows, cls=cls, steps=d // cls))
    return out


def _local_step(x, mem, loss_target, g_pre_mix, g_post_mix, g_pre_ffn, g_post_ffn, g_mem, b_gate, w, comm=None):
    bl = x.shape[0]
    t = bl * SEQ
    chips = range(N_CHIPS)
    half_ff = D_FF // 2

    def with_gathered(w, names, gathered, shards):
        return {**w, **{name: lax.dynamic_update_slice(g, s[None], (comm["me"][0], 0, 0))
                        for name, g, s in zip(names, gathered, shards)}}

    x2 = x.reshape(t, D_MODEL)
    tgt2 = loss_target.reshape(t, D_MODEL)
    mem2 = mem.reshape(bl * MEM_LEN, D_MODEL)

    h = _norm_fwd(x2, g_pre_mix, name="norm_x", side=("gather", comm["first_shards"]) if comm else None)
    if comm:
        w = with_gathered(w, comm["first_names"], h[1], comm["first_shards"])
        h = h[0]
    w_in_full = _join_shards(w["w_in"])
    proj = _mm([(h, w_in_full)], nt=False, tn=2176, out_dtypes=[BF16], name="proj",
               side=("gather", comm["mid_shards"]) if comm else None)
    if comm:
        w = with_gathered(w, comm["mid_names"], proj[1], comm["mid_shards"])
        proj = proj[0]
    w_mem_kv_full = w["w_mem_kv"].reshape(D_MODEL, 2 * MEM_W)
    gates = _mm([(h, w["w_gate"], None, "j")], nt=False,tn=w["w_gate"].shape[2], out_dtypes=[BF16], name="gates",
                bias=b_gate, epilogue=lambda acc: (_sigmoid(acc),))
    hm = _norm_fwd(mem2, g_mem, name="norm_mem")
    kv_m = _mm([(hm, w_mem_kv_full)], nt=False,tn=1024, out_dtypes=[BF16], name="mem_kv")
    proj3 = proj.reshape(bl, SEQ, D_IN)
    kv3 = kv_m.reshape(bl, MEM_LEN, 2 * MEM_W)

    o_a, o_a32, sb_weights, late_gathered = _sb_fwd(proj3, comm["late_shards"] if comm else [])
    if comm:
        w = with_gathered(w, comm["late_names"], late_gathered, comm["late_shards"])
    w_o_full = w["w_o"].reshape(D_MODEL, D_MODEL)
    w_ffn_out_full = w["w_ffn_out"].reshape(D_FF, D_MODEL)

    cos_t, sin_t = _rope_tables()
    dil0 = 3 * SB_W

    grp_w = 3 * DIL_W
    band = []
    for g, (d, cfg) in enumerate(zip(DIL_D, _band_groups())):
        a_g = proj3 if d == 1 else proj3[:, :, dil0 + g * grp_w:dil0 + (g + 1) * grp_w].reshape(bl, SEQ // d, d * grp_w)
        band.append(dict(cfg, a=a_g, col0=dil0 // grp_w if d == 1 else None, cos=cos_t.reshape(SEQ // d, d * DIL_W),
                         sin=sin_t.reshape(SEQ // d, d * DIL_W)))
    outs = [_band_group_fwd(b["a"], b["cos"], b["sin"], rows=b["rows"], cls=b["cls"], steps=b["steps"], col0=b["col0"],
                            name=f"band_fwd_{g}") for g, b in enumerate(band)]
    o_b, lse_b = _band_merge3([(o.reshape(t, DIL_W), l.reshape(t, LANES)) for o, l in outs])

    o_c = _mem_fwd(proj3, kv3)

    o_a2, o_c2 = o_a.reshape(t, SB_W), o_c.reshape(t, MEM_W)
    y_a, y_b, y_c, merged = _branch_merge_fwd(o_a2, o_b, o_c2, w["w_br_sb"], w["w_br_dil"], w["w_br_mem"], gates)
    mix = _mm([(merged, w_o_full)], nt=False,tn=1024, out_dtypes=[F32], name="mix")
    x1, h2 = _mid_fwd(mix, x2, g_post_mix, g_pre_ffn)
    gg, uu, f = _ffn_in_fwd(h2, w["w_ffn_in"])
    f2 = _mm([(f, w_ffn_out_full)], nt=False,tn=1024, out_dtypes=[F32], name="ffn_out")

    dy, df2, dg_post_ffn, loss_row = _loss_bwd(f2, x1, g_post_ffn, tgt2)

    dg_ffn, du_ffn = _mm([(df2, w_ffn_out_full)], nt=True,tn=half_ff, out_dtypes=[BF16, BF16], name="d_ffn_act",
                         extras=(gg, uu), epilogue=_swiglu_bwd_epilogue)
    gw = {}
    gw["w_ffn_out"] = _mm_tn(f, df2, tm=half_ff, tn=1024, name="gw_ffn_out").reshape(N_CHIPS, D_FF // N_CHIPS, D_MODEL)
    gw_ffn_g = _mm_tn(h2, dg_ffn, tm=1024, tn=half_ff, name="gw_ffn_gate", out_shards=True, slots=(N_CHIPS, 0))
    gw["w_ffn_in"] = _mm_tn(h2, du_ffn, tm=1024, tn=half_ff, name="gw_ffn_up", out_shards=True, slots=(N_CHIPS, 2), into=gw_ffn_g)
    dh2 = _mm([(dg_ffn, w["w_ffn_in"], 0, 0), (dg_ffn, w["w_ffn_in"], 1, 1), (du_ffn, w["w_ffn_in"], 0, 2),
               (du_ffn, w["w_ffn_in"], 1, 3)], nt=True, tn=1024, out_dtypes=[BF16], name="d_h2")
    dx1, dmix, dg_pre_ffn, dg_post_mix = _mid_bwd(dh2, x1, mix, g_pre_ffn, g_post_mix, dy)

    gw["w_o"] = _mm_tn(merged, dmix, tm=1024, tn=1024, name="gw_o").reshape(N_CHIPS, D_MODEL // N_CHIPS, D_MODEL)
    dmerged = _mm([(dmix, w_o_full)], nt=True, tn=1024, out_dtypes=[BF16], name="d_merged")
    dy_a, dy_b, dy_c, dgpre, db_gate = _gate_bwd(dmerged, gates, y_a, y_b, y_c)
    br_cols = D_MODEL // N_CHIPS
    gw["w_br_sb"] = _mm_tn(o_a2, dy_a, tm=512, tn=br_cols, name="gw_br_sb", out_shards=True)
    gw["w_br_dil"] = _mm_tn(o_b, dy_b, tm=256, tn=br_cols, name="gw_br_dil", out_shards=True)
    gw["w_br_mem"] = _mm_tn(o_c2, dy_c, tm=512, tn=br_cols, name="gw_br_mem", out_shards=True)
    gw["w_gate"] = _mm_tn(h, dgpre, tm=1024, tn=w["w_gate"].shape[2], name="gw_gate", out_shards=True)
    do_a = _mm([(dy_a, w["w_br_sb"], s, s) for s in chips], nt=True,tn=SB_W, out_dtypes=[BF16], name="d_o_a")
    do_b = _mm([(dy_b, w["w_br_dil"], s, s) for s in chips], nt=True,tn=DIL_W, out_dtypes=[BF16], name="d_o_b")
    do_c = _mm([(dy_c, w["w_br_mem"], s, s) for s in chips], nt=True,tn=MEM_W, out_dtypes=[BF16], name="d_o_c")

    dq_c, dkv_m = _mem_bwd(proj3, kv3, do_c.reshape(bl, SEQ, MEM_W))
    dkv_m = dkv_m.reshape(bl * MEM_LEN, 2 * MEM_W).astype(BF16)
    gw["w_mem_kv"] = _mm_tn(hm, dkv_m, tm=1024, tn=1024, name="gw_mem_kv").reshape(N_CHIPS, D_MODEL // N_CHIPS, 2 * MEM_W)
    dhm = _mm([(dkv_m, w_mem_kv_full)], nt=True,tn=1024, out_dtypes=[F32], name="d_hm")
    dg_mem = _mem_norm_bwd(dhm, mem2, g_mem)

    stats = _band_delta(do_b, o_b, lse_b)
    early = [name for name, _, _ in PACK if name != "w_in"] if comm else []
    grads = [gw[name] for name in early]
    d_dil = []
    for g, (d, b) in enumerate(zip(DIL_D, band)):
        out = _band_group_bwd(b["a"], do_b.reshape(bl, SEQ // d, d * DIL_W), stats.reshape(bl, SEQ // d, d * LANES),
                              b["cos"], b["sin"], rows=b["rows"], cls=b["cls"], steps=b["steps"], col0=b["col0"],
                              name=f"band_bwd_{g}", side=("pair", grads) if comm and g == 0 else None)
        if comm and g == 0:
            out, lands = out
        d_dil.append(out.reshape(bl, SEQ, grp_w))

    parts = [_pair_add(g, l, comm["c"], name="pair_add_" + name) for name, g, l in zip(early, grads, lands)] if comm else []
    dq_a, dk_a, dv_a, lands = _sb_bwd(proj3, o_a32, do_a.reshape(bl, SEQ, SB_W), sb_weights, parts)
    reduced = {name: (p, l) for name, p, l in zip(early, parts, lands)}

    in_cols = D_IN // N_CHIPS
    dproj_s = _split_to_shards([a.reshape(t, a.shape[-1]) for a in [dq_a, dk_a, dv_a] + d_dil + [dq_c]], name="dproj_shards")
    gw["w_in"] = _mm_tn(h, dproj_s, tm=1024, tn=in_cols, name="gw_in")
    if comm:
        land = _pair_exchange([gw["w_in"]], name="pair_exchange_w_in")[0]
        part_in = _pair_add(gw["w_in"], land, comm["c"], name="pair_add_w_in")
    dh = _mm([(dproj_s, w["w_in"], s, s) for s in chips] + [(dgpre, w["w_gate"], s, s) for s in chips],
             nt=True, tn=1024, out_dtypes=[BF16], name="d_h", side=("chip", [part_in]) if comm else None)
    if comm:
        dh, (land_in,) = dh
        reduced["w_in"] = (part_in, land_in)
    grad_x, dg_pre_mix = _first_bwd(dh, x2, g_pre_mix, dx1)
    small = jnp.concatenate([dg_pre_mix, dg_post_mix, dg_pre_ffn, dg_post_ffn, dg_mem, db_gate.reshape(3, D_MODEL)], axis=0)
    return loss_row[0, 0], grad_x.reshape(bl, SEQ, D_MODEL), gw, small, reduced


def kernel(x, mem, g_pre_mix, g_post_mix, g_pre_ffn, g_post_ffn, g_mem, w_in, w_mem_kv, w_br_sb, w_br_dil, w_br_mem, w_gate, b_gate, w_o, w_ffn_in, w_ffn_out, loss_target, m_g_pre_mix, m_g_post_mix, m_g_pre_ffn, m_g_post_ffn, m_g_mem, m_w_in, m_w_mem_kv, m_w_br_sb, m_w_br_dil, m_w_br_mem, m_w_gate, m_b_gate, m_w_o, m_w_ffn_in, m_w_ffn_out, v_g_pre_mix, v_g_post_mix, v_g_pre_ffn, v_g_post_ffn, v_g_mem, v_w_in, v_w_mem_kv, v_w_br_sb, v_w_br_dil, v_w_br_mem, v_w_gate, v_b_gate, v_w_o, v_w_ffn_in, v_w_ffn_out):
    w_shards = dict(w_in=w_in[0], w_mem_kv=w_mem_kv[0], w_br_sb=w_br_sb[0], w_br_dil=w_br_dil[0], w_br_mem=w_br_mem[0],
                    w_gate=w_gate[0], w_o=w_o[0], w_ffn_in=w_ffn_in[0], w_ffn_out=w_ffn_out[0])
    m_shards = dict(w_in=m_w_in[0], w_mem_kv=m_w_mem_kv[0], w_br_sb=m_w_br_sb[0], w_br_dil=m_w_br_dil[0], w_br_mem=m_w_br_mem[0],
                    w_gate=m_w_gate[0], w_o=m_w_o[0], w_ffn_in=m_w_ffn_in[0], w_ffn_out=m_w_ffn_out[0])
    v_shards = dict(w_in=v_w_in[0], w_mem_kv=v_w_mem_kv[0], w_br_sb=v_w_br_sb[0], w_br_dil=v_w_br_dil[0], w_br_mem=v_w_br_mem[0],
                    w_gate=v_w_gate[0], w_o=v_w_o[0], w_ffn_in=v_w_ffn_in[0], w_ffn_out=v_w_ffn_out[0])

    names = [name for name, _, _ in PACK]
    c_arr = lax.axis_index("c").astype(jnp.int32).reshape(1)
    me_arr = (2 * lax.axis_index("x") + lax.axis_index("y")).astype(jnp.int32).reshape(1)
    mid_names = ["w_gate", "w_mem_kv"]
    late_names = [name for name in names if name not in ["w_in"] + mid_names]
    bf = {name: w_shards[name].astype(BF16) for name in names}
    comm = dict(c=c_arr, me=me_arr, first_names=["w_in"], first_shards=[bf["w_in"]],
                mid_names=mid_names, mid_shards=[bf[name] for name in mid_names],
                late_names=late_names, late_shards=[bf[name] for name in late_names])

    loss_local, grad_x, gw, small, reduced = _local_step(x, mem, loss_target, g_pre_mix, g_post_mix, g_pre_ffn, g_post_ffn,
                                                         g_mem, b_gate, {}, comm)
    loss = lax.psum(loss_local, ("x", "y", "c"))

    halves =[_chip_add(reduced[name][1], reduced[name][0], me_arr, name="chip_add_" + name) for name in names]
    theirs = _pair_share(halves)
    small = _all_sum_small(small)

    upd = {}
    for name, mine, other in zip(names, halves, theirs):
        upd[name] = _adamw_halves(w_shards[name], mine, other, m_shards[name], v_shards[name], c_arr, name="adamw_" + name)
    g_shards = {name: u[0] for name, u in upd.items()}

    def small8(gs, b):
        return jnp.concatenate(gs + [b.reshape(3, D_MODEL)], axis=0)

    sw = small8([g_pre_mix, g_post_mix, g_pre_ffn, g_post_ffn, g_mem], b_gate)
    sm = small8([m_g_pre_mix, m_g_post_mix, m_g_pre_ffn, m_g_post_ffn, m_g_mem], m_b_gate)
    sv = small8([v_g_pre_mix, v_g_post_mix, v_g_pre_ffn, v_g_post_ffn, v_g_mem], v_b_gate)
    s_upd = _adamw(sw, small, sm, sv, tm=8, name="adamw_small")

    def small_out(a):
        return [a[0:1], a[1:2], a[2:3], a[3:4], a[4:5]]

    order = ["w_in", "w_mem_kv", "w_br_sb", "w_br_dil", "w_br_mem", "w_gate", "b_gate", "w_o", "w_ffn_in", "w_ffn_out"]

    def leaves(small_arr, big):
        out = small_out(small_arr)
        for name in order:
            out.append(small_arr[5:8].reshape(1, 3 * D_MODEL) if name == "b_gate" else big[name][None])
        return out

    grads_out = leaves(small, g_shards)
    delta_out = leaves(s_upd[0], {n: u[1] for n, u in upd.items()})
    m_out = leaves(s_upd[1], {n: u[2] for n, u in upd.items()})
    v_out = leaves(s_upd[2], {n: u[3] for n, u in upd.items()})
    return (loss, grad_x, *grads_out, *delta_out, *m_out, *v_out)
```

```python
import jax
import jax.numpy as jnp
import numpy as np
from jax import lax
from jax.experimental import pallas as pl
from jax.experimental.pallas import tpu as pltpu

F32 = jnp.float32
BF16 = jnp.bfloat16
MESH = pl.DeviceIdType.MESH

D_MODEL = 1024
SEQ = 2048
HEAD_DIM = 64
SB_W = 512
DIL_W = 256
MEM_W = 512
MEM_LEN = 256
D_IN = 3 * SB_W + 9 * DIL_W + MEM_W
D_FF = 2816
DIL_D = (1, 4, 16)
ROPE_THETA = 10000.0
NORM_EPS = 1e-6
NEG_INF = -1e30
LANES = 128

ADAM_LR = 0.001
ADAM_B1 = 0.9
ADAM_B2 = 0.999
ADAM_EPS = 1e-08
ADAM_WD = 0.01
ADAM_STEP = 10

N_CHIPS = 4
PACK = (
    ("w_in", (1024, 1088), 1),
    ("w_mem_kv", (256, 1024), 0),
    ("w_br_sb", (512, 256), 1),
    ("w_br_dil", (256, 256), 1),
    ("w_br_mem", (512, 256), 1),
    ("w_gate", (1024, 768), 1),
    ("w_o", (256, 1024), 0),
    ("w_ffn_in", (1024, 1408), 1),
    ("w_ffn_out", (704, 1024), 0),
)
PACK_ROWS = sum(a * b for _, (a, b), _ in PACK) // D_MODEL
HALF_ROWS = PACK_ROWS // 2


def _dot(a, b):
    return lax.dot_general(a, b, (((1,), (0,)), ((), ())), preferred_element_type=F32)


def _dot_nt(a, b):
    return lax.dot_general(a, b, (((1,), (1,)), ((), ())), preferred_element_type=F32)


def _dot_tn(a, b):
    return lax.dot_general(a, b, (((0,), (0,)), ((), ())), preferred_element_type=F32)


def _split_dot(x, u):
    hi = x.astype(BF16)
    lo = (x - hi.astype(F32)).astype(BF16)
    return _dot(hi, u) + _dot(lo, u)


V7X_VMEM_BUDGET = 44 * 2 ** 20


def _rows_that_fit(m, row_bytes, fixed_bytes):
    for tm in (1024, 512, 256, 128):
        if m % tm == 0 and fixed_bytes + tm * row_bytes <= V7X_VMEM_BUDGET:
            return tm
    return min(m, 128)


def _mm(pairs, *, nt, tn, out_dtypes, name, bias=None, extras=(), epilogue=None, side=None):
    pairs = [p if len(p) == 4 else (p[0], p[1], None, None) for p in pairs]
    m = pairs[0][0].shape[-2]
    b0 = pairs[0][1]
    if nt:
        n = b0.shape[-2]
    else:
        n = b0.shape[-1] * (b0.shape[0] if b0.ndim == 3 else 1)
    n_pairs, n_extra, n_out = len(pairs), len(extras), len(out_dtypes)
    assert n % tn == 0
    one_col = n == tn
    ks = [(b.shape[-1] if nt else b.shape[-2]) for _, b, _, _ in pairs]
    fixed = sum(k * tn * 2 for k in ks) * (1 if one_col else 2)
    row_bytes = 2 * sum(k * 2 for k in ks) + 2 * tn * (sum(jnp.dtype(dt).itemsize for dt in out_dtypes) + 2 * n_extra) + 2 * tn * 4
    tm = _rows_that_fit(m, row_bytes, fixed)
    assert m % tm == 0
    b_mode = dict(pipeline_mode=pl.Buffered(1)) if one_col else {}
    has_bias = bias is not None
    side_arrays = side[1] if side else []
    n_side = len(side_arrays)
    n_main_in = 2 * n_pairs + has_bias + n_extra
    n_steps = (n // tn) * (m // tm)

    def body(*refs):
        if n_side:
            step = pl.program_id(0) * (m // tm) + pl.program_id(1)
            finish = _run_side(side, refs[n_main_in:n_main_in + n_side],
                               refs[n_main_in + n_side + n_out:n_main_in + 2 * n_side + n_out],
                               refs[n_main_in + 2 * n_side + n_out:], step, n_steps)
        outs = refs[n_main_in + n_side:n_main_in + n_side + n_out]
        acc = None
        for i in range(n_pairs):
            a, b = refs[2 * i][...], refs[2 * i + 1][...]
            p = _dot_nt(a, b) if nt else _dot(a, b)
            acc = p if acc is None else acc + p
        pos = 2 * n_pairs
        if has_bias:
            acc = acc + refs[pos][...]
            pos += 1
        ex = [r[...] for r in refs[pos:pos + n_extra]]
        vals = (acc,) if epilogue is None else epilogue(acc, *ex)
        for r, v, dt in zip(outs, vals, out_dtypes):
            r[...] = v.astype(dt)
        if n_side:
            finish()

    in_specs, args = [], []
    for a, b, a_col, b_sel in pairs:
        k = b.shape[-1] if nt else b.shape[-2]
        assert a_col is not None or a.shape[1] == k
        if a.ndim == 3:
            in_specs.append(pl.BlockSpec((None, tm, k), lambda j, i, c=a_col: (c, i, 0)))
        else:
            in_specs.append(pl.BlockSpec((tm, k), lambda j, i, c=a_col or 0: (i, c)))
        if b.ndim == 2:
            in_specs.append(pl.BlockSpec((tn, k), lambda j, i: (j, 0), **b_mode) if nt
                            else pl.BlockSpec((k, tn), lambda j, i: (0, j), **b_mode))
        elif nt:
            in_specs.append(pl.BlockSpec((None, tn, k), lambda j, i, s=b_sel: (s, j, 0), **b_mode))
        else:
            assert b_sel == "j" and b.shape[-1] == tn
            in_specs.append(pl.BlockSpec((None, k, tn), lambda j, i: (j, 0, 0), **b_mode))
        args += [a, b]
    if has_bias:
        in_specs.append(pl.BlockSpec((1, tn), lambda j, i: (0, j)))
        args.append(bias)
    for e in extras:
        in_specs.append(pl.BlockSpec((tm, tn), lambda j, i: (i, j)))
        args.append(e)
    out = pl.pallas_call(
        body,
        grid=(n // tn, m // tm),
        in_specs=in_specs + [ANY] * n_side,
        out_specs=[pl.BlockSpec((tm, tn), lambda j, i: (i, j)) for _ in range(n_out)] + [ANY] * n_side,
        out_shape=[jax.ShapeDtypeStruct((m, n), dt) for dt in out_dtypes] + (_side_out_shapes(side) if n_side else []),
        scratch_shapes=_side_sems(side) if n_side else [],
        compiler_params=pltpu.CompilerParams(dimension_semantics=("arbitrary", "arbitrary") if n_side else ("parallel", "parallel")),
        name=name,
    )(*args, *side_arrays)
    if n_side:
        return (out[0] if n_out == 1 else out[:n_out]), out[n_out:]
    return out[0] if n_out == 1 else out


def _mm_tn(a, b, *, tm, tn, name, out_shards=False, slots=None, into=None, group=1):
    k, m = a.shape
    b_shards = b.ndim == 3
    out_shards = out_shards or b_shards
    n = b.shape[0] * b.shape[2] if b_shards else b.shape[1]
    tk = _rows_that_fit(k, 2 * 2 * (tm + group * tn), 3 * tm * group * tn * 4)
    assert m % tm == 0 and n % (group * tn) == 0 and k % tk == 0 and (not b_shards or b.shape[2] == tn)
    assert group == 1 or out_shards
    total, first = slots if slots else (n // tn, 0)

    def body(a_ref, b_ref, *rest):
        o_ref = rest[-1]

        @pl.when(pl.program_id(2) == 0)
        def _():
            o_ref[...] = jnp.zeros_like(o_ref)

        if group == 1:
            o_ref[...] += _dot_tn(a_ref[...], b_ref[...])
        elif b_shards:
            a_blk = a_ref[...]
            for s in range(group):
                o_ref[s] += _dot_tn(a_blk, b_ref[s])
        else:
            acc = _dot_tn(a_ref[...], b_ref[...])
            for s in range(group):
                o_ref[s] += acc[:, s * tn:(s + 1) * tn]

    lead = None if group == 1 else group
    if b_shards:
        b_spec = pl.BlockSpec((lead, tk, tn), lambda i, j, kk: (j, kk, 0))
    else:
        b_spec = pl.BlockSpec((tk, group * tn), lambda i, j, kk: (kk, j))
    if out_shards:
        out_spec = pl.BlockSpec((lead, tm, tn), lambda i, j, kk: (j + first // group, i, 0))
        out_shape = jax.ShapeDtypeStruct((total, m, tn), F32)
    else:
        out_spec = pl.BlockSpec((tm, tn), lambda i, j, kk: (i, j))
        out_shape = jax.ShapeDtypeStruct((m, n), F32)
    return pl.pallas_call(
        body,
        grid=(m // tm, n // (group * tn), k // tk),
        in_specs=[pl.BlockSpec((tk, tm), lambda i, j, kk: (kk, i)), b_spec] + ([ANY] if into is not None else []),
        out_specs=out_spec,
        out_shape=out_shape,
        input_output_aliases={2: 0} if into is not None else {},
        compiler_params=pltpu.CompilerParams(dimension_semantics=("parallel", "parallel", "arbitrary")),
        name=name,
    )(*([a, b] + ([into] if into is not None else [])))


def _rowwise(fn, ins, outs, *, tm, name, side=None):
    rows = next(a.shape[0] for a, kind in ins if kind == "row")
    tm = min(tm, rows)
    assert rows % tm == 0
    n_in, n_out = len(ins), len(outs)
    side_arrays = side[1] if side else []
    n_side = len(side_arrays)

    def body(*refs):
        if n_side:
            finish = _run_side(side, refs[n_in:n_in + n_side], refs[n_in + n_side + n_out:n_in + 2 * n_side + n_out],
                               refs[n_in + 2 * n_side + n_out:], pl.program_id(0), rows // tm)
        vals = fn(*[r[...] for r in refs[:n_in]])
        for (_, dt, kind), r, v in zip(outs, refs[n_in + n_side:n_in + n_side + n_out], vals):
            if kind == "row":
                r[...] = v.astype(dt)
            else:
                @pl.when(pl.program_id(0) == 0)
                def _(r=r):
                    r[...] = jnp.zeros_like(r)

                r[...] += v
        if n_side:
            finish()

    in_specs = [pl.BlockSpec((tm, a.shape[1]), lambda i: (i, 0)) if kind == "row" else pl.BlockSpec(a.shape, lambda i: (0, 0))
                for a, kind in ins]
    out_specs = [pl.BlockSpec((tm, c), lambda i: (i, 0)) if kind == "row" else pl.BlockSpec((1, c), lambda i: (0, 0))
                 for c, _, kind in outs]
    out_shape = [jax.ShapeDtypeStruct((rows if kind == "row" else 1, c), dt) for c, dt, kind in outs]
    ordered = n_side or any(kind == "acc" for _, _, kind in outs)
    return pl.pallas_call(
        body,
        grid=(rows // tm,),
        in_specs=in_specs + [ANY] * n_side,
        out_specs=out_specs + [ANY] * n_side,
        out_shape=out_shape + (_side_out_shapes(side) if n_side else []),
        scratch_shapes=_side_sems(side) if n_side else [],
        compiler_params=pltpu.CompilerParams(dimension_semantics=("arbitrary" if ordered else "parallel",)),
        name=name,
    )(*[a for a, _ in ins], *side_arrays)


def _rstd(x):
    return lax.rsqrt(jnp.mean(x * x, axis=-1, keepdims=True) + NORM_EPS)


def _norm_bwd(dout, xin, g):
    r = _rstd(xin)
    n = xin * r
    dn = dout * g
    dg = jnp.sum(dout * n, axis=0, keepdims=True)
    dx = r * (dn - n * jnp.mean(dn * n, axis=-1, keepdims=True))
    return dx, dg


def _sigmoid(x):
    return 0.5 * jnp.tanh(0.5 * x) + 0.5


def _norm_fwd(x, g, *, name, side=None):
    def fn(x, g):
        return ((x * _rstd(x)) * g,)

    out = _rowwise(fn, [(x, "row"), (g, "vec")], [(D_MODEL, BF16, "row")], tm=512, name=name, side=side)
    return (out[0], out[1:]) if side else out[0]


def _mid_fwd(mix, x, g_post_mix, g_pre_ffn):
    def fn(mix, x, g2, g3):
        x1 = x + (mix * _rstd(mix)) * g2
        return x1, (x1 * _rstd(x1)) * g3

    return _rowwise(fn, [(mix, "row"), (x, "row"), (g_post_mix, "vec"), (g_pre_ffn, "vec")],
                    [(D_MODEL, F32, "row"), (D_MODEL, BF16, "row")], tm=512, name="mid_fwd")


def _loss_bwd(f2, x1, g_post_ffn, tgt):
    def fn(f2, x1, g4, tgt):
        r = _rstd(f2)
        n = f2 * r
        err = x1 + n * g4 - tgt
        loss = 0.5 * jnp.sum(jnp.mean(err * err, axis=-1, keepdims=True), axis=0, keepdims=True)
        dy = err * (1.0 / D_MODEL)
        dn = dy * g4
        dg4 = jnp.sum(dy * n, axis=0, keepdims=True)
        df2 = r * (dn - n * jnp.mean(dn * n, axis=-1, keepdims=True))
        return dy, df2, dg4, jnp.broadcast_to(loss, (1, LANES))

    return _rowwise(fn, [(f2, "row"), (x1, "row"), (g_post_ffn, "vec"), (tgt, "row")],
                    [(D_MODEL, BF16, "row"), (D_MODEL, BF16, "row"), (D_MODEL, F32, "acc"), (LANES, F32, "acc")],
                    tm=512, name="loss_bwd")


def _mid_bwd(dh2, x1, mix, g_pre_ffn, g_post_mix, dy):
    def fn(dh2, x1, mix, g3, g2, dy):
        d3, dg3 = _norm_bwd(dh2.astype(F32), x1, g3)
        dx1 = dy.astype(F32) + d3
        dmix, dg2 = _norm_bwd(dx1, mix, g2)
        return dx1, dmix, dg3, dg2

    return _rowwise(fn, [(dh2, "row"), (x1, "row"), (mix, "row"), (g_pre_ffn, "vec"), (g_post_mix, "vec"), (dy, "row")],
                    [(D_MODEL, BF16, "row"), (D_MODEL, BF16, "row"), (D_MODEL, F32, "acc"), (D_MODEL, F32, "acc")],
                    tm=256, name="mid_bwd")


def _first_bwd(dh, x, g_pre_mix, dx1):
    def fn(dh, x, g1, dx1):
        d1, dg1 = _norm_bwd(dh.astype(F32), x, g1)
        return dx1.astype(F32) + d1, dg1

    return _rowwise(fn, [(dh, "row"), (x, "row"), (g_pre_mix, "vec"), (dx1, "row")],
                    [(D_MODEL, F32, "row"), (D_MODEL, F32, "acc")], tm=512, name="first_bwd")


def _mem_norm_bwd(dhm, mem, g_mem):
    def fn(dhm, mem, g):
        return (jnp.sum(dhm * (mem * _rstd(mem)), axis=0, keepdims=True),)

    return _rowwise(fn, [(dhm, "row"), (mem, "row"), (g_mem, "vec")], [(D_MODEL, F32, "acc")], tm=512, name="mem_norm_bwd")[0]


def _gate_bwd(dmerged, gates, ya, yb, yc):
    def fn(dm, gt, ya, yb, yc):
        dm = dm.astype(F32)
        gt = gt.astype(F32)
        outs, dgp = [], []
        for i, y in enumerate((ya, yb, yc)):
            gi = gt[:, i * D_MODEL:(i + 1) * D_MODEL]
            outs.append(dm * gi)
            dgp.append(dm * y.astype(F32) * gi * (1.0 - gi))
        dgpre = jnp.concatenate(dgp, axis=1)
        return outs[0], outs[1], outs[2], dgpre, jnp.sum(dgpre, axis=0, keepdims=True)

    return _rowwise(fn, [(dmerged, "row"), (gates, "row"), (ya, "row"), (yb, "row"), (yc, "row")],
                    [(D_MODEL, BF16, "row")] * 3 + [(3 * D_MODEL, BF16, "row"), (3 * D_MODEL, F32, "acc")],
                    tm=256, name="gate_bwd")


def _adamw_math(w, g, m, v):
    m = ADAM_B1 * m + (1.0 - ADAM_B1) * g
    v = ADAM_B2 * v + (1.0 - ADAM_B2) * (g * g)
    m_hat = m / (1.0 - ADAM_B1 ** ADAM_STEP)
    v_hat = v / (1.0 - ADAM_B2 ** ADAM_STEP)
    delta = -ADAM_LR * (m_hat / (jnp.sqrt(v_hat) + ADAM_EPS) + ADAM_WD * w)
    return delta, m, v


def _adamw(w, g, m, v, *, tm, name):
    c = w.shape[1]
    return _rowwise(_adamw_math, [(w, "row"), (g, "row"), (m, "row"), (v, "row")], [(c, F32, "row")] * 3, tm=tm, name=name)


def _adamw_halves(w, g_mine, g_theirs, m, v, c_arr, *, name):
    a, b = w.shape
    hf = a // 2
    tr = next(t for t in (hf, hf // 2, hf // 4) if 9 * 3 * t * b * 4 <= V7X_VMEM_BUDGET)

    def body(c_ref, w_ref, gm_ref, gt_ref, m_ref, v_ref, g_out, d_out, m_out, v_out):
        g = jnp.where(pl.program_id(0) == c_ref[0], gm_ref[...], gt_ref[...])
        d, m_new, v_new = _adamw_math(w_ref[...], g, m_ref[...], v_ref[...])
        g_out[...] = g
        d_out[...] = d
        m_out[...] = m_new
        v_out[...] = v_new

    full = pl.BlockSpec((tr, b), lambda hh, i, c_ref: (hh * (hf // tr) + i, 0))
    half = pl.BlockSpec((tr, b), lambda hh, i, c_ref: (i, 0))
    return pl.pallas_call(
        body,
        grid_spec=pltpu.PrefetchScalarGridSpec(
            num_scalar_prefetch=1,
            grid=(2, hf // tr),
            in_specs=[full, half, half, full, full],
            out_specs=[full] * 4,
        ),
        out_shape=[jax.ShapeDtypeStruct((a, b), F32)] * 4,
        compiler_params=pltpu.CompilerParams(dimension_semantics=("parallel", "parallel")),
        name=name,
    )(c_arr, w, g_mine, g_theirs, m, v)


def _ffn_in_fwd(h2, w_ffn):
    m, tm, tn = h2.shape[0], 512, w_ffn.shape[2]
    assert 2 * tn == D_FF

    def body(h_ref, wg_ref, wu_ref, g_ref, u_ref, f_ref):
        h = h_ref[...]
        g = _dot(h, wg_ref[...])
        u = _dot(h, wu_ref[...])
        g_ref[...] = g.astype(BF16)
        u_ref[...] = u.astype(BF16)
        f_ref[...] = (g * _sigmoid(g) * u).astype(BF16)

    o_spec = pl.BlockSpec((tm, tn), lambda j, i: (i, j))
    return pl.pallas_call(
        body,
        grid=(D_FF // tn, m // tm),
        in_specs=[pl.BlockSpec((tm, D_MODEL), lambda j, i: (i, 0)),
                  pl.BlockSpec((None, D_MODEL, tn), lambda j, i: (j, 0, 0)),
                  pl.BlockSpec((None, D_MODEL, tn), lambda j, i: (j + 2, 0, 0))],
        out_specs=[o_spec, o_spec, o_spec],
        out_shape=[jax.ShapeDtypeStruct((m, D_FF), BF16)] * 3,
        compiler_params=pltpu.CompilerParams(dimension_semantics=("parallel", "parallel")),
        name="ffn_in_fwd",
    )(h2, w_ffn, w_ffn)


def _join_shards(w4):
    _, rows, cols = w4.shape
    tr = rows // 4

    def body(w_ref, o_ref):
        for s in range(N_CHIPS):
            o_ref[:, s * cols:(s + 1) * cols] = w_ref[s]

    return pl.pallas_call(
        body,
        grid=(rows // tr,),
        in_specs=[pl.BlockSpec((N_CHIPS, tr, cols), lambda i: (0, i, 0))],
        out_specs=pl.BlockSpec((tr, N_CHIPS * cols), lambda i: (i, 0)),
        out_shape=jax.ShapeDtypeStruct((rows, N_CHIPS * cols), w4.dtype),
        compiler_params=pltpu.CompilerParams(dimension_semantics=("parallel",)),
        name="join_shards",
    )(w4)


def _split_to_shards(pieces, *, name):
    t = pieces[0].shape[0]
    widths = [p.shape[1] for p in pieces]
    cols = sum(widths) // N_CHIPS
    tm = 512
    plan, start = [], 0
    for p, wd in enumerate(widths):
        for s in range(N_CHIPS):
            lo, hi = max(start, s * cols), min(start + wd, (s + 1) * cols)
            if lo < hi:
                plan.append((s, p, lo - s * cols, hi - s * cols, lo - start, hi - start))
        start += wd

    def body(*refs):
        o_ref = refs[-1]
        for s, p, o_lo, o_hi, p_lo, p_hi in plan:
            o_ref[s, :, o_lo:o_hi] = refs[p][:, p_lo:p_hi]

    return pl.pallas_call(
        body,
        grid=(t // tm,),
        in_specs=[pl.BlockSpec((tm, wd), lambda i: (i, 0)) for wd in widths],
        out_specs=pl.BlockSpec((N_CHIPS, tm, cols), lambda i: (0, i, 0)),
        out_shape=jax.ShapeDtypeStruct((N_CHIPS, t, cols), pieces[0].dtype),
        compiler_params=pltpu.CompilerParams(dimension_semantics=("parallel",)),
        name=name,
    )(*pieces)


def _swiglu_bwd_epilogue(df, g, u):
    g = g.astype(F32)
    u = u.astype(F32)
    sg = _sigmoid(g)
    return df * u * (sg * (1.0 + g * (1.0 - sg))), df * (g * sg)


def _branch_merge_fwd(o_a, o_b, o_c, w_sb, w_dil, w_mem, gates):
    m, tm = o_a.shape[0], 256

    def body(oa_ref, ob_ref, oc_ref, wa_ref, wb_ref, wc_ref, gt_ref, ya_ref, yb_ref, yc_ref, mg_ref):
        def project(o_ref, w_ref):
            o = o_ref[...]
            return jnp.concatenate([_dot(o, w_ref[s]) for s in range(N_CHIPS)], axis=1)

        ya = project(oa_ref, wa_ref)
        yb = project(ob_ref, wb_ref)
        yc = project(oc_ref, wc_ref)
        gt = gt_ref[...].astype(F32)
        ya_ref[...] = ya.astype(BF16)
        yb_ref[...] = yb.astype(BF16)
        yc_ref[...] = yc.astype(BF16)
        mg_ref[...] = (gt[:, :D_MODEL] * ya + gt[:, D_MODEL:2 * D_MODEL] * yb + gt[:, 2 * D_MODEL:] * yc).astype(BF16)

    row = lambda c: pl.BlockSpec((tm, c), lambda i: (i, 0))
    full = lambda a: pl.BlockSpec(a.shape, lambda i: (0, 0, 0))
    return pl.pallas_call(
        body,
        grid=(m // tm,),
        in_specs=[row(SB_W), row(DIL_W), row(MEM_W), full(w_sb), full(w_dil), full(w_mem), row(3 * D_MODEL)],
        out_specs=[row(D_MODEL)] * 4,
        out_shape=[jax.ShapeDtypeStruct((m, D_MODEL), BF16)] * 4,
        compiler_params=pltpu.CompilerParams(dimension_semantics=("parallel",)),
        name="branch_merge_fwd",
    )(o_a, o_b, o_c, w_sb, w_dil, w_mem, gates)


SB_T = 256
SB_SCALE = HEAD_DIM ** -0.5


def _sb_masks():
    row = lax.broadcasted_iota(jnp.int32, (SB_T, SB_T), 0)
    col = lax.broadcasted_iota(jnp.int32, (SB_T, SB_T), 1)
    lane = lax.broadcasted_iota(jnp.int32, (1, LANES), 1)
    return row, col, lane


def _sb_logs(z):
    lb = jnp.minimum(z, 0.0) - jnp.log(1.0 + jnp.exp(-jnp.abs(z)))
    return lb, lb - z


def _sb_specs(n_heads_pairs, col0):
    q = pl.BlockSpec((None, SB_T, LANES), lambda b, p, i: (b, i, col0 + p))
    k = pl.BlockSpec((None, SEQ, LANES), lambda b, p, i: (b, 0, col0 + n_heads_pairs + p))
    v = pl.BlockSpec((None, SEQ, LANES), lambda b, p, i: (b, 0, col0 + 2 * n_heads_pairs + p))
    return q, k, v


def _sb_first_blocks(i):
    rem = lax.rem(i + 1, 3)
    return jnp.where(rem == 0, 3, rem)


def _grid_step(n_pairs, nq):
    return (pl.program_id(0) * n_pairs + pl.program_id(1)) * nq + pl.program_id(2)


def _sb_fwd(proj3, late_shards):
    bl = proj3.shape[0]
    n_pairs = SB_W // LANES
    nq = SEQ // SB_T
    n_late = len(late_shards)
    n_steps = bl * n_pairs * nq

    def body(q_ref, k_ref, v_ref, *rest):
        late_in, (o_ref, o32_ref, w_ref), late_out = rest[:n_late], rest[n_late:n_late + 3], rest[n_late + 3:2 * n_late + 3]
        step = _grid_step(n_pairs, nq)
        if n_late:
            send, forward, finish = _gather_phases(late_in, late_out, *rest[2 * n_late + 3:])
            pl.when(step == 0)(send)
            pl.when(step == n_steps // 2)(forward)
        i = pl.program_id(2)
        row, col, lane = _sb_masks()
        causal = col < row
        u_excl = (row > col).astype(BF16)
        q = q_ref[...]
        heads = []
        for h in range(2):
            mh = (lane // HEAD_DIM) == h
            heads.append((mh, jnp.where(mh, q, jnp.zeros_like(q)) * SB_SCALE))

        def blocks(js, diags, carries, acc):
            ks = [k_ref[pl.ds(pl.multiple_of(j * SB_T, SB_T), SB_T), :] for j in js]
            vs = [v_ref[pl.ds(pl.multiple_of(j * SB_T, SB_T), SB_T), :] for j in js]
            chains = [(b, h) for b in range(len(js)) for h in range(2)]
            z = {c: _dot_nt(heads[c[1]][1], ks[c[0]]) for c in chains}
            lb, lk = {}, {}
            for c in chains:
                lb[c], lk[c] = _sb_logs(z[c])
                if diags[c[0]]:
                    lk[c] = jnp.where(causal, lk[c], 0.0)
            r = {c: _split_dot(lk[c], u_excl) for c in chains}
            carries = list(carries)
            w = {}
            for b, h in chains:
                w_c = jnp.exp(lb[b, h] + r[b, h] + carries[h])
                w[b, h] = (jnp.where(causal, w_c, 0.0) if diags[b] else w_c).astype(BF16)
                w_ref[h, js[b]] = w[b, h]
                carries[h] = carries[h] + (r[b, h][:, :1] + lk[b, h][:, :1])
            for b, h in chains:
                acc = acc + _dot(w[b, h], jnp.where(heads[h][0], vs[b], jnp.zeros_like(vs[b])))
            return tuple(carries), acc

        zero = jnp.zeros((SB_T, 1), F32)
        init = ((zero, zero), jnp.zeros((SB_T, LANES), F32))
        first = _sb_first_blocks(i)
        carries, acc = lax.cond(
            first == 1, lambda: blocks([i], (True,), *init),
            lambda: lax.cond(first == 2, lambda: blocks([i, i - 1], (True, False), *init),
                             lambda: blocks([i, i - 1, i - 2], (True, False, False), *init)))
        rest = i - first
        carries, acc = lax.fori_loop(
            0, (rest + 1) // 3,
            lambda jj, c: blocks([rest - 3 * jj, rest - 1 - 3 * jj, rest - 2 - 3 * jj], (False, False, False), c[0], c[1]),
            (carries, acc))
        o_ref[...] = acc.astype(BF16)
        o32_ref[...] = acc
        if n_late:
            pl.when(step == n_steps - 1)(finish)

    q_spec, k_spec, v_spec = _sb_specs(n_pairs, 0)
    blk = pl.BlockSpec((None, SB_T, LANES), lambda b, p, i: (b, i, p))
    out = pl.pallas_call(
        body,
        grid=(bl, n_pairs, nq),
        in_specs=[q_spec, k_spec, v_spec] + [ANY] * n_late,
        out_specs=[blk, blk, _sb_weight_spec(nq)] + [ANY] * n_late,
        out_shape=[jax.ShapeDtypeStruct((bl, SEQ, SB_W), BF16), jax.ShapeDtypeStruct((bl, SEQ, SB_W), F32),
                   jax.ShapeDtypeStruct((bl, n_pairs, nq, 2, nq, SB_T, SB_T), BF16)] + _gather_out_shapes(late_shards),
        scratch_shapes=_gather_sems(n_late) if n_late else [],
        compiler_params=pltpu.CompilerParams(dimension_semantics=("arbitrary", "arbitrary", "arbitrary")),
        name="sb_fwd",
    )(proj3, proj3, proj3, *late_shards)
    return out[0], out[1], out[2], out[3:]


def _sb_weight_spec(nq):
    return pl.BlockSpec((None, None, None, 2, nq, SB_T, SB_T), lambda b, p, i: (b, p, i, 0, 0, 0, 0))


def _sb_bwd(proj3, o_a, do_a, w_all, parts):
    bl = proj3.shape[0]
    n_pairs = SB_W // LANES
    nq = SEQ // SB_T
    n_parts = len(parts)
    n_steps = bl * n_pairs * nq

    def body(q_ref, k_ref, v_ref, o_ref, do_ref, w_ref, *rest):
        p_refs, (dq_ref, dk_ref, dv_ref), land_refs = rest[:n_parts], rest[n_parts:n_parts + 3], rest[n_parts + 3:2 * n_parts + 3]
        dk_acc, dv_acc = rest[2 * n_parts + 3:2 * n_parts + 5]
        step = _grid_step(n_pairs, nq)
        if n_parts:
            send, finish = _chip_exchange_phases(p_refs, land_refs, *rest[2 * n_parts + 5:])
            pl.when(step == 0)(send)
        i = pl.program_id(2)

        @pl.when(i == 0)
        def _():
            dk_acc[...] = jnp.zeros_like(dk_acc)
            dv_acc[...] = jnp.zeros_like(dv_acc)

        row, col, lane = _sb_masks()
        causal = col < row
        u_incl = (row >= col).astype(BF16)
        q = q_ref[...]
        do = do_ref[...]
        prod = do.astype(F32) * o_ref[...]
        heads = []
        for h in range(2):
            mh = (lane // HEAD_DIM) == h
            d_tot = jnp.sum(jnp.where(mh, prod, 0.0), axis=1, keepdims=True)
            heads.append((mh, jnp.where(mh, q, jnp.zeros_like(q)) * SB_SCALE, jnp.where(mh, do, jnp.zeros_like(do)), d_tot))

        def blocks(js, diags, c_das, dq):
            starts = [pl.multiple_of(j * SB_T, SB_T) for j in js]
            ks = [k_ref[pl.ds(s, SB_T), :] for s in starts]
            vs = [v_ref[pl.ds(s, SB_T), :] for s in starts]
            chains = [(b, h) for b in range(len(js)) for h in range(2)]
            z = {c: _dot_nt(heads[c[1]][1], ks[c[0]]) for c in chains}
            dw = {c: _dot_nt(heads[c[1]][2], vs[c[0]]) for c in chains}
            wb = {(b, h): w_ref[h, js[b]] for b, h in chains}
            da = {c: dw[c] * wb[c].astype(F32) for c in chains}
            sfx = {c: _split_dot(da[c], u_incl) for c in chains}
            c_das = list(c_das)
            dz = {}
            for b, h in chains:
                dlk = heads[h][3] - c_das[h] - sfx[b, h]
                if diags[b]:
                    dlk = jnp.where(causal, dlk, 0.0)
                c_das[h] = c_das[h] + sfx[b, h][:, :1]
                e = jnp.exp(-jnp.abs(z[b, h]))
                inv = 1.0 / (1.0 + e)
                pos = z[b, h] >= 0.0
                beta = jnp.where(pos, inv, e * inv)
                one_m_beta = jnp.where(pos, e * inv, inv)
                dz[b, h] = (da[b, h] * one_m_beta - dlk * beta).astype(BF16)
            for b, h in chains:
                dq = dq + _dot(dz[b, h], jnp.where(heads[h][0], ks[b], jnp.zeros_like(ks[b])))
            for b in range(len(js)):
                dk_acc[pl.ds(starts[b], SB_T), :] += _dot_tn(dz[b, 0], heads[0][1]) + _dot_tn(dz[b, 1], heads[1][1])
                dv_acc[pl.ds(starts[b], SB_T), :] += _dot_tn(wb[b, 0], heads[0][2]) + _dot_tn(wb[b, 1], heads[1][2])
            return tuple(c_das), dq

        zero = jnp.zeros((SB_T, 1), F32)
        init = ((zero, zero), jnp.zeros((SB_T, LANES), F32))
        first = _sb_first_blocks(i)
        state = lax.cond(
            first == 1, lambda: blocks([i], (True,), *init),
            lambda: lax.cond(first == 2, lambda: blocks([i, i - 1], (True, False), *init),
                             lambda: blocks([i, i - 1, i - 2], (True, False, False), *init)))
        rest = i - first
        state = lax.fori_loop(
            0, (rest + 1) // 3,
            lambda jj, c: blocks([rest - 3 * jj, rest - 1 - 3 * jj, rest - 2 - 3 * jj], (False, False, False), c[0], c[1]), state)
        dq_ref[...] = (state[1] * SB_SCALE).astype(BF16)

        @pl.when(i == nq - 1)
        def _():
            dk_ref[...] = dk_acc[...].astype(BF16)
            dv_ref[...] = dv_acc[...].astype(BF16)

        if n_parts:
            pl.when(step == n_steps - 1)(finish)

    q_spec, k_spec, v_spec = _sb_specs(n_pairs, 0)
    blk = pl.BlockSpec((None, SB_T, LANES), lambda b, p, i: (b, i, p))
    seq = pl.BlockSpec((None, SEQ, LANES), lambda b, p, i: (b, 0, p))
    shape = jax.ShapeDtypeStruct((bl, SEQ, SB_W), BF16)
    out = pl.pallas_call(
        body,
        grid=(bl, n_pairs, nq),
        in_specs=[q_spec, k_spec, v_spec, blk, blk, _sb_weight_spec(nq)] + [ANY] * n_parts,
        out_specs=[blk, seq, seq] + [ANY] * n_parts,
        out_shape=[shape, shape, shape] + [jax.ShapeDtypeStruct(p.shape, p.dtype) for p in parts],
        scratch_shapes=[pltpu.VMEM((SEQ, LANES), F32), pltpu.VMEM((SEQ, LANES), F32)]
        + (_chip_exchange_sems(n_parts) if n_parts else []),
        compiler_params=pltpu.CompilerParams(dimension_semantics=("arbitrary", "arbitrary", "arbitrary")),
        name="sb_bwd",
    )(proj3, proj3, proj3, o_a, do_a, w_all, *parts)
    return out[0], out[1], out[2], out[3:]


BAND = 128


BAND_CH = 4
BAND_HEADS = DIL_W // HEAD_DIM


def _swap_half(x):
    n = x.shape[-1]
    lane = lax.broadcasted_iota(jnp.int32, (1, n), 1)
    return jnp.where((lane % HEAD_DIM) < HEAD_DIM // 2, pltpu.roll(x, n - HEAD_DIM // 2, 1), pltpu.roll(x, HEAD_DIM // 2, 1))


def _rope(x, cos, sin_signed):
    x = x.astype(F32)
    return x * cos + _swap_half(x) * sin_signed


def _band_valid(g, blk):
    nb = jnp.where(g == 0, 16, jnp.where(g == 1, 4, 1))
    first_key = jnp.where(lax.rem(blk, nb) != 0, 0, BAND)
    qi = lax.broadcasted_iota(jnp.int32, (BAND, 2 * BAND), 0) + BAND
    kj = lax.broadcasted_iota(jnp.int32, (BAND, 2 * BAND), 1)
    dist = qi - kj
    return (dist >= 0) & (dist <= BAND) & (kj >= first_key)


def _band_specs():
    last_before = lambda i: jnp.maximum(i * BAND_CH - 1, 0)
    cur = lambda col: pl.BlockSpec((None, BAND_CH, BAND, DIL_W), lambda g, i: (g, i, 0, col))
    prev = lambda col: pl.BlockSpec((None, None, BAND, DIL_W), lambda g, i: (g, last_before(i), 0, col))
    tab = pl.BlockSpec((None, BAND_CH, BAND, DIL_W), lambda g, i: (g, lax.rem(i, 16 // BAND_CH), 0, 0))
    tab_prev = pl.BlockSpec((None, None, BAND, DIL_W), lambda g, i: (g, lax.rem(last_before(i), 16), 0, 0))
    return cur, prev, tab, tab_prev


def _band_load(q_ref, k_ref, kp_ref, v_ref, vp_ref, c_ref, s_ref, cp_ref, sp_ref):
    qs = [(_rope(q_ref[b], c_ref[b], s_ref[b]) * SB_SCALE).astype(BF16) for b in range(BAND_CH)]
    ks = [_rope(kp_ref[...], cp_ref[...], sp_ref[...]).astype(BF16)]
    ks += [_rope(k_ref[b], c_ref[b], s_ref[b]).astype(BF16) for b in range(BAND_CH)]
    vs = [vp_ref[...]] + [v_ref[b] for b in range(BAND_CH)]
    k2 = [jnp.concatenate([ks[b], ks[b + 1]], axis=0) for b in range(BAND_CH)]
    v2 = [jnp.concatenate([vs[b], vs[b + 1]], axis=0) for b in range(BAND_CH)]
    return qs, k2, v2


def _band_fwd(qkv_s, cos_t, sin_t):
    def body(q_ref, k_ref, kp_ref, v_ref, vp_ref, c_ref, s_ref, cp_ref, sp_ref, ol_ref):
        g, i = pl.program_id(0), pl.program_id(1)
        qs, k2, v2 = _band_load(q_ref, k_ref, kp_ref, v_ref, vp_ref, c_ref, s_ref, cp_ref, sp_ref)
        lane = lax.broadcasted_iota(jnp.int32, (1, DIL_W), 1)
        for b in range(BAND_CH):
            valid = _band_valid(g, i * BAND_CH + b)
            hs = range(BAND_HEADS)
            mh = [(lane // HEAD_DIM) == h for h in hs]
            s = [jnp.where(valid, _dot_nt(jnp.where(mh[h], qs[b], jnp.zeros_like(qs[b])), k2[b]), NEG_INF) for h in hs]
            m = [jnp.max(s[h], axis=1, keepdims=True) for h in hs]
            p = [jnp.exp(s[h] - m[h]) for h in hs]
            den = [jnp.sum(p[h], axis=1, keepdims=True) for h in hs]
            pv = [_dot(p[h].astype(BF16), jnp.where(mh[h], v2[b], jnp.zeros_like(v2[b]))) for h in hs]
            o = jnp.zeros((BAND, DIL_W), F32)
            lse = jnp.zeros((BAND, DIL_W), F32)
            for h in hs:
                o = o + pv[h] * (1.0 / den[h])
                lse = jnp.where(mh[h], m[h] + jnp.log(den[h]), lse)
            ol_ref[b, :, :DIL_W] = o
            ol_ref[b, :, DIL_W:] = lse

    cur, prev, tab, tab_prev = _band_specs()
    n_blk = qkv_s.shape[1]
    return pl.pallas_call(
        body,
        grid=(3, n_blk // BAND_CH),
        in_specs=[cur(0), cur(1), prev(1), cur(2), prev(2), tab, tab, tab_prev, tab_prev],
        out_specs=pl.BlockSpec((None, BAND_CH, BAND, 2 * DIL_W), lambda g, i: (g, i, 0, 0)),
        out_shape=jax.ShapeDtypeStruct((3, n_blk, BAND, 2 * DIL_W), F32),
        compiler_params=pltpu.CompilerParams(dimension_semantics=("parallel", "parallel")),
        name="band_fwd",
    )(qkv_s, qkv_s, qkv_s, qkv_s, qkv_s, cos_t, sin_t, cos_t, sin_t)


def _band_bwd(qkv_s, cos_t, sin_t, dcat_s):
    def body(q_ref, k_ref, kp_ref, v_ref, vp_ref, c_ref, s_ref, cp_ref, sp_ref, do_ref, lse_ref, dl_ref,
             dq_ref, dk_ref, dv_ref, dkf_ref, dvf_ref):
        g, i = pl.program_id(0), pl.program_id(1)
        qs, k2, v2 = _band_load(q_ref, k_ref, kp_ref, v_ref, vp_ref, c_ref, s_ref, cp_ref, sp_ref)
        lane = lax.broadcasted_iota(jnp.int32, (1, DIL_W), 1)
        dks, dvs = [], []
        for b in range(BAND_CH):
            valid = _band_valid(g, i * BAND_CH + b)
            do, lse, dl = do_ref[b].astype(BF16), lse_ref[b], dl_ref[b]
            hs = range(BAND_HEADS)
            mh = [(lane // HEAD_DIM) == h for h in hs]
            qh = [jnp.where(mh[h], qs[b], jnp.zeros_like(qs[b])) for h in hs]
            doh = [jnp.where(mh[h], do, jnp.zeros_like(do)) for h in hs]
            s = [_dot_nt(qh[h], k2[b]) for h in hs]
            dp = [_dot_nt(doh[h], v2[b]) for h in hs]
            p = [jnp.where(valid, jnp.exp(s[h] - lse[:, h * HEAD_DIM:h * HEAD_DIM + 1]), 0.0) for h in hs]
            ds = [(p[h] * (dp[h] - dl[:, h * HEAD_DIM:h * HEAD_DIM + 1])).astype(BF16) for h in hs]
            pb = [p[h].astype(BF16) for h in hs]
            dq = sum(_dot(ds[h], jnp.where(mh[h], k2[b], jnp.zeros_like(k2[b]))) for h in hs)
            dk2 = sum(_dot_tn(ds[h], qh[h]) for h in hs)
            dv2 = sum(_dot_tn(pb[h], doh[h]) for h in hs)
            dq_ref[b] = dq * SB_SCALE
            dks.append(dk2)
            dvs.append(dv2)
        dkf_ref[...] = dks[0][:BAND]
        dvf_ref[...] = dvs[0][:BAND]
        for b in range(BAND_CH):
            last = b == BAND_CH - 1
            dk_ref[b] = dks[b][BAND:] if last else dks[b][BAND:] + dks[b + 1][:BAND]
            dv_ref[b] = dvs[b][BAND:] if last else dvs[b][BAND:] + dvs[b + 1][:BAND]

    cur, prev, tab, tab_prev = _band_specs()
    first = pl.BlockSpec((None, None, BAND, DIL_W), lambda g, i: (g, i, 0, 0))
    n_blk = qkv_s.shape[1]
    n_chunks = n_blk // BAND_CH
    shape = jax.ShapeDtypeStruct((3, n_blk, BAND, DIL_W), F32)
    shape_first = jax.ShapeDtypeStruct((3, n_chunks, BAND, DIL_W), F32)
    return pl.pallas_call(
        body,
        grid=(3, n_chunks),
        in_specs=[cur(0), cur(1), prev(1), cur(2), prev(2), tab, tab, tab_prev, tab_prev, cur(0), cur(1), cur(2)],
        out_specs=[cur(0), cur(0), cur(0), first, first],
        out_shape=[shape, shape, shape, shape_first, shape_first],
        compiler_params=pltpu.CompilerParams(dimension_semantics=("parallel", "parallel")),
        name="band_bwd",
    )(qkv_s, qkv_s, qkv_s, qkv_s, qkv_s, cos_t, sin_t, cos_t, sin_t, dcat_s, dcat_s, dcat_s)


def _band_combine(dq, dk, dv, dk_first, dv_first, cos_t, sin_t):
    n_chunks = dk_first.shape[1]

    def body(dq_ref, dk_ref, dkn_ref, dv_ref, dvn_ref, c_ref, s_ref, out_ref):
        nxt = (pl.program_id(1) < n_chunks - 1).astype(F32)
        for b in range(BAND_CH):
            cos, sin = c_ref[b], s_ref[b]
            dq_b, dk_b, dv_b = dq_ref[b], dk_ref[b], dv_ref[b]
            if b == BAND_CH - 1:
                dk_b = dk_b + nxt * dkn_ref[...]
                dv_b = dv_b + nxt * dvn_ref[...]
            out_ref[b, :, :DIL_W] = (dq_b * cos - _swap_half(dq_b) * sin).astype(BF16)
            out_ref[b, :, DIL_W:2 * DIL_W] = (dk_b * cos - _swap_half(dk_b) * sin).astype(BF16)
            out_ref[b, :, 2 * DIL_W:] = dv_b.astype(BF16)

    cur, _, tab, _ = _band_specs()
    nxt = pl.BlockSpec((None, None, BAND, DIL_W), lambda g, i: (g, jnp.minimum(i + 1, n_chunks - 1), 0, 0))
    return pl.pallas_call(
        body,
        grid=(3, n_chunks),
        in_specs=[cur(0), cur(0), nxt, cur(0), nxt, tab, tab],
        out_specs=pl.BlockSpec((None, BAND_CH, BAND, 3 * DIL_W), lambda g, i: (g, i, 0, 0)),
        out_shape=jax.ShapeDtypeStruct(dq.shape[:3] + (3 * DIL_W,), BF16),
        compiler_params=pltpu.CompilerParams(dimension_semantics=("parallel", "parallel")),
        name="band_combine",
    )(dq, dk, dk_first, dv, dv_first, cos_t, sin_t)


def _band_merge(ol):
    t, tm = ol.shape[1], 512

    def body(o_ref, l_ref, ob_ref, lse_ref):
        l0, l1, l2 = l_ref[0], l_ref[1], l_ref[2]
        m = jnp.maximum(jnp.maximum(l0, l1), l2)
        lse = m + jnp.log(jnp.exp(l0 - m) + jnp.exp(l1 - m) + jnp.exp(l2 - m))
        ob_ref[...] = (jnp.exp(l0 - lse) * o_ref[0] + jnp.exp(l1 - lse) * o_ref[1] + jnp.exp(l2 - lse) * o_ref[2]).astype(BF16)
        lse_ref[...] = lse

    spec = pl.BlockSpec((tm, DIL_W), lambda i: (i, 0))
    return pl.pallas_call(
        body,
        grid=(t // tm,),
        in_specs=[pl.BlockSpec((3, tm, DIL_W), lambda i: (0, i, 0)), pl.BlockSpec((3, tm, DIL_W), lambda i: (0, i, 1))],
        out_specs=[spec, spec],
        out_shape=[jax.ShapeDtypeStruct((t, DIL_W), BF16), jax.ShapeDtypeStruct((t, DIL_W), F32)],
        compiler_params=pltpu.CompilerParams(dimension_semantics=("parallel",)),
        name="band_merge",
    )(ol, ol)


def _band_delta(do_b, o_b, lse_b):
    def fn(do, o, lse):
        lane_in = lax.broadcasted_iota(jnp.int32, (DIL_W, LANES), 0)
        col = lax.broadcasted_iota(jnp.int32, (DIL_W, LANES), 1)
        sum_head = ((lane_in // HEAD_DIM == col - BAND_HEADS) & (col >= BAND_HEADS) & (col < 2 * BAND_HEADS)).astype(BF16)
        return (lse + _split_dot(do.astype(F32) * o.astype(F32), sum_head),)

    return _rowwise(fn, [(do_b, "row"), (o_b, "row"), (lse_b, "row")], [(LANES, F32, "row")], tm=512, name="band_delta")[0]


def _band_masks():
    qi = lax.broadcasted_iota(jnp.int32, (BAND, 2 * BAND), 0) + BAND
    kj = lax.broadcasted_iota(jnp.int32, (BAND, 2 * BAND), 1)
    dist = qi - kj
    row = lax.broadcasted_iota(jnp.int32, (BAND, BAND), 0)
    col = lax.broadcasted_iota(jnp.int32, (BAND, BAND), 1)
    return col <= row, (dist >= 0) & (dist <= BAND)


def _band_attend(q, k, v, valid):
    lane = lax.broadcasted_iota(jnp.int32, (1, DIL_W), 1)
    stat_lane = lax.broadcasted_iota(jnp.int32, (1, LANES), 1)
    hs = range(BAND_HEADS)
    mh = [(lane // HEAD_DIM) == h for h in hs]
    s = [jnp.where(valid, _dot_nt(jnp.where(mh[h], q, jnp.zeros_like(q)), k), NEG_INF) for h in hs]
    m = [jnp.max(s[h], axis=1, keepdims=True) for h in hs]
    p = [jnp.exp(s[h] - m[h]) for h in hs]
    den = [jnp.sum(p[h], axis=1, keepdims=True) for h in hs]
    pv = [_dot(p[h].astype(BF16), jnp.where(mh[h], v, jnp.zeros_like(v))) for h in hs]
    o = jnp.zeros((BAND, DIL_W), F32)
    lse = jnp.zeros((BAND, LANES), F32)
    for h in hs:
        o = o + pv[h] * (1.0 / den[h])
        lse = jnp.where(stat_lane == h, m[h] + jnp.log(den[h]), lse)
    return o, lse


def _band_attend_bwd(q, k, v, valid, do, st):
    lane = lax.broadcasted_iota(jnp.int32, (1, DIL_W), 1)
    hs = range(BAND_HEADS)
    mh = [(lane // HEAD_DIM) == h for h in hs]
    qh = [jnp.where(mh[h], q, jnp.zeros_like(q)) for h in hs]
    doh = [jnp.where(mh[h], do, jnp.zeros_like(do)) for h in hs]
    s = [_dot_nt(qh[h], k) for h in hs]
    dp = [_dot_nt(doh[h], v) for h in hs]
    p = [jnp.where(valid, jnp.exp(s[h] - st[:, h:h + 1]), 0.0) for h in hs]
    ds = [(p[h] * (dp[h] - st[:, BAND_HEADS + h:BAND_HEADS + h + 1])).astype(BF16) for h in hs]
    pb = [p[h].astype(BF16) for h in hs]
    dq = sum(_dot(ds[h], jnp.where(mh[h], k, jnp.zeros_like(k))) for h in hs)
    dk = sum(_dot_tn(ds[h], qh[h]) for h in hs)
    dv = sum(_dot_tn(pb[h], doh[h]) for h in hs)
    return dq, dk, dv


def _band_group_specs(lead, rows, cls, col0):
    def spec(width):
        if lead == "rows":
            return pl.BlockSpec((None, rows, width), lambda b, i: (b, 0, col0))
        return pl.BlockSpec((None, rows, cls * width), lambda b, i: (b, 0, i))
    return spec


def _band_group_fwd(a, cos_g, sin_g, *, rows, cls, steps, col0, name):
    bl = a.shape[0]
    nb = rows // BAND
    grp_w = 3 * DIL_W

    def body(a_ref, c_ref, s_ref, o_ref, l_ref, qr_all, kr_all):
        first_valid, later_valid = _band_masks()
        for j in range(cls):
            qr, kr = qr_all.at[j], kr_all.at[j]
            a0, t0, s0 = j * grp_w, j * DIL_W, j * LANES
            cos, sin = c_ref[:, t0:t0 + DIL_W], s_ref[:, t0:t0 + DIL_W]
            qr[...] = (_rope(a_ref[:, a0:a0 + DIL_W], cos, sin) * SB_SCALE).astype(BF16)
            kr[...] = _rope(a_ref[:, a0 + DIL_W:a0 + 2 * DIL_W], cos, sin).astype(BF16)

            def block(q0, k0, keys, valid, a0=a0, t0=t0, s0=s0):
                o, lse = _band_attend(qr[pl.ds(q0, BAND), :], kr[pl.ds(k0, keys), :],
                                      a_ref[pl.ds(k0, keys), a0 + 2 * DIL_W:a0 + grp_w], valid)
                o_ref[pl.ds(q0, BAND), t0:t0 + DIL_W] = o.astype(BF16)
                l_ref[pl.ds(q0, BAND), s0:s0 + LANES] = lse

            block(0, 0, BAND, first_valid)
            if nb > 1:
                def later(b, carry, block=block):
                    block(pl.multiple_of(b * BAND, BAND), pl.multiple_of((b - 1) * BAND, BAND), 2 * BAND, later_valid)
                    return carry

                lax.fori_loop(1, nb, later, 0, unroll=3)

    lead = "rows" if col0 is not None else "cols"
    spec = _band_group_specs(lead, rows, cls, col0)
    tab = pl.BlockSpec((rows, cls * DIL_W), lambda b, i: (0, i))
    n_cls = cos_g.shape[1] // DIL_W
    return pl.pallas_call(
        body,
        grid=(bl, steps),
        in_specs=[spec(grp_w), tab, tab],
        out_specs=[pl.BlockSpec((None, rows, cls * DIL_W), lambda b, i: (b, 0, i)),
                   pl.BlockSpec((None, rows, cls * LANES), lambda b, i: (b, 0, i))],
        out_shape=[jax.ShapeDtypeStruct((bl, rows, n_cls * DIL_W), BF16), jax.ShapeDtypeStruct((bl, rows, n_cls * LANES), F32)],
        scratch_shapes=[pltpu.VMEM((cls, rows, DIL_W), BF16), pltpu.VMEM((cls, rows, DIL_W), BF16)],
        compiler_params=pltpu.CompilerParams(dimension_semantics=("parallel", "parallel")),
        name=name,
    )(a, cos_g, sin_g)


def _band_group_bwd(a, do, st, cos_g, sin_g, *, rows, cls, steps, col0, name, side=None):
    bl = a.shape[0]
    nb = rows // BAND
    grp_w = 3 * DIL_W
    side_arrays = side[1] if side else []
    n_side = len(side_arrays)

    def body(a_ref, do_ref, st_ref, c_ref, s_ref, *rest):
        out_ref = rest[n_side]
        scratch = rest[2 * n_side + 1:2 * n_side + 5]
        if n_side:
            step = pl.program_id(0) * steps + pl.program_id(1)
            finish = _run_side(side, rest[:n_side], rest[n_side + 1:2 * n_side + 1], rest[2 * n_side + 5:], step, bl * steps)
        first_valid, later_valid = _band_masks()
        for j in range(cls):
            qr, kr, dk_acc, dv_acc = (s.at[j] for s in scratch)
            a0, t0 = j * grp_w, j * DIL_W
            cos, sin = c_ref[:, t0:t0 + DIL_W], s_ref[:, t0:t0 + DIL_W]
            qr[...] = (_rope(a_ref[:, a0:a0 + DIL_W], cos, sin) * SB_SCALE).astype(BF16)
            kr[...] = _rope(a_ref[:, a0 + DIL_W:a0 + 2 * DIL_W], cos, sin).astype(BF16)
            dk_acc[...] = jnp.zeros_like(dk_acc)
            dv_acc[...] = jnp.zeros_like(dv_acc)

            def block(q0, k0, keys, valid, a0=a0, t0=t0, s0=j * LANES):
                qrows, krows = pl.ds(q0, BAND), pl.ds(k0, keys)
                dq, dk, dv = _band_attend_bwd(
                    qr[qrows, :], kr[krows, :], a_ref[krows, a0 + 2 * DIL_W:a0 + grp_w], valid,
                    do_ref[qrows, t0:t0 + DIL_W], st_ref[qrows, s0:s0 + LANES])
                dq = dq * SB_SCALE
                out_ref[qrows, a0:a0 + DIL_W] = (dq * c_ref[qrows, t0:t0 + DIL_W]
                                                 - _swap_half(dq) * s_ref[qrows, t0:t0 + DIL_W]).astype(BF16)
                dk_acc[krows, :] += dk
                dv_acc[krows, :] += dv

            block(0, 0, BAND, first_valid)
            if nb > 1:
                def later(b, carry, block=block):
                    block(pl.multiple_of(b * BAND, BAND), pl.multiple_of((b - 1) * BAND, BAND), 2 * BAND, later_valid)
                    return carry

                lax.fori_loop(1, nb, later, 0, unroll=3)
            dk = dk_acc[...]
            out_ref[:, a0 + DIL_W:a0 + 2 * DIL_W] = (dk * cos - _swap_half(dk) * sin).astype(BF16)
            out_ref[:, a0 + 2 * DIL_W:a0 + grp_w] = dv_acc[...].astype(BF16)
        if n_side:
            finish()

    lead = "rows" if col0 is not None else "cols"
    spec = _band_group_specs(lead, rows, cls, col0)
    dspec = _band_group_specs(lead, rows, cls, 0 if col0 is not None else None)
    tab = pl.BlockSpec((rows, cls * DIL_W), lambda b, i: (0, i))
    n_cls = cos_g.shape[1] // DIL_W
    out = pl.pallas_call(
        body,
        grid=(bl, steps),
        in_specs=[spec(grp_w), dspec(DIL_W), dspec(LANES), tab, tab] + [ANY] * n_side,
        out_specs=[pl.BlockSpec((None, rows, cls * grp_w), lambda b, i: (b, 0, i))] + [ANY] * n_side,
        out_shape=[jax.ShapeDtypeStruct((bl, rows, n_cls * grp_w), BF16)] + (_side_out_shapes(side) if n_side else []),
        scratch_shapes=[pltpu.VMEM((cls, rows, DIL_W), BF16), pltpu.VMEM((cls, rows, DIL_W), BF16),
                        pltpu.VMEM((cls, rows, DIL_W), F32), pltpu.VMEM((cls, rows, DIL_W), F32)] + (_side_sems(side) if n_side else []),
        compiler_params=pltpu.CompilerParams(dimension_semantics=("arbitrary", "arbitrary") if n_side else ("parallel", "parallel")),
        name=name,
    )(a, do, st, cos_g, sin_g, *side_arrays)
    return (out[0], out[1:]) if n_side else out[0]


def _band_merge3(groups):
    t, tm = groups[0][0].shape[0], 512

    def body(o0, l0, o1, l1, o2, l2, ob_ref, lse_ref):
        a, b, c = l0[...], l1[...], l2[...]
        m = jnp.maximum(jnp.maximum(a, b), c)
        lse = m + jnp.log(jnp.exp(a - m) + jnp.exp(b - m) + jnp.exp(c - m))
        lane = lax.broadcasted_iota(jnp.int32, (1, DIL_W), 1)
        acc = jnp.zeros((tm, DIL_W), F32)
        for o_ref, l in ((o0, a), (o1, b), (o2, c)):
            share = jnp.exp(l - lse)
            spread = jnp.zeros((tm, DIL_W), F32)
            for h in range(BAND_HEADS):
                spread = jnp.where(lane // HEAD_DIM == h, share[:, h:h + 1], spread)
            acc = acc + spread * o_ref[...].astype(F32)
        ob_ref[...] = acc.astype(BF16)
        stat_lane = lax.broadcasted_iota(jnp.int32, (1, LANES), 1)
        lse_ref[...] = jnp.where(stat_lane < BAND_HEADS, lse, 0.0)

    spec = pl.BlockSpec((tm, DIL_W), lambda i: (i, 0))
    spec_l = pl.BlockSpec((tm, LANES), lambda i: (i, 0))
    return pl.pallas_call(
        body,
        grid=(t // tm,),
        in_specs=[spec, spec_l] * 3,
        out_specs=[spec, spec_l],
        out_shape=[jax.ShapeDtypeStruct((t, DIL_W), BF16), jax.ShapeDtypeStruct((t, LANES), F32)],
        compiler_params=pltpu.CompilerParams(dimension_semantics=("parallel",)),
        name="band_merge",
    )(*[a for g in groups for a in g])


MEM_T = 512
MEM_SCALE = 128 ** -0.5
MEM_Q_COL = (D_IN - MEM_W) // LANES


MEM_HEADS = MEM_W // LANES


def _mem_specs():
    qs = [pl.BlockSpec((None, MEM_T, LANES), lambda b, i, h=h: (b, i, MEM_Q_COL + h)) for h in range(MEM_HEADS)]
    kv = pl.BlockSpec((None, MEM_LEN, 2 * MEM_W), lambda b, i: (b, 0, 0))
    blk = pl.BlockSpec((None, MEM_T, MEM_W), lambda b, i: (b, i, 0))
    return qs, kv, blk


def _mem_probs(q, k):
    s = _dot_nt(q, k) * MEM_SCALE
    p = jnp.exp(s - jnp.max(s, axis=1, keepdims=True))
    return p * (1.0 / jnp.sum(p, axis=1, keepdims=True))


def _head_cols(h, base=0):
    return slice(base + h * LANES, base + (h + 1) * LANES)


def _mem_fwd(proj3, kv3):
    bl = proj3.shape[0]
    hs = range(MEM_HEADS)

    def body(*refs):
        q_refs, kv_ref, o_ref = refs[:MEM_HEADS], refs[MEM_HEADS], refs[MEM_HEADS + 1]
        p = [_mem_probs(q_refs[h][...], kv_ref[:, _head_cols(h)]) for h in hs]
        for h in hs:
            o_ref[:, _head_cols(h)] = _dot(p[h].astype(BF16), kv_ref[:, _head_cols(h, MEM_W)]).astype(BF16)

    qs, kv, blk = _mem_specs()
    return pl.pallas_call(
        body,
        grid=(bl, SEQ // MEM_T),
        in_specs=qs + [kv],
        out_specs=blk,
        out_shape=jax.ShapeDtypeStruct((bl, SEQ, MEM_W), BF16),
        compiler_params=pltpu.CompilerParams(dimension_semantics=("parallel", "parallel")),
        name="mem_fwd",
    )(*([proj3] * MEM_HEADS), kv3)


def _mem_bwd(proj3, kv3, do_c):
    bl = proj3.shape[0]
    hs = range(MEM_HEADS)

    def body(*refs):
        q_refs, kv_ref, do_ref, dq_ref, dkv_ref = refs[:MEM_HEADS], *refs[MEM_HEADS:MEM_HEADS + 4]

        @pl.when(pl.program_id(1) == 0)
        def _():
            dkv_ref[...] = jnp.zeros_like(dkv_ref)

        q = [q_refs[h][...] for h in hs]
        do = [do_ref[:, _head_cols(h)] for h in hs]
        p = [_mem_probs(q[h], kv_ref[:, _head_cols(h)]) for h in hs]
        dp = [_dot_nt(do[h], kv_ref[:, _head_cols(h, MEM_W)]) for h in hs]
        ds = [(p[h] * (dp[h] - jnp.sum(p[h] * dp[h], axis=1, keepdims=True)) * MEM_SCALE).astype(BF16) for h in hs]
        for h in hs:
            dq_ref[:, _head_cols(h)] = _dot(ds[h], kv_ref[:, _head_cols(h)]).astype(BF16)
            dkv_ref[:, _head_cols(h)] += _dot_tn(ds[h], q[h])
            dkv_ref[:, _head_cols(h, MEM_W)] += _dot_tn(p[h].astype(BF16), do[h])

    qs, kv, blk = _mem_specs()
    return pl.pallas_call(
        body,
        grid=(bl, SEQ // MEM_T),
        in_specs=qs + [kv, blk],
        out_specs=[blk, kv],
        out_shape=[jax.ShapeDtypeStruct((bl, SEQ, MEM_W), BF16), jax.ShapeDtypeStruct((bl, MEM_LEN, 2 * MEM_W), F32)],
        compiler_params=pltpu.CompilerParams(dimension_semantics=("parallel", "arbitrary")),
        name="mem_bwd",
    )(*([proj3] * MEM_HEADS), kv3, do_c)


def _place():
    x, y, c = lax.axis_index("x"), lax.axis_index("y"), lax.axis_index("c")
    return x, y, c


def _other_chips(x, y):
    return [(1 - x, y), (x, 1 - y), (1 - x, 1 - y)]


def _remote(src, dst, send_sem, recv_sem, to):
    return pltpu.make_async_remote_copy(src_ref=src, dst_ref=dst, send_sem=send_sem, recv_sem=recv_sem,
                                        device_id=to, device_id_type=MESH)


ANY = pl.BlockSpec(memory_space=pl.ANY)


def _gather_weights(shards):
    n = len(shards)

    def body(*refs):
        send, forward, finish = _gather_phases(refs[:n], refs[n:2 * n], *refs[2 * n:])
        send()
        forward()
        finish()

    return pl.pallas_call(
        body,
        in_specs=[ANY] * n,
        out_specs=[ANY] * n,
        out_shape=_gather_out_shapes(shards),
        scratch_shapes=_gather_sems(n),
        name="gather_weights",
    )(*shards)


def _gather_out_shapes(shards):
    return [jax.ShapeDtypeStruct((N_CHIPS,) + s.shape, s.dtype) for s in shards]


def _gather_sems(n):
    return [pltpu.SemaphoreType.DMA((6 * n,)), pltpu.SemaphoreType.DMA((6 * n,))]


def _gather_phases(in_refs, out_refs, send_sems, recv_sems):
    x, y, c = _place()
    sibling = (x, y, 1 - c)
    chips = _other_chips(x, y)
    first, passed = [], []
    for k in range(len(in_refs)):
        hf = in_refs[k].shape[0] // 2

        def half(px, py, pc, k=k, hf=hf):
            return out_refs[k].at[2 * px + py, pl.ds(pc * hf, hf), :]

        src = in_refs[k].at[pl.ds(c * hf, hf), :]
        for j, chip in enumerate(chips):
            s = 6 * k + j
            first.append(_remote(src, half(x, y, c), send_sems.at[s], recv_sems.at[s], (*chip, c)))
            passed.append((_remote(src, half(*chip, c), send_sems.at[s], recv_sems.at[s], (*chip, c)),
                           _remote(half(*chip, c), half(*chip, c), send_sems.at[s + 3], recv_sems.at[s + 3], sibling),
                           _remote(src, half(*chip, 1 - c), send_sems.at[s + 3], recv_sems.at[s + 3], sibling)))

    def send():
        for cp in first:
            cp.start()

    def forward():
        for landed, fwd, _ in passed:
            landed.wait_recv()
            fwd.start()

    def finish():
        for _, _, from_sibling in passed:
            from_sibling.wait_recv()
        for cp in first:
            cp.wait_send()
        for _, fwd, _ in passed:
            fwd.wait_send()

    return send, forward, finish


def _pair_exchange(grads, *, name):
    n = len(grads)
    side = ("pair", grads)

    def body(*refs):
        send, _, finish = _side_phases(side, refs[:n], refs[n:2 * n], refs[2 * n:])
        send()
        finish()

    return pl.pallas_call(
        body,
        in_specs=[ANY] * n,
        out_specs=[ANY] * n,
        out_shape=_side_out_shapes(side),
        scratch_shapes=_side_sems(side),
        name=name,
    )(*grads)


def _pair_exchange_phases(g_refs, land_refs, send_sems, recv_sems):
    x, y, c = _place()
    cps = []
    for k in range(len(g_refs)):
        hf = g_refs[k].shape[1] // 2
        src = g_refs[k].at[:, pl.ds((1 - c) * hf, hf), :]
        cps.append(_remote(src, land_refs[k], send_sems.at[k], recv_sems.at[k], (x, y, 1 - c)))

    def send():
        for cp in cps:
            cp.start()

    def finish():
        for cp in cps:
            cp.wait()

    return send, finish


def _side_out_shapes(side):
    kind, arrays = side
    if kind == "gather":
        return _gather_out_shapes(arrays)
    if kind == "pair":
        return [jax.ShapeDtypeStruct((N_CHIPS, g.shape[1] // 2, g.shape[2]), g.dtype) for g in arrays]
    return [jax.ShapeDtypeStruct(p.shape, p.dtype) for p in arrays]


def _side_sems(side):
    kind, arrays = side
    n = len(arrays)
    if kind == "gather":
        return _gather_sems(n)
    if kind == "pair":
        return [pltpu.SemaphoreType.DMA((n,)), pltpu.SemaphoreType.DMA((n,))]
    return _chip_exchange_sems(n)


def _side_phases(side, in_refs, out_refs, sems):
    kind = side[0]
    if kind == "gather":
        return _gather_phases(in_refs, out_refs, *sems)
    send, finish = (_pair_exchange_phases if kind == "pair" else _chip_exchange_phases)(in_refs, out_refs, *sems)
    return send, None, finish


def _run_side(side, in_refs, out_refs, sems, step, n_steps):
    first, mid, last = _side_phases(side, in_refs, out_refs, sems)
    pl.when(step == 0)(first)
    if mid is not None:
        pl.when(step == n_steps // 2)(mid)
    return lambda: pl.when(step == n_steps - 1)(last)


def _pair_add(g, land, c_arr, *, name):
    _, a, b = g.shape
    hf = a // 2

    def body(c_ref, g_ref, l_ref, o_ref):
        o_ref[...] = (g_ref[...] + l_ref[...]).astype(BF16)

    return pl.pallas_call(
        body,
        grid_spec=pltpu.PrefetchScalarGridSpec(
            num_scalar_prefetch=1,
            grid=(N_CHIPS,),
            in_specs=[pl.BlockSpec((None, None, hf, b), lambda s, c_ref: (s, c_ref[0], 0, 0)),
                      pl.BlockSpec((None, hf, b), lambda s, c_ref: (s, 0, 0))],
            out_specs=pl.BlockSpec((None, hf, b), lambda s, c_ref: (s, 0, 0)),
        ),
        out_shape=jax.ShapeDtypeStruct((N_CHIPS, hf, b), BF16),
        compiler_params=pltpu.CompilerParams(dimension_semantics=("parallel",)),
        name=name,
    )(c_arr, g.reshape(N_CHIPS, 2, hf, b), land)


def _chip_exchange(parts):
    n = len(parts)

    def body(*refs):
        send, finish = _chip_exchange_phases(refs[:n], refs[n:2 * n], *refs[2 * n:])
        send()
        finish()

    return pl.pallas_call(
        body,
        in_specs=[ANY] * n,
        out_specs=[ANY] * n,
        out_shape=[jax.ShapeDtypeStruct(p.shape, p.dtype) for p in parts],
        scratch_shapes=_chip_exchange_sems(n),
        name="chip_exchange",
    )(*parts)


def _chip_exchange_sems(n):
    return [pltpu.SemaphoreType.DMA((3 * n,)), pltpu.SemaphoreType.DMA((3 * n,))]


def _chip_exchange_phases(p_refs, land_refs, send_sems, recv_sems):
    x, y, c = _place()
    me = 2 * x + y
    sends, recvs = [], []
    for k in range(len(p_refs)):
        for j, (cx, cy) in enumerate(_other_chips(x, y)):
            s = 3 * k + j
            sends.append(_remote(p_refs[k].at[2 * cx + cy], land_refs[k].at[me], send_sems.at[s], recv_sems.at[s], (cx, cy, c)))
            recvs.append(_remote(p_refs[k].at[me], land_refs[k].at[2 * cx + cy], send_sems.at[s], recv_sems.at[s], (cx, cy, c)))

    def send():
        for cp in sends:
            cp.start()

    def finish():
        for cp in recvs:
            cp.wait_recv()
        for cp in sends:
            cp.wait_send()

    return send, finish


def _chip_add(land, part, me_arr, *, name):
    _, r, b = land.shape

    def body(me_ref, p_ref, l1_ref, l2_ref, l3_ref, o_ref):
        o_ref[...] = ((p_ref[...].astype(F32) + l1_ref[...].astype(F32)) + l2_ref[...].astype(F32)) + l3_ref[...].astype(F32)

    tr = r // 2
    other = lambda j: pl.BlockSpec((None, tr, b), lambda i, me_ref: (jnp.bitwise_xor(me_ref[0], j), i, 0))
    return pl.pallas_call(
        body,
        grid_spec=pltpu.PrefetchScalarGridSpec(
            num_scalar_prefetch=1,
            grid=(r // tr,),
            in_specs=[pl.BlockSpec((None, tr, b), lambda i, me_ref: (me_ref[0], i, 0)), other(2), other(1), other(3)],
            out_specs=pl.BlockSpec((tr, b), lambda i, me_ref: (i, 0)),
        ),
        out_shape=jax.ShapeDtypeStruct((r, b), F32),
        compiler_params=pltpu.CompilerParams(dimension_semantics=("parallel",)),
        name=name,
    )(me_arr, part, land, land, land)


def _pair_share(halves):
    n = len(halves)

    def body(*refs):
        h_refs, out_refs = refs[:n], refs[n:2 * n]
        send_sems, recv_sems = refs[2 * n:]
        x, y, c = _place()
        cps = [_remote(h_refs[k], out_refs[k], send_sems.at[k], recv_sems.at[k], (x, y, 1 - c)) for k in range(n)]
        for cp in cps:
            cp.start()
        for cp in cps:
            cp.wait()

    return pl.pallas_call(
        body,
        in_specs=[ANY] * n,
        out_specs=[ANY] * n,
        out_shape=[jax.ShapeDtypeStruct(h.shape, F32) for h in halves],
        scratch_shapes=[pltpu.SemaphoreType.DMA((n,)), pltpu.SemaphoreType.DMA((n,))],
        name="pair_share",
    )(*halves)


def _all_sum_small(part):
    def body(p_ref, o_ref, slots, send_sems, recv_sems):
        x, y, c = _place()
        me = 4 * x + 2 * y + c
        slots[me] = p_ref[...]
        peers = [(x ^ dx, y ^ dy, c ^ dc) for dx in (0, 1) for dy in (0, 1) for dc in (0, 1)][1:]
        sends = [_remote(p_ref, slots.at[me], send_sems.at[k], recv_sems.at[k], peer) for k, peer in enumerate(peers)]
        for cp in sends:
            cp.start()
        for k, (px, py, pc) in enumerate(peers):
            _remote(p_ref, slots.at[4 * px + 2 * py + pc], send_sems.at[k], recv_sems.at[k], (px, py, pc)).wait_recv()
        for cp in sends:
            cp.wait_send()
        acc = slots[0]
        for d in range(1, 8):
            acc = acc + slots[d]
        o_ref[...] = acc

    vmem = pl.BlockSpec(memory_space=pltpu.VMEM)
    return pl.pallas_call(
        body,
        in_specs=[vmem],
        out_specs=vmem,
        out_shape=jax.ShapeDtypeStruct(part.shape, F32),
        scratch_shapes=[pltpu.VMEM((8,) + part.shape, F32), pltpu.SemaphoreType.DMA((7,)), pltpu.SemaphoreType.DMA((7,))],
        name="all_sum_small",
    )(part)


def _deinterleave(a, d):
    b, s, c = a.shape
    return a.reshape(b, s // d, d, c).transpose(0, 2, 1, 3).reshape(b * s // BAND, BAND, c)


def _reinterleave(a, d, b):
    c = a.shape[-1]
    return a.reshape(b, d, SEQ // d, c).transpose(0, 2, 1, 3).reshape(b, SEQ, c)


def _rope_tables():
    half = HEAD_DIM // 2
    inv_freq = np.float32(ROPE_THETA) ** (-np.arange(half, dtype=np.float32) * np.float32(2.0) / np.float32(HEAD_DIM))
    ang = np.arange(SEQ, dtype=np.float32)[:, None] * inv_freq[None, :].astype(np.float32)
    cos = np.tile(np.cos(ang).astype(np.float32), (1, 2 * BAND_HEADS))
    sin = np.tile(np.concatenate([-np.sin(ang), np.sin(ang)], axis=1).astype(np.float32), (1, BAND_HEADS))
    return jnp.asarray(cos), jnp.asarray(sin)


def _band_groups():
    out = []
    for d in DIL_D:
        rows = SEQ // d
        cls = max(1, 512 // rows) if d > 1 else 1
        out.append(dict(rows=rows, cls=cls, steps=d // cls))
    return out


def _local_step(x, mem, loss_target, g_pre_mix, g_post_mix, g_pre_ffn, g_post_ffn, g_mem, b_gate, w, comm=None):
    bl = x.shape[0]
    t = bl * SEQ
    chips = range(N_CHIPS)
    half_ff = D_FF // 2

    def with_gathered(w, names, gathered, shards):
        return {**w, **{name: lax.dynamic_update_slice(g, s[None], (comm["me"][0], 0, 0))
                        for name, g, s in zip(names, gathered, shards)}}

    x2 = x.reshape(t, D_MODEL)
    tgt2 = loss_target.reshape(t, D_MODEL)
    mem2 = mem.reshape(bl * MEM_LEN, D_MODEL)

    h = _norm_fwd(x2, g_pre_mix, name="norm_x", side=("gather", comm["first_shards"]) if comm else None)
    if comm:
        w = with_gathered(w, comm["first_names"], h[1], comm["first_shards"])
        h = h[0]
    w_in_full = _join_shards(w["w_in"])
    proj = _mm([(h, w_in_full)], nt=False, tn=2176, out_dtypes=[BF16], name="proj",
               side=("gather", comm["mid_shards"]) if comm else None)
    if comm:
        w = with_gathered(w, comm["mid_names"], proj[1], comm["mid_shards"])
        proj = proj[0]
    w_mem_kv_full = w["w_mem_kv"].reshape(D_MODEL, 2 * MEM_W)
    gates = _mm([(h, w["w_gate"], None, "j")], nt=False,tn=w["w_gate"].shape[2], out_dtypes=[BF16], name="gates",
                bias=b_gate, epilogue=lambda acc: (_sigmoid(acc),))
    hm = _norm_fwd(mem2, g_mem, name="norm_mem")
    kv_m = _mm([(hm, w_mem_kv_full)], nt=False,tn=1024, out_dtypes=[BF16], name="mem_kv")
    proj3 = proj.reshape(bl, SEQ, D_IN)
    kv3 = kv_m.reshape(bl, MEM_LEN, 2 * MEM_W)

    o_a, o_a32, sb_weights, late_gathered = _sb_fwd(proj3, comm["late_shards"] if comm else [])
    if comm:
        w = with_gathered(w, comm["late_names"], late_gathered, comm["late_shards"])
    w_o_full = w["w_o"].reshape(D_MODEL, D_MODEL)
    w_ffn_out_full = w["w_ffn_out"].reshape(D_FF, D_MODEL)

    cos_t, sin_t = _rope_tables()
    dil0 = 3 * SB_W

    grp_w = 3 * DIL_W
    band = []
    for g, (d, cfg) in enumerate(zip(DIL_D, _band_groups())):
        a_g = proj3 if d == 1 else proj3[:, :, dil0 + g * grp_w:dil0 + (g + 1) * grp_w].reshape(bl, SEQ // d, d * grp_w)
        band.append(dict(cfg, a=a_g, col0=dil0 // grp_w if d == 1 else None, cos=cos_t.reshape(SEQ // d, d * DIL_W),
                         sin=sin_t.reshape(SEQ // d, d * DIL_W)))
    outs = [_band_group_fwd(b["a"], b["cos"], b["sin"], rows=b["rows"], cls=b["cls"], steps=b["steps"], col0=b["col0"],
                            name=f"band_fwd_{g}") for g, b in enumerate(band)]
    o_b, lse_b = _band_merge3([(o.reshape(t, DIL_W), l.reshape(t, LANES)) for o, l in outs])

    o_c = _mem_fwd(proj3, kv3)

    o_a2, o_c2 = o_a.reshape(t, SB_W), o_c.reshape(t, MEM_W)
    y_a, y_b, y_c, merged = _branch_merge_fwd(o_a2, o_b, o_c2, w["w_br_sb"], w["w_br_dil"], w["w_br_mem"], gates)
    mix = _mm([(merged, w_o_full)], nt=False,tn=1024, out_dtypes=[F32], name="mix")
    x1, h2 = _mid_fwd(mix, x2, g_post_mix, g_pre_ffn)
    gg, uu, f = _ffn_in_fwd(h2, w["w_ffn_in"])
    f2 = _mm([(f, w_ffn_out_full)], nt=False,tn=1024, out_dtypes=[F32], name="ffn_out")

    dy, df2, dg_post_ffn, loss_row = _loss_bwd(f2, x1, g_post_ffn, tgt2)

    dg_ffn, du_ffn = _mm([(df2, w_ffn_out_full)], nt=True,tn=half_ff, out_dtypes=[BF16, BF16], name="d_ffn_act",
                         extras=(gg, uu), epilogue=_swiglu_bwd_epilogue)
    gw = {}
    gw["w_ffn_out"] = _mm_tn(f, df2, tm=half_ff, tn=1024, name="gw_ffn_out").reshape(N_CHIPS, D_FF // N_CHIPS, D_MODEL)
    gw_ffn_g = _mm_tn(h2, dg_ffn, tm=1024, tn=half_ff, name="gw_ffn_gate", out_shards=True, slots=(N_CHIPS, 0))
    gw["w_ffn_in"] = _mm_tn(h2, du_ffn, tm=1024, tn=half_ff, name="gw_ffn_up", out_shards=True, slots=(N_CHIPS, 2), into=gw_ffn_g)
    dh2 = _mm([(dg_ffn, w["w_ffn_in"], 0, 0), (dg_ffn, w["w_ffn_in"], 1, 1), (du_ffn, w["w_ffn_in"], 0, 2),
               (du_ffn, w["w_ffn_in"], 1, 3)], nt=True, tn=1024, out_dtypes=[BF16], name="d_h2")
    dx1, dmix, dg_pre_ffn, dg_post_mix = _mid_bwd(dh2, x1, mix, g_pre_ffn, g_post_mix, dy)

    gw["w_o"] = _mm_tn(merged, dmix, tm=1024, tn=1024, name="gw_o").reshape(N_CHIPS, D_MODEL // N_CHIPS, D_MODEL)
    dmerged = _mm([(dmix, w_o_full)], nt=True, tn=1024, out_dtypes=[BF16], name="d_merged")
    dy_a, dy_b, dy_c, dgpre, db_gate = _gate_bwd(dmerged, gates, y_a, y_b, y_c)
    br_cols = D_MODEL // N_CHIPS
    gw["w_br_sb"] = _mm_tn(o_a2, dy_a, tm=512, tn=br_cols, name="gw_br_sb", out_shards=True, group=N_CHIPS)
    gw["w_br_dil"] = _mm_tn(o_b, dy_b, tm=256, tn=br_cols, name="gw_br_dil", out_shards=True, group=N_CHIPS)
    gw["w_br_mem"] = _mm_tn(o_c2, dy_c, tm=512, tn=br_cols, name="gw_br_mem", out_shards=True, group=N_CHIPS)
    gw["w_gate"] = _mm_tn(h, dgpre, tm=1024, tn=w["w_gate"].shape[2], name="gw_gate", out_shards=True, group=2)
    do_a = _mm([(dy_a, w["w_br_sb"], s, s) for s in chips], nt=True,tn=SB_W, out_dtypes=[BF16], name="d_o_a")
    do_b = _mm([(dy_b, w["w_br_dil"], s, s) for s in chips], nt=True,tn=DIL_W, out_dtypes=[BF16], name="d_o_b")
    do_c = _mm([(dy_c, w["w_br_mem"], s, s) for s in chips], nt=True,tn=MEM_W, out_dtypes=[BF16], name="d_o_c")

    dq_c, dkv_m = _mem_bwd(proj3, kv3, do_c.reshape(bl, SEQ, MEM_W))
    dkv_m = dkv_m.reshape(bl * MEM_LEN, 2 * MEM_W).astype(BF16)
    gw["w_mem_kv"] = _mm_tn(hm, dkv_m, tm=1024, tn=1024, name="gw_mem_kv").reshape(N_CHIPS, D_MODEL // N_CHIPS, 2 * MEM_W)
    dhm = _mm([(dkv_m, w_mem_kv_full)], nt=True,tn=1024, out_dtypes=[F32], name="d_hm")
    dg_mem = _mem_norm_bwd(dhm, mem2, g_mem)

    stats = _band_delta(do_b, o_b, lse_b)
    early = [name for name, _, _ in PACK if name != "w_in"] if comm else []
    grads = [gw[name] for name in early]
    d_dil = []
    for g, (d, b) in enumerate(zip(DIL_D, band)):
        out = _band_group_bwd(b["a"], do_b.reshape(bl, SEQ // d, d * DIL_W), stats.reshape(bl, SEQ // d, d * LANES),
                              b["cos"], b["sin"], rows=b["rows"], cls=b["cls"], steps=b["steps"], col0=b["col0"],
                              name=f"band_bwd_{g}", side=("pair", grads) if comm and g == 0 else None)
        if comm and g == 0:
            out, lands = out
        d_dil.append(out.reshape(bl, SEQ, grp_w))

    parts = [_pair_add(g, l, comm["c"], name="pair_add_" + name) for name, g, l in zip(early, grads, lands)] if comm else []
    dq_a, dk_a, dv_a, lands = _sb_bwd(proj3, o_a32, do_a.reshape(bl, SEQ, SB_W), sb_weights, parts)
    reduced = {name: (p, l) for name, p, l in zip(early, parts, lands)}

    in_cols = D_IN // N_CHIPS
    dproj_s = _split_to_shards([a.reshape(t, a.shape[-1]) for a in [dq_a, dk_a, dv_a] + d_dil + [dq_c]], name="dproj_shards")
    gw["w_in"] = _mm_tn(h, dproj_s, tm=1024, tn=in_cols, name="gw_in", group=2)
    if comm:
        land = _pair_exchange([gw["w_in"]], name="pair_exchange_w_in")[0]
        part_in = _pair_add(gw["w_in"], land, comm["c"], name="pair_add_w_in")
    dh = _mm([(dproj_s, w["w_in"], s, s) for s in chips] + [(dgpre, w["w_gate"], s, s) for s in chips],
             nt=True, tn=1024, out_dtypes=[BF16], name="d_h", side=("chip", [part_in]) if comm else None)
    if comm:
        dh, (land_in,) = dh
        reduced["w_in"] = (part_in, land_in)
    grad_x, dg_pre_mix = _first_bwd(dh, x2, g_pre_mix, dx1)
    small = jnp.concatenate([dg_pre_mix, dg_post_mix, dg_pre_ffn, dg_post_ffn, dg_mem, db_gate.reshape(3, D_MODEL)], axis=0)
    return loss_row[0, 0], grad_x.reshape(bl, SEQ, D_MODEL), gw, small, reduced


def kernel(x, mem, g_pre_mix, g_post_mix, g_pre_ffn, g_post_ffn, g_mem, w_in, w_mem_kv, w_br_sb, w_br_dil, w_br_mem, w_gate, b_gate, w_o, w_ffn_in, w_ffn_out, loss_target, m_g_pre_mix, m_g_post_mix, m_g_pre_ffn, m_g_post_ffn, m_g_mem, m_w_in, m_w_mem_kv, m_w_br_sb, m_w_br_dil, m_w_br_mem, m_w_gate, m_b_gate, m_w_o, m_w_ffn_in, m_w_ffn_out, v_g_pre_mix, v_g_post_mix, v_g_pre_ffn, v_g_post_ffn, v_g_mem, v_w_in, v_w_mem_kv, v_w_br_sb, v_w_br_dil, v_w_br_mem, v_w_gate, v_b_gate, v_w_o, v_w_ffn_in, v_w_ffn_out):
    w_shards = dict(w_in=w_in[0], w_mem_kv=w_mem_kv[0], w_br_sb=w_br_sb[0], w_br_dil=w_br_dil[0], w_br_mem=w_br_mem[0],
                    w_gate=w_gate[0], w_o=w_o[0], w_ffn_in=w_ffn_in[0], w_ffn_out=w_ffn_out[0])
    m_shards = dict(w_in=m_w_in[0], w_mem_kv=m_w_mem_kv[0], w_br_sb=m_w_br_sb[0], w_br_dil=m_w_br_dil[0], w_br_mem=m_w_br_mem[0],
                    w_gate=m_w_gate[0], w_o=m_w_o[0], w_ffn_in=m_w_ffn_in[0], w_ffn_out=m_w_ffn_out[0])
    v_shards = dict(w_in=v_w_in[0], w_mem_kv=v_w_mem_kv[0], w_br_sb=v_w_br_sb[0], w_br_dil=v_w_br_dil[0], w_br_mem=v_w_br_mem[0],
                    w_gate=v_w_gate[0], w_o=v_w_o[0], w_ffn_in=v_w_ffn_in[0], w_ffn_out=v_w_ffn_out[0])

    names = [name for name, _, _ in PACK]
    c_arr = lax.axis_index("c").astype(jnp.int32).reshape(1)
    me_arr = (2 * lax.axis_index("x") + lax.axis_index("y")).astype(jnp.int32).reshape(1)
    mid_names = ["w_gate", "w_mem_kv"]
    late_names = [name for name in names if name not in ["w_in"] + mid_names]
    bf = {name: w_shards[name].astype(BF16) for name in names}
    comm = dict(c=c_arr, me=me_arr, first_names=["w_in"], first_shards=[bf["w_in"]],
                mid_names=mid_names, mid_shards=[bf[name] for name in mid_names],
                late_names=late_names, late_shards=[bf[name] for name in late_names])

    loss_local, grad_x, gw, small, reduced = _local_step(x, mem, loss_target, g_pre_mix, g_post_mix, g_pre_ffn, g_post_ffn,
                                                         g_mem, b_gate, {}, comm)
    loss = lax.psum(loss_local, ("x", "y", "c"))

    halves =[_chip_add(reduced[name][1], reduced[name][0], me_arr, name="chip_add_" + name) for name in names]
    theirs = _pair_share(halves)
    small = _all_sum_small(small)

    upd = {}
    for name, mine, other in zip(names, halves, theirs):
        upd[name] = _adamw_halves(w_shards[name], mine, other, m_shards[name], v_shards[name], c_arr, name="adamw_" + name)
    g_shards = {name: u[0] for name, u in upd.items()}

    def small8(gs, b):
        return jnp.concatenate(gs + [b.reshape(3, D_MODEL)], axis=0)

    sw = small8([g_pre_mix, g_post_mix, g_pre_ffn, g_post_ffn, g_mem], b_gate)
    sm = small8([m_g_pre_mix, m_g_post_mix, m_g_pre_ffn, m_g_post_ffn, m_g_mem], m_b_gate)
    sv = small8([v_g_pre_mix, v_g_post_mix, v_g_pre_ffn, v_g_post_ffn, v_g_mem], v_b_gate)
    s_upd = _adamw(sw, small, sm, sv, tm=8, name="adamw_small")

    def small_out(a):
        return [a[0:1], a[1:2], a[2:3], a[3:4], a[4:5]]

    order = ["w_in", "w_mem_kv", "w_br_sb", "w_br_dil", "w_br_mem", "w_gate", "b_gate", "w_o", "w_ffn_in", "w_ffn_out"]

    def leaves(small_arr, big):
        out = small_out(small_arr)
        for name in order:
            out.append(small_arr[5:8].reshape(1, 3 * D_MODEL) if name == "b_gate" else big[name][None])
        return out

    grads_out = leaves(small, g_shards)
    delta_out = leaves(s_upd[0], {n: u[1] for n, u in upd.items()})
    m_out = leaves(s_upd[1], {n: u[2] for n, u in upd.items()})
    v_out = leaves(s_upd[2], {n: u[3] for n, u in upd.items()})
    return (loss, grad_x, *grads_out, *delta_out, *m_out, *v_out)
```

```python
import jax
import jax.numpy as jnp
import numpy as np
from jax import lax
from jax.experimental import pallas as pl
from jax.experimental.pallas import tpu as pltpu

F32 = jnp.float32
BF16 = jnp.bfloat16
MESH = pl.DeviceIdType.MESH

D_MODEL = 1024
SEQ = 2048
HEAD_DIM = 64
SB_W = 512
DIL_W = 256
MEM_W = 512
MEM_LEN = 256
D_IN = 3 * SB_W + 9 * DIL_W + MEM_W
D_FF = 2816
DIL_D = (1, 4, 16)
ROPE_THETA = 10000.0
NORM_EPS = 1e-6
NEG_INF = -1e30
LANES = 128

ADAM_LR = 0.001
ADAM_B1 = 0.9
ADAM_B2 = 0.999
ADAM_EPS = 1e-08
ADAM_WD = 0.01
ADAM_STEP = 10

N_CHIPS = 4
PACK = (
    ("w_in", (1024, 1088), 1),
    ("w_mem_kv", (256, 1024), 0),
    ("w_br_sb", (512, 256), 1),
    ("w_br_dil", (256, 256), 1),
    ("w_br_mem", (512, 256), 1),
    ("w_gate", (1024, 768), 1),
    ("w_o", (256, 1024), 0),
    ("w_ffn_in", (1024, 1408), 1),
    ("w_ffn_out", (704, 1024), 0),
)
PACK_ROWS = sum(a * b for _, (a, b), _ in PACK) // D_MODEL
HALF_ROWS = PACK_ROWS // 2


def _dot(a, b):
    return lax.dot_general(a, b, (((1,), (0,)), ((), ())), preferred_element_type=F32)


def _dot_nt(a, b):
    return lax.dot_general(a, b, (((1,), (1,)), ((), ())), preferred_element_type=F32)


def _dot_tn(a, b):
    return lax.dot_general(a, b, (((0,), (0,)), ((), ())), preferred_element_type=F32)


def _split_dot(x, u):
    hi = x.astype(BF16)
    lo = (x - hi.astype(F32)).astype(BF16)
    return _dot(hi, u) + _dot(lo, u)


V7X_VMEM_BUDGET = 44 * 2 ** 20


def _rows_that_fit(m, row_bytes, fixed_bytes):
    for tm in (1024, 512, 256, 128):
        if m % tm == 0 and fixed_bytes + tm * row_bytes <= V7X_VMEM_BUDGET:
            return tm
    return min(m, 128)


def _mm(pairs, *, nt, tn, out_dtypes, name, bias=None, extras=(), epilogue=None, side=None):
    pairs = [p if len(p) == 4 else (p[0], p[1], None, None) for p in pairs]
    m = pairs[0][0].shape[-2]
    b0 = pairs[0][1]
    if nt:
        n = b0.shape[-2]
    else:
        n = b0.shape[-1] * (b0.shape[0] if b0.ndim == 3 else 1)
    n_pairs, n_extra, n_out = len(pairs), len(extras), len(out_dtypes)
    assert n % tn == 0
    one_col = n == tn
    ks = [(b.shape[-1] if nt else b.shape[-2]) for _, b, _, _ in pairs]
    fixed = sum(k * tn * 2 for k in ks) * (1 if one_col else 2)
    row_bytes = 2 * sum(k * 2 for k in ks) + 2 * tn * (sum(jnp.dtype(dt).itemsize for dt in out_dtypes) + 2 * n_extra) + 2 * tn * 4
    tm = _rows_that_fit(m, row_bytes, fixed)
    assert m % tm == 0
    b_mode = dict(pipeline_mode=pl.Buffered(1)) if one_col else {}
    has_bias = bias is not None
    side_arrays = side[1] if side else []
    n_side = len(side_arrays)
    n_main_in = 2 * n_pairs + has_bias + n_extra
    n_steps = (n // tn) * (m // tm)

    def body(*refs):
        if n_side:
            step = pl.program_id(0) * (m // tm) + pl.program_id(1)
            finish = _run_side(side, refs[n_main_in:n_main_in + n_side],
                               refs[n_main_in + n_side + n_out:n_main_in + 2 * n_side + n_out],
                               refs[n_main_in + 2 * n_side + n_out:], step, n_steps)
        outs = refs[n_main_in + n_side:n_main_in + n_side + n_out]
        acc = None
        for i in range(n_pairs):
            a, b = refs[2 * i][...], refs[2 * i + 1][...]
            p = _dot_nt(a, b) if nt else _dot(a, b)
            acc = p if acc is None else acc + p
        pos = 2 * n_pairs
        if has_bias:
            acc = acc + refs[pos][...]
            pos += 1
        ex = [r[...] for r in refs[pos:pos + n_extra]]
        vals = (acc,) if epilogue is None else epilogue(acc, *ex)
        for r, v, dt in zip(outs, vals, out_dtypes):
            r[...] = v.astype(dt)
        if n_side:
            finish()

    in_specs, args = [], []
    for a, b, a_col, b_sel in pairs:
        k = b.shape[-1] if nt else b.shape[-2]
        assert a_col is not None or a.shape[1] == k
        if a.ndim == 3:
            in_specs.append(pl.BlockSpec((None, tm, k), lambda j, i, c=a_col: (c, i, 0)))
        else:
            in_specs.append(pl.BlockSpec((tm, k), lambda j, i, c=a_col or 0: (i, c)))
        if b.ndim == 2:
            in_specs.append(pl.BlockSpec((tn, k), lambda j, i: (j, 0), **b_mode) if nt
                            else pl.BlockSpec((k, tn), lambda j, i: (0, j), **b_mode))
        elif nt:
            in_specs.append(pl.BlockSpec((None, tn, k), lambda j, i, s=b_sel: (s, j, 0), **b_mode))
        else:
            assert b_sel == "j" and b.shape[-1] == tn
            in_specs.append(pl.BlockSpec((None, k, tn), lambda j, i: (j, 0, 0), **b_mode))
        args += [a, b]
    if has_bias:
        in_specs.append(pl.BlockSpec((1, tn), lambda j, i: (0, j)))
        args.append(bias)
    for e in extras:
        in_specs.append(pl.BlockSpec((tm, tn), lambda j, i: (i, j)))
        args.append(e)
    out = pl.pallas_call(
        body,
        grid=(n // tn, m // tm),
        in_specs=in_specs + [ANY] * n_side,
        out_specs=[pl.BlockSpec((tm, tn), lambda j, i: (i, j)) for _ in range(n_out)] + [ANY] * n_side,
        out_shape=[jax.ShapeDtypeStruct((m, n), dt) for dt in out_dtypes] + (_side_out_shapes(side) if n_side else []),
        scratch_shapes=_side_sems(side) if n_side else [],
        compiler_params=pltpu.CompilerParams(dimension_semantics=("arbitrary", "arbitrary") if n_side else ("parallel", "parallel")),
        name=name,
    )(*args, *side_arrays)
    if n_side:
        return (out[0] if n_out == 1 else out[:n_out]), out[n_out:]
    return out[0] if n_out == 1 else out


def _mm_tn(a, b, *, tm, tn, name, out_shards=False, slots=None, into=None, group=1):
    k, m = a.shape
    b_shards = b.ndim == 3
    out_shards = out_shards or b_shards
    n = b.shape[0] * b.shape[2] if b_shards else b.shape[1]
    tk = _rows_that_fit(k, 2 * 2 * (tm + group * tn), 3 * tm * group * tn * 4)
    assert m % tm == 0 and n % (group * tn) == 0 and k % tk == 0 and (not b_shards or b.shape[2] == tn)
    assert group == 1 or out_shards
    total, first = slots if slots else (n // tn, 0)

    def body(a_ref, b_ref, *rest):
        o_ref = rest[-1]

        @pl.when(pl.program_id(2) == 0)
        def _():
            o_ref[...] = jnp.zeros_like(o_ref)

        if group == 1:
            o_ref[...] += _dot_tn(a_ref[...], b_ref[...])
        elif b_shards:
            a_blk = a_ref[...]
            for s in range(group):
                o_ref[s] += _dot_tn(a_blk, b_ref[s])
        else:
            acc = _dot_tn(a_ref[...], b_ref[...])
            for s in range(group):
                o_ref[s] += acc[:, s * tn:(s + 1) * tn]

    lead = None if group == 1 else group
    if b_shards:
        b_spec = pl.BlockSpec((lead, tk, tn), lambda i, j, kk: (j, kk, 0))
    else:
        b_spec = pl.BlockSpec((tk, group * tn), lambda i, j, kk: (kk, j))
    if out_shards:
        out_spec = pl.BlockSpec((lead, tm, tn), lambda i, j, kk: (j + first // group, i, 0))
        out_shape = jax.ShapeDtypeStruct((total, m, tn), F32)
    else:
        out_spec = pl.BlockSpec((tm, tn), lambda i, j, kk: (i, j))
        out_shape = jax.ShapeDtypeStruct((m, n), F32)
    return pl.pallas_call(
        body,
        grid=(m // tm, n // (group * tn), k // tk),
        in_specs=[pl.BlockSpec((tk, tm), lambda i, j, kk: (kk, i)), b_spec] + ([ANY] if into is not None else []),
        out_specs=out_spec,
        out_shape=out_shape,
        input_output_aliases={2: 0} if into is not None else {},
        compiler_params=pltpu.CompilerParams(dimension_semantics=("parallel", "parallel", "arbitrary")),
        name=name,
    )(*([a, b] + ([into] if into is not None else [])))


def _rowwise(fn, ins, outs, *, tm, name, side=None):
    rows = next(a.shape[0] for a, kind in ins if kind == "row")
    tm = min(tm, rows)
    assert rows % tm == 0
    n_in, n_out = len(ins), len(outs)
    side_arrays = side[1] if side else []
    n_side = len(side_arrays)

    def body(*refs):
        if n_side:
            finish = _run_side(side, refs[n_in:n_in + n_side], refs[n_in + n_side + n_out:n_in + 2 * n_side + n_out],
                               refs[n_in + 2 * n_side + n_out:], pl.program_id(0), rows // tm)
        vals = fn(*[r[...] for r in refs[:n_in]])
        for (_, dt, kind), r, v in zip(outs, refs[n_in + n_side:n_in + n_side + n_out], vals):
            if kind == "row":
                r[...] = v.astype(dt)
            else:
                @pl.when(pl.program_id(0) == 0)
                def _(r=r):
                    r[...] = jnp.zeros_like(r)

                r[...] += v
        if n_side:
            finish()

    in_specs = [pl.BlockSpec((tm, a.shape[1]), lambda i: (i, 0)) if kind == "row" else pl.BlockSpec(a.shape, lambda i: (0, 0))
                for a, kind in ins]
    out_specs = [pl.BlockSpec((tm, c), lambda i: (i, 0)) if kind == "row" else pl.BlockSpec((1, c), lambda i: (0, 0))
                 for c, _, kind in outs]
    out_shape = [jax.ShapeDtypeStruct((rows if kind == "row" else 1, c), dt) for c, dt, kind in outs]
    ordered = n_side or any(kind == "acc" for _, _, kind in outs)
    return pl.pallas_call(
        body,
        grid=(rows // tm,),
        in_specs=in_specs + [ANY] * n_side,
        out_specs=out_specs + [ANY] * n_side,
        out_shape=out_shape + (_side_out_shapes(side) if n_side else []),
        scratch_shapes=_side_sems(side) if n_side else [],
        compiler_params=pltpu.CompilerParams(dimension_semantics=("arbitrary" if ordered else "parallel",)),
        name=name,
    )(*[a for a, _ in ins], *side_arrays)


def _rstd(x):
    return lax.rsqrt(jnp.mean(x * x, axis=-1, keepdims=True) + NORM_EPS)


def _norm_bwd(dout, xin, g):
    r = _rstd(xin)
    n = xin * r
    dn = dout * g
    dg = jnp.sum(dout * n, axis=0, keepdims=True)
    dx = r * (dn - n * jnp.mean(dn * n, axis=-1, keepdims=True))
    return dx, dg


def _sigmoid(x):
    return 0.5 * jnp.tanh(0.5 * x) + 0.5


def _norm_fwd(x, g, *, name, side=None):
    def fn(x, g):
        return ((x * _rstd(x)) * g,)

    out = _rowwise(fn, [(x, "row"), (g, "vec")], [(D_MODEL, BF16, "row")], tm=512, name=name, side=side)
    return (out[0], out[1:]) if side else out[0]


def _mid_fwd(mix, x, g_post_mix, g_pre_ffn):
    def fn(mix, x, g2, g3):
        x1 = x + (mix * _rstd(mix)) * g2
        return x1, (x1 * _rstd(x1)) * g3

    return _rowwise(fn, [(mix, "row"), (x, "row"), (g_post_mix, "vec"), (g_pre_ffn, "vec")],
                    [(D_MODEL, F32, "row"), (D_MODEL, BF16, "row")], tm=512, name="mid_fwd")


def _loss_bwd(f2, x1, g_post_ffn, tgt):
    def fn(f2, x1, g4, tgt):
        r = _rstd(f2)
        n = f2 * r
        err = x1 + n * g4 - tgt
        loss = 0.5 * jnp.sum(jnp.mean(err * err, axis=-1, keepdims=True), axis=0, keepdims=True)
        dy = err * (1.0 / D_MODEL)
        dn = dy * g4
        dg4 = jnp.sum(dy * n, axis=0, keepdims=True)
        df2 = r * (dn - n * jnp.mean(dn * n, axis=-1, keepdims=True))
        return dy, df2, dg4, jnp.broadcast_to(loss, (1, LANES))

    return _rowwise(fn, [(f2, "row"), (x1, "row"), (g_post_ffn, "vec"), (tgt, "row")],
                    [(D_MODEL, BF16, "row"), (D_MODEL, BF16, "row"), (D_MODEL, F32, "acc"), (LANES, F32, "acc")],
                    tm=512, name="loss_bwd")


def _mid_bwd(dh2, x1, mix, g_pre_ffn, g_post_mix, dy):
    def fn(dh2, x1, mix, g3, g2, dy):
        d3, dg3 = _norm_bwd(dh2.astype(F32), x1, g3)
        dx1 = dy.astype(F32) + d3
        dmix, dg2 = _norm_bwd(dx1, mix, g2)
        return dx1, dmix, dg3, dg2

    return _rowwise(fn, [(dh2, "row"), (x1, "row"), (mix, "row"), (g_pre_ffn, "vec"), (g_post_mix, "vec"), (dy, "row")],
                    [(D_MODEL, BF16, "row"), (D_MODEL, BF16, "row"), (D_MODEL, F32, "acc"), (D_MODEL, F32, "acc")],
                    tm=256, name="mid_bwd")


def _first_bwd(dh, x, g_pre_mix, dx1):
    def fn(dh, x, g1, dx1):
        d1, dg1 = _norm_bwd(dh.astype(F32), x, g1)
        return dx1.astype(F32) + d1, dg1

    return _rowwise(fn, [(dh, "row"), (x, "row"), (g_pre_mix, "vec"), (dx1, "row")],
                    [(D_MODEL, F32, "row"), (D_MODEL, F32, "acc")], tm=512, name="first_bwd")


def _mem_norm_bwd(dhm, mem, g_mem):
    def fn(dhm, mem, g):
        return (jnp.sum(dhm * (mem * _rstd(mem)), axis=0, keepdims=True),)

    return _rowwise(fn, [(dhm, "row"), (mem, "row"), (g_mem, "vec")], [(D_MODEL, F32, "acc")], tm=512, name="mem_norm_bwd")[0]


def _gate_bwd(dmerged, gates, ya, yb, yc):
    def fn(dm, gt, ya, yb, yc):
        dm = dm.astype(F32)
        gt = gt.astype(F32)
        outs, dgp = [], []
        for i, y in enumerate((ya, yb, yc)):
            gi = gt[:, i * D_MODEL:(i + 1) * D_MODEL]
            outs.append(dm * gi)
            dgp.append(dm * y.astype(F32) * gi * (1.0 - gi))
        dgpre = jnp.concatenate(dgp, axis=1)
        return outs[0], outs[1], outs[2], dgpre, jnp.sum(dgpre, axis=0, keepdims=True)

    return _rowwise(fn, [(dmerged, "row"), (gates, "row"), (ya, "row"), (yb, "row"), (yc, "row")],
                    [(D_MODEL, BF16, "row")] * 3 + [(3 * D_MODEL, BF16, "row"), (3 * D_MODEL, F32, "acc")],
                    tm=256, name="gate_bwd")


def _adamw_math(w, g, m, v):
    m = ADAM_B1 * m + (1.0 - ADAM_B1) * g
    v = ADAM_B2 * v + (1.0 - ADAM_B2) * (g * g)
    m_hat = m / (1.0 - ADAM_B1 ** ADAM_STEP)
    v_hat = v / (1.0 - ADAM_B2 ** ADAM_STEP)
    delta = -ADAM_LR * (m_hat / (jnp.sqrt(v_hat) + ADAM_EPS) + ADAM_WD * w)
    return delta, m, v


def _adamw(w, g, m, v, *, tm, name):
    c = w.shape[1]
    return _rowwise(_adamw_math, [(w, "row"), (g, "row"), (m, "row"), (v, "row")], [(c, F32, "row")] * 3, tm=tm, name=name)


def _adamw_halves(w, g_mine, g_theirs, m, v, c_arr, *, name):
    a, b = w.shape
    hf = a // 2
    tr = next(t for t in (hf, hf // 2, hf // 4) if 9 * 3 * t * b * 4 <= V7X_VMEM_BUDGET)

    def body(c_ref, w_ref, gm_ref, gt_ref, m_ref, v_ref, g_out, d_out, m_out, v_out):
        g = jnp.where(pl.program_id(0) == c_ref[0], gm_ref[...], gt_ref[...])
        d, m_new, v_new = _adamw_math(w_ref[...], g, m_ref[...], v_ref[...])
        g_out[...] = g
        d_out[...] = d
        m_out[...] = m_new
        v_out[...] = v_new

    full = pl.BlockSpec((tr, b), lambda hh, i, c_ref: (hh * (hf // tr) + i, 0))
    half = pl.BlockSpec((tr, b), lambda hh, i, c_ref: (i, 0))
    return pl.pallas_call(
        body,
        grid_spec=pltpu.PrefetchScalarGridSpec(
            num_scalar_prefetch=1,
            grid=(2, hf // tr),
            in_specs=[full, half, half, full, full],
            out_specs=[full] * 4,
        ),
        out_shape=[jax.ShapeDtypeStruct((a, b), F32)] * 4,
        compiler_params=pltpu.CompilerParams(dimension_semantics=("parallel", "parallel")),
        name=name,
    )(c_arr, w, g_mine, g_theirs, m, v)


def _ffn_in_fwd(h2, w_ffn):
    m, tm, tn = h2.shape[0], 512, w_ffn.shape[2]
    assert 2 * tn == D_FF

    def body(h_ref, wg_ref, wu_ref, g_ref, u_ref, f_ref):
        h = h_ref[...]
        g = _dot(h, wg_ref[...])
        u = _dot(h, wu_ref[...])
        g_ref[...] = g.astype(BF16)
        u_ref[...] = u.astype(BF16)
        f_ref[...] = (g * _sigmoid(g) * u).astype(BF16)

    o_spec = pl.BlockSpec((tm, tn), lambda j, i: (i, j))
    return pl.pallas_call(
        body,
        grid=(D_FF // tn, m // tm),
        in_specs=[pl.BlockSpec((tm, D_MODEL), lambda j, i: (i, 0)),
                  pl.BlockSpec((None, D_MODEL, tn), lambda j, i: (j, 0, 0)),
                  pl.BlockSpec((None, D_MODEL, tn), lambda j, i: (j + 2, 0, 0))],
        out_specs=[o_spec, o_spec, o_spec],
        out_shape=[jax.ShapeDtypeStruct((m, D_FF), BF16)] * 3,
        compiler_params=pltpu.CompilerParams(dimension_semantics=("parallel", "parallel")),
        name="ffn_in_fwd",
    )(h2, w_ffn, w_ffn)


def _join_shards(w4):
    _, rows, cols = w4.shape
    tr = rows // 4

    def body(w_ref, o_ref):
        for s in range(N_CHIPS):
            o_ref[:, s * cols:(s + 1) * cols] = w_ref[s]

    return pl.pallas_call(
        body,
        grid=(rows // tr,),
        in_specs=[pl.BlockSpec((N_CHIPS, tr, cols), lambda i: (0, i, 0))],
        out_specs=pl.BlockSpec((tr, N_CHIPS * cols), lambda i: (i, 0)),
        out_shape=jax.ShapeDtypeStruct((rows, N_CHIPS * cols), w4.dtype),
        compiler_params=pltpu.CompilerParams(dimension_semantics=("parallel",)),
        name="join_shards",
    )(w4)


def _split_to_shards(pieces, *, name):
    t = pieces[0].shape[0]
    widths = [p.shape[1] for p in pieces]
    cols = sum(widths) // N_CHIPS
    tm = 512
    plan, start = [], 0
    for p, wd in enumerate(widths):
        for s in range(N_CHIPS):
            lo, hi = max(start, s * cols), min(start + wd, (s + 1) * cols)
            if lo < hi:
                plan.append((s, p, lo - s * cols, hi - s * cols, lo - start, hi - start))
        start += wd

    def body(*refs):
        o_ref = refs[-1]
        for s, p, o_lo, o_hi, p_lo, p_hi in plan:
            o_ref[s, :, o_lo:o_hi] = refs[p][:, p_lo:p_hi]

    return pl.pallas_call(
        body,
        grid=(t // tm,),
        in_specs=[pl.BlockSpec((tm, wd), lambda i: (i, 0)) for wd in widths],
        out_specs=pl.BlockSpec((N_CHIPS, tm, cols), lambda i: (0, i, 0)),
        out_shape=jax.ShapeDtypeStruct((N_CHIPS, t, cols), pieces[0].dtype),
        compiler_params=pltpu.CompilerParams(dimension_semantics=("parallel",)),
        name=name,
    )(*pieces)


def _swiglu_bwd_epilogue(df, g, u):
    g = g.astype(F32)
    u = u.astype(F32)
    sg = _sigmoid(g)
    return df * u * (sg * (1.0 + g * (1.0 - sg))), df * (g * sg)


def _branch_merge_fwd(o_a, o_b, o_c, w_sb, w_dil, w_mem, gates):
    m, tm = o_a.shape[0], 512

    def body(oa_ref, ob_ref, oc_ref, wa_ref, wb_ref, wc_ref, gt_ref, ya_ref, yb_ref, yc_ref, mg_ref):
        def project(o_ref, w_ref):
            o = o_ref[...]
            return jnp.concatenate([_dot(o, w_ref[s]) for s in range(N_CHIPS)], axis=1)

        ya = project(oa_ref, wa_ref)
        yb = project(ob_ref, wb_ref)
        yc = project(oc_ref, wc_ref)
        gt = gt_ref[...].astype(F32)
        ya_ref[...] = ya.astype(BF16)
        yb_ref[...] = yb.astype(BF16)
        yc_ref[...] = yc.astype(BF16)
        mg_ref[...] = (gt[:, :D_MODEL] * ya + gt[:, D_MODEL:2 * D_MODEL] * yb + gt[:, 2 * D_MODEL:] * yc).astype(BF16)

    row = lambda c: pl.BlockSpec((tm, c), lambda i: (i, 0))
    full = lambda a: pl.BlockSpec(a.shape, lambda i: (0, 0, 0))
    return pl.pallas_call(
        body,
        grid=(m // tm,),
        in_specs=[row(SB_W), row(DIL_W), row(MEM_W), full(w_sb), full(w_dil), full(w_mem), row(3 * D_MODEL)],
        out_specs=[row(D_MODEL)] * 4,
        out_shape=[jax.ShapeDtypeStruct((m, D_MODEL), BF16)] * 4,
        compiler_params=pltpu.CompilerParams(dimension_semantics=("parallel",)),
        name="branch_merge_fwd",
    )(o_a, o_b, o_c, w_sb, w_dil, w_mem, gates)


SB_T = 256
SB_SCALE = HEAD_DIM ** -0.5


def _sb_masks():
    row = lax.broadcasted_iota(jnp.int32, (SB_T, SB_T), 0)
    col = lax.broadcasted_iota(jnp.int32, (SB_T, SB_T), 1)
    lane = lax.broadcasted_iota(jnp.int32, (1, LANES), 1)
    return row, col, lane


def _sb_logs(z):
    lb = jnp.minimum(z, 0.0) - jnp.log(1.0 + jnp.exp(-jnp.abs(z)))
    return lb, lb - z


def _sb_specs(n_heads_pairs, col0):
    q = pl.BlockSpec((None, SB_T, LANES), lambda b, p, i: (b, i, col0 + p))
    k = pl.BlockSpec((None, SEQ, LANES), lambda b, p, i: (b, 0, col0 + n_heads_pairs + p))
    v = pl.BlockSpec((None, SEQ, LANES), lambda b, p, i: (b, 0, col0 + 2 * n_heads_pairs + p))
    return q, k, v


def _sb_first_blocks(i):
    rem = lax.rem(i + 1, 3)
    return jnp.where(rem == 0, 3, rem)


def _grid_step(n_pairs, nq):
    return (pl.program_id(0) * n_pairs + pl.program_id(1)) * nq + pl.program_id(2)


def _sb_fwd(proj3, late_shards):
    bl = proj3.shape[0]
    n_pairs = SB_W // LANES
    nq = SEQ // SB_T
    n_late = len(late_shards)
    n_steps = bl * n_pairs * nq

    def body(q_ref, k_ref, v_ref, *rest):
        late_in, (o_ref, o32_ref, w_ref), late_out = rest[:n_late], rest[n_late:n_late + 3], rest[n_late + 3:2 * n_late + 3]
        step = _grid_step(n_pairs, nq)
        if n_late:
            send, forward, finish = _gather_phases(late_in, late_out, *rest[2 * n_late + 3:])
            pl.when(step == 0)(send)
            pl.when(step == n_steps // 2)(forward)
        i = pl.program_id(2)
        row, col, lane = _sb_masks()
        causal = col < row
        u_excl = (row > col).astype(BF16)
        q = q_ref[...]
        heads = []
        for h in range(2):
            mh = (lane // HEAD_DIM) == h
            heads.append((mh, jnp.where(mh, q, jnp.zeros_like(q)) * SB_SCALE))

        def blocks(js, diags, carries, acc):
            ks = [k_ref[pl.ds(pl.multiple_of(j * SB_T, SB_T), SB_T), :] for j in js]
            vs = [v_ref[pl.ds(pl.multiple_of(j * SB_T, SB_T), SB_T), :] for j in js]
            chains = [(b, h) for b in range(len(js)) for h in range(2)]
            z = {c: _dot_nt(heads[c[1]][1], ks[c[0]]) for c in chains}
            lb, lk = {}, {}
            for c in chains:
                lb[c], lk[c] = _sb_logs(z[c])
                if diags[c[0]]:
                    lk[c] = jnp.where(causal, lk[c], 0.0)
            r = {c: _split_dot(lk[c], u_excl) for c in chains}
            carries = list(carries)
            w = {}
            for b, h in chains:
                w_c = jnp.exp(lb[b, h] + r[b, h] + carries[h])
                w[b, h] = (jnp.where(causal, w_c, 0.0) if diags[b] else w_c).astype(BF16)
                w_ref[h, js[b]] = w[b, h]
                carries[h] = carries[h] + (r[b, h][:, :1] + lk[b, h][:, :1])
            for b, h in chains:
                acc = acc + _dot(w[b, h], jnp.where(heads[h][0], vs[b], jnp.zeros_like(vs[b])))
            return tuple(carries), acc

        zero = jnp.zeros((SB_T, 1), F32)
        init = ((zero, zero), jnp.zeros((SB_T, LANES), F32))
        first = _sb_first_blocks(i)
        carries, acc = lax.cond(
            first == 1, lambda: blocks([i], (True,), *init),
            lambda: lax.cond(first == 2, lambda: blocks([i, i - 1], (True, False), *init),
                             lambda: blocks([i, i - 1, i - 2], (True, False, False), *init)))
        rest = i - first
        carries, acc = lax.fori_loop(
            0, (rest + 1) // 3,
            lambda jj, c: blocks([rest - 3 * jj, rest - 1 - 3 * jj, rest - 2 - 3 * jj], (False, False, False), c[0], c[1]),
            (carries, acc))
        o_ref[...] = acc.astype(BF16)
        o32_ref[...] = acc
        if n_late:
            pl.when(step == n_steps - 1)(finish)

    q_spec, k_spec, v_spec = _sb_specs(n_pairs, 0)
    blk = pl.BlockSpec((None, SB_T, LANES), lambda b, p, i: (b, i, p))
    out = pl.pallas_call(
        body,
        grid=(bl, n_pairs, nq),
        in_specs=[q_spec, k_spec, v_spec] + [ANY] * n_late,
        out_specs=[blk, blk, _sb_weight_spec(nq)] + [ANY] * n_late,
        out_shape=[jax.ShapeDtypeStruct((bl, SEQ, SB_W), BF16), jax.ShapeDtypeStruct((bl, SEQ, SB_W), F32),
                   jax.ShapeDtypeStruct((bl, n_pairs, nq, 2, nq, SB_T, SB_T), BF16)] + _gather_out_shapes(late_shards),
        scratch_shapes=_gather_sems(n_late) if n_late else [],
        compiler_params=pltpu.CompilerParams(dimension_semantics=("arbitrary", "arbitrary", "arbitrary")),
        name="sb_fwd",
    )(proj3, proj3, proj3, *late_shards)
    return out[0], out[1], out[2], out[3:]


def _sb_weight_spec(nq):
    return pl.BlockSpec((None, None, None, 2, nq, SB_T, SB_T), lambda b, p, i: (b, p, i, 0, 0, 0, 0))


def _sb_bwd(proj3, o_a, do_a, w_all, parts):
    bl = proj3.shape[0]
    n_pairs = SB_W // LANES
    nq = SEQ // SB_T
    n_parts = len(parts)
    n_steps = bl * n_pairs * nq

    def body(q_ref, k_ref, v_ref, o_ref, do_ref, w_ref, *rest):
        p_refs, (dq_ref, dk_ref, dv_ref), land_refs = rest[:n_parts], rest[n_parts:n_parts + 3], rest[n_parts + 3:2 * n_parts + 3]
        dk_acc, dv_acc = rest[2 * n_parts + 3:2 * n_parts + 5]
        step = _grid_step(n_pairs, nq)
        if n_parts:
            send, finish = _chip_exchange_phases(p_refs, land_refs, *rest[2 * n_parts + 5:])
            pl.when(step == 0)(send)
        i = pl.program_id(2)

        @pl.when(i == 0)
        def _():
            dk_acc[...] = jnp.zeros_like(dk_acc)
            dv_acc[...] = jnp.zeros_like(dv_acc)

        row, col, lane = _sb_masks()
        causal = col < row
        u_incl = (row >= col).astype(BF16)
        q = q_ref[...]
        do = do_ref[...]
        prod = do.astype(F32) * o_ref[...]
        heads = []
        for h in range(2):
            mh = (lane // HEAD_DIM) == h
            d_tot = jnp.sum(jnp.where(mh, prod, 0.0), axis=1, keepdims=True)
            heads.append((mh, jnp.where(mh, q, jnp.zeros_like(q)) * SB_SCALE, jnp.where(mh, do, jnp.zeros_like(do)), d_tot))

        def blocks(js, diags, c_das, dq):
            starts = [pl.multiple_of(j * SB_T, SB_T) for j in js]
            ks = [k_ref[pl.ds(s, SB_T), :] for s in starts]
            vs = [v_ref[pl.ds(s, SB_T), :] for s in starts]
            chains = [(b, h) for b in range(len(js)) for h in range(2)]
            z = {c: _dot_nt(heads[c[1]][1], ks[c[0]]) for c in chains}
            dw = {c: _dot_nt(heads[c[1]][2], vs[c[0]]) for c in chains}
            wb = {(b, h): w_ref[h, js[b]] for b, h in chains}
            da = {c: dw[c] * wb[c].astype(F32) for c in chains}
            sfx = {c: _split_dot(da[c], u_incl) for c in chains}
            c_das = list(c_das)
            dz = {}
            for b, h in chains:
                dlk = heads[h][3] - c_das[h] - sfx[b, h]
                if diags[b]:
                    dlk = jnp.where(causal, dlk, 0.0)
                c_das[h] = c_das[h] + sfx[b, h][:, :1]
                e = jnp.exp(-jnp.abs(z[b, h]))
                inv = 1.0 / (1.0 + e)
                pos = z[b, h] >= 0.0
                beta = jnp.where(pos, inv, e * inv)
                one_m_beta = jnp.where(pos, e * inv, inv)
                dz[b, h] = (da[b, h] * one_m_beta - dlk * beta).astype(BF16)
            for b, h in chains:
                dq = dq + _dot(dz[b, h], jnp.where(heads[h][0], ks[b], jnp.zeros_like(ks[b])))
            for b in range(len(js)):
                dk_acc[pl.ds(starts[b], SB_T), :] += _dot_tn(dz[b, 0], heads[0][1]) + _dot_tn(dz[b, 1], heads[1][1])
                dv_acc[pl.ds(starts[b], SB_T), :] += _dot_tn(wb[b, 0], heads[0][2]) + _dot_tn(wb[b, 1], heads[1][2])
            return tuple(c_das), dq

        zero = jnp.zeros((SB_T, 1), F32)
        init = ((zero, zero), jnp.zeros((SB_T, LANES), F32))
        first = _sb_first_blocks(i)
        state = lax.cond(
            first == 1, lambda: blocks([i], (True,), *init),
            lambda: lax.cond(first == 2, lambda: blocks([i, i - 1], (True, False), *init),
                             lambda: blocks([i, i - 1, i - 2], (True, False, False), *init)))
        rest = i - first
        state = lax.fori_loop(
            0, (rest + 1) // 3,
            lambda jj, c: blocks([rest - 3 * jj, rest - 1 - 3 * jj, rest - 2 - 3 * jj], (False, False, False), c[0], c[1]), state)
        dq_ref[...] = (state[1] * SB_SCALE).astype(BF16)

        @pl.when(i == nq - 1)
        def _():
            dk_ref[...] = dk_acc[...].astype(BF16)
            dv_ref[...] = dv_acc[...].astype(BF16)

        if n_parts:
            pl.when(step == n_steps - 1)(finish)

    q_spec, k_spec, v_spec = _sb_specs(n_pairs, 0)
    blk = pl.BlockSpec((None, SB_T, LANES), lambda b, p, i: (b, i, p))
    seq = pl.BlockSpec((None, SEQ, LANES), lambda b, p, i: (b, 0, p))
    shape = jax.ShapeDtypeStruct((bl, SEQ, SB_W), BF16)
    out = pl.pallas_call(
        body,
        grid=(bl, n_pairs, nq),
        in_specs=[q_spec, k_spec, v_spec, blk, blk, _sb_weight_spec(nq)] + [ANY] * n_parts,
        out_specs=[blk, seq, seq] + [ANY] * n_parts,
        out_shape=[shape, shape, shape] + [jax.ShapeDtypeStruct(p.shape, p.dtype) for p in parts],
        scratch_shapes=[pltpu.VMEM((SEQ, LANES), F32), pltpu.VMEM((SEQ, LANES), F32)]
        + (_chip_exchange_sems(n_parts) if n_parts else []),
        compiler_params=pltpu.CompilerParams(dimension_semantics=("arbitrary", "arbitrary", "arbitrary")),
        name="sb_bwd",
    )(proj3, proj3, proj3, o_a, do_a, w_all, *parts)
    return out[0], out[1], out[2], out[3:]


BAND = 128


BAND_CH = 4
BAND_HEADS = DIL_W // HEAD_DIM


def _swap_half(x):
    n = x.shape[-1]
    lane = lax.broadcasted_iota(jnp.int32, (1, n), 1)
    return jnp.where((lane % HEAD_DIM) < HEAD_DIM // 2, pltpu.roll(x, n - HEAD_DIM // 2, 1), pltpu.roll(x, HEAD_DIM // 2, 1))


def _rope(x, cos, sin_signed):
    x = x.astype(F32)
    return x * cos + _swap_half(x) * sin_signed


def _band_valid(g, blk):
    nb = jnp.where(g == 0, 16, jnp.where(g == 1, 4, 1))
    first_key = jnp.where(lax.rem(blk, nb) != 0, 0, BAND)
    qi = lax.broadcasted_iota(jnp.int32, (BAND, 2 * BAND), 0) + BAND
    kj = lax.broadcasted_iota(jnp.int32, (BAND, 2 * BAND), 1)
    dist = qi - kj
    return (dist >= 0) & (dist <= BAND) & (kj >= first_key)


def _band_specs():
    last_before = lambda i: jnp.maximum(i * BAND_CH - 1, 0)
    cur = lambda col: pl.BlockSpec((None, BAND_CH, BAND, DIL_W), lambda g, i: (g, i, 0, col))
    prev = lambda col: pl.BlockSpec((None, None, BAND, DIL_W), lambda g, i: (g, last_before(i), 0, col))
    tab = pl.BlockSpec((None, BAND_CH, BAND, DIL_W), lambda g, i: (g, lax.rem(i, 16 // BAND_CH), 0, 0))
    tab_prev = pl.BlockSpec((None, None, BAND, DIL_W), lambda g, i: (g, lax.rem(last_before(i), 16), 0, 0))
    return cur, prev, tab, tab_prev


def _band_load(q_ref, k_ref, kp_ref, v_ref, vp_ref, c_ref, s_ref, cp_ref, sp_ref):
    qs = [(_rope(q_ref[b], c_ref[b], s_ref[b]) * SB_SCALE).astype(BF16) for b in range(BAND_CH)]
    ks = [_rope(kp_ref[...], cp_ref[...], sp_ref[...]).astype(BF16)]
    ks += [_rope(k_ref[b], c_ref[b], s_ref[b]).astype(BF16) for b in range(BAND_CH)]
    vs = [vp_ref[...]] + [v_ref[b] for b in range(BAND_CH)]
    k2 = [jnp.concatenate([ks[b], ks[b + 1]], axis=0) for b in range(BAND_CH)]
    v2 = [jnp.concatenate([vs[b], vs[b + 1]], axis=0) for b in range(BAND_CH)]
    return qs, k2, v2


def _band_fwd(qkv_s, cos_t, sin_t):
    def body(q_ref, k_ref, kp_ref, v_ref, vp_ref, c_ref, s_ref, cp_ref, sp_ref, ol_ref):
        g, i = pl.program_id(0), pl.program_id(1)
        qs, k2, v2 = _band_load(q_ref, k_ref, kp_ref, v_ref, vp_ref, c_ref, s_ref, cp_ref, sp_ref)
        lane = lax.broadcasted_iota(jnp.int32, (1, DIL_W), 1)
        for b in range(BAND_CH):
            valid = _band_valid(g, i * BAND_CH + b)
            hs = range(BAND_HEADS)
            mh = [(lane // HEAD_DIM) == h for h in hs]
            s = [jnp.where(valid, _dot_nt(jnp.where(mh[h], qs[b], jnp.zeros_like(qs[b])), k2[b]), NEG_INF) for h in hs]
            m = [jnp.max(s[h], axis=1, keepdims=True) for h in hs]
            p = [jnp.exp(s[h] - m[h]) for h in hs]
            den = [jnp.sum(p[h], axis=1, keepdims=True) for h in hs]
            pv = [_dot(p[h].astype(BF16), jnp.where(mh[h], v2[b], jnp.zeros_like(v2[b]))) for h in hs]
            o = jnp.zeros((BAND, DIL_W), F32)
            lse = jnp.zeros((BAND, DIL_W), F32)
            for h in hs:
                o = o + pv[h] * (1.0 / den[h])
                lse = jnp.where(mh[h], m[h] + jnp.log(den[h]), lse)
            ol_ref[b, :, :DIL_W] = o
            ol_ref[b, :, DIL_W:] = lse

    cur, prev, tab, tab_prev = _band_specs()
    n_blk = qkv_s.shape[1]
    return pl.pallas_call(
        body,
        grid=(3, n_blk // BAND_CH),
        in_specs=[cur(0), cur(1), prev(1), cur(2), prev(2), tab, tab, tab_prev, tab_prev],
        out_specs=pl.BlockSpec((None, BAND_CH, BAND, 2 * DIL_W), lambda g, i: (g, i, 0, 0)),
        out_shape=jax.ShapeDtypeStruct((3, n_blk, BAND, 2 * DIL_W), F32),
        compiler_params=pltpu.CompilerParams(dimension_semantics=("parallel", "parallel")),
        name="band_fwd",
    )(qkv_s, qkv_s, qkv_s, qkv_s, qkv_s, cos_t, sin_t, cos_t, sin_t)


def _band_bwd(qkv_s, cos_t, sin_t, dcat_s):
    def body(q_ref, k_ref, kp_ref, v_ref, vp_ref, c_ref, s_ref, cp_ref, sp_ref, do_ref, lse_ref, dl_ref,
             dq_ref, dk_ref, dv_ref, dkf_ref, dvf_ref):
        g, i = pl.program_id(0), pl.program_id(1)
        qs, k2, v2 = _band_load(q_ref, k_ref, kp_ref, v_ref, vp_ref, c_ref, s_ref, cp_ref, sp_ref)
        lane = lax.broadcasted_iota(jnp.int32, (1, DIL_W), 1)
        dks, dvs = [], []
        for b in range(BAND_CH):
            valid = _band_valid(g, i * BAND_CH + b)
            do, lse, dl = do_ref[b].astype(BF16), lse_ref[b], dl_ref[b]
            hs = range(BAND_HEADS)
            mh = [(lane // HEAD_DIM) == h for h in hs]
            qh = [jnp.where(mh[h], qs[b], jnp.zeros_like(qs[b])) for h in hs]
            doh = [jnp.where(mh[h], do, jnp.zeros_like(do)) for h in hs]
            s = [_dot_nt(qh[h], k2[b]) for h in hs]
            dp = [_dot_nt(doh[h], v2[b]) for h in hs]
            p = [jnp.where(valid, jnp.exp(s[h] - lse[:, h * HEAD_DIM:h * HEAD_DIM + 1]), 0.0) for h in hs]
            ds = [(p[h] * (dp[h] - dl[:, h * HEAD_DIM:h * HEAD_DIM + 1])).astype(BF16) for h in hs]
            pb = [p[h].astype(BF16) for h in hs]
            dq = sum(_dot(ds[h], jnp.where(mh[h], k2[b], jnp.zeros_like(k2[b]))) for h in hs)
            dk2 = sum(_dot_tn(ds[h], qh[h]) for h in hs)
            dv2 = sum(_dot_tn(pb[h], doh[h]) for h in hs)
            dq_ref[b] = dq * SB_SCALE
            dks.append(dk2)
            dvs.append(dv2)
        dkf_ref[...] = dks[0][:BAND]
        dvf_ref[...] = dvs[0][:BAND]
        for b in range(BAND_CH):
            last = b == BAND_CH - 1
            dk_ref[b] = dks[b][BAND:] if last else dks[b][BAND:] + dks[b + 1][:BAND]
            dv_ref[b] = dvs[b][BAND:] if last else dvs[b][BAND:] + dvs[b + 1][:BAND]

    cur, prev, tab, tab_prev = _band_specs()
    first = pl.BlockSpec((None, None, BAND, DIL_W), lambda g, i: (g, i, 0, 0))
    n_blk = qkv_s.shape[1]
    n_chunks = n_blk // BAND_CH
    shape = jax.ShapeDtypeStruct((3, n_blk, BAND, DIL_W), F32)
    shape_first = jax.ShapeDtypeStruct((3, n_chunks, BAND, DIL_W), F32)
    return pl.pallas_call(
        body,
        grid=(3, n_chunks),
        in_specs=[cur(0), cur(1), prev(1), cur(2), prev(2), tab, tab, tab_prev, tab_prev, cur(0), cur(1), cur(2)],
        out_specs=[cur(0), cur(0), cur(0), first, first],
        out_shape=[shape, shape, shape, shape_first, shape_first],
        compiler_params=pltpu.CompilerParams(dimension_semantics=("parallel", "parallel")),
        name="band_bwd",
    )(qkv_s, qkv_s, qkv_s, qkv_s, qkv_s, cos_t, sin_t, cos_t, sin_t, dcat_s, dcat_s, dcat_s)


def _band_combine(dq, dk, dv, dk_first, dv_first, cos_t, sin_t):
    n_chunks = dk_first.shape[1]

    def body(dq_ref, dk_ref, dkn_ref, dv_ref, dvn_ref, c_ref, s_ref, out_ref):
        nxt = (pl.program_id(1) < n_chunks - 1).astype(F32)
        for b in range(BAND_CH):
            cos, sin = c_ref[b], s_ref[b]
            dq_b, dk_b, dv_b = dq_ref[b], dk_ref[b], dv_ref[b]
            if b == BAND_CH - 1:
                dk_b = dk_b + nxt * dkn_ref[...]
                dv_b = dv_b + nxt * dvn_ref[...]
            out_ref[b, :, :DIL_W] = (dq_b * cos - _swap_half(dq_b) * sin).astype(BF16)
            out_ref[b, :, DIL_W:2 * DIL_W] = (dk_b * cos - _swap_half(dk_b) * sin).astype(BF16)
            out_ref[b, :, 2 * DIL_W:] = dv_b.astype(BF16)

    cur, _, tab, _ = _band_specs()
    nxt = pl.BlockSpec((None, None, BAND, DIL_W), lambda g, i: (g, jnp.minimum(i + 1, n_chunks - 1), 0, 0))
    return pl.pallas_call(
        body,
        grid=(3, n_chunks),
        in_specs=[cur(0), cur(0), nxt, cur(0), nxt, tab, tab],
        out_specs=pl.BlockSpec((None, BAND_CH, BAND, 3 * DIL_W), lambda g, i: (g, i, 0, 0)),
        out_shape=jax.ShapeDtypeStruct(dq.shape[:3] + (3 * DIL_W,), BF16),
        compiler_params=pltpu.CompilerParams(dimension_semantics=("parallel", "parallel")),
        name="band_combine",
    )(dq, dk, dk_first, dv, dv_first, cos_t, sin_t)


def _band_merge(ol):
    t, tm = ol.shape[1], 512

    def body(o_ref, l_ref, ob_ref, lse_ref):
        l0, l1, l2 = l_ref[0], l_ref[1], l_ref[2]
        m = jnp.maximum(jnp.maximum(l0, l1), l2)
        lse = m + jnp.log(jnp.exp(l0 - m) + jnp.exp(l1 - m) + jnp.exp(l2 - m))
        ob_ref[...] = (jnp.exp(l0 - lse) * o_ref[0] + jnp.exp(l1 - lse) * o_ref[1] + jnp.exp(l2 - lse) * o_ref[2]).astype(BF16)
        lse_ref[...] = lse

    spec = pl.BlockSpec((tm, DIL_W), lambda i: (i, 0))
    return pl.pallas_call(
        body,
        grid=(t // tm,),
        in_specs=[pl.BlockSpec((3, tm, DIL_W), lambda i: (0, i, 0)), pl.BlockSpec((3, tm, DIL_W), lambda i: (0, i, 1))],
        out_specs=[spec, spec],
        out_shape=[jax.ShapeDtypeStruct((t, DIL_W), BF16), jax.ShapeDtypeStruct((t, DIL_W), F32)],
        compiler_params=pltpu.CompilerParams(dimension_semantics=("parallel",)),
        name="band_merge",
    )(ol, ol)


def _band_delta(do_b, o_b, lse_b):
    def fn(do, o, lse):
        lane_in = lax.broadcasted_iota(jnp.int32, (DIL_W, LANES), 0)
        col = lax.broadcasted_iota(jnp.int32, (DIL_W, LANES), 1)
        sum_head = ((lane_in // HEAD_DIM == col - BAND_HEADS) & (col >= BAND_HEADS) & (col < 2 * BAND_HEADS)).astype(BF16)
        return (lse + _split_dot(do.astype(F32) * o.astype(F32), sum_head),)

    return _rowwise(fn, [(do_b, "row"), (o_b, "row"), (lse_b, "row")], [(LANES, F32, "row")], tm=512, name="band_delta")[0]


def _band_masks():
    qi = lax.broadcasted_iota(jnp.int32, (BAND, 2 * BAND), 0) + BAND
    kj = lax.broadcasted_iota(jnp.int32, (BAND, 2 * BAND), 1)
    dist = qi - kj
    row = lax.broadcasted_iota(jnp.int32, (BAND, BAND), 0)
    col = lax.broadcasted_iota(jnp.int32, (BAND, BAND), 1)
    return col <= row, (dist >= 0) & (dist <= BAND)


def _band_attend(q, k, v, valid):
    lane = lax.broadcasted_iota(jnp.int32, (1, DIL_W), 1)
    stat_lane = lax.broadcasted_iota(jnp.int32, (1, LANES), 1)
    hs = range(BAND_HEADS)
    mh = [(lane // HEAD_DIM) == h for h in hs]
    s = [jnp.where(valid, _dot_nt(jnp.where(mh[h], q, jnp.zeros_like(q)), k), NEG_INF) for h in hs]
    m = [jnp.max(s[h], axis=1, keepdims=True) for h in hs]
    p = [jnp.exp(s[h] - m[h]) for h in hs]
    den = [jnp.sum(p[h], axis=1, keepdims=True) for h in hs]
    pv = [_dot(p[h].astype(BF16), jnp.where(mh[h], v, jnp.zeros_like(v))) for h in hs]
    o = jnp.zeros((BAND, DIL_W), F32)
    lse = jnp.zeros((BAND, LANES), F32)
    for h in hs:
        o = o + pv[h] * (1.0 / den[h])
        lse = jnp.where(stat_lane == h, m[h] + jnp.log(den[h]), lse)
    return o, lse


def _band_attend_bwd(q, k, v, valid, do, st):
    lane = lax.broadcasted_iota(jnp.int32, (1, DIL_W), 1)
    hs = range(BAND_HEADS)
    mh = [(lane // HEAD_DIM) == h for h in hs]
    qh = [jnp.where(mh[h], q, jnp.zeros_like(q)) for h in hs]
    doh = [jnp.where(mh[h], do, jnp.zeros_like(do)) for h in hs]
    s = [_dot_nt(qh[h], k) for h in hs]
    dp = [_dot_nt(doh[h], v) for h in hs]
    p = [jnp.where(valid, jnp.exp(s[h] - st[:, h:h + 1]), 0.0) for h in hs]
    ds = [(p[h] * (dp[h] - st[:, BAND_HEADS + h:BAND_HEADS + h + 1])).astype(BF16) for h in hs]
    pb = [p[h].astype(BF16) for h in hs]
    dq = sum(_dot(ds[h], jnp.where(mh[h], k, jnp.zeros_like(k))) for h in hs)
    dk = sum(_dot_tn(ds[h], qh[h]) for h in hs)
    dv = sum(_dot_tn(pb[h], doh[h]) for h in hs)
    return dq, dk, dv


def _band_group_specs(lead, rows, cls, col0):
    def spec(width):
        if lead == "rows":
            return pl.BlockSpec((None, rows, width), lambda b, i: (b, 0, col0))
        return pl.BlockSpec((None, rows, cls * width), lambda b, i: (b, 0, i))
    return spec


def _band_group_fwd(a, cos_g, sin_g, *, rows, cls, steps, col0, name):
    bl = a.shape[0]
    nb = rows // BAND
    grp_w = 3 * DIL_W

    def body(a_ref, c_ref, s_ref, o_ref, l_ref, qr_all, kr_all):
        first_valid, later_valid = _band_masks()
        for j in range(cls):
            qr, kr = qr_all.at[j], kr_all.at[j]
            a0, t0, s0 = j * grp_w, j * DIL_W, j * LANES
            cos, sin = c_ref[:, t0:t0 + DIL_W], s_ref[:, t0:t0 + DIL_W]
            qr[...] = (_rope(a_ref[:, a0:a0 + DIL_W], cos, sin) * SB_SCALE).astype(BF16)
            kr[...] = _rope(a_ref[:, a0 + DIL_W:a0 + 2 * DIL_W], cos, sin).astype(BF16)

            def block(q0, k0, keys, valid, a0=a0, t0=t0, s0=s0):
                o, lse = _band_attend(qr[pl.ds(q0, BAND), :], kr[pl.ds(k0, keys), :],
                                      a_ref[pl.ds(k0, keys), a0 + 2 * DIL_W:a0 + grp_w], valid)
                o_ref[pl.ds(q0, BAND), t0:t0 + DIL_W] = o.astype(BF16)
                l_ref[pl.ds(q0, BAND), s0:s0 + LANES] = lse

            block(0, 0, BAND, first_valid)
            if nb > 1:
                def later(b, carry, block=block):
                    block(pl.multiple_of(b * BAND, BAND), pl.multiple_of((b - 1) * BAND, BAND), 2 * BAND, later_valid)
                    return carry

                lax.fori_loop(1, nb, later, 0, unroll=3)

    lead = "rows" if col0 is not None else "cols"
    spec = _band_group_specs(lead, rows, cls, col0)
    tab = pl.BlockSpec((rows, cls * DIL_W), lambda b, i: (0, i))
    n_cls = cos_g.shape[1] // DIL_W
    return pl.pallas_call(
        body,
        grid=(bl, steps),
        in_specs=[spec(grp_w), tab, tab],
        out_specs=[pl.BlockSpec((None, rows, cls * DIL_W), lambda b, i: (b, 0, i)),
                   pl.BlockSpec((None, rows, cls * LANES), lambda b, i: (b, 0, i))],
        out_shape=[jax.ShapeDtypeStruct((bl, rows, n_cls * DIL_W), BF16), jax.ShapeDtypeStruct((bl, rows, n_cls * LANES), F32)],
        scratch_shapes=[pltpu.VMEM((cls, rows, DIL_W), BF16), pltpu.VMEM((cls, rows, DIL_W), BF16)],
        compiler_params=pltpu.CompilerParams(dimension_semantics=("parallel", "parallel")),
        name=name,
    )(a, cos_g, sin_g)


def _band_group_bwd(a, do, st, cos_g, sin_g, *, rows, cls, steps, col0, name, side=None):
    bl = a.shape[0]
    nb = rows // BAND
    grp_w = 3 * DIL_W
    side_arrays = side[1] if side else []
    n_side = len(side_arrays)

    def body(a_ref, do_ref, st_ref, c_ref, s_ref, *rest):
        out_ref = rest[n_side]
        scratch = rest[2 * n_side + 1:2 * n_side + 5]
        if n_side:
            step = pl.program_id(0) * steps + pl.program_id(1)
            finish = _run_side(side, rest[:n_side], rest[n_side + 1:2 * n_side + 1], rest[2 * n_side + 5:], step, bl * steps)
        first_valid, later_valid = _band_masks()
        for j in range(cls):
            qr, kr, dk_acc, dv_acc = (s.at[j] for s in scratch)
            a0, t0 = j * grp_w, j * DIL_W
            cos, sin = c_ref[:, t0:t0 + DIL_W], s_ref[:, t0:t0 + DIL_W]
            qr[...] = (_rope(a_ref[:, a0:a0 + DIL_W], cos, sin) * SB_SCALE).astype(BF16)
            kr[...] = _rope(a_ref[:, a0 + DIL_W:a0 + 2 * DIL_W], cos, sin).astype(BF16)
            dk_acc[...] = jnp.zeros_like(dk_acc)
            dv_acc[...] = jnp.zeros_like(dv_acc)

            def block(q0, k0, keys, valid, a0=a0, t0=t0, s0=j * LANES):
                qrows, krows = pl.ds(q0, BAND), pl.ds(k0, keys)
                dq, dk, dv = _band_attend_bwd(
                    qr[qrows, :], kr[krows, :], a_ref[krows, a0 + 2 * DIL_W:a0 + grp_w], valid,
                    do_ref[qrows, t0:t0 + DIL_W], st_ref[qrows, s0:s0 + LANES])
                dq = dq * SB_SCALE
                out_ref[qrows, a0:a0 + DIL_W] = (dq * c_ref[qrows, t0:t0 + DIL_W]
                                                 - _swap_half(dq) * s_ref[qrows, t0:t0 + DIL_W]).astype(BF16)
                dk_acc[krows, :] += dk
                dv_acc[krows, :] += dv

            block(0, 0, BAND, first_valid)
            if nb > 1:
                def later(b, carry, block=block):
                    block(pl.multiple_of(b * BAND, BAND), pl.multiple_of((b - 1) * BAND, BAND), 2 * BAND, later_valid)
                    return carry

                lax.fori_loop(1, nb, later, 0, unroll=3)
            dk = dk_acc[...]
            out_ref[:, a0 + DIL_W:a0 + 2 * DIL_W] = (dk * cos - _swap_half(dk) * sin).astype(BF16)
            out_ref[:, a0 + 2 * DIL_W:a0 + grp_w] = dv_acc[...].astype(BF16)
        if n_side:
            finish()

    lead = "rows" if col0 is not None else "cols"
    spec = _band_group_specs(lead, rows, cls, col0)
    dspec = _band_group_specs(lead, rows, cls, 0 if col0 is not None else None)
    tab = pl.BlockSpec((rows, cls * DIL_W), lambda b, i: (0, i))
    n_cls = cos_g.shape[1] // DIL_W
    out = pl.pallas_call(
        body,
        grid=(bl, steps),
        in_specs=[spec(grp_w), dspec(DIL_W), dspec(LANES), tab, tab] + [ANY] * n_side,
        out_specs=[pl.BlockSpec((None, rows, cls * grp_w), lambda b, i: (b, 0, i))] + [ANY] * n_side,
        out_shape=[jax.ShapeDtypeStruct((bl, rows, n_cls * grp_w), BF16)] + (_side_out_shapes(side) if n_side else []),
        scratch_shapes=[pltpu.VMEM((cls, rows, DIL_W), BF16), pltpu.VMEM((cls, rows, DIL_W), BF16),
                        pltpu.VMEM((cls, rows, DIL_W), F32), pltpu.VMEM((cls, rows, DIL_W), F32)] + (_side_sems(side) if n_side else []),
        compiler_params=pltpu.CompilerParams(dimension_semantics=("arbitrary", "arbitrary") if n_side else ("parallel", "parallel")),
        name=name,
    )(a, do, st, cos_g, sin_g, *side_arrays)
    return (out[0], out[1:]) if n_side else out[0]


def _band_merge3(groups):
    t, tm = groups[0][0].shape[0], 512

    def body(o0, l0, o1, l1, o2, l2, ob_ref, lse_ref):
        a, b, c = l0[...], l1[...], l2[...]
        m = jnp.maximum(jnp.maximum(a, b), c)
        lse = m + jnp.log(jnp.exp(a - m) + jnp.exp(b - m) + jnp.exp(c - m))
        lane = lax.broadcasted_iota(jnp.int32, (1, DIL_W), 1)
        acc = jnp.zeros((tm, DIL_W), F32)
        for o_ref, l in ((o0, a), (o1, b), (o2, c)):
            share = jnp.exp(l - lse)
            spread = jnp.zeros((tm, DIL_W), F32)
            for h in range(BAND_HEADS):
                spread = jnp.where(lane // HEAD_DIM == h, share[:, h:h + 1], spread)
            acc = acc + spread * o_ref[...].astype(F32)
        ob_ref[...] = acc.astype(BF16)
        stat_lane = lax.broadcasted_iota(jnp.int32, (1, LANES), 1)
        lse_ref[...] = jnp.where(stat_lane < BAND_HEADS, lse, 0.0)

    spec = pl.BlockSpec((tm, DIL_W), lambda i: (i, 0))
    spec_l = pl.BlockSpec((tm, LANES), lambda i: (i, 0))
    return pl.pallas_call(
        body,
        grid=(t // tm,),
        in_specs=[spec, spec_l] * 3,
        out_specs=[spec, spec_l],
        out_shape=[jax.ShapeDtypeStruct((t, DIL_W), BF16), jax.ShapeDtypeStruct((t, LANES), F32)],
        compiler_params=pltpu.CompilerParams(dimension_semantics=("parallel",)),
        name="band_merge",
    )(*[a for g in groups for a in g])


MEM_T = 512
MEM_SCALE = 128 ** -0.5
MEM_Q_COL = (D_IN - MEM_W) // LANES


MEM_HEADS = MEM_W // LANES


def _mem_specs():
    qs = [pl.BlockSpec((None, MEM_T, LANES), lambda b, i, h=h: (b, i, MEM_Q_COL + h)) for h in range(MEM_HEADS)]
    kv = pl.BlockSpec((None, MEM_LEN, 2 * MEM_W), lambda b, i: (b, 0, 0))
    blk = pl.BlockSpec((None, MEM_T, MEM_W), lambda b, i: (b, i, 0))
    return qs, kv, blk


def _mem_probs(q, k):
    s = _dot_nt(q, k) * MEM_SCALE
    p = jnp.exp(s - jnp.max(s, axis=1, keepdims=True))
    return p * (1.0 / jnp.sum(p, axis=1, keepdims=True))


def _head_cols(h, base=0):
    return slice(base + h * LANES, base + (h + 1) * LANES)


def _mem_fwd(proj3, kv3):
    bl = proj3.shape[0]
    hs = range(MEM_HEADS)

    def body(*refs):
        q_refs, kv_ref, o_ref = refs[:MEM_HEADS], refs[MEM_HEADS], refs[MEM_HEADS + 1]
        p = [_mem_probs(q_refs[h][...], kv_ref[:, _head_cols(h)]) for h in hs]
        for h in hs:
            o_ref[:, _head_cols(h)] = _dot(p[h].astype(BF16), kv_ref[:, _head_cols(h, MEM_W)]).astype(BF16)

    qs, kv, blk = _mem_specs()
    return pl.pallas_call(
        body,
        grid=(bl, SEQ // MEM_T),
        in_specs=qs + [kv],
        out_specs=blk,
        out_shape=jax.ShapeDtypeStruct((bl, SEQ, MEM_W), BF16),
        compiler_params=pltpu.CompilerParams(dimension_semantics=("parallel", "parallel")),
        name="mem_fwd",
    )(*([proj3] * MEM_HEADS), kv3)


def _mem_bwd(proj3, kv3, do_c):
    bl = proj3.shape[0]
    hs = range(MEM_HEADS)

    def body(*refs):
        q_refs, kv_ref, do_ref, dq_ref, dkv_ref = refs[:MEM_HEADS], *refs[MEM_HEADS:MEM_HEADS + 4]

        @pl.when(pl.program_id(1) == 0)
        def _():
            dkv_ref[...] = jnp.zeros_like(dkv_ref)

        q = [q_refs[h][...] for h in hs]
        do = [do_ref[:, _head_cols(h)] for h in hs]
        p = [_mem_probs(q[h], kv_ref[:, _head_cols(h)]) for h in hs]
        dp = [_dot_nt(do[h], kv_ref[:, _head_cols(h, MEM_W)]) for h in hs]
        ds = [(p[h] * (dp[h] - jnp.sum(p[h] * dp[h], axis=1, keepdims=True)) * MEM_SCALE).astype(BF16) for h in hs]
        for h in hs:
            dq_ref[:, _head_cols(h)] = _dot(ds[h], kv_ref[:, _head_cols(h)]).astype(BF16)
            dkv_ref[:, _head_cols(h)] += _dot_tn(ds[h], q[h])
            dkv_ref[:, _head_cols(h, MEM_W)] += _dot_tn(p[h].astype(BF16), do[h])

    qs, kv, blk = _mem_specs()
    return pl.pallas_call(
        body,
        grid=(bl, SEQ // MEM_T),
        in_specs=qs + [kv, blk],
        out_specs=[blk, kv],
        out_shape=[jax.ShapeDtypeStruct((bl, SEQ, MEM_W), BF16), jax.ShapeDtypeStruct((bl, MEM_LEN, 2 * MEM_W), F32)],
        compiler_params=pltpu.CompilerParams(dimension_semantics=("parallel", "arbitrary")),
        name="mem_bwd",
    )(*([proj3] * MEM_HEADS), kv3, do_c)


def _place():
    x, y, c = lax.axis_index("x"), lax.axis_index("y"), lax.axis_index("c")
    return x, y, c


def _other_chips(x, y):
    return [(1 - x, y), (x, 1 - y), (1 - x, 1 - y)]


def _remote(src, dst, send_sem, recv_sem, to):
    return pltpu.make_async_remote_copy(src_ref=src, dst_ref=dst, send_sem=send_sem, recv_sem=recv_sem,
                                        device_id=to, device_id_type=MESH)


ANY = pl.BlockSpec(memory_space=pl.ANY)


def _gather_weights(shards):
    n = len(shards)

    def body(*refs):
        send, forward, finish = _gather_phases(refs[:n], refs[n:2 * n], *refs[2 * n:])
        send()
        forward()
        finish()

    return pl.pallas_call(
        body,
        in_specs=[ANY] * n,
        out_specs=[ANY] * n,
        out_shape=_gather_out_shapes(shards),
        scratch_shapes=_gather_sems(n),
        name="gather_weights",
    )(*shards)


def _gather_out_shapes(shards):
    return [jax.ShapeDtypeStruct((N_CHIPS,) + s.shape, s.dtype) for s in shards]


def _gather_sems(n):
    return [pltpu.SemaphoreType.DMA((6 * n,)), pltpu.SemaphoreType.DMA((6 * n,))]


def _gather_phases(in_refs, out_refs, send_sems, recv_sems):
    x, y, c = _place()
    sibling = (x, y, 1 - c)
    chips = _other_chips(x, y)
    first, passed = [], []
    for k in range(len(in_refs)):
        hf = in_refs[k].shape[0] // 2

        def half(px, py, pc, k=k, hf=hf):
            return out_refs[k].at[2 * px + py, pl.ds(pc * hf, hf), :]

        src = in_refs[k].at[pl.ds(c * hf, hf), :]
        for j, chip in enumerate(chips):
            s = 6 * k + j
            first.append(_remote(src, half(x, y, c), send_sems.at[s], recv_sems.at[s], (*chip, c)))
            passed.append((_remote(src, half(*chip, c), send_sems.at[s], recv_sems.at[s], (*chip, c)),
                           _remote(half(*chip, c), half(*chip, c), send_sems.at[s + 3], recv_sems.at[s + 3], sibling),
                           _remote(src, half(*chip, 1 - c), send_sems.at[s + 3], recv_sems.at[s + 3], sibling)))

    def send():
        for cp in first:
            cp.start()

    def forward():
        for landed, fwd, _ in passed:
            landed.wait_recv()
            fwd.start()

    def finish():
        for _, _, from_sibling in passed:
            from_sibling.wait_recv()
        for cp in first:
            cp.wait_send()
        for _, fwd, _ in passed:
            fwd.wait_send()

    return send, forward, finish


def _pair_exchange(grads, *, name):
    n = len(grads)
    side = ("pair", grads)

    def body(*refs):
        send, _, finish = _side_phases(side, refs[:n], refs[n:2 * n], refs[2 * n:])
        send()
        finish()

    return pl.pallas_call(
        body,
        in_specs=[ANY] * n,
        out_specs=[ANY] * n,
        out_shape=_side_out_shapes(side),
        scratch_shapes=_side_sems(side),
        name=name,
    )(*grads)


def _pair_exchange_phases(g_refs, land_refs, send_sems, recv_sems):
    x, y, c = _place()
    cps = []
    for k in range(len(g_refs)):
        hf = g_refs[k].shape[1] // 2
        src = g_refs[k].at[:, pl.ds((1 - c) * hf, hf), :]
        cps.append(_remote(src, land_refs[k], send_sems.at[k], recv_sems.at[k], (x, y, 1 - c)))

    def send():
        for cp in cps:
            cp.start()

    def finish():
        for cp in cps:
            cp.wait()

    return send, finish


def _side_out_shapes(side):
    kind, arrays = side
    if kind == "gather":
        return _gather_out_shapes(arrays)
    if kind == "pair":
        return [jax.ShapeDtypeStruct((N_CHIPS, g.shape[1] // 2, g.shape[2]), g.dtype) for g in arrays]
    return [jax.ShapeDtypeStruct(p.shape, p.dtype) for p in arrays]


def _side_sems(side):
    kind, arrays = side
    n = len(arrays)
    if kind == "gather":
        return _gather_sems(n)
    if kind == "pair":
        return [pltpu.SemaphoreType.DMA((n,)), pltpu.SemaphoreType.DMA((n,))]
    return _chip_exchange_sems(n)


def _side_phases(side, in_refs, out_refs, sems):
    kind = side[0]
    if kind == "gather":
        return _gather_phases(in_refs, out_refs, *sems)
    send, finish = (_pair_exchange_phases if kind == "pair" else _chip_exchange_phases)(in_refs, out_refs, *sems)
    return send, None, finish


def _run_side(side, in_refs, out_refs, sems, step, n_steps):
    first, mid, last = _side_phases(side, in_refs, out_refs, sems)
    pl.when(step == 0)(first)
    if mid is not None:
        pl.when(step == n_steps // 2)(mid)
    return lambda: pl.when(step == n_steps - 1)(last)


def _pair_add(g, land, c_arr, *, name):
    _, a, b = g.shape
    hf = a // 2

    def body(c_ref, g_ref, l_ref, o_ref):
        o_ref[...] = (g_ref[...] + l_ref[...]).astype(BF16)

    return pl.pallas_call(
        body,
        grid_spec=pltpu.PrefetchScalarGridSpec(
            num_scalar_prefetch=1,
            grid=(N_CHIPS,),
            in_specs=[pl.BlockSpec((None, None, hf, b), lambda s, c_ref: (s, c_ref[0], 0, 0)),
                      pl.BlockSpec((None, hf, b), lambda s, c_ref: (s, 0, 0))],
            out_specs=pl.BlockSpec((None, hf, b), lambda s, c_ref: (s, 0, 0)),
        ),
        out_shape=jax.ShapeDtypeStruct((N_CHIPS, hf, b), BF16),
        compiler_params=pltpu.CompilerParams(dimension_semantics=("parallel",)),
        name=name,
    )(c_arr, g.reshape(N_CHIPS, 2, hf, b), land)


def _chip_exchange(parts):
    n = len(parts)

    def body(*refs):
        send, finish = _chip_exchange_phases(refs[:n], refs[n:2 * n], *refs[2 * n:])
        send()
        finish()

    return pl.pallas_call(
        body,
        in_specs=[ANY] * n,
        out_specs=[ANY] * n,
        out_shape=[jax.ShapeDtypeStruct(p.shape, p.dtype) for p in parts],
        scratch_shapes=_chip_exchange_sems(n),
        name="chip_exchange",
    )(*parts)


def _chip_exchange_sems(n):
    return [pltpu.SemaphoreType.DMA((3 * n,)), pltpu.SemaphoreType.DMA((3 * n,))]


def _chip_exchange_phases(p_refs, land_refs, send_sems, recv_sems):
    x, y, c = _place()
    me = 2 * x + y
    sends, recvs = [], []
    for k in range(len(p_refs)):
        for j, (cx, cy) in enumerate(_other_chips(x, y)):
            s = 3 * k + j
            sends.append(_remote(p_refs[k].at[2 * cx + cy], land_refs[k].at[me], send_sems.at[s], recv_sems.at[s], (cx, cy, c)))
            recvs.append(_remote(p_refs[k].at[me], land_refs[k].at[2 * cx + cy], send_sems.at[s], recv_sems.at[s], (cx, cy, c)))

    def send():
        for cp in sends:
            cp.start()

    def finish():
        for cp in recvs:
            cp.wait_recv()
        for cp in sends:
            cp.wait_send()

    return send, finish


def _chip_add(land, part, me_arr, *, name):
    _, r, b = land.shape

    def body(me_ref, p_ref, l1_ref, l2_ref, l3_ref, o_ref):
        o_ref[...] = ((p_ref[...].astype(F32) + l1_ref[...].astype(F32)) + l2_ref[...].astype(F32)) + l3_ref[...].astype(F32)

    tr = r // 2
    other = lambda j: pl.BlockSpec((None, tr, b), lambda i, me_ref: (jnp.bitwise_xor(me_ref[0], j), i, 0))
    return pl.pallas_call(
        body,
        grid_spec=pltpu.PrefetchScalarGridSpec(
            num_scalar_prefetch=1,
            grid=(r // tr,),
            in_specs=[pl.BlockSpec((None, tr, b), lambda i, me_ref: (me_ref[0], i, 0)), other(2), other(1), other(3)],
            out_specs=pl.BlockSpec((tr, b), lambda i, me_ref: (i, 0)),
        ),
        out_shape=jax.ShapeDtypeStruct((r, b), F32),
        compiler_params=pltpu.CompilerParams(dimension_semantics=("parallel",)),
        name=name,
    )(me_arr, part, land, land, land)


def _pair_share(halves):
    n = len(halves)

    def body(*refs):
        h_refs, out_refs = refs[:n], refs[n:2 * n]
        send_sems, recv_sems = refs[2 * n:]
        x, y, c = _place()
        cps = [_remote(h_refs[k], out_refs[k], send_sems.at[k], recv_sems.at[k], (x, y, 1 - c)) for k in range(n)]
        for cp in cps:
            cp.start()
        for cp in cps:
            cp.wait()

    return pl.pallas_call(
        body,
        in_specs=[ANY] * n,
        out_specs=[ANY] * n,
        out_shape=[jax.ShapeDtypeStruct(h.shape, F32) for h in halves],
        scratch_shapes=[pltpu.SemaphoreType.DMA((n,)), pltpu.SemaphoreType.DMA((n,))],
        name="pair_share",
    )(*halves)


def _all_sum_small(part):
    def body(p_ref, o_ref, slots, send_sems, recv_sems):
        x, y, c = _place()
        me = 4 * x + 2 * y + c
        slots[me] = p_ref[...]
        peers = [(x ^ dx, y ^ dy, c ^ dc) for dx in (0, 1) for dy in (0, 1) for dc in (0, 1)][1:]
        sends = [_remote(p_ref, slots.at[me], send_sems.at[k], recv_sems.at[k], peer) for k, peer in enumerate(peers)]
        for cp in sends:
            cp.start()
        for k, (px, py, pc) in enumerate(peers):
            _remote(p_ref, slots.at[4 * px + 2 * py + pc], send_sems.at[k], recv_sems.at[k], (px, py, pc)).wait_recv()
        for cp in sends:
            cp.wait_send()
        acc = slots[0]
        for d in range(1, 8):
            acc = acc + slots[d]
        o_ref[...] = acc

    vmem = pl.BlockSpec(memory_space=pltpu.VMEM)
    return pl.pallas_call(
        body,
        in_specs=[vmem],
        out_specs=vmem,
        out_shape=jax.ShapeDtypeStruct(part.shape, F32),
        scratch_shapes=[pltpu.VMEM((8,) + part.shape, F32), pltpu.SemaphoreType.DMA((7,)), pltpu.SemaphoreType.DMA((7,))],
        name="all_sum_small",
    )(part)


def _deinterleave(a, d):
    b, s, c = a.shape
    return a.reshape(b, s // d, d, c).transpose(0, 2, 1, 3).reshape(b * s // BAND, BAND, c)


def _reinterleave(a, d, b):
    c = a.shape[-1]
    return a.reshape(b, d, SEQ // d, c).transpose(0, 2, 1, 3).reshape(b, SEQ, c)


def _rope_tables():
    half = HEAD_DIM // 2
    inv_freq = np.float32(ROPE_THETA) ** (-np.arange(half, dtype=np.float32) * np.float32(2.0) / np.float32(HEAD_DIM))
    ang = np.arange(SEQ, dtype=np.float32)[:, None] * inv_freq[None, :].astype(np.float32)
    cos = np.tile(np.cos(ang).astype(np.float32), (1, 2 * BAND_HEADS))
    sin = np.tile(np.concatenate([-np.sin(ang), np.sin(ang)], axis=1).astype(np.float32), (1, BAND_HEADS))
    return jnp.asarray(cos), jnp.asarray(sin)


def _band_groups():
    out = []
    for d in DIL_D:
        rows = SEQ // d
        cls = max(1, 512 // rows) if d > 1 else 1
        out.append(dict(rows=rows, cls=cls, steps=d // cls))
    return out


def _local_step(x, mem, loss_target, g_pre_mix, g_post_mix, g_pre_ffn, g_post_ffn, g_mem, b_gate, w, comm=None):
    bl = x.shape[0]
    t = bl * SEQ
    chips = range(N_CHIPS)
    half_ff = D_FF // 2

    def with_gathered(w, names, gathered, shards):
        return {**w, **{name: lax.dynamic_update_slice(g, s[None], (comm["me"][0], 0, 0))
                        for name, g, s in zip(names, gathered, shards)}}

    x2 = x.reshape(t, D_MODEL)
    tgt2 = loss_target.reshape(t, D_MODEL)
    mem2 = mem.reshape(bl * MEM_LEN, D_MODEL)

    h = _norm_fwd(x2, g_pre_mix, name="norm_x", side=("gather", comm["first_shards"]) if comm else None)
    if comm:
        w = with_gathered(w, comm["first_names"], h[1], comm["first_shards"])
        h = h[0]
    w_in_full = _join_shards(w["w_in"])
    proj = _mm([(h, w_in_full)], nt=False, tn=2176, out_dtypes=[BF16], name="proj",
               side=("gather", comm["mid_shards"]) if comm else None)
    if comm:
        w = with_gathered(w, comm["mid_names"], proj[1], comm["mid_shards"])
        proj = proj[0]
    w_mem_kv_full = w["w_mem_kv"].reshape(D_MODEL, 2 * MEM_W)
    gates = _mm([(h, w["w_gate"], None, "j")], nt=False,tn=w["w_gate"].shape[2], out_dtypes=[BF16], name="gates",
                bias=b_gate, epilogue=lambda acc: (_sigmoid(acc),))
    hm = _norm_fwd(mem2, g_mem, name="norm_mem")
    kv_m = _mm([(hm, w_mem_kv_full)], nt=False,tn=1024, out_dtypes=[BF16], name="mem_kv")
    proj3 = proj.reshape(bl, SEQ, D_IN)
    kv3 = kv_m.reshape(bl, MEM_LEN, 2 * MEM_W)

    o_a, o_a32, sb_weights, late_gathered = _sb_fwd(proj3, comm["late_shards"] if comm else [])
    if comm:
        w = with_gathered(w, comm["late_names"], late_gathered, comm["late_shards"])
    w_o_full = w["w_o"].reshape(D_MODEL, D_MODEL)
    w_ffn_out_full = w["w_ffn_out"].reshape(D_FF, D_MODEL)

    cos_t, sin_t = _rope_tables()
    dil0 = 3 * SB_W

    grp_w = 3 * DIL_W
    band = []
    for g, (d, cfg) in enumerate(zip(DIL_D, _band_groups())):
        a_g = proj3 if d == 1 else proj3[:, :, dil0 + g * grp_w:dil0 + (g + 1) * grp_w].reshape(bl, SEQ // d, d * grp_w)
        band.append(dict(cfg, a=a_g, col0=dil0 // grp_w if d == 1 else None, cos=cos_t.reshape(SEQ // d, d * DIL_W),
                         sin=sin_t.reshape(SEQ // d, d * DIL_W)))
    outs = [_band_group_fwd(b["a"], b["cos"], b["sin"], rows=b["rows"], cls=b["cls"], steps=b["steps"], col0=b["col0"],
                            name=f"band_fwd_{g}") for g, b in enumerate(band)]
    o_b, lse_b = _band_merge3([(o.reshape(t, DIL_W), l.reshape(t, LANES)) for o, l in outs])

    o_c = _mem_fwd(proj3, kv3)

    o_a2, o_c2 = o_a.reshape(t, SB_W), o_c.reshape(t, MEM_W)
    y_a, y_b, y_c, merged = _branch_merge_fwd(o_a2, o_b, o_c2, w["w_br_sb"], w["w_br_dil"], w["w_br_mem"], gates)
    mix = _mm([(merged, w_o_full)], nt=False,tn=1024, out_dtypes=[F32], name="mix")
    x1, h2 = _mid_fwd(mix, x2, g_post_mix, g_pre_ffn)
    gg, uu, f = _ffn_in_fwd(h2, w["w_ffn_in"])
    f2 = _mm([(f, w_ffn_out_full)], nt=False,tn=1024, out_dtypes=[F32], name="ffn_out")

    dy, df2, dg_post_ffn, loss_row = _loss_bwd(f2, x1, g_post_ffn, tgt2)

    dg_ffn, du_ffn = _mm([(df2, w_ffn_out_full)], nt=True,tn=half_ff, out_dtypes=[BF16, BF16], name="d_ffn_act",
                         extras=(gg, uu), epilogue=_swiglu_bwd_epilogue)
    gw = {}
    gw["w_ffn_out"] = _mm_tn(f, df2, tm=half_ff, tn=1024, name="gw_ffn_out").reshape(N_CHIPS, D_FF // N_CHIPS, D_MODEL)
    gw_ffn_g = _mm_tn(h2, dg_ffn, tm=1024, tn=half_ff, name="gw_ffn_gate", out_shards=True, slots=(N_CHIPS, 0), group=2)
    gw["w_ffn_in"] = _mm_tn(h2, du_ffn, tm=1024, tn=half_ff, name="gw_ffn_up", out_shards=True, slots=(N_CHIPS, 2), into=gw_ffn_g,
                            group=2)
    dh2 = _mm([(dg_ffn, w["w_ffn_in"], 0, 0), (dg_ffn, w["w_ffn_in"], 1, 1), (du_ffn, w["w_ffn_in"], 0, 2),
               (du_ffn, w["w_ffn_in"], 1, 3)], nt=True, tn=1024, out_dtypes=[BF16], name="d_h2")
    dx1, dmix, dg_pre_ffn, dg_post_mix = _mid_bwd(dh2, x1, mix, g_pre_ffn, g_post_mix, dy)

    gw["w_o"] = _mm_tn(merged, dmix, tm=1024, tn=1024, name="gw_o").reshape(N_CHIPS, D_MODEL // N_CHIPS, D_MODEL)
    dmerged = _mm([(dmix, w_o_full)], nt=True, tn=1024, out_dtypes=[BF16], name="d_merged")
    dy_a, dy_b, dy_c, dgpre, db_gate = _gate_bwd(dmerged, gates, y_a, y_b, y_c)
    br_cols = D_MODEL // N_CHIPS
    gw["w_br_sb"] = _mm_tn(o_a2, dy_a, tm=512, tn=br_cols, name="gw_br_sb", out_shards=True, group=N_CHIPS)
    gw["w_br_dil"] = _mm_tn(o_b, dy_b, tm=256, tn=br_cols, name="gw_br_dil", out_shards=True, group=N_CHIPS)
    gw["w_br_mem"] = _mm_tn(o_c2, dy_c, tm=512, tn=br_cols, name="gw_br_mem", out_shards=True, group=N_CHIPS)
    gw["w_gate"] = _mm_tn(h, dgpre, tm=1024, tn=w["w_gate"].shape[2], name="gw_gate", out_shards=True, group=2)
    do_a = _mm([(dy_a, w["w_br_sb"], s, s) for s in chips], nt=True,tn=SB_W, out_dtypes=[BF16], name="d_o_a")
    do_b = _mm([(dy_b, w["w_br_dil"], s, s) for s in chips], nt=True,tn=DIL_W, out_dtypes=[BF16], name="d_o_b")
    do_c = _mm([(dy_c, w["w_br_mem"], s, s) for s in chips], nt=True,tn=MEM_W, out_dtypes=[BF16], name="d_o_c")

    dq_c, dkv_m = _mem_bwd(proj3, kv3, do_c.reshape(bl, SEQ, MEM_W))
    dkv_m = dkv_m.reshape(bl * MEM_LEN, 2 * MEM_W).astype(BF16)
    gw["w_mem_kv"] = _mm_tn(hm, dkv_m, tm=1024, tn=1024, name="gw_mem_kv").reshape(N_CHIPS, D_MODEL // N_CHIPS, 2 * MEM_W)
    dhm = _mm([(dkv_m, w_mem_kv_full)], nt=True,tn=1024, out_dtypes=[F32], name="d_hm")
    dg_mem = _mem_norm_bwd(dhm, mem2, g_mem)

    stats = _band_delta(do_b, o_b, lse_b)
    early = [name for name, _, _ in PACK if name != "w_in"] if comm else []
    grads = [gw[name] for name in early]
    d_dil = []
    for g, (d, b) in enumerate(zip(DIL_D, band)):
        out = _band_group_bwd(b["a"], do_b.reshape(bl, SEQ // d, d * DIL_W), stats.reshape(bl, SEQ // d, d * LANES),
                              b["cos"], b["sin"], rows=b["rows"], cls=b["cls"], steps=b["steps"], col0=b["col0"],
                              name=f"band_bwd_{g}", side=("pair", grads) if comm and g == 0 else None)
        if comm and g == 0:
            out, lands = out
        d_dil.append(out.reshape(bl, SEQ, grp_w))

    parts = [_pair_add(g, l, comm["c"], name="pair_add_" + name) for name, g, l in zip(early, grads, lands)] if comm else []
    dq_a, dk_a, dv_a, lands = _sb_bwd(proj3, o_a32, do_a.reshape(bl, SEQ, SB_W), sb_weights, parts)
    reduced = {name: (p, l) for name, p, l in zip(early, parts, lands)}

    in_cols = D_IN // N_CHIPS
    dproj_s = _split_to_shards([a.reshape(t, a.shape[-1]) for a in [dq_a, dk_a, dv_a] + d_dil + [dq_c]], name="dproj_shards")
    gw["w_in"] = _mm_tn(h, dproj_s, tm=1024, tn=in_cols, name="gw_in", group=2)
    if comm:
        land = _pair_exchange([gw["w_in"]], name="pair_exchange_w_in")[0]
        part_in = _pair_add(gw["w_in"], land, comm["c"], name="pair_add_w_in")
    dh = _mm([(dproj_s, w["w_in"], s, s) for s in chips] + [(dgpre, w["w_gate"], s, s) for s in chips],
             nt=True, tn=1024, out_dtypes=[BF16], name="d_h", side=("chip", [part_in]) if comm else None)
    if comm:
        dh, (land_in,) = dh
        reduced["w_in"] = (part_in, land_in)
    grad_x, dg_pre_mix = _first_bwd(dh, x2, g_pre_mix, dx1)
    small = jnp.concatenate([dg_pre_mix, dg_post_mix, dg_pre_ffn, dg_post_ffn, dg_mem, db_gate.reshape(3, D_MODEL)], axis=0)
    return loss_row[0, 0], grad_x.reshape(bl, SEQ, D_MODEL), gw, small, reduced


def kernel(x, mem, g_pre_mix, g_post_mix, g_pre_ffn, g_post_ffn, g_mem, w_in, w_mem_kv, w_br_sb, w_br_dil, w_br_mem, w_gate, b_gate, w_o, w_ffn_in, w_ffn_out, loss_target, m_g_pre_mix, m_g_post_mix, m_g_pre_ffn, m_g_post_ffn, m_g_mem, m_w_in, m_w_mem_kv, m_w_br_sb, m_w_br_dil, m_w_br_mem, m_w_gate, m_b_gate, m_w_o, m_w_ffn_in, m_w_ffn_out, v_g_pre_mix, v_g_post_mix, v_g_pre_ffn, v_g_post_ffn, v_g_mem, v_w_in, v_w_mem_kv, v_w_br_sb, v_w_br_dil, v_w_br_mem, v_w_gate, v_b_gate, v_w_o, v_w_ffn_in, v_w_ffn_out):
    w_shards = dict(w_in=w_in[0], w_mem_kv=w_mem_kv[0], w_br_sb=w_br_sb[0], w_br_dil=w_br_dil[0], w_br_mem=w_br_mem[0],
                    w_gate=w_gate[0], w_o=w_o[0], w_ffn_in=w_ffn_in[0], w_ffn_out=w_ffn_out[0])
    m_shards = dict(w_in=m_w_in[0], w_mem_kv=m_w_mem_kv[0], w_br_sb=m_w_br_sb[0], w_br_dil=m_w_br_dil[0], w_br_mem=m_w_br_mem[0],
                    w_gate=m_w_gate[0], w_o=m_w_o[0], w_ffn_in=m_w_ffn_in[0], w_ffn_out=m_w_ffn_out[0])
    v_shards = dict(w_in=v_w_in[0], w_mem_kv=v_w_mem_kv[0], w_br_sb=v_w_br_sb[0], w_br_dil=v_w_br_dil[0], w_br_mem=v_w_br_mem[0],
                    w_gate=v_w_gate[0], w_o=v_w_o[0], w_ffn_in=v_w_ffn_in[0], w_ffn_out=v_w_ffn_out[0])

    names = [name for name, _, _ in PACK]
    c_arr = lax.axis_index("c").astype(jnp.int32).reshape(1)
    me_arr = (2 * lax.axis_index("x") + lax.axis_index("y")).astype(jnp.int32).reshape(1)
    mid_names = ["w_gate", "w_mem_kv"]
    late_names = [name for name in names if name not in ["w_in"] + mid_names]
    bf = {name: w_shards[name].astype(BF16) for name in names}
    comm = dict(c=c_arr, me=me_arr, first_names=["w_in"], first_shards=[bf["w_in"]],
                mid_names=mid_names, mid_shards=[bf[name] for name in mid_names],
                late_names=late_names, late_shards=[bf[name] for name in late_names])

    loss_local, grad_x, gw, small, reduced = _local_step(x, mem, loss_target, g_pre_mix, g_post_mix, g_pre_ffn, g_post_ffn,
                                                         g_mem, b_gate, {}, comm)
    loss = lax.psum(loss_local, ("x", "y", "c"))

    halves =[_chip_add(reduced[name][1], reduced[name][0], me_arr, name="chip_add_" + name) for name in names]
    theirs = _pair_share(halves)
    small = _all_sum_small(small)

    upd = {}
    for name, mine, other in zip(names, halves, theirs):
        upd[name] = _adamw_halves(w_shards[name], mine, other, m_shards[name], v_shards[name], c_arr, name="adamw_" + name)
    g_shards = {name: u[0] for name, u in upd.items()}

    def small8(gs, b):
        return jnp.concatenate(gs + [b.reshape(3, D_MODEL)], axis=0)

    sw = small8([g_pre_mix, g_post_mix, g_pre_ffn, g_post_ffn, g_mem], b_gate)
    sm = small8([m_g_pre_mix, m_g_post_mix, m_g_pre_ffn, m_g_post_ffn, m_g_mem], m_b_gate)
    sv = small8([v_g_pre_mix, v_g_post_mix, v_g_pre_ffn, v_g_post_ffn, v_g_mem], v_b_gate)
    s_upd = _adamw(sw, small, sm, sv, tm=8, name="adamw_small")

    def small_out(a):
        return [a[0:1], a[1:2], a[2:3], a[3:4], a[4:5]]

    order = ["w_in", "w_mem_kv", "w_br_sb", "w_br_dil", "w_br_mem", "w_gate", "b_gate", "w_o", "w_ffn_in", "w_ffn_out"]

    def leaves(small_arr, big):
        out = small_out(small_arr)
        for name in order:
            out.append(small_arr[5:8].reshape(1, 3 * D_MODEL) if name == "b_gate" else big[name][None])
        return out

    grads_out = leaves(small, g_shards)
    delta_out = leaves(s_upd[0], {n: u[1] for n, u in upd.items()})
    m_out = leaves(s_upd[1], {n: u[2] for n, u in upd.items()})
    v_out = leaves(s_upd[2], {n: u[3] for n, u in upd.items()})
    return (loss, grad_x, *grads_out, *delta_out, *m_out, *v_out)
```

```python
import jax
import jax.numpy as jnp
import numpy as np
from jax import lax
from jax.experimental import pallas as pl
from jax.experimental.pallas import tpu as pltpu

F32 = jnp.float32
BF16 = jnp.bfloat16
MESH = pl.DeviceIdType.MESH

D_MODEL = 1024
SEQ = 2048
HEAD_DIM = 64
SB_W = 512
DIL_W = 256
MEM_W = 512
MEM_LEN = 256
D_IN = 3 * SB_W + 9 * DIL_W + MEM_W
D_FF = 2816
DIL_D = (1, 4, 16)
ROPE_THETA = 10000.0
NORM_EPS = 1e-6
NEG_INF = -1e30
LANES = 128

ADAM_LR = 0.001
ADAM_B1 = 0.9
ADAM_B2 = 0.999
ADAM_EPS = 1e-08
ADAM_WD = 0.01
ADAM_STEP = 10

N_CHIPS = 4
PACK = (
    ("w_in", (1024, 1088), 1),
    ("w_mem_kv", (256, 1024), 0),
    ("w_br_sb", (512, 256), 1),
    ("w_br_dil", (256, 256), 1),
    ("w_br_mem", (512, 256), 1),
    ("w_gate", (1024, 768), 1),
    ("w_o", (256, 1024), 0),
    ("w_ffn_in", (1024, 1408), 1),
    ("w_ffn_out", (704, 1024), 0),
)


def _dot(a, b):
    return lax.dot_general(a, b, (((1,), (0,)), ((), ())), preferred_element_type=F32)


def _dot_nt(a, b):
    return lax.dot_general(a, b, (((1,), (1,)), ((), ())), preferred_element_type=F32)


def _dot_tn(a, b):
    return lax.dot_general(a, b, (((0,), (0,)), ((), ())), preferred_element_type=F32)


def _split_dot(x, u):
    hi = x.astype(BF16)
    lo = (x - hi.astype(F32)).astype(BF16)
    return _dot(hi, u) + _dot(lo, u)


V7X_VMEM_BUDGET = 44 * 2 ** 20


def _rows_that_fit(m, row_bytes, fixed_bytes):
    for tm in (1024, 512, 256, 128):
        if m % tm == 0 and fixed_bytes + tm * row_bytes <= V7X_VMEM_BUDGET:
            return tm
    return min(m, 128)


def _mm(pairs, *, nt, tn, out_dtypes, name, bias=None, extras=(), epilogue=None, side=None):
    pairs = [p if len(p) == 4 else (p[0], p[1], None, None) for p in pairs]
    m = pairs[0][0].shape[-2]
    b0 = pairs[0][1]
    if nt:
        n = b0.shape[-2]
    else:
        n = b0.shape[-1] * (b0.shape[0] if b0.ndim == 3 else 1)
    n_pairs, n_extra, n_out = len(pairs), len(extras), len(out_dtypes)
    assert n % tn == 0
    one_col = n == tn
    ks = [(b.shape[-1] if nt else b.shape[-2]) for _, b, _, _ in pairs]
    fixed = sum(k * tn * 2 for k in ks) * (1 if one_col else 2)
    row_bytes = 2 * sum(k * 2 for k in ks) + 2 * tn * (sum(jnp.dtype(dt).itemsize for dt in out_dtypes) + 2 * n_extra) + 2 * tn * 4
    tm = _rows_that_fit(m, row_bytes, fixed)
    assert m % tm == 0
    b_mode = dict(pipeline_mode=pl.Buffered(1)) if one_col else {}
    has_bias = bias is not None
    side_arrays = side[1] if side else []
    n_side = len(side_arrays)
    n_main_in = 2 * n_pairs + has_bias + n_extra
    n_steps = (n // tn) * (m // tm)

    def body(*refs):
        if n_side:
            step = pl.program_id(0) * (m // tm) + pl.program_id(1)
            finish = _run_side(side, refs[n_main_in:n_main_in + n_side],
                               refs[n_main_in + n_side + n_out:n_main_in + 2 * n_side + n_out],
                               refs[n_main_in + 2 * n_side + n_out:], step, n_steps)
        outs = refs[n_main_in + n_side:n_main_in + n_side + n_out]
        acc = None
        for i in range(n_pairs):
            a, b = refs[2 * i][...], refs[2 * i + 1][...]
            p = _dot_nt(a, b) if nt else _dot(a, b)
            acc = p if acc is None else acc + p
        pos = 2 * n_pairs
        if has_bias:
            acc = acc + refs[pos][...]
            pos += 1
        ex = [r[...] for r in refs[pos:pos + n_extra]]
        vals = (acc,) if epilogue is None else epilogue(acc, *ex)
        for r, v, dt in zip(outs, vals, out_dtypes):
            r[...] = v.astype(dt)
        if n_side:
            finish()

    in_specs, args = [], []
    for a, b, a_col, b_sel in pairs:
        k = b.shape[-1] if nt else b.shape[-2]
        assert a_col is not None or a.shape[1] == k
        if a.ndim == 3:
            in_specs.append(pl.BlockSpec((None, tm, k), lambda j, i, c=a_col: (c, i, 0)))
        else:
            in_specs.append(pl.BlockSpec((tm, k), lambda j, i, c=a_col or 0: (i, c)))
        if b.ndim == 2:
            in_specs.append(pl.BlockSpec((tn, k), lambda j, i: (j, 0), **b_mode) if nt
                            else pl.BlockSpec((k, tn), lambda j, i: (0, j), **b_mode))
        elif nt:
            in_specs.append(pl.BlockSpec((None, tn, k), lambda j, i, s=b_sel: (s, j, 0), **b_mode))
        else:
            assert b_sel == "j" and b.shape[-1] == tn
            in_specs.append(pl.BlockSpec((None, k, tn), lambda j, i: (j, 0, 0), **b_mode))
        args += [a, b]
    if has_bias:
        in_specs.append(pl.BlockSpec((1, tn), lambda j, i: (0, j)))
        args.append(bias)
    for e in extras:
        in_specs.append(pl.BlockSpec((tm, tn), lambda j, i: (i, j)))
        args.append(e)
    out = pl.pallas_call(
        body,
        grid=(n // tn, m // tm),
        in_specs=in_specs + [ANY] * n_side,
        out_specs=[pl.BlockSpec((tm, tn), lambda j, i: (i, j)) for _ in range(n_out)] + [ANY] * n_side,
        out_shape=[jax.ShapeDtypeStruct((m, n), dt) for dt in out_dtypes] + (_side_out_shapes(side) if n_side else []),
        scratch_shapes=_side_sems(side) if n_side else [],
        compiler_params=pltpu.CompilerParams(dimension_semantics=("arbitrary", "arbitrary") if n_side else ("parallel", "parallel")),
        name=name,
    )(*args, *side_arrays)
    if n_side:
        return (out[0] if n_out == 1 else out[:n_out]), out[n_out:]
    return out[0] if n_out == 1 else out


def _mm_tn(a, b, *, tm, tn, name, out_shards=False, slots=None, into=None, group=1):
    k, m = a.shape
    b_shards = b.ndim == 3
    out_shards = out_shards or b_shards
    n = b.shape[0] * b.shape[2] if b_shards else b.shape[1]
    tk = _rows_that_fit(k, 2 * 2 * (tm + group * tn), 3 * tm * group * tn * 4)
    assert m % tm == 0 and n % (group * tn) == 0 and k % tk == 0 and (not b_shards or b.shape[2] == tn)
    assert group == 1 or out_shards
    total, first = slots if slots else (n // tn, 0)

    def body(a_ref, b_ref, *rest):
        o_ref = rest[-1]

        @pl.when(pl.program_id(2) == 0)
        def _():
            o_ref[...] = jnp.zeros_like(o_ref)

        if group == 1:
            o_ref[...] += _dot_tn(a_ref[...], b_ref[...])
        elif b_shards:
            a_blk = a_ref[...]
            for s in range(group):
                o_ref[s] += _dot_tn(a_blk, b_ref[s])
        else:
            acc = _dot_tn(a_ref[...], b_ref[...])
            for s in range(group):
                o_ref[s] += acc[:, s * tn:(s + 1) * tn]

    lead = None if group == 1 else group
    if b_shards:
        b_spec = pl.BlockSpec((lead, tk, tn), lambda i, j, kk: (j, kk, 0))
    else:
        b_spec = pl.BlockSpec((tk, group * tn), lambda i, j, kk: (kk, j))
    if out_shards:
        out_spec = pl.BlockSpec((lead, tm, tn), lambda i, j, kk: (j + first // group, i, 0))
        out_shape = jax.ShapeDtypeStruct((total, m, tn), F32)
    else:
        out_spec = pl.BlockSpec((tm, tn), lambda i, j, kk: (i, j))
        out_shape = jax.ShapeDtypeStruct((m, n), F32)
    return pl.pallas_call(
        body,
        grid=(m // tm, n // (group * tn), k // tk),
        in_specs=[pl.BlockSpec((tk, tm), lambda i, j, kk: (kk, i)), b_spec] + ([ANY] if into is not None else []),
        out_specs=out_spec,
        out_shape=out_shape,
        input_output_aliases={2: 0} if into is not None else {},
        compiler_params=pltpu.CompilerParams(dimension_semantics=("parallel", "parallel", "arbitrary")),
        name=name,
    )(*([a, b] + ([into] if into is not None else [])))


def _rowwise(fn, ins, outs, *, tm, name, side=None):
    rows = next(a.shape[0] for a, kind in ins if kind == "row")
    tm = min(tm, rows)
    assert rows % tm == 0
    n_in, n_out = len(ins), len(outs)
    side_arrays = side[1] if side else []
    n_side = len(side_arrays)

    def body(*refs):
        if n_side:
            finish = _run_side(side, refs[n_in:n_in + n_side], refs[n_in + n_side + n_out:n_in + 2 * n_side + n_out],
                               refs[n_in + 2 * n_side + n_out:], pl.program_id(0), rows // tm)
        vals = fn(*[r[...] for r in refs[:n_in]])
        for (_, dt, kind), r, v in zip(outs, refs[n_in + n_side:n_in + n_side + n_out], vals):
            if kind == "row":
                r[...] = v.astype(dt)
            else:
                @pl.when(pl.program_id(0) == 0)
                def _(r=r):
                    r[...] = jnp.zeros_like(r)

                r[...] += v
        if n_side:
            finish()

    in_specs = [pl.BlockSpec((tm, a.shape[1]), lambda i: (i, 0)) if kind == "row" else pl.BlockSpec(a.shape, lambda i: (0, 0))
                for a, kind in ins]
    out_specs = [pl.BlockSpec((tm, c), lambda i: (i, 0)) if kind == "row" else pl.BlockSpec((1, c), lambda i: (0, 0))
                 for c, _, kind in outs]
    out_shape = [jax.ShapeDtypeStruct((rows if kind == "row" else 1, c), dt) for c, dt, kind in outs]
    ordered = n_side or any(kind == "acc" for _, _, kind in outs)
    return pl.pallas_call(
        body,
        grid=(rows // tm,),
        in_specs=in_specs + [ANY] * n_side,
        out_specs=out_specs + [ANY] * n_side,
        out_shape=out_shape + (_side_out_shapes(side) if n_side else []),
        scratch_shapes=_side_sems(side) if n_side else [],
        compiler_params=pltpu.CompilerParams(dimension_semantics=("arbitrary" if ordered else "parallel",)),
        name=name,
    )(*[a for a, _ in ins], *side_arrays)


def _rstd(x):
    return lax.rsqrt(jnp.mean(x * x, axis=-1, keepdims=True) + NORM_EPS)


def _norm_bwd(dout, xin, g):
    r = _rstd(xin)
    n = xin * r
    dn = dout * g
    dg = jnp.sum(dout * n, axis=0, keepdims=True)
    dx = r * (dn - n * jnp.mean(dn * n, axis=-1, keepdims=True))
    return dx, dg


def _sigmoid(x):
    return 0.5 * jnp.tanh(0.5 * x) + 0.5


def _norm_fwd(x, g, *, name, side=None):
    def fn(x, g):
        return ((x * _rstd(x)) * g,)

    out = _rowwise(fn, [(x, "row"), (g, "vec")], [(D_MODEL, BF16, "row")], tm=512, name=name, side=side)
    return (out[0], out[1:]) if side else out[0]


def _mid_fwd(mix, x, g_post_mix, g_pre_ffn):
    def fn(mix, x, g2, g3):
        x1 = x + (mix * _rstd(mix)) * g2
        return x1, (x1 * _rstd(x1)) * g3

    return _rowwise(fn, [(mix, "row"), (x, "row"), (g_post_mix, "vec"), (g_pre_ffn, "vec")],
                    [(D_MODEL, F32, "row"), (D_MODEL, BF16, "row")], tm=512, name="mid_fwd")


def _loss_bwd(f2, x1, g_post_ffn, tgt):
    def fn(f2, x1, g4, tgt):
        r = _rstd(f2)
        n = f2 * r
        err = x1 + n * g4 - tgt
        loss = 0.5 * jnp.sum(jnp.mean(err * err, axis=-1, keepdims=True), axis=0, keepdims=True)
        dy = err * (1.0 / D_MODEL)
        dn = dy * g4
        dg4 = jnp.sum(dy * n, axis=0, keepdims=True)
        df2 = r * (dn - n * jnp.mean(dn * n, axis=-1, keepdims=True))
        return dy, df2, dg4, jnp.broadcast_to(loss, (1, LANES))

    return _rowwise(fn, [(f2, "row"), (x1, "row"), (g_post_ffn, "vec"), (tgt, "row")],
                    [(D_MODEL, BF16, "row"), (D_MODEL, BF16, "row"), (D_MODEL, F32, "acc"), (LANES, F32, "acc")],
                    tm=512, name="loss_bwd")


def _mid_bwd(dh2, x1, mix, g_pre_ffn, g_post_mix, dy):
    def fn(dh2, x1, mix, g3, g2, dy):
        d3, dg3 = _norm_bwd(dh2.astype(F32), x1, g3)
        dx1 = dy.astype(F32) + d3
        dmix, dg2 = _norm_bwd(dx1, mix, g2)
        return dx1, dmix, dg3, dg2

    return _rowwise(fn, [(dh2, "row"), (x1, "row"), (mix, "row"), (g_pre_ffn, "vec"), (g_post_mix, "vec"), (dy, "row")],
                    [(D_MODEL, BF16, "row"), (D_MODEL, BF16, "row"), (D_MODEL, F32, "acc"), (D_MODEL, F32, "acc")],
                    tm=256, name="mid_bwd")


def _first_bwd(dh, x, g_pre_mix, dx1):
    def fn(dh, x, g1, dx1):
        d1, dg1 = _norm_bwd(dh.astype(F32), x, g1)
        return dx1.astype(F32) + d1, dg1

    return _rowwise(fn, [(dh, "row"), (x, "row"), (g_pre_mix, "vec"), (dx1, "row")],
                    [(D_MODEL, F32, "row"), (D_MODEL, F32, "acc")], tm=512, name="first_bwd")


def _mem_norm_bwd(dhm, mem, g_mem):
    def fn(dhm, mem, g):
        return (jnp.sum(dhm * (mem * _rstd(mem)), axis=0, keepdims=True),)

    return _rowwise(fn, [(dhm, "row"), (mem, "row"), (g_mem, "vec")], [(D_MODEL, F32, "acc")], tm=512, name="mem_norm_bwd")[0]


def _gate_bwd(dmerged, gates, ya, yb, yc):
    def fn(dm, gt, ya, yb, yc):
        dm = dm.astype(F32)
        gt = gt.astype(F32)
        outs, dgp = [], []
        for i, y in enumerate((ya, yb, yc)):
            gi = gt[:, i * D_MODEL:(i + 1) * D_MODEL]
            outs.append(dm * gi)
            dgp.append(dm * y.astype(F32) * gi * (1.0 - gi))
        dgpre = jnp.concatenate(dgp, axis=1)
        return outs[0], outs[1], outs[2], dgpre, jnp.sum(dgpre, axis=0, keepdims=True)

    return _rowwise(fn, [(dmerged, "row"), (gates, "row"), (ya, "row"), (yb, "row"), (yc, "row")],
                    [(D_MODEL, BF16, "row")] * 3 + [(3 * D_MODEL, BF16, "row"), (3 * D_MODEL, F32, "acc")],
                    tm=256, name="gate_bwd")


def _adamw_math(w, g, m, v):
    m = ADAM_B1 * m + (1.0 - ADAM_B1) * g
    v = ADAM_B2 * v + (1.0 - ADAM_B2) * (g * g)
    m_hat = m / (1.0 - ADAM_B1 ** ADAM_STEP)
    v_hat = v / (1.0 - ADAM_B2 ** ADAM_STEP)
    delta = -ADAM_LR * (m_hat / (jnp.sqrt(v_hat) + ADAM_EPS) + ADAM_WD * w)
    return delta, m, v


def _adamw(w, g, m, v, *, tm, name):
    c = w.shape[1]
    return _rowwise(_adamw_math, [(w, "row"), (g, "row"), (m, "row"), (v, "row")], [(c, F32, "row")] * 3, tm=tm, name=name)


def _adamw_halves(w, g_mine, g_theirs, m, v, c_arr, *, name):
    a, b = w.shape
    hf = a // 2
    tr = next(t for t in (hf, hf // 2, hf // 4) if 9 * 3 * t * b * 4 <= V7X_VMEM_BUDGET)

    def body(c_ref, w_ref, gm_ref, gt_ref, m_ref, v_ref, g_out, d_out, m_out, v_out):
        g = jnp.where(pl.program_id(0) == c_ref[0], gm_ref[...], gt_ref[...])
        d, m_new, v_new = _adamw_math(w_ref[...], g, m_ref[...], v_ref[...])
        g_out[...] = g
        d_out[...] = d
        m_out[...] = m_new
        v_out[...] = v_new

    full = pl.BlockSpec((tr, b), lambda hh, i, c_ref: (hh * (hf // tr) + i, 0))
    half = pl.BlockSpec((tr, b), lambda hh, i, c_ref: (i, 0))
    return pl.pallas_call(
        body,
        grid_spec=pltpu.PrefetchScalarGridSpec(
            num_scalar_prefetch=1,
            grid=(2, hf // tr),
            in_specs=[full, half, half, full, full],
            out_specs=[full] * 4,
        ),
        out_shape=[jax.ShapeDtypeStruct((a, b), F32)] * 4,
        compiler_params=pltpu.CompilerParams(dimension_semantics=("parallel", "parallel")),
        name=name,
    )(c_arr, w, g_mine, g_theirs, m, v)


def _ffn_in_fwd(h2, w_ffn):
    m, tm, tn = h2.shape[0], 512, w_ffn.shape[2]
    assert 2 * tn == D_FF

    def body(h_ref, wg_ref, wu_ref, g_ref, u_ref, f_ref):
        h = h_ref[...]
        g = _dot(h, wg_ref[...])
        u = _dot(h, wu_ref[...])
        g_ref[...] = g.astype(BF16)
        u_ref[...] = u.astype(BF16)
        f_ref[...] = (g * _sigmoid(g) * u).astype(BF16)

    o_spec = pl.BlockSpec((tm, tn), lambda j, i: (i, j))
    return pl.pallas_call(
        body,
        grid=(D_FF // tn, m // tm),
        in_specs=[pl.BlockSpec((tm, D_MODEL), lambda j, i: (i, 0)),
                  pl.BlockSpec((None, D_MODEL, tn), lambda j, i: (j, 0, 0)),
                  pl.BlockSpec((None, D_MODEL, tn), lambda j, i: (j + 2, 0, 0))],
        out_specs=[o_spec, o_spec, o_spec],
        out_shape=[jax.ShapeDtypeStruct((m, D_FF), BF16)] * 3,
        compiler_params=pltpu.CompilerParams(dimension_semantics=("parallel", "parallel")),
        name="ffn_in_fwd",
    )(h2, w_ffn, w_ffn)


def _join_shards(w4):
    _, rows, cols = w4.shape
    tr = rows // 4

    def body(w_ref, o_ref):
        for s in range(N_CHIPS):
            o_ref[:, s * cols:(s + 1) * cols] = w_ref[s]

    return pl.pallas_call(
        body,
        grid=(rows // tr,),
        in_specs=[pl.BlockSpec((N_CHIPS, tr, cols), lambda i: (0, i, 0))],
        out_specs=pl.BlockSpec((tr, N_CHIPS * cols), lambda i: (i, 0)),
        out_shape=jax.ShapeDtypeStruct((rows, N_CHIPS * cols), w4.dtype),
        compiler_params=pltpu.CompilerParams(dimension_semantics=("parallel",)),
        name="join_shards",
    )(w4)


def _split_to_shards(pieces, *, name):
    t = pieces[0].shape[0]
    widths = [p.shape[1] for p in pieces]
    cols = sum(widths) // N_CHIPS
    tm = 512
    plan, start = [], 0
    for p, wd in enumerate(widths):
        for s in range(N_CHIPS):
            lo, hi = max(start, s * cols), min(start + wd, (s + 1) * cols)
            if lo < hi:
                plan.append((s, p, lo - s * cols, hi - s * cols, lo - start, hi - start))
        start += wd

    def body(*refs):
        o_ref = refs[-1]
        for s, p, o_lo, o_hi, p_lo, p_hi in plan:
            o_ref[s, :, o_lo:o_hi] = refs[p][:, p_lo:p_hi]

    return pl.pallas_call(
        body,
        grid=(t // tm,),
        in_specs=[pl.BlockSpec((tm, wd), lambda i: (i, 0)) for wd in widths],
        out_specs=pl.BlockSpec((N_CHIPS, tm, cols), lambda i: (0, i, 0)),
        out_shape=jax.ShapeDtypeStruct((N_CHIPS, t, cols), pieces[0].dtype),
        compiler_params=pltpu.CompilerParams(dimension_semantics=("parallel",)),
        name=name,
    )(*pieces)


def _swiglu_bwd_epilogue(df, g, u):
    g = g.astype(F32)
    u = u.astype(F32)
    sg = _sigmoid(g)
    return df * u * (sg * (1.0 + g * (1.0 - sg))), df * (g * sg)


def _branch_merge_fwd(o_a, o_b, o_c, w_sb, w_dil, w_mem, gates):
    m, tm = o_a.shape[0], 512

    def body(oa_ref, ob_ref, oc_ref, wa_ref, wb_ref, wc_ref, gt_ref, ya_ref, yb_ref, yc_ref, mg_ref):
        def project(o_ref, w_ref):
            o = o_ref[...]
            return jnp.concatenate([_dot(o, w_ref[s]) for s in range(N_CHIPS)], axis=1)

        ya = project(oa_ref, wa_ref)
        yb = project(ob_ref, wb_ref)
        yc = project(oc_ref, wc_ref)
        gt = gt_ref[...].astype(F32)
        ya_ref[...] = ya.astype(BF16)
        yb_ref[...] = yb.astype(BF16)
        yc_ref[...] = yc.astype(BF16)
        mg_ref[...] = (gt[:, :D_MODEL] * ya + gt[:, D_MODEL:2 * D_MODEL] * yb + gt[:, 2 * D_MODEL:] * yc).astype(BF16)

    row = lambda c: pl.BlockSpec((tm, c), lambda i: (i, 0))
    full = lambda a: pl.BlockSpec(a.shape, lambda i: (0, 0, 0))
    return pl.pallas_call(
        body,
        grid=(m // tm,),
        in_specs=[row(SB_W), row(DIL_W), row(MEM_W), full(w_sb), full(w_dil), full(w_mem), row(3 * D_MODEL)],
        out_specs=[row(D_MODEL)] * 4,
        out_shape=[jax.ShapeDtypeStruct((m, D_MODEL), BF16)] * 4,
        compiler_params=pltpu.CompilerParams(dimension_semantics=("parallel",)),
        name="branch_merge_fwd",
    )(o_a, o_b, o_c, w_sb, w_dil, w_mem, gates)


SB_T = 256
SB_SCALE = HEAD_DIM ** -0.5


def _sb_masks():
    row = lax.broadcasted_iota(jnp.int32, (SB_T, SB_T), 0)
    col = lax.broadcasted_iota(jnp.int32, (SB_T, SB_T), 1)
    lane = lax.broadcasted_iota(jnp.int32, (1, LANES), 1)
    return row, col, lane


def _sb_logs(z):
    lb = jnp.minimum(z, 0.0) - jnp.log(1.0 + jnp.exp(-jnp.abs(z)))
    return lb, lb - z


def _sb_specs(n_heads_pairs, col0):
    q = pl.BlockSpec((None, SB_T, LANES), lambda b, p, i: (b, i, col0 + p))
    k = pl.BlockSpec((None, SEQ, LANES), lambda b, p, i: (b, 0, col0 + n_heads_pairs + p))
    v = pl.BlockSpec((None, SEQ, LANES), lambda b, p, i: (b, 0, col0 + 2 * n_heads_pairs + p))
    return q, k, v


def _sb_first_blocks(i):
    rem = lax.rem(i + 1, 3)
    return jnp.where(rem == 0, 3, rem)


def _grid_step(n_pairs, nq):
    return (pl.program_id(0) * n_pairs + pl.program_id(1)) * nq + pl.program_id(2)


def _sb_fwd(proj3, late_shards):
    bl = proj3.shape[0]
    n_pairs = SB_W // LANES
    nq = SEQ // SB_T
    n_late = len(late_shards)
    n_steps = bl * n_pairs * nq

    def body(q_ref, k_ref, v_ref, *rest):
        late_in, (o_ref, o32_ref, w_ref), late_out = rest[:n_late], rest[n_late:n_late + 3], rest[n_late + 3:2 * n_late + 3]
        step = _grid_step(n_pairs, nq)
        if n_late:
            send, forward, finish = _gather_phases(late_in, late_out, *rest[2 * n_late + 3:])
            pl.when(step == 0)(send)
            pl.when(step == n_steps // 2)(forward)
        i = pl.program_id(2)
        row, col, lane = _sb_masks()
        causal = col < row
        u_excl = (row > col).astype(BF16)
        q = q_ref[...]
        heads = []
        for h in range(2):
            mh = (lane // HEAD_DIM) == h
            heads.append((mh, jnp.where(mh, q, jnp.zeros_like(q)) * SB_SCALE))

        def blocks(js, diags, carries, acc):
            ks = [k_ref[pl.ds(pl.multiple_of(j * SB_T, SB_T), SB_T), :] for j in js]
            vs = [v_ref[pl.ds(pl.multiple_of(j * SB_T, SB_T), SB_T), :] for j in js]
            chains = [(b, h) for b in range(len(js)) for h in range(2)]
            z = {c: _dot_nt(heads[c[1]][1], ks[c[0]]) for c in chains}
            lb, lk = {}, {}
            for c in chains:
                lb[c], lk[c] = _sb_logs(z[c])
                if diags[c[0]]:
                    lk[c] = jnp.where(causal, lk[c], 0.0)
            r = {c: _split_dot(lk[c], u_excl) for c in chains}
            carries = list(carries)
            w = {}
            for b, h in chains:
                w_c = jnp.exp(lb[b, h] + r[b, h] + carries[h])
                w[b, h] = (jnp.where(causal, w_c, 0.0) if diags[b] else w_c).astype(BF16)
                w_ref[h, js[b]] = w[b, h]
                carries[h] = carries[h] + (r[b, h][:, :1] + lk[b, h][:, :1])
            for b, h in chains:
                acc = acc + _dot(w[b, h], jnp.where(heads[h][0], vs[b], jnp.zeros_like(vs[b])))
            return tuple(carries), acc

        zero = jnp.zeros((SB_T, 1), F32)
        init = ((zero, zero), jnp.zeros((SB_T, LANES), F32))
        first = _sb_first_blocks(i)
        carries, acc = lax.cond(
            first == 1, lambda: blocks([i], (True,), *init),
            lambda: lax.cond(first == 2, lambda: blocks([i, i - 1], (True, False), *init),
                             lambda: blocks([i, i - 1, i - 2], (True, False, False), *init)))
        rest = i - first
        carries, acc = lax.fori_loop(
            0, (rest + 1) // 3,
            lambda jj, c: blocks([rest - 3 * jj, rest - 1 - 3 * jj, rest - 2 - 3 * jj], (False, False, False), c[0], c[1]),
            (carries, acc))
        o_ref[...] = acc.astype(BF16)
        o32_ref[...] = acc
        if n_late:
            pl.when(step == n_steps - 1)(finish)

    q_spec, k_spec, v_spec = _sb_specs(n_pairs, 0)
    blk = pl.BlockSpec((None, SB_T, LANES), lambda b, p, i: (b, i, p))
    out = pl.pallas_call(
        body,
        grid=(bl, n_pairs, nq),
        in_specs=[q_spec, k_spec, v_spec] + [ANY] * n_late,
        out_specs=[blk, blk, _sb_weight_spec(nq)] + [ANY] * n_late,
        out_shape=[jax.ShapeDtypeStruct((bl, SEQ, SB_W), BF16), jax.ShapeDtypeStruct((bl, SEQ, SB_W), F32),
                   jax.ShapeDtypeStruct((bl, n_pairs, nq, 2, nq, SB_T, SB_T), BF16)] + _gather_out_shapes(late_shards),
        scratch_shapes=_gather_sems(n_late) if n_late else [],
        compiler_params=pltpu.CompilerParams(dimension_semantics=("arbitrary", "arbitrary", "arbitrary")),
        name="sb_fwd",
    )(proj3, proj3, proj3, *late_shards)
    return out[0], out[1], out[2], out[3:]


def _sb_weight_spec(nq):
    return pl.BlockSpec((None, None, None, 2, nq, SB_T, SB_T), lambda b, p, i: (b, p, i, 0, 0, 0, 0))


def _sb_bwd(proj3, o_a, do_a, w_all, parts):
    bl = proj3.shape[0]
    n_pairs = SB_W // LANES
    nq = SEQ // SB_T
    n_parts = len(parts)
    n_steps = bl * n_pairs * nq

    def body(q_ref, k_ref, v_ref, o_ref, do_ref, w_ref, *rest):
        p_refs, (dq_ref, dk_ref, dv_ref), land_refs = rest[:n_parts], rest[n_parts:n_parts + 3], rest[n_parts + 3:2 * n_parts + 3]
        dk_acc, dv_acc = rest[2 * n_parts + 3:2 * n_parts + 5]
        step = _grid_step(n_pairs, nq)
        if n_parts:
            send, finish = _chip_exchange_phases(p_refs, land_refs, *rest[2 * n_parts + 5:])
            pl.when(step == 0)(send)
        i = pl.program_id(2)

        @pl.when(i == 0)
        def _():
            dk_acc[...] = jnp.zeros_like(dk_acc)
            dv_acc[...] = jnp.zeros_like(dv_acc)

        row, col, lane = _sb_masks()
        causal = col < row
        u_incl = (row >= col).astype(BF16)
        q = q_ref[...]
        do = do_ref[...]
        prod = do.astype(F32) * o_ref[...]
        heads = []
        for h in range(2):
            mh = (lane // HEAD_DIM) == h
            d_tot = jnp.sum(jnp.where(mh, prod, 0.0), axis=1, keepdims=True)
            heads.append((mh, jnp.where(mh, q, jnp.zeros_like(q)) * SB_SCALE, jnp.where(mh, do, jnp.zeros_like(do)), d_tot))

        def blocks(js, diags, c_das, dq):
            starts = [pl.multiple_of(j * SB_T, SB_T) for j in js]
            ks = [k_ref[pl.ds(s, SB_T), :] for s in starts]
            vs = [v_ref[pl.ds(s, SB_T), :] for s in starts]
            chains = [(b, h) for b in range(len(js)) for h in range(2)]
            z = {c: _dot_nt(heads[c[1]][1], ks[c[0]]) for c in chains}
            dw = {c: _dot_nt(heads[c[1]][2], vs[c[0]]) for c in chains}
            wb = {(b, h): w_ref[h, js[b]] for b, h in chains}
            da = {c: dw[c] * wb[c].astype(F32) for c in chains}
            sfx = {c: _split_dot(da[c], u_incl) for c in chains}
            c_das = list(c_das)
            dz = {}
            for b, h in chains:
                dlk = heads[h][3] - c_das[h] - sfx[b, h]
                if diags[b]:
                    dlk = jnp.where(causal, dlk, 0.0)
                c_das[h] = c_das[h] + sfx[b, h][:, :1]
                e = jnp.exp(-jnp.abs(z[b, h]))
                inv = 1.0 / (1.0 + e)
                pos = z[b, h] >= 0.0
                beta = jnp.where(pos, inv, e * inv)
                one_m_beta = jnp.where(pos, e * inv, inv)
                dz[b, h] = (da[b, h] * one_m_beta - dlk * beta).astype(BF16)
            for b, h in chains:
                dq = dq + _dot(dz[b, h], jnp.where(heads[h][0], ks[b], jnp.zeros_like(ks[b])))
            for b in range(len(js)):
                dk_acc[pl.ds(starts[b], SB_T), :] += _dot_tn(dz[b, 0], heads[0][1]) + _dot_tn(dz[b, 1], heads[1][1])
                dv_acc[pl.ds(starts[b], SB_T), :] += _dot_tn(wb[b, 0], heads[0][2]) + _dot_tn(wb[b, 1], heads[1][2])
            return tuple(c_das), dq

        zero = jnp.zeros((SB_T, 1), F32)
        init = ((zero, zero), jnp.zeros((SB_T, LANES), F32))
        first = _sb_first_blocks(i)
        state = lax.cond(
            first == 1, lambda: blocks([i], (True,), *init),
            lambda: lax.cond(first == 2, lambda: blocks([i, i - 1], (True, False), *init),
                             lambda: blocks([i, i - 1, i - 2], (True, False, False), *init)))
        rest = i - first
        state = lax.fori_loop(
            0, (rest + 1) // 3,
            lambda jj, c: blocks([rest - 3 * jj, rest - 1 - 3 * jj, rest - 2 - 3 * jj], (False, False, False), c[0], c[1]), state)
        dq_ref[...] = (state[1] * SB_SCALE).astype(BF16)

        @pl.when(i == nq - 1)
        def _():
            dk_ref[...] = dk_acc[...].astype(BF16)
            dv_ref[...] = dv_acc[...].astype(BF16)

        if n_parts:
            pl.when(step == n_steps - 1)(finish)

    q_spec, k_spec, v_spec = _sb_specs(n_pairs, 0)
    blk = pl.BlockSpec((None, SB_T, LANES), lambda b, p, i: (b, i, p))
    seq = pl.BlockSpec((None, SEQ, LANES), lambda b, p, i: (b, 0, p))
    shape = jax.ShapeDtypeStruct((bl, SEQ, SB_W), BF16)
    out = pl.pallas_call(
        body,
        grid=(bl, n_pairs, nq),
        in_specs=[q_spec, k_spec, v_spec, blk, blk, _sb_weight_spec(nq)] + [ANY] * n_parts,
        out_specs=[blk, seq, seq] + [ANY] * n_parts,
        out_shape=[shape, shape, shape] + [jax.ShapeDtypeStruct(p.shape, p.dtype) for p in parts],
        scratch_shapes=[pltpu.VMEM((SEQ, LANES), F32), pltpu.VMEM((SEQ, LANES), F32)]
        + (_chip_exchange_sems(n_parts) if n_parts else []),
        compiler_params=pltpu.CompilerParams(dimension_semantics=("arbitrary", "arbitrary", "arbitrary")),
        name="sb_bwd",
    )(proj3, proj3, proj3, o_a, do_a, w_all, *parts)
    return out[0], out[1], out[2], out[3:]


BAND = 128


BAND_HEADS = DIL_W // HEAD_DIM


def _swap_half(x):
    n = x.shape[-1]
    lane = lax.broadcasted_iota(jnp.int32, (1, n), 1)
    return jnp.where((lane % HEAD_DIM) < HEAD_DIM // 2, pltpu.roll(x, n - HEAD_DIM // 2, 1), pltpu.roll(x, HEAD_DIM // 2, 1))


def _rope(x, cos, sin_signed):
    x = x.astype(F32)
    return x * cos + _swap_half(x) * sin_signed


def _band_delta(do_b, o_b, lse_b):
    def fn(do, o, lse):
        lane_in = lax.broadcasted_iota(jnp.int32, (DIL_W, LANES), 0)
        col = lax.broadcasted_iota(jnp.int32, (DIL_W, LANES), 1)
        sum_head = ((lane_in // HEAD_DIM == col - BAND_HEADS) & (col >= BAND_HEADS) & (col < 2 * BAND_HEADS)).astype(BF16)
        return (lse + _split_dot(do.astype(F32) * o.astype(F32), sum_head),)

    return _rowwise(fn, [(do_b, "row"), (o_b, "row"), (lse_b, "row")], [(LANES, F32, "row")], tm=512, name="band_delta")[0]


def _band_masks():
    qi = lax.broadcasted_iota(jnp.int32, (BAND, 2 * BAND), 0) + BAND
    kj = lax.broadcasted_iota(jnp.int32, (BAND, 2 * BAND), 1)
    dist = qi - kj
    row = lax.broadcasted_iota(jnp.int32, (BAND, BAND), 0)
    col = lax.broadcasted_iota(jnp.int32, (BAND, BAND), 1)
    return col <= row, (dist >= 0) & (dist <= BAND)


def _band_attend(q, k, v, valid):
    lane = lax.broadcasted_iota(jnp.int32, (1, DIL_W), 1)
    stat_lane = lax.broadcasted_iota(jnp.int32, (1, LANES), 1)
    hs = range(BAND_HEADS)
    mh = [(lane // HEAD_DIM) == h for h in hs]
    s = [jnp.where(valid, _dot_nt(jnp.where(mh[h], q, jnp.zeros_like(q)), k), NEG_INF) for h in hs]
    m = [jnp.max(s[h], axis=1, keepdims=True) for h in hs]
    p = [jnp.exp(s[h] - m[h]) for h in hs]
    den = [jnp.sum(p[h], axis=1, keepdims=True) for h in hs]
    pv = [_dot(p[h].astype(BF16), jnp.where(mh[h], v, jnp.zeros_like(v))) for h in hs]
    o = jnp.zeros((BAND, DIL_W), F32)
    lse = jnp.zeros((BAND, LANES), F32)
    for h in hs:
        o = o + pv[h] * (1.0 / den[h])
        lse = jnp.where(stat_lane == h, m[h] + jnp.log(den[h]), lse)
    return o, lse


def _band_attend_bwd(q, k, v, valid, do, st):
    lane = lax.broadcasted_iota(jnp.int32, (1, DIL_W), 1)
    hs = range(BAND_HEADS)
    mh = [(lane // HEAD_DIM) == h for h in hs]
    qh = [jnp.where(mh[h], q, jnp.zeros_like(q)) for h in hs]
    doh = [jnp.where(mh[h], do, jnp.zeros_like(do)) for h in hs]
    s = [_dot_nt(qh[h], k) for h in hs]
    dp = [_dot_nt(doh[h], v) for h in hs]
    p = [jnp.where(valid, jnp.exp(s[h] - st[:, h:h + 1]), 0.0) for h in hs]
    ds = [(p[h] * (dp[h] - st[:, BAND_HEADS + h:BAND_HEADS + h + 1])).astype(BF16) for h in hs]
    pb = [p[h].astype(BF16) for h in hs]
    dq = sum(_dot(ds[h], jnp.where(mh[h], k, jnp.zeros_like(k))) for h in hs)
    dk = sum(_dot_tn(ds[h], qh[h]) for h in hs)
    dv = sum(_dot_tn(pb[h], doh[h]) for h in hs)
    return dq, dk, dv


def _band_group_specs(lead, rows, cls, col0):
    def spec(width):
        if lead == "rows":
            return pl.BlockSpec((None, rows, width), lambda b, i: (b, 0, col0))
        return pl.BlockSpec((None, rows, cls * width), lambda b, i: (b, 0, i))
    return spec


def _band_group_fwd(a, cos_g, sin_g, *, rows, cls, steps, col0, name):
    bl = a.shape[0]
    nb = rows // BAND
    grp_w = 3 * DIL_W

    def body(a_ref, c_ref, s_ref, o_ref, l_ref, qr_all, kr_all):
        first_valid, later_valid = _band_masks()
        for j in range(cls):
            qr, kr = qr_all.at[j], kr_all.at[j]
            a0, t0, s0 = j * grp_w, j * DIL_W, j * LANES
            cos, sin = c_ref[:, t0:t0 + DIL_W], s_ref[:, t0:t0 + DIL_W]
            qr[...] = (_rope(a_ref[:, a0:a0 + DIL_W], cos, sin) * SB_SCALE).astype(BF16)
            kr[...] = _rope(a_ref[:, a0 + DIL_W:a0 + 2 * DIL_W], cos, sin).astype(BF16)

            def block(q0, k0, keys, valid, a0=a0, t0=t0, s0=s0):
                o, lse = _band_attend(qr[pl.ds(q0, BAND), :], kr[pl.ds(k0, keys), :],
                                      a_ref[pl.ds(k0, keys), a0 + 2 * DIL_W:a0 + grp_w], valid)
                o_ref[pl.ds(q0, BAND), t0:t0 + DIL_W] = o.astype(BF16)
                l_ref[pl.ds(q0, BAND), s0:s0 + LANES] = lse

            block(0, 0, BAND, first_valid)
            if nb > 1:
                def later(b, carry, block=block):
                    block(pl.multiple_of(b * BAND, BAND), pl.multiple_of((b - 1) * BAND, BAND), 2 * BAND, later_valid)
                    return carry

                lax.fori_loop(1, nb, later, 0, unroll=5 if (nb - 1) % 5 == 0 else 3)

    lead = "rows" if col0 is not None else "cols"
    spec = _band_group_specs(lead, rows, cls, col0)
    tab = pl.BlockSpec((rows, cls * DIL_W), lambda b, i: (0, i))
    n_cls = cos_g.shape[1] // DIL_W
    return pl.pallas_call(
        body,
        grid=(bl, steps),
        in_specs=[spec(grp_w), tab, tab],
        out_specs=[pl.BlockSpec((None, rows, cls * DIL_W), lambda b, i: (b, 0, i)),
                   pl.BlockSpec((None, rows, cls * LANES), lambda b, i: (b, 0, i))],
        out_shape=[jax.ShapeDtypeStruct((bl, rows, n_cls * DIL_W), BF16), jax.ShapeDtypeStruct((bl, rows, n_cls * LANES), F32)],
        scratch_shapes=[pltpu.VMEM((cls, rows, DIL_W), BF16), pltpu.VMEM((cls, rows, DIL_W), BF16)],
        compiler_params=pltpu.CompilerParams(dimension_semantics=("parallel", "parallel")),
        name=name,
    )(a, cos_g, sin_g)


def _band_group_bwd(a, do, st, cos_g, sin_g, *, rows, cls, steps, col0, name, side=None):
    bl = a.shape[0]
    nb = rows // BAND
    grp_w = 3 * DIL_W
    side_arrays = side[1] if side else []
    n_side = len(side_arrays)

    def body(a_ref, do_ref, st_ref, c_ref, s_ref, *rest):
        out_ref = rest[n_side]
        scratch = rest[2 * n_side + 1:2 * n_side + 5]
        if n_side:
            step = pl.program_id(0) * steps + pl.program_id(1)
            finish = _run_side(side, rest[:n_side], rest[n_side + 1:2 * n_side + 1], rest[2 * n_side + 5:], step, bl * steps)
        first_valid, later_valid = _band_masks()
        for j in range(cls):
            qr, kr, dk_acc, dv_acc = (s.at[j] for s in scratch)
            a0, t0 = j * grp_w, j * DIL_W
            cos, sin = c_ref[:, t0:t0 + DIL_W], s_ref[:, t0:t0 + DIL_W]
            qr[...] = (_rope(a_ref[:, a0:a0 + DIL_W], cos, sin) * SB_SCALE).astype(BF16)
            kr[...] = _rope(a_ref[:, a0 + DIL_W:a0 + 2 * DIL_W], cos, sin).astype(BF16)
            dk_acc[...] = jnp.zeros_like(dk_acc)
            dv_acc[...] = jnp.zeros_like(dv_acc)

            def block(q0, k0, keys, valid, a0=a0, t0=t0, s0=j * LANES):
                qrows, krows = pl.ds(q0, BAND), pl.ds(k0, keys)
                dq, dk, dv = _band_attend_bwd(
                    qr[qrows, :], kr[krows, :], a_ref[krows, a0 + 2 * DIL_W:a0 + grp_w], valid,
                    do_ref[qrows, t0:t0 + DIL_W], st_ref[qrows, s0:s0 + LANES])
                dq = dq * SB_SCALE
                out_ref[qrows, a0:a0 + DIL_W] = (dq * c_ref[qrows, t0:t0 + DIL_W]
                                                 - _swap_half(dq) * s_ref[qrows, t0:t0 + DIL_W]).astype(BF16)
                dk_acc[krows, :] += dk
                dv_acc[krows, :] += dv

            block(0, 0, BAND, first_valid)
            if nb > 1:
                def later(b, carry, block=block):
                    block(pl.multiple_of(b * BAND, BAND), pl.multiple_of((b - 1) * BAND, BAND), 2 * BAND, later_valid)
                    return carry

                lax.fori_loop(1, nb, later, 0, unroll=5 if (nb - 1) % 5 == 0 else 3)
            dk = dk_acc[...]
            out_ref[:, a0 + DIL_W:a0 + 2 * DIL_W] = (dk * cos - _swap_half(dk) * sin).astype(BF16)
            out_ref[:, a0 + 2 * DIL_W:a0 + grp_w] = dv_acc[...].astype(BF16)
        if n_side:
            finish()

    lead = "rows" if col0 is not None else "cols"
    spec = _band_group_specs(lead, rows, cls, col0)
    dspec = _band_group_specs(lead, rows, cls, 0 if col0 is not None else None)
    tab = pl.BlockSpec((rows, cls * DIL_W), lambda b, i: (0, i))
    n_cls = cos_g.shape[1] // DIL_W
    out = pl.pallas_call(
        body,
        grid=(bl, steps),
        in_specs=[spec(grp_w), dspec(DIL_W), dspec(LANES), tab, tab] + [ANY] * n_side,
        out_specs=[pl.BlockSpec((None, rows, cls * grp_w), lambda b, i: (b, 0, i))] + [ANY] * n_side,
        out_shape=[jax.ShapeDtypeStruct((bl, rows, n_cls * grp_w), BF16)] + (_side_out_shapes(side) if n_side else []),
        scratch_shapes=[pltpu.VMEM((cls, rows, DIL_W), BF16), pltpu.VMEM((cls, rows, DIL_W), BF16),
                        pltpu.VMEM((cls, rows, DIL_W), F32), pltpu.VMEM((cls, rows, DIL_W), F32)] + (_side_sems(side) if n_side else []),
        compiler_params=pltpu.CompilerParams(dimension_semantics=("arbitrary", "arbitrary") if n_side else ("parallel", "parallel")),
        name=name,
    )(a, do, st, cos_g, sin_g, *side_arrays)
    return (out[0], out[1:]) if n_side else out[0]


def _band_merge3(groups):
    t, tm = groups[0][0].shape[0], 512

    def body(o0, l0, o1, l1, o2, l2, ob_ref, lse_ref):
        a, b, c = l0[...], l1[...], l2[...]
        m = jnp.maximum(jnp.maximum(a, b), c)
        lse = m + jnp.log(jnp.exp(a - m) + jnp.exp(b - m) + jnp.exp(c - m))
        lane = lax.broadcasted_iota(jnp.int32, (1, DIL_W), 1)
        acc = jnp.zeros((tm, DIL_W), F32)
        for o_ref, l in ((o0, a), (o1, b), (o2, c)):
            share = jnp.exp(l - lse)
            spread = jnp.zeros((tm, DIL_W), F32)
            for h in range(BAND_HEADS):
                spread = jnp.where(lane // HEAD_DIM == h, share[:, h:h + 1], spread)
            acc = acc + spread * o_ref[...].astype(F32)
        ob_ref[...] = acc.astype(BF16)
        stat_lane = lax.broadcasted_iota(jnp.int32, (1, LANES), 1)
        lse_ref[...] = jnp.where(stat_lane < BAND_HEADS, lse, 0.0)

    spec = pl.BlockSpec((tm, DIL_W), lambda i: (i, 0))
    spec_l = pl.BlockSpec((tm, LANES), lambda i: (i, 0))
    return pl.pallas_call(
        body,
        grid=(t // tm,),
        in_specs=[spec, spec_l] * 3,
        out_specs=[spec, spec_l],
        out_shape=[jax.ShapeDtypeStruct((t, DIL_W), BF16), jax.ShapeDtypeStruct((t, LANES), F32)],
        compiler_params=pltpu.CompilerParams(dimension_semantics=("parallel",)),
        name="band_merge",
    )(*[a for g in groups for a in g])


MEM_T = 512
MEM_SCALE = 128 ** -0.5
MEM_Q_COL = (D_IN - MEM_W) // LANES


MEM_HEADS = MEM_W // LANES


def _mem_specs():
    qs = [pl.BlockSpec((None, MEM_T, LANES), lambda b, i, h=h: (b, i, MEM_Q_COL + h)) for h in range(MEM_HEADS)]
    kv = pl.BlockSpec((None, MEM_LEN, 2 * MEM_W), lambda b, i: (b, 0, 0))
    blk = pl.BlockSpec((None, MEM_T, MEM_W), lambda b, i: (b, i, 0))
    return qs, kv, blk


def _mem_probs(q, k):
    s = _dot_nt(q, k) * MEM_SCALE
    p = jnp.exp(s - jnp.max(s, axis=1, keepdims=True))
    return p * (1.0 / jnp.sum(p, axis=1, keepdims=True))


def _head_cols(h, base=0):
    return slice(base + h * LANES, base + (h + 1) * LANES)


def _mem_fwd(proj3, kv3):
    bl = proj3.shape[0]
    hs = range(MEM_HEADS)

    def body(*refs):
        q_refs, kv_ref, o_ref = refs[:MEM_HEADS], refs[MEM_HEADS], refs[MEM_HEADS + 1]
        p = [_mem_probs(q_refs[h][...], kv_ref[:, _head_cols(h)]) for h in hs]
        for h in hs:
            o_ref[:, _head_cols(h)] = _dot(p[h].astype(BF16), kv_ref[:, _head_cols(h, MEM_W)]).astype(BF16)

    qs, kv, blk = _mem_specs()
    return pl.pallas_call(
        body,
        grid=(bl, SEQ // MEM_T),
        in_specs=qs + [kv],
        out_specs=blk,
        out_shape=jax.ShapeDtypeStruct((bl, SEQ, MEM_W), BF16),
        compiler_params=pltpu.CompilerParams(dimension_semantics=("parallel", "parallel")),
        name="mem_fwd",
    )(*([proj3] * MEM_HEADS), kv3)


def _mem_bwd(proj3, kv3, do_c):
    bl = proj3.shape[0]
    hs = range(MEM_HEADS)

    def body(*refs):
        q_refs, kv_ref, do_ref, dq_ref, dkv_ref = refs[:MEM_HEADS], *refs[MEM_HEADS:MEM_HEADS + 4]

        @pl.when(pl.program_id(1) == 0)
        def _():
            dkv_ref[...] = jnp.zeros_like(dkv_ref)

        q = [q_refs[h][...] for h in hs]
        do = [do_ref[:, _head_cols(h)] for h in hs]
        p = [_mem_probs(q[h], kv_ref[:, _head_cols(h)]) for h in hs]
        dp = [_dot_nt(do[h], kv_ref[:, _head_cols(h, MEM_W)]) for h in hs]
        ds = [(p[h] * (dp[h] - jnp.sum(p[h] * dp[h], axis=1, keepdims=True)) * MEM_SCALE).astype(BF16) for h in hs]
        for h in hs:
            dq_ref[:, _head_cols(h)] = _dot(ds[h], kv_ref[:, _head_cols(h)]).astype(BF16)
            dkv_ref[:, _head_cols(h)] += _dot_tn(ds[h], q[h])
            dkv_ref[:, _head_cols(h, MEM_W)] += _dot_tn(p[h].astype(BF16), do[h])

    qs, kv, blk = _mem_specs()
    return pl.pallas_call(
        body,
        grid=(bl, SEQ // MEM_T),
        in_specs=qs + [kv, blk],
        out_specs=[blk, kv],
        out_shape=[jax.ShapeDtypeStruct((bl, SEQ, MEM_W), BF16), jax.ShapeDtypeStruct((bl, MEM_LEN, 2 * MEM_W), F32)],
        compiler_params=pltpu.CompilerParams(dimension_semantics=("parallel", "arbitrary")),
        name="mem_bwd",
    )(*([proj3] * MEM_HEADS), kv3, do_c)


def _place():
    x, y, c = lax.axis_index("x"), lax.axis_index("y"), lax.axis_index("c")
    return x, y, c


def _other_chips(x, y):
    return [(1 - x, y), (x, 1 - y), (1 - x, 1 - y)]


def _remote(src, dst, send_sem, recv_sem, to):
    return pltpu.make_async_remote_copy(src_ref=src, dst_ref=dst, send_sem=send_sem, recv_sem=recv_sem,
                                        device_id=to, device_id_type=MESH)


ANY = pl.BlockSpec(memory_space=pl.ANY)


def _gather_out_shapes(shards):
    return [jax.ShapeDtypeStruct((N_CHIPS,) + s.shape, s.dtype) for s in shards]


def _gather_sems(n):
    return [pltpu.SemaphoreType.DMA((6 * n,)), pltpu.SemaphoreType.DMA((6 * n,))]


def _gather_phases(in_refs, out_refs, send_sems, recv_sems):
    x, y, c = _place()
    sibling = (x, y, 1 - c)
    chips = _other_chips(x, y)
    first, passed = [], []
    for k in range(len(in_refs)):
        hf = in_refs[k].shape[0] // 2

        def half(px, py, pc, k=k, hf=hf):
            return out_refs[k].at[2 * px + py, pl.ds(pc * hf, hf), :]

        src = in_refs[k].at[pl.ds(c * hf, hf), :]
        for j, chip in enumerate(chips):
            s = 6 * k + j
            first.append(_remote(src, half(x, y, c), send_sems.at[s], recv_sems.at[s], (*chip, c)))
            passed.append((_remote(src, half(*chip, c), send_sems.at[s], recv_sems.at[s], (*chip, c)),
                           _remote(half(*chip, c), half(*chip, c), send_sems.at[s + 3], recv_sems.at[s + 3], sibling),
                           _remote(src, half(*chip, 1 - c), send_sems.at[s + 3], recv_sems.at[s + 3], sibling)))

    def send():
        for cp in first:
            cp.start()

    def forward():
        for landed, fwd, _ in passed:
            landed.wait_recv()
            fwd.start()

    def finish():
        for _, _, from_sibling in passed:
            from_sibling.wait_recv()
        for cp in first:
            cp.wait_send()
        for _, fwd, _ in passed:
            fwd.wait_send()

    return send, forward, finish


def _pair_exchange(grads, *, name):
    n = len(grads)
    side = ("pair", grads)

    def body(*refs):
        send, _, finish = _side_phases(side, refs[:n], refs[n:2 * n], refs[2 * n:])
        send()
        finish()

    return pl.pallas_call(
        body,
        in_specs=[ANY] * n,
        out_specs=[ANY] * n,
        out_shape=_side_out_shapes(side),
        scratch_shapes=_side_sems(side),
        name=name,
    )(*grads)


def _pair_exchange_phases(g_refs, land_refs, send_sems, recv_sems):
    x, y, c = _place()
    cps = []
    for k in range(len(g_refs)):
        hf = g_refs[k].shape[1] // 2
        src = g_refs[k].at[:, pl.ds((1 - c) * hf, hf), :]
        cps.append(_remote(src, land_refs[k], send_sems.at[k], recv_sems.at[k], (x, y, 1 - c)))

    def send():
        for cp in cps:
            cp.start()

    def finish():
        for cp in cps:
            cp.wait()

    return send, finish


def _side_out_shapes(side):
    kind, arrays = side
    if kind == "gather":
        return _gather_out_shapes(arrays)
    if kind == "pair":
        return [jax.ShapeDtypeStruct((N_CHIPS, g.shape[1] // 2, g.shape[2]), g.dtype) for g in arrays]
    return [jax.ShapeDtypeStruct(p.shape, p.dtype) for p in arrays]


def _side_sems(side):
    kind, arrays = side
    n = len(arrays)
    if kind == "gather":
        return _gather_sems(n)
    if kind == "pair":
        return [pltpu.SemaphoreType.DMA((n,)), pltpu.SemaphoreType.DMA((n,))]
    return _chip_exchange_sems(n)


def _side_phases(side, in_refs, out_refs, sems):
    kind = side[0]
    if kind == "gather":
        return _gather_phases(in_refs, out_refs, *sems)
    send, finish = (_pair_exchange_phases if kind == "pair" else _chip_exchange_phases)(in_refs, out_refs, *sems)
    return send, None, finish


def _run_side(side, in_refs, out_refs, sems, step, n_steps):
    first, mid, last = _side_phases(side, in_refs, out_refs, sems)
    pl.when(step == 0)(first)
    if mid is not None:
        pl.when(step == n_steps // 2)(mid)
    return lambda: pl.when(step == n_steps - 1)(last)


def _pair_add(g, land, c_arr, *, name):
    _, a, b = g.shape
    hf = a // 2

    def body(c_ref, g_ref, l_ref, o_ref):
        o_ref[...] = (g_ref[...] + l_ref[...]).astype(BF16)

    return pl.pallas_call(
        body,
        grid_spec=pltpu.PrefetchScalarGridSpec(
            num_scalar_prefetch=1,
            grid=(N_CHIPS,),
            in_specs=[pl.BlockSpec((None, None, hf, b), lambda s, c_ref: (s, c_ref[0], 0, 0)),
                      pl.BlockSpec((None, hf, b), lambda s, c_ref: (s, 0, 0))],
            out_specs=pl.BlockSpec((None, hf, b), lambda s, c_ref: (s, 0, 0)),
        ),
        out_shape=jax.ShapeDtypeStruct((N_CHIPS, hf, b), BF16),
        compiler_params=pltpu.CompilerParams(dimension_semantics=("parallel",)),
        name=name,
    )(c_arr, g.reshape(N_CHIPS, 2, hf, b), land)


def _chip_exchange_sems(n):
    return [pltpu.SemaphoreType.DMA((3 * n,)), pltpu.SemaphoreType.DMA((3 * n,))]


def _chip_exchange_phases(p_refs, land_refs, send_sems, recv_sems):
    x, y, c = _place()
    me = 2 * x + y
    sends, recvs = [], []
    for k in range(len(p_refs)):
        for j, (cx, cy) in enumerate(_other_chips(x, y)):
            s = 3 * k + j
            sends.append(_remote(p_refs[k].at[2 * cx + cy], land_refs[k].at[me], send_sems.at[s], recv_sems.at[s], (cx, cy, c)))
            recvs.append(_remote(p_refs[k].at[me], land_refs[k].at[2 * cx + cy], send_sems.at[s], recv_sems.at[s], (cx, cy, c)))

    def send():
        for cp in sends:
            cp.start()

    def finish():
        for cp in recvs:
            cp.wait_recv()
        for cp in sends:
            cp.wait_send()

    return send, finish


def _chip_add(land, part, me_arr, *, name):
    _, r, b = land.shape

    def body(me_ref, p_ref, l1_ref, l2_ref, l3_ref, o_ref):
        o_ref[...] = ((p_ref[...].astype(F32) + l1_ref[...].astype(F32)) + l2_ref[...].astype(F32)) + l3_ref[...].astype(F32)

    tr = r // 2
    other = lambda j: pl.BlockSpec((None, tr, b), lambda i, me_ref: (jnp.bitwise_xor(me_ref[0], j), i, 0))
    return pl.pallas_call(
        body,
        grid_spec=pltpu.PrefetchScalarGridSpec(
            num_scalar_prefetch=1,
            grid=(r // tr,),
            in_specs=[pl.BlockSpec((None, tr, b), lambda i, me_ref: (me_ref[0], i, 0)), other(2), other(1), other(3)],
            out_specs=pl.BlockSpec((tr, b), lambda i, me_ref: (i, 0)),
        ),
        out_shape=jax.ShapeDtypeStruct((r, b), F32),
        compiler_params=pltpu.CompilerParams(dimension_semantics=("parallel",)),
        name=name,
    )(me_arr, part, land, land, land)


def _pair_share(halves):
    n = len(halves)

    def body(*refs):
        h_refs, out_refs = refs[:n], refs[n:2 * n]
        send_sems, recv_sems = refs[2 * n:]
        x, y, c = _place()
        cps = [_remote(h_refs[k], out_refs[k], send_sems.at[k], recv_sems.at[k], (x, y, 1 - c)) for k in range(n)]
        for cp in cps:
            cp.start()
        for cp in cps:
            cp.wait()

    return pl.pallas_call(
        body,
        in_specs=[ANY] * n,
        out_specs=[ANY] * n,
        out_shape=[jax.ShapeDtypeStruct(h.shape, F32) for h in halves],
        scratch_shapes=[pltpu.SemaphoreType.DMA((n,)), pltpu.SemaphoreType.DMA((n,))],
        name="pair_share",
    )(*halves)


def _all_sum_small(part):
    def body(p_ref, o_ref, slots, send_sems, recv_sems):
        x, y, c = _place()
        me = 4 * x + 2 * y + c
        slots[me] = p_ref[...]
        peers = [(x ^ dx, y ^ dy, c ^ dc) for dx in (0, 1) for dy in (0, 1) for dc in (0, 1)][1:]
        sends = [_remote(p_ref, slots.at[me], send_sems.at[k], recv_sems.at[k], peer) for k, peer in enumerate(peers)]
        for cp in sends:
            cp.start()
        for k, (px, py, pc) in enumerate(peers):
            _remote(p_ref, slots.at[4 * px + 2 * py + pc], send_sems.at[k], recv_sems.at[k], (px, py, pc)).wait_recv()
        for cp in sends:
            cp.wait_send()
        acc = slots[0]
        for d in range(1, 8):
            acc = acc + slots[d]
        o_ref[...] = acc

    vmem = pl.BlockSpec(memory_space=pltpu.VMEM)
    return pl.pallas_call(
        body,
        in_specs=[vmem],
        out_specs=vmem,
        out_shape=jax.ShapeDtypeStruct(part.shape, F32),
        scratch_shapes=[pltpu.VMEM((8,) + part.shape, F32), pltpu.SemaphoreType.DMA((7,)), pltpu.SemaphoreType.DMA((7,))],
        name="all_sum_small",
    )(part)


def _rope_tables():
    half = HEAD_DIM // 2
    inv_freq = np.float32(ROPE_THETA) ** (-np.arange(half, dtype=np.float32) * np.float32(2.0) / np.float32(HEAD_DIM))
    ang = np.arange(SEQ, dtype=np.float32)[:, None] * inv_freq[None, :].astype(np.float32)
    cos = np.tile(np.cos(ang).astype(np.float32), (1, 2 * BAND_HEADS))
    sin = np.tile(np.concatenate([-np.sin(ang), np.sin(ang)], axis=1).astype(np.float32), (1, BAND_HEADS))
    return jnp.asarray(cos), jnp.asarray(sin)


def _band_groups():
    out = []
    for d in DIL_D:
        rows = SEQ // d
        cls = max(1, 512 // rows) if d > 1 else 1
        out.append(dict(rows=rows, cls=cls, steps=d // cls))
    return out


def _local_step(x, mem, loss_target, g_pre_mix, g_post_mix, g_pre_ffn, g_post_ffn, g_mem, b_gate, w, comm=None):
    bl = x.shape[0]
    t = bl * SEQ
    chips = range(N_CHIPS)
    half_ff = D_FF // 2

    def with_gathered(w, names, gathered, shards):
        return {**w, **{name: lax.dynamic_update_slice(g, s[None], (comm["me"][0], 0, 0))
                        for name, g, s in zip(names, gathered, shards)}}

    x2 = x.reshape(t, D_MODEL)
    tgt2 = loss_target.reshape(t, D_MODEL)
    mem2 = mem.reshape(bl * MEM_LEN, D_MODEL)

    h = _norm_fwd(x2, g_pre_mix, name="norm_x", side=("gather", comm["first_shards"]) if comm else None)
    if comm:
        w = with_gathered(w, comm["first_names"], h[1], comm["first_shards"])
        h = h[0]
    w_in_full = _join_shards(w["w_in"])
    proj = _mm([(h, w_in_full)], nt=False, tn=2176, out_dtypes=[BF16], name="proj",
               side=("gather", comm["mid_shards"]) if comm else None)
    if comm:
        w = with_gathered(w, comm["mid_names"], proj[1], comm["mid_shards"])
        proj = proj[0]
    w_mem_kv_full = w["w_mem_kv"].reshape(D_MODEL, 2 * MEM_W)
    gates = _mm([(h, w["w_gate"], None, "j")], nt=False,tn=w["w_gate"].shape[2], out_dtypes=[BF16], name="gates",
                bias=b_gate, epilogue=lambda acc: (_sigmoid(acc),))
    hm = _norm_fwd(mem2, g_mem, name="norm_mem")
    kv_m = _mm([(hm, w_mem_kv_full)], nt=False,tn=1024, out_dtypes=[BF16], name="mem_kv")
    proj3 = proj.reshape(bl, SEQ, D_IN)
    kv3 = kv_m.reshape(bl, MEM_LEN, 2 * MEM_W)

    o_a, o_a32, sb_weights, late_gathered = _sb_fwd(proj3, comm["late_shards"] if comm else [])
    if comm:
        w = with_gathered(w, comm["late_names"], late_gathered, comm["late_shards"])
    w_o_full = w["w_o"].reshape(D_MODEL, D_MODEL)
    w_ffn_out_full = w["w_ffn_out"].reshape(D_FF, D_MODEL)

    cos_t, sin_t = _rope_tables()
    dil0 = 3 * SB_W

    grp_w = 3 * DIL_W
    band = []
    for g, (d, cfg) in enumerate(zip(DIL_D, _band_groups())):
        a_g = proj3 if d == 1 else proj3[:, :, dil0 + g * grp_w:dil0 + (g + 1) * grp_w].reshape(bl, SEQ // d, d * grp_w)
        band.append(dict(cfg, a=a_g, col0=dil0 // grp_w if d == 1 else None, cos=cos_t.reshape(SEQ // d, d * DIL_W),
                         sin=sin_t.reshape(SEQ // d, d * DIL_W)))
    outs = [_band_group_fwd(b["a"], b["cos"], b["sin"], rows=b["rows"], cls=b["cls"], steps=b["steps"], col0=b["col0"],
                            name=f"band_fwd_{g}") for g, b in enumerate(band)]
    o_b, lse_b = _band_merge3([(o.reshape(t, DIL_W), l.reshape(t, LANES)) for o, l in outs])

    o_c = _mem_fwd(proj3, kv3)

    o_a2, o_c2 = o_a.reshape(t, SB_W), o_c.reshape(t, MEM_W)
    y_a, y_b, y_c, merged = _branch_merge_fwd(o_a2, o_b, o_c2, w["w_br_sb"], w["w_br_dil"], w["w_br_mem"], gates)
    mix = _mm([(merged, w_o_full)], nt=False,tn=1024, out_dtypes=[F32], name="mix")
    x1, h2 = _mid_fwd(mix, x2, g_post_mix, g_pre_ffn)
    gg, uu, f = _ffn_in_fwd(h2, w["w_ffn_in"])
    f2 = _mm([(f, w_ffn_out_full)], nt=False,tn=1024, out_dtypes=[F32], name="ffn_out")

    dy, df2, dg_post_ffn, loss_row = _loss_bwd(f2, x1, g_post_ffn, tgt2)

    dg_ffn, du_ffn = _mm([(df2, w_ffn_out_full)], nt=True,tn=half_ff, out_dtypes=[BF16, BF16], name="d_ffn_act",
                         extras=(gg, uu), epilogue=_swiglu_bwd_epilogue)
    gw = {}
    gw["w_ffn_out"] = _mm_tn(f, df2, tm=half_ff, tn=1024, name="gw_ffn_out").reshape(N_CHIPS, D_FF // N_CHIPS, D_MODEL)
    gw_ffn_g = _mm_tn(h2, dg_ffn, tm=1024, tn=half_ff, name="gw_ffn_gate", out_shards=True, slots=(N_CHIPS, 0), group=2)
    gw["w_ffn_in"] = _mm_tn(h2, du_ffn, tm=1024, tn=half_ff, name="gw_ffn_up", out_shards=True, slots=(N_CHIPS, 2), into=gw_ffn_g,
                            group=2)
    dh2 = _mm([(dg_ffn, w["w_ffn_in"], 0, 0), (dg_ffn, w["w_ffn_in"], 1, 1), (du_ffn, w["w_ffn_in"], 0, 2),
               (du_ffn, w["w_ffn_in"], 1, 3)], nt=True, tn=1024, out_dtypes=[BF16], name="d_h2")
    dx1, dmix, dg_pre_ffn, dg_post_mix = _mid_bwd(dh2, x1, mix, g_pre_ffn, g_post_mix, dy)

    gw["w_o"] = _mm_tn(merged, dmix, tm=1024, tn=1024, name="gw_o").reshape(N_CHIPS, D_MODEL // N_CHIPS, D_MODEL)
    dmerged = _mm([(dmix, w_o_full)], nt=True, tn=1024, out_dtypes=[BF16], name="d_merged")
    dy_a, dy_b, dy_c, dgpre, db_gate = _gate_bwd(dmerged, gates, y_a, y_b, y_c)
    br_cols = D_MODEL // N_CHIPS
    gw["w_br_sb"] = _mm_tn(o_a2, dy_a, tm=512, tn=br_cols, name="gw_br_sb", out_shards=True, group=N_CHIPS)
    gw["w_br_dil"] = _mm_tn(o_b, dy_b, tm=256, tn=br_cols, name="gw_br_dil", out_shards=True, group=N_CHIPS)
    gw["w_br_mem"] = _mm_tn(o_c2, dy_c, tm=512, tn=br_cols, name="gw_br_mem", out_shards=True, group=N_CHIPS)
    gw["w_gate"] = _mm_tn(h, dgpre, tm=1024, tn=w["w_gate"].shape[2], name="gw_gate", out_shards=True, group=2)
    do_a = _mm([(dy_a, w["w_br_sb"], s, s) for s in chips], nt=True,tn=SB_W, out_dtypes=[BF16], name="d_o_a")
    do_b = _mm([(dy_b, w["w_br_dil"], s, s) for s in chips], nt=True,tn=DIL_W, out_dtypes=[BF16], name="d_o_b")
    do_c = _mm([(dy_c, w["w_br_mem"], s, s) for s in chips], nt=True,tn=MEM_W, out_dtypes=[BF16], name="d_o_c")

    dq_c, dkv_m = _mem_bwd(proj3, kv3, do_c.reshape(bl, SEQ, MEM_W))
    dkv_m = dkv_m.reshape(bl * MEM_LEN, 2 * MEM_W).astype(BF16)
    gw["w_mem_kv"] = _mm_tn(hm, dkv_m, tm=1024, tn=1024, name="gw_mem_kv").reshape(N_CHIPS, D_MODEL // N_CHIPS, 2 * MEM_W)
    dhm = _mm([(dkv_m, w_mem_kv_full)], nt=True,tn=1024, out_dtypes=[F32], name="d_hm")
    dg_mem = _mem_norm_bwd(dhm, mem2, g_mem)

    stats = _band_delta(do_b, o_b, lse_b)
    early = [name for name, _, _ in PACK if name != "w_in"] if comm else []
    grads = [gw[name] for name in early]
    d_dil = []
    for g, (d, b) in enumerate(zip(DIL_D, band)):
        out = _band_group_bwd(b["a"], do_b.reshape(bl, SEQ // d, d * DIL_W), stats.reshape(bl, SEQ // d, d * LANES),
                              b["cos"], b["sin"], rows=b["rows"], cls=b["cls"], steps=b["steps"], col0=b["col0"],
                              name=f"band_bwd_{g}", side=("pair", grads) if comm and g == 0 else None)
        if comm and g == 0:
            out, lands = out
        d_dil.append(out.reshape(bl, SEQ, grp_w))

    parts = [_pair_add(g, l, comm["c"], name="pair_add_" + name) for name, g, l in zip(early, grads, lands)] if comm else []
    dq_a, dk_a, dv_a, lands = _sb_bwd(proj3, o_a32, do_a.reshape(bl, SEQ, SB_W), sb_weights, parts)
    reduced = {name: (p, l) for name, p, l in zip(early, parts, lands)}

    in_cols = D_IN // N_CHIPS
    dproj_s = _split_to_shards([a.reshape(t, a.shape[-1]) for a in [dq_a, dk_a, dv_a] + d_dil + [dq_c]], name="dproj_shards")
    gw["w_in"] = _mm_tn(h, dproj_s, tm=1024, tn=in_cols, name="gw_in", group=2)
    if comm:
        land = _pair_exchange([gw["w_in"]], name="pair_exchange_w_in")[0]
        part_in = _pair_add(gw["w_in"], land, comm["c"], name="pair_add_w_in")
    dh = _mm([(dproj_s, w["w_in"], s, s) for s in chips] + [(dgpre, w["w_gate"], s, s) for s in chips],
             nt=True, tn=1024, out_dtypes=[BF16], name="d_h", side=("chip", [part_in]) if comm else None)
    if comm:
        dh, (land_in,) = dh
        reduced["w_in"] = (part_in, land_in)
    grad_x, dg_pre_mix = _first_bwd(dh, x2, g_pre_mix, dx1)
    small = jnp.concatenate([dg_pre_mix, dg_post_mix, dg_pre_ffn, dg_post_ffn, dg_mem, db_gate.reshape(3, D_MODEL)], axis=0)
    return loss_row[0, 0], grad_x.reshape(bl, SEQ, D_MODEL), gw, small, reduced


def kernel(x, mem, g_pre_mix, g_post_mix, g_pre_ffn, g_post_ffn, g_mem, w_in, w_mem_kv, w_br_sb, w_br_dil, w_br_mem, w_gate, b_gate, w_o, w_ffn_in, w_ffn_out, loss_target, m_g_pre_mix, m_g_post_mix, m_g_pre_ffn, m_g_post_ffn, m_g_mem, m_w_in, m_w_mem_kv, m_w_br_sb, m_w_br_dil, m_w_br_mem, m_w_gate, m_b_gate, m_w_o, m_w_ffn_in, m_w_ffn_out, v_g_pre_mix, v_g_post_mix, v_g_pre_ffn, v_g_post_ffn, v_g_mem, v_w_in, v_w_mem_kv, v_w_br_sb, v_w_br_dil, v_w_br_mem, v_w_gate, v_b_gate, v_w_o, v_w_ffn_in, v_w_ffn_out):
    w_shards = dict(w_in=w_in[0], w_mem_kv=w_mem_kv[0], w_br_sb=w_br_sb[0], w_br_dil=w_br_dil[0], w_br_mem=w_br_mem[0],
                    w_gate=w_gate[0], w_o=w_o[0], w_ffn_in=w_ffn_in[0], w_ffn_out=w_ffn_out[0])
    m_shards = dict(w_in=m_w_in[0], w_mem_kv=m_w_mem_kv[0], w_br_sb=m_w_br_sb[0], w_br_dil=m_w_br_dil[0], w_br_mem=m_w_br_mem[0],
                    w_gate=m_w_gate[0], w_o=m_w_o[0], w_ffn_in=m_w_ffn_in[0], w_ffn_out=m_w_ffn_out[0])
    v_shards = dict(w_in=v_w_in[0], w_mem_kv=v_w_mem_kv[0], w_br_sb=v_w_br_sb[0], w_br_dil=v_w_br_dil[0], w_br_mem=v_w_br_mem[0],
                    w_gate=v_w_gate[0], w_o=v_w_o[0], w_ffn_in=v_w_ffn_in[0], w_ffn_out=v_w_ffn_out[0])

    names = [name for name, _, _ in PACK]
    c_arr = lax.axis_index("c").astype(jnp.int32).reshape(1)
    me_arr = (2 * lax.axis_index("x") + lax.axis_index("y")).astype(jnp.int32).reshape(1)
    mid_names = ["w_gate", "w_mem_kv"]
    late_names = [name for name in names if name not in ["w_in"] + mid_names]
    bf = {name: w_shards[name].astype(BF16) for name in names}
    comm = dict(c=c_arr, me=me_arr, first_names=["w_in"], first_shards=[bf["w_in"]],
                mid_names=mid_names, mid_shards=[bf[name] for name in mid_names],
                late_names=late_names, late_shards=[bf[name] for name in late_names])

    loss_local, grad_x, gw, small, reduced = _local_step(x, mem, loss_target, g_pre_mix, g_post_mix, g_pre_ffn, g_post_ffn,
                                                         g_mem, b_gate, {}, comm)

    halves = [_chip_add(reduced[name][1], reduced[name][0], me_arr, name="chip_add_" + name) for name in names]
    theirs = _pair_share(halves)
    extra = jnp.concatenate([jnp.full((1, D_MODEL), loss_local, F32), jnp.zeros((7, D_MODEL), F32)], axis=0)
    summed = _all_sum_small(jnp.concatenate([small, extra], axis=0))
    small, loss = summed[:8], summed[8, 0]

    upd = {}
    for name, mine, other in zip(names, halves, theirs):
        upd[name] = _adamw_halves(w_shards[name], mine, other, m_shards[name], v_shards[name], c_arr, name="adamw_" + name)
    g_shards = {name: u[0] for name, u in upd.items()}

    def small8(gs, b):
        return jnp.concatenate(gs + [b.reshape(3, D_MODEL)], axis=0)

    sw = small8([g_pre_mix, g_post_mix, g_pre_ffn, g_post_ffn, g_mem], b_gate)
    sm = small8([m_g_pre_mix, m_g_post_mix, m_g_pre_ffn, m_g_post_ffn, m_g_mem], m_b_gate)
    sv = small8([v_g_pre_mix, v_g_post_mix, v_g_pre_ffn, v_g_post_ffn, v_g_mem], v_b_gate)
    s_upd = _adamw(sw, small, sm, sv, tm=8, name="adamw_small")

    def small_out(a):
        return [a[0:1], a[1:2], a[2:3], a[3:4], a[4:5]]

    order = ["w_in", "w_mem_kv", "w_br_sb", "w_br_dil", "w_br_mem", "w_gate", "b_gate", "w_o", "w_ffn_in", "w_ffn_out"]

    def leaves(small_arr, big):
        out = small_out(small_arr)
        for name in order:
            out.append(small_arr[5:8].reshape(1, 3 * D_MODEL) if name == "b_gate" else big[name][None])
        return out

    grads_out = leaves(small, g_shards)
    delta_out = leaves(s_upd[0], {n: u[1] for n, u in upd.items()})
    m_out = leaves(s_upd[1], {n: u[2] for n, u in upd.items()})
    v_out = leaves(s_upd[2], {n: u[3] for n, u in upd.items()})
    return (loss, grad_x, *grads_out, *delta_out, *m_out, *v_out)
```

```python
import jax
import jax.numpy as jnp
import numpy as np
from jax import lax
from jax.experimental import pallas as pl
from jax.experimental.pallas import tpu as pltpu

F32 = jnp.float32
BF16 = jnp.bfloat16
MESH = pl.DeviceIdType.MESH

D_MODEL = 1024
SEQ = 2048
HEAD_DIM = 64
SB_W = 512
DIL_W = 256
MEM_W = 512
MEM_LEN = 256
D_IN = 3 * SB_W + 9 * DIL_W + MEM_W
D_FF = 2816
DIL_D = (1, 4, 16)
ROPE_THETA = 10000.0
NORM_EPS = 1e-6
NEG_INF = -1e30
LANES = 128

ADAM_LR = 0.001
ADAM_B1 = 0.9
ADAM_B2 = 0.999
ADAM_EPS = 1e-08
ADAM_WD = 0.01
ADAM_STEP = 10

N_CHIPS = 4
PACK = (
    ("w_in", (1024, 1088), 1),
    ("w_mem_kv", (256, 1024), 0),
    ("w_br_sb", (512, 256), 1),
    ("w_br_dil", (256, 256), 1),
    ("w_br_mem", (512, 256), 1),
    ("w_gate", (1024, 768), 1),
    ("w_o", (256, 1024), 0),
    ("w_ffn_in", (1024, 1408), 1),
    ("w_ffn_out", (704, 1024), 0),
)


def _dot(a, b):
    return lax.dot_general(a, b, (((1,), (0,)), ((), ())), preferred_element_type=F32)


def _dot_nt(a, b):
    return lax.dot_general(a, b, (((1,), (1,)), ((), ())), preferred_element_type=F32)


def _dot_tn(a, b):
    return lax.dot_general(a, b, (((0,), (0,)), ((), ())), preferred_element_type=F32)


def _split_dot(x, u):
    hi = x.astype(BF16)
    lo = (x - hi.astype(F32)).astype(BF16)
    return _dot(hi, u) + _dot(lo, u)


V7X_VMEM_BUDGET = 44 * 2 ** 20


def _rows_that_fit(m, row_bytes, fixed_bytes):
    for tm in (1024, 512, 256, 128):
        if m % tm == 0 and fixed_bytes + tm * row_bytes <= V7X_VMEM_BUDGET:
            return tm
    return min(m, 128)


def _mm(pairs, *, nt, tn, out_dtypes, name, bias=None, extras=(), epilogue=None, side=None):
    pairs = [p if len(p) == 4 else (p[0], p[1], None, None) for p in pairs]
    m = pairs[0][0].shape[-2]
    b0 = pairs[0][1]
    if nt:
        n = b0.shape[-2]
    else:
        n = b0.shape[-1] * (b0.shape[0] if b0.ndim == 3 else 1)
    n_pairs, n_extra, n_out = len(pairs), len(extras), len(out_dtypes)
    assert n % tn == 0
    one_col = n == tn
    ks = [(b.shape[-1] if nt else b.shape[-2]) for _, b, _, _ in pairs]
    fixed = sum(k * tn * 2 for k in ks) * (1 if one_col else 2)
    row_bytes = 2 * sum(k * 2 for k in ks) + 2 * tn * (sum(jnp.dtype(dt).itemsize for dt in out_dtypes) + 2 * n_extra) + 2 * tn * 4
    tm = _rows_that_fit(m, row_bytes, fixed)
    assert m % tm == 0
    b_mode = dict(pipeline_mode=pl.Buffered(1)) if one_col else {}
    has_bias = bias is not None
    side_arrays = side[1] if side else []
    n_side = len(side_arrays)
    n_main_in = 2 * n_pairs + has_bias + n_extra
    n_steps = (n // tn) * (m // tm)

    def body(*refs):
        if n_side:
            step = pl.program_id(0) * (m // tm) + pl.program_id(1)
            finish = _run_side(side, refs[n_main_in:n_main_in + n_side],
                               refs[n_main_in + n_side + n_out:n_main_in + 2 * n_side + n_out],
                               refs[n_main_in + 2 * n_side + n_out:], step, n_steps)
        outs = refs[n_main_in + n_side:n_main_in + n_side + n_out]
        acc = None
        for i in range(n_pairs):
            a, b = refs[2 * i][...], refs[2 * i + 1][...]
            p = _dot_nt(a, b) if nt else _dot(a, b)
            acc = p if acc is None else acc + p
        pos = 2 * n_pairs
        if has_bias:
            acc = acc + refs[pos][...]
            pos += 1
        ex = [r[...] for r in refs[pos:pos + n_extra]]
        vals = (acc,) if epilogue is None else epilogue(acc, *ex)
        for r, v, dt in zip(outs, vals, out_dtypes):
            r[...] = v.astype(dt)
        if n_side:
            finish()

    in_specs, args = [], []
    for a, b, a_col, b_sel in pairs:
        k = b.shape[-1] if nt else b.shape[-2]
        assert a_col is not None or a.shape[1] == k
        if a.ndim == 3:
            in_specs.append(pl.BlockSpec((None, tm, k), lambda j, i, c=a_col: (c, i, 0)))
        else:
            in_specs.append(pl.BlockSpec((tm, k), lambda j, i, c=a_col or 0: (i, c)))
        if b.ndim == 2:
            in_specs.append(pl.BlockSpec((tn, k), lambda j, i: (j, 0), **b_mode) if nt
                            else pl.BlockSpec((k, tn), lambda j, i: (0, j), **b_mode))
        elif nt:
            in_specs.append(pl.BlockSpec((None, tn, k), lambda j, i, s=b_sel: (s, j, 0), **b_mode))
        else:
            assert b_sel == "j" and b.shape[-1] == tn
            in_specs.append(pl.BlockSpec((None, k, tn), lambda j, i: (j, 0, 0), **b_mode))
        args += [a, b]
    if has_bias:
        in_specs.append(pl.BlockSpec((1, tn), lambda j, i: (0, j)))
        args.append(bias)
    for e in extras:
        in_specs.append(pl.BlockSpec((tm, tn), lambda j, i: (i, j)))
        args.append(e)
    out = pl.pallas_call(
        body,
        grid=(n // tn, m // tm),
        in_specs=in_specs + [ANY] * n_side,
        out_specs=[pl.BlockSpec((tm, tn), lambda j, i: (i, j)) for _ in range(n_out)] + [ANY] * n_side,
        out_shape=[jax.ShapeDtypeStruct((m, n), dt) for dt in out_dtypes] + (_side_out_shapes(side) if n_side else []),
        scratch_shapes=_side_sems(side) if n_side else [],
        compiler_params=pltpu.CompilerParams(dimension_semantics=("arbitrary", "arbitrary") if n_side else ("parallel", "parallel")),
        name=name,
    )(*args, *side_arrays)
    if n_side:
        return (out[0] if n_out == 1 else out[:n_out]), out[n_out:]
    return out[0] if n_out == 1 else out


def _mm_tn(a, b, *, tm, tn, name, out_shards=False, slots=None, into=None, group=1):
    k, m = a.shape
    b_shards = b.ndim == 3
    out_shards = out_shards or b_shards
    n = b.shape[0] * b.shape[2] if b_shards else b.shape[1]
    tk = _rows_that_fit(k, 2 * 2 * (tm + group * tn), 3 * tm * group * tn * 4)
    assert m % tm == 0 and n % (group * tn) == 0 and k % tk == 0 and (not b_shards or b.shape[2] == tn)
    assert group == 1 or out_shards
    total, first = slots if slots else (n // tn, 0)

    def body(a_ref, b_ref, *rest):
        o_ref = rest[-1]

        @pl.when(pl.program_id(2) == 0)
        def _():
            o_ref[...] = jnp.zeros_like(o_ref)

        if group == 1:
            o_ref[...] += _dot_tn(a_ref[...], b_ref[...])
        elif b_shards:
            a_blk = a_ref[...]
            for s in range(group):
                o_ref[s] += _dot_tn(a_blk, b_ref[s])
        else:
            acc = _dot_tn(a_ref[...], b_ref[...])
            for s in range(group):
                o_ref[s] += acc[:, s * tn:(s + 1) * tn]

    lead = None if group == 1 else group
    if b_shards:
        b_spec = pl.BlockSpec((lead, tk, tn), lambda i, j, kk: (j, kk, 0))
    else:
        b_spec = pl.BlockSpec((tk, group * tn), lambda i, j, kk: (kk, j))
    if out_shards:
        out_spec = pl.BlockSpec((lead, tm, tn), lambda i, j, kk: (j + first // group, i, 0))
        out_shape = jax.ShapeDtypeStruct((total, m, tn), F32)
    else:
        out_spec = pl.BlockSpec((tm, tn), lambda i, j, kk: (i, j))
        out_shape = jax.ShapeDtypeStruct((m, n), F32)
    return pl.pallas_call(
        body,
        grid=(m // tm, n // (group * tn), k // tk),
        in_specs=[pl.BlockSpec((tk, tm), lambda i, j, kk: (kk, i)), b_spec] + ([ANY] if into is not None else []),
        out_specs=out_spec,
        out_shape=out_shape,
        input_output_aliases={2: 0} if into is not None else {},
        compiler_params=pltpu.CompilerParams(dimension_semantics=("parallel", "parallel", "arbitrary")),
        name=name,
    )(*([a, b] + ([into] if into is not None else [])))


def _rowwise(fn, ins, outs, *, tm, name, side=None):
    rows = next(a.shape[0] for a, kind in ins if kind == "row")
    tm = min(tm, rows)
    assert rows % tm == 0
    n_in, n_out = len(ins), len(outs)
    side_arrays = side[1] if side else []
    n_side = len(side_arrays)

    def body(*refs):
        if n_side:
            finish = _run_side(side, refs[n_in:n_in + n_side], refs[n_in + n_side + n_out:n_in + 2 * n_side + n_out],
                               refs[n_in + 2 * n_side + n_out:], pl.program_id(0), rows // tm)
        vals = fn(*[r[...] for r in refs[:n_in]])
        for (_, dt, kind), r, v in zip(outs, refs[n_in + n_side:n_in + n_side + n_out], vals):
            if kind == "row":
                r[...] = v.astype(dt)
            else:
                @pl.when(pl.program_id(0) == 0)
                def _(r=r):
                    r[...] = jnp.zeros_like(r)

                r[...] += v
        if n_side:
            finish()

    in_specs = [pl.BlockSpec((tm, a.shape[1]), lambda i: (i, 0)) if kind == "row" else pl.BlockSpec(a.shape, lambda i: (0, 0))
                for a, kind in ins]
    out_specs = [pl.BlockSpec((tm, c), lambda i: (i, 0)) if kind == "row" else pl.BlockSpec((1, c), lambda i: (0, 0))
                 for c, _, kind in outs]
    out_shape = [jax.ShapeDtypeStruct((rows if kind == "row" else 1, c), dt) for c, dt, kind in outs]
    ordered = n_side or any(kind == "acc" for _, _, kind in outs)
    return pl.pallas_call(
        body,
        grid=(rows // tm,),
        in_specs=in_specs + [ANY] * n_side,
        out_specs=out_specs + [ANY] * n_side,
        out_shape=out_shape + (_side_out_shapes(side) if n_side else []),
        scratch_shapes=_side_sems(side) if n_side else [],
        compiler_params=pltpu.CompilerParams(dimension_semantics=("arbitrary" if ordered else "parallel",)),
        name=name,
    )(*[a for a, _ in ins], *side_arrays)


def _rstd(x):
    return lax.rsqrt(jnp.mean(x * x, axis=-1, keepdims=True) + NORM_EPS)


def _norm_bwd(dout, xin, g):
    r = _rstd(xin)
    n = xin * r
    dn = dout * g
    dg = jnp.sum(dout * n, axis=0, keepdims=True)
    dx = r * (dn - n * jnp.mean(dn * n, axis=-1, keepdims=True))
    return dx, dg


def _sigmoid(x):
    return 0.5 * jnp.tanh(0.5 * x) + 0.5


def _norm_fwd(x, g, *, name, side=None):
    def fn(x, g):
        return ((x * _rstd(x)) * g,)

    out = _rowwise(fn, [(x, "row"), (g, "vec")], [(D_MODEL, BF16, "row")], tm=512, name=name, side=side)
    return (out[0], out[1:]) if side else out[0]


def _mid_fwd(mix, x, g_post_mix, g_pre_ffn):
    def fn(mix, x, g2, g3):
        x1 = x + (mix * _rstd(mix)) * g2
        return x1, (x1 * _rstd(x1)) * g3

    return _rowwise(fn, [(mix, "row"), (x, "row"), (g_post_mix, "vec"), (g_pre_ffn, "vec")],
                    [(D_MODEL, F32, "row"), (D_MODEL, BF16, "row")], tm=512, name="mid_fwd")


def _loss_bwd(f2, x1, g_post_ffn, tgt):
    def fn(f2, x1, g4, tgt):
        r = _rstd(f2)
        n = f2 * r
        err = x1 + n * g4 - tgt
        loss = 0.5 * jnp.sum(jnp.mean(err * err, axis=-1, keepdims=True), axis=0, keepdims=True)
        dy = err * (1.0 / D_MODEL)
        dn = dy * g4
        dg4 = jnp.sum(dy * n, axis=0, keepdims=True)
        df2 = r * (dn - n * jnp.mean(dn * n, axis=-1, keepdims=True))
        return dy, df2, dg4, jnp.broadcast_to(loss, (1, LANES))

    return _rowwise(fn, [(f2, "row"), (x1, "row"), (g_post_ffn, "vec"), (tgt, "row")],
                    [(D_MODEL, BF16, "row"), (D_MODEL, BF16, "row"), (D_MODEL, F32, "acc"), (LANES, F32, "acc")],
                    tm=512, name="loss_bwd")


def _mid_bwd(dh2, x1, mix, g_pre_ffn, g_post_mix, dy):
    def fn(dh2, x1, mix, g3, g2, dy):
        d3, dg3 = _norm_bwd(dh2.astype(F32), x1, g3)
        dx1 = dy.astype(F32) + d3
        dmix, dg2 = _norm_bwd(dx1, mix, g2)
        return dx1, dmix, dg3, dg2

    return _rowwise(fn, [(dh2, "row"), (x1, "row"), (mix, "row"), (g_pre_ffn, "vec"), (g_post_mix, "vec"), (dy, "row")],
                    [(D_MODEL, BF16, "row"), (D_MODEL, BF16, "row"), (D_MODEL, F32, "acc"), (D_MODEL, F32, "acc")],
                    tm=512, name="mid_bwd")


def _first_bwd(dh, x, g_pre_mix, dx1):
    def fn(dh, x, g1, dx1):
        d1, dg1 = _norm_bwd(dh.astype(F32), x, g1)
        return dx1.astype(F32) + d1, dg1

    return _rowwise(fn, [(dh, "row"), (x, "row"), (g_pre_mix, "vec"), (dx1, "row")],
                    [(D_MODEL, F32, "row"), (D_MODEL, F32, "acc")], tm=512, name="first_bwd")


def _mem_norm_bwd(dhm, mem, g_mem):
    def fn(dhm, mem, g):
        return (jnp.sum(dhm * (mem * _rstd(mem)), axis=0, keepdims=True),)

    return _rowwise(fn, [(dhm, "row"), (mem, "row"), (g_mem, "vec")], [(D_MODEL, F32, "acc")], tm=512, name="mem_norm_bwd")[0]


def _gate_bwd(dmerged, gates, ya, yb, yc):
    def fn(dm, gt, ya, yb, yc):
        dm = dm.astype(F32)
        gt = gt.astype(F32)
        outs, dgp = [], []
        for i, y in enumerate((ya, yb, yc)):
            gi = gt[:, i * D_MODEL:(i + 1) * D_MODEL]
            outs.append(dm * gi)
            dgp.append(dm * y.astype(F32) * gi * (1.0 - gi))
        dgpre = jnp.concatenate(dgp, axis=1)
        return outs[0], outs[1], outs[2], dgpre, jnp.sum(dgpre, axis=0, keepdims=True)

    return _rowwise(fn, [(dmerged, "row"), (gates, "row"), (ya, "row"), (yb, "row"), (yc, "row")],
                    [(D_MODEL, BF16, "row")] * 3 + [(3 * D_MODEL, BF16, "row"), (3 * D_MODEL, F32, "acc")],
                    tm=512, name="gate_bwd")


def _adamw_math(w, g, m, v):
    m = ADAM_B1 * m + (1.0 - ADAM_B1) * g
    v = ADAM_B2 * v + (1.0 - ADAM_B2) * (g * g)
    m_hat = m / (1.0 - ADAM_B1 ** ADAM_STEP)
    v_hat = v / (1.0 - ADAM_B2 ** ADAM_STEP)
    delta = -ADAM_LR * (m_hat / (jnp.sqrt(v_hat) + ADAM_EPS) + ADAM_WD * w)
    return delta, m, v


def _adamw(w, g, m, v, *, tm, name):
    c = w.shape[1]
    return _rowwise(_adamw_math, [(w, "row"), (g, "row"), (m, "row"), (v, "row")], [(c, F32, "row")] * 3, tm=tm, name=name)


def _adamw_halves(w, g_mine, g_theirs, m, v, c_arr, *, name):
    a, b = w.shape
    hf = a // 2
    tr = next(t for t in (hf, hf // 2, hf // 4) if 9 * 3 * t * b * 4 <= V7X_VMEM_BUDGET)

    def body(c_ref, w_ref, gm_ref, gt_ref, m_ref, v_ref, g_out, d_out, m_out, v_out):
        g = jnp.where(pl.program_id(0) == c_ref[0], gm_ref[...], gt_ref[...])
        d, m_new, v_new = _adamw_math(w_ref[...], g, m_ref[...], v_ref[...])
        g_out[...] = g
        d_out[...] = d
        m_out[...] = m_new
        v_out[...] = v_new

    full = pl.BlockSpec((tr, b), lambda hh, i, c_ref: (hh * (hf // tr) + i, 0))
    half = pl.BlockSpec((tr, b), lambda hh, i, c_ref: (i, 0))
    return pl.pallas_call(
        body,
        grid_spec=pltpu.PrefetchScalarGridSpec(
            num_scalar_prefetch=1,
            grid=(2, hf // tr),
            in_specs=[full, half, half, full, full],
            out_specs=[full] * 4,
        ),
        out_shape=[jax.ShapeDtypeStruct((a, b), F32)] * 4,
        compiler_params=pltpu.CompilerParams(dimension_semantics=("parallel", "parallel")),
        name=name,
    )(c_arr, w, g_mine, g_theirs, m, v)


def _ffn_in_fwd(h2, w_ffn):
    m, tm, tn = h2.shape[0], 512, w_ffn.shape[2]
    assert 2 * tn == D_FF

    def body(h_ref, wg_ref, wu_ref, g_ref, u_ref, f_ref):
        h = h_ref[...]
        g = _dot(h, wg_ref[...])
        u = _dot(h, wu_ref[...])
        g_ref[...] = g.astype(BF16)
        u_ref[...] = u.astype(BF16)
        f_ref[...] = (g * _sigmoid(g) * u).astype(BF16)

    o_spec = pl.BlockSpec((tm, tn), lambda j, i: (i, j))
    return pl.pallas_call(
        body,
        grid=(D_FF // tn, m // tm),
        in_specs=[pl.BlockSpec((tm, D_MODEL), lambda j, i: (i, 0)),
                  pl.BlockSpec((None, D_MODEL, tn), lambda j, i: (j, 0, 0)),
                  pl.BlockSpec((None, D_MODEL, tn), lambda j, i: (j + 2, 0, 0))],
        out_specs=[o_spec, o_spec, o_spec],
        out_shape=[jax.ShapeDtypeStruct((m, D_FF), BF16)] * 3,
        compiler_params=pltpu.CompilerParams(dimension_semantics=("parallel", "parallel")),
        name="ffn_in_fwd",
    )(h2, w_ffn, w_ffn)


def _join_shards(w4):
    _, rows, cols = w4.shape
    tr = rows // 4

    def body(w_ref, o_ref):
        for s in range(N_CHIPS):
            o_ref[:, s * cols:(s + 1) * cols] = w_ref[s]

    return pl.pallas_call(
        body,
        grid=(rows // tr,),
        in_specs=[pl.BlockSpec((N_CHIPS, tr, cols), lambda i: (0, i, 0))],
        out_specs=pl.BlockSpec((tr, N_CHIPS * cols), lambda i: (i, 0)),
        out_shape=jax.ShapeDtypeStruct((rows, N_CHIPS * cols), w4.dtype),
        compiler_params=pltpu.CompilerParams(dimension_semantics=("parallel",)),
        name="join_shards",
    )(w4)


def _split_to_shards(pieces, *, name):
    t = pieces[0].shape[0]
    widths = [p.shape[1] for p in pieces]
    cols = sum(widths) // N_CHIPS
    tm = 512
    plan, start = [], 0
    for p, wd in enumerate(widths):
        for s in range(N_CHIPS):
            lo, hi = max(start, s * cols), min(start + wd, (s + 1) * cols)
            if lo < hi:
                plan.append((s, p, lo - s * cols, hi - s * cols, lo - start, hi - start))
        start += wd

    def body(*refs):
        o_ref = refs[-1]
        for s, p, o_lo, o_hi, p_lo, p_hi in plan:
            o_ref[s, :, o_lo:o_hi] = refs[p][:, p_lo:p_hi]

    return pl.pallas_call(
        body,
        grid=(t // tm,),
        in_specs=[pl.BlockSpec((tm, wd), lambda i: (i, 0)) for wd in widths],
        out_specs=pl.BlockSpec((N_CHIPS, tm, cols), lambda i: (0, i, 0)),
        out_shape=jax.ShapeDtypeStruct((N_CHIPS, t, cols), pieces[0].dtype),
        compiler_params=pltpu.CompilerParams(dimension_semantics=("parallel",)),
        name=name,
    )(*pieces)


def _swiglu_bwd_epilogue(df, g, u):
    g = g.astype(F32)
    u = u.astype(F32)
    sg = _sigmoid(g)
    return df * u * (sg * (1.0 + g * (1.0 - sg))), df * (g * sg)


def _branch_merge_fwd(o_a, o_b, o_c, w_sb, w_dil, w_mem, gates):
    m, tm = o_a.shape[0], 512

    def body(oa_ref, ob_ref, oc_ref, wa_ref, wb_ref, wc_ref, gt_ref, ya_ref, yb_ref, yc_ref, mg_ref):
        def project(o_ref, w_ref):
            o = o_ref[...]
            return jnp.concatenate([_dot(o, w_ref[s]) for s in range(N_CHIPS)], axis=1)

        ya = project(oa_ref, wa_ref)
        yb = project(ob_ref, wb_ref)
        yc = project(oc_ref, wc_ref)
        gt = gt_ref[...].astype(F32)
        ya_ref[...] = ya.astype(BF16)
        yb_ref[...] = yb.astype(BF16)
        yc_ref[...] = yc.astype(BF16)
        mg_ref[...] = (gt[:, :D_MODEL] * ya + gt[:, D_MODEL:2 * D_MODEL] * yb + gt[:, 2 * D_MODEL:] * yc).astype(BF16)

    row = lambda c: pl.BlockSpec((tm, c), lambda i: (i, 0))
    full = lambda a: pl.BlockSpec(a.shape, lambda i: (0, 0, 0))
    return pl.pallas_call(
        body,
        grid=(m // tm,),
        in_specs=[row(SB_W), row(DIL_W), row(MEM_W), full(w_sb), full(w_dil), full(w_mem), row(3 * D_MODEL)],
        out_specs=[row(D_MODEL)] * 4,
        out_shape=[jax.ShapeDtypeStruct((m, D_MODEL), BF16)] * 4,
        compiler_params=pltpu.CompilerParams(dimension_semantics=("parallel",)),
        name="branch_merge_fwd",
    )(o_a, o_b, o_c, w_sb, w_dil, w_mem, gates)


SB_T = 256
SB_SCALE = HEAD_DIM ** -0.5


def _sb_masks():
    row = lax.broadcasted_iota(jnp.int32, (SB_T, SB_T), 0)
    col = lax.broadcasted_iota(jnp.int32, (SB_T, SB_T), 1)
    lane = lax.broadcasted_iota(jnp.int32, (1, LANES), 1)
    return row, col, lane


def _sb_logs(z):
    lb = jnp.minimum(z, 0.0) - jnp.log(1.0 + jnp.exp(-jnp.abs(z)))
    return lb, lb - z


def _sb_specs(n_heads_pairs, col0):
    q = pl.BlockSpec((None, SB_T, LANES), lambda b, p, i: (b, i, col0 + p))
    k = pl.BlockSpec((None, SEQ, LANES), lambda b, p, i: (b, 0, col0 + n_heads_pairs + p))
    v = pl.BlockSpec((None, SEQ, LANES), lambda b, p, i: (b, 0, col0 + 2 * n_heads_pairs + p))
    return q, k, v


def _sb_first_blocks(i):
    rem = lax.rem(i + 1, 3)
    return jnp.where(rem == 0, 3, rem)


def _grid_step(n_pairs, nq):
    return (pl.program_id(0) * n_pairs + pl.program_id(1)) * nq + pl.program_id(2)


def _sb_fwd(proj3, late_shards):
    bl = proj3.shape[0]
    n_pairs = SB_W // LANES
    nq = SEQ // SB_T
    n_late = len(late_shards)
    n_steps = bl * n_pairs * nq

    def body(q_ref, k_ref, v_ref, *rest):
        late_in, (o_ref, o32_ref, w_ref), late_out = rest[:n_late], rest[n_late:n_late + 3], rest[n_late + 3:2 * n_late + 3]
        step = _grid_step(n_pairs, nq)
        if n_late:
            send, forward, finish = _gather_phases(late_in, late_out, *rest[2 * n_late + 3:])
            pl.when(step == 0)(send)
            pl.when(step == n_steps // 2)(forward)
        i = pl.program_id(2)
        row, col, lane = _sb_masks()
        causal = col < row
        u_excl = (row > col).astype(BF16)
        q = q_ref[...]
        heads = []
        for h in range(2):
            mh = (lane // HEAD_DIM) == h
            heads.append((mh, jnp.where(mh, q, jnp.zeros_like(q)) * SB_SCALE))

        def blocks(js, diags, carries, acc):
            ks = [k_ref[pl.ds(pl.multiple_of(j * SB_T, SB_T), SB_T), :] for j in js]
            vs = [v_ref[pl.ds(pl.multiple_of(j * SB_T, SB_T), SB_T), :] for j in js]
            chains = [(b, h) for b in range(len(js)) for h in range(2)]
            z = {c: _dot_nt(heads[c[1]][1], ks[c[0]]) for c in chains}
            lb, lk = {}, {}
            for c in chains:
                lb[c], lk[c] = _sb_logs(z[c])
                if diags[c[0]]:
                    lk[c] = jnp.where(causal, lk[c], 0.0)
            r = {c: _split_dot(lk[c], u_excl) for c in chains}
            carries = list(carries)
            w = {}
            for b, h in chains:
                w_c = jnp.exp(lb[b, h] + r[b, h] + carries[h])
                w[b, h] = (jnp.where(causal, w_c, 0.0) if diags[b] else w_c).astype(BF16)
                w_ref[h, js[b]] = w[b, h]
                carries[h] = carries[h] + (r[b, h][:, :1] + lk[b, h][:, :1])
            for b, h in chains:
                acc = acc + _dot(w[b, h], jnp.where(heads[h][0], vs[b], jnp.zeros_like(vs[b])))
            return tuple(carries), acc

        zero = jnp.zeros((SB_T, 1), F32)
        init = ((zero, zero), jnp.zeros((SB_T, LANES), F32))
        first = _sb_first_blocks(i)
        carries, acc = lax.cond(
            first == 1, lambda: blocks([i], (True,), *init),
            lambda: lax.cond(first == 2, lambda: blocks([i, i - 1], (True, False), *init),
                             lambda: blocks([i, i - 1, i - 2], (True, False, False), *init)))
        rest = i - first
        carries, acc = lax.fori_loop(
            0, (rest + 1) // 3,
            lambda jj, c: blocks([rest - 3 * jj, rest - 1 - 3 * jj, rest - 2 - 3 * jj], (False, False, False), c[0], c[1]),
            (carries, acc))
        o_ref[...] = acc.astype(BF16)
        o32_ref[...] = acc
        if n_late:
            pl.when(step == n_steps - 1)(finish)

    q_spec, k_spec, v_spec = _sb_specs(n_pairs, 0)
    blk = pl.BlockSpec((None, SB_T, LANES), lambda b, p, i: (b, i, p))
    out = pl.pallas_call(
        body,
        grid=(bl, n_pairs, nq),
        in_specs=[q_spec, k_spec, v_spec] + [ANY] * n_late,
        out_specs=[blk, blk, _sb_weight_spec(nq)] + [ANY] * n_late,
        out_shape=[jax.ShapeDtypeStruct((bl, SEQ, SB_W), BF16), jax.ShapeDtypeStruct((bl, SEQ, SB_W), F32),
                   jax.ShapeDtypeStruct((bl, n_pairs, nq, 2, nq, SB_T, SB_T), BF16)] + _gather_out_shapes(late_shards),
        scratch_shapes=_gather_sems(n_late) if n_late else [],
        compiler_params=pltpu.CompilerParams(dimension_semantics=("arbitrary", "arbitrary", "arbitrary")),
        name="sb_fwd",
    )(proj3, proj3, proj3, *late_shards)
    return out[0], out[1], out[2], out[3:]


def _sb_weight_spec(nq):
    return pl.BlockSpec((None, None, None, 2, nq, SB_T, SB_T), lambda b, p, i: (b, p, i, 0, 0, 0, 0))


def _sb_bwd(proj3, o_a, do_a, w_all, parts):
    bl = proj3.shape[0]
    n_pairs = SB_W // LANES
    nq = SEQ // SB_T
    n_parts = len(parts)
    n_steps = bl * n_pairs * nq

    def body(q_ref, k_ref, v_ref, o_ref, do_ref, w_ref, *rest):
        p_refs, (dq_ref, dk_ref, dv_ref), land_refs = rest[:n_parts], rest[n_parts:n_parts + 3], rest[n_parts + 3:2 * n_parts + 3]
        dk_acc, dv_acc = rest[2 * n_parts + 3:2 * n_parts + 5]
        step = _grid_step(n_pairs, nq)
        if n_parts:
            send, finish = _chip_exchange_phases(p_refs, land_refs, *rest[2 * n_parts + 5:])
            pl.when(step == 0)(send)
        i = pl.program_id(2)

        @pl.when(i == 0)
        def _():
            dk_acc[...] = jnp.zeros_like(dk_acc)
            dv_acc[...] = jnp.zeros_like(dv_acc)

        row, col, lane = _sb_masks()
        causal = col < row
        u_incl = (row >= col).astype(BF16)
        q = q_ref[...]
        do = do_ref[...]
        prod = do.astype(F32) * o_ref[...]
        heads = []
        for h in range(2):
            mh = (lane // HEAD_DIM) == h
            d_tot = jnp.sum(jnp.where(mh, prod, 0.0), axis=1, keepdims=True)
            heads.append((mh, jnp.where(mh, q, jnp.zeros_like(q)) * SB_SCALE, jnp.where(mh, do, jnp.zeros_like(do)), d_tot))

        def blocks(js, diags, c_das, dq):
            starts = [pl.multiple_of(j * SB_T, SB_T) for j in js]
            ks = [k_ref[pl.ds(s, SB_T), :] for s in starts]
            vs = [v_ref[pl.ds(s, SB_T), :] for s in starts]
            chains = [(b, h) for b in range(len(js)) for h in range(2)]
            z = {c: _dot_nt(heads[c[1]][1], ks[c[0]]) for c in chains}
            dw = {c: _dot_nt(heads[c[1]][2], vs[c[0]]) for c in chains}
            wb = {(b, h): w_ref[h, js[b]] for b, h in chains}
            da = {c: dw[c] * wb[c].astype(F32) for c in chains}
            sfx = {c: _split_dot(da[c], u_incl) for c in chains}
            c_das = list(c_das)
            dz = {}
            for b, h in chains:
                dlk = heads[h][3] - c_das[h] - sfx[b, h]
                if diags[b]:
                    dlk = jnp.where(causal, dlk, 0.0)
                c_das[h] = c_das[h] + sfx[b, h][:, :1]
                e = jnp.exp(-jnp.abs(z[b, h]))
                inv = 1.0 / (1.0 + e)
                pos = z[b, h] >= 0.0
                beta = jnp.where(pos, inv, e * inv)
                one_m_beta = jnp.where(pos, e * inv, inv)
                dz[b, h] = (da[b, h] * one_m_beta - dlk * beta).astype(BF16)
            for b, h in chains:
                dq = dq + _dot(dz[b, h], jnp.where(heads[h][0], ks[b], jnp.zeros_like(ks[b])))
            for b in range(len(js)):
                dk_acc[pl.ds(starts[b], SB_T), :] += _dot_tn(dz[b, 0], heads[0][1]) + _dot_tn(dz[b, 1], heads[1][1])
                dv_acc[pl.ds(starts[b], SB_T), :] += _dot_tn(wb[b, 0], heads[0][2]) + _dot_tn(wb[b, 1], heads[1][2])
            return tuple(c_das), dq

        zero = jnp.zeros((SB_T, 1), F32)
        init = ((zero, zero), jnp.zeros((SB_T, LANES), F32))
        first = _sb_first_blocks(i)
        state = lax.cond(
            first == 1, lambda: blocks([i], (True,), *init),
            lambda: lax.cond(first == 2, lambda: blocks([i, i - 1], (True, False), *init),
                             lambda: blocks([i, i - 1, i - 2], (True, False, False), *init)))
        rest = i - first
        state = lax.fori_loop(
            0, (rest + 1) // 3,
            lambda jj, c: blocks([rest - 3 * jj, rest - 1 - 3 * jj, rest - 2 - 3 * jj], (False, False, False), c[0], c[1]), state)
        dq_ref[...] = (state[1] * SB_SCALE).astype(BF16)

        @pl.when(i == nq - 1)
        def _():
            dk_ref[...] = dk_acc[...].astype(BF16)
            dv_ref[...] = dv_acc[...].astype(BF16)

        if n_parts:
            pl.when(step == n_steps - 1)(finish)

    q_spec, k_spec, v_spec = _sb_specs(n_pairs, 0)
    blk = pl.BlockSpec((None, SB_T, LANES), lambda b, p, i: (b, i, p))
    seq = pl.BlockSpec((None, SEQ, LANES), lambda b, p, i: (b, 0, p))
    shape = jax.ShapeDtypeStruct((bl, SEQ, SB_W), BF16)
    out = pl.pallas_call(
        body,
        grid=(bl, n_pairs, nq),
        in_specs=[q_spec, k_spec, v_spec, blk, blk, _sb_weight_spec(nq)] + [ANY] * n_parts,
        out_specs=[blk, seq, seq] + [ANY] * n_parts,
        out_shape=[shape, shape, shape] + [jax.ShapeDtypeStruct(p.shape, p.dtype) for p in parts],
        scratch_shapes=[pltpu.VMEM((SEQ, LANES), F32), pltpu.VMEM((SEQ, LANES), F32)]
        + (_chip_exchange_sems(n_parts) if n_parts else []),
        compiler_params=pltpu.CompilerParams(dimension_semantics=("arbitrary", "arbitrary", "arbitrary")),
        name="sb_bwd",
    )(proj3, proj3, proj3, o_a, do_a, w_all, *parts)
    return out[0], out[1], out[2], out[3:]


BAND = 128


BAND_HEADS = DIL_W // HEAD_DIM


def _swap_half(x):
    n = x.shape[-1]
    lane = lax.broadcasted_iota(jnp.int32, (1, n), 1)
    return jnp.where((lane % HEAD_DIM) < HEAD_DIM // 2, pltpu.roll(x, n - HEAD_DIM // 2, 1), pltpu.roll(x, HEAD_DIM // 2, 1))


def _rope(x, cos, sin_signed):
    x = x.astype(F32)
    return x * cos + _swap_half(x) * sin_signed


def _band_delta(do_b, o_b, lse_b):
    def fn(do, o, lse):
        lane_in = lax.broadcasted_iota(jnp.int32, (DIL_W, LANES), 0)
        col = lax.broadcasted_iota(jnp.int32, (DIL_W, LANES), 1)
        sum_head = ((lane_in // HEAD_DIM == col - BAND_HEADS) & (col >= BAND_HEADS) & (col < 2 * BAND_HEADS)).astype(BF16)
        return (lse + _split_dot(do.astype(F32) * o.astype(F32), sum_head),)

    return _rowwise(fn, [(do_b, "row"), (o_b, "row"), (lse_b, "row")], [(LANES, F32, "row")], tm=512, name="band_delta")[0]


def _band_masks():
    qi = lax.broadcasted_iota(jnp.int32, (BAND, 2 * BAND), 0) + BAND
    kj = lax.broadcasted_iota(jnp.int32, (BAND, 2 * BAND), 1)
    dist = qi - kj
    row = lax.broadcasted_iota(jnp.int32, (BAND, BAND), 0)
    col = lax.broadcasted_iota(jnp.int32, (BAND, BAND), 1)
    return col <= row, (dist >= 0) & (dist <= BAND)


def _band_attend(q, k, v, valid):
    lane = lax.broadcasted_iota(jnp.int32, (1, DIL_W), 1)
    stat_lane = lax.broadcasted_iota(jnp.int32, (1, LANES), 1)
    hs = range(BAND_HEADS)
    mh = [(lane // HEAD_DIM) == h for h in hs]
    s = [jnp.where(valid, _dot_nt(jnp.where(mh[h], q, jnp.zeros_like(q)), k), NEG_INF) for h in hs]
    m = [jnp.max(s[h], axis=1, keepdims=True) for h in hs]
    p = [jnp.exp(s[h] - m[h]) for h in hs]
    den = [jnp.sum(p[h], axis=1, keepdims=True) for h in hs]
    pv = [_dot(p[h].astype(BF16), jnp.where(mh[h], v, jnp.zeros_like(v))) for h in hs]
    o = jnp.zeros((BAND, DIL_W), F32)
    lse = jnp.zeros((BAND, LANES), F32)
    for h in hs:
        o = o + pv[h] * (1.0 / den[h])
        lse = jnp.where(stat_lane == h, m[h] + jnp.log(den[h]), lse)
    return o, lse


def _band_attend_bwd(q, k, v, valid, do, st):
    lane = lax.broadcasted_iota(jnp.int32, (1, DIL_W), 1)
    hs = range(BAND_HEADS)
    mh = [(lane // HEAD_DIM) == h for h in hs]
    qh = [jnp.where(mh[h], q, jnp.zeros_like(q)) for h in hs]
    doh = [jnp.where(mh[h], do, jnp.zeros_like(do)) for h in hs]
    s = [_dot_nt(qh[h], k) for h in hs]
    dp = [_dot_nt(doh[h], v) for h in hs]
    p = [jnp.where(valid, jnp.exp(s[h] - st[:, h:h + 1]), 0.0) for h in hs]
    ds = [(p[h] * (dp[h] - st[:, BAND_HEADS + h:BAND_HEADS + h + 1])).astype(BF16) for h in hs]
    pb = [p[h].astype(BF16) for h in hs]
    dq = sum(_dot(ds[h], jnp.where(mh[h], k, jnp.zeros_like(k))) for h in hs)
    dk = sum(_dot_tn(ds[h], qh[h]) for h in hs)
    dv = sum(_dot_tn(pb[h], doh[h]) for h in hs)
    return dq, dk, dv


def _band_group_specs(lead, rows, cls, col0):
    def spec(width):
        if lead == "rows":
            return pl.BlockSpec((None, rows, width), lambda b, i: (b, 0, col0))
        return pl.BlockSpec((None, rows, cls * width), lambda b, i: (b, 0, i))
    return spec


def _band_group_fwd(a, cos_g, sin_g, *, rows, cls, steps, col0, name):
    bl = a.shape[0]
    nb = rows // BAND
    grp_w = 3 * DIL_W

    def body(a_ref, c_ref, s_ref, o_ref, l_ref, qr_all, kr_all):
        first_valid, later_valid = _band_masks()
        for j in range(cls):
            qr, kr = qr_all.at[j], kr_all.at[j]
            a0, t0, s0 = j * grp_w, j * DIL_W, j * LANES
            cos, sin = c_ref[:, t0:t0 + DIL_W], s_ref[:, t0:t0 + DIL_W]
            qr[...] = (_rope(a_ref[:, a0:a0 + DIL_W], cos, sin) * SB_SCALE).astype(BF16)
            kr[...] = _rope(a_ref[:, a0 + DIL_W:a0 + 2 * DIL_W], cos, sin).astype(BF16)

            def block(q0, k0, keys, valid, a0=a0, t0=t0, s0=s0):
                o, lse = _band_attend(qr[pl.ds(q0, BAND), :], kr[pl.ds(k0, keys), :],
                                      a_ref[pl.ds(k0, keys), a0 + 2 * DIL_W:a0 + grp_w], valid)
                o_ref[pl.ds(q0, BAND), t0:t0 + DIL_W] = o.astype(BF16)
                l_ref[pl.ds(q0, BAND), s0:s0 + LANES] = lse

            block(0, 0, BAND, first_valid)
            if nb > 1:
                def later(b, carry, block=block):
                    block(pl.multiple_of(b * BAND, BAND), pl.multiple_of((b - 1) * BAND, BAND), 2 * BAND, later_valid)
                    return carry

                lax.fori_loop(1, nb, later, 0, unroll=5 if (nb - 1) % 5 == 0 else 3)

    lead = "rows" if col0 is not None else "cols"
    spec = _band_group_specs(lead, rows, cls, col0)
    tab = pl.BlockSpec((rows, cls * DIL_W), lambda b, i: (0, i))
    n_cls = cos_g.shape[1] // DIL_W
    return pl.pallas_call(
        body,
        grid=(bl, steps),
        in_specs=[spec(grp_w), tab, tab],
        out_specs=[pl.BlockSpec((None, rows, cls * DIL_W), lambda b, i: (b, 0, i)),
                   pl.BlockSpec((None, rows, cls * LANES), lambda b, i: (b, 0, i))],
        out_shape=[jax.ShapeDtypeStruct((bl, rows, n_cls * DIL_W), BF16), jax.ShapeDtypeStruct((bl, rows, n_cls * LANES), F32)],
        scratch_shapes=[pltpu.VMEM((cls, rows, DIL_W), BF16), pltpu.VMEM((cls, rows, DIL_W), BF16)],
        compiler_params=pltpu.CompilerParams(dimension_semantics=("parallel", "parallel")),
        name=name,
    )(a, cos_g, sin_g)


def _band_group_bwd(a, do, st, cos_g, sin_g, *, rows, cls, steps, col0, name, side=None):
    bl = a.shape[0]
    nb = rows // BAND
    grp_w = 3 * DIL_W
    side_arrays = side[1] if side else []
    n_side = len(side_arrays)

    def body(a_ref, do_ref, st_ref, c_ref, s_ref, *rest):
        out_ref = rest[n_side]
        scratch = rest[2 * n_side + 1:2 * n_side + 5]
        if n_side:
            step = pl.program_id(0) * steps + pl.program_id(1)
            finish = _run_side(side, rest[:n_side], rest[n_side + 1:2 * n_side + 1], rest[2 * n_side + 5:], step, bl * steps)
        first_valid, later_valid = _band_masks()
        for j in range(cls):
            qr, kr, dk_acc, dv_acc = (s.at[j] for s in scratch)
            a0, t0 = j * grp_w, j * DIL_W
            cos, sin = c_ref[:, t0:t0 + DIL_W], s_ref[:, t0:t0 + DIL_W]
            qr[...] = (_rope(a_ref[:, a0:a0 + DIL_W], cos, sin) * SB_SCALE).astype(BF16)
            kr[...] = _rope(a_ref[:, a0 + DIL_W:a0 + 2 * DIL_W], cos, sin).astype(BF16)
            dk_acc[...] = jnp.zeros_like(dk_acc)
            dv_acc[...] = jnp.zeros_like(dv_acc)

            def block(q0, k0, keys, valid, a0=a0, t0=t0, s0=j * LANES):
                qrows, krows = pl.ds(q0, BAND), pl.ds(k0, keys)
                dq, dk, dv = _band_attend_bwd(
                    qr[qrows, :], kr[krows, :], a_ref[krows, a0 + 2 * DIL_W:a0 + grp_w], valid,
                    do_ref[qrows, t0:t0 + DIL_W], st_ref[qrows, s0:s0 + LANES])
                dq = dq * SB_SCALE
                out_ref[qrows, a0:a0 + DIL_W] = (dq * c_ref[qrows, t0:t0 + DIL_W]
                                                 - _swap_half(dq) * s_ref[qrows, t0:t0 + DIL_W]).astype(BF16)
                dk_acc[krows, :] += dk
                dv_acc[krows, :] += dv

            block(0, 0, BAND, first_valid)
            if nb > 1:
                def later(b, carry, block=block):
                    block(pl.multiple_of(b * BAND, BAND), pl.multiple_of((b - 1) * BAND, BAND), 2 * BAND, later_valid)
                    return carry

                lax.fori_loop(1, nb, later, 0, unroll=5 if (nb - 1) % 5 == 0 else 3)
            dk = dk_acc[...]
            out_ref[:, a0 + DIL_W:a0 + 2 * DIL_W] = (dk * cos - _swap_half(dk) * sin).astype(BF16)
            out_ref[:, a0 + 2 * DIL_W:a0 + grp_w] = dv_acc[...].astype(BF16)
        if n_side:
            finish()

    lead = "rows" if col0 is not None else "cols"
    spec = _band_group_specs(lead, rows, cls, col0)
    dspec = _band_group_specs(lead, rows, cls, 0 if col0 is not None else None)
    tab = pl.BlockSpec((rows, cls * DIL_W), lambda b, i: (0, i))
    n_cls = cos_g.shape[1] // DIL_W
    out = pl.pallas_call(
        body,
        grid=(bl, steps),
        in_specs=[spec(grp_w), dspec(DIL_W), dspec(LANES), tab, tab] + [ANY] * n_side,
        out_specs=[pl.BlockSpec((None, rows, cls * grp_w), lambda b, i: (b, 0, i))] + [ANY] * n_side,
        out_shape=[jax.ShapeDtypeStruct((bl, rows, n_cls * grp_w), BF16)] + (_side_out_shapes(side) if n_side else []),
        scratch_shapes=[pltpu.VMEM((cls, rows, DIL_W), BF16), pltpu.VMEM((cls, rows, DIL_W), BF16),
                        pltpu.VMEM((cls, rows, DIL_W), F32), pltpu.VMEM((cls, rows, DIL_W), F32)] + (_side_sems(side) if n_side else []),
        compiler_params=pltpu.CompilerParams(dimension_semantics=("arbitrary", "arbitrary") if n_side else ("parallel", "parallel")),
        name=name,
    )(a, do, st, cos_g, sin_g, *side_arrays)
    return (out[0], out[1:]) if n_side else out[0]


def _band_merge3(groups):
    t, tm = groups[0][0].shape[0], 512

    def body(o0, l0, o1, l1, o2, l2, ob_ref, lse_ref):
        a, b, c = l0[...], l1[...], l2[...]
        m = jnp.maximum(jnp.maximum(a, b), c)
        lse = m + jnp.log(jnp.exp(a - m) + jnp.exp(b - m) + jnp.exp(c - m))
        lane = lax.broadcasted_iota(jnp.int32, (1, DIL_W), 1)
        acc = jnp.zeros((tm, DIL_W), F32)
        for o_ref, l in ((o0, a), (o1, b), (o2, c)):
            share = jnp.exp(l - lse)
            spread = jnp.zeros((tm, DIL_W), F32)
            for h in range(BAND_HEADS):
                spread = jnp.where(lane // HEAD_DIM == h, share[:, h:h + 1], spread)
            acc = acc + spread * o_ref[...].astype(F32)
        ob_ref[...] = acc.astype(BF16)
        stat_lane = lax.broadcasted_iota(jnp.int32, (1, LANES), 1)
        lse_ref[...] = jnp.where(stat_lane < BAND_HEADS, lse, 0.0)

    spec = pl.BlockSpec((tm, DIL_W), lambda i: (i, 0))
    spec_l = pl.BlockSpec((tm, LANES), lambda i: (i, 0))
    return pl.pallas_call(
        body,
        grid=(t // tm,),
        in_specs=[spec, spec_l] * 3,
        out_specs=[spec, spec_l],
        out_shape=[jax.ShapeDtypeStruct((t, DIL_W), BF16), jax.ShapeDtypeStruct((t, LANES), F32)],
        compiler_params=pltpu.CompilerParams(dimension_semantics=("parallel",)),
        name="band_merge",
    )(*[a for g in groups for a in g])


MEM_T = 512
MEM_SCALE = 128 ** -0.5
MEM_Q_COL = (D_IN - MEM_W) // LANES


MEM_HEADS = MEM_W // LANES


def _mem_specs():
    qs = [pl.BlockSpec((None, MEM_T, LANES), lambda b, i, h=h: (b, i, MEM_Q_COL + h)) for h in range(MEM_HEADS)]
    kv = pl.BlockSpec((None, MEM_LEN, 2 * MEM_W), lambda b, i: (b, 0, 0))
    blk = pl.BlockSpec((None, MEM_T, MEM_W), lambda b, i: (b, i, 0))
    return qs, kv, blk


def _mem_probs(q, k):
    s = _dot_nt(q, k) * MEM_SCALE
    p = jnp.exp(s - jnp.max(s, axis=1, keepdims=True))
    return p * (1.0 / jnp.sum(p, axis=1, keepdims=True))


def _head_cols(h, base=0):
    return slice(base + h * LANES, base + (h + 1) * LANES)


def _mem_fwd(proj3, kv3):
    bl = proj3.shape[0]
    hs = range(MEM_HEADS)

    def body(*refs):
        q_refs, kv_ref, o_ref = refs[:MEM_HEADS], refs[MEM_HEADS], refs[MEM_HEADS + 1]
        p = [_mem_probs(q_refs[h][...], kv_ref[:, _head_cols(h)]) for h in hs]
        for h in hs:
            o_ref[:, _head_cols(h)] = _dot(p[h].astype(BF16), kv_ref[:, _head_cols(h, MEM_W)]).astype(BF16)

    qs, kv, blk = _mem_specs()
    return pl.pallas_call(
        body,
        grid=(bl, SEQ // MEM_T),
        in_specs=qs + [kv],
        out_specs=blk,
        out_shape=jax.ShapeDtypeStruct((bl, SEQ, MEM_W), BF16),
        compiler_params=pltpu.CompilerParams(dimension_semantics=("parallel", "parallel")),
        name="mem_fwd",
    )(*([proj3] * MEM_HEADS), kv3)


def _mem_bwd(proj3, kv3, do_c):
    bl = proj3.shape[0]
    hs = range(MEM_HEADS)

    def body(*refs):
        q_refs, kv_ref, do_ref, dq_ref, dkv_ref = refs[:MEM_HEADS], *refs[MEM_HEADS:MEM_HEADS + 4]

        @pl.when(pl.program_id(1) == 0)
        def _():
            dkv_ref[...] = jnp.zeros_like(dkv_ref)

        q = [q_refs[h][...] for h in hs]
        do = [do_ref[:, _head_cols(h)] for h in hs]
        p = [_mem_probs(q[h], kv_ref[:, _head_cols(h)]) for h in hs]
        dp = [_dot_nt(do[h], kv_ref[:, _head_cols(h, MEM_W)]) for h in hs]
        ds = [(p[h] * (dp[h] - jnp.sum(p[h] * dp[h], axis=1, keepdims=True)) * MEM_SCALE).astype(BF16) for h in hs]
        for h in hs:
            dq_ref[:, _head_cols(h)] = _dot(ds[h], kv_ref[:, _head_cols(h)]).astype(BF16)
            dkv_ref[:, _head_cols(h)] += _dot_tn(ds[h], q[h])
            dkv_ref[:, _head_cols(h, MEM_W)] += _dot_tn(p[h].astype(BF16), do[h])

    qs, kv, blk = _mem_specs()
    return pl.pallas_call(
        body,
        grid=(bl, SEQ // MEM_T),
        in_specs=qs + [kv, blk],
        out_specs=[blk, kv],
        out_shape=[jax.ShapeDtypeStruct((bl, SEQ, MEM_W), BF16), jax.ShapeDtypeStruct((bl, MEM_LEN, 2 * MEM_W), F32)],
        compiler_params=pltpu.CompilerParams(dimension_semantics=("parallel", "arbitrary")),
        name="mem_bwd",
    )(*([proj3] * MEM_HEADS), kv3, do_c)


def _place():
    x, y, c = lax.axis_index("x"), lax.axis_index("y"), lax.axis_index("c")
    return x, y, c


def _other_chips(x, y):
    return [(1 - x, y), (x, 1 - y), (1 - x, 1 - y)]


def _remote(src, dst, send_sem, recv_sem, to):
    return pltpu.make_async_remote_copy(src_ref=src, dst_ref=dst, send_sem=send_sem, recv_sem=recv_sem,
                                        device_id=to, device_id_type=MESH)


ANY = pl.BlockSpec(memory_space=pl.ANY)


def _gather_out_shapes(shards):
    return [jax.ShapeDtypeStruct((N_CHIPS,) + s.shape, s.dtype) for s in shards]


def _gather_sems(n):
    return [pltpu.SemaphoreType.DMA((6 * n,)), pltpu.SemaphoreType.DMA((6 * n,))]


def _gather_phases(in_refs, out_refs, send_sems, recv_sems):
    x, y, c = _place()
    sibling = (x, y, 1 - c)
    chips = _other_chips(x, y)
    first, passed = [], []
    for k in range(len(in_refs)):
        hf = in_refs[k].shape[0] // 2

        def half(px, py, pc, k=k, hf=hf):
            return out_refs[k].at[2 * px + py, pl.ds(pc * hf, hf), :]

        src = in_refs[k].at[pl.ds(c * hf, hf), :]
        for j, chip in enumerate(chips):
            s = 6 * k + j
            first.append(_remote(src, half(x, y, c), send_sems.at[s], recv_sems.at[s], (*chip, c)))
            passed.append((_remote(src, half(*chip, c), send_sems.at[s], recv_sems.at[s], (*chip, c)),
                           _remote(half(*chip, c), half(*chip, c), send_sems.at[s + 3], recv_sems.at[s + 3], sibling),
                           _remote(src, half(*chip, 1 - c), send_sems.at[s + 3], recv_sems.at[s + 3], sibling)))

    def send():
        for cp in first:
            cp.start()

    def forward():
        for landed, fwd, _ in passed:
            landed.wait_recv()
            fwd.start()

    def finish():
        for _, _, from_sibling in passed:
            from_sibling.wait_recv()
        for cp in first:
            cp.wait_send()
        for _, fwd, _ in passed:
            fwd.wait_send()

    return send, forward, finish


def _pair_exchange(grads, *, name):
    n = len(grads)
    side = ("pair", grads)

    def body(*refs):
        send, _, finish = _side_phases(side, refs[:n], refs[n:2 * n], refs[2 * n:])
        send()
        finish()

    return pl.pallas_call(
        body,
        in_specs=[ANY] * n,
        out_specs=[ANY] * n,
        out_shape=_side_out_shapes(side),
        scratch_shapes=_side_sems(side),
        name=name,
    )(*grads)


def _pair_exchange_phases(g_refs, land_refs, send_sems, recv_sems):
    x, y, c = _place()
    cps = []
    for k in range(len(g_refs)):
        hf = g_refs[k].shape[1] // 2
        src = g_refs[k].at[:, pl.ds((1 - c) * hf, hf), :]
        cps.append(_remote(src, land_refs[k], send_sems.at[k], recv_sems.at[k], (x, y, 1 - c)))

    def send():
        for cp in cps:
            cp.start()

    def finish():
        for cp in cps:
            cp.wait()

    return send, finish


def _side_out_shapes(side):
    kind, arrays = side
    if kind == "gather":
        return _gather_out_shapes(arrays)
    if kind == "pair":
        return [jax.ShapeDtypeStruct((N_CHIPS, g.shape[1] // 2, g.shape[2]), g.dtype) for g in arrays]
    return [jax.ShapeDtypeStruct(p.shape, p.dtype) for p in arrays]


def _side_sems(side):
    kind, arrays = side
    n = len(arrays)
    if kind == "gather":
        return _gather_sems(n)
    if kind == "pair":
        return [pltpu.SemaphoreType.DMA((n,)), pltpu.SemaphoreType.DMA((n,))]
    return _chip_exchange_sems(n)


def _side_phases(side, in_refs, out_refs, sems):
    kind = side[0]
    if kind == "gather":
        return _gather_phases(in_refs, out_refs, *sems)
    send, finish = (_pair_exchange_phases if kind == "pair" else _chip_exchange_phases)(in_refs, out_refs, *sems)
    return send, None, finish


def _run_side(side, in_refs, out_refs, sems, step, n_steps):
    first, mid, last = _side_phases(side, in_refs, out_refs, sems)
    pl.when(step == 0)(first)
    if mid is not None:
        pl.when(step == n_steps // 2)(mid)
    return lambda: pl.when(step == n_steps - 1)(last)


def _pair_add(g, land, c_arr, *, name):
    _, a, b = g.shape
    hf = a // 2

    def body(c_ref, g_ref, l_ref, o_ref):
        o_ref[...] = (g_ref[...] + l_ref[...]).astype(BF16)

    return pl.pallas_call(
        body,
        grid_spec=pltpu.PrefetchScalarGridSpec(
            num_scalar_prefetch=1,
            grid=(N_CHIPS,),
            in_specs=[pl.BlockSpec((None, None, hf, b), lambda s, c_ref: (s, c_ref[0], 0, 0)),
                      pl.BlockSpec((None, hf, b), lambda s, c_ref: (s, 0, 0))],
            out_specs=pl.BlockSpec((None, hf, b), lambda s, c_ref: (s, 0, 0)),
        ),
        out_shape=jax.ShapeDtypeStruct((N_CHIPS, hf, b), BF16),
        compiler_params=pltpu.CompilerParams(dimension_semantics=("parallel",)),
        name=name,
    )(c_arr, g.reshape(N_CHIPS, 2, hf, b), land)


def _chip_exchange_sems(n):
    return [pltpu.SemaphoreType.DMA((3 * n,)), pltpu.SemaphoreType.DMA((3 * n,))]


def _chip_exchange_phases(p_refs, land_refs, send_sems, recv_sems):
    x, y, c = _place()
    me = 2 * x + y
    sends, recvs = [], []
    for k in range(len(p_refs)):
        for j, (cx, cy) in enumerate(_other_chips(x, y)):
            s = 3 * k + j
            sends.append(_remote(p_refs[k].at[2 * cx + cy], land_refs[k].at[me], send_sems.at[s], recv_sems.at[s], (cx, cy, c)))
            recvs.append(_remote(p_refs[k].at[me], land_refs[k].at[2 * cx + cy], send_sems.at[s], recv_sems.at[s], (cx, cy, c)))

    def send():
        for cp in sends:
            cp.start()

    def finish():
        for cp in recvs:
            cp.wait_recv()
        for cp in sends:
            cp.wait_send()

    return send, finish


def _chip_add(land, part, me_arr, *, name):
    _, r, b = land.shape

    def body(me_ref, p_ref, l1_ref, l2_ref, l3_ref, o_ref):
        o_ref[...] = ((p_ref[...].astype(F32) + l1_ref[...].astype(F32)) + l2_ref[...].astype(F32)) + l3_ref[...].astype(F32)

    tr = r // 2
    other = lambda j: pl.BlockSpec((None, tr, b), lambda i, me_ref: (jnp.bitwise_xor(me_ref[0], j), i, 0))
    return pl.pallas_call(
        body,
        grid_spec=pltpu.PrefetchScalarGridSpec(
            num_scalar_prefetch=1,
            grid=(r // tr,),
            in_specs=[pl.BlockSpec((None, tr, b), lambda i, me_ref: (me_ref[0], i, 0)), other(2), other(1), other(3)],
            out_specs=pl.BlockSpec((tr, b), lambda i, me_ref: (i, 0)),
        ),
        out_shape=jax.ShapeDtypeStruct((r, b), F32),
        compiler_params=pltpu.CompilerParams(dimension_semantics=("parallel",)),
        name=name,
    )(me_arr, part, land, land, land)


def _pair_share(halves):
    n = len(halves)

    def body(*refs):
        h_refs, out_refs = refs[:n], refs[n:2 * n]
        send_sems, recv_sems = refs[2 * n:]
        x, y, c = _place()
        cps = [_remote(h_refs[k], out_refs[k], send_sems.at[k], recv_sems.at[k], (x, y, 1 - c)) for k in range(n)]
        for cp in cps:
            cp.start()
        for cp in cps:
            cp.wait()

    return pl.pallas_call(
        body,
        in_specs=[ANY] * n,
        out_specs=[ANY] * n,
        out_shape=[jax.ShapeDtypeStruct(h.shape, F32) for h in halves],
        scratch_shapes=[pltpu.SemaphoreType.DMA((n,)), pltpu.SemaphoreType.DMA((n,))],
        name="pair_share",
    )(*halves)


def _all_sum_small(part):
    def body(p_ref, o_ref, slots, send_sems, recv_sems):
        x, y, c = _place()
        me = 4 * x + 2 * y + c
        slots[me] = p_ref[...]
        peers = [(x ^ dx, y ^ dy, c ^ dc) for dx in (0, 1) for dy in (0, 1) for dc in (0, 1)][1:]
        sends = [_remote(p_ref, slots.at[me], send_sems.at[k], recv_sems.at[k], peer) for k, peer in enumerate(peers)]
        for cp in sends:
            cp.start()
        for k, (px, py, pc) in enumerate(peers):
            _remote(p_ref, slots.at[4 * px + 2 * py + pc], send_sems.at[k], recv_sems.at[k], (px, py, pc)).wait_recv()
        for cp in sends:
            cp.wait_send()
        acc = slots[0]
        for d in range(1, 8):
            acc = acc + slots[d]
        o_ref[...] = acc

    vmem = pl.BlockSpec(memory_space=pltpu.VMEM)
    return pl.pallas_call(
        body,
        in_specs=[vmem],
        out_specs=vmem,
        out_shape=jax.ShapeDtypeStruct(part.shape, F32),
        scratch_shapes=[pltpu.VMEM((8,) + part.shape, F32), pltpu.SemaphoreType.DMA((7,)), pltpu.SemaphoreType.DMA((7,))],
        name="all_sum_small",
    )(part)


def _rope_tables():
    half = HEAD_DIM // 2
    inv_freq = np.float32(ROPE_THETA) ** (-np.arange(half, dtype=np.float32) * np.float32(2.0) / np.float32(HEAD_DIM))
    ang = np.arange(SEQ, dtype=np.float32)[:, None] * inv_freq[None, :].astype(np.float32)
    cos = np.tile(np.cos(ang).astype(np.float32), (1, 2 * BAND_HEADS))
    sin = np.tile(np.concatenate([-np.sin(ang), np.sin(ang)], axis=1).astype(np.float32), (1, BAND_HEADS))
    return jnp.asarray(cos), jnp.asarray(sin)


def _band_groups():
    out = []
    for d in DIL_D:
        rows = SEQ // d
        cls = max(1, 512 // rows) if d > 1 else 1
        out.append(dict(rows=rows, cls=cls, steps=d // cls))
    return out


def _local_step(x, mem, loss_target, g_pre_mix, g_post_mix, g_pre_ffn, g_post_ffn, g_mem, b_gate, w, comm=None):
    bl = x.shape[0]
    t = bl * SEQ
    chips = range(N_CHIPS)
    half_ff = D_FF // 2

    def with_gathered(w, names, gathered, shards):
        return {**w, **{name: lax.dynamic_update_slice(g, s[None], (comm["me"][0], 0, 0))
                        for name, g, s in zip(names, gathered, shards)}}

    x2 = x.reshape(t, D_MODEL)
    tgt2 = loss_target.reshape(t, D_MODEL)
    mem2 = mem.reshape(bl * MEM_LEN, D_MODEL)

    h = _norm_fwd(x2, g_pre_mix, name="norm_x", side=("gather", comm["first_shards"]) if comm else None)
    if comm:
        w = with_gathered(w, comm["first_names"], h[1], comm["first_shards"])
        h = h[0]
    w_in_full = _join_shards(w["w_in"])
    proj = _mm([(h, w_in_full)], nt=False, tn=2176, out_dtypes=[BF16], name="proj",
               side=("gather", comm["mid_shards"]) if comm else None)
    if comm:
        w = with_gathered(w, comm["mid_names"], proj[1], comm["mid_shards"])
        proj = proj[0]
    w_mem_kv_full = w["w_mem_kv"].reshape(D_MODEL, 2 * MEM_W)
    gates = _mm([(h, w["w_gate"], None, "j")], nt=False,tn=w["w_gate"].shape[2], out_dtypes=[BF16], name="gates",
                bias=b_gate, epilogue=lambda acc: (_sigmoid(acc),))
    hm = _norm_fwd(mem2, g_mem, name="norm_mem")
    kv_m = _mm([(hm, w_mem_kv_full)], nt=False,tn=1024, out_dtypes=[BF16], name="mem_kv")
    proj3 = proj.reshape(bl, SEQ, D_IN)
    kv3 = kv_m.reshape(bl, MEM_LEN, 2 * MEM_W)

    o_a, o_a32, sb_weights, late_gathered = _sb_fwd(proj3, comm["late_shards"] if comm else [])
    if comm:
        w = with_gathered(w, comm["late_names"], late_gathered, comm["late_shards"])
    w_o_full = w["w_o"].reshape(D_MODEL, D_MODEL)
    w_ffn_out_full = w["w_ffn_out"].reshape(D_FF, D_MODEL)

    cos_t, sin_t = _rope_tables()
    dil0 = 3 * SB_W

    grp_w = 3 * DIL_W
    band = []
    for g, (d, cfg) in enumerate(zip(DIL_D, _band_groups())):
        a_g = proj3 if d == 1 else proj3[:, :, dil0 + g * grp_w:dil0 + (g + 1) * grp_w].reshape(bl, SEQ // d, d * grp_w)
        band.append(dict(cfg, a=a_g, col0=dil0 // grp_w if d == 1 else None, cos=cos_t.reshape(SEQ // d, d * DIL_W),
                         sin=sin_t.reshape(SEQ // d, d * DIL_W)))
    outs = [_band_group_fwd(b["a"], b["cos"], b["sin"], rows=b["rows"], cls=b["cls"], steps=b["steps"], col0=b["col0"],
                            name=f"band_fwd_{g}") for g, b in enumerate(band)]
    o_b, lse_b = _band_merge3([(o.reshape(t, DIL_W), l.reshape(t, LANES)) for o, l in outs])

    o_c = _mem_fwd(proj3, kv3)

    o_a2, o_c2 = o_a.reshape(t, SB_W), o_c.reshape(t, MEM_W)
    y_a, y_b, y_c, merged = _branch_merge_fwd(o_a2, o_b, o_c2, w["w_br_sb"], w["w_br_dil"], w["w_br_mem"], gates)
    mix = _mm([(merged, w_o_full)], nt=False,tn=1024, out_dtypes=[F32], name="mix")
    x1, h2 = _mid_fwd(mix, x2, g_post_mix, g_pre_ffn)
    gg, uu, f = _ffn_in_fwd(h2, w["w_ffn_in"])
    f2 = _mm([(f, w_ffn_out_full)], nt=False,tn=1024, out_dtypes=[F32], name="ffn_out")

    dy, df2, dg_post_ffn, loss_row = _loss_bwd(f2, x1, g_post_ffn, tgt2)

    dg_ffn, du_ffn = _mm([(df2, w_ffn_out_full)], nt=True,tn=half_ff, out_dtypes=[BF16, BF16], name="d_ffn_act",
                         extras=(gg, uu), epilogue=_swiglu_bwd_epilogue)
    gw = {}
    gw["w_ffn_out"] = _mm_tn(f, df2, tm=half_ff, tn=1024, name="gw_ffn_out").reshape(N_CHIPS, D_FF // N_CHIPS, D_MODEL)
    gw_ffn_g = _mm_tn(h2, dg_ffn, tm=1024, tn=half_ff, name="gw_ffn_gate", out_shards=True, slots=(N_CHIPS, 0), group=2)
    gw["w_ffn_in"] = _mm_tn(h2, du_ffn, tm=1024, tn=half_ff, name="gw_ffn_up", out_shards=True, slots=(N_CHIPS, 2), into=gw_ffn_g,
                            group=2)
    dh2 = _mm([(dg_ffn, w["w_ffn_in"], 0, 0), (dg_ffn, w["w_ffn_in"], 1, 1), (du_ffn, w["w_ffn_in"], 0, 2),
               (du_ffn, w["w_ffn_in"], 1, 3)], nt=True, tn=1024, out_dtypes=[BF16], name="d_h2")
    dx1, dmix, dg_pre_ffn, dg_post_mix = _mid_bwd(dh2, x1, mix, g_pre_ffn, g_post_mix, dy)

    gw["w_o"] = _mm_tn(merged, dmix, tm=1024, tn=1024, name="gw_o").reshape(N_CHIPS, D_MODEL // N_CHIPS, D_MODEL)
    dmerged = _mm([(dmix, w_o_full)], nt=True, tn=1024, out_dtypes=[BF16], name="d_merged")
    dy_a, dy_b, dy_c, dgpre, db_gate = _gate_bwd(dmerged, gates, y_a, y_b, y_c)
    br_cols = D_MODEL // N_CHIPS
    gw["w_br_sb"] = _mm_tn(o_a2, dy_a, tm=512, tn=br_cols, name="gw_br_sb", out_shards=True, group=N_CHIPS)
    gw["w_br_dil"] = _mm_tn(o_b, dy_b, tm=256, tn=br_cols, name="gw_br_dil", out_shards=True, group=N_CHIPS)
    gw["w_br_mem"] = _mm_tn(o_c2, dy_c, tm=512, tn=br_cols, name="gw_br_mem", out_shards=True, group=N_CHIPS)
    gw["w_gate"] = _mm_tn(h, dgpre, tm=1024, tn=w["w_gate"].shape[2], name="gw_gate", out_shards=True, group=2)
    do_a = _mm([(dy_a, w["w_br_sb"], s, s) for s in chips], nt=True,tn=SB_W, out_dtypes=[BF16], name="d_o_a")
    do_b = _mm([(dy_b, w["w_br_dil"], s, s) for s in chips], nt=True,tn=DIL_W, out_dtypes=[BF16], name="d_o_b")
    do_c = _mm([(dy_c, w["w_br_mem"], s, s) for s in chips], nt=True,tn=MEM_W, out_dtypes=[BF16], name="d_o_c")

    dq_c, dkv_m = _mem_bwd(proj3, kv3, do_c.reshape(bl, SEQ, MEM_W))
    dkv_m = dkv_m.reshape(bl * MEM_LEN, 2 * MEM_W).astype(BF16)
    gw["w_mem_kv"] = _mm_tn(hm, dkv_m, tm=1024, tn=1024, name="gw_mem_kv").reshape(N_CHIPS, D_MODEL // N_CHIPS, 2 * MEM_W)
    dhm = _mm([(dkv_m, w_mem_kv_full)], nt=True,tn=1024, out_dtypes=[F32], name="d_hm")
    dg_mem = _mem_norm_bwd(dhm, mem2, g_mem)

    stats = _band_delta(do_b, o_b, lse_b)
    early = [name for name, _, _ in PACK if name != "w_in"] if comm else []
    grads = [gw[name] for name in early]
    d_dil = []
    for g, (d, b) in enumerate(zip(DIL_D, band)):
        out = _band_group_bwd(b["a"], do_b.reshape(bl, SEQ // d, d * DIL_W), stats.reshape(bl, SEQ // d, d * LANES),
                              b["cos"], b["sin"], rows=b["rows"], cls=b["cls"], steps=b["steps"], col0=b["col0"],
                              name=f"band_bwd_{g}", side=("pair", grads) if comm and g == 0 else None)
        if comm and g == 0:
            out, lands = out
        d_dil.append(out.reshape(bl, SEQ, grp_w))

    parts = [_pair_add(g, l, comm["c"], name="pair_add_" + name) for name, g, l in zip(early, grads, lands)] if comm else []
    dq_a, dk_a, dv_a, lands = _sb_bwd(proj3, o_a32, do_a.reshape(bl, SEQ, SB_W), sb_weights, parts)
    reduced = {name: (p, l) for name, p, l in zip(early, parts, lands)}

    in_cols = D_IN // N_CHIPS
    dproj_s = _split_to_shards([a.reshape(t, a.shape[-1]) for a in [dq_a, dk_a, dv_a] + d_dil + [dq_c]], name="dproj_shards")
    gw["w_in"] = _mm_tn(h, dproj_s, tm=1024, tn=in_cols, name="gw_in", group=2)
    if comm:
        land = _pair_exchange([gw["w_in"]], name="pair_exchange_w_in")[0]
        part_in = _pair_add(gw["w_in"], land, comm["c"], name="pair_add_w_in")
    dh = _mm([(dproj_s, w["w_in"], s, s) for s in chips] + [(dgpre, w["w_gate"], s, s) for s in chips],
             nt=True, tn=1024, out_dtypes=[BF16], name="d_h", side=("chip", [part_in]) if comm else None)
    if comm:
        dh, (land_in,) = dh
        reduced["w_in"] = (part_in, land_in)
    grad_x, dg_pre_mix = _first_bwd(dh, x2, g_pre_mix, dx1)
    small = jnp.concatenate([dg_pre_mix, dg_post_mix, dg_pre_ffn, dg_post_ffn, dg_mem, db_gate.reshape(3, D_MODEL)], axis=0)
    return loss_row[0, 0], grad_x.reshape(bl, SEQ, D_MODEL), gw, small, reduced


def kernel(x, mem, g_pre_mix, g_post_mix, g_pre_ffn, g_post_ffn, g_mem, w_in, w_mem_kv, w_br_sb, w_br_dil, w_br_mem, w_gate, b_gate, w_o, w_ffn_in, w_ffn_out, loss_target, m_g_pre_mix, m_g_post_mix, m_g_pre_ffn, m_g_post_ffn, m_g_mem, m_w_in, m_w_mem_kv, m_w_br_sb, m_w_br_dil, m_w_br_mem, m_w_gate, m_b_gate, m_w_o, m_w_ffn_in, m_w_ffn_out, v_g_pre_mix, v_g_post_mix, v_g_pre_ffn, v_g_post_ffn, v_g_mem, v_w_in, v_w_mem_kv, v_w_br_sb, v_w_br_dil, v_w_br_mem, v_w_gate, v_b_gate, v_w_o, v_w_ffn_in, v_w_ffn_out):
    w_shards = dict(w_in=w_in[0], w_mem_kv=w_mem_kv[0], w_br_sb=w_br_sb[0], w_br_dil=w_br_dil[0], w_br_mem=w_br_mem[0],
                    w_gate=w_gate[0], w_o=w_o[0], w_ffn_in=w_ffn_in[0], w_ffn_out=w_ffn_out[0])
    m_shards = dict(w_in=m_w_in[0], w_mem_kv=m_w_mem_kv[0], w_br_sb=m_w_br_sb[0], w_br_dil=m_w_br_dil[0], w_br_mem=m_w_br_mem[0],
                    w_gate=m_w_gate[0], w_o=m_w_o[0], w_ffn_in=m_w_ffn_in[0], w_ffn_out=m_w_ffn_out[0])
    v_shards = dict(w_in=v_w_in[0], w_mem_kv=v_w_mem_kv[0], w_br_sb=v_w_br_sb[0], w_br_dil=v_w_br_dil[0], w_br_mem=v_w_br_mem[0],
                    w_gate=v_w_gate[0], w_o=v_w_o[0], w_ffn_in=v_w_ffn_in[0], w_ffn_out=v_w_ffn_out[0])

    names = [name for name, _, _ in PACK]
    c_arr = lax.axis_index("c").astype(jnp.int32).reshape(1)
    me_arr = (2 * lax.axis_index("x") + lax.axis_index("y")).astype(jnp.int32).reshape(1)
    mid_names = ["w_gate", "w_mem_kv"]
    late_names = [name for name in names if name not in ["w_in"] + mid_names]
    bf = {name: w_shards[name].astype(BF16) for name in names}
    comm = dict(c=c_arr, me=me_arr, first_names=["w_in"], first_shards=[bf["w_in"]],
                mid_names=mid_names, mid_shards=[bf[name] for name in mid_names],
                late_names=late_names, late_shards=[bf[name] for name in late_names])

    loss_local, grad_x, gw, small, reduced = _local_step(x, mem, loss_target, g_pre_mix, g_post_mix, g_pre_ffn, g_post_ffn,
                                                         g_mem, b_gate, {}, comm)

    halves = [_chip_add(reduced[name][1], reduced[name][0], me_arr, name="chip_add_" + name) for name in names]
    theirs = _pair_share(halves)
    extra = jnp.concatenate([jnp.full((1, D_MODEL), loss_local, F32), jnp.zeros((7, D_MODEL), F32)], axis=0)
    summed = _all_sum_small(jnp.concatenate([small, extra], axis=0))
    small, loss = summed[:8], summed[8, 0]

    upd = {}
    for name, mine, other in zip(names, halves, theirs):
        upd[name] = _adamw_halves(w_shards[name], mine, other, m_shards[name], v_shards[name], c_arr, name="adamw_" + name)
    g_shards = {name: u[0] for name, u in upd.items()}

    def small8(gs, b):
        return jnp.concatenate(gs + [b.reshape(3, D_MODEL)], axis=0)

    sw = small8([g_pre_mix, g_post_mix, g_pre_ffn, g_post_ffn, g_mem], b_gate)
    sm = small8([m_g_pre_mix, m_g_post_mix, m_g_pre_ffn, m_g_post_ffn, m_g_mem], m_b_gate)
    sv = small8([v_g_pre_mix, v_g_post_mix, v_g_pre_ffn, v_g_post_ffn, v_g_mem], v_b_gate)
    s_upd = _adamw(sw, small, sm, sv, tm=8, name="adamw_small")

    def small_out(a):
        return [a[0:1], a[1:2], a[2:3], a[3:4], a[4:5]]

    order = ["w_in", "w_mem_kv", "w_br_sb", "w_br_dil", "w_br_mem", "w_gate", "b_gate", "w_o", "w_ffn_in", "w_ffn_out"]

    def leaves(small_arr, big):
        out = small_out(small_arr)
        for name in order:
            out.append(small_arr[5:8].reshape(1, 3 * D_MODEL) if name == "b_gate" else big[name][None])
        return out

    grads_out = leaves(small, g_shards)
    delta_out = leaves(s_upd[0], {n: u[1] for n, u in upd.items()})
    m_out = leaves(s_upd[1], {n: u[2] for n, u in upd.items()})
    v_out = leaves(s_upd[2], {n: u[3] for n, u in upd.items()})
    return (loss, grad_x, *grads_out, *delta_out, *m_out, *v_out)
```

```python
import jax
import jax.numpy as jnp
import numpy as np
from jax import lax
from jax.experimental import pallas as pl
from jax.experimental.pallas import tpu as pltpu

F32 = jnp.float32
BF16 = jnp.bfloat16
MESH = pl.DeviceIdType.MESH

D_MODEL = 1024
SEQ = 2048
HEAD_DIM = 64
SB_W = 512
DIL_W = 256
MEM_W = 512
MEM_LEN = 256
D_IN = 3 * SB_W + 9 * DIL_W + MEM_W
D_FF = 2816
DIL_D = (1, 4, 16)
ROPE_THETA = 10000.0
NORM_EPS = 1e-6
NEG_INF = -1e30
LANES = 128

ADAM_LR = 0.001
ADAM_B1 = 0.9
ADAM_B2 = 0.999
ADAM_EPS = 1e-08
ADAM_WD = 0.01
ADAM_STEP = 10

N_CHIPS = 4
PACK = (
    ("w_in", (1024, 1088), 1),
    ("w_mem_kv", (256, 1024), 0),
    ("w_br_sb", (512, 256), 1),
    ("w_br_dil", (256, 256), 1),
    ("w_br_mem", (512, 256), 1),
    ("w_gate", (1024, 768), 1),
    ("w_o", (256, 1024), 0),
    ("w_ffn_in", (1024, 1408), 1),
    ("w_ffn_out", (704, 1024), 0),
)


def _dot(a, b):
    return lax.dot_general(a, b, (((1,), (0,)), ((), ())), preferred_element_type=F32)


def _dot_nt(a, b):
    return lax.dot_general(a, b, (((1,), (1,)), ((), ())), preferred_element_type=F32)


def _dot_tn(a, b):
    return lax.dot_general(a, b, (((0,), (0,)), ((), ())), preferred_element_type=F32)


def _split_dot(x, u):
    hi = x.astype(BF16)
    lo = (x - hi.astype(F32)).astype(BF16)
    return _dot(hi, u) + _dot(lo, u)


V7X_VMEM_BUDGET = 44 * 2 ** 20


def _rows_that_fit(m, row_bytes, fixed_bytes):
    for tm in (1024, 512, 256, 128):
        if m % tm == 0 and fixed_bytes + tm * row_bytes <= V7X_VMEM_BUDGET:
            return tm
    return min(m, 128)


def _mm(pairs, *, nt, tn, out_dtypes, name, bias=None, extras=(), epilogue=None, side=None):
    pairs = [p if len(p) == 4 else (p[0], p[1], None, None) for p in pairs]
    m = pairs[0][0].shape[-2]
    b0 = pairs[0][1]
    if nt:
        n = b0.shape[-2]
    else:
        n = b0.shape[-1] * (b0.shape[0] if b0.ndim == 3 else 1)
    n_pairs, n_extra, n_out = len(pairs), len(extras), len(out_dtypes)
    assert n % tn == 0
    one_col = n == tn
    ks = [(b.shape[-1] if nt else b.shape[-2]) for _, b, _, _ in pairs]
    fixed = sum(k * tn * 2 for k in ks) * (1 if one_col else 2)
    row_bytes = 2 * sum(k * 2 for k in ks) + 2 * tn * (sum(jnp.dtype(dt).itemsize for dt in out_dtypes) + 2 * n_extra) + 2 * tn * 4
    tm = _rows_that_fit(m, row_bytes, fixed)
    assert m % tm == 0
    b_mode = dict(pipeline_mode=pl.Buffered(1)) if one_col else {}
    has_bias = bias is not None
    side_arrays = side[1] if side else []
    n_side = len(side_arrays)
    n_main_in = 2 * n_pairs + has_bias + n_extra
    n_steps = (n // tn) * (m // tm)

    def body(*refs):
        if n_side:
            step = pl.program_id(0) * (m // tm) + pl.program_id(1)
            finish = _run_side(side, refs[n_main_in:n_main_in + n_side],
                               refs[n_main_in + n_side + n_out:n_main_in + 2 * n_side + n_out],
                               refs[n_main_in + 2 * n_side + n_out:], step, n_steps)
        outs = refs[n_main_in + n_side:n_main_in + n_side + n_out]
        acc = None
        for i in range(n_pairs):
            a, b = refs[2 * i][...], refs[2 * i + 1][...]
            p = _dot_nt(a, b) if nt else _dot(a, b)
            acc = p if acc is None else acc + p
        pos = 2 * n_pairs
        if has_bias:
            acc = acc + refs[pos][...]
            pos += 1
        ex = [r[...] for r in refs[pos:pos + n_extra]]
        vals = (acc,) if epilogue is None else epilogue(acc, *ex)
        for r, v, dt in zip(outs, vals, out_dtypes):
            r[...] = v.astype(dt)
        if n_side:
            finish()

    in_specs, args = [], []
    for a, b, a_col, b_sel in pairs:
        k = b.shape[-1] if nt else b.shape[-2]
        assert a_col is not None or a.shape[1] == k
        if a.ndim == 3:
            in_specs.append(pl.BlockSpec((None, tm, k), lambda j, i, c=a_col: (c, i, 0)))
        else:
            in_specs.append(pl.BlockSpec((tm, k), lambda j, i, c=a_col or 0: (i, c)))
        if b.ndim == 2:
            in_specs.append(pl.BlockSpec((tn, k), lambda j, i: (j, 0), **b_mode) if nt
                            else pl.BlockSpec((k, tn), lambda j, i: (0, j), **b_mode))
        elif nt:
            in_specs.append(pl.BlockSpec((None, tn, k), lambda j, i, s=b_sel: (s, j, 0), **b_mode))
        else:
            assert b_sel == "j" and b.shape[-1] == tn
            in_specs.append(pl.BlockSpec((None, k, tn), lambda j, i: (j, 0, 0), **b_mode))
        args += [a, b]
    if has_bias:
        in_specs.append(pl.BlockSpec((1, tn), lambda j, i: (0, j)))
        args.append(bias)
    for e in extras:
        in_specs.append(pl.BlockSpec((tm, tn), lambda j, i: (i, j)))
        args.append(e)
    out = pl.pallas_call(
        body,
        grid=(n // tn, m // tm),
        in_specs=in_specs + [ANY] * n_side,
        out_specs=[pl.BlockSpec((tm, tn), lambda j, i: (i, j)) for _ in range(n_out)] + [ANY] * n_side,
        out_shape=[jax.ShapeDtypeStruct((m, n), dt) for dt in out_dtypes] + (_side_out_shapes(side) if n_side else []),
        scratch_shapes=_side_sems(side) if n_side else [],
        compiler_params=pltpu.CompilerParams(dimension_semantics=("arbitrary", "arbitrary") if n_side else ("parallel", "parallel")),
        name=name,
    )(*args, *side_arrays)
    if n_side:
        return (out[0] if n_out == 1 else out[:n_out]), out[n_out:]
    return out[0] if n_out == 1 else out


def _mm_tn(a, b, *, tm, tn, name, out_shards=False, slots=None, into=None, group=1, a_is_transposed=False):
    k, m = a.shape[::-1] if a_is_transposed else a.shape
    mul = _dot if a_is_transposed else _dot_tn
    b_shards = b.ndim == 3
    out_shards = out_shards or b_shards
    n = b.shape[0] * b.shape[2] if b_shards else b.shape[1]
    tk = _rows_that_fit(k, 2 * 2 * (tm + group * tn), 3 * tm * group * tn * 4)
    assert m % tm == 0 and n % (group * tn) == 0 and k % tk == 0 and (not b_shards or b.shape[2] == tn)
    assert group == 1 or out_shards
    total, first = slots if slots else (n // tn, 0)

    def body(a_ref, b_ref, *rest):
        o_ref = rest[-1]

        @pl.when(pl.program_id(2) == 0)
        def _():
            o_ref[...] = jnp.zeros_like(o_ref)

        if group == 1:
            o_ref[...] += mul(a_ref[...], b_ref[...])
        elif b_shards:
            a_blk = a_ref[...]
            for s in range(group):
                o_ref[s] += mul(a_blk, b_ref[s])
        else:
            acc = mul(a_ref[...], b_ref[...])
            for s in range(group):
                o_ref[s] += acc[:, s * tn:(s + 1) * tn]

    lead = None if group == 1 else group
    if b_shards:
        b_spec = pl.BlockSpec((lead, tk, tn), lambda i, j, kk: (j, kk, 0))
    else:
        b_spec = pl.BlockSpec((tk, group * tn), lambda i, j, kk: (kk, j))
    if out_shards:
        out_spec = pl.BlockSpec((lead, tm, tn), lambda i, j, kk: (j + first // group, i, 0))
        out_shape = jax.ShapeDtypeStruct((total, m, tn), F32)
    else:
        out_spec = pl.BlockSpec((tm, tn), lambda i, j, kk: (i, j))
        out_shape = jax.ShapeDtypeStruct((m, n), F32)
    return pl.pallas_call(
        body,
        grid=(m // tm, n // (group * tn), k // tk),
        in_specs=[pl.BlockSpec((tm, tk), lambda i, j, kk: (i, kk)) if a_is_transposed
                  else pl.BlockSpec((tk, tm), lambda i, j, kk: (kk, i)), b_spec] + ([ANY] if into is not None else []),
        out_specs=out_spec,
        out_shape=out_shape,
        input_output_aliases={2: 0} if into is not None else {},
        compiler_params=pltpu.CompilerParams(dimension_semantics=("parallel", "parallel", "arbitrary")),
        name=name,
    )(*([a, b] + ([into] if into is not None else [])))


def _rowwise(fn, ins, outs, *, tm, name, side=None):
    rows = next(a.shape[0] for a, kind in ins if kind == "row")
    tm = min(tm, rows)
    assert rows % tm == 0
    n_in, n_out = len(ins), len(outs)
    side_arrays = side[1] if side else []
    n_side = len(side_arrays)

    def body(*refs):
        if n_side:
            finish = _run_side(side, refs[n_in:n_in + n_side], refs[n_in + n_side + n_out:n_in + 2 * n_side + n_out],
                               refs[n_in + 2 * n_side + n_out:], pl.program_id(0), rows // tm)
        vals = fn(*[r[...] for r in refs[:n_in]])
        for (_, dt, kind), r, v in zip(outs, refs[n_in + n_side:n_in + n_side + n_out], vals):
            if kind == "row":
                r[...] = v.astype(dt)
            else:
                @pl.when(pl.program_id(0) == 0)
                def _(r=r):
                    r[...] = jnp.zeros_like(r)

                r[...] += v
        if n_side:
            finish()

    in_specs = [pl.BlockSpec((tm, a.shape[1]), lambda i: (i, 0)) if kind == "row" else pl.BlockSpec(a.shape, lambda i: (0, 0))
                for a, kind in ins]
    out_specs = [pl.BlockSpec((tm, c), lambda i: (i, 0)) if kind == "row" else pl.BlockSpec((1, c), lambda i: (0, 0))
                 for c, _, kind in outs]
    out_shape = [jax.ShapeDtypeStruct((rows if kind == "row" else 1, c), dt) for c, dt, kind in outs]
    ordered = n_side or any(kind == "acc" for _, _, kind in outs)
    return pl.pallas_call(
        body,
        grid=(rows // tm,),
        in_specs=in_specs + [ANY] * n_side,
        out_specs=out_specs + [ANY] * n_side,
        out_shape=out_shape + (_side_out_shapes(side) if n_side else []),
        scratch_shapes=_side_sems(side) if n_side else [],
        compiler_params=pltpu.CompilerParams(dimension_semantics=("arbitrary" if ordered else "parallel",)),
        name=name,
    )(*[a for a, _ in ins], *side_arrays)


def _rstd(x):
    return lax.rsqrt(jnp.mean(x * x, axis=-1, keepdims=True) + NORM_EPS)


def _norm_bwd(dout, xin, g):
    r = _rstd(xin)
    n = xin * r
    dn = dout * g
    dg = jnp.sum(dout * n, axis=0, keepdims=True)
    dx = r * (dn - n * jnp.mean(dn * n, axis=-1, keepdims=True))
    return dx, dg


def _sigmoid(x):
    return 0.5 * jnp.tanh(0.5 * x) + 0.5


def _norm_fwd(x, g, *, name, side=None):
    def fn(x, g):
        return ((x * _rstd(x)) * g,)

    out = _rowwise(fn, [(x, "row"), (g, "vec")], [(D_MODEL, BF16, "row")], tm=512, name=name, side=side)
    return (out[0], out[1:]) if side else out[0]


def _mid_fwd(mix, x, g_post_mix, g_pre_ffn):
    def fn(mix, x, g2, g3):
        x1 = x + (mix * _rstd(mix)) * g2
        return x1, (x1 * _rstd(x1)) * g3

    return _rowwise(fn, [(mix, "row"), (x, "row"), (g_post_mix, "vec"), (g_pre_ffn, "vec")],
                    [(D_MODEL, F32, "row"), (D_MODEL, BF16, "row")], tm=512, name="mid_fwd")


def _loss_bwd(f2, x1, g_post_ffn, tgt):
    def fn(f2, x1, g4, tgt):
        r = _rstd(f2)
        n = f2 * r
        err = x1 + n * g4 - tgt
        loss = 0.5 * jnp.sum(jnp.mean(err * err, axis=-1, keepdims=True), axis=0, keepdims=True)
        dy = err * (1.0 / D_MODEL)
        dn = dy * g4
        dg4 = jnp.sum(dy * n, axis=0, keepdims=True)
        df2 = r * (dn - n * jnp.mean(dn * n, axis=-1, keepdims=True))
        return dy, df2, dg4, jnp.broadcast_to(loss, (1, LANES))

    return _rowwise(fn, [(f2, "row"), (x1, "row"), (g_post_ffn, "vec"), (tgt, "row")],
                    [(D_MODEL, BF16, "row"), (D_MODEL, BF16, "row"), (D_MODEL, F32, "acc"), (LANES, F32, "acc")],
                    tm=512, name="loss_bwd")


def _mid_bwd(dh2, x1, mix, g_pre_ffn, g_post_mix, dy):
    def fn(dh2, x1, mix, g3, g2, dy):
        d3, dg3 = _norm_bwd(dh2.astype(F32), x1, g3)
        dx1 = dy.astype(F32) + d3
        dmix, dg2 = _norm_bwd(dx1, mix, g2)
        return dx1, dmix, dg3, dg2

    return _rowwise(fn, [(dh2, "row"), (x1, "row"), (mix, "row"), (g_pre_ffn, "vec"), (g_post_mix, "vec"), (dy, "row")],
                    [(D_MODEL, BF16, "row"), (D_MODEL, BF16, "row"), (D_MODEL, F32, "acc"), (D_MODEL, F32, "acc")],
                    tm=512, name="mid_bwd")


def _first_bwd(dh, x, g_pre_mix, dx1):
    def fn(dh, x, g1, dx1):
        d1, dg1 = _norm_bwd(dh.astype(F32), x, g1)
        return dx1.astype(F32) + d1, dg1

    return _rowwise(fn, [(dh, "row"), (x, "row"), (g_pre_mix, "vec"), (dx1, "row")],
                    [(D_MODEL, F32, "row"), (D_MODEL, F32, "acc")], tm=512, name="first_bwd")


def _mem_norm_bwd(dhm, mem, g_mem):
    def fn(dhm, mem, g):
        return (jnp.sum(dhm * (mem * _rstd(mem)), axis=0, keepdims=True),)

    return _rowwise(fn, [(dhm, "row"), (mem, "row"), (g_mem, "vec")], [(D_MODEL, F32, "acc")], tm=512, name="mem_norm_bwd")[0]


def _gate_bwd(dmerged, gates, ya, yb, yc):
    def fn(dm, gt, ya, yb, yc):
        dm = dm.astype(F32)
        gt = gt.astype(F32)
        outs, dgp = [], []
        for i, y in enumerate((ya, yb, yc)):
            gi = gt[:, i * D_MODEL:(i + 1) * D_MODEL]
            outs.append(dm * gi)
            dgp.append(dm * y.astype(F32) * gi * (1.0 - gi))
        dgpre = jnp.concatenate(dgp, axis=1)
        return outs[0], outs[1], outs[2], dgpre, jnp.sum(dgpre, axis=0, keepdims=True)

    return _rowwise(fn, [(dmerged, "row"), (gates, "row"), (ya, "row"), (yb, "row"), (yc, "row")],
                    [(D_MODEL, BF16, "row")] * 3 + [(3 * D_MODEL, BF16, "row"), (3 * D_MODEL, F32, "acc")],
                    tm=512, name="gate_bwd")


def _adamw_math(w, g, m, v):
    m = ADAM_B1 * m + (1.0 - ADAM_B1) * g
    v = ADAM_B2 * v + (1.0 - ADAM_B2) * (g * g)
    m_hat = m / (1.0 - ADAM_B1 ** ADAM_STEP)
    v_hat = v / (1.0 - ADAM_B2 ** ADAM_STEP)
    delta = -ADAM_LR * (m_hat / (jnp.sqrt(v_hat) + ADAM_EPS) + ADAM_WD * w)
    return delta, m, v


def _adamw(w, g, m, v, *, tm, name):
    c = w.shape[1]
    return _rowwise(_adamw_math, [(w, "row"), (g, "row"), (m, "row"), (v, "row")], [(c, F32, "row")] * 3, tm=tm, name=name)


def _adamw_halves(w, g_mine, g_theirs, m, v, c_arr, *, name):
    a, b = w.shape
    hf = a // 2
    tr = next(t for t in (hf, hf // 2, hf // 4) if 9 * 3 * t * b * 4 <= V7X_VMEM_BUDGET)

    def body(c_ref, w_ref, gm_ref, gt_ref, m_ref, v_ref, g_out, d_out, m_out, v_out):
        g = jnp.where(pl.program_id(0) == c_ref[0], gm_ref[...], gt_ref[...])
        d, m_new, v_new = _adamw_math(w_ref[...], g, m_ref[...], v_ref[...])
        g_out[...] = g
        d_out[...] = d
        m_out[...] = m_new
        v_out[...] = v_new

    full = pl.BlockSpec((tr, b), lambda hh, i, c_ref: (hh * (hf // tr) + i, 0))
    half = pl.BlockSpec((tr, b), lambda hh, i, c_ref: (i, 0))
    return pl.pallas_call(
        body,
        grid_spec=pltpu.PrefetchScalarGridSpec(
            num_scalar_prefetch=1,
            grid=(2, hf // tr),
            in_specs=[full, half, half, full, full],
            out_specs=[full] * 4,
        ),
        out_shape=[jax.ShapeDtypeStruct((a, b), F32)] * 4,
        compiler_params=pltpu.CompilerParams(dimension_semantics=("parallel", "parallel")),
        name=name,
    )(c_arr, w, g_mine, g_theirs, m, v)


def _ffn_in_fwd(h2, w_ffn):
    m, tm, tn = h2.shape[0], 512, w_ffn.shape[2]
    assert 2 * tn == D_FF

    def body(h_ref, wg_ref, wu_ref, g_ref, u_ref, f_ref):
        h = h_ref[...]
        g = _dot(h, wg_ref[...])
        u = _dot(h, wu_ref[...])
        g_ref[...] = g.astype(BF16)
        u_ref[...] = u.astype(BF16)
        f_ref[...] = (g * _sigmoid(g) * u).astype(BF16)

    o_spec = pl.BlockSpec((tm, tn), lambda j, i: (i, j))
    return pl.pallas_call(
        body,
        grid=(D_FF // tn, m // tm),
        in_specs=[pl.BlockSpec((tm, D_MODEL), lambda j, i: (i, 0)),
                  pl.BlockSpec((None, D_MODEL, tn), lambda j, i: (j, 0, 0)),
                  pl.BlockSpec((None, D_MODEL, tn), lambda j, i: (j + 2, 0, 0))],
        out_specs=[o_spec, o_spec, o_spec],
        out_shape=[jax.ShapeDtypeStruct((m, D_FF), BF16)] * 3,
        compiler_params=pltpu.CompilerParams(dimension_semantics=("parallel", "parallel")),
        name="ffn_in_fwd",
    )(h2, w_ffn, w_ffn)


def _join_shards(w4):
    _, rows, cols = w4.shape
    tr = rows // 4

    def body(w_ref, o_ref):
        for s in range(N_CHIPS):
            o_ref[:, s * cols:(s + 1) * cols] = w_ref[s]

    return pl.pallas_call(
        body,
        grid=(rows // tr,),
        in_specs=[pl.BlockSpec((N_CHIPS, tr, cols), lambda i: (0, i, 0))],
        out_specs=pl.BlockSpec((tr, N_CHIPS * cols), lambda i: (i, 0)),
        out_shape=jax.ShapeDtypeStruct((rows, N_CHIPS * cols), w4.dtype),
        compiler_params=pltpu.CompilerParams(dimension_semantics=("parallel",)),
        name="join_shards",
    )(w4)


def _split_to_shards(pieces, *, name):
    t = pieces[0].shape[0]
    widths = [p.shape[1] for p in pieces]
    cols = sum(widths) // N_CHIPS
    tm = 512
    plan, start = [], 0
    for p, wd in enumerate(widths):
        for s in range(N_CHIPS):
            lo, hi = max(start, s * cols), min(start + wd, (s + 1) * cols)
            if lo < hi:
                plan.append((s, p, lo - s * cols, hi - s * cols, lo - start, hi - start))
        start += wd

    def body(*refs):
        o_ref = refs[-1]
        for s, p, o_lo, o_hi, p_lo, p_hi in plan:
            o_ref[s, :, o_lo:o_hi] = refs[p][:, p_lo:p_hi]

    return pl.pallas_call(
        body,
        grid=(t // tm,),
        in_specs=[pl.BlockSpec((tm, wd), lambda i: (i, 0)) for wd in widths],
        out_specs=pl.BlockSpec((N_CHIPS, tm, cols), lambda i: (0, i, 0)),
        out_shape=jax.ShapeDtypeStruct((N_CHIPS, t, cols), pieces[0].dtype),
        compiler_params=pltpu.CompilerParams(dimension_semantics=("parallel",)),
        name=name,
    )(*pieces)


def _swiglu_bwd_epilogue(df, g, u):
    g = g.astype(F32)
    u = u.astype(F32)
    sg = _sigmoid(g)
    return df * u * (sg * (1.0 + g * (1.0 - sg))), df * (g * sg)


def _branch_merge_fwd(o_a, o_b, o_c, w_sb, w_dil, w_mem, gates):
    m, tm = o_a.shape[0], 512

    def body(oa_ref, ob_ref, oc_ref, wa_ref, wb_ref, wc_ref, gt_ref, ya_ref, yb_ref, yc_ref, mg_ref):
        def project(o_ref, w_ref):
            o = o_ref[...]
            return jnp.concatenate([_dot(o, w_ref[s]) for s in range(N_CHIPS)], axis=1)

        ya = project(oa_ref, wa_ref)
        yb = project(ob_ref, wb_ref)
        yc = project(oc_ref, wc_ref)
        gt = gt_ref[...].astype(F32)
        ya_ref[...] = ya.astype(BF16)
        yb_ref[...] = yb.astype(BF16)
        yc_ref[...] = yc.astype(BF16)
        mg_ref[...] = (gt[:, :D_MODEL] * ya + gt[:, D_MODEL:2 * D_MODEL] * yb + gt[:, 2 * D_MODEL:] * yc).astype(BF16)

    row = lambda c: pl.BlockSpec((tm, c), lambda i: (i, 0))
    full = lambda a: pl.BlockSpec(a.shape, lambda i: (0, 0, 0))
    return pl.pallas_call(
        body,
        grid=(m // tm,),
        in_specs=[row(SB_W), row(DIL_W), row(MEM_W), full(w_sb), full(w_dil), full(w_mem), row(3 * D_MODEL)],
        out_specs=[row(D_MODEL)] * 4,
        out_shape=[jax.ShapeDtypeStruct((m, D_MODEL), BF16)] * 4,
        compiler_params=pltpu.CompilerParams(dimension_semantics=("parallel",)),
        name="branch_merge_fwd",
    )(o_a, o_b, o_c, w_sb, w_dil, w_mem, gates)


SB_T = 256
SB_SCALE = HEAD_DIM ** -0.5


def _sb_masks():
    row = lax.broadcasted_iota(jnp.int32, (SB_T, SB_T), 0)
    col = lax.broadcasted_iota(jnp.int32, (SB_T, SB_T), 1)
    lane = lax.broadcasted_iota(jnp.int32, (1, LANES), 1)
    return row, col, lane


def _sb_logs(z):
    lb = jnp.minimum(z, 0.0) - jnp.log(1.0 + jnp.exp(-jnp.abs(z)))
    return lb, lb - z


def _sb_specs(n_heads_pairs, col0):
    q = pl.BlockSpec((None, SB_T, LANES), lambda b, p, i: (b, i, col0 + p))
    k = pl.BlockSpec((None, SEQ, LANES), lambda b, p, i: (b, 0, col0 + n_heads_pairs + p))
    v = pl.BlockSpec((None, SEQ, LANES), lambda b, p, i: (b, 0, col0 + 2 * n_heads_pairs + p))
    return q, k, v


def _sb_first_blocks(i):
    rem = lax.rem(i + 1, 3)
    return jnp.where(rem == 0, 3, rem)


def _grid_step(n_pairs, nq):
    return (pl.program_id(0) * n_pairs + pl.program_id(1)) * nq + pl.program_id(2)


def _sb_fwd(proj3, late_shards):
    bl = proj3.shape[0]
    n_pairs = SB_W // LANES
    nq = SEQ // SB_T
    n_late = len(late_shards)
    n_steps = bl * n_pairs * nq

    def body(q_ref, k_ref, v_ref, *rest):
        late_in, (o_ref, o32_ref, w_ref), late_out = rest[:n_late], rest[n_late:n_late + 3], rest[n_late + 3:2 * n_late + 3]
        step = _grid_step(n_pairs, nq)
        if n_late:
            send, forward, finish = _gather_phases(late_in, late_out, *rest[2 * n_late + 3:])
            pl.when(step == 0)(send)
            pl.when(step == n_steps // 2)(forward)
        i = pl.program_id(2)
        row, col, lane = _sb_masks()
        causal = col < row
        u_excl = (row > col).astype(BF16)
        q = q_ref[...]
        heads = []
        for h in range(2):
            mh = (lane // HEAD_DIM) == h
            heads.append((mh, jnp.where(mh, q, jnp.zeros_like(q)) * SB_SCALE))

        def blocks(js, diags, carries, acc):
            ks = [k_ref[pl.ds(pl.multiple_of(j * SB_T, SB_T), SB_T), :] for j in js]
            vs = [v_ref[pl.ds(pl.multiple_of(j * SB_T, SB_T), SB_T), :] for j in js]
            chains = [(b, h) for b in range(len(js)) for h in range(2)]
            z = {c: _dot_nt(heads[c[1]][1], ks[c[0]]) for c in chains}
            lb, lk = {}, {}
            for c in chains:
                lb[c], lk[c] = _sb_logs(z[c])
                if diags[c[0]]:
                    lk[c] = jnp.where(causal, lk[c], 0.0)
            r = {c: _split_dot(lk[c], u_excl) for c in chains}
            carries = list(carries)
            w = {}
            for b, h in chains:
                w_c = jnp.exp(lb[b, h] + r[b, h] + carries[h])
                w[b, h] = (jnp.where(causal, w_c, 0.0) if diags[b] else w_c).astype(BF16)
                w_ref[h, js[b]] = w[b, h]
                carries[h] = carries[h] + (r[b, h][:, :1] + lk[b, h][:, :1])
            for b, h in chains:
                acc = acc + _dot(w[b, h], jnp.where(heads[h][0], vs[b], jnp.zeros_like(vs[b])))
            return tuple(carries), acc

        zero = jnp.zeros((SB_T, 1), F32)
        init = ((zero, zero), jnp.zeros((SB_T, LANES), F32))
        first = _sb_first_blocks(i)
        carries, acc = lax.cond(
            first == 1, lambda: blocks([i], (True,), *init),
            lambda: lax.cond(first == 2, lambda: blocks([i, i - 1], (True, False), *init),
                             lambda: blocks([i, i - 1, i - 2], (True, False, False), *init)))
        rest = i - first
        carries, acc = lax.fori_loop(
            0, (rest + 1) // 3,
            lambda jj, c: blocks([rest - 3 * jj, rest - 1 - 3 * jj, rest - 2 - 3 * jj], (False, False, False), c[0], c[1]),
            (carries, acc))
        o_ref[...] = acc.astype(BF16)
        o32_ref[...] = acc
        if n_late:
            pl.when(step == n_steps - 1)(finish)

    q_spec, k_spec, v_spec = _sb_specs(n_pairs, 0)
    blk = pl.BlockSpec((None, SB_T, LANES), lambda b, p, i: (b, i, p))
    out = pl.pallas_call(
        body,
        grid=(bl, n_pairs, nq),
        in_specs=[q_spec, k_spec, v_spec] + [ANY] * n_late,
        out_specs=[blk, blk, _sb_weight_spec(nq)] + [ANY] * n_late,
        out_shape=[jax.ShapeDtypeStruct((bl, SEQ, SB_W), BF16), jax.ShapeDtypeStruct((bl, SEQ, SB_W), F32),
                   jax.ShapeDtypeStruct((bl, n_pairs, nq, 2, nq, SB_T, SB_T), BF16)] + _gather_out_shapes(late_shards),
        scratch_shapes=_gather_sems(n_late) if n_late else [],
        compiler_params=pltpu.CompilerParams(dimension_semantics=("arbitrary", "arbitrary", "arbitrary")),
        name="sb_fwd",
    )(proj3, proj3, proj3, *late_shards)
    return out[0], out[1], out[2], out[3:]


def _sb_weight_spec(nq):
    return pl.BlockSpec((None, None, None, 2, nq, SB_T, SB_T), lambda b, p, i: (b, p, i, 0, 0, 0, 0))


def _sb_bwd(proj3, o_a, do_a, w_all, parts):
    bl = proj3.shape[0]
    n_pairs = SB_W // LANES
    nq = SEQ // SB_T
    n_parts = len(parts)
    n_steps = bl * n_pairs * nq

    def body(q_ref, k_ref, v_ref, o_ref, do_ref, w_ref, *rest):
        p_refs, (dq_ref, dk_ref, dv_ref), land_refs = rest[:n_parts], rest[n_parts:n_parts + 3], rest[n_parts + 3:2 * n_parts + 3]
        dk_acc, dv_acc = rest[2 * n_parts + 3:2 * n_parts + 5]
        step = _grid_step(n_pairs, nq)
        if n_parts:
            send, finish = _chip_exchange_phases(p_refs, land_refs, *rest[2 * n_parts + 5:])
            pl.when(step == 0)(send)
        i = pl.program_id(2)

        @pl.when(i == 0)
        def _():
            dk_acc[...] = jnp.zeros_like(dk_acc)
            dv_acc[...] = jnp.zeros_like(dv_acc)

        row, col, lane = _sb_masks()
        causal = col < row
        u_incl = (row >= col).astype(BF16)
        q = q_ref[...]
        do = do_ref[...]
        prod = do.astype(F32) * o_ref[...]
        heads = []
        for h in range(2):
            mh = (lane // HEAD_DIM) == h
            d_tot = jnp.sum(jnp.where(mh, prod, 0.0), axis=1, keepdims=True)
            heads.append((mh, jnp.where(mh, q, jnp.zeros_like(q)) * SB_SCALE, jnp.where(mh, do, jnp.zeros_like(do)), d_tot))

        def blocks(js, diags, c_das, dq):
            starts = [pl.multiple_of(j * SB_T, SB_T) for j in js]
            ks = [k_ref[pl.ds(s, SB_T), :] for s in starts]
            vs = [v_ref[pl.ds(s, SB_T), :] for s in starts]
            chains = [(b, h) for b in range(len(js)) for h in range(2)]
            z = {c: _dot_nt(heads[c[1]][1], ks[c[0]]) for c in chains}
            dw = {c: _dot_nt(heads[c[1]][2], vs[c[0]]) for c in chains}
            wb = {(b, h): w_ref[h, js[b]] for b, h in chains}
            da = {c: dw[c] * wb[c].astype(F32) for c in chains}
            sfx = {c: _split_dot(da[c], u_incl) for c in chains}
            c_das = list(c_das)
            dz = {}
            for b, h in chains:
                dlk = heads[h][3] - c_das[h] - sfx[b, h]
                if diags[b]:
                    dlk = jnp.where(causal, dlk, 0.0)
                c_das[h] = c_das[h] + sfx[b, h][:, :1]
                e = jnp.exp(-jnp.abs(z[b, h]))
                inv = 1.0 / (1.0 + e)
                pos = z[b, h] >= 0.0
                beta = jnp.where(pos, inv, e * inv)
                one_m_beta = jnp.where(pos, e * inv, inv)
                dz[b, h] = (da[b, h] * one_m_beta - dlk * beta).astype(BF16)
            for b, h in chains:
                dq = dq + _dot(dz[b, h], jnp.where(heads[h][0], ks[b], jnp.zeros_like(ks[b])))
            for b in range(len(js)):
                dk_acc[pl.ds(starts[b], SB_T), :] += _dot_tn(dz[b, 0], heads[0][1]) + _dot_tn(dz[b, 1], heads[1][1])
                dv_acc[pl.ds(starts[b], SB_T), :] += _dot_tn(wb[b, 0], heads[0][2]) + _dot_tn(wb[b, 1], heads[1][2])
            return tuple(c_das), dq

        zero = jnp.zeros((SB_T, 1), F32)
        init = ((zero, zero), jnp.zeros((SB_T, LANES), F32))
        first = _sb_first_blocks(i)
        state = lax.cond(
            first == 1, lambda: blocks([i], (True,), *init),
            lambda: lax.cond(first == 2, lambda: blocks([i, i - 1], (True, False), *init),
                             lambda: blocks([i, i - 1, i - 2], (True, False, False), *init)))
        rest = i - first
        state = lax.fori_loop(
            0, (rest + 1) // 3,
            lambda jj, c: blocks([rest - 3 * jj, rest - 1 - 3 * jj, rest - 2 - 3 * jj], (False, False, False), c[0], c[1]), state)
        dq_ref[...] = (state[1] * SB_SCALE).astype(BF16)

        @pl.when(i == nq - 1)
        def _():
            dk_ref[...] = dk_acc[...].astype(BF16)
            dv_ref[...] = dv_acc[...].astype(BF16)

        if n_parts:
            pl.when(step == n_steps - 1)(finish)

    q_spec, k_spec, v_spec = _sb_specs(n_pairs, 0)
    blk = pl.BlockSpec((None, SB_T, LANES), lambda b, p, i: (b, i, p))
    seq = pl.BlockSpec((None, SEQ, LANES), lambda b, p, i: (b, 0, p))
    shape = jax.ShapeDtypeStruct((bl, SEQ, SB_W), BF16)
    out = pl.pallas_call(
        body,
        grid=(bl, n_pairs, nq),
        in_specs=[q_spec, k_spec, v_spec, blk, blk, _sb_weight_spec(nq)] + [ANY] * n_parts,
        out_specs=[blk, seq, seq] + [ANY] * n_parts,
        out_shape=[shape, shape, shape] + [jax.ShapeDtypeStruct(p.shape, p.dtype) for p in parts],
        scratch_shapes=[pltpu.VMEM((SEQ, LANES), F32), pltpu.VMEM((SEQ, LANES), F32)]
        + (_chip_exchange_sems(n_parts) if n_parts else []),
        compiler_params=pltpu.CompilerParams(dimension_semantics=("arbitrary", "arbitrary", "arbitrary")),
        name="sb_bwd",
    )(proj3, proj3, proj3, o_a, do_a, w_all, *parts)
    return out[0], out[1], out[2], out[3:]


BAND = 128


BAND_HEADS = DIL_W // HEAD_DIM


def _swap_half(x):
    n = x.shape[-1]
    lane = lax.broadcasted_iota(jnp.int32, (1, n), 1)
    return jnp.where((lane % HEAD_DIM) < HEAD_DIM // 2, pltpu.roll(x, n - HEAD_DIM // 2, 1), pltpu.roll(x, HEAD_DIM // 2, 1))


def _rope(x, cos, sin_signed):
    x = x.astype(F32)
    return x * cos + _swap_half(x) * sin_signed


def _band_delta(do_b, o_b, lse_b):
    def fn(do, o, lse):
        lane_in = lax.broadcasted_iota(jnp.int32, (DIL_W, LANES), 0)
        col = lax.broadcasted_iota(jnp.int32, (DIL_W, LANES), 1)
        sum_head = ((lane_in // HEAD_DIM == col - BAND_HEADS) & (col >= BAND_HEADS) & (col < 2 * BAND_HEADS)).astype(BF16)
        return (lse + _split_dot(do.astype(F32) * o.astype(F32), sum_head),)

    return _rowwise(fn, [(do_b, "row"), (o_b, "row"), (lse_b, "row")], [(LANES, F32, "row")], tm=512, name="band_delta")[0]


def _band_masks():
    qi = lax.broadcasted_iota(jnp.int32, (BAND, 2 * BAND), 0) + BAND
    kj = lax.broadcasted_iota(jnp.int32, (BAND, 2 * BAND), 1)
    dist = qi - kj
    row = lax.broadcasted_iota(jnp.int32, (BAND, BAND), 0)
    col = lax.broadcasted_iota(jnp.int32, (BAND, BAND), 1)
    return col <= row, (dist >= 0) & (dist <= BAND)


def _band_attend(q, k, v, valid):
    lane = lax.broadcasted_iota(jnp.int32, (1, DIL_W), 1)
    stat_lane = lax.broadcasted_iota(jnp.int32, (1, LANES), 1)
    hs = range(BAND_HEADS)
    mh = [(lane // HEAD_DIM) == h for h in hs]
    s = [jnp.where(valid, _dot_nt(jnp.where(mh[h], q, jnp.zeros_like(q)), k), NEG_INF) for h in hs]
    m = [jnp.max(s[h], axis=1, keepdims=True) for h in hs]
    p = [jnp.exp(s[h] - m[h]) for h in hs]
    den = [jnp.sum(p[h], axis=1, keepdims=True) for h in hs]
    pv = [_dot(p[h].astype(BF16), jnp.where(mh[h], v, jnp.zeros_like(v))) for h in hs]
    o = jnp.zeros((BAND, DIL_W), F32)
    lse = jnp.zeros((BAND, LANES), F32)
    for h in hs:
        o = o + pv[h] * (1.0 / den[h])
        lse = jnp.where(stat_lane == h, m[h] + jnp.log(den[h]), lse)
    return o, lse


def _band_attend_bwd(q, k, v, valid, do, st):
    lane = lax.broadcasted_iota(jnp.int32, (1, DIL_W), 1)
    hs = range(BAND_HEADS)
    mh = [(lane // HEAD_DIM) == h for h in hs]
    qh = [jnp.where(mh[h], q, jnp.zeros_like(q)) for h in hs]
    doh = [jnp.where(mh[h], do, jnp.zeros_like(do)) for h in hs]
    s = [_dot_nt(qh[h], k) for h in hs]
    dp = [_dot_nt(doh[h], v) for h in hs]
    p = [jnp.where(valid, jnp.exp(s[h] - st[:, h:h + 1]), 0.0) for h in hs]
    ds = [(p[h] * (dp[h] - st[:, BAND_HEADS + h:BAND_HEADS + h + 1])).astype(BF16) for h in hs]
    pb = [p[h].astype(BF16) for h in hs]
    dq = sum(_dot(ds[h], jnp.where(mh[h], k, jnp.zeros_like(k))) for h in hs)
    dk = sum(_dot_tn(ds[h], qh[h]) for h in hs)
    dv = sum(_dot_tn(pb[h], doh[h]) for h in hs)
    return dq, dk, dv


def _band_group_specs(lead, rows, cls, col0):
    def spec(width):
        if lead == "rows":
            return pl.BlockSpec((None, rows, width), lambda b, i: (b, 0, col0))
        return pl.BlockSpec((None, rows, cls * width), lambda b, i: (b, 0, i))
    return spec


def _band_group_fwd(a, cos_g, sin_g, *, rows, cls, steps, col0, name):
    bl = a.shape[0]
    nb = rows // BAND
    grp_w = 3 * DIL_W

    def body(a_ref, c_ref, s_ref, o_ref, l_ref, qr_all, kr_all):
        first_valid, later_valid = _band_masks()
        for j in range(cls):
            qr, kr = qr_all.at[j], kr_all.at[j]
            a0, t0, s0 = j * grp_w, j * DIL_W, j * LANES
            cos, sin = c_ref[:, t0:t0 + DIL_W], s_ref[:, t0:t0 + DIL_W]
            qr[...] = (_rope(a_ref[:, a0:a0 + DIL_W], cos, sin) * SB_SCALE).astype(BF16)
            kr[...] = _rope(a_ref[:, a0 + DIL_W:a0 + 2 * DIL_W], cos, sin).astype(BF16)

            def block(q0, k0, keys, valid, a0=a0, t0=t0, s0=s0):
                o, lse = _band_attend(qr[pl.ds(q0, BAND), :], kr[pl.ds(k0, keys), :],
                                      a_ref[pl.ds(k0, keys), a0 + 2 * DIL_W:a0 + grp_w], valid)
                o_ref[pl.ds(q0, BAND), t0:t0 + DIL_W] = o.astype(BF16)
                l_ref[pl.ds(q0, BAND), s0:s0 + LANES] = lse

            block(0, 0, BAND, first_valid)
            if nb > 1:
                def later(b, carry, block=block):
                    block(pl.multiple_of(b * BAND, BAND), pl.multiple_of((b - 1) * BAND, BAND), 2 * BAND, later_valid)
                    return carry

                lax.fori_loop(1, nb, later, 0, unroll=5 if (nb - 1) % 5 == 0 else 3)

    lead = "rows" if col0 is not None else "cols"
    spec = _band_group_specs(lead, rows, cls, col0)
    tab = pl.BlockSpec((rows, cls * DIL_W), lambda b, i: (0, i))
    n_cls = cos_g.shape[1] // DIL_W
    return pl.pallas_call(
        body,
        grid=(bl, steps),
        in_specs=[spec(grp_w), tab, tab],
        out_specs=[pl.BlockSpec((None, rows, cls * DIL_W), lambda b, i: (b, 0, i)),
                   pl.BlockSpec((None, rows, cls * LANES), lambda b, i: (b, 0, i))],
        out_shape=[jax.ShapeDtypeStruct((bl, rows, n_cls * DIL_W), BF16), jax.ShapeDtypeStruct((bl, rows, n_cls * LANES), F32)],
        scratch_shapes=[pltpu.VMEM((cls, rows, DIL_W), BF16), pltpu.VMEM((cls, rows, DIL_W), BF16)],
        compiler_params=pltpu.CompilerParams(dimension_semantics=("parallel", "parallel")),
        name=name,
    )(a, cos_g, sin_g)


def _band_group_bwd(a, do, st, cos_g, sin_g, *, rows, cls, steps, col0, name, side=None):
    bl = a.shape[0]
    nb = rows // BAND
    grp_w = 3 * DIL_W
    side_arrays = side[1] if side else []
    n_side = len(side_arrays)

    def body(a_ref, do_ref, st_ref, c_ref, s_ref, *rest):
        out_ref = rest[n_side]
        scratch = rest[2 * n_side + 1:2 * n_side + 5]
        if n_side:
            step = pl.program_id(0) * steps + pl.program_id(1)
            finish = _run_side(side, rest[:n_side], rest[n_side + 1:2 * n_side + 1], rest[2 * n_side + 5:], step, bl * steps)
        first_valid, later_valid = _band_masks()
        for j in range(cls):
            qr, kr, dk_acc, dv_acc = (s.at[j] for s in scratch)
            a0, t0 = j * grp_w, j * DIL_W
            cos, sin = c_ref[:, t0:t0 + DIL_W], s_ref[:, t0:t0 + DIL_W]
            qr[...] = (_rope(a_ref[:, a0:a0 + DIL_W], cos, sin) * SB_SCALE).astype(BF16)
            kr[...] = _rope(a_ref[:, a0 + DIL_W:a0 + 2 * DIL_W], cos, sin).astype(BF16)
            dk_acc[...] = jnp.zeros_like(dk_acc)
            dv_acc[...] = jnp.zeros_like(dv_acc)

            def block(q0, k0, keys, valid, a0=a0, t0=t0, s0=j * LANES):
                qrows, krows = pl.ds(q0, BAND), pl.ds(k0, keys)
                dq, dk, dv = _band_attend_bwd(
                    qr[qrows, :], kr[krows, :], a_ref[krows, a0 + 2 * DIL_W:a0 + grp_w], valid,
                    do_ref[qrows, t0:t0 + DIL_W], st_ref[qrows, s0:s0 + LANES])
                dq = dq * SB_SCALE
                out_ref[qrows, a0:a0 + DIL_W] = (dq * c_ref[qrows, t0:t0 + DIL_W]
                                                 - _swap_half(dq) * s_ref[qrows, t0:t0 + DIL_W]).astype(BF16)
                dk_acc[krows, :] += dk
                dv_acc[krows, :] += dv

            block(0, 0, BAND, first_valid)
            if nb > 1:
                def later(b, carry, block=block):
                    block(pl.multiple_of(b * BAND, BAND), pl.multiple_of((b - 1) * BAND, BAND), 2 * BAND, later_valid)
                    return carry

                lax.fori_loop(1, nb, later, 0, unroll=5 if (nb - 1) % 5 == 0 else 3)
            dk = dk_acc[...]
            out_ref[:, a0 + DIL_W:a0 + 2 * DIL_W] = (dk * cos - _swap_half(dk) * sin).astype(BF16)
            out_ref[:, a0 + 2 * DIL_W:a0 + grp_w] = dv_acc[...].astype(BF16)
        if n_side:
            finish()

    lead = "rows" if col0 is not None else "cols"
    spec = _band_group_specs(lead, rows, cls, col0)
    dspec = _band_group_specs(lead, rows, cls, 0 if col0 is not None else None)
    tab = pl.BlockSpec((rows, cls * DIL_W), lambda b, i: (0, i))
    n_cls = cos_g.shape[1] // DIL_W
    out = pl.pallas_call(
        body,
        grid=(bl, steps),
        in_specs=[spec(grp_w), dspec(DIL_W), dspec(LANES), tab, tab] + [ANY] * n_side,
        out_specs=[pl.BlockSpec((None, rows, cls * grp_w), lambda b, i: (b, 0, i))] + [ANY] * n_side,
        out_shape=[jax.ShapeDtypeStruct((bl, rows, n_cls * grp_w), BF16)] + (_side_out_shapes(side) if n_side else []),
        scratch_shapes=[pltpu.VMEM((cls, rows, DIL_W), BF16), pltpu.VMEM((cls, rows, DIL_W), BF16),
                        pltpu.VMEM((cls, rows, DIL_W), F32), pltpu.VMEM((cls, rows, DIL_W), F32)] + (_side_sems(side) if n_side else []),
        compiler_params=pltpu.CompilerParams(dimension_semantics=("arbitrary", "arbitrary") if n_side else ("parallel", "parallel")),
        name=name,
    )(a, do, st, cos_g, sin_g, *side_arrays)
    return (out[0], out[1:]) if n_side else out[0]


def _band_merge3(groups):
    t, tm = groups[0][0].shape[0], 512

    def body(o0, l0, o1, l1, o2, l2, ob_ref, lse_ref):
        a, b, c = l0[...], l1[...], l2[...]
        m = jnp.maximum(jnp.maximum(a, b), c)
        lse = m + jnp.log(jnp.exp(a - m) + jnp.exp(b - m) + jnp.exp(c - m))
        lane = lax.broadcasted_iota(jnp.int32, (1, DIL_W), 1)
        acc = jnp.zeros((tm, DIL_W), F32)
        for o_ref, l in ((o0, a), (o1, b), (o2, c)):
            share = jnp.exp(l - lse)
            spread = jnp.zeros((tm, DIL_W), F32)
            for h in range(BAND_HEADS):
                spread = jnp.where(lane // HEAD_DIM == h, share[:, h:h + 1], spread)
            acc = acc + spread * o_ref[...].astype(F32)
        ob_ref[...] = acc.astype(BF16)
        stat_lane = lax.broadcasted_iota(jnp.int32, (1, LANES), 1)
        lse_ref[...] = jnp.where(stat_lane < BAND_HEADS, lse, 0.0)

    spec = pl.BlockSpec((tm, DIL_W), lambda i: (i, 0))
    spec_l = pl.BlockSpec((tm, LANES), lambda i: (i, 0))
    return pl.pallas_call(
        body,
        grid=(t // tm,),
        in_specs=[spec, spec_l] * 3,
        out_specs=[spec, spec_l],
        out_shape=[jax.ShapeDtypeStruct((t, DIL_W), BF16), jax.ShapeDtypeStruct((t, LANES), F32)],
        compiler_params=pltpu.CompilerParams(dimension_semantics=("parallel",)),
        name="band_merge",
    )(*[a for g in groups for a in g])


MEM_T = 512
MEM_SCALE = 128 ** -0.5
MEM_Q_COL = (D_IN - MEM_W) // LANES


MEM_HEADS = MEM_W // LANES


def _mem_specs():
    qs = [pl.BlockSpec((None, MEM_T, LANES), lambda b, i, h=h: (b, i, MEM_Q_COL + h)) for h in range(MEM_HEADS)]
    kv = pl.BlockSpec((None, MEM_LEN, 2 * MEM_W), lambda b, i: (b, 0, 0))
    blk = pl.BlockSpec((None, MEM_T, MEM_W), lambda b, i: (b, i, 0))
    return qs, kv, blk


def _mem_probs(q, k):
    s = _dot_nt(q, k) * MEM_SCALE
    p = jnp.exp(s - jnp.max(s, axis=1, keepdims=True))
    return p * (1.0 / jnp.sum(p, axis=1, keepdims=True))


def _head_cols(h, base=0):
    return slice(base + h * LANES, base + (h + 1) * LANES)


def _mem_fwd(proj3, kv3):
    bl = proj3.shape[0]
    hs = range(MEM_HEADS)

    def body(*refs):
        q_refs, kv_ref, o_ref = refs[:MEM_HEADS], refs[MEM_HEADS], refs[MEM_HEADS + 1]
        p = [_mem_probs(q_refs[h][...], kv_ref[:, _head_cols(h)]) for h in hs]
        for h in hs:
            o_ref[:, _head_cols(h)] = _dot(p[h].astype(BF16), kv_ref[:, _head_cols(h, MEM_W)]).astype(BF16)

    qs, kv, blk = _mem_specs()
    return pl.pallas_call(
        body,
        grid=(bl, SEQ // MEM_T),
        in_specs=qs + [kv],
        out_specs=blk,
        out_shape=jax.ShapeDtypeStruct((bl, SEQ, MEM_W), BF16),
        compiler_params=pltpu.CompilerParams(dimension_semantics=("parallel", "parallel")),
        name="mem_fwd",
    )(*([proj3] * MEM_HEADS), kv3)


def _mem_bwd(proj3, kv3, do_c):
    bl = proj3.shape[0]
    hs = range(MEM_HEADS)

    def body(*refs):
        q_refs, kv_ref, do_ref, dq_ref, dkv_ref = refs[:MEM_HEADS], *refs[MEM_HEADS:MEM_HEADS + 4]

        @pl.when(pl.program_id(1) == 0)
        def _():
            dkv_ref[...] = jnp.zeros_like(dkv_ref)

        q = [q_refs[h][...] for h in hs]
        do = [do_ref[:, _head_cols(h)] for h in hs]
        p = [_mem_probs(q[h], kv_ref[:, _head_cols(h)]) for h in hs]
        dp = [_dot_nt(do[h], kv_ref[:, _head_cols(h, MEM_W)]) for h in hs]
        ds = [(p[h] * (dp[h] - jnp.sum(p[h] * dp[h], axis=1, keepdims=True)) * MEM_SCALE).astype(BF16) for h in hs]
        for h in hs:
            dq_ref[:, _head_cols(h)] = _dot(ds[h], kv_ref[:, _head_cols(h)]).astype(BF16)
            dkv_ref[:, _head_cols(h)] += _dot_tn(ds[h], q[h])
            dkv_ref[:, _head_cols(h, MEM_W)] += _dot_tn(p[h].astype(BF16), do[h])

    qs, kv, blk = _mem_specs()
    return pl.pallas_call(
        body,
        grid=(bl, SEQ // MEM_T),
        in_specs=qs + [kv, blk],
        out_specs=[blk, kv],
        out_shape=[jax.ShapeDtypeStruct((bl, SEQ, MEM_W), BF16), jax.ShapeDtypeStruct((bl, MEM_LEN, 2 * MEM_W), F32)],
        compiler_params=pltpu.CompilerParams(dimension_semantics=("parallel", "arbitrary")),
        name="mem_bwd",
    )(*([proj3] * MEM_HEADS), kv3, do_c)


def _place():
    x, y, c = lax.axis_index("x"), lax.axis_index("y"), lax.axis_index("c")
    return x, y, c


def _other_chips(x, y):
    return [(1 - x, y), (x, 1 - y), (1 - x, 1 - y)]


def _remote(src, dst, send_sem, recv_sem, to):
    return pltpu.make_async_remote_copy(src_ref=src, dst_ref=dst, send_sem=send_sem, recv_sem=recv_sem,
                                        device_id=to, device_id_type=MESH)


ANY = pl.BlockSpec(memory_space=pl.ANY)


def _gather_out_shapes(shards):
    return [jax.ShapeDtypeStruct((N_CHIPS,) + s.shape, s.dtype) for s in shards]


def _gather_sems(n):
    return [pltpu.SemaphoreType.DMA((6 * n,)), pltpu.SemaphoreType.DMA((6 * n,))]


def _gather_phases(in_refs, out_refs, send_sems, recv_sems):
    x, y, c = _place()
    sibling = (x, y, 1 - c)
    chips = _other_chips(x, y)
    first, passed = [], []
    for k in range(len(in_refs)):
        hf = in_refs[k].shape[0] // 2

        def half(px, py, pc, k=k, hf=hf):
            return out_refs[k].at[2 * px + py, pl.ds(pc * hf, hf), :]

        src = in_refs[k].at[pl.ds(c * hf, hf), :]
        for j, chip in enumerate(chips):
            s = 6 * k + j
            first.append(_remote(src, half(x, y, c), send_sems.at[s], recv_sems.at[s], (*chip, c)))
            passed.append((_remote(src, half(*chip, c), send_sems.at[s], recv_sems.at[s], (*chip, c)),
                           _remote(half(*chip, c), half(*chip, c), send_sems.at[s + 3], recv_sems.at[s + 3], sibling),
                           _remote(src, half(*chip, 1 - c), send_sems.at[s + 3], recv_sems.at[s + 3], sibling)))

    def send():
        for cp in first:
            cp.start()

    def forward():
        for landed, fwd, _ in passed:
            landed.wait_recv()
            fwd.start()

    def finish():
        for _, _, from_sibling in passed:
            from_sibling.wait_recv()
        for cp in first:
            cp.wait_send()
        for _, fwd, _ in passed:
            fwd.wait_send()

    return send, forward, finish


def _pair_exchange(grads, *, name):
    n = len(grads)
    side = ("pair", grads)

    def body(*refs):
        send, _, finish = _side_phases(side, refs[:n], refs[n:2 * n], refs[2 * n:])
        send()
        finish()

    return pl.pallas_call(
        body,
        in_specs=[ANY] * n,
        out_specs=[ANY] * n,
        out_shape=_side_out_shapes(side),
        scratch_shapes=_side_sems(side),
        name=name,
    )(*grads)


def _pair_exchange_phases(g_refs, land_refs, send_sems, recv_sems):
    x, y, c = _place()
    cps = []
    for k in range(len(g_refs)):
        hf = g_refs[k].shape[1] // 2
        src = g_refs[k].at[:, pl.ds((1 - c) * hf, hf), :]
        cps.append(_remote(src, land_refs[k], send_sems.at[k], recv_sems.at[k], (x, y, 1 - c)))

    def send():
        for cp in cps:
            cp.start()

    def finish():
        for cp in cps:
            cp.wait()

    return send, finish


def _side_out_shapes(side):
    kind, arrays = side
    if kind == "gather":
        return _gather_out_shapes(arrays)
    if kind == "pair":
        return [jax.ShapeDtypeStruct((N_CHIPS, g.shape[1] // 2, g.shape[2]), g.dtype) for g in arrays]
    return [jax.ShapeDtypeStruct(p.shape, p.dtype) for p in arrays]


def _side_sems(side):
    kind, arrays = side
    n = len(arrays)
    if kind == "gather":
        return _gather_sems(n)
    if kind == "pair":
        return [pltpu.SemaphoreType.DMA((n,)), pltpu.SemaphoreType.DMA((n,))]
    return _chip_exchange_sems(n)


def _side_phases(side, in_refs, out_refs, sems):
    kind = side[0]
    if kind == "gather":
        return _gather_phases(in_refs, out_refs, *sems)
    send, finish = (_pair_exchange_phases if kind == "pair" else _chip_exchange_phases)(in_refs, out_refs, *sems)
    return send, None, finish


def _run_side(side, in_refs, out_refs, sems, step, n_steps):
    first, mid, last = _side_phases(side, in_refs, out_refs, sems)
    pl.when(step == 0)(first)
    if mid is not None:
        pl.when(step == n_steps // 2)(mid)
    return lambda: pl.when(step == n_steps - 1)(last)


def _pair_add(g, land, c_arr, *, name):
    _, a, b = g.shape
    hf = a // 2

    def body(c_ref, g_ref, l_ref, o_ref):
        o_ref[...] = (g_ref[...] + l_ref[...]).astype(BF16)

    return pl.pallas_call(
        body,
        grid_spec=pltpu.PrefetchScalarGridSpec(
            num_scalar_prefetch=1,
            grid=(N_CHIPS,),
            in_specs=[pl.BlockSpec((None, None, hf, b), lambda s, c_ref: (s, c_ref[0], 0, 0)),
                      pl.BlockSpec((None, hf, b), lambda s, c_ref: (s, 0, 0))],
            out_specs=pl.BlockSpec((None, hf, b), lambda s, c_ref: (s, 0, 0)),
        ),
        out_shape=jax.ShapeDtypeStruct((N_CHIPS, hf, b), BF16),
        compiler_params=pltpu.CompilerParams(dimension_semantics=("parallel",)),
        name=name,
    )(c_arr, g.reshape(N_CHIPS, 2, hf, b), land)


def _chip_exchange_sems(n):
    return [pltpu.SemaphoreType.DMA((3 * n,)), pltpu.SemaphoreType.DMA((3 * n,))]


def _chip_exchange_phases(p_refs, land_refs, send_sems, recv_sems):
    x, y, c = _place()
    me = 2 * x + y
    sends, recvs = [], []
    for k in range(len(p_refs)):
        for j, (cx, cy) in enumerate(_other_chips(x, y)):
            s = 3 * k + j
            sends.append(_remote(p_refs[k].at[2 * cx + cy], land_refs[k].at[me], send_sems.at[s], recv_sems.at[s], (cx, cy, c)))
            recvs.append(_remote(p_refs[k].at[me], land_refs[k].at[2 * cx + cy], send_sems.at[s], recv_sems.at[s], (cx, cy, c)))

    def send():
        for cp in sends:
            cp.start()

    def finish():
        for cp in recvs:
            cp.wait_recv()
        for cp in sends:
            cp.wait_send()

    return send, finish


def _chip_add(land, part, me_arr, *, name):
    _, r, b = land.shape

    def body(me_ref, p_ref, l1_ref, l2_ref, l3_ref, o_ref):
        o_ref[...] = ((p_ref[...].astype(F32) + l1_ref[...].astype(F32)) + l2_ref[...].astype(F32)) + l3_ref[...].astype(F32)

    tr = r // 2
    other = lambda j: pl.BlockSpec((None, tr, b), lambda i, me_ref: (jnp.bitwise_xor(me_ref[0], j), i, 0))
    return pl.pallas_call(
        body,
        grid_spec=pltpu.PrefetchScalarGridSpec(
            num_scalar_prefetch=1,
            grid=(r // tr,),
            in_specs=[pl.BlockSpec((None, tr, b), lambda i, me_ref: (me_ref[0], i, 0)), other(2), other(1), other(3)],
            out_specs=pl.BlockSpec((tr, b), lambda i, me_ref: (i, 0)),
        ),
        out_shape=jax.ShapeDtypeStruct((r, b), F32),
        compiler_params=pltpu.CompilerParams(dimension_semantics=("parallel",)),
        name=name,
    )(me_arr, part, land, land, land)


def _pair_share(halves):
    n = len(halves)

    def body(*refs):
        h_refs, out_refs = refs[:n], refs[n:2 * n]
        send_sems, recv_sems = refs[2 * n:]
        x, y, c = _place()
        cps = [_remote(h_refs[k], out_refs[k], send_sems.at[k], recv_sems.at[k], (x, y, 1 - c)) for k in range(n)]
        for cp in cps:
            cp.start()
        for cp in cps:
            cp.wait()

    return pl.pallas_call(
        body,
        in_specs=[ANY] * n,
        out_specs=[ANY] * n,
        out_shape=[jax.ShapeDtypeStruct(h.shape, F32) for h in halves],
        scratch_shapes=[pltpu.SemaphoreType.DMA((n,)), pltpu.SemaphoreType.DMA((n,))],
        name="pair_share",
    )(*halves)


def _all_sum_small(part):
    def body(p_ref, o_ref, slots, send_sems, recv_sems):
        x, y, c = _place()
        me = 4 * x + 2 * y + c
        slots[me] = p_ref[...]
        peers = [(x ^ dx, y ^ dy, c ^ dc) for dx in (0, 1) for dy in (0, 1) for dc in (0, 1)][1:]
        sends = [_remote(p_ref, slots.at[me], send_sems.at[k], recv_sems.at[k], peer) for k, peer in enumerate(peers)]
        for cp in sends:
            cp.start()
        for k, (px, py, pc) in enumerate(peers):
            _remote(p_ref, slots.at[4 * px + 2 * py + pc], send_sems.at[k], recv_sems.at[k], (px, py, pc)).wait_recv()
        for cp in sends:
            cp.wait_send()
        acc = slots[0]
        for d in range(1, 8):
            acc = acc + slots[d]
        o_ref[...] = acc

    vmem = pl.BlockSpec(memory_space=pltpu.VMEM)
    return pl.pallas_call(
        body,
        in_specs=[vmem],
        out_specs=vmem,
        out_shape=jax.ShapeDtypeStruct(part.shape, F32),
        scratch_shapes=[pltpu.VMEM((8,) + part.shape, F32), pltpu.SemaphoreType.DMA((7,)), pltpu.SemaphoreType.DMA((7,))],
        name="all_sum_small",
    )(part)


def _rope_tables():
    half = HEAD_DIM // 2
    inv_freq = np.float32(ROPE_THETA) ** (-np.arange(half, dtype=np.float32) * np.float32(2.0) / np.float32(HEAD_DIM))
    ang = np.arange(SEQ, dtype=np.float32)[:, None] * inv_freq[None, :].astype(np.float32)
    cos = np.tile(np.cos(ang).astype(np.float32), (1, 2 * BAND_HEADS))
    sin = np.tile(np.concatenate([-np.sin(ang), np.sin(ang)], axis=1).astype(np.float32), (1, BAND_HEADS))
    return jnp.asarray(cos), jnp.asarray(sin)


def _band_groups():
    out = []
    for d in DIL_D:
        rows = SEQ // d
        cls = max(1, 512 // rows) if d > 1 else 1
        out.append(dict(rows=rows, cls=cls, steps=d // cls))
    return out


def _local_step(x, mem, loss_target, g_pre_mix, g_post_mix, g_pre_ffn, g_post_ffn, g_mem, b_gate, w, comm=None):
    bl = x.shape[0]
    t = bl * SEQ
    chips = range(N_CHIPS)
    half_ff = D_FF // 2

    def with_gathered(w, names, gathered, shards):
        return {**w, **{name: lax.dynamic_update_slice(g, s[None], (comm["me"][0], 0, 0))
                        for name, g, s in zip(names, gathered, shards)}}

    x2 = x.reshape(t, D_MODEL)
    tgt2 = loss_target.reshape(t, D_MODEL)
    mem2 = mem.reshape(bl * MEM_LEN, D_MODEL)

    h = _norm_fwd(x2, g_pre_mix, name="norm_x", side=("gather", comm["first_shards"]) if comm else None)
    if comm:
        w = with_gathered(w, comm["first_names"], h[1], comm["first_shards"])
        h = h[0]
    w_in_full = _join_shards(w["w_in"])
    proj = _mm([(h, w_in_full)], nt=False, tn=2176, out_dtypes=[BF16], name="proj",
               side=("gather", comm["mid_shards"]) if comm else None)
    if comm:
        w = with_gathered(w, comm["mid_names"], proj[1], comm["mid_shards"])
        proj = proj[0]
    w_mem_kv_full = w["w_mem_kv"].reshape(D_MODEL, 2 * MEM_W)
    gates = _mm([(h, w["w_gate"], None, "j")], nt=False,tn=w["w_gate"].shape[2], out_dtypes=[BF16], name="gates",
                bias=b_gate, epilogue=lambda acc: (_sigmoid(acc),))
    hm = _norm_fwd(mem2, g_mem, name="norm_mem")
    kv_m = _mm([(hm, w_mem_kv_full)], nt=False,tn=1024, out_dtypes=[BF16], name="mem_kv")
    proj3 = proj.reshape(bl, SEQ, D_IN)
    kv3 = kv_m.reshape(bl, MEM_LEN, 2 * MEM_W)

    o_a, o_a32, sb_weights, late_gathered = _sb_fwd(proj3, comm["late_shards"] if comm else [])
    if comm:
        w = with_gathered(w, comm["late_names"], late_gathered, comm["late_shards"])
    w_o_full = w["w_o"].reshape(D_MODEL, D_MODEL)
    w_ffn_out_full = w["w_ffn_out"].reshape(D_FF, D_MODEL)

    cos_t, sin_t = _rope_tables()
    dil0 = 3 * SB_W

    grp_w = 3 * DIL_W
    band = []
    for g, (d, cfg) in enumerate(zip(DIL_D, _band_groups())):
        a_g = proj3 if d == 1 else proj3[:, :, dil0 + g * grp_w:dil0 + (g + 1) * grp_w].reshape(bl, SEQ // d, d * grp_w)
        band.append(dict(cfg, a=a_g, col0=dil0 // grp_w if d == 1 else None, cos=cos_t.reshape(SEQ // d, d * DIL_W),
                         sin=sin_t.reshape(SEQ // d, d * DIL_W)))
    outs = [_band_group_fwd(b["a"], b["cos"], b["sin"], rows=b["rows"], cls=b["cls"], steps=b["steps"], col0=b["col0"],
                            name=f"band_fwd_{g}") for g, b in enumerate(band)]
    o_b, lse_b = _band_merge3([(o.reshape(t, DIL_W), l.reshape(t, LANES)) for o, l in outs])

    o_c = _mem_fwd(proj3, kv3)

    o_a2, o_c2 = o_a.reshape(t, SB_W), o_c.reshape(t, MEM_W)
    y_a, y_b, y_c, merged = _branch_merge_fwd(o_a2, o_b, o_c2, w["w_br_sb"], w["w_br_dil"], w["w_br_mem"], gates)
    mix = _mm([(merged, w_o_full)], nt=False,tn=1024, out_dtypes=[F32], name="mix")
    x1, h2 = _mid_fwd(mix, x2, g_post_mix, g_pre_ffn)
    gg, uu, f = _ffn_in_fwd(h2, w["w_ffn_in"])
    f2 = _mm([(f, w_ffn_out_full)], nt=False,tn=1024, out_dtypes=[F32], name="ffn_out")

    dy, df2, dg_post_ffn, loss_row = _loss_bwd(f2, x1, g_post_ffn, tgt2)

    dg_ffn, du_ffn = _mm([(df2, w_ffn_out_full)], nt=True,tn=half_ff, out_dtypes=[BF16, BF16], name="d_ffn_act",
                         extras=(gg, uu), epilogue=_swiglu_bwd_epilogue)
    gw = {}
    gw["w_ffn_out"] = _mm_tn(f, df2, tm=half_ff, tn=1024, name="gw_ffn_out").reshape(N_CHIPS, D_FF // N_CHIPS, D_MODEL)
    gw_ffn_g = _mm_tn(h2, dg_ffn, tm=1024, tn=half_ff, name="gw_ffn_gate", out_shards=True, slots=(N_CHIPS, 0), group=2)
    gw["w_ffn_in"] = _mm_tn(h2, du_ffn, tm=1024, tn=half_ff, name="gw_ffn_up", out_shards=True, slots=(N_CHIPS, 2), into=gw_ffn_g,
                            group=2)
    dh2 = _mm([(dg_ffn, w["w_ffn_in"], 0, 0), (dg_ffn, w["w_ffn_in"], 1, 1), (du_ffn, w["w_ffn_in"], 0, 2),
               (du_ffn, w["w_ffn_in"], 1, 3)], nt=True, tn=1024, out_dtypes=[BF16], name="d_h2")
    dx1, dmix, dg_pre_ffn, dg_post_mix = _mid_bwd(dh2, x1, mix, g_pre_ffn, g_post_mix, dy)

    gw["w_o"] = _mm_tn(merged, dmix, tm=1024, tn=1024, name="gw_o").reshape(N_CHIPS, D_MODEL // N_CHIPS, D_MODEL)
    dmerged = _mm([(dmix, w_o_full)], nt=True, tn=1024, out_dtypes=[BF16], name="d_merged")
    dy_a, dy_b, dy_c, dgpre, db_gate = _gate_bwd(dmerged, gates, y_a, y_b, y_c)
    br_cols = D_MODEL // N_CHIPS
    gw["w_br_sb"] = _mm_tn(o_a2, dy_a, tm=512, tn=br_cols, name="gw_br_sb", out_shards=True, group=N_CHIPS)
    gw["w_br_dil"] = _mm_tn(o_b, dy_b, tm=256, tn=br_cols, name="gw_br_dil", out_shards=True, group=N_CHIPS)
    gw["w_br_mem"] = _mm_tn(o_c2, dy_c, tm=512, tn=br_cols, name="gw_br_mem", out_shards=True, group=N_CHIPS)
    h_t = h.T
    gw["w_gate"] = _mm_tn(h_t, dgpre, tm=1024, tn=w["w_gate"].shape[2], name="gw_gate", out_shards=True, group=2,
                          a_is_transposed=True)
    do_a = _mm([(dy_a, w["w_br_sb"], s, s) for s in chips], nt=True,tn=SB_W, out_dtypes=[BF16], name="d_o_a")
    do_b = _mm([(dy_b, w["w_br_dil"], s, s) for s in chips], nt=True,tn=DIL_W, out_dtypes=[BF16], name="d_o_b")
    do_c = _mm([(dy_c, w["w_br_mem"], s, s) for s in chips], nt=True,tn=MEM_W, out_dtypes=[BF16], name="d_o_c")

    dq_c, dkv_m = _mem_bwd(proj3, kv3, do_c.reshape(bl, SEQ, MEM_W))
    dkv_m = dkv_m.reshape(bl * MEM_LEN, 2 * MEM_W).astype(BF16)
    gw["w_mem_kv"] = _mm_tn(hm, dkv_m, tm=1024, tn=1024, name="gw_mem_kv").reshape(N_CHIPS, D_MODEL // N_CHIPS, 2 * MEM_W)
    dhm = _mm([(dkv_m, w_mem_kv_full)], nt=True,tn=1024, out_dtypes=[F32], name="d_hm")
    dg_mem = _mem_norm_bwd(dhm, mem2, g_mem)

    stats = _band_delta(do_b, o_b, lse_b)
    early = [name for name, _, _ in PACK if name != "w_in"] if comm else []
    grads = [gw[name] for name in early]
    d_dil = []
    for g, (d, b) in enumerate(zip(DIL_D, band)):
        out = _band_group_bwd(b["a"], do_b.reshape(bl, SEQ // d, d * DIL_W), stats.reshape(bl, SEQ // d, d * LANES),
                              b["cos"], b["sin"], rows=b["rows"], cls=b["cls"], steps=b["steps"], col0=b["col0"],
                              name=f"band_bwd_{g}", side=("pair", grads) if comm and g == 0 else None)
        if comm and g == 0:
            out, lands = out
        d_dil.append(out.reshape(bl, SEQ, grp_w))

    parts = [_pair_add(g, l, comm["c"], name="pair_add_" + name) for name, g, l in zip(early, grads, lands)] if comm else []
    dq_a, dk_a, dv_a, lands = _sb_bwd(proj3, o_a32, do_a.reshape(bl, SEQ, SB_W), sb_weights, parts)
    reduced = {name: (p, l) for name, p, l in zip(early, parts, lands)}

    in_cols = D_IN // N_CHIPS
    dproj_s = _split_to_shards([a.reshape(t, a.shape[-1]) for a in [dq_a, dk_a, dv_a] + d_dil + [dq_c]], name="dproj_shards")
    gw["w_in"] = _mm_tn(h_t, dproj_s, tm=1024, tn=in_cols, name="gw_in", group=2, a_is_transposed=True)
    if comm:
        land = _pair_exchange([gw["w_in"]], name="pair_exchange_w_in")[0]
        part_in = _pair_add(gw["w_in"], land, comm["c"], name="pair_add_w_in")
    dh = _mm([(dproj_s, w["w_in"], s, s) for s in chips] + [(dgpre, w["w_gate"], s, s) for s in chips],
             nt=True, tn=1024, out_dtypes=[BF16], name="d_h", side=("chip", [part_in]) if comm else None)
    if comm:
        dh, (land_in,) = dh
        reduced["w_in"] = (part_in, land_in)
    grad_x, dg_pre_mix = _first_bwd(dh, x2, g_pre_mix, dx1)
    small = jnp.concatenate([dg_pre_mix, dg_post_mix, dg_pre_ffn, dg_post_ffn, dg_mem, db_gate.reshape(3, D_MODEL)], axis=0)
    return loss_row[0, 0], grad_x.reshape(bl, SEQ, D_MODEL), gw, small, reduced


def kernel(x, mem, g_pre_mix, g_post_mix, g_pre_ffn, g_post_ffn, g_mem, w_in, w_mem_kv, w_br_sb, w_br_dil, w_br_mem, w_gate, b_gate, w_o, w_ffn_in, w_ffn_out, loss_target, m_g_pre_mix, m_g_post_mix, m_g_pre_ffn, m_g_post_ffn, m_g_mem, m_w_in, m_w_mem_kv, m_w_br_sb, m_w_br_dil, m_w_br_mem, m_w_gate, m_b_gate, m_w_o, m_w_ffn_in, m_w_ffn_out, v_g_pre_mix, v_g_post_mix, v_g_pre_ffn, v_g_post_ffn, v_g_mem, v_w_in, v_w_mem_kv, v_w_br_sb, v_w_br_dil, v_w_br_mem, v_w_gate, v_b_gate, v_w_o, v_w_ffn_in, v_w_ffn_out):
    w_shards = dict(w_in=w_in[0], w_mem_kv=w_mem_kv[0], w_br_sb=w_br_sb[0], w_br_dil=w_br_dil[0], w_br_mem=w_br_mem[0],
                    w_gate=w_gate[0], w_o=w_o[0], w_ffn_in=w_ffn_in[0], w_ffn_out=w_ffn_out[0])
    m_shards = dict(w_in=m_w_in[0], w_mem_kv=m_w_mem_kv[0], w_br_sb=m_w_br_sb[0], w_br_dil=m_w_br_dil[0], w_br_mem=m_w_br_mem[0],
                    w_gate=m_w_gate[0], w_o=m_w_o[0], w_ffn_in=m_w_ffn_in[0], w_ffn_out=m_w_ffn_out[0])
    v_shards = dict(w_in=v_w_in[0], w_mem_kv=v_w_mem_kv[0], w_br_sb=v_w_br_sb[0], w_br_dil=v_w_br_dil[0], w_br_mem=v_w_br_mem[0],
                    w_gate=v_w_gate[0], w_o=v_w_o[0], w_ffn_in=v_w_ffn_in[0], w_ffn_out=v_w_ffn_out[0])

    names = [name for name, _, _ in PACK]
    c_arr = lax.axis_index("c").astype(jnp.int32).reshape(1)
    me_arr = (2 * lax.axis_index("x") + lax.axis_index("y")).astype(jnp.int32).reshape(1)
    mid_names = ["w_gate", "w_mem_kv"]
    late_names = [name for name in names if name not in ["w_in"] + mid_names]
    bf = {name: w_shards[name].astype(BF16) for name in names}
    comm = dict(c=c_arr, me=me_arr, first_names=["w_in"], first_shards=[bf["w_in"]],
                mid_names=mid_names, mid_shards=[bf[name] for name in mid_names],
                late_names=late_names, late_shards=[bf[name] for name in late_names])

    loss_local, grad_x, gw, small, reduced = _local_step(x, mem, loss_target, g_pre_mix, g_post_mix, g_pre_ffn, g_post_ffn,
                                                         g_mem, b_gate, {}, comm)

    halves = [_chip_add(reduced[name][1], reduced[name][0], me_arr, name="chip_add_" + name) for name in names]
    theirs = _pair_share(halves)
    extra = jnp.concatenate([jnp.full((1, D_MODEL), loss_local, F32), jnp.zeros((7, D_MODEL), F32)], axis=0)
    summed = _all_sum_small(jnp.concatenate([small, extra], axis=0))
    small, loss = summed[:8], summed[8, 0]

    upd = {}
    for name, mine, other in zip(names, halves, theirs):
        upd[name] = _adamw_halves(w_shards[name], mine, other, m_shards[name], v_shards[name], c_arr, name="adamw_" + name)
    g_shards = {name: u[0] for name, u in upd.items()}

    def small8(gs, b):
        return jnp.concatenate(gs + [b.reshape(3, D_MODEL)], axis=0)

    sw = small8([g_pre_mix, g_post_mix, g_pre_ffn, g_post_ffn, g_mem], b_gate)
    sm = small8([m_g_pre_mix, m_g_post_mix, m_g_pre_ffn, m_g_post_ffn, m_g_mem], m_b_gate)
    sv = small8([v_g_pre_mix, v_g_post_mix, v_g_pre_ffn, v_g_post_ffn, v_g_mem], v_b_gate)
    s_upd = _adamw(sw, small, sm, sv, tm=8, name="adamw_small")

    def small_out(a):
        return [a[0:1], a[1:2], a[2:3], a[3:4], a[4:5]]

    order = ["w_in", "w_mem_kv", "w_br_sb", "w_br_dil", "w_br_mem", "w_gate", "b_gate", "w_o", "w_ffn_in", "w_ffn_out"]

    def leaves(small_arr, big):
        out = small_out(small_arr)
        for name in order:
            out.append(small_arr[5:8].reshape(1, 3 * D_MODEL) if name == "b_gate" else big[name][None])
        return out

    grads_out = leaves(small, g_shards)
    delta_out = leaves(s_upd[0], {n: u[1] for n, u in upd.items()})
    m_out = leaves(s_upd[1], {n: u[2] for n, u in upd.items()})
    v_out = leaves(s_upd[2], {n: u[3] for n, u in upd.items()})
    return (loss, grad_x, *grads_out, *delta_out, *m_out, *v_out)
```
